```python
import math
import jax, jax.numpy as jnp
from jax import lax
import numpy as np

D_MODEL = 1024
BATCH = 8
SEQ = 8192
DEPTH = 2

CHUNK = 64
DN_HEADS = 8
DN_DK = 128
DN_DV = 128
DN_QK_W = DN_HEADS * DN_DK
DN_V_W = DN_HEADS * DN_DV
CONV_K = 4
SGU_BLOCK = 128
SGU_GROUPS = 8
SGU_GROUP_DIM = 128
SGU_W = SGU_GROUPS * SGU_GROUP_DIM
FFN_HIDDEN = int(math.ceil(8 * D_MODEL / 3 / 256)) * 256
DEEPNORM_ALPHA = (2 * DEPTH) ** 0.25
DEEPNORM_BETA = (8 * DEPTH) ** -0.25
IN_SIZES = (DN_QK_W, DN_QK_W, DN_V_W,
            DN_V_W,
            DN_HEADS, DN_HEADS,
            SGU_W, SGU_W,
            D_MODEL, D_MODEL)
N_IN = sum(IN_SIZES)
LN_EPS = 1e-5
RMS_EPS = 1e-6

kernel_name = "hybrid_deltanet_sgu_deepnorm"


def layer_norm(x, g, b):
    xf = x.astype(jnp.float32)
    mu = jnp.mean(xf, -1, keepdims=True)
    var = jnp.mean(jnp.square(xf - mu), -1, keepdims=True)
    return ((xf - mu) * lax.rsqrt(var + LN_EPS) * g + b).astype(x.dtype)


def l2norm(x):
    xf = x.astype(jnp.float32)
    return xf * lax.rsqrt(jnp.sum(xf * xf, -1, keepdims=True) + RMS_EPS)


def causal_depthwise_conv(x, w):
    K, C = w.shape
    return lax.conv_general_dilated(
        x, w[:, None, :], window_strides=(1,), padding=[(K - 1, 0)],
        dimension_numbers=("NWC", "WIO", "NWC"), feature_group_count=C)


def gated_delta_rule(q, k, v, g, beta):
    f32 = jnp.float32
    B, S, H, dk = q.shape
    dv = v.shape[-1]
    N = S // CHUNK

    def chunks(t):
        t = t.astype(f32).reshape((B, N, CHUNK, H) + t.shape[3:])
        return jnp.moveaxis(t, 3, 1)

    q, k, v, g, beta = (chunks(t) for t in (q, k, v, g, beta))
    q = q * (dk ** -0.5)
    g = jnp.cumsum(g, axis=-1)
    idx = jnp.arange(CHUNK)
    tril = idx[:, None] >= idx[None, :]
    strict = idx[:, None] > idx[None, :]
    decay = jnp.exp(jnp.where(tril, g[..., :, None] - g[..., None, :], -jnp.inf))
    k_beta = k * beta[..., None]
    lower = jnp.einsum("bhnid,bhnjd->bhnij", k_beta, k) * decay
    lower = jnp.where(strict, lower, 0.0) + jnp.eye(CHUNK, dtype=f32)
    rhs = jnp.concatenate([v * beta[..., None], k_beta * jnp.exp(g)[..., None]], -1)
    sol = lax.linalg.triangular_solve(lower, rhs, left_side=True, lower=True,
                                      unit_diagonal=True)
    u_c, w_c = sol[..., :dv], sol[..., dv:]
    attn = jnp.einsum("bhnid,bhnjd->bhnij", q, k) * decay
    q_g = q * jnp.exp(g)[..., None]
    g_last = g[..., -1]
    k_dec = k * jnp.exp(g_last[..., None] - g)[..., None]

    def step(state, xs):
        qg_i, kd_i, u_i, w_i, a_i, gl_i = xs
        v_new = u_i - jnp.einsum("bhck,bhkv->bhcv", w_i, state)
        o_i = (jnp.einsum("bhck,bhkv->bhcv", qg_i, state)
               + jnp.einsum("bhij,bhjv->bhiv", a_i, v_new))
        state = (state * jnp.exp(gl_i)[..., None, None]
                 + jnp.einsum("bhck,bhcv->bhkv", kd_i, v_new))
        return state, o_i

    xs = tuple(jnp.moveaxis(t, 2, 0) for t in (q_g, k_dec, u_c, w_c, attn, g_last))
    state0 = jnp.zeros((B, H, dk, dv), f32)
    _, o = lax.scan(step, state0, xs)
    return jnp.transpose(o, (1, 0, 3, 2, 4)).reshape(B, S, H, dv)


def hybrid_mixer(x, w_in, conv_w, a_log, dt_bias, o_norm_w, sgu_ln_g, sgu_ln_b,
                 w_s, b_s, w_pa, w_pb, w_o):
    f32 = jnp.float32
    B, S, _ = x.shape
    proj = x @ w_in
    cuts = np.cumsum(IN_SIZES)[:-1].tolist()
    q, k, v, z, b_logit, a_logit, u, vg, ga, gb = jnp.split(proj, cuts, axis=-1)

    qkv = jax.nn.silu(causal_depthwise_conv(jnp.concatenate([q, k, v], -1), conv_w))
    q, k, v = jnp.split(qkv, [DN_QK_W, 2 * DN_QK_W], axis=-1)
    q = l2norm(q.reshape(B, S, DN_HEADS, DN_DK))
    k = l2norm(k.reshape(B, S, DN_HEADS, DN_DK))
    v = v.reshape(B, S, DN_HEADS, DN_DV)
    beta = jax.nn.sigmoid(b_logit.astype(f32))
    g = -jnp.exp(a_log.astype(f32)) * jax.nn.softplus(a_logit.astype(f32) + dt_bias)
    o = gated_delta_rule(q, k, v, g, beta)
    zf = z.reshape(B, S, DN_HEADS, DN_DV).astype(f32)
    o = o * lax.rsqrt(jnp.mean(o * o, -1, keepdims=True) + RMS_EPS) * o_norm_w * jax.nn.silu(zf)
    y_a = o.reshape(B, S, DN_V_W).astype(x.dtype)

    u = jax.nn.gelu(u, approximate=False)
    vg = layer_norm(jax.nn.gelu(vg, approximate=False), sgu_ln_g, sgu_ln_b)
    nb = S // SGU_BLOCK
    vb = vg.reshape(B, nb, SGU_BLOCK, SGU_GROUPS, SGU_GROUP_DIM)
    ch = jnp.arange(SGU_BLOCK) // CHUNK
    ws = jnp.where(ch[:, None] >= ch[None, :], w_s, 0.0)
    sp = jnp.einsum("gij,bnjgc->bnigc", ws, vb) + b_s.T[None, None, :, :, None]
    y_b = u * sp.reshape(B, S, SGU_W)

    m = jax.nn.sigmoid(ga) * (y_a @ w_pa) + jax.nn.sigmoid(gb) * (y_b @ w_pb)
    return m @ w_o


def swiglu(x, w_gate, w_up, w_down):
    return (jax.nn.silu(x @ w_gate) * (x @ w_up)) @ w_down


def _fwd_setup_inputs(seed: int = 0) -> dict:
    key = jax.random.key(seed)
    ks = jax.random.split(key, 24)
    f32 = jnp.float32
    L = DEPTH
    nrm = lambda k, shape, s: jax.random.normal(k, shape, f32) * s
    dt = jnp.exp(jax.random.uniform(ks[4], (L, DN_HEADS), f32,
                                    math.log(1e-3), math.log(1e-1)))
    return {
        "x": jax.random.normal(ks[0], (BATCH, SEQ, D_MODEL), f32),
        "w_in": nrm(ks[1], (L, D_MODEL, N_IN), D_MODEL ** -0.5),
        "conv_w": nrm(ks[2], (L, CONV_K, 2 * DN_QK_W + DN_V_W), CONV_K ** -0.5),
        "a_log": jnp.log(jax.random.uniform(ks[3], (L, DN_HEADS), f32, 1.0, 16.0)),
        "dt_bias": dt + jnp.log(-jnp.expm1(-dt)),
        "o_norm_w": 1.0 + nrm(ks[5], (L, DN_DV), 0.02),
        "sgu_ln_g": 1.0 + nrm(ks[6], (L, SGU_W), 0.02),
        "sgu_ln_b": nrm(ks[7], (L, SGU_W), 0.02),
        "w_s": nrm(ks[8], (L, SGU_GROUPS, SGU_BLOCK, SGU_BLOCK), SGU_BLOCK ** -0.5),
        "b_s": 1.0 + nrm(ks[9], (L, SGU_GROUPS, SGU_BLOCK), 0.02),
        "w_pa": nrm(ks[10], (L, DN_V_W, D_MODEL), DN_V_W ** -0.5 * DEEPNORM_BETA),
        "w_pb": nrm(ks[11], (L, SGU_W, D_MODEL), SGU_W ** -0.5 * DEEPNORM_BETA),
        "w_o": nrm(ks[12], (L, D_MODEL, D_MODEL), D_MODEL ** -0.5 * DEEPNORM_BETA),
        "ln1_g": 1.0 + nrm(ks[13], (L, D_MODEL), 0.02),
        "ln1_b": nrm(ks[14], (L, D_MODEL), 0.02),
        "w_ffn_gate": nrm(ks[15], (L, D_MODEL, FFN_HIDDEN), D_MODEL ** -0.5),
        "w_ffn_up": nrm(ks[16], (L, D_MODEL, FFN_HIDDEN), D_MODEL ** -0.5),
        "w_ffn_down": nrm(ks[17], (L, FFN_HIDDEN, D_MODEL), FFN_HIDDEN ** -0.5 * DEEPNORM_BETA),
        "ln2_g": 1.0 + nrm(ks[18], (L, D_MODEL), 0.02),
        "ln2_b": nrm(ks[19], (L, D_MODEL), 0.02),
    }


def _fwd_reference(x, w_in, conv_w, a_log, dt_bias, o_norm_w, sgu_ln_g, sgu_ln_b, w_s, b_s,
              w_pa, w_pb, w_o, ln1_g, ln1_b, w_ffn_gate, w_ffn_up, w_ffn_down,
              ln2_g, ln2_b):
    for l in range(DEPTH):
        mix = hybrid_mixer(x, w_in[l], conv_w[l], a_log[l], dt_bias[l], o_norm_w[l],
                           sgu_ln_g[l], sgu_ln_b[l], w_s[l], b_s[l],
                           w_pa[l], w_pb[l], w_o[l])
        x = layer_norm(DEEPNORM_ALPHA * x + mix, ln1_g[l], ln1_b[l])
        ffn = swiglu(x, w_ffn_gate[l], w_ffn_up[l], w_ffn_down[l])
        x = layer_norm(DEEPNORM_ALPHA * x + ffn, ln2_g[l], ln2_b[l])
    return x


import jax as _jax
import jax.numpy as _jnp

TWIN_FORMAT = 'train_step'
FWD_PARAMS = ['x', 'w_in', 'conv_w', 'a_log', 'dt_bias', 'o_norm_w', 'sgu_ln_g', 'sgu_ln_b', 'w_s', 'b_s', 'w_pa', 'w_pb', 'w_o', 'ln1_g', 'ln1_b', 'w_ffn_gate', 'w_ffn_up', 'w_ffn_down', 'ln2_g', 'ln2_b']
TWIN_WEIGHTS = ['w_in', 'conv_w', 'a_log', 'dt_bias', 'o_norm_w', 'sgu_ln_g', 'sgu_ln_b', 'w_s', 'b_s', 'w_pa', 'w_pb', 'w_o', 'ln1_g', 'ln1_b', 'w_ffn_gate', 'w_ffn_up', 'w_ffn_down', 'ln2_g', 'ln2_b']
TWIN_DIFF_INPUT = 'x'
TWIN_INPUTS = ['x', 'w_in', 'conv_w', 'a_log', 'dt_bias', 'o_norm_w', 'sgu_ln_g', 'sgu_ln_b', 'w_s', 'b_s', 'w_pa', 'w_pb', 'w_o', 'ln1_g', 'ln1_b', 'w_ffn_gate', 'w_ffn_up', 'w_ffn_down', 'ln2_g', 'ln2_b', 'loss_target', 'm_w_in', 'm_conv_w', 'm_a_log', 'm_dt_bias', 'm_o_norm_w', 'm_sgu_ln_g', 'm_sgu_ln_b', 'm_w_s', 'm_b_s', 'm_w_pa', 'm_w_pb', 'm_w_o', 'm_ln1_g', 'm_ln1_b', 'm_w_ffn_gate', 'm_w_ffn_up', 'm_w_ffn_down', 'm_ln2_g', 'm_ln2_b', 'v_w_in', 'v_conv_w', 'v_a_log', 'v_dt_bias', 'v_o_norm_w', 'v_sgu_ln_g', 'v_sgu_ln_b', 'v_w_s', 'v_b_s', 'v_w_pa', 'v_w_pb', 'v_w_o', 'v_ln1_g', 'v_ln1_b', 'v_w_ffn_gate', 'v_w_ffn_up', 'v_w_ffn_down', 'v_ln2_g', 'v_ln2_b']
TWIN_OUTPUTS = ['loss', 'grad_x', 'grad_w_in', 'grad_conv_w', 'grad_a_log', 'grad_dt_bias', 'grad_o_norm_w', 'grad_sgu_ln_g', 'grad_sgu_ln_b', 'grad_w_s', 'grad_b_s', 'grad_w_pa', 'grad_w_pb', 'grad_w_o', 'grad_ln1_g', 'grad_ln1_b', 'grad_w_ffn_gate', 'grad_w_ffn_up', 'grad_w_ffn_down', 'grad_ln2_g', 'grad_ln2_b', 'delta_w_in', 'delta_conv_w', 'delta_a_log', 'delta_dt_bias', 'delta_o_norm_w', 'delta_sgu_ln_g', 'delta_sgu_ln_b', 'delta_w_s', 'delta_b_s', 'delta_w_pa', 'delta_w_pb', 'delta_w_o', 'delta_ln1_g', 'delta_ln1_b', 'delta_w_ffn_gate', 'delta_w_ffn_up', 'delta_w_ffn_down', 'delta_ln2_g', 'delta_ln2_b', 'new_m_w_in', 'new_m_conv_w', 'new_m_a_log', 'new_m_dt_bias', 'new_m_o_norm_w', 'new_m_sgu_ln_g', 'new_m_sgu_ln_b', 'new_m_w_s', 'new_m_b_s', 'new_m_w_pa', 'new_m_w_pb', 'new_m_w_o', 'new_m_ln1_g', 'new_m_ln1_b', 'new_m_w_ffn_gate', 'new_m_w_ffn_up', 'new_m_w_ffn_down', 'new_m_ln2_g', 'new_m_ln2_b', 'new_v_w_in', 'new_v_conv_w', 'new_v_a_log', 'new_v_dt_bias', 'new_v_o_norm_w', 'new_v_sgu_ln_g', 'new_v_sgu_ln_b', 'new_v_w_s', 'new_v_b_s', 'new_v_w_pa', 'new_v_w_pb', 'new_v_w_o', 'new_v_ln1_g', 'new_v_ln1_b', 'new_v_w_ffn_gate', 'new_v_w_ffn_up', 'new_v_w_ffn_down', 'new_v_ln2_g', 'new_v_ln2_b']
TWIN_LEAF_KINDS = {'loss': 'loss', 'grad_x': 'grad_x', 'grad_w_in': 'grad_w', 'grad_conv_w': 'grad_w', 'grad_a_log': 'grad_w', 'grad_dt_bias': 'grad_w', 'grad_o_norm_w': 'grad_w', 'grad_sgu_ln_g': 'grad_w', 'grad_sgu_ln_b': 'grad_w', 'grad_w_s': 'grad_w', 'grad_b_s': 'grad_w', 'grad_w_pa': 'grad_w', 'grad_w_pb': 'grad_w', 'grad_w_o': 'grad_w', 'grad_ln1_g': 'grad_w', 'grad_ln1_b': 'grad_w', 'grad_w_ffn_gate': 'grad_w', 'grad_w_ffn_up': 'grad_w', 'grad_w_ffn_down': 'grad_w', 'grad_ln2_g': 'grad_w', 'grad_ln2_b': 'grad_w', 'delta_w_in': 'delta_w', 'delta_conv_w': 'delta_w', 'delta_a_log': 'delta_w', 'delta_dt_bias': 'delta_w', 'delta_o_norm_w': 'delta_w', 'delta_sgu_ln_g': 'delta_w', 'delta_sgu_ln_b': 'delta_w', 'delta_w_s': 'delta_w', 'delta_b_s': 'delta_w', 'delta_w_pa': 'delta_w', 'delta_w_pb': 'delta_w', 'delta_w_o': 'delta_w', 'delta_ln1_g': 'delta_w', 'delta_ln1_b': 'delta_w', 'delta_w_ffn_gate': 'delta_w', 'delta_w_ffn_up': 'delta_w', 'delta_w_ffn_down': 'delta_w', 'delta_ln2_g': 'delta_w', 'delta_ln2_b': 'delta_w', 'new_m_w_in': 'new_m', 'new_m_conv_w': 'new_m', 'new_m_a_log': 'new_m', 'new_m_dt_bias': 'new_m', 'new_m_o_norm_w': 'new_m', 'new_m_sgu_ln_g': 'new_m', 'new_m_sgu_ln_b': 'new_m', 'new_m_w_s': 'new_m', 'new_m_b_s': 'new_m', 'new_m_w_pa': 'new_m', 'new_m_w_pb': 'new_m', 'new_m_w_o': 'new_m', 'new_m_ln1_g': 'new_m', 'new_m_ln1_b': 'new_m', 'new_m_w_ffn_gate': 'new_m', 'new_m_w_ffn_up': 'new_m', 'new_m_w_ffn_down': 'new_m', 'new_m_ln2_g': 'new_m', 'new_m_ln2_b': 'new_m', 'new_v_w_in': 'new_v', 'new_v_conv_w': 'new_v', 'new_v_a_log': 'new_v', 'new_v_dt_bias': 'new_v', 'new_v_o_norm_w': 'new_v', 'new_v_sgu_ln_g': 'new_v', 'new_v_sgu_ln_b': 'new_v', 'new_v_w_s': 'new_v', 'new_v_b_s': 'new_v', 'new_v_w_pa': 'new_v', 'new_v_w_pb': 'new_v', 'new_v_w_o': 'new_v', 'new_v_ln1_g': 'new_v', 'new_v_ln1_b': 'new_v', 'new_v_w_ffn_gate': 'new_v', 'new_v_w_ffn_up': 'new_v', 'new_v_w_ffn_down': 'new_v', 'new_v_ln2_g': 'new_v', 'new_v_ln2_b': 'new_v'}


def _forward(args):
    return _fwd_reference(*[args[k] for k in FWD_PARAMS])


def _output_shape():
    def fwd():
        inp = _fwd_setup_inputs(0)
        return _fwd_reference(*[inp[k] for k in FWD_PARAMS])
    out = _jax.eval_shape(fwd)
    return out.shape, out.dtype

N_MICROBATCH = 1
ADAM_LR = 0.001
ADAM_B1 = 0.9
ADAM_B2 = 0.999
ADAM_EPS = 1e-08
ADAM_WD = 0.01
ADAM_STEP = 10
PER_EXAMPLE_BATCH_AXIS = {'x': 0, 'loss_target': 0}
SHARED_INPUTS = []
_WEIGHT_DTYPES = {'w_in': _jnp.float32, 'conv_w': _jnp.float32, 'a_log': _jnp.float32, 'dt_bias': _jnp.float32, 'o_norm_w': _jnp.float32, 'sgu_ln_g': _jnp.float32, 'sgu_ln_b': _jnp.float32, 'w_s': _jnp.float32, 'b_s': _jnp.float32, 'w_pa': _jnp.float32, 'w_pb': _jnp.float32, 'w_o': _jnp.float32, 'ln1_g': _jnp.float32, 'ln1_b': _jnp.float32, 'w_ffn_gate': _jnp.float32, 'w_ffn_up': _jnp.float32, 'w_ffn_down': _jnp.float32, 'ln2_g': _jnp.float32, 'ln2_b': _jnp.float32}
MOMENT_SCALE = {'w_in': 1.355132e-02, 'conv_w': 1.190273e-02, 'a_log': 6.019389e-02, 'dt_bias': 5.694488e-02, 'o_norm_w': 4.798891e-02, 'sgu_ln_g': 1.417061e-02, 'sgu_ln_b': 1.421774e-02, 'w_s': 1.396595e-02, 'b_s': 1.606843e-02, 'w_pa': 3.260901e-02, 'w_pb': 5.496053e-02, 'w_o': 6.472220e-02, 'ln1_g': 2.292715e+00, 'ln1_b': 9.665265e-01, 'w_ffn_gate': 3.274757e-02, 'w_ffn_up': 3.178466e-02, 'w_ffn_down': 1.052308e-01, 'ln2_g': 4.539472e+01, 'ln2_b': 2.355944e+00}


def _to_microbatches(a, axis):
    t = _jnp.moveaxis(a, axis, 0)
    t = t.reshape((N_MICROBATCH, t.shape[0] // N_MICROBATCH) + t.shape[1:])
    return _jnp.moveaxis(t, 1, axis + 1)


def setup_inputs(seed: int = 0) -> dict:
    inp = _fwd_setup_inputs(seed)
    key = _jax.random.fold_in(_jax.random.key(seed), 7919)
    shape, _ = _output_shape()
    out = dict(inp)
    out["loss_target"] = _jax.random.normal(_jax.random.fold_in(key, 0), shape, _jnp.float32)
    for i, name in enumerate(TWIN_WEIGHTS):
        w = inp[name].astype(_jnp.float32)
        if MOMENT_SCALE is None:
            s = _jnp.sqrt(_jnp.mean(_jnp.square(w)) + 1e-30)
        else:
            s = MOMENT_SCALE[name]
        km, kv = _jax.random.split(_jax.random.fold_in(key, i + 1))
        out[name] = w
        out["m_" + name] = s * _jax.random.normal(km, w.shape, _jnp.float32)
        out["v_" + name] = (s * s) * _jax.random.uniform(kv, w.shape, _jnp.float32, 0.5, 1.5)
    if N_MICROBATCH > 1:
        for name, axis in PER_EXAMPLE_BATCH_AXIS.items():
            out[name] = _to_microbatches(out[name], axis)
    return {'x': out['x'], 'w_in': out['w_in'], 'conv_w': out['conv_w'], 'a_log': out['a_log'], 'dt_bias': out['dt_bias'], 'o_norm_w': out['o_norm_w'], 'sgu_ln_g': out['sgu_ln_g'], 'sgu_ln_b': out['sgu_ln_b'], 'w_s': out['w_s'], 'b_s': out['b_s'], 'w_pa': out['w_pa'], 'w_pb': out['w_pb'], 'w_o': out['w_o'], 'ln1_g': out['ln1_g'], 'ln1_b': out['ln1_b'], 'w_ffn_gate': out['w_ffn_gate'], 'w_ffn_up': out['w_ffn_up'], 'w_ffn_down': out['w_ffn_down'], 'ln2_g': out['ln2_g'], 'ln2_b': out['ln2_b'], 'loss_target': out['loss_target'], 'm_w_in': out['m_w_in'], 'm_conv_w': out['m_conv_w'], 'm_a_log': out['m_a_log'], 'm_dt_bias': out['m_dt_bias'], 'm_o_norm_w': out['m_o_norm_w'], 'm_sgu_ln_g': out['m_sgu_ln_g'], 'm_sgu_ln_b': out['m_sgu_ln_b'], 'm_w_s': out['m_w_s'], 'm_b_s': out['m_b_s'], 'm_w_pa': out['m_w_pa'], 'm_w_pb': out['m_w_pb'], 'm_w_o': out['m_w_o'], 'm_ln1_g': out['m_ln1_g'], 'm_ln1_b': out['m_ln1_b'], 'm_w_ffn_gate': out['m_w_ffn_gate'], 'm_w_ffn_up': out['m_w_ffn_up'], 'm_w_ffn_down': out['m_w_ffn_down'], 'm_ln2_g': out['m_ln2_g'], 'm_ln2_b': out['m_ln2_b'], 'v_w_in': out['v_w_in'], 'v_conv_w': out['v_conv_w'], 'v_a_log': out['v_a_log'], 'v_dt_bias': out['v_dt_bias'], 'v_o_norm_w': out['v_o_norm_w'], 'v_sgu_ln_g': out['v_sgu_ln_g'], 'v_sgu_ln_b': out['v_sgu_ln_b'], 'v_w_s': out['v_w_s'], 'v_b_s': out['v_b_s'], 'v_w_pa': out['v_w_pa'], 'v_w_pb': out['v_w_pb'], 'v_w_o': out['v_w_o'], 'v_ln1_g': out['v_ln1_g'], 'v_ln1_b': out['v_ln1_b'], 'v_w_ffn_gate': out['v_w_ffn_gate'], 'v_w_ffn_up': out['v_w_ffn_up'], 'v_w_ffn_down': out['v_w_ffn_down'], 'v_ln2_g': out['v_ln2_g'], 'v_ln2_b': out['v_ln2_b']}


def _loss(weights, diff, rest, loss_target):
    with _jax.named_scope("forward"):
        args = {**rest, TWIN_DIFF_INPUT: diff, **{k: w.astype(_WEIGHT_DTYPES[k]) for k, w in weights.items()}}
        y = _forward(args)
    with _jax.named_scope("loss_head"):
        err = _jnp.square(y.astype(_jnp.float32) - loss_target)
        return 0.5 * _jnp.sum(_jnp.mean(err, axis=-1)) if err.ndim else 0.5 * err


def _adamw(w, g, m, v):
    m = ADAM_B1 * m + (1.0 - ADAM_B1) * g
    v = ADAM_B2 * v + (1.0 - ADAM_B2) * _jnp.square(g)
    m_hat = m / (1.0 - ADAM_B1 ** ADAM_STEP)
    v_hat = v / (1.0 - ADAM_B2 ** ADAM_STEP)
    delta = -ADAM_LR * (m_hat / (_jnp.sqrt(v_hat) + ADAM_EPS) + ADAM_WD * w)
    return delta, m, v


def reference(x, w_in, conv_w, a_log, dt_bias, o_norm_w, sgu_ln_g, sgu_ln_b, w_s, b_s, w_pa, w_pb, w_o, ln1_g, ln1_b, w_ffn_gate, w_ffn_up, w_ffn_down, ln2_g, ln2_b, loss_target, m_w_in, m_conv_w, m_a_log, m_dt_bias, m_o_norm_w, m_sgu_ln_g, m_sgu_ln_b, m_w_s, m_b_s, m_w_pa, m_w_pb, m_w_o, m_ln1_g, m_ln1_b, m_w_ffn_gate, m_w_ffn_up, m_w_ffn_down, m_ln2_g, m_ln2_b, v_w_in, v_conv_w, v_a_log, v_dt_bias, v_o_norm_w, v_sgu_ln_g, v_sgu_ln_b, v_w_s, v_b_s, v_w_pa, v_w_pb, v_w_o, v_ln1_g, v_ln1_b, v_w_ffn_gate, v_w_ffn_up, v_w_ffn_down, v_ln2_g, v_ln2_b):
    given = dict(x=x, w_in=w_in, conv_w=conv_w, a_log=a_log, dt_bias=dt_bias, o_norm_w=o_norm_w, sgu_ln_g=sgu_ln_g, sgu_ln_b=sgu_ln_b, w_s=w_s, b_s=b_s, w_pa=w_pa, w_pb=w_pb, w_o=w_o, ln1_g=ln1_g, ln1_b=ln1_b, w_ffn_gate=w_ffn_gate, w_ffn_up=w_ffn_up, w_ffn_down=w_ffn_down, ln2_g=ln2_g, ln2_b=ln2_b, loss_target=loss_target, m_w_in=m_w_in, m_conv_w=m_conv_w, m_a_log=m_a_log, m_dt_bias=m_dt_bias, m_o_norm_w=m_o_norm_w, m_sgu_ln_g=m_sgu_ln_g, m_sgu_ln_b=m_sgu_ln_b, m_w_s=m_w_s, m_b_s=m_b_s, m_w_pa=m_w_pa, m_w_pb=m_w_pb, m_w_o=m_w_o, m_ln1_g=m_ln1_g, m_ln1_b=m_ln1_b, m_w_ffn_gate=m_w_ffn_gate, m_w_ffn_up=m_w_ffn_up, m_w_ffn_down=m_w_ffn_down, m_ln2_g=m_ln2_g, m_ln2_b=m_ln2_b, v_w_in=v_w_in, v_conv_w=v_conv_w, v_a_log=v_a_log, v_dt_bias=v_dt_bias, v_o_norm_w=v_o_norm_w, v_sgu_ln_g=v_sgu_ln_g, v_sgu_ln_b=v_sgu_ln_b, v_w_s=v_w_s, v_b_s=v_b_s, v_w_pa=v_w_pa, v_w_pb=v_w_pb, v_w_o=v_w_o, v_ln1_g=v_ln1_g, v_ln1_b=v_ln1_b, v_w_ffn_gate=v_w_ffn_gate, v_w_ffn_up=v_w_ffn_up, v_w_ffn_down=v_w_ffn_down, v_ln2_g=v_ln2_g, v_ln2_b=v_ln2_b)
    weights = {n: given[n] for n in TWIN_WEIGHTS}
    shared = {n: given[n] for n in SHARED_INPUTS}
    per_example = {n: given[n] for n in ['x']}
    grad_fn = _jax.value_and_grad(_loss, argnums=(0, 1))

    def one_microbatch(ex, loss_target):
        ex = dict(ex)
        diff = ex.pop(TWIN_DIFF_INPUT)
        return grad_fn(weights, diff, {**shared, **ex}, loss_target)

    if N_MICROBATCH == 1:
        loss, (grad_w, grad_x) = one_microbatch(per_example, given["loss_target"])
    else:
        def body(carry, xs):
            loss_sum, grad_sum = carry
            l_k, (gw_k, gx_k) = one_microbatch(xs[0], xs[1])
            with _jax.named_scope("update"):
                return (loss_sum + l_k, _jax.tree.map(_jnp.add, grad_sum, gw_k)), gx_k

        init = (_jnp.zeros((), _jnp.float32), _jax.tree.map(_jnp.zeros_like, weights))
        (loss, grad_w), grad_x = _jax.lax.scan(body, init, (per_example, given["loss_target"]))
    with _jax.named_scope("update"):
        delta_w, new_m, new_v = {}, {}, {}
        for n in TWIN_WEIGHTS:
            delta_w[n], new_m[n], new_v[n] = _adamw(weights[n], grad_w[n], given["m_" + n], given["v_" + n])
    return (loss, grad_x, *[grad_w[n] for n in TWIN_WEIGHTS], *[delta_w[n] for n in TWIN_WEIGHTS],
            *[new_m[n] for n in TWIN_WEIGHTS], *[new_v[n] for n in TWIN_WEIGHTS])
```

```python
import functools
import math

import jax
import jax.numpy as jnp
from jax import lax
from jax.experimental import pallas as pl
from jax.experimental.pallas import tpu as pltpu

F32 = jnp.float32
BF = jnp.bfloat16
HIGHEST = lax.Precision.HIGHEST

D_MODEL = 1024
DEPTH = 2
N_HEADS = 8
D_HEAD = 128
CONV_K = 4
SGU_BLOCK = 128
SGU_GROUPS = 8
SGU_CHUNK = 64
FFN_HIDDEN = 2816
N_IN = 8208
ALPHA = (2 * DEPTH) ** 0.25
LN_EPS = 1e-5
RMS_EPS = 1e-6
ADAM_LR, ADAM_B1, ADAM_B2, ADAM_EPS, ADAM_WD, ADAM_STEP = 0.001, 0.9, 0.999, 1e-08, 0.01, 10

N_DEV = 8
MESH_AXES = ("x", "y", "c")
DELTA_CHUNK = 128
LANES = 128
SUBLANES = 8
VMEM_BYTES = 64 * 1024 * 1024
HALO = SUBLANES


def _cparams(est_bytes, dims=None):
    limit = int(min(max(2 * est_bytes + (8 << 20), 32 << 20), VMEM_BYTES - (6 << 20)))
    kw = dict(vmem_limit_bytes=limit)
    if dims is not None:
        kw["dimension_semantics"] = dims
    return pltpu.CompilerParams(**kw)


def _nbytes(shape, dtype):
    return math.prod(shape) * jnp.dtype(dtype).itemsize


def _dot(a, b):
    return jnp.dot(a.astype(BF), b.astype(BF), preferred_element_type=F32)


def _dot_nt(a, b):
    return lax.dot_general(a.astype(BF), b.astype(BF), (((1,), (1,)), ((), ())), preferred_element_type=F32)


def _dot_tn(a, b):
    return lax.dot_general(a.astype(BF), b.astype(BF), (((0,), (0,)), ((), ())), preferred_element_type=F32)


def _dotf(a, b):
    return jnp.dot(a, b, precision=HIGHEST, preferred_element_type=F32)


def _dotf_nt(a, b):
    return lax.dot_general(a, b, (((1,), (1,)), ((), ())), precision=HIGHEST, preferred_element_type=F32)


def _sigmoid(x):
    return jax.nn.sigmoid(x)


def _silu(x):
    return x * _sigmoid(x)


def _dsilu(x):
    s = _sigmoid(x)
    return s * (1.0 + x * (1.0 - s))


def _gelu(x):
    return 0.5 * x * (1.0 + lax.erf(x * 0.7071067811865476))


def _softplus(x):
    return jnp.maximum(x, 0.0) + jnp.log1p(jnp.exp(-jnp.abs(x)))


def _ln(x, g, b):
    mu = jnp.mean(x, -1, keepdims=True)
    xc = x - mu
    var = jnp.mean(xc * xc, -1, keepdims=True)
    return xc * lax.rsqrt(var + LN_EPS) * g + b


def _iota(shape, dim):
    return lax.broadcasted_iota(jnp.int32, shape, dim)


def _tile(n, pref, align):
    if n <= pref:
        return n
    t = (pref // align) * align
    while t >= align:
        if n % t == 0:
            return t
        t -= align
    raise ValueError(f"no tile for {n} (pref {pref}, align {align})")


def _bcast_rows(v, rows=SUBLANES):
    return jnp.broadcast_to(v, (rows, v.shape[-1]))


def _mm(a, b, *, mode, name, out_dtype=F32, add=None, add_scale=1.0, tm=512, tn=1024, tk=1024):
    if mode == "nn":
        (M, K), N = a.shape, b.shape[1]
    elif mode == "nt":
        (M, K), N = a.shape, b.shape[0]
    else:
        (K, M), N = a.shape, b.shape[1]
    tm = _tile(M, tm, LANES if mode == "tn" else SUBLANES * 2)
    tn = _tile(N, tn, LANES)
    tk = _tile(K, tk, LANES)
    nk = K // tk
    if mode == "nn":
        a_spec = pl.BlockSpec((tm, tk), lambda i, j, k: (i, k))
        b_spec = pl.BlockSpec((tk, tn), lambda i, j, k: (k, j))
        dot = _dot
    elif mode == "nt":
        a_spec = pl.BlockSpec((tm, tk), lambda i, j, k: (i, k))
        b_spec = pl.BlockSpec((tn, tk), lambda i, j, k: (j, k))
        dot = _dot_nt
    else:
        a_spec = pl.BlockSpec((tk, tm), lambda i, j, k: (k, i))
        b_spec = pl.BlockSpec((tk, tn), lambda i, j, k: (k, j))
        dot = _dot_tn
    o_spec = pl.BlockSpec((tm, tn), lambda i, j, k: (i, j))
    has_add = add is not None

    def body(*refs):
        if has_add:
            a_ref, b_ref, add_ref, o_ref, acc_ref = refs
        else:
            a_ref, b_ref, o_ref, acc_ref = refs
            add_ref = None
        k = pl.program_id(2)
        part = dot(a_ref[...], b_ref[...])

        def finish(total):
            if has_add:
                total = total + add_scale * add_ref[...]
            o_ref[...] = total.astype(out_dtype)

        if nk == 1:
            finish(part)
        else:
            @pl.when(k == 0)
            def _():
                acc_ref[...] = part

            @pl.when(jnp.logical_and(k > 0, k < nk - 1))
            def _():
                acc_ref[...] += part

            @pl.when(k == nk - 1)
            def _():
                finish(acc_ref[...] + part)

    in_specs = [a_spec, b_spec] + ([o_spec] if has_add else [])
    args = (a, b) + ((add,) if has_add else ())
    est = (_nbytes((tm, tk), a.dtype) + _nbytes((tk, tn), b.dtype) + 2 * _nbytes((tm, tn), F32)
           + (_nbytes((tm, tn), F32) if has_add else 0)) + 2 * _nbytes((tm, tn), F32)
    return pl.pallas_call(
        body, name=name,
        grid=(M // tm, N // tn, nk),
        in_specs=in_specs, out_specs=o_spec,
        out_shape=jax.ShapeDtypeStruct((M, N), out_dtype),
        scratch_shapes=[pltpu.VMEM((tm, tn) if nk > 1 else (SUBLANES, LANES), F32)],
        compiler_params=_cparams(est, ("parallel", "parallel", "arbitrary")),
    )(*args)


def _conv_taps(xt, halo, w_ref, first):
    halo = jnp.where(first, 0.0, halo)
    xc = jnp.concatenate([halo, xt], axis=0)
    shifted = [xt] + [pltpu.roll(xc, s, 0)[HALO:] for s in range(1, CONV_K)]
    out = shifted[0] * w_ref[CONV_K - 1:CONV_K, :]
    for s in range(1, CONV_K):
        out = out + shifted[s] * w_ref[CONV_K - 1 - s:CONV_K - s, :]
    return out, shifted


def _gates(ba, arow, dtrow):
    lane = _iota(ba.shape, 1)
    beta = _sigmoid(ba)
    g = -jnp.exp(arow) * _softplus(ba + dtrow)
    return jnp.where(lane < N_HEADS, beta, jnp.where(lane < 2 * N_HEADS, g, 0.0))


def _l2n(x):
    return x * lax.rsqrt(jnp.sum(x * x, -1, keepdims=True) + RMS_EPS)


def _qkv_prep(proj, ba, convw, arow, dtrow, *, tm=256):
    S = proj.shape[0]
    tm = _tile(S, tm, SUBLANES)
    W3 = 3 * D_MODEL
    hb = tm // HALO

    def body(xt_ref, halo_ref, ba_ref, w_ref, a_ref, dt_ref, q_ref, k_ref, v_ref, gb_ref):
        c, _ = _conv_taps(xt_ref[...], halo_ref[...], w_ref, pl.program_id(0) == 0)
        c = _silu(c)
        for h in range(N_HEADS):
            lo = h * D_HEAD
            q_ref[:, lo:lo + D_HEAD] = _l2n(c[:, lo:lo + D_HEAD])
            k_ref[:, lo:lo + D_HEAD] = _l2n(c[:, D_MODEL + lo:D_MODEL + lo + D_HEAD])
        v_ref[...] = c[:, 2 * D_MODEL:]
        gb_ref[...] = _gates(ba_ref[...], a_ref[...], dt_ref[...])

    row = lambda w, col=0: pl.BlockSpec((tm, w), lambda i: (i, col))
    full = lambda shape: pl.BlockSpec(shape, lambda i: (0,) * len(shape))
    est = 4 * _nbytes((tm, W3), F32)
    return pl.pallas_call(
        body, name="qkv_prep", grid=(S // tm,),
        in_specs=[row(W3), pl.BlockSpec((HALO, W3), lambda i: (jnp.maximum(i * hb - 1, 0), 0)), row(LANES),
                  full((CONV_K, W3)), full((1, LANES)), full((1, LANES))],
        out_specs=[row(D_MODEL), row(D_MODEL), row(D_MODEL), row(LANES)],
        out_shape=[jax.ShapeDtypeStruct((S, D_MODEL), F32)] * 3 + [jax.ShapeDtypeStruct((S, LANES), F32)],
        compiler_params=_cparams(est, ("arbitrary",)),
    )(proj, proj, ba, convw, arow, dtrow)


def _qkv_prep_bwd(proj, ba, convw, arow, dtrow, dq, dk, dv, dgb, *, tm=256):
    S = proj.shape[0]
    tm = _tile(S, tm, SUBLANES)
    W3 = 3 * D_MODEL
    hb = tm // HALO

    def body(xt_ref, halo_ref, ba_ref, w_ref, a_ref, dt_ref, dq_ref, dk_ref, dv_ref, dgb_ref,
             dc_ref, dba_ref, dw_ref, da_ref, ddt_ref):
        i = pl.program_id(0)

        @pl.when(i == 0)
        def _():
            dw_ref[...] = jnp.zeros_like(dw_ref)
            da_ref[...] = jnp.zeros_like(da_ref)
            ddt_ref[...] = jnp.zeros_like(ddt_ref)

        c, shifted = _conv_taps(xt_ref[...], halo_ref[...], w_ref, i == 0)
        a = _silu(c)
        ds = _dsilu(c)
        for h in range(N_HEADS):
            for base, d_ref in ((0, dq_ref), (D_MODEL, dk_ref)):
                lo = base + h * D_HEAD
                _, vj = jax.vjp(_l2n, a[:, lo:lo + D_HEAD])
                (dx,) = vj(d_ref[:, h * D_HEAD:(h + 1) * D_HEAD])
                dc_ref[:, lo:lo + D_HEAD] = dx * ds[:, lo:lo + D_HEAD]
        dc_ref[:, 2 * D_MODEL:] = dv_ref[...] * ds[:, 2 * D_MODEL:]
        dc = dc_ref[...]
        for s in range(CONV_K):
            kk = CONV_K - 1 - s
            dw_ref[kk:kk + 1, :] += jnp.sum(dc * shifted[s], axis=0, keepdims=True)
        _, vj = jax.vjp(_gates, ba_ref[...], a_ref[...], dt_ref[...])
        dba, da, ddt = vj(dgb_ref[...])
        dba_ref[...] = dba.astype(BF)
        da_ref[...] += _bcast_rows(da)
        ddt_ref[...] += _bcast_rows(ddt)

    row = lambda w, col=0: pl.BlockSpec((tm, w), lambda i: (i, col))
    full = lambda shape: pl.BlockSpec(shape, lambda i: (0,) * len(shape))
    est = 8 * _nbytes((tm, W3), F32)
    return pl.pallas_call(
        body, name="qkv_prep_bwd", grid=(S // tm,),
        in_specs=[row(W3), pl.BlockSpec((HALO, W3), lambda i: (jnp.maximum(i * hb - 1, 0), 0)), row(LANES),
                  full((CONV_K, W3)), full((1, LANES)), full((1, LANES)),
                  row(D_MODEL), row(D_MODEL), row(D_MODEL), row(LANES)],
        out_specs=[row(W3), row(LANES), full((SUBLANES, W3)), full((SUBLANES, LANES)), full((SUBLANES, LANES))],
        out_shape=[jax.ShapeDtypeStruct((S, W3), F32), jax.ShapeDtypeStruct((S, LANES), BF),
                   jax.ShapeDtypeStruct((SUBLANES, W3), F32), jax.ShapeDtypeStruct((SUBLANES, LANES), F32),
                   jax.ShapeDtypeStruct((SUBLANES, LANES), F32)],
        compiler_params=_cparams(est, ("arbitrary",)),
    )(proj, proj, ba, convw, arow, dtrow, dq, dk, dv, dgb)


def _conv_bwd(dc, convw, *, tm=256):
    S, W3 = dc.shape
    tm = _tile(S, tm, SUBLANES * 2)
    hb = tm // HALO
    nt = S // tm

    def body(dc_ref, nxt_ref, w_ref, o_ref):
        last = pl.program_id(0) == nt - 1
        nxt = jnp.where(last, 0.0, nxt_ref[...])
        cur = dc_ref[...]
        xc = jnp.concatenate([cur, nxt], axis=0)
        out = cur * w_ref[CONV_K - 1:CONV_K, :]
        for s in range(1, CONV_K):
            out = out + pltpu.roll(xc, tm + HALO - s, 0)[:tm] * w_ref[CONV_K - 1 - s:CONV_K - s, :]
        o_ref[...] = out.astype(BF)

    est = 5 * _nbytes((tm, W3), F32)
    return pl.pallas_call(
        body, name="conv_bwd", grid=(nt,),
        in_specs=[pl.BlockSpec((tm, W3), lambda i: (i, 0)),
                  pl.BlockSpec((HALO, W3), lambda i: (jnp.minimum((i + 1) * hb, S // HALO - 1), 0)),
                  pl.BlockSpec((CONV_K, W3), lambda i: (0, 0))],
        out_specs=pl.BlockSpec((tm, W3), lambda i: (i, 0)),
        out_shape=jax.ShapeDtypeStruct((S, W3), BF),
        compiler_params=_cparams(est, ("parallel",)),
    )(dc, dc, convw)


def _inv_unit_lower(A):
    C = A.shape[0]
    row, col = _iota((C, C), 0), _iota((C, C), 1)
    T = jnp.where(row == col, 1.0, 0.0).astype(F32)
    b = 1
    while b < C:
        hi = ~(2 * b - 1)
        off = ((row & hi) == (col & hi)) & ((row & b) != 0) & ((col & b) == 0)
        T = T - _dotf(_dotf(T, jnp.where(off, A, 0.0)), T)
        b *= 2
    return T


def _delta_common(q, k, g, beta):
    C = q.shape[0]
    row, col = _iota((C, C), 0), _iota((C, C), 1)
    tril = row >= col
    qs = q * (D_HEAD ** -0.5)
    gcb = _dotf(jnp.where(tril, 1.0, 0.0).astype(F32), jnp.broadcast_to(g, (C, LANES)))
    gc = gcb[:, :1]
    Dm = jnp.exp(jnp.where(tril, gc - gcb.T, -1e30))
    eg = jnp.exp(gc)
    gl = jnp.sum(jnp.where(_iota((C, 1), 0) == C - 1, gc, 0.0), keepdims=True)
    el = jnp.exp(gl)
    er = jnp.exp(gl - gc)
    kb = k * beta
    KK = _dot_nt(kb, k)
    QK = _dot_nt(qs, k)
    return dict(row=row, col=col, tril=tril, qs=qs, gc=gc, Dm=Dm, eg=eg, el=el, er=er, kb=kb, KK=KK, QK=QK)


def _delta_chunk_fwd(S0, q, k, v, g, beta, T=None):
    m = _delta_common(q, k, g, beta)
    if T is None:
        T = _inv_unit_lower(jnp.where(m["row"] > m["col"], m["KK"] * m["Dm"], 0.0))
    u = _dotf(T, v * beta)
    w = _dotf(T, m["kb"] * m["eg"])
    vn = u - _dot(w, S0)
    o = _dot(m["qs"] * m["eg"], S0) + _dot(m["QK"] * m["Dm"], vn)
    S1 = S0 * m["el"] + _dot(jnp.transpose(k * m["er"]), vn)
    return o, S1, T


def _delta_chunk_bwd(S0, q, k, v, g, beta, T, do, dS1):
    m = _delta_common(q, k, g, beta)
    C = q.shape[0]
    qs, Dm, eg, el, er, kb, KK, QK = (m[n] for n in ("qs", "Dm", "eg", "el", "er", "kb", "KK", "QK"))
    strict = m["row"] > m["col"]
    tT = jnp.transpose
    ru, rw = v * beta, kb * eg
    u = _dotf(T, ru)
    w = _dotf(T, rw)
    vn = u - _dot(w, S0)
    P = QK * Dm
    qg = qs * eg
    kr = k * er
    Tt = tT(T)

    dvn = _dot(tT(P), do) + _dot(kr, dS1)
    dS0 = dS1 * el + _dot(tT(qg), do) - _dot(tT(w), dvn)
    d_el = jnp.sum(dS1 * S0, keepdims=True)
    dqg = _dot_nt(do, S0)
    dqs = dqg * eg
    deg = jnp.sum(dqg * qs, -1, keepdims=True)
    dP = _dot_nt(do, vn)
    dPD = dP * Dm
    dqs = dqs + _dot(dPD, k)
    dk = _dot(tT(dPD), qs)
    dD = dP * QK
    dkr = _dot_nt(vn, dS1)
    dk = dk + dkr * er
    der = jnp.sum(dkr * k, -1, keepdims=True)
    dw = -_dot_nt(dvn, S0)
    dru = _dotf(Tt, dvn)
    drw = _dotf(Tt, dw)
    dT = _dotf_nt(dvn, ru) + _dotf_nt(dw, rw)
    dA = -_dotf(_dotf(Tt, dT), Tt)
    dAm = jnp.where(strict, dA, 0.0)
    dKK = dAm * Dm
    dkb = _dot(dKK, k)
    dk = dk + _dot(tT(dKK), kb)
    dD = dD + dAm * KK
    dv = dru * beta
    dbeta = jnp.sum(dru * v, -1, keepdims=True)
    dkb = dkb + drw * eg
    deg = deg + jnp.sum(drw * kb, -1, keepdims=True)
    dk = dk + dkb * beta
    dbeta = dbeta + jnp.sum(dkb * k, -1, keepdims=True)
    E = dD * Dm
    dgc = jnp.sum(E, -1, keepdims=True) - jnp.sum(tT(E), -1, keepdims=True)
    dgc = dgc + deg * eg - der * er
    dgl = jnp.sum(der * er, keepdims=True) + d_el * el
    dgc = dgc + jnp.where(_iota((C, 1), 0) == C - 1, dgl, 0.0)
    triu = jnp.where(m["row"] <= m["col"], 1.0, 0.0).astype(F32)
    dg = _dotf(triu, jnp.broadcast_to(dgc, (C, LANES)))[:, :1]
    dq = dqs * (D_HEAD ** -0.5)
    return dq, dk, dv, dg, dbeta, dS0


def _head_cols(gb, h):
    lane = _iota(gb.shape, 1)
    beta = jnp.sum(jnp.where(lane == h, gb, 0.0), -1, keepdims=True)
    g = jnp.sum(jnp.where(lane == N_HEADS + h, gb, 0.0), -1, keepdims=True)
    return g, beta


def _delta_fwd(q, k, v, gb):
    S = q.shape[0]
    C = DELTA_CHUNK
    N = S // C

    def body(q_ref, k_ref, v_ref, gb_ref, o_ref, st_ref, t_ref, s_scr):
        n, h = pl.program_id(0), pl.program_id(1)

        @pl.when(n == 0)
        def _():
            s_scr[h] = jnp.zeros((D_HEAD, D_HEAD), F32)

        g, beta = _head_cols(gb_ref[...], h)
        S0 = s_scr[h]
        o, S1, T = _delta_chunk_fwd(S0, q_ref[...], k_ref[...], v_ref[...], g, beta)
        st_ref[0, 0] = S0
        t_ref[0, 0] = T
        o_ref[...] = o
        s_scr[h] = S1

    hd = pl.BlockSpec((C, D_HEAD), lambda n, h: (n, h))
    mat = pl.BlockSpec((1, 1, D_HEAD, D_HEAD), lambda n, h: (h, n, 0, 0))
    est = 40 * _nbytes((C, D_HEAD), F32)
    return pl.pallas_call(
        body, name="delta_fwd", grid=(N, N_HEADS),
        in_specs=[hd, hd, hd, pl.BlockSpec((C, LANES), lambda n, h: (n, 0))],
        out_specs=[hd, mat, mat],
        out_shape=[jax.ShapeDtypeStruct((S, N_HEADS * D_HEAD), F32),
                   jax.ShapeDtypeStruct((N_HEADS, N, D_HEAD, D_HEAD), F32),
                   jax.ShapeDtypeStruct((N_HEADS, N, C, C), F32)],
        scratch_shapes=[pltpu.VMEM((N_HEADS, D_HEAD, D_HEAD), F32)],
        compiler_params=_cparams(est, ("arbitrary", "arbitrary")),
    )(q, k, v, gb)


def _delta_bwd(q, k, v, gb, st, tinv, do):
    S = q.shape[0]
    C = DELTA_CHUNK
    N = S // C

    def body(q_ref, k_ref, v_ref, gb_ref, st_ref, t_ref, do_ref, dq_ref, dk_ref, dv_ref, dgb_ref, ds_scr):
        n, h = pl.program_id(0), pl.program_id(1)

        @pl.when(n == 0)
        def _():
            ds_scr[h] = jnp.zeros((D_HEAD, D_HEAD), F32)

        @pl.when(h == 0)
        def _():
            dgb_ref[...] = jnp.zeros_like(dgb_ref)

        g, beta = _head_cols(gb_ref[...], h)
        dq, dk, dv, dg, dbeta, dS0 = _delta_chunk_bwd(
            st_ref[0, 0], q_ref[...], k_ref[...], v_ref[...], g, beta, t_ref[0, 0], do_ref[...], ds_scr[h])
        dq_ref[...] = dq
        dk_ref[...] = dk
        dv_ref[...] = dv
        lane = _iota((C, LANES), 1)
        dgb_ref[...] += jnp.where(lane == h, dbeta, 0.0) + jnp.where(lane == N_HEADS + h, dg, 0.0)
        ds_scr[h] = dS0

    hd = pl.BlockSpec((C, D_HEAD), lambda n, h: (N - 1 - n, h))
    mat = pl.BlockSpec((1, 1, D_HEAD, D_HEAD), lambda n, h: (h, N - 1 - n, 0, 0))
    gbs = pl.BlockSpec((C, LANES), lambda n, h: (N - 1 - n, 0))
    est = 60 * _nbytes((C, D_HEAD), F32)
    return pl.pallas_call(
        body, name="delta_bwd", grid=(N, N_HEADS),
        in_specs=[hd, hd, hd, gbs, mat, mat, hd],
        out_specs=[hd, hd, hd, gbs],
        out_shape=[jax.ShapeDtypeStruct((S, N_HEADS * D_HEAD), F32)] * 3 + [jax.ShapeDtypeStruct((S, LANES), F32)],
        scratch_shapes=[pltpu.VMEM((N_HEADS, D_HEAD, D_HEAD), F32)],
        compiler_params=_cparams(est, ("arbitrary", "arbitrary")),
    )(q, k, v, gb, st, tinv, do)


def _ya_head(o, z, onw):
    return o * lax.rsqrt(jnp.mean(o * o, -1, keepdims=True) + RMS_EPS) * onw * _silu(z)


def _sgu_pre(u, vg, sg, sb):
    return _gelu(u), _ln(_gelu(vg), sg, sb)


def _chunk_causal(shape, di, dj):
    sh = jnp.int32(int(math.log2(SGU_CHUNK)))
    return lax.shift_right_logical(_iota(shape, di), sh) >= lax.shift_right_logical(_iota(shape, dj), sh)


def _ws_masked(ws):
    return jnp.where(_chunk_causal(ws.shape, 1, 2), ws, 0.0)


def _mix_prep(o, proj, onw, sg, sb, ws, bst, *, tm=256):
    S = o.shape[0]
    tm = _tile(S, tm, SGU_BLOCK)

    def body(o_ref, z_ref, u_ref, vg_ref, onw_ref, sg_ref, sb_ref, ws_ref, bst_ref, ya_ref, yb_ref):
        onw = onw_ref[...]
        for h in range(N_HEADS):
            sl = slice(h * D_HEAD, (h + 1) * D_HEAD)
            ya_ref[:, sl] = _ya_head(o_ref[:, sl], z_ref[:, sl], onw).astype(BF)
        ua, vl = _sgu_pre(u_ref[...], vg_ref[...], sg_ref[...], sb_ref[...])
        wsm = _ws_masked(ws_ref[...])
        bst = bst_ref[...]
        for blk in range(tm // SGU_BLOCK):
            rs = slice(blk * SGU_BLOCK, (blk + 1) * SGU_BLOCK)
            for gi in range(SGU_GROUPS):
                cs = slice(gi * D_HEAD, (gi + 1) * D_HEAD)
                sp = _dot(wsm[gi], vl[rs, cs]) + bst[:, gi:gi + 1]
                yb_ref[rs, cs] = (ua[rs, cs] * sp).astype(BF)

    blk = lambda col: pl.BlockSpec((tm, D_MODEL), lambda i: (i, col))
    full = lambda shape: pl.BlockSpec(shape, lambda i: (0,) * len(shape))
    est = 10 * _nbytes((tm, D_MODEL), F32)
    return pl.pallas_call(
        body, name="mix_prep", grid=(S // tm,),
        in_specs=[blk(0), blk(3), blk(4), blk(5), full((1, D_HEAD)), full((1, D_MODEL)), full((1, D_MODEL)),
                  full((SGU_GROUPS, SGU_BLOCK, SGU_BLOCK)), full((SGU_BLOCK, LANES))],
        out_specs=[blk(0), blk(0)],
        out_shape=[jax.ShapeDtypeStruct((S, D_MODEL), BF)] * 2,
        compiler_params=_cparams(est, ("parallel",)),
    )(o, proj, proj, proj, onw, sg, sb, ws, bst)


def _mix_prep_bwd(o, proj, onw, sg, sb, ws, bst, dya, dyb, *, tm=256):
    S = o.shape[0]
    tm = _tile(S, tm, SGU_BLOCK)

    def body(o_ref, z_ref, u_ref, vg_ref, onw_ref, sg_ref, sb_ref, ws_ref, bst_ref, dya_ref, dyb_ref,
             do_ref, dz_ref, du_ref, dvg_ref, donw_ref, dsg_ref, dsb_ref, dws_ref, dbst_ref, dvl_scr, dua_scr):
        @pl.when(pl.program_id(0) == 0)
        def _():
            for r in (donw_ref, dsg_ref, dsb_ref, dws_ref, dbst_ref):
                r[...] = jnp.zeros_like(r)

        onw = onw_ref[...]
        donw = jnp.zeros((1, D_HEAD), F32)
        for h in range(N_HEADS):
            sl = slice(h * D_HEAD, (h + 1) * D_HEAD)
            _, vj = jax.vjp(_ya_head, o_ref[:, sl], z_ref[:, sl], onw)
            do_h, dz_h, donw_h = vj(dya_ref[:, sl])
            do_ref[:, sl] = do_h
            dz_ref[:, sl] = dz_h.astype(BF)
            donw = donw + donw_h
        donw_ref[...] += _bcast_rows(donw)

        (ua, vl), vj = jax.vjp(_sgu_pre, u_ref[...], vg_ref[...], sg_ref[...], sb_ref[...])
        wsm = _ws_masked(ws_ref[...])
        bst = bst_ref[...]
        lane = _iota((SGU_BLOCK, LANES), 1)
        dbst = jnp.zeros((SGU_BLOCK, LANES), F32)
        cmask = _chunk_causal((SGU_BLOCK, SGU_BLOCK), 0, 1)
        for gi in range(SGU_GROUPS):
            cs = slice(gi * D_HEAD, (gi + 1) * D_HEAD)
            wg = wsm[gi]
            wgt = jnp.transpose(wg)
            dwg = jnp.zeros((SGU_BLOCK, SGU_BLOCK), F32)
            for blk in range(tm // SGU_BLOCK):
                rs = slice(blk * SGU_BLOCK, (blk + 1) * SGU_BLOCK)
                sp = _dot(wg, vl[rs, cs]) + bst[:, gi:gi + 1]
                dyb = dyb_ref[rs, cs]
                dsp = dyb * ua[rs, cs]
                dua_scr[rs, cs] = dyb * sp
                dvl_scr[rs, cs] = _dot(wgt, dsp)
                dwg = dwg + _dot_nt(dsp, vl[rs, cs])
                dbst = dbst + jnp.where(lane == gi, jnp.sum(dsp, -1, keepdims=True), 0.0)
            dws_ref[gi] += jnp.where(cmask, dwg, 0.0)
        dbst_ref[...] += dbst
        du, dvg, dsg, dsb = vj((dua_scr[...], dvl_scr[...]))
        du_ref[...] = du.astype(BF)
        dvg_ref[...] = dvg.astype(BF)
        dsg_ref[...] += _bcast_rows(dsg)
        dsb_ref[...] += _bcast_rows(dsb)

    blk = lambda col: pl.BlockSpec((tm, D_MODEL), lambda i: (i, col))
    full = lambda shape: pl.BlockSpec(shape, lambda i: (0,) * len(shape))
    est = 16 * _nbytes((tm, D_MODEL), F32)
    outs = pl.pallas_call(
        body, name="mix_prep_bwd", grid=(S // tm,),
        in_specs=[blk(0), blk(3), blk(4), blk(5), full((1, D_HEAD)), full((1, D_MODEL)), full((1, D_MODEL)),
                  full((SGU_GROUPS, SGU_BLOCK, SGU_BLOCK)), full((SGU_BLOCK, LANES)), blk(0), blk(0)],
        out_specs=[blk(0)] * 4 + [full((SUBLANES, D_HEAD)), full((SUBLANES, D_MODEL)), full((SUBLANES, D_MODEL)),
                                  full((SGU_GROUPS, SGU_BLOCK, SGU_BLOCK)), full((SGU_BLOCK, LANES))],
        out_shape=[jax.ShapeDtypeStruct((S, D_MODEL), F32)] + [jax.ShapeDtypeStruct((S, D_MODEL), BF)] * 3
                  + [jax.ShapeDtypeStruct((SUBLANES, D_HEAD), F32), jax.ShapeDtypeStruct((SUBLANES, D_MODEL), F32),
                     jax.ShapeDtypeStruct((SUBLANES, D_MODEL), F32),
                     jax.ShapeDtypeStruct((SGU_GROUPS, SGU_BLOCK, SGU_BLOCK), F32),
                     jax.ShapeDtypeStruct((SGU_BLOCK, LANES), F32)],
        scratch_shapes=[pltpu.VMEM((tm, D_MODEL), F32)] * 2,
        compiler_params=_cparams(est, ("arbitrary",)),
    )(o, proj, proj, proj, onw, sg, sb, ws, bst, dya, dyb)
    return outs


def _gate_merge(pa, pb, proj, *, tm=512):
    S = pa.shape[0]
    tm = _tile(S, tm, SUBLANES * 2)

    def body(pa_ref, pb_ref, ga_ref, gb_ref, m_ref):
        m_ref[...] = (_sigmoid(ga_ref[...]) * pa_ref[...] + _sigmoid(gb_ref[...]) * pb_ref[...]).astype(BF)

    blk = lambda col: pl.BlockSpec((tm, D_MODEL), lambda i: (i, col))
    return pl.pallas_call(
        body, name="gate_merge", grid=(S // tm,),
        in_specs=[blk(0), blk(0), blk(6), blk(7)], out_specs=blk(0),
        out_shape=jax.ShapeDtypeStruct((S, D_MODEL), BF),
        compiler_params=_cparams(6 * _nbytes((tm, D_MODEL), F32), ("parallel",)),
    )(pa, pb, proj, proj)


def _gate_merge_bwd(pa, pb, proj, dm, *, tm=512):
    S = pa.shape[0]
    tm = _tile(S, tm, SUBLANES * 2)

    def body(pa_ref, pb_ref, ga_ref, gb_ref, dm_ref, dpa_ref, dpb_ref, dga_ref, dgb_ref):
        dm = dm_ref[...]
        sa, sb = _sigmoid(ga_ref[...]), _sigmoid(gb_ref[...])
        dpa_ref[...] = (dm * sa).astype(BF)
        dpb_ref[...] = (dm * sb).astype(BF)
        dga_ref[...] = (dm * pa_ref[...] * sa * (1.0 - sa)).astype(BF)
        dgb_ref[...] = (dm * pb_ref[...] * sb * (1.0 - sb)).astype(BF)

    blk = lambda col: pl.BlockSpec((tm, D_MODEL), lambda i: (i, col))
    return pl.pallas_call(
        body, name="gate_merge_bwd", grid=(S // tm,),
        in_specs=[blk(0), blk(0), blk(6), blk(7), blk(0)], out_specs=[blk(0)] * 4,
        out_shape=[jax.ShapeDtypeStruct((S, D_MODEL), BF)] * 4,
        compiler_params=_cparams(10 * _nbytes((tm, D_MODEL), F32), ("parallel",)),
    )(pa, pb, proj, proj, dm)


def _swiglu_act(hgu, *, tm=256):
    S = hgu.shape[0]
    tm = _tile(S, tm, SUBLANES * 2)

    def body(hg_ref, hu_ref, h_ref):
        h_ref[...] = (_silu(hg_ref[...]) * hu_ref[...]).astype(BF)

    blk = lambda col: pl.BlockSpec((tm, FFN_HIDDEN), lambda i: (i, col))
    return pl.pallas_call(
        body, name="swiglu_act", grid=(S // tm,),
        in_specs=[blk(0), blk(1)], out_specs=blk(0),
        out_shape=jax.ShapeDtypeStruct((S, FFN_HIDDEN), BF),
        compiler_params=_cparams(5 * _nbytes((tm, FFN_HIDDEN), F32), ("parallel",)),
    )(hgu, hgu)


def _swiglu_bwd(hgu, dh, *, tm=256):
    S = hgu.shape[0]
    tm = _tile(S, tm, SUBLANES * 2)

    def body(hg_ref, hu_ref, dh_ref, d_ref):
        hg, dh = hg_ref[...], dh_ref[...]
        d_ref[:, :FFN_HIDDEN] = (dh * hu_ref[...] * _dsilu(hg)).astype(BF)
        d_ref[:, FFN_HIDDEN:] = (dh * _silu(hg)).astype(BF)

    blk = lambda col: pl.BlockSpec((tm, FFN_HIDDEN), lambda i: (i, col))
    return pl.pallas_call(
        body, name="swiglu_bwd", grid=(S // tm,),
        in_specs=[blk(0), blk(1), blk(0)], out_specs=pl.BlockSpec((tm, 2 * FFN_HIDDEN), lambda i: (i, 0)),
        out_shape=jax.ShapeDtypeStruct((S, 2 * FFN_HIDDEN), BF),
        compiler_params=_cparams(8 * _nbytes((tm, FFN_HIDDEN), F32), ("parallel",)),
    )(hgu, hgu, dh)


def _resid_ln(x, r, g, b, *, tm=512):
    S = x.shape[0]
    tm = _tile(S, tm, SUBLANES)

    def body(x_ref, r_ref, g_ref, b_ref, y_ref):
        y_ref[...] = _ln(ALPHA * x_ref[...] + r_ref[...], g_ref[...], b_ref[...])

    blk = pl.BlockSpec((tm, D_MODEL), lambda i: (i, 0))
    vec = pl.BlockSpec((1, D_MODEL), lambda i: (0, 0))
    return pl.pallas_call(
        body, name="resid_ln", grid=(S // tm,),
        in_specs=[blk, blk, vec, vec], out_specs=blk,
        out_shape=jax.ShapeDtypeStruct((S, D_MODEL), F32),
        compiler_params=_cparams(6 * _nbytes((tm, D_MODEL), F32), ("parallel",)),
    )(x, r, g, b)


def _resid_ln_bwd(x, r, g, b, dy, *, tm=512):
    S = x.shape[0]
    tm = _tile(S, tm, SUBLANES)

    def body(x_ref, r_ref, g_ref, b_ref, dy_ref, dp_ref, dg_ref, db_ref):
        @pl.when(pl.program_id(0) == 0)
        def _():
            dg_ref[...] = jnp.zeros_like(dg_ref)
            db_ref[...] = jnp.zeros_like(db_ref)

        _, vj = jax.vjp(_ln, ALPHA * x_ref[...] + r_ref[...], g_ref[...], b_ref[...])
        dp, dg, db = vj(dy_ref[...])
        dp_ref[...] = dp
        dg_ref[...] += _bcast_rows(dg)
        db_ref[...] += _bcast_rows(db)

    blk = pl.BlockSpec((tm, D_MODEL), lambda i: (i, 0))
    vec = pl.BlockSpec((1, D_MODEL), lambda i: (0, 0))
    acc = pl.BlockSpec((SUBLANES, D_MODEL), lambda i: (0, 0))
    return pl.pallas_call(
        body, name="resid_ln_bwd", grid=(S // tm,),
        in_specs=[blk, blk, vec, vec, blk], out_specs=[blk, acc, acc],
        out_shape=[jax.ShapeDtypeStruct((S, D_MODEL), F32)] + [jax.ShapeDtypeStruct((SUBLANES, D_MODEL), F32)] * 2,
        compiler_params=_cparams(10 * _nbytes((tm, D_MODEL), F32), ("arbitrary",)),
    )(x, r, g, b, dy)


def _loss_head(y, tgt, *, tm=512):
    S = y.shape[0]
    tm = _tile(S, tm, SUBLANES)

    def body(y_ref, t_ref, dy_ref, l_ref):
        @pl.when(pl.program_id(0) == 0)
        def _():
            l_ref[...] = jnp.zeros_like(l_ref)

        e = y_ref[...] - t_ref[...]
        dy_ref[...] = e * (1.0 / D_MODEL)
        l_ref[...] += 0.5 * jnp.sum(jnp.mean(e * e, -1, keepdims=True), keepdims=True)

    blk = pl.BlockSpec((tm, D_MODEL), lambda i: (i, 0))
    return pl.pallas_call(
        body, name="loss_head", grid=(S // tm,),
        in_specs=[blk, blk], out_specs=[blk, pl.BlockSpec((SUBLANES, LANES), lambda i: (0, 0))],
        out_shape=[jax.ShapeDtypeStruct((S, D_MODEL), F32), jax.ShapeDtypeStruct((SUBLANES, LANES), F32)],
        compiler_params=_cparams(6 * _nbytes((tm, D_MODEL), F32), ("arbitrary",)),
    )(y, tgt)


def _layer_fwd(x, w):
    proj = _mm(x, w["win"], mode="nn", name="mm_in", tn=1024)
    ba = _mm(x, w["wba"], mode="nn", name="mm_in_ba", tm=1024, tn=LANES)
    qn, kn, vv, gb = _qkv_prep(proj, ba, w["convw"], w["arow"], w["dtrow"])
    o, st, tinv = _delta_fwd(qn, kn, vv, gb)
    ya, yb = _mix_prep(o, proj, w["onw"], w["sg"], w["sb"], w["ws"], w["bst"])
    pa = _mm(ya, w["wpa"], mode="nn", name="mm_sq")
    pb = _mm(yb, w["wpb"], mode="nn", name="mm_sq")
    m = _gate_merge(pa, pb, proj)
    mix = _mm(m, w["wo"], mode="nn", name="mm_sq")
    x1 = _resid_ln(x, mix, w["ln1g"], w["ln1b"])
    hgu = _mm(x1, w["wgu"], mode="nn", name="mm_gu", tn=1408)
    h = _swiglu_act(hgu)
    ffn = _mm(h, w["wd"], mode="nn", name="mm_down", tk=FFN_HIDDEN)
    x2 = _resid_ln(x1, ffn, w["ln2g"], w["ln2b"])
    saved = dict(x=x, proj=proj, ba=ba, qn=qn, kn=kn, vv=vv, gb=gb, o=o, st=st, tinv=tinv, ya=ya, yb=yb,
                 pa=pa, pb=pb, m=m, mix=mix, x1=x1, hgu=hgu, h=h, ffn=ffn)
    return x2, saved


def _layer_bwd(dx2, w, s):
    g = {}
    dpre2, g["ln2g"], g["ln2b"] = _resid_ln_bwd(s["x1"], s["ffn"], w["ln2g"], w["ln2b"], dx2)
    dh = _mm(dpre2, w["wd"], mode="nt", name="mm_nt_down", tn=1408)
    g["wd"] = _mm(s["h"], dpre2, mode="tn", name="mm_tn_down", tm=1408, tk=512)
    dhgu = _swiglu_bwd(s["hgu"], dh)
    dx1 = _mm(dhgu, w["wgu"], mode="nt", name="mm_nt_gu", add=dpre2, add_scale=ALPHA, tk=1408)
    g["wgu"] = _mm(s["x1"], dhgu, mode="tn", name="mm_tn_gu", tm=1024, tn=1408, tk=512)
    dpre1, g["ln1g"], g["ln1b"] = _resid_ln_bwd(s["x"], s["mix"], w["ln1g"], w["ln1b"], dx1)
    dm = _mm(dpre1, w["wo"], mode="nt", name="mm_nt_sq")
    g["wo"] = _mm(s["m"], dpre1, mode="tn", name="mm_tn_sq", tm=1024, tk=512)
    dpa, dpb, dga, dgb_gate = _gate_merge_bwd(s["pa"], s["pb"], s["proj"], dm)
    dya = _mm(dpa, w["wpa"], mode="nt", name="mm_nt_sq")
    g["wpa"] = _mm(s["ya"], dpa, mode="tn", name="mm_tn_sq", tm=1024, tk=512)
    dyb = _mm(dpb, w["wpb"], mode="nt", name="mm_nt_sq")
    g["wpb"] = _mm(s["yb"], dpb, mode="tn", name="mm_tn_sq", tm=1024, tk=512)
    do, dz, du, dvg, g["onw"], g["sg"], g["sb"], g["ws"], g["bst"] = _mix_prep_bwd(
        s["o"], s["proj"], w["onw"], w["sg"], w["sb"], w["ws"], w["bst"], dya, dyb)
    dqn, dkn, dvv, dgb = _delta_bwd(s["qn"], s["kn"], s["vv"], s["gb"], s["st"], s["tinv"], do)
    dc, dba, g["convw"], g["arow"], g["dtrow"] = _qkv_prep_bwd(
        s["proj"], s["ba"], w["convw"], w["arow"], w["dtrow"], dqn, dkn, dvv, dgb)
    dqkv = _conv_bwd(dc, w["convw"])
    dproj = jnp.concatenate([dqkv, dz, du, dvg, dga, dgb_gate], axis=1)
    dx = _mm(dproj, w["win"], mode="nt", name="mm_nt_in", add=dpre1, add_scale=ALPHA, tk=1024)
    dx = _mm(dba, w["wba"], mode="nt", name="mm_nt_ba", add=dx, add_scale=1.0, tm=1024)
    g["win"] = _mm(s["x"], dproj, mode="tn", name="mm_tn_in", tm=1024, tn=1024, tk=512)
    g["wba"] = _mm(s["x"], dba, mode="tn", name="mm_tn_ba", tm=1024, tn=LANES, tk=1024)
    return dx, g


def _local_step(x, tgt, ws):
    saved = []
    for w in ws:
        x, s = _layer_fwd(x, w)
        saved.append(s)
    dy, lacc = _loss_head(x, tgt)
    grads = [None] * len(ws)
    for l in reversed(range(len(ws))):
        dy, grads[l] = _layer_bwd(dy, ws[l], saved[l])
    return lacc[0, 0], dy, grads


_QKVZ = 4 * D_MODEL
_BA = 2 * N_HEADS


def _lane_row(v, at):
    return jnp.zeros((1, LANES), F32).at[0, at:at + v.shape[0]].set(v)


def _to_kernel_layout(p):
    w_in = p["w_in"]
    return dict(
        win=jnp.concatenate([w_in[:, :_QKVZ], w_in[:, _QKVZ + _BA:]], axis=1).astype(BF),
        wba=jnp.pad(w_in[:, _QKVZ:_QKVZ + _BA], ((0, 0), (0, LANES - _BA))).astype(BF),
        convw=p["conv_w"].astype(F32),
        arow=_lane_row(p["a_log"], N_HEADS), dtrow=_lane_row(p["dt_bias"], N_HEADS),
        onw=p["o_norm_w"][None], sg=p["sgu_ln_g"][None], sb=p["sgu_ln_b"][None],
        ws=p["w_s"], bst=jnp.pad(p["b_s"].T, ((0, 0), (0, LANES - SGU_GROUPS))),
        wpa=p["w_pa"].astype(BF), wpb=p["w_pb"].astype(BF), wo=p["w_o"].astype(BF),
        ln1g=p["ln1_g"][None], ln1b=p["ln1_b"][None],
        wgu=jnp.concatenate([p["w_ffn_gate"], p["w_ffn_up"]], axis=1).astype(BF),
        wd=p["w_ffn_down"].astype(BF),
        ln2g=p["ln2_g"][None], ln2b=p["ln2_b"][None],
    )


def _from_kernel_layout(g):
    return dict(
        w_in=jnp.concatenate([g["win"][:, :_QKVZ], g["wba"][:, :_BA], g["win"][:, _QKVZ:]], axis=1),
        conv_w=g["convw"][:CONV_K],
        a_log=g["arow"][0, N_HEADS:2 * N_HEADS], dt_bias=g["dtrow"][0, N_HEADS:2 * N_HEADS],
        o_norm_w=g["onw"][0], sgu_ln_g=g["sg"][0], sgu_ln_b=g["sb"][0],
        w_s=g["ws"], b_s=g["bst"][:, :SGU_GROUPS].T,
        w_pa=g["wpa"], w_pb=g["wpb"], w_o=g["wo"],
        ln1_g=g["ln1g"][0], ln1_b=g["ln1b"][0],
        w_ffn_gate=g["wgu"][:, :FFN_HIDDEN], w_ffn_up=g["wgu"][:, FFN_HIDDEN:], w_ffn_down=g["wd"],
        ln2_g=g["ln2g"][0], ln2_b=g["ln2b"][0],
    )


SHARDED = (("w_in", (DEPTH, D_MODEL, N_IN // N_DEV), 1), ("conv_w", (DEPTH, CONV_K, 3 * D_MODEL // N_DEV), 1),
           ("w_pa", (DEPTH, D_MODEL // N_DEV, D_MODEL), 0), ("w_pb", (DEPTH, D_MODEL // N_DEV, D_MODEL), 0),
           ("w_o", (DEPTH, D_MODEL // N_DEV, D_MODEL), 0),
           ("w_ffn_gate", (DEPTH, D_MODEL, FFN_HIDDEN // N_DEV), 1), ("w_ffn_up", (DEPTH, D_MODEL, FFN_HIDDEN // N_DEV), 1),
           ("w_ffn_down", (DEPTH, FFN_HIDDEN // N_DEV, D_MODEL), 0))
REPLICATED = (("a_log", (DEPTH, N_HEADS)), ("dt_bias", (DEPTH, N_HEADS)), ("o_norm_w", (DEPTH, D_HEAD)),
              ("sgu_ln_g", (DEPTH, D_MODEL)), ("sgu_ln_b", (DEPTH, D_MODEL)),
              ("w_s", (DEPTH, SGU_GROUPS, SGU_BLOCK, SGU_BLOCK)), ("b_s", (DEPTH, SGU_GROUPS, SGU_BLOCK)),
              ("ln1_g", (DEPTH, D_MODEL)), ("ln1_b", (DEPTH, D_MODEL)), ("ln2_g", (DEPTH, D_MODEL)), ("ln2_b", (DEPTH, D_MODEL)))
WEIGHT_NAMES = ("w_in", "conv_w", "a_log", "dt_bias", "o_norm_w", "sgu_ln_g", "sgu_ln_b", "w_s", "b_s", "w_pa", "w_pb",
                "w_o", "ln1_g", "ln1_b", "w_ffn_gate", "w_ffn_up", "w_ffn_down", "ln2_g", "ln2_b")
ADAM_TILE_ROWS = 1024
_N_LOCAL = sum(math.prod(s) for _, s, _ in SHARDED) + sum(math.prod(s) for _, s in REPLICATED)
PACK_ROWS = -(-_N_LOCAL // (LANES * ADAM_TILE_ROWS)) * ADAM_TILE_ROWS


def _rows(flat, rows, axis=-1):
    pad = rows * LANES - flat.shape[-1]
    flat = jnp.pad(flat, [(0, 0)] * (flat.ndim - 1) + [(0, pad)])
    return flat.reshape(flat.shape[:-1] + (rows, LANES))


def _pack_local(p):
    flat = jnp.concatenate([p[n].reshape(-1) for n, _, _ in SHARDED] + [p[n].reshape(-1) for n, _ in REPLICATED])
    return _rows(flat, PACK_ROWS)


def _unpack_local(rows):
    flat = rows.reshape(-1)
    out, off = {}, 0
    for n, shape in [(n, s) for n, s, _ in SHARDED] + list(REPLICATED):
        size = math.prod(shape)
        out[n] = flat[off:off + size].reshape(shape)
        off += size
    return out


def _pack_gather(p):
    parts = [p[n].astype(BF).reshape(-1) for n, _, _ in SHARDED if n != "conv_w"]
    parts.append(lax.bitcast_convert_type(p["conv_w"], BF).reshape(-1))
    flat = jnp.concatenate(parts)
    return _rows(flat, -(-flat.shape[0] // (LANES * 16)) * 16)


def _full_matrix(blocks, axis):
    n, r, c = blocks.shape
    if axis == 0:
        return blocks.reshape(n * r, c)
    return jnp.transpose(blocks, (1, 0, 2)).reshape(r, n * c)


def _unpack_gather(gathered, p):
    flat = gathered.reshape(N_DEV, -1)
    full, off = {}, 0
    for n, shape, axis in SHARDED:
        if n == "conv_w":
            continue
        size = math.prod(shape)
        blocks = flat[:, off:off + size].reshape((N_DEV,) + shape)
        full[n] = [_full_matrix(blocks[:, l], axis) for l in range(DEPTH)]
        off += size
    shape = dict((n, s) for n, s, _ in SHARDED)["conv_w"]
    size = math.prod(shape)
    cw = lax.bitcast_convert_type(flat[:, off:off + 2 * size].reshape(N_DEV, size, 2), F32).reshape((N_DEV,) + shape)
    full["conv_w"] = [_full_matrix(cw[:, l], 1) for l in range(DEPTH)]
    layers = []
    for l in range(DEPTH):
        layer = {n: full[n][l] for n, _, _ in SHARDED}
        layer.update({n: p[n][l] for n, _ in REPLICATED})
        layers.append(_to_kernel_layout(layer))
    return layers


def _pack_grads(grads):
    ref = [_from_kernel_layout(g) for g in grads]
    parts = []
    for n, shape, axis in SHARDED:
        per_layer = []
        for l in range(DEPTH):
            full = ref[l][n]
            r, c = shape[1], shape[2]
            if axis == 0:
                per_layer.append(full.reshape(N_DEV, r * c))
            else:
                per_layer.append(jnp.transpose(full.reshape(r, N_DEV, c), (1, 0, 2)).reshape(N_DEV, r * c))
        parts.append(jnp.concatenate(per_layer, axis=1))
    small = jnp.concatenate([jnp.stack([ref[l][n] for l in range(DEPTH)]).reshape(-1) for n, _ in REPLICATED])
    parts.append(jnp.broadcast_to(small[None], (N_DEV, small.shape[0])))
    return _rows(jnp.concatenate(parts, axis=1), PACK_ROWS)


def _mesh_place():
    x, y, c = (lax.axis_index(a) for a in MESH_AXES)
    return x, y, c


def _slot(x, y, c):
    return 4 * x + 2 * y + c


def _peer(place, j):
    x, y, c = place
    return (1 - x if j & 4 else x, 1 - y if j & 2 else y, 1 - c if j & 1 else c)


def _all_to_all_copy(src_of, dst_ref, send_sems, recv_sems, local_sem):
    place = _mesh_place()
    me = _slot(*place)
    local = pltpu.make_async_copy(src_of(me), dst_ref.at[me], local_sem)
    local.start()
    sends = []
    for j in range(1, N_DEV):
        peer = _peer(place, j)
        cp = pltpu.make_async_remote_copy(
            src_ref=src_of(_slot(*peer)), dst_ref=dst_ref.at[me], send_sem=send_sems.at[j - 1], recv_sem=recv_sems.at[j - 1],
            device_id=peer, device_id_type=pl.DeviceIdType.MESH)
        cp.start()
        sends.append(cp)
    for j in range(1, N_DEV):
        peer = _peer(place, j)
        pltpu.make_async_remote_copy(
            src_ref=src_of(me), dst_ref=dst_ref.at[_slot(*peer)], send_sem=send_sems.at[j - 1], recv_sem=recv_sems.at[j - 1],
            device_id=peer, device_id_type=pl.DeviceIdType.MESH).wait_recv()
    for cp in sends:
        cp.wait_send()
    local.wait()


def _comm_call(body, name, out_shape, arg):
    any_spec = pl.BlockSpec(memory_space=pl.ANY)
    return pl.pallas_call(
        body, name=name, in_specs=[any_spec], out_specs=any_spec, out_shape=out_shape,
        scratch_shapes=[pltpu.SemaphoreType.DMA((N_DEV - 1,)), pltpu.SemaphoreType.DMA((N_DEV - 1,)),
                        pltpu.SemaphoreType.DMA(())],
    )(arg)


def _all_gather(pack):
    def body(in_ref, out_ref, send_sems, recv_sems, local_sem):
        _all_to_all_copy(lambda slot: in_ref, out_ref, send_sems, recv_sems, local_sem)

    return _comm_call(body, "all_gather", jax.ShapeDtypeStruct((N_DEV,) + pack.shape, pack.dtype), pack)


def _exchange(gsend):
    def body(in_ref, out_ref, send_sems, recv_sems, local_sem):
        _all_to_all_copy(lambda slot: in_ref.at[slot], out_ref, send_sems, recv_sems, local_sem)

    return _comm_call(body, "grad_exchange", jax.ShapeDtypeStruct(gsend.shape, gsend.dtype), gsend)


def _adamw(recv, w, m, v):
    R = w.shape[0]
    tr = ADAM_TILE_ROWS

    def body(r_ref, w_ref, m_ref, v_ref, g_ref, d_ref, nm_ref, nv_ref):
        g = r_ref[0]
        for s in range(1, N_DEV):
            g = g + r_ref[s]
        m = ADAM_B1 * m_ref[...] + (1.0 - ADAM_B1) * g
        v = ADAM_B2 * v_ref[...] + (1.0 - ADAM_B2) * jnp.square(g)
        m_hat = m / (1.0 - ADAM_B1 ** ADAM_STEP)
        v_hat = v / (1.0 - ADAM_B2 ** ADAM_STEP)
        g_ref[...] = g
        d_ref[...] = -ADAM_LR * (m_hat / (jnp.sqrt(v_hat) + ADAM_EPS) + ADAM_WD * w_ref[...])
        nm_ref[...] = m
        nv_ref[...] = v

    blk = pl.BlockSpec((tr, LANES), lambda i: (i, 0))
    return pl.pallas_call(
        body, name="adamw", grid=(R // tr,),
        in_specs=[pl.BlockSpec((N_DEV, tr, LANES), lambda i: (0, i, 0)), blk, blk, blk],
        out_specs=[blk] * 4, out_shape=[jax.ShapeDtypeStruct((R, LANES), F32)] * 4,
        compiler_params=_cparams(16 * _nbytes((tr, LANES), F32), ("parallel",)),
    )(recv, w, m, v)


def kernel(x, w_in, conv_w, a_log, dt_bias, o_norm_w, sgu_ln_g, sgu_ln_b, w_s, b_s, w_pa, w_pb, w_o, ln1_g, ln1_b, w_ffn_gate, w_ffn_up, w_ffn_down, ln2_g, ln2_b, loss_target, m_w_in, m_conv_w, m_a_log, m_dt_bias, m_o_norm_w, m_sgu_ln_g, m_sgu_ln_b, m_w_s, m_b_s, m_w_pa, m_w_pb, m_w_o, m_ln1_g, m_ln1_b, m_w_ffn_gate, m_w_ffn_up, m_w_ffn_down, m_ln2_g, m_ln2_b, v_w_in, v_conv_w, v_a_log, v_dt_bias, v_o_norm_w, v_sgu_ln_g, v_sgu_ln_b, v_w_s, v_b_s, v_w_pa, v_w_pb, v_w_o, v_ln1_g, v_ln1_b, v_w_ffn_gate, v_w_ffn_up, v_w_ffn_down, v_ln2_g, v_ln2_b):
    given = dict(locals())
    P = {n: given[n] for n in WEIGHT_NAMES}
    M = {n: given["m_" + n] for n in WEIGHT_NAMES}
    V = {n: given["v_" + n] for n in WEIGHT_NAMES}

    layers = _unpack_gather(_all_gather(_pack_gather(P)), P)
    loss_local, dx, grads = _local_step(x[0], loss_target[0], layers)
    loss = lax.psum(loss_local, MESH_AXES)

    recv = _exchange(_pack_grads(grads))
    outs = [_unpack_local(o) for o in _adamw(recv, _pack_local(P), _pack_local(M), _pack_local(V))]
    return (loss, dx[None], *[o[n] for o in outs for n in WEIGHT_NAMES])
```

```python
import functools
import math

import jax
import jax.numpy as jnp
from jax import lax
from jax.experimental import pallas as pl
from jax.experimental.pallas import tpu as pltpu

F32 = jnp.float32
BF = jnp.bfloat16
HIGHEST = lax.Precision.HIGHEST

D_MODEL = 1024
DEPTH = 2
N_HEADS = 8
D_HEAD = 128
CONV_K = 4
SGU_BLOCK = 128
SGU_GROUPS = 8
SGU_CHUNK = 64
FFN_HIDDEN = 2816
N_IN = 8208
N_DEV = 8
IN_BLOCK, IN_PAD = N_IN // N_DEV, 1152
FFN_BLOCK, FFN_PAD = FFN_HIDDEN // N_DEV, 384
FFN_K = N_DEV * FFN_PAD
ALPHA = (2 * DEPTH) ** 0.25
LN_EPS = 1e-5
RMS_EPS = 1e-6
ADAM_LR, ADAM_B1, ADAM_B2, ADAM_EPS, ADAM_WD, ADAM_STEP = 0.001, 0.9, 0.999, 1e-08, 0.01, 10

MESH_AXES = ("x", "y", "c")
DELTA_CHUNK = 128
DELTA_HEADS_PER_STEP = 8
LANES = 128
SUBLANES = 8
VMEM_BYTES = 64 * 1024 * 1024
HALO = SUBLANES


def _cparams(est_bytes, dims=None):
    limit = int(min(max(2 * est_bytes + (8 << 20), 32 << 20), VMEM_BYTES - (6 << 20)))
    kw = dict(vmem_limit_bytes=limit)
    if dims is not None:
        kw["dimension_semantics"] = dims
    return pltpu.CompilerParams(**kw)


def _nbytes(shape, dtype):
    return math.prod(shape) * jnp.dtype(dtype).itemsize


def _dims(kind, ndim):
    lhs, rhs = {"nn": (1, 0), "nt": (1, 1), "tn": (0, 0)}[kind]
    b = ndim - 2
    return (((lhs + b,), (rhs + b,)), (tuple(range(b)), tuple(range(b))))


def _mxu(a, b, kind):
    return lax.dot_general(a, b, _dims(kind, a.ndim), preferred_element_type=F32)


def _dot(a, b):
    return _mxu(a.astype(BF), b.astype(BF), "nn")


def _dot_nt(a, b):
    return _mxu(a.astype(BF), b.astype(BF), "nt")


def _dot_tn(a, b):
    return _mxu(a.astype(BF), b.astype(BF), "tn")


def _split(a):
    hi = a.astype(BF)
    return hi, (a - hi.astype(F32)).astype(BF)


def _dot3(a, b, kind):
    (ah, al), (bh, bl) = _split(a), _split(b)
    return _mxu(ah, bh, kind) + (_mxu(ah, bl, kind) + _mxu(al, bh, kind))


def _dotf(a, b):
    return _dot3(a, b, "nn")


def _dotf_nt(a, b):
    return _dot3(a, b, "nt")


def _dotf_tn(a, b):
    return _dot3(a, b, "tn")


def _dot01(sel, x, kind="nn"):
    s = jnp.broadcast_to(sel.astype(BF), x.shape[:-2] + sel.shape)
    h1 = x.astype(BF)
    r1 = x - h1.astype(F32)
    h2 = r1.astype(BF)
    h3 = (r1 - h2.astype(F32)).astype(BF)
    return _mxu(s, h1, kind) + (_mxu(s, h2, kind) + _mxu(s, h3, kind))


def _sigmoid(x):
    return jax.nn.sigmoid(x)


def _silu(x):
    return x * _sigmoid(x)


def _dsilu(x):
    s = _sigmoid(x)
    return s * (1.0 + x * (1.0 - s))


def _gelu(x):
    return 0.5 * x * (1.0 + lax.erf(x * 0.7071067811865476))


def _softplus(x):
    return jnp.maximum(x, 0.0) + jnp.log1p(jnp.exp(-jnp.abs(x)))


def _ln(x, g, b):
    mu = jnp.mean(x, -1, keepdims=True)
    xc = x - mu
    var = jnp.mean(xc * xc, -1, keepdims=True)
    return xc * lax.rsqrt(var + LN_EPS) * g + b


def _iota(shape, dim):
    return lax.broadcasted_iota(jnp.int32, shape, dim)


def _tile(n, pref, align):
    if n <= pref:
        return n
    t = (pref // align) * align
    while t >= align:
        if n % t == 0:
            return t
        t -= align
    raise ValueError(f"no tile for {n} (pref {pref}, align {align})")


def _bcast_rows(v, rows=SUBLANES):
    return jnp.broadcast_to(v, (rows, v.shape[-1]))


def _mm(a, b, *, mode, name, out_dtype=F32, add=None, add_scale=1.0, tm=512, tn=1024, tk=1024):
    if mode == "nn":
        (M, K), N = a.shape, b.shape[1]
    elif mode == "nt":
        (M, K), N = a.shape, b.shape[0]
    else:
        (K, M), N = a.shape, b.shape[1]
    tm = _tile(M, tm, LANES if mode == "tn" else SUBLANES * 2)
    tn = _tile(N, tn, LANES)
    tk = _tile(K, tk, LANES)
    nk = K // tk
    if mode == "nn":
        a_spec = pl.BlockSpec((tm, tk), lambda i, j, k: (i, k))
        b_spec = pl.BlockSpec((tk, tn), lambda i, j, k: (k, j))
        dot = _dot
    elif mode == "nt":
        a_spec = pl.BlockSpec((tm, tk), lambda i, j, k: (i, k))
        b_spec = pl.BlockSpec((tn, tk), lambda i, j, k: (j, k))
        dot = _dot_nt
    else:
        a_spec = pl.BlockSpec((tk, tm), lambda i, j, k: (k, i))
        b_spec = pl.BlockSpec((tk, tn), lambda i, j, k: (k, j))
        dot = _dot_tn
    o_spec = pl.BlockSpec((tm, tn), lambda i, j, k: (i, j))
    has_add = add is not None

    def body(*refs):
        if has_add:
            a_ref, b_ref, add_ref, o_ref, acc_ref = refs
        else:
            a_ref, b_ref, o_ref, acc_ref = refs
            add_ref = None
        k = pl.program_id(2)
        part = dot(a_ref[...], b_ref[...])

        def finish(total):
            if has_add:
                total = total + add_scale * add_ref[...]
            o_ref[...] = total.astype(out_dtype)

        if nk == 1:
            finish(part)
        else:
            @pl.when(k == 0)
            def _():
                acc_ref[...] = part

            @pl.when(jnp.logical_and(k > 0, k < nk - 1))
            def _():
                acc_ref[...] += part

            @pl.when(k == nk - 1)
            def _():
                finish(acc_ref[...] + part)

    in_specs = [a_spec, b_spec] + ([o_spec] if has_add else [])
    args = (a, b) + ((add,) if has_add else ())
    est = (_nbytes((tm, tk), a.dtype) + _nbytes((tk, tn), b.dtype) + 2 * _nbytes((tm, tn), F32)
           + (_nbytes((tm, tn), F32) if has_add else 0)) + 2 * _nbytes((tm, tn), F32)
    return pl.pallas_call(
        body, name=name,
        grid=(M // tm, N // tn, nk),
        in_specs=in_specs, out_specs=o_spec,
        out_shape=jax.ShapeDtypeStruct((M, N), out_dtype),
        scratch_shapes=[pltpu.VMEM((tm, tn) if nk > 1 else (SUBLANES, LANES), F32)],
        compiler_params=_cparams(est, ("parallel", "parallel", "arbitrary")),
    )(*args)


def _conv_taps(xt, halo, w_ref, first):
    halo = jnp.where(first, 0.0, halo)
    xc = jnp.concatenate([halo, xt], axis=0)
    shifted = [xt] + [pltpu.roll(xc, s, 0)[HALO:] for s in range(1, CONV_K)]
    out = shifted[0] * w_ref[CONV_K - 1:CONV_K, :]
    for s in range(1, CONV_K):
        out = out + shifted[s] * w_ref[CONV_K - 1 - s:CONV_K - s, :]
    return out, shifted


def _gates(ba, arow, dtrow):
    lane = _iota(ba.shape, 1)
    beta = _sigmoid(ba)
    g = -jnp.exp(arow) * _softplus(ba + dtrow)
    return jnp.where(lane < N_HEADS, beta, jnp.where(lane < 2 * N_HEADS, g, 0.0))


def _l2n(x):
    return x * lax.rsqrt(jnp.sum(x * x, -1, keepdims=True) + RMS_EPS)


def _qkv_prep(proj, ba, convw, arow, dtrow, *, tm=256):
    S = proj.shape[0]
    tm = _tile(S, tm, SUBLANES)
    W3 = 3 * D_MODEL
    hb = tm // HALO

    def body(xt_ref, halo_ref, ba_ref, w_ref, a_ref, dt_ref, q_ref, k_ref, v_ref, gb_ref):
        c, _ = _conv_taps(xt_ref[...], halo_ref[...], w_ref, pl.program_id(0) == 0)
        c = _silu(c)
        for h in range(N_HEADS):
            lo = h * D_HEAD
            q_ref[:, lo:lo + D_HEAD] = _l2n(c[:, lo:lo + D_HEAD])
            k_ref[:, lo:lo + D_HEAD] = _l2n(c[:, D_MODEL + lo:D_MODEL + lo + D_HEAD])
        v_ref[...] = c[:, 2 * D_MODEL:]
        gb_ref[...] = _gates(ba_ref[...], a_ref[...], dt_ref[...])

    row = lambda w, col=0: pl.BlockSpec((tm, w), lambda i: (i, col))
    full = lambda shape: pl.BlockSpec(shape, lambda i: (0,) * len(shape))
    est = 4 * _nbytes((tm, W3), F32)
    return pl.pallas_call(
        body, name="qkv_prep", grid=(S // tm,),
        in_specs=[row(W3), pl.BlockSpec((HALO, W3), lambda i: (jnp.maximum(i * hb - 1, 0), 0)), row(LANES),
                  full((CONV_K, W3)), full((1, LANES)), full((1, LANES))],
        out_specs=[row(D_MODEL), row(D_MODEL), row(D_MODEL), row(LANES)],
        out_shape=[jax.ShapeDtypeStruct((S, D_MODEL), F32)] * 3 + [jax.ShapeDtypeStruct((S, LANES), F32)],
        compiler_params=_cparams(est, ("arbitrary",)),
    )(proj, proj, ba, convw, arow, dtrow)


def _qkv_prep_bwd(proj, ba, convw, arow, dtrow, dq, dk, dv, dgb, *, tm=256):
    S = proj.shape[0]
    tm = _tile(S, tm, SUBLANES)
    W3 = 3 * D_MODEL
    hb = tm // HALO

    def body(xt_ref, halo_ref, ba_ref, w_ref, a_ref, dt_ref, dq_ref, dk_ref, dv_ref, dgb_ref,
             dc_ref, dba_ref, dw_ref, da_ref, ddt_ref):
        i = pl.program_id(0)

        @pl.when(i == 0)
        def _():
            dw_ref[...] = jnp.zeros_like(dw_ref)
            da_ref[...] = jnp.zeros_like(da_ref)
            ddt_ref[...] = jnp.zeros_like(ddt_ref)

        c, shifted = _conv_taps(xt_ref[...], halo_ref[...], w_ref, i == 0)
        a = _silu(c)
        ds = _dsilu(c)
        for h in range(N_HEADS):
            for base, d_ref in ((0, dq_ref), (D_MODEL, dk_ref)):
                lo = base + h * D_HEAD
                _, vj = jax.vjp(_l2n, a[:, lo:lo + D_HEAD])
                (dx,) = vj(d_ref[:, h * D_HEAD:(h + 1) * D_HEAD])
                dc_ref[:, lo:lo + D_HEAD] = dx * ds[:, lo:lo + D_HEAD]
        dc_ref[:, 2 * D_MODEL:] = dv_ref[...] * ds[:, 2 * D_MODEL:]
        dc = dc_ref[...]
        for s in range(CONV_K):
            kk = CONV_K - 1 - s
            dw_ref[kk:kk + 1, :] += jnp.sum(dc * shifted[s], axis=0, keepdims=True)
        _, vj = jax.vjp(_gates, ba_ref[...], a_ref[...], dt_ref[...])
        dba, da, ddt = vj(dgb_ref[...])
        dba_ref[...] = dba.astype(BF)
        da_ref[...] += _bcast_rows(da)
        ddt_ref[...] += _bcast_rows(ddt)

    row = lambda w, col=0: pl.BlockSpec((tm, w), lambda i: (i, col))
    full = lambda shape: pl.BlockSpec(shape, lambda i: (0,) * len(shape))
    est = 8 * _nbytes((tm, W3), F32)
    return pl.pallas_call(
        body, name="qkv_prep_bwd", grid=(S // tm,),
        in_specs=[row(W3), pl.BlockSpec((HALO, W3), lambda i: (jnp.maximum(i * hb - 1, 0), 0)), row(LANES),
                  full((CONV_K, W3)), full((1, LANES)), full((1, LANES)),
                  row(D_MODEL), row(D_MODEL), row(D_MODEL), row(LANES)],
        out_specs=[row(W3), row(LANES), full((SUBLANES, W3)), full((SUBLANES, LANES)), full((SUBLANES, LANES))],
        out_shape=[jax.ShapeDtypeStruct((S, W3), F32), jax.ShapeDtypeStruct((S, LANES), BF),
                   jax.ShapeDtypeStruct((SUBLANES, W3), F32), jax.ShapeDtypeStruct((SUBLANES, LANES), F32),
                   jax.ShapeDtypeStruct((SUBLANES, LANES), F32)],
        compiler_params=_cparams(est, ("arbitrary",)),
    )(proj, proj, ba, convw, arow, dtrow, dq, dk, dv, dgb)


def _conv_bwd(dc, convw, *, tm=256):
    S, W3 = dc.shape
    tm = _tile(S, tm, SUBLANES * 2)
    hb = tm // HALO
    nt = S // tm

    def body(dc_ref, nxt_ref, w_ref, o_ref):
        last = pl.program_id(0) == nt - 1
        nxt = jnp.where(last, 0.0, nxt_ref[...])
        cur = dc_ref[...]
        xc = jnp.concatenate([cur, nxt], axis=0)
        out = cur * w_ref[CONV_K - 1:CONV_K, :]
        for s in range(1, CONV_K):
            out = out + pltpu.roll(xc, tm + HALO - s, 0)[:tm] * w_ref[CONV_K - 1 - s:CONV_K - s, :]
        o_ref[...] = out.astype(BF)

    est = 5 * _nbytes((tm, W3), F32)
    return pl.pallas_call(
        body, name="conv_bwd", grid=(nt,),
        in_specs=[pl.BlockSpec((tm, W3), lambda i: (i, 0)),
                  pl.BlockSpec((HALO, W3), lambda i: (jnp.minimum((i + 1) * hb, S // HALO - 1), 0)),
                  pl.BlockSpec((CONV_K, W3), lambda i: (0, 0))],
        out_specs=pl.BlockSpec((tm, W3), lambda i: (i, 0)),
        out_shape=jax.ShapeDtypeStruct((S, W3), BF),
        compiler_params=_cparams(est, ("parallel",)),
    )(dc, dc, convw)


def _inv_unit_lower(A):
    C = A.shape[-1]
    row, col = _iota((C, C), 0), _iota((C, C), 1)
    T = jnp.broadcast_to(jnp.where(row == col, 1.0, 0.0).astype(F32), A.shape)
    b = 1
    while b < C:
        hi = ~(2 * b - 1)
        off = ((row & hi) == (col & hi)) & ((row & b) != 0) & ((col & b) == 0)
        T = T - _dotf(_dotf(T, jnp.where(off, A, 0.0)), T)
        b *= 2
    return T


def _delta_common(q, k, g, beta):
    C = q.shape[-2]
    row, col = _iota((C, C), 0), _iota((C, C), 1)
    tril = row >= col
    qs = q * (D_HEAD ** -0.5)
    gcb = _dot01(jnp.where(tril, 1.0, 0.0), jnp.broadcast_to(g, g.shape[:-1] + (LANES,)))
    gc = gcb[..., :1]
    Dm = jnp.exp(jnp.where(tril, gc - jnp.swapaxes(gcb, -1, -2), -1e30))
    eg = jnp.exp(gc)
    gl = jnp.sum(jnp.where(_iota((C, 1), 0) == C - 1, gc, 0.0), axis=(-2, -1), keepdims=True)
    el = jnp.exp(gl)
    er = jnp.exp(gl - gc)
    kb = k * beta
    KK = _dot_nt(kb, k)
    QK = _dot_nt(qs, k)
    return dict(row=row, col=col, tril=tril, qs=qs, gc=gc, Dm=Dm, eg=eg, el=el, er=er, kb=kb, KK=KK, QK=QK)


def _delta_chunk_fwd(S0, q, k, v, g, beta, T=None):
    m = _delta_common(q, k, g, beta)
    if T is None:
        T = _inv_unit_lower(jnp.where(m["row"] > m["col"], m["KK"] * m["Dm"], 0.0))
    u = _dotf(T, v * beta)
    w = _dotf(T, m["kb"] * m["eg"])
    vn = u - _dot(w, S0)
    o = _dot(m["qs"] * m["eg"], S0) + _dot(m["QK"] * m["Dm"], vn)
    S1 = S0 * m["el"] + _dot_tn(k * m["er"], vn)
    return o, S1, T


def _delta_chunk_bwd(S0, q, k, v, g, beta, T, do, dS1):
    m = _delta_common(q, k, g, beta)
    C = q.shape[-2]
    qs, Dm, eg, el, er, kb, KK, QK = (m[n] for n in ("qs", "Dm", "eg", "el", "er", "kb", "KK", "QK"))
    strict = m["row"] > m["col"]
    total = lambda x: jnp.sum(x, axis=(-2, -1), keepdims=True)
    ru, rw = v * beta, kb * eg
    u = _dotf(T, ru)
    w = _dotf(T, rw)
    vn = u - _dot(w, S0)
    P = QK * Dm
    qg = qs * eg
    kr = k * er

    dvn = _dot_tn(P, do) + _dot(kr, dS1)
    dS0 = dS1 * el + _dot_tn(qg, do) - _dot_tn(w, dvn)
    d_el = total(dS1 * S0)
    dqg = _dot_nt(do, S0)
    dqs = dqg * eg
    deg = jnp.sum(dqg * qs, -1, keepdims=True)
    dP = _dot_nt(do, vn)
    dPD = dP * Dm
    dqs = dqs + _dot(dPD, k)
    dk = _dot_tn(dPD, qs)
    dD = dP * QK
    dkr = _dot_nt(vn, dS1)
    dk = dk + dkr * er
    der = jnp.sum(dkr * k, -1, keepdims=True)
    dw = -_dot_nt(dvn, S0)
    dru = _dotf_tn(T, dvn)
    drw = _dotf_tn(T, dw)
    dT = _dotf_nt(dvn, ru) + _dotf_nt(dw, rw)
    dA = -_dotf_nt(_dotf_tn(T, dT), T)
    dAm = jnp.where(strict, dA, 0.0)
    dKK = dAm * Dm
    dkb = _dot(dKK, k)
    dk = dk + _dot_tn(dKK, kb)
    dD = dD + dAm * KK
    dv = dru * beta
    dbeta = jnp.sum(dru * v, -1, keepdims=True)
    dkb = dkb + drw * eg
    deg = deg + jnp.sum(drw * kb, -1, keepdims=True)
    dk = dk + dkb * beta
    dbeta = dbeta + jnp.sum(dkb * k, -1, keepdims=True)
    E = dD * Dm
    dgc = jnp.sum(E, -1, keepdims=True) - jnp.sum(jnp.swapaxes(E, -1, -2), -1, keepdims=True)
    dgc = dgc + deg * eg - der * er
    dgl = total(der * er) + d_el * el
    dgc = dgc + jnp.where(_iota((C, 1), 0) == C - 1, dgl, 0.0)
    triu = jnp.where(m["row"] <= m["col"], 1.0, 0.0)
    dg = _dot01(triu, jnp.broadcast_to(dgc, dgc.shape[:-1] + (LANES,)))[..., :1]
    dq = dqs * (D_HEAD ** -0.5)
    return dq, dk, dv, dg, dbeta, dS0


def _head_cols(gb, h):
    lane = _iota(gb.shape, 1)
    beta = jnp.sum(jnp.where(lane == h, gb, 0.0), -1, keepdims=True)
    g = jnp.sum(jnp.where(lane == N_HEADS + h, gb, 0.0), -1, keepdims=True)
    return g, beta


def _delta_fwd(q, k, v, gb):
    S = q.shape[0]
    C = DELTA_CHUNK
    N = S // C

    HB = DELTA_HEADS_PER_STEP

    def body(q_ref, k_ref, v_ref, gb_ref, o_ref, st_ref, t_ref, s_scr):
        n, hb = pl.program_id(0), pl.program_id(1)
        gb = gb_ref[...]

        @pl.when(n == 0)
        def _():
            for hh in range(HB):
                s_scr[hb * HB + hh] = jnp.zeros((D_HEAD, D_HEAD), F32)

        heads = [hb * HB + hh for hh in range(HB)]
        cols = [slice(hh * D_HEAD, (hh + 1) * D_HEAD) for hh in range(HB)]
        per_head = lambda ref: jnp.stack([ref[:, c] for c in cols])
        g, beta = (jnp.stack(t) for t in zip(*[_head_cols(gb, h) for h in heads]))
        S0 = jnp.stack([s_scr[h] for h in heads])
        o, S1, T = _delta_chunk_fwd(S0, per_head(q_ref), per_head(k_ref), per_head(v_ref), g, beta)
        for hh in range(HB):
            st_ref[hh, 0] = S0[hh]
            t_ref[hh, 0] = T[hh]
            o_ref[:, cols[hh]] = o[hh]
            s_scr[heads[hh]] = S1[hh]

    hd = pl.BlockSpec((C, HB * D_HEAD), lambda n, h: (n, h))
    mat = pl.BlockSpec((HB, 1, D_HEAD, D_HEAD), lambda n, h: (h, n, 0, 0))
    est = 40 * HB * _nbytes((C, D_HEAD), F32)
    return pl.pallas_call(
        body, name="delta_fwd", grid=(N, N_HEADS // HB),
        in_specs=[hd, hd, hd, pl.BlockSpec((C, LANES), lambda n, h: (n, 0))],
        out_specs=[hd, mat, mat],
        out_shape=[jax.ShapeDtypeStruct((S, N_HEADS * D_HEAD), F32),
                   jax.ShapeDtypeStruct((N_HEADS, N, D_HEAD, D_HEAD), F32),
                   jax.ShapeDtypeStruct((N_HEADS, N, C, C), F32)],
        scratch_shapes=[pltpu.VMEM((N_HEADS, D_HEAD, D_HEAD), F32)],
        compiler_params=_cparams(est, ("arbitrary", "arbitrary")),
    )(q, k, v, gb)


def _delta_bwd(q, k, v, gb, st, tinv, do):
    S = q.shape[0]
    C = DELTA_CHUNK
    N = S // C

    HB = DELTA_HEADS_PER_STEP

    def body(q_ref, k_ref, v_ref, gb_ref, st_ref, t_ref, do_ref, dq_ref, dk_ref, dv_ref, dgb_ref, ds_scr):
        n, hb = pl.program_id(0), pl.program_id(1)
        gb = gb_ref[...]
        lane = _iota((C, LANES), 1)
        dgb = jnp.zeros((C, LANES), F32)

        @pl.when(n == 0)
        def _():
            for hh in range(HB):
                ds_scr[hb * HB + hh] = jnp.zeros((D_HEAD, D_HEAD), F32)

        heads = [hb * HB + hh for hh in range(HB)]
        cols = [slice(hh * D_HEAD, (hh + 1) * D_HEAD) for hh in range(HB)]
        per_head = lambda ref: jnp.stack([ref[:, c] for c in cols])
        g, beta = (jnp.stack(t) for t in zip(*[_head_cols(gb, h) for h in heads]))
        dS1 = jnp.stack([ds_scr[h] for h in heads])
        dq, dk, dv, dg, dbeta, dS0 = _delta_chunk_bwd(
            st_ref[:, 0], per_head(q_ref), per_head(k_ref), per_head(v_ref), g, beta, t_ref[:, 0], per_head(do_ref), dS1)
        for hh, h in enumerate(heads):
            dq_ref[:, cols[hh]] = dq[hh]
            dk_ref[:, cols[hh]] = dk[hh]
            dv_ref[:, cols[hh]] = dv[hh]
            dgb = dgb + jnp.where(lane == h, dbeta[hh], 0.0) + jnp.where(lane == N_HEADS + h, dg[hh], 0.0)
            ds_scr[h] = dS0[hh]

        @pl.when(hb == 0)
        def _():
            dgb_ref[...] = dgb

        @pl.when(hb > 0)
        def _():
            dgb_ref[...] += dgb

    hd = pl.BlockSpec((C, HB * D_HEAD), lambda n, h: (N - 1 - n, h))
    mat = pl.BlockSpec((HB, 1, D_HEAD, D_HEAD), lambda n, h: (h, N - 1 - n, 0, 0))
    gbs = pl.BlockSpec((C, LANES), lambda n, h: (N - 1 - n, 0))
    est = 60 * HB * _nbytes((C, D_HEAD), F32)
    return pl.pallas_call(
        body, name="delta_bwd", grid=(N, N_HEADS // HB),
        in_specs=[hd, hd, hd, gbs, mat, mat, hd],
        out_specs=[hd, hd, hd, gbs],
        out_shape=[jax.ShapeDtypeStruct((S, N_HEADS * D_HEAD), F32)] * 3 + [jax.ShapeDtypeStruct((S, LANES), F32)],
        scratch_shapes=[pltpu.VMEM((N_HEADS, D_HEAD, D_HEAD), F32)],
        compiler_params=_cparams(est, ("arbitrary", "arbitrary")),
    )(q, k, v, gb, st, tinv, do)


def _ya_head(o, z, onw):
    return o * lax.rsqrt(jnp.mean(o * o, -1, keepdims=True) + RMS_EPS) * onw * _silu(z)


def _sgu_pre(u, vg, sg, sb):
    return _gelu(u), _ln(_gelu(vg), sg, sb)


def _chunk_causal(shape, di, dj):
    sh = jnp.int32(int(math.log2(SGU_CHUNK)))
    return lax.shift_right_logical(_iota(shape, di), sh) >= lax.shift_right_logical(_iota(shape, dj), sh)


def _ws_masked(ws):
    return jnp.where(_chunk_causal(ws.shape, 1, 2), ws, 0.0)


def _mix_prep(o, proj, onw, sg, sb, ws, bst, *, tm=256):
    S = o.shape[0]
    tm = _tile(S, tm, SGU_BLOCK)

    def body(o_ref, z_ref, u_ref, vg_ref, onw_ref, sg_ref, sb_ref, ws_ref, bst_ref, ya_ref, yb_ref):
        onw = onw_ref[...]
        for h in range(N_HEADS):
            sl = slice(h * D_HEAD, (h + 1) * D_HEAD)
            ya_ref[:, sl] = _ya_head(o_ref[:, sl], z_ref[:, sl], onw).astype(BF)
        ua, vl = _sgu_pre(u_ref[...], vg_ref[...], sg_ref[...], sb_ref[...])
        wsm = _ws_masked(ws_ref[...])
        bst = bst_ref[...]
        for blk in range(tm // SGU_BLOCK):
            rs = slice(blk * SGU_BLOCK, (blk + 1) * SGU_BLOCK)
            for gi in range(SGU_GROUPS):
                cs = slice(gi * D_HEAD, (gi + 1) * D_HEAD)
                sp = _dot(wsm[gi], vl[rs, cs]) + bst[:, gi:gi + 1]
                yb_ref[rs, cs] = (ua[rs, cs] * sp).astype(BF)

    blk = lambda col: pl.BlockSpec((tm, D_MODEL), lambda i: (i, col))
    full = lambda shape: pl.BlockSpec(shape, lambda i: (0,) * len(shape))
    est = 10 * _nbytes((tm, D_MODEL), F32)
    return pl.pallas_call(
        body, name="mix_prep", grid=(S // tm,),
        in_specs=[blk(0), blk(3), blk(4), blk(5), full((1, D_HEAD)), full((1, D_MODEL)), full((1, D_MODEL)),
                  full((SGU_GROUPS, SGU_BLOCK, SGU_BLOCK)), full((SGU_BLOCK, LANES))],
        out_specs=[blk(0), blk(0)],
        out_shape=[jax.ShapeDtypeStruct((S, D_MODEL), BF)] * 2,
        compiler_params=_cparams(est, ("parallel",)),
    )(o, proj, proj, proj, onw, sg, sb, ws, bst)


def _mix_prep_bwd(o, proj, onw, sg, sb, ws, bst, dya, dyb, *, tm=256):
    S = o.shape[0]
    tm = _tile(S, tm, SGU_BLOCK)

    def body(o_ref, z_ref, u_ref, vg_ref, onw_ref, sg_ref, sb_ref, ws_ref, bst_ref, dya_ref, dyb_ref,
             do_ref, dz_ref, du_ref, dvg_ref, donw_ref, dsg_ref, dsb_ref, dws_ref, dbst_ref, dvl_scr, dua_scr):
        @pl.when(pl.program_id(0) == 0)
        def _():
            for r in (donw_ref, dsg_ref, dsb_ref, dws_ref, dbst_ref):
                r[...] = jnp.zeros_like(r)

        onw = onw_ref[...]
        donw = jnp.zeros((1, D_HEAD), F32)
        for h in range(N_HEADS):
            sl = slice(h * D_HEAD, (h + 1) * D_HEAD)
            _, vj = jax.vjp(_ya_head, o_ref[:, sl], z_ref[:, sl], onw)
            do_h, dz_h, donw_h = vj(dya_ref[:, sl])
            do_ref[:, sl] = do_h
            dz_ref[:, sl] = dz_h.astype(BF)
            donw = donw + donw_h
        donw_ref[...] += _bcast_rows(donw)

        (ua, vl), vj = jax.vjp(_sgu_pre, u_ref[...], vg_ref[...], sg_ref[...], sb_ref[...])
        wsm = _ws_masked(ws_ref[...])
        bst = bst_ref[...]
        lane = _iota((SGU_BLOCK, LANES), 1)
        dbst = jnp.zeros((SGU_BLOCK, LANES), F32)
        cmask = _chunk_causal((SGU_BLOCK, SGU_BLOCK), 0, 1)
        for gi in range(SGU_GROUPS):
            cs = slice(gi * D_HEAD, (gi + 1) * D_HEAD)
            wg = wsm[gi]
            wgt = jnp.transpose(wg)
            dwg = jnp.zeros((SGU_BLOCK, SGU_BLOCK), F32)
            for blk in range(tm // SGU_BLOCK):
                rs = slice(blk * SGU_BLOCK, (blk + 1) * SGU_BLOCK)
                sp = _dot(wg, vl[rs, cs]) + bst[:, gi:gi + 1]
                dyb = dyb_ref[rs, cs]
                dsp = dyb * ua[rs, cs]
                dua_scr[rs, cs] = dyb * sp
                dvl_scr[rs, cs] = _dot(wgt, dsp)
                dwg = dwg + _dot_nt(dsp, vl[rs, cs])
                dbst = dbst + jnp.where(lane == gi, jnp.sum(dsp, -1, keepdims=True), 0.0)
            dws_ref[gi] += jnp.where(cmask, dwg, 0.0)
        dbst_ref[...] += dbst
        du, dvg, dsg, dsb = vj((dua_scr[...], dvl_scr[...]))
        du_ref[...] = du.astype(BF)
        dvg_ref[...] = dvg.astype(BF)
        dsg_ref[...] += _bcast_rows(dsg)
        dsb_ref[...] += _bcast_rows(dsb)

    blk = lambda col: pl.BlockSpec((tm, D_MODEL), lambda i: (i, col))
    full = lambda shape: pl.BlockSpec(shape, lambda i: (0,) * len(shape))
    est = 16 * _nbytes((tm, D_MODEL), F32)
    outs = pl.pallas_call(
        body, name="mix_prep_bwd", grid=(S // tm,),
        in_specs=[blk(0), blk(3), blk(4), blk(5), full((1, D_HEAD)), full((1, D_MODEL)), full((1, D_MODEL)),
                  full((SGU_GROUPS, SGU_BLOCK, SGU_BLOCK)), full((SGU_BLOCK, LANES)), blk(0), blk(0)],
        out_specs=[blk(0)] * 4 + [full((SUBLANES, D_HEAD)), full((SUBLANES, D_MODEL)), full((SUBLANES, D_MODEL)),
                                  full((SGU_GROUPS, SGU_BLOCK, SGU_BLOCK)), full((SGU_BLOCK, LANES))],
        out_shape=[jax.ShapeDtypeStruct((S, D_MODEL), F32)] + [jax.ShapeDtypeStruct((S, D_MODEL), BF)] * 3
                  + [jax.ShapeDtypeStruct((SUBLANES, D_HEAD), F32), jax.ShapeDtypeStruct((SUBLANES, D_MODEL), F32),
                     jax.ShapeDtypeStruct((SUBLANES, D_MODEL), F32),
                     jax.ShapeDtypeStruct((SGU_GROUPS, SGU_BLOCK, SGU_BLOCK), F32),
                     jax.ShapeDtypeStruct((SGU_BLOCK, LANES), F32)],
        scratch_shapes=[pltpu.VMEM((tm, D_MODEL), F32)] * 2,
        compiler_params=_cparams(est, ("arbitrary",)),
    )(o, proj, proj, proj, onw, sg, sb, ws, bst, dya, dyb)
    return outs


def _gate_merge(pa, pb, proj, *, tm=512):
    S = pa.shape[0]
    tm = _tile(S, tm, SUBLANES * 2)

    def body(pa_ref, pb_ref, ga_ref, gb_ref, m_ref):
        m_ref[...] = (_sigmoid(ga_ref[...]) * pa_ref[...] + _sigmoid(gb_ref[...]) * pb_ref[...]).astype(BF)

    blk = lambda col: pl.BlockSpec((tm, D_MODEL), lambda i: (i, col))
    return pl.pallas_call(
        body, name="gate_merge", grid=(S // tm,),
        in_specs=[blk(0), blk(0), blk(6), blk(7)], out_specs=blk(0),
        out_shape=jax.ShapeDtypeStruct((S, D_MODEL), BF),
        compiler_params=_cparams(6 * _nbytes((tm, D_MODEL), F32), ("parallel",)),
    )(pa, pb, proj, proj)


def _gate_merge_bwd(pa, pb, proj, dm, *, tm=512):
    S = pa.shape[0]
    tm = _tile(S, tm, SUBLANES * 2)

    def body(pa_ref, pb_ref, ga_ref, gb_ref, dm_ref, dpa_ref, dpb_ref, dga_ref, dgb_ref):
        dm = dm_ref[...]
        sa, sb = _sigmoid(ga_ref[...]), _sigmoid(gb_ref[...])
        dpa_ref[...] = (dm * sa).astype(BF)
        dpb_ref[...] = (dm * sb).astype(BF)
        dga_ref[...] = (dm * pa_ref[...] * sa * (1.0 - sa)).astype(BF)
        dgb_ref[...] = (dm * pb_ref[...] * sb * (1.0 - sb)).astype(BF)

    blk = lambda col: pl.BlockSpec((tm, D_MODEL), lambda i: (i, col))
    return pl.pallas_call(
        body, name="gate_merge_bwd", grid=(S // tm,),
        in_specs=[blk(0), blk(0), blk(6), blk(7), blk(0)], out_specs=[blk(0)] * 4,
        out_shape=[jax.ShapeDtypeStruct((S, D_MODEL), BF)] * 4,
        compiler_params=_cparams(10 * _nbytes((tm, D_MODEL), F32), ("parallel",)),
    )(pa, pb, proj, proj, dm)


def _swiglu_act(hgu, *, tm=256):
    S = hgu.shape[0]
    tm = _tile(S, tm, SUBLANES * 2)

    def body(hg_ref, hu_ref, h_ref):
        h_ref[...] = (_silu(hg_ref[...]) * hu_ref[...]).astype(BF)

    blk = lambda col: pl.BlockSpec((tm, FFN_K), lambda i: (i, col))
    return pl.pallas_call(
        body, name="swiglu_act", grid=(S // tm,),
        in_specs=[blk(0), blk(1)], out_specs=blk(0),
        out_shape=jax.ShapeDtypeStruct((S, FFN_K), BF),
        compiler_params=_cparams(5 * _nbytes((tm, FFN_K), F32), ("parallel",)),
    )(hgu, hgu)


def _swiglu_bwd(hgu, dh, *, tm=256):
    S = hgu.shape[0]
    tm = _tile(S, tm, SUBLANES * 2)

    def body(hg_ref, hu_ref, dh_ref, d_ref):
        hg, dh = hg_ref[...], dh_ref[...]
        d_ref[:, :FFN_K] = (dh * hu_ref[...] * _dsilu(hg)).astype(BF)
        d_ref[:, FFN_K:] = (dh * _silu(hg)).astype(BF)

    blk = lambda col: pl.BlockSpec((tm, FFN_K), lambda i: (i, col))
    return pl.pallas_call(
        body, name="swiglu_bwd", grid=(S // tm,),
        in_specs=[blk(0), blk(1), blk(0)], out_specs=pl.BlockSpec((tm, 2 * FFN_K), lambda i: (i, 0)),
        out_shape=jax.ShapeDtypeStruct((S, 2 * FFN_K), BF),
        compiler_params=_cparams(8 * _nbytes((tm, FFN_K), F32), ("parallel",)),
    )(hgu, hgu, dh)


def _resid_ln(x, r, g, b, *, tm=512):
    S = x.shape[0]
    tm = _tile(S, tm, SUBLANES)

    def body(x_ref, r_ref, g_ref, b_ref, y_ref):
        y_ref[...] = _ln(ALPHA * x_ref[...] + r_ref[...], g_ref[...], b_ref[...])

    blk = pl.BlockSpec((tm, D_MODEL), lambda i: (i, 0))
    vec = pl.BlockSpec((1, D_MODEL), lambda i: (0, 0))
    return pl.pallas_call(
        body, name="resid_ln", grid=(S // tm,),
        in_specs=[blk, blk, vec, vec], out_specs=blk,
        out_shape=jax.ShapeDtypeStruct((S, D_MODEL), F32),
        compiler_params=_cparams(6 * _nbytes((tm, D_MODEL), F32), ("parallel",)),
    )(x, r, g, b)


def _resid_ln_bwd(x, r, g, b, dy, *, tm=512):
    S = x.shape[0]
    tm = _tile(S, tm, SUBLANES)

    def body(x_ref, r_ref, g_ref, b_ref, dy_ref, dp_ref, dg_ref, db_ref):
        @pl.when(pl.program_id(0) == 0)
        def _():
            dg_ref[...] = jnp.zeros_like(dg_ref)
            db_ref[...] = jnp.zeros_like(db_ref)

        _, vj = jax.vjp(_ln, ALPHA * x_ref[...] + r_ref[...], g_ref[...], b_ref[...])
        dp, dg, db = vj(dy_ref[...])
        dp_ref[...] = dp
        dg_ref[...] += _bcast_rows(dg)
        db_ref[...] += _bcast_rows(db)

    blk = pl.BlockSpec((tm, D_MODEL), lambda i: (i, 0))
    vec = pl.BlockSpec((1, D_MODEL), lambda i: (0, 0))
    acc = pl.BlockSpec((SUBLANES, D_MODEL), lambda i: (0, 0))
    return pl.pallas_call(
        body, name="resid_ln_bwd", grid=(S // tm,),
        in_specs=[blk, blk, vec, vec, blk], out_specs=[blk, acc, acc],
        out_shape=[jax.ShapeDtypeStruct((S, D_MODEL), F32)] + [jax.ShapeDtypeStruct((SUBLANES, D_MODEL), F32)] * 2,
        compiler_params=_cparams(10 * _nbytes((tm, D_MODEL), F32), ("arbitrary",)),
    )(x, r, g, b, dy)


def _loss_head(y, tgt, *, tm=512):
    S = y.shape[0]
    tm = _tile(S, tm, SUBLANES)

    def body(y_ref, t_ref, dy_ref, l_ref):
        @pl.when(pl.program_id(0) == 0)
        def _():
            l_ref[...] = jnp.zeros_like(l_ref)

        e = y_ref[...] - t_ref[...]
        dy_ref[...] = e * (1.0 / D_MODEL)
        l_ref[...] += 0.5 * jnp.sum(jnp.mean(e * e, -1, keepdims=True), keepdims=True)

    blk = pl.BlockSpec((tm, D_MODEL), lambda i: (i, 0))
    return pl.pallas_call(
        body, name="loss_head", grid=(S // tm,),
        in_specs=[blk, blk], out_specs=[blk, pl.BlockSpec((SUBLANES, LANES), lambda i: (0, 0))],
        out_shape=[jax.ShapeDtypeStruct((S, D_MODEL), F32), jax.ShapeDtypeStruct((SUBLANES, LANES), F32)],
        compiler_params=_cparams(6 * _nbytes((tm, D_MODEL), F32), ("arbitrary",)),
    )(y, tgt)


def _layer_fwd(x, w):
    proj = _mm(x, w["win"], mode="nn", name="mm_in", tn=1024)
    ba = _mm(x, w["wba"], mode="nn", name="mm_in_ba", tm=1024, tn=LANES)
    qn, kn, vv, gb = _qkv_prep(proj, ba, w["convw"], w["arow"], w["dtrow"])
    o, st, tinv = _delta_fwd(qn, kn, vv, gb)
    ya, yb = _mix_prep(o, proj, w["onw"], w["sg"], w["sb"], w["ws"], w["bst"])
    pa = _mm(ya, w["wpa"], mode="nn", name="mm_sq")
    pb = _mm(yb, w["wpb"], mode="nn", name="mm_sq")
    m = _gate_merge(pa, pb, proj)
    mix = _mm(m, w["wo"], mode="nn", name="mm_sq")
    x1 = _resid_ln(x, mix, w["ln1g"], w["ln1b"])
    hgu = _mm(x1, w["wgu"], mode="nn", name="mm_gu", tn=1536)
    h = _swiglu_act(hgu)
    ffn = _mm(h, w["wd"], mode="nn", name="mm_down", tk=FFN_K)
    x2 = _resid_ln(x1, ffn, w["ln2g"], w["ln2b"])
    saved = dict(x=x, proj=proj, ba=ba, qn=qn, kn=kn, vv=vv, gb=gb, o=o, st=st, tinv=tinv, ya=ya, yb=yb,
                 pa=pa, pb=pb, m=m, mix=mix, x1=x1, hgu=hgu, h=h, ffn=ffn)
    return x2, saved


def _layer_bwd(dx2, w, s):
    g = {}
    dpre2, g["ln2g"], g["ln2b"] = _resid_ln_bwd(s["x1"], s["ffn"], w["ln2g"], w["ln2b"], dx2)
    dh = _mm(dpre2, w["wd"], mode="nt", name="mm_nt_down", tn=1536)
    g["wd"] = _mm(s["h"], dpre2, mode="tn", name="mm_tn_down", tm=1536, tk=512, out_dtype=BF)
    dhgu = _swiglu_bwd(s["hgu"], dh)
    dx1 = _mm(dhgu, w["wgu"], mode="nt", name="mm_nt_gu", add=dpre2, add_scale=ALPHA, tk=1536)
    g["wgu"] = _mm(s["x1"], dhgu, mode="tn", name="mm_tn_gu", tm=1024, tn=1536, tk=512, out_dtype=BF)
    dpre1, g["ln1g"], g["ln1b"] = _resid_ln_bwd(s["x"], s["mix"], w["ln1g"], w["ln1b"], dx1)
    dm = _mm(dpre1, w["wo"], mode="nt", name="mm_nt_sq")
    g["wo"] = _mm(s["m"], dpre1, mode="tn", name="mm_tn_sq", tm=1024, tk=512, out_dtype=BF)
    dpa, dpb, dga, dgb_gate = _gate_merge_bwd(s["pa"], s["pb"], s["proj"], dm)
    dya = _mm(dpa, w["wpa"], mode="nt", name="mm_nt_sq")
    g["wpa"] = _mm(s["ya"], dpa, mode="tn", name="mm_tn_sq", tm=1024, tk=512, out_dtype=BF)
    dyb = _mm(dpb, w["wpb"], mode="nt", name="mm_nt_sq")
    g["wpb"] = _mm(s["yb"], dpb, mode="tn", name="mm_tn_sq", tm=1024, tk=512, out_dtype=BF)
    do, dz, du, dvg, g["onw"], g["sg"], g["sb"], g["ws"], g["bst"] = _mix_prep_bwd(
        s["o"], s["proj"], w["onw"], w["sg"], w["sb"], w["ws"], w["bst"], dya, dyb)
    dqn, dkn, dvv, dgb = _delta_bwd(s["qn"], s["kn"], s["vv"], s["gb"], s["st"], s["tinv"], do)
    dc, dba, g["convw"], g["arow"], g["dtrow"] = _qkv_prep_bwd(
        s["proj"], s["ba"], w["convw"], w["arow"], w["dtrow"], dqn, dkn, dvv, dgb)
    dqkv = _conv_bwd(dc, w["convw"])
    dproj = jnp.concatenate([dqkv, dz, du, dvg, dga, dgb_gate], axis=1)
    dx = _mm(dproj, w["win"], mode="nt", name="mm_nt_in", add=dpre1, add_scale=ALPHA, tk=1024)
    dx = _mm(dba, w["wba"], mode="nt", name="mm_nt_ba", add=dx, add_scale=1.0, tm=1024)
    g["win"] = _mm(s["x"], dproj, mode="tn", name="mm_tn_in", tm=1024, tn=1024, tk=512, out_dtype=BF)
    g["wba"] = _mm(s["x"], dba, mode="tn", name="mm_tn_ba", tm=1024, tn=LANES, tk=1024, out_dtype=BF)
    return dx, g


def _local_step(x, tgt, ws):
    saved = []
    for w in ws:
        x, s = _layer_fwd(x, w)
        saved.append(s)
    dy, lacc = _loss_head(x, tgt)
    grads = [None] * len(ws)
    for l in reversed(range(len(ws))):
        dy, grads[l] = _layer_bwd(dy, ws[l], saved[l])
    return lacc[0, 0], dy, grads


_QKVZ = 4 * D_MODEL
_BA = 2 * N_HEADS


WEIGHT_NAMES = ("w_in", "conv_w", "a_log", "dt_bias", "o_norm_w", "sgu_ln_g", "sgu_ln_b", "w_s", "b_s", "w_pa", "w_pb",
                "w_o", "ln1_g", "ln1_b", "w_ffn_gate", "w_ffn_up", "w_ffn_down", "ln2_g", "ln2_b")
WIRE = ("w_in", "w_ffn_gate", "w_ffn_up", "w_ffn_down", "w_pa", "w_pb", "w_o", "conv_w")
SMALL = (("a_log", N_HEADS), ("dt_bias", N_HEADS), ("o_norm_w", D_HEAD), ("sgu_ln_g", D_MODEL), ("sgu_ln_b", D_MODEL),
         ("w_s", SGU_GROUPS * SGU_BLOCK * SGU_BLOCK), ("b_s", SGU_GROUPS * SGU_BLOCK),
         ("ln1_g", D_MODEL), ("ln1_b", D_MODEL), ("ln2_g", D_MODEL), ("ln2_b", D_MODEL))
SMALL_ROWS = -(-sum(n for _, n in SMALL) // (LANES * SUBLANES)) * SUBLANES
N_MAIN_TILES = (N_IN - _BA) // D_MODEL
ADAM_TILES = dict(w_in=(128, "adamw_in"), w_ffn_gate=(256, "adamw_ffn_cols"), w_ffn_up=(256, "adamw_ffn_cols"),
                  w_ffn_down=(32, "adamw_ffn_rows"), w_pa=(128, "adamw_sq"), w_pb=(128, "adamw_sq"), w_o=(128, "adamw_sq"),
                  conv_w=(CONV_K, "adamw_conv"))


def _pad_to(a, axis, size):
    pads = [(0, 0)] * a.ndim
    pads[axis] = (0, size - a.shape[axis])
    return jnp.pad(a, pads)


def _wire_blocks(p):
    return dict(
        w_in=_pad_to(p["w_in"].astype(BF), 2, IN_PAD),
        w_ffn_gate=_pad_to(p["w_ffn_gate"].astype(BF), 2, FFN_PAD), w_ffn_up=_pad_to(p["w_ffn_up"].astype(BF), 2, FFN_PAD),
        w_ffn_down=_pad_to(p["w_ffn_down"].astype(BF), 1, FFN_PAD),
        w_pa=p["w_pa"].astype(BF), w_pb=p["w_pb"].astype(BF), w_o=p["w_o"].astype(BF),
        conv_w=_pad_to(p["conv_w"], 1, SUBLANES),
    )


def _by_columns(blocks):
    n, r, c = blocks.shape
    return jnp.transpose(blocks, (1, 0, 2)).reshape(r, n * c)


def _to_slots(full, c):
    r = full.shape[0]
    return jnp.transpose(full.reshape(r, N_DEV, c), (1, 0, 2))


def _lane_row(v, at):
    return jnp.pad(v[None], ((0, 0), (at, LANES - at - v.shape[0])))


def _layer_weights(stacks, p, l):
    return dict(
        win=_perm_in(stacks["w_in"], l, D_MODEL, N_MAIN_TILES), wba=_perm_in(stacks["w_in"], l, LANES, 1),
        convw=_by_columns(stacks["conv_w"][:, l, :CONV_K]),
        arow=_lane_row(p["a_log"][l], N_HEADS), dtrow=_lane_row(p["dt_bias"][l], N_HEADS),
        onw=p["o_norm_w"][l][None], sg=p["sgu_ln_g"][l][None], sb=p["sgu_ln_b"][l][None],
        ws=p["w_s"][l], bst=_pad_to(p["b_s"][l].T, 1, LANES),
        wpa=stacks["w_pa"][:, l].reshape(D_MODEL, D_MODEL), wpb=stacks["w_pb"][:, l].reshape(D_MODEL, D_MODEL),
        wo=stacks["w_o"][:, l].reshape(D_MODEL, D_MODEL),
        ln1g=p["ln1_g"][l][None], ln1b=p["ln1_b"][l][None],
        wgu=jnp.concatenate([_by_columns(stacks["w_ffn_gate"][:, l]), _by_columns(stacks["w_ffn_up"][:, l])], axis=1),
        wd=stacks["w_ffn_down"][:, l].reshape(FFN_K, D_MODEL),
        ln2g=p["ln2_g"][l][None], ln2b=p["ln2_b"][l][None],
    )


def _small_pack(parts):
    flat = jnp.concatenate([parts[n].reshape(-1) for n, _ in SMALL])
    return _pad_to(flat, 0, SMALL_ROWS * LANES).reshape(SMALL_ROWS, LANES)


def _small_unpack(rows, like):
    flat, out, off = rows.reshape(-1), {}, 0
    for n, size in SMALL:
        out[n] = flat[off:off + size].reshape(like[n].shape[1:])
        off += size
    return out


def _layer_slots(g):
    slots = dict(
        w_in=_perm_out(g["win"], g["wba"]),
        w_ffn_gate=_to_slots(g["wgu"][:, :FFN_K], FFN_PAD), w_ffn_up=_to_slots(g["wgu"][:, FFN_K:], FFN_PAD),
        w_ffn_down=g["wd"].reshape(N_DEV, FFN_PAD, D_MODEL),
        w_pa=g["wpa"].reshape(N_DEV, D_MODEL // N_DEV, D_MODEL), w_pb=g["wpb"].reshape(N_DEV, D_MODEL // N_DEV, D_MODEL),
        w_o=g["wo"].reshape(N_DEV, D_MODEL // N_DEV, D_MODEL),
        conv_w=_pad_to(_to_slots(g["convw"][:CONV_K], 3 * D_MODEL // N_DEV), 1, SUBLANES),
    )
    small = _small_pack(dict(
        a_log=g["arow"][0, N_HEADS:2 * N_HEADS], dt_bias=g["dtrow"][0, N_HEADS:2 * N_HEADS], o_norm_w=g["onw"][0],
        sgu_ln_g=g["sg"][0], sgu_ln_b=g["sb"][0], w_s=g["ws"], b_s=g["bst"][:, :SGU_GROUPS].T,
        ln1_g=g["ln1g"][0], ln1_b=g["ln1b"][0], ln2_g=g["ln2g"][0], ln2_b=g["ln2b"][0]))
    return [slots[n] for n in WIRE], small


def _in_tile_start(j, tile_w):
    if tile_w == LANES:
        return jnp.int32(_QKVZ)
    return j * D_MODEL + jnp.where(j >= _QKVZ // D_MODEL, _BA, 0)


def _select(rows_iota, cols_iota, dev, start, valid):
    hit = (rows_iota + (dev * IN_BLOCK - start) == cols_iota) & (rows_iota < IN_BLOCK) & (cols_iota < valid)
    return jnp.where(hit, 1.0, 0.0).astype(BF)


def _perm_in(stack, l, tile_w, n_tiles):
    valid = _BA if tile_w == LANES else tile_w

    def first_dev(j):
        return lax.div(_in_tile_start(j, tile_w), jnp.int32(IN_BLOCK))

    def body(w_ref, o_ref, acc_ref):
        j, k = pl.program_id(0), pl.program_id(1)
        sel = _select(_iota((IN_PAD, tile_w), 0), _iota((IN_PAD, tile_w), 1), first_dev(j) + k,
                      _in_tile_start(j, tile_w), valid)
        part = jnp.dot(w_ref[0, 0], sel, preferred_element_type=F32)

        @pl.when(k == 0)
        def _():
            acc_ref[...] = part

        @pl.when(k == 1)
        def _():
            o_ref[...] = (acc_ref[...] + part).astype(BF)

    est = _nbytes((D_MODEL, IN_PAD), BF) + 3 * _nbytes((D_MODEL, tile_w), F32) + 2 * _nbytes((IN_PAD, tile_w), F32)
    return pl.pallas_call(
        body, name="perm_in" if tile_w != LANES else "perm_in_ba", grid=(n_tiles, 2),
        in_specs=[pl.BlockSpec((1, 1, D_MODEL, IN_PAD), lambda j, k: (jnp.minimum(first_dev(j) + k, N_DEV - 1), l, 0, 0))],
        out_specs=pl.BlockSpec((D_MODEL, tile_w), lambda j, k: (0, j)),
        out_shape=jax.ShapeDtypeStruct((D_MODEL, n_tiles * tile_w), BF),
        scratch_shapes=[pltpu.VMEM((D_MODEL, tile_w), F32)],
        compiler_params=_cparams(est, ("parallel", "arbitrary")),
    )(stack)


def _perm_out(dmain, dba):
    n_main = dmain.shape[1] // D_MODEL

    def body(dm_ref, db_ref, o_ref, acc_ref):
        d, t = pl.program_id(0), pl.program_id(1)

        @pl.when(t == 0)
        def _():
            acc_ref[...] = jnp.zeros_like(acc_ref)

        start = _in_tile_start(t, D_MODEL)
        overlaps = (start < (d + 1) * IN_BLOCK) & (d * IN_BLOCK < start + D_MODEL)

        @pl.when((t < n_main) & overlaps)
        def _():
            sel = _select(_iota((D_MODEL, IN_PAD), 1), _iota((D_MODEL, IN_PAD), 0), d, start, D_MODEL)
            acc_ref[...] += jnp.dot(dm_ref[...], sel, preferred_element_type=F32)

        @pl.when(t == n_main)
        def _():
            sel = _select(_iota((LANES, IN_PAD), 1), _iota((LANES, IN_PAD), 0), d, jnp.int32(_QKVZ), _BA)
            o_ref[0] = (acc_ref[...] + jnp.dot(db_ref[...], sel, preferred_element_type=F32)).astype(BF)

    est = 2 * _nbytes((D_MODEL, D_MODEL), BF) + 4 * _nbytes((D_MODEL, IN_PAD), F32)
    return pl.pallas_call(
        body, name="perm_out", grid=(N_DEV, n_main + 1),
        in_specs=[pl.BlockSpec((D_MODEL, D_MODEL), lambda d, t: (0, jnp.minimum(t, n_main - 1))),
                  pl.BlockSpec((D_MODEL, LANES), lambda d, t: (0, 0))],
        out_specs=pl.BlockSpec((1, D_MODEL, IN_PAD), lambda d, t: (d, 0, 0)),
        out_shape=jax.ShapeDtypeStruct((N_DEV, D_MODEL, IN_PAD), BF),
        scratch_shapes=[pltpu.VMEM((D_MODEL, IN_PAD), F32)],
        compiler_params=_cparams(est, ("parallel", "arbitrary")),
    )(dmain, dba)


def _mesh_place():
    x, y, c = (lax.axis_index(a) for a in MESH_AXES)
    return x, y, c


def _slot(x, y, c):
    return 4 * x + 2 * y + c


def _peer(place, j):
    x, y, c = place
    return (1 - x if j & 4 else x, 1 - y if j & 2 else y, 1 - c if j & 1 else c)


def _all_to_all(srcs_of, dst_refs, send_sems, recv_sems, local_sems):
    place = _mesh_place()
    me = _slot(*place)
    n = len(dst_refs)
    local = [pltpu.make_async_copy(srcs_of[a](me), dst_refs[a].at[me], local_sems.at[a]) for a in range(n)]
    for cp in local:
        cp.start()
    sends = []
    for j in range(1, N_DEV):
        peer = _peer(place, j)
        for a in range(n):
            cp = pltpu.make_async_remote_copy(
                src_ref=srcs_of[a](_slot(*peer)), dst_ref=dst_refs[a].at[me],
                send_sem=send_sems.at[a, j - 1], recv_sem=recv_sems.at[a, j - 1],
                device_id=peer, device_id_type=pl.DeviceIdType.MESH)
            cp.start()
            sends.append(cp)
    for j in range(1, N_DEV):
        peer = _peer(place, j)
        for a in range(n):
            pltpu.make_async_remote_copy(
                src_ref=srcs_of[a](me), dst_ref=dst_refs[a].at[_slot(*peer)],
                send_sem=send_sems.at[a, j - 1], recv_sem=recv_sems.at[a, j - 1],
                device_id=peer, device_id_type=pl.DeviceIdType.MESH).wait_recv()
    for cp in sends:
        cp.wait_send()
    for cp in local:
        cp.wait()


def _comm_call(body, name, out_shapes, args):
    any_spec = pl.BlockSpec(memory_space=pl.ANY)
    n = len(out_shapes)
    return pl.pallas_call(
        body, name=name, in_specs=[any_spec] * len(args), out_specs=[any_spec] * n, out_shape=out_shapes,
        scratch_shapes=[pltpu.SemaphoreType.DMA((n, N_DEV - 1)), pltpu.SemaphoreType.DMA((n, N_DEV - 1)),
                        pltpu.SemaphoreType.DMA((n,))],
    )(*args)


def _all_gather(blocks):
    n = len(blocks)

    def body(*refs):
        ins, outs, (send_sems, recv_sems, local_sems) = refs[:n], refs[n:2 * n], refs[2 * n:]
        _all_to_all([lambda slot, r=r: r for r in ins], outs, send_sems, recv_sems, local_sems)

    return _comm_call(body, "all_gather", [jax.ShapeDtypeStruct((N_DEV,) + b.shape, b.dtype) for b in blocks], blocks)


def _grad_exchange(slots, small):
    n = len(slots) + 1

    def body(*refs):
        ins, outs, (send_sems, recv_sems, local_sems) = refs[:n], refs[n:2 * n], refs[2 * n:]
        srcs = [lambda slot, r=r: r.at[slot] for r in ins[:-1]] + [lambda slot: ins[-1]]
        _all_to_all(srcs, outs, send_sems, recv_sems, local_sems)

    shapes = [jax.ShapeDtypeStruct(s.shape, s.dtype) for s in slots] + [jax.ShapeDtypeStruct((N_DEV,) + small.shape, small.dtype)]
    return _comm_call(body, "grad_exchange", shapes, list(slots) + [small])


def _adam_update(g, w, m, v):
    m = ADAM_B1 * m + (1.0 - ADAM_B1) * g
    v = ADAM_B2 * v + (1.0 - ADAM_B2) * jnp.square(g)
    m_hat = m / (1.0 - ADAM_B1 ** ADAM_STEP)
    v_hat = v / (1.0 - ADAM_B2 ** ADAM_STEP)
    return -ADAM_LR * (m_hat / (jnp.sqrt(v_hat) + ADAM_EPS) + ADAM_WD * w), m, v


def _adamw(recvs, w, m, v, *, tr, name):
    L, R, C = w.shape
    rp = max(tr, SUBLANES * (4 // jnp.dtype(recvs[0].dtype).itemsize))
    Cp = recvs[0].shape[2]

    def body(*refs):
        r_refs, (w_ref, m_ref, v_ref, g_ref, d_ref, nm_ref, nv_ref) = refs[:L], refs[L:]
        for l in range(L):
            @pl.when(pl.program_id(0) == l)
            def _(r_ref=r_refs[l]):
                g = r_ref[0, :tr, :C].astype(F32)
                for s in range(1, N_DEV):
                    g = g + r_ref[s, :tr, :C].astype(F32)
                d, nm, nv = _adam_update(g, w_ref[0], m_ref[0], v_ref[0])
                g_ref[0], d_ref[0], nm_ref[0], nv_ref[0] = g, d, nm, nv

    blk = pl.BlockSpec((1, tr, C), lambda l, i: (l, i, 0))
    r_specs = [pl.BlockSpec((N_DEV, rp, Cp), lambda l, i, k=k: (0, jnp.where(l == k, i, 0), 0)) for k in range(L)]
    est = 2 * _nbytes((N_DEV, rp, Cp), recvs[0].dtype) + 8 * _nbytes((tr, Cp), F32)
    return pl.pallas_call(
        body, name=name, grid=(L, R // tr),
        in_specs=r_specs + [blk] * 3, out_specs=[blk] * 4,
        out_shape=[jax.ShapeDtypeStruct((L, R, C), F32)] * 4,
        compiler_params=_cparams(est, ("arbitrary", "arbitrary")),
    )(*recvs, w, m, v)


def _adamw_small(recv, w, m, v):
    def body(r_ref, w_ref, m_ref, v_ref, g_ref, d_ref, nm_ref, nv_ref):
        g = r_ref[0]
        for s in range(1, N_DEV):
            g = g + r_ref[s]
        g_ref[...] = g
        d_ref[...], nm_ref[...], nv_ref[...] = _adam_update(g, w_ref[...], m_ref[...], v_ref[...])

    vm = pl.BlockSpec(memory_space=pltpu.VMEM)
    return pl.pallas_call(
        body, name="adamw_small", in_specs=[vm] * 4, out_specs=[vm] * 4,
        out_shape=[jax.ShapeDtypeStruct((SMALL_ROWS, LANES), F32)] * 4,
        compiler_params=_cparams(20 * _nbytes((SMALL_ROWS, LANES), F32)),
    )(recv, w, m, v)


def kernel(x, w_in, conv_w, a_log, dt_bias, o_norm_w, sgu_ln_g, sgu_ln_b, w_s, b_s, w_pa, w_pb, w_o, ln1_g, ln1_b, w_ffn_gate, w_ffn_up, w_ffn_down, ln2_g, ln2_b, loss_target, m_w_in, m_conv_w, m_a_log, m_dt_bias, m_o_norm_w, m_sgu_ln_g, m_sgu_ln_b, m_w_s, m_b_s, m_w_pa, m_w_pb, m_w_o, m_ln1_g, m_ln1_b, m_w_ffn_gate, m_w_ffn_up, m_w_ffn_down, m_ln2_g, m_ln2_b, v_w_in, v_conv_w, v_a_log, v_dt_bias, v_o_norm_w, v_sgu_ln_g, v_sgu_ln_b, v_w_s, v_b_s, v_w_pa, v_w_pb, v_w_o, v_ln1_g, v_ln1_b, v_w_ffn_gate, v_w_ffn_up, v_w_ffn_down, v_ln2_g, v_ln2_b):
    given = dict(locals())
    P = {n: given[n] for n in WEIGHT_NAMES}
    M = {n: given["m_" + n] for n in WEIGHT_NAMES}
    V = {n: given["v_" + n] for n in WEIGHT_NAMES}

    wire = _wire_blocks(P)
    stacks = dict(zip(WIRE, _all_gather([wire[n] for n in WIRE])))
    layers = [_layer_weights(stacks, P, l) for l in range(DEPTH)]
    loss_local, dx, grads = _local_step(x[0], loss_target[0], layers)
    loss = lax.psum(loss_local, MESH_AXES)

    recv = []
    for l in range(DEPTH):
        slots, small = _layer_slots(grads[l])
        recv.append(dict(zip(WIRE + ("small",), _grad_exchange(slots, small))))

    out = {}
    for n in WIRE:
        tr, name = ADAM_TILES[n]
        out[n] = _adamw([recv[l][n] for l in range(DEPTH)], P[n], M[n], V[n], tr=tr, name=name)
    small = [_adamw_small(recv[l]["small"], *[_small_pack({n: T[n][l] for n, _ in SMALL}) for T in (P, M, V)])
             for l in range(DEPTH)]
    for n, _ in SMALL:
        out[n] = [jnp.stack([_small_unpack(small[l][i], P)[n] for l in range(DEPTH)]) for i in range(4)]
    return (loss, dx[None], *[out[n][i] for i in range(4) for n in WEIGHT_NAMES])
```

```python
import functools
import math

import jax
import jax.numpy as jnp
from jax import lax
from jax.experimental import pallas as pl
from jax.experimental.pallas import tpu as pltpu

F32 = jnp.float32
BF = jnp.bfloat16
HIGHEST = lax.Precision.HIGHEST

D_MODEL = 1024
DEPTH = 2
N_HEADS = 8
D_HEAD = 128
CONV_K = 4
SGU_BLOCK = 128
SGU_GROUPS = 8
SGU_CHUNK = 64
FFN_HIDDEN = 2816
N_IN = 8208
N_DEV = 8
IN_BLOCK, IN_PAD = N_IN // N_DEV, 1152
FFN_BLOCK, FFN_PAD = FFN_HIDDEN // N_DEV, 384
FFN_K = N_DEV * FFN_PAD
ALPHA = (2 * DEPTH) ** 0.25
LN_EPS = 1e-5
RMS_EPS = 1e-6
ADAM_LR, ADAM_B1, ADAM_B2, ADAM_EPS, ADAM_WD, ADAM_STEP = 0.001, 0.9, 0.999, 1e-08, 0.01, 10

MESH_AXES = ("x", "y", "c")
DELTA_CHUNK = 128
DELTA_HEADS_PER_STEP = 8
LANES = 128
SUBLANES = 8
VMEM_BYTES = 64 * 1024 * 1024
HALO = SUBLANES


def _cparams(est_bytes, dims=None):
    limit = int(min(max(2 * est_bytes + (8 << 20), 32 << 20), VMEM_BYTES - (6 << 20)))
    kw = dict(vmem_limit_bytes=limit)
    if dims is not None:
        kw["dimension_semantics"] = dims
    return pltpu.CompilerParams(**kw)


def _nbytes(shape, dtype):
    return math.prod(shape) * jnp.dtype(dtype).itemsize


def _dims(kind, ndim):
    lhs, rhs = {"nn": (1, 0), "nt": (1, 1), "tn": (0, 0)}[kind]
    b = ndim - 2
    return (((lhs + b,), (rhs + b,)), (tuple(range(b)), tuple(range(b))))


def _mxu(a, b, kind):
    return lax.dot_general(a, b, _dims(kind, a.ndim), preferred_element_type=F32)


def _dot(a, b):
    return _mxu(a.astype(BF), b.astype(BF), "nn")


def _dot_nt(a, b):
    return _mxu(a.astype(BF), b.astype(BF), "nt")


def _dot_tn(a, b):
    return _mxu(a.astype(BF), b.astype(BF), "tn")


def _split(a):
    hi = a.astype(BF)
    return hi, (a - hi.astype(F32)).astype(BF)


def _dot3(a, b, kind):
    (ah, al), (bh, bl) = _split(a), _split(b)
    return _mxu(ah, bh, kind) + (_mxu(ah, bl, kind) + _mxu(al, bh, kind))


def _dotf(a, b):
    return _dot3(a, b, "nn")


def _dotf_nt(a, b):
    return _dot3(a, b, "nt")


def _dotf_tn(a, b):
    return _dot3(a, b, "tn")


def _dot01(sel, x, kind="nn"):
    s = jnp.broadcast_to(sel.astype(BF), x.shape[:-2] + sel.shape)
    h1 = x.astype(BF)
    r1 = x - h1.astype(F32)
    h2 = r1.astype(BF)
    h3 = (r1 - h2.astype(F32)).astype(BF)
    return _mxu(s, h1, kind) + (_mxu(s, h2, kind) + _mxu(s, h3, kind))


def _sigmoid(x):
    return jax.nn.sigmoid(x)


def _silu(x):
    return x * _sigmoid(x)


def _dsilu(x):
    s = _sigmoid(x)
    return s * (1.0 + x * (1.0 - s))


def _gelu(x):
    return 0.5 * x * (1.0 + lax.erf(x * 0.7071067811865476))


def _softplus(x):
    return jnp.maximum(x, 0.0) + jnp.log1p(jnp.exp(-jnp.abs(x)))


def _ln(x, g, b):
    mu = jnp.mean(x, -1, keepdims=True)
    xc = x - mu
    var = jnp.mean(xc * xc, -1, keepdims=True)
    return xc * lax.rsqrt(var + LN_EPS) * g + b


def _iota(shape, dim):
    return lax.broadcasted_iota(jnp.int32, shape, dim)


def _tile(n, pref, align):
    if n <= pref:
        return n
    t = (pref // align) * align
    while t >= align:
        if n % t == 0:
            return t
        t -= align
    raise ValueError(f"no tile for {n} (pref {pref}, align {align})")


def _bcast_rows(v, rows=SUBLANES):
    return jnp.broadcast_to(v, (rows, v.shape[-1]))


def _mm(a, b, *, mode, name, out_dtype=F32, add=None, add_scale=1.0, tm=512, tn=1024, tk=1024):
    if mode == "nn":
        (M, K), N = a.shape, b.shape[1]
    elif mode == "nt":
        (M, K), N = a.shape, b.shape[0]
    else:
        (K, M), N = a.shape, b.shape[1]
    tm = _tile(M, tm, LANES if mode == "tn" else SUBLANES * 2)
    tn = _tile(N, tn, LANES)
    tk = _tile(K, tk, LANES)
    nk = K // tk
    if mode == "nn":
        a_spec = pl.BlockSpec((tm, tk), lambda i, j, k: (i, k))
        b_spec = pl.BlockSpec((tk, tn), lambda i, j, k: (k, j))
        dot = _dot
    elif mode == "nt":
        a_spec = pl.BlockSpec((tm, tk), lambda i, j, k: (i, k))
        b_spec = pl.BlockSpec((tn, tk), lambda i, j, k: (j, k))
        dot = _dot_nt
    else:
        a_spec = pl.BlockSpec((tk, tm), lambda i, j, k: (k, i))
        b_spec = pl.BlockSpec((tk, tn), lambda i, j, k: (k, j))
        dot = _dot_tn
    o_spec = pl.BlockSpec((tm, tn), lambda i, j, k: (i, j))
    has_add = add is not None

    def body(*refs):
        if has_add:
            a_ref, b_ref, add_ref, o_ref, acc_ref = refs
        else:
            a_ref, b_ref, o_ref, acc_ref = refs
            add_ref = None
        k = pl.program_id(2)
        part = dot(a_ref[...], b_ref[...])

        def finish(total):
            if has_add:
                total = total + add_scale * add_ref[...]
            o_ref[...] = total.astype(out_dtype)

        if nk == 1:
            finish(part)
        else:
            @pl.when(k == 0)
            def _():
                acc_ref[...] = part

            @pl.when(jnp.logical_and(k > 0, k < nk - 1))
            def _():
                acc_ref[...] += part

            @pl.when(k == nk - 1)
            def _():
                finish(acc_ref[...] + part)

    in_specs = [a_spec, b_spec] + ([o_spec] if has_add else [])
    args = (a, b) + ((add,) if has_add else ())
    est = (_nbytes((tm, tk), a.dtype) + _nbytes((tk, tn), b.dtype) + 2 * _nbytes((tm, tn), F32)
           + (_nbytes((tm, tn), F32) if has_add else 0)) + 2 * _nbytes((tm, tn), F32)
    return pl.pallas_call(
        body, name=name,
        grid=(M // tm, N // tn, nk),
        in_specs=in_specs, out_specs=o_spec,
        out_shape=jax.ShapeDtypeStruct((M, N), out_dtype),
        scratch_shapes=[pltpu.VMEM((tm, tn) if nk > 1 else (SUBLANES, LANES), F32)],
        compiler_params=_cparams(est, ("parallel", "parallel", "arbitrary")),
    )(*args)


def _conv_taps(xt, halo, w_ref, first):
    halo = jnp.where(first, 0.0, halo)
    xc = jnp.concatenate([halo, xt], axis=0)
    shifted = [xt] + [pltpu.roll(xc, s, 0)[HALO:] for s in range(1, CONV_K)]
    out = shifted[0] * w_ref[CONV_K - 1:CONV_K, :]
    for s in range(1, CONV_K):
        out = out + shifted[s] * w_ref[CONV_K - 1 - s:CONV_K - s, :]
    return out, shifted


def _gates(ba, arow, dtrow):
    lane = _iota(ba.shape, 1)
    beta = _sigmoid(ba)
    g = -jnp.exp(arow) * _softplus(ba + dtrow)
    return jnp.where(lane < N_HEADS, beta, jnp.where(lane < 2 * N_HEADS, g, 0.0))


def _l2n(x):
    return x * lax.rsqrt(jnp.sum(x * x, -1, keepdims=True) + RMS_EPS)


def _qkv_prep(proj, ba, convw, arow, dtrow, *, tm=256):
    S = proj.shape[0]
    tm = _tile(S, tm, SUBLANES)
    W3 = 3 * D_MODEL
    hb = tm // HALO

    def body(xt_ref, halo_ref, ba_ref, w_ref, a_ref, dt_ref, q_ref, k_ref, v_ref, gb_ref):
        c, _ = _conv_taps(xt_ref[...], halo_ref[...], w_ref, pl.program_id(0) == 0)
        c = _silu(c)
        for h in range(N_HEADS):
            lo = h * D_HEAD
            q_ref[:, lo:lo + D_HEAD] = _l2n(c[:, lo:lo + D_HEAD])
            k_ref[:, lo:lo + D_HEAD] = _l2n(c[:, D_MODEL + lo:D_MODEL + lo + D_HEAD])
        v_ref[...] = c[:, 2 * D_MODEL:]
        gb_ref[...] = _gates(ba_ref[...], a_ref[...], dt_ref[...])

    row = lambda w, col=0: pl.BlockSpec((tm, w), lambda i: (i, col))
    full = lambda shape: pl.BlockSpec(shape, lambda i: (0,) * len(shape))
    est = 4 * _nbytes((tm, W3), F32)
    return pl.pallas_call(
        body, name="qkv_prep", grid=(S // tm,),
        in_specs=[row(W3), pl.BlockSpec((HALO, W3), lambda i: (jnp.maximum(i * hb - 1, 0), 0)), row(LANES),
                  full((CONV_K, W3)), full((1, LANES)), full((1, LANES))],
        out_specs=[row(D_MODEL), row(D_MODEL), row(D_MODEL), row(LANES)],
        out_shape=[jax.ShapeDtypeStruct((S, D_MODEL), F32)] * 3 + [jax.ShapeDtypeStruct((S, LANES), F32)],
        compiler_params=_cparams(est, ("arbitrary",)),
    )(proj, proj, ba, convw, arow, dtrow)


def _qkv_prep_bwd(proj, ba, convw, arow, dtrow, dq, dk, dv, dgb, *, tm=256):
    S = proj.shape[0]
    tm = _tile(S, tm, SUBLANES)
    W3 = 3 * D_MODEL
    hb = tm // HALO

    def body(xt_ref, halo_ref, ba_ref, w_ref, a_ref, dt_ref, dq_ref, dk_ref, dv_ref, dgb_ref,
             dc_ref, dba_ref, dw_ref, da_ref, ddt_ref):
        i = pl.program_id(0)

        @pl.when(i == 0)
        def _():
            dw_ref[...] = jnp.zeros_like(dw_ref)
            da_ref[...] = jnp.zeros_like(da_ref)
            ddt_ref[...] = jnp.zeros_like(ddt_ref)

        c, shifted = _conv_taps(xt_ref[...], halo_ref[...], w_ref, i == 0)
        a = _silu(c)
        ds = _dsilu(c)
        for h in range(N_HEADS):
            for base, d_ref in ((0, dq_ref), (D_MODEL, dk_ref)):
                lo = base + h * D_HEAD
                _, vj = jax.vjp(_l2n, a[:, lo:lo + D_HEAD])
                (dx,) = vj(d_ref[:, h * D_HEAD:(h + 1) * D_HEAD])
                dc_ref[:, lo:lo + D_HEAD] = dx * ds[:, lo:lo + D_HEAD]
        dc_ref[:, 2 * D_MODEL:] = dv_ref[...] * ds[:, 2 * D_MODEL:]
        dc = dc_ref[...]
        for s in range(CONV_K):
            kk = CONV_K - 1 - s
            dw_ref[kk:kk + 1, :] += jnp.sum(dc * shifted[s], axis=0, keepdims=True)
        _, vj = jax.vjp(_gates, ba_ref[...], a_ref[...], dt_ref[...])
        dba, da, ddt = vj(dgb_ref[...])
        dba_ref[...] = dba.astype(BF)
        da_ref[...] += _bcast_rows(da)
        ddt_ref[...] += _bcast_rows(ddt)

    row = lambda w, col=0: pl.BlockSpec((tm, w), lambda i: (i, col))
    full = lambda shape: pl.BlockSpec(shape, lambda i: (0,) * len(shape))
    est = 8 * _nbytes((tm, W3), F32)
    return pl.pallas_call(
        body, name="qkv_prep_bwd", grid=(S // tm,),
        in_specs=[row(W3), pl.BlockSpec((HALO, W3), lambda i: (jnp.maximum(i * hb - 1, 0), 0)), row(LANES),
                  full((CONV_K, W3)), full((1, LANES)), full((1, LANES)),
                  row(D_MODEL), row(D_MODEL), row(D_MODEL), row(LANES)],
        out_specs=[row(W3), row(LANES), full((SUBLANES, W3)), full((SUBLANES, LANES)), full((SUBLANES, LANES))],
        out_shape=[jax.ShapeDtypeStruct((S, W3), F32), jax.ShapeDtypeStruct((S, LANES), BF),
                   jax.ShapeDtypeStruct((SUBLANES, W3), F32), jax.ShapeDtypeStruct((SUBLANES, LANES), F32),
                   jax.ShapeDtypeStruct((SUBLANES, LANES), F32)],
        compiler_params=_cparams(est, ("arbitrary",)),
    )(proj, proj, ba, convw, arow, dtrow, dq, dk, dv, dgb)


def _conv_bwd(dc, convw, *, tm=256):
    S, W3 = dc.shape
    tm = _tile(S, tm, SUBLANES * 2)
    hb = tm // HALO
    nt = S // tm

    def body(dc_ref, nxt_ref, w_ref, o_ref):
        last = pl.program_id(0) == nt - 1
        nxt = jnp.where(last, 0.0, nxt_ref[...])
        cur = dc_ref[...]
        xc = jnp.concatenate([cur, nxt], axis=0)
        out = cur * w_ref[CONV_K - 1:CONV_K, :]
        for s in range(1, CONV_K):
            out = out + pltpu.roll(xc, tm + HALO - s, 0)[:tm] * w_ref[CONV_K - 1 - s:CONV_K - s, :]
        o_ref[...] = out.astype(BF)

    est = 5 * _nbytes((tm, W3), F32)
    return pl.pallas_call(
        body, name="conv_bwd", grid=(nt,),
        in_specs=[pl.BlockSpec((tm, W3), lambda i: (i, 0)),
                  pl.BlockSpec((HALO, W3), lambda i: (jnp.minimum((i + 1) * hb, S // HALO - 1), 0)),
                  pl.BlockSpec((CONV_K, W3), lambda i: (0, 0))],
        out_specs=pl.BlockSpec((tm, W3), lambda i: (i, 0)),
        out_shape=jax.ShapeDtypeStruct((S, W3), BF),
        compiler_params=_cparams(est, ("parallel",)),
    )(dc, dc, convw)


def _inv_unit_lower(A):
    C = A.shape[-1]
    row, col = _iota((C, C), 0), _iota((C, C), 1)
    T = jnp.broadcast_to(jnp.where(row == col, 1.0, 0.0).astype(F32), A.shape)
    b = 1
    while b < C:
        hi = ~(2 * b - 1)
        off = ((row & hi) == (col & hi)) & ((row & b) != 0) & ((col & b) == 0)
        T = T - _dotf(_dotf(T, jnp.where(off, A, 0.0)), T)
        b *= 2
    return T


def _delta_common(q, k, g, beta):
    C = q.shape[-2]
    row, col = _iota((C, C), 0), _iota((C, C), 1)
    tril = row >= col
    qs = q * (D_HEAD ** -0.5)
    gcb = _dot01(jnp.where(tril, 1.0, 0.0), jnp.broadcast_to(g, g.shape[:-1] + (LANES,)))
    gc = gcb[..., :1]
    Dm = jnp.exp(jnp.where(tril, gc - jnp.swapaxes(gcb, -1, -2), -1e30))
    eg = jnp.exp(gc)
    gl = jnp.sum(jnp.where(_iota((C, 1), 0) == C - 1, gc, 0.0), axis=(-2, -1), keepdims=True)
    el = jnp.exp(gl)
    er = jnp.exp(gl - gc)
    kb = k * beta
    KK = _dot_nt(kb, k)
    QK = _dot_nt(qs, k)
    return dict(row=row, col=col, tril=tril, qs=qs, gc=gc, Dm=Dm, eg=eg, el=el, er=er, kb=kb, KK=KK, QK=QK)


def _delta_chunk_fwd(S0, q, k, v, g, beta, T=None):
    m = _delta_common(q, k, g, beta)
    if T is None:
        T = _inv_unit_lower(jnp.where(m["row"] > m["col"], m["KK"] * m["Dm"], 0.0))
    u = _dotf(T, v * beta)
    w = _dotf(T, m["kb"] * m["eg"])
    vn = u - _dot(w, S0)
    o = _dot(m["qs"] * m["eg"], S0) + _dot(m["QK"] * m["Dm"], vn)
    S1 = S0 * m["el"] + _dot_tn(k * m["er"], vn)
    return o, S1, T


def _delta_chunk_bwd(S0, q, k, v, g, beta, T, do, dS1):
    m = _delta_common(q, k, g, beta)
    C = q.shape[-2]
    qs, Dm, eg, el, er, kb, KK, QK = (m[n] for n in ("qs", "Dm", "eg", "el", "er", "kb", "KK", "QK"))
    strict = m["row"] > m["col"]
    total = lambda x: jnp.sum(x, axis=(-2, -1), keepdims=True)
    ru, rw = v * beta, kb * eg
    u = _dotf(T, ru)
    w = _dotf(T, rw)
    vn = u - _dot(w, S0)
    P = QK * Dm
    qg = qs * eg
    kr = k * er

    dvn = _dot_tn(P, do) + _dot(kr, dS1)
    dS0 = dS1 * el + _dot_tn(qg, do) - _dot_tn(w, dvn)
    d_el = total(dS1 * S0)
    dqg = _dot_nt(do, S0)
    dqs = dqg * eg
    deg = jnp.sum(dqg * qs, -1, keepdims=True)
    dP = _dot_nt(do, vn)
    dPD = dP * Dm
    dqs = dqs + _dot(dPD, k)
    dk = _dot_tn(dPD, qs)
    dD = dP * QK
    dkr = _dot_nt(vn, dS1)
    dk = dk + dkr * er
    der = jnp.sum(dkr * k, -1, keepdims=True)
    dw = -_dot_nt(dvn, S0)
    dru = _dotf_tn(T, dvn)
    drw = _dotf_tn(T, dw)
    dT = _dotf_nt(dvn, ru) + _dotf_nt(dw, rw)
    dA = -_dotf_nt(_dotf_tn(T, dT), T)
    dAm = jnp.where(strict, dA, 0.0)
    dKK = dAm * Dm
    dkb = _dot(dKK, k)
    dk = dk + _dot_tn(dKK, kb)
    dD = dD + dAm * KK
    dv = dru * beta
    dbeta = jnp.sum(dru * v, -1, keepdims=True)
    dkb = dkb + drw * eg
    deg = deg + jnp.sum(drw * kb, -1, keepdims=True)
    dk = dk + dkb * beta
    dbeta = dbeta + jnp.sum(dkb * k, -1, keepdims=True)
    E = dD * Dm
    dgc = jnp.sum(E, -1, keepdims=True) - jnp.sum(jnp.swapaxes(E, -1, -2), -1, keepdims=True)
    dgc = dgc + deg * eg - der * er
    dgl = total(der * er) + d_el * el
    dgc = dgc + jnp.where(_iota((C, 1), 0) == C - 1, dgl, 0.0)
    triu = jnp.where(m["row"] <= m["col"], 1.0, 0.0)
    dg = _dot01(triu, jnp.broadcast_to(dgc, dgc.shape[:-1] + (LANES,)))[..., :1]
    dq = dqs * (D_HEAD ** -0.5)
    return dq, dk, dv, dg, dbeta, dS0


def _head_cols(gb, h):
    lane = _iota(gb.shape, 1)
    beta = jnp.sum(jnp.where(lane == h, gb, 0.0), -1, keepdims=True)
    g = jnp.sum(jnp.where(lane == N_HEADS + h, gb, 0.0), -1, keepdims=True)
    return g, beta


def _delta_fwd(q, k, v, gb):
    S = q.shape[0]
    C = DELTA_CHUNK
    N = S // C

    HB = DELTA_HEADS_PER_STEP

    def body(q_ref, k_ref, v_ref, gb_ref, o_ref, st_ref, t_ref, s_scr):
        n, hb = pl.program_id(0), pl.program_id(1)
        gb = gb_ref[...]

        @pl.when(n == 0)
        def _():
            for hh in range(HB):
                s_scr[hb * HB + hh] = jnp.zeros((D_HEAD, D_HEAD), F32)

        heads = [hb * HB + hh for hh in range(HB)]
        cols = [slice(hh * D_HEAD, (hh + 1) * D_HEAD) for hh in range(HB)]
        per_head = lambda ref: jnp.stack([ref[:, c] for c in cols])
        g, beta = (jnp.stack(t) for t in zip(*[_head_cols(gb, h) for h in heads]))
        S0 = jnp.stack([s_scr[h] for h in heads])
        o, S1, T = _delta_chunk_fwd(S0, per_head(q_ref), per_head(k_ref), per_head(v_ref), g, beta)
        for hh in range(HB):
            st_ref[hh, 0] = S0[hh]
            t_ref[hh, 0] = T[hh]
            o_ref[:, cols[hh]] = o[hh]
            s_scr[heads[hh]] = S1[hh]

    hd = pl.BlockSpec((C, HB * D_HEAD), lambda n, h: (n, h))
    mat = pl.BlockSpec((HB, 1, D_HEAD, D_HEAD), lambda n, h: (h, n, 0, 0))
    est = 40 * HB * _nbytes((C, D_HEAD), F32)
    return pl.pallas_call(
        body, name="delta_fwd", grid=(N, N_HEADS // HB),
        in_specs=[hd, hd, hd, pl.BlockSpec((C, LANES), lambda n, h: (n, 0))],
        out_specs=[hd, mat, mat],
        out_shape=[jax.ShapeDtypeStruct((S, N_HEADS * D_HEAD), F32),
                   jax.ShapeDtypeStruct((N_HEADS, N, D_HEAD, D_HEAD), F32),
                   jax.ShapeDtypeStruct((N_HEADS, N, C, C), F32)],
        scratch_shapes=[pltpu.VMEM((N_HEADS, D_HEAD, D_HEAD), F32)],
        compiler_params=_cparams(est, ("arbitrary", "arbitrary")),
    )(q, k, v, gb)


def _delta_bwd(q, k, v, gb, st, tinv, do):
    S = q.shape[0]
    C = DELTA_CHUNK
    N = S // C

    HB = DELTA_HEADS_PER_STEP

    def body(q_ref, k_ref, v_ref, gb_ref, st_ref, t_ref, do_ref, dq_ref, dk_ref, dv_ref, dgb_ref, ds_scr):
        n, hb = pl.program_id(0), pl.program_id(1)
        gb = gb_ref[...]
        lane = _iota((C, LANES), 1)
        dgb = jnp.zeros((C, LANES), F32)

        @pl.when(n == 0)
        def _():
            for hh in range(HB):
                ds_scr[hb * HB + hh] = jnp.zeros((D_HEAD, D_HEAD), F32)

        heads = [hb * HB + hh for hh in range(HB)]
        cols = [slice(hh * D_HEAD, (hh + 1) * D_HEAD) for hh in range(HB)]
        per_head = lambda ref: jnp.stack([ref[:, c] for c in cols])
        g, beta = (jnp.stack(t) for t in zip(*[_head_cols(gb, h) for h in heads]))
        dS1 = jnp.stack([ds_scr[h] for h in heads])
        dq, dk, dv, dg, dbeta, dS0 = _delta_chunk_bwd(
            st_ref[:, 0], per_head(q_ref), per_head(k_ref), per_head(v_ref), g, beta, t_ref[:, 0], per_head(do_ref), dS1)
        for hh, h in enumerate(heads):
            dq_ref[:, cols[hh]] = dq[hh]
            dk_ref[:, cols[hh]] = dk[hh]
            dv_ref[:, cols[hh]] = dv[hh]
            dgb = dgb + jnp.where(lane == h, dbeta[hh], 0.0) + jnp.where(lane == N_HEADS + h, dg[hh], 0.0)
            ds_scr[h] = dS0[hh]

        @pl.when(hb == 0)
        def _():
            dgb_ref[...] = dgb

        @pl.when(hb > 0)
        def _():
            dgb_ref[...] += dgb

    hd = pl.BlockSpec((C, HB * D_HEAD), lambda n, h: (N - 1 - n, h))
    mat = pl.BlockSpec((HB, 1, D_HEAD, D_HEAD), lambda n, h: (h, N - 1 - n, 0, 0))
    gbs = pl.BlockSpec((C, LANES), lambda n, h: (N - 1 - n, 0))
    est = 60 * HB * _nbytes((C, D_HEAD), F32)
    return pl.pallas_call(
        body, name="delta_bwd", grid=(N, N_HEADS // HB),
        in_specs=[hd, hd, hd, gbs, mat, mat, hd],
        out_specs=[hd, hd, hd, gbs],
        out_shape=[jax.ShapeDtypeStruct((S, N_HEADS * D_HEAD), F32)] * 3 + [jax.ShapeDtypeStruct((S, LANES), F32)],
        scratch_shapes=[pltpu.VMEM((N_HEADS, D_HEAD, D_HEAD), F32)],
        compiler_params=_cparams(est, ("arbitrary", "arbitrary")),
    )(q, k, v, gb, st, tinv, do)


def _ya_head(o, z, onw):
    return o * lax.rsqrt(jnp.mean(o * o, -1, keepdims=True) + RMS_EPS) * onw * _silu(z)


def _sgu_pre(u, vg, sg, sb):
    return _gelu(u), _ln(_gelu(vg), sg, sb)


def _chunk_causal(shape, di, dj):
    sh = jnp.int32(int(math.log2(SGU_CHUNK)))
    return lax.shift_right_logical(_iota(shape, di), sh) >= lax.shift_right_logical(_iota(shape, dj), sh)


def _ws_masked(ws):
    return jnp.where(_chunk_causal(ws.shape, 1, 2), ws, 0.0)


def _mix_prep(o, proj, onw, sg, sb, ws, bst, *, tm=256):
    S = o.shape[0]
    tm = _tile(S, tm, SGU_BLOCK)

    def body(o_ref, z_ref, u_ref, vg_ref, onw_ref, sg_ref, sb_ref, ws_ref, bst_ref, ya_ref, yb_ref):
        onw = onw_ref[...]
        for h in range(N_HEADS):
            sl = slice(h * D_HEAD, (h + 1) * D_HEAD)
            ya_ref[:, sl] = _ya_head(o_ref[:, sl], z_ref[:, sl], onw).astype(BF)
        ua, vl = _sgu_pre(u_ref[...], vg_ref[...], sg_ref[...], sb_ref[...])
        wsm = _ws_masked(ws_ref[...])
        bst = bst_ref[...]
        for blk in range(tm // SGU_BLOCK):
            rs = slice(blk * SGU_BLOCK, (blk + 1) * SGU_BLOCK)
            for gi in range(SGU_GROUPS):
                cs = slice(gi * D_HEAD, (gi + 1) * D_HEAD)
                sp = _dot(wsm[gi], vl[rs, cs]) + bst[:, gi:gi + 1]
                yb_ref[rs, cs] = (ua[rs, cs] * sp).astype(BF)

    blk = lambda col: pl.BlockSpec((tm, D_MODEL), lambda i: (i, col))
    full = lambda shape: pl.BlockSpec(shape, lambda i: (0,) * len(shape))
    est = 10 * _nbytes((tm, D_MODEL), F32)
    return pl.pallas_call(
        body, name="mix_prep", grid=(S // tm,),
        in_specs=[blk(0), blk(3), blk(4), blk(5), full((1, D_HEAD)), full((1, D_MODEL)), full((1, D_MODEL)),
                  full((SGU_GROUPS, SGU_BLOCK, SGU_BLOCK)), full((SGU_BLOCK, LANES))],
        out_specs=[blk(0), blk(0)],
        out_shape=[jax.ShapeDtypeStruct((S, D_MODEL), BF)] * 2,
        compiler_params=_cparams(est, ("parallel",)),
    )(o, proj, proj, proj, onw, sg, sb, ws, bst)


def _mix_prep_bwd(o, proj, onw, sg, sb, ws, bst, dya, dyb, *, tm=256):
    S = o.shape[0]
    tm = _tile(S, tm, SGU_BLOCK)

    def body(o_ref, z_ref, u_ref, vg_ref, onw_ref, sg_ref, sb_ref, ws_ref, bst_ref, dya_ref, dyb_ref,
             do_ref, dz_ref, du_ref, dvg_ref, donw_ref, dsg_ref, dsb_ref, dws_ref, dbst_ref, dvl_scr, dua_scr):
        @pl.when(pl.program_id(0) == 0)
        def _():
            for r in (donw_ref, dsg_ref, dsb_ref, dws_ref, dbst_ref):
                r[...] = jnp.zeros_like(r)

        onw = onw_ref[...]
        donw = jnp.zeros((1, D_HEAD), F32)
        for h in range(N_HEADS):
            sl = slice(h * D_HEAD, (h + 1) * D_HEAD)
            _, vj = jax.vjp(_ya_head, o_ref[:, sl], z_ref[:, sl], onw)
            do_h, dz_h, donw_h = vj(dya_ref[:, sl])
            do_ref[:, sl] = do_h
            dz_ref[:, sl] = dz_h.astype(BF)
            donw = donw + donw_h
        donw_ref[...] += _bcast_rows(donw)

        (ua, vl), vj = jax.vjp(_sgu_pre, u_ref[...], vg_ref[...], sg_ref[...], sb_ref[...])
        wsm = _ws_masked(ws_ref[...])
        bst = bst_ref[...]
        lane = _iota((SGU_BLOCK, LANES), 1)
        dbst = jnp.zeros((SGU_BLOCK, LANES), F32)
        cmask = _chunk_causal((SGU_BLOCK, SGU_BLOCK), 0, 1)
        for gi in range(SGU_GROUPS):
            cs = slice(gi * D_HEAD, (gi + 1) * D_HEAD)
            wg = wsm[gi]
            wgt = jnp.transpose(wg)
            dwg = jnp.zeros((SGU_BLOCK, SGU_BLOCK), F32)
            for blk in range(tm // SGU_BLOCK):
                rs = slice(blk * SGU_BLOCK, (blk + 1) * SGU_BLOCK)
                sp = _dot(wg, vl[rs, cs]) + bst[:, gi:gi + 1]
                dyb = dyb_ref[rs, cs]
                dsp = dyb * ua[rs, cs]
                dua_scr[rs, cs] = dyb * sp
                dvl_scr[rs, cs] = _dot(wgt, dsp)
                dwg = dwg + _dot_nt(dsp, vl[rs, cs])
                dbst = dbst + jnp.where(lane == gi, jnp.sum(dsp, -1, keepdims=True), 0.0)
            dws_ref[gi] += jnp.where(cmask, dwg, 0.0)
        dbst_ref[...] += dbst
        du, dvg, dsg, dsb = vj((dua_scr[...], dvl_scr[...]))
        du_ref[...] = du.astype(BF)
        dvg_ref[...] = dvg.astype(BF)
        dsg_ref[...] += _bcast_rows(dsg)
        dsb_ref[...] += _bcast_rows(dsb)

    blk = lambda col: pl.BlockSpec((tm, D_MODEL), lambda i: (i, col))
    full = lambda shape: pl.BlockSpec(shape, lambda i: (0,) * len(shape))
    est = 16 * _nbytes((tm, D_MODEL), F32)
    outs = pl.pallas_call(
        body, name="mix_prep_bwd", grid=(S // tm,),
        in_specs=[blk(0), blk(3), blk(4), blk(5), full((1, D_HEAD)), full((1, D_MODEL)), full((1, D_MODEL)),
                  full((SGU_GROUPS, SGU_BLOCK, SGU_BLOCK)), full((SGU_BLOCK, LANES)), blk(0), blk(0)],
        out_specs=[blk(0)] * 4 + [full((SUBLANES, D_HEAD)), full((SUBLANES, D_MODEL)), full((SUBLANES, D_MODEL)),
                                  full((SGU_GROUPS, SGU_BLOCK, SGU_BLOCK)), full((SGU_BLOCK, LANES))],
        out_shape=[jax.ShapeDtypeStruct((S, D_MODEL), F32)] + [jax.ShapeDtypeStruct((S, D_MODEL), BF)] * 3
                  + [jax.ShapeDtypeStruct((SUBLANES, D_HEAD), F32), jax.ShapeDtypeStruct((SUBLANES, D_MODEL), F32),
                     jax.ShapeDtypeStruct((SUBLANES, D_MODEL), F32),
                     jax.ShapeDtypeStruct((SGU_GROUPS, SGU_BLOCK, SGU_BLOCK), F32),
                     jax.ShapeDtypeStruct((SGU_BLOCK, LANES), F32)],
        scratch_shapes=[pltpu.VMEM((tm, D_MODEL), F32)] * 2,
        compiler_params=_cparams(est, ("arbitrary",)),
    )(o, proj, proj, proj, onw, sg, sb, ws, bst, dya, dyb)
    return outs


def _gate_merge(pa, pb, proj, *, tm=512):
    S = pa.shape[0]
    tm = _tile(S, tm, SUBLANES * 2)

    def body(pa_ref, pb_ref, ga_ref, gb_ref, m_ref):
        m_ref[...] = (_sigmoid(ga_ref[...]) * pa_ref[...] + _sigmoid(gb_ref[...]) * pb_ref[...]).astype(BF)

    blk = lambda col: pl.BlockSpec((tm, D_MODEL), lambda i: (i, col))
    return pl.pallas_call(
        body, name="gate_merge", grid=(S // tm,),
        in_specs=[blk(0), blk(0), blk(6), blk(7)], out_specs=blk(0),
        out_shape=jax.ShapeDtypeStruct((S, D_MODEL), BF),
        compiler_params=_cparams(6 * _nbytes((tm, D_MODEL), F32), ("parallel",)),
    )(pa, pb, proj, proj)


def _gate_merge_bwd(pa, pb, proj, dm, *, tm=512):
    S = pa.shape[0]
    tm = _tile(S, tm, SUBLANES * 2)

    def body(pa_ref, pb_ref, ga_ref, gb_ref, dm_ref, dpa_ref, dpb_ref, dga_ref, dgb_ref):
        dm = dm_ref[...]
        sa, sb = _sigmoid(ga_ref[...]), _sigmoid(gb_ref[...])
        dpa_ref[...] = (dm * sa).astype(BF)
        dpb_ref[...] = (dm * sb).astype(BF)
        dga_ref[...] = (dm * pa_ref[...] * sa * (1.0 - sa)).astype(BF)
        dgb_ref[...] = (dm * pb_ref[...] * sb * (1.0 - sb)).astype(BF)

    blk = lambda col: pl.BlockSpec((tm, D_MODEL), lambda i: (i, col))
    return pl.pallas_call(
        body, name="gate_merge_bwd", grid=(S // tm,),
        in_specs=[blk(0), blk(0), blk(6), blk(7), blk(0)], out_specs=[blk(0)] * 4,
        out_shape=[jax.ShapeDtypeStruct((S, D_MODEL), BF)] * 4,
        compiler_params=_cparams(10 * _nbytes((tm, D_MODEL), F32), ("parallel",)),
    )(pa, pb, proj, proj, dm)


def _swiglu_act(hgu, *, tm=256):
    S = hgu.shape[0]
    tm = _tile(S, tm, SUBLANES * 2)

    def body(hg_ref, hu_ref, h_ref):
        h_ref[...] = (_silu(hg_ref[...]) * hu_ref[...]).astype(BF)

    blk = lambda col: pl.BlockSpec((tm, FFN_K), lambda i: (i, col))
    return pl.pallas_call(
        body, name="swiglu_act", grid=(S // tm,),
        in_specs=[blk(0), blk(1)], out_specs=blk(0),
        out_shape=jax.ShapeDtypeStruct((S, FFN_K), BF),
        compiler_params=_cparams(5 * _nbytes((tm, FFN_K), F32), ("parallel",)),
    )(hgu, hgu)


def _swiglu_bwd(hgu, dh, *, tm=256):
    S = hgu.shape[0]
    tm = _tile(S, tm, SUBLANES * 2)

    def body(hg_ref, hu_ref, dh_ref, d_ref):
        hg, dh = hg_ref[...], dh_ref[...]
        d_ref[:, :FFN_K] = (dh * hu_ref[...] * _dsilu(hg)).astype(BF)
        d_ref[:, FFN_K:] = (dh * _silu(hg)).astype(BF)

    blk = lambda col: pl.BlockSpec((tm, FFN_K), lambda i: (i, col))
    return pl.pallas_call(
        body, name="swiglu_bwd", grid=(S // tm,),
        in_specs=[blk(0), blk(1), blk(0)], out_specs=pl.BlockSpec((tm, 2 * FFN_K), lambda i: (i, 0)),
        out_shape=jax.ShapeDtypeStruct((S, 2 * FFN_K), BF),
        compiler_params=_cparams(8 * _nbytes((tm, FFN_K), F32), ("parallel",)),
    )(hgu, hgu, dh)


def _resid_ln(x, r, g, b, *, tm=512):
    S = x.shape[0]
    tm = _tile(S, tm, SUBLANES)

    def body(x_ref, r_ref, g_ref, b_ref, y_ref):
        y_ref[...] = _ln(ALPHA * x_ref[...] + r_ref[...], g_ref[...], b_ref[...])

    blk = pl.BlockSpec((tm, D_MODEL), lambda i: (i, 0))
    vec = pl.BlockSpec((1, D_MODEL), lambda i: (0, 0))
    return pl.pallas_call(
        body, name="resid_ln", grid=(S // tm,),
        in_specs=[blk, blk, vec, vec], out_specs=blk,
        out_shape=jax.ShapeDtypeStruct((S, D_MODEL), F32),
        compiler_params=_cparams(6 * _nbytes((tm, D_MODEL), F32), ("parallel",)),
    )(x, r, g, b)


def _resid_ln_bwd(x, r, g, b, dy, *, tm=512):
    S = x.shape[0]
    tm = _tile(S, tm, SUBLANES)

    def body(x_ref, r_ref, g_ref, b_ref, dy_ref, dp_ref, dg_ref, db_ref):
        @pl.when(pl.program_id(0) == 0)
        def _():
            dg_ref[...] = jnp.zeros_like(dg_ref)
            db_ref[...] = jnp.zeros_like(db_ref)

        _, vj = jax.vjp(_ln, ALPHA * x_ref[...] + r_ref[...], g_ref[...], b_ref[...])
        dp, dg, db = vj(dy_ref[...])
        dp_ref[...] = dp
        dg_ref[...] += _bcast_rows(dg)
        db_ref[...] += _bcast_rows(db)

    blk = pl.BlockSpec((tm, D_MODEL), lambda i: (i, 0))
    vec = pl.BlockSpec((1, D_MODEL), lambda i: (0, 0))
    acc = pl.BlockSpec((SUBLANES, D_MODEL), lambda i: (0, 0))
    return pl.pallas_call(
        body, name="resid_ln_bwd", grid=(S // tm,),
        in_specs=[blk, blk, vec, vec, blk], out_specs=[blk, acc, acc],
        out_shape=[jax.ShapeDtypeStruct((S, D_MODEL), F32)] + [jax.ShapeDtypeStruct((SUBLANES, D_MODEL), F32)] * 2,
        compiler_params=_cparams(10 * _nbytes((tm, D_MODEL), F32), ("arbitrary",)),
    )(x, r, g, b, dy)


def _loss_head(y, tgt, *, tm=512):
    S = y.shape[0]
    tm = _tile(S, tm, SUBLANES)

    def body(y_ref, t_ref, dy_ref, l_ref):
        @pl.when(pl.program_id(0) == 0)
        def _():
            l_ref[...] = jnp.zeros_like(l_ref)

        e = y_ref[...] - t_ref[...]
        dy_ref[...] = e * (1.0 / D_MODEL)
        l_ref[...] += 0.5 * jnp.sum(jnp.mean(e * e, -1, keepdims=True), keepdims=True)

    blk = pl.BlockSpec((tm, D_MODEL), lambda i: (i, 0))
    return pl.pallas_call(
        body, name="loss_head", grid=(S // tm,),
        in_specs=[blk, blk], out_specs=[blk, pl.BlockSpec((SUBLANES, LANES), lambda i: (0, 0))],
        out_shape=[jax.ShapeDtypeStruct((S, D_MODEL), F32), jax.ShapeDtypeStruct((SUBLANES, LANES), F32)],
        compiler_params=_cparams(6 * _nbytes((tm, D_MODEL), F32), ("arbitrary",)),
    )(y, tgt)


def _layer_fwd(x, w, late):
    proj = _mm(x, w["win"], mode="nn", name="mm_in", tn=1024)
    ba = _mm(x, w["wba"], mode="nn", name="mm_in_ba", tm=1024, tn=LANES)
    qn, kn, vv, gb = _qkv_prep(proj, ba, w["convw"], w["arow"], w["dtrow"])
    o, st, tinv = _delta_fwd(qn, kn, vv, gb)
    ya, yb = _mix_prep(o, proj, w["onw"], w["sg"], w["sb"], w["ws"], w["bst"])
    w = {**w, **late(ya)}
    pa = _mm(ya, w["wpa"], mode="nn", name="mm_sq")
    pb = _mm(yb, w["wpb"], mode="nn", name="mm_sq")
    m = _gate_merge(pa, pb, proj)
    mix = _mm(m, w["wo"], mode="nn", name="mm_sq")
    x1 = _resid_ln(x, mix, w["ln1g"], w["ln1b"])
    hgu = _mm(x1, w["wgu"], mode="nn", name="mm_gu", tn=1536)
    h = _swiglu_act(hgu)
    ffn = _mm(h, w["wd"], mode="nn", name="mm_down", tk=FFN_K)
    x2 = _resid_ln(x1, ffn, w["ln2g"], w["ln2b"])
    saved = dict(x=x, proj=proj, ba=ba, qn=qn, kn=kn, vv=vv, gb=gb, o=o, st=st, tinv=tinv, ya=ya, yb=yb,
                 pa=pa, pb=pb, m=m, mix=mix, x1=x1, hgu=hgu, h=h, ffn=ffn)
    return x2, saved, w


def _layer_bwd(dx2, w, s):
    g = {}
    dpre2, g["ln2g"], g["ln2b"] = _resid_ln_bwd(s["x1"], s["ffn"], w["ln2g"], w["ln2b"], dx2)
    dh = _mm(dpre2, w["wd"], mode="nt", name="mm_nt_down", tn=1536)
    g["wd"] = _mm(s["h"], dpre2, mode="tn", name="mm_tn_down", tm=1536, tk=512, out_dtype=BF)
    dhgu = _swiglu_bwd(s["hgu"], dh)
    dx1 = _mm(dhgu, w["wgu"], mode="nt", name="mm_nt_gu", add=dpre2, add_scale=ALPHA, tk=1536)
    g["wgu"] = _mm(s["x1"], dhgu, mode="tn", name="mm_tn_gu", tm=1024, tn=1536, tk=512, out_dtype=BF)
    dpre1, g["ln1g"], g["ln1b"] = _resid_ln_bwd(s["x"], s["mix"], w["ln1g"], w["ln1b"], dx1)
    dm = _mm(dpre1, w["wo"], mode="nt", name="mm_nt_sq")
    g["wo"] = _mm(s["m"], dpre1, mode="tn", name="mm_tn_sq", tm=1024, tk=512, out_dtype=BF)
    dpa, dpb, dga, dgb_gate = _gate_merge_bwd(s["pa"], s["pb"], s["proj"], dm)
    dya = _mm(dpa, w["wpa"], mode="nt", name="mm_nt_sq")
    g["wpa"] = _mm(s["ya"], dpa, mode="tn", name="mm_tn_sq", tm=1024, tk=512, out_dtype=BF)
    dyb = _mm(dpb, w["wpb"], mode="nt", name="mm_nt_sq")
    g["wpb"] = _mm(s["yb"], dpb, mode="tn", name="mm_tn_sq", tm=1024, tk=512, out_dtype=BF)
    do, dz, du, dvg, g["onw"], g["sg"], g["sb"], g["ws"], g["bst"] = _mix_prep_bwd(
        s["o"], s["proj"], w["onw"], w["sg"], w["sb"], w["ws"], w["bst"], dya, dyb)
    dqn, dkn, dvv, dgb = _delta_bwd(s["qn"], s["kn"], s["vv"], s["gb"], s["st"], s["tinv"], do)
    dc, dba, g["convw"], g["arow"], g["dtrow"] = _qkv_prep_bwd(
        s["proj"], s["ba"], w["convw"], w["arow"], w["dtrow"], dqn, dkn, dvv, dgb)
    dqkv = _conv_bwd(dc, w["convw"])
    dproj = jnp.concatenate([dqkv, dz, du, dvg, dga, dgb_gate], axis=1)
    dx = _mm(dproj, w["win"], mode="nt", name="mm_nt_in", add=dpre1, add_scale=ALPHA, tk=1024)
    dx = _mm(dba, w["wba"], mode="nt", name="mm_nt_ba", add=dx, add_scale=1.0, tm=1024)
    g["win"] = _mm(s["x"], dproj, mode="tn", name="mm_tn_in", tm=1024, tn=1024, tk=512, out_dtype=BF)
    g["wba"] = _mm(s["x"], dba, mode="tn", name="mm_tn_ba", tm=1024, tn=LANES, tk=1024, out_dtype=BF)
    return dx, g


def _local_step(x, tgt, layers, on_grads=None):
    saved, weights = [], []
    for layer in layers:
        x, s, w = _layer_fwd(x, *layer(x))
        saved.append(s)
        weights.append(w)
    dy, lacc = _loss_head(x, tgt)
    grads, token = [None] * len(layers), None
    for l in reversed(range(len(layers))):
        w = weights[l] if token is None else {**weights[l], "ln2g": weights[l]["ln2g"] + token}
        dy, grads[l] = _layer_bwd(dy, w, saved[l])
        token = on_grads(l, grads[l]) if on_grads is not None else None
    return lacc[0, 0], dy, grads


_QKVZ = 4 * D_MODEL
_BA = 2 * N_HEADS


WEIGHT_NAMES = ("w_in", "conv_w", "a_log", "dt_bias", "o_norm_w", "sgu_ln_g", "sgu_ln_b", "w_s", "b_s", "w_pa", "w_pb",
                "w_o", "ln1_g", "ln1_b", "w_ffn_gate", "w_ffn_up", "w_ffn_down", "ln2_g", "ln2_b")
WIRE = ("w_in", "w_ffn_gate", "w_ffn_up", "w_ffn_down", "w_pa", "w_pb", "w_o", "conv_w")
SMALL = (("a_log", N_HEADS), ("dt_bias", N_HEADS), ("o_norm_w", D_HEAD), ("sgu_ln_g", D_MODEL), ("sgu_ln_b", D_MODEL),
         ("w_s", SGU_GROUPS * SGU_BLOCK * SGU_BLOCK), ("b_s", SGU_GROUPS * SGU_BLOCK),
         ("ln1_g", D_MODEL), ("ln1_b", D_MODEL), ("ln2_g", D_MODEL), ("ln2_b", D_MODEL))
SMALL_ROWS = -(-sum(n for _, n in SMALL) // (LANES * SUBLANES)) * SUBLANES
N_MAIN_TILES = (N_IN - _BA) // D_MODEL
ADAM_TILES = dict(w_in=(128, "adamw_in"), w_ffn_gate=(256, "adamw_ffn_cols"), w_ffn_up=(256, "adamw_ffn_cols"),
                  w_ffn_down=(32, "adamw_ffn_rows"), w_pa=(128, "adamw_sq"), w_pb=(128, "adamw_sq"), w_o=(128, "adamw_sq"),
                  conv_w=(CONV_K, "adamw_conv"))


def _pad_to(a, axis, size):
    pads = [(0, 0)] * a.ndim
    pads[axis] = (0, size - a.shape[axis])
    return jnp.pad(a, pads)


def _wire_blocks(p):
    return dict(
        w_in=_pad_to(p["w_in"].astype(BF), 2, IN_PAD),
        w_ffn_gate=_pad_to(p["w_ffn_gate"].astype(BF), 2, FFN_PAD), w_ffn_up=_pad_to(p["w_ffn_up"].astype(BF), 2, FFN_PAD),
        w_ffn_down=_pad_to(p["w_ffn_down"].astype(BF), 1, FFN_PAD),
        w_pa=p["w_pa"].astype(BF), w_pb=p["w_pb"].astype(BF), w_o=p["w_o"].astype(BF),
        conv_w=_pad_to(p["conv_w"], 1, SUBLANES),
    )


def _by_columns(blocks):
    n, r, c = blocks.shape
    return jnp.transpose(blocks, (1, 0, 2)).reshape(r, n * c)


def _to_slots(full, c):
    r = full.shape[0]
    return jnp.transpose(full.reshape(r, N_DEV, c), (1, 0, 2))


def _lane_row(v, at):
    return jnp.pad(v[None], ((0, 0), (at, LANES - at - v.shape[0])))


EARLY = ("w_in", "conv_w")
LATE = ("w_pa", "w_pb", "w_o", "w_ffn_gate", "w_ffn_up", "w_ffn_down")


def _early_weights(stacks, p, l):
    return dict(
        win=_perm_in(stacks["w_in"], D_MODEL, N_MAIN_TILES), wba=_perm_in(stacks["w_in"], LANES, 1),
        convw=_by_columns(stacks["conv_w"][:, :CONV_K]),
        arow=_lane_row(p["a_log"][l], N_HEADS), dtrow=_lane_row(p["dt_bias"][l], N_HEADS),
        onw=p["o_norm_w"][l][None], sg=p["sgu_ln_g"][l][None], sb=p["sgu_ln_b"][l][None],
        ws=p["w_s"][l], bst=_pad_to(p["b_s"][l].T, 1, LANES),
        ln1g=p["ln1_g"][l][None], ln1b=p["ln1_b"][l][None], ln2g=p["ln2_g"][l][None], ln2b=p["ln2_b"][l][None],
    )


def _late_weights(stacks):
    return dict(
        wpa=stacks["w_pa"].reshape(D_MODEL, D_MODEL), wpb=stacks["w_pb"].reshape(D_MODEL, D_MODEL),
        wo=stacks["w_o"].reshape(D_MODEL, D_MODEL),
        wgu=jnp.concatenate([_by_columns(stacks["w_ffn_gate"]), _by_columns(stacks["w_ffn_up"])], axis=1),
        wd=stacks["w_ffn_down"].reshape(FFN_K, D_MODEL),
    )


def _small_pack(parts):
    flat = jnp.concatenate([parts[n].reshape(-1) for n, _ in SMALL])
    return _pad_to(flat, 0, SMALL_ROWS * LANES).reshape(SMALL_ROWS, LANES)


def _small_unpack(rows, like):
    flat, out, off = rows.reshape(-1), {}, 0
    for n, size in SMALL:
        out[n] = flat[off:off + size].reshape(like[n].shape[1:])
        off += size
    return out


def _layer_slots(g):
    slots = dict(
        w_in=_perm_out(g["win"], g["wba"]),
        w_ffn_gate=_to_slots(g["wgu"][:, :FFN_K], FFN_PAD), w_ffn_up=_to_slots(g["wgu"][:, FFN_K:], FFN_PAD),
        w_ffn_down=g["wd"].reshape(N_DEV, FFN_PAD, D_MODEL),
        w_pa=g["wpa"].reshape(N_DEV, D_MODEL // N_DEV, D_MODEL), w_pb=g["wpb"].reshape(N_DEV, D_MODEL // N_DEV, D_MODEL),
        w_o=g["wo"].reshape(N_DEV, D_MODEL // N_DEV, D_MODEL),
        conv_w=_pad_to(_to_slots(g["convw"][:CONV_K], 3 * D_MODEL // N_DEV), 1, SUBLANES),
    )
    small = _small_pack(dict(
        a_log=g["arow"][0, N_HEADS:2 * N_HEADS], dt_bias=g["dtrow"][0, N_HEADS:2 * N_HEADS], o_norm_w=g["onw"][0],
        sgu_ln_g=g["sg"][0], sgu_ln_b=g["sb"][0], w_s=g["ws"], b_s=g["bst"][:, :SGU_GROUPS].T,
        ln1_g=g["ln1g"][0], ln1_b=g["ln1b"][0], ln2_g=g["ln2g"][0], ln2_b=g["ln2b"][0]))
    return [slots[n] for n in WIRE], small


def _in_tile_start(j, tile_w):
    if tile_w == LANES:
        return jnp.int32(_QKVZ)
    return j * D_MODEL + jnp.where(j >= _QKVZ // D_MODEL, _BA, 0)


def _select(rows_iota, cols_iota, dev, start, valid):
    hit = (rows_iota + (dev * IN_BLOCK - start) == cols_iota) & (rows_iota < IN_BLOCK) & (cols_iota < valid)
    return jnp.where(hit, 1.0, 0.0).astype(BF)


def _perm_in(stack, tile_w, n_tiles):
    valid = _BA if tile_w == LANES else tile_w

    def first_dev(j):
        return lax.div(_in_tile_start(j, tile_w), jnp.int32(IN_BLOCK))

    def body(w_ref, o_ref, acc_ref):
        j, k = pl.program_id(0), pl.program_id(1)
        sel = _select(_iota((IN_PAD, tile_w), 0), _iota((IN_PAD, tile_w), 1), first_dev(j) + k,
                      _in_tile_start(j, tile_w), valid)
        part = jnp.dot(w_ref[0], sel, preferred_element_type=F32)

        @pl.when(k == 0)
        def _():
            acc_ref[...] = part

        @pl.when(k == 1)
        def _():
            o_ref[...] = (acc_ref[...] + part).astype(BF)

    est = _nbytes((D_MODEL, IN_PAD), BF) + 3 * _nbytes((D_MODEL, tile_w), F32) + 2 * _nbytes((IN_PAD, tile_w), F32)
    return pl.pallas_call(
        body, name="perm_in" if tile_w != LANES else "perm_in_ba", grid=(n_tiles, 2),
        in_specs=[pl.BlockSpec((1, D_MODEL, IN_PAD), lambda j, k: (jnp.minimum(first_dev(j) + k, N_DEV - 1), 0, 0))],
        out_specs=pl.BlockSpec((D_MODEL, tile_w), lambda j, k: (0, j)),
        out_shape=jax.ShapeDtypeStruct((D_MODEL, n_tiles * tile_w), BF),
        scratch_shapes=[pltpu.VMEM((D_MODEL, tile_w), F32)],
        compiler_params=_cparams(est, ("parallel", "arbitrary")),
    )(stack)


def _perm_out(dmain, dba):
    n_main = dmain.shape[1] // D_MODEL

    def body(dm_ref, db_ref, o_ref, acc_ref):
        d, t = pl.program_id(0), pl.program_id(1)

        @pl.when(t == 0)
        def _():
            acc_ref[...] = jnp.zeros_like(acc_ref)

        start = _in_tile_start(t, D_MODEL)
        overlaps = (start < (d + 1) * IN_BLOCK) & (d * IN_BLOCK < start + D_MODEL)

        @pl.when((t < n_main) & overlaps)
        def _():
            sel = _select(_iota((D_MODEL, IN_PAD), 1), _iota((D_MODEL, IN_PAD), 0), d, start, D_MODEL)
            acc_ref[...] += jnp.dot(dm_ref[...], sel, preferred_element_type=F32)

        @pl.when(t == n_main)
        def _():
            sel = _select(_iota((LANES, IN_PAD), 1), _iota((LANES, IN_PAD), 0), d, jnp.int32(_QKVZ), _BA)
            o_ref[0] = (acc_ref[...] + jnp.dot(db_ref[...], sel, preferred_element_type=F32)).astype(BF)

    est = 2 * _nbytes((D_MODEL, D_MODEL), BF) + 4 * _nbytes((D_MODEL, IN_PAD), F32)
    return pl.pallas_call(
        body, name="perm_out", grid=(N_DEV, n_main + 1),
        in_specs=[pl.BlockSpec((D_MODEL, D_MODEL), lambda d, t: (0, jnp.minimum(t, n_main - 1))),
                  pl.BlockSpec((D_MODEL, LANES), lambda d, t: (0, 0))],
        out_specs=pl.BlockSpec((1, D_MODEL, IN_PAD), lambda d, t: (d, 0, 0)),
        out_shape=jax.ShapeDtypeStruct((N_DEV, D_MODEL, IN_PAD), BF),
        scratch_shapes=[pltpu.VMEM((D_MODEL, IN_PAD), F32)],
        compiler_params=_cparams(est, ("parallel", "arbitrary")),
    )(dmain, dba)


def _mesh_place():
    x, y, c = (lax.axis_index(a) for a in MESH_AXES)
    return x, y, c


def _slot(x, y, c):
    return 4 * x + 2 * y + c


def _peer(place, j):
    x, y, c = place
    return (1 - x if j & 4 else x, 1 - y if j & 2 else y, 1 - c if j & 1 else c)


_HBM = pl.BlockSpec(memory_space=pltpu.HBM)
_SEM = pl.BlockSpec(memory_space=pltpu.SEMAPHORE)
_EFFECT = pltpu.SideEffectType.DATAFLOW_SIDE_EFFECTING


def _remote_copy(src_ref, land_ref, slot, per_slot, pslot, sems, u, j, peer):
    return pltpu.make_async_remote_copy(
        src_ref=src_ref.at[pslot] if per_slot else src_ref, dst_ref=land_ref.at[slot],
        send_sem=sems[0].at[u * (N_DEV - 1) + j - 1], recv_sem=sems[1].at[u * (N_DEV - 1) + j - 1],
        device_id=peer, device_id_type=pl.DeviceIdType.MESH)


def _exchange_start(name, srcs, lands, per_slot):
    n = len(srcs)

    def body(*refs):
        src_refs, land_refs, sems, token = refs[:n], refs[n:2 * n], refs[2 * n:2 * n + 2], refs[-1]
        place = _mesh_place()
        me = _slot(*place)
        for u in range(n):
            for j in range(1, N_DEV):
                peer = _peer(place, j)
                _remote_copy(src_refs[u], land_refs[u], me, per_slot[u], _slot(*peer), sems, u, j, peer).start()
        token[...] = jnp.zeros_like(token)

    hbm = lambda a: pltpu.HBM(a.shape, a.dtype)
    sem = pltpu.SemaphoreType.DMA((n * (N_DEV - 1),))
    outs = pl.pallas_call(
        body, name=name,
        out_shape=(sem, sem, *[hbm(a) for a in srcs], *[hbm(a) for a in lands], jax.ShapeDtypeStruct((SUBLANES, LANES), F32)),
        in_specs=[_HBM] * (2 * n), out_specs=(_SEM, _SEM, *[_HBM] * (2 * n), pl.BlockSpec(memory_space=pltpu.VMEM)),
        input_output_aliases={i: 2 + i for i in range(2 * n)},
        compiler_params=pltpu.CompilerParams(has_side_effects=_EFFECT),
    )(*[pltpu.with_memory_space_constraint(a, pltpu.HBM) for a in list(srcs) + list(lands)])
    return (outs[0], outs[1]), list(outs[2:2 + n]), list(outs[2 + n:2 + 2 * n]), outs[-1]


def _exchange_wait(name, sems, srcs, lands, units, per_slot, after):
    m = len(units)

    def body(*refs):
        src_refs, land_refs, sem_refs = refs[:m], refs[m:2 * m], refs[2 * m:2 * m + 2]
        place = _mesh_place()
        for i, u in enumerate(units):
            for j in range(1, N_DEV):
                peer = _peer(place, j)
                pslot = _slot(*peer)
                cp = _remote_copy(src_refs[i], land_refs[i], pslot, per_slot[u], pslot, sem_refs, u, j, peer)
                cp.wait_send()
                cp.wait_recv()

    hbm = lambda a: pltpu.HBM(a.shape, a.dtype)
    outs = pl.pallas_call(
        body, name=name, out_shape=tuple(hbm(a) for a in list(srcs) + list(lands)),
        in_specs=[_HBM] * (2 * m) + [_SEM, _SEM, pl.BlockSpec(memory_space=pl.ANY)], out_specs=tuple([_HBM] * (2 * m)),
        input_output_aliases={i: i for i in range(2 * m)},
        compiler_params=pltpu.CompilerParams(has_side_effects=_EFFECT),
    )(*srcs, *lands, *sems, after)
    return list(outs[m:])


def _adam_update(g, w, m, v):
    m = ADAM_B1 * m + (1.0 - ADAM_B1) * g
    v = ADAM_B2 * v + (1.0 - ADAM_B2) * jnp.square(g)
    m_hat = m / (1.0 - ADAM_B1 ** ADAM_STEP)
    v_hat = v / (1.0 - ADAM_B2 ** ADAM_STEP)
    return -ADAM_LR * (m_hat / (jnp.sqrt(v_hat) + ADAM_EPS) + ADAM_WD * w), m, v


def _adamw(recvs, w, m, v, *, tr, name):
    L, R, C = w.shape
    rp = max(tr, SUBLANES * (4 // jnp.dtype(recvs[0].dtype).itemsize))
    Cp = recvs[0].shape[2]

    def body(*refs):
        r_refs, (w_ref, m_ref, v_ref, g_ref, d_ref, nm_ref, nv_ref) = refs[:L], refs[L:]
        for l in range(L):
            @pl.when(pl.program_id(0) == l)
            def _(r_ref=r_refs[l]):
                g = r_ref[0, :tr, :C].astype(F32)
                for s in range(1, N_DEV):
                    g = g + r_ref[s, :tr, :C].astype(F32)
                d, nm, nv = _adam_update(g, w_ref[0], m_ref[0], v_ref[0])
                g_ref[0], d_ref[0], nm_ref[0], nv_ref[0] = g, d, nm, nv

    blk = pl.BlockSpec((1, tr, C), lambda l, i: (l, i, 0))
    r_specs = [pl.BlockSpec((N_DEV, rp, Cp), lambda l, i, k=k: (0, jnp.where(l == k, i, 0), 0)) for k in range(L)]
    est = 2 * _nbytes((N_DEV, rp, Cp), recvs[0].dtype) + 8 * _nbytes((tr, Cp), F32)
    return pl.pallas_call(
        body, name=name, grid=(L, R // tr),
        in_specs=r_specs + [blk] * 3, out_specs=[blk] * 4,
        out_shape=[jax.ShapeDtypeStruct((L, R, C), F32)] * 4,
        compiler_params=_cparams(est, ("arbitrary", "arbitrary")),
    )(*recvs, w, m, v)


def _adamw_small(recv, w, m, v):
    def body(r_ref, w_ref, m_ref, v_ref, g_ref, d_ref, nm_ref, nv_ref):
        g = r_ref[0]
        for s in range(1, N_DEV):
            g = g + r_ref[s]
        g_ref[...] = g
        d_ref[...], nm_ref[...], nv_ref[...] = _adam_update(g, w_ref[...], m_ref[...], v_ref[...])

    vm = pl.BlockSpec(memory_space=pltpu.VMEM)
    return pl.pallas_call(
        body, name="adamw_small", in_specs=[vm] * 4, out_specs=[vm] * 4,
        out_shape=[jax.ShapeDtypeStruct((SMALL_ROWS, LANES), F32)] * 4,
        compiler_params=_cparams(20 * _nbytes((SMALL_ROWS, LANES), F32)),
    )(recv, w, m, v)


def kernel(x, w_in, conv_w, a_log, dt_bias, o_norm_w, sgu_ln_g, sgu_ln_b, w_s, b_s, w_pa, w_pb, w_o, ln1_g, ln1_b, w_ffn_gate, w_ffn_up, w_ffn_down, ln2_g, ln2_b, loss_target, m_w_in, m_conv_w, m_a_log, m_dt_bias, m_o_norm_w, m_sgu_ln_g, m_sgu_ln_b, m_w_s, m_b_s, m_w_pa, m_w_pb, m_w_o, m_ln1_g, m_ln1_b, m_w_ffn_gate, m_w_ffn_up, m_w_ffn_down, m_ln2_g, m_ln2_b, v_w_in, v_conv_w, v_a_log, v_dt_bias, v_o_norm_w, v_sgu_ln_g, v_sgu_ln_b, v_w_s, v_b_s, v_w_pa, v_w_pb, v_w_o, v_ln1_g, v_ln1_b, v_w_ffn_gate, v_w_ffn_up, v_w_ffn_down, v_ln2_g, v_ln2_b):
    given = dict(locals())
    P = {n: given[n] for n in WEIGHT_NAMES}
    M = {n: given["m_" + n] for n in WEIGHT_NAMES}
    V = {n: given["v_" + n] for n in WEIGHT_NAMES}

    me = _slot(*_mesh_place())

    wire = _wire_blocks(P)
    units = [(n, l) for l in range(DEPTH) for n in EARLY + LATE]
    srcs = [wire[n][l] for n, l in units]
    lands = [lax.dynamic_update_slice(lax.empty((N_DEV,) + s.shape, s.dtype), s[None], (me,) + (0,) * s.ndim) for s in srcs]
    whole = [False] * len(units)
    g_sems, g_srcs, g_lands, g_token = _exchange_start("gather_start", srcs, lands, whole)

    def gathered(name, names, l, after):
        idx = [units.index((n, l)) for n in names]
        got = _exchange_wait(name, g_sems, [g_srcs[i] for i in idx], [g_lands[i] for i in idx], idx, whole, after)
        return dict(zip(names, got))

    def layer(l):
        def weights(x_in):
            after = g_token if l == 0 else x_in
            early = _early_weights(gathered(f"gather_wait_early{l}", EARLY, l, after), P, l)
            return early, lambda ya: _late_weights(gathered(f"gather_wait_late{l}", LATE, l, ya))
        return weights

    pending = {}

    def on_grads(l, g):
        slots, small = _layer_slots(g)
        srcs = slots + [small]
        lands = slots + [jnp.broadcast_to(small[None], (N_DEV,) + small.shape)]
        per_slot = [True] * len(slots) + [False]
        sems, s_thru, l_thru, token = _exchange_start(f"exchange_start{l}", srcs, lands, per_slot)
        pending[l] = (sems, s_thru, l_thru, per_slot)
        return token[0, 0]

    loss_local, dx, _ = _local_step(x[0], loss_target[0], [layer(l) for l in range(DEPTH)], on_grads)
    loss = lax.psum(loss_local, MESH_AXES)

    recv = []
    for l in range(DEPTH):
        sems, s_thru, l_thru, per_slot = pending[l]
        got = _exchange_wait(f"exchange_wait{l}", sems, s_thru, l_thru, list(range(len(s_thru))), per_slot, dx)
        recv.append(dict(zip(WIRE + ("small",), got)))

    out = {}
    for n in WIRE:
        tr, name = ADAM_TILES[n]
        out[n] = _adamw([recv[l][n] for l in range(DEPTH)], P[n], M[n], V[n], tr=tr, name=name)
    small = [_adamw_small(recv[l]["small"], *[_small_pack({n: T[n][l] for n, _ in SMALL}) for T in (P, M, V)])
             for l in range(DEPTH)]
    for n, _ in SMALL:
        out[n] = [jnp.stack([_small_unpack(small[l][i], P)[n] for l in range(DEPTH)]) for i in range(4)]
    return (loss, dx[None], *[out[n][i] for i in range(4) for n in WEIGHT_NAMES])
```

```python
import functools
import math

import jax
import jax.numpy as jnp
from jax import lax
from jax.experimental import pallas as pl
from jax.experimental.pallas import tpu as pltpu

F32 = jnp.float32
BF = jnp.bfloat16
HIGHEST = lax.Precision.HIGHEST

D_MODEL = 1024
DEPTH = 2
N_HEADS = 8
D_HEAD = 128
CONV_K = 4
SGU_BLOCK = 128
SGU_GROUPS = 8
SGU_CHUNK = 64
FFN_HIDDEN = 2816
N_IN = 8208
N_DEV = 8
IN_BLOCK, IN_PAD = N_IN // N_DEV, 1152
FFN_BLOCK, FFN_PAD = FFN_HIDDEN // N_DEV, 384
FFN_K = N_DEV * FFN_PAD
ALPHA = (2 * DEPTH) ** 0.25
LN_EPS = 1e-5
RMS_EPS = 1e-6
ADAM_LR, ADAM_B1, ADAM_B2, ADAM_EPS, ADAM_WD, ADAM_STEP = 0.001, 0.9, 0.999, 1e-08, 0.01, 10

MESH_AXES = ("x", "y", "c")
DELTA_CHUNK = 128
DELTA_HEADS_PER_STEP = 8
LANES = 128
SUBLANES = 8
VMEM_BYTES = 64 * 1024 * 1024
HALO = SUBLANES


def _cparams(est_bytes, dims=None):
    limit = int(min(max(2 * est_bytes + (8 << 20), 32 << 20), VMEM_BYTES - (6 << 20)))
    kw = dict(vmem_limit_bytes=limit)
    if dims is not None:
        kw["dimension_semantics"] = dims
    return pltpu.CompilerParams(**kw)


def _nbytes(shape, dtype):
    return math.prod(shape) * jnp.dtype(dtype).itemsize


def _dims(kind, ndim):
    lhs, rhs = {"nn": (1, 0), "nt": (1, 1), "tn": (0, 0)}[kind]
    b = ndim - 2
    return (((lhs + b,), (rhs + b,)), (tuple(range(b)), tuple(range(b))))


def _mxu(a, b, kind):
    return lax.dot_general(a, b, _dims(kind, a.ndim), preferred_element_type=F32)


def _dot(a, b):
    return _mxu(a.astype(BF), b.astype(BF), "nn")


def _dot_nt(a, b):
    return _mxu(a.astype(BF), b.astype(BF), "nt")


def _dot_tn(a, b):
    return _mxu(a.astype(BF), b.astype(BF), "tn")


def _split(a):
    hi = a.astype(BF)
    return hi, (a - hi.astype(F32)).astype(BF)


def _dot3(a, b, kind):
    (ah, al), (bh, bl) = _split(a), _split(b)
    return _mxu(ah, bh, kind) + (_mxu(ah, bl, kind) + _mxu(al, bh, kind))


def _dotf(a, b):
    return _dot3(a, b, "nn")


def _dotf_nt(a, b):
    return _dot3(a, b, "nt")


def _dotf_tn(a, b):
    return _dot3(a, b, "tn")


def _dot01(sel, x, kind="nn"):
    s = jnp.broadcast_to(sel.astype(BF), x.shape[:-2] + sel.shape)
    h1 = x.astype(BF)
    r1 = x - h1.astype(F32)
    h2 = r1.astype(BF)
    h3 = (r1 - h2.astype(F32)).astype(BF)
    return _mxu(s, h1, kind) + (_mxu(s, h2, kind) + _mxu(s, h3, kind))


def _sigmoid(x):
    return jax.nn.sigmoid(x)


def _silu(x):
    return x * _sigmoid(x)


def _dsilu(x):
    s = _sigmoid(x)
    return s * (1.0 + x * (1.0 - s))


def _gelu(x):
    return 0.5 * x * (1.0 + lax.erf(x * 0.7071067811865476))


def _softplus(x):
    return jnp.maximum(x, 0.0) + jnp.log1p(jnp.exp(-jnp.abs(x)))


def _ln(x, g, b):
    mu = jnp.mean(x, -1, keepdims=True)
    xc = x - mu
    var = jnp.mean(xc * xc, -1, keepdims=True)
    return xc * lax.rsqrt(var + LN_EPS) * g + b


def _iota(shape, dim):
    return lax.broadcasted_iota(jnp.int32, shape, dim)


def _tile(n, pref, align):
    if n <= pref:
        return n
    t = (pref // align) * align
    while t >= align:
        if n % t == 0:
            return t
        t -= align
    raise ValueError(f"no tile for {n} (pref {pref}, align {align})")


def _bcast_rows(v, rows=SUBLANES):
    return jnp.broadcast_to(v, (rows, v.shape[-1]))


def _mm(a, b, *, mode, name, out_dtype=F32, add=None, add_scale=1.0, tm=512, tn=1024, tk=1024):
    if mode == "nn":
        (M, K), N = a.shape, b.shape[1]
    elif mode == "nt":
        (M, K), N = a.shape, b.shape[0]
    else:
        (K, M), N = a.shape, b.shape[1]
    tm = _tile(M, tm, LANES if mode == "tn" else SUBLANES * 2)
    tn = _tile(N, tn, LANES)
    tk = _tile(K, tk, LANES)
    nk = K // tk
    if mode == "nn":
        a_spec = pl.BlockSpec((tm, tk), lambda i, j, k: (i, k))
        b_spec = pl.BlockSpec((tk, tn), lambda i, j, k: (k, j))
        dot = _dot
    elif mode == "nt":
        a_spec = pl.BlockSpec((tm, tk), lambda i, j, k: (i, k))
        b_spec = pl.BlockSpec((tn, tk), lambda i, j, k: (j, k))
        dot = _dot_nt
    else:
        a_spec = pl.BlockSpec((tk, tm), lambda i, j, k: (k, i))
        b_spec = pl.BlockSpec((tk, tn), lambda i, j, k: (k, j))
        dot = _dot_tn
    o_spec = pl.BlockSpec((tm, tn), lambda i, j, k: (i, j))
    has_add = add is not None

    def body(*refs):
        if has_add:
            a_ref, b_ref, add_ref, o_ref, acc_ref = refs
        else:
            a_ref, b_ref, o_ref, acc_ref = refs
            add_ref = None
        k = pl.program_id(2)
        part = dot(a_ref[...], b_ref[...])

        def finish(total):
            if has_add:
                total = total + add_scale * add_ref[...]
            o_ref[...] = total.astype(out_dtype)

        if nk == 1:
            finish(part)
        else:
            @pl.when(k == 0)
            def _():
                acc_ref[...] = part

            @pl.when(jnp.logical_and(k > 0, k < nk - 1))
            def _():
                acc_ref[...] += part

            @pl.when(k == nk - 1)
            def _():
                finish(acc_ref[...] + part)

    in_specs = [a_spec, b_spec] + ([o_spec] if has_add else [])
    args = (a, b) + ((add,) if has_add else ())
    est = (_nbytes((tm, tk), a.dtype) + _nbytes((tk, tn), b.dtype) + 2 * _nbytes((tm, tn), F32)
           + (_nbytes((tm, tn), F32) if has_add else 0)) + 2 * _nbytes((tm, tn), F32)
    return pl.pallas_call(
        body, name=name,
        grid=(M // tm, N // tn, nk),
        in_specs=in_specs, out_specs=o_spec,
        out_shape=jax.ShapeDtypeStruct((M, N), out_dtype),
        scratch_shapes=[pltpu.VMEM((tm, tn) if nk > 1 else (SUBLANES, LANES), F32)],
        compiler_params=_cparams(est, ("parallel", "parallel", "arbitrary")),
    )(*args)


def _conv_taps(xt, halo, w_ref, first):
    halo = jnp.where(first, 0.0, halo)
    xc = jnp.concatenate([halo, xt], axis=0)
    shifted = [xt] + [pltpu.roll(xc, s, 0)[HALO:] for s in range(1, CONV_K)]
    out = shifted[0] * w_ref[CONV_K - 1:CONV_K, :]
    for s in range(1, CONV_K):
        out = out + shifted[s] * w_ref[CONV_K - 1 - s:CONV_K - s, :]
    return out, shifted


def _gates(ba, arow, dtrow):
    lane = _iota(ba.shape, 1)
    beta = _sigmoid(ba)
    g = -jnp.exp(arow) * _softplus(ba + dtrow)
    return jnp.where(lane < N_HEADS, beta, jnp.where(lane < 2 * N_HEADS, g, 0.0))


def _l2n(x):
    return x * lax.rsqrt(jnp.sum(x * x, -1, keepdims=True) + RMS_EPS)


def _qkv_prep(proj, ba, convw, arow, dtrow, *, tm=256):
    S = proj.shape[0]
    tm = _tile(S, tm, SUBLANES)
    W3 = 3 * D_MODEL
    hb = tm // HALO

    def body(xt_ref, halo_ref, ba_ref, w_ref, a_ref, dt_ref, q_ref, k_ref, v_ref, gb_ref):
        c, _ = _conv_taps(xt_ref[...], halo_ref[...], w_ref, pl.program_id(0) == 0)
        c = _silu(c)
        for h in range(N_HEADS):
            lo = h * D_HEAD
            q_ref[:, lo:lo + D_HEAD] = _l2n(c[:, lo:lo + D_HEAD])
            k_ref[:, lo:lo + D_HEAD] = _l2n(c[:, D_MODEL + lo:D_MODEL + lo + D_HEAD])
        v_ref[...] = c[:, 2 * D_MODEL:]
        gb_ref[...] = _gates(ba_ref[...], a_ref[...], dt_ref[...])

    row = lambda w, col=0: pl.BlockSpec((tm, w), lambda i: (i, col))
    full = lambda shape: pl.BlockSpec(shape, lambda i: (0,) * len(shape))
    est = 4 * _nbytes((tm, W3), F32)
    return pl.pallas_call(
        body, name="qkv_prep", grid=(S // tm,),
        in_specs=[row(W3), pl.BlockSpec((HALO, W3), lambda i: (jnp.maximum(i * hb - 1, 0), 0)), row(LANES),
                  full((CONV_K, W3)), full((1, LANES)), full((1, LANES))],
        out_specs=[row(D_MODEL), row(D_MODEL), row(D_MODEL), row(LANES)],
        out_shape=[jax.ShapeDtypeStruct((S, D_MODEL), F32)] * 3 + [jax.ShapeDtypeStruct((S, LANES), F32)],
        compiler_params=_cparams(est, ("arbitrary",)),
    )(proj, proj, ba, convw, arow, dtrow)


def _qkv_prep_bwd(proj, ba, convw, arow, dtrow, dq, dk, dv, dgb, *, tm=256):
    S = proj.shape[0]
    tm = _tile(S, tm, SUBLANES)
    W3 = 3 * D_MODEL
    hb = tm // HALO

    def body(xt_ref, halo_ref, ba_ref, w_ref, a_ref, dt_ref, dq_ref, dk_ref, dv_ref, dgb_ref,
             dc_ref, dba_ref, dw_ref, da_ref, ddt_ref):
        i = pl.program_id(0)

        @pl.when(i == 0)
        def _():
            dw_ref[...] = jnp.zeros_like(dw_ref)
            da_ref[...] = jnp.zeros_like(da_ref)
            ddt_ref[...] = jnp.zeros_like(ddt_ref)

        c, shifted = _conv_taps(xt_ref[...], halo_ref[...], w_ref, i == 0)
        a = _silu(c)
        ds = _dsilu(c)
        for h in range(N_HEADS):
            for base, d_ref in ((0, dq_ref), (D_MODEL, dk_ref)):
                lo = base + h * D_HEAD
                _, vj = jax.vjp(_l2n, a[:, lo:lo + D_HEAD])
                (dx,) = vj(d_ref[:, h * D_HEAD:(h + 1) * D_HEAD])
                dc_ref[:, lo:lo + D_HEAD] = dx * ds[:, lo:lo + D_HEAD]
        dc_ref[:, 2 * D_MODEL:] = dv_ref[...] * ds[:, 2 * D_MODEL:]
        dc = dc_ref[...]
        for s in range(CONV_K):
            kk = CONV_K - 1 - s
            dw_ref[kk:kk + 1, :] += jnp.sum(dc * shifted[s], axis=0, keepdims=True)
        _, vj = jax.vjp(_gates, ba_ref[...], a_ref[...], dt_ref[...])
        dba, da, ddt = vj(dgb_ref[...])
        dba_ref[...] = dba.astype(BF)
        da_ref[...] += _bcast_rows(da)
        ddt_ref[...] += _bcast_rows(ddt)

    row = lambda w, col=0: pl.BlockSpec((tm, w), lambda i: (i, col))
    full = lambda shape: pl.BlockSpec(shape, lambda i: (0,) * len(shape))
    est = 8 * _nbytes((tm, W3), F32)
    return pl.pallas_call(
        body, name="qkv_prep_bwd", grid=(S // tm,),
        in_specs=[row(W3), pl.BlockSpec((HALO, W3), lambda i: (jnp.maximum(i * hb - 1, 0), 0)), row(LANES),
                  full((CONV_K, W3)), full((1, LANES)), full((1, LANES)),
                  row(D_MODEL), row(D_MODEL), row(D_MODEL), row(LANES)],
        out_specs=[row(W3), row(LANES), full((SUBLANES, W3)), full((SUBLANES, LANES)), full((SUBLANES, LANES))],
        out_shape=[jax.ShapeDtypeStruct((S, W3), F32), jax.ShapeDtypeStruct((S, LANES), BF),
                   jax.ShapeDtypeStruct((SUBLANES, W3), F32), jax.ShapeDtypeStruct((SUBLANES, LANES), F32),
                   jax.ShapeDtypeStruct((SUBLANES, LANES), F32)],
        compiler_params=_cparams(est, ("arbitrary",)),
    )(proj, proj, ba, convw, arow, dtrow, dq, dk, dv, dgb)


def _conv_bwd(dc, convw, *, tm=256):
    S, W3 = dc.shape
    tm = _tile(S, tm, SUBLANES * 2)
    hb = tm // HALO
    nt = S // tm

    def body(dc_ref, nxt_ref, w_ref, o_ref):
        last = pl.program_id(0) == nt - 1
        nxt = jnp.where(last, 0.0, nxt_ref[...])
        cur = dc_ref[...]
        xc = jnp.concatenate([cur, nxt], axis=0)
        out = cur * w_ref[CONV_K - 1:CONV_K, :]
        for s in range(1, CONV_K):
            out = out + pltpu.roll(xc, tm + HALO - s, 0)[:tm] * w_ref[CONV_K - 1 - s:CONV_K - s, :]
        o_ref[...] = out.astype(BF)

    est = 5 * _nbytes((tm, W3), F32)
    return pl.pallas_call(
        body, name="conv_bwd", grid=(nt,),
        in_specs=[pl.BlockSpec((tm, W3), lambda i: (i, 0)),
                  pl.BlockSpec((HALO, W3), lambda i: (jnp.minimum((i + 1) * hb, S // HALO - 1), 0)),
                  pl.BlockSpec((CONV_K, W3), lambda i: (0, 0))],
        out_specs=pl.BlockSpec((tm, W3), lambda i: (i, 0)),
        out_shape=jax.ShapeDtypeStruct((S, W3), BF),
        compiler_params=_cparams(est, ("parallel",)),
    )(dc, dc, convw)


def _inv_unit_lower(A):
    C = A.shape[-1]
    row, col = _iota((C, C), 0), _iota((C, C), 1)
    T = jnp.broadcast_to(jnp.where(row == col, 1.0, 0.0).astype(F32), A.shape)
    b = 1
    while b < C:
        hi = ~(2 * b - 1)
        off = ((row & hi) == (col & hi)) & ((row & b) != 0) & ((col & b) == 0)
        T = T - _dotf(_dotf(T, jnp.where(off, A, 0.0)), T)
        b *= 2
    return T


def _delta_common(q, k, g, beta):
    C = q.shape[-2]
    row, col = _iota((C, C), 0), _iota((C, C), 1)
    tril = row >= col
    qs = q * (D_HEAD ** -0.5)
    gcb = _dot01(jnp.where(tril, 1.0, 0.0), jnp.broadcast_to(g, g.shape[:-1] + (LANES,)))
    gc = gcb[..., :1]
    Dm = jnp.exp(jnp.where(tril, gc - jnp.swapaxes(gcb, -1, -2), -1e30))
    eg = jnp.exp(gc)
    gl = jnp.sum(jnp.where(_iota((C, 1), 0) == C - 1, gc, 0.0), axis=(-2, -1), keepdims=True)
    el = jnp.exp(gl)
    er = jnp.exp(gl - gc)
    kb = k * beta
    KK = _dot_nt(kb, k)
    QK = _dot_nt(qs, k)
    return dict(row=row, col=col, tril=tril, qs=qs, gc=gc, Dm=Dm, eg=eg, el=el, er=er, kb=kb, KK=KK, QK=QK)


def _delta_chunk_fwd(S0, q, k, v, g, beta, T=None):
    m = _delta_common(q, k, g, beta)
    if T is None:
        T = _inv_unit_lower(jnp.where(m["row"] > m["col"], m["KK"] * m["Dm"], 0.0))
    u = _dotf(T, v * beta)
    w = _dotf(T, m["kb"] * m["eg"])
    vn = u - _dot(w, S0)
    o = _dot(m["qs"] * m["eg"], S0) + _dot(m["QK"] * m["Dm"], vn)
    S1 = S0 * m["el"] + _dot_tn(k * m["er"], vn)
    return o, S1, T


def _delta_chunk_bwd(S0, q, k, v, g, beta, T, do, dS1):
    m = _delta_common(q, k, g, beta)
    C = q.shape[-2]
    qs, Dm, eg, el, er, kb, KK, QK = (m[n] for n in ("qs", "Dm", "eg", "el", "er", "kb", "KK", "QK"))
    strict = m["row"] > m["col"]
    total = lambda x: jnp.sum(x, axis=(-2, -1), keepdims=True)
    ru, rw = v * beta, kb * eg
    u = _dotf(T, ru)
    w = _dotf(T, rw)
    vn = u - _dot(w, S0)
    P = QK * Dm
    qg = qs * eg
    kr = k * er

    dvn = _dot_tn(P, do) + _dot(kr, dS1)
    dS0 = dS1 * el + _dot_tn(qg, do) - _dot_tn(w, dvn)
    d_el = total(dS1 * S0)
    dqg = _dot_nt(do, S0)
    dqs = dqg * eg
    deg = jnp.sum(dqg * qs, -1, keepdims=True)
    dP = _dot_nt(do, vn)
    dPD = dP * Dm
    dqs = dqs + _dot(dPD, k)
    dk = _dot_tn(dPD, qs)
    dD = dP * QK
    dkr = _dot_nt(vn, dS1)
    dk = dk + dkr * er
    der = jnp.sum(dkr * k, -1, keepdims=True)
    dw = -_dot_nt(dvn, S0)
    dru = _dotf_tn(T, dvn)
    drw = _dotf_tn(T, dw)
    dT = _dotf_nt(dvn, ru) + _dotf_nt(dw, rw)
    dA = -_dotf_nt(_dotf_tn(T, dT), T)
    dAm = jnp.where(strict, dA, 0.0)
    dKK = dAm * Dm
    dkb = _dot(dKK, k)
    dk = dk + _dot_tn(dKK, kb)
    dD = dD + dAm * KK
    dv = dru * beta
    dbeta = jnp.sum(dru * v, -1, keepdims=True)
    dkb = dkb + drw * eg
    deg = deg + jnp.sum(drw * kb, -1, keepdims=True)
    dk = dk + dkb * beta
    dbeta = dbeta + jnp.sum(dkb * k, -1, keepdims=True)
    E = dD * Dm
    dgc = jnp.sum(E, -1, keepdims=True) - jnp.sum(jnp.swapaxes(E, -1, -2), -1, keepdims=True)
    dgc = dgc + deg * eg - der * er
    dgl = total(der * er) + d_el * el
    dgc = dgc + jnp.where(_iota((C, 1), 0) == C - 1, dgl, 0.0)
    triu = jnp.where(m["row"] <= m["col"], 1.0, 0.0)
    dg = _dot01(triu, jnp.broadcast_to(dgc, dgc.shape[:-1] + (LANES,)))[..., :1]
    dq = dqs * (D_HEAD ** -0.5)
    return dq, dk, dv, dg, dbeta, dS0


def _head_cols(gb, h):
    lane = _iota(gb.shape, 1)
    beta = jnp.sum(jnp.where(lane == h, gb, 0.0), -1, keepdims=True)
    g = jnp.sum(jnp.where(lane == N_HEADS + h, gb, 0.0), -1, keepdims=True)
    return g, beta


def _delta_fwd(q, k, v, gb):
    S = q.shape[0]
    C = DELTA_CHUNK
    N = S // C

    HB = DELTA_HEADS_PER_STEP

    def body(q_ref, k_ref, v_ref, gb_ref, o_ref, st_ref, t_ref, s_scr):
        n, hb = pl.program_id(0), pl.program_id(1)
        gb = gb_ref[...]

        @pl.when(n == 0)
        def _():
            for hh in range(HB):
                s_scr[hb * HB + hh] = jnp.zeros((D_HEAD, D_HEAD), F32)

        heads = [hb * HB + hh for hh in range(HB)]
        cols = [slice(hh * D_HEAD, (hh + 1) * D_HEAD) for hh in range(HB)]
        per_head = lambda ref: jnp.stack([ref[:, c] for c in cols])
        g, beta = (jnp.stack(t) for t in zip(*[_head_cols(gb, h) for h in heads]))
        S0 = jnp.stack([s_scr[h] for h in heads])
        o, S1, T = _delta_chunk_fwd(S0, per_head(q_ref), per_head(k_ref), per_head(v_ref), g, beta)
        for hh in range(HB):
            st_ref[hh, 0] = S0[hh]
            t_ref[hh, 0] = T[hh]
            o_ref[:, cols[hh]] = o[hh]
            s_scr[heads[hh]] = S1[hh]

    hd = pl.BlockSpec((C, HB * D_HEAD), lambda n, h: (n, h))
    mat = pl.BlockSpec((HB, 1, D_HEAD, D_HEAD), lambda n, h: (h, n, 0, 0))
    est = 40 * HB * _nbytes((C, D_HEAD), F32)
    return pl.pallas_call(
        body, name="delta_fwd", grid=(N, N_HEADS // HB),
        in_specs=[hd, hd, hd, pl.BlockSpec((C, LANES), lambda n, h: (n, 0))],
        out_specs=[hd, mat, mat],
        out_shape=[jax.ShapeDtypeStruct((S, N_HEADS * D_HEAD), F32),
                   jax.ShapeDtypeStruct((N_HEADS, N, D_HEAD, D_HEAD), F32),
                   jax.ShapeDtypeStruct((N_HEADS, N, C, C), F32)],
        scratch_shapes=[pltpu.VMEM((N_HEADS, D_HEAD, D_HEAD), F32)],
        compiler_params=_cparams(est, ("arbitrary", "arbitrary")),
    )(q, k, v, gb)


def _delta_bwd(q, k, v, gb, st, tinv, do):
    S = q.shape[0]
    C = DELTA_CHUNK
    N = S // C

    HB = DELTA_HEADS_PER_STEP

    def body(q_ref, k_ref, v_ref, gb_ref, st_ref, t_ref, do_ref, dq_ref, dk_ref, dv_ref, dgb_ref, ds_scr):
        n, hb = pl.program_id(0), pl.program_id(1)
        gb = gb_ref[...]
        lane = _iota((C, LANES), 1)
        dgb = jnp.zeros((C, LANES), F32)

        @pl.when(n == 0)
        def _():
            for hh in range(HB):
                ds_scr[hb * HB + hh] = jnp.zeros((D_HEAD, D_HEAD), F32)

        heads = [hb * HB + hh for hh in range(HB)]
        cols = [slice(hh * D_HEAD, (hh + 1) * D_HEAD) for hh in range(HB)]
        per_head = lambda ref: jnp.stack([ref[:, c] for c in cols])
        g, beta = (jnp.stack(t) for t in zip(*[_head_cols(gb, h) for h in heads]))
        dS1 = jnp.stack([ds_scr[h] for h in heads])
        dq, dk, dv, dg, dbeta, dS0 = _delta_chunk_bwd(
            st_ref[:, 0], per_head(q_ref), per_head(k_ref), per_head(v_ref), g, beta, t_ref[:, 0], per_head(do_ref), dS1)
        for hh, h in enumerate(heads):
            dq_ref[:, cols[hh]] = dq[hh]
            dk_ref[:, cols[hh]] = dk[hh]
            dv_ref[:, cols[hh]] = dv[hh]
            dgb = dgb + jnp.where(lane == h, dbeta[hh], 0.0) + jnp.where(lane == N_HEADS + h, dg[hh], 0.0)
            ds_scr[h] = dS0[hh]

        @pl.when(hb == 0)
        def _():
            dgb_ref[...] = dgb

        @pl.when(hb > 0)
        def _():
            dgb_ref[...] += dgb

    hd = pl.BlockSpec((C, HB * D_HEAD), lambda n, h: (N - 1 - n, h))
    mat = pl.BlockSpec((HB, 1, D_HEAD, D_HEAD), lambda n, h: (h, N - 1 - n, 0, 0))
    gbs = pl.BlockSpec((C, LANES), lambda n, h: (N - 1 - n, 0))
    est = 60 * HB * _nbytes((C, D_HEAD), F32)
    return pl.pallas_call(
        body, name="delta_bwd", grid=(N, N_HEADS // HB),
        in_specs=[hd, hd, hd, gbs, mat, mat, hd],
        out_specs=[hd, hd, hd, gbs],
        out_shape=[jax.ShapeDtypeStruct((S, N_HEADS * D_HEAD), F32)] * 3 + [jax.ShapeDtypeStruct((S, LANES), F32)],
        scratch_shapes=[pltpu.VMEM((N_HEADS, D_HEAD, D_HEAD), F32)],
        compiler_params=_cparams(est, ("arbitrary", "arbitrary")),
    )(q, k, v, gb, st, tinv, do)


def _ya_head(o, z, onw):
    return o * lax.rsqrt(jnp.mean(o * o, -1, keepdims=True) + RMS_EPS) * onw * _silu(z)


def _sgu_pre(u, vg, sg, sb):
    return _gelu(u), _ln(_gelu(vg), sg, sb)


def _chunk_causal(shape, di, dj):
    sh = jnp.int32(int(math.log2(SGU_CHUNK)))
    return lax.shift_right_logical(_iota(shape, di), sh) >= lax.shift_right_logical(_iota(shape, dj), sh)


def _ws_masked(ws):
    return jnp.where(_chunk_causal(ws.shape, 1, 2), ws, 0.0)


def _mix_prep(o, proj, onw, sg, sb, ws, bst, *, tm=256):
    S = o.shape[0]
    tm = _tile(S, tm, SGU_BLOCK)

    def body(o_ref, z_ref, u_ref, vg_ref, onw_ref, sg_ref, sb_ref, ws_ref, bst_ref, ya_ref, yb_ref):
        onw = onw_ref[...]
        for h in range(N_HEADS):
            sl = slice(h * D_HEAD, (h + 1) * D_HEAD)
            ya_ref[:, sl] = _ya_head(o_ref[:, sl], z_ref[:, sl], onw).astype(BF)
        ua, vl = _sgu_pre(u_ref[...], vg_ref[...], sg_ref[...], sb_ref[...])
        wsm = _ws_masked(ws_ref[...])
        bst = bst_ref[...]
        for blk in range(tm // SGU_BLOCK):
            rs = slice(blk * SGU_BLOCK, (blk + 1) * SGU_BLOCK)
            for gi in range(SGU_GROUPS):
                cs = slice(gi * D_HEAD, (gi + 1) * D_HEAD)
                sp = _dot(wsm[gi], vl[rs, cs]) + bst[:, gi:gi + 1]
                yb_ref[rs, cs] = (ua[rs, cs] * sp).astype(BF)

    blk = lambda col: pl.BlockSpec((tm, D_MODEL), lambda i: (i, col))
    full = lambda shape: pl.BlockSpec(shape, lambda i: (0,) * len(shape))
    est = 10 * _nbytes((tm, D_MODEL), F32)
    return pl.pallas_call(
        body, name="mix_prep", grid=(S // tm,),
        in_specs=[blk(0), blk(3), blk(4), blk(5), full((1, D_HEAD)), full((1, D_MODEL)), full((1, D_MODEL)),
                  full((SGU_GROUPS, SGU_BLOCK, SGU_BLOCK)), full((SGU_BLOCK, LANES))],
        out_specs=[blk(0), blk(0)],
        out_shape=[jax.ShapeDtypeStruct((S, D_MODEL), BF)] * 2,
        compiler_params=_cparams(est, ("parallel",)),
    )(o, proj, proj, proj, onw, sg, sb, ws, bst)


def _mix_prep_bwd(o, proj, onw, sg, sb, ws, bst, dya, dyb, *, tm=256):
    S = o.shape[0]
    tm = _tile(S, tm, SGU_BLOCK)

    def body(o_ref, z_ref, u_ref, vg_ref, onw_ref, sg_ref, sb_ref, ws_ref, bst_ref, dya_ref, dyb_ref,
             do_ref, dz_ref, du_ref, dvg_ref, donw_ref, dsg_ref, dsb_ref, dws_ref, dbst_ref, dvl_scr, dua_scr):
        @pl.when(pl.program_id(0) == 0)
        def _():
            for r in (donw_ref, dsg_ref, dsb_ref, dws_ref, dbst_ref):
                r[...] = jnp.zeros_like(r)

        onw = onw_ref[...]
        donw = jnp.zeros((1, D_HEAD), F32)
        for h in range(N_HEADS):
            sl = slice(h * D_HEAD, (h + 1) * D_HEAD)
            _, vj = jax.vjp(_ya_head, o_ref[:, sl], z_ref[:, sl], onw)
            do_h, dz_h, donw_h = vj(dya_ref[:, sl])
            do_ref[:, sl] = do_h
            dz_ref[:, sl] = dz_h.astype(BF)
            donw = donw + donw_h
        donw_ref[...] += _bcast_rows(donw)

        (ua, vl), vj = jax.vjp(_sgu_pre, u_ref[...], vg_ref[...], sg_ref[...], sb_ref[...])
        wsm = _ws_masked(ws_ref[...])
        bst = bst_ref[...]
        lane = _iota((SGU_BLOCK, LANES), 1)
        dbst = jnp.zeros((SGU_BLOCK, LANES), F32)
        cmask = _chunk_causal((SGU_BLOCK, SGU_BLOCK), 0, 1)
        for gi in range(SGU_GROUPS):
            cs = slice(gi * D_HEAD, (gi + 1) * D_HEAD)
            wg = wsm[gi]
            wgt = jnp.transpose(wg)
            dwg = jnp.zeros((SGU_BLOCK, SGU_BLOCK), F32)
            for blk in range(tm // SGU_BLOCK):
                rs = slice(blk * SGU_BLOCK, (blk + 1) * SGU_BLOCK)
                sp = _dot(wg, vl[rs, cs]) + bst[:, gi:gi + 1]
                dyb = dyb_ref[rs, cs]
                dsp = dyb * ua[rs, cs]
                dua_scr[rs, cs] = dyb * sp
                dvl_scr[rs, cs] = _dot(wgt, dsp)
                dwg = dwg + _dot_nt(dsp, vl[rs, cs])
                dbst = dbst + jnp.where(lane == gi, jnp.sum(dsp, -1, keepdims=True), 0.0)
            dws_ref[gi] += jnp.where(cmask, dwg, 0.0)
        dbst_ref[...] += dbst
        du, dvg, dsg, dsb = vj((dua_scr[...], dvl_scr[...]))
        du_ref[...] = du.astype(BF)
        dvg_ref[...] = dvg.astype(BF)
        dsg_ref[...] += _bcast_rows(dsg)
        dsb_ref[...] += _bcast_rows(dsb)

    blk = lambda col: pl.BlockSpec((tm, D_MODEL), lambda i: (i, col))
    full = lambda shape: pl.BlockSpec(shape, lambda i: (0,) * len(shape))
    est = 16 * _nbytes((tm, D_MODEL), F32)
    outs = pl.pallas_call(
        body, name="mix_prep_bwd", grid=(S // tm,),
        in_specs=[blk(0), blk(3), blk(4), blk(5), full((1, D_HEAD)), full((1, D_MODEL)), full((1, D_MODEL)),
                  full((SGU_GROUPS, SGU_BLOCK, SGU_BLOCK)), full((SGU_BLOCK, LANES)), blk(0), blk(0)],
        out_specs=[blk(0)] * 4 + [full((SUBLANES, D_HEAD)), full((SUBLANES, D_MODEL)), full((SUBLANES, D_MODEL)),
                                  full((SGU_GROUPS, SGU_BLOCK, SGU_BLOCK)), full((SGU_BLOCK, LANES))],
        out_shape=[jax.ShapeDtypeStruct((S, D_MODEL), F32)] + [jax.ShapeDtypeStruct((S, D_MODEL), BF)] * 3
                  + [jax.ShapeDtypeStruct((SUBLANES, D_HEAD), F32), jax.ShapeDtypeStruct((SUBLANES, D_MODEL), F32),
                     jax.ShapeDtypeStruct((SUBLANES, D_MODEL), F32),
                     jax.ShapeDtypeStruct((SGU_GROUPS, SGU_BLOCK, SGU_BLOCK), F32),
                     jax.ShapeDtypeStruct((SGU_BLOCK, LANES), F32)],
        scratch_shapes=[pltpu.VMEM((tm, D_MODEL), F32)] * 2,
        compiler_params=_cparams(est, ("arbitrary",)),
    )(o, proj, proj, proj, onw, sg, sb, ws, bst, dya, dyb)
    return outs


def _gate_merge(pa, pb, proj, *, tm=512):
    S = pa.shape[0]
    tm = _tile(S, tm, SUBLANES * 2)

    def body(pa_ref, pb_ref, ga_ref, gb_ref, m_ref):
        m_ref[...] = (_sigmoid(ga_ref[...]) * pa_ref[...] + _sigmoid(gb_ref[...]) * pb_ref[...]).astype(BF)

    blk = lambda col: pl.BlockSpec((tm, D_MODEL), lambda i: (i, col))
    return pl.pallas_call(
        body, name="gate_merge", grid=(S // tm,),
        in_specs=[blk(0), blk(0), blk(6), blk(7)], out_specs=blk(0),
        out_shape=jax.ShapeDtypeStruct((S, D_MODEL), BF),
        compiler_params=_cparams(6 * _nbytes((tm, D_MODEL), F32), ("parallel",)),
    )(pa, pb, proj, proj)


def _gate_merge_bwd(pa, pb, proj, dm, *, tm=512):
    S = pa.shape[0]
    tm = _tile(S, tm, SUBLANES * 2)

    def body(pa_ref, pb_ref, ga_ref, gb_ref, dm_ref, dpa_ref, dpb_ref, dga_ref, dgb_ref):
        dm = dm_ref[...]
        sa, sb = _sigmoid(ga_ref[...]), _sigmoid(gb_ref[...])
        dpa_ref[...] = (dm * sa).astype(BF)
        dpb_ref[...] = (dm * sb).astype(BF)
        dga_ref[...] = (dm * pa_ref[...] * sa * (1.0 - sa)).astype(BF)
        dgb_ref[...] = (dm * pb_ref[...] * sb * (1.0 - sb)).astype(BF)

    blk = lambda col: pl.BlockSpec((tm, D_MODEL), lambda i: (i, col))
    return pl.pallas_call(
        body, name="gate_merge_bwd", grid=(S // tm,),
        in_specs=[blk(0), blk(0), blk(6), blk(7), blk(0)], out_specs=[blk(0)] * 4,
        out_shape=[jax.ShapeDtypeStruct((S, D_MODEL), BF)] * 4,
        compiler_params=_cparams(10 * _nbytes((tm, D_MODEL), F32), ("parallel",)),
    )(pa, pb, proj, proj, dm)


def _swiglu_act(hgu, *, tm=256):
    S = hgu.shape[0]
    tm = _tile(S, tm, SUBLANES * 2)

    def body(hg_ref, hu_ref, h_ref):
        h_ref[...] = (_silu(hg_ref[...]) * hu_ref[...]).astype(BF)

    blk = lambda col: pl.BlockSpec((tm, FFN_K), lambda i: (i, col))
    return pl.pallas_call(
        body, name="swiglu_act", grid=(S // tm,),
        in_specs=[blk(0), blk(1)], out_specs=blk(0),
        out_shape=jax.ShapeDtypeStruct((S, FFN_K), BF),
        compiler_params=_cparams(5 * _nbytes((tm, FFN_K), F32), ("parallel",)),
    )(hgu, hgu)


def _swiglu_bwd(hgu, dh, *, tm=256):
    S = hgu.shape[0]
    tm = _tile(S, tm, SUBLANES * 2)

    def body(hg_ref, hu_ref, dh_ref, d_ref):
        hg, dh = hg_ref[...], dh_ref[...]
        d_ref[:, :FFN_K] = (dh * hu_ref[...] * _dsilu(hg)).astype(BF)
        d_ref[:, FFN_K:] = (dh * _silu(hg)).astype(BF)

    blk = lambda col: pl.BlockSpec((tm, FFN_K), lambda i: (i, col))
    return pl.pallas_call(
        body, name="swiglu_bwd", grid=(S // tm,),
        in_specs=[blk(0), blk(1), blk(0)], out_specs=pl.BlockSpec((tm, 2 * FFN_K), lambda i: (i, 0)),
        out_shape=jax.ShapeDtypeStruct((S, 2 * FFN_K), BF),
        compiler_params=_cparams(8 * _nbytes((tm, FFN_K), F32), ("parallel",)),
    )(hgu, hgu, dh)


def _resid_ln(x, r, g, b, *, tm=512):
    S = x.shape[0]
    tm = _tile(S, tm, SUBLANES)

    def body(x_ref, r_ref, g_ref, b_ref, y_ref):
        y_ref[...] = _ln(ALPHA * x_ref[...] + r_ref[...], g_ref[...], b_ref[...])

    blk = pl.BlockSpec((tm, D_MODEL), lambda i: (i, 0))
    vec = pl.BlockSpec((1, D_MODEL), lambda i: (0, 0))
    return pl.pallas_call(
        body, name="resid_ln", grid=(S // tm,),
        in_specs=[blk, blk, vec, vec], out_specs=blk,
        out_shape=jax.ShapeDtypeStruct((S, D_MODEL), F32),
        compiler_params=_cparams(6 * _nbytes((tm, D_MODEL), F32), ("parallel",)),
    )(x, r, g, b)


def _resid_ln_bwd(x, r, g, b, dy, *, tm=512):
    S = x.shape[0]
    tm = _tile(S, tm, SUBLANES)

    def body(x_ref, r_ref, g_ref, b_ref, dy_ref, dp_ref, dg_ref, db_ref):
        @pl.when(pl.program_id(0) == 0)
        def _():
            dg_ref[...] = jnp.zeros_like(dg_ref)
            db_ref[...] = jnp.zeros_like(db_ref)

        _, vj = jax.vjp(_ln, ALPHA * x_ref[...] + r_ref[...], g_ref[...], b_ref[...])
        dp, dg, db = vj(dy_ref[...])
        dp_ref[...] = dp
        dg_ref[...] += _bcast_rows(dg)
        db_ref[...] += _bcast_rows(db)

    blk = pl.BlockSpec((tm, D_MODEL), lambda i: (i, 0))
    vec = pl.BlockSpec((1, D_MODEL), lambda i: (0, 0))
    acc = pl.BlockSpec((SUBLANES, D_MODEL), lambda i: (0, 0))
    return pl.pallas_call(
        body, name="resid_ln_bwd", grid=(S // tm,),
        in_specs=[blk, blk, vec, vec, blk], out_specs=[blk, acc, acc],
        out_shape=[jax.ShapeDtypeStruct((S, D_MODEL), F32)] + [jax.ShapeDtypeStruct((SUBLANES, D_MODEL), F32)] * 2,
        compiler_params=_cparams(10 * _nbytes((tm, D_MODEL), F32), ("arbitrary",)),
    )(x, r, g, b, dy)


def _loss_head(y, tgt, *, tm=512):
    S = y.shape[0]
    tm = _tile(S, tm, SUBLANES)

    def body(y_ref, t_ref, dy_ref, l_ref):
        @pl.when(pl.program_id(0) == 0)
        def _():
            l_ref[...] = jnp.zeros_like(l_ref)

        e = y_ref[...] - t_ref[...]
        dy_ref[...] = e * (1.0 / D_MODEL)
        l_ref[...] += 0.5 * jnp.sum(jnp.mean(e * e, -1, keepdims=True), keepdims=True)

    blk = pl.BlockSpec((tm, D_MODEL), lambda i: (i, 0))
    return pl.pallas_call(
        body, name="loss_head", grid=(S // tm,),
        in_specs=[blk, blk], out_specs=[blk, pl.BlockSpec((SUBLANES, LANES), lambda i: (0, 0))],
        out_shape=[jax.ShapeDtypeStruct((S, D_MODEL), F32), jax.ShapeDtypeStruct((SUBLANES, LANES), F32)],
        compiler_params=_cparams(6 * _nbytes((tm, D_MODEL), F32), ("arbitrary",)),
    )(y, tgt)


def _layer_fwd(x, w, late):
    proj = _mm(x, w["win"], mode="nn", name="mm_in", tn=1024)
    ba = _mm(x, w["wba"], mode="nn", name="mm_in_ba", tm=1024, tn=LANES)
    qn, kn, vv, gb = _qkv_prep(proj, ba, w["convw"], w["arow"], w["dtrow"])
    o, st, tinv = _delta_fwd(qn, kn, vv, gb)
    ya, yb = _mix_prep(o, proj, w["onw"], w["sg"], w["sb"], w["ws"], w["bst"])
    w = {**w, **late(ya)}
    pa = _mm(ya, w["wpa"], mode="nn", name="mm_sq")
    pb = _mm(yb, w["wpb"], mode="nn", name="mm_sq")
    m = _gate_merge(pa, pb, proj)
    mix = _mm(m, w["wo"], mode="nn", name="mm_sq")
    x1 = _resid_ln(x, mix, w["ln1g"], w["ln1b"])
    hgu = _mm(x1, w["wgu"], mode="nn", name="mm_gu", tn=1536)
    h = _swiglu_act(hgu)
    ffn = _mm(h, w["wd"], mode="nn", name="mm_down", tk=FFN_K)
    x2 = _resid_ln(x1, ffn, w["ln2g"], w["ln2b"])
    saved = dict(x=x, proj=proj, ba=ba, qn=qn, kn=kn, vv=vv, gb=gb, o=o, st=st, tinv=tinv, ya=ya, yb=yb,
                 pa=pa, pb=pb, m=m, mix=mix, x1=x1, hgu=hgu, h=h, ffn=ffn)
    return x2, saved, w


def _layer_bwd(dx2, w, s, on_part=None):
    g = {}
    started = lambda part: on_part(part, g) if on_part is not None else None
    after = lambda v, token: v if token is None else v + token.astype(v.dtype)
    dpre2, g["ln2g"], g["ln2b"] = _resid_ln_bwd(s["x1"], s["ffn"], w["ln2g"], w["ln2b"], dx2)
    dh = _mm(dpre2, w["wd"], mode="nt", name="mm_nt_down", tn=1536)
    g["wd"] = _mm(s["h"], dpre2, mode="tn", name="mm_tn_down", tm=1536, tk=512, out_dtype=BF)
    dhgu = _swiglu_bwd(s["hgu"], dh)
    dx1 = _mm(dhgu, w["wgu"], mode="nt", name="mm_nt_gu", add=dpre2, add_scale=ALPHA, tk=1536)
    g["wgu"] = _mm(s["x1"], dhgu, mode="tn", name="mm_tn_gu", tm=1024, tn=1536, tk=512, out_dtype=BF)
    dpre1, g["ln1g"], g["ln1b"] = _resid_ln_bwd(s["x"], s["mix"], w["ln1g"], w["ln1b"], dx1)
    dm = _mm(dpre1, w["wo"], mode="nt", name="mm_nt_sq")
    g["wo"] = _mm(s["m"], dpre1, mode="tn", name="mm_tn_sq", tm=1024, tk=512, out_dtype=BF)
    dpa, dpb, dga, dgb_gate = _gate_merge_bwd(s["pa"], s["pb"], s["proj"], dm)
    dya = _mm(dpa, w["wpa"], mode="nt", name="mm_nt_sq")
    g["wpa"] = _mm(s["ya"], dpa, mode="tn", name="mm_tn_sq", tm=1024, tk=512, out_dtype=BF)
    dyb = _mm(dpb, w["wpb"], mode="nt", name="mm_nt_sq")
    g["wpb"] = _mm(s["yb"], dpb, mode="tn", name="mm_tn_sq", tm=1024, tk=512, out_dtype=BF)
    do, dz, du, dvg, g["onw"], g["sg"], g["sb"], g["ws"], g["bst"] = _mix_prep_bwd(
        s["o"], s["proj"], after(w["onw"], started("late")), w["sg"], w["sb"], w["ws"], w["bst"], dya, dyb)
    dqn, dkn, dvv, dgb = _delta_bwd(s["qn"], s["kn"], s["vv"], s["gb"], s["st"], s["tinv"], do)
    dc, dba, g["convw"], g["arow"], g["dtrow"] = _qkv_prep_bwd(
        s["proj"], s["ba"], w["convw"], w["arow"], w["dtrow"], dqn, dkn, dvv, dgb)
    dqkv = _conv_bwd(dc, w["convw"])
    dproj = jnp.concatenate([dqkv, dz, du, dvg, dga, dgb_gate], axis=1)
    g["win"] = _mm(s["x"], dproj, mode="tn", name="mm_tn_in", tm=1024, tn=1024, tk=512, out_dtype=BF)
    g["wba"] = _mm(s["x"], dba, mode="tn", name="mm_tn_ba", tm=1024, tn=LANES, tk=1024, out_dtype=BF)
    dx = _mm(dba, after(w["wba"], started("early")), mode="nt", name="mm_nt_ba", add=dpre1, add_scale=ALPHA, tm=1024)
    dx = _mm(dproj, w["win"], mode="nt", name="mm_nt_in", add=dx, add_scale=1.0, tk=1024)
    return dx, g


def _local_step(x, tgt, layers, on_grads=None):
    saved, weights = [], []
    for layer in layers:
        x, s, w = _layer_fwd(x, *layer(x))
        saved.append(s)
        weights.append(w)
    dy, lacc = _loss_head(x, tgt)
    grads = [None] * len(layers)
    for l in reversed(range(len(layers))):
        on_part = functools.partial(on_grads, l) if on_grads is not None else None
        dy, grads[l] = _layer_bwd(dy, weights[l], saved[l], on_part)
    return lacc[0, 0], dy, grads


_QKVZ = 4 * D_MODEL
_BA = 2 * N_HEADS


WEIGHT_NAMES = ("w_in", "conv_w", "a_log", "dt_bias", "o_norm_w", "sgu_ln_g", "sgu_ln_b", "w_s", "b_s", "w_pa", "w_pb",
                "w_o", "ln1_g", "ln1_b", "w_ffn_gate", "w_ffn_up", "w_ffn_down", "ln2_g", "ln2_b")
WIRE = ("w_in", "w_ffn_gate", "w_ffn_up", "w_ffn_down", "w_pa", "w_pb", "w_o", "conv_w")
SMALL = (("a_log", N_HEADS), ("dt_bias", N_HEADS), ("o_norm_w", D_HEAD), ("sgu_ln_g", D_MODEL), ("sgu_ln_b", D_MODEL),
         ("w_s", SGU_GROUPS * SGU_BLOCK * SGU_BLOCK), ("b_s", SGU_GROUPS * SGU_BLOCK),
         ("ln1_g", D_MODEL), ("ln1_b", D_MODEL), ("ln2_g", D_MODEL), ("ln2_b", D_MODEL))
SMALL_ROWS = -(-sum(n for _, n in SMALL) // (LANES * SUBLANES)) * SUBLANES
N_MAIN_TILES = (N_IN - _BA) // D_MODEL
ADAM_TILES = dict(w_in=(128, "adamw_in"), w_ffn_gate=(256, "adamw_ffn_cols"), w_ffn_up=(256, "adamw_ffn_cols"),
                  w_ffn_down=(32, "adamw_ffn_rows"), w_pa=(128, "adamw_sq"), w_pb=(128, "adamw_sq"), w_o=(128, "adamw_sq"),
                  conv_w=(CONV_K, "adamw_conv"))


def _pad_to(a, axis, size):
    pads = [(0, 0)] * a.ndim
    pads[axis] = (0, size - a.shape[axis])
    return jnp.pad(a, pads)


def _wire_blocks(p):
    return dict(
        w_in=_pad_to(p["w_in"].astype(BF), 2, IN_PAD),
        w_ffn_gate=_pad_to(p["w_ffn_gate"].astype(BF), 2, FFN_PAD), w_ffn_up=_pad_to(p["w_ffn_up"].astype(BF), 2, FFN_PAD),
        w_ffn_down=_pad_to(p["w_ffn_down"].astype(BF), 1, FFN_PAD),
        w_pa=p["w_pa"].astype(BF), w_pb=p["w_pb"].astype(BF), w_o=p["w_o"].astype(BF),
        conv_w=_pad_to(p["conv_w"], 1, SUBLANES),
    )


def _by_columns(blocks):
    n, r, c = blocks.shape
    return jnp.transpose(blocks, (1, 0, 2)).reshape(r, n * c)


def _to_slots(full, c):
    r = full.shape[0]
    return jnp.transpose(full.reshape(r, N_DEV, c), (1, 0, 2))


def _lane_row(v, at):
    return jnp.pad(v[None], ((0, 0), (at, LANES - at - v.shape[0])))


EARLY = ("w_in", "conv_w")
LATE = ("w_pa", "w_pb", "w_o", "w_ffn_gate", "w_ffn_up", "w_ffn_down")


def _early_weights(stacks, p, l):
    return dict(
        win=_perm_in(stacks["w_in"], D_MODEL, N_MAIN_TILES), wba=_perm_in(stacks["w_in"], LANES, 1),
        convw=_by_columns(stacks["conv_w"][:, :CONV_K]),
        arow=_lane_row(p["a_log"][l], N_HEADS), dtrow=_lane_row(p["dt_bias"][l], N_HEADS),
        onw=p["o_norm_w"][l][None], sg=p["sgu_ln_g"][l][None], sb=p["sgu_ln_b"][l][None],
        ws=p["w_s"][l], bst=_pad_to(p["b_s"][l].T, 1, LANES),
        ln1g=p["ln1_g"][l][None], ln1b=p["ln1_b"][l][None], ln2g=p["ln2_g"][l][None], ln2b=p["ln2_b"][l][None],
    )


def _late_weights(stacks):
    return dict(
        wpa=stacks["w_pa"].reshape(D_MODEL, D_MODEL), wpb=stacks["w_pb"].reshape(D_MODEL, D_MODEL),
        wo=stacks["w_o"].reshape(D_MODEL, D_MODEL),
        wgu=jnp.concatenate([_by_columns(stacks["w_ffn_gate"]), _by_columns(stacks["w_ffn_up"])], axis=1),
        wd=stacks["w_ffn_down"].reshape(FFN_K, D_MODEL),
    )


def _small_pack(parts):
    flat = jnp.concatenate([parts[n].reshape(-1) for n, _ in SMALL])
    return _pad_to(flat, 0, SMALL_ROWS * LANES).reshape(SMALL_ROWS, LANES)


def _small_unpack(rows, like):
    flat, out, off = rows.reshape(-1), {}, 0
    for n, size in SMALL:
        out[n] = flat[off:off + size].reshape(like[n].shape[1:])
        off += size
    return out


def _late_slots(g):
    slots = dict(
        w_ffn_gate=_to_slots(g["wgu"][:, :FFN_K], FFN_PAD), w_ffn_up=_to_slots(g["wgu"][:, FFN_K:], FFN_PAD),
        w_ffn_down=g["wd"].reshape(N_DEV, FFN_PAD, D_MODEL),
        w_pa=g["wpa"].reshape(N_DEV, D_MODEL // N_DEV, D_MODEL), w_pb=g["wpb"].reshape(N_DEV, D_MODEL // N_DEV, D_MODEL),
        w_o=g["wo"].reshape(N_DEV, D_MODEL // N_DEV, D_MODEL),
    )
    return [slots[n] for n in LATE]


def _early_slots(g):
    slots = [_perm_out(g["win"], g["wba"]), _pad_to(_to_slots(g["convw"][:CONV_K], 3 * D_MODEL // N_DEV), 1, SUBLANES)]
    small = _small_pack(dict(
        a_log=g["arow"][0, N_HEADS:2 * N_HEADS], dt_bias=g["dtrow"][0, N_HEADS:2 * N_HEADS], o_norm_w=g["onw"][0],
        sgu_ln_g=g["sg"][0], sgu_ln_b=g["sb"][0], w_s=g["ws"], b_s=g["bst"][:, :SGU_GROUPS].T,
        ln1_g=g["ln1g"][0], ln1_b=g["ln1b"][0], ln2_g=g["ln2g"][0], ln2_b=g["ln2b"][0]))
    return slots, small


def _in_tile_start(j, tile_w):
    if tile_w == LANES:
        return jnp.int32(_QKVZ)
    return j * D_MODEL + jnp.where(j >= _QKVZ // D_MODEL, _BA, 0)


def _select(rows_iota, cols_iota, dev, start, valid):
    hit = (rows_iota + (dev * IN_BLOCK - start) == cols_iota) & (rows_iota < IN_BLOCK) & (cols_iota < valid)
    return jnp.where(hit, 1.0, 0.0).astype(BF)


def _perm_in(stack, tile_w, n_tiles):
    valid = _BA if tile_w == LANES else tile_w

    def first_dev(j):
        return lax.div(_in_tile_start(j, tile_w), jnp.int32(IN_BLOCK))

    def body(w_ref, o_ref, acc_ref):
        j, k = pl.program_id(0), pl.program_id(1)
        sel = _select(_iota((IN_PAD, tile_w), 0), _iota((IN_PAD, tile_w), 1), first_dev(j) + k,
                      _in_tile_start(j, tile_w), valid)
        part = jnp.dot(w_ref[0], sel, preferred_element_type=F32)

        @pl.when(k == 0)
        def _():
            acc_ref[...] = part

        @pl.when(k == 1)
        def _():
            o_ref[...] = (acc_ref[...] + part).astype(BF)

    est = _nbytes((D_MODEL, IN_PAD), BF) + 3 * _nbytes((D_MODEL, tile_w), F32) + 2 * _nbytes((IN_PAD, tile_w), F32)
    return pl.pallas_call(
        body, name="perm_in" if tile_w != LANES else "perm_in_ba", grid=(n_tiles, 2),
        in_specs=[pl.BlockSpec((1, D_MODEL, IN_PAD), lambda j, k: (jnp.minimum(first_dev(j) + k, N_DEV - 1), 0, 0))],
        out_specs=pl.BlockSpec((D_MODEL, tile_w), lambda j, k: (0, j)),
        out_shape=jax.ShapeDtypeStruct((D_MODEL, n_tiles * tile_w), BF),
        scratch_shapes=[pltpu.VMEM((D_MODEL, tile_w), F32)],
        compiler_params=_cparams(est, ("parallel", "arbitrary")),
    )(stack)


def _perm_out(dmain, dba):
    n_main = dmain.shape[1] // D_MODEL

    def body(dm_ref, db_ref, o_ref, acc_ref):
        d, t = pl.program_id(0), pl.program_id(1)

        @pl.when(t == 0)
        def _():
            acc_ref[...] = jnp.zeros_like(acc_ref)

        start = _in_tile_start(t, D_MODEL)
        overlaps = (start < (d + 1) * IN_BLOCK) & (d * IN_BLOCK < start + D_MODEL)

        @pl.when((t < n_main) & overlaps)
        def _():
            sel = _select(_iota((D_MODEL, IN_PAD), 1), _iota((D_MODEL, IN_PAD), 0), d, start, D_MODEL)
            acc_ref[...] += jnp.dot(dm_ref[...], sel, preferred_element_type=F32)

        @pl.when(t == n_main)
        def _():
            sel = _select(_iota((LANES, IN_PAD), 1), _iota((LANES, IN_PAD), 0), d, jnp.int32(_QKVZ), _BA)
            o_ref[0] = (acc_ref[...] + jnp.dot(db_ref[...], sel, preferred_element_type=F32)).astype(BF)

    est = 2 * _nbytes((D_MODEL, D_MODEL), BF) + 4 * _nbytes((D_MODEL, IN_PAD), F32)
    return pl.pallas_call(
        body, name="perm_out", grid=(N_DEV, n_main + 1),
        in_specs=[pl.BlockSpec((D_MODEL, D_MODEL), lambda d, t: (0, jnp.minimum(t, n_main - 1))),
                  pl.BlockSpec((D_MODEL, LANES), lambda d, t: (0, 0))],
        out_specs=pl.BlockSpec((1, D_MODEL, IN_PAD), lambda d, t: (d, 0, 0)),
        out_shape=jax.ShapeDtypeStruct((N_DEV, D_MODEL, IN_PAD), BF),
        scratch_shapes=[pltpu.VMEM((D_MODEL, IN_PAD), F32)],
        compiler_params=_cparams(est, ("parallel", "arbitrary")),
    )(dmain, dba)


def _mesh_place():
    x, y, c = (lax.axis_index(a) for a in MESH_AXES)
    return x, y, c


def _slot(x, y, c):
    return 4 * x + 2 * y + c


def _peer(place, j):
    x, y, c = place
    return (1 - x if j & 4 else x, 1 - y if j & 2 else y, 1 - c if j & 1 else c)


_HBM = pl.BlockSpec(memory_space=pltpu.HBM)
_SEM = pl.BlockSpec(memory_space=pltpu.SEMAPHORE)
_EFFECT = pltpu.SideEffectType.DATAFLOW_SIDE_EFFECTING


def _remote_copy(src_ref, land_ref, slot, per_slot, pslot, sems, u, j, peer):
    return pltpu.make_async_remote_copy(
        src_ref=src_ref.at[pslot] if per_slot else src_ref, dst_ref=land_ref.at[slot],
        send_sem=sems[0].at[u * (N_DEV - 1) + j - 1], recv_sem=sems[1].at[u * (N_DEV - 1) + j - 1],
        device_id=peer, device_id_type=pl.DeviceIdType.MESH)


def _own_copy(src_ref, land_ref, me, per_slot, sems, u):
    return pltpu.make_async_copy(src_ref.at[me] if per_slot else src_ref, land_ref.at[me], sems[2].at[u])


def _exchange_start(name, srcs, per_slot):
    n = len(srcs)
    lands = [jax.ShapeDtypeStruct(s.shape if p else (N_DEV,) + s.shape, s.dtype) for s, p in zip(srcs, per_slot)]

    def body(*refs):
        src_refs, sems, land_refs, token = refs[:n], refs[n:n + 3], refs[2 * n + 3:3 * n + 3], refs[-1]
        place = _mesh_place()
        me = _slot(*place)
        for u in range(n):
            _own_copy(src_refs[u], land_refs[u], me, per_slot[u], sems, u).start()
            for j in range(1, N_DEV):
                peer = _peer(place, j)
                _remote_copy(src_refs[u], land_refs[u], me, per_slot[u], _slot(*peer), sems, u, j, peer).start()
        token[...] = jnp.zeros_like(token)

    hbm = lambda a: pltpu.HBM(a.shape, a.dtype)
    sem = pltpu.SemaphoreType.DMA((n * (N_DEV - 1),))
    outs = pl.pallas_call(
        body, name=name,
        out_shape=(sem, sem, pltpu.SemaphoreType.DMA((n,)), *[hbm(a) for a in srcs], *[hbm(a) for a in lands],
                   jax.ShapeDtypeStruct((SUBLANES, LANES), F32)),
        in_specs=[_HBM] * n, out_specs=(_SEM, _SEM, _SEM, *[_HBM] * (2 * n), pl.BlockSpec(memory_space=pltpu.VMEM)),
        input_output_aliases={i: 3 + i for i in range(n)},
        compiler_params=pltpu.CompilerParams(has_side_effects=_EFFECT),
    )(*[pltpu.with_memory_space_constraint(a, pltpu.HBM) for a in srcs])
    return tuple(outs[:3]), list(outs[3:3 + n]), list(outs[3 + n:3 + 2 * n]), outs[-1]


def _exchange_wait(name, sems, srcs, lands, units, per_slot, after):
    m = len(units)

    def body(*refs):
        src_refs, land_refs, sem_refs = refs[:m], refs[m:2 * m], refs[2 * m:2 * m + 3]
        place = _mesh_place()
        me = _slot(*place)
        for i, u in enumerate(units):
            _own_copy(src_refs[i], land_refs[i], me, per_slot[u], sem_refs, u).wait()
            for j in range(1, N_DEV):
                peer = _peer(place, j)
                pslot = _slot(*peer)
                cp = _remote_copy(src_refs[i], land_refs[i], pslot, per_slot[u], pslot, sem_refs, u, j, peer)
                cp.wait_send()
                cp.wait_recv()

    hbm = lambda a: pltpu.HBM(a.shape, a.dtype)
    outs = pl.pallas_call(
        body, name=name, out_shape=tuple(hbm(a) for a in list(srcs) + list(lands)),
        in_specs=[_HBM] * (2 * m) + [_SEM] * 3 + [pl.BlockSpec(memory_space=pl.ANY)], out_specs=tuple([_HBM] * (2 * m)),
        input_output_aliases={i: i for i in range(2 * m)},
        compiler_params=pltpu.CompilerParams(has_side_effects=_EFFECT),
    )(*srcs, *lands, *sems, after)
    return list(outs[m:])


def _adam_update(g, w, m, v):
    m = ADAM_B1 * m + (1.0 - ADAM_B1) * g
    v = ADAM_B2 * v + (1.0 - ADAM_B2) * jnp.square(g)
    m_hat = m / (1.0 - ADAM_B1 ** ADAM_STEP)
    v_hat = v / (1.0 - ADAM_B2 ** ADAM_STEP)
    return -ADAM_LR * (m_hat / (jnp.sqrt(v_hat) + ADAM_EPS) + ADAM_WD * w), m, v


def _adamw(recvs, w, m, v, *, tr, name):
    L, R, C = w.shape
    rp = max(tr, SUBLANES * (4 // jnp.dtype(recvs[0].dtype).itemsize))
    Cp = recvs[0].shape[2]

    def body(*refs):
        r_refs, (w_ref, m_ref, v_ref, g_ref, d_ref, nm_ref, nv_ref) = refs[:L], refs[L:]
        for l in range(L):
            @pl.when(pl.program_id(0) == l)
            def _(r_ref=r_refs[l]):
                g = r_ref[0, :tr, :C].astype(F32)
                for s in range(1, N_DEV):
                    g = g + r_ref[s, :tr, :C].astype(F32)
                d, nm, nv = _adam_update(g, w_ref[0], m_ref[0], v_ref[0])
                g_ref[0], d_ref[0], nm_ref[0], nv_ref[0] = g, d, nm, nv

    blk = pl.BlockSpec((1, tr, C), lambda l, i: (l, i, 0))
    r_specs = [pl.BlockSpec((N_DEV, rp, Cp), lambda l, i, k=k: (0, jnp.where(l == k, i, 0), 0)) for k in range(L)]
    est = 2 * _nbytes((N_DEV, rp, Cp), recvs[0].dtype) + 8 * _nbytes((tr, Cp), F32)
    return pl.pallas_call(
        body, name=name, grid=(L, R // tr),
        in_specs=r_specs + [blk] * 3, out_specs=[blk] * 4,
        out_shape=[jax.ShapeDtypeStruct((L, R, C), F32)] * 4,
        compiler_params=_cparams(est, ("arbitrary", "arbitrary")),
    )(*recvs, w, m, v)


def _adamw_small(recv, w, m, v):
    def body(r_ref, w_ref, m_ref, v_ref, g_ref, d_ref, nm_ref, nv_ref):
        g = r_ref[0]
        for s in range(1, N_DEV):
            g = g + r_ref[s]
        g_ref[...] = g
        d_ref[...], nm_ref[...], nv_ref[...] = _adam_update(g, w_ref[...], m_ref[...], v_ref[...])

    vm = pl.BlockSpec(memory_space=pltpu.VMEM)
    return pl.pallas_call(
        body, name="adamw_small", in_specs=[vm] * 4, out_specs=[vm] * 4,
        out_shape=[jax.ShapeDtypeStruct((SMALL_ROWS, LANES), F32)] * 4,
        compiler_params=_cparams(20 * _nbytes((SMALL_ROWS, LANES), F32)),
    )(recv, w, m, v)


def kernel(x, w_in, conv_w, a_log, dt_bias, o_norm_w, sgu_ln_g, sgu_ln_b, w_s, b_s, w_pa, w_pb, w_o, ln1_g, ln1_b, w_ffn_gate, w_ffn_up, w_ffn_down, ln2_g, ln2_b, loss_target, m_w_in, m_conv_w, m_a_log, m_dt_bias, m_o_norm_w, m_sgu_ln_g, m_sgu_ln_b, m_w_s, m_b_s, m_w_pa, m_w_pb, m_w_o, m_ln1_g, m_ln1_b, m_w_ffn_gate, m_w_ffn_up, m_w_ffn_down, m_ln2_g, m_ln2_b, v_w_in, v_conv_w, v_a_log, v_dt_bias, v_o_norm_w, v_sgu_ln_g, v_sgu_ln_b, v_w_s, v_b_s, v_w_pa, v_w_pb, v_w_o, v_ln1_g, v_ln1_b, v_w_ffn_gate, v_w_ffn_up, v_w_ffn_down, v_ln2_g, v_ln2_b):
    given = dict(locals())
    P = {n: given[n] for n in WEIGHT_NAMES}
    M = {n: given["m_" + n] for n in WEIGHT_NAMES}
    V = {n: given["v_" + n] for n in WEIGHT_NAMES}

    wire = _wire_blocks(P)
    units = [(n, l) for l in range(DEPTH) for n in EARLY + LATE]
    whole = [False] * len(units)
    g_sems, g_srcs, g_lands, g_token = _exchange_start("gather_start", [wire[n][l] for n, l in units], whole)

    def gathered(name, names, l, after):
        idx = [units.index((n, l)) for n in names]
        got = _exchange_wait(name, g_sems, [g_srcs[i] for i in idx], [g_lands[i] for i in idx], idx, whole, after)
        return dict(zip(names, got))

    def layer(l):
        def weights(x_in):
            after = g_token if l == 0 else x_in
            early = _early_weights(gathered(f"gather_wait_early{l}", EARLY, l, after), P, l)
            return early, lambda ya: _late_weights(gathered(f"gather_wait_late{l}", LATE, l, ya))
        return weights

    pending = {}

    def on_grads(l, part, g):
        if part == "late":
            srcs, names = _late_slots(g), LATE
            per_slot = [True] * len(srcs)
        else:
            slots, small = _early_slots(g)
            srcs, names = slots + [small], EARLY + ("small",)
            per_slot = [True] * len(slots) + [False]
        sems, s_thru, l_thru, token = _exchange_start(f"exchange_start_{part}{l}", srcs, per_slot)
        pending[l, part] = (names, sems, s_thru, l_thru, per_slot)
        return token[0, 0]

    loss_local, dx, _ = _local_step(x[0], loss_target[0], [layer(l) for l in range(DEPTH)], on_grads)
    loss = lax.psum(loss_local, MESH_AXES)

    recv = [{} for _ in range(DEPTH)]
    for l in reversed(range(DEPTH)):
        for part in ("late", "early"):
            names, sems, s_thru, l_thru, per_slot = pending[l, part]
            got = _exchange_wait(f"exchange_wait_{part}{l}", sems, s_thru, l_thru, list(range(len(s_thru))), per_slot, dx)
            recv[l].update(zip(names, got))

    out = {}
    for n in WIRE:
        tr, name = ADAM_TILES[n]
        out[n] = _adamw([recv[l][n] for l in range(DEPTH)], P[n], M[n], V[n], tr=tr, name=name)
    small = [_adamw_small(recv[l]["small"], *[_small_pack({n: T[n][l] for n, _ in SMALL}) for T in (P, M, V)])
             for l in range(DEPTH)]
    for n, _ in SMALL:
        out[n] = [jnp.stack([_small_unpack(small[l][i], P)[n] for l in range(DEPTH)]) for i in range(4)]
    return (loss, dx[None], *[out[n][i] for i in range(4) for n in WEIGHT_NAMES])
```

```python
import functools
import math

import jax
import jax.numpy as jnp
from jax import lax
from jax.experimental import pallas as pl
from jax.experimental.pallas import tpu as pltpu

F32 = jnp.float32
BF = jnp.bfloat16
HIGHEST = lax.Precision.HIGHEST

D_MODEL = 1024
DEPTH = 2
N_HEADS = 8
D_HEAD = 128
CONV_K = 4
SGU_BLOCK = 128
SGU_GROUPS = 8
SGU_CHUNK = 64
FFN_HIDDEN = 2816
N_IN = 8208
N_DEV = 8
IN_BLOCK, IN_PAD = N_IN // N_DEV, 1152
FFN_BLOCK, FFN_PAD = FFN_HIDDEN // N_DEV, 384
FFN_K = N_DEV * FFN_PAD
ALPHA = (2 * DEPTH) ** 0.25
LN_EPS = 1e-5
RMS_EPS = 1e-6
ADAM_LR, ADAM_B1, ADAM_B2, ADAM_EPS, ADAM_WD, ADAM_STEP = 0.001, 0.9, 0.999, 1e-08, 0.01, 10

MESH_AXES = ("x", "y", "c")
DELTA_CHUNK = 128
DELTA_HEADS_PER_STEP = 8
LANES = 128
SUBLANES = 8
VMEM_BYTES = 64 * 1024 * 1024
HALO = SUBLANES
HALO_BF = 2 * SUBLANES


def _cparams(est_bytes, dims=None):
    limit = int(min(max(2 * est_bytes + (8 << 20), 32 << 20), VMEM_BYTES - (6 << 20)))
    kw = dict(vmem_limit_bytes=limit)
    if dims is not None:
        kw["dimension_semantics"] = dims
    return pltpu.CompilerParams(**kw)


def _nbytes(shape, dtype):
    return math.prod(shape) * jnp.dtype(dtype).itemsize


def _dims(kind, ndim):
    lhs, rhs = {"nn": (1, 0), "nt": (1, 1), "tn": (0, 0)}[kind]
    b = ndim - 2
    return (((lhs + b,), (rhs + b,)), (tuple(range(b)), tuple(range(b))))


def _mxu(a, b, kind):
    return lax.dot_general(a, b, _dims(kind, a.ndim), preferred_element_type=F32)


def _dot(a, b):
    return _mxu(a.astype(BF), b.astype(BF), "nn")


def _dot_nt(a, b):
    return _mxu(a.astype(BF), b.astype(BF), "nt")


def _dot_tn(a, b):
    return _mxu(a.astype(BF), b.astype(BF), "tn")


def _split(a):
    hi = a.astype(BF)
    return hi, (a - hi.astype(F32)).astype(BF)


def _dot3(a, b, kind):
    (ah, al), (bh, bl) = _split(a), _split(b)
    return _mxu(ah, bh, kind) + (_mxu(ah, bl, kind) + _mxu(al, bh, kind))


def _dotf(a, b):
    return _dot3(a, b, "nn")


def _dotf_nt(a, b):
    return _dot3(a, b, "nt")


def _dotf_tn(a, b):
    return _dot3(a, b, "tn")


def _dot01(sel, x, kind="nn"):
    s = jnp.broadcast_to(sel.astype(BF), x.shape[:-2] + sel.shape)
    h1 = x.astype(BF)
    r1 = x - h1.astype(F32)
    h2 = r1.astype(BF)
    h3 = (r1 - h2.astype(F32)).astype(BF)
    return _mxu(s, h1, kind) + (_mxu(s, h2, kind) + _mxu(s, h3, kind))


def _sigmoid(x):
    return jax.nn.sigmoid(x)


def _silu(x):
    return x * _sigmoid(x)


def _dsilu(x):
    s = _sigmoid(x)
    return s * (1.0 + x * (1.0 - s))


def _gelu(x):
    return 0.5 * x * (1.0 + lax.erf(x * 0.7071067811865476))


def _softplus(x):
    return jnp.maximum(x, 0.0) + jnp.log1p(jnp.exp(-jnp.abs(x)))


def _ln(x, g, b):
    mu = jnp.mean(x, -1, keepdims=True)
    xc = x - mu
    var = jnp.mean(xc * xc, -1, keepdims=True)
    return xc * lax.rsqrt(var + LN_EPS) * g + b


def _iota(shape, dim):
    return lax.broadcasted_iota(jnp.int32, shape, dim)


def _tile(n, pref, align):
    if n <= pref:
        return n
    t = (pref // align) * align
    while t >= align:
        if n % t == 0:
            return t
        t -= align
    raise ValueError(f"no tile for {n} (pref {pref}, align {align})")


def _bcast_rows(v, rows=SUBLANES):
    return jnp.broadcast_to(v, (rows, v.shape[-1]))


def _mm(a, b, *, mode, name, out_dtype=F32, add=None, add_scale=1.0, tm=512, tn=1024, tk=1024):
    if mode == "nn":
        (M, K), N = a.shape, b.shape[1]
    elif mode == "nt":
        (M, K), N = a.shape, b.shape[0]
    else:
        (K, M), N = a.shape, b.shape[1]
    tm = _tile(M, tm, LANES if mode == "tn" else SUBLANES * 2)
    tn = _tile(N, tn, LANES)
    tk = _tile(K, tk, LANES)
    nk = K // tk
    if mode == "nn":
        a_spec = pl.BlockSpec((tm, tk), lambda i, j, k: (i, k))
        b_spec = pl.BlockSpec((tk, tn), lambda i, j, k: (k, j))
        dot = _dot
    elif mode == "nt":
        a_spec = pl.BlockSpec((tm, tk), lambda i, j, k: (i, k))
        b_spec = pl.BlockSpec((tn, tk), lambda i, j, k: (j, k))
        dot = _dot_nt
    else:
        a_spec = pl.BlockSpec((tk, tm), lambda i, j, k: (k, i))
        b_spec = pl.BlockSpec((tk, tn), lambda i, j, k: (k, j))
        dot = _dot_tn
    o_spec = pl.BlockSpec((tm, tn), lambda i, j, k: (i, j))
    has_add = add is not None

    def body(*refs):
        if has_add:
            a_ref, b_ref, add_ref, o_ref, acc_ref = refs
        else:
            a_ref, b_ref, o_ref, acc_ref = refs
            add_ref = None
        k = pl.program_id(2)
        part = dot(a_ref[...], b_ref[...])

        def finish(total):
            if has_add:
                total = total + add_scale * add_ref[...]
            o_ref[...] = total.astype(out_dtype)

        if nk == 1:
            finish(part)
        else:
            @pl.when(k == 0)
            def _():
                acc_ref[...] = part

            @pl.when(jnp.logical_and(k > 0, k < nk - 1))
            def _():
                acc_ref[...] += part

            @pl.when(k == nk - 1)
            def _():
                finish(acc_ref[...] + part)

    in_specs = [a_spec, b_spec] + ([o_spec] if has_add else [])
    args = (a, b) + ((add,) if has_add else ())
    est = (_nbytes((tm, tk), a.dtype) + _nbytes((tk, tn), b.dtype) + 2 * _nbytes((tm, tn), F32)
           + (_nbytes((tm, tn), F32) if has_add else 0)) + 2 * _nbytes((tm, tn), F32)
    return pl.pallas_call(
        body, name=name,
        grid=(M // tm, N // tn, nk),
        in_specs=in_specs, out_specs=o_spec,
        out_shape=jax.ShapeDtypeStruct((M, N), out_dtype),
        scratch_shapes=[pltpu.VMEM((tm, tn) if nk > 1 else (SUBLANES, LANES), F32)],
        compiler_params=_cparams(est, ("parallel", "parallel", "arbitrary")),
    )(*args)


def _conv_taps(xt, halo, w_ref, first):
    halo = jnp.where(first, 0.0, halo)
    xc = jnp.concatenate([halo, xt], axis=0)
    shifted = [xt] + [pltpu.roll(xc, s, 0)[HALO:] for s in range(1, CONV_K)]
    out = shifted[0] * w_ref[CONV_K - 1:CONV_K, :]
    for s in range(1, CONV_K):
        out = out + shifted[s] * w_ref[CONV_K - 1 - s:CONV_K - s, :]
    return out, shifted


def _gates(ba, arow, dtrow):
    lane = _iota(ba.shape, 1)
    beta = _sigmoid(ba)
    g = -jnp.exp(arow) * _softplus(ba + dtrow)
    return jnp.where(lane < N_HEADS, beta, jnp.where(lane < 2 * N_HEADS, g, 0.0))


def _l2n(x):
    return x * lax.rsqrt(jnp.sum(x * x, -1, keepdims=True) + RMS_EPS)


def _qkv_prep(proj, ba, convw, arow, dtrow, *, tm=256):
    S = proj.shape[0]
    tm = _tile(S, tm, SUBLANES)
    W3 = 3 * D_MODEL
    hb = tm // HALO

    def body(xt_ref, halo_ref, ba_ref, w_ref, a_ref, dt_ref, q_ref, k_ref, v_ref, gb_ref):
        c, _ = _conv_taps(xt_ref[...], halo_ref[...], w_ref, pl.program_id(0) == 0)
        c = _silu(c)
        for h in range(N_HEADS):
            lo = h * D_HEAD
            q_ref[:, lo:lo + D_HEAD] = _l2n(c[:, lo:lo + D_HEAD])
            k_ref[:, lo:lo + D_HEAD] = _l2n(c[:, D_MODEL + lo:D_MODEL + lo + D_HEAD])
        v_ref[...] = c[:, 2 * D_MODEL:]
        gb_ref[...] = _gates(ba_ref[...], a_ref[...], dt_ref[...])

    row = lambda w, col=0: pl.BlockSpec((tm, w), lambda i: (i, col))
    full = lambda shape: pl.BlockSpec(shape, lambda i: (0,) * len(shape))
    est = 4 * _nbytes((tm, W3), F32)
    return pl.pallas_call(
        body, name="qkv_prep", grid=(S // tm,),
        in_specs=[row(W3), pl.BlockSpec((HALO, W3), lambda i: (jnp.maximum(i * hb - 1, 0), 0)), row(LANES),
                  full((CONV_K, W3)), full((1, LANES)), full((1, LANES))],
        out_specs=[row(D_MODEL), row(D_MODEL), row(D_MODEL), row(LANES)],
        out_shape=[jax.ShapeDtypeStruct((S, D_MODEL), F32)] * 3 + [jax.ShapeDtypeStruct((S, LANES), F32)],
        compiler_params=_cparams(est, ("arbitrary",)),
    )(proj, proj, ba, convw, arow, dtrow)


def _qkv_prep_bwd(proj, ba, convw, arow, dtrow, dq, dk, dv, dgb, *, tm=256):
    S = proj.shape[0]
    tm = _tile(S, tm, SUBLANES * 2)
    W3 = 3 * D_MODEL
    hb = tm // HALO

    def body(xt_ref, halo_ref, ba_ref, w_ref, a_ref, dt_ref, dq_ref, dk_ref, dv_ref, dgb_ref,
             dcb_ref, dba_ref, dw_ref, da_ref, ddt_ref, dc_ref):
        i = pl.program_id(0)

        @pl.when(i == 0)
        def _():
            dw_ref[...] = jnp.zeros_like(dw_ref)
            da_ref[...] = jnp.zeros_like(da_ref)
            ddt_ref[...] = jnp.zeros_like(ddt_ref)

        c, shifted = _conv_taps(xt_ref[...], halo_ref[...], w_ref, i == 0)
        a = _silu(c)
        ds = _dsilu(c)
        for h in range(N_HEADS):
            for base, d_ref in ((0, dq_ref), (D_MODEL, dk_ref)):
                lo = base + h * D_HEAD
                _, vj = jax.vjp(_l2n, a[:, lo:lo + D_HEAD])
                (dx,) = vj(d_ref[:, h * D_HEAD:(h + 1) * D_HEAD])
                dc_ref[:, lo:lo + D_HEAD] = dx * ds[:, lo:lo + D_HEAD]
        dc_ref[:, 2 * D_MODEL:] = dv_ref[...] * ds[:, 2 * D_MODEL:]
        dc = dc_ref[...]
        dcb_ref[...] = dc.astype(BF)
        for s in range(CONV_K):
            kk = CONV_K - 1 - s
            dw_ref[kk:kk + 1, :] += jnp.sum(dc * shifted[s], axis=0, keepdims=True)
        _, vj = jax.vjp(_gates, ba_ref[...], a_ref[...], dt_ref[...])
        dba, da, ddt = vj(dgb_ref[...])
        dba_ref[...] = dba.astype(BF)
        da_ref[...] += _bcast_rows(da)
        ddt_ref[...] += _bcast_rows(ddt)

    row = lambda w, col=0: pl.BlockSpec((tm, w), lambda i: (i, col))
    full = lambda shape: pl.BlockSpec(shape, lambda i: (0,) * len(shape))
    est = 8 * _nbytes((tm, W3), F32)
    return pl.pallas_call(
        body, name="qkv_prep_bwd", grid=(S // tm,),
        in_specs=[row(W3), pl.BlockSpec((HALO, W3), lambda i: (jnp.maximum(i * hb - 1, 0), 0)), row(LANES),
                  full((CONV_K, W3)), full((1, LANES)), full((1, LANES)),
                  row(D_MODEL), row(D_MODEL), row(D_MODEL), row(LANES)],
        out_specs=[row(W3), row(LANES), full((SUBLANES, W3)), full((SUBLANES, LANES)), full((SUBLANES, LANES))],
        out_shape=[jax.ShapeDtypeStruct((S, W3), BF), jax.ShapeDtypeStruct((S, LANES), BF),
                   jax.ShapeDtypeStruct((SUBLANES, W3), F32), jax.ShapeDtypeStruct((SUBLANES, LANES), F32),
                   jax.ShapeDtypeStruct((SUBLANES, LANES), F32)],
        scratch_shapes=[pltpu.VMEM((tm, W3), F32)],
        compiler_params=_cparams(est, ("arbitrary",)),
    )(proj, proj, ba, convw, arow, dtrow, dq, dk, dv, dgb)


def _conv_bwd(dc, convw, dproj, *, tm=256):
    S, W3 = dc.shape
    tm = _tile(S, tm, HALO_BF)
    hb = tm // HALO_BF
    nt = S // tm

    def body(dc_ref, nxt_ref, w_ref, dproj_ref, o_ref):
        last = pl.program_id(0) == nt - 1
        nxt = jnp.where(last, 0.0, nxt_ref[...].astype(F32))
        cur = dc_ref[...].astype(F32)
        xc = jnp.concatenate([cur, nxt], axis=0)
        out = cur * w_ref[CONV_K - 1:CONV_K, :]
        for s in range(1, CONV_K):
            out = out + pltpu.roll(xc, tm + HALO_BF - s, 0)[:tm] * w_ref[CONV_K - 1 - s:CONV_K - s, :]
        o_ref[...] = out.astype(BF)

    est = 5 * _nbytes((tm, W3), F32)
    return pl.pallas_call(
        body, name="conv_bwd", grid=(nt,),
        in_specs=[pl.BlockSpec((tm, W3), lambda i: (i, 0)),
                  pl.BlockSpec((HALO_BF, W3), lambda i: (jnp.minimum((i + 1) * hb, S // HALO_BF - 1), 0)),
                  pl.BlockSpec((CONV_K, W3), lambda i: (0, 0)), pl.BlockSpec(memory_space=pl.ANY)],
        out_specs=pl.BlockSpec((tm, W3), lambda i: (i, 0)),
        out_shape=jax.ShapeDtypeStruct(dproj.shape, BF),
        input_output_aliases={3: 0},
        compiler_params=_cparams(est, ("parallel",)),
    )(dc, dc, convw, dproj)


def _inv_unit_lower(A):
    C = A.shape[-1]
    row, col = _iota((C, C), 0), _iota((C, C), 1)
    T = jnp.broadcast_to(jnp.where(row == col, 1.0, 0.0).astype(F32), A.shape)
    b = 1
    while b < C:
        hi = ~(2 * b - 1)
        off = ((row & hi) == (col & hi)) & ((row & b) != 0) & ((col & b) == 0)
        T = T - _dotf(_dotf(T, jnp.where(off, A, 0.0)), T)
        b *= 2
    return T


def _delta_common(q, k, g, beta):
    C = q.shape[-2]
    row, col = _iota((C, C), 0), _iota((C, C), 1)
    tril = row >= col
    qs = q * (D_HEAD ** -0.5)
    gcb = _dot01(jnp.where(tril, 1.0, 0.0), jnp.broadcast_to(g, g.shape[:-1] + (LANES,)))
    gc = gcb[..., :1]
    Dm = jnp.exp(jnp.where(tril, gc - jnp.swapaxes(gcb, -1, -2), -1e30))
    eg = jnp.exp(gc)
    gl = jnp.sum(jnp.where(_iota((C, 1), 0) == C - 1, gc, 0.0), axis=(-2, -1), keepdims=True)
    el = jnp.exp(gl)
    er = jnp.exp(gl - gc)
    kb = k * beta
    KK = _dot_nt(kb, k)
    QK = _dot_nt(qs, k)
    return dict(row=row, col=col, tril=tril, qs=qs, gc=gc, Dm=Dm, eg=eg, el=el, er=er, kb=kb, KK=KK, QK=QK)


def _delta_chunk_fwd(S0, q, k, v, g, beta, T=None):
    m = _delta_common(q, k, g, beta)
    if T is None:
        T = _inv_unit_lower(jnp.where(m["row"] > m["col"], m["KK"] * m["Dm"], 0.0))
    u = _dotf(T, v * beta)
    w = _dotf(T, m["kb"] * m["eg"])
    vn = u - _dot(w, S0)
    o = _dot(m["qs"] * m["eg"], S0) + _dot(m["QK"] * m["Dm"], vn)
    S1 = S0 * m["el"] + _dot_tn(k * m["er"], vn)
    return o, S1, T


def _delta_chunk_bwd(S0, q, k, v, g, beta, T, do, dS1):
    m = _delta_common(q, k, g, beta)
    C = q.shape[-2]
    qs, Dm, eg, el, er, kb, KK, QK = (m[n] for n in ("qs", "Dm", "eg", "el", "er", "kb", "KK", "QK"))
    strict = m["row"] > m["col"]
    total = lambda x: jnp.sum(x, axis=(-2, -1), keepdims=True)
    ru, rw = v * beta, kb * eg
    u = _dotf(T, ru)
    w = _dotf(T, rw)
    vn = u - _dot(w, S0)
    P = QK * Dm
    qg = qs * eg
    kr = k * er

    dvn = _dot_tn(P, do) + _dot(kr, dS1)
    dS0 = dS1 * el + _dot_tn(qg, do) - _dot_tn(w, dvn)
    d_el = total(dS1 * S0)
    dqg = _dot_nt(do, S0)
    dqs = dqg * eg
    deg = jnp.sum(dqg * qs, -1, keepdims=True)
    dP = _dot_nt(do, vn)
    dPD = dP * Dm
    dqs = dqs + _dot(dPD, k)
    dk = _dot_tn(dPD, qs)
    dD = dP * QK
    dkr = _dot_nt(vn, dS1)
    dk = dk + dkr * er
    der = jnp.sum(dkr * k, -1, keepdims=True)
    dw = -_dot_nt(dvn, S0)
    dru = _dotf_tn(T, dvn)
    drw = _dotf_tn(T, dw)
    dT = _dotf_nt(dvn, ru) + _dotf_nt(dw, rw)
    dA = -_dotf_nt(_dotf_tn(T, dT), T)
    dAm = jnp.where(strict, dA, 0.0)
    dKK = dAm * Dm
    dkb = _dot(dKK, k)
    dk = dk + _dot_tn(dKK, kb)
    dD = dD + dAm * KK
    dv = dru * beta
    dbeta = jnp.sum(dru * v, -1, keepdims=True)
    dkb = dkb + drw * eg
    deg = deg + jnp.sum(drw * kb, -1, keepdims=True)
    dk = dk + dkb * beta
    dbeta = dbeta + jnp.sum(dkb * k, -1, keepdims=True)
    E = dD * Dm
    dgc = jnp.sum(E, -1, keepdims=True) - jnp.sum(jnp.swapaxes(E, -1, -2), -1, keepdims=True)
    dgc = dgc + deg * eg - der * er
    dgl = total(der * er) + d_el * el
    dgc = dgc + jnp.where(_iota((C, 1), 0) == C - 1, dgl, 0.0)
    triu = jnp.where(m["row"] <= m["col"], 1.0, 0.0)
    dg = _dot01(triu, jnp.broadcast_to(dgc, dgc.shape[:-1] + (LANES,)))[..., :1]
    dq = dqs * (D_HEAD ** -0.5)
    return dq, dk, dv, dg, dbeta, dS0


def _head_cols(gb, h):
    lane = _iota(gb.shape, 1)
    beta = jnp.sum(jnp.where(lane == h, gb, 0.0), -1, keepdims=True)
    g = jnp.sum(jnp.where(lane == N_HEADS + h, gb, 0.0), -1, keepdims=True)
    return g, beta


def _delta_fwd(q, k, v, gb):
    S = q.shape[0]
    C = DELTA_CHUNK
    N = S // C

    HB = DELTA_HEADS_PER_STEP

    def body(q_ref, k_ref, v_ref, gb_ref, o_ref, st_ref, t_ref, s_scr):
        n, hb = pl.program_id(0), pl.program_id(1)
        gb = gb_ref[...]

        @pl.when(n == 0)
        def _():
            for hh in range(HB):
                s_scr[hb * HB + hh] = jnp.zeros((D_HEAD, D_HEAD), F32)

        heads = [hb * HB + hh for hh in range(HB)]
        cols = [slice(hh * D_HEAD, (hh + 1) * D_HEAD) for hh in range(HB)]
        per_head = lambda ref: jnp.stack([ref[:, c] for c in cols])
        g, beta = (jnp.stack(t) for t in zip(*[_head_cols(gb, h) for h in heads]))
        S0 = jnp.stack([s_scr[h] for h in heads])
        o, S1, T = _delta_chunk_fwd(S0, per_head(q_ref), per_head(k_ref), per_head(v_ref), g, beta)
        for hh in range(HB):
            st_ref[hh, 0] = S0[hh]
            t_ref[hh, 0] = T[hh]
            o_ref[:, cols[hh]] = o[hh]
            s_scr[heads[hh]] = S1[hh]

    hd = pl.BlockSpec((C, HB * D_HEAD), lambda n, h: (n, h))
    mat = pl.BlockSpec((HB, 1, D_HEAD, D_HEAD), lambda n, h: (h, n, 0, 0))
    est = 40 * HB * _nbytes((C, D_HEAD), F32)
    return pl.pallas_call(
        body, name="delta_fwd", grid=(N, N_HEADS // HB),
        in_specs=[hd, hd, hd, pl.BlockSpec((C, LANES), lambda n, h: (n, 0))],
        out_specs=[hd, mat, mat],
        out_shape=[jax.ShapeDtypeStruct((S, N_HEADS * D_HEAD), F32),
                   jax.ShapeDtypeStruct((N_HEADS, N, D_HEAD, D_HEAD), F32),
                   jax.ShapeDtypeStruct((N_HEADS, N, C, C), F32)],
        scratch_shapes=[pltpu.VMEM((N_HEADS, D_HEAD, D_HEAD), F32)],
        compiler_params=_cparams(est, ("arbitrary", "arbitrary")),
    )(q, k, v, gb)


def _delta_bwd(q, k, v, gb, st, tinv, do):
    S = q.shape[0]
    C = DELTA_CHUNK
    N = S // C

    HB = DELTA_HEADS_PER_STEP

    def body(q_ref, k_ref, v_ref, gb_ref, st_ref, t_ref, do_ref, dq_ref, dk_ref, dv_ref, dgb_ref, ds_scr):
        n, hb = pl.program_id(0), pl.program_id(1)
        gb = gb_ref[...]
        lane = _iota((C, LANES), 1)
        dgb = jnp.zeros((C, LANES), F32)

        @pl.when(n == 0)
        def _():
            for hh in range(HB):
                ds_scr[hb * HB + hh] = jnp.zeros((D_HEAD, D_HEAD), F32)

        heads = [hb * HB + hh for hh in range(HB)]
        cols = [slice(hh * D_HEAD, (hh + 1) * D_HEAD) for hh in range(HB)]
        per_head = lambda ref: jnp.stack([ref[:, c] for c in cols])
        g, beta = (jnp.stack(t) for t in zip(*[_head_cols(gb, h) for h in heads]))
        dS1 = jnp.stack([ds_scr[h] for h in heads])
        dq, dk, dv, dg, dbeta, dS0 = _delta_chunk_bwd(
            st_ref[:, 0], per_head(q_ref), per_head(k_ref), per_head(v_ref), g, beta, t_ref[:, 0], per_head(do_ref), dS1)
        for hh, h in enumerate(heads):
            dq_ref[:, cols[hh]] = dq[hh]
            dk_ref[:, cols[hh]] = dk[hh]
            dv_ref[:, cols[hh]] = dv[hh]
            dgb = dgb + jnp.where(lane == h, dbeta[hh], 0.0) + jnp.where(lane == N_HEADS + h, dg[hh], 0.0)
            ds_scr[h] = dS0[hh]

        @pl.when(hb == 0)
        def _():
            dgb_ref[...] = dgb

        @pl.when(hb > 0)
        def _():
            dgb_ref[...] += dgb

    hd = pl.BlockSpec((C, HB * D_HEAD), lambda n, h: (N - 1 - n, h))
    mat = pl.BlockSpec((HB, 1, D_HEAD, D_HEAD), lambda n, h: (h, N - 1 - n, 0, 0))
    gbs = pl.BlockSpec((C, LANES), lambda n, h: (N - 1 - n, 0))
    est = 60 * HB * _nbytes((C, D_HEAD), F32)
    return pl.pallas_call(
        body, name="delta_bwd", grid=(N, N_HEADS // HB),
        in_specs=[hd, hd, hd, gbs, mat, mat, hd],
        out_specs=[hd, hd, hd, gbs],
        out_shape=[jax.ShapeDtypeStruct((S, N_HEADS * D_HEAD), F32)] * 3 + [jax.ShapeDtypeStruct((S, LANES), F32)],
        scratch_shapes=[pltpu.VMEM((N_HEADS, D_HEAD, D_HEAD), F32)],
        compiler_params=_cparams(est, ("arbitrary", "arbitrary")),
    )(q, k, v, gb, st, tinv, do)


def _ya_head(o, z, onw):
    return o * lax.rsqrt(jnp.mean(o * o, -1, keepdims=True) + RMS_EPS) * onw * _silu(z)


def _sgu_pre(u, vg, sg, sb):
    return _gelu(u), _ln(_gelu(vg), sg, sb)


def _chunk_causal(shape, di, dj):
    sh = jnp.int32(int(math.log2(SGU_CHUNK)))
    return lax.shift_right_logical(_iota(shape, di), sh) >= lax.shift_right_logical(_iota(shape, dj), sh)


def _ws_masked(ws):
    return jnp.where(_chunk_causal(ws.shape, 1, 2), ws, 0.0)


def _mix_prep(o, proj, onw, sg, sb, ws, bst, *, tm=256):
    S = o.shape[0]
    tm = _tile(S, tm, SGU_BLOCK)

    def body(o_ref, z_ref, u_ref, vg_ref, onw_ref, sg_ref, sb_ref, ws_ref, bst_ref, ya_ref, yb_ref):
        onw = onw_ref[...]
        for h in range(N_HEADS):
            sl = slice(h * D_HEAD, (h + 1) * D_HEAD)
            ya_ref[:, sl] = _ya_head(o_ref[:, sl], z_ref[:, sl], onw).astype(BF)
        ua, vl = _sgu_pre(u_ref[...], vg_ref[...], sg_ref[...], sb_ref[...])
        wsm = _ws_masked(ws_ref[...])
        bst = bst_ref[...]
        for blk in range(tm // SGU_BLOCK):
            rs = slice(blk * SGU_BLOCK, (blk + 1) * SGU_BLOCK)
            for gi in range(SGU_GROUPS):
                cs = slice(gi * D_HEAD, (gi + 1) * D_HEAD)
                sp = _dot(wsm[gi], vl[rs, cs]) + bst[:, gi:gi + 1]
                yb_ref[rs, cs] = (ua[rs, cs] * sp).astype(BF)

    blk = lambda col: pl.BlockSpec((tm, D_MODEL), lambda i: (i, col))
    full = lambda shape: pl.BlockSpec(shape, lambda i: (0,) * len(shape))
    est = 10 * _nbytes((tm, D_MODEL), F32)
    return pl.pallas_call(
        body, name="mix_prep", grid=(S // tm,),
        in_specs=[blk(0), blk(3), blk(4), blk(5), full((1, D_HEAD)), full((1, D_MODEL)), full((1, D_MODEL)),
                  full((SGU_GROUPS, SGU_BLOCK, SGU_BLOCK)), full((SGU_BLOCK, LANES))],
        out_specs=[blk(0), blk(0)],
        out_shape=[jax.ShapeDtypeStruct((S, D_MODEL), BF)] * 2,
        compiler_params=_cparams(est, ("parallel",)),
    )(o, proj, proj, proj, onw, sg, sb, ws, bst)


def _mix_prep_bwd(o, proj, onw, sg, sb, ws, bst, dya, dyb, dproj, *, tm=256):
    S = o.shape[0]
    tm = _tile(S, tm, SGU_BLOCK)

    def body(o_ref, z_ref, u_ref, vg_ref, onw_ref, sg_ref, sb_ref, ws_ref, bst_ref, dya_ref, dyb_ref, dproj_in,
             do_ref, dzuv_ref, donw_ref, dsg_ref, dsb_ref, dws_ref, dbst_ref, dvl_scr, dua_scr):
        dz_ref, du_ref, dvg_ref = (dzuv_ref.at[:, k * D_MODEL:(k + 1) * D_MODEL] for k in range(3))
        @pl.when(pl.program_id(0) == 0)
        def _():
            for r in (donw_ref, dsg_ref, dsb_ref, dws_ref, dbst_ref):
                r[...] = jnp.zeros_like(r)

        onw = onw_ref[...]
        donw = jnp.zeros((1, D_HEAD), F32)
        for h in range(N_HEADS):
            sl = slice(h * D_HEAD, (h + 1) * D_HEAD)
            _, vj = jax.vjp(_ya_head, o_ref[:, sl], z_ref[:, sl], onw)
            do_h, dz_h, donw_h = vj(dya_ref[:, sl])
            do_ref[:, sl] = do_h.astype(BF)
            dz_ref[:, sl] = dz_h.astype(BF)
            donw = donw + donw_h
        donw_ref[...] += _bcast_rows(donw)

        (ua, vl), vj = jax.vjp(_sgu_pre, u_ref[...], vg_ref[...], sg_ref[...], sb_ref[...])
        wsm = _ws_masked(ws_ref[...])
        bst = bst_ref[...]
        lane = _iota((SGU_BLOCK, LANES), 1)
        dbst = jnp.zeros((SGU_BLOCK, LANES), F32)
        cmask = _chunk_causal((SGU_BLOCK, SGU_BLOCK), 0, 1)
        for gi in range(SGU_GROUPS):
            cs = slice(gi * D_HEAD, (gi + 1) * D_HEAD)
            wg = wsm[gi]
            wgt = jnp.transpose(wg)
            dwg = jnp.zeros((SGU_BLOCK, SGU_BLOCK), F32)
            for blk in range(tm // SGU_BLOCK):
                rs = slice(blk * SGU_BLOCK, (blk + 1) * SGU_BLOCK)
                sp = _dot(wg, vl[rs, cs]) + bst[:, gi:gi + 1]
                dyb = dyb_ref[rs, cs]
                dsp = dyb * ua[rs, cs]
                dua_scr[rs, cs] = dyb * sp
                dvl_scr[rs, cs] = _dot(wgt, dsp)
                dwg = dwg + _dot_nt(dsp, vl[rs, cs])
                dbst = dbst + jnp.where(lane == gi, jnp.sum(dsp, -1, keepdims=True), 0.0)
            dws_ref[gi] += jnp.where(cmask, dwg, 0.0)
        dbst_ref[...] += dbst
        du, dvg, dsg, dsb = vj((dua_scr[...], dvl_scr[...]))
        du_ref[...] = du.astype(BF)
        dvg_ref[...] = dvg.astype(BF)
        dsg_ref[...] += _bcast_rows(dsg)
        dsb_ref[...] += _bcast_rows(dsb)

    blk = lambda col: pl.BlockSpec((tm, D_MODEL), lambda i: (i, col))
    full = lambda shape: pl.BlockSpec(shape, lambda i: (0,) * len(shape))
    est = 16 * _nbytes((tm, D_MODEL), F32)
    outs = pl.pallas_call(
        body, name="mix_prep_bwd", grid=(S // tm,),
        in_specs=[blk(0), blk(3), blk(4), blk(5), full((1, D_HEAD)), full((1, D_MODEL)), full((1, D_MODEL)),
                  full((SGU_GROUPS, SGU_BLOCK, SGU_BLOCK)), full((SGU_BLOCK, LANES)), blk(0), blk(0),
                  pl.BlockSpec(memory_space=pl.ANY)],
        out_specs=[blk(0), pl.BlockSpec((tm, 3 * D_MODEL), lambda i: (i, 1)),
                   full((SUBLANES, D_HEAD)), full((SUBLANES, D_MODEL)), full((SUBLANES, D_MODEL)),
                   full((SGU_GROUPS, SGU_BLOCK, SGU_BLOCK)), full((SGU_BLOCK, LANES))],
        out_shape=[jax.ShapeDtypeStruct((S, D_MODEL), BF), jax.ShapeDtypeStruct(dproj.shape, BF),
                   jax.ShapeDtypeStruct((SUBLANES, D_HEAD), F32), jax.ShapeDtypeStruct((SUBLANES, D_MODEL), F32),
                   jax.ShapeDtypeStruct((SUBLANES, D_MODEL), F32),
                   jax.ShapeDtypeStruct((SGU_GROUPS, SGU_BLOCK, SGU_BLOCK), F32),
                   jax.ShapeDtypeStruct((SGU_BLOCK, LANES), F32)],
        input_output_aliases={11: 1},
        scratch_shapes=[pltpu.VMEM((tm, D_MODEL), F32)] * 2,
        compiler_params=_cparams(est, ("arbitrary",)),
    )(o, proj, proj, proj, onw, sg, sb, ws, bst, dya, dyb, dproj)
    return outs


def _gate_merge(pa, pb, proj, *, tm=512):
    S = pa.shape[0]
    tm = _tile(S, tm, SUBLANES * 2)

    def body(pa_ref, pb_ref, ga_ref, gb_ref, m_ref):
        pa, pb = pa_ref[...].astype(F32), pb_ref[...].astype(F32)
        m_ref[...] = (_sigmoid(ga_ref[...]) * pa + _sigmoid(gb_ref[...]) * pb).astype(BF)

    blk = lambda col: pl.BlockSpec((tm, D_MODEL), lambda i: (i, col))
    return pl.pallas_call(
        body, name="gate_merge", grid=(S // tm,),
        in_specs=[blk(0), blk(0), blk(6), blk(7)], out_specs=blk(0),
        out_shape=jax.ShapeDtypeStruct((S, D_MODEL), BF),
        compiler_params=_cparams(6 * _nbytes((tm, D_MODEL), F32), ("parallel",)),
    )(pa, pb, proj, proj)


def _gate_merge_bwd(pa, pb, proj, dm, *, tm=512):
    S = pa.shape[0]
    tm = _tile(S, tm, SUBLANES * 2)

    def body(pa_ref, pb_ref, ga_ref, gb_ref, dm_ref, dpa_ref, dpb_ref, dg_ref):
        dm = dm_ref[...].astype(F32)
        sa, sb = _sigmoid(ga_ref[...]), _sigmoid(gb_ref[...])
        dpa_ref[...] = (dm * sa).astype(BF)
        dpb_ref[...] = (dm * sb).astype(BF)
        dg_ref[:, :D_MODEL] = (dm * pa_ref[...].astype(F32) * sa * (1.0 - sa)).astype(BF)
        dg_ref[:, D_MODEL:] = (dm * pb_ref[...].astype(F32) * sb * (1.0 - sb)).astype(BF)

    blk = lambda col: pl.BlockSpec((tm, D_MODEL), lambda i: (i, col))
    return pl.pallas_call(
        body, name="gate_merge_bwd", grid=(S // tm,),
        in_specs=[blk(0), blk(0), blk(6), blk(7), blk(0)],
        out_specs=[blk(0), blk(0), pl.BlockSpec((tm, 2 * D_MODEL), lambda i: (i, 3))],
        out_shape=[jax.ShapeDtypeStruct((S, D_MODEL), BF)] * 2 + [jax.ShapeDtypeStruct((S, 8 * D_MODEL), BF)],
        compiler_params=_cparams(10 * _nbytes((tm, D_MODEL), F32), ("parallel",)),
    )(pa, pb, proj, proj, dm)


def _swiglu_act(hgu, *, tm=256):
    S = hgu.shape[0]
    tm = _tile(S, tm, SUBLANES * 2)

    def body(hg_ref, hu_ref, h_ref):
        h_ref[...] = (_silu(hg_ref[...].astype(F32)) * hu_ref[...].astype(F32)).astype(BF)

    blk = lambda col: pl.BlockSpec((tm, FFN_K), lambda i: (i, col))
    return pl.pallas_call(
        body, name="swiglu_act", grid=(S // tm,),
        in_specs=[blk(0), blk(1)], out_specs=blk(0),
        out_shape=jax.ShapeDtypeStruct((S, FFN_K), BF),
        compiler_params=_cparams(5 * _nbytes((tm, FFN_K), F32), ("parallel",)),
    )(hgu, hgu)


def _swiglu_bwd(hgu, dh, *, tm=256):
    S = hgu.shape[0]
    tm = _tile(S, tm, SUBLANES * 2)

    def body(hg_ref, hu_ref, dh_ref, d_ref):
        hg, dh = hg_ref[...].astype(F32), dh_ref[...].astype(F32)
        d_ref[:, :FFN_K] = (dh * hu_ref[...].astype(F32) * _dsilu(hg)).astype(BF)
        d_ref[:, FFN_K:] = (dh * _silu(hg)).astype(BF)

    blk = lambda col: pl.BlockSpec((tm, FFN_K), lambda i: (i, col))
    return pl.pallas_call(
        body, name="swiglu_bwd", grid=(S // tm,),
        in_specs=[blk(0), blk(1), blk(0)], out_specs=pl.BlockSpec((tm, 2 * FFN_K), lambda i: (i, 0)),
        out_shape=jax.ShapeDtypeStruct((S, 2 * FFN_K), BF),
        compiler_params=_cparams(8 * _nbytes((tm, FFN_K), F32), ("parallel",)),
    )(hgu, hgu, dh)


def _resid_ln(x, r, g, b, *, tm=512):
    S = x.shape[0]
    tm = _tile(S, tm, SUBLANES * 2)

    def body(x_ref, r_ref, g_ref, b_ref, y_ref, yb_ref):
        y = _ln(ALPHA * x_ref[...] + r_ref[...], g_ref[...], b_ref[...])
        y_ref[...] = y
        yb_ref[...] = y.astype(BF)

    blk = pl.BlockSpec((tm, D_MODEL), lambda i: (i, 0))
    vec = pl.BlockSpec((1, D_MODEL), lambda i: (0, 0))
    return pl.pallas_call(
        body, name="resid_ln", grid=(S // tm,),
        in_specs=[blk, blk, vec, vec], out_specs=[blk, blk],
        out_shape=[jax.ShapeDtypeStruct((S, D_MODEL), F32), jax.ShapeDtypeStruct((S, D_MODEL), BF)],
        compiler_params=_cparams(6 * _nbytes((tm, D_MODEL), F32), ("parallel",)),
    )(x, r, g, b)


def _resid_ln_bwd(x, r, g, b, dy, *, tm=512):
    S = x.shape[0]
    tm = _tile(S, tm, SUBLANES)

    def body(x_ref, r_ref, g_ref, b_ref, dy_ref, dp_ref, dg_ref, db_ref):
        @pl.when(pl.program_id(0) == 0)
        def _():
            dg_ref[...] = jnp.zeros_like(dg_ref)
            db_ref[...] = jnp.zeros_like(db_ref)

        _, vj = jax.vjp(_ln, ALPHA * x_ref[...] + r_ref[...], g_ref[...], b_ref[...])
        dp, dg, db = vj(dy_ref[...])
        dp_ref[...] = dp
        dg_ref[...] += _bcast_rows(dg)
        db_ref[...] += _bcast_rows(db)

    blk = pl.BlockSpec((tm, D_MODEL), lambda i: (i, 0))
    vec = pl.BlockSpec((1, D_MODEL), lambda i: (0, 0))
    acc = pl.BlockSpec((SUBLANES, D_MODEL), lambda i: (0, 0))
    return pl.pallas_call(
        body, name="resid_ln_bwd", grid=(S // tm,),
        in_specs=[blk, blk, vec, vec, blk], out_specs=[blk, acc, acc],
        out_shape=[jax.ShapeDtypeStruct((S, D_MODEL), F32)] + [jax.ShapeDtypeStruct((SUBLANES, D_MODEL), F32)] * 2,
        compiler_params=_cparams(10 * _nbytes((tm, D_MODEL), F32), ("arbitrary",)),
    )(x, r, g, b, dy)


def _loss_head(y, tgt, *, tm=512):
    S = y.shape[0]
    tm = _tile(S, tm, SUBLANES)

    def body(y_ref, t_ref, dy_ref, l_ref):
        @pl.when(pl.program_id(0) == 0)
        def _():
            l_ref[...] = jnp.zeros_like(l_ref)

        e = y_ref[...] - t_ref[...]
        dy_ref[...] = e * (1.0 / D_MODEL)
        l_ref[...] += 0.5 * jnp.sum(jnp.mean(e * e, -1, keepdims=True), keepdims=True)

    blk = pl.BlockSpec((tm, D_MODEL), lambda i: (i, 0))
    return pl.pallas_call(
        body, name="loss_head", grid=(S // tm,),
        in_specs=[blk, blk], out_specs=[blk, pl.BlockSpec((SUBLANES, LANES), lambda i: (0, 0))],
        out_shape=[jax.ShapeDtypeStruct((S, D_MODEL), F32), jax.ShapeDtypeStruct((SUBLANES, LANES), F32)],
        compiler_params=_cparams(6 * _nbytes((tm, D_MODEL), F32), ("arbitrary",)),
    )(y, tgt)


def _layer_fwd(x, xb, w, late):
    proj = _mm(xb, w["win"], mode="nn", name="mm_in", tm=1024, tn=1024)
    ba = _mm(xb, w["wba"], mode="nn", name="mm_in_ba", tm=1024, tn=LANES)
    qn, kn, vv, gb = _qkv_prep(proj, ba, w["convw"], w["arow"], w["dtrow"])
    o, st, tinv = _delta_fwd(qn, kn, vv, gb)
    ya, yb = _mix_prep(o, proj, w["onw"], w["sg"], w["sb"], w["ws"], w["bst"])
    w = {**w, **late(ya)}
    pa = _mm(ya, w["wpa"], mode="nn", name="mm_sq", out_dtype=BF)
    pb = _mm(yb, w["wpb"], mode="nn", name="mm_sq", out_dtype=BF)
    m = _gate_merge(pa, pb, proj)
    mix = _mm(m, w["wo"], mode="nn", name="mm_sq_res")
    x1, x1b = _resid_ln(x, mix, w["ln1g"], w["ln1b"])
    hgu = _mm(x1b, w["wgu"], mode="nn", name="mm_gu", tm=1024, tn=1536, out_dtype=BF)
    h = _swiglu_act(hgu)
    ffn = _mm(h, w["wd"], mode="nn", name="mm_down", tk=FFN_K)
    x2, x2b = _resid_ln(x1, ffn, w["ln2g"], w["ln2b"])
    saved = dict(x=x, xb=xb, proj=proj, ba=ba, qn=qn, kn=kn, vv=vv, gb=gb, o=o, st=st, tinv=tinv, ya=ya, yb=yb,
                 pa=pa, pb=pb, m=m, mix=mix, x1=x1, x1b=x1b, hgu=hgu, h=h, ffn=ffn)
    return x2, x2b, saved, w


def _layer_bwd(dx2, w, s, on_part=None):
    g = {}
    started = lambda part: on_part(part, g) if on_part is not None else None
    after = lambda v, token: v if token is None else v + token.astype(v.dtype)
    dpre2, g["ln2g"], g["ln2b"] = _resid_ln_bwd(s["x1"], s["ffn"], w["ln2g"], w["ln2b"], dx2)
    dh = _mm(dpre2, w["wd"], mode="nt", name="mm_nt_down", tn=1536, out_dtype=BF)
    g["wd"] = _mm(s["h"], dpre2, mode="tn", name="mm_tn_down", tm=1536, tk=512, out_dtype=BF)
    dhgu = _swiglu_bwd(s["hgu"], dh)
    dx1 = _mm(dhgu, w["wgu"], mode="nt", name="mm_nt_gu", add=dpre2, add_scale=ALPHA, tk=1536)
    g["wgu"] = _mm(s["x1b"], dhgu, mode="tn", name="mm_tn_gu", tm=1024, tn=1536, tk=512, out_dtype=BF)
    dpre1, g["ln1g"], g["ln1b"] = _resid_ln_bwd(s["x"], s["mix"], w["ln1g"], w["ln1b"], dx1)
    dm = _mm(dpre1, w["wo"], mode="nt", name="mm_nt_sq_bf", out_dtype=BF)
    g["wo"] = _mm(s["m"], dpre1, mode="tn", name="mm_tn_sq", tm=1024, tk=512, out_dtype=BF)
    dpa, dpb, dproj = _gate_merge_bwd(s["pa"], s["pb"], s["proj"], dm)
    dya = _mm(dpa, w["wpa"], mode="nt", name="mm_nt_sq")
    g["wpa"] = _mm(s["ya"], dpa, mode="tn", name="mm_tn_sq", tm=1024, tk=512, out_dtype=BF)
    dyb = _mm(dpb, w["wpb"], mode="nt", name="mm_nt_sq")
    g["wpb"] = _mm(s["yb"], dpb, mode="tn", name="mm_tn_sq", tm=1024, tk=512, out_dtype=BF)
    do, dproj, g["onw"], g["sg"], g["sb"], g["ws"], g["bst"] = _mix_prep_bwd(
        s["o"], s["proj"], after(w["onw"], started("late")), w["sg"], w["sb"], w["ws"], w["bst"], dya, dyb, dproj)
    dqn, dkn, dvv, dgb = _delta_bwd(s["qn"], s["kn"], s["vv"], s["gb"], s["st"], s["tinv"], do)
    dc, dba, g["convw"], g["arow"], g["dtrow"] = _qkv_prep_bwd(
        s["proj"], s["ba"], w["convw"], w["arow"], w["dtrow"], dqn, dkn, dvv, dgb)
    dproj = _conv_bwd(dc, w["convw"], dproj)
    g["win"] = _mm(s["xb"], dproj, mode="tn", name="mm_tn_in", tm=1024, tn=1024, tk=512, out_dtype=BF)
    g["wba"] = _mm(s["xb"], dba, mode="tn", name="mm_tn_ba", tm=1024, tn=LANES, tk=1024, out_dtype=BF)
    dx = _mm(dba, after(w["wba"], started("early")), mode="nt", name="mm_nt_ba", add=dpre1, add_scale=ALPHA, tm=1024)
    dx = _mm(dproj, w["win"], mode="nt", name="mm_nt_in", add=dx, add_scale=1.0, tk=1024)
    return dx, g


def _local_step(x, tgt, layers, on_grads=None):
    saved, weights = [], []
    xb = x.astype(BF)
    for layer in layers:
        x, xb, s, w = _layer_fwd(x, xb, *layer(x))
        saved.append(s)
        weights.append(w)
    dy, lacc = _loss_head(x, tgt)
    grads = [None] * len(layers)
    for l in reversed(range(len(layers))):
        on_part = functools.partial(on_grads, l) if on_grads is not None else None
        dy, grads[l] = _layer_bwd(dy, weights[l], saved[l], on_part)
    return lacc[0, 0], dy, grads


_QKVZ = 4 * D_MODEL
_BA = 2 * N_HEADS


WEIGHT_NAMES = ("w_in", "conv_w", "a_log", "dt_bias", "o_norm_w", "sgu_ln_g", "sgu_ln_b", "w_s", "b_s", "w_pa", "w_pb",
                "w_o", "ln1_g", "ln1_b", "w_ffn_gate", "w_ffn_up", "w_ffn_down", "ln2_g", "ln2_b")
WIRE = ("w_in", "w_ffn_gate", "w_ffn_up", "w_ffn_down", "w_pa", "w_pb", "w_o", "conv_w")
SMALL = (("a_log", N_HEADS), ("dt_bias", N_HEADS), ("o_norm_w", D_HEAD), ("sgu_ln_g", D_MODEL), ("sgu_ln_b", D_MODEL),
         ("w_s", SGU_GROUPS * SGU_BLOCK * SGU_BLOCK), ("b_s", SGU_GROUPS * SGU_BLOCK),
         ("ln1_g", D_MODEL), ("ln1_b", D_MODEL), ("ln2_g", D_MODEL), ("ln2_b", D_MODEL))
SMALL_ROWS = -(-sum(n for _, n in SMALL) // (LANES * SUBLANES)) * SUBLANES
N_MAIN_TILES = (N_IN - _BA) // D_MODEL
ADAM_TILES = dict(w_in=(128, "adamw_in"), w_ffn_gate=(256, "adamw_ffn_cols"), w_ffn_up=(256, "adamw_ffn_cols"),
                  w_ffn_down=(32, "adamw_ffn_rows"), w_pa=(128, "adamw_sq"), w_pb=(128, "adamw_sq"), w_o=(128, "adamw_sq"),
                  conv_w=(CONV_K, "adamw_conv"))


def _pad_to(a, axis, size):
    pads = [(0, 0)] * a.ndim
    pads[axis] = (0, size - a.shape[axis])
    return jnp.pad(a, pads)


def _wire_blocks(p):
    return dict(
        w_in=_pad_to(p["w_in"].astype(BF), 2, IN_PAD),
        w_ffn_gate=_pad_to(p["w_ffn_gate"].astype(BF), 2, FFN_PAD), w_ffn_up=_pad_to(p["w_ffn_up"].astype(BF), 2, FFN_PAD),
        w_ffn_down=_pad_to(p["w_ffn_down"].astype(BF), 1, FFN_PAD),
        w_pa=p["w_pa"].astype(BF), w_pb=p["w_pb"].astype(BF), w_o=p["w_o"].astype(BF),
        conv_w=_pad_to(p["conv_w"], 1, SUBLANES),
    )


def _by_columns(blocks):
    n, r, c = blocks.shape
    return jnp.transpose(blocks, (1, 0, 2)).reshape(r, n * c)


def _to_slots(full, c):
    r = full.shape[0]
    return jnp.transpose(full.reshape(r, N_DEV, c), (1, 0, 2))


def _lane_row(v, at):
    return jnp.pad(v[None], ((0, 0), (at, LANES - at - v.shape[0])))


EARLY = ("w_in", "conv_w")
LATE = ("w_pa", "w_pb", "w_o", "w_ffn_gate", "w_ffn_up", "w_ffn_down")


def _early_weights(stacks, p, l):
    return dict(
        win=_perm_in(stacks["w_in"], D_MODEL, N_MAIN_TILES), wba=_perm_in(stacks["w_in"], LANES, 1),
        convw=_by_columns(stacks["conv_w"][:, :CONV_K]),
        arow=_lane_row(p["a_log"][l], N_HEADS), dtrow=_lane_row(p["dt_bias"][l], N_HEADS),
        onw=p["o_norm_w"][l][None], sg=p["sgu_ln_g"][l][None], sb=p["sgu_ln_b"][l][None],
        ws=p["w_s"][l], bst=_pad_to(p["b_s"][l].T, 1, LANES),
        ln1g=p["ln1_g"][l][None], ln1b=p["ln1_b"][l][None], ln2g=p["ln2_g"][l][None], ln2b=p["ln2_b"][l][None],
    )


def _late_weights(stacks):
    return dict(
        wpa=stacks["w_pa"].reshape(D_MODEL, D_MODEL), wpb=stacks["w_pb"].reshape(D_MODEL, D_MODEL),
        wo=stacks["w_o"].reshape(D_MODEL, D_MODEL),
        wgu=_by_columns(jnp.concatenate([stacks["w_ffn_gate"], stacks["w_ffn_up"]], axis=0)),
        wd=stacks["w_ffn_down"].reshape(FFN_K, D_MODEL),
    )


def _small_pack(parts):
    flat = jnp.concatenate([parts[n].reshape(-1) for n, _ in SMALL])
    return _pad_to(flat, 0, SMALL_ROWS * LANES).reshape(SMALL_ROWS, LANES)


def _small_unpack(rows, like):
    flat, out, off = rows.reshape(-1), {}, 0
    for n, size in SMALL:
        out[n] = flat[off:off + size].reshape(like[n].shape[1:])
        off += size
    return out


def _late_slots(g):
    slots = dict(
        w_ffn_gate=_to_slots(g["wgu"][:, :FFN_K], FFN_PAD), w_ffn_up=_to_slots(g["wgu"][:, FFN_K:], FFN_PAD),
        w_ffn_down=g["wd"].reshape(N_DEV, FFN_PAD, D_MODEL),
        w_pa=g["wpa"].reshape(N_DEV, D_MODEL // N_DEV, D_MODEL), w_pb=g["wpb"].reshape(N_DEV, D_MODEL // N_DEV, D_MODEL),
        w_o=g["wo"].reshape(N_DEV, D_MODEL // N_DEV, D_MODEL),
    )
    return [slots[n] for n in LATE]


def _early_slots(g):
    slots = [_perm_out(g["win"], g["wba"]), _pad_to(_to_slots(g["convw"][:CONV_K], 3 * D_MODEL // N_DEV), 1, SUBLANES)]
    small = _small_pack(dict(
        a_log=g["arow"][0, N_HEADS:2 * N_HEADS], dt_bias=g["dtrow"][0, N_HEADS:2 * N_HEADS], o_norm_w=g["onw"][0],
        sgu_ln_g=g["sg"][0], sgu_ln_b=g["sb"][0], w_s=g["ws"], b_s=g["bst"][:, :SGU_GROUPS].T,
        ln1_g=g["ln1g"][0], ln1_b=g["ln1b"][0], ln2_g=g["ln2g"][0], ln2_b=g["ln2b"][0]))
    return slots, small


def _in_tile_start(j, tile_w):
    if tile_w == LANES:
        return jnp.int32(_QKVZ)
    return j * D_MODEL + jnp.where(j >= _QKVZ // D_MODEL, _BA, 0)


def _select(rows_iota, cols_iota, dev, start, valid):
    hit = (rows_iota + (dev * IN_BLOCK - start) == cols_iota) & (rows_iota < IN_BLOCK) & (cols_iota < valid)
    return jnp.where(hit, 1.0, 0.0).astype(BF)


def _perm_in(stack, tile_w, n_tiles):
    valid = _BA if tile_w == LANES else tile_w

    def first_dev(j):
        return lax.div(_in_tile_start(j, tile_w), jnp.int32(IN_BLOCK))

    def body(w_ref, o_ref, acc_ref):
        j, k = pl.program_id(0), pl.program_id(1)
        sel = _select(_iota((IN_PAD, tile_w), 0), _iota((IN_PAD, tile_w), 1), first_dev(j) + k,
                      _in_tile_start(j, tile_w), valid)
        part = jnp.dot(w_ref[0], sel, preferred_element_type=F32)

        @pl.when(k == 0)
        def _():
            acc_ref[...] = part

        @pl.when(k == 1)
        def _():
            o_ref[...] = (acc_ref[...] + part).astype(BF)

    est = _nbytes((D_MODEL, IN_PAD), BF) + 3 * _nbytes((D_MODEL, tile_w), F32) + 2 * _nbytes((IN_PAD, tile_w), F32)
    return pl.pallas_call(
        body, name="perm_in" if tile_w != LANES else "perm_in_ba", grid=(n_tiles, 2),
        in_specs=[pl.BlockSpec((1, D_MODEL, IN_PAD), lambda j, k: (jnp.minimum(first_dev(j) + k, N_DEV - 1), 0, 0))],
        out_specs=pl.BlockSpec((D_MODEL, tile_w), lambda j, k: (0, j)),
        out_shape=jax.ShapeDtypeStruct((D_MODEL, n_tiles * tile_w), BF),
        scratch_shapes=[pltpu.VMEM((D_MODEL, tile_w), F32)],
        compiler_params=_cparams(est, ("parallel", "arbitrary")),
    )(stack)


def _perm_out(dmain, dba):
    n_main = dmain.shape[1] // D_MODEL

    def body(dm_ref, db_ref, o_ref, acc_ref):
        d, t = pl.program_id(0), pl.program_id(1)

        @pl.when(t == 0)
        def _():
            acc_ref[...] = jnp.zeros_like(acc_ref)

        start = _in_tile_start(t, D_MODEL)
        overlaps = (start < (d + 1) * IN_BLOCK) & (d * IN_BLOCK < start + D_MODEL)

        @pl.when((t < n_main) & overlaps)
        def _():
            sel = _select(_iota((D_MODEL, IN_PAD), 1), _iota((D_MODEL, IN_PAD), 0), d, start, D_MODEL)
            acc_ref[...] += jnp.dot(dm_ref[...], sel, preferred_element_type=F32)

        @pl.when(t == n_main)
        def _():
            sel = _select(_iota((LANES, IN_PAD), 1), _iota((LANES, IN_PAD), 0), d, jnp.int32(_QKVZ), _BA)
            o_ref[0] = (acc_ref[...] + jnp.dot(db_ref[...], sel, preferred_element_type=F32)).astype(BF)

    est = 2 * _nbytes((D_MODEL, D_MODEL), BF) + 4 * _nbytes((D_MODEL, IN_PAD), F32)
    return pl.pallas_call(
        body, name="perm_out", grid=(N_DEV, n_main + 1),
        in_specs=[pl.BlockSpec((D_MODEL, D_MODEL), lambda d, t: (0, jnp.minimum(t, n_main - 1))),
                  pl.BlockSpec((D_MODEL, LANES), lambda d, t: (0, 0))],
        out_specs=pl.BlockSpec((1, D_MODEL, IN_PAD), lambda d, t: (d, 0, 0)),
        out_shape=jax.ShapeDtypeStruct((N_DEV, D_MODEL, IN_PAD), BF),
        scratch_shapes=[pltpu.VMEM((D_MODEL, IN_PAD), F32)],
        compiler_params=_cparams(est, ("parallel", "arbitrary")),
    )(dmain, dba)


def _mesh_place():
    x, y, c = (lax.axis_index(a) for a in MESH_AXES)
    return x, y, c


def _slot(x, y, c):
    return 4 * x + 2 * y + c


def _peer(place, j):
    x, y, c = place
    return (1 - x if j & 4 else x, 1 - y if j & 2 else y, 1 - c if j & 1 else c)


_HBM = pl.BlockSpec(memory_space=pltpu.HBM)
_SEM = pl.BlockSpec(memory_space=pltpu.SEMAPHORE)
_EFFECT = pltpu.SideEffectType.DATAFLOW_SIDE_EFFECTING


def _remote_copy(src_ref, land_ref, slot, per_slot, pslot, sems, u, j, peer):
    return pltpu.make_async_remote_copy(
        src_ref=src_ref.at[pslot] if per_slot else src_ref, dst_ref=land_ref.at[slot],
        send_sem=sems[0].at[u * (N_DEV - 1) + j - 1], recv_sem=sems[1].at[u * (N_DEV - 1) + j - 1],
        device_id=peer, device_id_type=pl.DeviceIdType.MESH)


def _own_copy(src_ref, land_ref, me, per_slot, sems, u):
    return pltpu.make_async_copy(src_ref.at[me] if per_slot else src_ref, land_ref.at[me], sems[2].at[u])


def _exchange_start(name, srcs, per_slot):
    n = len(srcs)
    lands = [jax.ShapeDtypeStruct(s.shape if p else (N_DEV,) + s.shape, s.dtype) for s, p in zip(srcs, per_slot)]

    def body(*refs):
        src_refs, sems, land_refs, token = refs[:n], refs[n:n + 3], refs[2 * n + 3:3 * n + 3], refs[-1]
        place = _mesh_place()
        me = _slot(*place)
        for u in range(n):
            _own_copy(src_refs[u], land_refs[u], me, per_slot[u], sems, u).start()
            for j in range(1, N_DEV):
                peer = _peer(place, j)
                _remote_copy(src_refs[u], land_refs[u], me, per_slot[u], _slot(*peer), sems, u, j, peer).start()
        token[...] = jnp.zeros_like(token)

    hbm = lambda a: pltpu.HBM(a.shape, a.dtype)
    sem = pltpu.SemaphoreType.DMA((n * (N_DEV - 1),))
    outs = pl.pallas_call(
        body, name=name,
        out_shape=(sem, sem, pltpu.SemaphoreType.DMA((n,)), *[hbm(a) for a in srcs], *[hbm(a) for a in lands],
                   jax.ShapeDtypeStruct((SUBLANES, LANES), F32)),
        in_specs=[_HBM] * n, out_specs=(_SEM, _SEM, _SEM, *[_HBM] * (2 * n), pl.BlockSpec(memory_space=pltpu.VMEM)),
        input_output_aliases={i: 3 + i for i in range(n)},
        compiler_params=pltpu.CompilerParams(has_side_effects=_EFFECT),
    )(*[pltpu.with_memory_space_constraint(a, pltpu.HBM) for a in srcs])
    return tuple(outs[:3]), list(outs[3:3 + n]), list(outs[3 + n:3 + 2 * n]), outs[-1]


def _exchange_wait(name, sems, srcs, lands, units, per_slot, after):
    m = len(units)

    def body(*refs):
        src_refs, land_refs, sem_refs = refs[:m], refs[m:2 * m], refs[2 * m:2 * m + 3]
        place = _mesh_place()
        me = _slot(*place)
        for i, u in enumerate(units):
            _own_copy(src_refs[i], land_refs[i], me, per_slot[u], sem_refs, u).wait()
            for j in range(1, N_DEV):
                peer = _peer(place, j)
                pslot = _slot(*peer)
                cp = _remote_copy(src_refs[i], land_refs[i], pslot, per_slot[u], pslot, sem_refs, u, j, peer)
                cp.wait_send()
                cp.wait_recv()

    hbm = lambda a: pltpu.HBM(a.shape, a.dtype)
    outs = pl.pallas_call(
        body, name=name, out_shape=tuple(hbm(a) for a in list(srcs) + list(lands)),
        in_specs=[_HBM] * (2 * m) + [_SEM] * 3 + [pl.BlockSpec(memory_space=pl.ANY)], out_specs=tuple([_HBM] * (2 * m)),
        input_output_aliases={i: i for i in range(2 * m)},
        compiler_params=pltpu.CompilerParams(has_side_effects=_EFFECT),
    )(*srcs, *lands, *sems, after)
    return list(outs[m:])


def _adam_update(g, w, m, v):
    m = ADAM_B1 * m + (1.0 - ADAM_B1) * g
    v = ADAM_B2 * v + (1.0 - ADAM_B2) * jnp.square(g)
    m_hat = m / (1.0 - ADAM_B1 ** ADAM_STEP)
    v_hat = v / (1.0 - ADAM_B2 ** ADAM_STEP)
    return -ADAM_LR * (m_hat / (jnp.sqrt(v_hat) + ADAM_EPS) + ADAM_WD * w), m, v


def _adamw(recvs, w, m, v, *, tr, name):
    L, R, C = w.shape
    rp = max(tr, SUBLANES * (4 // jnp.dtype(recvs[0].dtype).itemsize))
    Cp = recvs[0].shape[2]

    def body(*refs):
        r_refs, (w_ref, m_ref, v_ref, g_ref, d_ref, nm_ref, nv_ref) = refs[:L], refs[L:]
        for l in range(L):
            @pl.when(pl.program_id(0) == l)
            def _(r_ref=r_refs[l]):
                g = r_ref[0, :tr, :C].astype(F32)
                for s in range(1, N_DEV):
                    g = g + r_ref[s, :tr, :C].astype(F32)
                d, nm, nv = _adam_update(g, w_ref[0], m_ref[0], v_ref[0])
                g_ref[0], d_ref[0], nm_ref[0], nv_ref[0] = g, d, nm, nv

    blk = pl.BlockSpec((1, tr, C), lambda l, i: (l, i, 0))
    r_specs = [pl.BlockSpec((N_DEV, rp, Cp), lambda l, i, k=k: (0, jnp.where(l == k, i, 0), 0)) for k in range(L)]
    est = 2 * _nbytes((N_DEV, rp, Cp), recvs[0].dtype) + 8 * _nbytes((tr, Cp), F32)
    return pl.pallas_call(
        body, name=name, grid=(L, R // tr),
        in_specs=r_specs + [blk] * 3, out_specs=[blk] * 4,
        out_shape=[jax.ShapeDtypeStruct((L, R, C), F32)] * 4,
        compiler_params=_cparams(est, ("arbitrary", "arbitrary")),
    )(*recvs, w, m, v)


def _adamw_small(recv, w, m, v):
    def body(r_ref, w_ref, m_ref, v_ref, g_ref, d_ref, nm_ref, nv_ref):
        g = r_ref[0]
        for s in range(1, N_DEV):
            g = g + r_ref[s]
        g_ref[...] = g
        d_ref[...], nm_ref[...], nv_ref[...] = _adam_update(g, w_ref[...], m_ref[...], v_ref[...])

    vm = pl.BlockSpec(memory_space=pltpu.VMEM)
    return pl.pallas_call(
        body, name="adamw_small", in_specs=[vm] * 4, out_specs=[vm] * 4,
        out_shape=[jax.ShapeDtypeStruct((SMALL_ROWS, LANES), F32)] * 4,
        compiler_params=_cparams(20 * _nbytes((SMALL_ROWS, LANES), F32)),
    )(recv, w, m, v)


def kernel(x, w_in, conv_w, a_log, dt_bias, o_norm_w, sgu_ln_g, sgu_ln_b, w_s, b_s, w_pa, w_pb, w_o, ln1_g, ln1_b, w_ffn_gate, w_ffn_up, w_ffn_down, ln2_g, ln2_b, loss_target, m_w_in, m_conv_w, m_a_log, m_dt_bias, m_o_norm_w, m_sgu_ln_g, m_sgu_ln_b, m_w_s, m_b_s, m_w_pa, m_w_pb, m_w_o, m_ln1_g, m_ln1_b, m_w_ffn_gate, m_w_ffn_up, m_w_ffn_down, m_ln2_g, m_ln2_b, v_w_in, v_conv_w, v_a_log, v_dt_bias, v_o_norm_w, v_sgu_ln_g, v_sgu_ln_b, v_w_s, v_b_s, v_w_pa, v_w_pb, v_w_o, v_ln1_g, v_ln1_b, v_w_ffn_gate, v_w_ffn_up, v_w_ffn_down, v_ln2_g, v_ln2_b):
    given = dict(locals())
    P = {n: given[n] for n in WEIGHT_NAMES}
    M = {n: given["m_" + n] for n in WEIGHT_NAMES}
    V = {n: given["v_" + n] for n in WEIGHT_NAMES}

    wire = _wire_blocks(P)
    units = [(n, l) for l in range(DEPTH) for n in EARLY + LATE]
    whole = [False] * len(units)
    g_sems, g_srcs, g_lands, g_token = _exchange_start("gather_start", [wire[n][l] for n, l in units], whole)

    def gathered(name, names, l, after):
        idx = [units.index((n, l)) for n in names]
        got = _exchange_wait(name, g_sems, [g_srcs[i] for i in idx], [g_lands[i] for i in idx], idx, whole, after)
        return dict(zip(names, got))

    def layer(l):
        def weights(x_in):
            after = g_token if l == 0 else x_in
            early = _early_weights(gathered(f"gather_wait_early{l}", EARLY, l, after), P, l)
            return early, lambda ya: _late_weights(gathered(f"gather_wait_late{l}", LATE, l, ya))
        return weights

    pending = {}

    def on_grads(l, part, g):
        if part == "late":
            srcs, names = _late_slots(g), LATE
            per_slot = [True] * len(srcs)
        else:
            slots, small = _early_slots(g)
            srcs, names = slots + [small], EARLY + ("small",)
            per_slot = [True] * len(slots) + [False]
        sems, s_thru, l_thru, token = _exchange_start(f"exchange_start_{part}{l}", srcs, per_slot)
        pending[l, part] = (names, sems, s_thru, l_thru, per_slot)
        return token[0, 0]

    loss_local, dx, _ = _local_step(x[0], loss_target[0], [layer(l) for l in range(DEPTH)], on_grads)
    loss = lax.psum(loss_local, MESH_AXES)

    recv = [{} for _ in range(DEPTH)]
    for l in reversed(range(DEPTH)):
        for part in ("late", "early"):
            names, sems, s_thru, l_thru, per_slot = pending[l, part]
            got = _exchange_wait(f"exchange_wait_{part}{l}", sems, s_thru, l_thru, list(range(len(s_thru))), per_slot, dx)
            recv[l].update(zip(names, got))

    out = {}
    for n in WIRE:
        tr, name = ADAM_TILES[n]
        out[n] = _adamw([recv[l][n] for l in range(DEPTH)], P[n], M[n], V[n], tr=tr, name=name)
    small = [_adamw_small(recv[l]["small"], *[_small_pack({n: T[n][l] for n, _ in SMALL}) for T in (P, M, V)])
             for l in range(DEPTH)]
    for n, _ in SMALL:
        out[n] = [jnp.stack([_small_unpack(small[l][i], P)[n] for l in range(DEPTH)]) for i in range(4)]
    return (loss, dx[None], *[out[n][i] for i in range(4) for n in WEIGHT_NAMES])
```

```python
import functools
import math

import jax
import jax.numpy as jnp
from jax import lax
from jax.experimental import pallas as pl
from jax.experimental.pallas import tpu as pltpu

F32 = jnp.float32
BF = jnp.bfloat16
HIGHEST = lax.Precision.HIGHEST

D_MODEL = 1024
DEPTH = 2
N_HEADS = 8
D_HEAD = 128
CONV_K = 4
SGU_BLOCK = 128
SGU_GROUPS = 8
SGU_CHUNK = 64
FFN_HIDDEN = 2816
N_IN = 8208
N_DEV = 8
IN_BLOCK, IN_PAD = N_IN // N_DEV, 1152
FFN_BLOCK, FFN_PAD = FFN_HIDDEN // N_DEV, 384
FFN_K = N_DEV * FFN_PAD
ALPHA = (2 * DEPTH) ** 0.25
LN_EPS = 1e-5
RMS_EPS = 1e-6
ADAM_LR, ADAM_B1, ADAM_B2, ADAM_EPS, ADAM_WD, ADAM_STEP = 0.001, 0.9, 0.999, 1e-08, 0.01, 10

MESH_AXES = ("x", "y", "c")
DELTA_CHUNK = 128
DELTA_HEADS_PER_STEP = 8
LANES = 128
SUBLANES = 8
VMEM_BYTES = 64 * 1024 * 1024
HALO = SUBLANES
HALO_BF = 2 * SUBLANES


def _cparams(est_bytes, dims=None):
    limit = int(min(max(2 * est_bytes + (8 << 20), 32 << 20), VMEM_BYTES - (6 << 20)))
    kw = dict(vmem_limit_bytes=limit)
    if dims is not None:
        kw["dimension_semantics"] = dims
    return pltpu.CompilerParams(**kw)


def _nbytes(shape, dtype):
    return math.prod(shape) * jnp.dtype(dtype).itemsize


def _dims(kind, ndim):
    lhs, rhs = {"nn": (1, 0), "nt": (1, 1), "tn": (0, 0)}[kind]
    b = ndim - 2
    return (((lhs + b,), (rhs + b,)), (tuple(range(b)), tuple(range(b))))


def _mxu(a, b, kind):
    return lax.dot_general(a, b, _dims(kind, a.ndim), preferred_element_type=F32)


def _dot(a, b):
    return _mxu(a.astype(BF), b.astype(BF), "nn")


def _dot_nt(a, b):
    return _mxu(a.astype(BF), b.astype(BF), "nt")


def _dot_tn(a, b):
    return _mxu(a.astype(BF), b.astype(BF), "tn")


def _split(a):
    hi = a.astype(BF)
    return hi, (a - hi.astype(F32)).astype(BF)


def _dot3(a, b, kind):
    (ah, al), (bh, bl) = _split(a), _split(b)
    return _mxu(ah, bh, kind) + (_mxu(ah, bl, kind) + _mxu(al, bh, kind))


def _dotf(a, b):
    return _dot3(a, b, "nn")


def _dotf_nt(a, b):
    return _dot3(a, b, "nt")


def _dotf_tn(a, b):
    return _dot3(a, b, "tn")


def _dot01(sel, x, kind="nn"):
    s = jnp.broadcast_to(sel.astype(BF), x.shape[:-2] + sel.shape)
    h1 = x.astype(BF)
    r1 = x - h1.astype(F32)
    h2 = r1.astype(BF)
    h3 = (r1 - h2.astype(F32)).astype(BF)
    return _mxu(s, h1, kind) + (_mxu(s, h2, kind) + _mxu(s, h3, kind))


def _sigmoid(x):
    return 0.5 * jnp.tanh(0.5 * x) + 0.5


def _silu(x):
    return x * _sigmoid(x)


def _silu_and_grad(x):
    s = _sigmoid(x)
    return x * s, s * (1.0 + x * (1.0 - s))


def _gelu(x):
    return 0.5 * x * (1.0 + lax.erf(x * 0.7071067811865476))


def _softplus(x):
    return jnp.maximum(x, 0.0) + jnp.log1p(jnp.exp(-jnp.abs(x)))


def _ln(x, g, b):
    mu = jnp.mean(x, -1, keepdims=True)
    xc = x - mu
    var = jnp.mean(xc * xc, -1, keepdims=True)
    return xc * lax.rsqrt(var + LN_EPS) * g + b


def _iota(shape, dim):
    return lax.broadcasted_iota(jnp.int32, shape, dim)


def _tile(n, pref, align):
    if n <= pref:
        return n
    t = (pref // align) * align
    while t >= align:
        if n % t == 0:
            return t
        t -= align
    raise ValueError(f"no tile for {n} (pref {pref}, align {align})")


def _bcast_rows(v, rows=SUBLANES):
    return jnp.broadcast_to(v, (rows, v.shape[-1]))


def _mm(a, b, *, mode, name, out_dtype=F32, add=None, add_scale=1.0, tm=512, tn=1024, tk=1024):
    if mode == "nn":
        (M, K), N = a.shape, b.shape[1]
    elif mode == "nt":
        (M, K), N = a.shape, b.shape[0]
    else:
        (K, M), N = a.shape, b.shape[1]
    tm = _tile(M, tm, LANES if mode == "tn" else SUBLANES * 2)
    tn = _tile(N, tn, LANES)
    tk = _tile(K, tk, LANES)
    nk = K // tk
    if mode == "nn":
        a_spec = pl.BlockSpec((tm, tk), lambda i, j, k: (i, k))
        b_spec = pl.BlockSpec((tk, tn), lambda i, j, k: (k, j))
        dot = _dot
    elif mode == "nt":
        a_spec = pl.BlockSpec((tm, tk), lambda i, j, k: (i, k))
        b_spec = pl.BlockSpec((tn, tk), lambda i, j, k: (j, k))
        dot = _dot_nt
    else:
        a_spec = pl.BlockSpec((tk, tm), lambda i, j, k: (k, i))
        b_spec = pl.BlockSpec((tk, tn), lambda i, j, k: (k, j))
        dot = _dot_tn
    o_spec = pl.BlockSpec((tm, tn), lambda i, j, k: (i, j))
    has_add = add is not None

    def body(*refs):
        if has_add:
            a_ref, b_ref, add_ref, o_ref, acc_ref = refs
        else:
            a_ref, b_ref, o_ref, acc_ref = refs
            add_ref = None
        k = pl.program_id(2)
        part = dot(a_ref[...], b_ref[...])

        def finish(total):
            if has_add:
                total = total + add_scale * add_ref[...]
            o_ref[...] = total.astype(out_dtype)

        if nk == 1:
            finish(part)
        else:
            @pl.when(k == 0)
            def _():
                acc_ref[...] = part

            @pl.when(jnp.logical_and(k > 0, k < nk - 1))
            def _():
                acc_ref[...] += part

            @pl.when(k == nk - 1)
            def _():
                finish(acc_ref[...] + part)

    in_specs = [a_spec, b_spec] + ([o_spec] if has_add else [])
    args = (a, b) + ((add,) if has_add else ())
    est = (_nbytes((tm, tk), a.dtype) + _nbytes((tk, tn), b.dtype) + 2 * _nbytes((tm, tn), F32)
           + (_nbytes((tm, tn), F32) if has_add else 0)) + 2 * _nbytes((tm, tn), F32)
    return pl.pallas_call(
        body, name=name,
        grid=(M // tm, N // tn, nk),
        in_specs=in_specs, out_specs=o_spec,
        out_shape=jax.ShapeDtypeStruct((M, N), out_dtype),
        scratch_shapes=[pltpu.VMEM((tm, tn) if nk > 1 else (SUBLANES, LANES), F32)],
        compiler_params=_cparams(est, ("parallel", "parallel", "arbitrary")),
    )(*args)


def _conv_taps(xt, halo, w_ref, first):
    halo = jnp.where(first, 0.0, halo)
    xc = jnp.concatenate([halo, xt], axis=0)
    shifted = [xt] + [pltpu.roll(xc, s, 0)[HALO:] for s in range(1, CONV_K)]
    out = shifted[0] * w_ref[CONV_K - 1:CONV_K, :]
    for s in range(1, CONV_K):
        out = out + shifted[s] * w_ref[CONV_K - 1 - s:CONV_K - s, :]
    return out, shifted


def _gates(ba, arow, dtrow):
    lane = _iota(ba.shape, 1)
    beta = _sigmoid(ba)
    g = -jnp.exp(arow) * _softplus(ba + dtrow)
    return jnp.where(lane < N_HEADS, beta, jnp.where(lane < 2 * N_HEADS, g, 0.0))


def _l2n(x):
    return x * lax.rsqrt(jnp.sum(x * x, -1, keepdims=True) + RMS_EPS)


def _qkv_prep(proj, ba, convw, arow, dtrow, *, tm=256):
    S = proj.shape[0]
    tm = _tile(S, tm, SUBLANES)
    W3 = 3 * D_MODEL
    hb = tm // HALO

    def body(xt_ref, halo_ref, ba_ref, w_ref, a_ref, dt_ref, q_ref, k_ref, v_ref, gb_ref):
        c, _ = _conv_taps(xt_ref[...], halo_ref[...], w_ref, pl.program_id(0) == 0)
        c = _silu(c)
        for h in range(N_HEADS):
            lo = h * D_HEAD
            q_ref[:, lo:lo + D_HEAD] = _l2n(c[:, lo:lo + D_HEAD])
            k_ref[:, lo:lo + D_HEAD] = _l2n(c[:, D_MODEL + lo:D_MODEL + lo + D_HEAD])
        v_ref[...] = c[:, 2 * D_MODEL:]
        gb_ref[...] = _gates(ba_ref[...], a_ref[...], dt_ref[...])

    row = lambda w, col=0: pl.BlockSpec((tm, w), lambda i: (i, col))
    full = lambda shape: pl.BlockSpec(shape, lambda i: (0,) * len(shape))
    est = 4 * _nbytes((tm, W3), F32)
    return pl.pallas_call(
        body, name="qkv_prep", grid=(S // tm,),
        in_specs=[row(W3), pl.BlockSpec((HALO, W3), lambda i: (jnp.maximum(i * hb - 1, 0), 0)), row(LANES),
                  full((CONV_K, W3)), full((1, LANES)), full((1, LANES))],
        out_specs=[row(D_MODEL), row(D_MODEL), row(D_MODEL), row(LANES)],
        out_shape=[jax.ShapeDtypeStruct((S, D_MODEL), F32)] * 3 + [jax.ShapeDtypeStruct((S, LANES), F32)],
        compiler_params=_cparams(est, ("arbitrary",)),
    )(proj, proj, ba, convw, arow, dtrow)


def _qkv_prep_bwd(proj, ba, convw, arow, dtrow, dq, dk, dv, dgb, *, tm=256):
    S = proj.shape[0]
    tm = _tile(S, tm, SUBLANES * 2)
    W3 = 3 * D_MODEL
    hb = tm // HALO

    def body(xt_ref, halo_ref, ba_ref, w_ref, a_ref, dt_ref, dq_ref, dk_ref, dv_ref, dgb_ref,
             dcb_ref, dba_ref, dw_ref, da_ref, ddt_ref, dc_ref):
        i = pl.program_id(0)

        @pl.when(i == 0)
        def _():
            dw_ref[...] = jnp.zeros_like(dw_ref)
            da_ref[...] = jnp.zeros_like(da_ref)
            ddt_ref[...] = jnp.zeros_like(ddt_ref)

        c, shifted = _conv_taps(xt_ref[...], halo_ref[...], w_ref, i == 0)
        a, ds = _silu_and_grad(c)
        for h in range(N_HEADS):
            for base, d_ref in ((0, dq_ref), (D_MODEL, dk_ref)):
                lo = base + h * D_HEAD
                _, vj = jax.vjp(_l2n, a[:, lo:lo + D_HEAD])
                (dx,) = vj(d_ref[:, h * D_HEAD:(h + 1) * D_HEAD])
                dc_ref[:, lo:lo + D_HEAD] = dx * ds[:, lo:lo + D_HEAD]
        dc_ref[:, 2 * D_MODEL:] = dv_ref[...] * ds[:, 2 * D_MODEL:]
        dc = dc_ref[...]
        dcb_ref[...] = dc.astype(BF)
        for s in range(CONV_K):
            kk = CONV_K - 1 - s
            dw_ref[kk:kk + 1, :] += jnp.sum(dc * shifted[s], axis=0, keepdims=True)
        _, vj = jax.vjp(_gates, ba_ref[...], a_ref[...], dt_ref[...])
        dba, da, ddt = vj(dgb_ref[...])
        dba_ref[...] = dba.astype(BF)
        da_ref[...] += _bcast_rows(da)
        ddt_ref[...] += _bcast_rows(ddt)

    row = lambda w, col=0: pl.BlockSpec((tm, w), lambda i: (i, col))
    full = lambda shape: pl.BlockSpec(shape, lambda i: (0,) * len(shape))
    est = 8 * _nbytes((tm, W3), F32)
    return pl.pallas_call(
        body, name="qkv_prep_bwd", grid=(S // tm,),
        in_specs=[row(W3), pl.BlockSpec((HALO, W3), lambda i: (jnp.maximum(i * hb - 1, 0), 0)), row(LANES),
                  full((CONV_K, W3)), full((1, LANES)), full((1, LANES)),
                  row(D_MODEL), row(D_MODEL), row(D_MODEL), row(LANES)],
        out_specs=[row(W3), row(LANES), full((SUBLANES, W3)), full((SUBLANES, LANES)), full((SUBLANES, LANES))],
        out_shape=[jax.ShapeDtypeStruct((S, W3), BF), jax.ShapeDtypeStruct((S, LANES), BF),
                   jax.ShapeDtypeStruct((SUBLANES, W3), F32), jax.ShapeDtypeStruct((SUBLANES, LANES), F32),
                   jax.ShapeDtypeStruct((SUBLANES, LANES), F32)],
        scratch_shapes=[pltpu.VMEM((tm, W3), F32)],
        compiler_params=_cparams(est, ("arbitrary",)),
    )(proj, proj, ba, convw, arow, dtrow, dq, dk, dv, dgb)


def _conv_bwd(dc, convw, dproj, *, tm=256):
    S, W3 = dc.shape
    tm = _tile(S, tm, HALO_BF)
    hb = tm // HALO_BF
    nt = S // tm

    def body(dc_ref, nxt_ref, w_ref, dproj_ref, o_ref):
        last = pl.program_id(0) == nt - 1
        nxt = jnp.where(last, 0.0, nxt_ref[...].astype(F32))
        cur = dc_ref[...].astype(F32)
        xc = jnp.concatenate([cur, nxt], axis=0)
        out = cur * w_ref[CONV_K - 1:CONV_K, :]
        for s in range(1, CONV_K):
            out = out + pltpu.roll(xc, tm + HALO_BF - s, 0)[:tm] * w_ref[CONV_K - 1 - s:CONV_K - s, :]
        o_ref[...] = out.astype(BF)

    est = 5 * _nbytes((tm, W3), F32)
    return pl.pallas_call(
        body, name="conv_bwd", grid=(nt,),
        in_specs=[pl.BlockSpec((tm, W3), lambda i: (i, 0)),
                  pl.BlockSpec((HALO_BF, W3), lambda i: (jnp.minimum((i + 1) * hb, S // HALO_BF - 1), 0)),
                  pl.BlockSpec((CONV_K, W3), lambda i: (0, 0)), pl.BlockSpec(memory_space=pl.ANY)],
        out_specs=pl.BlockSpec((tm, W3), lambda i: (i, 0)),
        out_shape=jax.ShapeDtypeStruct(dproj.shape, BF),
        input_output_aliases={3: 0},
        compiler_params=_cparams(est, ("parallel",)),
    )(dc, dc, convw, dproj)


def _inv_unit_lower(A):
    C = A.shape[-1]
    row, col = _iota((C, C), 0), _iota((C, C), 1)
    T = jnp.broadcast_to(jnp.where(row == col, 1.0, 0.0).astype(F32), A.shape)
    b = 1
    while b < C:
        hi = ~(2 * b - 1)
        off = ((row & hi) == (col & hi)) & ((row & b) != 0) & ((col & b) == 0)
        T = T - _dotf(_dotf(T, jnp.where(off, A, 0.0)), T)
        b *= 2
    return T


def _delta_common(q, k, g, beta):
    C = q.shape[-2]
    row, col = _iota((C, C), 0), _iota((C, C), 1)
    tril = row >= col
    qs = q * (D_HEAD ** -0.5)
    gcb = _dot01(jnp.where(tril, 1.0, 0.0), jnp.broadcast_to(g, g.shape[:-1] + (LANES,)))
    gc = gcb[..., :1]
    Dm = jnp.exp(jnp.where(tril, gc - jnp.swapaxes(gcb, -1, -2), -1e30))
    eg = jnp.exp(gc)
    gl = jnp.sum(jnp.where(_iota((C, 1), 0) == C - 1, gc, 0.0), axis=(-2, -1), keepdims=True)
    el = jnp.exp(gl)
    er = jnp.exp(gl - gc)
    kb = k * beta
    KK = _dot_nt(kb, k)
    QK = _dot_nt(qs, k)
    return dict(row=row, col=col, tril=tril, qs=qs, gc=gc, Dm=Dm, eg=eg, el=el, er=er, kb=kb, KK=KK, QK=QK)


def _delta_chunk_fwd(S0, q, k, v, g, beta, T=None):
    m = _delta_common(q, k, g, beta)
    if T is None:
        T = _inv_unit_lower(jnp.where(m["row"] > m["col"], m["KK"] * m["Dm"], 0.0))
    u = _dotf(T, v * beta)
    w = _dotf(T, m["kb"] * m["eg"])
    vn = u - _dot(w, S0)
    o = _dot(m["qs"] * m["eg"], S0) + _dot(m["QK"] * m["Dm"], vn)
    S1 = S0 * m["el"] + _dot_tn(k * m["er"], vn)
    return o, S1, T


def _delta_chunk_bwd(S0, q, k, v, g, beta, T, do, dS1):
    m = _delta_common(q, k, g, beta)
    C = q.shape[-2]
    qs, Dm, eg, el, er, kb, KK, QK = (m[n] for n in ("qs", "Dm", "eg", "el", "er", "kb", "KK", "QK"))
    strict = m["row"] > m["col"]
    total = lambda x: jnp.sum(x, axis=(-2, -1), keepdims=True)
    ru, rw = v * beta, kb * eg
    u = _dotf(T, ru)
    w = _dotf(T, rw)
    vn = u - _dot(w, S0)
    P = QK * Dm
    qg = qs * eg
    kr = k * er

    dvn = _dot_tn(P, do) + _dot(kr, dS1)
    dS0 = dS1 * el + _dot_tn(qg, do) - _dot_tn(w, dvn)
    d_el = total(dS1 * S0)
    dqg = _dot_nt(do, S0)
    dqs = dqg * eg
    deg = jnp.sum(dqg * qs, -1, keepdims=True)
    dP = _dot_nt(do, vn)
    dPD = dP * Dm
    dqs = dqs + _dot(dPD, k)
    dk = _dot_tn(dPD, qs)
    dD = dP * QK
    dkr = _dot_nt(vn, dS1)
    dk = dk + dkr * er
    der = jnp.sum(dkr * k, -1, keepdims=True)
    dw = -_dot_nt(dvn, S0)
    dru = _dotf_tn(T, dvn)
    drw = _dotf_tn(T, dw)
    dT = _dotf_nt(dvn, ru) + _dotf_nt(dw, rw)
    dA = -_dotf_nt(_dotf_tn(T, dT), T)
    dAm = jnp.where(strict, dA, 0.0)
    dKK = dAm * Dm
    dkb = _dot(dKK, k)
    dk = dk + _dot_tn(dKK, kb)
    dD = dD + dAm * KK
    dv = dru * beta
    dbeta = jnp.sum(dru * v, -1, keepdims=True)
    dkb = dkb + drw * eg
    deg = deg + jnp.sum(drw * kb, -1, keepdims=True)
    dk = dk + dkb * beta
    dbeta = dbeta + jnp.sum(dkb * k, -1, keepdims=True)
    E = dD * Dm
    dgc = jnp.sum(E, -1, keepdims=True) - jnp.sum(jnp.swapaxes(E, -1, -2), -1, keepdims=True)
    dgc = dgc + deg * eg - der * er
    dgl = total(der * er) + d_el * el
    dgc = dgc + jnp.where(_iota((C, 1), 0) == C - 1, dgl, 0.0)
    triu = jnp.where(m["row"] <= m["col"], 1.0, 0.0)
    dg = _dot01(triu, jnp.broadcast_to(dgc, dgc.shape[:-1] + (LANES,)))[..., :1]
    dq = dqs * (D_HEAD ** -0.5)
    return dq, dk, dv, dg, dbeta, dS0


def _head_cols(gb, h):
    lane = _iota(gb.shape, 1)
    beta = jnp.sum(jnp.where(lane == h, gb, 0.0), -1, keepdims=True)
    g = jnp.sum(jnp.where(lane == N_HEADS + h, gb, 0.0), -1, keepdims=True)
    return g, beta


def _delta_fwd(q, k, v, gb):
    S = q.shape[0]
    C = DELTA_CHUNK
    N = S // C

    HB = DELTA_HEADS_PER_STEP

    def body(q_ref, k_ref, v_ref, gb_ref, o_ref, st_ref, t_ref, s_scr):
        n, hb = pl.program_id(0), pl.program_id(1)
        gb = gb_ref[...]

        @pl.when(n == 0)
        def _():
            for hh in range(HB):
                s_scr[hb * HB + hh] = jnp.zeros((D_HEAD, D_HEAD), F32)

        heads = [hb * HB + hh for hh in range(HB)]
        cols = [slice(hh * D_HEAD, (hh + 1) * D_HEAD) for hh in range(HB)]
        per_head = lambda ref: jnp.stack([ref[:, c] for c in cols])
        g, beta = (jnp.stack(t) for t in zip(*[_head_cols(gb, h) for h in heads]))
        S0 = jnp.stack([s_scr[h] for h in heads])
        o, S1, T = _delta_chunk_fwd(S0, per_head(q_ref), per_head(k_ref), per_head(v_ref), g, beta)
        for hh in range(HB):
            st_ref[hh, 0] = S0[hh]
            t_ref[hh, 0] = T[hh]
            o_ref[:, cols[hh]] = o[hh]
            s_scr[heads[hh]] = S1[hh]

    hd = pl.BlockSpec((C, HB * D_HEAD), lambda n, h: (n, h))
    mat = pl.BlockSpec((HB, 1, D_HEAD, D_HEAD), lambda n, h: (h, n, 0, 0))
    est = 40 * HB * _nbytes((C, D_HEAD), F32)
    return pl.pallas_call(
        body, name="delta_fwd", grid=(N, N_HEADS // HB),
        in_specs=[hd, hd, hd, pl.BlockSpec((C, LANES), lambda n, h: (n, 0))],
        out_specs=[hd, mat, mat],
        out_shape=[jax.ShapeDtypeStruct((S, N_HEADS * D_HEAD), F32),
                   jax.ShapeDtypeStruct((N_HEADS, N, D_HEAD, D_HEAD), F32),
                   jax.ShapeDtypeStruct((N_HEADS, N, C, C), F32)],
        scratch_shapes=[pltpu.VMEM((N_HEADS, D_HEAD, D_HEAD), F32)],
        compiler_params=_cparams(est, ("arbitrary", "arbitrary")),
    )(q, k, v, gb)


def _delta_bwd(q, k, v, gb, st, tinv, do):
    S = q.shape[0]
    C = DELTA_CHUNK
    N = S // C

    HB = DELTA_HEADS_PER_STEP

    def body(q_ref, k_ref, v_ref, gb_ref, st_ref, t_ref, do_ref, dq_ref, dk_ref, dv_ref, dgb_ref, ds_scr):
        n, hb = pl.program_id(0), pl.program_id(1)
        gb = gb_ref[...]
        lane = _iota((C, LANES), 1)
        dgb = jnp.zeros((C, LANES), F32)

        @pl.when(n == 0)
        def _():
            for hh in range(HB):
                ds_scr[hb * HB + hh] = jnp.zeros((D_HEAD, D_HEAD), F32)

        heads = [hb * HB + hh for hh in range(HB)]
        cols = [slice(hh * D_HEAD, (hh + 1) * D_HEAD) for hh in range(HB)]
        per_head = lambda ref: jnp.stack([ref[:, c] for c in cols])
        g, beta = (jnp.stack(t) for t in zip(*[_head_cols(gb, h) for h in heads]))
        dS1 = jnp.stack([ds_scr[h] for h in heads])
        dq, dk, dv, dg, dbeta, dS0 = _delta_chunk_bwd(
            st_ref[:, 0], per_head(q_ref), per_head(k_ref), per_head(v_ref), g, beta, t_ref[:, 0], per_head(do_ref), dS1)
        for hh, h in enumerate(heads):
            dq_ref[:, cols[hh]] = dq[hh]
            dk_ref[:, cols[hh]] = dk[hh]
            dv_ref[:, cols[hh]] = dv[hh]
            dgb = dgb + jnp.where(lane == h, dbeta[hh], 0.0) + jnp.where(lane == N_HEADS + h, dg[hh], 0.0)
            ds_scr[h] = dS0[hh]

        @pl.when(hb == 0)
        def _():
            dgb_ref[...] = dgb

        @pl.when(hb > 0)
        def _():
            dgb_ref[...] += dgb

    hd = pl.BlockSpec((C, HB * D_HEAD), lambda n, h: (N - 1 - n, h))
    mat = pl.BlockSpec((HB, 1, D_HEAD, D_HEAD), lambda n, h: (h, N - 1 - n, 0, 0))
    gbs = pl.BlockSpec((C, LANES), lambda n, h: (N - 1 - n, 0))
    est = 60 * HB * _nbytes((C, D_HEAD), F32)
    return pl.pallas_call(
        body, name="delta_bwd", grid=(N, N_HEADS // HB),
        in_specs=[hd, hd, hd, gbs, mat, mat, hd],
        out_specs=[hd, hd, hd, gbs],
        out_shape=[jax.ShapeDtypeStruct((S, N_HEADS * D_HEAD), F32)] * 3 + [jax.ShapeDtypeStruct((S, LANES), F32)],
        scratch_shapes=[pltpu.VMEM((N_HEADS, D_HEAD, D_HEAD), F32)],
        compiler_params=_cparams(est, ("arbitrary", "arbitrary")),
    )(q, k, v, gb, st, tinv, do)


def _ya_head(o, z, onw):
    return o * lax.rsqrt(jnp.mean(o * o, -1, keepdims=True) + RMS_EPS) * onw * _silu(z)


def _sgu_pre(u, vg, sg, sb):
    return _gelu(u), _ln(_gelu(vg), sg, sb)


def _chunk_causal(shape, di, dj):
    sh = jnp.int32(int(math.log2(SGU_CHUNK)))
    return lax.shift_right_logical(_iota(shape, di), sh) >= lax.shift_right_logical(_iota(shape, dj), sh)


def _ws_masked(ws):
    return jnp.where(_chunk_causal(ws.shape, 1, 2), ws, 0.0)


def _mix_prep(o, proj, onw, sg, sb, ws, bst, *, tm=256):
    S = o.shape[0]
    tm = _tile(S, tm, SGU_BLOCK)

    def body(o_ref, z_ref, u_ref, vg_ref, onw_ref, sg_ref, sb_ref, ws_ref, bst_ref, ya_ref, yb_ref):
        onw = onw_ref[...]
        for h in range(N_HEADS):
            sl = slice(h * D_HEAD, (h + 1) * D_HEAD)
            ya_ref[:, sl] = _ya_head(o_ref[:, sl], z_ref[:, sl], onw).astype(BF)
        ua, vl = _sgu_pre(u_ref[...], vg_ref[...], sg_ref[...], sb_ref[...])
        wsm = _ws_masked(ws_ref[...])
        bst = bst_ref[...]
        for blk in range(tm // SGU_BLOCK):
            rs = slice(blk * SGU_BLOCK, (blk + 1) * SGU_BLOCK)
            for gi in range(SGU_GROUPS):
                cs = slice(gi * D_HEAD, (gi + 1) * D_HEAD)
                sp = _dot(wsm[gi], vl[rs, cs]) + bst[:, gi:gi + 1]
                yb_ref[rs, cs] = (ua[rs, cs] * sp).astype(BF)

    blk = lambda col: pl.BlockSpec((tm, D_MODEL), lambda i: (i, col))
    full = lambda shape: pl.BlockSpec(shape, lambda i: (0,) * len(shape))
    est = 10 * _nbytes((tm, D_MODEL), F32)
    return pl.pallas_call(
        body, name="mix_prep", grid=(S // tm,),
        in_specs=[blk(0), blk(3), blk(4), blk(5), full((1, D_HEAD)), full((1, D_MODEL)), full((1, D_MODEL)),
                  full((SGU_GROUPS, SGU_BLOCK, SGU_BLOCK)), full((SGU_BLOCK, LANES))],
        out_specs=[blk(0), blk(0)],
        out_shape=[jax.ShapeDtypeStruct((S, D_MODEL), BF)] * 2,
        compiler_params=_cparams(est, ("parallel",)),
    )(o, proj, proj, proj, onw, sg, sb, ws, bst)


def _mix_prep_bwd(o, proj, onw, sg, sb, ws, bst, dya, dyb, dproj, *, tm=256):
    S = o.shape[0]
    tm = _tile(S, tm, SGU_BLOCK)

    def body(o_ref, z_ref, u_ref, vg_ref, onw_ref, sg_ref, sb_ref, ws_ref, bst_ref, dya_ref, dyb_ref, dproj_in,
             do_ref, dzuv_ref, donw_ref, dsg_ref, dsb_ref, dws_ref, dbst_ref, dvl_scr, dua_scr):
        dz_ref, du_ref, dvg_ref = (dzuv_ref.at[:, k * D_MODEL:(k + 1) * D_MODEL] for k in range(3))
        @pl.when(pl.program_id(0) == 0)
        def _():
            for r in (donw_ref, dsg_ref, dsb_ref, dws_ref, dbst_ref):
                r[...] = jnp.zeros_like(r)

        onw = onw_ref[...]
        donw = jnp.zeros((1, D_HEAD), F32)
        for h in range(N_HEADS):
            sl = slice(h * D_HEAD, (h + 1) * D_HEAD)
            _, vj = jax.vjp(_ya_head, o_ref[:, sl], z_ref[:, sl], onw)
            do_h, dz_h, donw_h = vj(dya_ref[:, sl])
            do_ref[:, sl] = do_h.astype(BF)
            dz_ref[:, sl] = dz_h.astype(BF)
            donw = donw + donw_h
        donw_ref[...] += _bcast_rows(donw)

        (ua, vl), vj = jax.vjp(_sgu_pre, u_ref[...], vg_ref[...], sg_ref[...], sb_ref[...])
        wsm = _ws_masked(ws_ref[...])
        bst = bst_ref[...]
        lane = _iota((SGU_BLOCK, LANES), 1)
        dbst = jnp.zeros((SGU_BLOCK, LANES), F32)
        cmask = _chunk_causal((SGU_BLOCK, SGU_BLOCK), 0, 1)
        for gi in range(SGU_GROUPS):
            cs = slice(gi * D_HEAD, (gi + 1) * D_HEAD)
            wg = wsm[gi]
            wgt = jnp.transpose(wg)
            dwg = jnp.zeros((SGU_BLOCK, SGU_BLOCK), F32)
            for blk in range(tm // SGU_BLOCK):
                rs = slice(blk * SGU_BLOCK, (blk + 1) * SGU_BLOCK)
                sp = _dot(wg, vl[rs, cs]) + bst[:, gi:gi + 1]
                dyb = dyb_ref[rs, cs]
                dsp = dyb * ua[rs, cs]
                dua_scr[rs, cs] = dyb * sp
                dvl_scr[rs, cs] = _dot(wgt, dsp)
                dwg = dwg + _dot_nt(dsp, vl[rs, cs])
                dbst = dbst + jnp.where(lane == gi, jnp.sum(dsp, -1, keepdims=True), 0.0)
            dws_ref[gi] += jnp.where(cmask, dwg, 0.0)
        dbst_ref[...] += dbst
        du, dvg, dsg, dsb = vj((dua_scr[...], dvl_scr[...]))
        du_ref[...] = du.astype(BF)
        dvg_ref[...] = dvg.astype(BF)
        dsg_ref[...] += _bcast_rows(dsg)
        dsb_ref[...] += _bcast_rows(dsb)

    blk = lambda col: pl.BlockSpec((tm, D_MODEL), lambda i: (i, col))
    full = lambda shape: pl.BlockSpec(shape, lambda i: (0,) * len(shape))
    est = 16 * _nbytes((tm, D_MODEL), F32)
    outs = pl.pallas_call(
        body, name="mix_prep_bwd", grid=(S // tm,),
        in_specs=[blk(0), blk(3), blk(4), blk(5), full((1, D_HEAD)), full((1, D_MODEL)), full((1, D_MODEL)),
                  full((SGU_GROUPS, SGU_BLOCK, SGU_BLOCK)), full((SGU_BLOCK, LANES)), blk(0), blk(0),
                  pl.BlockSpec(memory_space=pl.ANY)],
        out_specs=[blk(0), pl.BlockSpec((tm, 3 * D_MODEL), lambda i: (i, 1)),
                   full((SUBLANES, D_HEAD)), full((SUBLANES, D_MODEL)), full((SUBLANES, D_MODEL)),
                   full((SGU_GROUPS, SGU_BLOCK, SGU_BLOCK)), full((SGU_BLOCK, LANES))],
        out_shape=[jax.ShapeDtypeStruct((S, D_MODEL), BF), jax.ShapeDtypeStruct(dproj.shape, BF),
                   jax.ShapeDtypeStruct((SUBLANES, D_HEAD), F32), jax.ShapeDtypeStruct((SUBLANES, D_MODEL), F32),
                   jax.ShapeDtypeStruct((SUBLANES, D_MODEL), F32),
                   jax.ShapeDtypeStruct((SGU_GROUPS, SGU_BLOCK, SGU_BLOCK), F32),
                   jax.ShapeDtypeStruct((SGU_BLOCK, LANES), F32)],
        input_output_aliases={11: 1},
        scratch_shapes=[pltpu.VMEM((tm, D_MODEL), F32)] * 2,
        compiler_params=_cparams(est, ("arbitrary",)),
    )(o, proj, proj, proj, onw, sg, sb, ws, bst, dya, dyb, dproj)
    return outs


def _gate_merge(pa, pb, proj, *, tm=512):
    S = pa.shape[0]
    tm = _tile(S, tm, SUBLANES * 2)

    def body(pa_ref, pb_ref, ga_ref, gb_ref, m_ref):
        pa, pb = pa_ref[...].astype(F32), pb_ref[...].astype(F32)
        m_ref[...] = (_sigmoid(ga_ref[...]) * pa + _sigmoid(gb_ref[...]) * pb).astype(BF)

    blk = lambda col: pl.BlockSpec((tm, D_MODEL), lambda i: (i, col))
    return pl.pallas_call(
        body, name="gate_merge", grid=(S // tm,),
        in_specs=[blk(0), blk(0), blk(6), blk(7)], out_specs=blk(0),
        out_shape=jax.ShapeDtypeStruct((S, D_MODEL), BF),
        compiler_params=_cparams(6 * _nbytes((tm, D_MODEL), F32), ("parallel",)),
    )(pa, pb, proj, proj)


def _gate_merge_bwd(pa, pb, proj, dm, *, tm=512):
    S = pa.shape[0]
    tm = _tile(S, tm, SUBLANES * 2)

    def body(pa_ref, pb_ref, ga_ref, gb_ref, dm_ref, dpa_ref, dpb_ref, dg_ref):
        dm = dm_ref[...].astype(F32)
        sa, sb = _sigmoid(ga_ref[...]), _sigmoid(gb_ref[...])
        dpa_ref[...] = (dm * sa).astype(BF)
        dpb_ref[...] = (dm * sb).astype(BF)
        dg_ref[:, :D_MODEL] = (dm * pa_ref[...].astype(F32) * sa * (1.0 - sa)).astype(BF)
        dg_ref[:, D_MODEL:] = (dm * pb_ref[...].astype(F32) * sb * (1.0 - sb)).astype(BF)

    blk = lambda col: pl.BlockSpec((tm, D_MODEL), lambda i: (i, col))
    return pl.pallas_call(
        body, name="gate_merge_bwd", grid=(S // tm,),
        in_specs=[blk(0), blk(0), blk(6), blk(7), blk(0)],
        out_specs=[blk(0), blk(0), pl.BlockSpec((tm, 2 * D_MODEL), lambda i: (i, 3))],
        out_shape=[jax.ShapeDtypeStruct((S, D_MODEL), BF)] * 2 + [jax.ShapeDtypeStruct((S, 8 * D_MODEL), BF)],
        compiler_params=_cparams(10 * _nbytes((tm, D_MODEL), F32), ("parallel",)),
    )(pa, pb, proj, proj, dm)


def _swiglu_act(hgu, *, tm=256):
    S = hgu.shape[0]
    tm = _tile(S, tm, SUBLANES * 2)

    def body(hg_ref, hu_ref, h_ref):
        h_ref[...] = (_silu(hg_ref[...].astype(F32)) * hu_ref[...].astype(F32)).astype(BF)

    blk = lambda col: pl.BlockSpec((tm, FFN_K), lambda i: (i, col))
    return pl.pallas_call(
        body, name="swiglu_act", grid=(S // tm,),
        in_specs=[blk(0), blk(1)], out_specs=blk(0),
        out_shape=jax.ShapeDtypeStruct((S, FFN_K), BF),
        compiler_params=_cparams(5 * _nbytes((tm, FFN_K), F32), ("parallel",)),
    )(hgu, hgu)


def _swiglu_bwd(hgu, dh, *, tm=256):
    S = hgu.shape[0]
    tm = _tile(S, tm, SUBLANES * 2)

    def body(hg_ref, hu_ref, dh_ref, d_ref):
        hg, dh = hg_ref[...].astype(F32), dh_ref[...].astype(F32)
        act, dact = _silu_and_grad(hg)
        d_ref[:, :FFN_K] = (dh * hu_ref[...].astype(F32) * dact).astype(BF)
        d_ref[:, FFN_K:] = (dh * act).astype(BF)

    blk = lambda col: pl.BlockSpec((tm, FFN_K), lambda i: (i, col))
    return pl.pallas_call(
        body, name="swiglu_bwd", grid=(S // tm,),
        in_specs=[blk(0), blk(1), blk(0)], out_specs=pl.BlockSpec((tm, 2 * FFN_K), lambda i: (i, 0)),
        out_shape=jax.ShapeDtypeStruct((S, 2 * FFN_K), BF),
        compiler_params=_cparams(8 * _nbytes((tm, FFN_K), F32), ("parallel",)),
    )(hgu, hgu, dh)


def _resid_ln(x, r, g, b, *, tm=512):
    S = x.shape[0]
    tm = _tile(S, tm, SUBLANES * 2)

    def body(x_ref, r_ref, g_ref, b_ref, y_ref, yb_ref):
        y = _ln(ALPHA * x_ref[...] + r_ref[...], g_ref[...], b_ref[...])
        y_ref[...] = y
        yb_ref[...] = y.astype(BF)

    blk = pl.BlockSpec((tm, D_MODEL), lambda i: (i, 0))
    vec = pl.BlockSpec((1, D_MODEL), lambda i: (0, 0))
    return pl.pallas_call(
        body, name="resid_ln", grid=(S // tm,),
        in_specs=[blk, blk, vec, vec], out_specs=[blk, blk],
        out_shape=[jax.ShapeDtypeStruct((S, D_MODEL), F32), jax.ShapeDtypeStruct((S, D_MODEL), BF)],
        compiler_params=_cparams(6 * _nbytes((tm, D_MODEL), F32), ("parallel",)),
    )(x, r, g, b)


def _resid_ln_bwd(x, r, g, b, dy, *, tm=512):
    S = x.shape[0]
    tm = _tile(S, tm, SUBLANES)

    def body(x_ref, r_ref, g_ref, b_ref, dy_ref, dp_ref, dg_ref, db_ref):
        @pl.when(pl.program_id(0) == 0)
        def _():
            dg_ref[...] = jnp.zeros_like(dg_ref)
            db_ref[...] = jnp.zeros_like(db_ref)

        _, vj = jax.vjp(_ln, ALPHA * x_ref[...] + r_ref[...], g_ref[...], b_ref[...])
        dp, dg, db = vj(dy_ref[...])
        dp_ref[...] = dp
        dg_ref[...] += _bcast_rows(dg)
        db_ref[...] += _bcast_rows(db)

    blk = pl.BlockSpec((tm, D_MODEL), lambda i: (i, 0))
    vec = pl.BlockSpec((1, D_MODEL), lambda i: (0, 0))
    acc = pl.BlockSpec((SUBLANES, D_MODEL), lambda i: (0, 0))
    return pl.pallas_call(
        body, name="resid_ln_bwd", grid=(S // tm,),
        in_specs=[blk, blk, vec, vec, blk], out_specs=[blk, acc, acc],
        out_shape=[jax.ShapeDtypeStruct((S, D_MODEL), F32)] + [jax.ShapeDtypeStruct((SUBLANES, D_MODEL), F32)] * 2,
        compiler_params=_cparams(10 * _nbytes((tm, D_MODEL), F32), ("arbitrary",)),
    )(x, r, g, b, dy)


def _loss_head(y, tgt, *, tm=512):
    S = y.shape[0]
    tm = _tile(S, tm, SUBLANES)

    def body(y_ref, t_ref, dy_ref, l_ref):
        @pl.when(pl.program_id(0) == 0)
        def _():
            l_ref[...] = jnp.zeros_like(l_ref)

        e = y_ref[...] - t_ref[...]
        dy_ref[...] = e * (1.0 / D_MODEL)
        l_ref[...] += 0.5 * jnp.sum(jnp.mean(e * e, -1, keepdims=True), keepdims=True)

    blk = pl.BlockSpec((tm, D_MODEL), lambda i: (i, 0))
    return pl.pallas_call(
        body, name="loss_head", grid=(S // tm,),
        in_specs=[blk, blk], out_specs=[blk, pl.BlockSpec((SUBLANES, LANES), lambda i: (0, 0))],
        out_shape=[jax.ShapeDtypeStruct((S, D_MODEL), F32), jax.ShapeDtypeStruct((SUBLANES, LANES), F32)],
        compiler_params=_cparams(6 * _nbytes((tm, D_MODEL), F32), ("arbitrary",)),
    )(y, tgt)


def _layer_fwd(x, xb, w, late):
    proj = _mm(xb, w["win"], mode="nn", name="mm_in", tm=1024, tn=1024)
    ba = _mm(xb, w["wba"], mode="nn", name="mm_in_ba", tm=1024, tn=LANES)
    qn, kn, vv, gb = _qkv_prep(proj, ba, w["convw"], w["arow"], w["dtrow"])
    o, st, tinv = _delta_fwd(qn, kn, vv, gb)
    ya, yb = _mix_prep(o, proj, w["onw"], w["sg"], w["sb"], w["ws"], w["bst"])
    w = {**w, **late(ya)}
    pa = _mm(ya, w["wpa"], mode="nn", name="mm_sq", out_dtype=BF)
    pb = _mm(yb, w["wpb"], mode="nn", name="mm_sq", out_dtype=BF)
    m = _gate_merge(pa, pb, proj)
    mix = _mm(m, w["wo"], mode="nn", name="mm_sq_res")
    x1, x1b = _resid_ln(x, mix, w["ln1g"], w["ln1b"])
    hgu = _mm(x1b, w["wgu"], mode="nn", name="mm_gu", tm=1024, tn=1536, out_dtype=BF)
    h = _swiglu_act(hgu)
    ffn = _mm(h, w["wd"], mode="nn", name="mm_down", tk=FFN_K)
    x2, x2b = _resid_ln(x1, ffn, w["ln2g"], w["ln2b"])
    saved = dict(x=x, xb=xb, proj=proj, ba=ba, qn=qn, kn=kn, vv=vv, gb=gb, o=o, st=st, tinv=tinv, ya=ya, yb=yb,
                 pa=pa, pb=pb, m=m, mix=mix, x1=x1, x1b=x1b, hgu=hgu, h=h, ffn=ffn)
    return x2, x2b, saved, w


def _layer_bwd(dx2, w, s, on_part=None):
    g = {}
    started = lambda part: on_part(part, g) if on_part is not None else None
    after = lambda v, token: v if token is None else v + token.astype(v.dtype)
    dpre2, g["ln2g"], g["ln2b"] = _resid_ln_bwd(s["x1"], s["ffn"], w["ln2g"], w["ln2b"], dx2)
    dh = _mm(dpre2, w["wd"], mode="nt", name="mm_nt_down", tn=1536, out_dtype=BF)
    g["wd"] = _mm(s["h"], dpre2, mode="tn", name="mm_tn_down", tm=1536, tk=1024, out_dtype=BF)
    dhgu = _swiglu_bwd(s["hgu"], dh)
    dx1 = _mm(dhgu, w["wgu"], mode="nt", name="mm_nt_gu", add=dpre2, add_scale=ALPHA, tk=3072)
    g["wgu"] = _mm(s["x1b"], dhgu, mode="tn", name="mm_tn_gu", tm=1024, tn=1536, tk=2048, out_dtype=BF)
    dpre1, g["ln1g"], g["ln1b"] = _resid_ln_bwd(s["x"], s["mix"], w["ln1g"], w["ln1b"], dx1)
    dm = _mm(dpre1, w["wo"], mode="nt", name="mm_nt_sq_bf", out_dtype=BF)
    g["wo"] = _mm(s["m"], dpre1, mode="tn", name="mm_tn_sq", tm=1024, tk=1024, out_dtype=BF)
    dpa, dpb, dproj = _gate_merge_bwd(s["pa"], s["pb"], s["proj"], dm)
    dya = _mm(dpa, w["wpa"], mode="nt", name="mm_nt_sq")
    g["wpa"] = _mm(s["ya"], dpa, mode="tn", name="mm_tn_sq", tm=1024, tk=1024, out_dtype=BF)
    dyb = _mm(dpb, w["wpb"], mode="nt", name="mm_nt_sq")
    g["wpb"] = _mm(s["yb"], dpb, mode="tn", name="mm_tn_sq", tm=1024, tk=1024, out_dtype=BF)
    do, dproj, g["onw"], g["sg"], g["sb"], g["ws"], g["bst"] = _mix_prep_bwd(
        s["o"], s["proj"], after(w["onw"], started("late")), w["sg"], w["sb"], w["ws"], w["bst"], dya, dyb, dproj)
    dqn, dkn, dvv, dgb = _delta_bwd(s["qn"], s["kn"], s["vv"], s["gb"], s["st"], s["tinv"], do)
    dc, dba, g["convw"], g["arow"], g["dtrow"] = _qkv_prep_bwd(
        s["proj"], s["ba"], w["convw"], w["arow"], w["dtrow"], dqn, dkn, dvv, dgb)
    dproj = _conv_bwd(dc, w["convw"], dproj)
    g["win"] = _mm(s["xb"], dproj, mode="tn", name="mm_tn_in", tm=1024, tn=1024, tk=2048, out_dtype=BF)
    g["wba"] = _mm(s["xb"], dba, mode="tn", name="mm_tn_ba", tm=1024, tn=LANES, tk=1024, out_dtype=BF)
    dx = _mm(dba, after(w["wba"], started("early")), mode="nt", name="mm_nt_ba", add=dpre1, add_scale=ALPHA, tm=1024)
    dx = _mm(dproj, w["win"], mode="nt", name="mm_nt_in", add=dx, add_scale=1.0, tk=2048)
    return dx, g


def _local_step(x, tgt, layers, on_grads=None):
    saved, weights = [], []
    xb = x.astype(BF)
    for layer in layers:
        x, xb, s, w = _layer_fwd(x, xb, *layer(x))
        saved.append(s)
        weights.append(w)
    dy, lacc = _loss_head(x, tgt)
    grads = [None] * len(layers)
    for l in reversed(range(len(layers))):
        on_part = functools.partial(on_grads, l) if on_grads is not None else None
        dy, grads[l] = _layer_bwd(dy, weights[l], saved[l], on_part)
    return lacc[0, 0], dy, grads


_QKVZ = 4 * D_MODEL
_BA = 2 * N_HEADS


WEIGHT_NAMES = ("w_in", "conv_w", "a_log", "dt_bias", "o_norm_w", "sgu_ln_g", "sgu_ln_b", "w_s", "b_s", "w_pa", "w_pb",
                "w_o", "ln1_g", "ln1_b", "w_ffn_gate", "w_ffn_up", "w_ffn_down", "ln2_g", "ln2_b")
WIRE = ("w_in", "w_ffn_gate", "w_ffn_up", "w_ffn_down", "w_pa", "w_pb", "w_o", "conv_w")
SMALL = (("a_log", N_HEADS), ("dt_bias", N_HEADS), ("o_norm_w", D_HEAD), ("sgu_ln_g", D_MODEL), ("sgu_ln_b", D_MODEL),
         ("w_s", SGU_GROUPS * SGU_BLOCK * SGU_BLOCK), ("b_s", SGU_GROUPS * SGU_BLOCK),
         ("ln1_g", D_MODEL), ("ln1_b", D_MODEL), ("ln2_g", D_MODEL), ("ln2_b", D_MODEL))
SMALL_ROWS = -(-sum(n for _, n in SMALL) // (LANES * SUBLANES)) * SUBLANES
N_MAIN_TILES = (N_IN - _BA) // D_MODEL
ADAM_TILES = dict(w_in=(128, "adamw_in"), w_ffn_gate=(256, "adamw_ffn_cols"), w_ffn_up=(256, "adamw_ffn_cols"),
                  w_ffn_down=(32, "adamw_ffn_rows"), w_pa=(128, "adamw_sq"), w_pb=(128, "adamw_sq"), w_o=(128, "adamw_sq"),
                  conv_w=(CONV_K, "adamw_conv"))


def _pad_to(a, axis, size):
    pads = [(0, 0)] * a.ndim
    pads[axis] = (0, size - a.shape[axis])
    return jnp.pad(a, pads)


def _wire_blocks(p):
    return dict(
        w_in=_pad_to(p["w_in"].astype(BF), 2, IN_PAD),
        w_ffn_gate=_pad_to(p["w_ffn_gate"].astype(BF), 2, FFN_PAD), w_ffn_up=_pad_to(p["w_ffn_up"].astype(BF), 2, FFN_PAD),
        w_ffn_down=_pad_to(p["w_ffn_down"].astype(BF), 1, FFN_PAD),
        w_pa=p["w_pa"].astype(BF), w_pb=p["w_pb"].astype(BF), w_o=p["w_o"].astype(BF),
        conv_w=_pad_to(p["conv_w"], 1, SUBLANES),
    )


def _by_columns(blocks):
    n, r, c = blocks.shape
    return jnp.transpose(blocks, (1, 0, 2)).reshape(r, n * c)


def _to_slots(full, c):
    r = full.shape[0]
    return jnp.transpose(full.reshape(r, N_DEV, c), (1, 0, 2))


def _lane_row(v, at):
    return jnp.pad(v[None], ((0, 0), (at, LANES - at - v.shape[0])))


EARLY = ("w_in", "conv_w")
LATE = ("w_pa", "w_pb", "w_o", "w_ffn_gate", "w_ffn_up", "w_ffn_down")


def _early_weights(stacks, p, l):
    return dict(
        win=_perm_in(stacks["w_in"], D_MODEL, N_MAIN_TILES), wba=_perm_in(stacks["w_in"], LANES, 1),
        convw=_by_columns(stacks["conv_w"][:, :CONV_K]),
        arow=_lane_row(p["a_log"][l], N_HEADS), dtrow=_lane_row(p["dt_bias"][l], N_HEADS),
        onw=p["o_norm_w"][l][None], sg=p["sgu_ln_g"][l][None], sb=p["sgu_ln_b"][l][None],
        ws=p["w_s"][l], bst=_pad_to(p["b_s"][l].T, 1, LANES),
        ln1g=p["ln1_g"][l][None], ln1b=p["ln1_b"][l][None], ln2g=p["ln2_g"][l][None], ln2b=p["ln2_b"][l][None],
    )


def _late_weights(stacks):
    return dict(
        wpa=stacks["w_pa"].reshape(D_MODEL, D_MODEL), wpb=stacks["w_pb"].reshape(D_MODEL, D_MODEL),
        wo=stacks["w_o"].reshape(D_MODEL, D_MODEL),
        wgu=_by_columns(jnp.concatenate([stacks["w_ffn_gate"], stacks["w_ffn_up"]], axis=0)),
        wd=stacks["w_ffn_down"].reshape(FFN_K, D_MODEL),
    )


def _small_pack(parts):
    flat = jnp.concatenate([parts[n].reshape(-1) for n, _ in SMALL])
    return _pad_to(flat, 0, SMALL_ROWS * LANES).reshape(SMALL_ROWS, LANES)


def _small_unpack(rows, like):
    flat, out, off = rows.reshape(-1), {}, 0
    for n, size in SMALL:
        out[n] = flat[off:off + size].reshape(like[n].shape[1:])
        off += size
    return out


def _late_slots(g):
    slots = dict(
        w_ffn_gate=_to_slots(g["wgu"][:, :FFN_K], FFN_PAD), w_ffn_up=_to_slots(g["wgu"][:, FFN_K:], FFN_PAD),
        w_ffn_down=g["wd"].reshape(N_DEV, FFN_PAD, D_MODEL),
        w_pa=g["wpa"].reshape(N_DEV, D_MODEL // N_DEV, D_MODEL), w_pb=g["wpb"].reshape(N_DEV, D_MODEL // N_DEV, D_MODEL),
        w_o=g["wo"].reshape(N_DEV, D_MODEL // N_DEV, D_MODEL),
    )
    return [slots[n] for n in LATE]


def _early_slots(g):
    slots = [_perm_out(g["win"], g["wba"]), _pad_to(_to_slots(g["convw"][:CONV_K], 3 * D_MODEL // N_DEV), 1, SUBLANES)]
    small = _small_pack(dict(
        a_log=g["arow"][0, N_HEADS:2 * N_HEADS], dt_bias=g["dtrow"][0, N_HEADS:2 * N_HEADS], o_norm_w=g["onw"][0],
        sgu_ln_g=g["sg"][0], sgu_ln_b=g["sb"][0], w_s=g["ws"], b_s=g["bst"][:, :SGU_GROUPS].T,
        ln1_g=g["ln1g"][0], ln1_b=g["ln1b"][0], ln2_g=g["ln2g"][0], ln2_b=g["ln2b"][0]))
    return slots, small


def _in_tile_start(j, tile_w):
    if tile_w == LANES:
        return jnp.int32(_QKVZ)
    return j * D_MODEL + jnp.where(j >= _QKVZ // D_MODEL, _BA, 0)


def _select(rows_iota, cols_iota, dev, start, valid):
    hit = (rows_iota + (dev * IN_BLOCK - start) == cols_iota) & (rows_iota < IN_BLOCK) & (cols_iota < valid)
    return jnp.where(hit, 1.0, 0.0).astype(BF)


def _perm_in(stack, tile_w, n_tiles):
    valid = _BA if tile_w == LANES else tile_w

    def first_dev(j):
        return lax.div(_in_tile_start(j, tile_w), jnp.int32(IN_BLOCK))

    def body(w_ref, o_ref, acc_ref):
        j, k = pl.program_id(0), pl.program_id(1)
        sel = _select(_iota((IN_PAD, tile_w), 0), _iota((IN_PAD, tile_w), 1), first_dev(j) + k,
                      _in_tile_start(j, tile_w), valid)
        part = jnp.dot(w_ref[0], sel, preferred_element_type=F32)

        @pl.when(k == 0)
        def _():
            acc_ref[...] = part

        @pl.when(k == 1)
        def _():
            o_ref[...] = (acc_ref[...] + part).astype(BF)

    est = _nbytes((D_MODEL, IN_PAD), BF) + 3 * _nbytes((D_MODEL, tile_w), F32) + 2 * _nbytes((IN_PAD, tile_w), F32)
    return pl.pallas_call(
        body, name="perm_in" if tile_w != LANES else "perm_in_ba", grid=(n_tiles, 2),
        in_specs=[pl.BlockSpec((1, D_MODEL, IN_PAD), lambda j, k: (jnp.minimum(first_dev(j) + k, N_DEV - 1), 0, 0))],
        out_specs=pl.BlockSpec((D_MODEL, tile_w), lambda j, k: (0, j)),
        out_shape=jax.ShapeDtypeStruct((D_MODEL, n_tiles * tile_w), BF),
        scratch_shapes=[pltpu.VMEM((D_MODEL, tile_w), F32)],
        compiler_params=_cparams(est, ("parallel", "arbitrary")),
    )(stack)


def _perm_out(dmain, dba):
    n_main = dmain.shape[1] // D_MODEL

    def body(dm_ref, db_ref, o_ref, acc_ref):
        d, t = pl.program_id(0), pl.program_id(1)

        @pl.when(t == 0)
        def _():
            acc_ref[...] = jnp.zeros_like(acc_ref)

        start = _in_tile_start(t, D_MODEL)
        overlaps = (start < (d + 1) * IN_BLOCK) & (d * IN_BLOCK < start + D_MODEL)

        @pl.when((t < n_main) & overlaps)
        def _():
            sel = _select(_iota((D_MODEL, IN_PAD), 1), _iota((D_MODEL, IN_PAD), 0), d, start, D_MODEL)
            acc_ref[...] += jnp.dot(dm_ref[...], sel, preferred_element_type=F32)

        @pl.when(t == n_main)
        def _():
            sel = _select(_iota((LANES, IN_PAD), 1), _iota((LANES, IN_PAD), 0), d, jnp.int32(_QKVZ), _BA)
            o_ref[0] = (acc_ref[...] + jnp.dot(db_ref[...], sel, preferred_element_type=F32)).astype(BF)

    est = 2 * _nbytes((D_MODEL, D_MODEL), BF) + 4 * _nbytes((D_MODEL, IN_PAD), F32)
    return pl.pallas_call(
        body, name="perm_out", grid=(N_DEV, n_main + 1),
        in_specs=[pl.BlockSpec((D_MODEL, D_MODEL), lambda d, t: (0, jnp.minimum(t, n_main - 1))),
                  pl.BlockSpec((D_MODEL, LANES), lambda d, t: (0, 0))],
        out_specs=pl.BlockSpec((1, D_MODEL, IN_PAD), lambda d, t: (d, 0, 0)),
        out_shape=jax.ShapeDtypeStruct((N_DEV, D_MODEL, IN_PAD), BF),
        scratch_shapes=[pltpu.VMEM((D_MODEL, IN_PAD), F32)],
        compiler_params=_cparams(est, ("parallel", "arbitrary")),
    )(dmain, dba)


def _mesh_place():
    x, y, c = (lax.axis_index(a) for a in MESH_AXES)
    return x, y, c


def _slot(x, y, c):
    return 4 * x + 2 * y + c


def _peer(place, j):
    x, y, c = place
    return (1 - x if j & 4 else x, 1 - y if j & 2 else y, 1 - c if j & 1 else c)


_HBM = pl.BlockSpec(memory_space=pltpu.HBM)
_SEM = pl.BlockSpec(memory_space=pltpu.SEMAPHORE)
_EFFECT = pltpu.SideEffectType.DATAFLOW_SIDE_EFFECTING


def _remote_copy(src_ref, land_ref, slot, per_slot, pslot, sems, u, j, peer):
    return pltpu.make_async_remote_copy(
        src_ref=src_ref.at[pslot] if per_slot else src_ref, dst_ref=land_ref.at[slot],
        send_sem=sems[0].at[u * (N_DEV - 1) + j - 1], recv_sem=sems[1].at[u * (N_DEV - 1) + j - 1],
        device_id=peer, device_id_type=pl.DeviceIdType.MESH)


def _own_copy(src_ref, land_ref, me, per_slot, sems, u):
    return pltpu.make_async_copy(src_ref.at[me] if per_slot else src_ref, land_ref.at[me], sems[2].at[u])


def _exchange_start(name, srcs, per_slot):
    n = len(srcs)
    lands = [jax.ShapeDtypeStruct(s.shape if p else (N_DEV,) + s.shape, s.dtype) for s, p in zip(srcs, per_slot)]

    def body(*refs):
        src_refs, sems, land_refs, token = refs[:n], refs[n:n + 3], refs[2 * n + 3:3 * n + 3], refs[-1]
        place = _mesh_place()
        me = _slot(*place)
        for u in range(n):
            _own_copy(src_refs[u], land_refs[u], me, per_slot[u], sems, u).start()
            for j in range(1, N_DEV):
                peer = _peer(place, j)
                _remote_copy(src_refs[u], land_refs[u], me, per_slot[u], _slot(*peer), sems, u, j, peer).start()
        token[...] = jnp.zeros_like(token)

    hbm = lambda a: pltpu.HBM(a.shape, a.dtype)
    sem = pltpu.SemaphoreType.DMA((n * (N_DEV - 1),))
    outs = pl.pallas_call(
        body, name=name,
        out_shape=(sem, sem, pltpu.SemaphoreType.DMA((n,)), *[hbm(a) for a in srcs], *[hbm(a) for a in lands],
                   jax.ShapeDtypeStruct((SUBLANES, LANES), F32)),
        in_specs=[_HBM] * n, out_specs=(_SEM, _SEM, _SEM, *[_HBM] * (2 * n), pl.BlockSpec(memory_space=pltpu.VMEM)),
        input_output_aliases={i: 3 + i for i in range(n)},
        compiler_params=pltpu.CompilerParams(has_side_effects=_EFFECT),
    )(*[pltpu.with_memory_space_constraint(a, pltpu.HBM) for a in srcs])
    return tuple(outs[:3]), list(outs[3:3 + n]), list(outs[3 + n:3 + 2 * n]), outs[-1]


def _exchange_wait(name, sems, srcs, lands, units, per_slot, after):
    m = len(units)

    def body(*refs):
        src_refs, land_refs, sem_refs = refs[:m], refs[m:2 * m], refs[2 * m:2 * m + 3]
        place = _mesh_place()
        me = _slot(*place)
        for i, u in enumerate(units):
            _own_copy(src_refs[i], land_refs[i], me, per_slot[u], sem_refs, u).wait()
            for j in range(1, N_DEV):
                peer = _peer(place, j)
                pslot = _slot(*peer)
                cp = _remote_copy(src_refs[i], land_refs[i], pslot, per_slot[u], pslot, sem_refs, u, j, peer)
                cp.wait_send()
                cp.wait_recv()

    hbm = lambda a: pltpu.HBM(a.shape, a.dtype)
    outs = pl.pallas_call(
        body, name=name, out_shape=tuple(hbm(a) for a in list(srcs) + list(lands)),
        in_specs=[_HBM] * (2 * m) + [_SEM] * 3 + [pl.BlockSpec(memory_space=pl.ANY)], out_specs=tuple([_HBM] * (2 * m)),
        input_output_aliases={i: i for i in range(2 * m)},
        compiler_params=pltpu.CompilerParams(has_side_effects=_EFFECT),
    )(*srcs, *lands, *sems, after)
    return list(outs[m:])


def _adam_update(g, w, m, v):
    m = ADAM_B1 * m + (1.0 - ADAM_B1) * g
    v = ADAM_B2 * v + (1.0 - ADAM_B2) * jnp.square(g)
    m_hat = m / (1.0 - ADAM_B1 ** ADAM_STEP)
    v_hat = v / (1.0 - ADAM_B2 ** ADAM_STEP)
    return -ADAM_LR * (m_hat / (jnp.sqrt(v_hat) + ADAM_EPS) + ADAM_WD * w), m, v


def _adamw(recvs, w, m, v, *, tr, name):
    L, R, C = w.shape
    rp = max(tr, SUBLANES * (4 // jnp.dtype(recvs[0].dtype).itemsize))
    Cp = recvs[0].shape[2]

    def body(*refs):
        r_refs, (w_ref, m_ref, v_ref, g_ref, d_ref, nm_ref, nv_ref) = refs[:L], refs[L:]
        for l in range(L):
            @pl.when(pl.program_id(0) == l)
            def _(r_ref=r_refs[l]):
                g = r_ref[0, :tr, :C].astype(F32)
                for s in range(1, N_DEV):
                    g = g + r_ref[s, :tr, :C].astype(F32)
                d, nm, nv = _adam_update(g, w_ref[0], m_ref[0], v_ref[0])
                g_ref[0], d_ref[0], nm_ref[0], nv_ref[0] = g, d, nm, nv

    blk = pl.BlockSpec((1, tr, C), lambda l, i: (l, i, 0))
    r_specs = [pl.BlockSpec((N_DEV, rp, Cp), lambda l, i, k=k: (0, jnp.where(l == k, i, 0), 0)) for k in range(L)]
    est = 2 * _nbytes((N_DEV, rp, Cp), recvs[0].dtype) + 8 * _nbytes((tr, Cp), F32)
    return pl.pallas_call(
        body, name=name, grid=(L, R // tr),
        in_specs=r_specs + [blk] * 3, out_specs=[blk] * 4,
        out_shape=[jax.ShapeDtypeStruct((L, R, C), F32)] * 4,
        compiler_params=_cparams(est, ("arbitrary", "arbitrary")),
    )(*recvs, w, m, v)


def _adamw_small(recv, w, m, v):
    def body(r_ref, w_ref, m_ref, v_ref, g_ref, d_ref, nm_ref, nv_ref):
        g = r_ref[0]
        for s in range(1, N_DEV):
            g = g + r_ref[s]
        g_ref[...] = g
        d_ref[...], nm_ref[...], nv_ref[...] = _adam_update(g, w_ref[...], m_ref[...], v_ref[...])

    vm = pl.BlockSpec(memory_space=pltpu.VMEM)
    return pl.pallas_call(
        body, name="adamw_small", in_specs=[vm] * 4, out_specs=[vm] * 4,
        out_shape=[jax.ShapeDtypeStruct((SMALL_ROWS, LANES), F32)] * 4,
        compiler_params=_cparams(20 * _nbytes((SMALL_ROWS, LANES), F32)),
    )(recv, w, m, v)


def kernel(x, w_in, conv_w, a_log, dt_bias, o_norm_w, sgu_ln_g, sgu_ln_b, w_s, b_s, w_pa, w_pb, w_o, ln1_g, ln1_b, w_ffn_gate, w_ffn_up, w_ffn_down, ln2_g, ln2_b, loss_target, m_w_in, m_conv_w, m_a_log, m_dt_bias, m_o_norm_w, m_sgu_ln_g, m_sgu_ln_b, m_w_s, m_b_s, m_w_pa, m_w_pb, m_w_o, m_ln1_g, m_ln1_b, m_w_ffn_gate, m_w_ffn_up, m_w_ffn_down, m_ln2_g, m_ln2_b, v_w_in, v_conv_w, v_a_log, v_dt_bias, v_o_norm_w, v_sgu_ln_g, v_sgu_ln_b, v_w_s, v_b_s, v_w_pa, v_w_pb, v_w_o, v_ln1_g, v_ln1_b, v_w_ffn_gate, v_w_ffn_up, v_w_ffn_down, v_ln2_g, v_ln2_b):
    given = dict(locals())
    P = {n: given[n] for n in WEIGHT_NAMES}
    M = {n: given["m_" + n] for n in WEIGHT_NAMES}
    V = {n: given["v_" + n] for n in WEIGHT_NAMES}

    wire = _wire_blocks(P)
    units = [(n, l) for l in range(DEPTH) for n in EARLY + LATE]
    whole = [False] * len(units)
    g_sems, g_srcs, g_lands, g_token = _exchange_start("gather_start", [wire[n][l] for n, l in units], whole)

    def gathered(name, names, l, after):
        idx = [units.index((n, l)) for n in names]
        got = _exchange_wait(name, g_sems, [g_srcs[i] for i in idx], [g_lands[i] for i in idx], idx, whole, after)
        return dict(zip(names, got))

    def layer(l):
        def weights(x_in):
            after = g_token if l == 0 else x_in
            early = _early_weights(gathered(f"gather_wait_early{l}", EARLY, l, after), P, l)
            return early, lambda ya: _late_weights(gathered(f"gather_wait_late{l}", LATE, l, ya))
        return weights

    pending = {}

    def on_grads(l, part, g):
        if part == "late":
            srcs, names = _late_slots(g), LATE
            per_slot = [True] * len(srcs)
        else:
            slots, small = _early_slots(g)
            srcs, names = slots + [small], EARLY + ("small",)
            per_slot = [True] * len(slots) + [False]
        sems, s_thru, l_thru, token = _exchange_start(f"exchange_start_{part}{l}", srcs, per_slot)
        pending[l, part] = (names, sems, s_thru, l_thru, per_slot)
        return token[0, 0]

    loss_local, dx, _ = _local_step(x[0], loss_target[0], [layer(l) for l in range(DEPTH)], on_grads)
    loss = lax.psum(loss_local, MESH_AXES)

    recv = [{} for _ in range(DEPTH)]
    for l in reversed(range(DEPTH)):
        for part in ("late", "early"):
            names, sems, s_thru, l_thru, per_slot = pending[l, part]
            got = _exchange_wait(f"exchange_wait_{part}{l}", sems, s_thru, l_thru, list(range(len(s_thru))), per_slot, dx)
            recv[l].update(zip(names, got))

    out = {}
    for n in WIRE:
        tr, name = ADAM_TILES[n]
        out[n] = _adamw([recv[l][n] for l in range(DEPTH)], P[n], M[n], V[n], tr=tr, name=name)
    small = [_adamw_small(recv[l]["small"], *[_small_pack({n: T[n][l] for n, _ in SMALL}) for T in (P, M, V)])
             for l in range(DEPTH)]
    for n, _ in SMALL:
        out[n] = [jnp.stack([_small_unpack(small[l][i], P)[n] for l in range(DEPTH)]) for i in range(4)]
    return (loss, dx[None], *[out[n][i] for i in range(4) for n in WEIGHT_NAMES])
```

```python
import functools
import math

import jax
import jax.numpy as jnp
from jax import lax
from jax.experimental import pallas as pl
from jax.experimental.pallas import tpu as pltpu

F32 = jnp.float32
BF = jnp.bfloat16
HIGHEST = lax.Precision.HIGHEST

D_MODEL = 1024
DEPTH = 2
N_HEADS = 8
D_HEAD = 128
CONV_K = 4
SGU_BLOCK = 128
SGU_GROUPS = 8
SGU_CHUNK = 64
FFN_HIDDEN = 2816
N_IN = 8208
N_DEV = 8
IN_BLOCK, IN_PAD = N_IN // N_DEV, 1152
FFN_BLOCK, FFN_PAD = FFN_HIDDEN // N_DEV, 384
FFN_K = N_DEV * FFN_PAD
ALPHA = (2 * DEPTH) ** 0.25
LN_EPS = 1e-5
RMS_EPS = 1e-6
ADAM_LR, ADAM_B1, ADAM_B2, ADAM_EPS, ADAM_WD, ADAM_STEP = 0.001, 0.9, 0.999, 1e-08, 0.01, 10

MESH_AXES = ("x", "y", "c")
DELTA_CHUNK = 128
DELTA_HEADS_PER_STEP = 8
LANES = 128
SUBLANES = 8
VMEM_BYTES = 64 * 1024 * 1024
HALO = SUBLANES
HALO_BF = 2 * SUBLANES


def _cparams(est_bytes, dims=None):
    limit = int(min(max(2 * est_bytes + (8 << 20), 32 << 20), VMEM_BYTES - (6 << 20)))
    kw = dict(vmem_limit_bytes=limit)
    if dims is not None:
        kw["dimension_semantics"] = dims
    return pltpu.CompilerParams(**kw)


def _nbytes(shape, dtype):
    return math.prod(shape) * jnp.dtype(dtype).itemsize


def _dims(kind, ndim):
    lhs, rhs = {"nn": (1, 0), "nt": (1, 1), "tn": (0, 0)}[kind]
    b = ndim - 2
    return (((lhs + b,), (rhs + b,)), (tuple(range(b)), tuple(range(b))))


def _mxu(a, b, kind):
    return lax.dot_general(a, b, _dims(kind, a.ndim), preferred_element_type=F32)


def _dot(a, b):
    return _mxu(a.astype(BF), b.astype(BF), "nn")


def _dot_nt(a, b):
    return _mxu(a.astype(BF), b.astype(BF), "nt")


def _dot_tn(a, b):
    return _mxu(a.astype(BF), b.astype(BF), "tn")


def _split(a):
    hi = a.astype(BF)
    return hi, (a - hi.astype(F32)).astype(BF)


def _dot3(a, b, kind):
    (ah, al), (bh, bl) = _split(a), _split(b)
    return _mxu(ah, bh, kind) + (_mxu(ah, bl, kind) + _mxu(al, bh, kind))


def _dotf(a, b):
    return _dot3(a, b, "nn")


def _dotf_nt(a, b):
    return _dot3(a, b, "nt")


def _dotf_tn(a, b):
    return _dot3(a, b, "tn")


def _dot01(sel, x, kind="nn"):
    s = jnp.broadcast_to(sel.astype(BF), x.shape[:-2] + sel.shape)
    h1 = x.astype(BF)
    r1 = x - h1.astype(F32)
    h2 = r1.astype(BF)
    h3 = (r1 - h2.astype(F32)).astype(BF)
    return _mxu(s, h1, kind) + (_mxu(s, h2, kind) + _mxu(s, h3, kind))


def _sigmoid(x):
    return 0.5 * jnp.tanh(0.5 * x) + 0.5


def _silu(x):
    return x * _sigmoid(x)


def _silu_and_grad(x):
    s = _sigmoid(x)
    return x * s, s * (1.0 + x * (1.0 - s))


def _gelu(x):
    return 0.5 * x * (1.0 + lax.erf(x * 0.7071067811865476))


def _softplus(x):
    return jnp.maximum(x, 0.0) + jnp.log1p(jnp.exp(-jnp.abs(x)))


def _ln(x, g, b):
    mu = jnp.mean(x, -1, keepdims=True)
    xc = x - mu
    var = jnp.mean(xc * xc, -1, keepdims=True)
    return xc * lax.rsqrt(var + LN_EPS) * g + b


def _iota(shape, dim):
    return lax.broadcasted_iota(jnp.int32, shape, dim)


def _tile(n, pref, align):
    if n <= pref:
        return n
    t = (pref // align) * align
    while t >= align:
        if n % t == 0:
            return t
        t -= align
    raise ValueError(f"no tile for {n} (pref {pref}, align {align})")


def _bcast_rows(v, rows=SUBLANES):
    return jnp.broadcast_to(v, (rows, v.shape[-1]))


def _mm(a, b, *, mode, name, out_dtype=F32, add=None, add_scale=1.0, tm=512, tn=1024, tk=1024, cols=None):
    if mode == "nn":
        (M, K), N = a.shape, b.shape[1]
    elif mode == "nt":
        (M, K), N = a.shape, b.shape[0]
    else:
        (K, M), N = a.shape, b.shape[1]
    col0 = 0
    if cols is not None:
        col0, N = cols
    tm = _tile(M, tm, LANES if mode == "tn" else SUBLANES * 2)
    tn = _tile(N, tn, LANES)
    tk = _tile(K, tk, LANES)
    nk = K // tk
    j0 = col0 // tn
    if mode == "nn":
        a_spec = pl.BlockSpec((tm, tk), lambda i, j, k: (i, k))
        b_spec = pl.BlockSpec((tk, tn), lambda i, j, k: (k, j + j0))
        dot = _dot
    elif mode == "nt":
        a_spec = pl.BlockSpec((tm, tk), lambda i, j, k: (i, k))
        b_spec = pl.BlockSpec((tn, tk), lambda i, j, k: (j, k))
        dot = _dot_nt
    else:
        a_spec = pl.BlockSpec((tk, tm), lambda i, j, k: (k, i))
        b_spec = pl.BlockSpec((tk, tn), lambda i, j, k: (k, j))
        dot = _dot_tn
    o_spec = pl.BlockSpec((tm, tn), lambda i, j, k: (i, j))
    has_add = add is not None

    def body(*refs):
        if has_add:
            a_ref, b_ref, add_ref, o_ref, acc_ref = refs
        else:
            a_ref, b_ref, o_ref, acc_ref = refs
            add_ref = None
        k = pl.program_id(2)
        part = dot(a_ref[...], b_ref[...])

        def finish(total):
            if has_add:
                total = total + add_scale * add_ref[...]
            o_ref[...] = total.astype(out_dtype)

        if nk == 1:
            finish(part)
        else:
            @pl.when(k == 0)
            def _():
                acc_ref[...] = part

            @pl.when(jnp.logical_and(k > 0, k < nk - 1))
            def _():
                acc_ref[...] += part

            @pl.when(k == nk - 1)
            def _():
                finish(acc_ref[...] + part)

    in_specs = [a_spec, b_spec] + ([o_spec] if has_add else [])
    args = (a, b) + ((add,) if has_add else ())
    est = (_nbytes((tm, tk), a.dtype) + _nbytes((tk, tn), b.dtype) + 2 * _nbytes((tm, tn), F32)
           + (_nbytes((tm, tn), F32) if has_add else 0)) + 2 * _nbytes((tm, tn), F32)
    return pl.pallas_call(
        body, name=name,
        grid=(M // tm, N // tn, nk),
        in_specs=in_specs, out_specs=o_spec,
        out_shape=jax.ShapeDtypeStruct((M, N), out_dtype),
        scratch_shapes=[pltpu.VMEM((tm, tn) if nk > 1 else (SUBLANES, LANES), F32)],
        compiler_params=_cparams(est, ("parallel", "parallel", "arbitrary")),
    )(*args)


def _conv_taps(xt, halo, w_ref, first):
    halo = jnp.where(first, 0.0, halo)
    xc = jnp.concatenate([halo, xt], axis=0)
    shifted = [xt] + [pltpu.roll(xc, s, 0)[HALO:] for s in range(1, CONV_K)]
    out = shifted[0] * w_ref[CONV_K - 1:CONV_K, :]
    for s in range(1, CONV_K):
        out = out + shifted[s] * w_ref[CONV_K - 1 - s:CONV_K - s, :]
    return out, shifted


def _gates(ba, arow, dtrow):
    lane = _iota(ba.shape, 1)
    beta = _sigmoid(ba)
    g = -jnp.exp(arow) * _softplus(ba + dtrow)
    return jnp.where(lane < N_HEADS, beta, jnp.where(lane < 2 * N_HEADS, g, 0.0))


def _l2n(x):
    return x * lax.rsqrt(jnp.sum(x * x, -1, keepdims=True) + RMS_EPS)


def _qkv_prep(proj, ba, convw, arow, dtrow, *, tm=256):
    S = proj.shape[0]
    tm = _tile(S, tm, SUBLANES)
    W3 = 3 * D_MODEL
    hb = tm // HALO

    def body(xt_ref, halo_ref, ba_ref, w_ref, a_ref, dt_ref, q_ref, k_ref, v_ref, gb_ref):
        c, _ = _conv_taps(xt_ref[...], halo_ref[...], w_ref, pl.program_id(0) == 0)
        c = _silu(c)
        for h in range(N_HEADS):
            lo = h * D_HEAD
            q_ref[:, lo:lo + D_HEAD] = _l2n(c[:, lo:lo + D_HEAD])
            k_ref[:, lo:lo + D_HEAD] = _l2n(c[:, D_MODEL + lo:D_MODEL + lo + D_HEAD])
        v_ref[...] = c[:, 2 * D_MODEL:]
        gb_ref[...] = _gates(ba_ref[...], a_ref[...], dt_ref[...])

    row = lambda w, col=0: pl.BlockSpec((tm, w), lambda i: (i, col))
    full = lambda shape: pl.BlockSpec(shape, lambda i: (0,) * len(shape))
    est = 4 * _nbytes((tm, W3), F32)
    return pl.pallas_call(
        body, name="qkv_prep", grid=(S // tm,),
        in_specs=[row(W3), pl.BlockSpec((HALO, W3), lambda i: (jnp.maximum(i * hb - 1, 0), 0)), row(LANES),
                  full((CONV_K, W3)), full((1, LANES)), full((1, LANES))],
        out_specs=[row(D_MODEL), row(D_MODEL), row(D_MODEL), row(LANES)],
        out_shape=[jax.ShapeDtypeStruct((S, D_MODEL), F32)] * 3 + [jax.ShapeDtypeStruct((S, LANES), F32)],
        compiler_params=_cparams(est, ("arbitrary",)),
    )(proj, proj, ba, convw, arow, dtrow)


def _qkv_prep_bwd(proj, ba, convw, arow, dtrow, dq, dk, dv, dgb, *, tm=256):
    S = proj.shape[0]
    tm = _tile(S, tm, SUBLANES * 2)
    W3 = 3 * D_MODEL
    hb = tm // HALO

    def body(xt_ref, halo_ref, ba_ref, w_ref, a_ref, dt_ref, dq_ref, dk_ref, dv_ref, dgb_ref,
             dcb_ref, dba_ref, dw_ref, da_ref, ddt_ref, dc_ref):
        i = pl.program_id(0)

        @pl.when(i == 0)
        def _():
            dw_ref[...] = jnp.zeros_like(dw_ref)
            da_ref[...] = jnp.zeros_like(da_ref)
            ddt_ref[...] = jnp.zeros_like(ddt_ref)

        c, shifted = _conv_taps(xt_ref[...], halo_ref[...], w_ref, i == 0)
        a, ds = _silu_and_grad(c)
        for h in range(N_HEADS):
            for base, d_ref in ((0, dq_ref), (D_MODEL, dk_ref)):
                lo = base + h * D_HEAD
                _, vj = jax.vjp(_l2n, a[:, lo:lo + D_HEAD])
                (dx,) = vj(d_ref[:, h * D_HEAD:(h + 1) * D_HEAD])
                dc_ref[:, lo:lo + D_HEAD] = dx * ds[:, lo:lo + D_HEAD]
        dc_ref[:, 2 * D_MODEL:] = dv_ref[...] * ds[:, 2 * D_MODEL:]
        dc = dc_ref[...]
        dcb_ref[...] = dc.astype(BF)
        for s in range(CONV_K):
            kk = CONV_K - 1 - s
            dw_ref[kk:kk + 1, :] += jnp.sum(dc * shifted[s], axis=0, keepdims=True)
        _, vj = jax.vjp(_gates, ba_ref[...], a_ref[...], dt_ref[...])
        dba, da, ddt = vj(dgb_ref[...])
        dba_ref[...] = dba.astype(BF)
        da_ref[...] += _bcast_rows(da)
        ddt_ref[...] += _bcast_rows(ddt)

    row = lambda w, col=0: pl.BlockSpec((tm, w), lambda i: (i, col))
    full = lambda shape: pl.BlockSpec(shape, lambda i: (0,) * len(shape))
    est = 8 * _nbytes((tm, W3), F32)
    return pl.pallas_call(
        body, name="qkv_prep_bwd", grid=(S // tm,),
        in_specs=[row(W3), pl.BlockSpec((HALO, W3), lambda i: (jnp.maximum(i * hb - 1, 0), 0)), row(LANES),
                  full((CONV_K, W3)), full((1, LANES)), full((1, LANES)),
                  row(D_MODEL), row(D_MODEL), row(D_MODEL), row(LANES)],
        out_specs=[row(W3), row(LANES), full((SUBLANES, W3)), full((SUBLANES, LANES)), full((SUBLANES, LANES))],
        out_shape=[jax.ShapeDtypeStruct((S, W3), BF), jax.ShapeDtypeStruct((S, LANES), BF),
                   jax.ShapeDtypeStruct((SUBLANES, W3), F32), jax.ShapeDtypeStruct((SUBLANES, LANES), F32),
                   jax.ShapeDtypeStruct((SUBLANES, LANES), F32)],
        scratch_shapes=[pltpu.VMEM((tm, W3), F32)],
        compiler_params=_cparams(est, ("arbitrary",)),
    )(proj, proj, ba, convw, arow, dtrow, dq, dk, dv, dgb)


def _conv_bwd(dc, convw, dproj, *, tm=256):
    S, W3 = dc.shape
    tm = _tile(S, tm, HALO_BF)
    hb = tm // HALO_BF
    nt = S // tm

    def body(dc_ref, nxt_ref, w_ref, dproj_ref, o_ref):
        last = pl.program_id(0) == nt - 1
        nxt = jnp.where(last, 0.0, nxt_ref[...].astype(F32))
        cur = dc_ref[...].astype(F32)
        xc = jnp.concatenate([cur, nxt], axis=0)
        out = cur * w_ref[CONV_K - 1:CONV_K, :]
        for s in range(1, CONV_K):
            out = out + pltpu.roll(xc, tm + HALO_BF - s, 0)[:tm] * w_ref[CONV_K - 1 - s:CONV_K - s, :]
        o_ref[...] = out.astype(BF)

    est = 5 * _nbytes((tm, W3), F32)
    return pl.pallas_call(
        body, name="conv_bwd", grid=(nt,),
        in_specs=[pl.BlockSpec((tm, W3), lambda i: (i, 0)),
                  pl.BlockSpec((HALO_BF, W3), lambda i: (jnp.minimum((i + 1) * hb, S // HALO_BF - 1), 0)),
                  pl.BlockSpec((CONV_K, W3), lambda i: (0, 0)), pl.BlockSpec(memory_space=pl.ANY)],
        out_specs=pl.BlockSpec((tm, W3), lambda i: (i, 0)),
        out_shape=jax.ShapeDtypeStruct(dproj.shape, BF),
        input_output_aliases={3: 0},
        compiler_params=_cparams(est, ("parallel",)),
    )(dc, dc, convw, dproj)


def _inv_unit_lower(A):
    C = A.shape[-1]
    row, col = _iota((C, C), 0), _iota((C, C), 1)
    T = jnp.broadcast_to(jnp.where(row == col, 1.0, 0.0).astype(F32), A.shape)
    b = 1
    while b < C:
        hi = ~(2 * b - 1)
        off = ((row & hi) == (col & hi)) & ((row & b) != 0) & ((col & b) == 0)
        T = T - _dotf(_dotf(T, jnp.where(off, A, 0.0)), T)
        b *= 2
    return T


def _delta_common(q, k, g, beta):
    C = q.shape[-2]
    row, col = _iota((C, C), 0), _iota((C, C), 1)
    tril = row >= col
    qs = q * (D_HEAD ** -0.5)
    gcb = _dot01(jnp.where(tril, 1.0, 0.0), jnp.broadcast_to(g, g.shape[:-1] + (LANES,)))
    gc = gcb[..., :1]
    Dm = jnp.exp(jnp.where(tril, gc - jnp.swapaxes(gcb, -1, -2), -1e30))
    eg = jnp.exp(gc)
    gl = jnp.sum(jnp.where(_iota((C, 1), 0) == C - 1, gc, 0.0), axis=(-2, -1), keepdims=True)
    el = jnp.exp(gl)
    er = jnp.exp(gl - gc)
    kb = k * beta
    KK = _dot_nt(kb, k)
    QK = _dot_nt(qs, k)
    return dict(row=row, col=col, tril=tril, qs=qs, gc=gc, Dm=Dm, eg=eg, el=el, er=er, kb=kb, KK=KK, QK=QK)


def _delta_chunk_fwd(S0, q, k, v, g, beta, T=None):
    m = _delta_common(q, k, g, beta)
    if T is None:
        T = _inv_unit_lower(jnp.where(m["row"] > m["col"], m["KK"] * m["Dm"], 0.0))
    u = _dotf(T, v * beta)
    w = _dotf(T, m["kb"] * m["eg"])
    vn = u - _dot(w, S0)
    o = _dot(m["qs"] * m["eg"], S0) + _dot(m["QK"] * m["Dm"], vn)
    S1 = S0 * m["el"] + _dot_tn(k * m["er"], vn)
    return o, S1, T


def _delta_chunk_bwd(S0, q, k, v, g, beta, T, do, dS1):
    m = _delta_common(q, k, g, beta)
    C = q.shape[-2]
    qs, Dm, eg, el, er, kb, KK, QK = (m[n] for n in ("qs", "Dm", "eg", "el", "er", "kb", "KK", "QK"))
    strict = m["row"] > m["col"]
    total = lambda x: jnp.sum(x, axis=(-2, -1), keepdims=True)
    ru, rw = v * beta, kb * eg
    u = _dotf(T, ru)
    w = _dotf(T, rw)
    vn = u - _dot(w, S0)
    P = QK * Dm
    qg = qs * eg
    kr = k * er

    dvn = _dot_tn(P, do) + _dot(kr, dS1)
    dS0 = dS1 * el + _dot_tn(qg, do) - _dot_tn(w, dvn)
    d_el = total(dS1 * S0)
    dqg = _dot_nt(do, S0)
    dqs = dqg * eg
    deg = jnp.sum(dqg * qs, -1, keepdims=True)
    dP = _dot_nt(do, vn)
    dPD = dP * Dm
    dqs = dqs + _dot(dPD, k)
    dk = _dot_tn(dPD, qs)
    dD = dP * QK
    dkr = _dot_nt(vn, dS1)
    dk = dk + dkr * er
    der = jnp.sum(dkr * k, -1, keepdims=True)
    dw = -_dot_nt(dvn, S0)
    dru = _dotf_tn(T, dvn)
    drw = _dotf_tn(T, dw)
    dT = _dotf_nt(dvn, ru) + _dotf_nt(dw, rw)
    dA = -_dotf_nt(_dotf_tn(T, dT), T)
    dAm = jnp.where(strict, dA, 0.0)
    dKK = dAm * Dm
    dkb = _dot(dKK, k)
    dk = dk + _dot_tn(dKK, kb)
    dD = dD + dAm * KK
    dv = dru * beta
    dbeta = jnp.sum(dru * v, -1, keepdims=True)
    dkb = dkb + drw * eg
    deg = deg + jnp.sum(drw * kb, -1, keepdims=True)
    dk = dk + dkb * beta
    dbeta = dbeta + jnp.sum(dkb * k, -1, keepdims=True)
    E = dD * Dm
    dgc = jnp.sum(E, -1, keepdims=True) - jnp.sum(jnp.swapaxes(E, -1, -2), -1, keepdims=True)
    dgc = dgc + deg * eg - der * er
    dgl = total(der * er) + d_el * el
    dgc = dgc + jnp.where(_iota((C, 1), 0) == C - 1, dgl, 0.0)
    triu = jnp.where(m["row"] <= m["col"], 1.0, 0.0)
    dg = _dot01(triu, jnp.broadcast_to(dgc, dgc.shape[:-1] + (LANES,)))[..., :1]
    dq = dqs * (D_HEAD ** -0.5)
    return dq, dk, dv, dg, dbeta, dS0


def _head_cols(gb, h):
    lane = _iota(gb.shape, 1)
    beta = jnp.sum(jnp.where(lane == h, gb, 0.0), -1, keepdims=True)
    g = jnp.sum(jnp.where(lane == N_HEADS + h, gb, 0.0), -1, keepdims=True)
    return g, beta


def _delta_fwd(q, k, v, gb):
    S = q.shape[0]
    C = DELTA_CHUNK
    N = S // C

    HB = DELTA_HEADS_PER_STEP

    def body(q_ref, k_ref, v_ref, gb_ref, o_ref, st_ref, t_ref, s_scr):
        n, hb = pl.program_id(0), pl.program_id(1)
        gb = gb_ref[...]

        @pl.when(n == 0)
        def _():
            for hh in range(HB):
                s_scr[hb * HB + hh] = jnp.zeros((D_HEAD, D_HEAD), F32)

        heads = [hb * HB + hh for hh in range(HB)]
        cols = [slice(hh * D_HEAD, (hh + 1) * D_HEAD) for hh in range(HB)]
        per_head = lambda ref: jnp.stack([ref[:, c] for c in cols])
        g, beta = (jnp.stack(t) for t in zip(*[_head_cols(gb, h) for h in heads]))
        S0 = jnp.stack([s_scr[h] for h in heads])
        o, S1, T = _delta_chunk_fwd(S0, per_head(q_ref), per_head(k_ref), per_head(v_ref), g, beta)
        for hh in range(HB):
            st_ref[hh, 0] = S0[hh]
            t_ref[hh, 0] = T[hh]
            o_ref[:, cols[hh]] = o[hh]
            s_scr[heads[hh]] = S1[hh]

    hd = pl.BlockSpec((C, HB * D_HEAD), lambda n, h: (n, h))
    mat = pl.BlockSpec((HB, 1, D_HEAD, D_HEAD), lambda n, h: (h, n, 0, 0))
    est = 40 * HB * _nbytes((C, D_HEAD), F32)
    return pl.pallas_call(
        body, name="delta_fwd", grid=(N, N_HEADS // HB),
        in_specs=[hd, hd, hd, pl.BlockSpec((C, LANES), lambda n, h: (n, 0))],
        out_specs=[hd, mat, mat],
        out_shape=[jax.ShapeDtypeStruct((S, N_HEADS * D_HEAD), F32),
                   jax.ShapeDtypeStruct((N_HEADS, N, D_HEAD, D_HEAD), F32),
                   jax.ShapeDtypeStruct((N_HEADS, N, C, C), F32)],
        scratch_shapes=[pltpu.VMEM((N_HEADS, D_HEAD, D_HEAD), F32)],
        compiler_params=_cparams(est, ("arbitrary", "arbitrary")),
    )(q, k, v, gb)


def _delta_bwd(q, k, v, gb, st, tinv, do):
    S = q.shape[0]
    C = DELTA_CHUNK
    N = S // C

    HB = DELTA_HEADS_PER_STEP

    def body(q_ref, k_ref, v_ref, gb_ref, st_ref, t_ref, do_ref, dq_ref, dk_ref, dv_ref, dgb_ref, ds_scr):
        n, hb = pl.program_id(0), pl.program_id(1)
        gb = gb_ref[...]
        lane = _iota((C, LANES), 1)
        dgb = jnp.zeros((C, LANES), F32)

        @pl.when(n == 0)
        def _():
            for hh in range(HB):
                ds_scr[hb * HB + hh] = jnp.zeros((D_HEAD, D_HEAD), F32)

        heads = [hb * HB + hh for hh in range(HB)]
        cols = [slice(hh * D_HEAD, (hh + 1) * D_HEAD) for hh in range(HB)]
        per_head = lambda ref: jnp.stack([ref[:, c] for c in cols])
        g, beta = (jnp.stack(t) for t in zip(*[_head_cols(gb, h) for h in heads]))
        dS1 = jnp.stack([ds_scr[h] for h in heads])
        dq, dk, dv, dg, dbeta, dS0 = _delta_chunk_bwd(
            st_ref[:, 0], per_head(q_ref), per_head(k_ref), per_head(v_ref), g, beta, t_ref[:, 0], per_head(do_ref), dS1)
        for hh, h in enumerate(heads):
            dq_ref[:, cols[hh]] = dq[hh]
            dk_ref[:, cols[hh]] = dk[hh]
            dv_ref[:, cols[hh]] = dv[hh]
            dgb = dgb + jnp.where(lane == h, dbeta[hh], 0.0) + jnp.where(lane == N_HEADS + h, dg[hh], 0.0)
            ds_scr[h] = dS0[hh]

        @pl.when(hb == 0)
        def _():
            dgb_ref[...] = dgb

        @pl.when(hb > 0)
        def _():
            dgb_ref[...] += dgb

    hd = pl.BlockSpec((C, HB * D_HEAD), lambda n, h: (N - 1 - n, h))
    mat = pl.BlockSpec((HB, 1, D_HEAD, D_HEAD), lambda n, h: (h, N - 1 - n, 0, 0))
    gbs = pl.BlockSpec((C, LANES), lambda n, h: (N - 1 - n, 0))
    est = 60 * HB * _nbytes((C, D_HEAD), F32)
    return pl.pallas_call(
        body, name="delta_bwd", grid=(N, N_HEADS // HB),
        in_specs=[hd, hd, hd, gbs, mat, mat, hd],
        out_specs=[hd, hd, hd, gbs],
        out_shape=[jax.ShapeDtypeStruct((S, N_HEADS * D_HEAD), F32)] * 3 + [jax.ShapeDtypeStruct((S, LANES), F32)],
        scratch_shapes=[pltpu.VMEM((N_HEADS, D_HEAD, D_HEAD), F32)],
        compiler_params=_cparams(est, ("arbitrary", "arbitrary")),
    )(q, k, v, gb, st, tinv, do)


def _ya_head(o, z, onw):
    return o * lax.rsqrt(jnp.mean(o * o, -1, keepdims=True) + RMS_EPS) * onw * _silu(z)


def _sgu_pre(u, vg, sg, sb):
    return _gelu(u), _ln(_gelu(vg), sg, sb)


def _chunk_causal(shape, di, dj):
    sh = jnp.int32(int(math.log2(SGU_CHUNK)))
    return lax.shift_right_logical(_iota(shape, di), sh) >= lax.shift_right_logical(_iota(shape, dj), sh)


def _ws_masked(ws):
    return jnp.where(_chunk_causal(ws.shape, 1, 2), ws, 0.0)


def _mix_prep(o, proj, onw, sg, sb, ws, bst, *, tm=256):
    S = o.shape[0]
    tm = _tile(S, tm, SGU_BLOCK)

    def body(o_ref, z_ref, u_ref, vg_ref, onw_ref, sg_ref, sb_ref, ws_ref, bst_ref, ya_ref, yb_ref):
        onw = onw_ref[...]
        for h in range(N_HEADS):
            sl = slice(h * D_HEAD, (h + 1) * D_HEAD)
            ya_ref[:, sl] = _ya_head(o_ref[:, sl], z_ref[:, sl].astype(F32), onw).astype(BF)
        ua, vl = _sgu_pre(u_ref[...].astype(F32), vg_ref[...].astype(F32), sg_ref[...], sb_ref[...])
        wsm = _ws_masked(ws_ref[...])
        bst = bst_ref[...]
        for blk in range(tm // SGU_BLOCK):
            rs = slice(blk * SGU_BLOCK, (blk + 1) * SGU_BLOCK)
            for gi in range(SGU_GROUPS):
                cs = slice(gi * D_HEAD, (gi + 1) * D_HEAD)
                sp = _dot(wsm[gi], vl[rs, cs]) + bst[:, gi:gi + 1]
                yb_ref[rs, cs] = (ua[rs, cs] * sp).astype(BF)

    blk = lambda col: pl.BlockSpec((tm, D_MODEL), lambda i: (i, col))
    full = lambda shape: pl.BlockSpec(shape, lambda i: (0,) * len(shape))
    est = 10 * _nbytes((tm, D_MODEL), F32)
    return pl.pallas_call(
        body, name="mix_prep", grid=(S // tm,),
        in_specs=[blk(0), blk(0), blk(1), blk(2), full((1, D_HEAD)), full((1, D_MODEL)), full((1, D_MODEL)),
                  full((SGU_GROUPS, SGU_BLOCK, SGU_BLOCK)), full((SGU_BLOCK, LANES))],
        out_specs=[blk(0), blk(0)],
        out_shape=[jax.ShapeDtypeStruct((S, D_MODEL), BF)] * 2,
        compiler_params=_cparams(est, ("parallel",)),
    )(o, proj, proj, proj, onw, sg, sb, ws, bst)


def _mix_prep_bwd(o, proj, onw, sg, sb, ws, bst, dya, dyb, dproj, *, tm=256):
    S = o.shape[0]
    tm = _tile(S, tm, SGU_BLOCK)

    def body(o_ref, z_ref, u_ref, vg_ref, onw_ref, sg_ref, sb_ref, ws_ref, bst_ref, dya_ref, dyb_ref, dproj_in,
             do_ref, dzuv_ref, donw_ref, dsg_ref, dsb_ref, dws_ref, dbst_ref, dvl_scr, dua_scr):
        dz_ref, du_ref, dvg_ref = (dzuv_ref.at[:, k * D_MODEL:(k + 1) * D_MODEL] for k in range(3))
        @pl.when(pl.program_id(0) == 0)
        def _():
            for r in (donw_ref, dsg_ref, dsb_ref, dws_ref, dbst_ref):
                r[...] = jnp.zeros_like(r)

        onw = onw_ref[...]
        donw = jnp.zeros((1, D_HEAD), F32)
        for h in range(N_HEADS):
            sl = slice(h * D_HEAD, (h + 1) * D_HEAD)
            _, vj = jax.vjp(_ya_head, o_ref[:, sl], z_ref[:, sl].astype(F32), onw)
            do_h, dz_h, donw_h = vj(dya_ref[:, sl])
            do_ref[:, sl] = do_h.astype(BF)
            dz_ref[:, sl] = dz_h.astype(BF)
            donw = donw + donw_h
        donw_ref[...] += _bcast_rows(donw)

        (ua, vl), vj = jax.vjp(_sgu_pre, u_ref[...].astype(F32), vg_ref[...].astype(F32), sg_ref[...], sb_ref[...])
        wsm = _ws_masked(ws_ref[...])
        bst = bst_ref[...]
        lane = _iota((SGU_BLOCK, LANES), 1)
        dbst = jnp.zeros((SGU_BLOCK, LANES), F32)
        cmask = _chunk_causal((SGU_BLOCK, SGU_BLOCK), 0, 1)
        for gi in range(SGU_GROUPS):
            cs = slice(gi * D_HEAD, (gi + 1) * D_HEAD)
            wg = wsm[gi]
            wgt = jnp.transpose(wg)
            dwg = jnp.zeros((SGU_BLOCK, SGU_BLOCK), F32)
            for blk in range(tm // SGU_BLOCK):
                rs = slice(blk * SGU_BLOCK, (blk + 1) * SGU_BLOCK)
                sp = _dot(wg, vl[rs, cs]) + bst[:, gi:gi + 1]
                dyb = dyb_ref[rs, cs]
                dsp = dyb * ua[rs, cs]
                dua_scr[rs, cs] = dyb * sp
                dvl_scr[rs, cs] = _dot(wgt, dsp)
                dwg = dwg + _dot_nt(dsp, vl[rs, cs])
                dbst = dbst + jnp.where(lane == gi, jnp.sum(dsp, -1, keepdims=True), 0.0)
            dws_ref[gi] += jnp.where(cmask, dwg, 0.0)
        dbst_ref[...] += dbst
        du, dvg, dsg, dsb = vj((dua_scr[...], dvl_scr[...]))
        du_ref[...] = du.astype(BF)
        dvg_ref[...] = dvg.astype(BF)
        dsg_ref[...] += _bcast_rows(dsg)
        dsb_ref[...] += _bcast_rows(dsb)

    blk = lambda col: pl.BlockSpec((tm, D_MODEL), lambda i: (i, col))
    full = lambda shape: pl.BlockSpec(shape, lambda i: (0,) * len(shape))
    est = 16 * _nbytes((tm, D_MODEL), F32)
    outs = pl.pallas_call(
        body, name="mix_prep_bwd", grid=(S // tm,),
        in_specs=[blk(0), blk(0), blk(1), blk(2), full((1, D_HEAD)), full((1, D_MODEL)), full((1, D_MODEL)),
                  full((SGU_GROUPS, SGU_BLOCK, SGU_BLOCK)), full((SGU_BLOCK, LANES)), blk(0), blk(0),
                  pl.BlockSpec(memory_space=pl.ANY)],
        out_specs=[blk(0), pl.BlockSpec((tm, 3 * D_MODEL), lambda i: (i, 1)),
                   full((SUBLANES, D_HEAD)), full((SUBLANES, D_MODEL)), full((SUBLANES, D_MODEL)),
                   full((SGU_GROUPS, SGU_BLOCK, SGU_BLOCK)), full((SGU_BLOCK, LANES))],
        out_shape=[jax.ShapeDtypeStruct((S, D_MODEL), BF), jax.ShapeDtypeStruct(dproj.shape, BF),
                   jax.ShapeDtypeStruct((SUBLANES, D_HEAD), F32), jax.ShapeDtypeStruct((SUBLANES, D_MODEL), F32),
                   jax.ShapeDtypeStruct((SUBLANES, D_MODEL), F32),
                   jax.ShapeDtypeStruct((SGU_GROUPS, SGU_BLOCK, SGU_BLOCK), F32),
                   jax.ShapeDtypeStruct((SGU_BLOCK, LANES), F32)],
        input_output_aliases={11: 1},
        scratch_shapes=[pltpu.VMEM((tm, D_MODEL), F32)] * 2,
        compiler_params=_cparams(est, ("arbitrary",)),
    )(o, proj, proj, proj, onw, sg, sb, ws, bst, dya, dyb, dproj)
    return outs


def _gate_merge(pa, pb, proj, *, tm=512):
    S = pa.shape[0]
    tm = _tile(S, tm, SUBLANES * 2)

    def body(pa_ref, pb_ref, ga_ref, gb_ref, m_ref):
        pa, pb = pa_ref[...].astype(F32), pb_ref[...].astype(F32)
        m_ref[...] = (_sigmoid(ga_ref[...].astype(F32)) * pa + _sigmoid(gb_ref[...].astype(F32)) * pb).astype(BF)

    blk = lambda col: pl.BlockSpec((tm, D_MODEL), lambda i: (i, col))
    return pl.pallas_call(
        body, name="gate_merge", grid=(S // tm,),
        in_specs=[blk(0), blk(0), blk(3), blk(4)], out_specs=blk(0),
        out_shape=jax.ShapeDtypeStruct((S, D_MODEL), BF),
        compiler_params=_cparams(6 * _nbytes((tm, D_MODEL), F32), ("parallel",)),
    )(pa, pb, proj, proj)


def _gate_merge_bwd(pa, pb, proj, dm, *, tm=512):
    S = pa.shape[0]
    tm = _tile(S, tm, SUBLANES * 2)

    def body(pa_ref, pb_ref, ga_ref, gb_ref, dm_ref, dpa_ref, dpb_ref, dg_ref):
        dm = dm_ref[...].astype(F32)
        sa, sb = _sigmoid(ga_ref[...].astype(F32)), _sigmoid(gb_ref[...].astype(F32))
        dpa_ref[...] = (dm * sa).astype(BF)
        dpb_ref[...] = (dm * sb).astype(BF)
        dg_ref[:, :D_MODEL] = (dm * pa_ref[...].astype(F32) * sa * (1.0 - sa)).astype(BF)
        dg_ref[:, D_MODEL:] = (dm * pb_ref[...].astype(F32) * sb * (1.0 - sb)).astype(BF)

    blk = lambda col: pl.BlockSpec((tm, D_MODEL), lambda i: (i, col))
    return pl.pallas_call(
        body, name="gate_merge_bwd", grid=(S // tm,),
        in_specs=[blk(0), blk(0), blk(3), blk(4), blk(0)],
        out_specs=[blk(0), blk(0), pl.BlockSpec((tm, 2 * D_MODEL), lambda i: (i, 3))],
        out_shape=[jax.ShapeDtypeStruct((S, D_MODEL), BF)] * 2 + [jax.ShapeDtypeStruct((S, 8 * D_MODEL), BF)],
        compiler_params=_cparams(10 * _nbytes((tm, D_MODEL), F32), ("parallel",)),
    )(pa, pb, proj, proj, dm)


def _swiglu_act(hgu, *, tm=256):
    S = hgu.shape[0]
    tm = _tile(S, tm, SUBLANES * 2)

    def body(hg_ref, hu_ref, h_ref):
        h_ref[...] = (_silu(hg_ref[...].astype(F32)) * hu_ref[...].astype(F32)).astype(BF)

    blk = lambda col: pl.BlockSpec((tm, FFN_K), lambda i: (i, col))
    return pl.pallas_call(
        body, name="swiglu_act", grid=(S // tm,),
        in_specs=[blk(0), blk(1)], out_specs=blk(0),
        out_shape=jax.ShapeDtypeStruct((S, FFN_K), BF),
        compiler_params=_cparams(5 * _nbytes((tm, FFN_K), F32), ("parallel",)),
    )(hgu, hgu)


def _swiglu_bwd(hgu, dh, *, tm=256):
    S = hgu.shape[0]
    tm = _tile(S, tm, SUBLANES * 2)

    def body(hg_ref, hu_ref, dh_ref, d_ref):
        hg, dh = hg_ref[...].astype(F32), dh_ref[...].astype(F32)
        act, dact = _silu_and_grad(hg)
        d_ref[:, :FFN_K] = (dh * hu_ref[...].astype(F32) * dact).astype(BF)
        d_ref[:, FFN_K:] = (dh * act).astype(BF)

    blk = lambda col: pl.BlockSpec((tm, FFN_K), lambda i: (i, col))
    return pl.pallas_call(
        body, name="swiglu_bwd", grid=(S // tm,),
        in_specs=[blk(0), blk(1), blk(0)], out_specs=pl.BlockSpec((tm, 2 * FFN_K), lambda i: (i, 0)),
        out_shape=jax.ShapeDtypeStruct((S, 2 * FFN_K), BF),
        compiler_params=_cparams(8 * _nbytes((tm, FFN_K), F32), ("parallel",)),
    )(hgu, hgu, dh)


def _resid_ln(x, r, g, b, *, tm=512):
    S = x.shape[0]
    tm = _tile(S, tm, SUBLANES * 2)

    def body(x_ref, r_ref, g_ref, b_ref, y_ref, yb_ref):
        y = _ln(ALPHA * x_ref[...] + r_ref[...], g_ref[...], b_ref[...])
        y_ref[...] = y
        yb_ref[...] = y.astype(BF)

    blk = pl.BlockSpec((tm, D_MODEL), lambda i: (i, 0))
    vec = pl.BlockSpec((1, D_MODEL), lambda i: (0, 0))
    return pl.pallas_call(
        body, name="resid_ln", grid=(S // tm,),
        in_specs=[blk, blk, vec, vec], out_specs=[blk, blk],
        out_shape=[jax.ShapeDtypeStruct((S, D_MODEL), F32), jax.ShapeDtypeStruct((S, D_MODEL), BF)],
        compiler_params=_cparams(6 * _nbytes((tm, D_MODEL), F32), ("parallel",)),
    )(x, r, g, b)


def _resid_ln_bwd(x, r, g, b, dy, *, tm=512):
    S = x.shape[0]
    tm = _tile(S, tm, SUBLANES)

    def body(x_ref, r_ref, g_ref, b_ref, dy_ref, dp_ref, dg_ref, db_ref):
        @pl.when(pl.program_id(0) == 0)
        def _():
            dg_ref[...] = jnp.zeros_like(dg_ref)
            db_ref[...] = jnp.zeros_like(db_ref)

        _, vj = jax.vjp(_ln, ALPHA * x_ref[...] + r_ref[...], g_ref[...], b_ref[...])
        dp, dg, db = vj(dy_ref[...])
        dp_ref[...] = dp
        dg_ref[...] += _bcast_rows(dg)
        db_ref[...] += _bcast_rows(db)

    blk = pl.BlockSpec((tm, D_MODEL), lambda i: (i, 0))
    vec = pl.BlockSpec((1, D_MODEL), lambda i: (0, 0))
    acc = pl.BlockSpec((SUBLANES, D_MODEL), lambda i: (0, 0))
    return pl.pallas_call(
        body, name="resid_ln_bwd", grid=(S // tm,),
        in_specs=[blk, blk, vec, vec, blk], out_specs=[blk, acc, acc],
        out_shape=[jax.ShapeDtypeStruct((S, D_MODEL), F32)] + [jax.ShapeDtypeStruct((SUBLANES, D_MODEL), F32)] * 2,
        compiler_params=_cparams(10 * _nbytes((tm, D_MODEL), F32), ("arbitrary",)),
    )(x, r, g, b, dy)


def _loss_head(y, tgt, *, tm=512):
    S = y.shape[0]
    tm = _tile(S, tm, SUBLANES)

    def body(y_ref, t_ref, dy_ref, l_ref):
        @pl.when(pl.program_id(0) == 0)
        def _():
            l_ref[...] = jnp.zeros_like(l_ref)

        e = y_ref[...] - t_ref[...]
        dy_ref[...] = e * (1.0 / D_MODEL)
        l_ref[...] += 0.5 * jnp.sum(jnp.mean(e * e, -1, keepdims=True), keepdims=True)

    blk = pl.BlockSpec((tm, D_MODEL), lambda i: (i, 0))
    return pl.pallas_call(
        body, name="loss_head", grid=(S // tm,),
        in_specs=[blk, blk], out_specs=[blk, pl.BlockSpec((SUBLANES, LANES), lambda i: (0, 0))],
        out_shape=[jax.ShapeDtypeStruct((S, D_MODEL), F32), jax.ShapeDtypeStruct((SUBLANES, LANES), F32)],
        compiler_params=_cparams(6 * _nbytes((tm, D_MODEL), F32), ("arbitrary",)),
    )(y, tgt)


def _layer_fwd(x, xb, w, late):
    pq = _mm(xb, w["win"], mode="nn", name="mm_in_qkv", tm=1024, tn=1024, cols=(0, 3 * D_MODEL))
    proj = _mm(xb, w["win"], mode="nn", name="mm_in_rest", tm=1024, tn=1024, cols=(3 * D_MODEL, 5 * D_MODEL), out_dtype=BF)
    ba = _mm(xb, w["wba"], mode="nn", name="mm_in_ba", tm=1024, tn=LANES)
    qn, kn, vv, gb = _qkv_prep(pq, ba, w["convw"], w["arow"], w["dtrow"])
    o, st, tinv = _delta_fwd(qn, kn, vv, gb)
    ya, yb = _mix_prep(o, proj, w["onw"], w["sg"], w["sb"], w["ws"], w["bst"])
    w = {**w, **late(ya)}
    pa = _mm(ya, w["wpa"], mode="nn", name="mm_sq", out_dtype=BF)
    pb = _mm(yb, w["wpb"], mode="nn", name="mm_sq", out_dtype=BF)
    m = _gate_merge(pa, pb, proj)
    mix = _mm(m, w["wo"], mode="nn", name="mm_sq_res")
    x1, x1b = _resid_ln(x, mix, w["ln1g"], w["ln1b"])
    hgu = _mm(x1b, w["wgu"], mode="nn", name="mm_gu", tm=1024, tn=1536, out_dtype=BF)
    h = _swiglu_act(hgu)
    ffn = _mm(h, w["wd"], mode="nn", name="mm_down", tk=FFN_K)
    x2, x2b = _resid_ln(x1, ffn, w["ln2g"], w["ln2b"])
    saved = dict(x=x, xb=xb, pq=pq, proj=proj, ba=ba, qn=qn, kn=kn, vv=vv, gb=gb, o=o, st=st, tinv=tinv, ya=ya, yb=yb,
                 pa=pa, pb=pb, m=m, mix=mix, x1=x1, x1b=x1b, hgu=hgu, h=h, ffn=ffn)
    return x2, x2b, saved, w


def _layer_bwd(dx2, w, s, on_part=None):
    g = {}
    started = lambda part: on_part(part, g) if on_part is not None else None
    after = lambda v, token: v if token is None else v + token.astype(v.dtype)
    dpre2, g["ln2g"], g["ln2b"] = _resid_ln_bwd(s["x1"], s["ffn"], w["ln2g"], w["ln2b"], dx2)
    dh = _mm(dpre2, w["wd"], mode="nt", name="mm_nt_down", tm=1024, tn=1536, out_dtype=BF)
    g["wd"] = _mm(s["h"], dpre2, mode="tn", name="mm_tn_down", tm=1536, tk=1024, out_dtype=BF)
    dhgu = _swiglu_bwd(s["hgu"], dh)
    dx1 = _mm(dhgu, w["wgu"], mode="nt", name="mm_nt_gu", add=dpre2, add_scale=ALPHA, tm=1024, tk=1536)
    g["wgu"] = _mm(s["x1b"], dhgu, mode="tn", name="mm_tn_gu", tm=1024, tn=1536, tk=2048, out_dtype=BF)
    dpre1, g["ln1g"], g["ln1b"] = _resid_ln_bwd(s["x"], s["mix"], w["ln1g"], w["ln1b"], dx1)
    dm = _mm(dpre1, w["wo"], mode="nt", name="mm_nt_sq_bf", out_dtype=BF)
    g["wo"] = _mm(s["m"], dpre1, mode="tn", name="mm_tn_sq", tm=1024, tk=1024, out_dtype=BF)
    dpa, dpb, dproj = _gate_merge_bwd(s["pa"], s["pb"], s["proj"], dm)
    dya = _mm(dpa, w["wpa"], mode="nt", name="mm_nt_sq")
    g["wpa"] = _mm(s["ya"], dpa, mode="tn", name="mm_tn_sq", tm=1024, tk=1024, out_dtype=BF)
    dyb = _mm(dpb, w["wpb"], mode="nt", name="mm_nt_sq")
    g["wpb"] = _mm(s["yb"], dpb, mode="tn", name="mm_tn_sq", tm=1024, tk=1024, out_dtype=BF)
    do, dproj, g["onw"], g["sg"], g["sb"], g["ws"], g["bst"] = _mix_prep_bwd(
        s["o"], s["proj"], after(w["onw"], started("late")), w["sg"], w["sb"], w["ws"], w["bst"], dya, dyb, dproj)
    dqn, dkn, dvv, dgb = _delta_bwd(s["qn"], s["kn"], s["vv"], s["gb"], s["st"], s["tinv"], do)
    dc, dba, g["convw"], g["arow"], g["dtrow"] = _qkv_prep_bwd(
        s["pq"], s["ba"], w["convw"], w["arow"], w["dtrow"], dqn, dkn, dvv, dgb)
    dproj = _conv_bwd(dc, w["convw"], dproj)
    g["win"] = _mm(s["xb"], dproj, mode="tn", name="mm_tn_in", tm=1024, tn=1024, tk=2048, out_dtype=BF)
    g["wba"] = _mm(s["xb"], dba, mode="tn", name="mm_tn_ba", tm=1024, tn=LANES, tk=1024, out_dtype=BF)
    dx = _mm(dba, after(w["wba"], started("early")), mode="nt", name="mm_nt_ba", add=dpre1, add_scale=ALPHA, tm=1024)
    dx = _mm(dproj, w["win"], mode="nt", name="mm_nt_in", add=dx, add_scale=1.0, tm=1024, tk=2048)
    return dx, g


def _local_step(x, tgt, layers, on_grads=None):
    saved, weights = [], []
    xb = x.astype(BF)
    for layer in layers:
        x, xb, s, w = _layer_fwd(x, xb, *layer(x))
        saved.append(s)
        weights.append(w)
    dy, lacc = _loss_head(x, tgt)
    grads = [None] * len(layers)
    for l in reversed(range(len(layers))):
        on_part = functools.partial(on_grads, l) if on_grads is not None else None
        dy, grads[l] = _layer_bwd(dy, weights[l], saved[l], on_part)
    return lacc[0, 0], dy, grads


_QKVZ = 4 * D_MODEL
_BA = 2 * N_HEADS


WEIGHT_NAMES = ("w_in", "conv_w", "a_log", "dt_bias", "o_norm_w", "sgu_ln_g", "sgu_ln_b", "w_s", "b_s", "w_pa", "w_pb",
                "w_o", "ln1_g", "ln1_b", "w_ffn_gate", "w_ffn_up", "w_ffn_down", "ln2_g", "ln2_b")
WIRE = ("w_in", "w_ffn_gate", "w_ffn_up", "w_ffn_down", "w_pa", "w_pb", "w_o", "conv_w")
SMALL = (("a_log", N_HEADS), ("dt_bias", N_HEADS), ("o_norm_w", D_HEAD), ("sgu_ln_g", D_MODEL), ("sgu_ln_b", D_MODEL),
         ("w_s", SGU_GROUPS * SGU_BLOCK * SGU_BLOCK), ("b_s", SGU_GROUPS * SGU_BLOCK),
         ("ln1_g", D_MODEL), ("ln1_b", D_MODEL), ("ln2_g", D_MODEL), ("ln2_b", D_MODEL))
SMALL_ROWS = -(-sum(n for _, n in SMALL) // (LANES * SUBLANES)) * SUBLANES
N_MAIN_TILES = (N_IN - _BA) // D_MODEL
ADAM_TILES = dict(w_in=(128, "adamw_in"), w_ffn_gate=(256, "adamw_ffn_cols"), w_ffn_up=(256, "adamw_ffn_cols"),
                  w_ffn_down=(32, "adamw_ffn_rows"), w_pa=(128, "adamw_sq"), w_pb=(128, "adamw_sq"), w_o=(128, "adamw_sq"),
                  conv_w=(CONV_K, "adamw_conv"))


def _pad_to(a, axis, size):
    pads = [(0, 0)] * a.ndim
    pads[axis] = (0, size - a.shape[axis])
    return jnp.pad(a, pads)


def _wire_blocks(p):
    return dict(
        w_in=_pad_to(p["w_in"].astype(BF), 2, IN_PAD),
        w_ffn_gate=_pad_to(p["w_ffn_gate"].astype(BF), 2, FFN_PAD), w_ffn_up=_pad_to(p["w_ffn_up"].astype(BF), 2, FFN_PAD),
        w_ffn_down=_pad_to(p["w_ffn_down"].astype(BF), 1, FFN_PAD),
        w_pa=p["w_pa"].astype(BF), w_pb=p["w_pb"].astype(BF), w_o=p["w_o"].astype(BF),
        conv_w=_pad_to(p["conv_w"], 1, SUBLANES),
    )


def _by_columns(blocks):
    n, r, c = blocks.shape
    return jnp.transpose(blocks, (1, 0, 2)).reshape(r, n * c)


def _to_slots(full, c):
    r = full.shape[0]
    return jnp.transpose(full.reshape(r, N_DEV, c), (1, 0, 2))


def _lane_row(v, at):
    return jnp.pad(v[None], ((0, 0), (at, LANES - at - v.shape[0])))


EARLY = ("w_in", "conv_w")
LATE = ("w_pa", "w_pb", "w_o", "w_ffn_gate", "w_ffn_up", "w_ffn_down")


def _early_weights(stacks, p, l):
    return dict(
        win=_perm_in(stacks["w_in"], D_MODEL, N_MAIN_TILES), wba=_perm_in(stacks["w_in"], LANES, 1),
        convw=_by_columns(stacks["conv_w"][:, :CONV_K]),
        arow=_lane_row(p["a_log"][l], N_HEADS), dtrow=_lane_row(p["dt_bias"][l], N_HEADS),
        onw=p["o_norm_w"][l][None], sg=p["sgu_ln_g"][l][None], sb=p["sgu_ln_b"][l][None],
        ws=p["w_s"][l], bst=_pad_to(p["b_s"][l].T, 1, LANES),
        ln1g=p["ln1_g"][l][None], ln1b=p["ln1_b"][l][None], ln2g=p["ln2_g"][l][None], ln2b=p["ln2_b"][l][None],
    )


def _late_weights(stacks):
    return dict(
        wpa=stacks["w_pa"].reshape(D_MODEL, D_MODEL), wpb=stacks["w_pb"].reshape(D_MODEL, D_MODEL),
        wo=stacks["w_o"].reshape(D_MODEL, D_MODEL),
        wgu=_by_columns(jnp.concatenate([stacks["w_ffn_gate"], stacks["w_ffn_up"]], axis=0)),
        wd=stacks["w_ffn_down"].reshape(FFN_K, D_MODEL),
    )


def _small_pack(parts):
    flat = jnp.concatenate([parts[n].reshape(-1) for n, _ in SMALL])
    return _pad_to(flat, 0, SMALL_ROWS * LANES).reshape(SMALL_ROWS, LANES)


def _small_unpack(rows, like):
    flat, out, off = rows.reshape(-1), {}, 0
    for n, size in SMALL:
        out[n] = flat[off:off + size].reshape(like[n].shape[1:])
        off += size
    return out


def _late_slots(g):
    slots = dict(
        w_ffn_gate=_to_slots(g["wgu"][:, :FFN_K], FFN_PAD), w_ffn_up=_to_slots(g["wgu"][:, FFN_K:], FFN_PAD),
        w_ffn_down=g["wd"].reshape(N_DEV, FFN_PAD, D_MODEL),
        w_pa=g["wpa"].reshape(N_DEV, D_MODEL // N_DEV, D_MODEL), w_pb=g["wpb"].reshape(N_DEV, D_MODEL // N_DEV, D_MODEL),
        w_o=g["wo"].reshape(N_DEV, D_MODEL // N_DEV, D_MODEL),
    )
    return [slots[n] for n in LATE]


def _early_slots(g):
    slots = [_perm_out(g["win"], g["wba"]), _pad_to(_to_slots(g["convw"][:CONV_K], 3 * D_MODEL // N_DEV), 1, SUBLANES)]
    small = _small_pack(dict(
        a_log=g["arow"][0, N_HEADS:2 * N_HEADS], dt_bias=g["dtrow"][0, N_HEADS:2 * N_HEADS], o_norm_w=g["onw"][0],
        sgu_ln_g=g["sg"][0], sgu_ln_b=g["sb"][0], w_s=g["ws"], b_s=g["bst"][:, :SGU_GROUPS].T,
        ln1_g=g["ln1g"][0], ln1_b=g["ln1b"][0], ln2_g=g["ln2g"][0], ln2_b=g["ln2b"][0]))
    return slots, small


def _in_tile_start(j, tile_w):
    if tile_w == LANES:
        return jnp.int32(_QKVZ)
    return j * D_MODEL + jnp.where(j >= _QKVZ // D_MODEL, _BA, 0)


def _select(rows_iota, cols_iota, dev, start, valid):
    hit = (rows_iota + (dev * IN_BLOCK - start) == cols_iota) & (rows_iota < IN_BLOCK) & (cols_iota < valid)
    return jnp.where(hit, 1.0, 0.0).astype(BF)


def _perm_in(stack, tile_w, n_tiles):
    valid = _BA if tile_w == LANES else tile_w

    def first_dev(j):
        return lax.div(_in_tile_start(j, tile_w), jnp.int32(IN_BLOCK))

    def body(w_ref, o_ref, acc_ref):
        j, k = pl.program_id(0), pl.program_id(1)
        sel = _select(_iota((IN_PAD, tile_w), 0), _iota((IN_PAD, tile_w), 1), first_dev(j) + k,
                      _in_tile_start(j, tile_w), valid)
        part = jnp.dot(w_ref[0], sel, preferred_element_type=F32)

        @pl.when(k == 0)
        def _():
            acc_ref[...] = part

        @pl.when(k == 1)
        def _():
            o_ref[...] = (acc_ref[...] + part).astype(BF)

    est = _nbytes((D_MODEL, IN_PAD), BF) + 3 * _nbytes((D_MODEL, tile_w), F32) + 2 * _nbytes((IN_PAD, tile_w), F32)
    return pl.pallas_call(
        body, name="perm_in" if tile_w != LANES else "perm_in_ba", grid=(n_tiles, 2),
        in_specs=[pl.BlockSpec((1, D_MODEL, IN_PAD), lambda j, k: (jnp.minimum(first_dev(j) + k, N_DEV - 1), 0, 0))],
        out_specs=pl.BlockSpec((D_MODEL, tile_w), lambda j, k: (0, j)),
        out_shape=jax.ShapeDtypeStruct((D_MODEL, n_tiles * tile_w), BF),
        scratch_shapes=[pltpu.VMEM((D_MODEL, tile_w), F32)],
        compiler_params=_cparams(est, ("parallel", "arbitrary")),
    )(stack)


def _perm_out(dmain, dba):
    def tile(d, s):
        c0 = d * IN_BLOCK
        first = lax.div(c0 - jnp.where(c0 < _QKVZ, 0, jnp.minimum(c0 - _QKVZ, _BA)), jnp.int32(D_MODEL))
        return jnp.minimum(first + jnp.minimum(s, 1), N_MAIN_TILES - 1)

    def body(dm_ref, db_ref, o_ref, acc_ref):
        d, s = pl.program_id(0), pl.program_id(1)

        @pl.when(s == 0)
        def _():
            acc_ref[...] = jnp.zeros_like(acc_ref)

        start = _in_tile_start(tile(d, s), D_MODEL)
        overlaps = (start < (d + 1) * IN_BLOCK) & (d * IN_BLOCK < start + D_MODEL)

        @pl.when((s < 2) & overlaps)
        def _():
            sel = _select(_iota((D_MODEL, IN_PAD), 1), _iota((D_MODEL, IN_PAD), 0), d, start, D_MODEL)
            acc_ref[...] += jnp.dot(dm_ref[...], sel, preferred_element_type=F32)

        @pl.when(s == 2)
        def _():
            sel = _select(_iota((LANES, IN_PAD), 1), _iota((LANES, IN_PAD), 0), d, jnp.int32(_QKVZ), _BA)
            o_ref[0] = (acc_ref[...] + jnp.dot(db_ref[...], sel, preferred_element_type=F32)).astype(BF)

    est = 2 * _nbytes((D_MODEL, D_MODEL), BF) + 4 * _nbytes((D_MODEL, IN_PAD), F32)
    return pl.pallas_call(
        body, name="perm_out", grid=(N_DEV, 3),
        in_specs=[pl.BlockSpec((D_MODEL, D_MODEL), lambda d, s: (0, tile(d, s))),
                  pl.BlockSpec((D_MODEL, LANES), lambda d, s: (0, 0))],
        out_specs=pl.BlockSpec((1, D_MODEL, IN_PAD), lambda d, t: (d, 0, 0)),
        out_shape=jax.ShapeDtypeStruct((N_DEV, D_MODEL, IN_PAD), BF),
        scratch_shapes=[pltpu.VMEM((D_MODEL, IN_PAD), F32)],
        compiler_params=_cparams(est, ("parallel", "arbitrary")),
    )(dmain, dba)


def _mesh_place():
    x, y, c = (lax.axis_index(a) for a in MESH_AXES)
    return x, y, c


def _slot(x, y, c):
    return 4 * x + 2 * y + c


def _peer(place, j):
    x, y, c = place
    return (1 - x if j & 4 else x, 1 - y if j & 2 else y, 1 - c if j & 1 else c)


_HBM = pl.BlockSpec(memory_space=pltpu.HBM)
_SEM = pl.BlockSpec(memory_space=pltpu.SEMAPHORE)
_EFFECT = pltpu.SideEffectType.DATAFLOW_SIDE_EFFECTING


def _remote_copy(src_ref, land_ref, slot, per_slot, pslot, sems, u, j, peer):
    return pltpu.make_async_remote_copy(
        src_ref=src_ref.at[pslot] if per_slot else src_ref, dst_ref=land_ref.at[slot],
        send_sem=sems[0].at[u * (N_DEV - 1) + j - 1], recv_sem=sems[1].at[u * (N_DEV - 1) + j - 1],
        device_id=peer, device_id_type=pl.DeviceIdType.MESH)


def _own_copy(src_ref, land_ref, me, per_slot, sems, u):
    return pltpu.make_async_copy(src_ref.at[me] if per_slot else src_ref, land_ref.at[me], sems[2].at[u])


def _exchange_start(name, srcs, per_slot):
    n = len(srcs)
    lands = [jax.ShapeDtypeStruct(s.shape if p else (N_DEV,) + s.shape, s.dtype) for s, p in zip(srcs, per_slot)]

    def body(*refs):
        src_refs, sems, land_refs, token = refs[:n], refs[n:n + 3], refs[2 * n + 3:3 * n + 3], refs[-1]
        place = _mesh_place()
        me = _slot(*place)
        for u in range(n):
            _own_copy(src_refs[u], land_refs[u], me, per_slot[u], sems, u).start()
            for j in range(1, N_DEV):
                peer = _peer(place, j)
                _remote_copy(src_refs[u], land_refs[u], me, per_slot[u], _slot(*peer), sems, u, j, peer).start()
        token[...] = jnp.zeros_like(token)

    hbm = lambda a: pltpu.HBM(a.shape, a.dtype)
    sem = pltpu.SemaphoreType.DMA((n * (N_DEV - 1),))
    outs = pl.pallas_call(
        body, name=name,
        out_shape=(sem, sem, pltpu.SemaphoreType.DMA((n,)), *[hbm(a) for a in srcs], *[hbm(a) for a in lands],
                   jax.ShapeDtypeStruct((SUBLANES, LANES), F32)),
        in_specs=[_HBM] * n, out_specs=(_SEM, _SEM, _SEM, *[_HBM] * (2 * n), pl.BlockSpec(memory_space=pltpu.VMEM)),
        input_output_aliases={i: 3 + i for i in range(n)},
        compiler_params=pltpu.CompilerParams(has_side_effects=_EFFECT),
    )(*[pltpu.with_memory_space_constraint(a, pltpu.HBM) for a in srcs])
    return tuple(outs[:3]), list(outs[3:3 + n]), list(outs[3 + n:3 + 2 * n]), outs[-1]


def _exchange_wait(name, sems, srcs, lands, units, per_slot, after):
    m = len(units)

    def body(*refs):
        src_refs, land_refs, sem_refs = refs[:m], refs[m:2 * m], refs[2 * m:2 * m + 3]
        place = _mesh_place()
        me = _slot(*place)
        for i, u in enumerate(units):
            _own_copy(src_refs[i], land_refs[i], me, per_slot[u], sem_refs, u).wait()
            for j in range(1, N_DEV):
                peer = _peer(place, j)
                pslot = _slot(*peer)
                cp = _remote_copy(src_refs[i], land_refs[i], pslot, per_slot[u], pslot, sem_refs, u, j, peer)
                cp.wait_send()
                cp.wait_recv()

    hbm = lambda a: pltpu.HBM(a.shape, a.dtype)
    outs = pl.pallas_call(
        body, name=name, out_shape=tuple(hbm(a) for a in list(srcs) + list(lands)),
        in_specs=[_HBM] * (2 * m) + [_SEM] * 3 + [pl.BlockSpec(memory_space=pl.ANY)], out_specs=tuple([_HBM] * (2 * m)),
        input_output_aliases={i: i for i in range(2 * m)},
        compiler_params=pltpu.CompilerParams(has_side_effects=_EFFECT),
    )(*srcs, *lands, *sems, after)
    return list(outs[m:])


def _adam_update(g, w, m, v):
    m = ADAM_B1 * m + (1.0 - ADAM_B1) * g
    v = ADAM_B2 * v + (1.0 - ADAM_B2) * jnp.square(g)
    m_hat = m / (1.0 - ADAM_B1 ** ADAM_STEP)
    v_hat = v / (1.0 - ADAM_B2 ** ADAM_STEP)
    return -ADAM_LR * (m_hat / (jnp.sqrt(v_hat) + ADAM_EPS) + ADAM_WD * w), m, v


def _adamw(recvs, w, m, v, *, tr, name):
    L, R, C = w.shape
    rp = max(tr, SUBLANES * (4 // jnp.dtype(recvs[0].dtype).itemsize))
    Cp = recvs[0].shape[2]

    def body(*refs):
        r_refs, (w_ref, m_ref, v_ref, g_ref, d_ref, nm_ref, nv_ref) = refs[:L], refs[L:]
        for l in range(L):
            @pl.when(pl.program_id(0) == l)
            def _(r_ref=r_refs[l]):
                g = r_ref[0, :tr, :C].astype(F32)
                for s in range(1, N_DEV):
                    g = g + r_ref[s, :tr, :C].astype(F32)
                d, nm, nv = _adam_update(g, w_ref[0], m_ref[0], v_ref[0])
                g_ref[0], d_ref[0], nm_ref[0], nv_ref[0] = g, d, nm, nv

    blk = pl.BlockSpec((1, tr, C), lambda l, i: (l, i, 0))
    r_specs = [pl.BlockSpec((N_DEV, rp, Cp), lambda l, i, k=k: (0, jnp.where(l == k, i, 0), 0)) for k in range(L)]
    est = 2 * _nbytes((N_DEV, rp, Cp), recvs[0].dtype) + 8 * _nbytes((tr, Cp), F32)
    return pl.pallas_call(
        body, name=name, grid=(L, R // tr),
        in_specs=r_specs + [blk] * 3, out_specs=[blk] * 4,
        out_shape=[jax.ShapeDtypeStruct((L, R, C), F32)] * 4,
        compiler_params=_cparams(est, ("arbitrary", "arbitrary")),
    )(*recvs, w, m, v)


def _adamw_small(recv, w, m, v):
    def body(r_ref, w_ref, m_ref, v_ref, g_ref, d_ref, nm_ref, nv_ref):
        g = r_ref[0]
        for s in range(1, N_DEV):
            g = g + r_ref[s]
        g_ref[...] = g
        d_ref[...], nm_ref[...], nv_ref[...] = _adam_update(g, w_ref[...], m_ref[...], v_ref[...])

    vm = pl.BlockSpec(memory_space=pltpu.VMEM)
    return pl.pallas_call(
        body, name="adamw_small", in_specs=[vm] * 4, out_specs=[vm] * 4,
        out_shape=[jax.ShapeDtypeStruct((SMALL_ROWS, LANES), F32)] * 4,
        compiler_params=_cparams(20 * _nbytes((SMALL_ROWS, LANES), F32)),
    )(recv, w, m, v)


def kernel(x, w_in, conv_w, a_log, dt_bias, o_norm_w, sgu_ln_g, sgu_ln_b, w_s, b_s, w_pa, w_pb, w_o, ln1_g, ln1_b, w_ffn_gate, w_ffn_up, w_ffn_down, ln2_g, ln2_b, loss_target, m_w_in, m_conv_w, m_a_log, m_dt_bias, m_o_norm_w, m_sgu_ln_g, m_sgu_ln_b, m_w_s, m_b_s, m_w_pa, m_w_pb, m_w_o, m_ln1_g, m_ln1_b, m_w_ffn_gate, m_w_ffn_up, m_w_ffn_down, m_ln2_g, m_ln2_b, v_w_in, v_conv_w, v_a_log, v_dt_bias, v_o_norm_w, v_sgu_ln_g, v_sgu_ln_b, v_w_s, v_b_s, v_w_pa, v_w_pb, v_w_o, v_ln1_g, v_ln1_b, v_w_ffn_gate, v_w_ffn_up, v_w_ffn_down, v_ln2_g, v_ln2_b):
    given = dict(locals())
    P = {n: given[n] for n in WEIGHT_NAMES}
    M = {n: given["m_" + n] for n in WEIGHT_NAMES}
    V = {n: given["v_" + n] for n in WEIGHT_NAMES}

    wire = _wire_blocks(P)
    units = [(n, l) for l in range(DEPTH) for n in EARLY + LATE]
    whole = [False] * len(units)
    g_sems, g_srcs, g_lands, g_token = _exchange_start("gather_start", [wire[n][l] for n, l in units], whole)

    def gathered(name, names, l, after):
        idx = [units.index((n, l)) for n in names]
        got = _exchange_wait(name, g_sems, [g_srcs[i] for i in idx], [g_lands[i] for i in idx], idx, whole, after)
        return dict(zip(names, got))

    def layer(l):
        def weights(x_in):
            after = g_token if l == 0 else x_in
            early = _early_weights(gathered(f"gather_wait_early{l}", EARLY, l, after), P, l)
            return early, lambda ya: _late_weights(gathered(f"gather_wait_late{l}", LATE, l, ya))
        return weights

    pending = {}

    def on_grads(l, part, g):
        if part == "late":
            srcs, names = _late_slots(g), LATE
            per_slot = [True] * len(srcs)
        else:
            slots, small = _early_slots(g)
            srcs, names = slots + [small], EARLY + ("small",)
            per_slot = [True] * len(slots) + [False]
        sems, s_thru, l_thru, token = _exchange_start(f"exchange_start_{part}{l}", srcs, per_slot)
        pending[l, part] = (names, sems, s_thru, l_thru, per_slot)
        return token[0, 0]

    loss_local, dx, _ = _local_step(x[0], loss_target[0], [layer(l) for l in range(DEPTH)], on_grads)
    loss = lax.psum(loss_local, MESH_AXES)

    recv = [{} for _ in range(DEPTH)]
    for l in reversed(range(DEPTH)):
        for part in ("late", "early"):
            names, sems, s_thru, l_thru, per_slot = pending[l, part]
            got = _exchange_wait(f"exchange_wait_{part}{l}", sems, s_thru, l_thru, list(range(len(s_thru))), per_slot, dx)
            recv[l].update(zip(names, got))

    out = {}
    for n in WIRE:
        tr, name = ADAM_TILES[n]
        out[n] = _adamw([recv[l][n] for l in range(DEPTH)], P[n], M[n], V[n], tr=tr, name=name)
    small = [_adamw_small(recv[l]["small"], *[_small_pack({n: T[n][l] for n, _ in SMALL}) for T in (P, M, V)])
             for l in range(DEPTH)]
    for n, _ in SMALL:
        out[n] = [jnp.stack([_small_unpack(small[l][i], P)[n] for l in range(DEPTH)]) for i in range(4)]
    return (loss, dx[None], *[out[n][i] for i in range(4) for n in WEIGHT_NAMES])
```

```python
import functools
import math

import jax
import jax.numpy as jnp
from jax import lax
from jax.experimental import pallas as pl
from jax.experimental.pallas import tpu as pltpu

F32 = jnp.float32
BF = jnp.bfloat16
HIGHEST = lax.Precision.HIGHEST

D_MODEL = 1024
DEPTH = 2
N_HEADS = 8
D_HEAD = 128
CONV_K = 4
SGU_BLOCK = 128
SGU_GROUPS = 8
SGU_CHUNK = 64
FFN_HIDDEN = 2816
N_IN = 8208
N_DEV = 8
IN_BLOCK, IN_PAD = N_IN // N_DEV, 1152
FFN_BLOCK, FFN_PAD = FFN_HIDDEN // N_DEV, 384
FFN_K = N_DEV * FFN_PAD
ALPHA = (2 * DEPTH) ** 0.25
LN_EPS = 1e-5
RMS_EPS = 1e-6
ADAM_LR, ADAM_B1, ADAM_B2, ADAM_EPS, ADAM_WD, ADAM_STEP = 0.001, 0.9, 0.999, 1e-08, 0.01, 10

MESH_AXES = ("x", "y", "c")
DELTA_CHUNK = 128
DELTA_HEADS_PER_STEP = 8
LANES = 128
SUBLANES = 8
VMEM_BYTES = 64 * 1024 * 1024
HALO = SUBLANES
HALO_BF = 2 * SUBLANES


def _cparams(est_bytes, dims=None):
    limit = int(min(max(2 * est_bytes + (8 << 20), 32 << 20), VMEM_BYTES - (6 << 20)))
    kw = dict(vmem_limit_bytes=limit)
    if dims is not None:
        kw["dimension_semantics"] = dims
    return pltpu.CompilerParams(**kw)


def _nbytes(shape, dtype):
    return math.prod(shape) * jnp.dtype(dtype).itemsize


def _dims(kind, ndim):
    lhs, rhs = {"nn": (1, 0), "nt": (1, 1), "tn": (0, 0)}[kind]
    b = ndim - 2
    return (((lhs + b,), (rhs + b,)), (tuple(range(b)), tuple(range(b))))


def _mxu(a, b, kind):
    return lax.dot_general(a, b, _dims(kind, a.ndim), preferred_element_type=F32)


def _dot(a, b):
    return _mxu(a.astype(BF), b.astype(BF), "nn")


def _dot_nt(a, b):
    return _mxu(a.astype(BF), b.astype(BF), "nt")


def _dot_tn(a, b):
    return _mxu(a.astype(BF), b.astype(BF), "tn")


def _split(a):
    hi = a.astype(BF)
    return hi, (a - hi.astype(F32)).astype(BF)


def _dot3(a, b, kind):
    (ah, al), (bh, bl) = _split(a), _split(b)
    return _mxu(ah, bh, kind) + (_mxu(ah, bl, kind) + _mxu(al, bh, kind))


def _dotf(a, b):
    return _dot3(a, b, "nn")


def _dotf_nt(a, b):
    return _dot3(a, b, "nt")


def _dotf_tn(a, b):
    return _dot3(a, b, "tn")


def _dot01(sel, x, kind="nn"):
    s = jnp.broadcast_to(sel.astype(BF), x.shape[:-2] + sel.shape)
    h1 = x.astype(BF)
    r1 = x - h1.astype(F32)
    h2 = r1.astype(BF)
    h3 = (r1 - h2.astype(F32)).astype(BF)
    return _mxu(s, h1, kind) + (_mxu(s, h2, kind) + _mxu(s, h3, kind))


def _sigmoid(x):
    return 0.5 * jnp.tanh(0.5 * x) + 0.5


def _silu(x):
    return x * _sigmoid(x)


def _silu_and_grad(x):
    s = _sigmoid(x)
    return x * s, s * (1.0 + x * (1.0 - s))


def _gelu(x):
    return 0.5 * x * (1.0 + lax.erf(x * 0.7071067811865476))


def _softplus(x):
    return jnp.maximum(x, 0.0) + jnp.log1p(jnp.exp(-jnp.abs(x)))


def _ln(x, g, b):
    mu = jnp.mean(x, -1, keepdims=True)
    xc = x - mu
    var = jnp.mean(xc * xc, -1, keepdims=True)
    return xc * lax.rsqrt(var + LN_EPS) * g + b


def _iota(shape, dim):
    return lax.broadcasted_iota(jnp.int32, shape, dim)


def _tile(n, pref, align):
    if n <= pref:
        return n
    t = (pref // align) * align
    while t >= align:
        if n % t == 0:
            return t
        t -= align
    raise ValueError(f"no tile for {n} (pref {pref}, align {align})")


def _bcast_rows(v, rows=SUBLANES):
    return jnp.broadcast_to(v, (rows, v.shape[-1]))


def _mm(a, b, *, mode, name, out_dtype=F32, add=None, add_scale=1.0, tm=512, tn=1024, tk=1024, cols=None):
    if mode == "nn":
        (M, K), N = a.shape, b.shape[1]
    elif mode == "nt":
        (M, K), N = a.shape, b.shape[0]
    else:
        (K, M), N = a.shape, b.shape[1]
    col0 = 0
    if cols is not None:
        col0, N = cols
    tm = _tile(M, tm, LANES if mode == "tn" else SUBLANES * 2)
    tn = _tile(N, tn, LANES)
    tk = _tile(K, tk, LANES)
    nk = K // tk
    j0 = col0 // tn
    if mode == "nn":
        a_spec = pl.BlockSpec((tm, tk), lambda i, j, k: (i, k))
        b_spec = pl.BlockSpec((tk, tn), lambda i, j, k: (k, j + j0))
        dot = _dot
    elif mode == "nt":
        a_spec = pl.BlockSpec((tm, tk), lambda i, j, k: (i, k))
        b_spec = pl.BlockSpec((tn, tk), lambda i, j, k: (j, k))
        dot = _dot_nt
    else:
        a_spec = pl.BlockSpec((tk, tm), lambda i, j, k: (k, i))
        b_spec = pl.BlockSpec((tk, tn), lambda i, j, k: (k, j))
        dot = _dot_tn
    o_spec = pl.BlockSpec((tm, tn), lambda i, j, k: (i, j))
    has_add = add is not None

    def body(*refs):
        if has_add:
            a_ref, b_ref, add_ref, o_ref, acc_ref = refs
        else:
            a_ref, b_ref, o_ref, acc_ref = refs
            add_ref = None
        k = pl.program_id(2)
        part = dot(a_ref[...], b_ref[...])

        def finish(total):
            if has_add:
                total = total + add_scale * add_ref[...]
            o_ref[...] = total.astype(out_dtype)

        if nk == 1:
            finish(part)
        else:
            @pl.when(k == 0)
            def _():
                acc_ref[...] = part

            @pl.when(jnp.logical_and(k > 0, k < nk - 1))
            def _():
                acc_ref[...] += part

            @pl.when(k == nk - 1)
            def _():
                finish(acc_ref[...] + part)

    in_specs = [a_spec, b_spec] + ([o_spec] if has_add else [])
    args = (a, b) + ((add,) if has_add else ())
    est = (_nbytes((tm, tk), a.dtype) + _nbytes((tk, tn), b.dtype) + 2 * _nbytes((tm, tn), F32)
           + (_nbytes((tm, tn), F32) if has_add else 0)) + 2 * _nbytes((tm, tn), F32)
    return pl.pallas_call(
        body, name=name,
        grid=(M // tm, N // tn, nk),
        in_specs=in_specs, out_specs=o_spec,
        out_shape=jax.ShapeDtypeStruct((M, N), out_dtype),
        scratch_shapes=[pltpu.VMEM((tm, tn) if nk > 1 else (SUBLANES, LANES), F32)],
        compiler_params=_cparams(est, ("parallel", "parallel", "arbitrary")),
    )(*args)


def _conv_taps(xt, halo, w_ref, first):
    halo = jnp.where(first, 0.0, halo)
    xc = jnp.concatenate([halo, xt], axis=0)
    shifted = [xt] + [pltpu.roll(xc, s, 0)[HALO:] for s in range(1, CONV_K)]
    out = shifted[0] * w_ref[CONV_K - 1:CONV_K, :]
    for s in range(1, CONV_K):
        out = out + shifted[s] * w_ref[CONV_K - 1 - s:CONV_K - s, :]
    return out, shifted


def _gates(ba, arow, dtrow):
    lane = _iota(ba.shape, 1)
    beta = _sigmoid(ba)
    g = -jnp.exp(arow) * _softplus(ba + dtrow)
    return jnp.where(lane < N_HEADS, beta, jnp.where(lane < 2 * N_HEADS, g, 0.0))


def _l2n(x):
    return x * lax.rsqrt(jnp.sum(x * x, -1, keepdims=True) + RMS_EPS)


def _qkv_prep(proj, ba, convw, arow, dtrow, *, tm=256):
    S = proj.shape[0]
    tm = _tile(S, tm, SUBLANES)
    W3 = 3 * D_MODEL
    hb = tm // HALO

    def body(xt_ref, halo_ref, ba_ref, w_ref, a_ref, dt_ref, q_ref, k_ref, v_ref, gb_ref):
        c, _ = _conv_taps(xt_ref[...], halo_ref[...], w_ref, pl.program_id(0) == 0)
        c = _silu(c)
        for h in range(N_HEADS):
            lo = h * D_HEAD
            q_ref[:, lo:lo + D_HEAD] = _l2n(c[:, lo:lo + D_HEAD])
            k_ref[:, lo:lo + D_HEAD] = _l2n(c[:, D_MODEL + lo:D_MODEL + lo + D_HEAD])
        v_ref[...] = c[:, 2 * D_MODEL:]
        gb_ref[...] = _gates(ba_ref[...], a_ref[...], dt_ref[...])

    row = lambda w, col=0: pl.BlockSpec((tm, w), lambda i: (i, col))
    full = lambda shape: pl.BlockSpec(shape, lambda i: (0,) * len(shape))
    est = 4 * _nbytes((tm, W3), F32)
    return pl.pallas_call(
        body, name="qkv_prep", grid=(S // tm,),
        in_specs=[row(W3), pl.BlockSpec((HALO, W3), lambda i: (jnp.maximum(i * hb - 1, 0), 0)), row(LANES),
                  full((CONV_K, W3)), full((1, LANES)), full((1, LANES))],
        out_specs=[row(D_MODEL), row(D_MODEL), row(D_MODEL), row(LANES)],
        out_shape=[jax.ShapeDtypeStruct((S, D_MODEL), F32)] * 3 + [jax.ShapeDtypeStruct((S, LANES), F32)],
        compiler_params=_cparams(est, ("arbitrary",)),
    )(proj, proj, ba, convw, arow, dtrow)


def _qkv_prep_bwd(proj, ba, convw, arow, dtrow, dq, dk, dv, dgb, *, tm=256):
    S = proj.shape[0]
    tm = _tile(S, tm, SUBLANES * 2)
    W3 = 3 * D_MODEL
    hb = tm // HALO

    def body(xt_ref, halo_ref, ba_ref, w_ref, a_ref, dt_ref, dq_ref, dk_ref, dv_ref, dgb_ref,
             dcb_ref, dba_ref, dw_ref, da_ref, ddt_ref, dc_ref):
        i = pl.program_id(0)

        @pl.when(i == 0)
        def _():
            dw_ref[...] = jnp.zeros_like(dw_ref)
            da_ref[...] = jnp.zeros_like(da_ref)
            ddt_ref[...] = jnp.zeros_like(ddt_ref)

        c, shifted = _conv_taps(xt_ref[...], halo_ref[...], w_ref, i == 0)
        a, ds = _silu_and_grad(c)
        for h in range(N_HEADS):
            for base, d_ref in ((0, dq_ref), (D_MODEL, dk_ref)):
                lo = base + h * D_HEAD
                _, vj = jax.vjp(_l2n, a[:, lo:lo + D_HEAD])
                (dx,) = vj(d_ref[:, h * D_HEAD:(h + 1) * D_HEAD])
                dc_ref[:, lo:lo + D_HEAD] = dx * ds[:, lo:lo + D_HEAD]
        dc_ref[:, 2 * D_MODEL:] = dv_ref[...] * ds[:, 2 * D_MODEL:]
        dc = dc_ref[...]
        dcb_ref[...] = dc.astype(BF)
        for s in range(CONV_K):
            kk = CONV_K - 1 - s
            dw_ref[kk:kk + 1, :] += jnp.sum(dc * shifted[s], axis=0, keepdims=True)
        _, vj = jax.vjp(_gates, ba_ref[...], a_ref[...], dt_ref[...])
        dba, da, ddt = vj(dgb_ref[...])
        dba_ref[...] = dba.astype(BF)
        da_ref[...] += _bcast_rows(da)
        ddt_ref[...] += _bcast_rows(ddt)

    row = lambda w, col=0: pl.BlockSpec((tm, w), lambda i: (i, col))
    full = lambda shape: pl.BlockSpec(shape, lambda i: (0,) * len(shape))
    est = 8 * _nbytes((tm, W3), F32)
    return pl.pallas_call(
        body, name="qkv_prep_bwd", grid=(S // tm,),
        in_specs=[row(W3), pl.BlockSpec((HALO, W3), lambda i: (jnp.maximum(i * hb - 1, 0), 0)), row(LANES),
                  full((CONV_K, W3)), full((1, LANES)), full((1, LANES)),
                  row(D_MODEL), row(D_MODEL), row(D_MODEL), row(LANES)],
        out_specs=[row(W3), row(LANES), full((SUBLANES, W3)), full((SUBLANES, LANES)), full((SUBLANES, LANES))],
        out_shape=[jax.ShapeDtypeStruct((S, W3), BF), jax.ShapeDtypeStruct((S, LANES), BF),
                   jax.ShapeDtypeStruct((SUBLANES, W3), F32), jax.ShapeDtypeStruct((SUBLANES, LANES), F32),
                   jax.ShapeDtypeStruct((SUBLANES, LANES), F32)],
        scratch_shapes=[pltpu.VMEM((tm, W3), F32)],
        compiler_params=_cparams(est, ("arbitrary",)),
    )(proj, proj, ba, convw, arow, dtrow, dq, dk, dv, dgb)


def _conv_bwd(dc, convw, dproj, *, tm=256):
    S, W3 = dc.shape
    tm = _tile(S, tm, HALO_BF)
    hb = tm // HALO_BF
    nt = S // tm

    def body(dc_ref, nxt_ref, w_ref, dproj_ref, o_ref):
        last = pl.program_id(0) == nt - 1
        nxt = jnp.where(last, 0.0, nxt_ref[...].astype(F32))
        cur = dc_ref[...].astype(F32)
        xc = jnp.concatenate([cur, nxt], axis=0)
        out = cur * w_ref[CONV_K - 1:CONV_K, :]
        for s in range(1, CONV_K):
            out = out + pltpu.roll(xc, tm + HALO_BF - s, 0)[:tm] * w_ref[CONV_K - 1 - s:CONV_K - s, :]
        o_ref[...] = out.astype(BF)

    est = 5 * _nbytes((tm, W3), F32)
    return pl.pallas_call(
        body, name="conv_bwd", grid=(nt,),
        in_specs=[pl.BlockSpec((tm, W3), lambda i: (i, 0)),
                  pl.BlockSpec((HALO_BF, W3), lambda i: (jnp.minimum((i + 1) * hb, S // HALO_BF - 1), 0)),
                  pl.BlockSpec((CONV_K, W3), lambda i: (0, 0)), pl.BlockSpec(memory_space=pl.ANY)],
        out_specs=pl.BlockSpec((tm, W3), lambda i: (i, 0)),
        out_shape=jax.ShapeDtypeStruct(dproj.shape, BF),
        input_output_aliases={3: 0},
        compiler_params=_cparams(est, ("parallel",)),
    )(dc, dc, convw, dproj)


def _inv_unit_lower(A):
    C = A.shape[-1]
    row, col = _iota((C, C), 0), _iota((C, C), 1)
    T = jnp.broadcast_to(jnp.where(row == col, 1.0, 0.0).astype(F32), A.shape)
    b = 1
    while b < C:
        hi = ~(2 * b - 1)
        off = ((row & hi) == (col & hi)) & ((row & b) != 0) & ((col & b) == 0)
        T = T - _dotf(_dotf(T, jnp.where(off, A, 0.0)), T)
        b *= 2
    return T


def _delta_common(q, k, g, beta):
    C = q.shape[-2]
    row, col = _iota((C, C), 0), _iota((C, C), 1)
    tril = row >= col
    qs = q * (D_HEAD ** -0.5)
    gcb = _dot01(jnp.where(tril, 1.0, 0.0), jnp.broadcast_to(g, g.shape[:-1] + (LANES,)))
    gc = gcb[..., :1]
    Dm = jnp.exp(jnp.where(tril, gc - jnp.swapaxes(gcb, -1, -2), -1e30))
    eg = jnp.exp(gc)
    gl = jnp.sum(jnp.where(_iota((C, 1), 0) == C - 1, gc, 0.0), axis=(-2, -1), keepdims=True)
    el = jnp.exp(gl)
    er = jnp.exp(gl - gc)
    kb = k * beta
    KK = _dot_nt(kb, k)
    QK = _dot_nt(qs, k)
    return dict(row=row, col=col, tril=tril, qs=qs, gc=gc, Dm=Dm, eg=eg, el=el, er=er, kb=kb, KK=KK, QK=QK)


def _delta_chunk_fwd(S0, q, k, v, g, beta, T=None):
    m = _delta_common(q, k, g, beta)
    if T is None:
        T = _inv_unit_lower(jnp.where(m["row"] > m["col"], m["KK"] * m["Dm"], 0.0))
    u = _dotf(T, v * beta)
    w = _dotf(T, m["kb"] * m["eg"])
    vn = u - _dot(w, S0)
    o = _dot(m["qs"] * m["eg"], S0) + _dot(m["QK"] * m["Dm"], vn)
    S1 = S0 * m["el"] + _dot_tn(k * m["er"], vn)
    return o, S1, T


def _delta_chunk_bwd(S0, q, k, v, g, beta, T, do, dS1):
    m = _delta_common(q, k, g, beta)
    C = q.shape[-2]
    qs, Dm, eg, el, er, kb, KK, QK = (m[n] for n in ("qs", "Dm", "eg", "el", "er", "kb", "KK", "QK"))
    strict = m["row"] > m["col"]
    total = lambda x: jnp.sum(x, axis=(-2, -1), keepdims=True)
    ru, rw = v * beta, kb * eg
    u = _dotf(T, ru)
    w = _dotf(T, rw)
    vn = u - _dot(w, S0)
    P = QK * Dm
    qg = qs * eg
    kr = k * er

    dvn = _dot_tn(P, do) + _dot(kr, dS1)
    dS0 = dS1 * el + _dot_tn(qg, do) - _dot_tn(w, dvn)
    d_el = total(dS1 * S0)
    dqg = _dot_nt(do, S0)
    dqs = dqg * eg
    deg = jnp.sum(dqg * qs, -1, keepdims=True)
    dP = _dot_nt(do, vn)
    dPD = dP * Dm
    dqs = dqs + _dot(dPD, k)
    dk = _dot_tn(dPD, qs)
    dD = dP * QK
    dkr = _dot_nt(vn, dS1)
    dk = dk + dkr * er
    der = jnp.sum(dkr * k, -1, keepdims=True)
    dw = -_dot_nt(dvn, S0)
    dru = _dotf_tn(T, dvn)
    drw = _dotf_tn(T, dw)
    dA = -(_dotf_nt(dru, u) + _dotf_nt(drw, w))
    dAm = jnp.where(strict, dA, 0.0)
    dKK = dAm * Dm
    dkb = _dot(dKK, k)
    dk = dk + _dot_tn(dKK, kb)
    dD = dD + dAm * KK
    dv = dru * beta
    dbeta = jnp.sum(dru * v, -1, keepdims=True)
    dkb = dkb + drw * eg
    deg = deg + jnp.sum(drw * kb, -1, keepdims=True)
    dk = dk + dkb * beta
    dbeta = dbeta + jnp.sum(dkb * k, -1, keepdims=True)
    E = dD * Dm
    dgc = jnp.sum(E, -1, keepdims=True) - jnp.sum(jnp.swapaxes(E, -1, -2), -1, keepdims=True)
    dgc = dgc + deg * eg - der * er
    dgl = total(der * er) + d_el * el
    dgc = dgc + jnp.where(_iota((C, 1), 0) == C - 1, dgl, 0.0)
    triu = jnp.where(m["row"] <= m["col"], 1.0, 0.0)
    dg = _dot01(triu, jnp.broadcast_to(dgc, dgc.shape[:-1] + (LANES,)))[..., :1]
    dq = dqs * (D_HEAD ** -0.5)
    return dq, dk, dv, dg, dbeta, dS0


def _head_cols(gb, h):
    lane = _iota(gb.shape, 1)
    beta = jnp.sum(jnp.where(lane == h, gb, 0.0), -1, keepdims=True)
    g = jnp.sum(jnp.where(lane == N_HEADS + h, gb, 0.0), -1, keepdims=True)
    return g, beta


def _delta_fwd(q, k, v, gb):
    S = q.shape[0]
    C = DELTA_CHUNK
    N = S // C

    HB = DELTA_HEADS_PER_STEP

    def body(q_ref, k_ref, v_ref, gb_ref, o_ref, st_ref, t_ref, s_scr):
        n, hb = pl.program_id(0), pl.program_id(1)
        gb = gb_ref[...]

        @pl.when(n == 0)
        def _():
            for hh in range(HB):
                s_scr[hb * HB + hh] = jnp.zeros((D_HEAD, D_HEAD), F32)

        heads = [hb * HB + hh for hh in range(HB)]
        cols = [slice(hh * D_HEAD, (hh + 1) * D_HEAD) for hh in range(HB)]
        per_head = lambda ref: jnp.stack([ref[:, c] for c in cols])
        g, beta = (jnp.stack(t) for t in zip(*[_head_cols(gb, h) for h in heads]))
        S0 = jnp.stack([s_scr[h] for h in heads])
        o, S1, T = _delta_chunk_fwd(S0, per_head(q_ref), per_head(k_ref), per_head(v_ref), g, beta)
        for hh in range(HB):
            st_ref[hh, 0] = S0[hh]
            t_ref[hh, 0] = T[hh]
            o_ref[:, cols[hh]] = o[hh]
            s_scr[heads[hh]] = S1[hh]

    hd = pl.BlockSpec((C, HB * D_HEAD), lambda n, h: (n, h))
    mat = pl.BlockSpec((HB, 1, D_HEAD, D_HEAD), lambda n, h: (h, n, 0, 0))
    est = 40 * HB * _nbytes((C, D_HEAD), F32)
    return pl.pallas_call(
        body, name="delta_fwd", grid=(N, N_HEADS // HB),
        in_specs=[hd, hd, hd, pl.BlockSpec((C, LANES), lambda n, h: (n, 0))],
        out_specs=[hd, mat, mat],
        out_shape=[jax.ShapeDtypeStruct((S, N_HEADS * D_HEAD), F32),
                   jax.ShapeDtypeStruct((N_HEADS, N, D_HEAD, D_HEAD), F32),
                   jax.ShapeDtypeStruct((N_HEADS, N, C, C), F32)],
        scratch_shapes=[pltpu.VMEM((N_HEADS, D_HEAD, D_HEAD), F32)],
        compiler_params=_cparams(est, ("arbitrary", "arbitrary")),
    )(q, k, v, gb)


def _delta_bwd(q, k, v, gb, st, tinv, do):
    S = q.shape[0]
    C = DELTA_CHUNK
    N = S // C

    HB = DELTA_HEADS_PER_STEP

    def body(q_ref, k_ref, v_ref, gb_ref, st_ref, t_ref, do_ref, dq_ref, dk_ref, dv_ref, dgb_ref, ds_scr):
        n, hb = pl.program_id(0), pl.program_id(1)
        gb = gb_ref[...]
        lane = _iota((C, LANES), 1)
        dgb = jnp.zeros((C, LANES), F32)

        @pl.when(n == 0)
        def _():
            for hh in range(HB):
                ds_scr[hb * HB + hh] = jnp.zeros((D_HEAD, D_HEAD), F32)

        heads = [hb * HB + hh for hh in range(HB)]
        cols = [slice(hh * D_HEAD, (hh + 1) * D_HEAD) for hh in range(HB)]
        per_head = lambda ref: jnp.stack([ref[:, c] for c in cols])
        g, beta = (jnp.stack(t) for t in zip(*[_head_cols(gb, h) for h in heads]))
        dS1 = jnp.stack([ds_scr[h] for h in heads])
        dq, dk, dv, dg, dbeta, dS0 = _delta_chunk_bwd(
            st_ref[:, 0], per_head(q_ref), per_head(k_ref), per_head(v_ref), g, beta, t_ref[:, 0], per_head(do_ref), dS1)
        for hh, h in enumerate(heads):
            dq_ref[:, cols[hh]] = dq[hh]
            dk_ref[:, cols[hh]] = dk[hh]
            dv_ref[:, cols[hh]] = dv[hh]
            dgb = dgb + jnp.where(lane == h, dbeta[hh], 0.0) + jnp.where(lane == N_HEADS + h, dg[hh], 0.0)
            ds_scr[h] = dS0[hh]

        @pl.when(hb == 0)
        def _():
            dgb_ref[...] = dgb

        @pl.when(hb > 0)
        def _():
            dgb_ref[...] += dgb

    hd = pl.BlockSpec((C, HB * D_HEAD), lambda n, h: (N - 1 - n, h))
    mat = pl.BlockSpec((HB, 1, D_HEAD, D_HEAD), lambda n, h: (h, N - 1 - n, 0, 0))
    gbs = pl.BlockSpec((C, LANES), lambda n, h: (N - 1 - n, 0))
    est = 60 * HB * _nbytes((C, D_HEAD), F32)
    return pl.pallas_call(
        body, name="delta_bwd", grid=(N, N_HEADS // HB),
        in_specs=[hd, hd, hd, gbs, mat, mat, hd],
        out_specs=[hd, hd, hd, gbs],
        out_shape=[jax.ShapeDtypeStruct((S, N_HEADS * D_HEAD), F32)] * 3 + [jax.ShapeDtypeStruct((S, LANES), F32)],
        scratch_shapes=[pltpu.VMEM((N_HEADS, D_HEAD, D_HEAD), F32)],
        compiler_params=_cparams(est, ("arbitrary", "arbitrary")),
    )(q, k, v, gb, st, tinv, do)


def _ya_head(o, z, onw):
    return o * lax.rsqrt(jnp.mean(o * o, -1, keepdims=True) + RMS_EPS) * onw * _silu(z)


def _sgu_pre(u, vg, sg, sb):
    return _gelu(u), _ln(_gelu(vg), sg, sb)


def _chunk_causal(shape, di, dj):
    sh = jnp.int32(int(math.log2(SGU_CHUNK)))
    return lax.shift_right_logical(_iota(shape, di), sh) >= lax.shift_right_logical(_iota(shape, dj), sh)


def _ws_masked(ws):
    return jnp.where(_chunk_causal(ws.shape, 1, 2), ws, 0.0)


def _mix_prep(o, proj, onw, sg, sb, ws, bst, *, tm=256):
    S = o.shape[0]
    tm = _tile(S, tm, SGU_BLOCK)

    def body(o_ref, z_ref, u_ref, vg_ref, onw_ref, sg_ref, sb_ref, ws_ref, bst_ref, ya_ref, yb_ref):
        onw = onw_ref[...]
        for h in range(N_HEADS):
            sl = slice(h * D_HEAD, (h + 1) * D_HEAD)
            ya_ref[:, sl] = _ya_head(o_ref[:, sl], z_ref[:, sl].astype(F32), onw).astype(BF)
        ua, vl = _sgu_pre(u_ref[...].astype(F32), vg_ref[...].astype(F32), sg_ref[...], sb_ref[...])
        wsm = _ws_masked(ws_ref[...])
        bst = bst_ref[...]
        for blk in range(tm // SGU_BLOCK):
            rs = slice(blk * SGU_BLOCK, (blk + 1) * SGU_BLOCK)
            for gi in range(SGU_GROUPS):
                cs = slice(gi * D_HEAD, (gi + 1) * D_HEAD)
                sp = _dot(wsm[gi], vl[rs, cs]) + bst[:, gi:gi + 1]
                yb_ref[rs, cs] = (ua[rs, cs] * sp).astype(BF)

    blk = lambda col: pl.BlockSpec((tm, D_MODEL), lambda i: (i, col))
    full = lambda shape: pl.BlockSpec(shape, lambda i: (0,) * len(shape))
    est = 10 * _nbytes((tm, D_MODEL), F32)
    return pl.pallas_call(
        body, name="mix_prep", grid=(S // tm,),
        in_specs=[blk(0), blk(0), blk(1), blk(2), full((1, D_HEAD)), full((1, D_MODEL)), full((1, D_MODEL)),
                  full((SGU_GROUPS, SGU_BLOCK, SGU_BLOCK)), full((SGU_BLOCK, LANES))],
        out_specs=[blk(0), blk(0)],
        out_shape=[jax.ShapeDtypeStruct((S, D_MODEL), BF)] * 2,
        compiler_params=_cparams(est, ("parallel",)),
    )(o, proj, proj, proj, onw, sg, sb, ws, bst)


def _mix_prep_bwd(o, proj, onw, sg, sb, ws, bst, dya, dyb, dproj, *, tm=256):
    S = o.shape[0]
    tm = _tile(S, tm, SGU_BLOCK)

    def body(o_ref, z_ref, u_ref, vg_ref, onw_ref, sg_ref, sb_ref, ws_ref, bst_ref, dya_ref, dyb_ref, dproj_in,
             do_ref, dzuv_ref, donw_ref, dsg_ref, dsb_ref, dws_ref, dbst_ref, dvl_scr, dua_scr):
        dz_ref, du_ref, dvg_ref = (dzuv_ref.at[:, k * D_MODEL:(k + 1) * D_MODEL] for k in range(3))
        @pl.when(pl.program_id(0) == 0)
        def _():
            for r in (donw_ref, dsg_ref, dsb_ref, dws_ref, dbst_ref):
                r[...] = jnp.zeros_like(r)

        onw = onw_ref[...]
        donw = jnp.zeros((1, D_HEAD), F32)
        for h in range(N_HEADS):
            sl = slice(h * D_HEAD, (h + 1) * D_HEAD)
            _, vj = jax.vjp(_ya_head, o_ref[:, sl], z_ref[:, sl].astype(F32), onw)
            do_h, dz_h, donw_h = vj(dya_ref[:, sl])
            do_ref[:, sl] = do_h.astype(BF)
            dz_ref[:, sl] = dz_h.astype(BF)
            donw = donw + donw_h
        donw_ref[...] += _bcast_rows(donw)

        (ua, vl), vj = jax.vjp(_sgu_pre, u_ref[...].astype(F32), vg_ref[...].astype(F32), sg_ref[...], sb_ref[...])
        wsm = _ws_masked(ws_ref[...])
        bst = bst_ref[...]
        lane = _iota((SGU_BLOCK, LANES), 1)
        dbst = jnp.zeros((SGU_BLOCK, LANES), F32)
        cmask = _chunk_causal((SGU_BLOCK, SGU_BLOCK), 0, 1)
        for gi in range(SGU_GROUPS):
            cs = slice(gi * D_HEAD, (gi + 1) * D_HEAD)
            wg = wsm[gi]
            wgt = jnp.transpose(wg)
            dwg = jnp.zeros((SGU_BLOCK, SGU_BLOCK), F32)
            for blk in range(tm // SGU_BLOCK):
                rs = slice(blk * SGU_BLOCK, (blk + 1) * SGU_BLOCK)
                sp = _dot(wg, vl[rs, cs]) + bst[:, gi:gi + 1]
                dyb = dyb_ref[rs, cs]
                dsp = dyb * ua[rs, cs]
                dua_scr[rs, cs] = dyb * sp
                dvl_scr[rs, cs] = _dot(wgt, dsp)
                dwg = dwg + _dot_nt(dsp, vl[rs, cs])
                dbst = dbst + jnp.where(lane == gi, jnp.sum(dsp, -1, keepdims=True), 0.0)
            dws_ref[gi] += jnp.where(cmask, dwg, 0.0)
        dbst_ref[...] += dbst
        du, dvg, dsg, dsb = vj((dua_scr[...], dvl_scr[...]))
        du_ref[...] = du.astype(BF)
        dvg_ref[...] = dvg.astype(BF)
        dsg_ref[...] += _bcast_rows(dsg)
        dsb_ref[...] += _bcast_rows(dsb)

    blk = lambda col: pl.BlockSpec((tm, D_MODEL), lambda i: (i, col))
    full = lambda shape: pl.BlockSpec(shape, lambda i: (0,) * len(shape))
    est = 16 * _nbytes((tm, D_MODEL), F32)
    outs = pl.pallas_call(
        body, name="mix_prep_bwd", grid=(S // tm,),
        in_specs=[blk(0), blk(0), blk(1), blk(2), full((1, D_HEAD)), full((1, D_MODEL)), full((1, D_MODEL)),
                  full((SGU_GROUPS, SGU_BLOCK, SGU_BLOCK)), full((SGU_BLOCK, LANES)), blk(0), blk(0),
                  pl.BlockSpec(memory_space=pl.ANY)],
        out_specs=[blk(0), pl.BlockSpec((tm, 3 * D_MODEL), lambda i: (i, 1)),
                   full((SUBLANES, D_HEAD)), full((SUBLANES, D_MODEL)), full((SUBLANES, D_MODEL)),
                   full((SGU_GROUPS, SGU_BLOCK, SGU_BLOCK)), full((SGU_BLOCK, LANES))],
        out_shape=[jax.ShapeDtypeStruct((S, D_MODEL), BF), jax.ShapeDtypeStruct(dproj.shape, BF),
                   jax.ShapeDtypeStruct((SUBLANES, D_HEAD), F32), jax.ShapeDtypeStruct((SUBLANES, D_MODEL), F32),
                   jax.ShapeDtypeStruct((SUBLANES, D_MODEL), F32),
                   jax.ShapeDtypeStruct((SGU_GROUPS, SGU_BLOCK, SGU_BLOCK), F32),
                   jax.ShapeDtypeStruct((SGU_BLOCK, LANES), F32)],
        input_output_aliases={11: 1},
        scratch_shapes=[pltpu.VMEM((tm, D_MODEL), F32)] * 2,
        compiler_params=_cparams(est, ("arbitrary",)),
    )(o, proj, proj, proj, onw, sg, sb, ws, bst, dya, dyb, dproj)
    return outs


def _mm_gate_merge(ya, yb, wpa, wpb, proj, *, tm=512):
    S = ya.shape[0]
    tm = _tile(S, tm, SUBLANES * 2)

    def body(ya_ref, yb_ref, wa_ref, wb_ref, ga_ref, gb_ref, pa_ref, pb_ref, m_ref):
        pa = _dot(ya_ref[...], wa_ref[...]).astype(BF)
        pb = _dot(yb_ref[...], wb_ref[...]).astype(BF)
        pa_ref[...] = pa
        pb_ref[...] = pb
        m_ref[...] = (_sigmoid(ga_ref[...].astype(F32)) * pa.astype(F32)
                      + _sigmoid(gb_ref[...].astype(F32)) * pb.astype(F32)).astype(BF)

    blk = lambda col: pl.BlockSpec((tm, D_MODEL), lambda i: (i, col))
    wsp = pl.BlockSpec((D_MODEL, D_MODEL), lambda i: (0, 0))
    return pl.pallas_call(
        body, name="mm_gate_merge", grid=(S // tm,),
        in_specs=[blk(0), blk(0), wsp, wsp, blk(3), blk(4)], out_specs=[blk(0)] * 3,
        out_shape=[jax.ShapeDtypeStruct((S, D_MODEL), BF)] * 3,
        compiler_params=_cparams(2 * _nbytes((D_MODEL, D_MODEL), BF) + 8 * _nbytes((tm, D_MODEL), F32), ("parallel",)),
    )(ya, yb, wpa, wpb, proj, proj)


def _gate_merge_bwd(pa, pb, proj, dm, *, tm=512):
    S = pa.shape[0]
    tm = _tile(S, tm, SUBLANES * 2)

    def body(pa_ref, pb_ref, ga_ref, gb_ref, dm_ref, dpa_ref, dpb_ref, dg_ref):
        dm = dm_ref[...].astype(F32)
        sa, sb = _sigmoid(ga_ref[...].astype(F32)), _sigmoid(gb_ref[...].astype(F32))
        dpa_ref[...] = (dm * sa).astype(BF)
        dpb_ref[...] = (dm * sb).astype(BF)
        dg_ref[:, :D_MODEL] = (dm * pa_ref[...].astype(F32) * sa * (1.0 - sa)).astype(BF)
        dg_ref[:, D_MODEL:] = (dm * pb_ref[...].astype(F32) * sb * (1.0 - sb)).astype(BF)

    blk = lambda col: pl.BlockSpec((tm, D_MODEL), lambda i: (i, col))
    return pl.pallas_call(
        body, name="gate_merge_bwd", grid=(S // tm,),
        in_specs=[blk(0), blk(0), blk(3), blk(4), blk(0)],
        out_specs=[blk(0), blk(0), pl.BlockSpec((tm, 2 * D_MODEL), lambda i: (i, 3))],
        out_shape=[jax.ShapeDtypeStruct((S, D_MODEL), BF)] * 2 + [jax.ShapeDtypeStruct((S, 8 * D_MODEL), BF)],
        compiler_params=_cparams(10 * _nbytes((tm, D_MODEL), F32), ("parallel",)),
    )(pa, pb, proj, proj, dm)


def _mm_swiglu(xb, wgu, *, tm=1024, tn=768):
    S, K = xb.shape
    tm = _tile(S, tm, SUBLANES * 2)
    tn = _tile(FFN_K, tn, LANES)
    nj = FFN_K // tn

    def body(x_ref, wg_ref, wu_ref, hg_ref, hu_ref, h_ref):
        x = x_ref[...]
        hg = _dot(x, wg_ref[...]).astype(BF)
        hu = _dot(x, wu_ref[...]).astype(BF)
        hg_ref[...] = hg
        hu_ref[...] = hu
        h_ref[...] = (_silu(hg.astype(F32)) * hu.astype(F32)).astype(BF)

    out = pl.BlockSpec((tm, tn), lambda i, j: (i, j))
    est = _nbytes((tm, K), BF) + 2 * _nbytes((K, tn), BF) + 6 * _nbytes((tm, tn), F32)
    return pl.pallas_call(
        body, name="mm_swiglu", grid=(S // tm, nj),
        in_specs=[pl.BlockSpec((tm, K), lambda i, j: (i, 0)), pl.BlockSpec((K, tn), lambda i, j: (0, j)),
                  pl.BlockSpec((K, tn), lambda i, j: (0, j + nj))],
        out_specs=[out] * 3, out_shape=[jax.ShapeDtypeStruct((S, FFN_K), BF)] * 3,
        compiler_params=_cparams(est, ("parallel", "parallel")),
    )(xb, wgu, wgu)


def _swiglu_bwd(hg, hu, dh, *, tm=256):
    S = hg.shape[0]
    tm = _tile(S, tm, SUBLANES * 2)

    def body(hg_ref, hu_ref, dh_ref, d_ref):
        dh = dh_ref[...].astype(F32)
        act, dact = _silu_and_grad(hg_ref[...].astype(F32))
        d_ref[:, :FFN_K] = (dh * hu_ref[...].astype(F32) * dact).astype(BF)
        d_ref[:, FFN_K:] = (dh * act).astype(BF)

    blk = pl.BlockSpec((tm, FFN_K), lambda i: (i, 0))
    return pl.pallas_call(
        body, name="swiglu_bwd", grid=(S // tm,),
        in_specs=[blk, blk, blk], out_specs=pl.BlockSpec((tm, 2 * FFN_K), lambda i: (i, 0)),
        out_shape=jax.ShapeDtypeStruct((S, 2 * FFN_K), BF),
        compiler_params=_cparams(8 * _nbytes((tm, FFN_K), F32), ("parallel",)),
    )(hg, hu, dh)


def _mm_resid_ln(a, bmat, x, g, b, *, name, tm=512):
    S, K = a.shape
    tm = _tile(S, tm, SUBLANES * 2)

    def body(a_ref, w_ref, x_ref, g_ref, b_ref, pre_ref, y_ref, yb_ref):
        pre = ALPHA * x_ref[...] + _dot(a_ref[...], w_ref[...])
        y = _ln(pre, g_ref[...], b_ref[...])
        pre_ref[...] = pre
        y_ref[...] = y
        yb_ref[...] = y.astype(BF)

    blk = pl.BlockSpec((tm, D_MODEL), lambda i: (i, 0))
    vec = pl.BlockSpec((1, D_MODEL), lambda i: (0, 0))
    est = _nbytes((tm, K), BF) + _nbytes((K, D_MODEL), BF) + 8 * _nbytes((tm, D_MODEL), F32)
    return pl.pallas_call(
        body, name=name, grid=(S // tm,),
        in_specs=[pl.BlockSpec((tm, K), lambda i: (i, 0)), pl.BlockSpec((K, D_MODEL), lambda i: (0, 0)), blk, vec, vec],
        out_specs=[blk, blk, blk],
        out_shape=[jax.ShapeDtypeStruct((S, D_MODEL), F32)] * 2 + [jax.ShapeDtypeStruct((S, D_MODEL), BF)],
        compiler_params=_cparams(est, ("parallel",)),
    )(a, bmat, x, g, b)


def _ln_bwd(pre, g, b, dy, *, tm=512):
    S = pre.shape[0]
    tm = _tile(S, tm, SUBLANES)

    def body(p_ref, g_ref, b_ref, dy_ref, dp_ref, dg_ref, db_ref):
        @pl.when(pl.program_id(0) == 0)
        def _():
            dg_ref[...] = jnp.zeros_like(dg_ref)
            db_ref[...] = jnp.zeros_like(db_ref)

        _, vj = jax.vjp(_ln, p_ref[...], g_ref[...], b_ref[...])
        dp, dg, db = vj(dy_ref[...])
        dp_ref[...] = dp
        dg_ref[...] += _bcast_rows(dg)
        db_ref[...] += _bcast_rows(db)

    blk = pl.BlockSpec((tm, D_MODEL), lambda i: (i, 0))
    vec = pl.BlockSpec((1, D_MODEL), lambda i: (0, 0))
    acc = pl.BlockSpec((SUBLANES, D_MODEL), lambda i: (0, 0))
    return pl.pallas_call(
        body, name="ln_bwd", grid=(S // tm,),
        in_specs=[blk, vec, vec, blk], out_specs=[blk, acc, acc],
        out_shape=[jax.ShapeDtypeStruct((S, D_MODEL), F32)] + [jax.ShapeDtypeStruct((SUBLANES, D_MODEL), F32)] * 2,
        compiler_params=_cparams(10 * _nbytes((tm, D_MODEL), F32), ("arbitrary",)),
    )(pre, g, b, dy)


def _loss_head(y, tgt, *, tm=512):
    S = y.shape[0]
    tm = _tile(S, tm, SUBLANES)

    def body(y_ref, t_ref, dy_ref, l_ref):
        @pl.when(pl.program_id(0) == 0)
        def _():
            l_ref[...] = jnp.zeros_like(l_ref)

        e = y_ref[...] - t_ref[...]
        dy_ref[...] = e * (1.0 / D_MODEL)
        l_ref[...] += 0.5 * jnp.sum(jnp.mean(e * e, -1, keepdims=True), keepdims=True)

    blk = pl.BlockSpec((tm, D_MODEL), lambda i: (i, 0))
    return pl.pallas_call(
        body, name="loss_head", grid=(S // tm,),
        in_specs=[blk, blk], out_specs=[blk, pl.BlockSpec((SUBLANES, LANES), lambda i: (0, 0))],
        out_shape=[jax.ShapeDtypeStruct((S, D_MODEL), F32), jax.ShapeDtypeStruct((SUBLANES, LANES), F32)],
        compiler_params=_cparams(6 * _nbytes((tm, D_MODEL), F32), ("arbitrary",)),
    )(y, tgt)


def _layer_fwd(x, xb, w, late):
    pq = _mm(xb, w["win"], mode="nn", name="mm_in_qkv", tm=1024, tn=1024, cols=(0, 3 * D_MODEL))
    proj = _mm(xb, w["win"], mode="nn", name="mm_in_rest", tm=1024, tn=1024, cols=(3 * D_MODEL, 5 * D_MODEL), out_dtype=BF)
    ba = _mm(xb, w["wba"], mode="nn", name="mm_in_ba", tm=1024, tn=LANES)
    qn, kn, vv, gb = _qkv_prep(pq, ba, w["convw"], w["arow"], w["dtrow"])
    o, st, tinv = _delta_fwd(qn, kn, vv, gb)
    ya, yb = _mix_prep(o, proj, w["onw"], w["sg"], w["sb"], w["ws"], w["bst"])
    w = {**w, **late(ya)}
    pa, pb, m = _mm_gate_merge(ya, yb, w["wpa"], w["wpb"], proj)
    pre1, x1, x1b = _mm_resid_ln(m, w["wo"], x, w["ln1g"], w["ln1b"], name="mm_out_ln")
    hg, hu, h = _mm_swiglu(x1b, w["wgu"])
    pre2, x2, x2b = _mm_resid_ln(h, w["wd"], x1, w["ln2g"], w["ln2b"], name="mm_down_ln")
    saved = dict(xb=xb, pq=pq, proj=proj, ba=ba, qn=qn, kn=kn, vv=vv, gb=gb, o=o, st=st, tinv=tinv, ya=ya, yb=yb,
                 pa=pa, pb=pb, m=m, pre1=pre1, x1b=x1b, hg=hg, hu=hu, h=h, pre2=pre2)
    return x2, x2b, saved, w


def _layer_bwd(dx2, w, s, on_part=None):
    g = {}
    started = lambda part: on_part(part, g) if on_part is not None else None
    after = lambda v, token: v if token is None else v + token.astype(v.dtype)
    dpre2, g["ln2g"], g["ln2b"] = _ln_bwd(s["pre2"], w["ln2g"], w["ln2b"], dx2)
    dh = _mm(dpre2, w["wd"], mode="nt", name="mm_nt_down", tm=1024, tn=1536, out_dtype=BF)
    g["wd"] = _mm(s["h"], dpre2, mode="tn", name="mm_tn_down", tm=1536, tk=1024, out_dtype=BF)
    dhgu = _swiglu_bwd(s["hg"], s["hu"], dh)
    dx1 = _mm(dhgu, w["wgu"], mode="nt", name="mm_nt_gu", add=dpre2, add_scale=ALPHA, tm=1024, tk=1536)
    g["wgu"] = _mm(s["x1b"], dhgu, mode="tn", name="mm_tn_gu", tm=1024, tn=1536, tk=2048, out_dtype=BF)
    dpre1, g["ln1g"], g["ln1b"] = _ln_bwd(s["pre1"], w["ln1g"], w["ln1b"], dx1)
    dm = _mm(dpre1, w["wo"], mode="nt", name="mm_nt_sq_bf", out_dtype=BF)
    g["wo"] = _mm(s["m"], dpre1, mode="tn", name="mm_tn_sq", tm=1024, tk=1024, out_dtype=BF)
    dpa, dpb, dproj = _gate_merge_bwd(s["pa"], s["pb"], s["proj"], dm)
    dya = _mm(dpa, w["wpa"], mode="nt", name="mm_nt_sq")
    g["wpa"] = _mm(s["ya"], dpa, mode="tn", name="mm_tn_sq", tm=1024, tk=1024, out_dtype=BF)
    dyb = _mm(dpb, w["wpb"], mode="nt", name="mm_nt_sq")
    g["wpb"] = _mm(s["yb"], dpb, mode="tn", name="mm_tn_sq", tm=1024, tk=1024, out_dtype=BF)
    do, dproj, g["onw"], g["sg"], g["sb"], g["ws"], g["bst"] = _mix_prep_bwd(
        s["o"], s["proj"], after(w["onw"], started("late")), w["sg"], w["sb"], w["ws"], w["bst"], dya, dyb, dproj)
    dqn, dkn, dvv, dgb = _delta_bwd(s["qn"], s["kn"], s["vv"], s["gb"], s["st"], s["tinv"], do)
    dc, dba, g["convw"], g["arow"], g["dtrow"] = _qkv_prep_bwd(
        s["pq"], s["ba"], w["convw"], w["arow"], w["dtrow"], dqn, dkn, dvv, dgb)
    dproj = _conv_bwd(dc, w["convw"], dproj)
    g["win"] = _mm(s["xb"], dproj, mode="tn", name="mm_tn_in", tm=1024, tn=1024, tk=2048, out_dtype=BF)
    g["wba"] = _mm(s["xb"], dba, mode="tn", name="mm_tn_ba", tm=1024, tn=LANES, tk=1024, out_dtype=BF)
    dx = _mm(dba, after(w["wba"], started("early")), mode="nt", name="mm_nt_ba", add=dpre1, add_scale=ALPHA, tm=1024)
    dx = _mm(dproj, w["win"], mode="nt", name="mm_nt_in", add=dx, add_scale=1.0, tm=1024, tk=2048)
    return dx, g


def _local_step(x, tgt, layers, on_grads=None):
    saved, weights = [], []
    xb = x.astype(BF)
    for layer in layers:
        x, xb, s, w = _layer_fwd(x, xb, *layer(x))
        saved.append(s)
        weights.append(w)
    dy, lacc = _loss_head(x, tgt)
    grads = [None] * len(layers)
    for l in reversed(range(len(layers))):
        on_part = functools.partial(on_grads, l) if on_grads is not None else None
        dy, grads[l] = _layer_bwd(dy, weights[l], saved[l], on_part)
    return lacc[0, 0], dy, grads


_QKVZ = 4 * D_MODEL
_BA = 2 * N_HEADS


WEIGHT_NAMES = ("w_in", "conv_w", "a_log", "dt_bias", "o_norm_w", "sgu_ln_g", "sgu_ln_b", "w_s", "b_s", "w_pa", "w_pb",
                "w_o", "ln1_g", "ln1_b", "w_ffn_gate", "w_ffn_up", "w_ffn_down", "ln2_g", "ln2_b")
WIRE = ("w_in", "w_ffn_gate", "w_ffn_up", "w_ffn_down", "w_pa", "w_pb", "w_o", "conv_w")
SMALL = (("a_log", N_HEADS), ("dt_bias", N_HEADS), ("o_norm_w", D_HEAD), ("sgu_ln_g", D_MODEL), ("sgu_ln_b", D_MODEL),
         ("w_s", SGU_GROUPS * SGU_BLOCK * SGU_BLOCK), ("b_s", SGU_GROUPS * SGU_BLOCK),
         ("ln1_g", D_MODEL), ("ln1_b", D_MODEL), ("ln2_g", D_MODEL), ("ln2_b", D_MODEL))
SMALL_ROWS = -(-sum(n for _, n in SMALL) // (LANES * SUBLANES)) * SUBLANES
N_MAIN_TILES = (N_IN - _BA) // D_MODEL
ADAM_TILES = dict(w_in=(128, "adamw_in"), w_ffn_gate=(256, "adamw_ffn_cols"), w_ffn_up=(256, "adamw_ffn_cols"),
                  w_ffn_down=(32, "adamw_ffn_rows"), w_pa=(128, "adamw_sq"), w_pb=(128, "adamw_sq"), w_o=(128, "adamw_sq"),
                  conv_w=(CONV_K, "adamw_conv"))


def _pad_to(a, axis, size):
    pads = [(0, 0)] * a.ndim
    pads[axis] = (0, size - a.shape[axis])
    return jnp.pad(a, pads)


def _wire_blocks(p):
    return dict(
        w_in=_pad_to(p["w_in"].astype(BF), 2, IN_PAD),
        w_ffn_gate=_pad_to(p["w_ffn_gate"].astype(BF), 2, FFN_PAD), w_ffn_up=_pad_to(p["w_ffn_up"].astype(BF), 2, FFN_PAD),
        w_ffn_down=_pad_to(p["w_ffn_down"].astype(BF), 1, FFN_PAD),
        w_pa=p["w_pa"].astype(BF), w_pb=p["w_pb"].astype(BF), w_o=p["w_o"].astype(BF),
        conv_w=_pad_to(p["conv_w"], 1, SUBLANES),
    )


def _by_columns(blocks):
    n, r, c = blocks.shape
    return jnp.transpose(blocks, (1, 0, 2)).reshape(r, n * c)


def _to_slots(full, c):
    r = full.shape[0]
    return jnp.transpose(full.reshape(r, N_DEV, c), (1, 0, 2))


def _lane_row(v, at):
    return jnp.pad(v[None], ((0, 0), (at, LANES - at - v.shape[0])))


EARLY = ("w_in", "conv_w")
LATE = ("w_pa", "w_pb", "w_o", "w_ffn_gate", "w_ffn_up", "w_ffn_down")


def _early_weights(stacks, p, l):
    return dict(
        win=_perm_in(stacks["w_in"], D_MODEL, N_MAIN_TILES), wba=_perm_in(stacks["w_in"], LANES, 1),
        convw=_by_columns(stacks["conv_w"][:, :CONV_K]),
        arow=_lane_row(p["a_log"][l], N_HEADS), dtrow=_lane_row(p["dt_bias"][l], N_HEADS),
        onw=p["o_norm_w"][l][None], sg=p["sgu_ln_g"][l][None], sb=p["sgu_ln_b"][l][None],
        ws=p["w_s"][l], bst=_pad_to(p["b_s"][l].T, 1, LANES),
        ln1g=p["ln1_g"][l][None], ln1b=p["ln1_b"][l][None], ln2g=p["ln2_g"][l][None], ln2b=p["ln2_b"][l][None],
    )


def _late_weights(stacks):
    return dict(
        wpa=stacks["w_pa"].reshape(D_MODEL, D_MODEL), wpb=stacks["w_pb"].reshape(D_MODEL, D_MODEL),
        wo=stacks["w_o"].reshape(D_MODEL, D_MODEL),
        wgu=_by_columns(jnp.concatenate([stacks["w_ffn_gate"], stacks["w_ffn_up"]], axis=0)),
        wd=stacks["w_ffn_down"].reshape(FFN_K, D_MODEL),
    )


def _small_pack(parts):
    flat = jnp.concatenate([parts[n].reshape(-1) for n, _ in SMALL])
    return _pad_to(flat, 0, SMALL_ROWS * LANES).reshape(SMALL_ROWS, LANES)


def _small_unpack(rows, like):
    flat, out, off = rows.reshape(-1), {}, 0
    for n, size in SMALL:
        out[n] = flat[off:off + size].reshape(like[n].shape[1:])
        off += size
    return out


def _late_slots(g):
    slots = dict(
        w_ffn_gate=_to_slots(g["wgu"][:, :FFN_K], FFN_PAD), w_ffn_up=_to_slots(g["wgu"][:, FFN_K:], FFN_PAD),
        w_ffn_down=g["wd"].reshape(N_DEV, FFN_PAD, D_MODEL),
        w_pa=g["wpa"].reshape(N_DEV, D_MODEL // N_DEV, D_MODEL), w_pb=g["wpb"].reshape(N_DEV, D_MODEL // N_DEV, D_MODEL),
        w_o=g["wo"].reshape(N_DEV, D_MODEL // N_DEV, D_MODEL),
    )
    return [slots[n] for n in LATE]


def _early_slots(g):
    slots = [_perm_out(g["win"], g["wba"]), _pad_to(_to_slots(g["convw"][:CONV_K], 3 * D_MODEL // N_DEV), 1, SUBLANES)]
    small = _small_pack(dict(
        a_log=g["arow"][0, N_HEADS:2 * N_HEADS], dt_bias=g["dtrow"][0, N_HEADS:2 * N_HEADS], o_norm_w=g["onw"][0],
        sgu_ln_g=g["sg"][0], sgu_ln_b=g["sb"][0], w_s=g["ws"], b_s=g["bst"][:, :SGU_GROUPS].T,
        ln1_g=g["ln1g"][0], ln1_b=g["ln1b"][0], ln2_g=g["ln2g"][0], ln2_b=g["ln2b"][0]))
    return slots, small


def _in_tile_start(j, tile_w):
    if tile_w == LANES:
        return jnp.int32(_QKVZ)
    return j * D_MODEL + jnp.where(j >= _QKVZ // D_MODEL, _BA, 0)


def _select(rows_iota, cols_iota, dev, start, valid):
    hit = (rows_iota + (dev * IN_BLOCK - start) == cols_iota) & (rows_iota < IN_BLOCK) & (cols_iota < valid)
    return jnp.where(hit, 1.0, 0.0).astype(BF)


def _perm_in(stack, tile_w, n_tiles):
    valid = _BA if tile_w == LANES else tile_w

    def first_dev(j):
        return lax.div(_in_tile_start(j, tile_w), jnp.int32(IN_BLOCK))

    def body(w_ref, o_ref, acc_ref):
        j, k = pl.program_id(0), pl.program_id(1)
        sel = _select(_iota((IN_PAD, tile_w), 0), _iota((IN_PAD, tile_w), 1), first_dev(j) + k,
                      _in_tile_start(j, tile_w), valid)
        part = jnp.dot(w_ref[0], sel, preferred_element_type=F32)

        @pl.when(k == 0)
        def _():
            acc_ref[...] = part

        @pl.when(k == 1)
        def _():
            o_ref[...] = (acc_ref[...] + part).astype(BF)

    est = _nbytes((D_MODEL, IN_PAD), BF) + 3 * _nbytes((D_MODEL, tile_w), F32) + 2 * _nbytes((IN_PAD, tile_w), F32)
    return pl.pallas_call(
        body, name="perm_in" if tile_w != LANES else "perm_in_ba", grid=(n_tiles, 2),
        in_specs=[pl.BlockSpec((1, D_MODEL, IN_PAD), lambda j, k: (jnp.minimum(first_dev(j) + k, N_DEV - 1), 0, 0))],
        out_specs=pl.BlockSpec((D_MODEL, tile_w), lambda j, k: (0, j)),
        out_shape=jax.ShapeDtypeStruct((D_MODEL, n_tiles * tile_w), BF),
        scratch_shapes=[pltpu.VMEM((D_MODEL, tile_w), F32)],
        compiler_params=_cparams(est, ("parallel", "arbitrary")),
    )(stack)


def _perm_out(dmain, dba):
    def tile(d, s):
        c0 = d * IN_BLOCK
        first = lax.div(c0 - jnp.where(c0 < _QKVZ, 0, jnp.minimum(c0 - _QKVZ, _BA)), jnp.int32(D_MODEL))
        return jnp.minimum(first + jnp.minimum(s, 1), N_MAIN_TILES - 1)

    def body(dm_ref, db_ref, o_ref, acc_ref):
        d, s = pl.program_id(0), pl.program_id(1)

        @pl.when(s == 0)
        def _():
            acc_ref[...] = jnp.zeros_like(acc_ref)

        start = _in_tile_start(tile(d, s), D_MODEL)
        overlaps = (start < (d + 1) * IN_BLOCK) & (d * IN_BLOCK < start + D_MODEL)

        @pl.when((s < 2) & overlaps)
        def _():
            sel = _select(_iota((D_MODEL, IN_PAD), 1), _iota((D_MODEL, IN_PAD), 0), d, start, D_MODEL)
            acc_ref[...] += jnp.dot(dm_ref[...], sel, preferred_element_type=F32)

        @pl.when(s == 2)
        def _():
            sel = _select(_iota((LANES, IN_PAD), 1), _iota((LANES, IN_PAD), 0), d, jnp.int32(_QKVZ), _BA)
            o_ref[0] = (acc_ref[...] + jnp.dot(db_ref[...], sel, preferred_element_type=F32)).astype(BF)

    est = 2 * _nbytes((D_MODEL, D_MODEL), BF) + 4 * _nbytes((D_MODEL, IN_PAD), F32)
    return pl.pallas_call(
        body, name="perm_out", grid=(N_DEV, 3),
        in_specs=[pl.BlockSpec((D_MODEL, D_MODEL), lambda d, s: (0, tile(d, s))),
                  pl.BlockSpec((D_MODEL, LANES), lambda d, s: (0, 0))],
        out_specs=pl.BlockSpec((1, D_MODEL, IN_PAD), lambda d, t: (d, 0, 0)),
        out_shape=jax.ShapeDtypeStruct((N_DEV, D_MODEL, IN_PAD), BF),
        scratch_shapes=[pltpu.VMEM((D_MODEL, IN_PAD), F32)],
        compiler_params=_cparams(est, ("parallel", "arbitrary")),
    )(dmain, dba)


def _mesh_place():
    x, y, c = (lax.axis_index(a) for a in MESH_AXES)
    return x, y, c


def _slot(x, y, c):
    return 4 * x + 2 * y + c


def _peer(place, j):
    x, y, c = place
    return (1 - x if j & 4 else x, 1 - y if j & 2 else y, 1 - c if j & 1 else c)


_HBM = pl.BlockSpec(memory_space=pltpu.HBM)
_SEM = pl.BlockSpec(memory_space=pltpu.SEMAPHORE)
_EFFECT = pltpu.SideEffectType.DATAFLOW_SIDE_EFFECTING


def _remote_copy(src_ref, land_ref, slot, per_slot, pslot, sems, u, j, peer):
    return pltpu.make_async_remote_copy(
        src_ref=src_ref.at[pslot] if per_slot else src_ref, dst_ref=land_ref.at[slot],
        send_sem=sems[0].at[u * (N_DEV - 1) + j - 1], recv_sem=sems[1].at[u * (N_DEV - 1) + j - 1],
        device_id=peer, device_id_type=pl.DeviceIdType.MESH)


def _own_copy(src_ref, land_ref, me, per_slot, sems, u):
    return pltpu.make_async_copy(src_ref.at[me] if per_slot else src_ref, land_ref.at[me], sems[2].at[u])


def _exchange_start(name, srcs, per_slot):
    n = len(srcs)
    lands = [jax.ShapeDtypeStruct(s.shape if p else (N_DEV,) + s.shape, s.dtype) for s, p in zip(srcs, per_slot)]

    def body(*refs):
        src_refs, sems, land_refs, token = refs[:n], refs[n:n + 3], refs[2 * n + 3:3 * n + 3], refs[-1]
        place = _mesh_place()
        me = _slot(*place)
        for u in range(n):
            _own_copy(src_refs[u], land_refs[u], me, per_slot[u], sems, u).start()
            for j in range(1, N_DEV):
                peer = _peer(place, j)
                _remote_copy(src_refs[u], land_refs[u], me, per_slot[u], _slot(*peer), sems, u, j, peer).start()
        token[...] = jnp.zeros_like(token)

    hbm = lambda a: pltpu.HBM(a.shape, a.dtype)
    sem = pltpu.SemaphoreType.DMA((n * (N_DEV - 1),))
    outs = pl.pallas_call(
        body, name=name,
        out_shape=(sem, sem, pltpu.SemaphoreType.DMA((n,)), *[hbm(a) for a in srcs], *[hbm(a) for a in lands],
                   jax.ShapeDtypeStruct((SUBLANES, LANES), F32)),
        in_specs=[_HBM] * n, out_specs=(_SEM, _SEM, _SEM, *[_HBM] * (2 * n), pl.BlockSpec(memory_space=pltpu.VMEM)),
        input_output_aliases={i: 3 + i for i in range(n)},
        compiler_params=pltpu.CompilerParams(has_side_effects=_EFFECT),
    )(*[pltpu.with_memory_space_constraint(a, pltpu.HBM) for a in srcs])
    return tuple(outs[:3]), list(outs[3:3 + n]), list(outs[3 + n:3 + 2 * n]), outs[-1]


def _exchange_wait(name, sems, srcs, lands, units, per_slot, after):
    m = len(units)

    def body(*refs):
        src_refs, land_refs, sem_refs = refs[:m], refs[m:2 * m], refs[2 * m:2 * m + 3]
        place = _mesh_place()
        me = _slot(*place)
        for i, u in enumerate(units):
            _own_copy(src_refs[i], land_refs[i], me, per_slot[u], sem_refs, u).wait()
            for j in range(1, N_DEV):
                peer = _peer(place, j)
                pslot = _slot(*peer)
                cp = _remote_copy(src_refs[i], land_refs[i], pslot, per_slot[u], pslot, sem_refs, u, j, peer)
                cp.wait_send()
                cp.wait_recv()

    hbm = lambda a: pltpu.HBM(a.shape, a.dtype)
    outs = pl.pallas_call(
        body, name=name, out_shape=tuple(hbm(a) for a in list(srcs) + list(lands)),
        in_specs=[_HBM] * (2 * m) + [_SEM] * 3 + [pl.BlockSpec(memory_space=pl.ANY)], out_specs=tuple([_HBM] * (2 * m)),
        input_output_aliases={i: i for i in range(2 * m)},
        compiler_params=pltpu.CompilerParams(has_side_effects=_EFFECT),
    )(*srcs, *lands, *sems, after)
    return list(outs[m:])


def _adam_update(g, w, m, v):
    m = ADAM_B1 * m + (1.0 - ADAM_B1) * g
    v = ADAM_B2 * v + (1.0 - ADAM_B2) * jnp.square(g)
    m_hat = m / (1.0 - ADAM_B1 ** ADAM_STEP)
    v_hat = v / (1.0 - ADAM_B2 ** ADAM_STEP)
    return -ADAM_LR * (m_hat / (jnp.sqrt(v_hat) + ADAM_EPS) + ADAM_WD * w), m, v


def _adamw(recvs, w, m, v, *, tr, name):
    L, R, C = w.shape
    rp = max(tr, SUBLANES * (4 // jnp.dtype(recvs[0].dtype).itemsize))
    Cp = recvs[0].shape[2]

    def body(*refs):
        r_refs, (w_ref, m_ref, v_ref, g_ref, d_ref, nm_ref, nv_ref) = refs[:L], refs[L:]
        for l in range(L):
            @pl.when(pl.program_id(0) == l)
            def _(r_ref=r_refs[l]):
                g = r_ref[0, :tr, :C].astype(F32)
                for s in range(1, N_DEV):
                    g = g + r_ref[s, :tr, :C].astype(F32)
                d, nm, nv = _adam_update(g, w_ref[0], m_ref[0], v_ref[0])
                g_ref[0], d_ref[0], nm_ref[0], nv_ref[0] = g, d, nm, nv

    blk = pl.BlockSpec((1, tr, C), lambda l, i: (l, i, 0))
    r_specs = [pl.BlockSpec((N_DEV, rp, Cp), lambda l, i, k=k: (0, jnp.where(l == k, i, 0), 0)) for k in range(L)]
    est = 2 * _nbytes((N_DEV, rp, Cp), recvs[0].dtype) + 8 * _nbytes((tr, Cp), F32)
    return pl.pallas_call(
        body, name=name, grid=(L, R // tr),
        in_specs=r_specs + [blk] * 3, out_specs=[blk] * 4,
        out_shape=[jax.ShapeDtypeStruct((L, R, C), F32)] * 4,
        compiler_params=_cparams(est, ("arbitrary", "arbitrary")),
    )(*recvs, w, m, v)


def _adamw_small(recv, w, m, v):
    def body(r_ref, w_ref, m_ref, v_ref, g_ref, d_ref, nm_ref, nv_ref):
        g = r_ref[0]
        for s in range(1, N_DEV):
            g = g + r_ref[s]
        g_ref[...] = g
        d_ref[...], nm_ref[...], nv_ref[...] = _adam_update(g, w_ref[...], m_ref[...], v_ref[...])

    vm = pl.BlockSpec(memory_space=pltpu.VMEM)
    return pl.pallas_call(
        body, name="adamw_small", in_specs=[vm] * 4, out_specs=[vm] * 4,
        out_shape=[jax.ShapeDtypeStruct((SMALL_ROWS, LANES), F32)] * 4,
        compiler_params=_cparams(20 * _nbytes((SMALL_ROWS, LANES), F32)),
    )(recv, w, m, v)


def kernel(x, w_in, conv_w, a_log, dt_bias, o_norm_w, sgu_ln_g, sgu_ln_b, w_s, b_s, w_pa, w_pb, w_o, ln1_g, ln1_b, w_ffn_gate, w_ffn_up, w_ffn_down, ln2_g, ln2_b, loss_target, m_w_in, m_conv_w, m_a_log, m_dt_bias, m_o_norm_w, m_sgu_ln_g, m_sgu_ln_b, m_w_s, m_b_s, m_w_pa, m_w_pb, m_w_o, m_ln1_g, m_ln1_b, m_w_ffn_gate, m_w_ffn_up, m_w_ffn_down, m_ln2_g, m_ln2_b, v_w_in, v_conv_w, v_a_log, v_dt_bias, v_o_norm_w, v_sgu_ln_g, v_sgu_ln_b, v_w_s, v_b_s, v_w_pa, v_w_pb, v_w_o, v_ln1_g, v_ln1_b, v_w_ffn_gate, v_w_ffn_up, v_w_ffn_down, v_ln2_g, v_ln2_b):
    given = dict(locals())
    P = {n: given[n] for n in WEIGHT_NAMES}
    M = {n: given["m_" + n] for n in WEIGHT_NAMES}
    V = {n: given["v_" + n] for n in WEIGHT_NAMES}

    wire = _wire_blocks(P)
    units = [(n, l) for l in range(DEPTH) for n in EARLY + LATE]
    whole = [False] * len(units)
    g_sems, g_srcs, g_lands, g_token = _exchange_start("gather_start", [wire[n][l] for n, l in units], whole)

    def gathered(name, names, l, after):
        idx = [units.index((n, l)) for n in names]
        got = _exchange_wait(name, g_sems, [g_srcs[i] for i in idx], [g_lands[i] for i in idx], idx, whole, after)
        return dict(zip(names, got))

    def layer(l):
        def weights(x_in):
            after = g_token if l == 0 else x_in
            early = _early_weights(gathered(f"gather_wait_early{l}", EARLY, l, after), P, l)
            return early, lambda ya: _late_weights(gathered(f"gather_wait_late{l}", LATE, l, ya))
        return weights

    pending = {}

    def on_grads(l, part, g):
        if part == "late":
            srcs, names = _late_slots(g), LATE
            per_slot = [True] * len(srcs)
        else:
            slots, small = _early_slots(g)
            srcs, names = slots + [small], EARLY + ("small",)
            per_slot = [True] * len(slots) + [False]
        sems, s_thru, l_thru, token = _exchange_start(f"exchange_start_{part}{l}", srcs, per_slot)
        pending[l, part] = (names, sems, s_thru, l_thru, per_slot)
        return token[0, 0]

    loss_local, dx, _ = _local_step(x[0], loss_target[0], [layer(l) for l in range(DEPTH)], on_grads)
    loss = lax.psum(loss_local, MESH_AXES)

    recv = [{} for _ in range(DEPTH)]
    for l in reversed(range(DEPTH)):
        for part in ("late", "early"):
            names, sems, s_thru, l_thru, per_slot = pending[l, part]
            got = _exchange_wait(f"exchange_wait_{part}{l}", sems, s_thru, l_thru, list(range(len(s_thru))), per_slot, dx)
            recv[l].update(zip(names, got))

    out = {}
    for n in WIRE:
        tr, name = ADAM_TILES[n]
        out[n] = _adamw([recv[l][n] for l in range(DEPTH)], P[n], M[n], V[n], tr=tr, name=name)
    small = [_adamw_small(recv[l]["small"], *[_small_pack({n: T[n][l] for n, _ in SMALL}) for T in (P, M, V)])
             for l in range(DEPTH)]
    for n, _ in SMALL:
        out[n] = [jnp.stack([_small_unpack(small[l][i], P)[n] for l in range(DEPTH)]) for i in range(4)]
    return (loss, dx[None], *[out[n][i] for i in range(4) for n in WEIGHT_NAMES])
```

```python
import functools
import math

import jax
import jax.numpy as jnp
from jax import lax
from jax.experimental import pallas as pl
from jax.experimental.pallas import tpu as pltpu

F32 = jnp.float32
BF = jnp.bfloat16
HIGHEST = lax.Precision.HIGHEST

D_MODEL = 1024
DEPTH = 2
N_HEADS = 8
D_HEAD = 128
CONV_K = 4
SGU_BLOCK = 128
SGU_GROUPS = 8
SGU_CHUNK = 64
FFN_HIDDEN = 2816
N_IN = 8208
N_DEV = 8
IN_BLOCK, IN_PAD = N_IN // N_DEV, 1152
FFN_BLOCK, FFN_PAD = FFN_HIDDEN // N_DEV, 384
FFN_K = N_DEV * FFN_PAD
ALPHA = (2 * DEPTH) ** 0.25
LN_EPS = 1e-5
RMS_EPS = 1e-6
ADAM_LR, ADAM_B1, ADAM_B2, ADAM_EPS, ADAM_WD, ADAM_STEP = 0.001, 0.9, 0.999, 1e-08, 0.01, 10

MESH_AXES = ("x", "y", "c")
DELTA_CHUNK = 128
DELTA_HEADS_PER_STEP = 8
LANES = 128
SUBLANES = 8
VMEM_BYTES = 64 * 1024 * 1024
HALO = SUBLANES
HALO_BF = 2 * SUBLANES


def _cparams(est_bytes, dims=None):
    limit = int(min(max(2 * est_bytes + (8 << 20), 32 << 20), VMEM_BYTES - (6 << 20)))
    kw = dict(vmem_limit_bytes=limit)
    if dims is not None:
        kw["dimension_semantics"] = dims
    return pltpu.CompilerParams(**kw)


def _nbytes(shape, dtype):
    return math.prod(shape) * jnp.dtype(dtype).itemsize


def _dims(kind, ndim):
    lhs, rhs = {"nn": (1, 0), "nt": (1, 1), "tn": (0, 0)}[kind]
    b = ndim - 2
    return (((lhs + b,), (rhs + b,)), (tuple(range(b)), tuple(range(b))))


def _mxu(a, b, kind):
    return lax.dot_general(a, b, _dims(kind, a.ndim), preferred_element_type=F32)


def _dot(a, b):
    return _mxu(a.astype(BF), b.astype(BF), "nn")


def _dot_nt(a, b):
    return _mxu(a.astype(BF), b.astype(BF), "nt")


def _dot_tn(a, b):
    return _mxu(a.astype(BF), b.astype(BF), "tn")


def _split(a):
    hi = a.astype(BF)
    return hi, (a - hi.astype(F32)).astype(BF)


def _dot3(a, b, kind):
    (ah, al), (bh, bl) = _split(a), _split(b)
    return _mxu(ah, bh, kind) + (_mxu(ah, bl, kind) + _mxu(al, bh, kind))


def _dotf(a, b):
    return _dot3(a, b, "nn")


def _dotf_nt(a, b):
    return _dot3(a, b, "nt")


def _dotf_tn(a, b):
    return _dot3(a, b, "tn")


def _dot01(sel, x, kind="nn"):
    s = jnp.broadcast_to(sel.astype(BF), x.shape[:-2] + sel.shape)
    h1 = x.astype(BF)
    r1 = x - h1.astype(F32)
    h2 = r1.astype(BF)
    h3 = (r1 - h2.astype(F32)).astype(BF)
    return _mxu(s, h1, kind) + (_mxu(s, h2, kind) + _mxu(s, h3, kind))


def _sigmoid(x):
    return 0.5 * jnp.tanh(0.5 * x) + 0.5


def _silu(x):
    return x * _sigmoid(x)


def _silu_and_grad(x):
    s = _sigmoid(x)
    return x * s, s * (1.0 + x * (1.0 - s))


def _gelu(x):
    return 0.5 * x * (1.0 + lax.erf(x * 0.7071067811865476))


def _softplus(x):
    return jnp.maximum(x, 0.0) + jnp.log1p(jnp.exp(-jnp.abs(x)))


def _ln(x, g, b):
    mu = jnp.mean(x, -1, keepdims=True)
    xc = x - mu
    var = jnp.mean(xc * xc, -1, keepdims=True)
    return xc * lax.rsqrt(var + LN_EPS) * g + b


def _iota(shape, dim):
    return lax.broadcasted_iota(jnp.int32, shape, dim)


def _tile(n, pref, align):
    if n <= pref:
        return n
    t = (pref // align) * align
    while t >= align:
        if n % t == 0:
            return t
        t -= align
    raise ValueError(f"no tile for {n} (pref {pref}, align {align})")


def _bcast_rows(v, rows=SUBLANES):
    return jnp.broadcast_to(v, (rows, v.shape[-1]))


def _mm(a, b, *, mode, name, out_dtype=F32, add=None, add_scale=1.0, tm=512, tn=1024, tk=1024, cols=None, pair=None):
    if mode == "nn":
        (M, K), N = a.shape, b.shape[1]
    elif mode == "nt":
        (M, K), N = a.shape, b.shape[0]
    else:
        (K, M), N = a.shape, b.shape[1]
    col0 = 0
    if cols is not None:
        col0, N = cols
    tm = _tile(M, tm, LANES if mode == "tn" else SUBLANES * 2)
    tn = _tile(N, tn, LANES)
    tk = _tile(K, tk, LANES)
    nk = K // tk
    j0 = col0 // tn
    if mode == "nn":
        a_spec = pl.BlockSpec((tm, tk), lambda i, j, k: (i, k))
        b_spec = pl.BlockSpec((tk, tn), lambda i, j, k: (k, j + j0))
        dot = _dot
    elif mode == "nt":
        a_spec = pl.BlockSpec((tm, tk), lambda i, j, k: (i, k))
        b_spec = pl.BlockSpec((tn, tk), lambda i, j, k: (j, k))
        dot = _dot_nt
    else:
        a_spec = pl.BlockSpec((tk, tm), lambda i, j, k: (k, i))
        b_spec = pl.BlockSpec((tk, tn), lambda i, j, k: (k, j))
        dot = _dot_tn
    o_spec = pl.BlockSpec((tm, tn), lambda i, j, k: (i, j))
    has_add = add is not None

    n_ab = 2 if pair is None else 4

    def body(*refs):
        ab, (o_ref, acc_ref) = refs[:n_ab], refs[-2:]
        add_ref = refs[n_ab] if has_add else None
        k = pl.program_id(2)
        part = dot(ab[0][...], ab[1][...])
        if pair is not None:
            part = part + dot(ab[2][...], ab[3][...])

        def finish(total):
            if has_add:
                total = total + add_scale * add_ref[...]
            o_ref[...] = total.astype(out_dtype)

        if nk == 1:
            finish(part)
        else:
            @pl.when(k == 0)
            def _():
                acc_ref[...] = part

            @pl.when(jnp.logical_and(k > 0, k < nk - 1))
            def _():
                acc_ref[...] += part

            @pl.when(k == nk - 1)
            def _():
                finish(acc_ref[...] + part)

    in_specs = [a_spec, b_spec] * (n_ab // 2) + ([o_spec] if has_add else [])
    args = (a, b) + (tuple(pair) if pair is not None else ()) + ((add,) if has_add else ())
    est = ((n_ab // 2) * (_nbytes((tm, tk), a.dtype) + _nbytes((tk, tn), b.dtype)) + 2 * _nbytes((tm, tn), F32)
           + (_nbytes((tm, tn), F32) if has_add else 0)) + 2 * _nbytes((tm, tn), F32)
    return pl.pallas_call(
        body, name=name,
        grid=(M // tm, N // tn, nk),
        in_specs=in_specs, out_specs=o_spec,
        out_shape=jax.ShapeDtypeStruct((M, N), out_dtype),
        scratch_shapes=[pltpu.VMEM((tm, tn) if nk > 1 else (SUBLANES, LANES), F32)],
        compiler_params=_cparams(est, ("parallel", "parallel", "arbitrary")),
    )(*args)


def _conv_taps(xt, halo, w_ref, first):
    halo = jnp.where(first, 0.0, halo)
    xc = jnp.concatenate([halo, xt], axis=0)
    shifted = [xt] + [pltpu.roll(xc, s, 0)[HALO:] for s in range(1, CONV_K)]
    out = shifted[0] * w_ref[CONV_K - 1:CONV_K, :]
    for s in range(1, CONV_K):
        out = out + shifted[s] * w_ref[CONV_K - 1 - s:CONV_K - s, :]
    return out, shifted


def _gates(ba, arow, dtrow):
    lane = _iota(ba.shape, 1)
    beta = _sigmoid(ba)
    g = -jnp.exp(arow) * _softplus(ba + dtrow)
    return jnp.where(lane < N_HEADS, beta, jnp.where(lane < 2 * N_HEADS, g, 0.0))


def _l2n(x):
    return x * lax.rsqrt(jnp.sum(x * x, -1, keepdims=True) + RMS_EPS)


def _qkv_prep(proj, ba, convw, arow, dtrow, *, tm=256):
    S = proj.shape[0]
    tm = _tile(S, tm, SUBLANES)
    W3 = 3 * D_MODEL
    hb = tm // HALO

    def body(xt_ref, halo_ref, ba_ref, w_ref, a_ref, dt_ref, q_ref, k_ref, v_ref, gb_ref):
        c, _ = _conv_taps(xt_ref[...], halo_ref[...], w_ref, pl.program_id(0) == 0)
        c = _silu(c)
        for h in range(N_HEADS):
            lo = h * D_HEAD
            q_ref[:, lo:lo + D_HEAD] = _l2n(c[:, lo:lo + D_HEAD])
            k_ref[:, lo:lo + D_HEAD] = _l2n(c[:, D_MODEL + lo:D_MODEL + lo + D_HEAD])
        v_ref[...] = c[:, 2 * D_MODEL:]
        gb_ref[...] = _gates(ba_ref[...], a_ref[...], dt_ref[...])

    row = lambda w, col=0: pl.BlockSpec((tm, w), lambda i: (i, col))
    full = lambda shape: pl.BlockSpec(shape, lambda i: (0,) * len(shape))
    est = 4 * _nbytes((tm, W3), F32)
    return pl.pallas_call(
        body, name="qkv_prep", grid=(S // tm,),
        in_specs=[row(W3), pl.BlockSpec((HALO, W3), lambda i: (jnp.maximum(i * hb - 1, 0), 0)), row(LANES),
                  full((CONV_K, W3)), full((1, LANES)), full((1, LANES))],
        out_specs=[row(D_MODEL), row(D_MODEL), row(D_MODEL), row(LANES)],
        out_shape=[jax.ShapeDtypeStruct((S, D_MODEL), F32)] * 3 + [jax.ShapeDtypeStruct((S, LANES), F32)],
        compiler_params=_cparams(est, ("arbitrary",)),
    )(proj, proj, ba, convw, arow, dtrow)


def _qkv_prep_bwd(proj, ba, convw, arow, dtrow, dq, dk, dv, dgb, *, tm=256):
    S = proj.shape[0]
    tm = _tile(S, tm, SUBLANES * 2)
    W3 = 3 * D_MODEL
    hb = tm // HALO

    def body(xt_ref, halo_ref, ba_ref, w_ref, a_ref, dt_ref, dq_ref, dk_ref, dv_ref, dgb_ref,
             dcb_ref, dba_ref, dw_ref, da_ref, ddt_ref, dc_ref):
        i = pl.program_id(0)

        @pl.when(i == 0)
        def _():
            dw_ref[...] = jnp.zeros_like(dw_ref)
            da_ref[...] = jnp.zeros_like(da_ref)
            ddt_ref[...] = jnp.zeros_like(ddt_ref)

        c, shifted = _conv_taps(xt_ref[...], halo_ref[...], w_ref, i == 0)
        a, ds = _silu_and_grad(c)
        for h in range(N_HEADS):
            for base, d_ref in ((0, dq_ref), (D_MODEL, dk_ref)):
                lo = base + h * D_HEAD
                _, vj = jax.vjp(_l2n, a[:, lo:lo + D_HEAD])
                (dx,) = vj(d_ref[:, h * D_HEAD:(h + 1) * D_HEAD])
                dc_ref[:, lo:lo + D_HEAD] = dx * ds[:, lo:lo + D_HEAD]
        dc_ref[:, 2 * D_MODEL:] = dv_ref[...] * ds[:, 2 * D_MODEL:]
        dc = dc_ref[...]
        dcb_ref[...] = dc.astype(BF)
        for s in range(CONV_K):
            kk = CONV_K - 1 - s
            dw_ref[kk:kk + 1, :] += jnp.sum(dc * shifted[s], axis=0, keepdims=True)
        _, vj = jax.vjp(_gates, ba_ref[...], a_ref[...], dt_ref[...])
        dba, da, ddt = vj(dgb_ref[...])
        dba_ref[...] = dba.astype(BF)
        da_ref[...] += _bcast_rows(da)
        ddt_ref[...] += _bcast_rows(ddt)

    row = lambda w, col=0: pl.BlockSpec((tm, w), lambda i: (i, col))
    full = lambda shape: pl.BlockSpec(shape, lambda i: (0,) * len(shape))
    est = 8 * _nbytes((tm, W3), F32)
    return pl.pallas_call(
        body, name="qkv_prep_bwd", grid=(S // tm,),
        in_specs=[row(W3), pl.BlockSpec((HALO, W3), lambda i: (jnp.maximum(i * hb - 1, 0), 0)), row(LANES),
                  full((CONV_K, W3)), full((1, LANES)), full((1, LANES)),
                  row(D_MODEL), row(D_MODEL), row(D_MODEL), row(LANES)],
        out_specs=[row(W3), row(LANES), full((SUBLANES, W3)), full((SUBLANES, LANES)), full((SUBLANES, LANES))],
        out_shape=[jax.ShapeDtypeStruct((S, W3), BF), jax.ShapeDtypeStruct((S, LANES), BF),
                   jax.ShapeDtypeStruct((SUBLANES, W3), F32), jax.ShapeDtypeStruct((SUBLANES, LANES), F32),
                   jax.ShapeDtypeStruct((SUBLANES, LANES), F32)],
        scratch_shapes=[pltpu.VMEM((tm, W3), F32)],
        compiler_params=_cparams(est, ("arbitrary",)),
    )(proj, proj, ba, convw, arow, dtrow, dq, dk, dv, dgb)


def _conv_bwd(dc, convw, dproj, *, tm=256):
    S, W3 = dc.shape
    tm = _tile(S, tm, HALO_BF)
    hb = tm // HALO_BF
    nt = S // tm

    def body(dc_ref, nxt_ref, w_ref, dproj_ref, o_ref):
        last = pl.program_id(0) == nt - 1
        nxt = jnp.where(last, 0.0, nxt_ref[...].astype(F32))
        cur = dc_ref[...].astype(F32)
        xc = jnp.concatenate([cur, nxt], axis=0)
        out = cur * w_ref[CONV_K - 1:CONV_K, :]
        for s in range(1, CONV_K):
            out = out + pltpu.roll(xc, tm + HALO_BF - s, 0)[:tm] * w_ref[CONV_K - 1 - s:CONV_K - s, :]
        o_ref[...] = out.astype(BF)

    est = 5 * _nbytes((tm, W3), F32)
    return pl.pallas_call(
        body, name="conv_bwd", grid=(nt,),
        in_specs=[pl.BlockSpec((tm, W3), lambda i: (i, 0)),
                  pl.BlockSpec((HALO_BF, W3), lambda i: (jnp.minimum((i + 1) * hb, S // HALO_BF - 1), 0)),
                  pl.BlockSpec((CONV_K, W3), lambda i: (0, 0)), pl.BlockSpec(memory_space=pl.ANY)],
        out_specs=pl.BlockSpec((tm, W3), lambda i: (i, 0)),
        out_shape=jax.ShapeDtypeStruct(dproj.shape, BF),
        input_output_aliases={3: 0},
        compiler_params=_cparams(est, ("parallel",)),
    )(dc, dc, convw, dproj)


def _inv_unit_lower(A):
    C = A.shape[-1]
    row, col = _iota((C, C), 0), _iota((C, C), 1)
    T = jnp.broadcast_to(jnp.where(row == col, 1.0, 0.0).astype(F32), A.shape)
    b = 1
    while b < C:
        hi = ~(2 * b - 1)
        off = ((row & hi) == (col & hi)) & ((row & b) != 0) & ((col & b) == 0)
        T = T - _dotf(_dotf(T, jnp.where(off, A, 0.0)), T)
        b *= 2
    return T


def _delta_common(q, k, g, beta):
    C = q.shape[-2]
    row, col = _iota((C, C), 0), _iota((C, C), 1)
    tril = row >= col
    qs = q * (D_HEAD ** -0.5)
    gcb = _dot01(jnp.where(tril, 1.0, 0.0), jnp.broadcast_to(g, g.shape[:-1] + (LANES,)))
    gc = gcb[..., :1]
    Dm = jnp.exp(jnp.where(tril, gc - jnp.swapaxes(gcb, -1, -2), -1e30))
    eg = jnp.exp(gc)
    gl = jnp.sum(jnp.where(_iota((C, 1), 0) == C - 1, gc, 0.0), axis=(-2, -1), keepdims=True)
    el = jnp.exp(gl)
    er = jnp.exp(gl - gc)
    kb = k * beta
    KK = _dot_nt(kb, k)
    QK = _dot_nt(qs, k)
    return dict(row=row, col=col, tril=tril, qs=qs, gc=gc, Dm=Dm, eg=eg, el=el, er=er, kb=kb, KK=KK, QK=QK)


def _delta_chunk_fwd(S0, q, k, v, g, beta, T=None):
    m = _delta_common(q, k, g, beta)
    if T is None:
        T = _inv_unit_lower(jnp.where(m["row"] > m["col"], m["KK"] * m["Dm"], 0.0))
    u = _dotf(T, v * beta)
    w = _dotf(T, m["kb"] * m["eg"])
    vn = u - _dot(w, S0)
    o = _dot(m["qs"] * m["eg"], S0) + _dot(m["QK"] * m["Dm"], vn)
    S1 = S0 * m["el"] + _dot_tn(k * m["er"], vn)
    return o, S1, T


def _delta_chunk_bwd(S0, q, k, v, g, beta, T, do, dS1):
    m = _delta_common(q, k, g, beta)
    C = q.shape[-2]
    qs, Dm, eg, el, er, kb, KK, QK = (m[n] for n in ("qs", "Dm", "eg", "el", "er", "kb", "KK", "QK"))
    strict = m["row"] > m["col"]
    total = lambda x: jnp.sum(x, axis=(-2, -1), keepdims=True)
    ru, rw = v * beta, kb * eg
    u = _dotf(T, ru)
    w = _dotf(T, rw)
    vn = u - _dot(w, S0)
    P = QK * Dm
    qg = qs * eg
    kr = k * er

    dvn = _dot_tn(P, do) + _dot(kr, dS1)
    dS0 = dS1 * el + _dot_tn(qg, do) - _dot_tn(w, dvn)
    d_el = total(dS1 * S0)
    dqg = _dot_nt(do, S0)
    dqs = dqg * eg
    deg = jnp.sum(dqg * qs, -1, keepdims=True)
    dP = _dot_nt(do, vn)
    dPD = dP * Dm
    dqs = dqs + _dot(dPD, k)
    dk = _dot_tn(dPD, qs)
    dD = dP * QK
    dkr = _dot_nt(vn, dS1)
    dk = dk + dkr * er
    der = jnp.sum(dkr * k, -1, keepdims=True)
    dw = -_dot_nt(dvn, S0)
    dru = _dotf_tn(T, dvn)
    drw = _dotf_tn(T, dw)
    dA = -(_dotf_nt(dru, u) + _dotf_nt(drw, w))
    dAm = jnp.where(strict, dA, 0.0)
    dKK = dAm * Dm
    dkb = _dot(dKK, k)
    dk = dk + _dot_tn(dKK, kb)
    dD = dD + dAm * KK
    dv = dru * beta
    dbeta = jnp.sum(dru * v, -1, keepdims=True)
    dkb = dkb + drw * eg
    deg = deg + jnp.sum(drw * kb, -1, keepdims=True)
    dk = dk + dkb * beta
    dbeta = dbeta + jnp.sum(dkb * k, -1, keepdims=True)
    E = dD * Dm
    dgc = jnp.sum(E, -1, keepdims=True) - jnp.sum(jnp.swapaxes(E, -1, -2), -1, keepdims=True)
    dgc = dgc + deg * eg - der * er
    dgl = total(der * er) + d_el * el
    dgc = dgc + jnp.where(_iota((C, 1), 0) == C - 1, dgl, 0.0)
    triu = jnp.where(m["row"] <= m["col"], 1.0, 0.0)
    dg = _dot01(triu, jnp.broadcast_to(dgc, dgc.shape[:-1] + (LANES,)))[..., :1]
    dq = dqs * (D_HEAD ** -0.5)
    return dq, dk, dv, dg, dbeta, dS0


def _head_cols(gb, h):
    lane = _iota(gb.shape, 1)
    beta = jnp.sum(jnp.where(lane == h, gb, 0.0), -1, keepdims=True)
    g = jnp.sum(jnp.where(lane == N_HEADS + h, gb, 0.0), -1, keepdims=True)
    return g, beta


def _delta_fwd(q, k, v, gb):
    S = q.shape[0]
    C = DELTA_CHUNK
    N = S // C

    HB = DELTA_HEADS_PER_STEP

    def body(q_ref, k_ref, v_ref, gb_ref, o_ref, st_ref, t_ref, s_scr):
        n, hb = pl.program_id(0), pl.program_id(1)
        gb = gb_ref[...]

        @pl.when(n == 0)
        def _():
            for hh in range(HB):
                s_scr[hb * HB + hh] = jnp.zeros((D_HEAD, D_HEAD), F32)

        heads = [hb * HB + hh for hh in range(HB)]
        cols = [slice(hh * D_HEAD, (hh + 1) * D_HEAD) for hh in range(HB)]
        per_head = lambda ref: jnp.stack([ref[:, c] for c in cols])
        g, beta = (jnp.stack(t) for t in zip(*[_head_cols(gb, h) for h in heads]))
        S0 = jnp.stack([s_scr[h] for h in heads])
        o, S1, T = _delta_chunk_fwd(S0, per_head(q_ref), per_head(k_ref), per_head(v_ref), g, beta)
        for hh in range(HB):
            st_ref[hh, 0] = S0[hh]
            t_ref[hh, 0] = T[hh]
            o_ref[:, cols[hh]] = o[hh]
            s_scr[heads[hh]] = S1[hh]

    hd = pl.BlockSpec((C, HB * D_HEAD), lambda n, h: (n, h))
    mat = pl.BlockSpec((HB, 1, D_HEAD, D_HEAD), lambda n, h: (h, n, 0, 0))
    est = 40 * HB * _nbytes((C, D_HEAD), F32)
    return pl.pallas_call(
        body, name="delta_fwd", grid=(N, N_HEADS // HB),
        in_specs=[hd, hd, hd, pl.BlockSpec((C, LANES), lambda n, h: (n, 0))],
        out_specs=[hd, mat, mat],
        out_shape=[jax.ShapeDtypeStruct((S, N_HEADS * D_HEAD), F32),
                   jax.ShapeDtypeStruct((N_HEADS, N, D_HEAD, D_HEAD), F32),
                   jax.ShapeDtypeStruct((N_HEADS, N, C, C), F32)],
        scratch_shapes=[pltpu.VMEM((N_HEADS, D_HEAD, D_HEAD), F32)],
        compiler_params=_cparams(est, ("arbitrary", "arbitrary")),
    )(q, k, v, gb)


def _delta_bwd(q, k, v, gb, st, tinv, do):
    S = q.shape[0]
    C = DELTA_CHUNK
    N = S // C

    HB = DELTA_HEADS_PER_STEP

    def body(q_ref, k_ref, v_ref, gb_ref, st_ref, t_ref, do_ref, dq_ref, dk_ref, dv_ref, dgb_ref, ds_scr):
        n, hb = pl.program_id(0), pl.program_id(1)
        gb = gb_ref[...]
        lane = _iota((C, LANES), 1)
        dgb = jnp.zeros((C, LANES), F32)

        @pl.when(n == 0)
        def _():
            for hh in range(HB):
                ds_scr[hb * HB + hh] = jnp.zeros((D_HEAD, D_HEAD), F32)

        heads = [hb * HB + hh for hh in range(HB)]
        cols = [slice(hh * D_HEAD, (hh + 1) * D_HEAD) for hh in range(HB)]
        per_head = lambda ref: jnp.stack([ref[:, c] for c in cols])
        g, beta = (jnp.stack(t) for t in zip(*[_head_cols(gb, h) for h in heads]))
        dS1 = jnp.stack([ds_scr[h] for h in heads])
        dq, dk, dv, dg, dbeta, dS0 = _delta_chunk_bwd(
            st_ref[:, 0], per_head(q_ref), per_head(k_ref), per_head(v_ref), g, beta, t_ref[:, 0], per_head(do_ref), dS1)
        for hh, h in enumerate(heads):
            dq_ref[:, cols[hh]] = dq[hh]
            dk_ref[:, cols[hh]] = dk[hh]
            dv_ref[:, cols[hh]] = dv[hh]
            dgb = dgb + jnp.where(lane == h, dbeta[hh], 0.0) + jnp.where(lane == N_HEADS + h, dg[hh], 0.0)
            ds_scr[h] = dS0[hh]

        @pl.when(hb == 0)
        def _():
            dgb_ref[...] = dgb

        @pl.when(hb > 0)
        def _():
            dgb_ref[...] += dgb

    hd = pl.BlockSpec((C, HB * D_HEAD), lambda n, h: (N - 1 - n, h))
    mat = pl.BlockSpec((HB, 1, D_HEAD, D_HEAD), lambda n, h: (h, N - 1 - n, 0, 0))
    gbs = pl.BlockSpec((C, LANES), lambda n, h: (N - 1 - n, 0))
    est = 60 * HB * _nbytes((C, D_HEAD), F32)
    return pl.pallas_call(
        body, name="delta_bwd", grid=(N, N_HEADS // HB),
        in_specs=[hd, hd, hd, gbs, mat, mat, hd],
        out_specs=[hd, hd, hd, gbs],
        out_shape=[jax.ShapeDtypeStruct((S, N_HEADS * D_HEAD), F32)] * 3 + [jax.ShapeDtypeStruct((S, LANES), F32)],
        scratch_shapes=[pltpu.VMEM((N_HEADS, D_HEAD, D_HEAD), F32)],
        compiler_params=_cparams(est, ("arbitrary", "arbitrary")),
    )(q, k, v, gb, st, tinv, do)


def _ya_head(o, z, onw):
    return o * lax.rsqrt(jnp.mean(o * o, -1, keepdims=True) + RMS_EPS) * onw * _silu(z)


def _sgu_pre(u, vg, sg, sb):
    return _gelu(u), _ln(_gelu(vg), sg, sb)


def _chunk_causal(shape, di, dj):
    sh = jnp.int32(int(math.log2(SGU_CHUNK)))
    return lax.shift_right_logical(_iota(shape, di), sh) >= lax.shift_right_logical(_iota(shape, dj), sh)


def _ws_masked(ws):
    return jnp.where(_chunk_causal(ws.shape, 1, 2), ws, 0.0)


def _mix_prep(o, proj, onw, sg, sb, ws, bst, *, tm=256):
    S = o.shape[0]
    tm = _tile(S, tm, SGU_BLOCK)

    def body(o_ref, z_ref, u_ref, vg_ref, onw_ref, sg_ref, sb_ref, ws_ref, bst_ref, ya_ref, yb_ref):
        onw = onw_ref[...]
        for h in range(N_HEADS):
            sl = slice(h * D_HEAD, (h + 1) * D_HEAD)
            ya_ref[:, sl] = _ya_head(o_ref[:, sl], z_ref[:, sl].astype(F32), onw).astype(BF)
        ua, vl = _sgu_pre(u_ref[...].astype(F32), vg_ref[...].astype(F32), sg_ref[...], sb_ref[...])
        wsm = _ws_masked(ws_ref[...])
        bst = bst_ref[...]
        for blk in range(tm // SGU_BLOCK):
            rs = slice(blk * SGU_BLOCK, (blk + 1) * SGU_BLOCK)
            for gi in range(SGU_GROUPS):
                cs = slice(gi * D_HEAD, (gi + 1) * D_HEAD)
                sp = _dot(wsm[gi], vl[rs, cs]) + bst[:, gi:gi + 1]
                yb_ref[rs, cs] = (ua[rs, cs] * sp).astype(BF)

    blk = lambda col: pl.BlockSpec((tm, D_MODEL), lambda i: (i, col))
    full = lambda shape: pl.BlockSpec(shape, lambda i: (0,) * len(shape))
    est = 10 * _nbytes((tm, D_MODEL), F32)
    return pl.pallas_call(
        body, name="mix_prep", grid=(S // tm,),
        in_specs=[blk(0), blk(0), blk(1), blk(2), full((1, D_HEAD)), full((1, D_MODEL)), full((1, D_MODEL)),
                  full((SGU_GROUPS, SGU_BLOCK, SGU_BLOCK)), full((SGU_BLOCK, LANES))],
        out_specs=[blk(0), blk(0)],
        out_shape=[jax.ShapeDtypeStruct((S, D_MODEL), BF)] * 2,
        compiler_params=_cparams(est, ("parallel",)),
    )(o, proj, proj, proj, onw, sg, sb, ws, bst)


def _mix_prep_bwd(o, proj, onw, sg, sb, ws, bst, dya, dyb, dproj, *, tm=256):
    S = o.shape[0]
    tm = _tile(S, tm, SGU_BLOCK)

    def body(o_ref, z_ref, u_ref, vg_ref, onw_ref, sg_ref, sb_ref, ws_ref, bst_ref, dya_ref, dyb_ref, dproj_in,
             do_ref, dzuv_ref, donw_ref, dsg_ref, dsb_ref, dws_ref, dbst_ref, dvl_scr, dua_scr):
        dz_ref, du_ref, dvg_ref = (dzuv_ref.at[:, k * D_MODEL:(k + 1) * D_MODEL] for k in range(3))
        @pl.when(pl.program_id(0) == 0)
        def _():
            for r in (donw_ref, dsg_ref, dsb_ref, dws_ref, dbst_ref):
                r[...] = jnp.zeros_like(r)

        onw = onw_ref[...]
        donw = jnp.zeros((1, D_HEAD), F32)
        for h in range(N_HEADS):
            sl = slice(h * D_HEAD, (h + 1) * D_HEAD)
            _, vj = jax.vjp(_ya_head, o_ref[:, sl], z_ref[:, sl].astype(F32), onw)
            do_h, dz_h, donw_h = vj(dya_ref[:, sl])
            do_ref[:, sl] = do_h.astype(BF)
            dz_ref[:, sl] = dz_h.astype(BF)
            donw = donw + donw_h
        donw_ref[...] += _bcast_rows(donw)

        (ua, vl), vj = jax.vjp(_sgu_pre, u_ref[...].astype(F32), vg_ref[...].astype(F32), sg_ref[...], sb_ref[...])
        wsm = _ws_masked(ws_ref[...])
        bst = bst_ref[...]
        lane = _iota((SGU_BLOCK, LANES), 1)
        dbst = jnp.zeros((SGU_BLOCK, LANES), F32)
        cmask = _chunk_causal((SGU_BLOCK, SGU_BLOCK), 0, 1)
        for gi in range(SGU_GROUPS):
            cs = slice(gi * D_HEAD, (gi + 1) * D_HEAD)
            wg = wsm[gi]
            wgt = jnp.transpose(wg)
            dwg = jnp.zeros((SGU_BLOCK, SGU_BLOCK), F32)
            for blk in range(tm // SGU_BLOCK):
                rs = slice(blk * SGU_BLOCK, (blk + 1) * SGU_BLOCK)
                sp = _dot(wg, vl[rs, cs]) + bst[:, gi:gi + 1]
                dyb = dyb_ref[rs, cs]
                dsp = dyb * ua[rs, cs]
                dua_scr[rs, cs] = dyb * sp
                dvl_scr[rs, cs] = _dot(wgt, dsp)
                dwg = dwg + _dot_nt(dsp, vl[rs, cs])
                dbst = dbst + jnp.where(lane == gi, jnp.sum(dsp, -1, keepdims=True), 0.0)
            dws_ref[gi] += jnp.where(cmask, dwg, 0.0)
        dbst_ref[...] += dbst
        du, dvg, dsg, dsb = vj((dua_scr[...], dvl_scr[...]))
        du_ref[...] = du.astype(BF)
        dvg_ref[...] = dvg.astype(BF)
        dsg_ref[...] += _bcast_rows(dsg)
        dsb_ref[...] += _bcast_rows(dsb)

    blk = lambda col: pl.BlockSpec((tm, D_MODEL), lambda i: (i, col))
    full = lambda shape: pl.BlockSpec(shape, lambda i: (0,) * len(shape))
    est = 16 * _nbytes((tm, D_MODEL), F32)
    outs = pl.pallas_call(
        body, name="mix_prep_bwd", grid=(S // tm,),
        in_specs=[blk(0), blk(0), blk(1), blk(2), full((1, D_HEAD)), full((1, D_MODEL)), full((1, D_MODEL)),
                  full((SGU_GROUPS, SGU_BLOCK, SGU_BLOCK)), full((SGU_BLOCK, LANES)), blk(0), blk(0),
                  pl.BlockSpec(memory_space=pl.ANY)],
        out_specs=[blk(0), pl.BlockSpec((tm, 3 * D_MODEL), lambda i: (i, 1)),
                   full((SUBLANES, D_HEAD)), full((SUBLANES, D_MODEL)), full((SUBLANES, D_MODEL)),
                   full((SGU_GROUPS, SGU_BLOCK, SGU_BLOCK)), full((SGU_BLOCK, LANES))],
        out_shape=[jax.ShapeDtypeStruct((S, D_MODEL), BF), jax.ShapeDtypeStruct(dproj.shape, BF),
                   jax.ShapeDtypeStruct((SUBLANES, D_HEAD), F32), jax.ShapeDtypeStruct((SUBLANES, D_MODEL), F32),
                   jax.ShapeDtypeStruct((SUBLANES, D_MODEL), F32),
                   jax.ShapeDtypeStruct((SGU_GROUPS, SGU_BLOCK, SGU_BLOCK), F32),
                   jax.ShapeDtypeStruct((SGU_BLOCK, LANES), F32)],
        input_output_aliases={11: 1},
        scratch_shapes=[pltpu.VMEM((tm, D_MODEL), F32)] * 2,
        compiler_params=_cparams(est, ("arbitrary",)),
    )(o, proj, proj, proj, onw, sg, sb, ws, bst, dya, dyb, dproj)
    return outs


def _mm_gate_merge(ya, yb, wpa, wpb, proj, *, tm=512):
    S = ya.shape[0]
    tm = _tile(S, tm, SUBLANES * 2)

    def body(ya_ref, yb_ref, wa_ref, wb_ref, ga_ref, gb_ref, pa_ref, pb_ref, m_ref):
        pa = _dot(ya_ref[...], wa_ref[...]).astype(BF)
        pb = _dot(yb_ref[...], wb_ref[...]).astype(BF)
        pa_ref[...] = pa
        pb_ref[...] = pb
        m_ref[...] = (_sigmoid(ga_ref[...].astype(F32)) * pa.astype(F32)
                      + _sigmoid(gb_ref[...].astype(F32)) * pb.astype(F32)).astype(BF)

    blk = lambda col: pl.BlockSpec((tm, D_MODEL), lambda i: (i, col))
    wsp = pl.BlockSpec((D_MODEL, D_MODEL), lambda i: (0, 0))
    return pl.pallas_call(
        body, name="mm_gate_merge", grid=(S // tm,),
        in_specs=[blk(0), blk(0), wsp, wsp, blk(3), blk(4)], out_specs=[blk(0)] * 3,
        out_shape=[jax.ShapeDtypeStruct((S, D_MODEL), BF)] * 3,
        compiler_params=_cparams(2 * _nbytes((D_MODEL, D_MODEL), BF) + 8 * _nbytes((tm, D_MODEL), F32), ("parallel",)),
    )(ya, yb, wpa, wpb, proj, proj)


def _mm_gate_merge_bwd(dmix, wo, pa, pb, proj, *, tm=512):
    S = pa.shape[0]
    tm = _tile(S, tm, SUBLANES * 2)

    def body(d_ref, w_ref, pa_ref, pb_ref, ga_ref, gb_ref, dpa_ref, dpb_ref, dg_ref):
        dm = _dot_nt(d_ref[...], w_ref[...])
        sa, sb = _sigmoid(ga_ref[...].astype(F32)), _sigmoid(gb_ref[...].astype(F32))
        dpa_ref[...] = (dm * sa).astype(BF)
        dpb_ref[...] = (dm * sb).astype(BF)
        dg_ref[:, :D_MODEL] = (dm * pa_ref[...].astype(F32) * sa * (1.0 - sa)).astype(BF)
        dg_ref[:, D_MODEL:] = (dm * pb_ref[...].astype(F32) * sb * (1.0 - sb)).astype(BF)

    blk = lambda col: pl.BlockSpec((tm, D_MODEL), lambda i: (i, col))
    est = _nbytes((D_MODEL, D_MODEL), BF) + 10 * _nbytes((tm, D_MODEL), F32)
    return pl.pallas_call(
        body, name="mm_gate_merge_bwd", grid=(S // tm,),
        in_specs=[blk(0), pl.BlockSpec((D_MODEL, D_MODEL), lambda i: (0, 0)), blk(0), blk(0), blk(3), blk(4)],
        out_specs=[blk(0), blk(0), pl.BlockSpec((tm, 2 * D_MODEL), lambda i: (i, 3))],
        out_shape=[jax.ShapeDtypeStruct((S, D_MODEL), BF)] * 2 + [jax.ShapeDtypeStruct((S, 8 * D_MODEL), BF)],
        compiler_params=_cparams(est, ("parallel",)),
    )(dmix, wo, pa, pb, proj, proj)


def _mm_swiglu(xb, wg, wu, *, tm=1024, tn=768):
    S, K = xb.shape
    tm = _tile(S, tm, SUBLANES * 2)
    tn = _tile(FFN_K, tn, LANES)

    def body(x_ref, wg_ref, wu_ref, hg_ref, hu_ref, h_ref):
        x = x_ref[...]
        hg = _dot(x, wg_ref[...]).astype(BF)
        hu = _dot(x, wu_ref[...]).astype(BF)
        hg_ref[...] = hg
        hu_ref[...] = hu
        h_ref[...] = (_silu(hg.astype(F32)) * hu.astype(F32)).astype(BF)

    out = pl.BlockSpec((tm, tn), lambda i, j: (i, j))
    est = _nbytes((tm, K), BF) + 2 * _nbytes((K, tn), BF) + 6 * _nbytes((tm, tn), F32)
    return pl.pallas_call(
        body, name="mm_swiglu", grid=(S // tm, FFN_K // tn),
        in_specs=[pl.BlockSpec((tm, K), lambda i, j: (i, 0)), pl.BlockSpec((K, tn), lambda i, j: (0, j)),
                  pl.BlockSpec((K, tn), lambda i, j: (0, j))],
        out_specs=[out] * 3, out_shape=[jax.ShapeDtypeStruct((S, FFN_K), BF)] * 3,
        compiler_params=_cparams(est, ("parallel", "parallel")),
    )(xb, wg, wu)


def _mm_swiglu_bwd(dffn, wd, hg, hu, *, tm=1024, tn=768):
    S, K = dffn.shape
    tm = _tile(S, tm, SUBLANES * 2)
    tn = _tile(FFN_K, tn, LANES)

    def body(d_ref, w_ref, hg_ref, hu_ref, dhg_ref, dhu_ref):
        dh = _dot_nt(d_ref[...], w_ref[...])
        act, dact = _silu_and_grad(hg_ref[...].astype(F32))
        dhg_ref[...] = (dh * hu_ref[...].astype(F32) * dact).astype(BF)
        dhu_ref[...] = (dh * act).astype(BF)

    out = pl.BlockSpec((tm, tn), lambda i, j: (i, j))
    est = _nbytes((tm, K), dffn.dtype) + _nbytes((tn, K), BF) + 8 * _nbytes((tm, tn), F32)
    return pl.pallas_call(
        body, name="mm_swiglu_bwd", grid=(S // tm, FFN_K // tn),
        in_specs=[pl.BlockSpec((tm, K), lambda i, j: (i, 0)), pl.BlockSpec((tn, K), lambda i, j: (j, 0)), out, out],
        out_specs=[out, out], out_shape=[jax.ShapeDtypeStruct((S, FFN_K), BF)] * 2,
        compiler_params=_cparams(est, ("parallel", "parallel")),
    )(dffn, wd, hg, hu)


def _mm_resid_ln(a, bmat, x, g, b, *, name, tm=512):
    S, K = a.shape
    tm = _tile(S, tm, SUBLANES * 2)

    def body(a_ref, w_ref, x_ref, g_ref, b_ref, pre_ref, y_ref, yb_ref):
        pre = ALPHA * x_ref[...] + _dot(a_ref[...], w_ref[...])
        y = _ln(pre, g_ref[...], b_ref[...])
        pre_ref[...] = pre
        y_ref[...] = y
        yb_ref[...] = y.astype(BF)

    blk = pl.BlockSpec((tm, D_MODEL), lambda i: (i, 0))
    vec = pl.BlockSpec((1, D_MODEL), lambda i: (0, 0))
    est = _nbytes((tm, K), BF) + _nbytes((K, D_MODEL), BF) + 8 * _nbytes((tm, D_MODEL), F32)
    return pl.pallas_call(
        body, name=name, grid=(S // tm,),
        in_specs=[pl.BlockSpec((tm, K), lambda i: (i, 0)), pl.BlockSpec((K, D_MODEL), lambda i: (0, 0)), blk, vec, vec],
        out_specs=[blk, blk, blk],
        out_shape=[jax.ShapeDtypeStruct((S, D_MODEL), F32)] * 2 + [jax.ShapeDtypeStruct((S, D_MODEL), BF)],
        compiler_params=_cparams(est, ("parallel",)),
    )(a, bmat, x, g, b)


def _ln_bwd(pre, g, b, dy, *, tm=512):
    S = pre.shape[0]
    tm = _tile(S, tm, SUBLANES)

    def body(p_ref, g_ref, b_ref, dy_ref, dp_ref, dg_ref, db_ref):
        @pl.when(pl.program_id(0) == 0)
        def _():
            dg_ref[...] = jnp.zeros_like(dg_ref)
            db_ref[...] = jnp.zeros_like(db_ref)

        _, vj = jax.vjp(_ln, p_ref[...], g_ref[...], b_ref[...])
        dp, dg, db = vj(dy_ref[...])
        dp_ref[...] = dp
        dg_ref[...] += _bcast_rows(dg)
        db_ref[...] += _bcast_rows(db)

    blk = pl.BlockSpec((tm, D_MODEL), lambda i: (i, 0))
    vec = pl.BlockSpec((1, D_MODEL), lambda i: (0, 0))
    acc = pl.BlockSpec((SUBLANES, D_MODEL), lambda i: (0, 0))
    return pl.pallas_call(
        body, name="ln_bwd", grid=(S // tm,),
        in_specs=[blk, vec, vec, blk], out_specs=[blk, acc, acc],
        out_shape=[jax.ShapeDtypeStruct((S, D_MODEL), F32)] + [jax.ShapeDtypeStruct((SUBLANES, D_MODEL), F32)] * 2,
        compiler_params=_cparams(10 * _nbytes((tm, D_MODEL), F32), ("arbitrary",)),
    )(pre, g, b, dy)


def _loss_ln_bwd(y, tgt, pre, g, b, *, tm=512):
    S = y.shape[0]
    tm = _tile(S, tm, SUBLANES)

    def body(y_ref, t_ref, p_ref, g_ref, b_ref, dp_ref, dg_ref, db_ref, l_ref):
        @pl.when(pl.program_id(0) == 0)
        def _():
            for r in (dg_ref, db_ref, l_ref):
                r[...] = jnp.zeros_like(r)

        e = y_ref[...] - t_ref[...]
        l_ref[...] += 0.5 * jnp.sum(jnp.mean(e * e, -1, keepdims=True), keepdims=True)
        _, vj = jax.vjp(_ln, p_ref[...], g_ref[...], b_ref[...])
        dp, dg, db = vj(e * (1.0 / D_MODEL))
        dp_ref[...] = dp
        dg_ref[...] += _bcast_rows(dg)
        db_ref[...] += _bcast_rows(db)

    blk = pl.BlockSpec((tm, D_MODEL), lambda i: (i, 0))
    vec = pl.BlockSpec((1, D_MODEL), lambda i: (0, 0))
    acc = pl.BlockSpec((SUBLANES, D_MODEL), lambda i: (0, 0))
    return pl.pallas_call(
        body, name="loss_ln_bwd", grid=(S // tm,),
        in_specs=[blk, blk, blk, vec, vec], out_specs=[blk, acc, acc, pl.BlockSpec((SUBLANES, LANES), lambda i: (0, 0))],
        out_shape=[jax.ShapeDtypeStruct((S, D_MODEL), F32)] + [jax.ShapeDtypeStruct((SUBLANES, D_MODEL), F32)] * 2
                  + [jax.ShapeDtypeStruct((SUBLANES, LANES), F32)],
        compiler_params=_cparams(12 * _nbytes((tm, D_MODEL), F32), ("arbitrary",)),
    )(y, tgt, pre, g, b)


def _layer_fwd(x, xb, w, late):
    pq = _mm(xb, w["win"], mode="nn", name="mm_in_qkv", tm=1024, tn=1024, cols=(0, 3 * D_MODEL))
    proj = _mm(xb, w["win"], mode="nn", name="mm_in_rest", tm=1024, tn=1024, cols=(3 * D_MODEL, 5 * D_MODEL), out_dtype=BF)
    ba = _mm(xb, w["wba"], mode="nn", name="mm_in_ba", tm=1024, tn=LANES)
    qn, kn, vv, gb = _qkv_prep(pq, ba, w["convw"], w["arow"], w["dtrow"])
    o, st, tinv = _delta_fwd(qn, kn, vv, gb)
    ya, yb = _mix_prep(o, proj, w["onw"], w["sg"], w["sb"], w["ws"], w["bst"])
    w = {**w, **late(ya)}
    pa, pb, m = _mm_gate_merge(ya, yb, w["wpa"], w["wpb"], proj)
    pre1, x1, x1b = _mm_resid_ln(m, w["wo"], x, w["ln1g"], w["ln1b"], name="mm_out_ln")
    hg, hu, h = _mm_swiglu(x1b, w["wg"], w["wu"])
    pre2, x2, x2b = _mm_resid_ln(h, w["wd"], x1, w["ln2g"], w["ln2b"], name="mm_down_ln")
    saved = dict(xb=xb, pq=pq, proj=proj, ba=ba, qn=qn, kn=kn, vv=vv, gb=gb, o=o, st=st, tinv=tinv, ya=ya, yb=yb,
                 pa=pa, pb=pb, m=m, pre1=pre1, x1b=x1b, hg=hg, hu=hu, h=h, pre2=pre2)
    return x2, x2b, saved, w


def _layer_bwd(dpre2, ln2_grads, w, s, on_part=None):
    g = {}
    started = lambda part: on_part(part, g) if on_part is not None else None
    after = lambda v, token: v if token is None else v + token.astype(v.dtype)
    g["ln2g"], g["ln2b"] = ln2_grads
    dhg, dhu = _mm_swiglu_bwd(dpre2, w["wd"], s["hg"], s["hu"])
    g["wd"] = _mm(s["h"], dpre2, mode="tn", name="mm_tn_down", tm=1536, tk=1024, out_dtype=BF)
    dx1 = _mm(dhg, w["wg"], mode="nt", name="mm_nt_gu", pair=(dhu, w["wu"]), add=dpre2, add_scale=ALPHA, tm=1024, tk=1536)
    g["wg"] = _mm(s["x1b"], dhg, mode="tn", name="mm_tn_gu", tm=1024, tn=1536, tk=2048, out_dtype=BF)
    g["wu"] = _mm(s["x1b"], dhu, mode="tn", name="mm_tn_gu", tm=1024, tn=1536, tk=2048, out_dtype=BF)
    dpre1, g["ln1g"], g["ln1b"] = _ln_bwd(s["pre1"], w["ln1g"], w["ln1b"], dx1)
    g["wo"] = _mm(s["m"], dpre1, mode="tn", name="mm_tn_sq", tm=1024, tk=1024, out_dtype=BF)
    dpa, dpb, dproj = _mm_gate_merge_bwd(dpre1, w["wo"], s["pa"], s["pb"], s["proj"])
    dya = _mm(dpa, w["wpa"], mode="nt", name="mm_nt_sq")
    g["wpa"] = _mm(s["ya"], dpa, mode="tn", name="mm_tn_sq", tm=1024, tk=1024, out_dtype=BF)
    dyb = _mm(dpb, w["wpb"], mode="nt", name="mm_nt_sq")
    g["wpb"] = _mm(s["yb"], dpb, mode="tn", name="mm_tn_sq", tm=1024, tk=1024, out_dtype=BF)
    do, dproj, g["onw"], g["sg"], g["sb"], g["ws"], g["bst"] = _mix_prep_bwd(
        s["o"], s["proj"], after(w["onw"], started("late")), w["sg"], w["sb"], w["ws"], w["bst"], dya, dyb, dproj)
    dqn, dkn, dvv, dgb = _delta_bwd(s["qn"], s["kn"], s["vv"], s["gb"], s["st"], s["tinv"], do)
    dc, dba, g["convw"], g["arow"], g["dtrow"] = _qkv_prep_bwd(
        s["pq"], s["ba"], w["convw"], w["arow"], w["dtrow"], dqn, dkn, dvv, dgb)
    dproj = _conv_bwd(dc, w["convw"], dproj)
    g["win"] = _mm(s["xb"], dproj, mode="tn", name="mm_tn_in", tm=1024, tn=1024, tk=2048, out_dtype=BF)
    g["wba"] = _mm(s["xb"], dba, mode="tn", name="mm_tn_ba", tm=1024, tn=LANES, tk=1024, out_dtype=BF)
    dx = _mm(dba, after(w["wba"], started("early")), mode="nt", name="mm_nt_ba", add=dpre1, add_scale=ALPHA, tm=1024)
    dx = _mm(dproj, w["win"], mode="nt", name="mm_nt_in", add=dx, add_scale=1.0, tm=1024, tk=2048)
    return dx, g


def _local_step(x, tgt, layers, on_grads=None):
    saved, weights = [], []
    xb = x.astype(BF)
    for layer in layers:
        x, xb, s, w = _layer_fwd(x, xb, *layer(x))
        saved.append(s)
        weights.append(w)
    last = len(layers) - 1
    dpre2, dg, db, lacc = _loss_ln_bwd(x, tgt, saved[last]["pre2"], weights[last]["ln2g"], weights[last]["ln2b"])
    grads = [None] * len(layers)
    for l in reversed(range(len(layers))):
        on_part = functools.partial(on_grads, l) if on_grads is not None else None
        dx, grads[l] = _layer_bwd(dpre2, (dg, db), weights[l], saved[l], on_part)
        if l > 0:
            dpre2, dg, db = _ln_bwd(saved[l - 1]["pre2"], weights[l - 1]["ln2g"], weights[l - 1]["ln2b"], dx)
    return lacc[0, 0], dx, grads


_QKVZ = 4 * D_MODEL
_BA = 2 * N_HEADS


WEIGHT_NAMES = ("w_in", "conv_w", "a_log", "dt_bias", "o_norm_w", "sgu_ln_g", "sgu_ln_b", "w_s", "b_s", "w_pa", "w_pb",
                "w_o", "ln1_g", "ln1_b", "w_ffn_gate", "w_ffn_up", "w_ffn_down", "ln2_g", "ln2_b")
WIRE = ("w_in", "w_ffn_gate", "w_ffn_up", "w_ffn_down", "w_pa", "w_pb", "w_o", "conv_w")
SMALL = (("a_log", N_HEADS), ("dt_bias", N_HEADS), ("o_norm_w", D_HEAD), ("sgu_ln_g", D_MODEL), ("sgu_ln_b", D_MODEL),
         ("w_s", SGU_GROUPS * SGU_BLOCK * SGU_BLOCK), ("b_s", SGU_GROUPS * SGU_BLOCK),
         ("ln1_g", D_MODEL), ("ln1_b", D_MODEL), ("ln2_g", D_MODEL), ("ln2_b", D_MODEL))
SMALL_ROWS = -(-sum(n for _, n in SMALL) // (LANES * SUBLANES)) * SUBLANES
N_MAIN_TILES = (N_IN - _BA) // D_MODEL
ADAM_TILES = dict(w_in=(128, "adamw_in"), w_ffn_gate=(256, "adamw_ffn_cols"), w_ffn_up=(256, "adamw_ffn_cols"),
                  w_ffn_down=(32, "adamw_ffn_rows"), w_pa=(128, "adamw_sq"), w_pb=(128, "adamw_sq"), w_o=(128, "adamw_sq"),
                  conv_w=(CONV_K, "adamw_conv"))


def _pad_to(a, axis, size):
    pads = [(0, 0)] * a.ndim
    pads[axis] = (0, size - a.shape[axis])
    return jnp.pad(a, pads)


def _wire_blocks(p):
    return dict(
        w_in=_pad_to(p["w_in"].astype(BF), 2, IN_PAD),
        w_ffn_gate=_pad_to(p["w_ffn_gate"].astype(BF), 2, FFN_PAD), w_ffn_up=_pad_to(p["w_ffn_up"].astype(BF), 2, FFN_PAD),
        w_ffn_down=_pad_to(p["w_ffn_down"].astype(BF), 1, FFN_PAD),
        w_pa=p["w_pa"].astype(BF), w_pb=p["w_pb"].astype(BF), w_o=p["w_o"].astype(BF),
        conv_w=_pad_to(p["conv_w"], 1, SUBLANES),
    )


def _by_columns(blocks):
    n, r, c = blocks.shape
    return jnp.transpose(blocks, (1, 0, 2)).reshape(r, n * c)


def _to_slots(full, c):
    r = full.shape[0]
    return jnp.transpose(full.reshape(r, N_DEV, c), (1, 0, 2))


def _lane_row(v, at):
    return jnp.pad(v[None], ((0, 0), (at, LANES - at - v.shape[0])))


EARLY = ("w_in", "conv_w")
LATE = ("w_pa", "w_pb", "w_o", "w_ffn_gate", "w_ffn_up", "w_ffn_down")


def _early_weights(stacks, p, l):
    return dict(
        win=_perm_in(stacks["w_in"], D_MODEL, N_MAIN_TILES), wba=_perm_in(stacks["w_in"], LANES, 1),
        convw=_by_columns(stacks["conv_w"][:, :CONV_K]),
        arow=_lane_row(p["a_log"][l], N_HEADS), dtrow=_lane_row(p["dt_bias"][l], N_HEADS),
        onw=p["o_norm_w"][l][None], sg=p["sgu_ln_g"][l][None], sb=p["sgu_ln_b"][l][None],
        ws=p["w_s"][l], bst=_pad_to(p["b_s"][l].T, 1, LANES),
        ln1g=p["ln1_g"][l][None], ln1b=p["ln1_b"][l][None], ln2g=p["ln2_g"][l][None], ln2b=p["ln2_b"][l][None],
    )


def _late_weights(stacks):
    return dict(
        wpa=stacks["w_pa"].reshape(D_MODEL, D_MODEL), wpb=stacks["w_pb"].reshape(D_MODEL, D_MODEL),
        wo=stacks["w_o"].reshape(D_MODEL, D_MODEL),
        wg=_by_columns(stacks["w_ffn_gate"]), wu=_by_columns(stacks["w_ffn_up"]),
        wd=stacks["w_ffn_down"].reshape(FFN_K, D_MODEL),
    )


def _small_pack(parts):
    flat = jnp.concatenate([parts[n].reshape(-1) for n, _ in SMALL])
    return _pad_to(flat, 0, SMALL_ROWS * LANES).reshape(SMALL_ROWS, LANES)


def _small_unpack(rows, like):
    flat, out, off = rows.reshape(-1), {}, 0
    for n, size in SMALL:
        out[n] = flat[off:off + size].reshape(like[n].shape[1:])
        off += size
    return out


def _late_slots(g):
    slots = dict(
        w_ffn_gate=_to_slots(g["wg"], FFN_PAD), w_ffn_up=_to_slots(g["wu"], FFN_PAD),
        w_ffn_down=g["wd"].reshape(N_DEV, FFN_PAD, D_MODEL),
        w_pa=g["wpa"].reshape(N_DEV, D_MODEL // N_DEV, D_MODEL), w_pb=g["wpb"].reshape(N_DEV, D_MODEL // N_DEV, D_MODEL),
        w_o=g["wo"].reshape(N_DEV, D_MODEL // N_DEV, D_MODEL),
    )
    return [slots[n] for n in LATE]


def _early_slots(g):
    slots = [_perm_out(g["win"], g["wba"]), _pad_to(_to_slots(g["convw"][:CONV_K], 3 * D_MODEL // N_DEV), 1, SUBLANES)]
    small = _small_pack(dict(
        a_log=g["arow"][0, N_HEADS:2 * N_HEADS], dt_bias=g["dtrow"][0, N_HEADS:2 * N_HEADS], o_norm_w=g["onw"][0],
        sgu_ln_g=g["sg"][0], sgu_ln_b=g["sb"][0], w_s=g["ws"], b_s=g["bst"][:, :SGU_GROUPS].T,
        ln1_g=g["ln1g"][0], ln1_b=g["ln1b"][0], ln2_g=g["ln2g"][0], ln2_b=g["ln2b"][0]))
    return slots, small


def _in_tile_start(j, tile_w):
    if tile_w == LANES:
        return jnp.int32(_QKVZ)
    return j * D_MODEL + jnp.where(j >= _QKVZ // D_MODEL, _BA, 0)


def _select(rows_iota, cols_iota, dev, start, valid):
    hit = (rows_iota + (dev * IN_BLOCK - start) == cols_iota) & (rows_iota < IN_BLOCK) & (cols_iota < valid)
    return jnp.where(hit, 1.0, 0.0).astype(BF)


def _perm_in(stack, tile_w, n_tiles):
    valid = _BA if tile_w == LANES else tile_w

    def first_dev(j):
        return lax.div(_in_tile_start(j, tile_w), jnp.int32(IN_BLOCK))

    def body(w_ref, o_ref, acc_ref):
        j, k = pl.program_id(0), pl.program_id(1)
        sel = _select(_iota((IN_PAD, tile_w), 0), _iota((IN_PAD, tile_w), 1), first_dev(j) + k,
                      _in_tile_start(j, tile_w), valid)
        part = jnp.dot(w_ref[0], sel, preferred_element_type=F32)

        @pl.when(k == 0)
        def _():
            acc_ref[...] = part

        @pl.when(k == 1)
        def _():
            o_ref[...] = (acc_ref[...] + part).astype(BF)

    est = _nbytes((D_MODEL, IN_PAD), BF) + 3 * _nbytes((D_MODEL, tile_w), F32) + 2 * _nbytes((IN_PAD, tile_w), F32)
    return pl.pallas_call(
        body, name="perm_in" if tile_w != LANES else "perm_in_ba", grid=(n_tiles, 2),
        in_specs=[pl.BlockSpec((1, D_MODEL, IN_PAD), lambda j, k: (jnp.minimum(first_dev(j) + k, N_DEV - 1), 0, 0))],
        out_specs=pl.BlockSpec((D_MODEL, tile_w), lambda j, k: (0, j)),
        out_shape=jax.ShapeDtypeStruct((D_MODEL, n_tiles * tile_w), BF),
        scratch_shapes=[pltpu.VMEM((D_MODEL, tile_w), F32)],
        compiler_params=_cparams(est, ("parallel", "arbitrary")),
    )(stack)


def _perm_out(dmain, dba):
    def tile(d, s):
        c0 = d * IN_BLOCK
        first = lax.div(c0 - jnp.where(c0 < _QKVZ, 0, jnp.minimum(c0 - _QKVZ, _BA)), jnp.int32(D_MODEL))
        return jnp.minimum(first + jnp.minimum(s, 1), N_MAIN_TILES - 1)

    def body(dm_ref, db_ref, o_ref, acc_ref):
        d, s = pl.program_id(0), pl.program_id(1)

        @pl.when(s == 0)
        def _():
            acc_ref[...] = jnp.zeros_like(acc_ref)

        start = _in_tile_start(tile(d, s), D_MODEL)
        overlaps = (start < (d + 1) * IN_BLOCK) & (d * IN_BLOCK < start + D_MODEL)

        @pl.when((s < 2) & overlaps)
        def _():
            sel = _select(_iota((D_MODEL, IN_PAD), 1), _iota((D_MODEL, IN_PAD), 0), d, start, D_MODEL)
            acc_ref[...] += jnp.dot(dm_ref[...], sel, preferred_element_type=F32)

        @pl.when(s == 2)
        def _():
            sel = _select(_iota((LANES, IN_PAD), 1), _iota((LANES, IN_PAD), 0), d, jnp.int32(_QKVZ), _BA)
            o_ref[0] = (acc_ref[...] + jnp.dot(db_ref[...], sel, preferred_element_type=F32)).astype(BF)

    est = 2 * _nbytes((D_MODEL, D_MODEL), BF) + 4 * _nbytes((D_MODEL, IN_PAD), F32)
    return pl.pallas_call(
        body, name="perm_out", grid=(N_DEV, 3),
        in_specs=[pl.BlockSpec((D_MODEL, D_MODEL), lambda d, s: (0, tile(d, s))),
                  pl.BlockSpec((D_MODEL, LANES), lambda d, s: (0, 0))],
        out_specs=pl.BlockSpec((1, D_MODEL, IN_PAD), lambda d, t: (d, 0, 0)),
        out_shape=jax.ShapeDtypeStruct((N_DEV, D_MODEL, IN_PAD), BF),
        scratch_shapes=[pltpu.VMEM((D_MODEL, IN_PAD), F32)],
        compiler_params=_cparams(est, ("parallel", "arbitrary")),
    )(dmain, dba)


def _mesh_place():
    x, y, c = (lax.axis_index(a) for a in MESH_AXES)
    return x, y, c


def _slot(x, y, c):
    return 4 * x + 2 * y + c


def _peer(place, j):
    x, y, c = place
    return (1 - x if j & 4 else x, 1 - y if j & 2 else y, 1 - c if j & 1 else c)


_HBM = pl.BlockSpec(memory_space=pltpu.HBM)
_SEM = pl.BlockSpec(memory_space=pltpu.SEMAPHORE)
_EFFECT = pltpu.SideEffectType.DATAFLOW_SIDE_EFFECTING


def _remote_copy(src_ref, land_ref, slot, per_slot, pslot, sems, u, j, peer):
    return pltpu.make_async_remote_copy(
        src_ref=src_ref.at[pslot] if per_slot else src_ref, dst_ref=land_ref.at[slot],
        send_sem=sems[0].at[u * (N_DEV - 1) + j - 1], recv_sem=sems[1].at[u * (N_DEV - 1) + j - 1],
        device_id=peer, device_id_type=pl.DeviceIdType.MESH)


def _own_copy(src_ref, land_ref, me, per_slot, sems, u):
    return pltpu.make_async_copy(src_ref.at[me] if per_slot else src_ref, land_ref.at[me], sems[2].at[u])


def _exchange_start(name, srcs, per_slot):
    n = len(srcs)
    lands = [jax.ShapeDtypeStruct(s.shape if p else (N_DEV,) + s.shape, s.dtype) for s, p in zip(srcs, per_slot)]

    def body(*refs):
        src_refs, sems, land_refs, token = refs[:n], refs[n:n + 3], refs[2 * n + 3:3 * n + 3], refs[-1]
        place = _mesh_place()
        me = _slot(*place)
        for u in range(n):
            _own_copy(src_refs[u], land_refs[u], me, per_slot[u], sems, u).start()
            for j in range(1, N_DEV):
                peer = _peer(place, j)
                _remote_copy(src_refs[u], land_refs[u], me, per_slot[u], _slot(*peer), sems, u, j, peer).start()
        token[...] = jnp.zeros_like(token)

    hbm = lambda a: pltpu.HBM(a.shape, a.dtype)
    sem = pltpu.SemaphoreType.DMA((n * (N_DEV - 1),))
    outs = pl.pallas_call(
        body, name=name,
        out_shape=(sem, sem, pltpu.SemaphoreType.DMA((n,)), *[hbm(a) for a in srcs], *[hbm(a) for a in lands],
                   jax.ShapeDtypeStruct((SUBLANES, LANES), F32)),
        in_specs=[_HBM] * n, out_specs=(_SEM, _SEM, _SEM, *[_HBM] * (2 * n), pl.BlockSpec(memory_space=pltpu.VMEM)),
        input_output_aliases={i: 3 + i for i in range(n)},
        compiler_params=pltpu.CompilerParams(has_side_effects=_EFFECT),
    )(*[pltpu.with_memory_space_constraint(a, pltpu.HBM) for a in srcs])
    return tuple(outs[:3]), list(outs[3:3 + n]), list(outs[3 + n:3 + 2 * n]), outs[-1]


def _exchange_wait(name, sems, srcs, lands, units, per_slot, after):
    m = len(units)

    def body(*refs):
        src_refs, land_refs, sem_refs = refs[:m], refs[m:2 * m], refs[2 * m:2 * m + 3]
        place = _mesh_place()
        me = _slot(*place)
        for i, u in enumerate(units):
            _own_copy(src_refs[i], land_refs[i], me, per_slot[u], sem_refs, u).wait()
            for j in range(1, N_DEV):
                peer = _peer(place, j)
                pslot = _slot(*peer)
                cp = _remote_copy(src_refs[i], land_refs[i], pslot, per_slot[u], pslot, sem_refs, u, j, peer)
                cp.wait_send()
                cp.wait_recv()

    hbm = lambda a: pltpu.HBM(a.shape, a.dtype)
    outs = pl.pallas_call(
        body, name=name, out_shape=tuple(hbm(a) for a in list(srcs) + list(lands)),
        in_specs=[_HBM] * (2 * m) + [_SEM] * 3 + [pl.BlockSpec(memory_space=pl.ANY)], out_specs=tuple([_HBM] * (2 * m)),
        input_output_aliases={i: i for i in range(2 * m)},
        compiler_params=pltpu.CompilerParams(has_side_effects=_EFFECT),
    )(*srcs, *lands, *sems, after)
    return list(outs[m:])


def _adam_update(g, w, m, v):
    m = ADAM_B1 * m + (1.0 - ADAM_B1) * g
    v = ADAM_B2 * v + (1.0 - ADAM_B2) * jnp.square(g)
    m_hat = m / (1.0 - ADAM_B1 ** ADAM_STEP)
    v_hat = v / (1.0 - ADAM_B2 ** ADAM_STEP)
    return -ADAM_LR * (m_hat / (jnp.sqrt(v_hat) + ADAM_EPS) + ADAM_WD * w), m, v


def _adamw(recvs, w, m, v, *, tr, name):
    L, R, C = w.shape
    rp = max(tr, SUBLANES * (4 // jnp.dtype(recvs[0].dtype).itemsize))
    Cp = recvs[0].shape[2]

    def body(*refs):
        r_refs, (w_ref, m_ref, v_ref, g_ref, d_ref, nm_ref, nv_ref) = refs[:L], refs[L:]
        for l in range(L):
            @pl.when(pl.program_id(0) == l)
            def _(r_ref=r_refs[l]):
                g = r_ref[0, :tr, :C].astype(F32)
                for s in range(1, N_DEV):
                    g = g + r_ref[s, :tr, :C].astype(F32)
                d, nm, nv = _adam_update(g, w_ref[0], m_ref[0], v_ref[0])
                g_ref[0], d_ref[0], nm_ref[0], nv_ref[0] = g, d, nm, nv

    blk = pl.BlockSpec((1, tr, C), lambda l, i: (l, i, 0))
    r_specs = [pl.BlockSpec((N_DEV, rp, Cp), lambda l, i, k=k: (0, jnp.where(l == k, i, 0), 0)) for k in range(L)]
    est = 2 * _nbytes((N_DEV, rp, Cp), recvs[0].dtype) + 8 * _nbytes((tr, Cp), F32)
    return pl.pallas_call(
        body, name=name, grid=(L, R // tr),
        in_specs=r_specs + [blk] * 3, out_specs=[blk] * 4,
        out_shape=[jax.ShapeDtypeStruct((L, R, C), F32)] * 4,
        compiler_params=_cparams(est, ("arbitrary", "arbitrary")),
    )(*recvs, w, m, v)


def _adamw_small(recv, w, m, v):
    def body(r_ref, w_ref, m_ref, v_ref, g_ref, d_ref, nm_ref, nv_ref):
        g = r_ref[0]
        for s in range(1, N_DEV):
            g = g + r_ref[s]
        g_ref[...] = g
        d_ref[...], nm_ref[...], nv_ref[...] = _adam_update(g, w_ref[...], m_ref[...], v_ref[...])

    vm = pl.BlockSpec(memory_space=pltpu.VMEM)
    return pl.pallas_call(
        body, name="adamw_small", in_specs=[vm] * 4, out_specs=[vm] * 4,
        out_shape=[jax.ShapeDtypeStruct((SMALL_ROWS, LANES), F32)] * 4,
        compiler_params=_cparams(20 * _nbytes((SMALL_ROWS, LANES), F32)),
    )(recv, w, m, v)


def kernel(x, w_in, conv_w, a_log, dt_bias, o_norm_w, sgu_ln_g, sgu_ln_b, w_s, b_s, w_pa, w_pb, w_o, ln1_g, ln1_b, w_ffn_gate, w_ffn_up, w_ffn_down, ln2_g, ln2_b, loss_target, m_w_in, m_conv_w, m_a_log, m_dt_bias, m_o_norm_w, m_sgu_ln_g, m_sgu_ln_b, m_w_s, m_b_s, m_w_pa, m_w_pb, m_w_o, m_ln1_g, m_ln1_b, m_w_ffn_gate, m_w_ffn_up, m_w_ffn_down, m_ln2_g, m_ln2_b, v_w_in, v_conv_w, v_a_log, v_dt_bias, v_o_norm_w, v_sgu_ln_g, v_sgu_ln_b, v_w_s, v_b_s, v_w_pa, v_w_pb, v_w_o, v_ln1_g, v_ln1_b, v_w_ffn_gate, v_w_ffn_up, v_w_ffn_down, v_ln2_g, v_ln2_b):
    given = dict(locals())
    P = {n: given[n] for n in WEIGHT_NAMES}
    M = {n: given["m_" + n] for n in WEIGHT_NAMES}
    V = {n: given["v_" + n] for n in WEIGHT_NAMES}

    wire = _wire_blocks(P)
    units = [(n, l) for l in range(DEPTH) for n in EARLY + LATE]
    whole = [False] * len(units)
    g_sems, g_srcs, g_lands, g_token = _exchange_start("gather_start", [wire[n][l] for n, l in units], whole)

    def gathered(name, names, l, after):
        idx = [units.index((n, l)) for n in names]
        got = _exchange_wait(name, g_sems, [g_srcs[i] for i in idx], [g_lands[i] for i in idx], idx, whole, after)
        return dict(zip(names, got))

    def layer(l):
        def weights(x_in):
            after = g_token if l == 0 else x_in
            early = _early_weights(gathered(f"gather_wait_early{l}", EARLY, l, after), P, l)
            return early, lambda ya: _late_weights(gathered(f"gather_wait_late{l}", LATE, l, ya))
        return weights

    pending = {}

    def on_grads(l, part, g):
        if part == "late":
            srcs, names = _late_slots(g), LATE
            per_slot = [True] * len(srcs)
        else:
            slots, small = _early_slots(g)
            srcs, names = slots + [small], EARLY + ("small",)
            per_slot = [True] * len(slots) + [False]
        sems, s_thru, l_thru, token = _exchange_start(f"exchange_start_{part}{l}", srcs, per_slot)
        pending[l, part] = (names, sems, s_thru, l_thru, per_slot)
        return token[0, 0]

    loss_local, dx, _ = _local_step(x[0], loss_target[0], [layer(l) for l in range(DEPTH)], on_grads)
    loss = lax.psum(loss_local, MESH_AXES)

    recv = [{} for _ in range(DEPTH)]
    for l in reversed(range(DEPTH)):
        for part in ("late", "early"):
            names, sems, s_thru, l_thru, per_slot = pending[l, part]
            got = _exchange_wait(f"exchange_wait_{part}{l}", sems, s_thru, l_thru, list(range(len(s_thru))), per_slot, dx)
            recv[l].update(zip(names, got))

    out = {}
    for n in WIRE:
        tr, name = ADAM_TILES[n]
        out[n] = _adamw([recv[l][n] for l in range(DEPTH)], P[n], M[n], V[n], tr=tr, name=name)
    small = [_adamw_small(recv[l]["small"], *[_small_pack({n: T[n][l] for n, _ in SMALL}) for T in (P, M, V)])
             for l in range(DEPTH)]
    for n, _ in SMALL:
        out[n] = [jnp.stack([_small_unpack(small[l][i], P)[n] for l in range(DEPTH)]) for i in range(4)]
    return (loss, dx[None], *[out[n][i] for i in range(4) for n in WEIGHT_NAMES])
```

```python
import functools
import math

import jax
import jax.numpy as jnp
from jax import lax
from jax.experimental import pallas as pl
from jax.experimental.pallas import tpu as pltpu

F32 = jnp.float32
BF = jnp.bfloat16
HIGHEST = lax.Precision.HIGHEST

D_MODEL = 1024
DEPTH = 2
N_HEADS = 8
D_HEAD = 128
CONV_K = 4
SGU_BLOCK = 128
SGU_GROUPS = 8
SGU_CHUNK = 64
FFN_HIDDEN = 2816
N_IN = 8208
N_DEV = 8
IN_BLOCK, IN_PAD = N_IN // N_DEV, 1152
FFN_BLOCK, FFN_PAD = FFN_HIDDEN // N_DEV, 384
FFN_K = N_DEV * FFN_PAD
ALPHA = (2 * DEPTH) ** 0.25
LN_EPS = 1e-5
RMS_EPS = 1e-6
ADAM_LR, ADAM_B1, ADAM_B2, ADAM_EPS, ADAM_WD, ADAM_STEP = 0.001, 0.9, 0.999, 1e-08, 0.01, 10

MESH_AXES = ("x", "y", "c")
DELTA_CHUNK = 128
DELTA_HEADS_PER_STEP = 8
LANES = 128
SUBLANES = 8
VMEM_BYTES = 64 * 1024 * 1024
HALO = SUBLANES
HALO_BF = 2 * SUBLANES


def _cparams(est_bytes, dims=None):
    limit = int(min(max(2 * est_bytes + (8 << 20), 32 << 20), VMEM_BYTES - (6 << 20)))
    kw = dict(vmem_limit_bytes=limit)
    if dims is not None:
        kw["dimension_semantics"] = dims
    return pltpu.CompilerParams(**kw)


def _nbytes(shape, dtype):
    return math.prod(shape) * jnp.dtype(dtype).itemsize


def _dims(kind, ndim):
    lhs, rhs = {"nn": (1, 0), "nt": (1, 1), "tn": (0, 0)}[kind]
    b = ndim - 2
    return (((lhs + b,), (rhs + b,)), (tuple(range(b)), tuple(range(b))))


def _mxu(a, b, kind):
    return lax.dot_general(a, b, _dims(kind, a.ndim), preferred_element_type=F32)


def _dot(a, b):
    return _mxu(a.astype(BF), b.astype(BF), "nn")


def _dot_nt(a, b):
    return _mxu(a.astype(BF), b.astype(BF), "nt")


def _dot_tn(a, b):
    return _mxu(a.astype(BF), b.astype(BF), "tn")


def _split(a):
    hi = a.astype(BF)
    return hi, (a - hi.astype(F32)).astype(BF)


def _dot3(a, b, kind):
    (ah, al), (bh, bl) = _split(a), _split(b)
    return _mxu(ah, bh, kind) + (_mxu(ah, bl, kind) + _mxu(al, bh, kind))


def _dotf(a, b):
    return _dot3(a, b, "nn")


def _dotf_nt(a, b):
    return _dot3(a, b, "nt")


def _dotf_tn(a, b):
    return _dot3(a, b, "tn")


def _dot01(sel, x, kind="nn"):
    s = jnp.broadcast_to(sel.astype(BF), x.shape[:-2] + sel.shape)
    h1 = x.astype(BF)
    r1 = x - h1.astype(F32)
    h2 = r1.astype(BF)
    h3 = (r1 - h2.astype(F32)).astype(BF)
    return _mxu(s, h1, kind) + (_mxu(s, h2, kind) + _mxu(s, h3, kind))


def _sigmoid(x):
    return 0.5 * jnp.tanh(0.5 * x) + 0.5


def _silu(x):
    return x * _sigmoid(x)


def _silu_and_grad(x):
    s = _sigmoid(x)
    return x * s, s * (1.0 + x * (1.0 - s))


def _gelu(x):
    return 0.5 * x * (1.0 + lax.erf(x * 0.7071067811865476))


def _softplus(x):
    return jnp.maximum(x, 0.0) + jnp.log1p(jnp.exp(-jnp.abs(x)))


def _ln(x, g, b):
    mu = jnp.mean(x, -1, keepdims=True)
    xc = x - mu
    var = jnp.mean(xc * xc, -1, keepdims=True)
    return xc * lax.rsqrt(var + LN_EPS) * g + b


def _iota(shape, dim):
    return lax.broadcasted_iota(jnp.int32, shape, dim)


def _tile(n, pref, align):
    if n <= pref:
        return n
    t = (pref // align) * align
    while t >= align:
        if n % t == 0:
            return t
        t -= align
    raise ValueError(f"no tile for {n} (pref {pref}, align {align})")


def _bcast_rows(v, rows=SUBLANES):
    return jnp.broadcast_to(v, (rows, v.shape[-1]))


def _mm(a, b, *, mode, name, out_dtype=F32, add=None, add_scale=1.0, tm=512, tn=1024, tk=1024, cols=None, pair=None):
    if mode == "nn":
        (M, K), N = a.shape, b.shape[1]
    elif mode == "nt":
        (M, K), N = a.shape, b.shape[0]
    else:
        (K, M), N = a.shape, b.shape[1]
    col0 = 0
    if cols is not None:
        col0, N = cols
    tm = _tile(M, tm, LANES if mode == "tn" else SUBLANES * 2)
    tn = _tile(N, tn, LANES)
    tk = _tile(K, tk, LANES)
    nk = K // tk
    j0 = col0 // tn
    if mode == "nn":
        a_spec = pl.BlockSpec((tm, tk), lambda i, j, k: (i, k))
        b_spec = pl.BlockSpec((tk, tn), lambda i, j, k: (k, j + j0))
        dot = _dot
    elif mode == "nt":
        a_spec = pl.BlockSpec((tm, tk), lambda i, j, k: (i, k))
        b_spec = pl.BlockSpec((tn, tk), lambda i, j, k: (j, k))
        dot = _dot_nt
    else:
        a_spec = pl.BlockSpec((tk, tm), lambda i, j, k: (k, i))
        b_spec = pl.BlockSpec((tk, tn), lambda i, j, k: (k, j))
        dot = _dot_tn
    o_spec = pl.BlockSpec((tm, tn), lambda i, j, k: (i, j))
    has_add = add is not None

    n_ab = 2 if pair is None else 4

    def body(*refs):
        ab, (o_ref, acc_ref) = refs[:n_ab], refs[-2:]
        add_ref = refs[n_ab] if has_add else None
        k = pl.program_id(2)
        part = dot(ab[0][...], ab[1][...])
        if pair is not None:
            part = part + dot(ab[2][...], ab[3][...])

        def finish(total):
            if has_add:
                total = total + add_scale * add_ref[...]
            o_ref[...] = total.astype(out_dtype)

        if nk == 1:
            finish(part)
        else:
            @pl.when(k == 0)
            def _():
                acc_ref[...] = part

            @pl.when(jnp.logical_and(k > 0, k < nk - 1))
            def _():
                acc_ref[...] += part

            @pl.when(k == nk - 1)
            def _():
                finish(acc_ref[...] + part)

    in_specs = [a_spec, b_spec] * (n_ab // 2) + ([o_spec] if has_add else [])
    args = (a, b) + (tuple(pair) if pair is not None else ()) + ((add,) if has_add else ())
    est = ((n_ab // 2) * (_nbytes((tm, tk), a.dtype) + _nbytes((tk, tn), b.dtype)) + 2 * _nbytes((tm, tn), F32)
           + (_nbytes((tm, tn), F32) if has_add else 0)) + 2 * _nbytes((tm, tn), F32)
    return pl.pallas_call(
        body, name=name,
        grid=(M // tm, N // tn, nk),
        in_specs=in_specs, out_specs=o_spec,
        out_shape=jax.ShapeDtypeStruct((M, N), out_dtype),
        scratch_shapes=[pltpu.VMEM((tm, tn) if nk > 1 else (SUBLANES, LANES), F32)],
        compiler_params=_cparams(est, ("parallel", "parallel", "arbitrary")),
    )(*args)


def _conv_taps(xt, halo, w_ref, first):
    halo = jnp.where(first, 0.0, halo)
    xc = jnp.concatenate([halo, xt], axis=0)
    shifted = [xt] + [pltpu.roll(xc, s, 0)[HALO:] for s in range(1, CONV_K)]
    out = shifted[0] * w_ref[CONV_K - 1:CONV_K, :]
    for s in range(1, CONV_K):
        out = out + shifted[s] * w_ref[CONV_K - 1 - s:CONV_K - s, :]
    return out, shifted


def _gates(ba, arow, dtrow):
    lane = _iota(ba.shape, 1)
    beta = _sigmoid(ba)
    g = -jnp.exp(arow) * _softplus(ba + dtrow)
    return jnp.where(lane < N_HEADS, beta, jnp.where(lane < 2 * N_HEADS, g, 0.0))


def _l2n(x):
    return x * lax.rsqrt(jnp.sum(x * x, -1, keepdims=True) + RMS_EPS)


def _qkv_prep(proj, ba, convw, arow, dtrow, *, tm=256):
    S = proj.shape[0]
    tm = _tile(S, tm, SUBLANES)
    W3 = 3 * D_MODEL
    hb = tm // HALO

    def body(xt_ref, halo_ref, ba_ref, w_ref, a_ref, dt_ref, q_ref, k_ref, v_ref, gb_ref):
        c, _ = _conv_taps(xt_ref[...], halo_ref[...], w_ref, pl.program_id(0) == 0)
        c = _silu(c)
        for h in range(N_HEADS):
            lo = h * D_HEAD
            q_ref[:, lo:lo + D_HEAD] = _l2n(c[:, lo:lo + D_HEAD])
            k_ref[:, lo:lo + D_HEAD] = _l2n(c[:, D_MODEL + lo:D_MODEL + lo + D_HEAD])
        v_ref[...] = c[:, 2 * D_MODEL:]
        gb_ref[...] = _gates(ba_ref[...], a_ref[...], dt_ref[...])

    row = lambda w, col=0: pl.BlockSpec((tm, w), lambda i: (i, col))
    full = lambda shape: pl.BlockSpec(shape, lambda i: (0,) * len(shape))
    est = 4 * _nbytes((tm, W3), F32)
    return pl.pallas_call(
        body, name="qkv_prep", grid=(S // tm,),
        in_specs=[row(W3), pl.BlockSpec((HALO, W3), lambda i: (jnp.maximum(i * hb - 1, 0), 0)), row(LANES),
                  full((CONV_K, W3)), full((1, LANES)), full((1, LANES))],
        out_specs=[row(D_MODEL), row(D_MODEL), row(D_MODEL), row(LANES)],
        out_shape=[jax.ShapeDtypeStruct((S, D_MODEL), F32)] * 3 + [jax.ShapeDtypeStruct((S, LANES), F32)],
        compiler_params=_cparams(est, ("arbitrary",)),
    )(proj, proj, ba, convw, arow, dtrow)


def _qkv_prep_bwd(proj, ba, convw, arow, dtrow, dq, dk, dv, dgb, *, tm=256):
    S = proj.shape[0]
    tm = _tile(S, tm, SUBLANES * 2)
    W3 = 3 * D_MODEL
    hb = tm // HALO

    def body(xt_ref, halo_ref, ba_ref, w_ref, a_ref, dt_ref, dq_ref, dk_ref, dv_ref, dgb_ref,
             dcb_ref, dba_ref, dw_ref, da_ref, ddt_ref, dc_ref):
        i = pl.program_id(0)

        @pl.when(i == 0)
        def _():
            dw_ref[...] = jnp.zeros_like(dw_ref)
            da_ref[...] = jnp.zeros_like(da_ref)
            ddt_ref[...] = jnp.zeros_like(ddt_ref)

        c, shifted = _conv_taps(xt_ref[...], halo_ref[...], w_ref, i == 0)
        a, ds = _silu_and_grad(c)
        for h in range(N_HEADS):
            for base, d_ref in ((0, dq_ref), (D_MODEL, dk_ref)):
                lo = base + h * D_HEAD
                _, vj = jax.vjp(_l2n, a[:, lo:lo + D_HEAD])
                (dx,) = vj(d_ref[:, h * D_HEAD:(h + 1) * D_HEAD])
                dc_ref[:, lo:lo + D_HEAD] = dx * ds[:, lo:lo + D_HEAD]
        dc_ref[:, 2 * D_MODEL:] = dv_ref[...] * ds[:, 2 * D_MODEL:]
        dc = dc_ref[...]
        dcb_ref[...] = dc.astype(BF)
        for s in range(CONV_K):
            kk = CONV_K - 1 - s
            dw_ref[kk:kk + 1, :] += jnp.sum(dc * shifted[s], axis=0, keepdims=True)
        _, vj = jax.vjp(_gates, ba_ref[...], a_ref[...], dt_ref[...])
        dba, da, ddt = vj(dgb_ref[...])
        dba_ref[...] = dba.astype(BF)
        da_ref[...] += _bcast_rows(da)
        ddt_ref[...] += _bcast_rows(ddt)

    row = lambda w, col=0: pl.BlockSpec((tm, w), lambda i: (i, col))
    full = lambda shape: pl.BlockSpec(shape, lambda i: (0,) * len(shape))
    est = 8 * _nbytes((tm, W3), F32)
    return pl.pallas_call(
        body, name="qkv_prep_bwd", grid=(S // tm,),
        in_specs=[row(W3), pl.BlockSpec((HALO, W3), lambda i: (jnp.maximum(i * hb - 1, 0), 0)), row(LANES),
                  full((CONV_K, W3)), full((1, LANES)), full((1, LANES)),
                  row(D_MODEL), row(D_MODEL), row(D_MODEL), row(LANES)],
        out_specs=[row(W3), row(LANES), full((SUBLANES, W3)), full((SUBLANES, LANES)), full((SUBLANES, LANES))],
        out_shape=[jax.ShapeDtypeStruct((S, W3), BF), jax.ShapeDtypeStruct((S, LANES), BF),
                   jax.ShapeDtypeStruct((SUBLANES, W3), F32), jax.ShapeDtypeStruct((SUBLANES, LANES), F32),
                   jax.ShapeDtypeStruct((SUBLANES, LANES), F32)],
        scratch_shapes=[pltpu.VMEM((tm, W3), F32)],
        compiler_params=_cparams(est, ("arbitrary",)),
    )(proj, proj, ba, convw, arow, dtrow, dq, dk, dv, dgb)


def _conv_bwd(dc, convw, dproj, *, tm=256):
    S, W3 = dc.shape
    tm = _tile(S, tm, HALO_BF)
    hb = tm // HALO_BF
    nt = S // tm

    def body(dc_ref, nxt_ref, w_ref, dproj_ref, o_ref):
        last = pl.program_id(0) == nt - 1
        nxt = jnp.where(last, 0.0, nxt_ref[...].astype(F32))
        cur = dc_ref[...].astype(F32)
        xc = jnp.concatenate([cur, nxt], axis=0)
        out = cur * w_ref[CONV_K - 1:CONV_K, :]
        for s in range(1, CONV_K):
            out = out + pltpu.roll(xc, tm + HALO_BF - s, 0)[:tm] * w_ref[CONV_K - 1 - s:CONV_K - s, :]
        o_ref[...] = out.astype(BF)

    est = 5 * _nbytes((tm, W3), F32)
    return pl.pallas_call(
        body, name="conv_bwd", grid=(nt,),
        in_specs=[pl.BlockSpec((tm, W3), lambda i: (i, 0)),
                  pl.BlockSpec((HALO_BF, W3), lambda i: (jnp.minimum((i + 1) * hb, S // HALO_BF - 1), 0)),
                  pl.BlockSpec((CONV_K, W3), lambda i: (0, 0)), pl.BlockSpec(memory_space=pl.ANY)],
        out_specs=pl.BlockSpec((tm, W3), lambda i: (i, 0)),
        out_shape=jax.ShapeDtypeStruct(dproj.shape, BF),
        input_output_aliases={3: 0},
        compiler_params=_cparams(est, ("parallel",)),
    )(dc, dc, convw, dproj)


NEUMANN_BLOCK = 8


def _inv_unit_lower(A):
    C = A.shape[-1]
    row, col = _iota((C, C), 0), _iota((C, C), 1)
    eye = jnp.where(row == col, 1.0, 0.0).astype(F32)
    Ab = A.astype(BF)
    sh = jnp.int32(int(math.log2(NEUMANN_BLOCK)))
    B = jnp.where(lax.shift_right_logical(row, sh) == lax.shift_right_logical(col, sh), Ab, jnp.zeros_like(Ab))
    B2 = _mxu(B, B, "nn")
    B4 = _dot3(B2, B2, "nn")
    b2h, b2l = _split(B2)
    P = eye - B.astype(F32) + B2 - (_mxu(B, b2h, "nn") + _mxu(B, b2l, "nn"))
    T = P + _dot3(P, B4, "nn")
    b = NEUMANN_BLOCK
    while b < C:
        hi = ~(2 * b - 1)
        off = ((row & hi) == (col & hi)) & ((row & b) != 0) & ((col & b) == 0)
        Aoff = jnp.where(off, Ab, jnp.zeros_like(Ab))
        th, tl = _split(T)
        xh, xl = _split(_mxu(th, Aoff, "nn") + _mxu(tl, Aoff, "nn"))
        T = T - (_mxu(xh, th, "nn") + (_mxu(xh, tl, "nn") + _mxu(xl, th, "nn")))
        b *= 2
    return T


def _delta_common(q, k, g, beta):
    C = q.shape[-2]
    row, col = _iota((C, C), 0), _iota((C, C), 1)
    tril = row >= col
    qs = q * (D_HEAD ** -0.5)
    gcb = _dot01(jnp.where(tril, 1.0, 0.0), jnp.broadcast_to(g, g.shape[:-1] + (LANES,)))
    gc = gcb[..., :1]
    Dm = jnp.exp(jnp.where(tril, gc - jnp.swapaxes(gcb, -1, -2), -1e30))
    eg = jnp.exp(gc)
    gl = jnp.sum(jnp.where(_iota((C, 1), 0) == C - 1, gc, 0.0), axis=(-2, -1), keepdims=True)
    el = jnp.exp(gl)
    er = jnp.exp(gl - gc)
    kb = k * beta
    KK = _dot_nt(kb, k)
    QK = _dot_nt(qs, k)
    return dict(row=row, col=col, tril=tril, qs=qs, gc=gc, Dm=Dm, eg=eg, el=el, er=er, kb=kb, KK=KK, QK=QK)


def _delta_chunk_fwd(S0, q, k, v, g, beta):
    m = _delta_common(q, k, g, beta)
    T = _inv_unit_lower(jnp.where(m["row"] > m["col"], m["KK"] * m["Dm"], 0.0))
    u = _dotf(T, v * beta)
    w = _dotf(T, m["kb"] * m["eg"])
    vn = u - _dot(w, S0)
    o = _dot(m["qs"] * m["eg"], S0) + _dot(m["QK"] * m["Dm"], vn)
    S1 = S0 * m["el"] + _dot_tn(k * m["er"], vn)
    return o, S1, T, u, w


def _delta_chunk_bwd(S0, q, k, v, g, beta, T, u, w, do, dS1):
    m = _delta_common(q, k, g, beta)
    C = q.shape[-2]
    qs, Dm, eg, el, er, kb, KK, QK = (m[n] for n in ("qs", "Dm", "eg", "el", "er", "kb", "KK", "QK"))
    strict = m["row"] > m["col"]
    total = lambda x: jnp.sum(x, axis=(-2, -1), keepdims=True)
    vn = u - _dot(w, S0)
    P = QK * Dm
    qg = qs * eg
    kr = k * er

    dvn = _dot_tn(P, do) + _dot(kr, dS1)
    dS0 = dS1 * el + _dot_tn(qg, do) - _dot_tn(w, dvn)
    d_el = total(dS1 * S0)
    dqg = _dot_nt(do, S0)
    dqs = dqg * eg
    deg = jnp.sum(dqg * qs, -1, keepdims=True)
    dP = _dot_nt(do, vn)
    dPD = dP * Dm
    dqs = dqs + _dot(dPD, k)
    dk = _dot_tn(dPD, qs)
    dD = dP * QK
    dkr = _dot_nt(vn, dS1)
    dk = dk + dkr * er
    der = jnp.sum(dkr * k, -1, keepdims=True)
    dw = -_dot_nt(dvn, S0)
    dru = _dotf_tn(T, dvn)
    drw = _dotf_tn(T, dw)
    dA = -(_dotf_nt(dru, u) + _dotf_nt(drw, w))
    dAm = jnp.where(strict, dA, 0.0)
    dKK = dAm * Dm
    dkb = _dot(dKK, k)
    dk = dk + _dot_tn(dKK, kb)
    dD = dD + dAm * KK
    dv = dru * beta
    dbeta = jnp.sum(dru * v, -1, keepdims=True)
    dkb = dkb + drw * eg
    deg = deg + jnp.sum(drw * kb, -1, keepdims=True)
    dk = dk + dkb * beta
    dbeta = dbeta + jnp.sum(dkb * k, -1, keepdims=True)
    E = dD * Dm
    dgc = jnp.sum(E, -1, keepdims=True) - jnp.sum(jnp.swapaxes(E, -1, -2), -1, keepdims=True)
    dgc = dgc + deg * eg - der * er
    dgl = total(der * er) + d_el * el
    dgc = dgc + jnp.where(_iota((C, 1), 0) == C - 1, dgl, 0.0)
    triu = jnp.where(m["row"] <= m["col"], 1.0, 0.0)
    dg = _dot01(triu, jnp.broadcast_to(dgc, dgc.shape[:-1] + (LANES,)))[..., :1]
    dq = dqs * (D_HEAD ** -0.5)
    return dq, dk, dv, dg, dbeta, dS0


def _head_cols(gb, h):
    lane = _iota(gb.shape, 1)
    beta = jnp.sum(jnp.where(lane == h, gb, 0.0), -1, keepdims=True)
    g = jnp.sum(jnp.where(lane == N_HEADS + h, gb, 0.0), -1, keepdims=True)
    return g, beta


def _delta_fwd(q, k, v, gb):
    S = q.shape[0]
    C = DELTA_CHUNK
    N = S // C

    HB = DELTA_HEADS_PER_STEP

    def body(q_ref, k_ref, v_ref, gb_ref, o_ref, st_ref, t_ref, u_ref, w_ref, s_scr):
        n, hb = pl.program_id(0), pl.program_id(1)
        gb = gb_ref[...]

        @pl.when(n == 0)
        def _():
            for hh in range(HB):
                s_scr[hb * HB + hh] = jnp.zeros((D_HEAD, D_HEAD), F32)

        heads = [hb * HB + hh for hh in range(HB)]
        cols = [slice(hh * D_HEAD, (hh + 1) * D_HEAD) for hh in range(HB)]
        per_head = lambda ref: jnp.stack([ref[:, c] for c in cols])
        g, beta = (jnp.stack(t) for t in zip(*[_head_cols(gb, h) for h in heads]))
        S0 = jnp.stack([s_scr[h] for h in heads])
        o, S1, T, u, w = _delta_chunk_fwd(S0, per_head(q_ref), per_head(k_ref), per_head(v_ref), g, beta)
        for hh in range(HB):
            st_ref[hh, 0] = S0[hh]
            t_ref[hh, 0] = T[hh]
            o_ref[:, cols[hh]] = o[hh]
            u_ref[:, cols[hh]] = u[hh]
            w_ref[:, cols[hh]] = w[hh]
            s_scr[heads[hh]] = S1[hh]

    hd = pl.BlockSpec((C, HB * D_HEAD), lambda n, h: (n, h))
    mat = pl.BlockSpec((HB, 1, D_HEAD, D_HEAD), lambda n, h: (h, n, 0, 0))
    est = 40 * HB * _nbytes((C, D_HEAD), F32)
    seq = jax.ShapeDtypeStruct((S, N_HEADS * D_HEAD), F32)
    return pl.pallas_call(
        body, name="delta_fwd", grid=(N, N_HEADS // HB),
        in_specs=[hd, hd, hd, pl.BlockSpec((C, LANES), lambda n, h: (n, 0))],
        out_specs=[hd, mat, mat, hd, hd],
        out_shape=[seq, jax.ShapeDtypeStruct((N_HEADS, N, D_HEAD, D_HEAD), F32),
                   jax.ShapeDtypeStruct((N_HEADS, N, C, C), F32), seq, seq],
        scratch_shapes=[pltpu.VMEM((N_HEADS, D_HEAD, D_HEAD), F32)],
        compiler_params=_cparams(est, ("arbitrary", "arbitrary")),
    )(q, k, v, gb)


def _delta_bwd(q, k, v, gb, st, tinv, u, w, do):
    S = q.shape[0]
    C = DELTA_CHUNK
    N = S // C

    HB = DELTA_HEADS_PER_STEP

    def body(q_ref, k_ref, v_ref, gb_ref, st_ref, t_ref, u_ref, w_ref, do_ref, dq_ref, dk_ref, dv_ref, dgb_ref, ds_scr):
        n, hb = pl.program_id(0), pl.program_id(1)
        gb = gb_ref[...]
        lane = _iota((C, LANES), 1)
        dgb = jnp.zeros((C, LANES), F32)

        @pl.when(n == 0)
        def _():
            for hh in range(HB):
                ds_scr[hb * HB + hh] = jnp.zeros((D_HEAD, D_HEAD), F32)

        heads = [hb * HB + hh for hh in range(HB)]
        cols = [slice(hh * D_HEAD, (hh + 1) * D_HEAD) for hh in range(HB)]
        per_head = lambda ref: jnp.stack([ref[:, c] for c in cols])
        g, beta = (jnp.stack(t) for t in zip(*[_head_cols(gb, h) for h in heads]))
        dS1 = jnp.stack([ds_scr[h] for h in heads])
        dq, dk, dv, dg, dbeta, dS0 = _delta_chunk_bwd(
            st_ref[:, 0], per_head(q_ref), per_head(k_ref), per_head(v_ref), g, beta, t_ref[:, 0],
            per_head(u_ref), per_head(w_ref), per_head(do_ref), dS1)
        for hh, h in enumerate(heads):
            dq_ref[:, cols[hh]] = dq[hh]
            dk_ref[:, cols[hh]] = dk[hh]
            dv_ref[:, cols[hh]] = dv[hh]
            dgb = dgb + jnp.where(lane == h, dbeta[hh], 0.0) + jnp.where(lane == N_HEADS + h, dg[hh], 0.0)
            ds_scr[h] = dS0[hh]

        @pl.when(hb == 0)
        def _():
            dgb_ref[...] = dgb

        @pl.when(hb > 0)
        def _():
            dgb_ref[...] += dgb

    hd = pl.BlockSpec((C, HB * D_HEAD), lambda n, h: (N - 1 - n, h))
    mat = pl.BlockSpec((HB, 1, D_HEAD, D_HEAD), lambda n, h: (h, N - 1 - n, 0, 0))
    gbs = pl.BlockSpec((C, LANES), lambda n, h: (N - 1 - n, 0))
    est = 60 * HB * _nbytes((C, D_HEAD), F32)
    return pl.pallas_call(
        body, name="delta_bwd", grid=(N, N_HEADS // HB),
        in_specs=[hd, hd, hd, gbs, mat, mat, hd, hd, hd],
        out_specs=[hd, hd, hd, gbs],
        out_shape=[jax.ShapeDtypeStruct((S, N_HEADS * D_HEAD), F32)] * 3 + [jax.ShapeDtypeStruct((S, LANES), F32)],
        scratch_shapes=[pltpu.VMEM((N_HEADS, D_HEAD, D_HEAD), F32)],
        compiler_params=_cparams(est, ("arbitrary", "arbitrary")),
    )(q, k, v, gb, st, tinv, u, w, do)


def _ya_head(o, z, onw):
    return o * lax.rsqrt(jnp.mean(o * o, -1, keepdims=True) + RMS_EPS) * onw * _silu(z)


def _sgu_pre(u, vg, sg, sb):
    return _gelu(u), _ln(_gelu(vg), sg, sb)


def _chunk_causal(shape, di, dj):
    sh = jnp.int32(int(math.log2(SGU_CHUNK)))
    return lax.shift_right_logical(_iota(shape, di), sh) >= lax.shift_right_logical(_iota(shape, dj), sh)


def _ws_masked(ws):
    return jnp.where(_chunk_causal(ws.shape, 1, 2), ws, 0.0)


def _mix_prep(o, proj, onw, sg, sb, ws, bst, *, tm=256):
    S = o.shape[0]
    tm = _tile(S, tm, SGU_BLOCK)

    def body(o_ref, z_ref, u_ref, vg_ref, onw_ref, sg_ref, sb_ref, ws_ref, bst_ref, ya_ref, yb_ref):
        onw = onw_ref[...]
        for h in range(N_HEADS):
            sl = slice(h * D_HEAD, (h + 1) * D_HEAD)
            ya_ref[:, sl] = _ya_head(o_ref[:, sl], z_ref[:, sl].astype(F32), onw).astype(BF)
        ua, vl = _sgu_pre(u_ref[...].astype(F32), vg_ref[...].astype(F32), sg_ref[...], sb_ref[...])
        wsm = _ws_masked(ws_ref[...])
        bst = bst_ref[...]
        for blk in range(tm // SGU_BLOCK):
            rs = slice(blk * SGU_BLOCK, (blk + 1) * SGU_BLOCK)
            for gi in range(SGU_GROUPS):
                cs = slice(gi * D_HEAD, (gi + 1) * D_HEAD)
                sp = _dot(wsm[gi], vl[rs, cs]) + bst[:, gi:gi + 1]
                yb_ref[rs, cs] = (ua[rs, cs] * sp).astype(BF)

    blk = lambda col: pl.BlockSpec((tm, D_MODEL), lambda i: (i, col))
    full = lambda shape: pl.BlockSpec(shape, lambda i: (0,) * len(shape))
    est = 10 * _nbytes((tm, D_MODEL), F32)
    return pl.pallas_call(
        body, name="mix_prep", grid=(S // tm,),
        in_specs=[blk(0), blk(0), blk(1), blk(2), full((1, D_HEAD)), full((1, D_MODEL)), full((1, D_MODEL)),
                  full((SGU_GROUPS, SGU_BLOCK, SGU_BLOCK)), full((SGU_BLOCK, LANES))],
        out_specs=[blk(0), blk(0)],
        out_shape=[jax.ShapeDtypeStruct((S, D_MODEL), BF)] * 2,
        compiler_params=_cparams(est, ("parallel",)),
    )(o, proj, proj, proj, onw, sg, sb, ws, bst)


def _mix_prep_bwd(o, proj, onw, sg, sb, ws, bst, dya, dyb, dproj, *, tm=256):
    S = o.shape[0]
    tm = _tile(S, tm, SGU_BLOCK)

    def body(o_ref, z_ref, u_ref, vg_ref, onw_ref, sg_ref, sb_ref, ws_ref, bst_ref, dya_ref, dyb_ref, dproj_in,
             do_ref, dzuv_ref, donw_ref, dsg_ref, dsb_ref, dws_ref, dbst_ref, dvl_scr, dua_scr):
        dz_ref, du_ref, dvg_ref = (dzuv_ref.at[:, k * D_MODEL:(k + 1) * D_MODEL] for k in range(3))
        @pl.when(pl.program_id(0) == 0)
        def _():
            for r in (donw_ref, dsg_ref, dsb_ref, dws_ref, dbst_ref):
                r[...] = jnp.zeros_like(r)

        onw = onw_ref[...]
        donw = jnp.zeros((1, D_HEAD), F32)
        for h in range(N_HEADS):
            sl = slice(h * D_HEAD, (h + 1) * D_HEAD)
            _, vj = jax.vjp(_ya_head, o_ref[:, sl], z_ref[:, sl].astype(F32), onw)
            do_h, dz_h, donw_h = vj(dya_ref[:, sl])
            do_ref[:, sl] = do_h.astype(BF)
            dz_ref[:, sl] = dz_h.astype(BF)
            donw = donw + donw_h
        donw_ref[...] += _bcast_rows(donw)

        (ua, vl), vj = jax.vjp(_sgu_pre, u_ref[...].astype(F32), vg_ref[...].astype(F32), sg_ref[...], sb_ref[...])
        wsm = _ws_masked(ws_ref[...])
        bst = bst_ref[...]
        lane = _iota((SGU_BLOCK, LANES), 1)
        dbst = jnp.zeros((SGU_BLOCK, LANES), F32)
        cmask = _chunk_causal((SGU_BLOCK, SGU_BLOCK), 0, 1)
        for gi in range(SGU_GROUPS):
            cs = slice(gi * D_HEAD, (gi + 1) * D_HEAD)
            wg = wsm[gi]
            wgt = jnp.transpose(wg)
            dwg = jnp.zeros((SGU_BLOCK, SGU_BLOCK), F32)
            for blk in range(tm // SGU_BLOCK):
                rs = slice(blk * SGU_BLOCK, (blk + 1) * SGU_BLOCK)
                sp = _dot(wg, vl[rs, cs]) + bst[:, gi:gi + 1]
                dyb = dyb_ref[rs, cs]
                dsp = dyb * ua[rs, cs]
                dua_scr[rs, cs] = dyb * sp
                dvl_scr[rs, cs] = _dot(wgt, dsp)
                dwg = dwg + _dot_nt(dsp, vl[rs, cs])
                dbst = dbst + jnp.where(lane == gi, jnp.sum(dsp, -1, keepdims=True), 0.0)
            dws_ref[gi] += jnp.where(cmask, dwg, 0.0)
        dbst_ref[...] += dbst
        du, dvg, dsg, dsb = vj((dua_scr[...], dvl_scr[...]))
        du_ref[...] = du.astype(BF)
        dvg_ref[...] = dvg.astype(BF)
        dsg_ref[...] += _bcast_rows(dsg)
        dsb_ref[...] += _bcast_rows(dsb)

    blk = lambda col: pl.BlockSpec((tm, D_MODEL), lambda i: (i, col))
    full = lambda shape: pl.BlockSpec(shape, lambda i: (0,) * len(shape))
    est = 16 * _nbytes((tm, D_MODEL), F32)
    outs = pl.pallas_call(
        body, name="mix_prep_bwd", grid=(S // tm,),
        in_specs=[blk(0), blk(0), blk(1), blk(2), full((1, D_HEAD)), full((1, D_MODEL)), full((1, D_MODEL)),
                  full((SGU_GROUPS, SGU_BLOCK, SGU_BLOCK)), full((SGU_BLOCK, LANES)), blk(0), blk(0),
                  pl.BlockSpec(memory_space=pl.ANY)],
        out_specs=[blk(0), pl.BlockSpec((tm, 3 * D_MODEL), lambda i: (i, 1)),
                   full((SUBLANES, D_HEAD)), full((SUBLANES, D_MODEL)), full((SUBLANES, D_MODEL)),
                   full((SGU_GROUPS, SGU_BLOCK, SGU_BLOCK)), full((SGU_BLOCK, LANES))],
        out_shape=[jax.ShapeDtypeStruct((S, D_MODEL), BF), jax.ShapeDtypeStruct(dproj.shape, BF),
                   jax.ShapeDtypeStruct((SUBLANES, D_HEAD), F32), jax.ShapeDtypeStruct((SUBLANES, D_MODEL), F32),
                   jax.ShapeDtypeStruct((SUBLANES, D_MODEL), F32),
                   jax.ShapeDtypeStruct((SGU_GROUPS, SGU_BLOCK, SGU_BLOCK), F32),
                   jax.ShapeDtypeStruct((SGU_BLOCK, LANES), F32)],
        input_output_aliases={11: 1},
        scratch_shapes=[pltpu.VMEM((tm, D_MODEL), F32)] * 2,
        compiler_params=_cparams(est, ("arbitrary",)),
    )(o, proj, proj, proj, onw, sg, sb, ws, bst, dya, dyb, dproj)
    return outs


def _mm_gate_merge(ya, yb, wpa, wpb, proj, *, tm=512):
    S = ya.shape[0]
    tm = _tile(S, tm, SUBLANES * 2)

    def body(ya_ref, yb_ref, wa_ref, wb_ref, ga_ref, gb_ref, pa_ref, pb_ref, m_ref):
        pa = _dot(ya_ref[...], wa_ref[...]).astype(BF)
        pb = _dot(yb_ref[...], wb_ref[...]).astype(BF)
        pa_ref[...] = pa
        pb_ref[...] = pb
        m_ref[...] = (_sigmoid(ga_ref[...].astype(F32)) * pa.astype(F32)
                      + _sigmoid(gb_ref[...].astype(F32)) * pb.astype(F32)).astype(BF)

    blk = lambda col: pl.BlockSpec((tm, D_MODEL), lambda i: (i, col))
    wsp = pl.BlockSpec((D_MODEL, D_MODEL), lambda i: (0, 0))
    return pl.pallas_call(
        body, name="mm_gate_merge", grid=(S // tm,),
        in_specs=[blk(0), blk(0), wsp, wsp, blk(3), blk(4)], out_specs=[blk(0)] * 3,
        out_shape=[jax.ShapeDtypeStruct((S, D_MODEL), BF)] * 3,
        compiler_params=_cparams(2 * _nbytes((D_MODEL, D_MODEL), BF) + 8 * _nbytes((tm, D_MODEL), F32), ("parallel",)),
    )(ya, yb, wpa, wpb, proj, proj)


def _mm_gate_merge_bwd(dmix, wo, pa, pb, proj, *, tm=512):
    S = pa.shape[0]
    tm = _tile(S, tm, SUBLANES * 2)

    def body(d_ref, w_ref, pa_ref, pb_ref, ga_ref, gb_ref, dpa_ref, dpb_ref, dg_ref):
        dm = _dot_nt(d_ref[...], w_ref[...])
        sa, sb = _sigmoid(ga_ref[...].astype(F32)), _sigmoid(gb_ref[...].astype(F32))
        dpa_ref[...] = (dm * sa).astype(BF)
        dpb_ref[...] = (dm * sb).astype(BF)
        dg_ref[:, :D_MODEL] = (dm * pa_ref[...].astype(F32) * sa * (1.0 - sa)).astype(BF)
        dg_ref[:, D_MODEL:] = (dm * pb_ref[...].astype(F32) * sb * (1.0 - sb)).astype(BF)

    blk = lambda col: pl.BlockSpec((tm, D_MODEL), lambda i: (i, col))
    est = _nbytes((D_MODEL, D_MODEL), BF) + 10 * _nbytes((tm, D_MODEL), F32)
    return pl.pallas_call(
        body, name="mm_gate_merge_bwd", grid=(S // tm,),
        in_specs=[blk(0), pl.BlockSpec((D_MODEL, D_MODEL), lambda i: (0, 0)), blk(0), blk(0), blk(3), blk(4)],
        out_specs=[blk(0), blk(0), pl.BlockSpec((tm, 2 * D_MODEL), lambda i: (i, 3))],
        out_shape=[jax.ShapeDtypeStruct((S, D_MODEL), BF)] * 2 + [jax.ShapeDtypeStruct((S, 8 * D_MODEL), BF)],
        compiler_params=_cparams(est, ("parallel",)),
    )(dmix, wo, pa, pb, proj, proj)


def _mm_swiglu(xb, wg, wu, *, tm=1024, tn=768):
    S, K = xb.shape
    tm = _tile(S, tm, SUBLANES * 2)
    tn = _tile(FFN_K, tn, LANES)

    def body(x_ref, wg_ref, wu_ref, hg_ref, hu_ref, h_ref):
        x = x_ref[...]
        hg = _dot(x, wg_ref[...]).astype(BF)
        hu = _dot(x, wu_ref[...]).astype(BF)
        hg_ref[...] = hg
        hu_ref[...] = hu
        h_ref[...] = (_silu(hg.astype(F32)) * hu.astype(F32)).astype(BF)

    out = pl.BlockSpec((tm, tn), lambda i, j: (i, j))
    est = _nbytes((tm, K), BF) + 2 * _nbytes((K, tn), BF) + 6 * _nbytes((tm, tn), F32)
    return pl.pallas_call(
        body, name="mm_swiglu", grid=(S // tm, FFN_K // tn),
        in_specs=[pl.BlockSpec((tm, K), lambda i, j: (i, 0)), pl.BlockSpec((K, tn), lambda i, j: (0, j)),
                  pl.BlockSpec((K, tn), lambda i, j: (0, j))],
        out_specs=[out] * 3, out_shape=[jax.ShapeDtypeStruct((S, FFN_K), BF)] * 3,
        compiler_params=_cparams(est, ("parallel", "parallel")),
    )(xb, wg, wu)


def _mm_swiglu_bwd(dffn, wd, hg, hu, *, tm=1024, tn=768):
    S, K = dffn.shape
    tm = _tile(S, tm, SUBLANES * 2)
    tn = _tile(FFN_K, tn, LANES)

    def body(d_ref, w_ref, hg_ref, hu_ref, dhg_ref, dhu_ref):
        dh = _dot_nt(d_ref[...], w_ref[...])
        act, dact = _silu_and_grad(hg_ref[...].astype(F32))
        dhg_ref[...] = (dh * hu_ref[...].astype(F32) * dact).astype(BF)
        dhu_ref[...] = (dh * act).astype(BF)

    out = pl.BlockSpec((tm, tn), lambda i, j: (i, j))
    est = _nbytes((tm, K), dffn.dtype) + _nbytes((tn, K), BF) + 8 * _nbytes((tm, tn), F32)
    return pl.pallas_call(
        body, name="mm_swiglu_bwd", grid=(S // tm, FFN_K // tn),
        in_specs=[pl.BlockSpec((tm, K), lambda i, j: (i, 0)), pl.BlockSpec((tn, K), lambda i, j: (j, 0)), out, out],
        out_specs=[out, out], out_shape=[jax.ShapeDtypeStruct((S, FFN_K), BF)] * 2,
        compiler_params=_cparams(est, ("parallel", "parallel")),
    )(dffn, wd, hg, hu)


def _mm_resid_ln(a, bmat, x, g, b, *, name, tm=512):
    S, K = a.shape
    tm = _tile(S, tm, SUBLANES * 2)

    def body(a_ref, w_ref, x_ref, g_ref, b_ref, pre_ref, y_ref, yb_ref):
        pre = ALPHA * x_ref[...] + _dot(a_ref[...], w_ref[...])
        y = _ln(pre, g_ref[...], b_ref[...])
        pre_ref[...] = pre
        y_ref[...] = y
        yb_ref[...] = y.astype(BF)

    blk = pl.BlockSpec((tm, D_MODEL), lambda i: (i, 0))
    vec = pl.BlockSpec((1, D_MODEL), lambda i: (0, 0))
    est = _nbytes((tm, K), BF) + _nbytes((K, D_MODEL), BF) + 8 * _nbytes((tm, D_MODEL), F32)
    return pl.pallas_call(
        body, name=name, grid=(S // tm,),
        in_specs=[pl.BlockSpec((tm, K), lambda i: (i, 0)), pl.BlockSpec((K, D_MODEL), lambda i: (0, 0)), blk, vec, vec],
        out_specs=[blk, blk, blk],
        out_shape=[jax.ShapeDtypeStruct((S, D_MODEL), F32)] * 2 + [jax.ShapeDtypeStruct((S, D_MODEL), BF)],
        compiler_params=_cparams(est, ("parallel",)),
    )(a, bmat, x, g, b)


def _ln_bwd(pre, g, b, dy, *, tm=512):
    S = pre.shape[0]
    tm = _tile(S, tm, SUBLANES)

    def body(p_ref, g_ref, b_ref, dy_ref, dp_ref, dg_ref, db_ref):
        @pl.when(pl.program_id(0) == 0)
        def _():
            dg_ref[...] = jnp.zeros_like(dg_ref)
            db_ref[...] = jnp.zeros_like(db_ref)

        _, vj = jax.vjp(_ln, p_ref[...], g_ref[...], b_ref[...])
        dp, dg, db = vj(dy_ref[...])
        dp_ref[...] = dp
        dg_ref[...] += _bcast_rows(dg)
        db_ref[...] += _bcast_rows(db)

    blk = pl.BlockSpec((tm, D_MODEL), lambda i: (i, 0))
    vec = pl.BlockSpec((1, D_MODEL), lambda i: (0, 0))
    acc = pl.BlockSpec((SUBLANES, D_MODEL), lambda i: (0, 0))
    return pl.pallas_call(
        body, name="ln_bwd", grid=(S // tm,),
        in_specs=[blk, vec, vec, blk], out_specs=[blk, acc, acc],
        out_shape=[jax.ShapeDtypeStruct((S, D_MODEL), F32)] + [jax.ShapeDtypeStruct((SUBLANES, D_MODEL), F32)] * 2,
        compiler_params=_cparams(10 * _nbytes((tm, D_MODEL), F32), ("arbitrary",)),
    )(pre, g, b, dy)


def _loss_ln_bwd(y, tgt, pre, g, b, *, tm=512):
    S = y.shape[0]
    tm = _tile(S, tm, SUBLANES)

    def body(y_ref, t_ref, p_ref, g_ref, b_ref, dp_ref, dg_ref, db_ref, l_ref):
        @pl.when(pl.program_id(0) == 0)
        def _():
            for r in (dg_ref, db_ref, l_ref):
                r[...] = jnp.zeros_like(r)

        e = y_ref[...] - t_ref[...]
        l_ref[...] += 0.5 * jnp.sum(jnp.mean(e * e, -1, keepdims=True), keepdims=True)
        _, vj = jax.vjp(_ln, p_ref[...], g_ref[...], b_ref[...])
        dp, dg, db = vj(e * (1.0 / D_MODEL))
        dp_ref[...] = dp
        dg_ref[...] += _bcast_rows(dg)
        db_ref[...] += _bcast_rows(db)

    blk = pl.BlockSpec((tm, D_MODEL), lambda i: (i, 0))
    vec = pl.BlockSpec((1, D_MODEL), lambda i: (0, 0))
    acc = pl.BlockSpec((SUBLANES, D_MODEL), lambda i: (0, 0))
    return pl.pallas_call(
        body, name="loss_ln_bwd", grid=(S // tm,),
        in_specs=[blk, blk, blk, vec, vec], out_specs=[blk, acc, acc, pl.BlockSpec((SUBLANES, LANES), lambda i: (0, 0))],
        out_shape=[jax.ShapeDtypeStruct((S, D_MODEL), F32)] + [jax.ShapeDtypeStruct((SUBLANES, D_MODEL), F32)] * 2
                  + [jax.ShapeDtypeStruct((SUBLANES, LANES), F32)],
        compiler_params=_cparams(12 * _nbytes((tm, D_MODEL), F32), ("arbitrary",)),
    )(y, tgt, pre, g, b)


def _layer_fwd(x, xb, w, late):
    pq = _mm(xb, w["win"], mode="nn", name="mm_in_qkv", tm=1024, tn=1024, cols=(0, 3 * D_MODEL))
    proj = _mm(xb, w["win"], mode="nn", name="mm_in_rest", tm=1024, tn=1024, cols=(3 * D_MODEL, 5 * D_MODEL), out_dtype=BF)
    ba = _mm(xb, w["wba"], mode="nn", name="mm_in_ba", tm=1024, tn=LANES)
    qn, kn, vv, gb = _qkv_prep(pq, ba, w["convw"], w["arow"], w["dtrow"])
    o, st, tinv, wy_u, wy_w = _delta_fwd(qn, kn, vv, gb)
    ya, yb = _mix_prep(o, proj, w["onw"], w["sg"], w["sb"], w["ws"], w["bst"])
    w = {**w, **late(ya)}
    pa, pb, m = _mm_gate_merge(ya, yb, w["wpa"], w["wpb"], proj)
    pre1, x1, x1b = _mm_resid_ln(m, w["wo"], x, w["ln1g"], w["ln1b"], name="mm_out_ln")
    hg, hu, h = _mm_swiglu(x1b, w["wg"], w["wu"])
    pre2, x2, x2b = _mm_resid_ln(h, w["wd"], x1, w["ln2g"], w["ln2b"], name="mm_down_ln")
    saved = dict(xb=xb, pq=pq, proj=proj, ba=ba, qn=qn, kn=kn, vv=vv, gb=gb, o=o, st=st, tinv=tinv, wy_u=wy_u, wy_w=wy_w,
                 ya=ya, yb=yb,
                 pa=pa, pb=pb, m=m, pre1=pre1, x1b=x1b, hg=hg, hu=hu, h=h, pre2=pre2)
    return x2, x2b, saved, w


def _layer_bwd(dpre2, ln2_grads, w, s, on_part=None):
    g = {}
    started = lambda part: on_part(part, g) if on_part is not None else None
    after = lambda v, token: v if token is None else v + token.astype(v.dtype)
    g["ln2g"], g["ln2b"] = ln2_grads
    dhg, dhu = _mm_swiglu_bwd(dpre2, w["wd"], s["hg"], s["hu"])
    g["wd"] = _mm(s["h"], dpre2, mode="tn", name="mm_tn_down", tm=1536, tk=1024, out_dtype=BF)
    dx1 = _mm(dhg, w["wg"], mode="nt", name="mm_nt_gu", pair=(dhu, w["wu"]), add=dpre2, add_scale=ALPHA, tm=1024, tk=1536)
    g["wg"] = _mm(s["x1b"], dhg, mode="tn", name="mm_tn_gu", tm=1024, tn=1536, tk=2048, out_dtype=BF)
    g["wu"] = _mm(s["x1b"], dhu, mode="tn", name="mm_tn_gu", tm=1024, tn=1536, tk=2048, out_dtype=BF)
    dpre1, g["ln1g"], g["ln1b"] = _ln_bwd(s["pre1"], w["ln1g"], w["ln1b"], dx1)
    g["wo"] = _mm(s["m"], dpre1, mode="tn", name="mm_tn_sq", tm=1024, tk=1024, out_dtype=BF)
    dpa, dpb, dproj = _mm_gate_merge_bwd(dpre1, w["wo"], s["pa"], s["pb"], s["proj"])
    dya = _mm(dpa, w["wpa"], mode="nt", name="mm_nt_sq")
    g["wpa"] = _mm(s["ya"], dpa, mode="tn", name="mm_tn_sq", tm=1024, tk=1024, out_dtype=BF)
    dyb = _mm(dpb, w["wpb"], mode="nt", name="mm_nt_sq")
    g["wpb"] = _mm(s["yb"], dpb, mode="tn", name="mm_tn_sq", tm=1024, tk=1024, out_dtype=BF)
    do, dproj, g["onw"], g["sg"], g["sb"], g["ws"], g["bst"] = _mix_prep_bwd(
        s["o"], s["proj"], after(w["onw"], started("late")), w["sg"], w["sb"], w["ws"], w["bst"], dya, dyb, dproj)
    dqn, dkn, dvv, dgb = _delta_bwd(s["qn"], s["kn"], s["vv"], s["gb"], s["st"], s["tinv"], s["wy_u"], s["wy_w"], do)
    dc, dba, g["convw"], g["arow"], g["dtrow"] = _qkv_prep_bwd(
        s["pq"], s["ba"], w["convw"], w["arow"], w["dtrow"], dqn, dkn, dvv, dgb)
    dproj = _conv_bwd(dc, w["convw"], dproj)
    g["win"] = _mm(s["xb"], dproj, mode="tn", name="mm_tn_in", tm=1024, tn=1024, tk=2048, out_dtype=BF)
    g["wba"] = _mm(s["xb"], dba, mode="tn", name="mm_tn_ba", tm=1024, tn=LANES, tk=1024, out_dtype=BF)
    dx = _mm(dba, after(w["wba"], started("early")), mode="nt", name="mm_nt_ba", add=dpre1, add_scale=ALPHA, tm=1024)
    dx = _mm(dproj, w["win"], mode="nt", name="mm_nt_in", add=dx, add_scale=1.0, tm=1024, tk=2048)
    return dx, g


def _local_step(x, tgt, layers, on_grads=None):
    saved, weights = [], []
    xb = x.astype(BF)
    for layer in layers:
        x, xb, s, w = _layer_fwd(x, xb, *layer(x))
        saved.append(s)
        weights.append(w)
    last = len(layers) - 1
    dpre2, dg, db, lacc = _loss_ln_bwd(x, tgt, saved[last]["pre2"], weights[last]["ln2g"], weights[last]["ln2b"])
    grads = [None] * len(layers)
    for l in reversed(range(len(layers))):
        on_part = functools.partial(on_grads, l) if on_grads is not None else None
        dx, grads[l] = _layer_bwd(dpre2, (dg, db), weights[l], saved[l], on_part)
        if l > 0:
            dpre2, dg, db = _ln_bwd(saved[l - 1]["pre2"], weights[l - 1]["ln2g"], weights[l - 1]["ln2b"], dx)
    return lacc[0, 0], dx, grads


_QKVZ = 4 * D_MODEL
_BA = 2 * N_HEADS


WEIGHT_NAMES = ("w_in", "conv_w", "a_log", "dt_bias", "o_norm_w", "sgu_ln_g", "sgu_ln_b", "w_s", "b_s", "w_pa", "w_pb",
                "w_o", "ln1_g", "ln1_b", "w_ffn_gate", "w_ffn_up", "w_ffn_down", "ln2_g", "ln2_b")
WIRE = ("w_in", "w_ffn_gate", "w_ffn_up", "w_ffn_down", "w_pa", "w_pb", "w_o", "conv_w")
SMALL = (("a_log", N_HEADS), ("dt_bias", N_HEADS), ("o_norm_w", D_HEAD), ("sgu_ln_g", D_MODEL), ("sgu_ln_b", D_MODEL),
         ("w_s", SGU_GROUPS * SGU_BLOCK * SGU_BLOCK), ("b_s", SGU_GROUPS * SGU_BLOCK),
         ("ln1_g", D_MODEL), ("ln1_b", D_MODEL), ("ln2_g", D_MODEL), ("ln2_b", D_MODEL))
SMALL_ROWS = -(-sum(n for _, n in SMALL) // (LANES * SUBLANES)) * SUBLANES
N_MAIN_TILES = (N_IN - _BA) // D_MODEL
ADAM_TILES = dict(w_in=(128, "adamw_in"), w_ffn_gate=(256, "adamw_ffn_cols"), w_ffn_up=(256, "adamw_ffn_cols"),
                  w_ffn_down=(32, "adamw_ffn_rows"), w_pa=(128, "adamw_sq"), w_pb=(128, "adamw_sq"), w_o=(128, "adamw_sq"),
                  conv_w=(CONV_K, "adamw_conv"))


def _pad_to(a, axis, size):
    pads = [(0, 0)] * a.ndim
    pads[axis] = (0, size - a.shape[axis])
    return jnp.pad(a, pads)


def _wire_blocks(p):
    return dict(
        w_in=_pad_to(p["w_in"].astype(BF), 2, IN_PAD),
        w_ffn_gate=_pad_to(p["w_ffn_gate"].astype(BF), 2, FFN_PAD), w_ffn_up=_pad_to(p["w_ffn_up"].astype(BF), 2, FFN_PAD),
        w_ffn_down=_pad_to(p["w_ffn_down"].astype(BF), 1, FFN_PAD),
        w_pa=p["w_pa"].astype(BF), w_pb=p["w_pb"].astype(BF), w_o=p["w_o"].astype(BF),
        conv_w=_pad_to(p["conv_w"], 1, SUBLANES),
    )


def _by_columns(blocks):
    n, r, c = blocks.shape
    return jnp.transpose(blocks, (1, 0, 2)).reshape(r, n * c)


def _to_slots(full, c):
    r = full.shape[0]
    return jnp.transpose(full.reshape(r, N_DEV, c), (1, 0, 2))


def _lane_row(v, at):
    return jnp.pad(v[None], ((0, 0), (at, LANES - at - v.shape[0])))


EARLY = ("w_in", "conv_w")
LATE = ("w_pa", "w_pb", "w_o", "w_ffn_gate", "w_ffn_up", "w_ffn_down")


def _early_weights(stacks, p, l):
    return dict(
        win=_perm_in(stacks["w_in"], D_MODEL, N_MAIN_TILES), wba=_perm_in(stacks["w_in"], LANES, 1),
        convw=_by_columns(stacks["conv_w"][:, :CONV_K]),
        arow=_lane_row(p["a_log"][l], N_HEADS), dtrow=_lane_row(p["dt_bias"][l], N_HEADS),
        onw=p["o_norm_w"][l][None], sg=p["sgu_ln_g"][l][None], sb=p["sgu_ln_b"][l][None],
        ws=p["w_s"][l], bst=_pad_to(p["b_s"][l].T, 1, LANES),
        ln1g=p["ln1_g"][l][None], ln1b=p["ln1_b"][l][None], ln2g=p["ln2_g"][l][None], ln2b=p["ln2_b"][l][None],
    )


def _late_weights(stacks):
    return dict(
        wpa=stacks["w_pa"].reshape(D_MODEL, D_MODEL), wpb=stacks["w_pb"].reshape(D_MODEL, D_MODEL),
        wo=stacks["w_o"].reshape(D_MODEL, D_MODEL),
        wg=_by_columns(stacks["w_ffn_gate"]), wu=_by_columns(stacks["w_ffn_up"]),
        wd=stacks["w_ffn_down"].reshape(FFN_K, D_MODEL),
    )


def _small_pack(parts):
    flat = jnp.concatenate([parts[n].reshape(-1) for n, _ in SMALL])
    return _pad_to(flat, 0, SMALL_ROWS * LANES).reshape(SMALL_ROWS, LANES)


def _small_unpack(rows, like):
    flat, out, off = rows.reshape(-1), {}, 0
    for n, size in SMALL:
        out[n] = flat[off:off + size].reshape(like[n].shape[1:])
        off += size
    return out


def _late_slots(g):
    slots = dict(
        w_ffn_gate=_to_slots(g["wg"], FFN_PAD), w_ffn_up=_to_slots(g["wu"], FFN_PAD),
        w_ffn_down=g["wd"].reshape(N_DEV, FFN_PAD, D_MODEL),
        w_pa=g["wpa"].reshape(N_DEV, D_MODEL // N_DEV, D_MODEL), w_pb=g["wpb"].reshape(N_DEV, D_MODEL // N_DEV, D_MODEL),
        w_o=g["wo"].reshape(N_DEV, D_MODEL // N_DEV, D_MODEL),
    )
    return [slots[n] for n in LATE]


def _early_slots(g):
    slots = [_perm_out(g["win"], g["wba"]), _pad_to(_to_slots(g["convw"][:CONV_K], 3 * D_MODEL // N_DEV), 1, SUBLANES)]
    small = _small_pack(dict(
        a_log=g["arow"][0, N_HEADS:2 * N_HEADS], dt_bias=g["dtrow"][0, N_HEADS:2 * N_HEADS], o_norm_w=g["onw"][0],
        sgu_ln_g=g["sg"][0], sgu_ln_b=g["sb"][0], w_s=g["ws"], b_s=g["bst"][:, :SGU_GROUPS].T,
        ln1_g=g["ln1g"][0], ln1_b=g["ln1b"][0], ln2_g=g["ln2g"][0], ln2_b=g["ln2b"][0]))
    return slots, small


def _in_tile_start(j, tile_w):
    if tile_w == LANES:
        return jnp.int32(_QKVZ)
    return j * D_MODEL + jnp.where(j >= _QKVZ // D_MODEL, _BA, 0)


def _select(rows_iota, cols_iota, dev, start, valid):
    hit = (rows_iota + (dev * IN_BLOCK - start) == cols_iota) & (rows_iota < IN_BLOCK) & (cols_iota < valid)
    return jnp.where(hit, 1.0, 0.0).astype(BF)


def _perm_in(stack, tile_w, n_tiles):
    valid = _BA if tile_w == LANES else tile_w

    def first_dev(j):
        return lax.div(_in_tile_start(j, tile_w), jnp.int32(IN_BLOCK))

    def body(w_ref, o_ref, acc_ref):
        j, k = pl.program_id(0), pl.program_id(1)
        sel = _select(_iota((IN_PAD, tile_w), 0), _iota((IN_PAD, tile_w), 1), first_dev(j) + k,
                      _in_tile_start(j, tile_w), valid)
        part = jnp.dot(w_ref[0], sel, preferred_element_type=F32)

        @pl.when(k == 0)
        def _():
            acc_ref[...] = part

        @pl.when(k == 1)
        def _():
            o_ref[...] = (acc_ref[...] + part).astype(BF)

    est = _nbytes((D_MODEL, IN_PAD), BF) + 3 * _nbytes((D_MODEL, tile_w), F32) + 2 * _nbytes((IN_PAD, tile_w), F32)
    return pl.pallas_call(
        body, name="perm_in" if tile_w != LANES else "perm_in_ba", grid=(n_tiles, 2),
        in_specs=[pl.BlockSpec((1, D_MODEL, IN_PAD), lambda j, k: (jnp.minimum(first_dev(j) + k, N_DEV - 1), 0, 0))],
        out_specs=pl.BlockSpec((D_MODEL, tile_w), lambda j, k: (0, j)),
        out_shape=jax.ShapeDtypeStruct((D_MODEL, n_tiles * tile_w), BF),
        scratch_shapes=[pltpu.VMEM((D_MODEL, tile_w), F32)],
        compiler_params=_cparams(est, ("parallel", "arbitrary")),
    )(stack)


def _perm_out(dmain, dba):
    def tile(d, s):
        c0 = d * IN_BLOCK
        first = lax.div(c0 - jnp.where(c0 < _QKVZ, 0, jnp.minimum(c0 - _QKVZ, _BA)), jnp.int32(D_MODEL))
        return jnp.minimum(first + jnp.minimum(s, 1), N_MAIN_TILES - 1)

    def body(dm_ref, db_ref, o_ref, acc_ref):
        d, s = pl.program_id(0), pl.program_id(1)

        @pl.when(s == 0)
        def _():
            acc_ref[...] = jnp.zeros_like(acc_ref)

        start = _in_tile_start(tile(d, s), D_MODEL)
        overlaps = (start < (d + 1) * IN_BLOCK) & (d * IN_BLOCK < start + D_MODEL)

        @pl.when((s < 2) & overlaps)
        def _():
            sel = _select(_iota((D_MODEL, IN_PAD), 1), _iota((D_MODEL, IN_PAD), 0), d, start, D_MODEL)
            acc_ref[...] += jnp.dot(dm_ref[...], sel, preferred_element_type=F32)

        @pl.when(s == 2)
        def _():
            sel = _select(_iota((LANES, IN_PAD), 1), _iota((LANES, IN_PAD), 0), d, jnp.int32(_QKVZ), _BA)
            o_ref[0] = (acc_ref[...] + jnp.dot(db_ref[...], sel, preferred_element_type=F32)).astype(BF)

    est = 2 * _nbytes((D_MODEL, D_MODEL), BF) + 4 * _nbytes((D_MODEL, IN_PAD), F32)
    return pl.pallas_call(
        body, name="perm_out", grid=(N_DEV, 3),
        in_specs=[pl.BlockSpec((D_MODEL, D_MODEL), lambda d, s: (0, tile(d, s))),
                  pl.BlockSpec((D_MODEL, LANES), lambda d, s: (0, 0))],
        out_specs=pl.BlockSpec((1, D_MODEL, IN_PAD), lambda d, t: (d, 0, 0)),
        out_shape=jax.ShapeDtypeStruct((N_DEV, D_MODEL, IN_PAD), BF),
        scratch_shapes=[pltpu.VMEM((D_MODEL, IN_PAD), F32)],
        compiler_params=_cparams(est, ("parallel", "arbitrary")),
    )(dmain, dba)


def _mesh_place():
    x, y, c = (lax.axis_index(a) for a in MESH_AXES)
    return x, y, c


def _slot(x, y, c):
    return 4 * x + 2 * y + c


def _peer(place, j):
    x, y, c = place
    return (1 - x if j & 4 else x, 1 - y if j & 2 else y, 1 - c if j & 1 else c)


_HBM = pl.BlockSpec(memory_space=pltpu.HBM)
_SEM = pl.BlockSpec(memory_space=pltpu.SEMAPHORE)
_EFFECT = pltpu.SideEffectType.DATAFLOW_SIDE_EFFECTING


def _remote_copy(src_ref, land_ref, slot, per_slot, pslot, sems, u, j, peer):
    return pltpu.make_async_remote_copy(
        src_ref=src_ref.at[pslot] if per_slot else src_ref, dst_ref=land_ref.at[slot],
        send_sem=sems[0].at[u * (N_DEV - 1) + j - 1], recv_sem=sems[1].at[u * (N_DEV - 1) + j - 1],
        device_id=peer, device_id_type=pl.DeviceIdType.MESH)


def _own_copy(src_ref, land_ref, me, per_slot, sems, u):
    return pltpu.make_async_copy(src_ref.at[me] if per_slot else src_ref, land_ref.at[me], sems[2].at[u])


def _exchange_start(name, srcs, per_slot):
    n = len(srcs)
    lands = [jax.ShapeDtypeStruct(s.shape if p else (N_DEV,) + s.shape, s.dtype) for s, p in zip(srcs, per_slot)]

    def body(*refs):
        src_refs, sems, land_refs, token = refs[:n], refs[n:n + 3], refs[2 * n + 3:3 * n + 3], refs[-1]
        place = _mesh_place()
        me = _slot(*place)
        for u in range(n):
            _own_copy(src_refs[u], land_refs[u], me, per_slot[u], sems, u).start()
            for j in range(1, N_DEV):
                peer = _peer(place, j)
                _remote_copy(src_refs[u], land_refs[u], me, per_slot[u], _slot(*peer), sems, u, j, peer).start()
        token[...] = jnp.zeros_like(token)

    hbm = lambda a: pltpu.HBM(a.shape, a.dtype)
    sem = pltpu.SemaphoreType.DMA((n * (N_DEV - 1),))
    outs = pl.pallas_call(
        body, name=name,
        out_shape=(sem, sem, pltpu.SemaphoreType.DMA((n,)), *[hbm(a) for a in srcs], *[hbm(a) for a in lands],
                   jax.ShapeDtypeStruct((SUBLANES, LANES), F32)),
        in_specs=[_HBM] * n, out_specs=(_SEM, _SEM, _SEM, *[_HBM] * (2 * n), pl.BlockSpec(memory_space=pltpu.VMEM)),
        input_output_aliases={i: 3 + i for i in range(n)},
        compiler_params=pltpu.CompilerParams(has_side_effects=_EFFECT),
    )(*[pltpu.with_memory_space_constraint(a, pltpu.HBM) for a in srcs])
    return tuple(outs[:3]), list(outs[3:3 + n]), list(outs[3 + n:3 + 2 * n]), outs[-1]


def _exchange_wait(name, sems, srcs, lands, units, per_slot, after):
    m = len(units)

    def body(*refs):
        src_refs, land_refs, sem_refs = refs[:m], refs[m:2 * m], refs[2 * m:2 * m + 3]
        place = _mesh_place()
        me = _slot(*place)
        for i, u in enumerate(units):
            _own_copy(src_refs[i], land_refs[i], me, per_slot[u], sem_refs, u).wait()
            for j in range(1, N_DEV):
                peer = _peer(place, j)
                pslot = _slot(*peer)
                cp = _remote_copy(src_refs[i], land_refs[i], pslot, per_slot[u], pslot, sem_refs, u, j, peer)
                cp.wait_send()
                cp.wait_recv()

    hbm = lambda a: pltpu.HBM(a.shape, a.dtype)
    outs = pl.pallas_call(
        body, name=name, out_shape=tuple(hbm(a) for a in list(srcs) + list(lands)),
        in_specs=[_HBM] * (2 * m) + [_SEM] * 3 + [pl.BlockSpec(memory_space=pl.ANY)], out_specs=tuple([_HBM] * (2 * m)),
        input_output_aliases={i: i for i in range(2 * m)},
        compiler_params=pltpu.CompilerParams(has_side_effects=_EFFECT),
    )(*srcs, *lands, *sems, after)
    return list(outs[m:])


def _adam_update(g, w, m, v):
    m = ADAM_B1 * m + (1.0 - ADAM_B1) * g
    v = ADAM_B2 * v + (1.0 - ADAM_B2) * jnp.square(g)
    m_hat = m / (1.0 - ADAM_B1 ** ADAM_STEP)
    v_hat = v / (1.0 - ADAM_B2 ** ADAM_STEP)
    return -ADAM_LR * (m_hat / (jnp.sqrt(v_hat) + ADAM_EPS) + ADAM_WD * w), m, v


def _adamw(recvs, w, m, v, *, tr, name):
    L, R, C = w.shape
    rp = max(tr, SUBLANES * (4 // jnp.dtype(recvs[0].dtype).itemsize))
    Cp = recvs[0].shape[2]

    def body(*refs):
        r_refs, (w_ref, m_ref, v_ref, g_ref, d_ref, nm_ref, nv_ref) = refs[:L], refs[L:]
        for l in range(L):
            @pl.when(pl.program_id(0) == l)
            def _(r_ref=r_refs[l]):
                g = r_ref[0, :tr, :C].astype(F32)
                for s in range(1, N_DEV):
                    g = g + r_ref[s, :tr, :C].astype(F32)
                d, nm, nv = _adam_update(g, w_ref[0], m_ref[0], v_ref[0])
                g_ref[0], d_ref[0], nm_ref[0], nv_ref[0] = g, d, nm, nv

    blk = pl.BlockSpec((1, tr, C), lambda l, i: (l, i, 0))
    r_specs = [pl.BlockSpec((N_DEV, rp, Cp), lambda l, i, k=k: (0, jnp.where(l == k, i, 0), 0)) for k in range(L)]
    est = 2 * _nbytes((N_DEV, rp, Cp), recvs[0].dtype) + 8 * _nbytes((tr, Cp), F32)
    return pl.pallas_call(
        body, name=name, grid=(L, R // tr),
        in_specs=r_specs + [blk] * 3, out_specs=[blk] * 4,
        out_shape=[jax.ShapeDtypeStruct((L, R, C), F32)] * 4,
        compiler_params=_cparams(est, ("arbitrary", "arbitrary")),
    )(*recvs, w, m, v)


def _adamw_small(recv, w, m, v):
    def body(r_ref, w_ref, m_ref, v_ref, g_ref, d_ref, nm_ref, nv_ref):
        g = r_ref[0]
        for s in range(1, N_DEV):
            g = g + r_ref[s]
        g_ref[...] = g
        d_ref[...], nm_ref[...], nv_ref[...] = _adam_update(g, w_ref[...], m_ref[...], v_ref[...])

    vm = pl.BlockSpec(memory_space=pltpu.VMEM)
    return pl.pallas_call(
        body, name="adamw_small", in_specs=[vm] * 4, out_specs=[vm] * 4,
        out_shape=[jax.ShapeDtypeStruct((SMALL_ROWS, LANES), F32)] * 4,
        compiler_params=_cparams(20 * _nbytes((SMALL_ROWS, LANES), F32)),
    )(recv, w, m, v)


def kernel(x, w_in, conv_w, a_log, dt_bias, o_norm_w, sgu_ln_g, sgu_ln_b, w_s, b_s, w_pa, w_pb, w_o, ln1_g, ln1_b, w_ffn_gate, w_ffn_up, w_ffn_down, ln2_g, ln2_b, loss_target, m_w_in, m_conv_w, m_a_log, m_dt_bias, m_o_norm_w, m_sgu_ln_g, m_sgu_ln_b, m_w_s, m_b_s, m_w_pa, m_w_pb, m_w_o, m_ln1_g, m_ln1_b, m_w_ffn_gate, m_w_ffn_up, m_w_ffn_down, m_ln2_g, m_ln2_b, v_w_in, v_conv_w, v_a_log, v_dt_bias, v_o_norm_w, v_sgu_ln_g, v_sgu_ln_b, v_w_s, v_b_s, v_w_pa, v_w_pb, v_w_o, v_ln1_g, v_ln1_b, v_w_ffn_gate, v_w_ffn_up, v_w_ffn_down, v_ln2_g, v_ln2_b):
    given = dict(locals())
    P = {n: given[n] for n in WEIGHT_NAMES}
    M = {n: given["m_" + n] for n in WEIGHT_NAMES}
    V = {n: given["v_" + n] for n in WEIGHT_NAMES}

    wire = _wire_blocks(P)
    units = [(n, l) for l in range(DEPTH) for n in EARLY + LATE]
    whole = [False] * len(units)
    g_sems, g_srcs, g_lands, g_token = _exchange_start("gather_start", [wire[n][l] for n, l in units], whole)

    def gathered(name, names, l, after):
        idx = [units.index((n, l)) for n in names]
        got = _exchange_wait(name, g_sems, [g_srcs[i] for i in idx], [g_lands[i] for i in idx], idx, whole, after)
        return dict(zip(names, got))

    def layer(l):
        def weights(x_in):
            after = g_token if l == 0 else x_in
            early = _early_weights(gathered(f"gather_wait_early{l}", EARLY, l, after), P, l)
            return early, lambda ya: _late_weights(gathered(f"gather_wait_late{l}", LATE, l, ya))
        return weights

    pending = {}

    def on_grads(l, part, g):
        if part == "late":
            srcs, names = _late_slots(g), LATE
            per_slot = [True] * len(srcs)
        else:
            slots, small = _early_slots(g)
            srcs, names = slots + [small], EARLY + ("small",)
            per_slot = [True] * len(slots) + [False]
        sems, s_thru, l_thru, token = _exchange_start(f"exchange_start_{part}{l}", srcs, per_slot)
        pending[l, part] = (names, sems, s_thru, l_thru, per_slot)
        return token[0, 0]

    loss_local, dx, _ = _local_step(x[0], loss_target[0], [layer(l) for l in range(DEPTH)], on_grads)
    loss = lax.psum(loss_local, MESH_AXES)

    recv = [{} for _ in range(DEPTH)]
    for l in reversed(range(DEPTH)):
        for part in ("late", "early"):
            names, sems, s_thru, l_thru, per_slot = pending[l, part]
            got = _exchange_wait(f"exchange_wait_{part}{l}", sems, s_thru, l_thru, list(range(len(s_thru))), per_slot, dx)
            recv[l].update(zip(names, got))

    out = {}
    for n in WIRE:
        tr, name = ADAM_TILES[n]
        out[n] = _adamw([recv[l][n] for l in range(DEPTH)], P[n], M[n], V[n], tr=tr, name=name)
    small = [_adamw_small(recv[l]["small"], *[_small_pack({n: T[n][l] for n, _ in SMALL}) for T in (P, M, V)])
             for l in range(DEPTH)]
    for n, _ in SMALL:
        out[n] = [jnp.stack([_small_unpack(small[l][i], P)[n] for l in range(DEPTH)]) for i in range(4)]
    return (loss, dx[None], *[out[n][i] for i in range(4) for n in WEIGHT_NAMES])
```

```python
import functools
import math

import jax
import jax.numpy as jnp
from jax import lax
from jax.experimental import pallas as pl
from jax.experimental.pallas import tpu as pltpu

F32 = jnp.float32
BF = jnp.bfloat16
HIGHEST = lax.Precision.HIGHEST

D_MODEL = 1024
DEPTH = 2
N_HEADS = 8
D_HEAD = 128
CONV_K = 4
SGU_BLOCK = 128
SGU_GROUPS = 8
SGU_CHUNK = 64
FFN_HIDDEN = 2816
N_IN = 8208
N_DEV = 8
IN_BLOCK, IN_PAD = N_IN // N_DEV, 1152
FFN_BLOCK, FFN_PAD = FFN_HIDDEN // N_DEV, 384
FFN_K = N_DEV * FFN_PAD
ALPHA = (2 * DEPTH) ** 0.25
LN_EPS = 1e-5
RMS_EPS = 1e-6
ADAM_LR, ADAM_B1, ADAM_B2, ADAM_EPS, ADAM_WD, ADAM_STEP = 0.001, 0.9, 0.999, 1e-08, 0.01, 10

MESH_AXES = ("x", "y", "c")
DELTA_CHUNK = 128
DELTA_HEADS_PER_STEP = 8
LANES = 128
SUBLANES = 8
VMEM_BYTES = 64 * 1024 * 1024
HALO = SUBLANES
HALO_BF = 2 * SUBLANES


def _cparams(est_bytes, dims=None):
    limit = int(min(max(2 * est_bytes + (8 << 20), 32 << 20), VMEM_BYTES - (6 << 20)))
    kw = dict(vmem_limit_bytes=limit)
    if dims is not None:
        kw["dimension_semantics"] = dims
    return pltpu.CompilerParams(**kw)


def _nbytes(shape, dtype):
    return math.prod(shape) * jnp.dtype(dtype).itemsize


def _dims(kind, ndim):
    lhs, rhs = {"nn": (1, 0), "nt": (1, 1), "tn": (0, 0)}[kind]
    b = ndim - 2
    return (((lhs + b,), (rhs + b,)), (tuple(range(b)), tuple(range(b))))


def _mxu(a, b, kind):
    return lax.dot_general(a, b, _dims(kind, a.ndim), preferred_element_type=F32)


def _dot(a, b):
    return _mxu(a.astype(BF), b.astype(BF), "nn")


def _dot_nt(a, b):
    return _mxu(a.astype(BF), b.astype(BF), "nt")


def _dot_tn(a, b):
    return _mxu(a.astype(BF), b.astype(BF), "tn")


def _split(a):
    hi = a.astype(BF)
    return hi, (a - hi.astype(F32)).astype(BF)


def _dot3(a, b, kind):
    (ah, al), (bh, bl) = _split(a), _split(b)
    return _mxu(ah, bh, kind) + (_mxu(ah, bl, kind) + _mxu(al, bh, kind))


def _dotf(a, b):
    return _dot3(a, b, "nn")


def _dotf_nt(a, b):
    return _dot3(a, b, "nt")


def _dot01(sel, x, kind="nn"):
    s = jnp.broadcast_to(sel.astype(BF), x.shape[:-2] + sel.shape)
    h1 = x.astype(BF)
    r1 = x - h1.astype(F32)
    h2 = r1.astype(BF)
    h3 = (r1 - h2.astype(F32)).astype(BF)
    return _mxu(s, h1, kind) + (_mxu(s, h2, kind) + _mxu(s, h3, kind))


def _sigmoid(x):
    return 0.5 * jnp.tanh(0.5 * x) + 0.5


def _silu(x):
    return x * _sigmoid(x)


def _silu_and_grad(x):
    s = _sigmoid(x)
    return x * s, s * (1.0 + x * (1.0 - s))


def _gelu(x):
    return 0.5 * x * (1.0 + lax.erf(x * 0.7071067811865476))


def _softplus(x):
    return jnp.maximum(x, 0.0) + jnp.log1p(jnp.exp(-jnp.abs(x)))


def _ln(x, g, b):
    mu = jnp.mean(x, -1, keepdims=True)
    xc = x - mu
    var = jnp.mean(xc * xc, -1, keepdims=True)
    return xc * lax.rsqrt(var + LN_EPS) * g + b


def _iota(shape, dim):
    return lax.broadcasted_iota(jnp.int32, shape, dim)


def _tile(n, pref, align):
    if n <= pref:
        return n
    t = (pref // align) * align
    while t >= align:
        if n % t == 0:
            return t
        t -= align
    raise ValueError(f"no tile for {n} (pref {pref}, align {align})")


def _bcast_rows(v, rows=SUBLANES):
    return jnp.broadcast_to(v, (rows, v.shape[-1]))


def _mm(a, b, *, mode, name, out_dtype=F32, add=None, add_scale=1.0, tm=512, tn=1024, tk=1024, cols=None, pair=None):
    if mode == "nn":
        (M, K), N = a.shape, b.shape[1]
    elif mode == "nt":
        (M, K), N = a.shape, b.shape[0]
    else:
        (K, M), N = a.shape, b.shape[1]
    col0 = 0
    if cols is not None:
        col0, N = cols
    tm = _tile(M, tm, LANES if mode == "tn" else SUBLANES * 2)
    tn = _tile(N, tn, LANES)
    tk = _tile(K, tk, LANES)
    nk = K // tk
    j0 = col0 // tn
    if mode == "nn":
        a_spec = pl.BlockSpec((tm, tk), lambda i, j, k: (i, k))
        b_spec = pl.BlockSpec((tk, tn), lambda i, j, k: (k, j + j0))
        dot = _dot
    elif mode == "nt":
        a_spec = pl.BlockSpec((tm, tk), lambda i, j, k: (i, k))
        b_spec = pl.BlockSpec((tn, tk), lambda i, j, k: (j, k))
        dot = _dot_nt
    else:
        a_spec = pl.BlockSpec((tk, tm), lambda i, j, k: (k, i))
        b_spec = pl.BlockSpec((tk, tn), lambda i, j, k: (k, j))
        dot = _dot_tn
    o_spec = pl.BlockSpec((tm, tn), lambda i, j, k: (i, j))
    has_add = add is not None

    n_ab = 2 if pair is None else 4

    def body(*refs):
        ab, (o_ref, acc_ref) = refs[:n_ab], refs[-2:]
        add_ref = refs[n_ab] if has_add else None
        k = pl.program_id(2)
        part = dot(ab[0][...], ab[1][...])
        if pair is not None:
            part = part + dot(ab[2][...], ab[3][...])

        def finish(total):
            if has_add:
                total = total + add_scale * add_ref[...]
            o_ref[...] = total.astype(out_dtype)

        if nk == 1:
            finish(part)
        else:
            @pl.when(k == 0)
            def _():
                acc_ref[...] = part

            @pl.when(jnp.logical_and(k > 0, k < nk - 1))
            def _():
                acc_ref[...] += part

            @pl.when(k == nk - 1)
            def _():
                finish(acc_ref[...] + part)

    in_specs = [a_spec, b_spec] * (n_ab // 2) + ([o_spec] if has_add else [])
    args = (a, b) + (tuple(pair) if pair is not None else ()) + ((add,) if has_add else ())
    est = ((n_ab // 2) * (_nbytes((tm, tk), a.dtype) + _nbytes((tk, tn), b.dtype)) + 2 * _nbytes((tm, tn), F32)
           + (_nbytes((tm, tn), F32) if has_add else 0)) + 2 * _nbytes((tm, tn), F32)
    return pl.pallas_call(
        body, name=name,
        grid=(M // tm, N // tn, nk),
        in_specs=in_specs, out_specs=o_spec,
        out_shape=jax.ShapeDtypeStruct((M, N), out_dtype),
        scratch_shapes=[pltpu.VMEM((tm, tn) if nk > 1 else (SUBLANES, LANES), F32)],
        compiler_params=_cparams(est, ("parallel", "parallel", "arbitrary")),
    )(*args)


def _conv_taps(xt, halo, w_ref, first):
    halo = jnp.where(first, 0.0, halo)
    xc = jnp.concatenate([halo, xt], axis=0)
    shifted = [xt] + [pltpu.roll(xc, s, 0)[HALO:] for s in range(1, CONV_K)]
    out = shifted[0] * w_ref[CONV_K - 1:CONV_K, :]
    for s in range(1, CONV_K):
        out = out + shifted[s] * w_ref[CONV_K - 1 - s:CONV_K - s, :]
    return out, shifted


def _gates(ba, arow, dtrow):
    lane = _iota(ba.shape, 1)
    beta = _sigmoid(ba)
    g = -jnp.exp(arow) * _softplus(ba + dtrow)
    return jnp.where(lane < N_HEADS, beta, jnp.where(lane < 2 * N_HEADS, g, 0.0))


def _l2n(x):
    return x * lax.rsqrt(jnp.sum(x * x, -1, keepdims=True) + RMS_EPS)


def _qkv_prep(proj, ba, convw, arow, dtrow, *, tm=256):
    S = proj.shape[0]
    tm = _tile(S, tm, SUBLANES)
    W3 = 3 * D_MODEL
    hb = tm // HALO

    def body(xt_ref, halo_ref, ba_ref, w_ref, a_ref, dt_ref, q_ref, k_ref, v_ref, gb_ref):
        c, _ = _conv_taps(xt_ref[...], halo_ref[...], w_ref, pl.program_id(0) == 0)
        c = _silu(c)
        for h in range(N_HEADS):
            lo = h * D_HEAD
            q_ref[:, lo:lo + D_HEAD] = _l2n(c[:, lo:lo + D_HEAD])
            k_ref[:, lo:lo + D_HEAD] = _l2n(c[:, D_MODEL + lo:D_MODEL + lo + D_HEAD])
        v_ref[...] = c[:, 2 * D_MODEL:]
        gb_ref[...] = _gates(ba_ref[...], a_ref[...], dt_ref[...])

    row = lambda w, col=0: pl.BlockSpec((tm, w), lambda i: (i, col))
    full = lambda shape: pl.BlockSpec(shape, lambda i: (0,) * len(shape))
    est = 4 * _nbytes((tm, W3), F32)
    return pl.pallas_call(
        body, name="qkv_prep", grid=(S // tm,),
        in_specs=[row(W3), pl.BlockSpec((HALO, W3), lambda i: (jnp.maximum(i * hb - 1, 0), 0)), row(LANES),
                  full((CONV_K, W3)), full((1, LANES)), full((1, LANES))],
        out_specs=[row(D_MODEL), row(D_MODEL), row(D_MODEL), row(LANES)],
        out_shape=[jax.ShapeDtypeStruct((S, D_MODEL), F32)] * 3 + [jax.ShapeDtypeStruct((S, LANES), F32)],
        compiler_params=_cparams(est, ("arbitrary",)),
    )(proj, proj, ba, convw, arow, dtrow)


def _qkv_prep_bwd(proj, ba, convw, arow, dtrow, dq, dk, dv, dgb, *, tm=256):
    S = proj.shape[0]
    tm = _tile(S, tm, SUBLANES * 2)
    W3 = 3 * D_MODEL
    hb = tm // HALO

    def body(xt_ref, halo_ref, ba_ref, w_ref, a_ref, dt_ref, dq_ref, dk_ref, dv_ref, dgb_ref,
             dcb_ref, dba_ref, dw_ref, da_ref, ddt_ref, dc_ref):
        i = pl.program_id(0)

        @pl.when(i == 0)
        def _():
            dw_ref[...] = jnp.zeros_like(dw_ref)
            da_ref[...] = jnp.zeros_like(da_ref)
            ddt_ref[...] = jnp.zeros_like(ddt_ref)

        c, shifted = _conv_taps(xt_ref[...], halo_ref[...], w_ref, i == 0)
        a, ds = _silu_and_grad(c)
        for h in range(N_HEADS):
            for base, d_ref in ((0, dq_ref), (D_MODEL, dk_ref)):
                lo = base + h * D_HEAD
                _, vj = jax.vjp(_l2n, a[:, lo:lo + D_HEAD])
                (dx,) = vj(d_ref[:, h * D_HEAD:(h + 1) * D_HEAD])
                dc_ref[:, lo:lo + D_HEAD] = dx * ds[:, lo:lo + D_HEAD]
        dc_ref[:, 2 * D_MODEL:] = dv_ref[...] * ds[:, 2 * D_MODEL:]
        dc = dc_ref[...]
        dcb_ref[...] = dc.astype(BF)
        for s in range(CONV_K):
            kk = CONV_K - 1 - s
            dw_ref[kk:kk + 1, :] += jnp.sum(dc * shifted[s], axis=0, keepdims=True)
        _, vj = jax.vjp(_gates, ba_ref[...], a_ref[...], dt_ref[...])
        dba, da, ddt = vj(dgb_ref[...])
        dba_ref[...] = dba.astype(BF)
        da_ref[...] += _bcast_rows(da)
        ddt_ref[...] += _bcast_rows(ddt)

    row = lambda w, col=0: pl.BlockSpec((tm, w), lambda i: (i, col))
    full = lambda shape: pl.BlockSpec(shape, lambda i: (0,) * len(shape))
    est = 8 * _nbytes((tm, W3), F32)
    return pl.pallas_call(
        body, name="qkv_prep_bwd", grid=(S // tm,),
        in_specs=[row(W3), pl.BlockSpec((HALO, W3), lambda i: (jnp.maximum(i * hb - 1, 0), 0)), row(LANES),
                  full((CONV_K, W3)), full((1, LANES)), full((1, LANES)),
                  row(D_MODEL), row(D_MODEL), row(D_MODEL), row(LANES)],
        out_specs=[row(W3), row(LANES), full((SUBLANES, W3)), full((SUBLANES, LANES)), full((SUBLANES, LANES))],
        out_shape=[jax.ShapeDtypeStruct((S, W3), BF), jax.ShapeDtypeStruct((S, LANES), BF),
                   jax.ShapeDtypeStruct((SUBLANES, W3), F32), jax.ShapeDtypeStruct((SUBLANES, LANES), F32),
                   jax.ShapeDtypeStruct((SUBLANES, LANES), F32)],
        scratch_shapes=[pltpu.VMEM((tm, W3), F32)],
        compiler_params=_cparams(est, ("arbitrary",)),
    )(proj, proj, ba, convw, arow, dtrow, dq, dk, dv, dgb)


def _conv_bwd(dc, convw, dproj, *, tm=256):
    S, W3 = dc.shape
    tm = _tile(S, tm, HALO_BF)
    hb = tm // HALO_BF
    nt = S // tm

    def body(dc_ref, nxt_ref, w_ref, dproj_ref, o_ref):
        last = pl.program_id(0) == nt - 1
        nxt = jnp.where(last, 0.0, nxt_ref[...].astype(F32))
        cur = dc_ref[...].astype(F32)
        xc = jnp.concatenate([cur, nxt], axis=0)
        out = cur * w_ref[CONV_K - 1:CONV_K, :]
        for s in range(1, CONV_K):
            out = out + pltpu.roll(xc, tm + HALO_BF - s, 0)[:tm] * w_ref[CONV_K - 1 - s:CONV_K - s, :]
        o_ref[...] = out.astype(BF)

    est = 5 * _nbytes((tm, W3), F32)
    return pl.pallas_call(
        body, name="conv_bwd", grid=(nt,),
        in_specs=[pl.BlockSpec((tm, W3), lambda i: (i, 0)),
                  pl.BlockSpec((HALO_BF, W3), lambda i: (jnp.minimum((i + 1) * hb, S // HALO_BF - 1), 0)),
                  pl.BlockSpec((CONV_K, W3), lambda i: (0, 0)), pl.BlockSpec(memory_space=pl.ANY)],
        out_specs=pl.BlockSpec((tm, W3), lambda i: (i, 0)),
        out_shape=jax.ShapeDtypeStruct(dproj.shape, BF),
        input_output_aliases={3: 0},
        compiler_params=_cparams(est, ("parallel",)),
    )(dc, dc, convw, dproj)


NEUMANN_BLOCK = 8


def _inv_unit_lower(A):
    C = A.shape[-1]
    row, col = _iota((C, C), 0), _iota((C, C), 1)
    eye = jnp.where(row == col, 1.0, 0.0).astype(F32)
    Ab = A.astype(BF)
    sh = jnp.int32(int(math.log2(NEUMANN_BLOCK)))
    B = jnp.where(lax.shift_right_logical(row, sh) == lax.shift_right_logical(col, sh), Ab, jnp.zeros_like(Ab))
    B2 = _mxu(B, B, "nn")
    B4 = _dot3(B2, B2, "nn")
    b2h, b2l = _split(B2)
    P = eye - B.astype(F32) + B2 - (_mxu(B, b2h, "nn") + _mxu(B, b2l, "nn"))
    T = P + _dot3(P, B4, "nn")
    b = NEUMANN_BLOCK
    while b < C:
        hi = ~(2 * b - 1)
        off = ((row & hi) == (col & hi)) & ((row & b) != 0) & ((col & b) == 0)
        Aoff = jnp.where(off, Ab, jnp.zeros_like(Ab))
        th, tl = _split(T)
        xh, xl = _split(_mxu(th, Aoff, "nn") + _mxu(tl, Aoff, "nn"))
        T = T - (_mxu(xh, th, "nn") + (_mxu(xh, tl, "nn") + _mxu(xl, th, "nn")))
        b *= 2
    return T


def _delta_common(q, k, g, beta):
    C = q.shape[-2]
    row, col = _iota((C, C), 0), _iota((C, C), 1)
    tril = row >= col
    qs = q * (D_HEAD ** -0.5)
    gcb = _dot01(jnp.where(tril, 1.0, 0.0), jnp.broadcast_to(g, g.shape[:-1] + (LANES,)))
    gc = gcb[..., :1]
    gr = jnp.swapaxes(gcb, -1, -2)
    Dm = jnp.exp(jnp.where(tril, gc - gr, -1e30))
    Dmt = jnp.exp(jnp.where(row <= col, gr - gc, -1e30))
    eg = jnp.exp(gc)
    gl = jnp.sum(jnp.where(_iota((C, 1), 0) == C - 1, gc, 0.0), axis=(-2, -1), keepdims=True)
    el = jnp.exp(gl)
    er = jnp.exp(gl - gc)
    kb = k * beta
    KK = _dot_nt(kb, k)
    QK = _dot_nt(qs, k)
    return dict(row=row, col=col, tril=tril, qs=qs, gc=gc, Dm=Dm, Dmt=Dmt, eg=eg, el=el, er=er, kb=kb, KK=KK, QK=QK)


def _delta_chunk_fwd(S0, q, k, v, g, beta):
    m = _delta_common(q, k, g, beta)
    T = _inv_unit_lower(jnp.where(m["row"] > m["col"], m["KK"] * m["Dm"], 0.0))
    u = _dotf(T, v * beta)
    w = _dotf(T, m["kb"] * m["eg"])
    vn = u - _dot(w, S0)
    o = _dot(m["qs"] * m["eg"], S0) + _dot(m["QK"] * m["Dm"], vn)
    S1 = S0 * m["el"] + _dot_tn(k * m["er"], vn)
    return o, S1, jnp.swapaxes(T, -1, -2), u, w


def _delta_chunk_bwd(S0, q, k, v, g, beta, Tt, u, w, do, dS1):
    m = _delta_common(q, k, g, beta)
    C = q.shape[-2]
    qs, Dm, Dmt, eg, el, er, kb, KK, QK = (m[n] for n in ("qs", "Dm", "Dmt", "eg", "el", "er", "kb", "KK", "QK"))
    strict = m["row"] > m["col"]
    total = lambda x: jnp.sum(x, axis=(-2, -1), keepdims=True)
    vn = u - _dot(w, S0)
    qg = qs * eg
    kr = k * er

    dvn = _dot(_dot_nt(k, qs) * Dmt, do) + _dot(kr, dS1)
    dS0 = dS1 * el + _dot_tn(qg, do) - _dot_tn(w, dvn)
    d_el = total(dS1 * S0)
    dqg = _dot_nt(do, S0)
    dqs = dqg * eg
    deg = jnp.sum(dqg * qs, -1, keepdims=True)
    dP = _dot_nt(do, vn)
    dPD = dP * Dm
    dqs = dqs + _dot(dPD, k)
    dk = _dot(_dot_nt(vn, do) * Dmt, qs)
    dD = dP * QK
    dkr = _dot_nt(vn, dS1)
    dk = dk + dkr * er
    der = jnp.sum(dkr * k, -1, keepdims=True)
    dw = -_dot_nt(dvn, S0)
    th, tl = _split(Tt)

    def tt_times(x):
        xh, xl = _split(x)
        return _mxu(th, xh, "nn") + (_mxu(th, xl, "nn") + _mxu(tl, xh, "nn"))

    dru = tt_times(dvn)
    drw = tt_times(dw)
    dA = -(_dotf_nt(dru, u) + _dotf_nt(drw, w))
    dAm = jnp.where(strict, dA, 0.0)
    dKK = dAm * Dm
    dkb = _dot(dKK, k)
    dk = dk + _dot_tn(dKK, kb)
    dD = dD + dAm * KK
    dv = dru * beta
    dbeta = jnp.sum(dru * v, -1, keepdims=True)
    dkb = dkb + drw * eg
    deg = deg + jnp.sum(drw * kb, -1, keepdims=True)
    dk = dk + dkb * beta
    dbeta = dbeta + jnp.sum(dkb * k, -1, keepdims=True)
    E = dD * Dm
    dgc = jnp.sum(E, -1, keepdims=True) - jnp.sum(jnp.swapaxes(E, -1, -2), -1, keepdims=True)
    dgc = dgc + deg * eg - der * er
    dgl = total(der * er) + d_el * el
    dgc = dgc + jnp.where(_iota((C, 1), 0) == C - 1, dgl, 0.0)
    triu = jnp.where(m["row"] <= m["col"], 1.0, 0.0)
    dg = _dot01(triu, jnp.broadcast_to(dgc, dgc.shape[:-1] + (LANES,)))[..., :1]
    dq = dqs * (D_HEAD ** -0.5)
    return dq, dk, dv, dg, dbeta, dS0


def _head_cols(gb, h):
    lane = _iota(gb.shape, 1)
    beta = jnp.sum(jnp.where(lane == h, gb, 0.0), -1, keepdims=True)
    g = jnp.sum(jnp.where(lane == N_HEADS + h, gb, 0.0), -1, keepdims=True)
    return g, beta


def _delta_fwd(q, k, v, gb):
    S = q.shape[0]
    C = DELTA_CHUNK
    N = S // C

    HB = DELTA_HEADS_PER_STEP

    def body(q_ref, k_ref, v_ref, gb_ref, o_ref, st_ref, t_ref, u_ref, w_ref, s_scr):
        n, hb = pl.program_id(0), pl.program_id(1)
        gb = gb_ref[...]

        @pl.when(n == 0)
        def _():
            for hh in range(HB):
                s_scr[hb * HB + hh] = jnp.zeros((D_HEAD, D_HEAD), F32)

        heads = [hb * HB + hh for hh in range(HB)]
        cols = [slice(hh * D_HEAD, (hh + 1) * D_HEAD) for hh in range(HB)]
        per_head = lambda ref: jnp.stack([ref[:, c] for c in cols])
        g, beta = (jnp.stack(t) for t in zip(*[_head_cols(gb, h) for h in heads]))
        S0 = jnp.stack([s_scr[h] for h in heads])
        o, S1, Tt, u, w = _delta_chunk_fwd(S0, per_head(q_ref), per_head(k_ref), per_head(v_ref), g, beta)
        for hh in range(HB):
            st_ref[hh, 0] = S0[hh]
            t_ref[hh, 0] = Tt[hh]
            o_ref[:, cols[hh]] = o[hh]
            u_ref[:, cols[hh]] = u[hh]
            w_ref[:, cols[hh]] = w[hh]
            s_scr[heads[hh]] = S1[hh]

    hd = pl.BlockSpec((C, HB * D_HEAD), lambda n, h: (n, h))
    mat = pl.BlockSpec((HB, 1, D_HEAD, D_HEAD), lambda n, h: (h, n, 0, 0))
    est = 40 * HB * _nbytes((C, D_HEAD), F32)
    seq = jax.ShapeDtypeStruct((S, N_HEADS * D_HEAD), F32)
    return pl.pallas_call(
        body, name="delta_fwd", grid=(N, N_HEADS // HB),
        in_specs=[hd, hd, hd, pl.BlockSpec((C, LANES), lambda n, h: (n, 0))],
        out_specs=[hd, mat, mat, hd, hd],
        out_shape=[seq, jax.ShapeDtypeStruct((N_HEADS, N, D_HEAD, D_HEAD), F32),
                   jax.ShapeDtypeStruct((N_HEADS, N, C, C), F32), seq, seq],
        scratch_shapes=[pltpu.VMEM((N_HEADS, D_HEAD, D_HEAD), F32)],
        compiler_params=_cparams(est, ("arbitrary", "arbitrary")),
    )(q, k, v, gb)


def _delta_bwd(q, k, v, gb, st, tinv, u, w, do):
    S = q.shape[0]
    C = DELTA_CHUNK
    N = S // C

    HB = DELTA_HEADS_PER_STEP

    def body(q_ref, k_ref, v_ref, gb_ref, st_ref, t_ref, u_ref, w_ref, do_ref, dq_ref, dk_ref, dv_ref, dgb_ref, ds_scr):
        n, hb = pl.program_id(0), pl.program_id(1)
        gb = gb_ref[...]
        lane = _iota((C, LANES), 1)
        dgb = jnp.zeros((C, LANES), F32)

        @pl.when(n == 0)
        def _():
            for hh in range(HB):
                ds_scr[hb * HB + hh] = jnp.zeros((D_HEAD, D_HEAD), F32)

        heads = [hb * HB + hh for hh in range(HB)]
        cols = [slice(hh * D_HEAD, (hh + 1) * D_HEAD) for hh in range(HB)]
        per_head = lambda ref: jnp.stack([ref[:, c] for c in cols])
        g, beta = (jnp.stack(t) for t in zip(*[_head_cols(gb, h) for h in heads]))
        dS1 = jnp.stack([ds_scr[h] for h in heads])
        dq, dk, dv, dg, dbeta, dS0 = _delta_chunk_bwd(
            st_ref[:, 0], per_head(q_ref), per_head(k_ref), per_head(v_ref), g, beta, t_ref[:, 0],
            per_head(u_ref), per_head(w_ref), per_head(do_ref), dS1)
        for hh, h in enumerate(heads):
            dq_ref[:, cols[hh]] = dq[hh]
            dk_ref[:, cols[hh]] = dk[hh]
            dv_ref[:, cols[hh]] = dv[hh]
            dgb = dgb + jnp.where(lane == h, dbeta[hh], 0.0) + jnp.where(lane == N_HEADS + h, dg[hh], 0.0)
            ds_scr[h] = dS0[hh]

        @pl.when(hb == 0)
        def _():
            dgb_ref[...] = dgb

        @pl.when(hb > 0)
        def _():
            dgb_ref[...] += dgb

    hd = pl.BlockSpec((C, HB * D_HEAD), lambda n, h: (N - 1 - n, h))
    mat = pl.BlockSpec((HB, 1, D_HEAD, D_HEAD), lambda n, h: (h, N - 1 - n, 0, 0))
    gbs = pl.BlockSpec((C, LANES), lambda n, h: (N - 1 - n, 0))
    est = 60 * HB * _nbytes((C, D_HEAD), F32)
    return pl.pallas_call(
        body, name="delta_bwd", grid=(N, N_HEADS // HB),
        in_specs=[hd, hd, hd, gbs, mat, mat, hd, hd, hd],
        out_specs=[hd, hd, hd, gbs],
        out_shape=[jax.ShapeDtypeStruct((S, N_HEADS * D_HEAD), F32)] * 3 + [jax.ShapeDtypeStruct((S, LANES), F32)],
        scratch_shapes=[pltpu.VMEM((N_HEADS, D_HEAD, D_HEAD), F32)],
        compiler_params=_cparams(est, ("arbitrary", "arbitrary")),
    )(q, k, v, gb, st, tinv, u, w, do)


def _ya_head(o, z, onw):
    return o * lax.rsqrt(jnp.mean(o * o, -1, keepdims=True) + RMS_EPS) * onw * _silu(z)


def _sgu_pre(u, vg, sg, sb):
    return _gelu(u), _ln(_gelu(vg), sg, sb)


def _chunk_causal(shape, di, dj):
    sh = jnp.int32(int(math.log2(SGU_CHUNK)))
    return lax.shift_right_logical(_iota(shape, di), sh) >= lax.shift_right_logical(_iota(shape, dj), sh)


def _ws_masked(ws):
    return jnp.where(_chunk_causal(ws.shape, 1, 2), ws, 0.0)


def _mix_prep(o, proj, onw, sg, sb, ws, bst, *, tm=256):
    S = o.shape[0]
    tm = _tile(S, tm, SGU_BLOCK)

    def body(o_ref, z_ref, u_ref, vg_ref, onw_ref, sg_ref, sb_ref, ws_ref, bst_ref, ya_ref, yb_ref):
        onw = onw_ref[...]
        for h in range(N_HEADS):
            sl = slice(h * D_HEAD, (h + 1) * D_HEAD)
            ya_ref[:, sl] = _ya_head(o_ref[:, sl], z_ref[:, sl].astype(F32), onw).astype(BF)
        ua, vl = _sgu_pre(u_ref[...].astype(F32), vg_ref[...].astype(F32), sg_ref[...], sb_ref[...])
        wsm = _ws_masked(ws_ref[...])
        bst = bst_ref[...]
        for blk in range(tm // SGU_BLOCK):
            rs = slice(blk * SGU_BLOCK, (blk + 1) * SGU_BLOCK)
            for gi in range(SGU_GROUPS):
                cs = slice(gi * D_HEAD, (gi + 1) * D_HEAD)
                sp = _dot(wsm[gi], vl[rs, cs]) + bst[:, gi:gi + 1]
                yb_ref[rs, cs] = (ua[rs, cs] * sp).astype(BF)

    blk = lambda col: pl.BlockSpec((tm, D_MODEL), lambda i: (i, col))
    full = lambda shape: pl.BlockSpec(shape, lambda i: (0,) * len(shape))
    est = 10 * _nbytes((tm, D_MODEL), F32)
    return pl.pallas_call(
        body, name="mix_prep", grid=(S // tm,),
        in_specs=[blk(0), blk(0), blk(1), blk(2), full((1, D_HEAD)), full((1, D_MODEL)), full((1, D_MODEL)),
                  full((SGU_GROUPS, SGU_BLOCK, SGU_BLOCK)), full((SGU_BLOCK, LANES))],
        out_specs=[blk(0), blk(0)],
        out_shape=[jax.ShapeDtypeStruct((S, D_MODEL), BF)] * 2,
        compiler_params=_cparams(est, ("parallel",)),
    )(o, proj, proj, proj, onw, sg, sb, ws, bst)


def _mix_prep_bwd(o, proj, onw, sg, sb, ws, bst, dya, dyb, dproj, *, tm=256):
    S = o.shape[0]
    tm = _tile(S, tm, SGU_BLOCK)

    def body(o_ref, z_ref, u_ref, vg_ref, onw_ref, sg_ref, sb_ref, ws_ref, bst_ref, dya_ref, dyb_ref, dproj_in,
             do_ref, dzuv_ref, donw_ref, dsg_ref, dsb_ref, dws_ref, dbst_ref, dvl_scr, dua_scr):
        dz_ref, du_ref, dvg_ref = (dzuv_ref.at[:, k * D_MODEL:(k + 1) * D_MODEL] for k in range(3))
        @pl.when(pl.program_id(0) == 0)
        def _():
            for r in (donw_ref, dsg_ref, dsb_ref, dws_ref, dbst_ref):
                r[...] = jnp.zeros_like(r)

        onw = onw_ref[...]
        donw = jnp.zeros((1, D_HEAD), F32)
        for h in range(N_HEADS):
            sl = slice(h * D_HEAD, (h + 1) * D_HEAD)
            _, vj = jax.vjp(_ya_head, o_ref[:, sl], z_ref[:, sl].astype(F32), onw)
            do_h, dz_h, donw_h = vj(dya_ref[:, sl])
            do_ref[:, sl] = do_h.astype(BF)
            dz_ref[:, sl] = dz_h.astype(BF)
            donw = donw + donw_h
        donw_ref[...] += _bcast_rows(donw)

        (ua, vl), vj = jax.vjp(_sgu_pre, u_ref[...].astype(F32), vg_ref[...].astype(F32), sg_ref[...], sb_ref[...])
        wsm = _ws_masked(ws_ref[...])
        bst = bst_ref[...]
        lane = _iota((SGU_BLOCK, LANES), 1)
        dbst = jnp.zeros((SGU_BLOCK, LANES), F32)
        cmask = _chunk_causal((SGU_BLOCK, SGU_BLOCK), 0, 1)
        for gi in range(SGU_GROUPS):
            cs = slice(gi * D_HEAD, (gi + 1) * D_HEAD)
            wg = wsm[gi]
            wgt = jnp.transpose(wg)
            dwg = jnp.zeros((SGU_BLOCK, SGU_BLOCK), F32)
            for blk in range(tm // SGU_BLOCK):
                rs = slice(blk * SGU_BLOCK, (blk + 1) * SGU_BLOCK)
                sp = _dot(wg, vl[rs, cs]) + bst[:, gi:gi + 1]
                dyb = dyb_ref[rs, cs]
                dsp = dyb * ua[rs, cs]
                dua_scr[rs, cs] = dyb * sp
                dvl_scr[rs, cs] = _dot(wgt, dsp)
                dwg = dwg + _dot_nt(dsp, vl[rs, cs])
                dbst = dbst + jnp.where(lane == gi, jnp.sum(dsp, -1, keepdims=True), 0.0)
            dws_ref[gi] += jnp.where(cmask, dwg, 0.0)
        dbst_ref[...] += dbst
        du, dvg, dsg, dsb = vj((dua_scr[...], dvl_scr[...]))
        du_ref[...] = du.astype(BF)
        dvg_ref[...] = dvg.astype(BF)
        dsg_ref[...] += _bcast_rows(dsg)
        dsb_ref[...] += _bcast_rows(dsb)

    blk = lambda col: pl.BlockSpec((tm, D_MODEL), lambda i: (i, col))
    full = lambda shape: pl.BlockSpec(shape, lambda i: (0,) * len(shape))
    est = 16 * _nbytes((tm, D_MODEL), F32)
    outs = pl.pallas_call(
        body, name="mix_prep_bwd", grid=(S // tm,),
        in_specs=[blk(0), blk(0), blk(1), blk(2), full((1, D_HEAD)), full((1, D_MODEL)), full((1, D_MODEL)),
                  full((SGU_GROUPS, SGU_BLOCK, SGU_BLOCK)), full((SGU_BLOCK, LANES)), blk(0), blk(0),
                  pl.BlockSpec(memory_space=pl.ANY)],
        out_specs=[blk(0), pl.BlockSpec((tm, 3 * D_MODEL), lambda i: (i, 1)),
                   full((SUBLANES, D_HEAD)), full((SUBLANES, D_MODEL)), full((SUBLANES, D_MODEL)),
                   full((SGU_GROUPS, SGU_BLOCK, SGU_BLOCK)), full((SGU_BLOCK, LANES))],
        out_shape=[jax.ShapeDtypeStruct((S, D_MODEL), BF), jax.ShapeDtypeStruct(dproj.shape, BF),
                   jax.ShapeDtypeStruct((SUBLANES, D_HEAD), F32), jax.ShapeDtypeStruct((SUBLANES, D_MODEL), F32),
                   jax.ShapeDtypeStruct((SUBLANES, D_MODEL), F32),
                   jax.ShapeDtypeStruct((SGU_GROUPS, SGU_BLOCK, SGU_BLOCK), F32),
                   jax.ShapeDtypeStruct((SGU_BLOCK, LANES), F32)],
        input_output_aliases={11: 1},
        scratch_shapes=[pltpu.VMEM((tm, D_MODEL), F32)] * 2,
        compiler_params=_cparams(est, ("arbitrary",)),
    )(o, proj, proj, proj, onw, sg, sb, ws, bst, dya, dyb, dproj)
    return outs


def _mm_gate_merge(ya, yb, wpa, wpb, proj, *, tm=512):
    S = ya.shape[0]
    tm = _tile(S, tm, SUBLANES * 2)

    def body(ya_ref, yb_ref, wa_ref, wb_ref, ga_ref, gb_ref, pa_ref, pb_ref, m_ref):
        pa = _dot(ya_ref[...], wa_ref[...]).astype(BF)
        pb = _dot(yb_ref[...], wb_ref[...]).astype(BF)
        pa_ref[...] = pa
        pb_ref[...] = pb
        m_ref[...] = (_sigmoid(ga_ref[...].astype(F32)) * pa.astype(F32)
                      + _sigmoid(gb_ref[...].astype(F32)) * pb.astype(F32)).astype(BF)

    blk = lambda col: pl.BlockSpec((tm, D_MODEL), lambda i: (i, col))
    wsp = pl.BlockSpec((D_MODEL, D_MODEL), lambda i: (0, 0))
    return pl.pallas_call(
        body, name="mm_gate_merge", grid=(S // tm,),
        in_specs=[blk(0), blk(0), wsp, wsp, blk(3), blk(4)], out_specs=[blk(0)] * 3,
        out_shape=[jax.ShapeDtypeStruct((S, D_MODEL), BF)] * 3,
        compiler_params=_cparams(2 * _nbytes((D_MODEL, D_MODEL), BF) + 8 * _nbytes((tm, D_MODEL), F32), ("parallel",)),
    )(ya, yb, wpa, wpb, proj, proj)


def _mm_gate_merge_bwd(dmix, wo, pa, pb, proj, *, tm=512):
    S = pa.shape[0]
    tm = _tile(S, tm, SUBLANES * 2)

    def body(d_ref, w_ref, pa_ref, pb_ref, ga_ref, gb_ref, dpa_ref, dpb_ref, dg_ref):
        dm = _dot_nt(d_ref[...], w_ref[...])
        sa, sb = _sigmoid(ga_ref[...].astype(F32)), _sigmoid(gb_ref[...].astype(F32))
        dpa_ref[...] = (dm * sa).astype(BF)
        dpb_ref[...] = (dm * sb).astype(BF)
        dg_ref[:, :D_MODEL] = (dm * pa_ref[...].astype(F32) * sa * (1.0 - sa)).astype(BF)
        dg_ref[:, D_MODEL:] = (dm * pb_ref[...].astype(F32) * sb * (1.0 - sb)).astype(BF)

    blk = lambda col: pl.BlockSpec((tm, D_MODEL), lambda i: (i, col))
    est = _nbytes((D_MODEL, D_MODEL), BF) + 10 * _nbytes((tm, D_MODEL), F32)
    return pl.pallas_call(
        body, name="mm_gate_merge_bwd", grid=(S // tm,),
        in_specs=[blk(0), pl.BlockSpec((D_MODEL, D_MODEL), lambda i: (0, 0)), blk(0), blk(0), blk(3), blk(4)],
        out_specs=[blk(0), blk(0), pl.BlockSpec((tm, 2 * D_MODEL), lambda i: (i, 3))],
        out_shape=[jax.ShapeDtypeStruct((S, D_MODEL), BF)] * 2 + [jax.ShapeDtypeStruct((S, 8 * D_MODEL), BF)],
        compiler_params=_cparams(est, ("parallel",)),
    )(dmix, wo, pa, pb, proj, proj)


def _mm_swiglu(xb, wg, wu, *, tm=1024, tn=768):
    S, K = xb.shape
    tm = _tile(S, tm, SUBLANES * 2)
    tn = _tile(FFN_K, tn, LANES)

    def body(x_ref, wg_ref, wu_ref, hg_ref, hu_ref, h_ref):
        x = x_ref[...]
        hg = _dot(x, wg_ref[...]).astype(BF)
        hu = _dot(x, wu_ref[...]).astype(BF)
        hg_ref[...] = hg
        hu_ref[...] = hu
        h_ref[...] = (_silu(hg.astype(F32)) * hu.astype(F32)).astype(BF)

    out = pl.BlockSpec((tm, tn), lambda i, j: (i, j))
    est = _nbytes((tm, K), BF) + 2 * _nbytes((K, tn), BF) + 6 * _nbytes((tm, tn), F32)
    return pl.pallas_call(
        body, name="mm_swiglu", grid=(S // tm, FFN_K // tn),
        in_specs=[pl.BlockSpec((tm, K), lambda i, j: (i, 0)), pl.BlockSpec((K, tn), lambda i, j: (0, j)),
                  pl.BlockSpec((K, tn), lambda i, j: (0, j))],
        out_specs=[out] * 3, out_shape=[jax.ShapeDtypeStruct((S, FFN_K), BF)] * 3,
        compiler_params=_cparams(est, ("parallel", "parallel")),
    )(xb, wg, wu)


def _mm_swiglu_bwd(dffn, wd, hg, hu, *, tm=1024, tn=768):
    S, K = dffn.shape
    tm = _tile(S, tm, SUBLANES * 2)
    tn = _tile(FFN_K, tn, LANES)

    def body(d_ref, w_ref, hg_ref, hu_ref, dhg_ref, dhu_ref):
        dh = _dot_nt(d_ref[...], w_ref[...])
        act, dact = _silu_and_grad(hg_ref[...].astype(F32))
        dhg_ref[...] = (dh * hu_ref[...].astype(F32) * dact).astype(BF)
        dhu_ref[...] = (dh * act).astype(BF)

    out = pl.BlockSpec((tm, tn), lambda i, j: (i, j))
    est = _nbytes((tm, K), dffn.dtype) + _nbytes((tn, K), BF) + 8 * _nbytes((tm, tn), F32)
    return pl.pallas_call(
        body, name="mm_swiglu_bwd", grid=(S // tm, FFN_K // tn),
        in_specs=[pl.BlockSpec((tm, K), lambda i, j: (i, 0)), pl.BlockSpec((tn, K), lambda i, j: (j, 0)), out, out],
        out_specs=[out, out], out_shape=[jax.ShapeDtypeStruct((S, FFN_K), BF)] * 2,
        compiler_params=_cparams(est, ("parallel", "parallel")),
    )(dffn, wd, hg, hu)


def _mm_resid_ln(a, bmat, x, g, b, *, name, tm=512):
    S, K = a.shape
    tm = _tile(S, tm, SUBLANES * 2)

    def body(a_ref, w_ref, x_ref, g_ref, b_ref, pre_ref, y_ref, yb_ref):
        pre = ALPHA * x_ref[...] + _dot(a_ref[...], w_ref[...])
        y = _ln(pre, g_ref[...], b_ref[...])
        pre_ref[...] = pre
        y_ref[...] = y
        yb_ref[...] = y.astype(BF)

    blk = pl.BlockSpec((tm, D_MODEL), lambda i: (i, 0))
    vec = pl.BlockSpec((1, D_MODEL), lambda i: (0, 0))
    est = _nbytes((tm, K), BF) + _nbytes((K, D_MODEL), BF) + 8 * _nbytes((tm, D_MODEL), F32)
    return pl.pallas_call(
        body, name=name, grid=(S // tm,),
        in_specs=[pl.BlockSpec((tm, K), lambda i: (i, 0)), pl.BlockSpec((K, D_MODEL), lambda i: (0, 0)), blk, vec, vec],
        out_specs=[blk, blk, blk],
        out_shape=[jax.ShapeDtypeStruct((S, D_MODEL), F32)] * 2 + [jax.ShapeDtypeStruct((S, D_MODEL), BF)],
        compiler_params=_cparams(est, ("parallel",)),
    )(a, bmat, x, g, b)


def _ln_bwd(pre, g, b, dy, *, tm=512):
    S = pre.shape[0]
    tm = _tile(S, tm, SUBLANES)

    def body(p_ref, g_ref, b_ref, dy_ref, dp_ref, dg_ref, db_ref):
        @pl.when(pl.program_id(0) == 0)
        def _():
            dg_ref[...] = jnp.zeros_like(dg_ref)
            db_ref[...] = jnp.zeros_like(db_ref)

        _, vj = jax.vjp(_ln, p_ref[...], g_ref[...], b_ref[...])
        dp, dg, db = vj(dy_ref[...])
        dp_ref[...] = dp
        dg_ref[...] += _bcast_rows(dg)
        db_ref[...] += _bcast_rows(db)

    blk = pl.BlockSpec((tm, D_MODEL), lambda i: (i, 0))
    vec = pl.BlockSpec((1, D_MODEL), lambda i: (0, 0))
    acc = pl.BlockSpec((SUBLANES, D_MODEL), lambda i: (0, 0))
    return pl.pallas_call(
        body, name="ln_bwd", grid=(S // tm,),
        in_specs=[blk, vec, vec, blk], out_specs=[blk, acc, acc],
        out_shape=[jax.ShapeDtypeStruct((S, D_MODEL), F32)] + [jax.ShapeDtypeStruct((SUBLANES, D_MODEL), F32)] * 2,
        compiler_params=_cparams(10 * _nbytes((tm, D_MODEL), F32), ("arbitrary",)),
    )(pre, g, b, dy)


def _loss_ln_bwd(y, tgt, pre, g, b, *, tm=512):
    S = y.shape[0]
    tm = _tile(S, tm, SUBLANES)

    def body(y_ref, t_ref, p_ref, g_ref, b_ref, dp_ref, dg_ref, db_ref, l_ref):
        @pl.when(pl.program_id(0) == 0)
        def _():
            for r in (dg_ref, db_ref, l_ref):
                r[...] = jnp.zeros_like(r)

        e = y_ref[...] - t_ref[...]
        l_ref[...] += 0.5 * jnp.sum(jnp.mean(e * e, -1, keepdims=True), keepdims=True)
        _, vj = jax.vjp(_ln, p_ref[...], g_ref[...], b_ref[...])
        dp, dg, db = vj(e * (1.0 / D_MODEL))
        dp_ref[...] = dp
        dg_ref[...] += _bcast_rows(dg)
        db_ref[...] += _bcast_rows(db)

    blk = pl.BlockSpec((tm, D_MODEL), lambda i: (i, 0))
    vec = pl.BlockSpec((1, D_MODEL), lambda i: (0, 0))
    acc = pl.BlockSpec((SUBLANES, D_MODEL), lambda i: (0, 0))
    return pl.pallas_call(
        body, name="loss_ln_bwd", grid=(S // tm,),
        in_specs=[blk, blk, blk, vec, vec], out_specs=[blk, acc, acc, pl.BlockSpec((SUBLANES, LANES), lambda i: (0, 0))],
        out_shape=[jax.ShapeDtypeStruct((S, D_MODEL), F32)] + [jax.ShapeDtypeStruct((SUBLANES, D_MODEL), F32)] * 2
                  + [jax.ShapeDtypeStruct((SUBLANES, LANES), F32)],
        compiler_params=_cparams(12 * _nbytes((tm, D_MODEL), F32), ("arbitrary",)),
    )(y, tgt, pre, g, b)


def _layer_fwd(x, xb, w, late):
    pq = _mm(xb, w["win"], mode="nn", name="mm_in_qkv", tm=1024, tn=1024, cols=(0, 3 * D_MODEL))
    proj = _mm(xb, w["win"], mode="nn", name="mm_in_rest", tm=1024, tn=1024, cols=(3 * D_MODEL, 5 * D_MODEL), out_dtype=BF)
    ba = _mm(xb, w["wba"], mode="nn", name="mm_in_ba", tm=1024, tn=LANES)
    qn, kn, vv, gb = _qkv_prep(pq, ba, w["convw"], w["arow"], w["dtrow"])
    o, st, tinv, wy_u, wy_w = _delta_fwd(qn, kn, vv, gb)
    ya, yb = _mix_prep(o, proj, w["onw"], w["sg"], w["sb"], w["ws"], w["bst"])
    w = {**w, **late(ya)}
    pa, pb, m = _mm_gate_merge(ya, yb, w["wpa"], w["wpb"], proj)
    pre1, x1, x1b = _mm_resid_ln(m, w["wo"], x, w["ln1g"], w["ln1b"], name="mm_out_ln")
    hg, hu, h = _mm_swiglu(x1b, w["wg"], w["wu"])
    pre2, x2, x2b = _mm_resid_ln(h, w["wd"], x1, w["ln2g"], w["ln2b"], name="mm_down_ln")
    saved = dict(xb=xb, pq=pq, proj=proj, ba=ba, qn=qn, kn=kn, vv=vv, gb=gb, o=o, st=st, tinv=tinv, wy_u=wy_u, wy_w=wy_w,
                 ya=ya, yb=yb,
                 pa=pa, pb=pb, m=m, pre1=pre1, x1b=x1b, hg=hg, hu=hu, h=h, pre2=pre2)
    return x2, x2b, saved, w


def _layer_bwd(dpre2, ln2_grads, w, s, on_part=None):
    g = {}
    started = lambda part: on_part(part, g) if on_part is not None else None
    after = lambda v, token: v if token is None else v + token.astype(v.dtype)
    g["ln2g"], g["ln2b"] = ln2_grads
    dhg, dhu = _mm_swiglu_bwd(dpre2, w["wd"], s["hg"], s["hu"])
    g["wd"] = _mm(s["h"], dpre2, mode="tn", name="mm_tn_down", tm=1536, tk=1024, out_dtype=BF)
    dx1 = _mm(dhg, w["wg"], mode="nt", name="mm_nt_gu", pair=(dhu, w["wu"]), add=dpre2, add_scale=ALPHA, tm=1024, tk=1536)
    g["wg"] = _mm(s["x1b"], dhg, mode="tn", name="mm_tn_gu", tm=1024, tn=1536, tk=2048, out_dtype=BF)
    g["wu"] = _mm(s["x1b"], dhu, mode="tn", name="mm_tn_gu", tm=1024, tn=1536, tk=2048, out_dtype=BF)
    dpre1, g["ln1g"], g["ln1b"] = _ln_bwd(s["pre1"], w["ln1g"], w["ln1b"], dx1)
    g["wo"] = _mm(s["m"], dpre1, mode="tn", name="mm_tn_sq", tm=1024, tk=1024, out_dtype=BF)
    dpa, dpb, dproj = _mm_gate_merge_bwd(dpre1, w["wo"], s["pa"], s["pb"], s["proj"])
    dya = _mm(dpa, w["wpa"], mode="nt", name="mm_nt_sq")
    g["wpa"] = _mm(s["ya"], dpa, mode="tn", name="mm_tn_sq", tm=1024, tk=1024, out_dtype=BF)
    dyb = _mm(dpb, w["wpb"], mode="nt", name="mm_nt_sq")
    g["wpb"] = _mm(s["yb"], dpb, mode="tn", name="mm_tn_sq", tm=1024, tk=1024, out_dtype=BF)
    do, dproj, g["onw"], g["sg"], g["sb"], g["ws"], g["bst"] = _mix_prep_bwd(
        s["o"], s["proj"], after(w["onw"], started("late")), w["sg"], w["sb"], w["ws"], w["bst"], dya, dyb, dproj)
    dqn, dkn, dvv, dgb = _delta_bwd(s["qn"], s["kn"], s["vv"], s["gb"], s["st"], s["tinv"], s["wy_u"], s["wy_w"], do)
    dc, dba, g["convw"], g["arow"], g["dtrow"] = _qkv_prep_bwd(
        s["pq"], s["ba"], w["convw"], w["arow"], w["dtrow"], dqn, dkn, dvv, dgb)
    dproj = _conv_bwd(dc, w["convw"], dproj)
    g["win"] = _mm(s["xb"], dproj, mode="tn", name="mm_tn_in", tm=1024, tn=1024, tk=2048, out_dtype=BF)
    g["wba"] = _mm(s["xb"], dba, mode="tn", name="mm_tn_ba", tm=1024, tn=LANES, tk=1024, out_dtype=BF)
    dx = _mm(dba, after(w["wba"], started("early")), mode="nt", name="mm_nt_ba", add=dpre1, add_scale=ALPHA, tm=1024)
    dx = _mm(dproj, w["win"], mode="nt", name="mm_nt_in", add=dx, add_scale=1.0, tm=1024, tk=2048)
    return dx, g


def _local_step(x, tgt, layers, on_grads=None):
    saved, weights = [], []
    xb = x.astype(BF)
    for layer in layers:
        x, xb, s, w = _layer_fwd(x, xb, *layer(x))
        saved.append(s)
        weights.append(w)
    last = len(layers) - 1
    dpre2, dg, db, lacc = _loss_ln_bwd(x, tgt, saved[last]["pre2"], weights[last]["ln2g"], weights[last]["ln2b"])
    grads = [None] * len(layers)
    for l in reversed(range(len(layers))):
        on_part = functools.partial(on_grads, l) if on_grads is not None else None
        dx, grads[l] = _layer_bwd(dpre2, (dg, db), weights[l], saved[l], on_part)
        if l > 0:
            dpre2, dg, db = _ln_bwd(saved[l - 1]["pre2"], weights[l - 1]["ln2g"], weights[l - 1]["ln2b"], dx)
    return lacc[0, 0], dx, grads


_QKVZ = 4 * D_MODEL
_BA = 2 * N_HEADS


WEIGHT_NAMES = ("w_in", "conv_w", "a_log", "dt_bias", "o_norm_w", "sgu_ln_g", "sgu_ln_b", "w_s", "b_s", "w_pa", "w_pb",
                "w_o", "ln1_g", "ln1_b", "w_ffn_gate", "w_ffn_up", "w_ffn_down", "ln2_g", "ln2_b")
WIRE = ("w_in", "w_ffn_gate", "w_ffn_up", "w_ffn_down", "w_pa", "w_pb", "w_o", "conv_w")
SMALL = (("a_log", N_HEADS), ("dt_bias", N_HEADS), ("o_norm_w", D_HEAD), ("sgu_ln_g", D_MODEL), ("sgu_ln_b", D_MODEL),
         ("w_s", SGU_GROUPS * SGU_BLOCK * SGU_BLOCK), ("b_s", SGU_GROUPS * SGU_BLOCK),
         ("ln1_g", D_MODEL), ("ln1_b", D_MODEL), ("ln2_g", D_MODEL), ("ln2_b", D_MODEL))
SMALL_ROWS = -(-sum(n for _, n in SMALL) // (LANES * SUBLANES)) * SUBLANES
N_MAIN_TILES = (N_IN - _BA) // D_MODEL
ADAM_TILES = dict(w_in=(128, "adamw_in"), w_ffn_gate=(256, "adamw_ffn_cols"), w_ffn_up=(256, "adamw_ffn_cols"),
                  w_ffn_down=(32, "adamw_ffn_rows"), w_pa=(128, "adamw_sq"), w_pb=(128, "adamw_sq"), w_o=(128, "adamw_sq"),
                  conv_w=(CONV_K, "adamw_conv"))


def _pad_to(a, axis, size):
    pads = [(0, 0)] * a.ndim
    pads[axis] = (0, size - a.shape[axis])
    return jnp.pad(a, pads)


def _wire_blocks(p):
    return dict(
        w_in=_pad_to(p["w_in"].astype(BF), 2, IN_PAD),
        w_ffn_gate=_pad_to(p["w_ffn_gate"].astype(BF), 2, FFN_PAD), w_ffn_up=_pad_to(p["w_ffn_up"].astype(BF), 2, FFN_PAD),
        w_ffn_down=_pad_to(p["w_ffn_down"].astype(BF), 1, FFN_PAD),
        w_pa=p["w_pa"].astype(BF), w_pb=p["w_pb"].astype(BF), w_o=p["w_o"].astype(BF),
        conv_w=_pad_to(p["conv_w"], 1, SUBLANES),
    )


def _by_columns(blocks):
    n, r, c = blocks.shape
    return jnp.transpose(blocks, (1, 0, 2)).reshape(r, n * c)


def _to_slots(full, c):
    r = full.shape[0]
    return jnp.transpose(full.reshape(r, N_DEV, c), (1, 0, 2))


def _lane_row(v, at):
    return jnp.pad(v[None], ((0, 0), (at, LANES - at - v.shape[0])))


UNALIGNED = ("w_in", "w_ffn_gate", "w_ffn_up")
EARLY = ("w_in", "conv_w")
LATE = ("w_pa", "w_pb", "w_o", "w_ffn_gate", "w_ffn_up", "w_ffn_down")


def _early_weights(stacks, p, l):
    return dict(
        win=_perm_in(stacks["w_in"], D_MODEL, N_MAIN_TILES), wba=_perm_in(stacks["w_in"], LANES, 1),
        convw=_by_columns(stacks["conv_w"][:, :CONV_K]),
        arow=_lane_row(p["a_log"][l], N_HEADS), dtrow=_lane_row(p["dt_bias"][l], N_HEADS),
        onw=p["o_norm_w"][l][None], sg=p["sgu_ln_g"][l][None], sb=p["sgu_ln_b"][l][None],
        ws=p["w_s"][l], bst=_pad_to(p["b_s"][l].T, 1, LANES),
        ln1g=p["ln1_g"][l][None], ln1b=p["ln1_b"][l][None], ln2g=p["ln2_g"][l][None], ln2b=p["ln2_b"][l][None],
    )


def _late_weights(stacks):
    return dict(
        wpa=stacks["w_pa"].reshape(D_MODEL, D_MODEL), wpb=stacks["w_pb"].reshape(D_MODEL, D_MODEL),
        wo=stacks["w_o"].reshape(D_MODEL, D_MODEL),
        wg=_by_columns(stacks["w_ffn_gate"]), wu=_by_columns(stacks["w_ffn_up"]),
        wd=stacks["w_ffn_down"].reshape(FFN_K, D_MODEL),
    )


def _small_pack(parts):
    flat = jnp.concatenate([parts[n].reshape(-1) for n, _ in SMALL])
    return _pad_to(flat, 0, SMALL_ROWS * LANES).reshape(SMALL_ROWS, LANES)


def _small_unpack(rows, like):
    flat, out, off = rows.reshape(-1), {}, 0
    for n, size in SMALL:
        out[n] = flat[off:off + size].reshape(like[n].shape[1:])
        off += size
    return out


def _late_slots(g):
    slots = dict(
        w_ffn_gate=_to_slots(g["wg"], FFN_PAD), w_ffn_up=_to_slots(g["wu"], FFN_PAD),
        w_ffn_down=g["wd"].reshape(N_DEV, FFN_PAD, D_MODEL),
        w_pa=g["wpa"].reshape(N_DEV, D_MODEL // N_DEV, D_MODEL), w_pb=g["wpb"].reshape(N_DEV, D_MODEL // N_DEV, D_MODEL),
        w_o=g["wo"].reshape(N_DEV, D_MODEL // N_DEV, D_MODEL),
    )
    return [slots[n] for n in LATE]


def _early_slots(g):
    slots = [_perm_out(g["win"], g["wba"]), _pad_to(_to_slots(g["convw"][:CONV_K], 3 * D_MODEL // N_DEV), 1, SUBLANES)]
    small = _small_pack(dict(
        a_log=g["arow"][0, N_HEADS:2 * N_HEADS], dt_bias=g["dtrow"][0, N_HEADS:2 * N_HEADS], o_norm_w=g["onw"][0],
        sgu_ln_g=g["sg"][0], sgu_ln_b=g["sb"][0], w_s=g["ws"], b_s=g["bst"][:, :SGU_GROUPS].T,
        ln1_g=g["ln1g"][0], ln1_b=g["ln1b"][0], ln2_g=g["ln2g"][0], ln2_b=g["ln2b"][0]))
    return slots, small


def _in_tile_start(j, tile_w):
    if tile_w == LANES:
        return jnp.int32(_QKVZ)
    return j * D_MODEL + jnp.where(j >= _QKVZ // D_MODEL, _BA, 0)


def _select(rows_iota, cols_iota, dev, start, valid):
    hit = (rows_iota + (dev * IN_BLOCK - start) == cols_iota) & (rows_iota < IN_BLOCK) & (cols_iota < valid)
    return jnp.where(hit, 1.0, 0.0).astype(BF)


def _perm_in(stack, tile_w, n_tiles):
    valid = _BA if tile_w == LANES else tile_w

    def first_dev(j):
        return lax.div(_in_tile_start(j, tile_w), jnp.int32(IN_BLOCK))

    def body(w_ref, o_ref, acc_ref):
        j, k = pl.program_id(0), pl.program_id(1)
        sel = _select(_iota((IN_PAD, tile_w), 0), _iota((IN_PAD, tile_w), 1), first_dev(j) + k,
                      _in_tile_start(j, tile_w), valid)
        part = jnp.dot(w_ref[0], sel, preferred_element_type=F32)

        @pl.when(k == 0)
        def _():
            acc_ref[...] = part

        @pl.when(k == 1)
        def _():
            o_ref[...] = (acc_ref[...] + part).astype(BF)

    est = _nbytes((D_MODEL, IN_PAD), BF) + 3 * _nbytes((D_MODEL, tile_w), F32) + 2 * _nbytes((IN_PAD, tile_w), F32)
    return pl.pallas_call(
        body, name="perm_in" if tile_w != LANES else "perm_in_ba", grid=(n_tiles, 2),
        in_specs=[pl.BlockSpec((1, D_MODEL, IN_PAD), lambda j, k: (jnp.minimum(first_dev(j) + k, N_DEV - 1), 0, 0))],
        out_specs=pl.BlockSpec((D_MODEL, tile_w), lambda j, k: (0, j)),
        out_shape=jax.ShapeDtypeStruct((D_MODEL, n_tiles * tile_w), BF),
        scratch_shapes=[pltpu.VMEM((D_MODEL, tile_w), F32)],
        compiler_params=_cparams(est, ("parallel", "arbitrary")),
    )(stack)


def _perm_out(dmain, dba):
    def tile(d, s):
        c0 = d * IN_BLOCK
        first = lax.div(c0 - jnp.where(c0 < _QKVZ, 0, jnp.minimum(c0 - _QKVZ, _BA)), jnp.int32(D_MODEL))
        return jnp.minimum(first + jnp.minimum(s, 1), N_MAIN_TILES - 1)

    def body(dm_ref, db_ref, o_ref, acc_ref):
        d, s = pl.program_id(0), pl.program_id(1)

        @pl.when(s == 0)
        def _():
            acc_ref[...] = jnp.zeros_like(acc_ref)

        start = _in_tile_start(tile(d, s), D_MODEL)
        overlaps = (start < (d + 1) * IN_BLOCK) & (d * IN_BLOCK < start + D_MODEL)

        @pl.when((s < 2) & overlaps)
        def _():
            sel = _select(_iota((D_MODEL, IN_PAD), 1), _iota((D_MODEL, IN_PAD), 0), d, start, D_MODEL)
            acc_ref[...] += jnp.dot(dm_ref[...], sel, preferred_element_type=F32)

        @pl.when(s == 2)
        def _():
            sel = _select(_iota((LANES, IN_PAD), 1), _iota((LANES, IN_PAD), 0), d, jnp.int32(_QKVZ), _BA)
            o_ref[0] = (acc_ref[...] + jnp.dot(db_ref[...], sel, preferred_element_type=F32)).astype(BF)

    est = 2 * _nbytes((D_MODEL, D_MODEL), BF) + 4 * _nbytes((D_MODEL, IN_PAD), F32)
    return pl.pallas_call(
        body, name="perm_out", grid=(N_DEV, 3),
        in_specs=[pl.BlockSpec((D_MODEL, D_MODEL), lambda d, s: (0, tile(d, s))),
                  pl.BlockSpec((D_MODEL, LANES), lambda d, s: (0, 0))],
        out_specs=pl.BlockSpec((1, D_MODEL, IN_PAD), lambda d, t: (d, 0, 0)),
        out_shape=jax.ShapeDtypeStruct((N_DEV, D_MODEL, IN_PAD), BF),
        scratch_shapes=[pltpu.VMEM((D_MODEL, IN_PAD), F32)],
        compiler_params=_cparams(est, ("parallel", "arbitrary")),
    )(dmain, dba)


def _mesh_place():
    x, y, c = (lax.axis_index(a) for a in MESH_AXES)
    return x, y, c


def _slot(x, y, c):
    return 4 * x + 2 * y + c


def _peer(place, j):
    x, y, c = place
    return (1 - x if j & 4 else x, 1 - y if j & 2 else y, 1 - c if j & 1 else c)


_HBM = pl.BlockSpec(memory_space=pltpu.HBM)
_SEM = pl.BlockSpec(memory_space=pltpu.SEMAPHORE)
_EFFECT = pltpu.SideEffectType.DATAFLOW_SIDE_EFFECTING


def _remote_copy(src_ref, land_ref, slot, per_slot, pslot, sems, u, j, peer):
    return pltpu.make_async_remote_copy(
        src_ref=src_ref.at[pslot] if per_slot else src_ref, dst_ref=land_ref.at[slot],
        send_sem=sems[0].at[u * (N_DEV - 1) + j - 1], recv_sem=sems[1].at[u * (N_DEV - 1) + j - 1],
        device_id=peer, device_id_type=pl.DeviceIdType.MESH)


def _own_copy(src_ref, land_ref, me, per_slot, sems, u):
    return pltpu.make_async_copy(src_ref.at[me] if per_slot else src_ref, land_ref.at[me], sems[2].at[u])


def _exchange_start(name, srcs, per_slot):
    n = len(srcs)
    lands = [jax.ShapeDtypeStruct(s.shape if p else (N_DEV,) + s.shape, s.dtype) for s, p in zip(srcs, per_slot)]

    def body(*refs):
        src_refs, sems, land_refs, token = refs[:n], refs[n:n + 3], refs[2 * n + 3:3 * n + 3], refs[-1]
        place = _mesh_place()
        me = _slot(*place)
        for u in range(n):
            _own_copy(src_refs[u], land_refs[u], me, per_slot[u], sems, u).start()
            for j in range(1, N_DEV):
                peer = _peer(place, j)
                _remote_copy(src_refs[u], land_refs[u], me, per_slot[u], _slot(*peer), sems, u, j, peer).start()
        token[...] = jnp.zeros_like(token)

    hbm = lambda a: pltpu.HBM(a.shape, a.dtype)
    sem = pltpu.SemaphoreType.DMA((n * (N_DEV - 1),))
    outs = pl.pallas_call(
        body, name=name,
        out_shape=(sem, sem, pltpu.SemaphoreType.DMA((n,)), *[hbm(a) for a in srcs], *[hbm(a) for a in lands],
                   jax.ShapeDtypeStruct((SUBLANES, LANES), F32)),
        in_specs=[_HBM] * n, out_specs=(_SEM, _SEM, _SEM, *[_HBM] * (2 * n), pl.BlockSpec(memory_space=pltpu.VMEM)),
        input_output_aliases={i: 3 + i for i in range(n)},
        compiler_params=pltpu.CompilerParams(has_side_effects=_EFFECT),
    )(*[pltpu.with_memory_space_constraint(a, pltpu.HBM) for a in srcs])
    return tuple(outs[:3]), list(outs[3:3 + n]), list(outs[3 + n:3 + 2 * n]), outs[-1]


def _exchange_wait(name, sems, srcs, lands, units, per_slot, after):
    m = len(units)

    def body(*refs):
        src_refs, land_refs, sem_refs = refs[:m], refs[m:2 * m], refs[2 * m:2 * m + 3]
        place = _mesh_place()
        me = _slot(*place)
        for i, u in enumerate(units):
            _own_copy(src_refs[i], land_refs[i], me, per_slot[u], sem_refs, u).wait()
            for j in range(1, N_DEV):
                peer = _peer(place, j)
                pslot = _slot(*peer)
                cp = _remote_copy(src_refs[i], land_refs[i], pslot, per_slot[u], pslot, sem_refs, u, j, peer)
                cp.wait_send()
                cp.wait_recv()

    hbm = lambda a: pltpu.HBM(a.shape, a.dtype)
    outs = pl.pallas_call(
        body, name=name, out_shape=tuple(hbm(a) for a in list(srcs) + list(lands)),
        in_specs=[_HBM] * (2 * m) + [_SEM] * 3 + [pl.BlockSpec(memory_space=pl.ANY)], out_specs=tuple([_HBM] * (2 * m)),
        input_output_aliases={i: i for i in range(2 * m)},
        compiler_params=pltpu.CompilerParams(has_side_effects=_EFFECT),
    )(*srcs, *lands, *sems, after)
    return list(outs[m:])


def _adam_update(g, w, m, v):
    m = ADAM_B1 * m + (1.0 - ADAM_B1) * g
    v = ADAM_B2 * v + (1.0 - ADAM_B2) * jnp.square(g)
    m_hat = m / (1.0 - ADAM_B1 ** ADAM_STEP)
    v_hat = v / (1.0 - ADAM_B2 ** ADAM_STEP)
    return -ADAM_LR * (m_hat / (jnp.sqrt(v_hat) + ADAM_EPS) + ADAM_WD * w), m, v


def _adamw(recvs, w, m, v, *, tr, name):
    L, R, C = w.shape
    rp = max(tr, SUBLANES * (4 // jnp.dtype(recvs[0].dtype).itemsize))
    Cp = recvs[0].shape[2]

    def body(*refs):
        r_refs, (w_ref, m_ref, v_ref, g_ref, d_ref, nm_ref, nv_ref) = refs[:L], refs[L:]
        for l in range(L):
            @pl.when(pl.program_id(0) == l)
            def _(r_ref=r_refs[l]):
                g = r_ref[0, :tr, :C].astype(F32)
                for s in range(1, N_DEV):
                    g = g + r_ref[s, :tr, :C].astype(F32)
                d, nm, nv = _adam_update(g, w_ref[0], m_ref[0], v_ref[0])
                g_ref[0], d_ref[0], nm_ref[0], nv_ref[0] = g, d, nm, nv

    blk = pl.BlockSpec((1, tr, C), lambda l, i: (l, i, 0))
    r_specs = [pl.BlockSpec((N_DEV, rp, Cp), lambda l, i, k=k: (0, jnp.where(l == k, i, 0), 0)) for k in range(L)]
    est = 2 * _nbytes((N_DEV, rp, Cp), recvs[0].dtype) + 8 * _nbytes((tr, Cp), F32)
    return pl.pallas_call(
        body, name=name, grid=(L, R // tr),
        in_specs=r_specs + [blk] * 3, out_specs=[blk] * 4,
        out_shape=[jax.ShapeDtypeStruct((L, R, C), F32)] * 4,
        compiler_params=_cparams(est, ("arbitrary", "arbitrary")),
    )(*recvs, w, m, v)


def _adamw_small(recv, w, m, v):
    def body(r_ref, w_ref, m_ref, v_ref, g_ref, d_ref, nm_ref, nv_ref):
        g = r_ref[0]
        for s in range(1, N_DEV):
            g = g + r_ref[s]
        g_ref[...] = g
        d_ref[...], nm_ref[...], nv_ref[...] = _adam_update(g, w_ref[...], m_ref[...], v_ref[...])

    vm = pl.BlockSpec(memory_space=pltpu.VMEM)
    return pl.pallas_call(
        body, name="adamw_small", in_specs=[vm] * 4, out_specs=[vm] * 4,
        out_shape=[jax.ShapeDtypeStruct((SMALL_ROWS, LANES), F32)] * 4,
        compiler_params=_cparams(20 * _nbytes((SMALL_ROWS, LANES), F32)),
    )(recv, w, m, v)


def kernel(x, w_in, conv_w, a_log, dt_bias, o_norm_w, sgu_ln_g, sgu_ln_b, w_s, b_s, w_pa, w_pb, w_o, ln1_g, ln1_b, w_ffn_gate, w_ffn_up, w_ffn_down, ln2_g, ln2_b, loss_target, m_w_in, m_conv_w, m_a_log, m_dt_bias, m_o_norm_w, m_sgu_ln_g, m_sgu_ln_b, m_w_s, m_b_s, m_w_pa, m_w_pb, m_w_o, m_ln1_g, m_ln1_b, m_w_ffn_gate, m_w_ffn_up, m_w_ffn_down, m_ln2_g, m_ln2_b, v_w_in, v_conv_w, v_a_log, v_dt_bias, v_o_norm_w, v_sgu_ln_g, v_sgu_ln_b, v_w_s, v_b_s, v_w_pa, v_w_pb, v_w_o, v_ln1_g, v_ln1_b, v_w_ffn_gate, v_w_ffn_up, v_w_ffn_down, v_ln2_g, v_ln2_b):
    given = dict(locals())
    P = {n: given[n] for n in WEIGHT_NAMES}
    M = {n: given["m_" + n] for n in WEIGHT_NAMES}
    V = {n: given["v_" + n] for n in WEIGHT_NAMES}

    wire = _wire_blocks(P)
    units = [(n, l) for l in range(DEPTH) for n in EARLY + LATE]
    whole = [False] * len(units)
    g_sems, g_srcs, g_lands, g_token = _exchange_start("gather_start", [wire[n][l] for n, l in units], whole)

    one = 1.0 + g_token[0, 0]
    adam_in = {n: [t[n] * one for t in (P, M, V)] if n in UNALIGNED else [P[n], M[n], V[n]] for n in WIRE}
    first_after = g_token + sum(a[0, 0, 0] for n in UNALIGNED for a in adam_in[n])

    def gathered(name, names, l, after):
        idx = [units.index((n, l)) for n in names]
        got = _exchange_wait(name, g_sems, [g_srcs[i] for i in idx], [g_lands[i] for i in idx], idx, whole, after)
        return dict(zip(names, got))

    def layer(l):
        def weights(x_in):
            after = first_after if l == 0 else x_in
            early = _early_weights(gathered(f"gather_wait_early{l}", EARLY, l, after), P, l)
            return early, lambda ya: _late_weights(gathered(f"gather_wait_late{l}", LATE, l, ya))
        return weights

    pending = {}

    def on_grads(l, part, g):
        if part == "late":
            srcs, names = _late_slots(g), LATE
            per_slot = [True] * len(srcs)
        else:
            slots, small = _early_slots(g)
            srcs, names = slots + [small], EARLY + ("small",)
            per_slot = [True] * len(slots) + [False]
        sems, s_thru, l_thru, token = _exchange_start(f"exchange_start_{part}{l}", srcs, per_slot)
        pending[l, part] = (names, sems, s_thru, l_thru, per_slot)
        return token[0, 0]

    loss_local, dx, _ = _local_step(x[0], loss_target[0], [layer(l) for l in range(DEPTH)], on_grads)
    loss = lax.psum(loss_local, MESH_AXES)

    recv = [{} for _ in range(DEPTH)]
    for l in reversed(range(DEPTH)):
        for part in ("late", "early"):
            names, sems, s_thru, l_thru, per_slot = pending[l, part]
            got = _exchange_wait(f"exchange_wait_{part}{l}", sems, s_thru, l_thru, list(range(len(s_thru))), per_slot, dx)
            recv[l].update(zip(names, got))

    out = {}
    for n in WIRE:
        tr, name = ADAM_TILES[n]
        out[n] = _adamw([recv[l][n] for l in range(DEPTH)], *adam_in[n], tr=tr, name=name)
    small = [_adamw_small(recv[l]["small"], *[_small_pack({n: T[n][l] for n, _ in SMALL}) for T in (P, M, V)])
             for l in range(DEPTH)]
    for n, _ in SMALL:
        out[n] = [jnp.stack([_small_unpack(small[l][i], P)[n] for l in range(DEPTH)]) for i in range(4)]
    return (loss, dx[None], *[out[n][i] for i in range(4) for n in WEIGHT_NAMES])
```

```python
import functools
import math

import jax
import jax.numpy as jnp
from jax import lax
from jax.experimental import pallas as pl
from jax.experimental.pallas import tpu as pltpu

F32 = jnp.float32
BF = jnp.bfloat16
HIGHEST = lax.Precision.HIGHEST

D_MODEL = 1024
DEPTH = 2
N_HEADS = 8
D_HEAD = 128
CONV_K = 4
SGU_BLOCK = 128
SGU_GROUPS = 8
SGU_CHUNK = 64
FFN_HIDDEN = 2816
N_IN = 8208
N_DEV = 8
IN_BLOCK, IN_PAD = N_IN // N_DEV, 1152
FFN_BLOCK, FFN_PAD = FFN_HIDDEN // N_DEV, 384
FFN_K = N_DEV * FFN_PAD
ALPHA = (2 * DEPTH) ** 0.25
LN_EPS = 1e-5
RMS_EPS = 1e-6
ADAM_LR, ADAM_B1, ADAM_B2, ADAM_EPS, ADAM_WD, ADAM_STEP = 0.001, 0.9, 0.999, 1e-08, 0.01, 10

MESH_AXES = ("x", "y", "c")
DELTA_CHUNK = 128
DELTA_HEADS_PER_STEP = 8
LANES = 128
SUBLANES = 8
VMEM_BYTES = 64 * 1024 * 1024
HALO = SUBLANES
HALO_BF = 2 * SUBLANES


def _cparams(est_bytes, dims=None):
    limit = int(min(max(2 * est_bytes + (8 << 20), 32 << 20), VMEM_BYTES - (6 << 20)))
    kw = dict(vmem_limit_bytes=limit)
    if dims is not None:
        kw["dimension_semantics"] = dims
    return pltpu.CompilerParams(**kw)


def _nbytes(shape, dtype):
    return math.prod(shape) * jnp.dtype(dtype).itemsize


def _dims(kind, ndim):
    lhs, rhs = {"nn": (1, 0), "nt": (1, 1), "tn": (0, 0)}[kind]
    b = ndim - 2
    return (((lhs + b,), (rhs + b,)), (tuple(range(b)), tuple(range(b))))


def _mxu(a, b, kind):
    return lax.dot_general(a, b, _dims(kind, a.ndim), preferred_element_type=F32)


def _dot(a, b):
    return _mxu(a.astype(BF), b.astype(BF), "nn")


def _dot_nt(a, b):
    return _mxu(a.astype(BF), b.astype(BF), "nt")


def _dot_tn(a, b):
    return _mxu(a.astype(BF), b.astype(BF), "tn")


def _split(a):
    hi = a.astype(BF)
    return hi, (a - hi.astype(F32)).astype(BF)


def _dot3(a, b, kind):
    (ah, al), (bh, bl) = _split(a), _split(b)
    return _mxu(ah, bh, kind) + (_mxu(ah, bl, kind) + _mxu(al, bh, kind))


def _dotf(a, b):
    return _dot3(a, b, "nn")


def _dotf_nt(a, b):
    return _dot3(a, b, "nt")


def _dot01(sel, x, kind="nn"):
    s = jnp.broadcast_to(sel.astype(BF), x.shape[:-2] + sel.shape)
    h1 = x.astype(BF)
    r1 = x - h1.astype(F32)
    h2 = r1.astype(BF)
    h3 = (r1 - h2.astype(F32)).astype(BF)
    return _mxu(s, h1, kind) + (_mxu(s, h2, kind) + _mxu(s, h3, kind))


def _sigmoid(x):
    return 0.5 * jnp.tanh(0.5 * x) + 0.5


def _silu(x):
    return x * _sigmoid(x)


def _silu_and_grad(x):
    s = _sigmoid(x)
    return x * s, s * (1.0 + x * (1.0 - s))


def _softplus(x):
    return jnp.maximum(x, 0.0) + jnp.log1p(jnp.exp(-jnp.abs(x)))


def _ln(x, g, b):
    mu = jnp.mean(x, -1, keepdims=True)
    xc = x - mu
    var = jnp.mean(xc * xc, -1, keepdims=True)
    return xc * lax.rsqrt(var + LN_EPS) * g + b


def _iota(shape, dim):
    return lax.broadcasted_iota(jnp.int32, shape, dim)


def _tile(n, pref, align):
    if n <= pref:
        return n
    t = (pref // align) * align
    while t >= align:
        if n % t == 0:
            return t
        t -= align
    raise ValueError(f"no tile for {n} (pref {pref}, align {align})")


def _bcast_rows(v, rows=SUBLANES):
    return jnp.broadcast_to(v, (rows, v.shape[-1]))


def _mm(a, b, *, mode, name, out_dtype=F32, add=None, add_scale=1.0, tm=512, tn=1024, tk=1024, cols=None, pair=None):
    if mode == "nn":
        (M, K), N = a.shape, b.shape[1]
    elif mode == "nt":
        (M, K), N = a.shape, b.shape[0]
    else:
        (K, M), N = a.shape, b.shape[1]
    col0 = 0
    if cols is not None:
        col0, N = cols
    tm = _tile(M, tm, LANES if mode == "tn" else SUBLANES * 2)
    tn = _tile(N, tn, LANES)
    tk = _tile(K, tk, LANES)
    nk = K // tk
    j0 = col0 // tn
    if mode == "nn":
        a_spec = pl.BlockSpec((tm, tk), lambda i, j, k: (i, k))
        b_spec = pl.BlockSpec((tk, tn), lambda i, j, k: (k, j + j0))
        dot = _dot
    elif mode == "nt":
        a_spec = pl.BlockSpec((tm, tk), lambda i, j, k: (i, k))
        b_spec = pl.BlockSpec((tn, tk), lambda i, j, k: (j, k))
        dot = _dot_nt
    else:
        a_spec = pl.BlockSpec((tk, tm), lambda i, j, k: (k, i))
        b_spec = pl.BlockSpec((tk, tn), lambda i, j, k: (k, j))
        dot = _dot_tn
    o_spec = pl.BlockSpec((tm, tn), lambda i, j, k: (i, j))
    has_add = add is not None

    n_ab = 2 if pair is None else 4

    def body(*refs):
        ab, (o_ref, acc_ref) = refs[:n_ab], refs[-2:]
        add_ref = refs[n_ab] if has_add else None
        k = pl.program_id(2)
        part = dot(ab[0][...], ab[1][...])
        if pair is not None:
            part = part + dot(ab[2][...], ab[3][...])

        def finish(total):
            if has_add:
                total = total + add_scale * add_ref[...]
            o_ref[...] = total.astype(out_dtype)

        if nk == 1:
            finish(part)
        else:
            @pl.when(k == 0)
            def _():
                acc_ref[...] = part

            @pl.when(jnp.logical_and(k > 0, k < nk - 1))
            def _():
                acc_ref[...] += part

            @pl.when(k == nk - 1)
            def _():
                finish(acc_ref[...] + part)

    in_specs = [a_spec, b_spec] * (n_ab // 2) + ([o_spec] if has_add else [])
    args = (a, b) + (tuple(pair) if pair is not None else ()) + ((add,) if has_add else ())
    est = ((n_ab // 2) * (_nbytes((tm, tk), a.dtype) + _nbytes((tk, tn), b.dtype)) + 2 * _nbytes((tm, tn), F32)
           + (_nbytes((tm, tn), F32) if has_add else 0)) + 2 * _nbytes((tm, tn), F32)
    return pl.pallas_call(
        body, name=name,
        grid=(M // tm, N // tn, nk),
        in_specs=in_specs, out_specs=o_spec,
        out_shape=jax.ShapeDtypeStruct((M, N), out_dtype),
        scratch_shapes=[pltpu.VMEM((tm, tn) if nk > 1 else (SUBLANES, LANES), F32)],
        compiler_params=_cparams(est, ("parallel", "parallel", "arbitrary")),
    )(*args)


def _shifted(xt, halo, first):
    halo = jnp.where(first, 0.0, halo)
    xc = jnp.concatenate([halo, xt], axis=0)
    return [xt] + [pltpu.roll(xc, s, 0)[HALO:] for s in range(1, CONV_K)]


def _conv_taps(shifted, w_ref):
    out = shifted[0] * w_ref[CONV_K - 1:CONV_K, :]
    for s in range(1, CONV_K):
        out = out + shifted[s] * w_ref[CONV_K - 1 - s:CONV_K - s, :]
    return out


def _gates(ba, arow, dtrow):
    lane = _iota(ba.shape, 1)
    beta = _sigmoid(ba)
    g = -jnp.exp(arow) * _softplus(ba + dtrow)
    return jnp.where(lane < N_HEADS, beta, jnp.where(lane < 2 * N_HEADS, g, 0.0))


def _l2n(x):
    return x * lax.rsqrt(jnp.sum(x * x, -1, keepdims=True) + RMS_EPS)


def _qkv_prep(proj, ba, convw, arow, dtrow, *, tm=256):
    S = proj.shape[0]
    tm = _tile(S, tm, SUBLANES)
    W3 = 3 * D_MODEL
    hb = tm // HALO

    def body(xt_ref, halo_ref, ba_ref, w_ref, a_ref, dt_ref, q_ref, k_ref, v_ref, gb_ref, c_ref):
        c = _conv_taps(_shifted(xt_ref[...], halo_ref[...], pl.program_id(0) == 0), w_ref)
        c_ref[...] = c
        c = _silu(c)
        for h in range(N_HEADS):
            lo = h * D_HEAD
            q_ref[:, lo:lo + D_HEAD] = _l2n(c[:, lo:lo + D_HEAD])
            k_ref[:, lo:lo + D_HEAD] = _l2n(c[:, D_MODEL + lo:D_MODEL + lo + D_HEAD])
        v_ref[...] = c[:, 2 * D_MODEL:]
        gb_ref[...] = _gates(ba_ref[...], a_ref[...], dt_ref[...])

    row = lambda w, col=0: pl.BlockSpec((tm, w), lambda i: (i, col))
    full = lambda shape: pl.BlockSpec(shape, lambda i: (0,) * len(shape))
    est = 4 * _nbytes((tm, W3), F32)
    return pl.pallas_call(
        body, name="qkv_prep", grid=(S // tm,),
        in_specs=[row(W3), pl.BlockSpec((HALO, W3), lambda i: (jnp.maximum(i * hb - 1, 0), 0)), row(LANES),
                  full((CONV_K, W3)), full((1, LANES)), full((1, LANES))],
        out_specs=[row(D_MODEL), row(D_MODEL), row(D_MODEL), row(LANES), row(W3)],
        out_shape=[jax.ShapeDtypeStruct((S, D_MODEL), F32)] * 3 + [jax.ShapeDtypeStruct((S, LANES), F32),
                                                                   jax.ShapeDtypeStruct((S, W3), F32)],
        compiler_params=_cparams(est, ("arbitrary",)),
    )(proj, proj, ba, convw, arow, dtrow)


def _qkv_prep_bwd(proj, conv_out, ba, arow, dtrow, dq, dk, dv, dgb, *, tm=256):
    S = proj.shape[0]
    tm = _tile(S, tm, SUBLANES * 2)
    W3 = 3 * D_MODEL
    hb = tm // HALO

    def body(xt_ref, halo_ref, c_ref, ba_ref, a_ref, dt_ref, dq_ref, dk_ref, dv_ref, dgb_ref,
             dcb_ref, dba_ref, dw_ref, da_ref, ddt_ref, dc_ref):
        i = pl.program_id(0)

        @pl.when(i == 0)
        def _():
            dw_ref[...] = jnp.zeros_like(dw_ref)
            da_ref[...] = jnp.zeros_like(da_ref)
            ddt_ref[...] = jnp.zeros_like(ddt_ref)

        shifted = _shifted(xt_ref[...], halo_ref[...], i == 0)
        a, ds = _silu_and_grad(c_ref[...])
        for h in range(N_HEADS):
            for base, d_ref in ((0, dq_ref), (D_MODEL, dk_ref)):
                lo = base + h * D_HEAD
                _, vj = jax.vjp(_l2n, a[:, lo:lo + D_HEAD])
                (dx,) = vj(d_ref[:, h * D_HEAD:(h + 1) * D_HEAD])
                dc_ref[:, lo:lo + D_HEAD] = dx * ds[:, lo:lo + D_HEAD]
        dc_ref[:, 2 * D_MODEL:] = dv_ref[...] * ds[:, 2 * D_MODEL:]
        dc = dc_ref[...]
        dcb_ref[...] = dc.astype(BF)
        for s in range(CONV_K):
            kk = CONV_K - 1 - s
            dw_ref[kk:kk + 1, :] += jnp.sum(dc * shifted[s], axis=0, keepdims=True)
        _, vj = jax.vjp(_gates, ba_ref[...], a_ref[...], dt_ref[...])
        dba, da, ddt = vj(dgb_ref[...])
        dba_ref[...] = dba.astype(BF)
        da_ref[...] += _bcast_rows(da)
        ddt_ref[...] += _bcast_rows(ddt)

    row = lambda w, col=0: pl.BlockSpec((tm, w), lambda i: (i, col))
    full = lambda shape: pl.BlockSpec(shape, lambda i: (0,) * len(shape))
    est = 8 * _nbytes((tm, W3), F32)
    return pl.pallas_call(
        body, name="qkv_prep_bwd", grid=(S // tm,),
        in_specs=[row(W3), pl.BlockSpec((HALO, W3), lambda i: (jnp.maximum(i * hb - 1, 0), 0)), row(W3), row(LANES),
                  full((1, LANES)), full((1, LANES)),
                  row(D_MODEL), row(D_MODEL), row(D_MODEL), row(LANES)],
        out_specs=[row(W3), row(LANES), full((SUBLANES, W3)), full((SUBLANES, LANES)), full((SUBLANES, LANES))],
        out_shape=[jax.ShapeDtypeStruct((S, W3), BF), jax.ShapeDtypeStruct((S, LANES), BF),
                   jax.ShapeDtypeStruct((SUBLANES, W3), F32), jax.ShapeDtypeStruct((SUBLANES, LANES), F32),
                   jax.ShapeDtypeStruct((SUBLANES, LANES), F32)],
        scratch_shapes=[pltpu.VMEM((tm, W3), F32)],
        compiler_params=_cparams(est, ("arbitrary",)),
    )(proj, proj, conv_out, ba, arow, dtrow, dq, dk, dv, dgb)


def _conv_bwd(dc, convw, dproj, *, tm=256):
    S, W3 = dc.shape
    tm = _tile(S, tm, HALO_BF)
    hb = tm // HALO_BF
    nt = S // tm

    def body(dc_ref, nxt_ref, w_ref, dproj_ref, o_ref):
        last = pl.program_id(0) == nt - 1
        nxt = jnp.where(last, 0.0, nxt_ref[...].astype(F32))
        cur = dc_ref[...].astype(F32)
        xc = jnp.concatenate([cur, nxt], axis=0)
        out = cur * w_ref[CONV_K - 1:CONV_K, :]
        for s in range(1, CONV_K):
            out = out + pltpu.roll(xc, tm + HALO_BF - s, 0)[:tm] * w_ref[CONV_K - 1 - s:CONV_K - s, :]
        o_ref[...] = out.astype(BF)

    est = 5 * _nbytes((tm, W3), F32)
    return pl.pallas_call(
        body, name="conv_bwd", grid=(nt,),
        in_specs=[pl.BlockSpec((tm, W3), lambda i: (i, 0)),
                  pl.BlockSpec((HALO_BF, W3), lambda i: (jnp.minimum((i + 1) * hb, S // HALO_BF - 1), 0)),
                  pl.BlockSpec((CONV_K, W3), lambda i: (0, 0)), pl.BlockSpec(memory_space=pl.ANY)],
        out_specs=pl.BlockSpec((tm, W3), lambda i: (i, 0)),
        out_shape=jax.ShapeDtypeStruct(dproj.shape, BF),
        input_output_aliases={3: 0},
        compiler_params=_cparams(est, ("parallel",)),
    )(dc, dc, convw, dproj)


NEUMANN_BLOCK = 8


def _inv_unit_lower(A):
    C = A.shape[-1]
    row, col = _iota((C, C), 0), _iota((C, C), 1)
    eye = jnp.where(row == col, 1.0, 0.0).astype(F32)
    Ab = A.astype(BF)
    sh = jnp.int32(int(math.log2(NEUMANN_BLOCK)))
    B = jnp.where(lax.shift_right_logical(row, sh) == lax.shift_right_logical(col, sh), Ab, jnp.zeros_like(Ab))
    B2 = _mxu(B, B, "nn")
    B4 = _dot3(B2, B2, "nn")
    b2h, b2l = _split(B2)
    P = eye - B.astype(F32) + B2 - (_mxu(B, b2h, "nn") + _mxu(B, b2l, "nn"))
    T = P + _dot3(P, B4, "nn")
    b = NEUMANN_BLOCK
    while b < C:
        hi = ~(2 * b - 1)
        off = ((row & hi) == (col & hi)) & ((row & b) != 0) & ((col & b) == 0)
        Aoff = jnp.where(off, Ab, jnp.zeros_like(Ab))
        th, tl = _split(T)
        xh, xl = _split(_mxu(th, Aoff, "nn") + _mxu(tl, Aoff, "nn"))
        T = T - (_mxu(xh, th, "nn") + (_mxu(xh, tl, "nn") + _mxu(xl, th, "nn")))
        b *= 2
    return T


def _delta_common(q, k, g, beta):
    C = q.shape[-2]
    row, col = _iota((C, C), 0), _iota((C, C), 1)
    tril = row >= col
    qs = q * (D_HEAD ** -0.5)
    gcb = _dot01(jnp.where(tril, 1.0, 0.0), jnp.broadcast_to(g, g.shape[:-1] + (LANES,)))
    gc = gcb[..., :1]
    gr = jnp.swapaxes(gcb, -1, -2)
    Dm = jnp.exp(jnp.where(tril, gc - gr, -1e30))
    Dmt = jnp.exp(jnp.where(row <= col, gr - gc, -1e30))
    eg = jnp.exp(gc)
    gl = jnp.sum(jnp.where(_iota((C, 1), 0) == C - 1, gc, 0.0), axis=(-2, -1), keepdims=True)
    el = jnp.exp(gl)
    er = jnp.exp(gl - gc)
    kb = k * beta
    KK = _dot_nt(kb, k)
    QK = _dot_nt(qs, k)
    return dict(row=row, col=col, tril=tril, qs=qs, gc=gc, Dm=Dm, Dmt=Dmt, eg=eg, el=el, er=er, kb=kb, KK=KK, QK=QK)


def _delta_chunk_fwd(S0, q, k, v, g, beta):
    m = _delta_common(q, k, g, beta)
    T = _inv_unit_lower(jnp.where(m["row"] > m["col"], m["KK"] * m["Dm"], 0.0))
    u = _dotf(T, v * beta)
    w = _dotf(T, m["kb"] * m["eg"])
    vn = u - _dot(w, S0)
    o = _dot(m["qs"] * m["eg"], S0) + _dot(m["QK"] * m["Dm"], vn)
    S1 = S0 * m["el"] + _dot_tn(k * m["er"], vn)
    return o, S1, jnp.swapaxes(T, -1, -2), u, w


def _delta_chunk_bwd(S0, q, k, v, g, beta, Tt, u, w, do, dS1):
    m = _delta_common(q, k, g, beta)
    C = q.shape[-2]
    qs, Dm, Dmt, eg, el, er, kb, KK, QK = (m[n] for n in ("qs", "Dm", "Dmt", "eg", "el", "er", "kb", "KK", "QK"))
    strict = m["row"] > m["col"]
    total = lambda x: jnp.sum(x, axis=(-2, -1), keepdims=True)
    vn = u - _dot(w, S0)
    qg = qs * eg
    kr = k * er

    dvn = _dot(_dot_nt(k, qs) * Dmt, do) + _dot(kr, dS1)
    dS0 = dS1 * el + _dot_tn(qg, do) - _dot_tn(w, dvn)
    d_el = total(dS1 * S0)
    dqg = _dot_nt(do, S0)
    dqs = dqg * eg
    deg = jnp.sum(dqg * qs, -1, keepdims=True)
    dP = _dot_nt(do, vn)
    dPD = dP * Dm
    dqs = dqs + _dot(dPD, k)
    dk = _dot(_dot_nt(vn, do) * Dmt, qs)
    dD = dP * QK
    dkr = _dot_nt(vn, dS1)
    dk = dk + dkr * er
    der = jnp.sum(dkr * k, -1, keepdims=True)
    dw = -_dot_nt(dvn, S0)
    th, tl = _split(Tt)

    def tt_times(x):
        xh, xl = _split(x)
        return _mxu(th, xh, "nn") + (_mxu(th, xl, "nn") + _mxu(tl, xh, "nn"))

    dru = tt_times(dvn)
    drw = tt_times(dw)
    dA = -(_dotf_nt(dru, u) + _dotf_nt(drw, w))
    dAm = jnp.where(strict, dA, 0.0)
    dKK = dAm * Dm
    dkb = _dot(dKK, k)
    dk = dk + _dot_tn(dKK, kb)
    dD = dD + dAm * KK
    dv = dru * beta
    dbeta = jnp.sum(dru * v, -1, keepdims=True)
    dkb = dkb + drw * eg
    deg = deg + jnp.sum(drw * kb, -1, keepdims=True)
    dk = dk + dkb * beta
    dbeta = dbeta + jnp.sum(dkb * k, -1, keepdims=True)
    E = dD * Dm
    dgc = jnp.sum(E, -1, keepdims=True) - jnp.sum(jnp.swapaxes(E, -1, -2), -1, keepdims=True)
    dgc = dgc + deg * eg - der * er
    dgl = total(der * er) + d_el * el
    dgc = dgc + jnp.where(_iota((C, 1), 0) == C - 1, dgl, 0.0)
    triu = jnp.where(m["row"] <= m["col"], 1.0, 0.0)
    dg = _dot01(triu, jnp.broadcast_to(dgc, dgc.shape[:-1] + (LANES,)))[..., :1]
    dq = dqs * (D_HEAD ** -0.5)
    return dq, dk, dv, dg, dbeta, dS0


def _head_cols(gb, h):
    lane = _iota(gb.shape, 1)
    beta = jnp.sum(jnp.where(lane == h, gb, 0.0), -1, keepdims=True)
    g = jnp.sum(jnp.where(lane == N_HEADS + h, gb, 0.0), -1, keepdims=True)
    return g, beta


def _delta_fwd(q, k, v, gb):
    S = q.shape[0]
    C = DELTA_CHUNK
    N = S // C

    HB = DELTA_HEADS_PER_STEP

    def body(q_ref, k_ref, v_ref, gb_ref, o_ref, st_ref, t_ref, u_ref, w_ref, s_scr):
        n, hb = pl.program_id(0), pl.program_id(1)
        gb = gb_ref[...]

        @pl.when(n == 0)
        def _():
            for hh in range(HB):
                s_scr[hb * HB + hh] = jnp.zeros((D_HEAD, D_HEAD), F32)

        heads = [hb * HB + hh for hh in range(HB)]
        cols = [slice(hh * D_HEAD, (hh + 1) * D_HEAD) for hh in range(HB)]
        per_head = lambda ref: jnp.stack([ref[:, c] for c in cols])
        g, beta = (jnp.stack(t) for t in zip(*[_head_cols(gb, h) for h in heads]))
        S0 = jnp.stack([s_scr[h] for h in heads])
        o, S1, Tt, u, w = _delta_chunk_fwd(S0, per_head(q_ref), per_head(k_ref), per_head(v_ref), g, beta)
        for hh in range(HB):
            st_ref[hh, 0] = S0[hh]
            t_ref[hh, 0] = Tt[hh]
            o_ref[:, cols[hh]] = o[hh]
            u_ref[:, cols[hh]] = u[hh]
            w_ref[:, cols[hh]] = w[hh]
            s_scr[heads[hh]] = S1[hh]

    hd = pl.BlockSpec((C, HB * D_HEAD), lambda n, h: (n, h))
    mat = pl.BlockSpec((HB, 1, D_HEAD, D_HEAD), lambda n, h: (h, n, 0, 0))
    est = 40 * HB * _nbytes((C, D_HEAD), F32)
    seq = jax.ShapeDtypeStruct((S, N_HEADS * D_HEAD), F32)
    return pl.pallas_call(
        body, name="delta_fwd", grid=(N, N_HEADS // HB),
        in_specs=[hd, hd, hd, pl.BlockSpec((C, LANES), lambda n, h: (n, 0))],
        out_specs=[hd, mat, mat, hd, hd],
        out_shape=[seq, jax.ShapeDtypeStruct((N_HEADS, N, D_HEAD, D_HEAD), F32),
                   jax.ShapeDtypeStruct((N_HEADS, N, C, C), F32), seq, seq],
        scratch_shapes=[pltpu.VMEM((N_HEADS, D_HEAD, D_HEAD), F32)],
        compiler_params=_cparams(est, ("arbitrary", "arbitrary")),
    )(q, k, v, gb)


def _delta_bwd(q, k, v, gb, st, tinv, u, w, do):
    S = q.shape[0]
    C = DELTA_CHUNK
    N = S // C

    HB = DELTA_HEADS_PER_STEP

    def body(q_ref, k_ref, v_ref, gb_ref, st_ref, t_ref, u_ref, w_ref, do_ref, dq_ref, dk_ref, dv_ref, dgb_ref, ds_scr):
        n, hb = pl.program_id(0), pl.program_id(1)
        gb = gb_ref[...]
        lane = _iota((C, LANES), 1)
        dgb = jnp.zeros((C, LANES), F32)

        @pl.when(n == 0)
        def _():
            for hh in range(HB):
                ds_scr[hb * HB + hh] = jnp.zeros((D_HEAD, D_HEAD), F32)

        heads = [hb * HB + hh for hh in range(HB)]
        cols = [slice(hh * D_HEAD, (hh + 1) * D_HEAD) for hh in range(HB)]
        per_head = lambda ref: jnp.stack([ref[:, c] for c in cols])
        g, beta = (jnp.stack(t) for t in zip(*[_head_cols(gb, h) for h in heads]))
        dS1 = jnp.stack([ds_scr[h] for h in heads])
        dq, dk, dv, dg, dbeta, dS0 = _delta_chunk_bwd(
            st_ref[:, 0], per_head(q_ref), per_head(k_ref), per_head(v_ref), g, beta, t_ref[:, 0],
            per_head(u_ref), per_head(w_ref), per_head(do_ref), dS1)
        for hh, h in enumerate(heads):
            dq_ref[:, cols[hh]] = dq[hh]
            dk_ref[:, cols[hh]] = dk[hh]
            dv_ref[:, cols[hh]] = dv[hh]
            dgb = dgb + jnp.where(lane == h, dbeta[hh], 0.0) + jnp.where(lane == N_HEADS + h, dg[hh], 0.0)
            ds_scr[h] = dS0[hh]

        @pl.when(hb == 0)
        def _():
            dgb_ref[...] = dgb

        @pl.when(hb > 0)
        def _():
            dgb_ref[...] += dgb

    hd = pl.BlockSpec((C, HB * D_HEAD), lambda n, h: (N - 1 - n, h))
    mat = pl.BlockSpec((HB, 1, D_HEAD, D_HEAD), lambda n, h: (h, N - 1 - n, 0, 0))
    gbs = pl.BlockSpec((C, LANES), lambda n, h: (N - 1 - n, 0))
    est = 60 * HB * _nbytes((C, D_HEAD), F32)
    return pl.pallas_call(
        body, name="delta_bwd", grid=(N, N_HEADS // HB),
        in_specs=[hd, hd, hd, gbs, mat, mat, hd, hd, hd],
        out_specs=[hd, hd, hd, gbs],
        out_shape=[jax.ShapeDtypeStruct((S, N_HEADS * D_HEAD), F32)] * 3 + [jax.ShapeDtypeStruct((S, LANES), F32)],
        scratch_shapes=[pltpu.VMEM((N_HEADS, D_HEAD, D_HEAD), F32)],
        compiler_params=_cparams(est, ("arbitrary", "arbitrary")),
    )(q, k, v, gb, st, tinv, u, w, do)


def _ya_head(o, z, onw):
    return o * lax.rsqrt(jnp.mean(o * o, -1, keepdims=True) + RMS_EPS) * onw * _silu(z)


def _norm_cdf(x):
    return 0.5 * (1.0 + lax.erf(x * 0.7071067811865476))


def _norm_pdf(x):
    return jnp.exp(-0.5 * x * x) * 0.3989422804014327


def _chunk_causal(shape, di, dj):
    sh = jnp.int32(int(math.log2(SGU_CHUNK)))
    return lax.shift_right_logical(_iota(shape, di), sh) >= lax.shift_right_logical(_iota(shape, dj), sh)


def _ws_masked(ws):
    return jnp.where(_chunk_causal(ws.shape, 1, 2), ws, 0.0)


def _mix_prep(o, proj, onw, sg, sb, ws, bst, *, tm=256):
    S = o.shape[0]
    tm = _tile(S, tm, SGU_BLOCK)

    def body(o_ref, z_ref, u_ref, vg_ref, onw_ref, sg_ref, sb_ref, ws_ref, bst_ref, ya_ref, yb_ref, phi_ref):
        onw = onw_ref[...]
        for h in range(N_HEADS):
            sl = slice(h * D_HEAD, (h + 1) * D_HEAD)
            ya_ref[:, sl] = _ya_head(o_ref[:, sl], z_ref[:, sl].astype(F32), onw).astype(BF)
        u, vg = u_ref[...].astype(F32), vg_ref[...].astype(F32)
        phi_u, phi_v = _norm_cdf(u), _norm_cdf(vg)
        phi_ref[:, :D_MODEL] = phi_u
        phi_ref[:, D_MODEL:] = phi_v
        ua, vl = u * phi_u, _ln(vg * phi_v, sg_ref[...], sb_ref[...])
        wsm = _ws_masked(ws_ref[...])
        bst = bst_ref[...]
        for blk in range(tm // SGU_BLOCK):
            rs = slice(blk * SGU_BLOCK, (blk + 1) * SGU_BLOCK)
            for gi in range(SGU_GROUPS):
                cs = slice(gi * D_HEAD, (gi + 1) * D_HEAD)
                sp = _dot(wsm[gi], vl[rs, cs]) + bst[:, gi:gi + 1]
                yb_ref[rs, cs] = (ua[rs, cs] * sp).astype(BF)

    blk = lambda col: pl.BlockSpec((tm, D_MODEL), lambda i: (i, col))
    full = lambda shape: pl.BlockSpec(shape, lambda i: (0,) * len(shape))
    est = 10 * _nbytes((tm, D_MODEL), F32)
    return pl.pallas_call(
        body, name="mix_prep", grid=(S // tm,),
        in_specs=[blk(0), blk(0), blk(1), blk(2), full((1, D_HEAD)), full((1, D_MODEL)), full((1, D_MODEL)),
                  full((SGU_GROUPS, SGU_BLOCK, SGU_BLOCK)), full((SGU_BLOCK, LANES))],
        out_specs=[blk(0), blk(0), pl.BlockSpec((tm, 2 * D_MODEL), lambda i: (i, 0))],
        out_shape=[jax.ShapeDtypeStruct((S, D_MODEL), BF)] * 2 + [jax.ShapeDtypeStruct((S, 2 * D_MODEL), F32)],
        compiler_params=_cparams(est, ("parallel",)),
    )(o, proj, proj, proj, onw, sg, sb, ws, bst)


def _mix_prep_bwd(o, proj, phi, onw, sg, sb, ws, bst, dya, dyb, dproj, *, tm=256):
    S = o.shape[0]
    tm = _tile(S, tm, SGU_BLOCK)

    def body(o_ref, z_ref, u_ref, vg_ref, phi_ref, onw_ref, sg_ref, sb_ref, ws_ref, bst_ref, dya_ref, dyb_ref, dproj_in,
             do_ref, dzuv_ref, donw_ref, dsg_ref, dsb_ref, dws_ref, dbst_ref, dvl_scr, dua_scr):
        dz_ref, du_ref, dvg_ref = (dzuv_ref.at[:, k * D_MODEL:(k + 1) * D_MODEL] for k in range(3))
        @pl.when(pl.program_id(0) == 0)
        def _():
            for r in (donw_ref, dsg_ref, dsb_ref, dws_ref, dbst_ref):
                r[...] = jnp.zeros_like(r)

        onw = onw_ref[...]
        donw = jnp.zeros((1, D_HEAD), F32)
        for h in range(N_HEADS):
            sl = slice(h * D_HEAD, (h + 1) * D_HEAD)
            _, vj = jax.vjp(_ya_head, o_ref[:, sl], z_ref[:, sl].astype(F32), onw)
            do_h, dz_h, donw_h = vj(dya_ref[:, sl])
            do_ref[:, sl] = do_h.astype(BF)
            dz_ref[:, sl] = dz_h.astype(BF)
            donw = donw + donw_h
        donw_ref[...] += _bcast_rows(donw)

        u, vg = u_ref[...].astype(F32), vg_ref[...].astype(F32)
        phi_u, phi_v = phi_ref[:, :D_MODEL], phi_ref[:, D_MODEL:]
        ua = u * phi_u
        vl, vj = jax.vjp(_ln, vg * phi_v, sg_ref[...], sb_ref[...])
        wsm = _ws_masked(ws_ref[...])
        bst = bst_ref[...]
        lane = _iota((SGU_BLOCK, LANES), 1)
        dbst = jnp.zeros((SGU_BLOCK, LANES), F32)
        cmask = _chunk_causal((SGU_BLOCK, SGU_BLOCK), 0, 1)
        for gi in range(SGU_GROUPS):
            cs = slice(gi * D_HEAD, (gi + 1) * D_HEAD)
            wg = wsm[gi]
            wgt = jnp.transpose(wg)
            dwg = jnp.zeros((SGU_BLOCK, SGU_BLOCK), F32)
            for blk in range(tm // SGU_BLOCK):
                rs = slice(blk * SGU_BLOCK, (blk + 1) * SGU_BLOCK)
                sp = _dot(wg, vl[rs, cs]) + bst[:, gi:gi + 1]
                dyb = dyb_ref[rs, cs]
                dsp = dyb * ua[rs, cs]
                dua_scr[rs, cs] = dyb * sp
                dvl_scr[rs, cs] = _dot(wgt, dsp)
                dwg = dwg + _dot_nt(dsp, vl[rs, cs])
                dbst = dbst + jnp.where(lane == gi, jnp.sum(dsp, -1, keepdims=True), 0.0)
            dws_ref[gi] += jnp.where(cmask, dwg, 0.0)
        dbst_ref[...] += dbst
        dgv, dsg, dsb = vj(dvl_scr[...])
        du_ref[...] = (dua_scr[...] * (phi_u + u * _norm_pdf(u))).astype(BF)
        dvg_ref[...] = (dgv * (phi_v + vg * _norm_pdf(vg))).astype(BF)
        dsg_ref[...] += _bcast_rows(dsg)
        dsb_ref[...] += _bcast_rows(dsb)

    blk = lambda col: pl.BlockSpec((tm, D_MODEL), lambda i: (i, col))
    full = lambda shape: pl.BlockSpec(shape, lambda i: (0,) * len(shape))
    est = 16 * _nbytes((tm, D_MODEL), F32)
    outs = pl.pallas_call(
        body, name="mix_prep_bwd", grid=(S // tm,),
        in_specs=[blk(0), blk(0), blk(1), blk(2), pl.BlockSpec((tm, 2 * D_MODEL), lambda i: (i, 0)),
                  full((1, D_HEAD)), full((1, D_MODEL)), full((1, D_MODEL)),
                  full((SGU_GROUPS, SGU_BLOCK, SGU_BLOCK)), full((SGU_BLOCK, LANES)), blk(0), blk(0),
                  pl.BlockSpec(memory_space=pl.ANY)],
        out_specs=[blk(0), pl.BlockSpec((tm, 3 * D_MODEL), lambda i: (i, 1)),
                   full((SUBLANES, D_HEAD)), full((SUBLANES, D_MODEL)), full((SUBLANES, D_MODEL)),
                   full((SGU_GROUPS, SGU_BLOCK, SGU_BLOCK)), full((SGU_BLOCK, LANES))],
        out_shape=[jax.ShapeDtypeStruct((S, D_MODEL), BF), jax.ShapeDtypeStruct(dproj.shape, BF),
                   jax.ShapeDtypeStruct((SUBLANES, D_HEAD), F32), jax.ShapeDtypeStruct((SUBLANES, D_MODEL), F32),
                   jax.ShapeDtypeStruct((SUBLANES, D_MODEL), F32),
                   jax.ShapeDtypeStruct((SGU_GROUPS, SGU_BLOCK, SGU_BLOCK), F32),
                   jax.ShapeDtypeStruct((SGU_BLOCK, LANES), F32)],
        input_output_aliases={12: 1},
        scratch_shapes=[pltpu.VMEM((tm, D_MODEL), F32)] * 2,
        compiler_params=_cparams(est, ("arbitrary",)),
    )(o, proj, proj, proj, phi, onw, sg, sb, ws, bst, dya, dyb, dproj)
    return outs


def _mm_gate_merge(ya, yb, wpa, wpb, proj, *, tm=512):
    S = ya.shape[0]
    tm = _tile(S, tm, SUBLANES * 2)

    def body(ya_ref, yb_ref, wa_ref, wb_ref, ga_ref, gb_ref, pa_ref, pb_ref, m_ref):
        pa = _dot(ya_ref[...], wa_ref[...]).astype(BF)
        pb = _dot(yb_ref[...], wb_ref[...]).astype(BF)
        pa_ref[...] = pa
        pb_ref[...] = pb
        m_ref[...] = (_sigmoid(ga_ref[...].astype(F32)) * pa.astype(F32)
                      + _sigmoid(gb_ref[...].astype(F32)) * pb.astype(F32)).astype(BF)

    blk = lambda col: pl.BlockSpec((tm, D_MODEL), lambda i: (i, col))
    wsp = pl.BlockSpec((D_MODEL, D_MODEL), lambda i: (0, 0))
    return pl.pallas_call(
        body, name="mm_gate_merge", grid=(S // tm,),
        in_specs=[blk(0), blk(0), wsp, wsp, blk(3), blk(4)], out_specs=[blk(0)] * 3,
        out_shape=[jax.ShapeDtypeStruct((S, D_MODEL), BF)] * 3,
        compiler_params=_cparams(2 * _nbytes((D_MODEL, D_MODEL), BF) + 8 * _nbytes((tm, D_MODEL), F32), ("parallel",)),
    )(ya, yb, wpa, wpb, proj, proj)


def _mm_gate_merge_bwd(dmix, wo, pa, pb, proj, *, tm=512):
    S = pa.shape[0]
    tm = _tile(S, tm, SUBLANES * 2)

    def body(d_ref, w_ref, pa_ref, pb_ref, ga_ref, gb_ref, dpa_ref, dpb_ref, dg_ref):
        dm = _dot_nt(d_ref[...], w_ref[...])
        sa, sb = _sigmoid(ga_ref[...].astype(F32)), _sigmoid(gb_ref[...].astype(F32))
        dpa_ref[...] = (dm * sa).astype(BF)
        dpb_ref[...] = (dm * sb).astype(BF)
        dg_ref[:, :D_MODEL] = (dm * pa_ref[...].astype(F32) * sa * (1.0 - sa)).astype(BF)
        dg_ref[:, D_MODEL:] = (dm * pb_ref[...].astype(F32) * sb * (1.0 - sb)).astype(BF)

    blk = lambda col: pl.BlockSpec((tm, D_MODEL), lambda i: (i, col))
    est = _nbytes((D_MODEL, D_MODEL), BF) + 10 * _nbytes((tm, D_MODEL), F32)
    return pl.pallas_call(
        body, name="mm_gate_merge_bwd", grid=(S // tm,),
        in_specs=[blk(0), pl.BlockSpec((D_MODEL, D_MODEL), lambda i: (0, 0)), blk(0), blk(0), blk(3), blk(4)],
        out_specs=[blk(0), blk(0), pl.BlockSpec((tm, 2 * D_MODEL), lambda i: (i, 3))],
        out_shape=[jax.ShapeDtypeStruct((S, D_MODEL), BF)] * 2 + [jax.ShapeDtypeStruct((S, 8 * D_MODEL), BF)],
        compiler_params=_cparams(est, ("parallel",)),
    )(dmix, wo, pa, pb, proj, proj)


def _mm_swiglu(xb, wg, wu, *, tm=1024, tn=768):
    S, K = xb.shape
    tm = _tile(S, tm, SUBLANES * 2)
    tn = _tile(FFN_K, tn, LANES)

    def body(x_ref, wg_ref, wu_ref, hg_ref, hu_ref, h_ref):
        x = x_ref[...]
        hg = _dot(x, wg_ref[...]).astype(BF)
        hu = _dot(x, wu_ref[...]).astype(BF)
        hg_ref[...] = hg
        hu_ref[...] = hu
        h_ref[...] = (_silu(hg.astype(F32)) * hu.astype(F32)).astype(BF)

    out = pl.BlockSpec((tm, tn), lambda i, j: (i, j))
    est = _nbytes((tm, K), BF) + 2 * _nbytes((K, tn), BF) + 6 * _nbytes((tm, tn), F32)
    return pl.pallas_call(
        body, name="mm_swiglu", grid=(S // tm, FFN_K // tn),
        in_specs=[pl.BlockSpec((tm, K), lambda i, j: (i, 0)), pl.BlockSpec((K, tn), lambda i, j: (0, j)),
                  pl.BlockSpec((K, tn), lambda i, j: (0, j))],
        out_specs=[out] * 3, out_shape=[jax.ShapeDtypeStruct((S, FFN_K), BF)] * 3,
        compiler_params=_cparams(est, ("parallel", "parallel")),
    )(xb, wg, wu)


def _mm_swiglu_bwd(dffn, wd, hg, hu, *, tm=1024, tn=768):
    S, K = dffn.shape
    tm = _tile(S, tm, SUBLANES * 2)
    tn = _tile(FFN_K, tn, LANES)

    def body(d_ref, w_ref, hg_ref, hu_ref, dhg_ref, dhu_ref):
        dh = _dot_nt(d_ref[...], w_ref[...])
        act, dact = _silu_and_grad(hg_ref[...].astype(F32))
        dhg_ref[...] = (dh * hu_ref[...].astype(F32) * dact).astype(BF)
        dhu_ref[...] = (dh * act).astype(BF)

    out = pl.BlockSpec((tm, tn), lambda i, j: (i, j))
    est = _nbytes((tm, K), dffn.dtype) + _nbytes((tn, K), BF) + 8 * _nbytes((tm, tn), F32)
    return pl.pallas_call(
        body, name="mm_swiglu_bwd", grid=(S // tm, FFN_K // tn),
        in_specs=[pl.BlockSpec((tm, K), lambda i, j: (i, 0)), pl.BlockSpec((tn, K), lambda i, j: (j, 0)), out, out],
        out_specs=[out, out], out_shape=[jax.ShapeDtypeStruct((S, FFN_K), BF)] * 2,
        compiler_params=_cparams(est, ("parallel", "parallel")),
    )(dffn, wd, hg, hu)


def _mm_resid_ln(a, bmat, x, g, b, *, name, tm=512):
    S, K = a.shape
    tm = _tile(S, tm, SUBLANES * 2)

    def body(a_ref, w_ref, x_ref, g_ref, b_ref, pre_ref, y_ref, yb_ref):
        pre = ALPHA * x_ref[...] + _dot(a_ref[...], w_ref[...])
        y = _ln(pre, g_ref[...], b_ref[...])
        pre_ref[...] = pre
        y_ref[...] = y
        yb_ref[...] = y.astype(BF)

    blk = pl.BlockSpec((tm, D_MODEL), lambda i: (i, 0))
    vec = pl.BlockSpec((1, D_MODEL), lambda i: (0, 0))
    est = _nbytes((tm, K), BF) + _nbytes((K, D_MODEL), BF) + 8 * _nbytes((tm, D_MODEL), F32)
    return pl.pallas_call(
        body, name=name, grid=(S // tm,),
        in_specs=[pl.BlockSpec((tm, K), lambda i: (i, 0)), pl.BlockSpec((K, D_MODEL), lambda i: (0, 0)), blk, vec, vec],
        out_specs=[blk, blk, blk],
        out_shape=[jax.ShapeDtypeStruct((S, D_MODEL), F32)] * 2 + [jax.ShapeDtypeStruct((S, D_MODEL), BF)],
        compiler_params=_cparams(est, ("parallel",)),
    )(a, bmat, x, g, b)


def _ln_bwd(pre, g, b, dy, *, tm=512):
    S = pre.shape[0]
    tm = _tile(S, tm, SUBLANES)

    def body(p_ref, g_ref, b_ref, dy_ref, dp_ref, dg_ref, db_ref):
        @pl.when(pl.program_id(0) == 0)
        def _():
            dg_ref[...] = jnp.zeros_like(dg_ref)
            db_ref[...] = jnp.zeros_like(db_ref)

        _, vj = jax.vjp(_ln, p_ref[...], g_ref[...], b_ref[...])
        dp, dg, db = vj(dy_ref[...])
        dp_ref[...] = dp
        dg_ref[...] += _bcast_rows(dg)
        db_ref[...] += _bcast_rows(db)

    blk = pl.BlockSpec((tm, D_MODEL), lambda i: (i, 0))
    vec = pl.BlockSpec((1, D_MODEL), lambda i: (0, 0))
    acc = pl.BlockSpec((SUBLANES, D_MODEL), lambda i: (0, 0))
    return pl.pallas_call(
        body, name="ln_bwd", grid=(S // tm,),
        in_specs=[blk, vec, vec, blk], out_specs=[blk, acc, acc],
        out_shape=[jax.ShapeDtypeStruct((S, D_MODEL), F32)] + [jax.ShapeDtypeStruct((SUBLANES, D_MODEL), F32)] * 2,
        compiler_params=_cparams(10 * _nbytes((tm, D_MODEL), F32), ("arbitrary",)),
    )(pre, g, b, dy)


def _loss_ln_bwd(y, tgt, pre, g, b, *, tm=512):
    S = y.shape[0]
    tm = _tile(S, tm, SUBLANES)

    def body(y_ref, t_ref, p_ref, g_ref, b_ref, dp_ref, dg_ref, db_ref, l_ref):
        @pl.when(pl.program_id(0) == 0)
        def _():
            for r in (dg_ref, db_ref, l_ref):
                r[...] = jnp.zeros_like(r)

        e = y_ref[...] - t_ref[...]
        l_ref[...] += 0.5 * jnp.sum(jnp.mean(e * e, -1, keepdims=True), keepdims=True)
        _, vj = jax.vjp(_ln, p_ref[...], g_ref[...], b_ref[...])
        dp, dg, db = vj(e * (1.0 / D_MODEL))
        dp_ref[...] = dp
        dg_ref[...] += _bcast_rows(dg)
        db_ref[...] += _bcast_rows(db)

    blk = pl.BlockSpec((tm, D_MODEL), lambda i: (i, 0))
    vec = pl.BlockSpec((1, D_MODEL), lambda i: (0, 0))
    acc = pl.BlockSpec((SUBLANES, D_MODEL), lambda i: (0, 0))
    return pl.pallas_call(
        body, name="loss_ln_bwd", grid=(S // tm,),
        in_specs=[blk, blk, blk, vec, vec], out_specs=[blk, acc, acc, pl.BlockSpec((SUBLANES, LANES), lambda i: (0, 0))],
        out_shape=[jax.ShapeDtypeStruct((S, D_MODEL), F32)] + [jax.ShapeDtypeStruct((SUBLANES, D_MODEL), F32)] * 2
                  + [jax.ShapeDtypeStruct((SUBLANES, LANES), F32)],
        compiler_params=_cparams(12 * _nbytes((tm, D_MODEL), F32), ("arbitrary",)),
    )(y, tgt, pre, g, b)


def _layer_fwd(x, xb, w, late):
    pq = _mm(xb, w["win"], mode="nn", name="mm_in_qkv", tm=1024, tn=1024, cols=(0, 3 * D_MODEL))
    proj = _mm(xb, w["win"], mode="nn", name="mm_in_rest", tm=1024, tn=1024, cols=(3 * D_MODEL, 5 * D_MODEL), out_dtype=BF)
    ba = _mm(xb, w["wba"], mode="nn", name="mm_in_ba", tm=1024, tn=LANES)
    qn, kn, vv, gb, conv_out = _qkv_prep(pq, ba, w["convw"], w["arow"], w["dtrow"])
    o, st, tinv, wy_u, wy_w = _delta_fwd(qn, kn, vv, gb)
    ya, yb, phi = _mix_prep(o, proj, w["onw"], w["sg"], w["sb"], w["ws"], w["bst"])
    w = {**w, **late(ya)}
    pa, pb, m = _mm_gate_merge(ya, yb, w["wpa"], w["wpb"], proj)
    pre1, x1, x1b = _mm_resid_ln(m, w["wo"], x, w["ln1g"], w["ln1b"], name="mm_out_ln")
    hg, hu, h = _mm_swiglu(x1b, w["wg"], w["wu"])
    pre2, x2, x2b = _mm_resid_ln(h, w["wd"], x1, w["ln2g"], w["ln2b"], name="mm_down_ln")
    saved = dict(xb=xb, pq=pq, conv_out=conv_out, proj=proj, phi=phi, ba=ba, qn=qn, kn=kn, vv=vv, gb=gb, o=o, st=st, tinv=tinv, wy_u=wy_u, wy_w=wy_w,
                 ya=ya, yb=yb,
                 pa=pa, pb=pb, m=m, pre1=pre1, x1b=x1b, hg=hg, hu=hu, h=h, pre2=pre2)
    return x2, x2b, saved, w


def _layer_bwd(dpre2, ln2_grads, w, s, on_part=None):
    g = {}
    started = lambda part: on_part(part, g) if on_part is not None else None
    after = lambda v, token: v if token is None else v + token.astype(v.dtype)
    g["ln2g"], g["ln2b"] = ln2_grads
    dhg, dhu = _mm_swiglu_bwd(dpre2, w["wd"], s["hg"], s["hu"])
    g["wd"] = _mm(s["h"], dpre2, mode="tn", name="mm_tn_down", tm=1536, tk=1024, out_dtype=BF)
    dx1 = _mm(dhg, w["wg"], mode="nt", name="mm_nt_gu", pair=(dhu, w["wu"]), add=dpre2, add_scale=ALPHA, tm=1024, tk=1536)
    g["wg"] = _mm(s["x1b"], dhg, mode="tn", name="mm_tn_gu", tm=1024, tn=1536, tk=2048, out_dtype=BF)
    g["wu"] = _mm(s["x1b"], dhu, mode="tn", name="mm_tn_gu", tm=1024, tn=1536, tk=2048, out_dtype=BF)
    dpre1, g["ln1g"], g["ln1b"] = _ln_bwd(s["pre1"], w["ln1g"], w["ln1b"], dx1)
    g["wo"] = _mm(s["m"], dpre1, mode="tn", name="mm_tn_sq", tm=1024, tk=1024, out_dtype=BF)
    dpa, dpb, dproj = _mm_gate_merge_bwd(dpre1, w["wo"], s["pa"], s["pb"], s["proj"])
    dya = _mm(dpa, w["wpa"], mode="nt", name="mm_nt_sq")
    g["wpa"] = _mm(s["ya"], dpa, mode="tn", name="mm_tn_sq", tm=1024, tk=1024, out_dtype=BF)
    dyb = _mm(dpb, w["wpb"], mode="nt", name="mm_nt_sq")
    g["wpb"] = _mm(s["yb"], dpb, mode="tn", name="mm_tn_sq", tm=1024, tk=1024, out_dtype=BF)
    do, dproj, g["onw"], g["sg"], g["sb"], g["ws"], g["bst"] = _mix_prep_bwd(
        s["o"], s["proj"], s["phi"], after(w["onw"], started("late")), w["sg"], w["sb"], w["ws"], w["bst"], dya, dyb, dproj)
    dqn, dkn, dvv, dgb = _delta_bwd(s["qn"], s["kn"], s["vv"], s["gb"], s["st"], s["tinv"], s["wy_u"], s["wy_w"], do)
    dc, dba, g["convw"], g["arow"], g["dtrow"] = _qkv_prep_bwd(
        s["pq"], s["conv_out"], s["ba"], w["arow"], w["dtrow"], dqn, dkn, dvv, dgb)
    dproj = _conv_bwd(dc, w["convw"], dproj)
    g["win"] = _mm(s["xb"], dproj, mode="tn", name="mm_tn_in", tm=1024, tn=1024, tk=2048, out_dtype=BF)
    g["wba"] = _mm(s["xb"], dba, mode="tn", name="mm_tn_ba", tm=1024, tn=LANES, tk=1024, out_dtype=BF)
    dx = _mm(dba, after(w["wba"], started("early")), mode="nt", name="mm_nt_ba", add=dpre1, add_scale=ALPHA, tm=1024)
    dx = _mm(dproj, w["win"], mode="nt", name="mm_nt_in", add=dx, add_scale=1.0, tm=1024, tk=2048)
    return dx, g


def _local_step(x, tgt, layers, on_grads=None):
    saved, weights = [], []
    xb = x.astype(BF)
    for layer in layers:
        x, xb, s, w = _layer_fwd(x, xb, *layer(x))
        saved.append(s)
        weights.append(w)
    last = len(layers) - 1
    dpre2, dg, db, lacc = _loss_ln_bwd(x, tgt, saved[last]["pre2"], weights[last]["ln2g"], weights[last]["ln2b"])
    grads = [None] * len(layers)
    for l in reversed(range(len(layers))):
        on_part = functools.partial(on_grads, l) if on_grads is not None else None
        dx, grads[l] = _layer_bwd(dpre2, (dg, db), weights[l], saved[l], on_part)
        if l > 0:
            dpre2, dg, db = _ln_bwd(saved[l - 1]["pre2"], weights[l - 1]["ln2g"], weights[l - 1]["ln2b"], dx)
    return lacc[0, 0], dx, grads


_QKVZ = 4 * D_MODEL
_BA = 2 * N_HEADS


WEIGHT_NAMES = ("w_in", "conv_w", "a_log", "dt_bias", "o_norm_w", "sgu_ln_g", "sgu_ln_b", "w_s", "b_s", "w_pa", "w_pb",
                "w_o", "ln1_g", "ln1_b", "w_ffn_gate", "w_ffn_up", "w_ffn_down", "ln2_g", "ln2_b")
WIRE = ("w_in", "w_ffn_gate", "w_ffn_up", "w_ffn_down", "w_pa", "w_pb", "w_o", "conv_w")
SMALL = (("a_log", N_HEADS), ("dt_bias", N_HEADS), ("o_norm_w", D_HEAD), ("sgu_ln_g", D_MODEL), ("sgu_ln_b", D_MODEL),
         ("w_s", SGU_GROUPS * SGU_BLOCK * SGU_BLOCK), ("b_s", SGU_GROUPS * SGU_BLOCK),
         ("ln1_g", D_MODEL), ("ln1_b", D_MODEL), ("ln2_g", D_MODEL), ("ln2_b", D_MODEL))
SMALL_ROWS = -(-sum(n for _, n in SMALL) // (LANES * SUBLANES)) * SUBLANES
N_MAIN_TILES = (N_IN - _BA) // D_MODEL
ADAM_TILES = dict(w_in=(128, "adamw_in"), w_ffn_gate=(256, "adamw_ffn_cols"), w_ffn_up=(256, "adamw_ffn_cols"),
                  w_ffn_down=(32, "adamw_ffn_rows"), w_pa=(128, "adamw_sq"), w_pb=(128, "adamw_sq"), w_o=(128, "adamw_sq"),
                  conv_w=(CONV_K, "adamw_conv"))


def _pad_to(a, axis, size):
    pads = [(0, 0)] * a.ndim
    pads[axis] = (0, size - a.shape[axis])
    return jnp.pad(a, pads)


def _wire_blocks(p):
    return dict(
        w_in=_pad_to(p["w_in"].astype(BF), 2, IN_PAD),
        w_ffn_gate=_pad_to(p["w_ffn_gate"].astype(BF), 2, FFN_PAD), w_ffn_up=_pad_to(p["w_ffn_up"].astype(BF), 2, FFN_PAD),
        w_ffn_down=_pad_to(p["w_ffn_down"].astype(BF), 1, FFN_PAD),
        w_pa=p["w_pa"].astype(BF), w_pb=p["w_pb"].astype(BF), w_o=p["w_o"].astype(BF),
        conv_w=_pad_to(p["conv_w"], 1, SUBLANES),
    )


def _by_columns(blocks):
    n, r, c = blocks.shape
    return jnp.transpose(blocks, (1, 0, 2)).reshape(r, n * c)


def _to_slots(full, c):
    r = full.shape[0]
    return jnp.transpose(full.reshape(r, N_DEV, c), (1, 0, 2))


def _lane_row(v, at):
    return jnp.pad(v[None], ((0, 0), (at, LANES - at - v.shape[0])))


EARLY = ("w_in", "conv_w")
LATE = ("w_pa", "w_pb", "w_o", "w_ffn_gate", "w_ffn_up", "w_ffn_down")


def _early_weights(stacks, p, l):
    return dict(
        win=_perm_in(stacks["w_in"], D_MODEL, N_MAIN_TILES), wba=_perm_in(stacks["w_in"], LANES, 1),
        convw=_by_columns(stacks["conv_w"][:, :CONV_K]),
        arow=_lane_row(p["a_log"][l], N_HEADS), dtrow=_lane_row(p["dt_bias"][l], N_HEADS),
        onw=p["o_norm_w"][l][None], sg=p["sgu_ln_g"][l][None], sb=p["sgu_ln_b"][l][None],
        ws=p["w_s"][l], bst=_pad_to(p["b_s"][l].T, 1, LANES),
        ln1g=p["ln1_g"][l][None], ln1b=p["ln1_b"][l][None], ln2g=p["ln2_g"][l][None], ln2b=p["ln2_b"][l][None],
    )


def _late_weights(stacks):
    return dict(
        wpa=stacks["w_pa"].reshape(D_MODEL, D_MODEL), wpb=stacks["w_pb"].reshape(D_MODEL, D_MODEL),
        wo=stacks["w_o"].reshape(D_MODEL, D_MODEL),
        wg=_by_columns(stacks["w_ffn_gate"]), wu=_by_columns(stacks["w_ffn_up"]),
        wd=stacks["w_ffn_down"].reshape(FFN_K, D_MODEL),
    )


def _small_pack(parts):
    flat = jnp.concatenate([parts[n].reshape(-1) for n, _ in SMALL])
    return _pad_to(flat, 0, SMALL_ROWS * LANES).reshape(SMALL_ROWS, LANES)


def _small_unpack(rows, like):
    flat, out, off = rows.reshape(-1), {}, 0
    for n, size in SMALL:
        out[n] = flat[off:off + size].reshape(like[n].shape[1:])
        off += size
    return out


def _late_slots(g):
    slots = dict(
        w_ffn_gate=_to_slots(g["wg"], FFN_PAD), w_ffn_up=_to_slots(g["wu"], FFN_PAD),
        w_ffn_down=g["wd"].reshape(N_DEV, FFN_PAD, D_MODEL),
        w_pa=g["wpa"].reshape(N_DEV, D_MODEL // N_DEV, D_MODEL), w_pb=g["wpb"].reshape(N_DEV, D_MODEL // N_DEV, D_MODEL),
        w_o=g["wo"].reshape(N_DEV, D_MODEL // N_DEV, D_MODEL),
    )
    return [slots[n] for n in LATE]


def _early_slots(g):
    slots = [_perm_out(g["win"], g["wba"]), _pad_to(_to_slots(g["convw"][:CONV_K], 3 * D_MODEL // N_DEV), 1, SUBLANES)]
    small = _small_pack(dict(
        a_log=g["arow"][0, N_HEADS:2 * N_HEADS], dt_bias=g["dtrow"][0, N_HEADS:2 * N_HEADS], o_norm_w=g["onw"][0],
        sgu_ln_g=g["sg"][0], sgu_ln_b=g["sb"][0], w_s=g["ws"], b_s=g["bst"][:, :SGU_GROUPS].T,
        ln1_g=g["ln1g"][0], ln1_b=g["ln1b"][0], ln2_g=g["ln2g"][0], ln2_b=g["ln2b"][0]))
    return slots, small


def _in_tile_start(j, tile_w):
    if tile_w == LANES:
        return jnp.int32(_QKVZ)
    return j * D_MODEL + jnp.where(j >= _QKVZ // D_MODEL, _BA, 0)


def _select(rows_iota, cols_iota, dev, start, valid):
    hit = (rows_iota + (dev * IN_BLOCK - start) == cols_iota) & (rows_iota < IN_BLOCK) & (cols_iota < valid)
    return jnp.where(hit, 1.0, 0.0).astype(BF)


def _perm_in(stack, tile_w, n_tiles):
    valid = _BA if tile_w == LANES else tile_w

    def first_dev(j):
        return lax.div(_in_tile_start(j, tile_w), jnp.int32(IN_BLOCK))

    def body(w_ref, o_ref, acc_ref):
        j, k = pl.program_id(0), pl.program_id(1)
        sel = _select(_iota((IN_PAD, tile_w), 0), _iota((IN_PAD, tile_w), 1), first_dev(j) + k,
                      _in_tile_start(j, tile_w), valid)
        part = jnp.dot(w_ref[0], sel, preferred_element_type=F32)

        @pl.when(k == 0)
        def _():
            acc_ref[...] = part

        @pl.when(k == 1)
        def _():
            o_ref[...] = (acc_ref[...] + part).astype(BF)

    est = _nbytes((D_MODEL, IN_PAD), BF) + 3 * _nbytes((D_MODEL, tile_w), F32) + 2 * _nbytes((IN_PAD, tile_w), F32)
    return pl.pallas_call(
        body, name="perm_in" if tile_w != LANES else "perm_in_ba", grid=(n_tiles, 2),
        in_specs=[pl.BlockSpec((1, D_MODEL, IN_PAD), lambda j, k: (jnp.minimum(first_dev(j) + k, N_DEV - 1), 0, 0))],
        out_specs=pl.BlockSpec((D_MODEL, tile_w), lambda j, k: (0, j)),
        out_shape=jax.ShapeDtypeStruct((D_MODEL, n_tiles * tile_w), BF),
        scratch_shapes=[pltpu.VMEM((D_MODEL, tile_w), F32)],
        compiler_params=_cparams(est, ("parallel", "arbitrary")),
    )(stack)


def _perm_out(dmain, dba):
    def tile(d, s):
        c0 = d * IN_BLOCK
        first = lax.div(c0 - jnp.where(c0 < _QKVZ, 0, jnp.minimum(c0 - _QKVZ, _BA)), jnp.int32(D_MODEL))
        return jnp.minimum(first + jnp.minimum(s, 1), N_MAIN_TILES - 1)

    def body(dm_ref, db_ref, o_ref, acc_ref):
        d, s = pl.program_id(0), pl.program_id(1)

        @pl.when(s == 0)
        def _():
            acc_ref[...] = jnp.zeros_like(acc_ref)

        start = _in_tile_start(tile(d, s), D_MODEL)
        overlaps = (start < (d + 1) * IN_BLOCK) & (d * IN_BLOCK < start + D_MODEL)

        @pl.when((s < 2) & overlaps)
        def _():
            sel = _select(_iota((D_MODEL, IN_PAD), 1), _iota((D_MODEL, IN_PAD), 0), d, start, D_MODEL)
            acc_ref[...] += jnp.dot(dm_ref[...], sel, preferred_element_type=F32)

        @pl.when(s == 2)
        def _():
            sel = _select(_iota((LANES, IN_PAD), 1), _iota((LANES, IN_PAD), 0), d, jnp.int32(_QKVZ), _BA)
            o_ref[0] = (acc_ref[...] + jnp.dot(db_ref[...], sel, preferred_element_type=F32)).astype(BF)

    est = 2 * _nbytes((D_MODEL, D_MODEL), BF) + 4 * _nbytes((D_MODEL, IN_PAD), F32)
    return pl.pallas_call(
        body, name="perm_out", grid=(N_DEV, 3),
        in_specs=[pl.BlockSpec((D_MODEL, D_MODEL), lambda d, s: (0, tile(d, s))),
                  pl.BlockSpec((D_MODEL, LANES), lambda d, s: (0, 0))],
        out_specs=pl.BlockSpec((1, D_MODEL, IN_PAD), lambda d, t: (d, 0, 0)),
        out_shape=jax.ShapeDtypeStruct((N_DEV, D_MODEL, IN_PAD), BF),
        scratch_shapes=[pltpu.VMEM((D_MODEL, IN_PAD), F32)],
        compiler_params=_cparams(est, ("parallel", "arbitrary")),
    )(dmain, dba)


def _mesh_place():
    x, y, c = (lax.axis_index(a) for a in MESH_AXES)
    return x, y, c


def _slot(x, y, c):
    return 4 * x + 2 * y + c


def _peer(place, j):
    x, y, c = place
    return (1 - x if j & 4 else x, 1 - y if j & 2 else y, 1 - c if j & 1 else c)


_HBM = pl.BlockSpec(memory_space=pltpu.HBM)
_SEM = pl.BlockSpec(memory_space=pltpu.SEMAPHORE)
_EFFECT = pltpu.SideEffectType.DATAFLOW_SIDE_EFFECTING


def _remote_copy(src_ref, land_ref, slot, per_slot, pslot, sems, u, j, peer):
    return pltpu.make_async_remote_copy(
        src_ref=src_ref.at[pslot] if per_slot else src_ref, dst_ref=land_ref.at[slot],
        send_sem=sems[0].at[u * (N_DEV - 1) + j - 1], recv_sem=sems[1].at[u * (N_DEV - 1) + j - 1],
        device_id=peer, device_id_type=pl.DeviceIdType.MESH)


def _own_copy(src_ref, land_ref, me, per_slot, sems, u):
    return pltpu.make_async_copy(src_ref.at[me] if per_slot else src_ref, land_ref.at[me], sems[2].at[u])


def _exchange_start(name, srcs, per_slot):
    n = len(srcs)
    lands = [jax.ShapeDtypeStruct(s.shape if p else (N_DEV,) + s.shape, s.dtype) for s, p in zip(srcs, per_slot)]

    def body(*refs):
        src_refs, sems, land_refs, token = refs[:n], refs[n:n + 3], refs[2 * n + 3:3 * n + 3], refs[-1]
        place = _mesh_place()
        me = _slot(*place)
        for u in range(n):
            _own_copy(src_refs[u], land_refs[u], me, per_slot[u], sems, u).start()
            for j in range(1, N_DEV):
                peer = _peer(place, j)
                _remote_copy(src_refs[u], land_refs[u], me, per_slot[u], _slot(*peer), sems, u, j, peer).start()
        token[...] = jnp.zeros_like(token)

    hbm = lambda a: pltpu.HBM(a.shape, a.dtype)
    sem = pltpu.SemaphoreType.DMA((n * (N_DEV - 1),))
    outs = pl.pallas_call(
        body, name=name,
        out_shape=(sem, sem, pltpu.SemaphoreType.DMA((n,)), *[hbm(a) for a in srcs], *[hbm(a) for a in lands],
                   jax.ShapeDtypeStruct((SUBLANES, LANES), F32)),
        in_specs=[_HBM] * n, out_specs=(_SEM, _SEM, _SEM, *[_HBM] * (2 * n), pl.BlockSpec(memory_space=pltpu.VMEM)),
        input_output_aliases={i: 3 + i for i in range(n)},
        compiler_params=pltpu.CompilerParams(has_side_effects=_EFFECT),
    )(*[pltpu.with_memory_space_constraint(a, pltpu.HBM) for a in srcs])
    return tuple(outs[:3]), list(outs[3:3 + n]), list(outs[3 + n:3 + 2 * n]), outs[-1]


def _exchange_wait(name, sems, srcs, lands, units, per_slot, after):
    m = len(units)

    def body(*refs):
        src_refs, land_refs, sem_refs = refs[:m], refs[m:2 * m], refs[2 * m:2 * m + 3]
        place = _mesh_place()
        me = _slot(*place)
        for i, u in enumerate(units):
            _own_copy(src_refs[i], land_refs[i], me, per_slot[u], sem_refs, u).wait()
            for j in range(1, N_DEV):
                peer = _peer(place, j)
                pslot = _slot(*peer)
                cp = _remote_copy(src_refs[i], land_refs[i], pslot, per_slot[u], pslot, sem_refs, u, j, peer)
                cp.wait_send()
                cp.wait_recv()

    hbm = lambda a: pltpu.HBM(a.shape, a.dtype)
    outs = pl.pallas_call(
        body, name=name, out_shape=tuple(hbm(a) for a in list(srcs) + list(lands)),
        in_specs=[_HBM] * (2 * m) + [_SEM] * 3 + [pl.BlockSpec(memory_space=pl.ANY)], out_specs=tuple([_HBM] * (2 * m)),
        input_output_aliases={i: i for i in range(2 * m)},
        compiler_params=pltpu.CompilerParams(has_side_effects=_EFFECT),
    )(*srcs, *lands, *sems, after)
    return list(outs[m:])


def _adam_update(g, w, m, v):
    m = ADAM_B1 * m + (1.0 - ADAM_B1) * g
    v = ADAM_B2 * v + (1.0 - ADAM_B2) * jnp.square(g)
    m_hat = m / (1.0 - ADAM_B1 ** ADAM_STEP)
    v_hat = v / (1.0 - ADAM_B2 ** ADAM_STEP)
    return -ADAM_LR * (m_hat / (jnp.sqrt(v_hat) + ADAM_EPS) + ADAM_WD * w), m, v


def _adamw(recvs, w, m, v, *, tr, name):
    L, R, C = w.shape
    rp = max(tr, SUBLANES * (4 // jnp.dtype(recvs[0].dtype).itemsize))
    Cp = recvs[0].shape[2]

    def body(*refs):
        r_refs, (w_ref, m_ref, v_ref, g_ref, d_ref, nm_ref, nv_ref) = refs[:L], refs[L:]
        for l in range(L):
            @pl.when(pl.program_id(0) == l)
            def _(r_ref=r_refs[l]):
                g = r_ref[0, :tr, :C].astype(F32)
                for s in range(1, N_DEV):
                    g = g + r_ref[s, :tr, :C].astype(F32)
                d, nm, nv = _adam_update(g, w_ref[0], m_ref[0], v_ref[0])
                g_ref[0], d_ref[0], nm_ref[0], nv_ref[0] = g, d, nm, nv

    blk = pl.BlockSpec((1, tr, C), lambda l, i: (l, i, 0))
    r_specs = [pl.BlockSpec((N_DEV, rp, Cp), lambda l, i, k=k: (0, jnp.where(l == k, i, 0), 0)) for k in range(L)]
    est = 2 * _nbytes((N_DEV, rp, Cp), recvs[0].dtype) + 8 * _nbytes((tr, Cp), F32)
    return pl.pallas_call(
        body, name=name, grid=(L, R // tr),
        in_specs=r_specs + [blk] * 3, out_specs=[blk] * 4,
        out_shape=[jax.ShapeDtypeStruct((L, R, C), F32)] * 4,
        compiler_params=_cparams(est, ("arbitrary", "arbitrary")),
    )(*recvs, w, m, v)


def _adamw_small(recv, w, m, v):
    def body(r_ref, w_ref, m_ref, v_ref, g_ref, d_ref, nm_ref, nv_ref):
        g = r_ref[0]
        for s in range(1, N_DEV):
            g = g + r_ref[s]
        g_ref[...] = g
        d_ref[...], nm_ref[...], nv_ref[...] = _adam_update(g, w_ref[...], m_ref[...], v_ref[...])

    vm = pl.BlockSpec(memory_space=pltpu.VMEM)
    return pl.pallas_call(
        body, name="adamw_small", in_specs=[vm] * 4, out_specs=[vm] * 4,
        out_shape=[jax.ShapeDtypeStruct((SMALL_ROWS, LANES), F32)] * 4,
        compiler_params=_cparams(20 * _nbytes((SMALL_ROWS, LANES), F32)),
    )(recv, w, m, v)


def kernel(x, w_in, conv_w, a_log, dt_bias, o_norm_w, sgu_ln_g, sgu_ln_b, w_s, b_s, w_pa, w_pb, w_o, ln1_g, ln1_b, w_ffn_gate, w_ffn_up, w_ffn_down, ln2_g, ln2_b, loss_target, m_w_in, m_conv_w, m_a_log, m_dt_bias, m_o_norm_w, m_sgu_ln_g, m_sgu_ln_b, m_w_s, m_b_s, m_w_pa, m_w_pb, m_w_o, m_ln1_g, m_ln1_b, m_w_ffn_gate, m_w_ffn_up, m_w_ffn_down, m_ln2_g, m_ln2_b, v_w_in, v_conv_w, v_a_log, v_dt_bias, v_o_norm_w, v_sgu_ln_g, v_sgu_ln_b, v_w_s, v_b_s, v_w_pa, v_w_pb, v_w_o, v_ln1_g, v_ln1_b, v_w_ffn_gate, v_w_ffn_up, v_w_ffn_down, v_ln2_g, v_ln2_b):
    given = dict(locals())
    P = {n: given[n] for n in WEIGHT_NAMES}
    M = {n: given["m_" + n] for n in WEIGHT_NAMES}
    V = {n: given["v_" + n] for n in WEIGHT_NAMES}

    wire = _wire_blocks(P)
    units = [(n, l) for l in range(DEPTH) for n in EARLY + LATE]
    whole = [False] * len(units)
    g_sems, g_srcs, g_lands, g_token = _exchange_start("gather_start", [wire[n][l] for n, l in units], whole)

    def gathered(name, names, l, after):
        idx = [units.index((n, l)) for n in names]
        got = _exchange_wait(name, g_sems, [g_srcs[i] for i in idx], [g_lands[i] for i in idx], idx, whole, after)
        return dict(zip(names, got))

    def layer(l):
        def weights(x_in):
            after = g_token if l == 0 else x_in
            early = _early_weights(gathered(f"gather_wait_early{l}", EARLY, l, after), P, l)
            return early, lambda ya: _late_weights(gathered(f"gather_wait_late{l}", LATE, l, ya))
        return weights

    pending = {}

    def on_grads(l, part, g):
        if part == "late":
            srcs, names = _late_slots(g), LATE
            per_slot = [True] * len(srcs)
        else:
            slots, small = _early_slots(g)
            srcs, names = slots + [small], EARLY + ("small",)
            per_slot = [True] * len(slots) + [False]
        sems, s_thru, l_thru, token = _exchange_start(f"exchange_start_{part}{l}", srcs, per_slot)
        pending[l, part] = (names, sems, s_thru, l_thru, per_slot)
        return token[0, 0]

    loss_local, dx, _ = _local_step(x[0], loss_target[0], [layer(l) for l in range(DEPTH)], on_grads)
    loss = lax.psum(loss_local, MESH_AXES)

    recv = [{} for _ in range(DEPTH)]
    for l in reversed(range(DEPTH)):
        for part in ("late", "early"):
            names, sems, s_thru, l_thru, per_slot = pending[l, part]
            got = _exchange_wait(f"exchange_wait_{part}{l}", sems, s_thru, l_thru, list(range(len(s_thru))), per_slot, dx)
            recv[l].update(zip(names, got))

    out = {}
    for n in WIRE:
        tr, name = ADAM_TILES[n]
        out[n] = _adamw([recv[l][n] for l in range(DEPTH)], P[n], M[n], V[n], tr=tr, name=name)
    small = [_adamw_small(recv[l]["small"], *[_small_pack({n: T[n][l] for n, _ in SMALL}) for T in (P, M, V)])
             for l in range(DEPTH)]
    for n, _ in SMALL:
        out[n] = [jnp.stack([_small_unpack(small[l][i], P)[n] for l in range(DEPTH)]) for i in range(4)]
    return (loss, dx[None], *[out[n][i] for i in range(4) for n in WEIGHT_NAMES])
```

```python
import functools
import math

import jax
import jax.numpy as jnp
from jax import lax
from jax.experimental import pallas as pl
from jax.experimental.pallas import tpu as pltpu

F32 = jnp.float32
BF = jnp.bfloat16
HIGHEST = lax.Precision.HIGHEST

D_MODEL = 1024
DEPTH = 2
N_HEADS = 8
D_HEAD = 128
CONV_K = 4
SGU_BLOCK = 128
SGU_GROUPS = 8
SGU_CHUNK = 64
FFN_HIDDEN = 2816
N_IN = 8208
N_DEV = 8
IN_BLOCK, IN_PAD = N_IN // N_DEV, 1152
FFN_BLOCK, FFN_PAD = FFN_HIDDEN // N_DEV, 384
FFN_K = N_DEV * FFN_PAD
ALPHA = (2 * DEPTH) ** 0.25
LN_EPS = 1e-5
RMS_EPS = 1e-6
ADAM_LR, ADAM_B1, ADAM_B2, ADAM_EPS, ADAM_WD, ADAM_STEP = 0.001, 0.9, 0.999, 1e-08, 0.01, 10

MESH_AXES = ("x", "y", "c")
DELTA_CHUNK = 128
DELTA_HEADS_PER_STEP = 8
LANES = 128
SUBLANES = 8
VMEM_BYTES = 64 * 1024 * 1024
HALO = SUBLANES
HALO_BF = 2 * SUBLANES


def _cparams(est_bytes, dims=None):
    limit = int(min(max(2 * est_bytes + (8 << 20), 32 << 20), VMEM_BYTES - (6 << 20)))
    kw = dict(vmem_limit_bytes=limit)
    if dims is not None:
        kw["dimension_semantics"] = dims
    return pltpu.CompilerParams(**kw)


def _nbytes(shape, dtype):
    return math.prod(shape) * jnp.dtype(dtype).itemsize


def _dims(kind, ndim):
    lhs, rhs = {"nn": (1, 0), "nt": (1, 1), "tn": (0, 0)}[kind]
    b = ndim - 2
    return (((lhs + b,), (rhs + b,)), (tuple(range(b)), tuple(range(b))))


def _mxu(a, b, kind):
    return lax.dot_general(a, b, _dims(kind, a.ndim), preferred_element_type=F32)


def _dot(a, b):
    return _mxu(a.astype(BF), b.astype(BF), "nn")


def _dot_nt(a, b):
    return _mxu(a.astype(BF), b.astype(BF), "nt")


def _dot_tn(a, b):
    return _mxu(a.astype(BF), b.astype(BF), "tn")


def _split(a):
    hi = a.astype(BF)
    return hi, (a - hi.astype(F32)).astype(BF)


def _dot3(a, b, kind):
    (ah, al), (bh, bl) = _split(a), _split(b)
    return _mxu(ah, bh, kind) + (_mxu(ah, bl, kind) + _mxu(al, bh, kind))


def _dotf(a, b):
    return _dot3(a, b, "nn")


def _dotf_nt(a, b):
    return _dot3(a, b, "nt")


def _dot01(sel, x, kind="nn"):
    s = jnp.broadcast_to(sel.astype(BF), x.shape[:-2] + sel.shape)
    h1 = x.astype(BF)
    r1 = x - h1.astype(F32)
    h2 = r1.astype(BF)
    h3 = (r1 - h2.astype(F32)).astype(BF)
    return _mxu(s, h1, kind) + (_mxu(s, h2, kind) + _mxu(s, h3, kind))


def _sigmoid(x):
    return 0.5 * jnp.tanh(0.5 * x) + 0.5


def _silu(x):
    return x * _sigmoid(x)


def _silu_and_grad(x):
    s = _sigmoid(x)
    return x * s, s * (1.0 + x * (1.0 - s))


def _softplus(x):
    return jnp.maximum(x, 0.0) + jnp.log1p(jnp.exp(-jnp.abs(x)))


def _ln(x, g, b):
    mu = jnp.mean(x, -1, keepdims=True)
    xc = x - mu
    var = jnp.mean(xc * xc, -1, keepdims=True)
    return xc * lax.rsqrt(var + LN_EPS) * g + b


def _iota(shape, dim):
    return lax.broadcasted_iota(jnp.int32, shape, dim)


def _tile(n, pref, align):
    if n <= pref:
        return n
    t = (pref // align) * align
    while t >= align:
        if n % t == 0:
            return t
        t -= align
    raise ValueError(f"no tile for {n} (pref {pref}, align {align})")


def _bcast_rows(v, rows=SUBLANES):
    return jnp.broadcast_to(v, (rows, v.shape[-1]))


def _mm(a, b, *, mode, name, out_dtype=F32, add=None, add_scale=1.0, tm=512, tn=1024, tk=1024, cols=None, pair=None):
    if mode == "nn":
        (M, K), N = a.shape, b.shape[1]
    elif mode == "nt":
        (M, K), N = a.shape, b.shape[0]
    else:
        (K, M), N = a.shape, b.shape[1]
    col0 = 0
    if cols is not None:
        col0, N = cols
    tm = _tile(M, tm, LANES if mode == "tn" else SUBLANES * 2)
    tn = _tile(N, tn, LANES)
    tk = _tile(K, tk, LANES)
    nk = K // tk
    j0 = col0 // tn
    if mode == "nn":
        a_spec = pl.BlockSpec((tm, tk), lambda i, j, k: (i, k))
        b_spec = pl.BlockSpec((tk, tn), lambda i, j, k: (k, j + j0))
        dot = _dot
    elif mode == "nt":
        a_spec = pl.BlockSpec((tm, tk), lambda i, j, k: (i, k))
        b_spec = pl.BlockSpec((tn, tk), lambda i, j, k: (j, k))
        dot = _dot_nt
    else:
        a_spec = pl.BlockSpec((tk, tm), lambda i, j, k: (k, i))
        b_spec = pl.BlockSpec((tk, tn), lambda i, j, k: (k, j))
        dot = _dot_tn
    o_spec = pl.BlockSpec((tm, tn), lambda i, j, k: (i, j))
    has_add = add is not None

    n_ab = 2 if pair is None else 4

    def body(*refs):
        ab, (o_ref, acc_ref) = refs[:n_ab], refs[-2:]
        add_ref = refs[n_ab] if has_add else None
        k = pl.program_id(2)
        part = dot(ab[0][...], ab[1][...])
        if pair is not None:
            part = part + dot(ab[2][...], ab[3][...])

        def finish(total):
            if has_add:
                total = total + add_scale * add_ref[...]
            o_ref[...] = total.astype(out_dtype)

        if nk == 1:
            finish(part)
        else:
            @pl.when(k == 0)
            def _():
                acc_ref[...] = part

            @pl.when(jnp.logical_and(k > 0, k < nk - 1))
            def _():
                acc_ref[...] += part

            @pl.when(k == nk - 1)
            def _():
                finish(acc_ref[...] + part)

    in_specs = [a_spec, b_spec] * (n_ab // 2) + ([o_spec] if has_add else [])
    args = (a, b) + (tuple(pair) if pair is not None else ()) + ((add,) if has_add else ())
    est = ((n_ab // 2) * (_nbytes((tm, tk), a.dtype) + _nbytes((tk, tn), b.dtype)) + 2 * _nbytes((tm, tn), F32)
           + (_nbytes((tm, tn), F32) if has_add else 0)) + 2 * _nbytes((tm, tn), F32)
    return pl.pallas_call(
        body, name=name,
        grid=(M // tm, N // tn, nk),
        in_specs=in_specs, out_specs=o_spec,
        out_shape=jax.ShapeDtypeStruct((M, N), out_dtype),
        scratch_shapes=[pltpu.VMEM((tm, tn) if nk > 1 else (SUBLANES, LANES), F32)],
        compiler_params=_cparams(est, ("parallel", "parallel", "arbitrary")),
    )(*args)


def _shifted(xt, halo, first):
    halo = jnp.where(first, 0.0, halo)
    xc = jnp.concatenate([halo, xt], axis=0)
    return [xt] + [pltpu.roll(xc, s, 0)[HALO:] for s in range(1, CONV_K)]


def _conv_taps(shifted, w_ref):
    out = shifted[0] * w_ref[CONV_K - 1:CONV_K, :]
    for s in range(1, CONV_K):
        out = out + shifted[s] * w_ref[CONV_K - 1 - s:CONV_K - s, :]
    return out


def _gates(ba, arow, dtrow):
    lane = _iota(ba.shape, 1)
    beta = _sigmoid(ba)
    g = -jnp.exp(arow) * _softplus(ba + dtrow)
    return jnp.where(lane < N_HEADS, beta, jnp.where(lane < 2 * N_HEADS, g, 0.0))


def _l2n(x):
    return x * lax.rsqrt(jnp.sum(x * x, -1, keepdims=True) + RMS_EPS)


def _qkv_prep(proj, ba, convw, arow, dtrow, *, tm=256):
    S = proj.shape[0]
    tm = _tile(S, tm, SUBLANES)
    W3 = 3 * D_MODEL
    hb = tm // HALO

    def body(xt_ref, halo_ref, ba_ref, w_ref, a_ref, dt_ref, q_ref, k_ref, v_ref, gb_ref, c_ref):
        c = _conv_taps(_shifted(xt_ref[...], halo_ref[...], pl.program_id(0) == 0), w_ref)
        c_ref[...] = c
        c = _silu(c)
        for h in range(N_HEADS):
            lo = h * D_HEAD
            q_ref[:, lo:lo + D_HEAD] = _l2n(c[:, lo:lo + D_HEAD])
            k_ref[:, lo:lo + D_HEAD] = _l2n(c[:, D_MODEL + lo:D_MODEL + lo + D_HEAD])
        v_ref[...] = c[:, 2 * D_MODEL:]
        gb_ref[...] = _gates(ba_ref[...], a_ref[...], dt_ref[...])

    row = lambda w, col=0: pl.BlockSpec((tm, w), lambda i: (i, col))
    full = lambda shape: pl.BlockSpec(shape, lambda i: (0,) * len(shape))
    est = 4 * _nbytes((tm, W3), F32)
    return pl.pallas_call(
        body, name="qkv_prep", grid=(S // tm,),
        in_specs=[row(W3), pl.BlockSpec((HALO, W3), lambda i: (jnp.maximum(i * hb - 1, 0), 0)), row(LANES),
                  full((CONV_K, W3)), full((1, LANES)), full((1, LANES))],
        out_specs=[row(D_MODEL), row(D_MODEL), row(D_MODEL), row(LANES), row(W3)],
        out_shape=[jax.ShapeDtypeStruct((S, D_MODEL), F32)] * 3 + [jax.ShapeDtypeStruct((S, LANES), F32),
                                                                   jax.ShapeDtypeStruct((S, W3), F32)],
        compiler_params=_cparams(est, ("arbitrary",)),
    )(proj, proj, ba, convw, arow, dtrow)


def _qkv_prep_bwd(proj, conv_out, ba, arow, dtrow, dq, dk, dv, dgb, *, tm=256):
    S = proj.shape[0]
    tm = _tile(S, tm, SUBLANES * 2)
    W3 = 3 * D_MODEL
    hb = tm // HALO

    def body(xt_ref, halo_ref, c_ref, ba_ref, a_ref, dt_ref, dq_ref, dk_ref, dv_ref, dgb_ref,
             dcb_ref, dba_ref, dw_ref, da_ref, ddt_ref, dc_ref):
        i = pl.program_id(0)

        @pl.when(i == 0)
        def _():
            dw_ref[...] = jnp.zeros_like(dw_ref)
            da_ref[...] = jnp.zeros_like(da_ref)
            ddt_ref[...] = jnp.zeros_like(ddt_ref)

        shifted = _shifted(xt_ref[...], halo_ref[...], i == 0)
        a, ds = _silu_and_grad(c_ref[...])
        for h in range(N_HEADS):
            for base, d_ref in ((0, dq_ref), (D_MODEL, dk_ref)):
                lo = base + h * D_HEAD
                _, vj = jax.vjp(_l2n, a[:, lo:lo + D_HEAD])
                (dx,) = vj(d_ref[:, h * D_HEAD:(h + 1) * D_HEAD])
                dc_ref[:, lo:lo + D_HEAD] = dx * ds[:, lo:lo + D_HEAD]
        dc_ref[:, 2 * D_MODEL:] = dv_ref[...] * ds[:, 2 * D_MODEL:]
        dc = dc_ref[...]
        dcb_ref[...] = dc.astype(BF)
        for s in range(CONV_K):
            kk = CONV_K - 1 - s
            dw_ref[kk:kk + 1, :] += jnp.sum(dc * shifted[s], axis=0, keepdims=True)
        _, vj = jax.vjp(_gates, ba_ref[...], a_ref[...], dt_ref[...])
        dba, da, ddt = vj(dgb_ref[...])
        dba_ref[...] = dba.astype(BF)
        da_ref[...] += _bcast_rows(da)
        ddt_ref[...] += _bcast_rows(ddt)

    row = lambda w, col=0: pl.BlockSpec((tm, w), lambda i: (i, col))
    full = lambda shape: pl.BlockSpec(shape, lambda i: (0,) * len(shape))
    est = 8 * _nbytes((tm, W3), F32)
    return pl.pallas_call(
        body, name="qkv_prep_bwd", grid=(S // tm,),
        in_specs=[row(W3), pl.BlockSpec((HALO, W3), lambda i: (jnp.maximum(i * hb - 1, 0), 0)), row(W3), row(LANES),
                  full((1, LANES)), full((1, LANES)),
                  row(D_MODEL), row(D_MODEL), row(D_MODEL), row(LANES)],
        out_specs=[row(W3), row(LANES), full((SUBLANES, W3)), full((SUBLANES, LANES)), full((SUBLANES, LANES))],
        out_shape=[jax.ShapeDtypeStruct((S, W3), BF), jax.ShapeDtypeStruct((S, LANES), BF),
                   jax.ShapeDtypeStruct((SUBLANES, W3), F32), jax.ShapeDtypeStruct((SUBLANES, LANES), F32),
                   jax.ShapeDtypeStruct((SUBLANES, LANES), F32)],
        scratch_shapes=[pltpu.VMEM((tm, W3), F32)],
        compiler_params=_cparams(est, ("arbitrary",)),
    )(proj, proj, conv_out, ba, arow, dtrow, dq, dk, dv, dgb)


def _conv_bwd(dc, convw, dproj, *, tm=256):
    S, W3 = dc.shape
    tm = _tile(S, tm, HALO_BF)
    hb = tm // HALO_BF
    nt = S // tm

    def body(dc_ref, nxt_ref, w_ref, dproj_ref, o_ref):
        last = pl.program_id(0) == nt - 1
        nxt = jnp.where(last, 0.0, nxt_ref[...].astype(F32))
        cur = dc_ref[...].astype(F32)
        xc = jnp.concatenate([cur, nxt], axis=0)
        out = cur * w_ref[CONV_K - 1:CONV_K, :]
        for s in range(1, CONV_K):
            out = out + pltpu.roll(xc, tm + HALO_BF - s, 0)[:tm] * w_ref[CONV_K - 1 - s:CONV_K - s, :]
        o_ref[...] = out.astype(BF)

    est = 5 * _nbytes((tm, W3), F32)
    return pl.pallas_call(
        body, name="conv_bwd", grid=(nt,),
        in_specs=[pl.BlockSpec((tm, W3), lambda i: (i, 0)),
                  pl.BlockSpec((HALO_BF, W3), lambda i: (jnp.minimum((i + 1) * hb, S // HALO_BF - 1), 0)),
                  pl.BlockSpec((CONV_K, W3), lambda i: (0, 0)), pl.BlockSpec(memory_space=pl.ANY)],
        out_specs=pl.BlockSpec((tm, W3), lambda i: (i, 0)),
        out_shape=jax.ShapeDtypeStruct(dproj.shape, BF),
        input_output_aliases={3: 0},
        compiler_params=_cparams(est, ("parallel",)),
    )(dc, dc, convw, dproj)


NEUMANN_BLOCK = 8


def _inv_unit_lower(A):
    C = A.shape[-1]
    row, col = _iota((C, C), 0), _iota((C, C), 1)
    eye = jnp.where(row == col, 1.0, 0.0).astype(F32)
    Ab = A.astype(BF)
    sh = jnp.int32(int(math.log2(NEUMANN_BLOCK)))
    B = jnp.where(lax.shift_right_logical(row, sh) == lax.shift_right_logical(col, sh), Ab, jnp.zeros_like(Ab))
    B2 = _mxu(B, B, "nn")
    B4 = _dot3(B2, B2, "nn")
    b2h, b2l = _split(B2)
    P = eye - B.astype(F32) + B2 - (_mxu(B, b2h, "nn") + _mxu(B, b2l, "nn"))
    T = P + _dot3(P, B4, "nn")
    b = NEUMANN_BLOCK
    while b < C:
        hi = ~(2 * b - 1)
        off = ((row & hi) == (col & hi)) & ((row & b) != 0) & ((col & b) == 0)
        Aoff = jnp.where(off, Ab, jnp.zeros_like(Ab))
        th, tl = _split(T)
        xh, xl = _split(_mxu(th, Aoff, "nn") + _mxu(tl, Aoff, "nn"))
        T = T - (_mxu(xh, th, "nn") + (_mxu(xh, tl, "nn") + _mxu(xl, th, "nn")))
        b *= 2
    return T


def _delta_common(q, k, g, beta):
    C = q.shape[-2]
    row, col = _iota((C, C), 0), _iota((C, C), 1)
    tril = row >= col
    qs = q * (D_HEAD ** -0.5)
    gcb = _dot01(jnp.where(tril, 1.0, 0.0), jnp.broadcast_to(g, g.shape[:-1] + (LANES,)))
    gc = gcb[..., :1]
    gr = jnp.swapaxes(gcb, -1, -2)
    Dm = jnp.exp(jnp.where(tril, gc - gr, -1e30))
    Dmt = jnp.exp(jnp.where(row <= col, gr - gc, -1e30))
    eg = jnp.exp(gc)
    gl = jnp.sum(jnp.where(_iota((C, 1), 0) == C - 1, gc, 0.0), axis=(-2, -1), keepdims=True)
    el = jnp.exp(gl)
    er = jnp.exp(gl - gc)
    kb = k * beta
    KK = _dot_nt(kb, k)
    QK = _dot_nt(qs, k)
    return dict(row=row, col=col, tril=tril, qs=qs, gc=gc, Dm=Dm, Dmt=Dmt, eg=eg, el=el, er=er, kb=kb, KK=KK, QK=QK)


def _delta_chunk_fwd(S0, q, k, v, g, beta):
    m = _delta_common(q, k, g, beta)
    T = _inv_unit_lower(jnp.where(m["row"] > m["col"], m["KK"] * m["Dm"], 0.0))
    u = _dotf(T, v * beta)
    w = _dotf(T, m["kb"] * m["eg"])
    vn = u - _dot(w, S0)
    o = _dot(m["qs"] * m["eg"], S0) + _dot(m["QK"] * m["Dm"], vn)
    S1 = S0 * m["el"] + _dot_tn(k * m["er"], vn)
    return o, S1, jnp.swapaxes(T, -1, -2), u, w


def _delta_chunk_bwd(S0, q, k, v, g, beta, Tt, u, w, do, dS1):
    m = _delta_common(q, k, g, beta)
    C = q.shape[-2]
    qs, Dm, Dmt, eg, el, er, kb, KK, QK = (m[n] for n in ("qs", "Dm", "Dmt", "eg", "el", "er", "kb", "KK", "QK"))
    strict = m["row"] > m["col"]
    total = lambda x: jnp.sum(x, axis=(-2, -1), keepdims=True)
    vn = u - _dot(w, S0)
    qg = qs * eg
    kr = k * er

    dvn = _dot(_dot_nt(k, qs) * Dmt, do) + _dot(kr, dS1)
    dS0 = dS1 * el + _dot_tn(qg, do) - _dot_tn(w, dvn)
    d_el = total(dS1 * S0)
    dqg = _dot_nt(do, S0)
    dqs = dqg * eg
    deg = jnp.sum(dqg * qs, -1, keepdims=True)
    dP = _dot_nt(do, vn)
    dPD = dP * Dm
    dqs = dqs + _dot(dPD, k)
    dk = _dot(_dot_nt(vn, do) * Dmt, qs)
    dD = dP * QK
    dkr = _dot_nt(vn, dS1)
    dk = dk + dkr * er
    der = jnp.sum(dkr * k, -1, keepdims=True)
    dw = -_dot_nt(dvn, S0)
    th, tl = _split(Tt)

    def tt_times(x):
        xh, xl = _split(x)
        return _mxu(th, xh, "nn") + (_mxu(th, xl, "nn") + _mxu(tl, xh, "nn"))

    dru = tt_times(dvn)
    drw = tt_times(dw)
    dA = -(_dotf_nt(dru, u) + _dotf_nt(drw, w))
    dAm = jnp.where(strict, dA, 0.0)
    dKK = dAm * Dm
    dkb = _dot(dKK, k)
    dk = dk + _dot_tn(dKK, kb)
    dD = dD + dAm * KK
    dv = dru * beta
    dbeta = jnp.sum(dru * v, -1, keepdims=True)
    dkb = dkb + drw * eg
    deg = deg + jnp.sum(drw * kb, -1, keepdims=True)
    dk = dk + dkb * beta
    dbeta = dbeta + jnp.sum(dkb * k, -1, keepdims=True)
    E = dD * Dm
    dgc = jnp.sum(E, -1, keepdims=True) - jnp.sum(jnp.swapaxes(E, -1, -2), -1, keepdims=True)
    dgc = dgc + deg * eg - der * er
    dgl = total(der * er) + d_el * el
    dgc = dgc + jnp.where(_iota((C, 1), 0) == C - 1, dgl, 0.0)
    triu = jnp.where(m["row"] <= m["col"], 1.0, 0.0)
    dg = _dot01(triu, jnp.broadcast_to(dgc, dgc.shape[:-1] + (LANES,)))[..., :1]
    dq = dqs * (D_HEAD ** -0.5)
    return dq, dk, dv, dg, dbeta, dS0


def _head_cols(gb, h):
    lane = _iota(gb.shape, 1)
    beta = jnp.sum(jnp.where(lane == h, gb, 0.0), -1, keepdims=True)
    g = jnp.sum(jnp.where(lane == N_HEADS + h, gb, 0.0), -1, keepdims=True)
    return g, beta


def _delta_fwd(q, k, v, gb):
    S = q.shape[0]
    C = DELTA_CHUNK
    N = S // C

    HB = DELTA_HEADS_PER_STEP

    def body(q_ref, k_ref, v_ref, gb_ref, o_ref, st_ref, t_ref, u_ref, w_ref, s_scr):
        n, hb = pl.program_id(0), pl.program_id(1)
        gb = gb_ref[...]

        @pl.when(n == 0)
        def _():
            for hh in range(HB):
                s_scr[hb * HB + hh] = jnp.zeros((D_HEAD, D_HEAD), F32)

        heads = [hb * HB + hh for hh in range(HB)]
        cols = [slice(hh * D_HEAD, (hh + 1) * D_HEAD) for hh in range(HB)]
        per_head = lambda ref: jnp.stack([ref[:, c] for c in cols])
        g, beta = (jnp.stack(t) for t in zip(*[_head_cols(gb, h) for h in heads]))
        S0 = jnp.stack([s_scr[h] for h in heads])
        o, S1, Tt, u, w = _delta_chunk_fwd(S0, per_head(q_ref), per_head(k_ref), per_head(v_ref), g, beta)
        for hh in range(HB):
            st_ref[hh, 0] = S0[hh]
            t_ref[hh, 0] = Tt[hh]
            o_ref[:, cols[hh]] = o[hh]
            u_ref[:, cols[hh]] = u[hh]
            w_ref[:, cols[hh]] = w[hh]
            s_scr[heads[hh]] = S1[hh]

    hd = pl.BlockSpec((C, HB * D_HEAD), lambda n, h: (n, h))
    mat = pl.BlockSpec((HB, 1, D_HEAD, D_HEAD), lambda n, h: (h, n, 0, 0))
    est = 40 * HB * _nbytes((C, D_HEAD), F32)
    seq = jax.ShapeDtypeStruct((S, N_HEADS * D_HEAD), F32)
    return pl.pallas_call(
        body, name="delta_fwd", grid=(N, N_HEADS // HB),
        in_specs=[hd, hd, hd, pl.BlockSpec((C, LANES), lambda n, h: (n, 0))],
        out_specs=[hd, mat, mat, hd, hd],
        out_shape=[seq, jax.ShapeDtypeStruct((N_HEADS, N, D_HEAD, D_HEAD), F32),
                   jax.ShapeDtypeStruct((N_HEADS, N, C, C), F32), seq, seq],
        scratch_shapes=[pltpu.VMEM((N_HEADS, D_HEAD, D_HEAD), F32)],
        compiler_params=_cparams(est, ("arbitrary", "arbitrary")),
    )(q, k, v, gb)


def _delta_bwd(q, k, v, gb, st, tinv, u, w, do):
    S = q.shape[0]
    C = DELTA_CHUNK
    N = S // C

    HB = DELTA_HEADS_PER_STEP

    def body(q_ref, k_ref, v_ref, gb_ref, st_ref, t_ref, u_ref, w_ref, do_ref, dq_ref, dk_ref, dv_ref, dgb_ref, ds_scr):
        n, hb = pl.program_id(0), pl.program_id(1)
        gb = gb_ref[...]
        lane = _iota((C, LANES), 1)
        dgb = jnp.zeros((C, LANES), F32)

        @pl.when(n == 0)
        def _():
            for hh in range(HB):
                ds_scr[hb * HB + hh] = jnp.zeros((D_HEAD, D_HEAD), F32)

        heads = [hb * HB + hh for hh in range(HB)]
        cols = [slice(hh * D_HEAD, (hh + 1) * D_HEAD) for hh in range(HB)]
        per_head = lambda ref: jnp.stack([ref[:, c] for c in cols])
        g, beta = (jnp.stack(t) for t in zip(*[_head_cols(gb, h) for h in heads]))
        dS1 = jnp.stack([ds_scr[h] for h in heads])
        dq, dk, dv, dg, dbeta, dS0 = _delta_chunk_bwd(
            st_ref[:, 0], per_head(q_ref), per_head(k_ref), per_head(v_ref), g, beta, t_ref[:, 0],
            per_head(u_ref), per_head(w_ref), per_head(do_ref), dS1)
        for hh, h in enumerate(heads):
            dq_ref[:, cols[hh]] = dq[hh]
            dk_ref[:, cols[hh]] = dk[hh]
            dv_ref[:, cols[hh]] = dv[hh]
            dgb = dgb + jnp.where(lane == h, dbeta[hh], 0.0) + jnp.where(lane == N_HEADS + h, dg[hh], 0.0)
            ds_scr[h] = dS0[hh]

        @pl.when(hb == 0)
        def _():
            dgb_ref[...] = dgb

        @pl.when(hb > 0)
        def _():
            dgb_ref[...] += dgb

    hd = pl.BlockSpec((C, HB * D_HEAD), lambda n, h: (N - 1 - n, h))
    mat = pl.BlockSpec((HB, 1, D_HEAD, D_HEAD), lambda n, h: (h, N - 1 - n, 0, 0))
    gbs = pl.BlockSpec((C, LANES), lambda n, h: (N - 1 - n, 0))
    est = 60 * HB * _nbytes((C, D_HEAD), F32)
    return pl.pallas_call(
        body, name="delta_bwd", grid=(N, N_HEADS // HB),
        in_specs=[hd, hd, hd, gbs, mat, mat, hd, hd, hd],
        out_specs=[hd, hd, hd, gbs],
        out_shape=[jax.ShapeDtypeStruct((S, N_HEADS * D_HEAD), F32)] * 3 + [jax.ShapeDtypeStruct((S, LANES), F32)],
        scratch_shapes=[pltpu.VMEM((N_HEADS, D_HEAD, D_HEAD), F32)],
        compiler_params=_cparams(est, ("arbitrary", "arbitrary")),
    )(q, k, v, gb, st, tinv, u, w, do)


def _ya_head(o, z, onw):
    return o * lax.rsqrt(jnp.mean(o * o, -1, keepdims=True) + RMS_EPS) * onw * _silu(z)


def _norm_cdf(x):
    return 0.5 * (1.0 + lax.erf(x * 0.7071067811865476))


def _norm_pdf(x):
    return jnp.exp(-0.5 * x * x) * 0.3989422804014327


def _chunk_causal(shape, di, dj):
    sh = jnp.int32(int(math.log2(SGU_CHUNK)))
    return lax.shift_right_logical(_iota(shape, di), sh) >= lax.shift_right_logical(_iota(shape, dj), sh)


def _ws_masked(ws):
    return jnp.where(_chunk_causal(ws.shape, 1, 2), ws, 0.0)


def _mix_prep(o, proj, onw, sg, sb, ws, bst, *, tm=256):
    S = o.shape[0]
    tm = _tile(S, tm, SGU_BLOCK)

    def body(o_ref, z_ref, u_ref, vg_ref, onw_ref, sg_ref, sb_ref, ws_ref, bst_ref, ya_ref, yb_ref, phi_ref):
        onw = onw_ref[...]
        for h in range(N_HEADS):
            sl = slice(h * D_HEAD, (h + 1) * D_HEAD)
            ya_ref[:, sl] = _ya_head(o_ref[:, sl], z_ref[:, sl].astype(F32), onw).astype(BF)
        u, vg = u_ref[...].astype(F32), vg_ref[...].astype(F32)
        phi_u, phi_v = _norm_cdf(u), _norm_cdf(vg)
        phi_ref[:, :D_MODEL] = phi_u
        phi_ref[:, D_MODEL:] = phi_v
        ua, vl = u * phi_u, _ln(vg * phi_v, sg_ref[...], sb_ref[...])
        wsm = _ws_masked(ws_ref[...])
        bst = bst_ref[...]
        for blk in range(tm // SGU_BLOCK):
            rs = slice(blk * SGU_BLOCK, (blk + 1) * SGU_BLOCK)
            for gi in range(SGU_GROUPS):
                cs = slice(gi * D_HEAD, (gi + 1) * D_HEAD)
                sp = _dot(wsm[gi], vl[rs, cs]) + bst[:, gi:gi + 1]
                yb_ref[rs, cs] = (ua[rs, cs] * sp).astype(BF)

    blk = lambda col: pl.BlockSpec((tm, D_MODEL), lambda i: (i, col))
    full = lambda shape: pl.BlockSpec(shape, lambda i: (0,) * len(shape))
    est = 10 * _nbytes((tm, D_MODEL), F32)
    return pl.pallas_call(
        body, name="mix_prep", grid=(S // tm,),
        in_specs=[blk(0), blk(0), blk(1), blk(2), full((1, D_HEAD)), full((1, D_MODEL)), full((1, D_MODEL)),
                  full((SGU_GROUPS, SGU_BLOCK, SGU_BLOCK)), full((SGU_BLOCK, LANES))],
        out_specs=[blk(0), blk(0), pl.BlockSpec((tm, 2 * D_MODEL), lambda i: (i, 0))],
        out_shape=[jax.ShapeDtypeStruct((S, D_MODEL), BF)] * 2 + [jax.ShapeDtypeStruct((S, 2 * D_MODEL), F32)],
        compiler_params=_cparams(est, ("parallel",)),
    )(o, proj, proj, proj, onw, sg, sb, ws, bst)


def _mix_prep_bwd(o, proj, phi, onw, sg, sb, ws, bst, dya, dyb, dproj, *, tm=256):
    S = o.shape[0]
    tm = _tile(S, tm, SGU_BLOCK)

    def body(o_ref, z_ref, u_ref, vg_ref, phi_ref, onw_ref, sg_ref, sb_ref, ws_ref, bst_ref, dya_ref, dyb_ref, dproj_in,
             do_ref, dzuv_ref, donw_ref, dsg_ref, dsb_ref, dws_ref, dbst_ref, dvl_scr, dua_scr):
        dz_ref, du_ref, dvg_ref = (dzuv_ref.at[:, k * D_MODEL:(k + 1) * D_MODEL] for k in range(3))
        @pl.when(pl.program_id(0) == 0)
        def _():
            for r in (donw_ref, dsg_ref, dsb_ref, dws_ref, dbst_ref):
                r[...] = jnp.zeros_like(r)

        onw = onw_ref[...]
        donw = jnp.zeros((1, D_HEAD), F32)
        for h in range(N_HEADS):
            sl = slice(h * D_HEAD, (h + 1) * D_HEAD)
            _, vj = jax.vjp(_ya_head, o_ref[:, sl], z_ref[:, sl].astype(F32), onw)
            do_h, dz_h, donw_h = vj(dya_ref[:, sl])
            do_ref[:, sl] = do_h.astype(BF)
            dz_ref[:, sl] = dz_h.astype(BF)
            donw = donw + donw_h
        donw_ref[...] += _bcast_rows(donw)

        u, vg = u_ref[...].astype(F32), vg_ref[...].astype(F32)
        phi_u, phi_v = phi_ref[:, :D_MODEL], phi_ref[:, D_MODEL:]
        ua = u * phi_u
        vl, vj = jax.vjp(_ln, vg * phi_v, sg_ref[...], sb_ref[...])
        wsm = _ws_masked(ws_ref[...])
        bst = bst_ref[...]
        lane = _iota((SGU_BLOCK, LANES), 1)
        dbst = jnp.zeros((SGU_BLOCK, LANES), F32)
        cmask = _chunk_causal((SGU_BLOCK, SGU_BLOCK), 0, 1)
        for gi in range(SGU_GROUPS):
            cs = slice(gi * D_HEAD, (gi + 1) * D_HEAD)
            wg = wsm[gi]
            wgt = jnp.transpose(wg)
            dwg = jnp.zeros((SGU_BLOCK, SGU_BLOCK), F32)
            for blk in range(tm // SGU_BLOCK):
                rs = slice(blk * SGU_BLOCK, (blk + 1) * SGU_BLOCK)
                sp = _dot(wg, vl[rs, cs]) + bst[:, gi:gi + 1]
                dyb = dyb_ref[rs, cs]
                dsp = dyb * ua[rs, cs]
                dua_scr[rs, cs] = dyb * sp
                dvl_scr[rs, cs] = _dot(wgt, dsp)
                dwg = dwg + _dot_nt(dsp, vl[rs, cs])
                dbst = dbst + jnp.where(lane == gi, jnp.sum(dsp, -1, keepdims=True), 0.0)
            dws_ref[gi] += jnp.where(cmask, dwg, 0.0)
        dbst_ref[...] += dbst
        dgv, dsg, dsb = vj(dvl_scr[...])
        du_ref[...] = (dua_scr[...] * (phi_u + u * _norm_pdf(u))).astype(BF)
        dvg_ref[...] = (dgv * (phi_v + vg * _norm_pdf(vg))).astype(BF)
        dsg_ref[...] += _bcast_rows(dsg)
        dsb_ref[...] += _bcast_rows(dsb)

    blk = lambda col: pl.BlockSpec((tm, D_MODEL), lambda i: (i, col))
    full = lambda shape: pl.BlockSpec(shape, lambda i: (0,) * len(shape))
    est = 16 * _nbytes((tm, D_MODEL), F32)
    outs = pl.pallas_call(
        body, name="mix_prep_bwd", grid=(S // tm,),
        in_specs=[blk(0), blk(0), blk(1), blk(2), pl.BlockSpec((tm, 2 * D_MODEL), lambda i: (i, 0)),
                  full((1, D_HEAD)), full((1, D_MODEL)), full((1, D_MODEL)),
                  full((SGU_GROUPS, SGU_BLOCK, SGU_BLOCK)), full((SGU_BLOCK, LANES)), blk(0), blk(0),
                  pl.BlockSpec(memory_space=pl.ANY)],
        out_specs=[blk(0), pl.BlockSpec((tm, 3 * D_MODEL), lambda i: (i, 1)),
                   full((SUBLANES, D_HEAD)), full((SUBLANES, D_MODEL)), full((SUBLANES, D_MODEL)),
                   full((SGU_GROUPS, SGU_BLOCK, SGU_BLOCK)), full((SGU_BLOCK, LANES))],
        out_shape=[jax.ShapeDtypeStruct((S, D_MODEL), BF), jax.ShapeDtypeStruct(dproj.shape, BF),
                   jax.ShapeDtypeStruct((SUBLANES, D_HEAD), F32), jax.ShapeDtypeStruct((SUBLANES, D_MODEL), F32),
                   jax.ShapeDtypeStruct((SUBLANES, D_MODEL), F32),
                   jax.ShapeDtypeStruct((SGU_GROUPS, SGU_BLOCK, SGU_BLOCK), F32),
                   jax.ShapeDtypeStruct((SGU_BLOCK, LANES), F32)],
        input_output_aliases={12: 1},
        scratch_shapes=[pltpu.VMEM((tm, D_MODEL), F32)] * 2,
        compiler_params=_cparams(est, ("arbitrary",)),
    )(o, proj, proj, proj, phi, onw, sg, sb, ws, bst, dya, dyb, dproj)
    return outs


def _mm_gate_merge(ya, yb, wpa, wpb, proj, *, tm=512):
    S = ya.shape[0]
    tm = _tile(S, tm, SUBLANES * 2)

    def body(ya_ref, yb_ref, wa_ref, wb_ref, ga_ref, gb_ref, pa_ref, pb_ref, m_ref):
        pa = _dot(ya_ref[...], wa_ref[...]).astype(BF)
        pb = _dot(yb_ref[...], wb_ref[...]).astype(BF)
        pa_ref[...] = pa
        pb_ref[...] = pb
        m_ref[...] = (_sigmoid(ga_ref[...].astype(F32)) * pa.astype(F32)
                      + _sigmoid(gb_ref[...].astype(F32)) * pb.astype(F32)).astype(BF)

    blk = lambda col: pl.BlockSpec((tm, D_MODEL), lambda i: (i, col))
    wsp = pl.BlockSpec((D_MODEL, D_MODEL), lambda i: (0, 0))
    return pl.pallas_call(
        body, name="mm_gate_merge", grid=(S // tm,),
        in_specs=[blk(0), blk(0), wsp, wsp, blk(3), blk(4)], out_specs=[blk(0)] * 3,
        out_shape=[jax.ShapeDtypeStruct((S, D_MODEL), BF)] * 3,
        compiler_params=_cparams(2 * _nbytes((D_MODEL, D_MODEL), BF) + 8 * _nbytes((tm, D_MODEL), F32), ("parallel",)),
    )(ya, yb, wpa, wpb, proj, proj)


def _mm_gate_merge_bwd(dmix, wo, pa, pb, proj, *, tm=512):
    S = pa.shape[0]
    tm = _tile(S, tm, SUBLANES * 2)

    def body(d_ref, w_ref, pa_ref, pb_ref, ga_ref, gb_ref, dpa_ref, dpb_ref, dg_ref):
        dm = _dot_nt(d_ref[...], w_ref[...])
        sa, sb = _sigmoid(ga_ref[...].astype(F32)), _sigmoid(gb_ref[...].astype(F32))
        dpa_ref[...] = (dm * sa).astype(BF)
        dpb_ref[...] = (dm * sb).astype(BF)
        dg_ref[:, :D_MODEL] = (dm * pa_ref[...].astype(F32) * sa * (1.0 - sa)).astype(BF)
        dg_ref[:, D_MODEL:] = (dm * pb_ref[...].astype(F32) * sb * (1.0 - sb)).astype(BF)

    blk = lambda col: pl.BlockSpec((tm, D_MODEL), lambda i: (i, col))
    est = _nbytes((D_MODEL, D_MODEL), BF) + 10 * _nbytes((tm, D_MODEL), F32)
    return pl.pallas_call(
        body, name="mm_gate_merge_bwd", grid=(S // tm,),
        in_specs=[blk(0), pl.BlockSpec((D_MODEL, D_MODEL), lambda i: (0, 0)), blk(0), blk(0), blk(3), blk(4)],
        out_specs=[blk(0), blk(0), pl.BlockSpec((tm, 2 * D_MODEL), lambda i: (i, 3))],
        out_shape=[jax.ShapeDtypeStruct((S, D_MODEL), BF)] * 2 + [jax.ShapeDtypeStruct((S, 8 * D_MODEL), BF)],
        compiler_params=_cparams(est, ("parallel",)),
    )(dmix, wo, pa, pb, proj, proj)


def _mm_swiglu(xb, wg, wu, *, tm=1024, tn=768):
    S, K = xb.shape
    tm = _tile(S, tm, SUBLANES * 2)
    tn = _tile(FFN_K, tn, LANES)

    def body(x_ref, wg_ref, wu_ref, hg_ref, hu_ref, h_ref):
        x = x_ref[...]
        hg = _dot(x, wg_ref[...]).astype(BF)
        hu = _dot(x, wu_ref[...]).astype(BF)
        hg_ref[...] = hg
        hu_ref[...] = hu
        h_ref[...] = (_silu(hg.astype(F32)) * hu.astype(F32)).astype(BF)

    out = pl.BlockSpec((tm, tn), lambda i, j: (i, j))
    est = _nbytes((tm, K), BF) + 2 * _nbytes((K, tn), BF) + 6 * _nbytes((tm, tn), F32)
    return pl.pallas_call(
        body, name="mm_swiglu", grid=(S // tm, FFN_K // tn),
        in_specs=[pl.BlockSpec((tm, K), lambda i, j: (i, 0)), pl.BlockSpec((K, tn), lambda i, j: (0, j)),
                  pl.BlockSpec((K, tn), lambda i, j: (0, j))],
        out_specs=[out] * 3, out_shape=[jax.ShapeDtypeStruct((S, FFN_K), BF)] * 3,
        compiler_params=_cparams(est, ("parallel", "parallel")),
    )(xb, wg, wu)


def _mm_swiglu_bwd(dffn, wd, hg, hu, *, tm=1024, tn=768):
    S, K = dffn.shape
    tm = _tile(S, tm, SUBLANES * 2)
    tn = _tile(FFN_K, tn, LANES)

    def body(d_ref, w_ref, hg_ref, hu_ref, dhg_ref, dhu_ref):
        dh = _dot_nt(d_ref[...], w_ref[...])
        act, dact = _silu_and_grad(hg_ref[...].astype(F32))
        dhg_ref[...] = (dh * hu_ref[...].astype(F32) * dact).astype(BF)
        dhu_ref[...] = (dh * act).astype(BF)

    out = pl.BlockSpec((tm, tn), lambda i, j: (i, j))
    est = _nbytes((tm, K), dffn.dtype) + _nbytes((tn, K), BF) + 8 * _nbytes((tm, tn), F32)
    return pl.pallas_call(
        body, name="mm_swiglu_bwd", grid=(S // tm, FFN_K // tn),
        in_specs=[pl.BlockSpec((tm, K), lambda i, j: (i, 0)), pl.BlockSpec((tn, K), lambda i, j: (j, 0)), out, out],
        out_specs=[out, out], out_shape=[jax.ShapeDtypeStruct((S, FFN_K), BF)] * 2,
        compiler_params=_cparams(est, ("parallel", "parallel")),
    )(dffn, wd, hg, hu)


def _mm_resid_ln(a, bmat, x, g, b, *, name, tm=512):
    S, K = a.shape
    tm = _tile(S, tm, SUBLANES * 2)

    def body(a_ref, w_ref, x_ref, g_ref, b_ref, pre_ref, y_ref, yb_ref):
        pre = ALPHA * x_ref[...] + _dot(a_ref[...], w_ref[...])
        y = _ln(pre, g_ref[...], b_ref[...])
        pre_ref[...] = pre
        y_ref[...] = y
        yb_ref[...] = y.astype(BF)

    blk = pl.BlockSpec((tm, D_MODEL), lambda i: (i, 0))
    vec = pl.BlockSpec((1, D_MODEL), lambda i: (0, 0))
    est = _nbytes((tm, K), BF) + _nbytes((K, D_MODEL), BF) + 8 * _nbytes((tm, D_MODEL), F32)
    return pl.pallas_call(
        body, name=name, grid=(S // tm,),
        in_specs=[pl.BlockSpec((tm, K), lambda i: (i, 0)), pl.BlockSpec((K, D_MODEL), lambda i: (0, 0)), blk, vec, vec],
        out_specs=[blk, blk, blk],
        out_shape=[jax.ShapeDtypeStruct((S, D_MODEL), F32)] * 2 + [jax.ShapeDtypeStruct((S, D_MODEL), BF)],
        compiler_params=_cparams(est, ("parallel",)),
    )(a, bmat, x, g, b)


def _ln_bwd(pre, g, b, dy, *, tm=512):
    S = pre.shape[0]
    tm = _tile(S, tm, SUBLANES)

    def body(p_ref, g_ref, b_ref, dy_ref, dp_ref, dg_ref, db_ref):
        @pl.when(pl.program_id(0) == 0)
        def _():
            dg_ref[...] = jnp.zeros_like(dg_ref)
            db_ref[...] = jnp.zeros_like(db_ref)

        _, vj = jax.vjp(_ln, p_ref[...], g_ref[...], b_ref[...])
        dp, dg, db = vj(dy_ref[...])
        dp_ref[...] = dp
        dg_ref[...] += _bcast_rows(dg)
        db_ref[...] += _bcast_rows(db)

    blk = pl.BlockSpec((tm, D_MODEL), lambda i: (i, 0))
    vec = pl.BlockSpec((1, D_MODEL), lambda i: (0, 0))
    acc = pl.BlockSpec((SUBLANES, D_MODEL), lambda i: (0, 0))
    return pl.pallas_call(
        body, name="ln_bwd", grid=(S // tm,),
        in_specs=[blk, vec, vec, blk], out_specs=[blk, acc, acc],
        out_shape=[jax.ShapeDtypeStruct((S, D_MODEL), F32)] + [jax.ShapeDtypeStruct((SUBLANES, D_MODEL), F32)] * 2,
        compiler_params=_cparams(10 * _nbytes((tm, D_MODEL), F32), ("arbitrary",)),
    )(pre, g, b, dy)


def _loss_ln_bwd(y, tgt, pre, g, b, *, tm=512):
    S = y.shape[0]
    tm = _tile(S, tm, SUBLANES)

    def body(y_ref, t_ref, p_ref, g_ref, b_ref, dp_ref, dg_ref, db_ref, l_ref):
        @pl.when(pl.program_id(0) == 0)
        def _():
            for r in (dg_ref, db_ref, l_ref):
                r[...] = jnp.zeros_like(r)

        e = y_ref[...] - t_ref[...]
        l_ref[...] += 0.5 * jnp.sum(jnp.mean(e * e, -1, keepdims=True), keepdims=True)
        _, vj = jax.vjp(_ln, p_ref[...], g_ref[...], b_ref[...])
        dp, dg, db = vj(e * (1.0 / D_MODEL))
        dp_ref[...] = dp
        dg_ref[...] += _bcast_rows(dg)
        db_ref[...] += _bcast_rows(db)

    blk = pl.BlockSpec((tm, D_MODEL), lambda i: (i, 0))
    vec = pl.BlockSpec((1, D_MODEL), lambda i: (0, 0))
    acc = pl.BlockSpec((SUBLANES, D_MODEL), lambda i: (0, 0))
    return pl.pallas_call(
        body, name="loss_ln_bwd", grid=(S // tm,),
        in_specs=[blk, blk, blk, vec, vec], out_specs=[blk, acc, acc, pl.BlockSpec((SUBLANES, LANES), lambda i: (0, 0))],
        out_shape=[jax.ShapeDtypeStruct((S, D_MODEL), F32)] + [jax.ShapeDtypeStruct((SUBLANES, D_MODEL), F32)] * 2
                  + [jax.ShapeDtypeStruct((SUBLANES, LANES), F32)],
        compiler_params=_cparams(12 * _nbytes((tm, D_MODEL), F32), ("arbitrary",)),
    )(y, tgt, pre, g, b)


def _layer_fwd(x, xb, w, late):
    pq = _mm(xb, w["win"], mode="nn", name="mm_in_qkv", tm=1024, tn=1024, cols=(0, 3 * D_MODEL))
    proj = _mm(xb, w["win"], mode="nn", name="mm_in_rest", tm=1024, tn=1024, cols=(3 * D_MODEL, 5 * D_MODEL), out_dtype=BF)
    ba = _mm(xb, w["wba"], mode="nn", name="mm_in_ba", tm=1024, tn=LANES)
    qn, kn, vv, gb, conv_out = _qkv_prep(pq, ba, w["convw"], w["arow"], w["dtrow"])
    o, st, tinv, wy_u, wy_w = _delta_fwd(qn, kn, vv, gb)
    ya, yb, phi = _mix_prep(o, proj, w["onw"], w["sg"], w["sb"], w["ws"], w["bst"])
    w = {**w, **late(ya)}
    pa, pb, m = _mm_gate_merge(ya, yb, w["wpa"], w["wpb"], proj)
    pre1, x1, x1b = _mm_resid_ln(m, w["wo"], x, w["ln1g"], w["ln1b"], name="mm_out_ln")
    hg, hu, h = _mm_swiglu(x1b, w["wg"], w["wu"])
    pre2, x2, x2b = _mm_resid_ln(h, w["wd"], x1, w["ln2g"], w["ln2b"], name="mm_down_ln")
    saved = dict(xb=xb, pq=pq, conv_out=conv_out, proj=proj, phi=phi, ba=ba, qn=qn, kn=kn, vv=vv, gb=gb, o=o, st=st, tinv=tinv, wy_u=wy_u, wy_w=wy_w,
                 ya=ya, yb=yb,
                 pa=pa, pb=pb, m=m, pre1=pre1, x1b=x1b, hg=hg, hu=hu, h=h, pre2=pre2)
    return x2, x2b, saved, w


def _layer_bwd(dpre2, ln2_grads, w, s, on_part=None):
    g = {}
    started = lambda part: on_part(part, g) if on_part is not None else None
    after = lambda v, token: v if token is None else v + token.astype(v.dtype)
    g["ln2g"], g["ln2b"] = ln2_grads
    dhg, dhu = _mm_swiglu_bwd(dpre2, w["wd"], s["hg"], s["hu"])
    g["wd"] = _mm(s["h"], dpre2, mode="tn", name="mm_tn_down", tm=1536, tk=1024, out_dtype=BF)
    dx1 = _mm(dhg, w["wg"], mode="nt", name="mm_nt_gu", pair=(dhu, w["wu"]), add=dpre2, add_scale=ALPHA, tm=1024, tk=1536)
    g["wg"] = _mm(s["x1b"], dhg, mode="tn", name="mm_tn_gu", tm=1024, tn=1536, tk=2048, out_dtype=BF)
    g["wu"] = _mm(s["x1b"], dhu, mode="tn", name="mm_tn_gu", tm=1024, tn=1536, tk=2048, out_dtype=BF)
    dpre1, g["ln1g"], g["ln1b"] = _ln_bwd(s["pre1"], w["ln1g"], w["ln1b"], dx1)
    g["wo"] = _mm(s["m"], dpre1, mode="tn", name="mm_tn_sq", tm=1024, tk=1024, out_dtype=BF)
    dpa, dpb, dproj = _mm_gate_merge_bwd(dpre1, w["wo"], s["pa"], s["pb"], s["proj"])
    dya = _mm(dpa, w["wpa"], mode="nt", name="mm_nt_sq")
    g["wpa"] = _mm(s["ya"], dpa, mode="tn", name="mm_tn_sq", tm=1024, tk=1024, out_dtype=BF)
    dyb = _mm(dpb, w["wpb"], mode="nt", name="mm_nt_sq")
    g["wpb"] = _mm(s["yb"], dpb, mode="tn", name="mm_tn_sq", tm=1024, tk=1024, out_dtype=BF)
    do, dproj, g["onw"], g["sg"], g["sb"], g["ws"], g["bst"] = _mix_prep_bwd(
        s["o"], s["proj"], s["phi"], after(w["onw"], started("late")), w["sg"], w["sb"], w["ws"], w["bst"], dya, dyb, dproj)
    dqn, dkn, dvv, dgb = _delta_bwd(s["qn"], s["kn"], s["vv"], s["gb"], s["st"], s["tinv"], s["wy_u"], s["wy_w"], do)
    dc, dba, g["convw"], g["arow"], g["dtrow"] = _qkv_prep_bwd(
        s["pq"], s["conv_out"], s["ba"], w["arow"], w["dtrow"], dqn, dkn, dvv, dgb)
    dproj = _conv_bwd(dc, w["convw"], dproj)
    g["win"] = _mm(s["xb"], dproj, mode="tn", name="mm_tn_in", tm=1024, tn=1024, tk=2048, out_dtype=BF)
    g["wba"] = _mm(s["xb"], dba, mode="tn", name="mm_tn_ba", tm=1024, tn=LANES, tk=1024, out_dtype=BF)
    dx = _mm(dba, after(w["wba"], started("early")), mode="nt", name="mm_nt_ba", add=dpre1, add_scale=ALPHA, tm=1024)
    dx = _mm(dproj, w["win"], mode="nt", name="mm_nt_in", add=dx, add_scale=1.0, tm=1024, tk=2048)
    return dx, g


def _local_step(x, tgt, layers, on_grads=None):
    saved, weights = [], []
    xb = x.astype(BF)
    for layer in layers:
        x, xb, s, w = _layer_fwd(x, xb, *layer(x))
        saved.append(s)
        weights.append(w)
    last = len(layers) - 1
    dpre2, dg, db, lacc = _loss_ln_bwd(x, tgt, saved[last]["pre2"], weights[last]["ln2g"], weights[last]["ln2b"])
    grads = [None] * len(layers)
    for l in reversed(range(len(layers))):
        on_part = functools.partial(on_grads, l) if on_grads is not None else None
        dx, grads[l] = _layer_bwd(dpre2, (dg, db), weights[l], saved[l], on_part)
        if l > 0:
            dpre2, dg, db = _ln_bwd(saved[l - 1]["pre2"], weights[l - 1]["ln2g"], weights[l - 1]["ln2b"], dx)
    return lacc[0, 0], dx, grads


_QKVZ = 4 * D_MODEL
_BA = 2 * N_HEADS


WEIGHT_NAMES = ("w_in", "conv_w", "a_log", "dt_bias", "o_norm_w", "sgu_ln_g", "sgu_ln_b", "w_s", "b_s", "w_pa", "w_pb",
                "w_o", "ln1_g", "ln1_b", "w_ffn_gate", "w_ffn_up", "w_ffn_down", "ln2_g", "ln2_b")
WIRE = ("w_in", "w_ffn_gate", "w_ffn_up", "w_ffn_down", "w_pa", "w_pb", "w_o", "conv_w")
SMALL = (("a_log", N_HEADS), ("dt_bias", N_HEADS), ("o_norm_w", D_HEAD), ("sgu_ln_g", D_MODEL), ("sgu_ln_b", D_MODEL),
         ("w_s", SGU_GROUPS * SGU_BLOCK * SGU_BLOCK), ("b_s", SGU_GROUPS * SGU_BLOCK),
         ("ln1_g", D_MODEL), ("ln1_b", D_MODEL), ("ln2_g", D_MODEL), ("ln2_b", D_MODEL))
SMALL_ROWS = -(-sum(n for _, n in SMALL) // (LANES * SUBLANES)) * SUBLANES
N_MAIN_TILES = (N_IN - _BA) // D_MODEL
ADAM_TILES = dict(w_in=(128, "adamw_in"), w_ffn_gate=(256, "adamw_ffn_cols"), w_ffn_up=(256, "adamw_ffn_cols"),
                  w_ffn_down=(32, "adamw_ffn_rows"), w_pa=(128, "adamw_sq"), w_pb=(128, "adamw_sq"), w_o=(128, "adamw_sq"),
                  conv_w=(CONV_K, "adamw_conv"))


def _pad_to(a, axis, size):
    pads = [(0, 0)] * a.ndim
    pads[axis] = (0, size - a.shape[axis])
    return jnp.pad(a, pads)


def _wire_blocks(p):
    return dict(
        w_in=_pad_to(p["w_in"].astype(BF), 2, IN_PAD),
        w_ffn_gate=_pad_to(p["w_ffn_gate"].astype(BF), 2, FFN_PAD), w_ffn_up=_pad_to(p["w_ffn_up"].astype(BF), 2, FFN_PAD),
        w_ffn_down=_pad_to(p["w_ffn_down"].astype(BF), 1, FFN_PAD),
        w_pa=p["w_pa"].astype(BF), w_pb=p["w_pb"].astype(BF), w_o=p["w_o"].astype(BF),
        conv_w=_pad_to(p["conv_w"], 1, SUBLANES),
    )


def _by_columns(blocks):
    n, r, c = blocks.shape
    return jnp.transpose(blocks, (1, 0, 2)).reshape(r, n * c)


def _to_slots(full, c):
    r = full.shape[0]
    return jnp.transpose(full.reshape(r, N_DEV, c), (1, 0, 2))


def _lane_row(v, at):
    return jnp.pad(v[None], ((0, 0), (at, LANES - at - v.shape[0])))


UNALIGNED = ("w_in", "w_ffn_gate", "w_ffn_up")
EARLY = ("w_in", "conv_w")
LATE = ("w_pa", "w_pb", "w_o", "w_ffn_gate", "w_ffn_up", "w_ffn_down")


def _early_weights(stacks, p, l):
    return dict(
        win=_perm_in(stacks["w_in"], D_MODEL, N_MAIN_TILES), wba=_perm_in(stacks["w_in"], LANES, 1),
        convw=_by_columns(stacks["conv_w"][:, :CONV_K]),
        arow=_lane_row(p["a_log"][l], N_HEADS), dtrow=_lane_row(p["dt_bias"][l], N_HEADS),
        onw=p["o_norm_w"][l][None], sg=p["sgu_ln_g"][l][None], sb=p["sgu_ln_b"][l][None],
        ws=p["w_s"][l], bst=_pad_to(p["b_s"][l].T, 1, LANES),
        ln1g=p["ln1_g"][l][None], ln1b=p["ln1_b"][l][None], ln2g=p["ln2_g"][l][None], ln2b=p["ln2_b"][l][None],
    )


def _late_weights(stacks):
    return dict(
        wpa=stacks["w_pa"].reshape(D_MODEL, D_MODEL), wpb=stacks["w_pb"].reshape(D_MODEL, D_MODEL),
        wo=stacks["w_o"].reshape(D_MODEL, D_MODEL),
        wg=_by_columns(stacks["w_ffn_gate"]), wu=_by_columns(stacks["w_ffn_up"]),
        wd=stacks["w_ffn_down"].reshape(FFN_K, D_MODEL),
    )


def _small_pack(parts):
    flat = jnp.concatenate([parts[n].reshape(-1) for n, _ in SMALL])
    return _pad_to(flat, 0, SMALL_ROWS * LANES).reshape(SMALL_ROWS, LANES)


def _small_unpack(rows, like):
    flat, out, off = rows.reshape(-1), {}, 0
    for n, size in SMALL:
        out[n] = flat[off:off + size].reshape(like[n].shape[1:])
        off += size
    return out


def _late_slots(g):
    slots = dict(
        w_ffn_gate=_to_slots(g["wg"], FFN_PAD), w_ffn_up=_to_slots(g["wu"], FFN_PAD),
        w_ffn_down=g["wd"].reshape(N_DEV, FFN_PAD, D_MODEL),
        w_pa=g["wpa"].reshape(N_DEV, D_MODEL // N_DEV, D_MODEL), w_pb=g["wpb"].reshape(N_DEV, D_MODEL // N_DEV, D_MODEL),
        w_o=g["wo"].reshape(N_DEV, D_MODEL // N_DEV, D_MODEL),
    )
    return [slots[n] for n in LATE]


def _early_slots(g):
    slots = [_perm_out(g["win"], g["wba"]), _pad_to(_to_slots(g["convw"][:CONV_K], 3 * D_MODEL // N_DEV), 1, SUBLANES)]
    small = _small_pack(dict(
        a_log=g["arow"][0, N_HEADS:2 * N_HEADS], dt_bias=g["dtrow"][0, N_HEADS:2 * N_HEADS], o_norm_w=g["onw"][0],
        sgu_ln_g=g["sg"][0], sgu_ln_b=g["sb"][0], w_s=g["ws"], b_s=g["bst"][:, :SGU_GROUPS].T,
        ln1_g=g["ln1g"][0], ln1_b=g["ln1b"][0], ln2_g=g["ln2g"][0], ln2_b=g["ln2b"][0]))
    return slots, small


def _in_tile_start(j, tile_w):
    if tile_w == LANES:
        return jnp.int32(_QKVZ)
    return j * D_MODEL + jnp.where(j >= _QKVZ // D_MODEL, _BA, 0)


def _select(rows_iota, cols_iota, dev, start, valid):
    hit = (rows_iota + (dev * IN_BLOCK - start) == cols_iota) & (rows_iota < IN_BLOCK) & (cols_iota < valid)
    return jnp.where(hit, 1.0, 0.0).astype(BF)


def _perm_in(stack, tile_w, n_tiles):
    valid = _BA if tile_w == LANES else tile_w

    def first_dev(j):
        return lax.div(_in_tile_start(j, tile_w), jnp.int32(IN_BLOCK))

    def body(w_ref, o_ref, acc_ref):
        j, k = pl.program_id(0), pl.program_id(1)
        sel = _select(_iota((IN_PAD, tile_w), 0), _iota((IN_PAD, tile_w), 1), first_dev(j) + k,
                      _in_tile_start(j, tile_w), valid)
        part = jnp.dot(w_ref[0], sel, preferred_element_type=F32)

        @pl.when(k == 0)
        def _():
            acc_ref[...] = part

        @pl.when(k == 1)
        def _():
            o_ref[...] = (acc_ref[...] + part).astype(BF)

    est = _nbytes((D_MODEL, IN_PAD), BF) + 3 * _nbytes((D_MODEL, tile_w), F32) + 2 * _nbytes((IN_PAD, tile_w), F32)
    return pl.pallas_call(
        body, name="perm_in" if tile_w != LANES else "perm_in_ba", grid=(n_tiles, 2),
        in_specs=[pl.BlockSpec((1, D_MODEL, IN_PAD), lambda j, k: (jnp.minimum(first_dev(j) + k, N_DEV - 1), 0, 0))],
        out_specs=pl.BlockSpec((D_MODEL, tile_w), lambda j, k: (0, j)),
        out_shape=jax.ShapeDtypeStruct((D_MODEL, n_tiles * tile_w), BF),
        scratch_shapes=[pltpu.VMEM((D_MODEL, tile_w), F32)],
        compiler_params=_cparams(est, ("parallel", "arbitrary")),
    )(stack)


def _perm_out(dmain, dba):
    def tile(d, s):
        c0 = d * IN_BLOCK
        first = lax.div(c0 - jnp.where(c0 < _QKVZ, 0, jnp.minimum(c0 - _QKVZ, _BA)), jnp.int32(D_MODEL))
        return jnp.minimum(first + jnp.minimum(s, 1), N_MAIN_TILES - 1)

    def body(dm_ref, db_ref, o_ref, acc_ref):
        d, s = pl.program_id(0), pl.program_id(1)

        @pl.when(s == 0)
        def _():
            acc_ref[...] = jnp.zeros_like(acc_ref)

        start = _in_tile_start(tile(d, s), D_MODEL)
        overlaps = (start < (d + 1) * IN_BLOCK) & (d * IN_BLOCK < start + D_MODEL)

        @pl.when((s < 2) & overlaps)
        def _():
            sel = _select(_iota((D_MODEL, IN_PAD), 1), _iota((D_MODEL, IN_PAD), 0), d, start, D_MODEL)
            acc_ref[...] += jnp.dot(dm_ref[...], sel, preferred_element_type=F32)

        @pl.when(s == 2)
        def _():
            sel = _select(_iota((LANES, IN_PAD), 1), _iota((LANES, IN_PAD), 0), d, jnp.int32(_QKVZ), _BA)
            o_ref[0] = (acc_ref[...] + jnp.dot(db_ref[...], sel, preferred_element_type=F32)).astype(BF)

    est = 2 * _nbytes((D_MODEL, D_MODEL), BF) + 4 * _nbytes((D_MODEL, IN_PAD), F32)
    return pl.pallas_call(
        body, name="perm_out", grid=(N_DEV, 3),
        in_specs=[pl.BlockSpec((D_MODEL, D_MODEL), lambda d, s: (0, tile(d, s))),
                  pl.BlockSpec((D_MODEL, LANES), lambda d, s: (0, 0))],
        out_specs=pl.BlockSpec((1, D_MODEL, IN_PAD), lambda d, t: (d, 0, 0)),
        out_shape=jax.ShapeDtypeStruct((N_DEV, D_MODEL, IN_PAD), BF),
        scratch_shapes=[pltpu.VMEM((D_MODEL, IN_PAD), F32)],
        compiler_params=_cparams(est, ("parallel", "arbitrary")),
    )(dmain, dba)


def _mesh_place():
    x, y, c = (lax.axis_index(a) for a in MESH_AXES)
    return x, y, c


def _slot(x, y, c):
    return 4 * x + 2 * y + c


def _peer(place, j):
    x, y, c = place
    return (1 - x if j & 4 else x, 1 - y if j & 2 else y, 1 - c if j & 1 else c)


_HBM = pl.BlockSpec(memory_space=pltpu.HBM)
_SEM = pl.BlockSpec(memory_space=pltpu.SEMAPHORE)
_EFFECT = pltpu.SideEffectType.DATAFLOW_SIDE_EFFECTING


def _remote_copy(src_ref, land_ref, slot, per_slot, pslot, sems, u, j, peer):
    return pltpu.make_async_remote_copy(
        src_ref=src_ref.at[pslot] if per_slot else src_ref, dst_ref=land_ref.at[slot],
        send_sem=sems[0].at[u * (N_DEV - 1) + j - 1], recv_sem=sems[1].at[u * (N_DEV - 1) + j - 1],
        device_id=peer, device_id_type=pl.DeviceIdType.MESH)


def _own_copy(src_ref, land_ref, me, per_slot, sems, u):
    return pltpu.make_async_copy(src_ref.at[me] if per_slot else src_ref, land_ref.at[me], sems[2].at[u])


def _exchange_start(name, srcs, per_slot):
    n = len(srcs)
    lands = [jax.ShapeDtypeStruct(s.shape if p else (N_DEV,) + s.shape, s.dtype) for s, p in zip(srcs, per_slot)]

    def body(*refs):
        src_refs, sems, land_refs, token = refs[:n], refs[n:n + 3], refs[2 * n + 3:3 * n + 3], refs[-1]
        place = _mesh_place()
        me = _slot(*place)
        for u in range(n):
            _own_copy(src_refs[u], land_refs[u], me, per_slot[u], sems, u).start()
            for j in range(1, N_DEV):
                peer = _peer(place, j)
                _remote_copy(src_refs[u], land_refs[u], me, per_slot[u], _slot(*peer), sems, u, j, peer).start()
        token[...] = jnp.zeros_like(token)

    hbm = lambda a: pltpu.HBM(a.shape, a.dtype)
    sem = pltpu.SemaphoreType.DMA((n * (N_DEV - 1),))
    outs = pl.pallas_call(
        body, name=name,
        out_shape=(sem, sem, pltpu.SemaphoreType.DMA((n,)), *[hbm(a) for a in srcs], *[hbm(a) for a in lands],
                   jax.ShapeDtypeStruct((SUBLANES, LANES), F32)),
        in_specs=[_HBM] * n, out_specs=(_SEM, _SEM, _SEM, *[_HBM] * (2 * n), pl.BlockSpec(memory_space=pltpu.VMEM)),
        input_output_aliases={i: 3 + i for i in range(n)},
        compiler_params=pltpu.CompilerParams(has_side_effects=_EFFECT),
    )(*[pltpu.with_memory_space_constraint(a, pltpu.HBM) for a in srcs])
    return tuple(outs[:3]), list(outs[3:3 + n]), list(outs[3 + n:3 + 2 * n]), outs[-1]


def _exchange_wait(name, sems, srcs, lands, units, per_slot, after):
    m = len(units)

    def body(*refs):
        src_refs, land_refs, sem_refs = refs[:m], refs[m:2 * m], refs[2 * m:2 * m + 3]
        place = _mesh_place()
        me = _slot(*place)
        for i, u in enumerate(units):
            _own_copy(src_refs[i], land_refs[i], me, per_slot[u], sem_refs, u).wait()
            for j in range(1, N_DEV):
                peer = _peer(place, j)
                pslot = _slot(*peer)
                cp = _remote_copy(src_refs[i], land_refs[i], pslot, per_slot[u], pslot, sem_refs, u, j, peer)
                cp.wait_send()
                cp.wait_recv()

    hbm = lambda a: pltpu.HBM(a.shape, a.dtype)
    outs = pl.pallas_call(
        body, name=name, out_shape=tuple(hbm(a) for a in list(srcs) + list(lands)),
        in_specs=[_HBM] * (2 * m) + [_SEM] * 3 + [pl.BlockSpec(memory_space=pl.ANY)], out_specs=tuple([_HBM] * (2 * m)),
        input_output_aliases={i: i for i in range(2 * m)},
        compiler_params=pltpu.CompilerParams(has_side_effects=_EFFECT),
    )(*srcs, *lands, *sems, after)
    return list(outs[m:])


def _hold(arrays, token):
    n = len(arrays)

    def body(*refs):
        refs[-1][...] = refs[n][...]

    any_spec = pl.BlockSpec(memory_space=pl.ANY)
    vm = pl.BlockSpec(memory_space=pltpu.VMEM)
    outs = pl.pallas_call(
        body, name="hold", in_specs=[any_spec] * n + [vm], out_specs=[any_spec] * n + [vm],
        out_shape=[jax.ShapeDtypeStruct(a.shape, a.dtype) for a in arrays] + [jax.ShapeDtypeStruct(token.shape, token.dtype)],
        input_output_aliases={i: i for i in range(n)},
    )(*arrays, token)
    return list(outs[:n]), outs[n]


def _adam_update(g, w, m, v):
    m = ADAM_B1 * m + (1.0 - ADAM_B1) * g
    v = ADAM_B2 * v + (1.0 - ADAM_B2) * jnp.square(g)
    m_hat = m / (1.0 - ADAM_B1 ** ADAM_STEP)
    v_hat = v / (1.0 - ADAM_B2 ** ADAM_STEP)
    return -ADAM_LR * (m_hat / (jnp.sqrt(v_hat) + ADAM_EPS) + ADAM_WD * w), m, v


def _adamw(recvs, w, m, v, *, tr, name):
    L, R, C = w.shape
    rp = max(tr, SUBLANES * (4 // jnp.dtype(recvs[0].dtype).itemsize))
    Cp = recvs[0].shape[2]

    def body(*refs):
        r_refs, (w_ref, m_ref, v_ref, g_ref, d_ref, nm_ref, nv_ref) = refs[:L], refs[L:]
        for l in range(L):
            @pl.when(pl.program_id(0) == l)
            def _(r_ref=r_refs[l]):
                g = r_ref[0, :tr, :C].astype(F32)
                for s in range(1, N_DEV):
                    g = g + r_ref[s, :tr, :C].astype(F32)
                d, nm, nv = _adam_update(g, w_ref[0], m_ref[0], v_ref[0])
                g_ref[0], d_ref[0], nm_ref[0], nv_ref[0] = g, d, nm, nv

    blk = pl.BlockSpec((1, tr, C), lambda l, i: (l, i, 0))
    r_specs = [pl.BlockSpec((N_DEV, rp, Cp), lambda l, i, k=k: (0, jnp.where(l == k, i, 0), 0)) for k in range(L)]
    est = 2 * _nbytes((N_DEV, rp, Cp), recvs[0].dtype) + 8 * _nbytes((tr, Cp), F32)
    return pl.pallas_call(
        body, name=name, grid=(L, R // tr),
        in_specs=r_specs + [blk] * 3, out_specs=[blk] * 4,
        out_shape=[jax.ShapeDtypeStruct((L, R, C), F32)] * 4,
        compiler_params=_cparams(est, ("arbitrary", "arbitrary")),
    )(*recvs, w, m, v)


def _adamw_small(recv, w, m, v):
    def body(r_ref, w_ref, m_ref, v_ref, g_ref, d_ref, nm_ref, nv_ref):
        g = r_ref[0]
        for s in range(1, N_DEV):
            g = g + r_ref[s]
        g_ref[...] = g
        d_ref[...], nm_ref[...], nv_ref[...] = _adam_update(g, w_ref[...], m_ref[...], v_ref[...])

    vm = pl.BlockSpec(memory_space=pltpu.VMEM)
    return pl.pallas_call(
        body, name="adamw_small", in_specs=[vm] * 4, out_specs=[vm] * 4,
        out_shape=[jax.ShapeDtypeStruct((SMALL_ROWS, LANES), F32)] * 4,
        compiler_params=_cparams(20 * _nbytes((SMALL_ROWS, LANES), F32)),
    )(recv, w, m, v)


def kernel(x, w_in, conv_w, a_log, dt_bias, o_norm_w, sgu_ln_g, sgu_ln_b, w_s, b_s, w_pa, w_pb, w_o, ln1_g, ln1_b, w_ffn_gate, w_ffn_up, w_ffn_down, ln2_g, ln2_b, loss_target, m_w_in, m_conv_w, m_a_log, m_dt_bias, m_o_norm_w, m_sgu_ln_g, m_sgu_ln_b, m_w_s, m_b_s, m_w_pa, m_w_pb, m_w_o, m_ln1_g, m_ln1_b, m_w_ffn_gate, m_w_ffn_up, m_w_ffn_down, m_ln2_g, m_ln2_b, v_w_in, v_conv_w, v_a_log, v_dt_bias, v_o_norm_w, v_sgu_ln_g, v_sgu_ln_b, v_w_s, v_b_s, v_w_pa, v_w_pb, v_w_o, v_ln1_g, v_ln1_b, v_w_ffn_gate, v_w_ffn_up, v_w_ffn_down, v_ln2_g, v_ln2_b):
    given = dict(locals())
    P = {n: given[n] for n in WEIGHT_NAMES}
    M = {n: given["m_" + n] for n in WEIGHT_NAMES}
    V = {n: given["v_" + n] for n in WEIGHT_NAMES}

    wire = _wire_blocks(P)
    units = [(n, l) for l in range(DEPTH) for n in EARLY + LATE]
    whole = [False] * len(units)
    g_sems, g_srcs, g_lands, g_token = _exchange_start("gather_start", [wire[n][l] for n, l in units], whole)

    held, g_token = _hold([t[n] for n in UNALIGNED for t in (P, M, V)], g_token)
    adam_in = {n: (P[n], M[n], V[n]) for n in WIRE}
    adam_in.update({n: held[3 * i:3 * i + 3] for i, n in enumerate(UNALIGNED)})

    def gathered(name, names, l, after):
        idx = [units.index((n, l)) for n in names]
        got = _exchange_wait(name, g_sems, [g_srcs[i] for i in idx], [g_lands[i] for i in idx], idx, whole, after)
        return dict(zip(names, got))

    def layer(l):
        def weights(x_in):
            after = g_token if l == 0 else x_in
            early = _early_weights(gathered(f"gather_wait_early{l}", EARLY, l, after), P, l)
            return early, lambda ya: _late_weights(gathered(f"gather_wait_late{l}", LATE, l, ya))
        return weights

    pending = {}

    def on_grads(l, part, g):
        if part == "late":
            srcs, names = _late_slots(g), LATE
            per_slot = [True] * len(srcs)
        else:
            slots, small = _early_slots(g)
            srcs, names = slots + [small], EARLY + ("small",)
            per_slot = [True] * len(slots) + [False]
        sems, s_thru, l_thru, token = _exchange_start(f"exchange_start_{part}{l}", srcs, per_slot)
        pending[l, part] = (names, sems, s_thru, l_thru, per_slot)
        return token[0, 0]

    loss_local, dx, _ = _local_step(x[0], loss_target[0], [layer(l) for l in range(DEPTH)], on_grads)
    loss = lax.psum(loss_local, MESH_AXES)

    recv = [{} for _ in range(DEPTH)]

    def received(l, part, after):
        names, sems, s_thru, l_thru, per_slot = pending[l, part]
        got = _exchange_wait(f"exchange_wait_{part}{l}", sems, s_thru, l_thru, list(range(len(s_thru))), per_slot, after)
        recv[l].update(zip(names, got))

    out = {}

    def adamw(names):
        for n in names:
            tr, name = ADAM_TILES[n]
            out[n] = _adamw([recv[l][n] for l in range(DEPTH)], *adam_in[n], tr=tr, name=name)

    for l in reversed(range(DEPTH)):
        received(l, "late", dx)
    adamw(LATE)
    for l in reversed(range(DEPTH)):
        received(l, "early", out[LATE[-1]][0])
    adamw(EARLY)
    small = [_adamw_small(recv[l]["small"], *[_small_pack({n: T[n][l] for n, _ in SMALL}) for T in (P, M, V)])
             for l in range(DEPTH)]
    for n, _ in SMALL:
        out[n] = [jnp.stack([_small_unpack(small[l][i], P)[n] for l in range(DEPTH)]) for i in range(4)]
    return (loss, dx[None], *[out[n][i] for i in range(4) for n in WEIGHT_NAMES])
```

```python
import functools
import math

import jax
import jax.numpy as jnp
from jax import lax
from jax.experimental import pallas as pl
from jax.experimental.pallas import tpu as pltpu

F32 = jnp.float32
BF = jnp.bfloat16
HIGHEST = lax.Precision.HIGHEST

D_MODEL = 1024
DEPTH = 2
N_HEADS = 8
D_HEAD = 128
CONV_K = 4
SGU_BLOCK = 128
SGU_GROUPS = 8
SGU_CHUNK = 64
FFN_HIDDEN = 2816
N_IN = 8208
N_DEV = 8
IN_BLOCK, IN_PAD = N_IN // N_DEV, 1152
FFN_BLOCK, FFN_PAD = FFN_HIDDEN // N_DEV, 384
FFN_K = N_DEV * FFN_PAD
ALPHA = (2 * DEPTH) ** 0.25
LN_EPS = 1e-5
RMS_EPS = 1e-6
ADAM_LR, ADAM_B1, ADAM_B2, ADAM_EPS, ADAM_WD, ADAM_STEP = 0.001, 0.9, 0.999, 1e-08, 0.01, 10

MESH_AXES = ("x", "y", "c")
DELTA_CHUNK = 128
DELTA_HEADS_PER_STEP = 8
LANES = 128
SUBLANES = 8
VMEM_BYTES = 64 * 1024 * 1024
HALO = SUBLANES
HALO_BF = 2 * SUBLANES


def _cparams(est_bytes, dims=None):
    limit = int(min(max(2 * est_bytes + (8 << 20), 32 << 20), VMEM_BYTES - (6 << 20)))
    kw = dict(vmem_limit_bytes=limit)
    if dims is not None:
        kw["dimension_semantics"] = dims
    return pltpu.CompilerParams(**kw)


def _nbytes(shape, dtype):
    return math.prod(shape) * jnp.dtype(dtype).itemsize


def _dims(kind, ndim):
    lhs, rhs = {"nn": (1, 0), "nt": (1, 1), "tn": (0, 0)}[kind]
    b = ndim - 2
    return (((lhs + b,), (rhs + b,)), (tuple(range(b)), tuple(range(b))))


def _mxu(a, b, kind):
    return lax.dot_general(a, b, _dims(kind, a.ndim), preferred_element_type=F32)


def _dot(a, b):
    return _mxu(a.astype(BF), b.astype(BF), "nn")


def _dot_nt(a, b):
    return _mxu(a.astype(BF), b.astype(BF), "nt")


def _dot_tn(a, b):
    return _mxu(a.astype(BF), b.astype(BF), "tn")


def _split(a):
    hi = a.astype(BF)
    return hi, (a - hi.astype(F32)).astype(BF)


def _dot3(a, b, kind):
    (ah, al), (bh, bl) = _split(a), _split(b)
    return _mxu(ah, bh, kind) + (_mxu(ah, bl, kind) + _mxu(al, bh, kind))


def _dotf(a, b):
    return _dot3(a, b, "nn")


def _dotf_nt(a, b):
    return _dot3(a, b, "nt")


def _dot01(sel, x, kind="nn"):
    s = jnp.broadcast_to(sel.astype(BF), x.shape[:-2] + sel.shape)
    h1 = x.astype(BF)
    r1 = x - h1.astype(F32)
    h2 = r1.astype(BF)
    h3 = (r1 - h2.astype(F32)).astype(BF)
    return _mxu(s, h1, kind) + (_mxu(s, h2, kind) + _mxu(s, h3, kind))


def _sigmoid(x):
    return 0.5 * jnp.tanh(0.5 * x) + 0.5


def _silu(x):
    return x * _sigmoid(x)


def _silu_and_grad(x):
    s = _sigmoid(x)
    return x * s, s * (1.0 + x * (1.0 - s))


def _softplus(x):
    return jnp.maximum(x, 0.0) + jnp.log1p(jnp.exp(-jnp.abs(x)))


def _ln(x, g, b):
    mu = jnp.mean(x, -1, keepdims=True)
    xc = x - mu
    var = jnp.mean(xc * xc, -1, keepdims=True)
    return xc * lax.rsqrt(var + LN_EPS) * g + b


def _iota(shape, dim):
    return lax.broadcasted_iota(jnp.int32, shape, dim)


def _tile(n, pref, align):
    if n <= pref:
        return n
    t = (pref // align) * align
    while t >= align:
        if n % t == 0:
            return t
        t -= align
    raise ValueError(f"no tile for {n} (pref {pref}, align {align})")


def _bcast_rows(v, rows=SUBLANES):
    return jnp.broadcast_to(v, (rows, v.shape[-1]))


def _mm(a, b, *, mode, name, out_dtype=F32, add=None, add_scale=1.0, tm=512, tn=1024, tk=1024, cols=None, pair=None):
    if mode == "nn":
        (M, K), N = a.shape, b.shape[1]
    elif mode == "nt":
        (M, K), N = a.shape, b.shape[0]
    else:
        (K, M), N = a.shape, b.shape[1]
    col0 = 0
    if cols is not None:
        col0, N = cols
    tm = _tile(M, tm, LANES if mode == "tn" else SUBLANES * 2)
    tn = _tile(N, tn, LANES)
    tk = _tile(K, tk, LANES)
    nk = K // tk
    j0 = col0 // tn
    if mode == "nn":
        a_spec = pl.BlockSpec((tm, tk), lambda i, j, k: (i, k))
        b_spec = pl.BlockSpec((tk, tn), lambda i, j, k: (k, j + j0))
        dot = _dot
    elif mode == "nt":
        a_spec = pl.BlockSpec((tm, tk), lambda i, j, k: (i, k))
        b_spec = pl.BlockSpec((tn, tk), lambda i, j, k: (j, k))
        dot = _dot_nt
    else:
        a_spec = pl.BlockSpec((tk, tm), lambda i, j, k: (k, i))
        b_spec = pl.BlockSpec((tk, tn), lambda i, j, k: (k, j))
        dot = _dot_tn
    o_spec = pl.BlockSpec((tm, tn), lambda i, j, k: (i, j))
    has_add = add is not None

    n_ab = 2 if pair is None else 4

    def body(*refs):
        ab, (o_ref, acc_ref) = refs[:n_ab], refs[-2:]
        add_ref = refs[n_ab] if has_add else None
        k = pl.program_id(2)
        part = dot(ab[0][...], ab[1][...])
        if pair is not None:
            part = part + dot(ab[2][...], ab[3][...])

        def finish(total):
            if has_add:
                total = total + add_scale * add_ref[...]
            o_ref[...] = total.astype(out_dtype)

        if nk == 1:
            finish(part)
        else:
            @pl.when(k == 0)
            def _():
                acc_ref[...] = part

            @pl.when(jnp.logical_and(k > 0, k < nk - 1))
            def _():
                acc_ref[...] += part

            @pl.when(k == nk - 1)
            def _():
                finish(acc_ref[...] + part)

    in_specs = [a_spec, b_spec] * (n_ab // 2) + ([o_spec] if has_add else [])
    args = (a, b) + (tuple(pair) if pair is not None else ()) + ((add,) if has_add else ())
    est = ((n_ab // 2) * (_nbytes((tm, tk), a.dtype) + _nbytes((tk, tn), b.dtype)) + 2 * _nbytes((tm, tn), F32)
           + (_nbytes((tm, tn), F32) if has_add else 0)) + 2 * _nbytes((tm, tn), F32)
    return pl.pallas_call(
        body, name=name,
        grid=(M // tm, N // tn, nk),
        in_specs=in_specs, out_specs=o_spec,
        out_shape=jax.ShapeDtypeStruct((M, N), out_dtype),
        scratch_shapes=[pltpu.VMEM((tm, tn) if nk > 1 else (SUBLANES, LANES), F32)],
        compiler_params=_cparams(est, ("parallel", "parallel", "arbitrary")),
    )(*args)


def _shifted(xt, halo, first):
    halo = jnp.where(first, 0.0, halo)
    xc = jnp.concatenate([halo, xt], axis=0)
    return [xt] + [pltpu.roll(xc, s, 0)[HALO:] for s in range(1, CONV_K)]


def _conv_taps(shifted, w_ref):
    out = shifted[0] * w_ref[CONV_K - 1:CONV_K, :]
    for s in range(1, CONV_K):
        out = out + shifted[s] * w_ref[CONV_K - 1 - s:CONV_K - s, :]
    return out


def _gates(ba, arow, dtrow):
    lane = _iota(ba.shape, 1)
    beta = _sigmoid(ba)
    g = -jnp.exp(arow) * _softplus(ba + dtrow)
    return jnp.where(lane < N_HEADS, beta, jnp.where(lane < 2 * N_HEADS, g, 0.0))


def _l2n(x):
    return x * lax.rsqrt(jnp.sum(x * x, -1, keepdims=True) + RMS_EPS)


def _qkv_prep(proj, ba, convw, arow, dtrow, *, tm=256):
    S = proj.shape[0]
    tm = _tile(S, tm, SUBLANES)
    W3 = 3 * D_MODEL
    hb = tm // HALO

    def body(xt_ref, halo_ref, ba_ref, w_ref, a_ref, dt_ref, q_ref, k_ref, v_ref, gb_ref, c_ref):
        c = _conv_taps(_shifted(xt_ref[...], halo_ref[...], pl.program_id(0) == 0), w_ref)
        c_ref[...] = c
        c = _silu(c)
        for h in range(N_HEADS):
            lo = h * D_HEAD
            q_ref[:, lo:lo + D_HEAD] = _l2n(c[:, lo:lo + D_HEAD])
            k_ref[:, lo:lo + D_HEAD] = _l2n(c[:, D_MODEL + lo:D_MODEL + lo + D_HEAD])
        v_ref[...] = c[:, 2 * D_MODEL:]
        gb_ref[...] = _gates(ba_ref[...], a_ref[...], dt_ref[...])

    row = lambda w, col=0: pl.BlockSpec((tm, w), lambda i: (i, col))
    full = lambda shape: pl.BlockSpec(shape, lambda i: (0,) * len(shape))
    est = 4 * _nbytes((tm, W3), F32)
    return pl.pallas_call(
        body, name="qkv_prep", grid=(S // tm,),
        in_specs=[row(W3), pl.BlockSpec((HALO, W3), lambda i: (jnp.maximum(i * hb - 1, 0), 0)), row(LANES),
                  full((CONV_K, W3)), full((1, LANES)), full((1, LANES))],
        out_specs=[row(D_MODEL), row(D_MODEL), row(D_MODEL), row(LANES), row(W3)],
        out_shape=[jax.ShapeDtypeStruct((S, D_MODEL), F32)] * 3 + [jax.ShapeDtypeStruct((S, LANES), F32),
                                                                   jax.ShapeDtypeStruct((S, W3), F32)],
        compiler_params=_cparams(est, ("arbitrary",)),
    )(proj, proj, ba, convw, arow, dtrow)


def _qkv_prep_bwd(proj, conv_out, ba, arow, dtrow, dq, dk, dv, dgb, *, tm=256):
    S = proj.shape[0]
    tm = _tile(S, tm, SUBLANES * 2)
    W3 = 3 * D_MODEL
    hb = tm // HALO

    def body(xt_ref, halo_ref, c_ref, ba_ref, a_ref, dt_ref, dq_ref, dk_ref, dv_ref, dgb_ref,
             dcb_ref, dba_ref, dw_ref, da_ref, ddt_ref, dc_ref):
        i = pl.program_id(0)

        @pl.when(i == 0)
        def _():
            dw_ref[...] = jnp.zeros_like(dw_ref)
            da_ref[...] = jnp.zeros_like(da_ref)
            ddt_ref[...] = jnp.zeros_like(ddt_ref)

        shifted = _shifted(xt_ref[...], halo_ref[...], i == 0)
        a, ds = _silu_and_grad(c_ref[...])
        for h in range(N_HEADS):
            for base, d_ref in ((0, dq_ref), (D_MODEL, dk_ref)):
                lo = base + h * D_HEAD
                _, vj = jax.vjp(_l2n, a[:, lo:lo + D_HEAD])
                (dx,) = vj(d_ref[:, h * D_HEAD:(h + 1) * D_HEAD])
                dc_ref[:, lo:lo + D_HEAD] = dx * ds[:, lo:lo + D_HEAD]
        dc_ref[:, 2 * D_MODEL:] = dv_ref[...] * ds[:, 2 * D_MODEL:]
        dc = dc_ref[...]
        dcb_ref[...] = dc.astype(BF)
        for s in range(CONV_K):
            kk = CONV_K - 1 - s
            dw_ref[kk:kk + 1, :] += jnp.sum(dc * shifted[s], axis=0, keepdims=True)
        _, vj = jax.vjp(_gates, ba_ref[...], a_ref[...], dt_ref[...])
        dba, da, ddt = vj(dgb_ref[...])
        dba_ref[...] = dba.astype(BF)
        da_ref[...] += _bcast_rows(da)
        ddt_ref[...] += _bcast_rows(ddt)

    row = lambda w, col=0: pl.BlockSpec((tm, w), lambda i: (i, col))
    full = lambda shape: pl.BlockSpec(shape, lambda i: (0,) * len(shape))
    est = 8 * _nbytes((tm, W3), F32)
    return pl.pallas_call(
        body, name="qkv_prep_bwd", grid=(S // tm,),
        in_specs=[row(W3), pl.BlockSpec((HALO, W3), lambda i: (jnp.maximum(i * hb - 1, 0), 0)), row(W3), row(LANES),
                  full((1, LANES)), full((1, LANES)),
                  row(D_MODEL), row(D_MODEL), row(D_MODEL), row(LANES)],
        out_specs=[row(W3), row(LANES), full((SUBLANES, W3)), full((SUBLANES, LANES)), full((SUBLANES, LANES))],
        out_shape=[jax.ShapeDtypeStruct((S, W3), BF), jax.ShapeDtypeStruct((S, LANES), BF),
                   jax.ShapeDtypeStruct((SUBLANES, W3), F32), jax.ShapeDtypeStruct((SUBLANES, LANES), F32),
                   jax.ShapeDtypeStruct((SUBLANES, LANES), F32)],
        scratch_shapes=[pltpu.VMEM((tm, W3), F32)],
        compiler_params=_cparams(est, ("arbitrary",)),
    )(proj, proj, conv_out, ba, arow, dtrow, dq, dk, dv, dgb)


def _conv_bwd(dc, convw, dproj, *, tm=256):
    S, W3 = dc.shape
    tm = _tile(S, tm, HALO_BF)
    hb = tm // HALO_BF
    nt = S // tm

    def body(dc_ref, nxt_ref, w_ref, dproj_ref, o_ref):
        last = pl.program_id(0) == nt - 1
        nxt = jnp.where(last, 0.0, nxt_ref[...].astype(F32))
        cur = dc_ref[...].astype(F32)
        xc = jnp.concatenate([cur, nxt], axis=0)
        out = cur * w_ref[CONV_K - 1:CONV_K, :]
        for s in range(1, CONV_K):
            out = out + pltpu.roll(xc, tm + HALO_BF - s, 0)[:tm] * w_ref[CONV_K - 1 - s:CONV_K - s, :]
        o_ref[...] = out.astype(BF)

    est = 5 * _nbytes((tm, W3), F32)
    return pl.pallas_call(
        body, name="conv_bwd", grid=(nt,),
        in_specs=[pl.BlockSpec((tm, W3), lambda i: (i, 0)),
                  pl.BlockSpec((HALO_BF, W3), lambda i: (jnp.minimum((i + 1) * hb, S // HALO_BF - 1), 0)),
                  pl.BlockSpec((CONV_K, W3), lambda i: (0, 0)), pl.BlockSpec(memory_space=pl.ANY)],
        out_specs=pl.BlockSpec((tm, W3), lambda i: (i, 0)),
        out_shape=jax.ShapeDtypeStruct(dproj.shape, BF),
        input_output_aliases={3: 0},
        compiler_params=_cparams(est, ("parallel",)),
    )(dc, dc, convw, dproj)


NEUMANN_BLOCK = 8


def _inv_unit_lower(A):
    C = A.shape[-1]
    row, col = _iota((C, C), 0), _iota((C, C), 1)
    eye = jnp.where(row == col, 1.0, 0.0).astype(F32)
    Ab = A.astype(BF)
    sh = jnp.int32(int(math.log2(NEUMANN_BLOCK)))
    B = jnp.where(lax.shift_right_logical(row, sh) == lax.shift_right_logical(col, sh), Ab, jnp.zeros_like(Ab))
    B2 = _mxu(B, B, "nn")
    B4 = _dot3(B2, B2, "nn")
    b2h, b2l = _split(B2)
    P = eye - B.astype(F32) + B2 - (_mxu(B, b2h, "nn") + _mxu(B, b2l, "nn"))
    T = P + _dot3(P, B4, "nn")
    b = NEUMANN_BLOCK
    while b < C:
        hi = ~(2 * b - 1)
        off = ((row & hi) == (col & hi)) & ((row & b) != 0) & ((col & b) == 0)
        Aoff = jnp.where(off, Ab, jnp.zeros_like(Ab))
        th, tl = _split(T)
        xh, xl = _split(_mxu(th, Aoff, "nn") + _mxu(tl, Aoff, "nn"))
        T = T - (_mxu(xh, th, "nn") + (_mxu(xh, tl, "nn") + _mxu(xl, th, "nn")))
        b *= 2
    return T


def _delta_common(q, k, g, beta):
    C = q.shape[-2]
    row, col = _iota((C, C), 0), _iota((C, C), 1)
    tril = row >= col
    qs = q * (D_HEAD ** -0.5)
    gcb = _dot01(jnp.where(tril, 1.0, 0.0), jnp.broadcast_to(g, g.shape[:-1] + (LANES,)))
    gc = gcb[..., :1]
    gr = jnp.swapaxes(gcb, -1, -2)
    Dm = jnp.exp(jnp.where(tril, gc - gr, -1e30))
    Dmt = jnp.exp(jnp.where(row <= col, gr - gc, -1e30))
    eg = jnp.exp(gc)
    gl = jnp.sum(jnp.where(_iota((C, 1), 0) == C - 1, gc, 0.0), axis=(-2, -1), keepdims=True)
    el = jnp.exp(gl)
    er = jnp.exp(gl - gc)
    kb = k * beta
    KK = _dot_nt(kb, k)
    QK = _dot_nt(qs, k)
    return dict(row=row, col=col, tril=tril, qs=qs, gc=gc, Dm=Dm, Dmt=Dmt, eg=eg, el=el, er=er, kb=kb, KK=KK, QK=QK)


def _delta_chunk_fwd(S0, q, k, v, g, beta):
    m = _delta_common(q, k, g, beta)
    T = _inv_unit_lower(jnp.where(m["row"] > m["col"], m["KK"] * m["Dm"], 0.0))
    u = _dotf(T, v * beta)
    w = _dotf(T, m["kb"] * m["eg"])
    vn = u - _dot(w, S0)
    o = _dot(m["qs"] * m["eg"], S0) + _dot(m["QK"] * m["Dm"], vn)
    S1 = S0 * m["el"] + _dot_tn(k * m["er"], vn)
    return o, S1, jnp.swapaxes(T, -1, -2), u, w


def _delta_chunk_bwd(S0, q, k, v, g, beta, Tt, u, w, do, dS1):
    m = _delta_common(q, k, g, beta)
    C = q.shape[-2]
    qs, Dm, Dmt, eg, el, er, kb, KK, QK = (m[n] for n in ("qs", "Dm", "Dmt", "eg", "el", "er", "kb", "KK", "QK"))
    strict = m["row"] > m["col"]
    total = lambda x: jnp.sum(x, axis=(-2, -1), keepdims=True)
    vn = u - _dot(w, S0)
    qg = qs * eg
    kr = k * er

    dvn = _dot(_dot_nt(k, qs) * Dmt, do) + _dot(kr, dS1)
    dS0 = dS1 * el + _dot_tn(qg, do) - _dot_tn(w, dvn)
    d_el = total(dS1 * S0)
    dqg = _dot_nt(do, S0)
    dqs = dqg * eg
    deg = jnp.sum(dqg * qs, -1, keepdims=True)
    dP = _dot_nt(do, vn)
    dPD = dP * Dm
    dqs = dqs + _dot(dPD, k)
    dk = _dot(_dot_nt(vn, do) * Dmt, qs)
    dD = dP * QK
    dkr = _dot_nt(vn, dS1)
    dk = dk + dkr * er
    der = jnp.sum(dkr * k, -1, keepdims=True)
    dw = -_dot_nt(dvn, S0)
    th, tl = _split(Tt)

    def tt_times(x):
        xh, xl = _split(x)
        return _mxu(th, xh, "nn") + (_mxu(th, xl, "nn") + _mxu(tl, xh, "nn"))

    dru = tt_times(dvn)
    drw = tt_times(dw)
    dA = -(_dotf_nt(dru, u) + _dotf_nt(drw, w))
    dAm = jnp.where(strict, dA, 0.0)
    dKK = dAm * Dm
    dkb = _dot(dKK, k)
    dk = dk + _dot_tn(dKK, kb)
    dD = dD + dAm * KK
    dv = dru * beta
    dbeta = jnp.sum(dru * v, -1, keepdims=True)
    dkb = dkb + drw * eg
    deg = deg + jnp.sum(drw * kb, -1, keepdims=True)
    dk = dk + dkb * beta
    dbeta = dbeta + jnp.sum(dkb * k, -1, keepdims=True)
    E = dD * Dm
    dgc = jnp.sum(E, -1, keepdims=True) - jnp.sum(jnp.swapaxes(E, -1, -2), -1, keepdims=True)
    dgc = dgc + deg * eg - der * er
    dgl = total(der * er) + d_el * el
    dgc = dgc + jnp.where(_iota((C, 1), 0) == C - 1, dgl, 0.0)
    triu = jnp.where(m["row"] <= m["col"], 1.0, 0.0)
    dg = _dot01(triu, jnp.broadcast_to(dgc, dgc.shape[:-1] + (LANES,)))[..., :1]
    dq = dqs * (D_HEAD ** -0.5)
    return dq, dk, dv, dg, dbeta, dS0


def _head_cols(gb, h):
    lane = _iota(gb.shape, 1)
    beta = jnp.sum(jnp.where(lane == h, gb, 0.0), -1, keepdims=True)
    g = jnp.sum(jnp.where(lane == N_HEADS + h, gb, 0.0), -1, keepdims=True)
    return g, beta


def _delta_fwd(q, k, v, gb):
    S = q.shape[0]
    C = DELTA_CHUNK
    N = S // C

    HB = DELTA_HEADS_PER_STEP

    def body(q_ref, k_ref, v_ref, gb_ref, o_ref, st_ref, t_ref, u_ref, w_ref, s_scr):
        n, hb = pl.program_id(0), pl.program_id(1)
        gb = gb_ref[...]

        @pl.when(n == 0)
        def _():
            for hh in range(HB):
                s_scr[hb * HB + hh] = jnp.zeros((D_HEAD, D_HEAD), F32)

        heads = [hb * HB + hh for hh in range(HB)]
        cols = [slice(hh * D_HEAD, (hh + 1) * D_HEAD) for hh in range(HB)]
        per_head = lambda ref: jnp.stack([ref[:, c] for c in cols])
        g, beta = (jnp.stack(t) for t in zip(*[_head_cols(gb, h) for h in heads]))
        S0 = jnp.stack([s_scr[h] for h in heads])
        o, S1, Tt, u, w = _delta_chunk_fwd(S0, per_head(q_ref), per_head(k_ref), per_head(v_ref), g, beta)
        for hh in range(HB):
            st_ref[hh, 0] = S0[hh]
            t_ref[hh, 0] = Tt[hh]
            o_ref[:, cols[hh]] = o[hh]
            u_ref[:, cols[hh]] = u[hh]
            w_ref[:, cols[hh]] = w[hh]
            s_scr[heads[hh]] = S1[hh]

    hd = pl.BlockSpec((C, HB * D_HEAD), lambda n, h: (n, h))
    mat = pl.BlockSpec((HB, 1, D_HEAD, D_HEAD), lambda n, h: (h, n, 0, 0))
    est = 40 * HB * _nbytes((C, D_HEAD), F32)
    seq = jax.ShapeDtypeStruct((S, N_HEADS * D_HEAD), F32)
    return pl.pallas_call(
        body, name="delta_fwd", grid=(N, N_HEADS // HB),
        in_specs=[hd, hd, hd, pl.BlockSpec((C, LANES), lambda n, h: (n, 0))],
        out_specs=[hd, mat, mat, hd, hd],
        out_shape=[seq, jax.ShapeDtypeStruct((N_HEADS, N, D_HEAD, D_HEAD), F32),
                   jax.ShapeDtypeStruct((N_HEADS, N, C, C), F32), seq, seq],
        scratch_shapes=[pltpu.VMEM((N_HEADS, D_HEAD, D_HEAD), F32)],
        compiler_params=_cparams(est, ("arbitrary", "arbitrary")),
    )(q, k, v, gb)


def _delta_bwd(q, k, v, gb, st, tinv, u, w, do):
    S = q.shape[0]
    C = DELTA_CHUNK
    N = S // C

    HB = DELTA_HEADS_PER_STEP

    def body(q_ref, k_ref, v_ref, gb_ref, st_ref, t_ref, u_ref, w_ref, do_ref, dq_ref, dk_ref, dv_ref, dgb_ref, ds_scr):
        n, hb = pl.program_id(0), pl.program_id(1)
        gb = gb_ref[...]
        lane = _iota((C, LANES), 1)
        dgb = jnp.zeros((C, LANES), F32)

        @pl.when(n == 0)
        def _():
            for hh in range(HB):
                ds_scr[hb * HB + hh] = jnp.zeros((D_HEAD, D_HEAD), F32)

        heads = [hb * HB + hh for hh in range(HB)]
        cols = [slice(hh * D_HEAD, (hh + 1) * D_HEAD) for hh in range(HB)]
        per_head = lambda ref: jnp.stack([ref[:, c] for c in cols])
        g, beta = (jnp.stack(t) for t in zip(*[_head_cols(gb, h) for h in heads]))
        dS1 = jnp.stack([ds_scr[h] for h in heads])
        dq, dk, dv, dg, dbeta, dS0 = _delta_chunk_bwd(
            st_ref[:, 0], per_head(q_ref), per_head(k_ref), per_head(v_ref), g, beta, t_ref[:, 0],
            per_head(u_ref), per_head(w_ref), per_head(do_ref), dS1)
        for hh, h in enumerate(heads):
            dq_ref[:, cols[hh]] = dq[hh]
            dk_ref[:, cols[hh]] = dk[hh]
            dv_ref[:, cols[hh]] = dv[hh]
            dgb = dgb + jnp.where(lane == h, dbeta[hh], 0.0) + jnp.where(lane == N_HEADS + h, dg[hh], 0.0)
            ds_scr[h] = dS0[hh]

        @pl.when(hb == 0)
        def _():
            dgb_ref[...] = dgb

        @pl.when(hb > 0)
        def _():
            dgb_ref[...] += dgb

    hd = pl.BlockSpec((C, HB * D_HEAD), lambda n, h: (N - 1 - n, h))
    mat = pl.BlockSpec((HB, 1, D_HEAD, D_HEAD), lambda n, h: (h, N - 1 - n, 0, 0))
    gbs = pl.BlockSpec((C, LANES), lambda n, h: (N - 1 - n, 0))
    est = 60 * HB * _nbytes((C, D_HEAD), F32)
    return pl.pallas_call(
        body, name="delta_bwd", grid=(N, N_HEADS // HB),
        in_specs=[hd, hd, hd, gbs, mat, mat, hd, hd, hd],
        out_specs=[hd, hd, hd, gbs],
        out_shape=[jax.ShapeDtypeStruct((S, N_HEADS * D_HEAD), F32)] * 3 + [jax.ShapeDtypeStruct((S, LANES), F32)],
        scratch_shapes=[pltpu.VMEM((N_HEADS, D_HEAD, D_HEAD), F32)],
        compiler_params=_cparams(est, ("arbitrary", "arbitrary")),
    )(q, k, v, gb, st, tinv, u, w, do)


def _ya_head(o, z, onw):
    return o * lax.rsqrt(jnp.mean(o * o, -1, keepdims=True) + RMS_EPS) * onw * _silu(z)


def _norm_cdf(x):
    return 0.5 * (1.0 + lax.erf(x * 0.7071067811865476))


def _norm_pdf(x):
    return jnp.exp(-0.5 * x * x) * 0.3989422804014327


def _chunk_causal(shape, di, dj):
    sh = jnp.int32(int(math.log2(SGU_CHUNK)))
    return lax.shift_right_logical(_iota(shape, di), sh) >= lax.shift_right_logical(_iota(shape, dj), sh)


def _ws_masked(ws):
    return jnp.where(_chunk_causal(ws.shape, 1, 2), ws, 0.0)


def _mix_prep(o, proj, onw, sg, sb, ws, bst, *, tm=256):
    S = o.shape[0]
    tm = _tile(S, tm, SGU_BLOCK)

    def body(o_ref, z_ref, u_ref, vg_ref, onw_ref, sg_ref, sb_ref, ws_ref, bst_ref, ya_ref, yb_ref, phi_ref):
        onw = onw_ref[...]
        for h in range(N_HEADS):
            sl = slice(h * D_HEAD, (h + 1) * D_HEAD)
            ya_ref[:, sl] = _ya_head(o_ref[:, sl], z_ref[:, sl].astype(F32), onw).astype(BF)
        u, vg = u_ref[...].astype(F32), vg_ref[...].astype(F32)
        phi_u, phi_v = _norm_cdf(u), _norm_cdf(vg)
        phi_ref[:, :D_MODEL] = phi_u
        phi_ref[:, D_MODEL:] = phi_v
        ua, vl = u * phi_u, _ln(vg * phi_v, sg_ref[...], sb_ref[...])
        wsm = _ws_masked(ws_ref[...])
        bst = bst_ref[...]
        for blk in range(tm // SGU_BLOCK):
            rs = slice(blk * SGU_BLOCK, (blk + 1) * SGU_BLOCK)
            for gi in range(SGU_GROUPS):
                cs = slice(gi * D_HEAD, (gi + 1) * D_HEAD)
                sp = _dot(wsm[gi], vl[rs, cs]) + bst[:, gi:gi + 1]
                yb_ref[rs, cs] = (ua[rs, cs] * sp).astype(BF)

    blk = lambda col: pl.BlockSpec((tm, D_MODEL), lambda i: (i, col))
    full = lambda shape: pl.BlockSpec(shape, lambda i: (0,) * len(shape))
    est = 10 * _nbytes((tm, D_MODEL), F32)
    return pl.pallas_call(
        body, name="mix_prep", grid=(S // tm,),
        in_specs=[blk(0), blk(0), blk(1), blk(2), full((1, D_HEAD)), full((1, D_MODEL)), full((1, D_MODEL)),
                  full((SGU_GROUPS, SGU_BLOCK, SGU_BLOCK)), full((SGU_BLOCK, LANES))],
        out_specs=[blk(0), blk(0), pl.BlockSpec((tm, 2 * D_MODEL), lambda i: (i, 0))],
        out_shape=[jax.ShapeDtypeStruct((S, D_MODEL), BF)] * 2 + [jax.ShapeDtypeStruct((S, 2 * D_MODEL), F32)],
        compiler_params=_cparams(est, ("parallel",)),
    )(o, proj, proj, proj, onw, sg, sb, ws, bst)


def _mix_prep_bwd(o, proj, phi, onw, sg, sb, ws, bst, dya, dyb, dproj, *, tm=256):
    S = o.shape[0]
    tm = _tile(S, tm, SGU_BLOCK)

    def body(o_ref, z_ref, u_ref, vg_ref, phi_ref, onw_ref, sg_ref, sb_ref, ws_ref, bst_ref, dya_ref, dyb_ref, dproj_in,
             do_ref, dzuv_ref, donw_ref, dsg_ref, dsb_ref, dws_ref, dbst_ref, dvl_scr, dua_scr):
        dz_ref, du_ref, dvg_ref = (dzuv_ref.at[:, k * D_MODEL:(k + 1) * D_MODEL] for k in range(3))
        @pl.when(pl.program_id(0) == 0)
        def _():
            for r in (donw_ref, dsg_ref, dsb_ref, dws_ref, dbst_ref):
                r[...] = jnp.zeros_like(r)

        onw = onw_ref[...]
        donw = jnp.zeros((1, D_HEAD), F32)
        for h in range(N_HEADS):
            sl = slice(h * D_HEAD, (h + 1) * D_HEAD)
            _, vj = jax.vjp(_ya_head, o_ref[:, sl], z_ref[:, sl].astype(F32), onw)
            do_h, dz_h, donw_h = vj(dya_ref[:, sl])
            do_ref[:, sl] = do_h.astype(BF)
            dz_ref[:, sl] = dz_h.astype(BF)
            donw = donw + donw_h
        donw_ref[...] += _bcast_rows(donw)

        u, vg = u_ref[...].astype(F32), vg_ref[...].astype(F32)
        phi_u, phi_v = phi_ref[:, :D_MODEL], phi_ref[:, D_MODEL:]
        ua = u * phi_u
        vl, vj = jax.vjp(_ln, vg * phi_v, sg_ref[...], sb_ref[...])
        wsm = _ws_masked(ws_ref[...])
        bst = bst_ref[...]
        lane = _iota((SGU_BLOCK, LANES), 1)
        dbst = jnp.zeros((SGU_BLOCK, LANES), F32)
        cmask = _chunk_causal((SGU_BLOCK, SGU_BLOCK), 0, 1)
        for gi in range(SGU_GROUPS):
            cs = slice(gi * D_HEAD, (gi + 1) * D_HEAD)
            wg = wsm[gi]
            wgt = jnp.transpose(wg)
            dwg = jnp.zeros((SGU_BLOCK, SGU_BLOCK), F32)
            for blk in range(tm // SGU_BLOCK):
                rs = slice(blk * SGU_BLOCK, (blk + 1) * SGU_BLOCK)
                sp = _dot(wg, vl[rs, cs]) + bst[:, gi:gi + 1]
                dyb = dyb_ref[rs, cs]
                dsp = dyb * ua[rs, cs]
                dua_scr[rs, cs] = dyb * sp
                dvl_scr[rs, cs] = _dot(wgt, dsp)
                dwg = dwg + _dot_nt(dsp, vl[rs, cs])
                dbst = dbst + jnp.where(lane == gi, jnp.sum(dsp, -1, keepdims=True), 0.0)
            dws_ref[gi] += jnp.where(cmask, dwg, 0.0)
        dbst_ref[...] += dbst
        dgv, dsg, dsb = vj(dvl_scr[...])
        du_ref[...] = (dua_scr[...] * (phi_u + u * _norm_pdf(u))).astype(BF)
        dvg_ref[...] = (dgv * (phi_v + vg * _norm_pdf(vg))).astype(BF)
        dsg_ref[...] += _bcast_rows(dsg)
        dsb_ref[...] += _bcast_rows(dsb)

    blk = lambda col: pl.BlockSpec((tm, D_MODEL), lambda i: (i, col))
    full = lambda shape: pl.BlockSpec(shape, lambda i: (0,) * len(shape))
    est = 16 * _nbytes((tm, D_MODEL), F32)
    outs = pl.pallas_call(
        body, name="mix_prep_bwd", grid=(S // tm,),
        in_specs=[blk(0), blk(0), blk(1), blk(2), pl.BlockSpec((tm, 2 * D_MODEL), lambda i: (i, 0)),
                  full((1, D_HEAD)), full((1, D_MODEL)), full((1, D_MODEL)),
                  full((SGU_GROUPS, SGU_BLOCK, SGU_BLOCK)), full((SGU_BLOCK, LANES)), blk(0), blk(0),
                  pl.BlockSpec(memory_space=pl.ANY)],
        out_specs=[blk(0), pl.BlockSpec((tm, 3 * D_MODEL), lambda i: (i, 1)),
                   full((SUBLANES, D_HEAD)), full((SUBLANES, D_MODEL)), full((SUBLANES, D_MODEL)),
                   full((SGU_GROUPS, SGU_BLOCK, SGU_BLOCK)), full((SGU_BLOCK, LANES))],
        out_shape=[jax.ShapeDtypeStruct((S, D_MODEL), BF), jax.ShapeDtypeStruct(dproj.shape, BF),
                   jax.ShapeDtypeStruct((SUBLANES, D_HEAD), F32), jax.ShapeDtypeStruct((SUBLANES, D_MODEL), F32),
                   jax.ShapeDtypeStruct((SUBLANES, D_MODEL), F32),
                   jax.ShapeDtypeStruct((SGU_GROUPS, SGU_BLOCK, SGU_BLOCK), F32),
                   jax.ShapeDtypeStruct((SGU_BLOCK, LANES), F32)],
        input_output_aliases={12: 1},
        scratch_shapes=[pltpu.VMEM((tm, D_MODEL), F32)] * 2,
        compiler_params=_cparams(est, ("arbitrary",)),
    )(o, proj, proj, proj, phi, onw, sg, sb, ws, bst, dya, dyb, dproj)
    return outs


def _mm_gate_merge(ya, yb, wpa, wpb, proj, *, tm=512):
    S = ya.shape[0]
    tm = _tile(S, tm, SUBLANES * 2)

    def body(ya_ref, yb_ref, wa_ref, wb_ref, ga_ref, gb_ref, pa_ref, pb_ref, m_ref):
        pa = _dot(ya_ref[...], wa_ref[...]).astype(BF)
        pb = _dot(yb_ref[...], wb_ref[...]).astype(BF)
        pa_ref[...] = pa
        pb_ref[...] = pb
        m_ref[...] = (_sigmoid(ga_ref[...].astype(F32)) * pa.astype(F32)
                      + _sigmoid(gb_ref[...].astype(F32)) * pb.astype(F32)).astype(BF)

    blk = lambda col: pl.BlockSpec((tm, D_MODEL), lambda i: (i, col))
    wsp = pl.BlockSpec((D_MODEL, D_MODEL), lambda i: (0, 0))
    return pl.pallas_call(
        body, name="mm_gate_merge", grid=(S // tm,),
        in_specs=[blk(0), blk(0), wsp, wsp, blk(3), blk(4)], out_specs=[blk(0)] * 3,
        out_shape=[jax.ShapeDtypeStruct((S, D_MODEL), BF)] * 3,
        compiler_params=_cparams(2 * _nbytes((D_MODEL, D_MODEL), BF) + 8 * _nbytes((tm, D_MODEL), F32), ("parallel",)),
    )(ya, yb, wpa, wpb, proj, proj)


def _mm_gate_merge_bwd(dmix, wo, pa, pb, proj, *, tm=512):
    S = pa.shape[0]
    tm = _tile(S, tm, SUBLANES * 2)

    def body(d_ref, w_ref, pa_ref, pb_ref, ga_ref, gb_ref, dpa_ref, dpb_ref, dg_ref):
        dm = _dot_nt(d_ref[...], w_ref[...])
        sa, sb = _sigmoid(ga_ref[...].astype(F32)), _sigmoid(gb_ref[...].astype(F32))
        dpa_ref[...] = (dm * sa).astype(BF)
        dpb_ref[...] = (dm * sb).astype(BF)
        dg_ref[:, :D_MODEL] = (dm * pa_ref[...].astype(F32) * sa * (1.0 - sa)).astype(BF)
        dg_ref[:, D_MODEL:] = (dm * pb_ref[...].astype(F32) * sb * (1.0 - sb)).astype(BF)

    blk = lambda col: pl.BlockSpec((tm, D_MODEL), lambda i: (i, col))
    est = _nbytes((D_MODEL, D_MODEL), BF) + 10 * _nbytes((tm, D_MODEL), F32)
    return pl.pallas_call(
        body, name="mm_gate_merge_bwd", grid=(S // tm,),
        in_specs=[blk(0), pl.BlockSpec((D_MODEL, D_MODEL), lambda i: (0, 0)), blk(0), blk(0), blk(3), blk(4)],
        out_specs=[blk(0), blk(0), pl.BlockSpec((tm, 2 * D_MODEL), lambda i: (i, 3))],
        out_shape=[jax.ShapeDtypeStruct((S, D_MODEL), BF)] * 2 + [jax.ShapeDtypeStruct((S, 8 * D_MODEL), BF)],
        compiler_params=_cparams(est, ("parallel",)),
    )(dmix, wo, pa, pb, proj, proj)


def _mm_swiglu(xb, wgt, wut, *, tm=1024, tn=768):
    S, K = xb.shape
    tm = _tile(S, tm, SUBLANES * 2)
    tn = _tile(FFN_K, tn, LANES)

    def body(x_ref, wg_ref, wu_ref, hg_ref, hu_ref, h_ref):
        x = x_ref[...]
        hg = _dot_nt(x, wg_ref[...]).astype(BF)
        hu = _dot_nt(x, wu_ref[...]).astype(BF)
        hg_ref[...] = hg
        hu_ref[...] = hu
        h_ref[...] = (_silu(hg.astype(F32)) * hu.astype(F32)).astype(BF)

    out = pl.BlockSpec((tm, tn), lambda i, j: (i, j))
    est = _nbytes((tm, K), BF) + 2 * _nbytes((K, tn), BF) + 6 * _nbytes((tm, tn), F32)
    return pl.pallas_call(
        body, name="mm_swiglu", grid=(S // tm, FFN_K // tn),
        in_specs=[pl.BlockSpec((tm, K), lambda i, j: (i, 0)), pl.BlockSpec((tn, K), lambda i, j: (j, 0)),
                  pl.BlockSpec((tn, K), lambda i, j: (j, 0))],
        out_specs=[out] * 3, out_shape=[jax.ShapeDtypeStruct((S, FFN_K), BF)] * 3,
        compiler_params=_cparams(est, ("parallel", "parallel")),
    )(xb, wgt, wut)


def _mm_swiglu_bwd(dffn, wd, hg, hu, *, tm=1024, tn=768):
    S, K = dffn.shape
    tm = _tile(S, tm, SUBLANES * 2)
    tn = _tile(FFN_K, tn, LANES)

    def body(d_ref, w_ref, hg_ref, hu_ref, dhg_ref, dhu_ref):
        dh = _dot_nt(d_ref[...], w_ref[...])
        act, dact = _silu_and_grad(hg_ref[...].astype(F32))
        dhg_ref[...] = (dh * hu_ref[...].astype(F32) * dact).astype(BF)
        dhu_ref[...] = (dh * act).astype(BF)

    out = pl.BlockSpec((tm, tn), lambda i, j: (i, j))
    est = _nbytes((tm, K), dffn.dtype) + _nbytes((tn, K), BF) + 8 * _nbytes((tm, tn), F32)
    return pl.pallas_call(
        body, name="mm_swiglu_bwd", grid=(S // tm, FFN_K // tn),
        in_specs=[pl.BlockSpec((tm, K), lambda i, j: (i, 0)), pl.BlockSpec((tn, K), lambda i, j: (j, 0)), out, out],
        out_specs=[out, out], out_shape=[jax.ShapeDtypeStruct((S, FFN_K), BF)] * 2,
        compiler_params=_cparams(est, ("parallel", "parallel")),
    )(dffn, wd, hg, hu)


def _mm_resid_ln(a, bmat, x, g, b, *, name, tm=512):
    S, K = a.shape
    tm = _tile(S, tm, SUBLANES * 2)

    def body(a_ref, w_ref, x_ref, g_ref, b_ref, pre_ref, y_ref, yb_ref):
        pre = ALPHA * x_ref[...] + _dot(a_ref[...], w_ref[...])
        y = _ln(pre, g_ref[...], b_ref[...])
        pre_ref[...] = pre
        y_ref[...] = y
        yb_ref[...] = y.astype(BF)

    blk = pl.BlockSpec((tm, D_MODEL), lambda i: (i, 0))
    vec = pl.BlockSpec((1, D_MODEL), lambda i: (0, 0))
    est = _nbytes((tm, K), BF) + _nbytes((K, D_MODEL), BF) + 8 * _nbytes((tm, D_MODEL), F32)
    return pl.pallas_call(
        body, name=name, grid=(S // tm,),
        in_specs=[pl.BlockSpec((tm, K), lambda i: (i, 0)), pl.BlockSpec((K, D_MODEL), lambda i: (0, 0)), blk, vec, vec],
        out_specs=[blk, blk, blk],
        out_shape=[jax.ShapeDtypeStruct((S, D_MODEL), F32)] * 2 + [jax.ShapeDtypeStruct((S, D_MODEL), BF)],
        compiler_params=_cparams(est, ("parallel",)),
    )(a, bmat, x, g, b)


def _ln_bwd(pre, g, b, dy, *, tm=512):
    S = pre.shape[0]
    tm = _tile(S, tm, SUBLANES)

    def body(p_ref, g_ref, b_ref, dy_ref, dp_ref, dg_ref, db_ref):
        @pl.when(pl.program_id(0) == 0)
        def _():
            dg_ref[...] = jnp.zeros_like(dg_ref)
            db_ref[...] = jnp.zeros_like(db_ref)

        _, vj = jax.vjp(_ln, p_ref[...], g_ref[...], b_ref[...])
        dp, dg, db = vj(dy_ref[...])
        dp_ref[...] = dp
        dg_ref[...] += _bcast_rows(dg)
        db_ref[...] += _bcast_rows(db)

    blk = pl.BlockSpec((tm, D_MODEL), lambda i: (i, 0))
    vec = pl.BlockSpec((1, D_MODEL), lambda i: (0, 0))
    acc = pl.BlockSpec((SUBLANES, D_MODEL), lambda i: (0, 0))
    return pl.pallas_call(
        body, name="ln_bwd", grid=(S // tm,),
        in_specs=[blk, vec, vec, blk], out_specs=[blk, acc, acc],
        out_shape=[jax.ShapeDtypeStruct((S, D_MODEL), F32)] + [jax.ShapeDtypeStruct((SUBLANES, D_MODEL), F32)] * 2,
        compiler_params=_cparams(10 * _nbytes((tm, D_MODEL), F32), ("arbitrary",)),
    )(pre, g, b, dy)


def _loss_ln_bwd(y, tgt, pre, g, b, *, tm=512):
    S = y.shape[0]
    tm = _tile(S, tm, SUBLANES)

    def body(y_ref, t_ref, p_ref, g_ref, b_ref, dp_ref, dg_ref, db_ref, l_ref):
        @pl.when(pl.program_id(0) == 0)
        def _():
            for r in (dg_ref, db_ref, l_ref):
                r[...] = jnp.zeros_like(r)

        e = y_ref[...] - t_ref[...]
        l_ref[...] += 0.5 * jnp.sum(jnp.mean(e * e, -1, keepdims=True), keepdims=True)
        _, vj = jax.vjp(_ln, p_ref[...], g_ref[...], b_ref[...])
        dp, dg, db = vj(e * (1.0 / D_MODEL))
        dp_ref[...] = dp
        dg_ref[...] += _bcast_rows(dg)
        db_ref[...] += _bcast_rows(db)

    blk = pl.BlockSpec((tm, D_MODEL), lambda i: (i, 0))
    vec = pl.BlockSpec((1, D_MODEL), lambda i: (0, 0))
    acc = pl.BlockSpec((SUBLANES, D_MODEL), lambda i: (0, 0))
    return pl.pallas_call(
        body, name="loss_ln_bwd", grid=(S // tm,),
        in_specs=[blk, blk, blk, vec, vec], out_specs=[blk, acc, acc, pl.BlockSpec((SUBLANES, LANES), lambda i: (0, 0))],
        out_shape=[jax.ShapeDtypeStruct((S, D_MODEL), F32)] + [jax.ShapeDtypeStruct((SUBLANES, D_MODEL), F32)] * 2
                  + [jax.ShapeDtypeStruct((SUBLANES, LANES), F32)],
        compiler_params=_cparams(12 * _nbytes((tm, D_MODEL), F32), ("arbitrary",)),
    )(y, tgt, pre, g, b)


def _layer_fwd(x, xb, w, late):
    pq = _mm(xb, w["win"], mode="nn", name="mm_in_qkv", tm=1024, tn=1024, cols=(0, 3 * D_MODEL))
    proj = _mm(xb, w["win"], mode="nn", name="mm_in_rest", tm=1024, tn=1024, cols=(3 * D_MODEL, 5 * D_MODEL), out_dtype=BF)
    ba = _mm(xb, w["wba"], mode="nn", name="mm_in_ba", tm=1024, tn=LANES)
    qn, kn, vv, gb, conv_out = _qkv_prep(pq, ba, w["convw"], w["arow"], w["dtrow"])
    o, st, tinv, wy_u, wy_w = _delta_fwd(qn, kn, vv, gb)
    ya, yb, phi = _mix_prep(o, proj, w["onw"], w["sg"], w["sb"], w["ws"], w["bst"])
    w = {**w, **late(ya)}
    pa, pb, m = _mm_gate_merge(ya, yb, w["wpa"], w["wpb"], proj)
    pre1, x1, x1b = _mm_resid_ln(m, w["wo"], x, w["ln1g"], w["ln1b"], name="mm_out_ln")
    hg, hu, h = _mm_swiglu(x1b, w["wgt"], w["wut"])
    pre2, x2, x2b = _mm_resid_ln(h, w["wd"], x1, w["ln2g"], w["ln2b"], name="mm_down_ln")
    saved = dict(xb=xb, pq=pq, conv_out=conv_out, proj=proj, phi=phi, ba=ba, qn=qn, kn=kn, vv=vv, gb=gb, o=o, st=st, tinv=tinv, wy_u=wy_u, wy_w=wy_w,
                 ya=ya, yb=yb,
                 pa=pa, pb=pb, m=m, pre1=pre1, x1b=x1b, hg=hg, hu=hu, h=h, pre2=pre2)
    return x2, x2b, saved, w


def _layer_bwd(dpre2, ln2_grads, w, s, on_part=None):
    g = {}
    started = lambda part: on_part(part, g) if on_part is not None else None
    after = lambda v, token: v if token is None else v + token.astype(v.dtype)
    g["ln2g"], g["ln2b"] = ln2_grads
    dhg, dhu = _mm_swiglu_bwd(dpre2, w["wd"], s["hg"], s["hu"])
    g["wd"] = _mm(s["h"], dpre2, mode="tn", name="mm_tn_down", tm=1536, tk=1024, out_dtype=BF)
    dx1 = _mm(dhg, w["wgt"], mode="nn", name="mm_nn_gu", pair=(dhu, w["wut"]), add=dpre2, add_scale=ALPHA, tm=1024, tk=1536)
    g["wgt"] = _mm(dhg, s["x1b"], mode="tn", name="mm_tn_gu", tm=1536, tn=1024, tk=2048, out_dtype=BF)
    g["wut"] = _mm(dhu, s["x1b"], mode="tn", name="mm_tn_gu", tm=1536, tn=1024, tk=2048, out_dtype=BF)
    dpre1, g["ln1g"], g["ln1b"] = _ln_bwd(s["pre1"], w["ln1g"], w["ln1b"], dx1)
    g["wo"] = _mm(s["m"], dpre1, mode="tn", name="mm_tn_sq", tm=1024, tk=1024, out_dtype=BF)
    dpa, dpb, dproj = _mm_gate_merge_bwd(dpre1, w["wo"], s["pa"], s["pb"], s["proj"])
    dya = _mm(dpa, w["wpa"], mode="nt", name="mm_nt_sq")
    g["wpa"] = _mm(s["ya"], dpa, mode="tn", name="mm_tn_sq", tm=1024, tk=1024, out_dtype=BF)
    dyb = _mm(dpb, w["wpb"], mode="nt", name="mm_nt_sq")
    g["wpb"] = _mm(s["yb"], dpb, mode="tn", name="mm_tn_sq", tm=1024, tk=1024, out_dtype=BF)
    do, dproj, g["onw"], g["sg"], g["sb"], g["ws"], g["bst"] = _mix_prep_bwd(
        s["o"], s["proj"], s["phi"], after(w["onw"], started("late")), w["sg"], w["sb"], w["ws"], w["bst"], dya, dyb, dproj)
    dqn, dkn, dvv, dgb = _delta_bwd(s["qn"], s["kn"], s["vv"], s["gb"], s["st"], s["tinv"], s["wy_u"], s["wy_w"], do)
    dc, dba, g["convw"], g["arow"], g["dtrow"] = _qkv_prep_bwd(
        s["pq"], s["conv_out"], s["ba"], w["arow"], w["dtrow"], dqn, dkn, dvv, dgb)
    dproj = _conv_bwd(dc, w["convw"], dproj)
    g["win"] = _mm(s["xb"], dproj, mode="tn", name="mm_tn_in", tm=1024, tn=1024, tk=2048, out_dtype=BF)
    g["wba"] = _mm(s["xb"], dba, mode="tn", name="mm_tn_ba", tm=1024, tn=LANES, tk=1024, out_dtype=BF)
    dx = _mm(dba, after(w["wba"], started("early")), mode="nt", name="mm_nt_ba", add=dpre1, add_scale=ALPHA, tm=1024)
    dx = _mm(dproj, w["win"], mode="nt", name="mm_nt_in", add=dx, add_scale=1.0, tm=1024, tk=2048)
    return dx, g


def _local_step(x, tgt, layers, on_grads=None):
    saved, weights = [], []
    xb = x.astype(BF)
    for layer in layers:
        x, xb, s, w = _layer_fwd(x, xb, *layer(x))
        saved.append(s)
        weights.append(w)
    last = len(layers) - 1
    dpre2, dg, db, lacc = _loss_ln_bwd(x, tgt, saved[last]["pre2"], weights[last]["ln2g"], weights[last]["ln2b"])
    grads = [None] * len(layers)
    for l in reversed(range(len(layers))):
        on_part = functools.partial(on_grads, l) if on_grads is not None else None
        dx, grads[l] = _layer_bwd(dpre2, (dg, db), weights[l], saved[l], on_part)
        if l > 0:
            dpre2, dg, db = _ln_bwd(saved[l - 1]["pre2"], weights[l - 1]["ln2g"], weights[l - 1]["ln2b"], dx)
    return lacc[0, 0], dx, grads


_QKVZ = 4 * D_MODEL
_BA = 2 * N_HEADS


WEIGHT_NAMES = ("w_in", "conv_w", "a_log", "dt_bias", "o_norm_w", "sgu_ln_g", "sgu_ln_b", "w_s", "b_s", "w_pa", "w_pb",
                "w_o", "ln1_g", "ln1_b", "w_ffn_gate", "w_ffn_up", "w_ffn_down", "ln2_g", "ln2_b")
WIRE = ("w_in", "w_ffn_gate", "w_ffn_up", "w_ffn_down", "w_pa", "w_pb", "w_o", "conv_w")
SMALL = (("a_log", N_HEADS), ("dt_bias", N_HEADS), ("o_norm_w", D_HEAD), ("sgu_ln_g", D_MODEL), ("sgu_ln_b", D_MODEL),
         ("w_s", SGU_GROUPS * SGU_BLOCK * SGU_BLOCK), ("b_s", SGU_GROUPS * SGU_BLOCK),
         ("ln1_g", D_MODEL), ("ln1_b", D_MODEL), ("ln2_g", D_MODEL), ("ln2_b", D_MODEL))
SMALL_ROWS = -(-sum(n for _, n in SMALL) // (LANES * SUBLANES)) * SUBLANES
N_MAIN_TILES = (N_IN - _BA) // D_MODEL
ADAM_TILES = dict(w_in=(128, "adamw_in"), w_ffn_gate=(32, "adamw_ffn_rows"), w_ffn_up=(32, "adamw_ffn_rows"),
                  w_ffn_down=(32, "adamw_ffn_rows"), w_pa=(128, "adamw_sq"), w_pb=(128, "adamw_sq"), w_o=(128, "adamw_sq"),
                  conv_w=(CONV_K, "adamw_conv"))


def _pad_to(a, axis, size):
    pads = [(0, 0)] * a.ndim
    pads[axis] = (0, size - a.shape[axis])
    return jnp.pad(a, pads)


def _t(a):
    return jnp.swapaxes(a, 1, 2)


def _wire_blocks(p):
    return dict(
        w_in=_pad_to(p["w_in"].astype(BF), 2, IN_PAD),
        w_ffn_gate=_pad_to(_t(p["w_ffn_gate"]).astype(BF), 1, FFN_PAD), w_ffn_up=_pad_to(_t(p["w_ffn_up"]).astype(BF), 1, FFN_PAD),
        w_ffn_down=_pad_to(p["w_ffn_down"].astype(BF), 1, FFN_PAD),
        w_pa=p["w_pa"].astype(BF), w_pb=p["w_pb"].astype(BF), w_o=p["w_o"].astype(BF),
        conv_w=_pad_to(p["conv_w"], 1, SUBLANES),
    )


def _by_columns(blocks):
    n, r, c = blocks.shape
    return jnp.transpose(blocks, (1, 0, 2)).reshape(r, n * c)


def _to_slots(full, c):
    r = full.shape[0]
    return jnp.transpose(full.reshape(r, N_DEV, c), (1, 0, 2))


def _lane_row(v, at):
    return jnp.pad(v[None], ((0, 0), (at, LANES - at - v.shape[0])))


TRANSPOSED = ("w_ffn_gate", "w_ffn_up")
EARLY = ("w_in", "conv_w")
LATE = ("w_pa", "w_pb", "w_o", "w_ffn_gate", "w_ffn_up", "w_ffn_down")


def _early_weights(stacks, p, l):
    return dict(
        win=_perm_in(stacks["w_in"], D_MODEL, N_MAIN_TILES), wba=_perm_in(stacks["w_in"], LANES, 1),
        convw=_by_columns(stacks["conv_w"][:, :CONV_K]),
        arow=_lane_row(p["a_log"][l], N_HEADS), dtrow=_lane_row(p["dt_bias"][l], N_HEADS),
        onw=p["o_norm_w"][l][None], sg=p["sgu_ln_g"][l][None], sb=p["sgu_ln_b"][l][None],
        ws=p["w_s"][l], bst=_pad_to(p["b_s"][l].T, 1, LANES),
        ln1g=p["ln1_g"][l][None], ln1b=p["ln1_b"][l][None], ln2g=p["ln2_g"][l][None], ln2b=p["ln2_b"][l][None],
    )


def _late_weights(stacks):
    return dict(
        wpa=stacks["w_pa"].reshape(D_MODEL, D_MODEL), wpb=stacks["w_pb"].reshape(D_MODEL, D_MODEL),
        wo=stacks["w_o"].reshape(D_MODEL, D_MODEL),
        wgt=stacks["w_ffn_gate"].reshape(FFN_K, D_MODEL), wut=stacks["w_ffn_up"].reshape(FFN_K, D_MODEL),
        wd=stacks["w_ffn_down"].reshape(FFN_K, D_MODEL),
    )


def _small_pack(parts):
    flat = jnp.concatenate([parts[n].reshape(-1) for n, _ in SMALL])
    return _pad_to(flat, 0, SMALL_ROWS * LANES).reshape(SMALL_ROWS, LANES)


def _small_unpack(rows, like):
    flat, out, off = rows.reshape(-1), {}, 0
    for n, size in SMALL:
        out[n] = flat[off:off + size].reshape(like[n].shape[1:])
        off += size
    return out


def _late_slots(g):
    slots = dict(
        w_ffn_gate=g["wgt"].reshape(N_DEV, FFN_PAD, D_MODEL), w_ffn_up=g["wut"].reshape(N_DEV, FFN_PAD, D_MODEL),
        w_ffn_down=g["wd"].reshape(N_DEV, FFN_PAD, D_MODEL),
        w_pa=g["wpa"].reshape(N_DEV, D_MODEL // N_DEV, D_MODEL), w_pb=g["wpb"].reshape(N_DEV, D_MODEL // N_DEV, D_MODEL),
        w_o=g["wo"].reshape(N_DEV, D_MODEL // N_DEV, D_MODEL),
    )
    return [slots[n] for n in LATE]


def _early_slots(g):
    slots = [_perm_out(g["win"], g["wba"]), _pad_to(_to_slots(g["convw"][:CONV_K], 3 * D_MODEL // N_DEV), 1, SUBLANES)]
    small = _small_pack(dict(
        a_log=g["arow"][0, N_HEADS:2 * N_HEADS], dt_bias=g["dtrow"][0, N_HEADS:2 * N_HEADS], o_norm_w=g["onw"][0],
        sgu_ln_g=g["sg"][0], sgu_ln_b=g["sb"][0], w_s=g["ws"], b_s=g["bst"][:, :SGU_GROUPS].T,
        ln1_g=g["ln1g"][0], ln1_b=g["ln1b"][0], ln2_g=g["ln2g"][0], ln2_b=g["ln2b"][0]))
    return slots, small


def _in_tile_start(j, tile_w):
    if tile_w == LANES:
        return jnp.int32(_QKVZ)
    return j * D_MODEL + jnp.where(j >= _QKVZ // D_MODEL, _BA, 0)


def _select(rows_iota, cols_iota, dev, start, valid):
    hit = (rows_iota + (dev * IN_BLOCK - start) == cols_iota) & (rows_iota < IN_BLOCK) & (cols_iota < valid)
    return jnp.where(hit, 1.0, 0.0).astype(BF)


def _perm_in(stack, tile_w, n_tiles):
    valid = _BA if tile_w == LANES else tile_w

    def first_dev(j):
        return lax.div(_in_tile_start(j, tile_w), jnp.int32(IN_BLOCK))

    def body(w_ref, o_ref, acc_ref):
        j, k = pl.program_id(0), pl.program_id(1)
        sel = _select(_iota((IN_PAD, tile_w), 0), _iota((IN_PAD, tile_w), 1), first_dev(j) + k,
                      _in_tile_start(j, tile_w), valid)
        part = jnp.dot(w_ref[0], sel, preferred_element_type=F32)

        @pl.when(k == 0)
        def _():
            acc_ref[...] = part

        @pl.when(k == 1)
        def _():
            o_ref[...] = (acc_ref[...] + part).astype(BF)

    est = _nbytes((D_MODEL, IN_PAD), BF) + 3 * _nbytes((D_MODEL, tile_w), F32) + 2 * _nbytes((IN_PAD, tile_w), F32)
    return pl.pallas_call(
        body, name="perm_in" if tile_w != LANES else "perm_in_ba", grid=(n_tiles, 2),
        in_specs=[pl.BlockSpec((1, D_MODEL, IN_PAD), lambda j, k: (jnp.minimum(first_dev(j) + k, N_DEV - 1), 0, 0))],
        out_specs=pl.BlockSpec((D_MODEL, tile_w), lambda j, k: (0, j)),
        out_shape=jax.ShapeDtypeStruct((D_MODEL, n_tiles * tile_w), BF),
        scratch_shapes=[pltpu.VMEM((D_MODEL, tile_w), F32)],
        compiler_params=_cparams(est, ("parallel", "arbitrary")),
    )(stack)


def _perm_out(dmain, dba):
    def tile(d, s):
        c0 = d * IN_BLOCK
        first = lax.div(c0 - jnp.where(c0 < _QKVZ, 0, jnp.minimum(c0 - _QKVZ, _BA)), jnp.int32(D_MODEL))
        return jnp.minimum(first + jnp.minimum(s, 1), N_MAIN_TILES - 1)

    def body(dm_ref, db_ref, o_ref, acc_ref):
        d, s = pl.program_id(0), pl.program_id(1)

        @pl.when(s == 0)
        def _():
            acc_ref[...] = jnp.zeros_like(acc_ref)

        start = _in_tile_start(tile(d, s), D_MODEL)
        overlaps = (start < (d + 1) * IN_BLOCK) & (d * IN_BLOCK < start + D_MODEL)

        @pl.when((s < 2) & overlaps)
        def _():
            sel = _select(_iota((D_MODEL, IN_PAD), 1), _iota((D_MODEL, IN_PAD), 0), d, start, D_MODEL)
            acc_ref[...] += jnp.dot(dm_ref[...], sel, preferred_element_type=F32)

        @pl.when(s == 2)
        def _():
            sel = _select(_iota((LANES, IN_PAD), 1), _iota((LANES, IN_PAD), 0), d, jnp.int32(_QKVZ), _BA)
            o_ref[0] = (acc_ref[...] + jnp.dot(db_ref[...], sel, preferred_element_type=F32)).astype(BF)

    est = 2 * _nbytes((D_MODEL, D_MODEL), BF) + 4 * _nbytes((D_MODEL, IN_PAD), F32)
    return pl.pallas_call(
        body, name="perm_out", grid=(N_DEV, 3),
        in_specs=[pl.BlockSpec((D_MODEL, D_MODEL), lambda d, s: (0, tile(d, s))),
                  pl.BlockSpec((D_MODEL, LANES), lambda d, s: (0, 0))],
        out_specs=pl.BlockSpec((1, D_MODEL, IN_PAD), lambda d, t: (d, 0, 0)),
        out_shape=jax.ShapeDtypeStruct((N_DEV, D_MODEL, IN_PAD), BF),
        scratch_shapes=[pltpu.VMEM((D_MODEL, IN_PAD), F32)],
        compiler_params=_cparams(est, ("parallel", "arbitrary")),
    )(dmain, dba)


def _mesh_place():
    x, y, c = (lax.axis_index(a) for a in MESH_AXES)
    return x, y, c


def _slot(x, y, c):
    return 4 * x + 2 * y + c


def _peer(place, j):
    x, y, c = place
    return (1 - x if j & 4 else x, 1 - y if j & 2 else y, 1 - c if j & 1 else c)


_HBM = pl.BlockSpec(memory_space=pltpu.HBM)
_SEM = pl.BlockSpec(memory_space=pltpu.SEMAPHORE)
_EFFECT = pltpu.SideEffectType.DATAFLOW_SIDE_EFFECTING


def _remote_copy(src_ref, land_ref, slot, per_slot, pslot, sems, u, j, peer):
    return pltpu.make_async_remote_copy(
        src_ref=src_ref.at[pslot] if per_slot else src_ref, dst_ref=land_ref.at[slot],
        send_sem=sems[0].at[u * (N_DEV - 1) + j - 1], recv_sem=sems[1].at[u * (N_DEV - 1) + j - 1],
        device_id=peer, device_id_type=pl.DeviceIdType.MESH)


def _own_copy(src_ref, land_ref, me, per_slot, sems, u):
    return pltpu.make_async_copy(src_ref.at[me] if per_slot else src_ref, land_ref.at[me], sems[2].at[u])


def _exchange_start(name, srcs, per_slot):
    n = len(srcs)
    lands = [jax.ShapeDtypeStruct(s.shape if p else (N_DEV,) + s.shape, s.dtype) for s, p in zip(srcs, per_slot)]

    def body(*refs):
        src_refs, sems, land_refs, token = refs[:n], refs[n:n + 3], refs[2 * n + 3:3 * n + 3], refs[-1]
        place = _mesh_place()
        me = _slot(*place)
        for u in range(n):
            _own_copy(src_refs[u], land_refs[u], me, per_slot[u], sems, u).start()
            for j in range(1, N_DEV):
                peer = _peer(place, j)
                _remote_copy(src_refs[u], land_refs[u], me, per_slot[u], _slot(*peer), sems, u, j, peer).start()
        token[...] = jnp.zeros_like(token)

    hbm = lambda a: pltpu.HBM(a.shape, a.dtype)
    sem = pltpu.SemaphoreType.DMA((n * (N_DEV - 1),))
    outs = pl.pallas_call(
        body, name=name,
        out_shape=(sem, sem, pltpu.SemaphoreType.DMA((n,)), *[hbm(a) for a in srcs], *[hbm(a) for a in lands],
                   jax.ShapeDtypeStruct((SUBLANES, LANES), F32)),
        in_specs=[_HBM] * n, out_specs=(_SEM, _SEM, _SEM, *[_HBM] * (2 * n), pl.BlockSpec(memory_space=pltpu.VMEM)),
        input_output_aliases={i: 3 + i for i in range(n)},
        compiler_params=pltpu.CompilerParams(has_side_effects=_EFFECT),
    )(*[pltpu.with_memory_space_constraint(a, pltpu.HBM) for a in srcs])
    return tuple(outs[:3]), list(outs[3:3 + n]), list(outs[3 + n:3 + 2 * n]), outs[-1]


def _exchange_wait(name, sems, srcs, lands, units, per_slot, after):
    m = len(units)

    def body(*refs):
        src_refs, land_refs, sem_refs = refs[:m], refs[m:2 * m], refs[2 * m:2 * m + 3]
        place = _mesh_place()
        me = _slot(*place)
        for i, u in enumerate(units):
            _own_copy(src_refs[i], land_refs[i], me, per_slot[u], sem_refs, u).wait()
            for j in range(1, N_DEV):
                peer = _peer(place, j)
                pslot = _slot(*peer)
                cp = _remote_copy(src_refs[i], land_refs[i], pslot, per_slot[u], pslot, sem_refs, u, j, peer)
                cp.wait_send()
                cp.wait_recv()

    hbm = lambda a: pltpu.HBM(a.shape, a.dtype)
    outs = pl.pallas_call(
        body, name=name, out_shape=tuple(hbm(a) for a in list(srcs) + list(lands)),
        in_specs=[_HBM] * (2 * m) + [_SEM] * 3 + [pl.BlockSpec(memory_space=pl.ANY)], out_specs=tuple([_HBM] * (2 * m)),
        input_output_aliases={i: i for i in range(2 * m)},
        compiler_params=pltpu.CompilerParams(has_side_effects=_EFFECT),
    )(*srcs, *lands, *sems, after)
    return list(outs[m:])


def _adam_update(g, w, m, v):
    m = ADAM_B1 * m + (1.0 - ADAM_B1) * g
    v = ADAM_B2 * v + (1.0 - ADAM_B2) * jnp.square(g)
    m_hat = m / (1.0 - ADAM_B1 ** ADAM_STEP)
    v_hat = v / (1.0 - ADAM_B2 ** ADAM_STEP)
    return -ADAM_LR * (m_hat / (jnp.sqrt(v_hat) + ADAM_EPS) + ADAM_WD * w), m, v


def _adamw(recvs, w, m, v, *, tr, name):
    L, R, C = w.shape
    rp = max(tr, SUBLANES * (4 // jnp.dtype(recvs[0].dtype).itemsize))
    Cp = recvs[0].shape[2]

    def body(*refs):
        r_refs, (w_ref, m_ref, v_ref, g_ref, d_ref, nm_ref, nv_ref) = refs[:L], refs[L:]
        for l in range(L):
            @pl.when(pl.program_id(0) == l)
            def _(r_ref=r_refs[l]):
                g = r_ref[0, :tr, :C].astype(F32)
                for s in range(1, N_DEV):
                    g = g + r_ref[s, :tr, :C].astype(F32)
                d, nm, nv = _adam_update(g, w_ref[0], m_ref[0], v_ref[0])
                g_ref[0], d_ref[0], nm_ref[0], nv_ref[0] = g, d, nm, nv

    blk = pl.BlockSpec((1, tr, C), lambda l, i: (l, i, 0))
    r_specs = [pl.BlockSpec((N_DEV, rp, Cp), lambda l, i, k=k: (0, jnp.where(l == k, i, 0), 0)) for k in range(L)]
    est = 2 * _nbytes((N_DEV, rp, Cp), recvs[0].dtype) + 8 * _nbytes((tr, Cp), F32)
    return pl.pallas_call(
        body, name=name, grid=(L, R // tr),
        in_specs=r_specs + [blk] * 3, out_specs=[blk] * 4,
        out_shape=[jax.ShapeDtypeStruct((L, R, C), F32)] * 4,
        compiler_params=_cparams(est, ("arbitrary", "arbitrary")),
    )(*recvs, w, m, v)


def _adamw_small(recv, w, m, v):
    def body(r_ref, w_ref, m_ref, v_ref, g_ref, d_ref, nm_ref, nv_ref):
        g = r_ref[0]
        for s in range(1, N_DEV):
            g = g + r_ref[s]
        g_ref[...] = g
        d_ref[...], nm_ref[...], nv_ref[...] = _adam_update(g, w_ref[...], m_ref[...], v_ref[...])

    vm = pl.BlockSpec(memory_space=pltpu.VMEM)
    return pl.pallas_call(
        body, name="adamw_small", in_specs=[vm] * 4, out_specs=[vm] * 4,
        out_shape=[jax.ShapeDtypeStruct((SMALL_ROWS, LANES), F32)] * 4,
        compiler_params=_cparams(20 * _nbytes((SMALL_ROWS, LANES), F32)),
    )(recv, w, m, v)


def kernel(x, w_in, conv_w, a_log, dt_bias, o_norm_w, sgu_ln_g, sgu_ln_b, w_s, b_s, w_pa, w_pb, w_o, ln1_g, ln1_b, w_ffn_gate, w_ffn_up, w_ffn_down, ln2_g, ln2_b, loss_target, m_w_in, m_conv_w, m_a_log, m_dt_bias, m_o_norm_w, m_sgu_ln_g, m_sgu_ln_b, m_w_s, m_b_s, m_w_pa, m_w_pb, m_w_o, m_ln1_g, m_ln1_b, m_w_ffn_gate, m_w_ffn_up, m_w_ffn_down, m_ln2_g, m_ln2_b, v_w_in, v_conv_w, v_a_log, v_dt_bias, v_o_norm_w, v_sgu_ln_g, v_sgu_ln_b, v_w_s, v_b_s, v_w_pa, v_w_pb, v_w_o, v_ln1_g, v_ln1_b, v_w_ffn_gate, v_w_ffn_up, v_w_ffn_down, v_ln2_g, v_ln2_b):
    given = dict(locals())
    P = {n: given[n] for n in WEIGHT_NAMES}
    M = {n: given["m_" + n] for n in WEIGHT_NAMES}
    V = {n: given["v_" + n] for n in WEIGHT_NAMES}

    wire = _wire_blocks(P)
    units = [(n, l) for l in range(DEPTH) for n in EARLY + LATE]
    whole = [False] * len(units)
    g_sems, g_srcs, g_lands, g_token = _exchange_start("gather_start", [wire[n][l] for n, l in units], whole)

    def gathered(name, names, l, after):
        idx = [units.index((n, l)) for n in names]
        got = _exchange_wait(name, g_sems, [g_srcs[i] for i in idx], [g_lands[i] for i in idx], idx, whole, after)
        return dict(zip(names, got))

    def layer(l):
        def weights(x_in):
            after = g_token if l == 0 else x_in
            early = _early_weights(gathered(f"gather_wait_early{l}", EARLY, l, after), P, l)
            return early, lambda ya: _late_weights(gathered(f"gather_wait_late{l}", LATE, l, ya))
        return weights

    pending = {}

    def on_grads(l, part, g):
        if part == "late":
            srcs, names = _late_slots(g), LATE
            per_slot = [True] * len(srcs)
        else:
            slots, small = _early_slots(g)
            srcs, names = slots + [small], EARLY + ("small",)
            per_slot = [True] * len(slots) + [False]
        sems, s_thru, l_thru, token = _exchange_start(f"exchange_start_{part}{l}", srcs, per_slot)
        pending[l, part] = (names, sems, s_thru, l_thru, per_slot)
        return token[0, 0]

    loss_local, dx, _ = _local_step(x[0], loss_target[0], [layer(l) for l in range(DEPTH)], on_grads)
    loss = lax.psum(loss_local, MESH_AXES)

    recv = [{} for _ in range(DEPTH)]

    def received(l, part, after):
        names, sems, s_thru, l_thru, per_slot = pending[l, part]
        got = _exchange_wait(f"exchange_wait_{part}{l}", sems, s_thru, l_thru, list(range(len(s_thru))), per_slot, after)
        recv[l].update(zip(names, got))

    out = {}

    def adamw(names):
        for n in names:
            tr, name = ADAM_TILES[n]
            view = _t if n in TRANSPOSED else (lambda a: a)
            res = _adamw([recv[l][n] for l in range(DEPTH)], view(P[n]), view(M[n]), view(V[n]), tr=tr, name=name)
            out[n] = [view(r) for r in res]

    for l in reversed(range(DEPTH)):
        received(l, "late", dx)
    adamw(LATE)
    for l in reversed(range(DEPTH)):
        received(l, "early", out[LATE[-1]][0])
    adamw(EARLY)
    small = [_adamw_small(recv[l]["small"], *[_small_pack({n: T[n][l] for n, _ in SMALL}) for T in (P, M, V)])
             for l in range(DEPTH)]
    for n, _ in SMALL:
        out[n] = [jnp.stack([_small_unpack(small[l][i], P)[n] for l in range(DEPTH)]) for i in range(4)]
    return (loss, dx[None], *[out[n][i] for i in range(4) for n in WEIGHT_NAMES])
```

```python
import functools
import math

import jax
import jax.numpy as jnp
from jax import lax
from jax.experimental import pallas as pl
from jax.experimental.pallas import tpu as pltpu

F32 = jnp.float32
BF = jnp.bfloat16
HIGHEST = lax.Precision.HIGHEST

D_MODEL = 1024
DEPTH = 2
N_HEADS = 8
D_HEAD = 128
CONV_K = 4
SGU_BLOCK = 128
SGU_GROUPS = 8
SGU_CHUNK = 64
FFN_HIDDEN = 2816
N_IN = 8208
N_DEV = 8
IN_BLOCK, IN_PAD = N_IN // N_DEV, 1152
FFN_BLOCK, FFN_PAD = FFN_HIDDEN // N_DEV, 384
FFN_K = N_DEV * FFN_PAD
ALPHA = (2 * DEPTH) ** 0.25
LN_EPS = 1e-5
RMS_EPS = 1e-6
ADAM_LR, ADAM_B1, ADAM_B2, ADAM_EPS, ADAM_WD, ADAM_STEP = 0.001, 0.9, 0.999, 1e-08, 0.01, 10

MESH_AXES = ("x", "y", "c")
DELTA_CHUNK = 128
DELTA_HEADS_PER_STEP = 8
LANES = 128
SUBLANES = 8
VMEM_BYTES = 64 * 1024 * 1024
HALO = SUBLANES
HALO_BF = 2 * SUBLANES


def _cparams(est_bytes, dims=None):
    limit = int(min(max(2 * est_bytes + (8 << 20), 32 << 20), VMEM_BYTES - (6 << 20)))
    kw = dict(vmem_limit_bytes=limit)
    if dims is not None:
        kw["dimension_semantics"] = dims
    return pltpu.CompilerParams(**kw)


def _nbytes(shape, dtype):
    return math.prod(shape) * jnp.dtype(dtype).itemsize


def _dims(kind, ndim):
    lhs, rhs = {"nn": (1, 0), "nt": (1, 1), "tn": (0, 0)}[kind]
    b = ndim - 2
    return (((lhs + b,), (rhs + b,)), (tuple(range(b)), tuple(range(b))))


def _mxu(a, b, kind):
    return lax.dot_general(a, b, _dims(kind, a.ndim), preferred_element_type=F32)


def _dot(a, b):
    return _mxu(a.astype(BF), b.astype(BF), "nn")


def _dot_nt(a, b):
    return _mxu(a.astype(BF), b.astype(BF), "nt")


def _dot_tn(a, b):
    return _mxu(a.astype(BF), b.astype(BF), "tn")


def _split(a):
    hi = a.astype(BF)
    return hi, (a - hi.astype(F32)).astype(BF)


def _dot3(a, b, kind):
    (ah, al), (bh, bl) = _split(a), _split(b)
    return _mxu(ah, bh, kind) + (_mxu(ah, bl, kind) + _mxu(al, bh, kind))


def _dotf(a, b):
    return _dot3(a, b, "nn")


def _dotf_nt(a, b):
    return _dot3(a, b, "nt")


def _dot01(sel, x, kind="nn"):
    s = jnp.broadcast_to(sel.astype(BF), x.shape[:-2] + sel.shape)
    h1 = x.astype(BF)
    r1 = x - h1.astype(F32)
    h2 = r1.astype(BF)
    h3 = (r1 - h2.astype(F32)).astype(BF)
    return _mxu(s, h1, kind) + (_mxu(s, h2, kind) + _mxu(s, h3, kind))


def _sigmoid(x):
    return 0.5 * jnp.tanh(0.5 * x) + 0.5


def _silu(x):
    return x * _sigmoid(x)


def _silu_and_grad(x):
    s = _sigmoid(x)
    return x * s, s * (1.0 + x * (1.0 - s))


def _softplus(x):
    return jnp.maximum(x, 0.0) + jnp.log1p(jnp.exp(-jnp.abs(x)))


def _ln(x, g, b):
    mu = jnp.mean(x, -1, keepdims=True)
    xc = x - mu
    var = jnp.mean(xc * xc, -1, keepdims=True)
    return xc * lax.rsqrt(var + LN_EPS) * g + b


def _iota(shape, dim):
    return lax.broadcasted_iota(jnp.int32, shape, dim)


def _tile(n, pref, align):
    if n <= pref:
        return n
    t = (pref // align) * align
    while t >= align:
        if n % t == 0:
            return t
        t -= align
    raise ValueError(f"no tile for {n} (pref {pref}, align {align})")


def _bcast_rows(v, rows=SUBLANES):
    return jnp.broadcast_to(v, (rows, v.shape[-1]))


def _mm(a, b, *, mode, name, out_dtype=F32, add=None, add_scale=1.0, tm=512, tn=1024, tk=1024, cols=None, pair=None):
    if mode == "nn":
        (M, K), N = a.shape, b.shape[1]
    elif mode == "nt":
        (M, K), N = a.shape, b.shape[0]
    else:
        (K, M), N = a.shape, b.shape[1]
    col0 = 0
    if cols is not None:
        col0, N = cols
    tm = _tile(M, tm, LANES if mode == "tn" else SUBLANES * 2)
    tn = _tile(N, tn, LANES)
    tk = _tile(K, tk, LANES)
    nk = K // tk
    j0 = col0 // tn
    if mode == "nn":
        a_spec = pl.BlockSpec((tm, tk), lambda i, j, k: (i, k))
        b_spec = pl.BlockSpec((tk, tn), lambda i, j, k: (k, j + j0))
        dot = _dot
    elif mode == "nt":
        a_spec = pl.BlockSpec((tm, tk), lambda i, j, k: (i, k))
        b_spec = pl.BlockSpec((tn, tk), lambda i, j, k: (j, k))
        dot = _dot_nt
    else:
        a_spec = pl.BlockSpec((tk, tm), lambda i, j, k: (k, i))
        b_spec = pl.BlockSpec((tk, tn), lambda i, j, k: (k, j))
        dot = _dot_tn
    o_spec = pl.BlockSpec((tm, tn), lambda i, j, k: (i, j))
    has_add = add is not None

    n_ab = 2 if pair is None else 4

    def body(*refs):
        ab, (o_ref, acc_ref) = refs[:n_ab], refs[-2:]
        add_ref = refs[n_ab] if has_add else None
        k = pl.program_id(2)
        part = dot(ab[0][...], ab[1][...])
        if pair is not None:
            part = part + dot(ab[2][...], ab[3][...])

        def finish(total):
            if has_add:
                total = total + add_scale * add_ref[...]
            o_ref[...] = total.astype(out_dtype)

        if nk == 1:
            finish(part)
        else:
            @pl.when(k == 0)
            def _():
                acc_ref[...] = part

            @pl.when(jnp.logical_and(k > 0, k < nk - 1))
            def _():
                acc_ref[...] += part

            @pl.when(k == nk - 1)
            def _():
                finish(acc_ref[...] + part)

    in_specs = [a_spec, b_spec] * (n_ab // 2) + ([o_spec] if has_add else [])
    args = (a, b) + (tuple(pair) if pair is not None else ()) + ((add,) if has_add else ())
    est = ((n_ab // 2) * (_nbytes((tm, tk), a.dtype) + _nbytes((tk, tn), b.dtype)) + 2 * _nbytes((tm, tn), F32)
           + (_nbytes((tm, tn), F32) if has_add else 0)) + 2 * _nbytes((tm, tn), F32)
    return pl.pallas_call(
        body, name=name,
        grid=(M // tm, N // tn, nk),
        in_specs=in_specs, out_specs=o_spec,
        out_shape=jax.ShapeDtypeStruct((M, N), out_dtype),
        scratch_shapes=[pltpu.VMEM((tm, tn) if nk > 1 else (SUBLANES, LANES), F32)],
        compiler_params=_cparams(est, ("parallel", "parallel", "arbitrary")),
    )(*args)


def _shifted(xt, halo, first):
    halo = jnp.where(first, 0.0, halo)
    xc = jnp.concatenate([halo, xt], axis=0)
    return [xt] + [pltpu.roll(xc, s, 0)[HALO:] for s in range(1, CONV_K)]


def _conv_taps(shifted, w_ref):
    out = shifted[0] * w_ref[CONV_K - 1:CONV_K, :]
    for s in range(1, CONV_K):
        out = out + shifted[s] * w_ref[CONV_K - 1 - s:CONV_K - s, :]
    return out


def _gates(ba, arow, dtrow):
    lane = _iota(ba.shape, 1)
    beta = _sigmoid(ba)
    g = -jnp.exp(arow) * _softplus(ba + dtrow)
    return jnp.where(lane < N_HEADS, beta, jnp.where(lane < 2 * N_HEADS, g, 0.0))


def _l2n(x):
    return x * lax.rsqrt(jnp.sum(x * x, -1, keepdims=True) + RMS_EPS)


def _qkv_prep(proj, ba, convw, arow, dtrow, *, tm=256):
    S = proj.shape[0]
    tm = _tile(S, tm, SUBLANES)
    W3 = 3 * D_MODEL
    hb = tm // HALO

    def body(xt_ref, halo_ref, ba_ref, w_ref, a_ref, dt_ref, q_ref, k_ref, v_ref, gb_ref, c_ref):
        c = _conv_taps(_shifted(xt_ref[...], halo_ref[...], pl.program_id(0) == 0), w_ref)
        c_ref[...] = c
        c = _silu(c)
        for h in range(N_HEADS):
            lo = h * D_HEAD
            q_ref[:, lo:lo + D_HEAD] = _l2n(c[:, lo:lo + D_HEAD])
            k_ref[:, lo:lo + D_HEAD] = _l2n(c[:, D_MODEL + lo:D_MODEL + lo + D_HEAD])
        v_ref[...] = c[:, 2 * D_MODEL:]
        gb_ref[...] = _gates(ba_ref[...], a_ref[...], dt_ref[...])

    row = lambda w, col=0: pl.BlockSpec((tm, w), lambda i: (i, col))
    full = lambda shape: pl.BlockSpec(shape, lambda i: (0,) * len(shape))
    est = 4 * _nbytes((tm, W3), F32)
    return pl.pallas_call(
        body, name="qkv_prep", grid=(S // tm,),
        in_specs=[row(W3), pl.BlockSpec((HALO, W3), lambda i: (jnp.maximum(i * hb - 1, 0), 0)), row(LANES),
                  full((CONV_K, W3)), full((1, LANES)), full((1, LANES))],
        out_specs=[row(D_MODEL), row(D_MODEL), row(D_MODEL), row(LANES), row(W3)],
        out_shape=[jax.ShapeDtypeStruct((S, D_MODEL), F32)] * 3 + [jax.ShapeDtypeStruct((S, LANES), F32),
                                                                   jax.ShapeDtypeStruct((S, W3), F32)],
        compiler_params=_cparams(est, ("arbitrary",)),
    )(proj, proj, ba, convw, arow, dtrow)


def _qkv_prep_bwd(proj, conv_out, ba, arow, dtrow, dq, dk, dv, dgb, *, tm=256):
    S = proj.shape[0]
    tm = _tile(S, tm, SUBLANES * 2)
    W3 = 3 * D_MODEL
    hb = tm // HALO

    def body(xt_ref, halo_ref, c_ref, ba_ref, a_ref, dt_ref, dq_ref, dk_ref, dv_ref, dgb_ref,
             dcb_ref, dba_ref, dw_ref, da_ref, ddt_ref, dc_ref):
        i = pl.program_id(0)

        @pl.when(i == 0)
        def _():
            dw_ref[...] = jnp.zeros_like(dw_ref)
            da_ref[...] = jnp.zeros_like(da_ref)
            ddt_ref[...] = jnp.zeros_like(ddt_ref)

        shifted = _shifted(xt_ref[...], halo_ref[...], i == 0)
        a, ds = _silu_and_grad(c_ref[...])
        for h in range(N_HEADS):
            for base, d_ref in ((0, dq_ref), (D_MODEL, dk_ref)):
                lo = base + h * D_HEAD
                _, vj = jax.vjp(_l2n, a[:, lo:lo + D_HEAD])
                (dx,) = vj(d_ref[:, h * D_HEAD:(h + 1) * D_HEAD])
                dc_ref[:, lo:lo + D_HEAD] = dx * ds[:, lo:lo + D_HEAD]
        dc_ref[:, 2 * D_MODEL:] = dv_ref[...] * ds[:, 2 * D_MODEL:]
        dc = dc_ref[...]
        dcb_ref[...] = dc.astype(BF)
        for s in range(CONV_K):
            kk = CONV_K - 1 - s
            dw_ref[kk:kk + 1, :] += jnp.sum(dc * shifted[s], axis=0, keepdims=True)
        _, vj = jax.vjp(_gates, ba_ref[...], a_ref[...], dt_ref[...])
        dba, da, ddt = vj(dgb_ref[...])
        dba_ref[...] = dba.astype(BF)
        da_ref[...] += _bcast_rows(da)
        ddt_ref[...] += _bcast_rows(ddt)

    row = lambda w, col=0: pl.BlockSpec((tm, w), lambda i: (i, col))
    full = lambda shape: pl.BlockSpec(shape, lambda i: (0,) * len(shape))
    est = 8 * _nbytes((tm, W3), F32)
    return pl.pallas_call(
        body, name="qkv_prep_bwd", grid=(S // tm,),
        in_specs=[row(W3), pl.BlockSpec((HALO, W3), lambda i: (jnp.maximum(i * hb - 1, 0), 0)), row(W3), row(LANES),
                  full((1, LANES)), full((1, LANES)),
                  row(D_MODEL), row(D_MODEL), row(D_MODEL), row(LANES)],
        out_specs=[row(W3), row(LANES), full((SUBLANES, W3)), full((SUBLANES, LANES)), full((SUBLANES, LANES))],
        out_shape=[jax.ShapeDtypeStruct((S, W3), BF), jax.ShapeDtypeStruct((S, LANES), BF),
                   jax.ShapeDtypeStruct((SUBLANES, W3), F32), jax.ShapeDtypeStruct((SUBLANES, LANES), F32),
                   jax.ShapeDtypeStruct((SUBLANES, LANES), F32)],
        scratch_shapes=[pltpu.VMEM((tm, W3), F32)],
        compiler_params=_cparams(est, ("arbitrary",)),
    )(proj, proj, conv_out, ba, arow, dtrow, dq, dk, dv, dgb)


def _conv_bwd(dc, convw, dproj, *, tm=256):
    S, W3 = dc.shape
    tm = _tile(S, tm, HALO_BF)
    hb = tm // HALO_BF
    nt = S // tm

    def body(dc_ref, nxt_ref, w_ref, dproj_ref, o_ref):
        last = pl.program_id(0) == nt - 1
        nxt = jnp.where(last, 0.0, nxt_ref[...].astype(F32))
        cur = dc_ref[...].astype(F32)
        xc = jnp.concatenate([cur, nxt], axis=0)
        out = cur * w_ref[CONV_K - 1:CONV_K, :]
        for s in range(1, CONV_K):
            out = out + pltpu.roll(xc, tm + HALO_BF - s, 0)[:tm] * w_ref[CONV_K - 1 - s:CONV_K - s, :]
        o_ref[...] = out.astype(BF)

    est = 5 * _nbytes((tm, W3), F32)
    return pl.pallas_call(
        body, name="conv_bwd", grid=(nt,),
        in_specs=[pl.BlockSpec((tm, W3), lambda i: (i, 0)),
                  pl.BlockSpec((HALO_BF, W3), lambda i: (jnp.minimum((i + 1) * hb, S // HALO_BF - 1), 0)),
                  pl.BlockSpec((CONV_K, W3), lambda i: (0, 0)), pl.BlockSpec(memory_space=pl.ANY)],
        out_specs=pl.BlockSpec((tm, W3), lambda i: (i, 0)),
        out_shape=jax.ShapeDtypeStruct(dproj.shape, BF),
        input_output_aliases={3: 0},
        compiler_params=_cparams(est, ("parallel",)),
    )(dc, dc, convw, dproj)


NEUMANN_BLOCK = 8


def _inv_unit_lower(A):
    C = A.shape[-1]
    row, col = _iota((C, C), 0), _iota((C, C), 1)
    eye = jnp.where(row == col, 1.0, 0.0).astype(F32)
    Ab = A.astype(BF)
    sh = jnp.int32(int(math.log2(NEUMANN_BLOCK)))
    B = jnp.where(lax.shift_right_logical(row, sh) == lax.shift_right_logical(col, sh), Ab, jnp.zeros_like(Ab))
    B2 = _mxu(B, B, "nn")
    B4 = _dot3(B2, B2, "nn")
    b2h, b2l = _split(B2)
    P = eye - B.astype(F32) + B2 - (_mxu(B, b2h, "nn") + _mxu(B, b2l, "nn"))
    T = P + _dot3(P, B4, "nn")
    b = NEUMANN_BLOCK
    while b < C:
        hi = ~(2 * b - 1)
        off = ((row & hi) == (col & hi)) & ((row & b) != 0) & ((col & b) == 0)
        Aoff = jnp.where(off, Ab, jnp.zeros_like(Ab))
        th, tl = _split(T)
        xh, xl = _split(_mxu(th, Aoff, "nn") + _mxu(tl, Aoff, "nn"))
        T = T - (_mxu(xh, th, "nn") + (_mxu(xh, tl, "nn") + _mxu(xl, th, "nn")))
        b *= 2
    return T


def _delta_common(q, k, g, beta):
    C = q.shape[-2]
    row, col = _iota((C, C), 0), _iota((C, C), 1)
    tril = row >= col
    qs = q * (D_HEAD ** -0.5)
    gcb = _dot01(jnp.where(tril, 1.0, 0.0), jnp.broadcast_to(g, g.shape[:-1] + (LANES,)))
    gc = gcb[..., :1]
    gr = jnp.swapaxes(gcb, -1, -2)
    Dm = jnp.exp(jnp.where(tril, gc - gr, -1e30))
    Dmt = jnp.exp(jnp.where(row <= col, gr - gc, -1e30))
    eg = jnp.exp(gc)
    gl = jnp.sum(jnp.where(_iota((C, 1), 0) == C - 1, gc, 0.0), axis=(-2, -1), keepdims=True)
    el = jnp.exp(gl)
    er = jnp.exp(gl - gc)
    kb = k * beta
    KK = _dot_nt(kb, k)
    QK = _dot_nt(qs, k)
    return dict(row=row, col=col, tril=tril, qs=qs, gc=gc, Dm=Dm, Dmt=Dmt, eg=eg, el=el, er=er, kb=kb, KK=KK, QK=QK)


def _delta_chunk_fwd(S0, q, k, v, g, beta):
    m = _delta_common(q, k, g, beta)
    T = _inv_unit_lower(jnp.where(m["row"] > m["col"], m["KK"] * m["Dm"], 0.0))
    u = _dotf(T, v * beta)
    w = _dotf(T, m["kb"] * m["eg"])
    vn = u - _dot(w, S0)
    o = _dot(m["qs"] * m["eg"], S0) + _dot(m["QK"] * m["Dm"], vn)
    S1 = S0 * m["el"] + _dot_tn(k * m["er"], vn)
    return o, S1, jnp.swapaxes(T, -1, -2), u, w


def _delta_chunk_bwd(S0, q, k, v, g, beta, Tt, u, w, do, dS1):
    m = _delta_common(q, k, g, beta)
    C = q.shape[-2]
    qs, Dm, Dmt, eg, el, er, kb, KK, QK = (m[n] for n in ("qs", "Dm", "Dmt", "eg", "el", "er", "kb", "KK", "QK"))
    strict = m["row"] > m["col"]
    total = lambda x: jnp.sum(x, axis=(-2, -1), keepdims=True)
    vn = u - _dot(w, S0)
    qg = qs * eg
    kr = k * er

    dvn = _dot(_dot_nt(k, qs) * Dmt, do) + _dot(kr, dS1)
    dS0 = dS1 * el + _dot_tn(qg, do) - _dot_tn(w, dvn)
    d_el = total(dS1 * S0)
    dqg = _dot_nt(do, S0)
    dqs = dqg * eg
    deg = jnp.sum(dqg * qs, -1, keepdims=True)
    dP = _dot_nt(do, vn)
    dPD = dP * Dm
    dqs = dqs + _dot(dPD, k)
    dk = _dot(_dot_nt(vn, do) * Dmt, qs)
    dD = dP * QK
    dkr = _dot_nt(vn, dS1)
    dk = dk + dkr * er
    der = jnp.sum(dkr * k, -1, keepdims=True)
    dw = -_dot_nt(dvn, S0)
    th, tl = _split(Tt)

    def tt_times(x):
        xh, xl = _split(x)
        return _mxu(th, xh, "nn") + (_mxu(th, xl, "nn") + _mxu(tl, xh, "nn"))

    dru = tt_times(dvn)
    drw = tt_times(dw)
    dA = -(_dotf_nt(dru, u) + _dotf_nt(drw, w))
    dAm = jnp.where(strict, dA, 0.0)
    dKK = dAm * Dm
    dkb = _dot(dKK, k)
    dk = dk + _dot_tn(dKK, kb)
    dD = dD + dAm * KK
    dv = dru * beta
    dbeta = jnp.sum(dru * v, -1, keepdims=True)
    dkb = dkb + drw * eg
    deg = deg + jnp.sum(drw * kb, -1, keepdims=True)
    dk = dk + dkb * beta
    dbeta = dbeta + jnp.sum(dkb * k, -1, keepdims=True)
    E = dD * Dm
    dgc = jnp.sum(E, -1, keepdims=True) - jnp.sum(jnp.swapaxes(E, -1, -2), -1, keepdims=True)
    dgc = dgc + deg * eg - der * er
    dgl = total(der * er) + d_el * el
    dgc = dgc + jnp.where(_iota((C, 1), 0) == C - 1, dgl, 0.0)
    triu = jnp.where(m["row"] <= m["col"], 1.0, 0.0)
    dg = _dot01(triu, jnp.broadcast_to(dgc, dgc.shape[:-1] + (LANES,)))[..., :1]
    dq = dqs * (D_HEAD ** -0.5)
    return dq, dk, dv, dg, dbeta, dS0


def _head_cols(gb, h):
    lane = _iota(gb.shape, 1)
    beta = jnp.sum(jnp.where(lane == h, gb, 0.0), -1, keepdims=True)
    g = jnp.sum(jnp.where(lane == N_HEADS + h, gb, 0.0), -1, keepdims=True)
    return g, beta


def _delta_fwd(q, k, v, gb):
    S = q.shape[0]
    C = DELTA_CHUNK
    N = S // C

    HB = DELTA_HEADS_PER_STEP

    def body(q_ref, k_ref, v_ref, gb_ref, o_ref, st_ref, t_ref, u_ref, w_ref, s_scr):
        n, hb = pl.program_id(0), pl.program_id(1)
        gb = gb_ref[...]

        @pl.when(n == 0)
        def _():
            for hh in range(HB):
                s_scr[hb * HB + hh] = jnp.zeros((D_HEAD, D_HEAD), F32)

        heads = [hb * HB + hh for hh in range(HB)]
        cols = [slice(hh * D_HEAD, (hh + 1) * D_HEAD) for hh in range(HB)]
        per_head = lambda ref: jnp.stack([ref[:, c] for c in cols])
        g, beta = (jnp.stack(t) for t in zip(*[_head_cols(gb, h) for h in heads]))
        S0 = jnp.stack([s_scr[h] for h in heads])
        o, S1, Tt, u, w = _delta_chunk_fwd(S0, per_head(q_ref), per_head(k_ref), per_head(v_ref), g, beta)
        for hh in range(HB):
            st_ref[hh, 0] = S0[hh]
            t_ref[hh, 0] = Tt[hh]
            o_ref[:, cols[hh]] = o[hh]
            u_ref[:, cols[hh]] = u[hh]
            w_ref[:, cols[hh]] = w[hh]
            s_scr[heads[hh]] = S1[hh]

    hd = pl.BlockSpec((C, HB * D_HEAD), lambda n, h: (n, h))
    mat = pl.BlockSpec((HB, 1, D_HEAD, D_HEAD), lambda n, h: (h, n, 0, 0))
    est = 40 * HB * _nbytes((C, D_HEAD), F32)
    seq = jax.ShapeDtypeStruct((S, N_HEADS * D_HEAD), F32)
    return pl.pallas_call(
        body, name="delta_fwd", grid=(N, N_HEADS // HB),
        in_specs=[hd, hd, hd, pl.BlockSpec((C, LANES), lambda n, h: (n, 0))],
        out_specs=[hd, mat, mat, hd, hd],
        out_shape=[seq, jax.ShapeDtypeStruct((N_HEADS, N, D_HEAD, D_HEAD), F32),
                   jax.ShapeDtypeStruct((N_HEADS, N, C, C), F32), seq, seq],
        scratch_shapes=[pltpu.VMEM((N_HEADS, D_HEAD, D_HEAD), F32)],
        compiler_params=_cparams(est, ("arbitrary", "arbitrary")),
    )(q, k, v, gb)


def _delta_bwd(q, k, v, gb, st, tinv, u, w, do):
    S = q.shape[0]
    C = DELTA_CHUNK
    N = S // C

    HB = DELTA_HEADS_PER_STEP

    def body(q_ref, k_ref, v_ref, gb_ref, st_ref, t_ref, u_ref, w_ref, do_ref, dq_ref, dk_ref, dv_ref, dgb_ref, ds_scr):
        n, hb = pl.program_id(0), pl.program_id(1)
        gb = gb_ref[...]
        lane = _iota((C, LANES), 1)
        dgb = jnp.zeros((C, LANES), F32)

        @pl.when(n == 0)
        def _():
            for hh in range(HB):
                ds_scr[hb * HB + hh] = jnp.zeros((D_HEAD, D_HEAD), F32)

        heads = [hb * HB + hh for hh in range(HB)]
        cols = [slice(hh * D_HEAD, (hh + 1) * D_HEAD) for hh in range(HB)]
        per_head = lambda ref: jnp.stack([ref[:, c] for c in cols])
        g, beta = (jnp.stack(t) for t in zip(*[_head_cols(gb, h) for h in heads]))
        dS1 = jnp.stack([ds_scr[h] for h in heads])
        dq, dk, dv, dg, dbeta, dS0 = _delta_chunk_bwd(
            st_ref[:, 0], per_head(q_ref), per_head(k_ref), per_head(v_ref), g, beta, t_ref[:, 0],
            per_head(u_ref), per_head(w_ref), per_head(do_ref), dS1)
        for hh, h in enumerate(heads):
            dq_ref[:, cols[hh]] = dq[hh]
            dk_ref[:, cols[hh]] = dk[hh]
            dv_ref[:, cols[hh]] = dv[hh]
            dgb = dgb + jnp.where(lane == h, dbeta[hh], 0.0) + jnp.where(lane == N_HEADS + h, dg[hh], 0.0)
            ds_scr[h] = dS0[hh]

        @pl.when(hb == 0)
        def _():
            dgb_ref[...] = dgb

        @pl.when(hb > 0)
        def _():
            dgb_ref[...] += dgb

    hd = pl.BlockSpec((C, HB * D_HEAD), lambda n, h: (N - 1 - n, h))
    mat = pl.BlockSpec((HB, 1, D_HEAD, D_HEAD), lambda n, h: (h, N - 1 - n, 0, 0))
    gbs = pl.BlockSpec((C, LANES), lambda n, h: (N - 1 - n, 0))
    est = 60 * HB * _nbytes((C, D_HEAD), F32)
    return pl.pallas_call(
        body, name="delta_bwd", grid=(N, N_HEADS // HB),
        in_specs=[hd, hd, hd, gbs, mat, mat, hd, hd, hd],
        out_specs=[hd, hd, hd, gbs],
        out_shape=[jax.ShapeDtypeStruct((S, N_HEADS * D_HEAD), F32)] * 3 + [jax.ShapeDtypeStruct((S, LANES), F32)],
        scratch_shapes=[pltpu.VMEM((N_HEADS, D_HEAD, D_HEAD), F32)],
        compiler_params=_cparams(est, ("arbitrary", "arbitrary")),
    )(q, k, v, gb, st, tinv, u, w, do)


def _ya_head(o, z, onw):
    return o * lax.rsqrt(jnp.mean(o * o, -1, keepdims=True) + RMS_EPS) * onw * _silu(z)


def _norm_cdf(x):
    return 0.5 * (1.0 + lax.erf(x * 0.7071067811865476))


def _norm_pdf(x):
    return jnp.exp(-0.5 * x * x) * 0.3989422804014327


def _chunk_causal(shape, di, dj):
    sh = jnp.int32(int(math.log2(SGU_CHUNK)))
    return lax.shift_right_logical(_iota(shape, di), sh) >= lax.shift_right_logical(_iota(shape, dj), sh)


def _ws_masked(ws):
    return jnp.where(_chunk_causal(ws.shape, 1, 2), ws, 0.0)


def _mix_prep(o, proj, onw, sg, sb, ws, bst, *, tm=256):
    S = o.shape[0]
    tm = _tile(S, tm, SGU_BLOCK)

    def body(o_ref, z_ref, u_ref, vg_ref, onw_ref, sg_ref, sb_ref, ws_ref, bst_ref, ya_ref, yb_ref, phi_ref):
        onw = onw_ref[...]
        for h in range(N_HEADS):
            sl = slice(h * D_HEAD, (h + 1) * D_HEAD)
            ya_ref[:, sl] = _ya_head(o_ref[:, sl], z_ref[:, sl].astype(F32), onw).astype(BF)
        u, vg = u_ref[...].astype(F32), vg_ref[...].astype(F32)
        phi_u, phi_v = _norm_cdf(u), _norm_cdf(vg)
        phi_ref[:, :D_MODEL] = phi_u
        phi_ref[:, D_MODEL:] = phi_v
        ua, vl = u * phi_u, _ln(vg * phi_v, sg_ref[...], sb_ref[...])
        wsm = _ws_masked(ws_ref[...])
        bst = bst_ref[...]
        for blk in range(tm // SGU_BLOCK):
            rs = slice(blk * SGU_BLOCK, (blk + 1) * SGU_BLOCK)
            for gi in range(SGU_GROUPS):
                cs = slice(gi * D_HEAD, (gi + 1) * D_HEAD)
                sp = _dot(wsm[gi], vl[rs, cs]) + bst[:, gi:gi + 1]
                yb_ref[rs, cs] = (ua[rs, cs] * sp).astype(BF)

    blk = lambda col: pl.BlockSpec((tm, D_MODEL), lambda i: (i, col))
    full = lambda shape: pl.BlockSpec(shape, lambda i: (0,) * len(shape))
    est = 10 * _nbytes((tm, D_MODEL), F32)
    return pl.pallas_call(
        body, name="mix_prep", grid=(S // tm,),
        in_specs=[blk(0), blk(0), blk(1), blk(2), full((1, D_HEAD)), full((1, D_MODEL)), full((1, D_MODEL)),
                  full((SGU_GROUPS, SGU_BLOCK, SGU_BLOCK)), full((SGU_BLOCK, LANES))],
        out_specs=[blk(0), blk(0), pl.BlockSpec((tm, 2 * D_MODEL), lambda i: (i, 0))],
        out_shape=[jax.ShapeDtypeStruct((S, D_MODEL), BF)] * 2 + [jax.ShapeDtypeStruct((S, 2 * D_MODEL), F32)],
        compiler_params=_cparams(est, ("parallel",)),
    )(o, proj, proj, proj, onw, sg, sb, ws, bst)


def _mix_prep_bwd(o, proj, phi, onw, sg, sb, ws, bst, dya, dyb, dproj, *, tm=256):
    S = o.shape[0]
    tm = _tile(S, tm, SGU_BLOCK)

    def body(o_ref, z_ref, u_ref, vg_ref, phi_ref, onw_ref, sg_ref, sb_ref, ws_ref, bst_ref, dya_ref, dyb_ref, dproj_in,
             do_ref, dzuv_ref, donw_ref, dsg_ref, dsb_ref, dws_ref, dbst_ref, dvl_scr, dua_scr):
        dz_ref, du_ref, dvg_ref = (dzuv_ref.at[:, k * D_MODEL:(k + 1) * D_MODEL] for k in range(3))
        @pl.when(pl.program_id(0) == 0)
        def _():
            for r in (donw_ref, dsg_ref, dsb_ref, dws_ref, dbst_ref):
                r[...] = jnp.zeros_like(r)

        onw = onw_ref[...]
        donw = jnp.zeros((1, D_HEAD), F32)
        for h in range(N_HEADS):
            sl = slice(h * D_HEAD, (h + 1) * D_HEAD)
            _, vj = jax.vjp(_ya_head, o_ref[:, sl], z_ref[:, sl].astype(F32), onw)
            do_h, dz_h, donw_h = vj(dya_ref[:, sl])
            do_ref[:, sl] = do_h.astype(BF)
            dz_ref[:, sl] = dz_h.astype(BF)
            donw = donw + donw_h
        donw_ref[...] += _bcast_rows(donw)

        u, vg = u_ref[...].astype(F32), vg_ref[...].astype(F32)
        phi_u, phi_v = phi_ref[:, :D_MODEL], phi_ref[:, D_MODEL:]
        ua = u * phi_u
        vl, vj = jax.vjp(_ln, vg * phi_v, sg_ref[...], sb_ref[...])
        wsm = _ws_masked(ws_ref[...])
        bst = bst_ref[...]
        lane = _iota((SGU_BLOCK, LANES), 1)
        dbst = jnp.zeros((SGU_BLOCK, LANES), F32)
        cmask = _chunk_causal((SGU_BLOCK, SGU_BLOCK), 0, 1)
        for gi in range(SGU_GROUPS):
            cs = slice(gi * D_HEAD, (gi + 1) * D_HEAD)
            wg = wsm[gi]
            wgt = jnp.transpose(wg)
            dwg = jnp.zeros((SGU_BLOCK, SGU_BLOCK), F32)
            for blk in range(tm // SGU_BLOCK):
                rs = slice(blk * SGU_BLOCK, (blk + 1) * SGU_BLOCK)
                sp = _dot(wg, vl[rs, cs]) + bst[:, gi:gi + 1]
                dyb = dyb_ref[rs, cs]
                dsp = dyb * ua[rs, cs]
                dua_scr[rs, cs] = dyb * sp
                dvl_scr[rs, cs] = _dot(wgt, dsp)
                dwg = dwg + _dot_nt(dsp, vl[rs, cs])
                dbst = dbst + jnp.where(lane == gi, jnp.sum(dsp, -1, keepdims=True), 0.0)
            dws_ref[gi] += jnp.where(cmask, dwg, 0.0)
        dbst_ref[...] += dbst
        dgv, dsg, dsb = vj(dvl_scr[...])
        du_ref[...] = (dua_scr[...] * (phi_u + u * _norm_pdf(u))).astype(BF)
        dvg_ref[...] = (dgv * (phi_v + vg * _norm_pdf(vg))).astype(BF)
        dsg_ref[...] += _bcast_rows(dsg)
        dsb_ref[...] += _bcast_rows(dsb)

    blk = lambda col: pl.BlockSpec((tm, D_MODEL), lambda i: (i, col))
    full = lambda shape: pl.BlockSpec(shape, lambda i: (0,) * len(shape))
    est = 16 * _nbytes((tm, D_MODEL), F32)
    outs = pl.pallas_call(
        body, name="mix_prep_bwd", grid=(S // tm,),
        in_specs=[blk(0), blk(0), blk(1), blk(2), pl.BlockSpec((tm, 2 * D_MODEL), lambda i: (i, 0)),
                  full((1, D_HEAD)), full((1, D_MODEL)), full((1, D_MODEL)),
                  full((SGU_GROUPS, SGU_BLOCK, SGU_BLOCK)), full((SGU_BLOCK, LANES)), blk(0), blk(0),
                  pl.BlockSpec(memory_space=pl.ANY)],
        out_specs=[blk(0), pl.BlockSpec((tm, 3 * D_MODEL), lambda i: (i, 1)),
                   full((SUBLANES, D_HEAD)), full((SUBLANES, D_MODEL)), full((SUBLANES, D_MODEL)),
                   full((SGU_GROUPS, SGU_BLOCK, SGU_BLOCK)), full((SGU_BLOCK, LANES))],
        out_shape=[jax.ShapeDtypeStruct((S, D_MODEL), BF), jax.ShapeDtypeStruct(dproj.shape, BF),
                   jax.ShapeDtypeStruct((SUBLANES, D_HEAD), F32), jax.ShapeDtypeStruct((SUBLANES, D_MODEL), F32),
                   jax.ShapeDtypeStruct((SUBLANES, D_MODEL), F32),
                   jax.ShapeDtypeStruct((SGU_GROUPS, SGU_BLOCK, SGU_BLOCK), F32),
                   jax.ShapeDtypeStruct((SGU_BLOCK, LANES), F32)],
        input_output_aliases={12: 1},
        scratch_shapes=[pltpu.VMEM((tm, D_MODEL), F32)] * 2,
        compiler_params=_cparams(est, ("arbitrary",)),
    )(o, proj, proj, proj, phi, onw, sg, sb, ws, bst, dya, dyb, dproj)
    return outs


def _mm_gate_merge(ya, yb, wpa, wpb, proj, *, tm=512):
    S = ya.shape[0]
    tm = _tile(S, tm, SUBLANES * 2)

    def body(ya_ref, yb_ref, wa_ref, wb_ref, ga_ref, gb_ref, pa_ref, pb_ref, m_ref):
        pa = _dot(ya_ref[...], wa_ref[...]).astype(BF)
        pb = _dot(yb_ref[...], wb_ref[...]).astype(BF)
        pa_ref[...] = pa
        pb_ref[...] = pb
        m_ref[...] = (_sigmoid(ga_ref[...].astype(F32)) * pa.astype(F32)
                      + _sigmoid(gb_ref[...].astype(F32)) * pb.astype(F32)).astype(BF)

    blk = lambda col: pl.BlockSpec((tm, D_MODEL), lambda i: (i, col))
    wsp = pl.BlockSpec((D_MODEL, D_MODEL), lambda i: (0, 0))
    return pl.pallas_call(
        body, name="mm_gate_merge", grid=(S // tm,),
        in_specs=[blk(0), blk(0), wsp, wsp, blk(3), blk(4)], out_specs=[blk(0)] * 3,
        out_shape=[jax.ShapeDtypeStruct((S, D_MODEL), BF)] * 3,
        compiler_params=_cparams(2 * _nbytes((D_MODEL, D_MODEL), BF) + 8 * _nbytes((tm, D_MODEL), F32), ("parallel",)),
    )(ya, yb, wpa, wpb, proj, proj)


def _mm_gate_merge_bwd(dmix, wo, pa, pb, proj, *, tm=512):
    S = pa.shape[0]
    tm = _tile(S, tm, SUBLANES * 2)

    def body(d_ref, w_ref, pa_ref, pb_ref, ga_ref, gb_ref, dpa_ref, dpb_ref, dg_ref):
        dm = _dot_nt(d_ref[...], w_ref[...])
        sa, sb = _sigmoid(ga_ref[...].astype(F32)), _sigmoid(gb_ref[...].astype(F32))
        dpa_ref[...] = (dm * sa).astype(BF)
        dpb_ref[...] = (dm * sb).astype(BF)
        dg_ref[:, :D_MODEL] = (dm * pa_ref[...].astype(F32) * sa * (1.0 - sa)).astype(BF)
        dg_ref[:, D_MODEL:] = (dm * pb_ref[...].astype(F32) * sb * (1.0 - sb)).astype(BF)

    blk = lambda col: pl.BlockSpec((tm, D_MODEL), lambda i: (i, col))
    est = _nbytes((D_MODEL, D_MODEL), BF) + 10 * _nbytes((tm, D_MODEL), F32)
    return pl.pallas_call(
        body, name="mm_gate_merge_bwd", grid=(S // tm,),
        in_specs=[blk(0), pl.BlockSpec((D_MODEL, D_MODEL), lambda i: (0, 0)), blk(0), blk(0), blk(3), blk(4)],
        out_specs=[blk(0), blk(0), pl.BlockSpec((tm, 2 * D_MODEL), lambda i: (i, 3))],
        out_shape=[jax.ShapeDtypeStruct((S, D_MODEL), BF)] * 2 + [jax.ShapeDtypeStruct((S, 8 * D_MODEL), BF)],
        compiler_params=_cparams(est, ("parallel",)),
    )(dmix, wo, pa, pb, proj, proj)


def _mm_swiglu(xb, wgt, wut, *, tm=1024, tn=768):
    S, K = xb.shape
    tm = _tile(S, tm, SUBLANES * 2)
    tn = _tile(FFN_K, tn, LANES)

    def body(x_ref, wg_ref, wu_ref, hg_ref, hu_ref, h_ref):
        x = x_ref[...]
        hg = _dot_nt(x, wg_ref[...]).astype(BF)
        hu = _dot_nt(x, wu_ref[...]).astype(BF)
        hg_ref[...] = hg
        hu_ref[...] = hu
        h_ref[...] = (_silu(hg.astype(F32)) * hu.astype(F32)).astype(BF)

    out = pl.BlockSpec((tm, tn), lambda i, j: (i, j))
    est = _nbytes((tm, K), BF) + 2 * _nbytes((K, tn), BF) + 6 * _nbytes((tm, tn), F32)
    return pl.pallas_call(
        body, name="mm_swiglu", grid=(S // tm, FFN_K // tn),
        in_specs=[pl.BlockSpec((tm, K), lambda i, j: (i, 0)), pl.BlockSpec((tn, K), lambda i, j: (j, 0)),
                  pl.BlockSpec((tn, K), lambda i, j: (j, 0))],
        out_specs=[out] * 3, out_shape=[jax.ShapeDtypeStruct((S, FFN_K), BF)] * 3,
        compiler_params=_cparams(est, ("parallel", "parallel")),
    )(xb, wgt, wut)


def _mm_swiglu_bwd(dffn, wd, hg, hu, *, tm=1024, tn=768):
    S, K = dffn.shape
    tm = _tile(S, tm, SUBLANES * 2)
    tn = _tile(FFN_K, tn, LANES)

    def body(d_ref, w_ref, hg_ref, hu_ref, dhg_ref, dhu_ref):
        dh = _dot_nt(d_ref[...], w_ref[...])
        act, dact = _silu_and_grad(hg_ref[...].astype(F32))
        dhg_ref[...] = (dh * hu_ref[...].astype(F32) * dact).astype(BF)
        dhu_ref[...] = (dh * act).astype(BF)

    out = pl.BlockSpec((tm, tn), lambda i, j: (i, j))
    est = _nbytes((tm, K), dffn.dtype) + _nbytes((tn, K), BF) + 8 * _nbytes((tm, tn), F32)
    return pl.pallas_call(
        body, name="mm_swiglu_bwd", grid=(S // tm, FFN_K // tn),
        in_specs=[pl.BlockSpec((tm, K), lambda i, j: (i, 0)), pl.BlockSpec((tn, K), lambda i, j: (j, 0)), out, out],
        out_specs=[out, out], out_shape=[jax.ShapeDtypeStruct((S, FFN_K), BF)] * 2,
        compiler_params=_cparams(est, ("parallel", "parallel")),
    )(dffn, wd, hg, hu)


def _mm_resid_ln(a, bmat, x, g, b, *, name, tm=512):
    S, K = a.shape
    tm = _tile(S, tm, SUBLANES * 2)

    def body(a_ref, w_ref, x_ref, g_ref, b_ref, pre_ref, y_ref, yb_ref):
        pre = ALPHA * x_ref[...] + _dot(a_ref[...], w_ref[...])
        y = _ln(pre, g_ref[...], b_ref[...])
        pre_ref[...] = pre
        y_ref[...] = y
        yb_ref[...] = y.astype(BF)

    blk = pl.BlockSpec((tm, D_MODEL), lambda i: (i, 0))
    vec = pl.BlockSpec((1, D_MODEL), lambda i: (0, 0))
    est = _nbytes((tm, K), BF) + _nbytes((K, D_MODEL), BF) + 8 * _nbytes((tm, D_MODEL), F32)
    return pl.pallas_call(
        body, name=name, grid=(S // tm,),
        in_specs=[pl.BlockSpec((tm, K), lambda i: (i, 0)), pl.BlockSpec((K, D_MODEL), lambda i: (0, 0)), blk, vec, vec],
        out_specs=[blk, blk, blk],
        out_shape=[jax.ShapeDtypeStruct((S, D_MODEL), F32)] * 2 + [jax.ShapeDtypeStruct((S, D_MODEL), BF)],
        compiler_params=_cparams(est, ("parallel",)),
    )(a, bmat, x, g, b)


def _ln_bwd(pre, g, b, dy, *, tm=512):
    S = pre.shape[0]
    tm = _tile(S, tm, SUBLANES)

    def body(p_ref, g_ref, b_ref, dy_ref, dp_ref, dg_ref, db_ref):
        @pl.when(pl.program_id(0) == 0)
        def _():
            dg_ref[...] = jnp.zeros_like(dg_ref)
            db_ref[...] = jnp.zeros_like(db_ref)

        _, vj = jax.vjp(_ln, p_ref[...], g_ref[...], b_ref[...])
        dp, dg, db = vj(dy_ref[...])
        dp_ref[...] = dp
        dg_ref[...] += _bcast_rows(dg)
        db_ref[...] += _bcast_rows(db)

    blk = pl.BlockSpec((tm, D_MODEL), lambda i: (i, 0))
    vec = pl.BlockSpec((1, D_MODEL), lambda i: (0, 0))
    acc = pl.BlockSpec((SUBLANES, D_MODEL), lambda i: (0, 0))
    return pl.pallas_call(
        body, name="ln_bwd", grid=(S // tm,),
        in_specs=[blk, vec, vec, blk], out_specs=[blk, acc, acc],
        out_shape=[jax.ShapeDtypeStruct((S, D_MODEL), F32)] + [jax.ShapeDtypeStruct((SUBLANES, D_MODEL), F32)] * 2,
        compiler_params=_cparams(10 * _nbytes((tm, D_MODEL), F32), ("arbitrary",)),
    )(pre, g, b, dy)


def _loss_ln_bwd(y, tgt, pre, g, b, *, tm=512):
    S = y.shape[0]
    tm = _tile(S, tm, SUBLANES)

    def body(y_ref, t_ref, p_ref, g_ref, b_ref, dp_ref, dg_ref, db_ref, l_ref):
        @pl.when(pl.program_id(0) == 0)
        def _():
            for r in (dg_ref, db_ref, l_ref):
                r[...] = jnp.zeros_like(r)

        e = y_ref[...] - t_ref[...]
        l_ref[...] += 0.5 * jnp.sum(jnp.mean(e * e, -1, keepdims=True), keepdims=True)
        _, vj = jax.vjp(_ln, p_ref[...], g_ref[...], b_ref[...])
        dp, dg, db = vj(e * (1.0 / D_MODEL))
        dp_ref[...] = dp
        dg_ref[...] += _bcast_rows(dg)
        db_ref[...] += _bcast_rows(db)

    blk = pl.BlockSpec((tm, D_MODEL), lambda i: (i, 0))
    vec = pl.BlockSpec((1, D_MODEL), lambda i: (0, 0))
    acc = pl.BlockSpec((SUBLANES, D_MODEL), lambda i: (0, 0))
    return pl.pallas_call(
        body, name="loss_ln_bwd", grid=(S // tm,),
        in_specs=[blk, blk, blk, vec, vec], out_specs=[blk, acc, acc, pl.BlockSpec((SUBLANES, LANES), lambda i: (0, 0))],
        out_shape=[jax.ShapeDtypeStruct((S, D_MODEL), F32)] + [jax.ShapeDtypeStruct((SUBLANES, D_MODEL), F32)] * 2
                  + [jax.ShapeDtypeStruct((SUBLANES, LANES), F32)],
        compiler_params=_cparams(12 * _nbytes((tm, D_MODEL), F32), ("arbitrary",)),
    )(y, tgt, pre, g, b)


def _layer_fwd(x, xb, w, late):
    pq = _mm(xb, w["win"], mode="nn", name="mm_in_qkv", tm=1024, tn=1024, cols=(0, 3 * D_MODEL))
    proj = _mm(xb, w["win"], mode="nn", name="mm_in_rest", tm=1024, tn=1024, cols=(3 * D_MODEL, 5 * D_MODEL), out_dtype=BF)
    ba = _mm(xb, w["wba"], mode="nn", name="mm_in_ba", tm=1024, tn=LANES)
    qn, kn, vv, gb, conv_out = _qkv_prep(pq, ba, w["convw"], w["arow"], w["dtrow"])
    o, st, tinv, wy_u, wy_w = _delta_fwd(qn, kn, vv, gb)
    ya, yb, phi = _mix_prep(o, proj, w["onw"], w["sg"], w["sb"], w["ws"], w["bst"])
    w = {**w, **late(ya)}
    pa, pb, m = _mm_gate_merge(ya, yb, w["wpa"], w["wpb"], proj)
    pre1, x1, x1b = _mm_resid_ln(m, w["wo"], x, w["ln1g"], w["ln1b"], name="mm_out_ln")
    hg, hu, h = _mm_swiglu(x1b, w["wgt"], w["wut"])
    pre2, x2, x2b = _mm_resid_ln(h, w["wd"], x1, w["ln2g"], w["ln2b"], name="mm_down_ln")
    saved = dict(xb=xb, pq=pq, conv_out=conv_out, proj=proj, phi=phi, ba=ba, qn=qn, kn=kn, vv=vv, gb=gb, o=o, st=st, tinv=tinv, wy_u=wy_u, wy_w=wy_w,
                 ya=ya, yb=yb,
                 pa=pa, pb=pb, m=m, pre1=pre1, x1b=x1b, hg=hg, hu=hu, h=h, pre2=pre2)
    return x2, x2b, saved, w


def _layer_bwd(dpre2, ln2_grads, w, s, on_part=None):
    g = {}
    started = lambda part: on_part(part, g) if on_part is not None else None
    after = lambda v, token: v if token is None else v + token.astype(v.dtype)
    g["ln2g"], g["ln2b"] = ln2_grads
    dhg, dhu = _mm_swiglu_bwd(dpre2, w["wd"], s["hg"], s["hu"])
    g["wd"] = _mm(s["h"], dpre2, mode="tn", name="mm_tn_down", tm=1536, tk=1024, out_dtype=BF)
    dx1 = _mm(dhg, w["wgt"], mode="nn", name="mm_nn_gu", pair=(dhu, w["wut"]), add=dpre2, add_scale=ALPHA, tm=1024, tk=1536)
    g["wgt"] = _mm(dhg, s["x1b"], mode="tn", name="mm_tn_gu", tm=1536, tn=1024, tk=2048, out_dtype=BF)
    g["wut"] = _mm(dhu, s["x1b"], mode="tn", name="mm_tn_gu", tm=1536, tn=1024, tk=2048, out_dtype=BF)
    dpre1, g["ln1g"], g["ln1b"] = _ln_bwd(s["pre1"], w["ln1g"], w["ln1b"], dx1)
    g["wo"] = _mm(s["m"], dpre1, mode="tn", name="mm_tn_sq", tm=1024, tk=1024, out_dtype=BF)
    dpa, dpb, dproj = _mm_gate_merge_bwd(dpre1, w["wo"], s["pa"], s["pb"], s["proj"])
    dya = _mm(dpa, w["wpa"], mode="nt", name="mm_nt_sq")
    g["wpa"] = _mm(s["ya"], dpa, mode="tn", name="mm_tn_sq", tm=1024, tk=1024, out_dtype=BF)
    dyb = _mm(dpb, w["wpb"], mode="nt", name="mm_nt_sq")
    g["wpb"] = _mm(s["yb"], dpb, mode="tn", name="mm_tn_sq", tm=1024, tk=1024, out_dtype=BF)
    do, dproj, g["onw"], g["sg"], g["sb"], g["ws"], g["bst"] = _mix_prep_bwd(
        s["o"], s["proj"], s["phi"], after(w["onw"], started("late")), w["sg"], w["sb"], w["ws"], w["bst"], dya, dyb, dproj)
    dqn, dkn, dvv, dgb = _delta_bwd(s["qn"], s["kn"], s["vv"], s["gb"], s["st"], s["tinv"], s["wy_u"], s["wy_w"], do)
    dc, dba, g["convw"], g["arow"], g["dtrow"] = _qkv_prep_bwd(
        s["pq"], s["conv_out"], s["ba"], w["arow"], w["dtrow"], dqn, dkn, dvv, dgb)
    dproj = _conv_bwd(dc, w["convw"], dproj)
    g["win"] = _mm(s["xb"], dproj, mode="tn", name="mm_tn_in", tm=1024, tn=1024, tk=2048, out_dtype=BF)
    g["wba"] = _mm(s["xb"], dba, mode="tn", name="mm_tn_ba", tm=1024, tn=LANES, tk=1024, out_dtype=BF)
    dx = _mm(dba, after(w["wba"], started("early")), mode="nt", name="mm_nt_ba", add=dpre1, add_scale=ALPHA, tm=1024)
    dx = _mm(dproj, w["win"], mode="nt", name="mm_nt_in", add=dx, add_scale=1.0, tm=1024, tk=2048)
    return dx, g


def _local_step(x, xb, tgt, layers, on_grads=None):
    saved, weights = [], []
    for layer in layers:
        x, xb, s, w = _layer_fwd(x, xb, *layer(x))
        saved.append(s)
        weights.append(w)
    last = len(layers) - 1
    dpre2, dg, db, lacc = _loss_ln_bwd(x, tgt, saved[last]["pre2"], weights[last]["ln2g"], weights[last]["ln2b"])
    grads = [None] * len(layers)
    for l in reversed(range(len(layers))):
        on_part = functools.partial(on_grads, l) if on_grads is not None else None
        dx, grads[l] = _layer_bwd(dpre2, (dg, db), weights[l], saved[l], on_part)
        if l > 0:
            dpre2, dg, db = _ln_bwd(saved[l - 1]["pre2"], weights[l - 1]["ln2g"], weights[l - 1]["ln2b"], dx)
    return lacc[0, 0], dx, grads


_QKVZ = 4 * D_MODEL
_BA = 2 * N_HEADS


WEIGHT_NAMES = ("w_in", "conv_w", "a_log", "dt_bias", "o_norm_w", "sgu_ln_g", "sgu_ln_b", "w_s", "b_s", "w_pa", "w_pb",
                "w_o", "ln1_g", "ln1_b", "w_ffn_gate", "w_ffn_up", "w_ffn_down", "ln2_g", "ln2_b")
WIRE = ("w_in", "w_ffn_gate", "w_ffn_up", "w_ffn_down", "w_pa", "w_pb", "w_o", "conv_w")
SMALL = (("a_log", N_HEADS), ("dt_bias", N_HEADS), ("o_norm_w", D_HEAD), ("sgu_ln_g", D_MODEL), ("sgu_ln_b", D_MODEL),
         ("w_s", SGU_GROUPS * SGU_BLOCK * SGU_BLOCK), ("b_s", SGU_GROUPS * SGU_BLOCK),
         ("ln1_g", D_MODEL), ("ln1_b", D_MODEL), ("ln2_g", D_MODEL), ("ln2_b", D_MODEL))
SMALL_ROWS = -(-sum(n for _, n in SMALL) // (LANES * SUBLANES)) * SUBLANES
N_MAIN_TILES = (N_IN - _BA) // D_MODEL
ADAM_TILES = dict(w_in=(128, "adamw_in"), w_ffn_gate=(32, "adamw_ffn_rows"), w_ffn_up=(32, "adamw_ffn_rows"),
                  w_ffn_down=(32, "adamw_ffn_rows"), w_pa=(128, "adamw_sq"), w_pb=(128, "adamw_sq"), w_o=(128, "adamw_sq"),
                  conv_w=(CONV_K, "adamw_conv"))


def _pad_to(a, axis, size):
    pads = [(0, 0)] * a.ndim
    pads[axis] = (0, size - a.shape[axis])
    return jnp.pad(a, pads)


def _t(a):
    return jnp.swapaxes(a, 1, 2)


def _wire_blocks(p):
    return dict(
        w_in=_pad_to(p["w_in"].astype(BF), 2, IN_PAD),
        w_ffn_gate=_pad_to(_t(p["w_ffn_gate"]).astype(BF), 1, FFN_PAD), w_ffn_up=_pad_to(_t(p["w_ffn_up"]).astype(BF), 1, FFN_PAD),
        w_ffn_down=_pad_to(p["w_ffn_down"].astype(BF), 1, FFN_PAD),
        w_pa=p["w_pa"].astype(BF), w_pb=p["w_pb"].astype(BF), w_o=p["w_o"].astype(BF),
        conv_w=_pad_to(p["conv_w"], 1, SUBLANES),
    )


def _by_columns(blocks):
    n, r, c = blocks.shape
    return jnp.transpose(blocks, (1, 0, 2)).reshape(r, n * c)


def _to_slots(full, c):
    r = full.shape[0]
    return jnp.transpose(full.reshape(r, N_DEV, c), (1, 0, 2))


def _lane_row(v, at):
    return jnp.pad(v[None], ((0, 0), (at, LANES - at - v.shape[0])))


TRANSPOSED = ("w_ffn_gate", "w_ffn_up")
EARLY = ("w_in", "conv_w")
LATE = ("w_pa", "w_pb", "w_o", "w_ffn_gate", "w_ffn_up", "w_ffn_down")


def _early_weights(stacks, p, l):
    return dict(
        win=_perm_in(stacks["w_in"], D_MODEL, N_MAIN_TILES), wba=_perm_in(stacks["w_in"], LANES, 1),
        convw=_by_columns(stacks["conv_w"][:, :CONV_K]),
        arow=_lane_row(p["a_log"][l], N_HEADS), dtrow=_lane_row(p["dt_bias"][l], N_HEADS),
        onw=p["o_norm_w"][l][None], sg=p["sgu_ln_g"][l][None], sb=p["sgu_ln_b"][l][None],
        ws=p["w_s"][l], bst=_pad_to(p["b_s"][l].T, 1, LANES),
        ln1g=p["ln1_g"][l][None], ln1b=p["ln1_b"][l][None], ln2g=p["ln2_g"][l][None], ln2b=p["ln2_b"][l][None],
    )


def _late_weights(stacks):
    return dict(
        wpa=stacks["w_pa"].reshape(D_MODEL, D_MODEL), wpb=stacks["w_pb"].reshape(D_MODEL, D_MODEL),
        wo=stacks["w_o"].reshape(D_MODEL, D_MODEL),
        wgt=stacks["w_ffn_gate"].reshape(FFN_K, D_MODEL), wut=stacks["w_ffn_up"].reshape(FFN_K, D_MODEL),
        wd=stacks["w_ffn_down"].reshape(FFN_K, D_MODEL),
    )


def _small_pack(parts):
    flat = jnp.concatenate([parts[n].reshape(-1) for n, _ in SMALL])
    return _pad_to(flat, 0, SMALL_ROWS * LANES).reshape(SMALL_ROWS, LANES)


def _small_unpack(rows, like):
    flat, out, off = rows.reshape(-1), {}, 0
    for n, size in SMALL:
        out[n] = flat[off:off + size].reshape(like[n].shape[1:])
        off += size
    return out


def _late_slots(g):
    slots = dict(
        w_ffn_gate=g["wgt"].reshape(N_DEV, FFN_PAD, D_MODEL), w_ffn_up=g["wut"].reshape(N_DEV, FFN_PAD, D_MODEL),
        w_ffn_down=g["wd"].reshape(N_DEV, FFN_PAD, D_MODEL),
        w_pa=g["wpa"].reshape(N_DEV, D_MODEL // N_DEV, D_MODEL), w_pb=g["wpb"].reshape(N_DEV, D_MODEL // N_DEV, D_MODEL),
        w_o=g["wo"].reshape(N_DEV, D_MODEL // N_DEV, D_MODEL),
    )
    return [slots[n] for n in LATE]


def _early_slots(g):
    slots = [_perm_out(g["win"], g["wba"]), _pad_to(_to_slots(g["convw"][:CONV_K], 3 * D_MODEL // N_DEV), 1, SUBLANES)]
    small = _small_pack(dict(
        a_log=g["arow"][0, N_HEADS:2 * N_HEADS], dt_bias=g["dtrow"][0, N_HEADS:2 * N_HEADS], o_norm_w=g["onw"][0],
        sgu_ln_g=g["sg"][0], sgu_ln_b=g["sb"][0], w_s=g["ws"], b_s=g["bst"][:, :SGU_GROUPS].T,
        ln1_g=g["ln1g"][0], ln1_b=g["ln1b"][0], ln2_g=g["ln2g"][0], ln2_b=g["ln2b"][0]))
    return slots, small


def _in_tile_start(j, tile_w):
    if tile_w == LANES:
        return jnp.int32(_QKVZ)
    return j * D_MODEL + jnp.where(j >= _QKVZ // D_MODEL, _BA, 0)


def _select(rows_iota, cols_iota, dev, start, valid):
    hit = (rows_iota + (dev * IN_BLOCK - start) == cols_iota) & (rows_iota < IN_BLOCK) & (cols_iota < valid)
    return jnp.where(hit, 1.0, 0.0).astype(BF)


def _perm_in(stack, tile_w, n_tiles):
    valid = _BA if tile_w == LANES else tile_w

    def first_dev(j):
        return lax.div(_in_tile_start(j, tile_w), jnp.int32(IN_BLOCK))

    def body(w_ref, o_ref, acc_ref):
        j, k = pl.program_id(0), pl.program_id(1)
        sel = _select(_iota((IN_PAD, tile_w), 0), _iota((IN_PAD, tile_w), 1), first_dev(j) + k,
                      _in_tile_start(j, tile_w), valid)
        part = jnp.dot(w_ref[0], sel, preferred_element_type=F32)

        @pl.when(k == 0)
        def _():
            acc_ref[...] = part

        @pl.when(k == 1)
        def _():
            o_ref[...] = (acc_ref[...] + part).astype(BF)

    est = _nbytes((D_MODEL, IN_PAD), BF) + 3 * _nbytes((D_MODEL, tile_w), F32) + 2 * _nbytes((IN_PAD, tile_w), F32)
    return pl.pallas_call(
        body, name="perm_in" if tile_w != LANES else "perm_in_ba", grid=(n_tiles, 2),
        in_specs=[pl.BlockSpec((1, D_MODEL, IN_PAD), lambda j, k: (jnp.minimum(first_dev(j) + k, N_DEV - 1), 0, 0))],
        out_specs=pl.BlockSpec((D_MODEL, tile_w), lambda j, k: (0, j)),
        out_shape=jax.ShapeDtypeStruct((D_MODEL, n_tiles * tile_w), BF),
        scratch_shapes=[pltpu.VMEM((D_MODEL, tile_w), F32)],
        compiler_params=_cparams(est, ("parallel", "arbitrary")),
    )(stack)


def _perm_out(dmain, dba):
    def tile(d, s):
        c0 = d * IN_BLOCK
        first = lax.div(c0 - jnp.where(c0 < _QKVZ, 0, jnp.minimum(c0 - _QKVZ, _BA)), jnp.int32(D_MODEL))
        return jnp.minimum(first + jnp.minimum(s, 1), N_MAIN_TILES - 1)

    def body(dm_ref, db_ref, o_ref, acc_ref):
        d, s = pl.program_id(0), pl.program_id(1)

        @pl.when(s == 0)
        def _():
            acc_ref[...] = jnp.zeros_like(acc_ref)

        start = _in_tile_start(tile(d, s), D_MODEL)
        overlaps = (start < (d + 1) * IN_BLOCK) & (d * IN_BLOCK < start + D_MODEL)

        @pl.when((s < 2) & overlaps)
        def _():
            sel = _select(_iota((D_MODEL, IN_PAD), 1), _iota((D_MODEL, IN_PAD), 0), d, start, D_MODEL)
            acc_ref[...] += jnp.dot(dm_ref[...], sel, preferred_element_type=F32)

        @pl.when(s == 2)
        def _():
            sel = _select(_iota((LANES, IN_PAD), 1), _iota((LANES, IN_PAD), 0), d, jnp.int32(_QKVZ), _BA)
            o_ref[0] = (acc_ref[...] + jnp.dot(db_ref[...], sel, preferred_element_type=F32)).astype(BF)

    est = 2 * _nbytes((D_MODEL, D_MODEL), BF) + 4 * _nbytes((D_MODEL, IN_PAD), F32)
    return pl.pallas_call(
        body, name="perm_out", grid=(N_DEV, 3),
        in_specs=[pl.BlockSpec((D_MODEL, D_MODEL), lambda d, s: (0, tile(d, s))),
                  pl.BlockSpec((D_MODEL, LANES), lambda d, s: (0, 0))],
        out_specs=pl.BlockSpec((1, D_MODEL, IN_PAD), lambda d, t: (d, 0, 0)),
        out_shape=jax.ShapeDtypeStruct((N_DEV, D_MODEL, IN_PAD), BF),
        scratch_shapes=[pltpu.VMEM((D_MODEL, IN_PAD), F32)],
        compiler_params=_cparams(est, ("parallel", "arbitrary")),
    )(dmain, dba)


def _mesh_place():
    x, y, c = (lax.axis_index(a) for a in MESH_AXES)
    return x, y, c


def _slot(x, y, c):
    return 4 * x + 2 * y + c


def _peer(place, j):
    x, y, c = place
    return (1 - x if j & 4 else x, 1 - y if j & 2 else y, 1 - c if j & 1 else c)


_HBM = pl.BlockSpec(memory_space=pltpu.HBM)
_SEM = pl.BlockSpec(memory_space=pltpu.SEMAPHORE)
_EFFECT = pltpu.SideEffectType.DATAFLOW_SIDE_EFFECTING


def _remote_copy(src_ref, land_ref, slot, per_slot, pslot, sems, u, j, peer):
    return pltpu.make_async_remote_copy(
        src_ref=src_ref.at[pslot] if per_slot else src_ref, dst_ref=land_ref.at[slot],
        send_sem=sems[0].at[u * (N_DEV - 1) + j - 1], recv_sem=sems[1].at[u * (N_DEV - 1) + j - 1],
        device_id=peer, device_id_type=pl.DeviceIdType.MESH)


def _own_copy(src_ref, land_ref, me, per_slot, sems, u):
    return pltpu.make_async_copy(src_ref.at[me] if per_slot else src_ref, land_ref.at[me], sems[2].at[u])


def _exchange_start(name, srcs, per_slot):
    n = len(srcs)
    lands = [jax.ShapeDtypeStruct(s.shape if p else (N_DEV,) + s.shape, s.dtype) for s, p in zip(srcs, per_slot)]

    def body(*refs):
        src_refs, sems, land_refs, token = refs[:n], refs[n:n + 3], refs[2 * n + 3:3 * n + 3], refs[-1]
        place = _mesh_place()
        me = _slot(*place)
        for u in range(n):
            _own_copy(src_refs[u], land_refs[u], me, per_slot[u], sems, u).start()
            for j in range(1, N_DEV):
                peer = _peer(place, j)
                _remote_copy(src_refs[u], land_refs[u], me, per_slot[u], _slot(*peer), sems, u, j, peer).start()
        token[...] = jnp.zeros_like(token)

    hbm = lambda a: pltpu.HBM(a.shape, a.dtype)
    sem = pltpu.SemaphoreType.DMA((n * (N_DEV - 1),))
    outs = pl.pallas_call(
        body, name=name,
        out_shape=(sem, sem, pltpu.SemaphoreType.DMA((n,)), *[hbm(a) for a in srcs], *[hbm(a) for a in lands],
                   jax.ShapeDtypeStruct((SUBLANES, LANES), F32)),
        in_specs=[_HBM] * n, out_specs=(_SEM, _SEM, _SEM, *[_HBM] * (2 * n), pl.BlockSpec(memory_space=pltpu.VMEM)),
        input_output_aliases={i: 3 + i for i in range(n)},
        compiler_params=pltpu.CompilerParams(has_side_effects=_EFFECT),
    )(*[pltpu.with_memory_space_constraint(a, pltpu.HBM) for a in srcs])
    return tuple(outs[:3]), list(outs[3:3 + n]), list(outs[3 + n:3 + 2 * n]), outs[-1]


def _exchange_wait(name, sems, srcs, lands, units, per_slot, after):
    m = len(units)
    after = list(after) if isinstance(after, (list, tuple)) else [after]

    def body(*refs):
        src_refs, land_refs, sem_refs = refs[:m], refs[m:2 * m], refs[2 * m:2 * m + 3]
        place = _mesh_place()
        me = _slot(*place)
        for i, u in enumerate(units):
            _own_copy(src_refs[i], land_refs[i], me, per_slot[u], sem_refs, u).wait()
            for j in range(1, N_DEV):
                peer = _peer(place, j)
                pslot = _slot(*peer)
                cp = _remote_copy(src_refs[i], land_refs[i], pslot, per_slot[u], pslot, sem_refs, u, j, peer)
                cp.wait_send()
                cp.wait_recv()

    hbm = lambda a: pltpu.HBM(a.shape, a.dtype)
    outs = pl.pallas_call(
        body, name=name, out_shape=tuple(hbm(a) for a in list(srcs) + list(lands)),
        in_specs=[_HBM] * (2 * m) + [_SEM] * 3 + [pl.BlockSpec(memory_space=pl.ANY)] * len(after),
        out_specs=tuple([_HBM] * (2 * m)),
        input_output_aliases={i: i for i in range(2 * m)},
        compiler_params=pltpu.CompilerParams(has_side_effects=_EFFECT),
    )(*srcs, *lands, *sems, *after)
    return list(outs[m:])


def _adam_update(g, w, m, v):
    m = ADAM_B1 * m + (1.0 - ADAM_B1) * g
    v = ADAM_B2 * v + (1.0 - ADAM_B2) * jnp.square(g)
    m_hat = m / (1.0 - ADAM_B1 ** ADAM_STEP)
    v_hat = v / (1.0 - ADAM_B2 ** ADAM_STEP)
    return -ADAM_LR * (m_hat / (jnp.sqrt(v_hat) + ADAM_EPS) + ADAM_WD * w), m, v


def _adamw(recvs, w, m, v, *, tr, name):
    L, R, C = w.shape
    rp = max(tr, SUBLANES * (4 // jnp.dtype(recvs[0].dtype).itemsize))
    Cp = recvs[0].shape[2]

    def body(*refs):
        r_refs, (w_ref, m_ref, v_ref, g_ref, d_ref, nm_ref, nv_ref) = refs[:L], refs[L:]
        for l in range(L):
            @pl.when(pl.program_id(0) == l)
            def _(r_ref=r_refs[l]):
                g = r_ref[0, :tr, :C].astype(F32)
                for s in range(1, N_DEV):
                    g = g + r_ref[s, :tr, :C].astype(F32)
                d, nm, nv = _adam_update(g, w_ref[0], m_ref[0], v_ref[0])
                g_ref[0], d_ref[0], nm_ref[0], nv_ref[0] = g, d, nm, nv

    blk = pl.BlockSpec((1, tr, C), lambda l, i: (l, i, 0))
    r_specs = [pl.BlockSpec((N_DEV, rp, Cp), lambda l, i, k=k: (0, jnp.where(l == k, i, 0), 0)) for k in range(L)]
    est = 2 * _nbytes((N_DEV, rp, Cp), recvs[0].dtype) + 8 * _nbytes((tr, Cp), F32)
    return pl.pallas_call(
        body, name=name, grid=(L, R // tr),
        in_specs=r_specs + [blk] * 3, out_specs=[blk] * 4,
        out_shape=[jax.ShapeDtypeStruct((L, R, C), F32)] * 4,
        compiler_params=_cparams(est, ("arbitrary", "arbitrary")),
    )(*recvs, w, m, v)


def _adamw_small(recv, w, m, v):
    def body(r_ref, w_ref, m_ref, v_ref, g_ref, d_ref, nm_ref, nv_ref):
        g = r_ref[0]
        for s in range(1, N_DEV):
            g = g + r_ref[s]
        g_ref[...] = g
        d_ref[...], nm_ref[...], nv_ref[...] = _adam_update(g, w_ref[...], m_ref[...], v_ref[...])

    vm = pl.BlockSpec(memory_space=pltpu.VMEM)
    return pl.pallas_call(
        body, name="adamw_small", in_specs=[vm] * 4, out_specs=[vm] * 4,
        out_shape=[jax.ShapeDtypeStruct((SMALL_ROWS, LANES), F32)] * 4,
        compiler_params=_cparams(20 * _nbytes((SMALL_ROWS, LANES), F32)),
    )(recv, w, m, v)


def kernel(x, w_in, conv_w, a_log, dt_bias, o_norm_w, sgu_ln_g, sgu_ln_b, w_s, b_s, w_pa, w_pb, w_o, ln1_g, ln1_b, w_ffn_gate, w_ffn_up, w_ffn_down, ln2_g, ln2_b, loss_target, m_w_in, m_conv_w, m_a_log, m_dt_bias, m_o_norm_w, m_sgu_ln_g, m_sgu_ln_b, m_w_s, m_b_s, m_w_pa, m_w_pb, m_w_o, m_ln1_g, m_ln1_b, m_w_ffn_gate, m_w_ffn_up, m_w_ffn_down, m_ln2_g, m_ln2_b, v_w_in, v_conv_w, v_a_log, v_dt_bias, v_o_norm_w, v_sgu_ln_g, v_sgu_ln_b, v_w_s, v_b_s, v_w_pa, v_w_pb, v_w_o, v_ln1_g, v_ln1_b, v_w_ffn_gate, v_w_ffn_up, v_w_ffn_down, v_ln2_g, v_ln2_b):
    given = dict(locals())
    P = {n: given[n] for n in WEIGHT_NAMES}
    M = {n: given["m_" + n] for n in WEIGHT_NAMES}
    V = {n: given["v_" + n] for n in WEIGHT_NAMES}

    wire = _wire_blocks(P)
    units = [(n, l) for l in range(DEPTH) for n in EARLY + LATE]
    whole = [False] * len(units)
    g_sems, g_srcs, g_lands, g_token = _exchange_start("gather_start", [wire[n][l] for n, l in units], whole)

    one = 1.0 + g_token[0, 0]
    xb = (x[0] * one).astype(BF)
    small_in = [[_small_pack({n: T[n][l] * one for n, _ in SMALL}) for T in (P, M, V)] for l in range(DEPTH)]
    adam_in = {n: (P[n], M[n], V[n]) for n in WIRE}
    adam_in["w_in"], _ = lax.optimization_barrier((adam_in["w_in"], g_token))
    prepared = [xb, *[a for packs in small_in for a in packs], *adam_in["w_in"]]

    def gathered(name, names, l, after):
        idx = [units.index((n, l)) for n in names]
        got = _exchange_wait(name, g_sems, [g_srcs[i] for i in idx], [g_lands[i] for i in idx], idx, whole, after)
        return dict(zip(names, got))

    def layer(l):
        def weights(x_in):
            after = prepared if l == 0 else x_in
            early = _early_weights(gathered(f"gather_wait_early{l}", EARLY, l, after), P, l)
            return early, lambda ya: _late_weights(gathered(f"gather_wait_late{l}", LATE, l, ya))
        return weights

    pending = {}

    def on_grads(l, part, g):
        if part == "late":
            srcs, names = _late_slots(g), LATE
            per_slot = [True] * len(srcs)
        else:
            slots, small = _early_slots(g)
            srcs, names = slots + [small], EARLY + ("small",)
            per_slot = [True] * len(slots) + [False]
        sems, s_thru, l_thru, token = _exchange_start(f"exchange_start_{part}{l}", srcs, per_slot)
        pending[l, part] = (names, sems, s_thru, l_thru, per_slot)
        return token[0, 0]

    loss_local, dx, _ = _local_step(x[0], xb, loss_target[0], [layer(l) for l in range(DEPTH)], on_grads)
    loss = lax.psum(loss_local, MESH_AXES)

    recv = [{} for _ in range(DEPTH)]

    def received(l, part, after):
        names, sems, s_thru, l_thru, per_slot = pending[l, part]
        got = _exchange_wait(f"exchange_wait_{part}{l}", sems, s_thru, l_thru, list(range(len(s_thru))), per_slot, after)
        recv[l].update(zip(names, got))

    out = {}

    def adamw(names):
        for n in names:
            tr, name = ADAM_TILES[n]
            view = _t if n in TRANSPOSED else (lambda a: a)
            res = _adamw([recv[l][n] for l in range(DEPTH)], *[view(a) for a in adam_in[n]], tr=tr, name=name)
            out[n] = [view(r) for r in res]

    for l in reversed(range(DEPTH)):
        received(l, "late", dx)
    adamw(LATE)
    for l in reversed(range(DEPTH)):
        received(l, "early", out[LATE[-1]][0])
    adamw(EARLY)
    small = [_adamw_small(recv[l]["small"], *small_in[l]) for l in range(DEPTH)]
    for n, _ in SMALL:
        out[n] = [jnp.stack([_small_unpack(small[l][i], P)[n] for l in range(DEPTH)]) for i in range(4)]
    return (loss, dx[None], *[out[n][i] for i in range(4) for n in WEIGHT_NAMES])
```

```python
import functools
import math

import jax
import jax.numpy as jnp
from jax import lax
from jax.experimental import pallas as pl
from jax.experimental.pallas import tpu as pltpu

F32 = jnp.float32
BF = jnp.bfloat16
HIGHEST = lax.Precision.HIGHEST

D_MODEL = 1024
DEPTH = 2
N_HEADS = 8
D_HEAD = 128
CONV_K = 4
SGU_BLOCK = 128
SGU_GROUPS = 8
SGU_CHUNK = 64
FFN_HIDDEN = 2816
N_IN = 8208
N_DEV = 8
IN_BLOCK, IN_PAD = N_IN // N_DEV, 1152
FFN_BLOCK, FFN_PAD = FFN_HIDDEN // N_DEV, 384
FFN_K = N_DEV * FFN_PAD
ALPHA = (2 * DEPTH) ** 0.25
LN_EPS = 1e-5
RMS_EPS = 1e-6
ADAM_LR, ADAM_B1, ADAM_B2, ADAM_EPS, ADAM_WD, ADAM_STEP = 0.001, 0.9, 0.999, 1e-08, 0.01, 10

MESH_AXES = ("x", "y", "c")
DELTA_CHUNK = 128
DELTA_HEADS_PER_STEP = 8
LANES = 128
SUBLANES = 8
VMEM_BYTES = 64 * 1024 * 1024
HALO = SUBLANES
HALO_BF = 2 * SUBLANES


def _cparams(est_bytes, dims=None):
    limit = int(min(max(2 * est_bytes + (8 << 20), 32 << 20), VMEM_BYTES - (6 << 20)))
    kw = dict(vmem_limit_bytes=limit)
    if dims is not None:
        kw["dimension_semantics"] = dims
    return pltpu.CompilerParams(**kw)


def _nbytes(shape, dtype):
    return math.prod(shape) * jnp.dtype(dtype).itemsize


def _dims(kind, ndim):
    lhs, rhs = {"nn": (1, 0), "nt": (1, 1), "tn": (0, 0)}[kind]
    b = ndim - 2
    return (((lhs + b,), (rhs + b,)), (tuple(range(b)), tuple(range(b))))


def _mxu(a, b, kind):
    return lax.dot_general(a, b, _dims(kind, a.ndim), preferred_element_type=F32)


def _dot(a, b):
    return _mxu(a.astype(BF), b.astype(BF), "nn")


def _dot_nt(a, b):
    return _mxu(a.astype(BF), b.astype(BF), "nt")


def _dot_tn(a, b):
    return _mxu(a.astype(BF), b.astype(BF), "tn")


def _split(a):
    hi = a.astype(BF)
    return hi, (a - hi.astype(F32)).astype(BF)


def _dot3(a, b, kind):
    (ah, al), (bh, bl) = _split(a), _split(b)
    return _mxu(ah, bh, kind) + (_mxu(ah, bl, kind) + _mxu(al, bh, kind))


def _dotf(a, b):
    return _dot3(a, b, "nn")


def _dotf_nt(a, b):
    return _dot3(a, b, "nt")


def _dot01(sel, x, kind="nn"):
    s = jnp.broadcast_to(sel.astype(BF), x.shape[:-2] + sel.shape)
    h1 = x.astype(BF)
    r1 = x - h1.astype(F32)
    h2 = r1.astype(BF)
    h3 = (r1 - h2.astype(F32)).astype(BF)
    return _mxu(s, h1, kind) + (_mxu(s, h2, kind) + _mxu(s, h3, kind))


def _sigmoid(x):
    return 0.5 * jnp.tanh(0.5 * x) + 0.5


def _silu(x):
    return x * _sigmoid(x)


def _silu_and_grad(x):
    s = _sigmoid(x)
    return x * s, s * (1.0 + x * (1.0 - s))


def _softplus(x):
    return jnp.maximum(x, 0.0) + jnp.log1p(jnp.exp(-jnp.abs(x)))


def _ln(x, g, b):
    mu = jnp.mean(x, -1, keepdims=True)
    xc = x - mu
    var = jnp.mean(xc * xc, -1, keepdims=True)
    return xc * lax.rsqrt(var + LN_EPS) * g + b


def _iota(shape, dim):
    return lax.broadcasted_iota(jnp.int32, shape, dim)


def _tile(n, pref, align):
    if n <= pref:
        return n
    t = (pref // align) * align
    while t >= align:
        if n % t == 0:
            return t
        t -= align
    raise ValueError(f"no tile for {n} (pref {pref}, align {align})")


def _bcast_rows(v, rows=SUBLANES):
    return jnp.broadcast_to(v, (rows, v.shape[-1]))


def _mm(a, b, *, mode, name, out_dtype=F32, add=None, add_scale=1.0, tm=512, tn=1024, tk=1024, cols=None, pair=None):
    if mode == "nn":
        (M, K), N = a.shape, b.shape[1]
    elif mode == "nt":
        (M, K), N = a.shape, b.shape[0]
    else:
        (K, M), N = a.shape, b.shape[1]
    col0 = 0
    if cols is not None:
        col0, N = cols
    tm = _tile(M, tm, LANES if mode == "tn" else SUBLANES * 2)
    tn = _tile(N, tn, LANES)
    tk = _tile(K, tk, LANES)
    nk = K // tk
    j0 = col0 // tn
    if mode == "nn":
        a_spec = pl.BlockSpec((tm, tk), lambda i, j, k: (i, k))
        b_spec = pl.BlockSpec((tk, tn), lambda i, j, k: (k, j + j0))
        dot = _dot
    elif mode == "nt":
        a_spec = pl.BlockSpec((tm, tk), lambda i, j, k: (i, k))
        b_spec = pl.BlockSpec((tn, tk), lambda i, j, k: (j, k))
        dot = _dot_nt
    else:
        a_spec = pl.BlockSpec((tk, tm), lambda i, j, k: (k, i))
        b_spec = pl.BlockSpec((tk, tn), lambda i, j, k: (k, j))
        dot = _dot_tn
    o_spec = pl.BlockSpec((tm, tn), lambda i, j, k: (i, j))
    has_add = add is not None

    n_ab = 2 if pair is None else 4

    def body(*refs):
        ab, (o_ref, acc_ref) = refs[:n_ab], refs[-2:]
        add_ref = refs[n_ab] if has_add else None
        k = pl.program_id(2)
        part = dot(ab[0][...], ab[1][...])
        if pair is not None:
            part = part + dot(ab[2][...], ab[3][...])

        def finish(total):
            if has_add:
                total = total + add_scale * add_ref[...]
            o_ref[...] = total.astype(out_dtype)

        if nk == 1:
            finish(part)
        else:
            @pl.when(k == 0)
            def _():
                acc_ref[...] = part

            @pl.when(jnp.logical_and(k > 0, k < nk - 1))
            def _():
                acc_ref[...] += part

            @pl.when(k == nk - 1)
            def _():
                finish(acc_ref[...] + part)

    in_specs = [a_spec, b_spec] * (n_ab // 2) + ([o_spec] if has_add else [])
    args = (a, b) + (tuple(pair) if pair is not None else ()) + ((add,) if has_add else ())
    est = ((n_ab // 2) * (_nbytes((tm, tk), a.dtype) + _nbytes((tk, tn), b.dtype)) + 2 * _nbytes((tm, tn), F32)
           + (_nbytes((tm, tn), F32) if has_add else 0)) + 2 * _nbytes((tm, tn), F32)
    return pl.pallas_call(
        body, name=name,
        grid=(M // tm, N // tn, nk),
        in_specs=in_specs, out_specs=o_spec,
        out_shape=jax.ShapeDtypeStruct((M, N), out_dtype),
        scratch_shapes=[pltpu.VMEM((tm, tn) if nk > 1 else (SUBLANES, LANES), F32)],
        compiler_params=_cparams(est, ("parallel", "parallel", "arbitrary")),
    )(*args)


def _shifted(xt, halo, first):
    halo = jnp.where(first, 0.0, halo)
    xc = jnp.concatenate([halo, xt], axis=0)
    return [xt] + [pltpu.roll(xc, s, 0)[HALO:] for s in range(1, CONV_K)]


def _conv_taps(shifted, w_ref):
    out = shifted[0] * w_ref[CONV_K - 1:CONV_K, :]
    for s in range(1, CONV_K):
        out = out + shifted[s] * w_ref[CONV_K - 1 - s:CONV_K - s, :]
    return out


def _gates(ba, arow, dtrow):
    lane = _iota(ba.shape, 1)
    beta = _sigmoid(ba)
    g = -jnp.exp(arow) * _softplus(ba + dtrow)
    return jnp.where(lane < N_HEADS, beta, jnp.where(lane < 2 * N_HEADS, g, 0.0))


def _l2n(x):
    return x * lax.rsqrt(jnp.sum(x * x, -1, keepdims=True) + RMS_EPS)


def _qkv_prep(proj, ba, convw, arow, dtrow, *, tm=256):
    S = proj.shape[0]
    tm = _tile(S, tm, SUBLANES)
    W3 = 3 * D_MODEL
    hb = tm // HALO

    def body(xt_ref, halo_ref, ba_ref, w_ref, a_ref, dt_ref, q_ref, k_ref, v_ref, gb_ref, c_ref):
        c = _conv_taps(_shifted(xt_ref[...], halo_ref[...], pl.program_id(0) == 0), w_ref)
        c_ref[...] = c.astype(BF)
        c = _silu(c)
        for h in range(N_HEADS):
            lo = h * D_HEAD
            q_ref[:, lo:lo + D_HEAD] = _l2n(c[:, lo:lo + D_HEAD])
            k_ref[:, lo:lo + D_HEAD] = _l2n(c[:, D_MODEL + lo:D_MODEL + lo + D_HEAD])
        v_ref[...] = c[:, 2 * D_MODEL:]
        gb_ref[...] = _gates(ba_ref[...], a_ref[...], dt_ref[...])

    row = lambda w, col=0: pl.BlockSpec((tm, w), lambda i: (i, col))
    full = lambda shape: pl.BlockSpec(shape, lambda i: (0,) * len(shape))
    est = 4 * _nbytes((tm, W3), F32)
    return pl.pallas_call(
        body, name="qkv_prep", grid=(S // tm,),
        in_specs=[row(W3), pl.BlockSpec((HALO, W3), lambda i: (jnp.maximum(i * hb - 1, 0), 0)), row(LANES),
                  full((CONV_K, W3)), full((1, LANES)), full((1, LANES))],
        out_specs=[row(D_MODEL), row(D_MODEL), row(D_MODEL), row(LANES), row(W3)],
        out_shape=[jax.ShapeDtypeStruct((S, D_MODEL), F32)] * 3 + [jax.ShapeDtypeStruct((S, LANES), F32),
                                                                   jax.ShapeDtypeStruct((S, W3), BF)],
        compiler_params=_cparams(est, ("arbitrary",)),
    )(proj, proj, ba, convw, arow, dtrow)


def _qkv_prep_bwd(proj, conv_out, ba, arow, dtrow, dq, dk, dv, dgb, *, tm=256):
    S = proj.shape[0]
    tm = _tile(S, tm, SUBLANES * 2)
    W3 = 3 * D_MODEL
    hb = tm // HALO

    def body(xt_ref, halo_ref, c_ref, ba_ref, a_ref, dt_ref, dq_ref, dk_ref, dv_ref, dgb_ref,
             dcb_ref, dba_ref, dw_ref, da_ref, ddt_ref, dc_ref):
        i = pl.program_id(0)

        @pl.when(i == 0)
        def _():
            dw_ref[...] = jnp.zeros_like(dw_ref)
            da_ref[...] = jnp.zeros_like(da_ref)
            ddt_ref[...] = jnp.zeros_like(ddt_ref)

        shifted = _shifted(xt_ref[...], halo_ref[...], i == 0)
        a, ds = _silu_and_grad(c_ref[...].astype(F32))
        for h in range(N_HEADS):
            for base, d_ref in ((0, dq_ref), (D_MODEL, dk_ref)):
                lo = base + h * D_HEAD
                _, vj = jax.vjp(_l2n, a[:, lo:lo + D_HEAD])
                (dx,) = vj(d_ref[:, h * D_HEAD:(h + 1) * D_HEAD])
                dc_ref[:, lo:lo + D_HEAD] = dx * ds[:, lo:lo + D_HEAD]
        dc_ref[:, 2 * D_MODEL:] = dv_ref[...] * ds[:, 2 * D_MODEL:]
        dc = dc_ref[...]
        dcb_ref[...] = dc.astype(BF)
        for s in range(CONV_K):
            kk = CONV_K - 1 - s
            dw_ref[kk:kk + 1, :] += jnp.sum(dc * shifted[s], axis=0, keepdims=True)
        _, vj = jax.vjp(_gates, ba_ref[...], a_ref[...], dt_ref[...])
        dba, da, ddt = vj(dgb_ref[...])
        dba_ref[...] = dba.astype(BF)
        da_ref[...] += _bcast_rows(da)
        ddt_ref[...] += _bcast_rows(ddt)

    row = lambda w, col=0: pl.BlockSpec((tm, w), lambda i: (i, col))
    full = lambda shape: pl.BlockSpec(shape, lambda i: (0,) * len(shape))
    est = 8 * _nbytes((tm, W3), F32)
    return pl.pallas_call(
        body, name="qkv_prep_bwd", grid=(S // tm,),
        in_specs=[row(W3), pl.BlockSpec((HALO, W3), lambda i: (jnp.maximum(i * hb - 1, 0), 0)), row(W3), row(LANES),
                  full((1, LANES)), full((1, LANES)),
                  row(D_MODEL), row(D_MODEL), row(D_MODEL), row(LANES)],
        out_specs=[row(W3), row(LANES), full((SUBLANES, W3)), full((SUBLANES, LANES)), full((SUBLANES, LANES))],
        out_shape=[jax.ShapeDtypeStruct((S, W3), BF), jax.ShapeDtypeStruct((S, LANES), BF),
                   jax.ShapeDtypeStruct((SUBLANES, W3), F32), jax.ShapeDtypeStruct((SUBLANES, LANES), F32),
                   jax.ShapeDtypeStruct((SUBLANES, LANES), F32)],
        scratch_shapes=[pltpu.VMEM((tm, W3), F32)],
        compiler_params=_cparams(est, ("arbitrary",)),
    )(proj, proj, conv_out, ba, arow, dtrow, dq, dk, dv, dgb)


def _conv_bwd(dc, convw, dproj, *, tm=256):
    S, W3 = dc.shape
    tm = _tile(S, tm, HALO_BF)
    hb = tm // HALO_BF
    nt = S // tm

    def body(dc_ref, nxt_ref, w_ref, dproj_ref, o_ref):
        last = pl.program_id(0) == nt - 1
        nxt = jnp.where(last, 0.0, nxt_ref[...].astype(F32))
        cur = dc_ref[...].astype(F32)
        xc = jnp.concatenate([cur, nxt], axis=0)
        out = cur * w_ref[CONV_K - 1:CONV_K, :]
        for s in range(1, CONV_K):
            out = out + pltpu.roll(xc, tm + HALO_BF - s, 0)[:tm] * w_ref[CONV_K - 1 - s:CONV_K - s, :]
        o_ref[...] = out.astype(BF)

    est = 5 * _nbytes((tm, W3), F32)
    return pl.pallas_call(
        body, name="conv_bwd", grid=(nt,),
        in_specs=[pl.BlockSpec((tm, W3), lambda i: (i, 0)),
                  pl.BlockSpec((HALO_BF, W3), lambda i: (jnp.minimum((i + 1) * hb, S // HALO_BF - 1), 0)),
                  pl.BlockSpec((CONV_K, W3), lambda i: (0, 0)), pl.BlockSpec(memory_space=pl.ANY)],
        out_specs=pl.BlockSpec((tm, W3), lambda i: (i, 0)),
        out_shape=jax.ShapeDtypeStruct(dproj.shape, BF),
        input_output_aliases={3: 0},
        compiler_params=_cparams(est, ("parallel",)),
    )(dc, dc, convw, dproj)


NEUMANN_BLOCK = 8


def _inv_unit_lower(A):
    C = A.shape[-1]
    row, col = _iota((C, C), 0), _iota((C, C), 1)
    eye = jnp.where(row == col, 1.0, 0.0).astype(F32)
    Ab = A.astype(BF)
    sh = jnp.int32(int(math.log2(NEUMANN_BLOCK)))
    B = jnp.where(lax.shift_right_logical(row, sh) == lax.shift_right_logical(col, sh), Ab, jnp.zeros_like(Ab))
    B2 = _mxu(B, B, "nn")
    B4 = _dot3(B2, B2, "nn")
    b2h, b2l = _split(B2)
    P = eye - B.astype(F32) + B2 - (_mxu(B, b2h, "nn") + _mxu(B, b2l, "nn"))
    T = P + _dot3(P, B4, "nn")
    b = NEUMANN_BLOCK
    while b < C:
        hi = ~(2 * b - 1)
        off = ((row & hi) == (col & hi)) & ((row & b) != 0) & ((col & b) == 0)
        Aoff = jnp.where(off, Ab, jnp.zeros_like(Ab))
        th, tl = _split(T)
        xh, xl = _split(_mxu(th, Aoff, "nn") + _mxu(tl, Aoff, "nn"))
        T = T - (_mxu(xh, th, "nn") + (_mxu(xh, tl, "nn") + _mxu(xl, th, "nn")))
        b *= 2
    return T


def _delta_common(q, k, g, beta):
    C = q.shape[-2]
    row, col = _iota((C, C), 0), _iota((C, C), 1)
    tril = row >= col
    qs = q * (D_HEAD ** -0.5)
    gcb = _dot01(jnp.where(tril, 1.0, 0.0), jnp.broadcast_to(g, g.shape[:-1] + (LANES,)))
    gc = gcb[..., :1]
    gr = jnp.swapaxes(gcb, -1, -2)
    Dm = jnp.exp(jnp.where(tril, gc - gr, -1e30))
    Dmt = jnp.exp(jnp.where(row <= col, gr - gc, -1e30))
    eg = jnp.exp(gc)
    gl = jnp.sum(jnp.where(_iota((C, 1), 0) == C - 1, gc, 0.0), axis=(-2, -1), keepdims=True)
    el = jnp.exp(gl)
    er = jnp.exp(gl - gc)
    kb = k * beta
    KK = _dot_nt(kb, k)
    QK = _dot_nt(qs, k)
    return dict(row=row, col=col, tril=tril, qs=qs, gc=gc, Dm=Dm, Dmt=Dmt, eg=eg, el=el, er=er, kb=kb, KK=KK, QK=QK)


def _delta_chunk_fwd(S0, q, k, v, g, beta):
    m = _delta_common(q, k, g, beta)
    T = _inv_unit_lower(jnp.where(m["row"] > m["col"], m["KK"] * m["Dm"], 0.0))
    u = _dotf(T, v * beta)
    w = _dotf(T, m["kb"] * m["eg"])
    vn = u - _dot(w, S0)
    o = _dot(m["qs"] * m["eg"], S0) + _dot(m["QK"] * m["Dm"], vn)
    S1 = S0 * m["el"] + _dot_tn(k * m["er"], vn)
    return o, S1, jnp.swapaxes(T, -1, -2), u, w


def _delta_chunk_bwd(S0, q, k, v, g, beta, Tt, u, w, do, dS1):
    m = _delta_common(q, k, g, beta)
    C = q.shape[-2]
    qs, Dm, Dmt, eg, el, er, kb, KK, QK = (m[n] for n in ("qs", "Dm", "Dmt", "eg", "el", "er", "kb", "KK", "QK"))
    strict = m["row"] > m["col"]
    total = lambda x: jnp.sum(x, axis=(-2, -1), keepdims=True)
    vn = u - _dot(w, S0)
    qg = qs * eg
    kr = k * er

    dvn = _dot(_dot_nt(k, qs) * Dmt, do) + _dot(kr, dS1)
    dS0 = dS1 * el + _dot_tn(qg, do) - _dot_tn(w, dvn)
    d_el = total(dS1 * S0)
    dqg = _dot_nt(do, S0)
    dqs = dqg * eg
    deg = jnp.sum(dqg * qs, -1, keepdims=True)
    dP = _dot_nt(do, vn)
    dPD = dP * Dm
    dqs = dqs + _dot(dPD, k)
    dk = _dot(_dot_nt(vn, do) * Dmt, qs)
    dD = dP * QK
    dkr = _dot_nt(vn, dS1)
    dk = dk + dkr * er
    der = jnp.sum(dkr * k, -1, keepdims=True)
    dw = -_dot_nt(dvn, S0)
    th, tl = _split(Tt)

    def tt_times(x):
        xh, xl = _split(x)
        return _mxu(th, xh, "nn") + (_mxu(th, xl, "nn") + _mxu(tl, xh, "nn"))

    dru = tt_times(dvn)
    drw = tt_times(dw)
    dA = -(_dotf_nt(dru, u) + _dotf_nt(drw, w))
    dAm = jnp.where(strict, dA, 0.0)
    dKK = dAm * Dm
    dkb = _dot(dKK, k)
    dk = dk + _dot_tn(dKK, kb)
    dD = dD + dAm * KK
    dv = dru * beta
    dbeta = jnp.sum(dru * v, -1, keepdims=True)
    dkb = dkb + drw * eg
    deg = deg + jnp.sum(drw * kb, -1, keepdims=True)
    dk = dk + dkb * beta
    dbeta = dbeta + jnp.sum(dkb * k, -1, keepdims=True)
    E = dD * Dm
    dgc = jnp.sum(E, -1, keepdims=True) - jnp.sum(jnp.swapaxes(E, -1, -2), -1, keepdims=True)
    dgc = dgc + deg * eg - der * er
    dgl = total(der * er) + d_el * el
    dgc = dgc + jnp.where(_iota((C, 1), 0) == C - 1, dgl, 0.0)
    triu = jnp.where(m["row"] <= m["col"], 1.0, 0.0)
    dg = _dot01(triu, jnp.broadcast_to(dgc, dgc.shape[:-1] + (LANES,)))[..., :1]
    dq = dqs * (D_HEAD ** -0.5)
    return dq, dk, dv, dg, dbeta, dS0


def _head_cols(gb, h):
    lane = _iota(gb.shape, 1)
    beta = jnp.sum(jnp.where(lane == h, gb, 0.0), -1, keepdims=True)
    g = jnp.sum(jnp.where(lane == N_HEADS + h, gb, 0.0), -1, keepdims=True)
    return g, beta


def _delta_fwd(q, k, v, gb):
    S = q.shape[0]
    C = DELTA_CHUNK
    N = S // C

    HB = DELTA_HEADS_PER_STEP

    def body(q_ref, k_ref, v_ref, gb_ref, o_ref, st_ref, t_ref, u_ref, w_ref, s_scr):
        n, hb = pl.program_id(0), pl.program_id(1)
        gb = gb_ref[...]

        @pl.when(n == 0)
        def _():
            for hh in range(HB):
                s_scr[hb * HB + hh] = jnp.zeros((D_HEAD, D_HEAD), F32)

        heads = [hb * HB + hh for hh in range(HB)]
        cols = [slice(hh * D_HEAD, (hh + 1) * D_HEAD) for hh in range(HB)]
        per_head = lambda ref: jnp.stack([ref[:, c] for c in cols])
        g, beta = (jnp.stack(t) for t in zip(*[_head_cols(gb, h) for h in heads]))
        S0 = jnp.stack([s_scr[h] for h in heads])
        o, S1, Tt, u, w = _delta_chunk_fwd(S0, per_head(q_ref), per_head(k_ref), per_head(v_ref), g, beta)
        for hh in range(HB):
            st_ref[hh, 0] = S0[hh]
            t_ref[hh, 0] = Tt[hh]
            o_ref[:, cols[hh]] = o[hh]
            u_ref[:, cols[hh]] = u[hh]
            w_ref[:, cols[hh]] = w[hh]
            s_scr[heads[hh]] = S1[hh]

    hd = pl.BlockSpec((C, HB * D_HEAD), lambda n, h: (n, h))
    mat = pl.BlockSpec((HB, 1, D_HEAD, D_HEAD), lambda n, h: (h, n, 0, 0))
    est = 40 * HB * _nbytes((C, D_HEAD), F32)
    seq = jax.ShapeDtypeStruct((S, N_HEADS * D_HEAD), F32)
    return pl.pallas_call(
        body, name="delta_fwd", grid=(N, N_HEADS // HB),
        in_specs=[hd, hd, hd, pl.BlockSpec((C, LANES), lambda n, h: (n, 0))],
        out_specs=[hd, mat, mat, hd, hd],
        out_shape=[seq, jax.ShapeDtypeStruct((N_HEADS, N, D_HEAD, D_HEAD), F32),
                   jax.ShapeDtypeStruct((N_HEADS, N, C, C), F32), seq, seq],
        scratch_shapes=[pltpu.VMEM((N_HEADS, D_HEAD, D_HEAD), F32)],
        compiler_params=_cparams(est, ("arbitrary", "arbitrary")),
    )(q, k, v, gb)


def _delta_bwd(q, k, v, gb, st, tinv, u, w, do):
    S = q.shape[0]
    C = DELTA_CHUNK
    N = S // C

    HB = DELTA_HEADS_PER_STEP

    def body(q_ref, k_ref, v_ref, gb_ref, st_ref, t_ref, u_ref, w_ref, do_ref, dq_ref, dk_ref, dv_ref, dgb_ref, ds_scr):
        n, hb = pl.program_id(0), pl.program_id(1)
        gb = gb_ref[...]
        lane = _iota((C, LANES), 1)
        dgb = jnp.zeros((C, LANES), F32)

        @pl.when(n == 0)
        def _():
            for hh in range(HB):
                ds_scr[hb * HB + hh] = jnp.zeros((D_HEAD, D_HEAD), F32)

        heads = [hb * HB + hh for hh in range(HB)]
        cols = [slice(hh * D_HEAD, (hh + 1) * D_HEAD) for hh in range(HB)]
        per_head = lambda ref: jnp.stack([ref[:, c] for c in cols])
        g, beta = (jnp.stack(t) for t in zip(*[_head_cols(gb, h) for h in heads]))
        dS1 = jnp.stack([ds_scr[h] for h in heads])
        dq, dk, dv, dg, dbeta, dS0 = _delta_chunk_bwd(
            st_ref[:, 0], per_head(q_ref), per_head(k_ref), per_head(v_ref), g, beta, t_ref[:, 0],
            per_head(u_ref), per_head(w_ref), per_head(do_ref), dS1)
        for hh, h in enumerate(heads):
            dq_ref[:, cols[hh]] = dq[hh]
            dk_ref[:, cols[hh]] = dk[hh]
            dv_ref[:, cols[hh]] = dv[hh]
            dgb = dgb + jnp.where(lane == h, dbeta[hh], 0.0) + jnp.where(lane == N_HEADS + h, dg[hh], 0.0)
            ds_scr[h] = dS0[hh]

        @pl.when(hb == 0)
        def _():
            dgb_ref[...] = dgb

        @pl.when(hb > 0)
        def _():
            dgb_ref[...] += dgb

    hd = pl.BlockSpec((C, HB * D_HEAD), lambda n, h: (N - 1 - n, h))
    mat = pl.BlockSpec((HB, 1, D_HEAD, D_HEAD), lambda n, h: (h, N - 1 - n, 0, 0))
    gbs = pl.BlockSpec((C, LANES), lambda n, h: (N - 1 - n, 0))
    est = 60 * HB * _nbytes((C, D_HEAD), F32)
    return pl.pallas_call(
        body, name="delta_bwd", grid=(N, N_HEADS // HB),
        in_specs=[hd, hd, hd, gbs, mat, mat, hd, hd, hd],
        out_specs=[hd, hd, hd, gbs],
        out_shape=[jax.ShapeDtypeStruct((S, N_HEADS * D_HEAD), F32)] * 3 + [jax.ShapeDtypeStruct((S, LANES), F32)],
        scratch_shapes=[pltpu.VMEM((N_HEADS, D_HEAD, D_HEAD), F32)],
        compiler_params=_cparams(est, ("arbitrary", "arbitrary")),
    )(q, k, v, gb, st, tinv, u, w, do)


def _ya_head(o, z, onw):
    return o * lax.rsqrt(jnp.mean(o * o, -1, keepdims=True) + RMS_EPS) * onw * _silu(z)


def _norm_cdf(x):
    return 0.5 * (1.0 + lax.erf(x * 0.7071067811865476))


def _norm_pdf(x):
    return jnp.exp(-0.5 * x * x) * 0.3989422804014327


def _chunk_causal(shape, di, dj):
    sh = jnp.int32(int(math.log2(SGU_CHUNK)))
    return lax.shift_right_logical(_iota(shape, di), sh) >= lax.shift_right_logical(_iota(shape, dj), sh)


def _ws_masked(ws):
    return jnp.where(_chunk_causal(ws.shape, 1, 2), ws, 0.0)


def _mix_prep(o, proj, onw, sg, sb, ws, bst, *, tm=256):
    S = o.shape[0]
    tm = _tile(S, tm, SGU_BLOCK)

    def body(o_ref, z_ref, u_ref, vg_ref, onw_ref, sg_ref, sb_ref, ws_ref, bst_ref, ya_ref, yb_ref, phi_ref):
        onw = onw_ref[...]
        for h in range(N_HEADS):
            sl = slice(h * D_HEAD, (h + 1) * D_HEAD)
            ya_ref[:, sl] = _ya_head(o_ref[:, sl], z_ref[:, sl].astype(F32), onw).astype(BF)
        u, vg = u_ref[...].astype(F32), vg_ref[...].astype(F32)
        phi_u, phi_v = _norm_cdf(u), _norm_cdf(vg)
        phi_ref[:, :D_MODEL] = phi_u.astype(BF)
        phi_ref[:, D_MODEL:] = phi_v.astype(BF)
        ua, vl = u * phi_u, _ln(vg * phi_v, sg_ref[...], sb_ref[...])
        wsm = _ws_masked(ws_ref[...])
        bst = bst_ref[...]
        for blk in range(tm // SGU_BLOCK):
            rs = slice(blk * SGU_BLOCK, (blk + 1) * SGU_BLOCK)
            for gi in range(SGU_GROUPS):
                cs = slice(gi * D_HEAD, (gi + 1) * D_HEAD)
                sp = _dot(wsm[gi], vl[rs, cs]) + bst[:, gi:gi + 1]
                yb_ref[rs, cs] = (ua[rs, cs] * sp).astype(BF)

    blk = lambda col: pl.BlockSpec((tm, D_MODEL), lambda i: (i, col))
    full = lambda shape: pl.BlockSpec(shape, lambda i: (0,) * len(shape))
    est = 10 * _nbytes((tm, D_MODEL), F32)
    return pl.pallas_call(
        body, name="mix_prep", grid=(S // tm,),
        in_specs=[blk(0), blk(0), blk(1), blk(2), full((1, D_HEAD)), full((1, D_MODEL)), full((1, D_MODEL)),
                  full((SGU_GROUPS, SGU_BLOCK, SGU_BLOCK)), full((SGU_BLOCK, LANES))],
        out_specs=[blk(0), blk(0), pl.BlockSpec((tm, 2 * D_MODEL), lambda i: (i, 0))],
        out_shape=[jax.ShapeDtypeStruct((S, D_MODEL), BF)] * 2 + [jax.ShapeDtypeStruct((S, 2 * D_MODEL), BF)],
        compiler_params=_cparams(est, ("parallel",)),
    )(o, proj, proj, proj, onw, sg, sb, ws, bst)


def _mix_prep_bwd(o, proj, phi, onw, sg, sb, ws, bst, dya, dyb, dproj, *, tm=256):
    S = o.shape[0]
    tm = _tile(S, tm, SGU_BLOCK)

    def body(o_ref, z_ref, u_ref, vg_ref, phi_ref, onw_ref, sg_ref, sb_ref, ws_ref, bst_ref, dya_ref, dyb_ref, dproj_in,
             do_ref, dzuv_ref, donw_ref, dsg_ref, dsb_ref, dws_ref, dbst_ref, dvl_scr, dua_scr):
        dz_ref, du_ref, dvg_ref = (dzuv_ref.at[:, k * D_MODEL:(k + 1) * D_MODEL] for k in range(3))
        @pl.when(pl.program_id(0) == 0)
        def _():
            for r in (donw_ref, dsg_ref, dsb_ref, dws_ref, dbst_ref):
                r[...] = jnp.zeros_like(r)

        onw = onw_ref[...]
        donw = jnp.zeros((1, D_HEAD), F32)
        for h in range(N_HEADS):
            sl = slice(h * D_HEAD, (h + 1) * D_HEAD)
            _, vj = jax.vjp(_ya_head, o_ref[:, sl], z_ref[:, sl].astype(F32), onw)
            do_h, dz_h, donw_h = vj(dya_ref[:, sl])
            do_ref[:, sl] = do_h.astype(BF)
            dz_ref[:, sl] = dz_h.astype(BF)
            donw = donw + donw_h
        donw_ref[...] += _bcast_rows(donw)

        u, vg = u_ref[...].astype(F32), vg_ref[...].astype(F32)
        phi_u, phi_v = phi_ref[:, :D_MODEL].astype(F32), phi_ref[:, D_MODEL:].astype(F32)
        ua = u * phi_u
        vl, vj = jax.vjp(_ln, vg * phi_v, sg_ref[...], sb_ref[...])
        wsm = _ws_masked(ws_ref[...])
        bst = bst_ref[...]
        lane = _iota((SGU_BLOCK, LANES), 1)
        dbst = jnp.zeros((SGU_BLOCK, LANES), F32)
        cmask = _chunk_causal((SGU_BLOCK, SGU_BLOCK), 0, 1)
        for gi in range(SGU_GROUPS):
            cs = slice(gi * D_HEAD, (gi + 1) * D_HEAD)
            wg = wsm[gi]
            wgt = jnp.transpose(wg)
            dwg = jnp.zeros((SGU_BLOCK, SGU_BLOCK), F32)
            for blk in range(tm // SGU_BLOCK):
                rs = slice(blk * SGU_BLOCK, (blk + 1) * SGU_BLOCK)
                sp = _dot(wg, vl[rs, cs]) + bst[:, gi:gi + 1]
                dyb = dyb_ref[rs, cs]
                dsp = dyb * ua[rs, cs]
                dua_scr[rs, cs] = dyb * sp
                dvl_scr[rs, cs] = _dot(wgt, dsp)
                dwg = dwg + _dot_nt(dsp, vl[rs, cs])
                dbst = dbst + jnp.where(lane == gi, jnp.sum(dsp, -1, keepdims=True), 0.0)
            dws_ref[gi] += jnp.where(cmask, dwg, 0.0)
        dbst_ref[...] += dbst
        dgv, dsg, dsb = vj(dvl_scr[...])
        du_ref[...] = (dua_scr[...] * (phi_u + u * _norm_pdf(u))).astype(BF)
        dvg_ref[...] = (dgv * (phi_v + vg * _norm_pdf(vg))).astype(BF)
        dsg_ref[...] += _bcast_rows(dsg)
        dsb_ref[...] += _bcast_rows(dsb)

    blk = lambda col: pl.BlockSpec((tm, D_MODEL), lambda i: (i, col))
    full = lambda shape: pl.BlockSpec(shape, lambda i: (0,) * len(shape))
    est = 16 * _nbytes((tm, D_MODEL), F32)
    outs = pl.pallas_call(
        body, name="mix_prep_bwd", grid=(S // tm,),
        in_specs=[blk(0), blk(0), blk(1), blk(2), pl.BlockSpec((tm, 2 * D_MODEL), lambda i: (i, 0)),
                  full((1, D_HEAD)), full((1, D_MODEL)), full((1, D_MODEL)),
                  full((SGU_GROUPS, SGU_BLOCK, SGU_BLOCK)), full((SGU_BLOCK, LANES)), blk(0), blk(0),
                  pl.BlockSpec(memory_space=pl.ANY)],
        out_specs=[blk(0), pl.BlockSpec((tm, 3 * D_MODEL), lambda i: (i, 1)),
                   full((SUBLANES, D_HEAD)), full((SUBLANES, D_MODEL)), full((SUBLANES, D_MODEL)),
                   full((SGU_GROUPS, SGU_BLOCK, SGU_BLOCK)), full((SGU_BLOCK, LANES))],
        out_shape=[jax.ShapeDtypeStruct((S, D_MODEL), BF), jax.ShapeDtypeStruct(dproj.shape, BF),
                   jax.ShapeDtypeStruct((SUBLANES, D_HEAD), F32), jax.ShapeDtypeStruct((SUBLANES, D_MODEL), F32),
                   jax.ShapeDtypeStruct((SUBLANES, D_MODEL), F32),
                   jax.ShapeDtypeStruct((SGU_GROUPS, SGU_BLOCK, SGU_BLOCK), F32),
                   jax.ShapeDtypeStruct((SGU_BLOCK, LANES), F32)],
        input_output_aliases={12: 1},
        scratch_shapes=[pltpu.VMEM((tm, D_MODEL), F32)] * 2,
        compiler_params=_cparams(est, ("arbitrary",)),
    )(o, proj, proj, proj, phi, onw, sg, sb, ws, bst, dya, dyb, dproj)
    return outs


def _mm_gate_merge(ya, yb, wpa, wpb, proj, *, tm=512):
    S = ya.shape[0]
    tm = _tile(S, tm, SUBLANES * 2)

    def body(ya_ref, yb_ref, wa_ref, wb_ref, ga_ref, gb_ref, pa_ref, pb_ref, m_ref):
        pa = _dot(ya_ref[...], wa_ref[...]).astype(BF)
        pb = _dot(yb_ref[...], wb_ref[...]).astype(BF)
        pa_ref[...] = pa
        pb_ref[...] = pb
        m_ref[...] = (_sigmoid(ga_ref[...].astype(F32)) * pa.astype(F32)
                      + _sigmoid(gb_ref[...].astype(F32)) * pb.astype(F32)).astype(BF)

    blk = lambda col: pl.BlockSpec((tm, D_MODEL), lambda i: (i, col))
    wsp = pl.BlockSpec((D_MODEL, D_MODEL), lambda i: (0, 0))
    return pl.pallas_call(
        body, name="mm_gate_merge", grid=(S // tm,),
        in_specs=[blk(0), blk(0), wsp, wsp, blk(3), blk(4)], out_specs=[blk(0)] * 3,
        out_shape=[jax.ShapeDtypeStruct((S, D_MODEL), BF)] * 3,
        compiler_params=_cparams(2 * _nbytes((D_MODEL, D_MODEL), BF) + 8 * _nbytes((tm, D_MODEL), F32), ("parallel",)),
    )(ya, yb, wpa, wpb, proj, proj)


def _mm_gate_merge_bwd(dmix, wo, pa, pb, proj, *, tm=512):
    S = pa.shape[0]
    tm = _tile(S, tm, SUBLANES * 2)

    def body(d_ref, w_ref, pa_ref, pb_ref, ga_ref, gb_ref, dpa_ref, dpb_ref, dg_ref):
        dm = _dot_nt(d_ref[...], w_ref[...])
        sa, sb = _sigmoid(ga_ref[...].astype(F32)), _sigmoid(gb_ref[...].astype(F32))
        dpa_ref[...] = (dm * sa).astype(BF)
        dpb_ref[...] = (dm * sb).astype(BF)
        dg_ref[:, :D_MODEL] = (dm * pa_ref[...].astype(F32) * sa * (1.0 - sa)).astype(BF)
        dg_ref[:, D_MODEL:] = (dm * pb_ref[...].astype(F32) * sb * (1.0 - sb)).astype(BF)

    blk = lambda col: pl.BlockSpec((tm, D_MODEL), lambda i: (i, col))
    est = _nbytes((D_MODEL, D_MODEL), BF) + 10 * _nbytes((tm, D_MODEL), F32)
    return pl.pallas_call(
        body, name="mm_gate_merge_bwd", grid=(S // tm,),
        in_specs=[blk(0), pl.BlockSpec((D_MODEL, D_MODEL), lambda i: (0, 0)), blk(0), blk(0), blk(3), blk(4)],
        out_specs=[blk(0), blk(0), pl.BlockSpec((tm, 2 * D_MODEL), lambda i: (i, 3))],
        out_shape=[jax.ShapeDtypeStruct((S, D_MODEL), BF)] * 2 + [jax.ShapeDtypeStruct((S, 8 * D_MODEL), BF)],
        compiler_params=_cparams(est, ("parallel",)),
    )(dmix, wo, pa, pb, proj, proj)


def _mm_swiglu(xb, wgt, wut, *, tm=1024, tn=768):
    S, K = xb.shape
    tm = _tile(S, tm, SUBLANES * 2)
    tn = _tile(FFN_K, tn, LANES)

    def body(x_ref, wg_ref, wu_ref, hg_ref, hu_ref, h_ref):
        x = x_ref[...]
        hg = _dot_nt(x, wg_ref[...]).astype(BF)
        hu = _dot_nt(x, wu_ref[...]).astype(BF)
        hg_ref[...] = hg
        hu_ref[...] = hu
        h_ref[...] = (_silu(hg.astype(F32)) * hu.astype(F32)).astype(BF)

    out = pl.BlockSpec((tm, tn), lambda i, j: (i, j))
    est = _nbytes((tm, K), BF) + 2 * _nbytes((K, tn), BF) + 6 * _nbytes((tm, tn), F32)
    return pl.pallas_call(
        body, name="mm_swiglu", grid=(S // tm, FFN_K // tn),
        in_specs=[pl.BlockSpec((tm, K), lambda i, j: (i, 0)), pl.BlockSpec((tn, K), lambda i, j: (j, 0)),
                  pl.BlockSpec((tn, K), lambda i, j: (j, 0))],
        out_specs=[out] * 3, out_shape=[jax.ShapeDtypeStruct((S, FFN_K), BF)] * 3,
        compiler_params=_cparams(est, ("parallel", "parallel")),
    )(xb, wgt, wut)


def _mm_swiglu_bwd(dffn, wd, hg, hu, *, tm=1024, tn=768):
    S, K = dffn.shape
    tm = _tile(S, tm, SUBLANES * 2)
    tn = _tile(FFN_K, tn, LANES)

    def body(d_ref, w_ref, hg_ref, hu_ref, dhg_ref, dhu_ref):
        dh = _dot_nt(d_ref[...], w_ref[...])
        act, dact = _silu_and_grad(hg_ref[...].astype(F32))
        dhg_ref[...] = (dh * hu_ref[...].astype(F32) * dact).astype(BF)
        dhu_ref[...] = (dh * act).astype(BF)

    out = pl.BlockSpec((tm, tn), lambda i, j: (i, j))
    est = _nbytes((tm, K), dffn.dtype) + _nbytes((tn, K), BF) + 8 * _nbytes((tm, tn), F32)
    return pl.pallas_call(
        body, name="mm_swiglu_bwd", grid=(S // tm, FFN_K // tn),
        in_specs=[pl.BlockSpec((tm, K), lambda i, j: (i, 0)), pl.BlockSpec((tn, K), lambda i, j: (j, 0)), out, out],
        out_specs=[out, out], out_shape=[jax.ShapeDtypeStruct((S, FFN_K), BF)] * 2,
        compiler_params=_cparams(est, ("parallel", "parallel")),
    )(dffn, wd, hg, hu)


def _mm_resid_ln(a, bmat, x, g, b, *, name, tm=512):
    S, K = a.shape
    tm = _tile(S, tm, SUBLANES * 2)

    def body(a_ref, w_ref, x_ref, g_ref, b_ref, pre_ref, y_ref, yb_ref):
        pre = ALPHA * x_ref[...] + _dot(a_ref[...], w_ref[...])
        y = _ln(pre, g_ref[...], b_ref[...])
        pre_ref[...] = pre
        y_ref[...] = y
        yb_ref[...] = y.astype(BF)

    blk = pl.BlockSpec((tm, D_MODEL), lambda i: (i, 0))
    vec = pl.BlockSpec((1, D_MODEL), lambda i: (0, 0))
    est = _nbytes((tm, K), BF) + _nbytes((K, D_MODEL), BF) + 8 * _nbytes((tm, D_MODEL), F32)
    return pl.pallas_call(
        body, name=name, grid=(S // tm,),
        in_specs=[pl.BlockSpec((tm, K), lambda i: (i, 0)), pl.BlockSpec((K, D_MODEL), lambda i: (0, 0)), blk, vec, vec],
        out_specs=[blk, blk, blk],
        out_shape=[jax.ShapeDtypeStruct((S, D_MODEL), F32)] * 2 + [jax.ShapeDtypeStruct((S, D_MODEL), BF)],
        compiler_params=_cparams(est, ("parallel",)),
    )(a, bmat, x, g, b)


def _ln_bwd(pre, g, b, dy, *, tm=512):
    S = pre.shape[0]
    tm = _tile(S, tm, SUBLANES)

    def body(p_ref, g_ref, b_ref, dy_ref, dp_ref, dg_ref, db_ref):
        @pl.when(pl.program_id(0) == 0)
        def _():
            dg_ref[...] = jnp.zeros_like(dg_ref)
            db_ref[...] = jnp.zeros_like(db_ref)

        _, vj = jax.vjp(_ln, p_ref[...], g_ref[...], b_ref[...])
        dp, dg, db = vj(dy_ref[...])
        dp_ref[...] = dp
        dg_ref[...] += _bcast_rows(dg)
        db_ref[...] += _bcast_rows(db)

    blk = pl.BlockSpec((tm, D_MODEL), lambda i: (i, 0))
    vec = pl.BlockSpec((1, D_MODEL), lambda i: (0, 0))
    acc = pl.BlockSpec((SUBLANES, D_MODEL), lambda i: (0, 0))
    return pl.pallas_call(
        body, name="ln_bwd", grid=(S // tm,),
        in_specs=[blk, vec, vec, blk], out_specs=[blk, acc, acc],
        out_shape=[jax.ShapeDtypeStruct((S, D_MODEL), F32)] + [jax.ShapeDtypeStruct((SUBLANES, D_MODEL), F32)] * 2,
        compiler_params=_cparams(10 * _nbytes((tm, D_MODEL), F32), ("arbitrary",)),
    )(pre, g, b, dy)


def _loss_ln_bwd(y, tgt, pre, g, b, *, tm=512):
    S = y.shape[0]
    tm = _tile(S, tm, SUBLANES)

    def body(y_ref, t_ref, p_ref, g_ref, b_ref, dp_ref, dg_ref, db_ref, l_ref):
        @pl.when(pl.program_id(0) == 0)
        def _():
            for r in (dg_ref, db_ref, l_ref):
                r[...] = jnp.zeros_like(r)

        e = y_ref[...] - t_ref[...]
        l_ref[...] += 0.5 * jnp.sum(jnp.mean(e * e, -1, keepdims=True), keepdims=True)
        _, vj = jax.vjp(_ln, p_ref[...], g_ref[...], b_ref[...])
        dp, dg, db = vj(e * (1.0 / D_MODEL))
        dp_ref[...] = dp
        dg_ref[...] += _bcast_rows(dg)
        db_ref[...] += _bcast_rows(db)

    blk = pl.BlockSpec((tm, D_MODEL), lambda i: (i, 0))
    vec = pl.BlockSpec((1, D_MODEL), lambda i: (0, 0))
    acc = pl.BlockSpec((SUBLANES, D_MODEL), lambda i: (0, 0))
    return pl.pallas_call(
        body, name="loss_ln_bwd", grid=(S // tm,),
        in_specs=[blk, blk, blk, vec, vec], out_specs=[blk, acc, acc, pl.BlockSpec((SUBLANES, LANES), lambda i: (0, 0))],
        out_shape=[jax.ShapeDtypeStruct((S, D_MODEL), F32)] + [jax.ShapeDtypeStruct((SUBLANES, D_MODEL), F32)] * 2
                  + [jax.ShapeDtypeStruct((SUBLANES, LANES), F32)],
        compiler_params=_cparams(12 * _nbytes((tm, D_MODEL), F32), ("arbitrary",)),
    )(y, tgt, pre, g, b)


def _layer_fwd(x, xb, w, late):
    pq = _mm(xb, w["win"], mode="nn", name="mm_in_qkv", tm=1024, tn=1024, cols=(0, 3 * D_MODEL))
    proj = _mm(xb, w["win"], mode="nn", name="mm_in_rest", tm=1024, tn=1024, cols=(3 * D_MODEL, 5 * D_MODEL), out_dtype=BF)
    ba = _mm(xb, w["wba"], mode="nn", name="mm_in_ba", tm=1024, tn=LANES)
    qn, kn, vv, gb, conv_out = _qkv_prep(pq, ba, w["convw"], w["arow"], w["dtrow"])
    o, st, tinv, wy_u, wy_w = _delta_fwd(qn, kn, vv, gb)
    ya, yb, phi = _mix_prep(o, proj, w["onw"], w["sg"], w["sb"], w["ws"], w["bst"])
    w = {**w, **late(ya)}
    pa, pb, m = _mm_gate_merge(ya, yb, w["wpa"], w["wpb"], proj)
    pre1, x1, x1b = _mm_resid_ln(m, w["wo"], x, w["ln1g"], w["ln1b"], name="mm_out_ln")
    hg, hu, h = _mm_swiglu(x1b, w["wgt"], w["wut"])
    pre2, x2, x2b = _mm_resid_ln(h, w["wd"], x1, w["ln2g"], w["ln2b"], name="mm_down_ln")
    saved = dict(xb=xb, pq=pq, conv_out=conv_out, proj=proj, phi=phi, ba=ba, qn=qn, kn=kn, vv=vv, gb=gb, o=o, st=st, tinv=tinv, wy_u=wy_u, wy_w=wy_w,
                 ya=ya, yb=yb,
                 pa=pa, pb=pb, m=m, pre1=pre1, x1b=x1b, hg=hg, hu=hu, h=h, pre2=pre2)
    return x2, x2b, saved, w


def _layer_bwd(dpre2, ln2_grads, w, s, on_part=None):
    g = {}
    started = lambda part: on_part(part, g) if on_part is not None else None
    after = lambda v, token: v if token is None else v + token.astype(v.dtype)
    g["ln2g"], g["ln2b"] = ln2_grads
    dhg, dhu = _mm_swiglu_bwd(dpre2, w["wd"], s["hg"], s["hu"])
    g["wd"] = _mm(s["h"], dpre2, mode="tn", name="mm_tn_down", tm=1536, tk=1024, out_dtype=BF)
    dx1 = _mm(dhg, w["wgt"], mode="nn", name="mm_nn_gu", pair=(dhu, w["wut"]), add=dpre2, add_scale=ALPHA, tm=1024, tk=1536)
    g["wgt"] = _mm(dhg, s["x1b"], mode="tn", name="mm_tn_gu", tm=1536, tn=1024, tk=2048, out_dtype=BF)
    g["wut"] = _mm(dhu, s["x1b"], mode="tn", name="mm_tn_gu", tm=1536, tn=1024, tk=2048, out_dtype=BF)
    dpre1, g["ln1g"], g["ln1b"] = _ln_bwd(s["pre1"], w["ln1g"], w["ln1b"], dx1)
    g["wo"] = _mm(s["m"], dpre1, mode="tn", name="mm_tn_sq", tm=1024, tk=1024, out_dtype=BF)
    dpa, dpb, dproj = _mm_gate_merge_bwd(dpre1, w["wo"], s["pa"], s["pb"], s["proj"])
    dya = _mm(dpa, w["wpa"], mode="nt", name="mm_nt_sq")
    g["wpa"] = _mm(s["ya"], dpa, mode="tn", name="mm_tn_sq", tm=1024, tk=1024, out_dtype=BF)
    dyb = _mm(dpb, w["wpb"], mode="nt", name="mm_nt_sq")
    g["wpb"] = _mm(s["yb"], dpb, mode="tn", name="mm_tn_sq", tm=1024, tk=1024, out_dtype=BF)
    do, dproj, g["onw"], g["sg"], g["sb"], g["ws"], g["bst"] = _mix_prep_bwd(
        s["o"], s["proj"], s["phi"], after(w["onw"], started("late")), w["sg"], w["sb"], w["ws"], w["bst"], dya, dyb, dproj)
    dqn, dkn, dvv, dgb = _delta_bwd(s["qn"], s["kn"], s["vv"], s["gb"], s["st"], s["tinv"], s["wy_u"], s["wy_w"], do)
    dc, dba, g["convw"], g["arow"], g["dtrow"] = _qkv_prep_bwd(
        s["pq"], s["conv_out"], s["ba"], w["arow"], w["dtrow"], dqn, dkn, dvv, dgb)
    dproj = _conv_bwd(dc, w["convw"], dproj)
    g["win"] = _mm(s["xb"], dproj, mode="tn", name="mm_tn_in", tm=1024, tn=1024, tk=2048, out_dtype=BF)
    g["wba"] = _mm(s["xb"], dba, mode="tn", name="mm_tn_ba", tm=1024, tn=LANES, tk=1024, out_dtype=BF)
    dx = _mm(dba, after(w["wba"], started("early")), mode="nt", name="mm_nt_ba", add=dpre1, add_scale=ALPHA, tm=1024)
    dx = _mm(dproj, w["win"], mode="nt", name="mm_nt_in", add=dx, add_scale=1.0, tm=1024, tk=2048)
    return dx, g


def _local_step(x, xb, tgt, layers, on_grads=None):
    saved, weights = [], []
    for layer in layers:
        x, xb, s, w = _layer_fwd(x, xb, *layer(x))
        saved.append(s)
        weights.append(w)
    last = len(layers) - 1
    dpre2, dg, db, lacc = _loss_ln_bwd(x, tgt, saved[last]["pre2"], weights[last]["ln2g"], weights[last]["ln2b"])
    grads = [None] * len(layers)
    for l in reversed(range(len(layers))):
        on_part = functools.partial(on_grads, l) if on_grads is not None else None
        dx, grads[l] = _layer_bwd(dpre2, (dg, db), weights[l], saved[l], on_part)
        if l > 0:
            dpre2, dg, db = _ln_bwd(saved[l - 1]["pre2"], weights[l - 1]["ln2g"], weights[l - 1]["ln2b"], dx)
    return lacc[0, 0], dx, grads


_QKVZ = 4 * D_MODEL
_BA = 2 * N_HEADS


WEIGHT_NAMES = ("w_in", "conv_w", "a_log", "dt_bias", "o_norm_w", "sgu_ln_g", "sgu_ln_b", "w_s", "b_s", "w_pa", "w_pb",
                "w_o", "ln1_g", "ln1_b", "w_ffn_gate", "w_ffn_up", "w_ffn_down", "ln2_g", "ln2_b")
WIRE = ("w_in", "w_ffn_gate", "w_ffn_up", "w_ffn_down", "w_pa", "w_pb", "w_o", "conv_w")
SMALL = (("a_log", N_HEADS), ("dt_bias", N_HEADS), ("o_norm_w", D_HEAD), ("sgu_ln_g", D_MODEL), ("sgu_ln_b", D_MODEL),
         ("w_s", SGU_GROUPS * SGU_BLOCK * SGU_BLOCK), ("b_s", SGU_GROUPS * SGU_BLOCK),
         ("ln1_g", D_MODEL), ("ln1_b", D_MODEL), ("ln2_g", D_MODEL), ("ln2_b", D_MODEL))
SMALL_ROWS = -(-sum(n for _, n in SMALL) // (LANES * SUBLANES)) * SUBLANES
N_MAIN_TILES = (N_IN - _BA) // D_MODEL
ADAM_TILES = dict(w_in=(128, "adamw_in"), w_ffn_gate=(32, "adamw_ffn_rows"), w_ffn_up=(32, "adamw_ffn_rows"),
                  w_ffn_down=(32, "adamw_ffn_rows"), w_pa=(128, "adamw_sq"), w_pb=(128, "adamw_sq"), w_o=(128, "adamw_sq"),
                  conv_w=(CONV_K, "adamw_conv"))


def _pad_to(a, axis, size):
    pads = [(0, 0)] * a.ndim
    pads[axis] = (0, size - a.shape[axis])
    return jnp.pad(a, pads)


def _t(a):
    return jnp.swapaxes(a, 1, 2)


def _wire_blocks(p):
    return dict(
        w_in=_pad_to(p["w_in"].astype(BF), 2, IN_PAD),
        w_ffn_gate=_pad_to(_t(p["w_ffn_gate"]).astype(BF), 1, FFN_PAD), w_ffn_up=_pad_to(_t(p["w_ffn_up"]).astype(BF), 1, FFN_PAD),
        w_ffn_down=_pad_to(p["w_ffn_down"].astype(BF), 1, FFN_PAD),
        w_pa=p["w_pa"].astype(BF), w_pb=p["w_pb"].astype(BF), w_o=p["w_o"].astype(BF),
        conv_w=_pad_to(p["conv_w"], 1, SUBLANES),
    )


def _by_columns(blocks):
    n, r, c = blocks.shape
    return jnp.transpose(blocks, (1, 0, 2)).reshape(r, n * c)


def _to_slots(full, c):
    r = full.shape[0]
    return jnp.transpose(full.reshape(r, N_DEV, c), (1, 0, 2))


def _lane_row(v, at):
    return jnp.pad(v[None], ((0, 0), (at, LANES - at - v.shape[0])))


TRANSPOSED = ("w_ffn_gate", "w_ffn_up")
EARLY = ("w_in", "conv_w")
LATE = ("w_pa", "w_pb", "w_o", "w_ffn_gate", "w_ffn_up", "w_ffn_down")


def _early_weights(stacks, p, l):
    return dict(
        win=_perm_in(stacks["w_in"], D_MODEL, N_MAIN_TILES), wba=_perm_in(stacks["w_in"], LANES, 1),
        convw=_by_columns(stacks["conv_w"][:, :CONV_K]),
        arow=_lane_row(p["a_log"][l], N_HEADS), dtrow=_lane_row(p["dt_bias"][l], N_HEADS),
        onw=p["o_norm_w"][l][None], sg=p["sgu_ln_g"][l][None], sb=p["sgu_ln_b"][l][None],
        ws=p["w_s"][l], bst=_pad_to(p["b_s"][l].T, 1, LANES),
        ln1g=p["ln1_g"][l][None], ln1b=p["ln1_b"][l][None], ln2g=p["ln2_g"][l][None], ln2b=p["ln2_b"][l][None],
    )


def _late_weights(stacks):
    return dict(
        wpa=stacks["w_pa"].reshape(D_MODEL, D_MODEL), wpb=stacks["w_pb"].reshape(D_MODEL, D_MODEL),
        wo=stacks["w_o"].reshape(D_MODEL, D_MODEL),
        wgt=stacks["w_ffn_gate"].reshape(FFN_K, D_MODEL), wut=stacks["w_ffn_up"].reshape(FFN_K, D_MODEL),
        wd=stacks["w_ffn_down"].reshape(FFN_K, D_MODEL),
    )


def _small_pack(parts):
    flat = jnp.concatenate([parts[n].reshape(-1) for n, _ in SMALL])
    return _pad_to(flat, 0, SMALL_ROWS * LANES).reshape(SMALL_ROWS, LANES)


def _small_unpack(rows, like):
    flat, out, off = rows.reshape(-1), {}, 0
    for n, size in SMALL:
        out[n] = flat[off:off + size].reshape(like[n].shape[1:])
        off += size
    return out


def _late_slots(g):
    slots = dict(
        w_ffn_gate=g["wgt"].reshape(N_DEV, FFN_PAD, D_MODEL), w_ffn_up=g["wut"].reshape(N_DEV, FFN_PAD, D_MODEL),
        w_ffn_down=g["wd"].reshape(N_DEV, FFN_PAD, D_MODEL),
        w_pa=g["wpa"].reshape(N_DEV, D_MODEL // N_DEV, D_MODEL), w_pb=g["wpb"].reshape(N_DEV, D_MODEL // N_DEV, D_MODEL),
        w_o=g["wo"].reshape(N_DEV, D_MODEL // N_DEV, D_MODEL),
    )
    return [slots[n] for n in LATE]


def _early_slots(g):
    slots = [_perm_out(g["win"], g["wba"]), _pad_to(_to_slots(g["convw"][:CONV_K], 3 * D_MODEL // N_DEV), 1, SUBLANES)]
    small = _small_pack(dict(
        a_log=g["arow"][0, N_HEADS:2 * N_HEADS], dt_bias=g["dtrow"][0, N_HEADS:2 * N_HEADS], o_norm_w=g["onw"][0],
        sgu_ln_g=g["sg"][0], sgu_ln_b=g["sb"][0], w_s=g["ws"], b_s=g["bst"][:, :SGU_GROUPS].T,
        ln1_g=g["ln1g"][0], ln1_b=g["ln1b"][0], ln2_g=g["ln2g"][0], ln2_b=g["ln2b"][0]))
    return slots, small


def _in_tile_start(j, tile_w):
    if tile_w == LANES:
        return jnp.int32(_QKVZ)
    return j * D_MODEL + jnp.where(j >= _QKVZ // D_MODEL, _BA, 0)


def _select(rows_iota, cols_iota, dev, start, valid):
    hit = (rows_iota + (dev * IN_BLOCK - start) == cols_iota) & (rows_iota < IN_BLOCK) & (cols_iota < valid)
    return jnp.where(hit, 1.0, 0.0).astype(BF)


def _perm_in(stack, tile_w, n_tiles):
    valid = _BA if tile_w == LANES else tile_w

    def first_dev(j):
        return lax.div(_in_tile_start(j, tile_w), jnp.int32(IN_BLOCK))

    def body(w_ref, o_ref, acc_ref):
        j, k = pl.program_id(0), pl.program_id(1)
        sel = _select(_iota((IN_PAD, tile_w), 0), _iota((IN_PAD, tile_w), 1), first_dev(j) + k,
                      _in_tile_start(j, tile_w), valid)
        part = jnp.dot(w_ref[0], sel, preferred_element_type=F32)

        @pl.when(k == 0)
        def _():
            acc_ref[...] = part

        @pl.when(k == 1)
        def _():
            o_ref[...] = (acc_ref[...] + part).astype(BF)

    est = _nbytes((D_MODEL, IN_PAD), BF) + 3 * _nbytes((D_MODEL, tile_w), F32) + 2 * _nbytes((IN_PAD, tile_w), F32)
    return pl.pallas_call(
        body, name="perm_in" if tile_w != LANES else "perm_in_ba", grid=(n_tiles, 2),
        in_specs=[pl.BlockSpec((1, D_MODEL, IN_PAD), lambda j, k: (jnp.minimum(first_dev(j) + k, N_DEV - 1), 0, 0))],
        out_specs=pl.BlockSpec((D_MODEL, tile_w), lambda j, k: (0, j)),
        out_shape=jax.ShapeDtypeStruct((D_MODEL, n_tiles * tile_w), BF),
        scratch_shapes=[pltpu.VMEM((D_MODEL, tile_w), F32)],
        compiler_params=_cparams(est, ("parallel", "arbitrary")),
    )(stack)


def _perm_out(dmain, dba):
    def tile(d, s):
        c0 = d * IN_BLOCK
        first = lax.div(c0 - jnp.where(c0 < _QKVZ, 0, jnp.minimum(c0 - _QKVZ, _BA)), jnp.int32(D_MODEL))
        return jnp.minimum(first + jnp.minimum(s, 1), N_MAIN_TILES - 1)

    def body(dm_ref, db_ref, o_ref, acc_ref):
        d, s = pl.program_id(0), pl.program_id(1)

        @pl.when(s == 0)
        def _():
            acc_ref[...] = jnp.zeros_like(acc_ref)

        start = _in_tile_start(tile(d, s), D_MODEL)
        overlaps = (start < (d + 1) * IN_BLOCK) & (d * IN_BLOCK < start + D_MODEL)

        @pl.when((s < 2) & overlaps)
        def _():
            sel = _select(_iota((D_MODEL, IN_PAD), 1), _iota((D_MODEL, IN_PAD), 0), d, start, D_MODEL)
            acc_ref[...] += jnp.dot(dm_ref[...], sel, preferred_element_type=F32)

        @pl.when(s == 2)
        def _():
            sel = _select(_iota((LANES, IN_PAD), 1), _iota((LANES, IN_PAD), 0), d, jnp.int32(_QKVZ), _BA)
            o_ref[0] = (acc_ref[...] + jnp.dot(db_ref[...], sel, preferred_element_type=F32)).astype(BF)

    est = 2 * _nbytes((D_MODEL, D_MODEL), BF) + 4 * _nbytes((D_MODEL, IN_PAD), F32)
    return pl.pallas_call(
        body, name="perm_out", grid=(N_DEV, 3),
        in_specs=[pl.BlockSpec((D_MODEL, D_MODEL), lambda d, s: (0, tile(d, s))),
                  pl.BlockSpec((D_MODEL, LANES), lambda d, s: (0, 0))],
        out_specs=pl.BlockSpec((1, D_MODEL, IN_PAD), lambda d, t: (d, 0, 0)),
        out_shape=jax.ShapeDtypeStruct((N_DEV, D_MODEL, IN_PAD), BF),
        scratch_shapes=[pltpu.VMEM((D_MODEL, IN_PAD), F32)],
        compiler_params=_cparams(est, ("parallel", "arbitrary")),
    )(dmain, dba)


def _mesh_place():
    x, y, c = (lax.axis_index(a) for a in MESH_AXES)
    return x, y, c


def _slot(x, y, c):
    return 4 * x + 2 * y + c


def _peer(place, j):
    x, y, c = place
    return (1 - x if j & 4 else x, 1 - y if j & 2 else y, 1 - c if j & 1 else c)


_HBM = pl.BlockSpec(memory_space=pltpu.HBM)
_SEM = pl.BlockSpec(memory_space=pltpu.SEMAPHORE)
_EFFECT = pltpu.SideEffectType.DATAFLOW_SIDE_EFFECTING


def _remote_copy(src_ref, land_ref, slot, per_slot, pslot, sems, u, j, peer):
    return pltpu.make_async_remote_copy(
        src_ref=src_ref.at[pslot] if per_slot else src_ref, dst_ref=land_ref.at[slot],
        send_sem=sems[0].at[u * (N_DEV - 1) + j - 1], recv_sem=sems[1].at[u * (N_DEV - 1) + j - 1],
        device_id=peer, device_id_type=pl.DeviceIdType.MESH)


def _own_copy(src_ref, land_ref, me, per_slot, sems, u):
    return pltpu.make_async_copy(src_ref.at[me] if per_slot else src_ref, land_ref.at[me], sems[2].at[u])


def _exchange_start(name, srcs, per_slot):
    n = len(srcs)
    lands = [jax.ShapeDtypeStruct(s.shape if p else (N_DEV,) + s.shape, s.dtype) for s, p in zip(srcs, per_slot)]

    def body(*refs):
        src_refs, sems, land_refs, token = refs[:n], refs[n:n + 3], refs[2 * n + 3:3 * n + 3], refs[-1]
        place = _mesh_place()
        me = _slot(*place)
        for u in range(n):
            _own_copy(src_refs[u], land_refs[u], me, per_slot[u], sems, u).start()
            for j in range(1, N_DEV):
                peer = _peer(place, j)
                _remote_copy(src_refs[u], land_refs[u], me, per_slot[u], _slot(*peer), sems, u, j, peer).start()
        token[...] = jnp.zeros_like(token)

    hbm = lambda a: pltpu.HBM(a.shape, a.dtype)
    sem = pltpu.SemaphoreType.DMA((n * (N_DEV - 1),))
    outs = pl.pallas_call(
        body, name=name,
        out_shape=(sem, sem, pltpu.SemaphoreType.DMA((n,)), *[hbm(a) for a in srcs], *[hbm(a) for a in lands],
                   jax.ShapeDtypeStruct((SUBLANES, LANES), F32)),
        in_specs=[_HBM] * n, out_specs=(_SEM, _SEM, _SEM, *[_HBM] * (2 * n), pl.BlockSpec(memory_space=pltpu.VMEM)),
        input_output_aliases={i: 3 + i for i in range(n)},
        compiler_params=pltpu.CompilerParams(has_side_effects=_EFFECT),
    )(*[pltpu.with_memory_space_constraint(a, pltpu.HBM) for a in srcs])
    return tuple(outs[:3]), list(outs[3:3 + n]), list(outs[3 + n:3 + 2 * n]), outs[-1]


def _exchange_wait(name, sems, srcs, lands, units, per_slot, after):
    m = len(units)
    after = list(after) if isinstance(after, (list, tuple)) else [after]

    def body(*refs):
        src_refs, land_refs, sem_refs = refs[:m], refs[m:2 * m], refs[2 * m:2 * m + 3]
        place = _mesh_place()
        me = _slot(*place)
        for i, u in enumerate(units):
            _own_copy(src_refs[i], land_refs[i], me, per_slot[u], sem_refs, u).wait()
            for j in range(1, N_DEV):
                peer = _peer(place, j)
                pslot = _slot(*peer)
                cp = _remote_copy(src_refs[i], land_refs[i], pslot, per_slot[u], pslot, sem_refs, u, j, peer)
                cp.wait_send()
                cp.wait_recv()

    hbm = lambda a: pltpu.HBM(a.shape, a.dtype)
    outs = pl.pallas_call(
        body, name=name, out_shape=tuple(hbm(a) for a in list(srcs) + list(lands)),
        in_specs=[_HBM] * (2 * m) + [_SEM] * 3 + [pl.BlockSpec(memory_space=pl.ANY)] * len(after),
        out_specs=tuple([_HBM] * (2 * m)),
        input_output_aliases={i: i for i in range(2 * m)},
        compiler_params=pltpu.CompilerParams(has_side_effects=_EFFECT),
    )(*srcs, *lands, *sems, *after)
    return list(outs[m:])


def _adam_update(g, w, m, v):
    m = ADAM_B1 * m + (1.0 - ADAM_B1) * g
    v = ADAM_B2 * v + (1.0 - ADAM_B2) * jnp.square(g)
    m_hat = m / (1.0 - ADAM_B1 ** ADAM_STEP)
    v_hat = v / (1.0 - ADAM_B2 ** ADAM_STEP)
    return -ADAM_LR * (m_hat / (jnp.sqrt(v_hat) + ADAM_EPS) + ADAM_WD * w), m, v


def _adamw(recvs, w, m, v, *, tr, name):
    L, R, C = w.shape
    rp = max(tr, SUBLANES * (4 // jnp.dtype(recvs[0].dtype).itemsize))
    Cp = recvs[0].shape[2]

    def body(*refs):
        r_refs, (w_ref, m_ref, v_ref, g_ref, d_ref, nm_ref, nv_ref) = refs[:L], refs[L:]
        for l in range(L):
            @pl.when(pl.program_id(0) == l)
            def _(r_ref=r_refs[l]):
                g = r_ref[0, :tr, :C].astype(F32)
                for s in range(1, N_DEV):
                    g = g + r_ref[s, :tr, :C].astype(F32)
                d, nm, nv = _adam_update(g, w_ref[0], m_ref[0], v_ref[0])
                g_ref[0], d_ref[0], nm_ref[0], nv_ref[0] = g, d, nm, nv

    blk = pl.BlockSpec((1, tr, C), lambda l, i: (l, i, 0))
    r_specs = [pl.BlockSpec((N_DEV, rp, Cp), lambda l, i, k=k: (0, jnp.where(l == k, i, 0), 0)) for k in range(L)]
    est = 2 * _nbytes((N_DEV, rp, Cp), recvs[0].dtype) + 8 * _nbytes((tr, Cp), F32)
    return pl.pallas_call(
        body, name=name, grid=(L, R // tr),
        in_specs=r_specs + [blk] * 3, out_specs=[blk] * 4,
        out_shape=[jax.ShapeDtypeStruct((L, R, C), F32)] * 4,
        compiler_params=_cparams(est, ("arbitrary", "arbitrary")),
    )(*recvs, w, m, v)


def _adamw_small(recv, w, m, v):
    def body(r_ref, w_ref, m_ref, v_ref, g_ref, d_ref, nm_ref, nv_ref):
        g = r_ref[0]
        for s in range(1, N_DEV):
            g = g + r_ref[s]
        g_ref[...] = g
        d_ref[...], nm_ref[...], nv_ref[...] = _adam_update(g, w_ref[...], m_ref[...], v_ref[...])

    vm = pl.BlockSpec(memory_space=pltpu.VMEM)
    return pl.pallas_call(
        body, name="adamw_small", in_specs=[vm] * 4, out_specs=[vm] * 4,
        out_shape=[jax.ShapeDtypeStruct((SMALL_ROWS, LANES), F32)] * 4,
        compiler_params=_cparams(20 * _nbytes((SMALL_ROWS, LANES), F32)),
    )(recv, w, m, v)


def kernel(x, w_in, conv_w, a_log, dt_bias, o_norm_w, sgu_ln_g, sgu_ln_b, w_s, b_s, w_pa, w_pb, w_o, ln1_g, ln1_b, w_ffn_gate, w_ffn_up, w_ffn_down, ln2_g, ln2_b, loss_target, m_w_in, m_conv_w, m_a_log, m_dt_bias, m_o_norm_w, m_sgu_ln_g, m_sgu_ln_b, m_w_s, m_b_s, m_w_pa, m_w_pb, m_w_o, m_ln1_g, m_ln1_b, m_w_ffn_gate, m_w_ffn_up, m_w_ffn_down, m_ln2_g, m_ln2_b, v_w_in, v_conv_w, v_a_log, v_dt_bias, v_o_norm_w, v_sgu_ln_g, v_sgu_ln_b, v_w_s, v_b_s, v_w_pa, v_w_pb, v_w_o, v_ln1_g, v_ln1_b, v_w_ffn_gate, v_w_ffn_up, v_w_ffn_down, v_ln2_g, v_ln2_b):
    given = dict(locals())
    P = {n: given[n] for n in WEIGHT_NAMES}
    M = {n: given["m_" + n] for n in WEIGHT_NAMES}
    V = {n: given["v_" + n] for n in WEIGHT_NAMES}

    wire = _wire_blocks(P)
    units = [(n, l) for l in range(DEPTH) for n in EARLY + LATE]
    whole = [False] * len(units)
    g_sems, g_srcs, g_lands, g_token = _exchange_start("gather_start", [wire[n][l] for n, l in units], whole)

    one = 1.0 + g_token[0, 0]
    xb = (x[0] * one).astype(BF)
    small_in = [[_small_pack({n: T[n][l] * one for n, _ in SMALL}) for T in (P, M, V)] for l in range(DEPTH)]
    adam_in = {n: (P[n], M[n], V[n]) for n in WIRE}
    adam_in["w_in"], _ = lax.optimization_barrier((adam_in["w_in"], g_token))
    prepared = [xb, *[a for packs in small_in for a in packs], *adam_in["w_in"]]

    def gathered(name, names, l, after):
        idx = [units.index((n, l)) for n in names]
        got = _exchange_wait(name, g_sems, [g_srcs[i] for i in idx], [g_lands[i] for i in idx], idx, whole, after)
        return dict(zip(names, got))

    def layer(l):
        def weights(x_in):
            after = prepared if l == 0 else x_in
            early = _early_weights(gathered(f"gather_wait_early{l}", EARLY, l, after), P, l)
            return early, lambda ya: _late_weights(gathered(f"gather_wait_late{l}", LATE, l, ya))
        return weights

    pending = {}

    def on_grads(l, part, g):
        if part == "late":
            srcs, names = _late_slots(g), LATE
            per_slot = [True] * len(srcs)
        else:
            slots, small = _early_slots(g)
            srcs, names = slots + [small], EARLY + ("small",)
            per_slot = [True] * len(slots) + [False]
        sems, s_thru, l_thru, token = _exchange_start(f"exchange_start_{part}{l}", srcs, per_slot)
        pending[l, part] = (names, sems, s_thru, l_thru, per_slot)
        return token[0, 0]

    loss_local, dx, _ = _local_step(x[0], xb, loss_target[0], [layer(l) for l in range(DEPTH)], on_grads)
    loss = lax.psum(loss_local, MESH_AXES)

    recv = [{} for _ in range(DEPTH)]

    def received(l, part, after):
        names, sems, s_thru, l_thru, per_slot = pending[l, part]
        got = _exchange_wait(f"exchange_wait_{part}{l}", sems, s_thru, l_thru, list(range(len(s_thru))), per_slot, after)
        recv[l].update(zip(names, got))

    out = {}

    def adamw(names):
        for n in names:
            tr, name = ADAM_TILES[n]
            view = _t if n in TRANSPOSED else (lambda a: a)
            res = _adamw([recv[l][n] for l in range(DEPTH)], *[view(a) for a in adam_in[n]], tr=tr, name=name)
            out[n] = [view(r) for r in res]

    for l in reversed(range(DEPTH)):
        received(l, "late", dx)
    adamw(LATE)
    for l in reversed(range(DEPTH)):
        received(l, "early", out[LATE[-1]][0])
    adamw(EARLY)
    small = [_adamw_small(recv[l]["small"], *small_in[l]) for l in range(DEPTH)]
    for n, _ in SMALL:
        out[n] = [jnp.stack([_small_unpack(small[l][i], P)[n] for l in range(DEPTH)]) for i in range(4)]
    return (loss, dx[None], *[out[n][i] for i in range(4) for n in WEIGHT_NAMES])
```

```python
import functools
import math

import jax
import jax.numpy as jnp
from jax import lax
from jax.experimental import pallas as pl
from jax.experimental.pallas import tpu as pltpu

F32 = jnp.float32
BF = jnp.bfloat16
HIGHEST = lax.Precision.HIGHEST

D_MODEL = 1024
DEPTH = 2
N_HEADS = 8
D_HEAD = 128
CONV_K = 4
SGU_BLOCK = 128
SGU_GROUPS = 8
SGU_CHUNK = 64
FFN_HIDDEN = 2816
N_IN = 8208
N_DEV = 8
IN_BLOCK, IN_PAD = N_IN // N_DEV, 1152
FFN_BLOCK, FFN_PAD = FFN_HIDDEN // N_DEV, 384
FFN_K = N_DEV * FFN_PAD
ALPHA = (2 * DEPTH) ** 0.25
LN_EPS = 1e-5
RMS_EPS = 1e-6
ADAM_LR, ADAM_B1, ADAM_B2, ADAM_EPS, ADAM_WD, ADAM_STEP = 0.001, 0.9, 0.999, 1e-08, 0.01, 10

MESH_AXES = ("x", "y", "c")
DELTA_CHUNK = 128
DELTA_HEADS_PER_STEP = 8
LANES = 128
SUBLANES = 8
VMEM_BYTES = 64 * 1024 * 1024
HALO = SUBLANES
HALO_BF = 2 * SUBLANES


def _cparams(est_bytes, dims=None):
    limit = int(min(max(2 * est_bytes + (8 << 20), 32 << 20), VMEM_BYTES - (6 << 20)))
    kw = dict(vmem_limit_bytes=limit)
    if dims is not None:
        kw["dimension_semantics"] = dims
    return pltpu.CompilerParams(**kw)


def _nbytes(shape, dtype):
    return math.prod(shape) * jnp.dtype(dtype).itemsize


def _dims(kind, ndim):
    lhs, rhs = {"nn": (1, 0), "nt": (1, 1), "tn": (0, 0)}[kind]
    b = ndim - 2
    return (((lhs + b,), (rhs + b,)), (tuple(range(b)), tuple(range(b))))


def _mxu(a, b, kind):
    return lax.dot_general(a, b, _dims(kind, a.ndim), preferred_element_type=F32)


def _dot(a, b):
    return _mxu(a.astype(BF), b.astype(BF), "nn")


def _dot_nt(a, b):
    return _mxu(a.astype(BF), b.astype(BF), "nt")


def _dot_tn(a, b):
    return _mxu(a.astype(BF), b.astype(BF), "tn")


def _split(a):
    hi = a.astype(BF)
    return hi, (a - hi.astype(F32)).astype(BF)


def _dot3(a, b, kind):
    (ah, al), (bh, bl) = _split(a), _split(b)
    return _mxu(ah, bh, kind) + (_mxu(ah, bl, kind) + _mxu(al, bh, kind))


def _dotf(a, b):
    return _dot3(a, b, "nn")


def _dotf_nt(a, b):
    return _dot3(a, b, "nt")


def _dot01(sel, x, kind="nn"):
    s = jnp.broadcast_to(sel.astype(BF), x.shape[:-2] + sel.shape)
    h1 = x.astype(BF)
    r1 = x - h1.astype(F32)
    h2 = r1.astype(BF)
    h3 = (r1 - h2.astype(F32)).astype(BF)
    return _mxu(s, h1, kind) + (_mxu(s, h2, kind) + _mxu(s, h3, kind))


def _sigmoid(x):
    return 0.5 * jnp.tanh(0.5 * x) + 0.5


def _silu(x):
    return x * _sigmoid(x)


def _silu_and_grad(x):
    s = _sigmoid(x)
    return x * s, s * (1.0 + x * (1.0 - s))


def _softplus(x):
    return jnp.maximum(x, 0.0) + jnp.log1p(jnp.exp(-jnp.abs(x)))


def _ln(x, g, b):
    mu = jnp.mean(x, -1, keepdims=True)
    xc = x - mu
    var = jnp.mean(xc * xc, -1, keepdims=True)
    return xc * lax.rsqrt(var + LN_EPS) * g + b


def _iota(shape, dim):
    return lax.broadcasted_iota(jnp.int32, shape, dim)


def _tile(n, pref, align):
    if n <= pref:
        return n
    t = (pref // align) * align
    while t >= align:
        if n % t == 0:
            return t
        t -= align
    raise ValueError(f"no tile for {n} (pref {pref}, align {align})")


def _bcast_rows(v, rows=SUBLANES):
    return jnp.broadcast_to(v, (rows, v.shape[-1]))


def _in_hbm(t):
    return pltpu.with_memory_space_constraint(t, pltpu.HBM)


def _mm(a, b, *, mode, name, out_dtype=F32, add=None, add_scale=1.0, tm=512, tn=1024, tk=1024, cols=None, pair=None):
    if mode == "nn":
        (M, K), N = a.shape, b.shape[1]
    elif mode == "nt":
        (M, K), N = a.shape, b.shape[0]
    else:
        (K, M), N = a.shape, b.shape[1]
    col0 = 0
    if cols is not None:
        col0, N = cols
    tm = _tile(M, tm, LANES if mode == "tn" else SUBLANES * 2)
    tn = _tile(N, tn, LANES)
    tk = _tile(K, tk, LANES)
    nk = K // tk
    j0 = col0 // tn
    if mode == "nn":
        a_spec = pl.BlockSpec((tm, tk), lambda i, j, k: (i, k))
        b_spec = pl.BlockSpec((tk, tn), lambda i, j, k: (k, j + j0))
        dot = _dot
    elif mode == "nt":
        a_spec = pl.BlockSpec((tm, tk), lambda i, j, k: (i, k))
        b_spec = pl.BlockSpec((tn, tk), lambda i, j, k: (j, k))
        dot = _dot_nt
    else:
        a_spec = pl.BlockSpec((tk, tm), lambda i, j, k: (k, i))
        b_spec = pl.BlockSpec((tk, tn), lambda i, j, k: (k, j))
        dot = _dot_tn
    o_spec = pl.BlockSpec((tm, tn), lambda i, j, k: (i, j))
    has_add = add is not None

    n_ab = 2 if pair is None else 4

    def body(*refs):
        ab, (o_ref, acc_ref) = refs[:n_ab], refs[-2:]
        add_ref = refs[n_ab] if has_add else None
        k = pl.program_id(2)
        part = dot(ab[0][...], ab[1][...])
        if pair is not None:
            part = part + dot(ab[2][...], ab[3][...])

        def finish(total):
            if has_add:
                total = total + add_scale * add_ref[...]
            o_ref[...] = total.astype(out_dtype)

        if nk == 1:
            finish(part)
        else:
            @pl.when(k == 0)
            def _():
                acc_ref[...] = part

            @pl.when(jnp.logical_and(k > 0, k < nk - 1))
            def _():
                acc_ref[...] += part

            @pl.when(k == nk - 1)
            def _():
                finish(acc_ref[...] + part)

    in_specs = [a_spec, b_spec] * (n_ab // 2) + ([o_spec] if has_add else [])
    args = (a, b) + (tuple(pair) if pair is not None else ()) + ((add,) if has_add else ())
    est = ((n_ab // 2) * (_nbytes((tm, tk), a.dtype) + _nbytes((tk, tn), b.dtype)) + 2 * _nbytes((tm, tn), F32)
           + (_nbytes((tm, tn), F32) if has_add else 0)) + 2 * _nbytes((tm, tn), F32)
    return pl.pallas_call(
        body, name=name,
        grid=(M // tm, N // tn, nk),
        in_specs=in_specs, out_specs=o_spec,
        out_shape=jax.ShapeDtypeStruct((M, N), out_dtype),
        scratch_shapes=[pltpu.VMEM((tm, tn) if nk > 1 else (SUBLANES, LANES), F32)],
        compiler_params=_cparams(est, ("parallel", "parallel", "arbitrary")),
    )(*[_in_hbm(t) for t in args])


def _shifted(xt, halo, first):
    halo = jnp.where(first, 0.0, halo)
    xc = jnp.concatenate([halo, xt], axis=0)
    return [xt] + [pltpu.roll(xc, s, 0)[HALO:] for s in range(1, CONV_K)]


def _conv_taps(shifted, w_ref):
    out = shifted[0] * w_ref[CONV_K - 1:CONV_K, :]
    for s in range(1, CONV_K):
        out = out + shifted[s] * w_ref[CONV_K - 1 - s:CONV_K - s, :]
    return out


def _gates(ba, arow, dtrow):
    lane = _iota(ba.shape, 1)
    beta = _sigmoid(ba)
    g = -jnp.exp(arow) * _softplus(ba + dtrow)
    return jnp.where(lane < N_HEADS, beta, jnp.where(lane < 2 * N_HEADS, g, 0.0))


def _l2n(x):
    return x * lax.rsqrt(jnp.sum(x * x, -1, keepdims=True) + RMS_EPS)


def _qkv_prep(proj, ba, convw, arow, dtrow, *, tm=256):
    S = proj.shape[0]
    tm = _tile(S, tm, SUBLANES)
    W3 = 3 * D_MODEL
    hb = tm // HALO

    def body(xt_ref, halo_ref, ba_ref, w_ref, a_ref, dt_ref, q_ref, k_ref, v_ref, gb_ref, c_ref):
        c = _conv_taps(_shifted(xt_ref[...], halo_ref[...], pl.program_id(0) == 0), w_ref)
        c_ref[...] = c
        c = _silu(c)
        for h in range(N_HEADS):
            lo = h * D_HEAD
            q_ref[:, lo:lo + D_HEAD] = _l2n(c[:, lo:lo + D_HEAD])
            k_ref[:, lo:lo + D_HEAD] = _l2n(c[:, D_MODEL + lo:D_MODEL + lo + D_HEAD])
        v_ref[...] = c[:, 2 * D_MODEL:]
        gb_ref[...] = _gates(ba_ref[...], a_ref[...], dt_ref[...])

    row = lambda w, col=0: pl.BlockSpec((tm, w), lambda i: (i, col))
    full = lambda shape: pl.BlockSpec(shape, lambda i: (0,) * len(shape))
    est = 4 * _nbytes((tm, W3), F32)
    return pl.pallas_call(
        body, name="qkv_prep", grid=(S // tm,),
        in_specs=[row(W3), pl.BlockSpec((HALO, W3), lambda i: (jnp.maximum(i * hb - 1, 0), 0)), row(LANES),
                  full((CONV_K, W3)), full((1, LANES)), full((1, LANES))],
        out_specs=[row(D_MODEL), row(D_MODEL), row(D_MODEL), row(LANES), row(W3)],
        out_shape=[jax.ShapeDtypeStruct((S, D_MODEL), F32)] * 3 + [jax.ShapeDtypeStruct((S, LANES), F32),
                                                                   jax.ShapeDtypeStruct((S, W3), F32)],
        compiler_params=_cparams(est, ("arbitrary",)),
    )(proj, proj, ba, convw, arow, dtrow)


def _qkv_prep_bwd(proj, conv_out, ba, arow, dtrow, dq, dk, dv, dgb, *, tm=256):
    S = proj.shape[0]
    tm = _tile(S, tm, SUBLANES * 2)
    W3 = 3 * D_MODEL
    hb = tm // HALO

    def body(xt_ref, halo_ref, c_ref, ba_ref, a_ref, dt_ref, dq_ref, dk_ref, dv_ref, dgb_ref,
             dcb_ref, dba_ref, dw_ref, da_ref, ddt_ref, dc_ref):
        i = pl.program_id(0)

        @pl.when(i == 0)
        def _():
            dw_ref[...] = jnp.zeros_like(dw_ref)
            da_ref[...] = jnp.zeros_like(da_ref)
            ddt_ref[...] = jnp.zeros_like(ddt_ref)

        shifted = _shifted(xt_ref[...], halo_ref[...], i == 0)
        a, ds = _silu_and_grad(c_ref[...])
        for h in range(N_HEADS):
            for base, d_ref in ((0, dq_ref), (D_MODEL, dk_ref)):
                lo = base + h * D_HEAD
                _, vj = jax.vjp(_l2n, a[:, lo:lo + D_HEAD])
                (dx,) = vj(d_ref[:, h * D_HEAD:(h + 1) * D_HEAD])
                dc_ref[:, lo:lo + D_HEAD] = dx * ds[:, lo:lo + D_HEAD]
        dc_ref[:, 2 * D_MODEL:] = dv_ref[...] * ds[:, 2 * D_MODEL:]
        dc = dc_ref[...]
        dcb_ref[...] = dc.astype(BF)
        for s in range(CONV_K):
            kk = CONV_K - 1 - s
            dw_ref[kk:kk + 1, :] += jnp.sum(dc * shifted[s], axis=0, keepdims=True)
        _, vj = jax.vjp(_gates, ba_ref[...], a_ref[...], dt_ref[...])
        dba, da, ddt = vj(dgb_ref[...])
        dba_ref[...] = dba.astype(BF)
        da_ref[...] += _bcast_rows(da)
        ddt_ref[...] += _bcast_rows(ddt)

    row = lambda w, col=0: pl.BlockSpec((tm, w), lambda i: (i, col))
    full = lambda shape: pl.BlockSpec(shape, lambda i: (0,) * len(shape))
    est = 8 * _nbytes((tm, W3), F32)
    return pl.pallas_call(
        body, name="qkv_prep_bwd", grid=(S // tm,),
        in_specs=[row(W3), pl.BlockSpec((HALO, W3), lambda i: (jnp.maximum(i * hb - 1, 0), 0)), row(W3), row(LANES),
                  full((1, LANES)), full((1, LANES)),
                  row(D_MODEL), row(D_MODEL), row(D_MODEL), row(LANES)],
        out_specs=[row(W3), row(LANES), full((SUBLANES, W3)), full((SUBLANES, LANES)), full((SUBLANES, LANES))],
        out_shape=[jax.ShapeDtypeStruct((S, W3), BF), jax.ShapeDtypeStruct((S, LANES), BF),
                   jax.ShapeDtypeStruct((SUBLANES, W3), F32), jax.ShapeDtypeStruct((SUBLANES, LANES), F32),
                   jax.ShapeDtypeStruct((SUBLANES, LANES), F32)],
        scratch_shapes=[pltpu.VMEM((tm, W3), F32)],
        compiler_params=_cparams(est, ("arbitrary",)),
    )(proj, proj, conv_out, ba, arow, dtrow, dq, dk, dv, dgb)


def _conv_bwd(dc, convw, dproj, *, tm=256):
    S, W3 = dc.shape
    tm = _tile(S, tm, HALO_BF)
    hb = tm // HALO_BF
    nt = S // tm

    def body(dc_ref, nxt_ref, w_ref, dproj_ref, o_ref):
        last = pl.program_id(0) == nt - 1
        nxt = jnp.where(last, 0.0, nxt_ref[...].astype(F32))
        cur = dc_ref[...].astype(F32)
        xc = jnp.concatenate([cur, nxt], axis=0)
        out = cur * w_ref[CONV_K - 1:CONV_K, :]
        for s in range(1, CONV_K):
            out = out + pltpu.roll(xc, tm + HALO_BF - s, 0)[:tm] * w_ref[CONV_K - 1 - s:CONV_K - s, :]
        o_ref[...] = out.astype(BF)

    est = 5 * _nbytes((tm, W3), F32)
    return pl.pallas_call(
        body, name="conv_bwd", grid=(nt,),
        in_specs=[pl.BlockSpec((tm, W3), lambda i: (i, 0)),
                  pl.BlockSpec((HALO_BF, W3), lambda i: (jnp.minimum((i + 1) * hb, S // HALO_BF - 1), 0)),
                  pl.BlockSpec((CONV_K, W3), lambda i: (0, 0)), pl.BlockSpec(memory_space=pl.ANY)],
        out_specs=pl.BlockSpec((tm, W3), lambda i: (i, 0)),
        out_shape=jax.ShapeDtypeStruct(dproj.shape, BF),
        input_output_aliases={3: 0},
        compiler_params=_cparams(est, ("parallel",)),
    )(dc, dc, convw, dproj)


NEUMANN_BLOCK = 8


def _inv_unit_lower(A):
    C = A.shape[-1]
    row, col = _iota((C, C), 0), _iota((C, C), 1)
    eye = jnp.where(row == col, 1.0, 0.0).astype(F32)
    Ab = A.astype(BF)
    sh = jnp.int32(int(math.log2(NEUMANN_BLOCK)))
    B = jnp.where(lax.shift_right_logical(row, sh) == lax.shift_right_logical(col, sh), Ab, jnp.zeros_like(Ab))
    B2 = _mxu(B, B, "nn")
    B4 = _dot3(B2, B2, "nn")
    b2h, b2l = _split(B2)
    P = eye - B.astype(F32) + B2 - (_mxu(B, b2h, "nn") + _mxu(B, b2l, "nn"))
    T = P + _dot3(P, B4, "nn")
    b = NEUMANN_BLOCK
    while b < C:
        hi = ~(2 * b - 1)
        off = ((row & hi) == (col & hi)) & ((row & b) != 0) & ((col & b) == 0)
        Aoff = jnp.where(off, Ab, jnp.zeros_like(Ab))
        th, tl = _split(T)
        xh, xl = _split(_mxu(th, Aoff, "nn") + _mxu(tl, Aoff, "nn"))
        T = T - (_mxu(xh, th, "nn") + (_mxu(xh, tl, "nn") + _mxu(xl, th, "nn")))
        b *= 2
    return T


def _delta_common(q, k, g, beta):
    C = q.shape[-2]
    row, col = _iota((C, C), 0), _iota((C, C), 1)
    tril = row >= col
    qs = q * (D_HEAD ** -0.5)
    gcb = _dot01(jnp.where(tril, 1.0, 0.0), jnp.broadcast_to(g, g.shape[:-1] + (LANES,)))
    gc = gcb[..., :1]
    gr = jnp.swapaxes(gcb, -1, -2)
    Dm = jnp.exp(jnp.where(tril, gc - gr, -1e30))
    Dmt = jnp.exp(jnp.where(row <= col, gr - gc, -1e30))
    eg = jnp.exp(gc)
    gl = jnp.sum(jnp.where(_iota((C, 1), 0) == C - 1, gc, 0.0), axis=(-2, -1), keepdims=True)
    el = jnp.exp(gl)
    er = jnp.exp(gl - gc)
    kb = k * beta
    KK = _dot_nt(kb, k)
    QK = _dot_nt(qs, k)
    return dict(row=row, col=col, tril=tril, qs=qs, gc=gc, Dm=Dm, Dmt=Dmt, eg=eg, el=el, er=er, kb=kb, KK=KK, QK=QK)


def _delta_chunk_fwd(S0, q, k, v, g, beta):
    m = _delta_common(q, k, g, beta)
    T = _inv_unit_lower(jnp.where(m["row"] > m["col"], m["KK"] * m["Dm"], 0.0))
    u = _dotf(T, v * beta)
    w = _dotf(T, m["kb"] * m["eg"])
    vn = u - _dot(w, S0)
    o = _dot(m["qs"] * m["eg"], S0) + _dot(m["QK"] * m["Dm"], vn)
    S1 = S0 * m["el"] + _dot_tn(k * m["er"], vn)
    return o, S1, jnp.swapaxes(T, -1, -2), u, w


def _delta_chunk_bwd(S0, q, k, v, g, beta, Tt, u, w, do, dS1):
    m = _delta_common(q, k, g, beta)
    C = q.shape[-2]
    qs, Dm, Dmt, eg, el, er, kb, KK, QK = (m[n] for n in ("qs", "Dm", "Dmt", "eg", "el", "er", "kb", "KK", "QK"))
    strict = m["row"] > m["col"]
    total = lambda x: jnp.sum(x, axis=(-2, -1), keepdims=True)
    vn = u - _dot(w, S0)
    qg = qs * eg
    kr = k * er

    dvn = _dot(_dot_nt(k, qs) * Dmt, do) + _dot(kr, dS1)
    dS0 = dS1 * el + _dot_tn(qg, do) - _dot_tn(w, dvn)
    d_el = total(dS1 * S0)
    dqg = _dot_nt(do, S0)
    dqs = dqg * eg
    deg = jnp.sum(dqg * qs, -1, keepdims=True)
    dP = _dot_nt(do, vn)
    dPD = dP * Dm
    dqs = dqs + _dot(dPD, k)
    dk = _dot(_dot_nt(vn, do) * Dmt, qs)
    dD = dP * QK
    dkr = _dot_nt(vn, dS1)
    dk = dk + dkr * er
    der = jnp.sum(dkr * k, -1, keepdims=True)
    dw = -_dot_nt(dvn, S0)
    th, tl = _split(Tt)

    def tt_times(x):
        xh, xl = _split(x)
        return _mxu(th, xh, "nn") + (_mxu(th, xl, "nn") + _mxu(tl, xh, "nn"))

    dru = tt_times(dvn)
    drw = tt_times(dw)
    dA = -(_dotf_nt(dru, u) + _dotf_nt(drw, w))
    dAm = jnp.where(strict, dA, 0.0)
    dKK = dAm * Dm
    dkb = _dot(dKK, k)
    dk = dk + _dot_tn(dKK, kb)
    dD = dD + dAm * KK
    dv = dru * beta
    dbeta = jnp.sum(dru * v, -1, keepdims=True)
    dkb = dkb + drw * eg
    deg = deg + jnp.sum(drw * kb, -1, keepdims=True)
    dk = dk + dkb * beta
    dbeta = dbeta + jnp.sum(dkb * k, -1, keepdims=True)
    E = dD * Dm
    dgc = jnp.sum(E, -1, keepdims=True) - jnp.sum(jnp.swapaxes(E, -1, -2), -1, keepdims=True)
    dgc = dgc + deg * eg - der * er
    dgl = total(der * er) + d_el * el
    dgc = dgc + jnp.where(_iota((C, 1), 0) == C - 1, dgl, 0.0)
    triu = jnp.where(m["row"] <= m["col"], 1.0, 0.0)
    dg = _dot01(triu, jnp.broadcast_to(dgc, dgc.shape[:-1] + (LANES,)))[..., :1]
    dq = dqs * (D_HEAD ** -0.5)
    return dq, dk, dv, dg, dbeta, dS0


def _head_cols(gb, h):
    lane = _iota(gb.shape, 1)
    beta = jnp.sum(jnp.where(lane == h, gb, 0.0), -1, keepdims=True)
    g = jnp.sum(jnp.where(lane == N_HEADS + h, gb, 0.0), -1, keepdims=True)
    return g, beta


def _delta_fwd(q, k, v, gb):
    S = q.shape[0]
    C = DELTA_CHUNK
    N = S // C

    HB = DELTA_HEADS_PER_STEP

    def body(q_ref, k_ref, v_ref, gb_ref, o_ref, st_ref, t_ref, u_ref, w_ref, s_scr):
        n, hb = pl.program_id(0), pl.program_id(1)
        gb = gb_ref[...]

        @pl.when(n == 0)
        def _():
            for hh in range(HB):
                s_scr[hb * HB + hh] = jnp.zeros((D_HEAD, D_HEAD), F32)

        heads = [hb * HB + hh for hh in range(HB)]
        cols = [slice(hh * D_HEAD, (hh + 1) * D_HEAD) for hh in range(HB)]
        per_head = lambda ref: jnp.stack([ref[:, c] for c in cols])
        g, beta = (jnp.stack(t) for t in zip(*[_head_cols(gb, h) for h in heads]))
        S0 = jnp.stack([s_scr[h] for h in heads])
        o, S1, Tt, u, w = _delta_chunk_fwd(S0, per_head(q_ref), per_head(k_ref), per_head(v_ref), g, beta)
        for hh in range(HB):
            st_ref[hh, 0] = S0[hh]
            t_ref[hh, 0] = Tt[hh]
            o_ref[:, cols[hh]] = o[hh]
            u_ref[:, cols[hh]] = u[hh]
            w_ref[:, cols[hh]] = w[hh]
            s_scr[heads[hh]] = S1[hh]

    hd = pl.BlockSpec((C, HB * D_HEAD), lambda n, h: (n, h))
    mat = pl.BlockSpec((HB, 1, D_HEAD, D_HEAD), lambda n, h: (h, n, 0, 0))
    est = 40 * HB * _nbytes((C, D_HEAD), F32)
    seq = jax.ShapeDtypeStruct((S, N_HEADS * D_HEAD), F32)
    return pl.pallas_call(
        body, name="delta_fwd", grid=(N, N_HEADS // HB),
        in_specs=[hd, hd, hd, pl.BlockSpec((C, LANES), lambda n, h: (n, 0))],
        out_specs=[hd, mat, mat, hd, hd],
        out_shape=[seq, jax.ShapeDtypeStruct((N_HEADS, N, D_HEAD, D_HEAD), F32),
                   jax.ShapeDtypeStruct((N_HEADS, N, C, C), F32), seq, seq],
        scratch_shapes=[pltpu.VMEM((N_HEADS, D_HEAD, D_HEAD), F32)],
        compiler_params=_cparams(est, ("arbitrary", "arbitrary")),
    )(q, k, v, gb)


def _delta_bwd(q, k, v, gb, st, tinv, u, w, do):
    S = q.shape[0]
    C = DELTA_CHUNK
    N = S // C

    HB = DELTA_HEADS_PER_STEP

    def body(q_ref, k_ref, v_ref, gb_ref, st_ref, t_ref, u_ref, w_ref, do_ref, dq_ref, dk_ref, dv_ref, dgb_ref, ds_scr):
        n, hb = pl.program_id(0), pl.program_id(1)
        gb = gb_ref[...]
        lane = _iota((C, LANES), 1)
        dgb = jnp.zeros((C, LANES), F32)

        @pl.when(n == 0)
        def _():
            for hh in range(HB):
                ds_scr[hb * HB + hh] = jnp.zeros((D_HEAD, D_HEAD), F32)

        heads = [hb * HB + hh for hh in range(HB)]
        cols = [slice(hh * D_HEAD, (hh + 1) * D_HEAD) for hh in range(HB)]
        per_head = lambda ref: jnp.stack([ref[:, c] for c in cols])
        g, beta = (jnp.stack(t) for t in zip(*[_head_cols(gb, h) for h in heads]))
        dS1 = jnp.stack([ds_scr[h] for h in heads])
        dq, dk, dv, dg, dbeta, dS0 = _delta_chunk_bwd(
            st_ref[:, 0], per_head(q_ref), per_head(k_ref), per_head(v_ref), g, beta, t_ref[:, 0],
            per_head(u_ref), per_head(w_ref), per_head(do_ref), dS1)
        for hh, h in enumerate(heads):
            dq_ref[:, cols[hh]] = dq[hh]
            dk_ref[:, cols[hh]] = dk[hh]
            dv_ref[:, cols[hh]] = dv[hh]
            dgb = dgb + jnp.where(lane == h, dbeta[hh], 0.0) + jnp.where(lane == N_HEADS + h, dg[hh], 0.0)
            ds_scr[h] = dS0[hh]

        @pl.when(hb == 0)
        def _():
            dgb_ref[...] = dgb

        @pl.when(hb > 0)
        def _():
            dgb_ref[...] += dgb

    hd = pl.BlockSpec((C, HB * D_HEAD), lambda n, h: (N - 1 - n, h))
    mat = pl.BlockSpec((HB, 1, D_HEAD, D_HEAD), lambda n, h: (h, N - 1 - n, 0, 0))
    gbs = pl.BlockSpec((C, LANES), lambda n, h: (N - 1 - n, 0))
    est = 60 * HB * _nbytes((C, D_HEAD), F32)
    return pl.pallas_call(
        body, name="delta_bwd", grid=(N, N_HEADS // HB),
        in_specs=[hd, hd, hd, gbs, mat, mat, hd, hd, hd],
        out_specs=[hd, hd, hd, gbs],
        out_shape=[jax.ShapeDtypeStruct((S, N_HEADS * D_HEAD), F32)] * 3 + [jax.ShapeDtypeStruct((S, LANES), F32)],
        scratch_shapes=[pltpu.VMEM((N_HEADS, D_HEAD, D_HEAD), F32)],
        compiler_params=_cparams(est, ("arbitrary", "arbitrary")),
    )(q, k, v, gb, st, tinv, u, w, do)


def _ya_head(o, z, onw):
    return o * lax.rsqrt(jnp.mean(o * o, -1, keepdims=True) + RMS_EPS) * onw * _silu(z)


def _norm_cdf(x):
    return 0.5 * (1.0 + lax.erf(x * 0.7071067811865476))


def _norm_pdf(x):
    return jnp.exp(-0.5 * x * x) * 0.3989422804014327


def _chunk_causal(shape, di, dj):
    sh = jnp.int32(int(math.log2(SGU_CHUNK)))
    return lax.shift_right_logical(_iota(shape, di), sh) >= lax.shift_right_logical(_iota(shape, dj), sh)


def _ws_masked(ws):
    return jnp.where(_chunk_causal(ws.shape, 1, 2), ws, 0.0)


def _mix_prep(o, proj, onw, sg, sb, ws, bst, *, tm=256):
    S = o.shape[0]
    tm = _tile(S, tm, SGU_BLOCK)

    def body(o_ref, z_ref, u_ref, vg_ref, onw_ref, sg_ref, sb_ref, ws_ref, bst_ref, ya_ref, yb_ref, phi_ref):
        onw = onw_ref[...]
        for h in range(N_HEADS):
            sl = slice(h * D_HEAD, (h + 1) * D_HEAD)
            ya_ref[:, sl] = _ya_head(o_ref[:, sl], z_ref[:, sl].astype(F32), onw).astype(BF)
        u, vg = u_ref[...].astype(F32), vg_ref[...].astype(F32)
        phi_u, phi_v = _norm_cdf(u), _norm_cdf(vg)
        phi_ref[:, :D_MODEL] = phi_u
        phi_ref[:, D_MODEL:] = phi_v
        ua, vl = u * phi_u, _ln(vg * phi_v, sg_ref[...], sb_ref[...])
        wsm = _ws_masked(ws_ref[...])
        bst = bst_ref[...]
        for blk in range(tm // SGU_BLOCK):
            rs = slice(blk * SGU_BLOCK, (blk + 1) * SGU_BLOCK)
            for gi in range(SGU_GROUPS):
                cs = slice(gi * D_HEAD, (gi + 1) * D_HEAD)
                sp = _dot(wsm[gi], vl[rs, cs]) + bst[:, gi:gi + 1]
                yb_ref[rs, cs] = (ua[rs, cs] * sp).astype(BF)

    blk = lambda col: pl.BlockSpec((tm, D_MODEL), lambda i: (i, col))
    full = lambda shape: pl.BlockSpec(shape, lambda i: (0,) * len(shape))
    est = 10 * _nbytes((tm, D_MODEL), F32)
    return pl.pallas_call(
        body, name="mix_prep", grid=(S // tm,),
        in_specs=[blk(0), blk(0), blk(1), blk(2), full((1, D_HEAD)), full((1, D_MODEL)), full((1, D_MODEL)),
                  full((SGU_GROUPS, SGU_BLOCK, SGU_BLOCK)), full((SGU_BLOCK, LANES))],
        out_specs=[blk(0), blk(0), pl.BlockSpec((tm, 2 * D_MODEL), lambda i: (i, 0))],
        out_shape=[jax.ShapeDtypeStruct((S, D_MODEL), BF)] * 2 + [jax.ShapeDtypeStruct((S, 2 * D_MODEL), F32)],
        compiler_params=_cparams(est, ("parallel",)),
    )(o, proj, proj, proj, onw, sg, sb, ws, bst)


def _mix_prep_bwd(o, proj, phi, onw, sg, sb, ws, bst, dya, dyb, dproj, *, tm=256):
    S = o.shape[0]
    tm = _tile(S, tm, SGU_BLOCK)

    def body(o_ref, z_ref, u_ref, vg_ref, phi_ref, onw_ref, sg_ref, sb_ref, ws_ref, bst_ref, dya_ref, dyb_ref, dproj_in,
             do_ref, dzuv_ref, donw_ref, dsg_ref, dsb_ref, dws_ref, dbst_ref, dvl_scr, dua_scr):
        dz_ref, du_ref, dvg_ref = (dzuv_ref.at[:, k * D_MODEL:(k + 1) * D_MODEL] for k in range(3))
        @pl.when(pl.program_id(0) == 0)
        def _():
            for r in (donw_ref, dsg_ref, dsb_ref, dws_ref, dbst_ref):
                r[...] = jnp.zeros_like(r)

        onw = onw_ref[...]
        donw = jnp.zeros((1, D_HEAD), F32)
        for h in range(N_HEADS):
            sl = slice(h * D_HEAD, (h + 1) * D_HEAD)
            _, vj = jax.vjp(_ya_head, o_ref[:, sl], z_ref[:, sl].astype(F32), onw)
            do_h, dz_h, donw_h = vj(dya_ref[:, sl])
            do_ref[:, sl] = do_h.astype(BF)
            dz_ref[:, sl] = dz_h.astype(BF)
            donw = donw + donw_h
        donw_ref[...] += _bcast_rows(donw)

        u, vg = u_ref[...].astype(F32), vg_ref[...].astype(F32)
        phi_u, phi_v = phi_ref[:, :D_MODEL], phi_ref[:, D_MODEL:]
        ua = u * phi_u
        vl, vj = jax.vjp(_ln, vg * phi_v, sg_ref[...], sb_ref[...])
        wsm = _ws_masked(ws_ref[...])
        bst = bst_ref[...]
        lane = _iota((SGU_BLOCK, LANES), 1)
        dbst = jnp.zeros((SGU_BLOCK, LANES), F32)
        cmask = _chunk_causal((SGU_BLOCK, SGU_BLOCK), 0, 1)
        for gi in range(SGU_GROUPS):
            cs = slice(gi * D_HEAD, (gi + 1) * D_HEAD)
            wg = wsm[gi]
            wgt = jnp.transpose(wg)
            dwg = jnp.zeros((SGU_BLOCK, SGU_BLOCK), F32)
            for blk in range(tm // SGU_BLOCK):
                rs = slice(blk * SGU_BLOCK, (blk + 1) * SGU_BLOCK)
                sp = _dot(wg, vl[rs, cs]) + bst[:, gi:gi + 1]
                dyb = dyb_ref[rs, cs]
                dsp = dyb * ua[rs, cs]
                dua_scr[rs, cs] = dyb * sp
                dvl_scr[rs, cs] = _dot(wgt, dsp)
                dwg = dwg + _dot_nt(dsp, vl[rs, cs])
                dbst = dbst + jnp.where(lane == gi, jnp.sum(dsp, -1, keepdims=True), 0.0)
            dws_ref[gi] += jnp.where(cmask, dwg, 0.0)
        dbst_ref[...] += dbst
        dgv, dsg, dsb = vj(dvl_scr[...])
        du_ref[...] = (dua_scr[...] * (phi_u + u * _norm_pdf(u))).astype(BF)
        dvg_ref[...] = (dgv * (phi_v + vg * _norm_pdf(vg))).astype(BF)
        dsg_ref[...] += _bcast_rows(dsg)
        dsb_ref[...] += _bcast_rows(dsb)

    blk = lambda col: pl.BlockSpec((tm, D_MODEL), lambda i: (i, col))
    full = lambda shape: pl.BlockSpec(shape, lambda i: (0,) * len(shape))
    est = 16 * _nbytes((tm, D_MODEL), F32)
    outs = pl.pallas_call(
        body, name="mix_prep_bwd", grid=(S // tm,),
        in_specs=[blk(0), blk(0), blk(1), blk(2), pl.BlockSpec((tm, 2 * D_MODEL), lambda i: (i, 0)),
                  full((1, D_HEAD)), full((1, D_MODEL)), full((1, D_MODEL)),
                  full((SGU_GROUPS, SGU_BLOCK, SGU_BLOCK)), full((SGU_BLOCK, LANES)), blk(0), blk(0),
                  pl.BlockSpec(memory_space=pl.ANY)],
        out_specs=[blk(0), pl.BlockSpec((tm, 3 * D_MODEL), lambda i: (i, 1)),
                   full((SUBLANES, D_HEAD)), full((SUBLANES, D_MODEL)), full((SUBLANES, D_MODEL)),
                   full((SGU_GROUPS, SGU_BLOCK, SGU_BLOCK)), full((SGU_BLOCK, LANES))],
        out_shape=[jax.ShapeDtypeStruct((S, D_MODEL), BF), jax.ShapeDtypeStruct(dproj.shape, BF),
                   jax.ShapeDtypeStruct((SUBLANES, D_HEAD), F32), jax.ShapeDtypeStruct((SUBLANES, D_MODEL), F32),
                   jax.ShapeDtypeStruct((SUBLANES, D_MODEL), F32),
                   jax.ShapeDtypeStruct((SGU_GROUPS, SGU_BLOCK, SGU_BLOCK), F32),
                   jax.ShapeDtypeStruct((SGU_BLOCK, LANES), F32)],
        input_output_aliases={12: 1},
        scratch_shapes=[pltpu.VMEM((tm, D_MODEL), F32)] * 2,
        compiler_params=_cparams(est, ("arbitrary",)),
    )(o, proj, proj, proj, phi, onw, sg, sb, ws, bst, dya, dyb, dproj)
    return outs


def _mm_gate_merge(ya, yb, wpa, wpb, proj, *, tm=512):
    S = ya.shape[0]
    tm = _tile(S, tm, SUBLANES * 2)

    def body(ya_ref, yb_ref, wa_ref, wb_ref, ga_ref, gb_ref, pa_ref, pb_ref, m_ref):
        pa = _dot(ya_ref[...], wa_ref[...]).astype(BF)
        pb = _dot(yb_ref[...], wb_ref[...]).astype(BF)
        pa_ref[...] = pa
        pb_ref[...] = pb
        m_ref[...] = (_sigmoid(ga_ref[...].astype(F32)) * pa.astype(F32)
                      + _sigmoid(gb_ref[...].astype(F32)) * pb.astype(F32)).astype(BF)

    blk = lambda col: pl.BlockSpec((tm, D_MODEL), lambda i: (i, col))
    wsp = pl.BlockSpec((D_MODEL, D_MODEL), lambda i: (0, 0))
    return pl.pallas_call(
        body, name="mm_gate_merge", grid=(S // tm,),
        in_specs=[blk(0), blk(0), wsp, wsp, blk(3), blk(4)], out_specs=[blk(0)] * 3,
        out_shape=[jax.ShapeDtypeStruct((S, D_MODEL), BF)] * 3,
        compiler_params=_cparams(2 * _nbytes((D_MODEL, D_MODEL), BF) + 8 * _nbytes((tm, D_MODEL), F32), ("parallel",)),
    )(ya, yb, wpa, wpb, proj, proj)


def _mm_gate_merge_bwd(dmix, wo, pa, pb, proj, *, tm=512):
    S = pa.shape[0]
    tm = _tile(S, tm, SUBLANES * 2)

    def body(d_ref, w_ref, pa_ref, pb_ref, ga_ref, gb_ref, dpa_ref, dpb_ref, dg_ref):
        dm = _dot_nt(d_ref[...], w_ref[...])
        sa, sb = _sigmoid(ga_ref[...].astype(F32)), _sigmoid(gb_ref[...].astype(F32))
        dpa_ref[...] = (dm * sa).astype(BF)
        dpb_ref[...] = (dm * sb).astype(BF)
        dg_ref[:, :D_MODEL] = (dm * pa_ref[...].astype(F32) * sa * (1.0 - sa)).astype(BF)
        dg_ref[:, D_MODEL:] = (dm * pb_ref[...].astype(F32) * sb * (1.0 - sb)).astype(BF)

    blk = lambda col: pl.BlockSpec((tm, D_MODEL), lambda i: (i, col))
    est = _nbytes((D_MODEL, D_MODEL), BF) + 10 * _nbytes((tm, D_MODEL), F32)
    return pl.pallas_call(
        body, name="mm_gate_merge_bwd", grid=(S // tm,),
        in_specs=[blk(0), pl.BlockSpec((D_MODEL, D_MODEL), lambda i: (0, 0)), blk(0), blk(0), blk(3), blk(4)],
        out_specs=[blk(0), blk(0), pl.BlockSpec((tm, 2 * D_MODEL), lambda i: (i, 3))],
        out_shape=[jax.ShapeDtypeStruct((S, D_MODEL), BF)] * 2 + [jax.ShapeDtypeStruct((S, 8 * D_MODEL), BF)],
        compiler_params=_cparams(est, ("parallel",)),
    )(dmix, wo, pa, pb, proj, proj)


def _mm_swiglu(xb, wgt, wut, *, tm=1024, tn=768):
    S, K = xb.shape
    tm = _tile(S, tm, SUBLANES * 2)
    tn = _tile(FFN_K, tn, LANES)

    def body(x_ref, wg_ref, wu_ref, hg_ref, hu_ref, h_ref):
        x = x_ref[...]
        hg = _dot_nt(x, wg_ref[...]).astype(BF)
        hu = _dot_nt(x, wu_ref[...]).astype(BF)
        hg_ref[...] = hg
        hu_ref[...] = hu
        h_ref[...] = (_silu(hg.astype(F32)) * hu.astype(F32)).astype(BF)

    out = pl.BlockSpec((tm, tn), lambda i, j: (i, j))
    est = _nbytes((tm, K), BF) + 2 * _nbytes((K, tn), BF) + 6 * _nbytes((tm, tn), F32)
    return pl.pallas_call(
        body, name="mm_swiglu", grid=(S // tm, FFN_K // tn),
        in_specs=[pl.BlockSpec((tm, K), lambda i, j: (i, 0)), pl.BlockSpec((tn, K), lambda i, j: (j, 0)),
                  pl.BlockSpec((tn, K), lambda i, j: (j, 0))],
        out_specs=[out] * 3, out_shape=[jax.ShapeDtypeStruct((S, FFN_K), BF)] * 3,
        compiler_params=_cparams(est, ("parallel", "parallel")),
    )(xb, wgt, wut)


def _mm_swiglu_bwd(dffn, wd, hg, hu, *, tm=1024, tn=768):
    S, K = dffn.shape
    tm = _tile(S, tm, SUBLANES * 2)
    tn = _tile(FFN_K, tn, LANES)

    def body(d_ref, w_ref, hg_ref, hu_ref, dhg_ref, dhu_ref):
        dh = _dot_nt(d_ref[...], w_ref[...])
        act, dact = _silu_and_grad(hg_ref[...].astype(F32))
        dhg_ref[...] = (dh * hu_ref[...].astype(F32) * dact).astype(BF)
        dhu_ref[...] = (dh * act).astype(BF)

    out = pl.BlockSpec((tm, tn), lambda i, j: (i, j))
    est = _nbytes((tm, K), dffn.dtype) + _nbytes((tn, K), BF) + 8 * _nbytes((tm, tn), F32)
    return pl.pallas_call(
        body, name="mm_swiglu_bwd", grid=(S // tm, FFN_K // tn),
        in_specs=[pl.BlockSpec((tm, K), lambda i, j: (i, 0)), pl.BlockSpec((tn, K), lambda i, j: (j, 0)), out, out],
        out_specs=[out, out], out_shape=[jax.ShapeDtypeStruct((S, FFN_K), BF)] * 2,
        compiler_params=_cparams(est, ("parallel", "parallel")),
    )(dffn, wd, hg, hu)


def _mm_resid_ln(a, bmat, x, g, b, *, name, tm=512):
    S, K = a.shape
    tm = _tile(S, tm, SUBLANES * 2)

    def body(a_ref, w_ref, x_ref, g_ref, b_ref, pre_ref, y_ref, yb_ref):
        pre = ALPHA * x_ref[...] + _dot(a_ref[...], w_ref[...])
        y = _ln(pre, g_ref[...], b_ref[...])
        pre_ref[...] = pre
        y_ref[...] = y
        yb_ref[...] = y.astype(BF)

    blk = pl.BlockSpec((tm, D_MODEL), lambda i: (i, 0))
    vec = pl.BlockSpec((1, D_MODEL), lambda i: (0, 0))
    est = _nbytes((tm, K), BF) + _nbytes((K, D_MODEL), BF) + 8 * _nbytes((tm, D_MODEL), F32)
    return pl.pallas_call(
        body, name=name, grid=(S // tm,),
        in_specs=[pl.BlockSpec((tm, K), lambda i: (i, 0)), pl.BlockSpec((K, D_MODEL), lambda i: (0, 0)), blk, vec, vec],
        out_specs=[blk, blk, blk],
        out_shape=[jax.ShapeDtypeStruct((S, D_MODEL), F32)] * 2 + [jax.ShapeDtypeStruct((S, D_MODEL), BF)],
        compiler_params=_cparams(est, ("parallel",)),
    )(a, bmat, x, g, b)


def _ln_bwd(pre, g, b, dy, *, tm=512):
    S = pre.shape[0]
    tm = _tile(S, tm, SUBLANES)

    def body(p_ref, g_ref, b_ref, dy_ref, dp_ref, dg_ref, db_ref):
        @pl.when(pl.program_id(0) == 0)
        def _():
            dg_ref[...] = jnp.zeros_like(dg_ref)
            db_ref[...] = jnp.zeros_like(db_ref)

        _, vj = jax.vjp(_ln, p_ref[...], g_ref[...], b_ref[...])
        dp, dg, db = vj(dy_ref[...])
        dp_ref[...] = dp
        dg_ref[...] += _bcast_rows(dg)
        db_ref[...] += _bcast_rows(db)

    blk = pl.BlockSpec((tm, D_MODEL), lambda i: (i, 0))
    vec = pl.BlockSpec((1, D_MODEL), lambda i: (0, 0))
    acc = pl.BlockSpec((SUBLANES, D_MODEL), lambda i: (0, 0))
    return pl.pallas_call(
        body, name="ln_bwd", grid=(S // tm,),
        in_specs=[blk, vec, vec, blk], out_specs=[blk, acc, acc],
        out_shape=[jax.ShapeDtypeStruct((S, D_MODEL), F32)] + [jax.ShapeDtypeStruct((SUBLANES, D_MODEL), F32)] * 2,
        compiler_params=_cparams(10 * _nbytes((tm, D_MODEL), F32), ("arbitrary",)),
    )(pre, g, b, dy)


def _loss_ln_bwd(y, tgt, pre, g, b, *, tm=512):
    S = y.shape[0]
    tm = _tile(S, tm, SUBLANES)

    def body(y_ref, t_ref, p_ref, g_ref, b_ref, dp_ref, dg_ref, db_ref, l_ref):
        @pl.when(pl.program_id(0) == 0)
        def _():
            for r in (dg_ref, db_ref, l_ref):
                r[...] = jnp.zeros_like(r)

        e = y_ref[...] - t_ref[...]
        l_ref[...] += 0.5 * jnp.sum(jnp.mean(e * e, -1, keepdims=True), keepdims=True)
        _, vj = jax.vjp(_ln, p_ref[...], g_ref[...], b_ref[...])
        dp, dg, db = vj(e * (1.0 / D_MODEL))
        dp_ref[...] = dp
        dg_ref[...] += _bcast_rows(dg)
        db_ref[...] += _bcast_rows(db)

    blk = pl.BlockSpec((tm, D_MODEL), lambda i: (i, 0))
    vec = pl.BlockSpec((1, D_MODEL), lambda i: (0, 0))
    acc = pl.BlockSpec((SUBLANES, D_MODEL), lambda i: (0, 0))
    return pl.pallas_call(
        body, name="loss_ln_bwd", grid=(S // tm,),
        in_specs=[blk, blk, blk, vec, vec], out_specs=[blk, acc, acc, pl.BlockSpec((SUBLANES, LANES), lambda i: (0, 0))],
        out_shape=[jax.ShapeDtypeStruct((S, D_MODEL), F32)] + [jax.ShapeDtypeStruct((SUBLANES, D_MODEL), F32)] * 2
                  + [jax.ShapeDtypeStruct((SUBLANES, LANES), F32)],
        compiler_params=_cparams(12 * _nbytes((tm, D_MODEL), F32), ("arbitrary",)),
    )(y, tgt, pre, g, b)


def _layer_fwd(x, xb, w, late):
    pq = _mm(xb, w["win"], mode="nn", name="mm_in_qkv", tm=1024, tn=1024, cols=(0, 3 * D_MODEL))
    proj = _mm(xb, w["win"], mode="nn", name="mm_in_rest", tm=1024, tn=1024, cols=(3 * D_MODEL, 5 * D_MODEL), out_dtype=BF)
    ba = _mm(xb, w["wba"], mode="nn", name="mm_in_ba", tm=1024, tn=LANES)
    qn, kn, vv, gb, conv_out = _qkv_prep(pq, ba, w["convw"], w["arow"], w["dtrow"])
    o, st, tinv, wy_u, wy_w = _delta_fwd(qn, kn, vv, gb)
    ya, yb, phi = _mix_prep(o, proj, w["onw"], w["sg"], w["sb"], w["ws"], w["bst"])
    w = {**w, **late(ya)}
    pa, pb, m = _mm_gate_merge(ya, yb, w["wpa"], w["wpb"], proj)
    pre1, x1, x1b = _mm_resid_ln(m, w["wo"], x, w["ln1g"], w["ln1b"], name="mm_out_ln")
    hg, hu, h = _mm_swiglu(x1b, w["wgt"], w["wut"])
    pre2, x2, x2b = _mm_resid_ln(h, w["wd"], x1, w["ln2g"], w["ln2b"], name="mm_down_ln")
    saved = dict(xb=xb, pq=pq, conv_out=conv_out, proj=proj, phi=phi, ba=ba, qn=qn, kn=kn, vv=vv, gb=gb, o=o, st=st, tinv=tinv, wy_u=wy_u, wy_w=wy_w,
                 ya=ya, yb=yb,
                 pa=pa, pb=pb, m=m, pre1=pre1, x1b=x1b, hg=hg, hu=hu, h=h, pre2=pre2)
    return x2, x2b, saved, w


def _layer_bwd(dpre2, ln2_grads, w, s, on_part=None):
    g = {}
    started = lambda part: on_part(part, g) if on_part is not None else None
    after = lambda v, token: v if token is None else v + token.astype(v.dtype)
    g["ln2g"], g["ln2b"] = ln2_grads
    dhg, dhu = _mm_swiglu_bwd(dpre2, w["wd"], s["hg"], s["hu"])
    g["wd"] = _mm(s["h"], dpre2, mode="tn", name="mm_tn_down", tm=1536, tk=1024, out_dtype=BF)
    dx1 = _mm(dhg, w["wgt"], mode="nn", name="mm_nn_gu", pair=(dhu, w["wut"]), add=dpre2, add_scale=ALPHA, tm=1024, tk=1536)
    g["wgt"] = _mm(dhg, s["x1b"], mode="tn", name="mm_tn_gu", tm=1536, tn=1024, tk=2048, out_dtype=BF)
    g["wut"] = _mm(dhu, s["x1b"], mode="tn", name="mm_tn_gu", tm=1536, tn=1024, tk=2048, out_dtype=BF)
    dpre1, g["ln1g"], g["ln1b"] = _ln_bwd(s["pre1"], w["ln1g"], w["ln1b"], dx1)
    g["wo"] = _mm(s["m"], dpre1, mode="tn", name="mm_tn_sq", tm=1024, tk=1024, out_dtype=BF)
    dpa, dpb, dproj = _mm_gate_merge_bwd(dpre1, w["wo"], s["pa"], s["pb"], s["proj"])
    dya = _mm(dpa, w["wpa"], mode="nt", name="mm_nt_sq")
    g["wpa"] = _mm(s["ya"], dpa, mode="tn", name="mm_tn_sq", tm=1024, tk=1024, out_dtype=BF)
    dyb = _mm(dpb, w["wpb"], mode="nt", name="mm_nt_sq")
    g["wpb"] = _mm(s["yb"], dpb, mode="tn", name="mm_tn_sq", tm=1024, tk=1024, out_dtype=BF)
    do, dproj, g["onw"], g["sg"], g["sb"], g["ws"], g["bst"] = _mix_prep_bwd(
        s["o"], s["proj"], s["phi"], after(w["onw"], started("late")), w["sg"], w["sb"], w["ws"], w["bst"], dya, dyb, dproj)
    dqn, dkn, dvv, dgb = _delta_bwd(s["qn"], s["kn"], s["vv"], s["gb"], s["st"], s["tinv"], s["wy_u"], s["wy_w"], do)
    dc, dba, g["convw"], g["arow"], g["dtrow"] = _qkv_prep_bwd(
        s["pq"], s["conv_out"], s["ba"], w["arow"], w["dtrow"], dqn, dkn, dvv, dgb)
    dproj = _conv_bwd(dc, w["convw"], dproj)
    g["win"] = _mm(s["xb"], dproj, mode="tn", name="mm_tn_in", tm=1024, tn=1024, tk=2048, out_dtype=BF)
    g["wba"] = _mm(s["xb"], dba, mode="tn", name="mm_tn_ba", tm=1024, tn=LANES, tk=1024, out_dtype=BF)
    dx = _mm(dba, after(w["wba"], started("early")), mode="nt", name="mm_nt_ba", add=dpre1, add_scale=ALPHA, tm=1024)
    dx = _mm(dproj, w["win"], mode="nt", name="mm_nt_in", add=dx, add_scale=1.0, tm=1024, tk=2048)
    return dx, g


def _local_step(x, xb, tgt, layers, on_grads=None):
    saved, weights = [], []
    for layer in layers:
        x, xb, s, w = _layer_fwd(x, xb, *layer(x))
        saved.append(s)
        weights.append(w)
    last = len(layers) - 1
    dpre2, dg, db, lacc = _loss_ln_bwd(x, tgt, saved[last]["pre2"], weights[last]["ln2g"], weights[last]["ln2b"])
    grads = [None] * len(layers)
    for l in reversed(range(len(layers))):
        on_part = functools.partial(on_grads, l) if on_grads is not None else None
        dx, grads[l] = _layer_bwd(dpre2, (dg, db), weights[l], saved[l], on_part)
        if l > 0:
            dpre2, dg, db = _ln_bwd(saved[l - 1]["pre2"], weights[l - 1]["ln2g"], weights[l - 1]["ln2b"], dx)
    return lacc[0, 0], dx, grads


_QKVZ = 4 * D_MODEL
_BA = 2 * N_HEADS


WEIGHT_NAMES = ("w_in", "conv_w", "a_log", "dt_bias", "o_norm_w", "sgu_ln_g", "sgu_ln_b", "w_s", "b_s", "w_pa", "w_pb",
                "w_o", "ln1_g", "ln1_b", "w_ffn_gate", "w_ffn_up", "w_ffn_down", "ln2_g", "ln2_b")
WIRE = ("w_in", "w_ffn_gate", "w_ffn_up", "w_ffn_down", "w_pa", "w_pb", "w_o", "conv_w")
SMALL = (("a_log", N_HEADS), ("dt_bias", N_HEADS), ("o_norm_w", D_HEAD), ("sgu_ln_g", D_MODEL), ("sgu_ln_b", D_MODEL),
         ("w_s", SGU_GROUPS * SGU_BLOCK * SGU_BLOCK), ("b_s", SGU_GROUPS * SGU_BLOCK),
         ("ln1_g", D_MODEL), ("ln1_b", D_MODEL), ("ln2_g", D_MODEL), ("ln2_b", D_MODEL))
SMALL_ROWS = -(-sum(n for _, n in SMALL) // (LANES * SUBLANES)) * SUBLANES
N_MAIN_TILES = (N_IN - _BA) // D_MODEL
ADAM_TILES = dict(w_in=(128, "adamw_in"), w_ffn_gate=(32, "adamw_ffn_rows"), w_ffn_up=(32, "adamw_ffn_rows"),
                  w_ffn_down=(32, "adamw_ffn_rows"), w_pa=(128, "adamw_sq"), w_pb=(128, "adamw_sq"), w_o=(128, "adamw_sq"),
                  conv_w=(CONV_K, "adamw_conv"))


def _pad_to(a, axis, size):
    pads = [(0, 0)] * a.ndim
    pads[axis] = (0, size - a.shape[axis])
    return jnp.pad(a, pads)


def _t(a):
    return jnp.swapaxes(a, 1, 2)


def _wire_blocks(p):
    return dict(
        w_in=_pad_to(p["w_in"].astype(BF), 2, IN_PAD),
        w_ffn_gate=_pad_to(_t(p["w_ffn_gate"]).astype(BF), 1, FFN_PAD), w_ffn_up=_pad_to(_t(p["w_ffn_up"]).astype(BF), 1, FFN_PAD),
        w_ffn_down=_pad_to(p["w_ffn_down"].astype(BF), 1, FFN_PAD),
        w_pa=p["w_pa"].astype(BF), w_pb=p["w_pb"].astype(BF), w_o=p["w_o"].astype(BF),
        conv_w=_pad_to(p["conv_w"], 1, SUBLANES),
    )


def _by_columns(blocks):
    n, r, c = blocks.shape
    return jnp.transpose(blocks, (1, 0, 2)).reshape(r, n * c)


def _to_slots(full, c):
    r = full.shape[0]
    return jnp.transpose(full.reshape(r, N_DEV, c), (1, 0, 2))


def _lane_row(v, at):
    return jnp.pad(v[None], ((0, 0), (at, LANES - at - v.shape[0])))


TRANSPOSED = ("w_ffn_gate", "w_ffn_up")
EARLY = ("w_in", "conv_w")
LATE = ("w_pa", "w_pb", "w_o", "w_ffn_gate", "w_ffn_up", "w_ffn_down")


def _early_weights(stacks, p, l):
    return dict(
        win=_perm_in(stacks["w_in"], D_MODEL, N_MAIN_TILES), wba=_perm_in(stacks["w_in"], LANES, 1),
        convw=_by_columns(stacks["conv_w"][:, :CONV_K]),
        arow=_lane_row(p["a_log"][l], N_HEADS), dtrow=_lane_row(p["dt_bias"][l], N_HEADS),
        onw=p["o_norm_w"][l][None], sg=p["sgu_ln_g"][l][None], sb=p["sgu_ln_b"][l][None],
        ws=p["w_s"][l], bst=_pad_to(p["b_s"][l].T, 1, LANES),
        ln1g=p["ln1_g"][l][None], ln1b=p["ln1_b"][l][None], ln2g=p["ln2_g"][l][None], ln2b=p["ln2_b"][l][None],
    )


def _late_weights(stacks):
    return dict(
        wpa=stacks["w_pa"].reshape(D_MODEL, D_MODEL), wpb=stacks["w_pb"].reshape(D_MODEL, D_MODEL),
        wo=stacks["w_o"].reshape(D_MODEL, D_MODEL),
        wgt=stacks["w_ffn_gate"].reshape(FFN_K, D_MODEL), wut=stacks["w_ffn_up"].reshape(FFN_K, D_MODEL),
        wd=stacks["w_ffn_down"].reshape(FFN_K, D_MODEL),
    )


def _small_pack(parts):
    flat = jnp.concatenate([parts[n].reshape(-1) for n, _ in SMALL])
    return _pad_to(flat, 0, SMALL_ROWS * LANES).reshape(SMALL_ROWS, LANES)


def _small_unpack(rows, like):
    flat, out, off = rows.reshape(-1), {}, 0
    for n, size in SMALL:
        out[n] = flat[off:off + size].reshape(like[n].shape[1:])
        off += size
    return out


def _late_slots(g):
    slots = dict(
        w_ffn_gate=g["wgt"].reshape(N_DEV, FFN_PAD, D_MODEL), w_ffn_up=g["wut"].reshape(N_DEV, FFN_PAD, D_MODEL),
        w_ffn_down=g["wd"].reshape(N_DEV, FFN_PAD, D_MODEL),
        w_pa=g["wpa"].reshape(N_DEV, D_MODEL // N_DEV, D_MODEL), w_pb=g["wpb"].reshape(N_DEV, D_MODEL // N_DEV, D_MODEL),
        w_o=g["wo"].reshape(N_DEV, D_MODEL // N_DEV, D_MODEL),
    )
    return [slots[n] for n in LATE]


def _early_slots(g):
    slots = [_perm_out(g["win"], g["wba"]), _pad_to(_to_slots(g["convw"][:CONV_K], 3 * D_MODEL // N_DEV), 1, SUBLANES)]
    small = _small_pack(dict(
        a_log=g["arow"][0, N_HEADS:2 * N_HEADS], dt_bias=g["dtrow"][0, N_HEADS:2 * N_HEADS], o_norm_w=g["onw"][0],
        sgu_ln_g=g["sg"][0], sgu_ln_b=g["sb"][0], w_s=g["ws"], b_s=g["bst"][:, :SGU_GROUPS].T,
        ln1_g=g["ln1g"][0], ln1_b=g["ln1b"][0], ln2_g=g["ln2g"][0], ln2_b=g["ln2b"][0]))
    return slots, small


def _in_tile_start(j, tile_w):
    if tile_w == LANES:
        return jnp.int32(_QKVZ)
    return j * D_MODEL + jnp.where(j >= _QKVZ // D_MODEL, _BA, 0)


def _select(rows_iota, cols_iota, dev, start, valid):
    hit = (rows_iota + (dev * IN_BLOCK - start) == cols_iota) & (rows_iota < IN_BLOCK) & (cols_iota < valid)
    return jnp.where(hit, 1.0, 0.0).astype(BF)


def _perm_in(stack, tile_w, n_tiles):
    valid = _BA if tile_w == LANES else tile_w

    def first_dev(j):
        return lax.div(_in_tile_start(j, tile_w), jnp.int32(IN_BLOCK))

    def body(w_ref, o_ref, acc_ref):
        j, k = pl.program_id(0), pl.program_id(1)
        sel = _select(_iota((IN_PAD, tile_w), 0), _iota((IN_PAD, tile_w), 1), first_dev(j) + k,
                      _in_tile_start(j, tile_w), valid)
        part = jnp.dot(w_ref[0], sel, preferred_element_type=F32)

        @pl.when(k == 0)
        def _():
            acc_ref[...] = part

        @pl.when(k == 1)
        def _():
            o_ref[...] = (acc_ref[...] + part).astype(BF)

    est = _nbytes((D_MODEL, IN_PAD), BF) + 3 * _nbytes((D_MODEL, tile_w), F32) + 2 * _nbytes((IN_PAD, tile_w), F32)
    return pl.pallas_call(
        body, name="perm_in" if tile_w != LANES else "perm_in_ba", grid=(n_tiles, 2),
        in_specs=[pl.BlockSpec((1, D_MODEL, IN_PAD), lambda j, k: (jnp.minimum(first_dev(j) + k, N_DEV - 1), 0, 0))],
        out_specs=pl.BlockSpec((D_MODEL, tile_w), lambda j, k: (0, j)),
        out_shape=jax.ShapeDtypeStruct((D_MODEL, n_tiles * tile_w), BF),
        scratch_shapes=[pltpu.VMEM((D_MODEL, tile_w), F32)],
        compiler_params=_cparams(est, ("parallel", "arbitrary")),
    )(_in_hbm(stack))


def _perm_out(dmain, dba):
    def tile(d, s):
        c0 = d * IN_BLOCK
        first = lax.div(c0 - jnp.where(c0 < _QKVZ, 0, jnp.minimum(c0 - _QKVZ, _BA)), jnp.int32(D_MODEL))
        return jnp.minimum(first + jnp.minimum(s, 1), N_MAIN_TILES - 1)

    def body(dm_ref, db_ref, o_ref, acc_ref):
        d, s = pl.program_id(0), pl.program_id(1)

        @pl.when(s == 0)
        def _():
            acc_ref[...] = jnp.zeros_like(acc_ref)

        start = _in_tile_start(tile(d, s), D_MODEL)
        overlaps = (start < (d + 1) * IN_BLOCK) & (d * IN_BLOCK < start + D_MODEL)

        @pl.when((s < 2) & overlaps)
        def _():
            sel = _select(_iota((D_MODEL, IN_PAD), 1), _iota((D_MODEL, IN_PAD), 0), d, start, D_MODEL)
            acc_ref[...] += jnp.dot(dm_ref[...], sel, preferred_element_type=F32)

        @pl.when(s == 2)
        def _():
            sel = _select(_iota((LANES, IN_PAD), 1), _iota((LANES, IN_PAD), 0), d, jnp.int32(_QKVZ), _BA)
            o_ref[0] = (acc_ref[...] + jnp.dot(db_ref[...], sel, preferred_element_type=F32)).astype(BF)

    est = 2 * _nbytes((D_MODEL, D_MODEL), BF) + 4 * _nbytes((D_MODEL, IN_PAD), F32)
    return pl.pallas_call(
        body, name="perm_out", grid=(N_DEV, 3),
        in_specs=[pl.BlockSpec((D_MODEL, D_MODEL), lambda d, s: (0, tile(d, s))),
                  pl.BlockSpec((D_MODEL, LANES), lambda d, s: (0, 0))],
        out_specs=pl.BlockSpec((1, D_MODEL, IN_PAD), lambda d, t: (d, 0, 0)),
        out_shape=jax.ShapeDtypeStruct((N_DEV, D_MODEL, IN_PAD), BF),
        scratch_shapes=[pltpu.VMEM((D_MODEL, IN_PAD), F32)],
        compiler_params=_cparams(est, ("parallel", "arbitrary")),
    )(dmain, dba)


def _mesh_place():
    x, y, c = (lax.axis_index(a) for a in MESH_AXES)
    return x, y, c


def _slot(x, y, c):
    return 4 * x + 2 * y + c


def _peer(place, j):
    x, y, c = place
    return (1 - x if j & 4 else x, 1 - y if j & 2 else y, 1 - c if j & 1 else c)


_HBM = pl.BlockSpec(memory_space=pltpu.HBM)
_SEM = pl.BlockSpec(memory_space=pltpu.SEMAPHORE)
_EFFECT = pltpu.SideEffectType.DATAFLOW_SIDE_EFFECTING


def _remote_copy(src_ref, land_ref, slot, per_slot, pslot, sems, u, j, peer):
    return pltpu.make_async_remote_copy(
        src_ref=src_ref.at[pslot] if per_slot else src_ref, dst_ref=land_ref.at[slot],
        send_sem=sems[0].at[u * (N_DEV - 1) + j - 1], recv_sem=sems[1].at[u * (N_DEV - 1) + j - 1],
        device_id=peer, device_id_type=pl.DeviceIdType.MESH)


def _own_copy(src_ref, land_ref, me, per_slot, sems, u):
    return pltpu.make_async_copy(src_ref.at[me] if per_slot else src_ref, land_ref.at[me], sems[2].at[u])


def _exchange_start(name, srcs, per_slot):
    n = len(srcs)
    lands = [jax.ShapeDtypeStruct(s.shape if p else (N_DEV,) + s.shape, s.dtype) for s, p in zip(srcs, per_slot)]

    def body(*refs):
        src_refs, sems, land_refs, token = refs[:n], refs[n:n + 3], refs[2 * n + 3:3 * n + 3], refs[-1]
        place = _mesh_place()
        me = _slot(*place)
        for u in range(n):
            _own_copy(src_refs[u], land_refs[u], me, per_slot[u], sems, u).start()
            for j in range(1, N_DEV):
                peer = _peer(place, j)
                _remote_copy(src_refs[u], land_refs[u], me, per_slot[u], _slot(*peer), sems, u, j, peer).start()
        token[...] = jnp.zeros_like(token)

    hbm = lambda a: pltpu.HBM(a.shape, a.dtype)
    sem = pltpu.SemaphoreType.DMA((n * (N_DEV - 1),))
    outs = pl.pallas_call(
        body, name=name,
        out_shape=(sem, sem, pltpu.SemaphoreType.DMA((n,)), *[hbm(a) for a in srcs], *[hbm(a) for a in lands],
                   jax.ShapeDtypeStruct((SUBLANES, LANES), F32)),
        in_specs=[_HBM] * n, out_specs=(_SEM, _SEM, _SEM, *[_HBM] * (2 * n), pl.BlockSpec(memory_space=pltpu.VMEM)),
        input_output_aliases={i: 3 + i for i in range(n)},
        compiler_params=pltpu.CompilerParams(has_side_effects=_EFFECT),
    )(*[pltpu.with_memory_space_constraint(a, pltpu.HBM) for a in srcs])
    return tuple(outs[:3]), list(outs[3:3 + n]), list(outs[3 + n:3 + 2 * n]), outs[-1]


def _exchange_wait(name, sems, srcs, lands, units, per_slot, after):
    m = len(units)
    after = list(after) if isinstance(after, (list, tuple)) else [after]

    def body(*refs):
        src_refs, land_refs, sem_refs = refs[:m], refs[m:2 * m], refs[2 * m:2 * m + 3]
        place = _mesh_place()
        me = _slot(*place)
        for i, u in enumerate(units):
            _own_copy(src_refs[i], land_refs[i], me, per_slot[u], sem_refs, u).wait()
            for j in range(1, N_DEV):
                peer = _peer(place, j)
                pslot = _slot(*peer)
                cp = _remote_copy(src_refs[i], land_refs[i], pslot, per_slot[u], pslot, sem_refs, u, j, peer)
                cp.wait_send()
                cp.wait_recv()

    hbm = lambda a: pltpu.HBM(a.shape, a.dtype)
    outs = pl.pallas_call(
        body, name=name, out_shape=tuple(hbm(a) for a in list(srcs) + list(lands)),
        in_specs=[_HBM] * (2 * m) + [_SEM] * 3 + [pl.BlockSpec(memory_space=pl.ANY)] * len(after),
        out_specs=tuple([_HBM] * (2 * m)),
        input_output_aliases={i: i for i in range(2 * m)},
        compiler_params=pltpu.CompilerParams(has_side_effects=_EFFECT),
    )(*srcs, *lands, *sems, *after)
    return list(outs[m:])


def _adam_update(g, w, m, v):
    m = ADAM_B1 * m + (1.0 - ADAM_B1) * g
    v = ADAM_B2 * v + (1.0 - ADAM_B2) * jnp.square(g)
    m_hat = m / (1.0 - ADAM_B1 ** ADAM_STEP)
    v_hat = v / (1.0 - ADAM_B2 ** ADAM_STEP)
    return -ADAM_LR * (m_hat / (jnp.sqrt(v_hat) + ADAM_EPS) + ADAM_WD * w), m, v


def _adamw(recvs, w, m, v, *, tr, name):
    L, R, C = w.shape
    rp = max(tr, SUBLANES * (4 // jnp.dtype(recvs[0].dtype).itemsize))
    Cp = recvs[0].shape[2]

    def body(*refs):
        r_refs, (w_ref, m_ref, v_ref, g_ref, d_ref, nm_ref, nv_ref) = refs[:L], refs[L:]
        for l in range(L):
            @pl.when(pl.program_id(0) == l)
            def _(r_ref=r_refs[l]):
                g = r_ref[0, :tr, :C].astype(F32)
                for s in range(1, N_DEV):
                    g = g + r_ref[s, :tr, :C].astype(F32)
                d, nm, nv = _adam_update(g, w_ref[0], m_ref[0], v_ref[0])
                g_ref[0], d_ref[0], nm_ref[0], nv_ref[0] = g, d, nm, nv

    blk = pl.BlockSpec((1, tr, C), lambda l, i: (l, i, 0))
    r_specs = [pl.BlockSpec((N_DEV, rp, Cp), lambda l, i, k=k: (0, jnp.where(l == k, i, 0), 0)) for k in range(L)]
    est = 2 * _nbytes((N_DEV, rp, Cp), recvs[0].dtype) + 8 * _nbytes((tr, Cp), F32)
    return pl.pallas_call(
        body, name=name, grid=(L, R // tr),
        in_specs=r_specs + [blk] * 3, out_specs=[blk] * 4,
        out_shape=[jax.ShapeDtypeStruct((L, R, C), F32)] * 4,
        compiler_params=_cparams(est, ("arbitrary", "arbitrary")),
    )(*recvs, w, m, v)


def _adamw_small(recv, w, m, v):
    def body(r_ref, w_ref, m_ref, v_ref, g_ref, d_ref, nm_ref, nv_ref):
        g = r_ref[0]
        for s in range(1, N_DEV):
            g = g + r_ref[s]
        g_ref[...] = g
        d_ref[...], nm_ref[...], nv_ref[...] = _adam_update(g, w_ref[...], m_ref[...], v_ref[...])

    vm = pl.BlockSpec(memory_space=pltpu.VMEM)
    return pl.pallas_call(
        body, name="adamw_small", in_specs=[vm] * 4, out_specs=[vm] * 4,
        out_shape=[jax.ShapeDtypeStruct((SMALL_ROWS, LANES), F32)] * 4,
        compiler_params=_cparams(20 * _nbytes((SMALL_ROWS, LANES), F32)),
    )(recv, w, m, v)


def kernel(x, w_in, conv_w, a_log, dt_bias, o_norm_w, sgu_ln_g, sgu_ln_b, w_s, b_s, w_pa, w_pb, w_o, ln1_g, ln1_b, w_ffn_gate, w_ffn_up, w_ffn_down, ln2_g, ln2_b, loss_target, m_w_in, m_conv_w, m_a_log, m_dt_bias, m_o_norm_w, m_sgu_ln_g, m_sgu_ln_b, m_w_s, m_b_s, m_w_pa, m_w_pb, m_w_o, m_ln1_g, m_ln1_b, m_w_ffn_gate, m_w_ffn_up, m_w_ffn_down, m_ln2_g, m_ln2_b, v_w_in, v_conv_w, v_a_log, v_dt_bias, v_o_norm_w, v_sgu_ln_g, v_sgu_ln_b, v_w_s, v_b_s, v_w_pa, v_w_pb, v_w_o, v_ln1_g, v_ln1_b, v_w_ffn_gate, v_w_ffn_up, v_w_ffn_down, v_ln2_g, v_ln2_b):
    given = dict(locals())
    P = {n: given[n] for n in WEIGHT_NAMES}
    M = {n: given["m_" + n] for n in WEIGHT_NAMES}
    V = {n: given["v_" + n] for n in WEIGHT_NAMES}

    wire = _wire_blocks(P)
    units = [(n, l) for l in range(DEPTH) for n in EARLY + LATE]
    whole = [False] * len(units)
    g_sems, g_srcs, g_lands, g_token = _exchange_start("gather_start", [wire[n][l] for n, l in units], whole)

    one = 1.0 + g_token[0, 0]
    xb = (x[0] * one).astype(BF)
    small_in = [[_small_pack({n: T[n][l] * one for n, _ in SMALL}) for T in (P, M, V)] for l in range(DEPTH)]
    adam_in = {n: (P[n], M[n], V[n]) for n in WIRE}
    adam_in["w_in"], _ = lax.optimization_barrier((adam_in["w_in"], g_token))
    prepared = [xb, *[a for packs in small_in for a in packs], *adam_in["w_in"]]

    def gathered(name, names, l, after):
        idx = [units.index((n, l)) for n in names]
        got = _exchange_wait(name, g_sems, [g_srcs[i] for i in idx], [g_lands[i] for i in idx], idx, whole, after)
        return dict(zip(names, got))

    def layer(l):
        def weights(x_in):
            after = prepared if l == 0 else x_in
            early = _early_weights(gathered(f"gather_wait_early{l}", EARLY, l, after), P, l)
            return early, lambda ya: _late_weights(gathered(f"gather_wait_late{l}", LATE, l, ya))
        return weights

    pending = {}

    def on_grads(l, part, g):
        if part == "late":
            srcs, names = _late_slots(g), LATE
            per_slot = [True] * len(srcs)
        else:
            slots, small = _early_slots(g)
            srcs, names = slots + [small], EARLY + ("small",)
            per_slot = [True] * len(slots) + [False]
        sems, s_thru, l_thru, token = _exchange_start(f"exchange_start_{part}{l}", srcs, per_slot)
        pending[l, part] = (names, sems, s_thru, l_thru, per_slot)
        return token[0, 0]

    loss_local, dx, _ = _local_step(x[0], xb, loss_target[0], [layer(l) for l in range(DEPTH)], on_grads)
    loss = lax.psum(loss_local, MESH_AXES)

    recv = [{} for _ in range(DEPTH)]

    def received(l, part, after):
        names, sems, s_thru, l_thru, per_slot = pending[l, part]
        got = _exchange_wait(f"exchange_wait_{part}{l}", sems, s_thru, l_thru, list(range(len(s_thru))), per_slot, after)
        recv[l].update(zip(names, got))

    out = {}

    def adamw(names):
        for n in names:
            tr, name = ADAM_TILES[n]
            view = _t if n in TRANSPOSED else (lambda a: a)
            res = _adamw([recv[l][n] for l in range(DEPTH)], *[view(a) for a in adam_in[n]], tr=tr, name=name)
            out[n] = [view(r) for r in res]

    for l in reversed(range(DEPTH)):
        received(l, "late", dx)
    adamw(LATE)
    for l in reversed(range(DEPTH)):
        received(l, "early", out[LATE[-1]][0])
    adamw(EARLY)
    small = [_adamw_small(recv[l]["small"], *small_in[l]) for l in range(DEPTH)]
    for n, _ in SMALL:
        out[n] = [jnp.stack([_small_unpack(small[l][i], P)[n] for l in range(DEPTH)]) for i in range(4)]
    return (loss, dx[None], *[out[n][i] for i in range(4) for n in WEIGHT_NAMES])
```

```python
import functools
import math

import jax
import jax.numpy as jnp
from jax import lax
from jax.experimental import pallas as pl
from jax.experimental.pallas import tpu as pltpu

F32 = jnp.float32
BF = jnp.bfloat16
HIGHEST = lax.Precision.HIGHEST

D_MODEL = 1024
DEPTH = 2
N_HEADS = 8
D_HEAD = 128
CONV_K = 4
SGU_BLOCK = 128
SGU_GROUPS = 8
SGU_CHUNK = 64
FFN_HIDDEN = 2816
N_IN = 8208
N_DEV = 8
IN_BLOCK, IN_PAD = N_IN // N_DEV, 1152
FFN_BLOCK, FFN_PAD = FFN_HIDDEN // N_DEV, 384
FFN_K = N_DEV * FFN_PAD
ALPHA = (2 * DEPTH) ** 0.25
LN_EPS = 1e-5
RMS_EPS = 1e-6
ADAM_LR, ADAM_B1, ADAM_B2, ADAM_EPS, ADAM_WD, ADAM_STEP = 0.001, 0.9, 0.999, 1e-08, 0.01, 10

MESH_AXES = ("x", "y", "c")
DELTA_CHUNK = 128
DELTA_HEADS_PER_STEP = 8
LANES = 128
SUBLANES = 8
VMEM_BYTES = 64 * 1024 * 1024
HALO = SUBLANES
HALO_BF = 2 * SUBLANES


def _cparams(est_bytes, dims=None):
    limit = int(min(max(2 * est_bytes + (8 << 20), 32 << 20), VMEM_BYTES - (6 << 20)))
    kw = dict(vmem_limit_bytes=limit)
    if dims is not None:
        kw["dimension_semantics"] = dims
    return pltpu.CompilerParams(**kw)


def _nbytes(shape, dtype):
    return math.prod(shape) * jnp.dtype(dtype).itemsize


def _dims(kind, ndim):
    lhs, rhs = {"nn": (1, 0), "nt": (1, 1), "tn": (0, 0)}[kind]
    b = ndim - 2
    return (((lhs + b,), (rhs + b,)), (tuple(range(b)), tuple(range(b))))


def _mxu(a, b, kind):
    return lax.dot_general(a, b, _dims(kind, a.ndim), preferred_element_type=F32)


def _dot(a, b):
    return _mxu(a.astype(BF), b.astype(BF), "nn")


def _dot_nt(a, b):
    return _mxu(a.astype(BF), b.astype(BF), "nt")


def _dot_tn(a, b):
    return _mxu(a.astype(BF), b.astype(BF), "tn")


def _split(a):
    hi = a.astype(BF)
    return hi, (a - hi.astype(F32)).astype(BF)


def _dot3(a, b, kind):
    (ah, al), (bh, bl) = _split(a), _split(b)
    return _mxu(ah, bh, kind) + (_mxu(ah, bl, kind) + _mxu(al, bh, kind))


def _dotf(a, b):
    return _dot3(a, b, "nn")


def _dotf_nt(a, b):
    return _dot3(a, b, "nt")


def _dot01(sel, x, kind="nn"):
    s = jnp.broadcast_to(sel.astype(BF), x.shape[:-2] + sel.shape)
    h1 = x.astype(BF)
    r1 = x - h1.astype(F32)
    h2 = r1.astype(BF)
    h3 = (r1 - h2.astype(F32)).astype(BF)
    return _mxu(s, h1, kind) + (_mxu(s, h2, kind) + _mxu(s, h3, kind))


def _sigmoid(x):
    return 0.5 * jnp.tanh(0.5 * x) + 0.5


def _silu(x):
    return x * _sigmoid(x)


def _silu_and_grad(x):
    s = _sigmoid(x)
    return x * s, s * (1.0 + x * (1.0 - s))


def _softplus(x):
    return jnp.maximum(x, 0.0) + jnp.log1p(jnp.exp(-jnp.abs(x)))


def _ln(x, g, b):
    mu = jnp.mean(x, -1, keepdims=True)
    xc = x - mu
    var = jnp.mean(xc * xc, -1, keepdims=True)
    return xc * lax.rsqrt(var + LN_EPS) * g + b


def _iota(shape, dim):
    return lax.broadcasted_iota(jnp.int32, shape, dim)


def _tile(n, pref, align):
    if n <= pref:
        return n
    t = (pref // align) * align
    while t >= align:
        if n % t == 0:
            return t
        t -= align
    raise ValueError(f"no tile for {n} (pref {pref}, align {align})")


def _bcast_rows(v, rows=SUBLANES):
    return jnp.broadcast_to(v, (rows, v.shape[-1]))


def _in_hbm(t):
    return pltpu.with_memory_space_constraint(t, pltpu.HBM)


def _mm(a, b, *, mode, name, out_dtype=F32, add=None, add_scale=1.0, tm=512, tn=1024, tk=1024, cols=None, pair=None):
    if mode == "nn":
        (M, K), N = a.shape, b.shape[1]
    elif mode == "nt":
        (M, K), N = a.shape, b.shape[0]
    else:
        (K, M), N = a.shape, b.shape[1]
    col0 = 0
    if cols is not None:
        col0, N = cols
    tm = _tile(M, tm, LANES if mode == "tn" else SUBLANES * 2)
    tn = _tile(N, tn, LANES)
    tk = _tile(K, tk, LANES)
    nk = K // tk
    j0 = col0 // tn
    if mode == "nn":
        a_spec = pl.BlockSpec((tm, tk), lambda i, j, k: (i, k))
        b_spec = pl.BlockSpec((tk, tn), lambda i, j, k: (k, j + j0))
        dot = _dot
    elif mode == "nt":
        a_spec = pl.BlockSpec((tm, tk), lambda i, j, k: (i, k))
        b_spec = pl.BlockSpec((tn, tk), lambda i, j, k: (j, k))
        dot = _dot_nt
    else:
        a_spec = pl.BlockSpec((tk, tm), lambda i, j, k: (k, i))
        b_spec = pl.BlockSpec((tk, tn), lambda i, j, k: (k, j))
        dot = _dot_tn
    o_spec = pl.BlockSpec((tm, tn), lambda i, j, k: (i, j))
    has_add = add is not None

    n_ab = 2 if pair is None else 4

    def body(*refs):
        ab, (o_ref, acc_ref) = refs[:n_ab], refs[-2:]
        add_ref = refs[n_ab] if has_add else None
        k = pl.program_id(2)
        part = dot(ab[0][...], ab[1][...])
        if pair is not None:
            part = part + dot(ab[2][...], ab[3][...])

        def finish(total):
            if has_add:
                total = total + add_scale * add_ref[...]
            o_ref[...] = total.astype(out_dtype)

        if nk == 1:
            finish(part)
        else:
            @pl.when(k == 0)
            def _():
                acc_ref[...] = part

            @pl.when(jnp.logical_and(k > 0, k < nk - 1))
            def _():
                acc_ref[...] += part

            @pl.when(k == nk - 1)
            def _():
                finish(acc_ref[...] + part)

    in_specs = [a_spec, b_spec] * (n_ab // 2) + ([o_spec] if has_add else [])
    args = (a, b) + (tuple(pair) if pair is not None else ()) + ((add,) if has_add else ())
    est = ((n_ab // 2) * (_nbytes((tm, tk), a.dtype) + _nbytes((tk, tn), b.dtype)) + 2 * _nbytes((tm, tn), F32)
           + (_nbytes((tm, tn), F32) if has_add else 0)) + 2 * _nbytes((tm, tn), F32)
    return pl.pallas_call(
        body, name=name,
        grid=(M // tm, N // tn, nk),
        in_specs=in_specs, out_specs=o_spec,
        out_shape=jax.ShapeDtypeStruct((M, N), out_dtype),
        scratch_shapes=[pltpu.VMEM((tm, tn) if nk > 1 else (SUBLANES, LANES), F32)],
        compiler_params=_cparams(est, ("parallel", "parallel", "arbitrary")),
    )(*[_in_hbm(t) for t in args])


def _shifted(xt, halo, first):
    halo = jnp.where(first, 0.0, halo)
    xc = jnp.concatenate([halo, xt], axis=0)
    return [xt] + [pltpu.roll(xc, s, 0)[HALO:] for s in range(1, CONV_K)]


def _conv_taps(shifted, w_ref):
    out = shifted[0] * w_ref[CONV_K - 1:CONV_K, :]
    for s in range(1, CONV_K):
        out = out + shifted[s] * w_ref[CONV_K - 1 - s:CONV_K - s, :]
    return out


def _gates(ba, arow, dtrow):
    lane = _iota(ba.shape, 1)
    beta = _sigmoid(ba)
    g = -jnp.exp(arow) * _softplus(ba + dtrow)
    return jnp.where(lane < N_HEADS, beta, jnp.where(lane < 2 * N_HEADS, g, 0.0))


def _l2n(x):
    return x * lax.rsqrt(jnp.sum(x * x, -1, keepdims=True) + RMS_EPS)


def _qkv_prep(proj, ba, convw, arow, dtrow, *, tm=256):
    S = proj.shape[0]
    tm = _tile(S, tm, SUBLANES)
    W3 = 3 * D_MODEL
    hb = tm // HALO

    def body(xt_ref, halo_ref, ba_ref, w_ref, a_ref, dt_ref, q_ref, k_ref, v_ref, gb_ref, c_ref):
        c = _conv_taps(_shifted(xt_ref[...], halo_ref[...], pl.program_id(0) == 0), w_ref)
        c_ref[...] = c
        c = _silu(c)
        for h in range(N_HEADS):
            lo = h * D_HEAD
            q_ref[:, lo:lo + D_HEAD] = _l2n(c[:, lo:lo + D_HEAD])
            k_ref[:, lo:lo + D_HEAD] = _l2n(c[:, D_MODEL + lo:D_MODEL + lo + D_HEAD])
        v_ref[...] = c[:, 2 * D_MODEL:]
        gb_ref[...] = _gates(ba_ref[...], a_ref[...], dt_ref[...])

    row = lambda w, col=0: pl.BlockSpec((tm, w), lambda i: (i, col))
    full = lambda shape: pl.BlockSpec(shape, lambda i: (0,) * len(shape))
    est = 4 * _nbytes((tm, W3), F32)
    return pl.pallas_call(
        body, name="qkv_prep", grid=(S // tm,),
        in_specs=[row(W3), pl.BlockSpec((HALO, W3), lambda i: (jnp.maximum(i * hb - 1, 0), 0)), row(LANES),
                  full((CONV_K, W3)), full((1, LANES)), full((1, LANES))],
        out_specs=[row(D_MODEL), row(D_MODEL), row(D_MODEL), row(LANES), row(W3)],
        out_shape=[jax.ShapeDtypeStruct((S, D_MODEL), F32)] * 3 + [jax.ShapeDtypeStruct((S, LANES), F32),
                                                                   jax.ShapeDtypeStruct((S, W3), F32)],
        compiler_params=_cparams(est, ("arbitrary",)),
    )(proj, proj, ba, convw, arow, dtrow)


def _qkv_prep_bwd(proj, conv_out, ba, arow, dtrow, dq, dk, dv, dgb, *, tm=256):
    S = proj.shape[0]
    tm = _tile(S, tm, SUBLANES * 2)
    W3 = 3 * D_MODEL
    hb = tm // HALO

    def body(xt_ref, halo_ref, c_ref, ba_ref, a_ref, dt_ref, dq_ref, dk_ref, dv_ref, dgb_ref,
             dcb_ref, dba_ref, dw_ref, da_ref, ddt_ref, dc_ref):
        i = pl.program_id(0)

        @pl.when(i == 0)
        def _():
            dw_ref[...] = jnp.zeros_like(dw_ref)
            da_ref[...] = jnp.zeros_like(da_ref)
            ddt_ref[...] = jnp.zeros_like(ddt_ref)

        shifted = _shifted(xt_ref[...], halo_ref[...], i == 0)
        a, ds = _silu_and_grad(c_ref[...])
        for h in range(N_HEADS):
            for base, d_ref in ((0, dq_ref), (D_MODEL, dk_ref)):
                lo = base + h * D_HEAD
                _, vj = jax.vjp(_l2n, a[:, lo:lo + D_HEAD])
                (dx,) = vj(d_ref[:, h * D_HEAD:(h + 1) * D_HEAD])
                dc_ref[:, lo:lo + D_HEAD] = dx * ds[:, lo:lo + D_HEAD]
        dc_ref[:, 2 * D_MODEL:] = dv_ref[...] * ds[:, 2 * D_MODEL:]
        dc = dc_ref[...]
        dcb_ref[...] = dc.astype(BF)
        for s in range(CONV_K):
            kk = CONV_K - 1 - s
            dw_ref[kk:kk + 1, :] += jnp.sum(dc * shifted[s], axis=0, keepdims=True)
        _, vj = jax.vjp(_gates, ba_ref[...], a_ref[...], dt_ref[...])
        dba, da, ddt = vj(dgb_ref[...])
        dba_ref[...] = dba.astype(BF)
        da_ref[...] += _bcast_rows(da)
        ddt_ref[...] += _bcast_rows(ddt)

    row = lambda w, col=0: pl.BlockSpec((tm, w), lambda i: (i, col))
    full = lambda shape: pl.BlockSpec(shape, lambda i: (0,) * len(shape))
    est = 8 * _nbytes((tm, W3), F32)
    return pl.pallas_call(
        body, name="qkv_prep_bwd", grid=(S // tm,),
        in_specs=[row(W3), pl.BlockSpec((HALO, W3), lambda i: (jnp.maximum(i * hb - 1, 0), 0)), row(W3), row(LANES),
                  full((1, LANES)), full((1, LANES)),
                  row(D_MODEL), row(D_MODEL), row(D_MODEL), row(LANES)],
        out_specs=[row(W3), row(LANES), full((SUBLANES, W3)), full((SUBLANES, LANES)), full((SUBLANES, LANES))],
        out_shape=[jax.ShapeDtypeStruct((S, W3), BF), jax.ShapeDtypeStruct((S, LANES), BF),
                   jax.ShapeDtypeStruct((SUBLANES, W3), F32), jax.ShapeDtypeStruct((SUBLANES, LANES), F32),
                   jax.ShapeDtypeStruct((SUBLANES, LANES), F32)],
        scratch_shapes=[pltpu.VMEM((tm, W3), F32)],
        compiler_params=_cparams(est, ("arbitrary",)),
    )(proj, proj, conv_out, ba, arow, dtrow, dq, dk, dv, dgb)


def _conv_bwd(dc, convw, dproj, *, tm=256):
    S, W3 = dc.shape
    tm = _tile(S, tm, HALO_BF)
    hb = tm // HALO_BF
    nt = S // tm

    def body(dc_ref, nxt_ref, w_ref, dproj_ref, o_ref):
        last = pl.program_id(0) == nt - 1
        nxt = jnp.where(last, 0.0, nxt_ref[...].astype(F32))
        cur = dc_ref[...].astype(F32)
        xc = jnp.concatenate([cur, nxt], axis=0)
        out = cur * w_ref[CONV_K - 1:CONV_K, :]
        for s in range(1, CONV_K):
            out = out + pltpu.roll(xc, tm + HALO_BF - s, 0)[:tm] * w_ref[CONV_K - 1 - s:CONV_K - s, :]
        o_ref[...] = out.astype(BF)

    est = 5 * _nbytes((tm, W3), F32)
    return pl.pallas_call(
        body, name="conv_bwd", grid=(nt,),
        in_specs=[pl.BlockSpec((tm, W3), lambda i: (i, 0)),
                  pl.BlockSpec((HALO_BF, W3), lambda i: (jnp.minimum((i + 1) * hb, S // HALO_BF - 1), 0)),
                  pl.BlockSpec((CONV_K, W3), lambda i: (0, 0)), pl.BlockSpec(memory_space=pl.ANY)],
        out_specs=pl.BlockSpec((tm, W3), lambda i: (i, 0)),
        out_shape=jax.ShapeDtypeStruct(dproj.shape, BF),
        input_output_aliases={3: 0},
        compiler_params=_cparams(est, ("parallel",)),
    )(dc, dc, convw, dproj)


NEUMANN_BLOCK = 8


def _inv_unit_lower(A):
    C = A.shape[-1]
    row, col = _iota((C, C), 0), _iota((C, C), 1)
    eye = jnp.where(row == col, 1.0, 0.0).astype(F32)
    Ab = A.astype(BF)
    sh = jnp.int32(int(math.log2(NEUMANN_BLOCK)))
    B = jnp.where(lax.shift_right_logical(row, sh) == lax.shift_right_logical(col, sh), Ab, jnp.zeros_like(Ab))
    B2 = _mxu(B, B, "nn")
    B4 = _dot3(B2, B2, "nn")
    b2h, b2l = _split(B2)
    P = eye - B.astype(F32) + B2 - (_mxu(B, b2h, "nn") + _mxu(B, b2l, "nn"))
    T = P + _dot3(P, B4, "nn")
    b = NEUMANN_BLOCK
    while b < C:
        hi = ~(2 * b - 1)
        off = ((row & hi) == (col & hi)) & ((row & b) != 0) & ((col & b) == 0)
        Aoff = jnp.where(off, Ab, jnp.zeros_like(Ab))
        th, tl = _split(T)
        xh, xl = _split(_mxu(th, Aoff, "nn") + _mxu(tl, Aoff, "nn"))
        T = T - (_mxu(xh, th, "nn") + (_mxu(xh, tl, "nn") + _mxu(xl, th, "nn")))
        b *= 2
    return T


def _delta_common(q, k, g, beta):
    C = q.shape[-2]
    row, col = _iota((C, C), 0), _iota((C, C), 1)
    tril = row >= col
    qs = q * (D_HEAD ** -0.5)
    gcb = _dot01(jnp.where(tril, 1.0, 0.0), jnp.broadcast_to(g, g.shape[:-1] + (LANES,)))
    gc = gcb[..., :1]
    gr = jnp.swapaxes(gcb, -1, -2)
    Dm = jnp.exp(jnp.where(tril, gc - gr, -1e30))
    Dmt = jnp.exp(jnp.where(row <= col, gr - gc, -1e30))
    eg = jnp.exp(gc)
    gl = jnp.sum(jnp.where(_iota((C, 1), 0) == C - 1, gc, 0.0), axis=(-2, -1), keepdims=True)
    el = jnp.exp(gl)
    er = jnp.exp(gl - gc)
    kb = k * beta
    KK = _dot_nt(kb, k)
    QK = _dot_nt(qs, k)
    return dict(row=row, col=col, tril=tril, qs=qs, gc=gc, Dm=Dm, Dmt=Dmt, eg=eg, el=el, er=er, kb=kb, KK=KK, QK=QK)


def _delta_chunk_fwd(S0, q, k, v, g, beta):
    m = _delta_common(q, k, g, beta)
    T = _inv_unit_lower(jnp.where(m["row"] > m["col"], m["KK"] * m["Dm"], 0.0))
    u = _dotf(T, v * beta)
    w = _dotf(T, m["kb"] * m["eg"])
    vn = u - _dot(w, S0)
    o = _dot(m["qs"] * m["eg"], S0) + _dot(m["QK"] * m["Dm"], vn)
    S1 = S0 * m["el"] + _dot_tn(k * m["er"], vn)
    return o, S1, jnp.swapaxes(T, -1, -2), u, w


def _delta_chunk_bwd(S0, q, k, v, g, beta, Tt, u, w, do, dS1):
    m = _delta_common(q, k, g, beta)
    C = q.shape[-2]
    qs, Dm, Dmt, eg, el, er, kb, KK, QK = (m[n] for n in ("qs", "Dm", "Dmt", "eg", "el", "er", "kb", "KK", "QK"))
    strict = m["row"] > m["col"]
    total = lambda x: jnp.sum(x, axis=(-2, -1), keepdims=True)
    vn = u - _dot(w, S0)
    qg = qs * eg
    kr = k * er

    dvn = _dot(_dot_nt(k, qs) * Dmt, do) + _dot(kr, dS1)
    dS0 = dS1 * el + _dot_tn(qg, do) - _dot_tn(w, dvn)
    d_el = total(dS1 * S0)
    dqg = _dot_nt(do, S0)
    dqs = dqg * eg
    deg = jnp.sum(dqg * qs, -1, keepdims=True)
    dP = _dot_nt(do, vn)
    dPD = dP * Dm
    dqs = dqs + _dot(dPD, k)
    dk = _dot(_dot_nt(vn, do) * Dmt, qs)
    dD = dP * QK
    dkr = _dot_nt(vn, dS1)
    dk = dk + dkr * er
    der = jnp.sum(dkr * k, -1, keepdims=True)
    dw = -_dot_nt(dvn, S0)
    th, tl = _split(Tt)

    def tt_times(x):
        xh, xl = _split(x)
        return _mxu(th, xh, "nn") + (_mxu(th, xl, "nn") + _mxu(tl, xh, "nn"))

    dru = tt_times(dvn)
    drw = tt_times(dw)
    dA = -(_dotf_nt(dru, u) + _dotf_nt(drw, w))
    dAm = jnp.where(strict, dA, 0.0)
    dKK = dAm * Dm
    dkb = _dot(dKK, k)
    dk = dk + _dot_tn(dKK, kb)
    dD = dD + dAm * KK
    dv = dru * beta
    dbeta = jnp.sum(dru * v, -1, keepdims=True)
    dkb = dkb + drw * eg
    deg = deg + jnp.sum(drw * kb, -1, keepdims=True)
    dk = dk + dkb * beta
    dbeta = dbeta + jnp.sum(dkb * k, -1, keepdims=True)
    E = dD * Dm
    dgc = jnp.sum(E, -1, keepdims=True) - jnp.sum(jnp.swapaxes(E, -1, -2), -1, keepdims=True)
    dgc = dgc + deg * eg - der * er
    dgl = total(der * er) + d_el * el
    dgc = dgc + jnp.where(_iota((C, 1), 0) == C - 1, dgl, 0.0)
    triu = jnp.where(m["row"] <= m["col"], 1.0, 0.0)
    dg = _dot01(triu, jnp.broadcast_to(dgc, dgc.shape[:-1] + (LANES,)))[..., :1]
    dq = dqs * (D_HEAD ** -0.5)
    return dq, dk, dv, dg, dbeta, dS0


def _head_cols(gb, h):
    lane = _iota(gb.shape, 1)
    beta = jnp.sum(jnp.where(lane == h, gb, 0.0), -1, keepdims=True)
    g = jnp.sum(jnp.where(lane == N_HEADS + h, gb, 0.0), -1, keepdims=True)
    return g, beta


def _delta_fwd(q, k, v, gb):
    S = q.shape[0]
    C = DELTA_CHUNK
    N = S // C

    HB = DELTA_HEADS_PER_STEP

    def body(q_ref, k_ref, v_ref, gb_ref, o_ref, st_ref, t_ref, u_ref, w_ref, s_scr):
        n, hb = pl.program_id(0), pl.program_id(1)
        gb = gb_ref[...]

        @pl.when(n == 0)
        def _():
            for hh in range(HB):
                s_scr[hb * HB + hh] = jnp.zeros((D_HEAD, D_HEAD), F32)

        heads = [hb * HB + hh for hh in range(HB)]
        cols = [slice(hh * D_HEAD, (hh + 1) * D_HEAD) for hh in range(HB)]
        per_head = lambda ref: jnp.stack([ref[:, c] for c in cols])
        g, beta = (jnp.stack(t) for t in zip(*[_head_cols(gb, h) for h in heads]))
        S0 = jnp.stack([s_scr[h] for h in heads])
        o, S1, Tt, u, w = _delta_chunk_fwd(S0, per_head(q_ref), per_head(k_ref), per_head(v_ref), g, beta)
        for hh in range(HB):
            st_ref[hh, 0] = S0[hh]
            t_ref[hh, 0] = Tt[hh]
            o_ref[:, cols[hh]] = o[hh]
            u_ref[:, cols[hh]] = u[hh]
            w_ref[:, cols[hh]] = w[hh]
            s_scr[heads[hh]] = S1[hh]

    hd = pl.BlockSpec((C, HB * D_HEAD), lambda n, h: (n, h))
    mat = pl.BlockSpec((HB, 1, D_HEAD, D_HEAD), lambda n, h: (h, n, 0, 0))
    est = 40 * HB * _nbytes((C, D_HEAD), F32)
    seq = jax.ShapeDtypeStruct((S, N_HEADS * D_HEAD), F32)
    return pl.pallas_call(
        body, name="delta_fwd", grid=(N, N_HEADS // HB),
        in_specs=[hd, hd, hd, pl.BlockSpec((C, LANES), lambda n, h: (n, 0))],
        out_specs=[hd, mat, mat, hd, hd],
        out_shape=[seq, jax.ShapeDtypeStruct((N_HEADS, N, D_HEAD, D_HEAD), F32),
                   jax.ShapeDtypeStruct((N_HEADS, N, C, C), F32), seq, seq],
        scratch_shapes=[pltpu.VMEM((N_HEADS, D_HEAD, D_HEAD), F32)],
        compiler_params=_cparams(est, ("arbitrary", "arbitrary")),
    )(q, k, v, gb)


def _delta_bwd(q, k, v, gb, st, tinv, u, w, do):
    S = q.shape[0]
    C = DELTA_CHUNK
    N = S // C

    HB = DELTA_HEADS_PER_STEP

    def body(q_ref, k_ref, v_ref, gb_ref, st_ref, t_ref, u_ref, w_ref, do_ref, dq_ref, dk_ref, dv_ref, dgb_ref, ds_scr):
        n, hb = pl.program_id(0), pl.program_id(1)
        gb = gb_ref[...]
        lane = _iota((C, LANES), 1)
        dgb = jnp.zeros((C, LANES), F32)

        @pl.when(n == 0)
        def _():
            for hh in range(HB):
                ds_scr[hb * HB + hh] = jnp.zeros((D_HEAD, D_HEAD), F32)

        heads = [hb * HB + hh for hh in range(HB)]
        cols = [slice(hh * D_HEAD, (hh + 1) * D_HEAD) for hh in range(HB)]
        per_head = lambda ref: jnp.stack([ref[:, c] for c in cols])
        g, beta = (jnp.stack(t) for t in zip(*[_head_cols(gb, h) for h in heads]))
        dS1 = jnp.stack([ds_scr[h] for h in heads])
        dq, dk, dv, dg, dbeta, dS0 = _delta_chunk_bwd(
            st_ref[:, 0], per_head(q_ref), per_head(k_ref), per_head(v_ref), g, beta, t_ref[:, 0],
            per_head(u_ref), per_head(w_ref), per_head(do_ref), dS1)
        for hh, h in enumerate(heads):
            dq_ref[:, cols[hh]] = dq[hh]
            dk_ref[:, cols[hh]] = dk[hh]
            dv_ref[:, cols[hh]] = dv[hh]
            dgb = dgb + jnp.where(lane == h, dbeta[hh], 0.0) + jnp.where(lane == N_HEADS + h, dg[hh], 0.0)
            ds_scr[h] = dS0[hh]

        @pl.when(hb == 0)
        def _():
            dgb_ref[...] = dgb

        @pl.when(hb > 0)
        def _():
            dgb_ref[...] += dgb

    hd = pl.BlockSpec((C, HB * D_HEAD), lambda n, h: (N - 1 - n, h))
    mat = pl.BlockSpec((HB, 1, D_HEAD, D_HEAD), lambda n, h: (h, N - 1 - n, 0, 0))
    gbs = pl.BlockSpec((C, LANES), lambda n, h: (N - 1 - n, 0))
    est = 60 * HB * _nbytes((C, D_HEAD), F32)
    return pl.pallas_call(
        body, name="delta_bwd", grid=(N, N_HEADS // HB),
        in_specs=[hd, hd, hd, gbs, mat, mat, hd, hd, hd],
        out_specs=[hd, hd, hd, gbs],
        out_shape=[jax.ShapeDtypeStruct((S, N_HEADS * D_HEAD), F32)] * 3 + [jax.ShapeDtypeStruct((S, LANES), F32)],
        scratch_shapes=[pltpu.VMEM((N_HEADS, D_HEAD, D_HEAD), F32)],
        compiler_params=_cparams(est, ("arbitrary", "arbitrary")),
    )(q, k, v, gb, st, tinv, u, w, do)


def _ya_head(o, z, onw):
    return o * lax.rsqrt(jnp.mean(o * o, -1, keepdims=True) + RMS_EPS) * onw * _silu(z)


def _norm_cdf(x):
    return 0.5 * (1.0 + lax.erf(x * 0.7071067811865476))


def _norm_pdf(x):
    return jnp.exp(-0.5 * x * x) * 0.3989422804014327


def _chunk_causal(shape, di, dj):
    sh = jnp.int32(int(math.log2(SGU_CHUNK)))
    return lax.shift_right_logical(_iota(shape, di), sh) >= lax.shift_right_logical(_iota(shape, dj), sh)


def _ws_masked(ws):
    return jnp.where(_chunk_causal(ws.shape, 1, 2), ws, 0.0)


def _mix_prep(o, proj, onw, sg, sb, ws, bst, *, tm=256):
    S = o.shape[0]
    tm = _tile(S, tm, SGU_BLOCK)

    def body(o_ref, z_ref, u_ref, vg_ref, onw_ref, sg_ref, sb_ref, ws_ref, bst_ref, ya_ref, yb_ref, phi_ref):
        onw = onw_ref[...]
        for h in range(N_HEADS):
            sl = slice(h * D_HEAD, (h + 1) * D_HEAD)
            ya_ref[:, sl] = _ya_head(o_ref[:, sl], z_ref[:, sl].astype(F32), onw).astype(BF)
        u, vg = u_ref[...].astype(F32), vg_ref[...].astype(F32)
        phi_u, phi_v = _norm_cdf(u), _norm_cdf(vg)
        phi_ref[:, :D_MODEL] = phi_u
        phi_ref[:, D_MODEL:] = phi_v
        ua, vl = u * phi_u, _ln(vg * phi_v, sg_ref[...], sb_ref[...])
        wsm = _ws_masked(ws_ref[...])
        bst = bst_ref[...]
        for blk in range(tm // SGU_BLOCK):
            rs = slice(blk * SGU_BLOCK, (blk + 1) * SGU_BLOCK)
            for gi in range(SGU_GROUPS):
                cs = slice(gi * D_HEAD, (gi + 1) * D_HEAD)
                sp = _dot(wsm[gi], vl[rs, cs]) + bst[:, gi:gi + 1]
                yb_ref[rs, cs] = (ua[rs, cs] * sp).astype(BF)

    blk = lambda col: pl.BlockSpec((tm, D_MODEL), lambda i: (i, col))
    full = lambda shape: pl.BlockSpec(shape, lambda i: (0,) * len(shape))
    est = 10 * _nbytes((tm, D_MODEL), F32)
    return pl.pallas_call(
        body, name="mix_prep", grid=(S // tm,),
        in_specs=[blk(0), blk(0), blk(1), blk(2), full((1, D_HEAD)), full((1, D_MODEL)), full((1, D_MODEL)),
                  full((SGU_GROUPS, SGU_BLOCK, SGU_BLOCK)), full((SGU_BLOCK, LANES))],
        out_specs=[blk(0), blk(0), pl.BlockSpec((tm, 2 * D_MODEL), lambda i: (i, 0))],
        out_shape=[jax.ShapeDtypeStruct((S, D_MODEL), BF)] * 2 + [jax.ShapeDtypeStruct((S, 2 * D_MODEL), F32)],
        compiler_params=_cparams(est, ("parallel",)),
    )(o, proj, proj, proj, onw, sg, sb, ws, bst)


def _mix_prep_bwd(o, proj, phi, onw, sg, sb, ws, bst, dpa, wpa, dpb, wpb, dproj, *, tm=256):
    S = o.shape[0]
    tm = _tile(S, tm, SGU_BLOCK)

    def body(o_ref, z_ref, u_ref, vg_ref, phi_ref, onw_ref, sg_ref, sb_ref, ws_ref, bst_ref, dpa_ref, wpa_ref, dpb_ref,
             wpb_ref, dproj_in, do_ref, dzuv_ref, donw_ref, dsg_ref, dsb_ref, dws_ref, dbst_ref, dvl_scr, dua_scr,
             dya_ref, dyb_ref):
        dz_ref, du_ref, dvg_ref = (dzuv_ref.at[:, k * D_MODEL:(k + 1) * D_MODEL] for k in range(3))
        @pl.when(pl.program_id(0) == 0)
        def _():
            for r in (donw_ref, dsg_ref, dsb_ref, dws_ref, dbst_ref):
                r[...] = jnp.zeros_like(r)

        dya_ref[...] = _dot_nt(dpa_ref[...], wpa_ref[...])
        dyb_ref[...] = _dot_nt(dpb_ref[...], wpb_ref[...])

        onw = onw_ref[...]
        donw = jnp.zeros((1, D_HEAD), F32)
        for h in range(N_HEADS):
            sl = slice(h * D_HEAD, (h + 1) * D_HEAD)
            _, vj = jax.vjp(_ya_head, o_ref[:, sl], z_ref[:, sl].astype(F32), onw)
            do_h, dz_h, donw_h = vj(dya_ref[:, sl])
            do_ref[:, sl] = do_h.astype(BF)
            dz_ref[:, sl] = dz_h.astype(BF)
            donw = donw + donw_h
        donw_ref[...] += _bcast_rows(donw)

        u, vg = u_ref[...].astype(F32), vg_ref[...].astype(F32)
        phi_u, phi_v = phi_ref[:, :D_MODEL], phi_ref[:, D_MODEL:]
        ua = u * phi_u
        vl, vj = jax.vjp(_ln, vg * phi_v, sg_ref[...], sb_ref[...])
        wsm = _ws_masked(ws_ref[...])
        bst = bst_ref[...]
        lane = _iota((SGU_BLOCK, LANES), 1)
        dbst = jnp.zeros((SGU_BLOCK, LANES), F32)
        cmask = _chunk_causal((SGU_BLOCK, SGU_BLOCK), 0, 1)
        for gi in range(SGU_GROUPS):
            cs = slice(gi * D_HEAD, (gi + 1) * D_HEAD)
            wg = wsm[gi]
            wgt = jnp.transpose(wg)
            dwg = jnp.zeros((SGU_BLOCK, SGU_BLOCK), F32)
            for blk in range(tm // SGU_BLOCK):
                rs = slice(blk * SGU_BLOCK, (blk + 1) * SGU_BLOCK)
                sp = _dot(wg, vl[rs, cs]) + bst[:, gi:gi + 1]
                dyb = dyb_ref[rs, cs]
                dsp = dyb * ua[rs, cs]
                dua_scr[rs, cs] = dyb * sp
                dvl_scr[rs, cs] = _dot(wgt, dsp)
                dwg = dwg + _dot_nt(dsp, vl[rs, cs])
                dbst = dbst + jnp.where(lane == gi, jnp.sum(dsp, -1, keepdims=True), 0.0)
            dws_ref[gi] += jnp.where(cmask, dwg, 0.0)
        dbst_ref[...] += dbst
        dgv, dsg, dsb = vj(dvl_scr[...])
        du_ref[...] = (dua_scr[...] * (phi_u + u * _norm_pdf(u))).astype(BF)
        dvg_ref[...] = (dgv * (phi_v + vg * _norm_pdf(vg))).astype(BF)
        dsg_ref[...] += _bcast_rows(dsg)
        dsb_ref[...] += _bcast_rows(dsb)

    blk = lambda col: pl.BlockSpec((tm, D_MODEL), lambda i: (i, col))
    full = lambda shape: pl.BlockSpec(shape, lambda i: (0,) * len(shape))
    est = 18 * _nbytes((tm, D_MODEL), F32) + 4 * _nbytes((D_MODEL, D_MODEL), BF)
    outs = pl.pallas_call(
        body, name="mix_prep_bwd", grid=(S // tm,),
        in_specs=[blk(0), blk(0), blk(1), blk(2), pl.BlockSpec((tm, 2 * D_MODEL), lambda i: (i, 0)),
                  full((1, D_HEAD)), full((1, D_MODEL)), full((1, D_MODEL)),
                  full((SGU_GROUPS, SGU_BLOCK, SGU_BLOCK)), full((SGU_BLOCK, LANES)),
                  blk(0), full((D_MODEL, D_MODEL)), blk(0), full((D_MODEL, D_MODEL)),
                  pl.BlockSpec(memory_space=pl.ANY)],
        out_specs=[blk(0), pl.BlockSpec((tm, 3 * D_MODEL), lambda i: (i, 1)),
                   full((SUBLANES, D_HEAD)), full((SUBLANES, D_MODEL)), full((SUBLANES, D_MODEL)),
                   full((SGU_GROUPS, SGU_BLOCK, SGU_BLOCK)), full((SGU_BLOCK, LANES))],
        out_shape=[jax.ShapeDtypeStruct((S, D_MODEL), BF), jax.ShapeDtypeStruct(dproj.shape, BF),
                   jax.ShapeDtypeStruct((SUBLANES, D_HEAD), F32), jax.ShapeDtypeStruct((SUBLANES, D_MODEL), F32),
                   jax.ShapeDtypeStruct((SUBLANES, D_MODEL), F32),
                   jax.ShapeDtypeStruct((SGU_GROUPS, SGU_BLOCK, SGU_BLOCK), F32),
                   jax.ShapeDtypeStruct((SGU_BLOCK, LANES), F32)],
        input_output_aliases={14: 1},
        scratch_shapes=[pltpu.VMEM((tm, D_MODEL), F32)] * 4,
        compiler_params=_cparams(est, ("arbitrary",)),
    )(o, proj, proj, proj, phi, onw, sg, sb, ws, bst, dpa, wpa, dpb, wpb, dproj)
    return outs


def _mm_gate_merge(ya, yb, wpa, wpb, proj, *, tm=512):
    S = ya.shape[0]
    tm = _tile(S, tm, SUBLANES * 2)

    def body(ya_ref, yb_ref, wa_ref, wb_ref, ga_ref, gb_ref, pa_ref, pb_ref, m_ref):
        pa = _dot(ya_ref[...], wa_ref[...]).astype(BF)
        pb = _dot(yb_ref[...], wb_ref[...]).astype(BF)
        pa_ref[...] = pa
        pb_ref[...] = pb
        m_ref[...] = (_sigmoid(ga_ref[...].astype(F32)) * pa.astype(F32)
                      + _sigmoid(gb_ref[...].astype(F32)) * pb.astype(F32)).astype(BF)

    blk = lambda col: pl.BlockSpec((tm, D_MODEL), lambda i: (i, col))
    wsp = pl.BlockSpec((D_MODEL, D_MODEL), lambda i: (0, 0))
    return pl.pallas_call(
        body, name="mm_gate_merge", grid=(S // tm,),
        in_specs=[blk(0), blk(0), wsp, wsp, blk(3), blk(4)], out_specs=[blk(0)] * 3,
        out_shape=[jax.ShapeDtypeStruct((S, D_MODEL), BF)] * 3,
        compiler_params=_cparams(2 * _nbytes((D_MODEL, D_MODEL), BF) + 8 * _nbytes((tm, D_MODEL), F32), ("parallel",)),
    )(ya, yb, wpa, wpb, proj, proj)


def _mm_gate_merge_bwd(dmix, wo, pa, pb, proj, *, tm=512):
    S = pa.shape[0]
    tm = _tile(S, tm, SUBLANES * 2)

    def body(d_ref, w_ref, pa_ref, pb_ref, ga_ref, gb_ref, dpa_ref, dpb_ref, dg_ref):
        dm = _dot_nt(d_ref[...], w_ref[...])
        sa, sb = _sigmoid(ga_ref[...].astype(F32)), _sigmoid(gb_ref[...].astype(F32))
        dpa_ref[...] = (dm * sa).astype(BF)
        dpb_ref[...] = (dm * sb).astype(BF)
        dg_ref[:, :D_MODEL] = (dm * pa_ref[...].astype(F32) * sa * (1.0 - sa)).astype(BF)
        dg_ref[:, D_MODEL:] = (dm * pb_ref[...].astype(F32) * sb * (1.0 - sb)).astype(BF)

    blk = lambda col: pl.BlockSpec((tm, D_MODEL), lambda i: (i, col))
    est = _nbytes((D_MODEL, D_MODEL), BF) + 10 * _nbytes((tm, D_MODEL), F32)
    return pl.pallas_call(
        body, name="mm_gate_merge_bwd", grid=(S // tm,),
        in_specs=[blk(0), pl.BlockSpec((D_MODEL, D_MODEL), lambda i: (0, 0)), blk(0), blk(0), blk(3), blk(4)],
        out_specs=[blk(0), blk(0), pl.BlockSpec((tm, 2 * D_MODEL), lambda i: (i, 3))],
        out_shape=[jax.ShapeDtypeStruct((S, D_MODEL), BF)] * 2 + [jax.ShapeDtypeStruct((S, 8 * D_MODEL), BF)],
        compiler_params=_cparams(est, ("parallel",)),
    )(dmix, wo, pa, pb, proj, proj)


def _mm_swiglu(xb, wgt, wut, *, tm=1024, tn=768):
    S, K = xb.shape
    tm = _tile(S, tm, SUBLANES * 2)
    tn = _tile(FFN_K, tn, LANES)

    def body(x_ref, wg_ref, wu_ref, hg_ref, hu_ref, h_ref):
        x = x_ref[...]
        hg = _dot_nt(x, wg_ref[...]).astype(BF)
        hu = _dot_nt(x, wu_ref[...]).astype(BF)
        hg_ref[...] = hg
        hu_ref[...] = hu
        h_ref[...] = (_silu(hg.astype(F32)) * hu.astype(F32)).astype(BF)

    out = pl.BlockSpec((tm, tn), lambda i, j: (i, j))
    est = _nbytes((tm, K), BF) + 2 * _nbytes((K, tn), BF) + 6 * _nbytes((tm, tn), F32)
    return pl.pallas_call(
        body, name="mm_swiglu", grid=(S // tm, FFN_K // tn),
        in_specs=[pl.BlockSpec((tm, K), lambda i, j: (i, 0)), pl.BlockSpec((tn, K), lambda i, j: (j, 0)),
                  pl.BlockSpec((tn, K), lambda i, j: (j, 0))],
        out_specs=[out] * 3, out_shape=[jax.ShapeDtypeStruct((S, FFN_K), BF)] * 3,
        compiler_params=_cparams(est, ("parallel", "parallel")),
    )(xb, wgt, wut)


def _mm_swiglu_bwd(dffn, wd, hg, hu, *, tm=1024, tn=768):
    S, K = dffn.shape
    tm = _tile(S, tm, SUBLANES * 2)
    tn = _tile(FFN_K, tn, LANES)

    def body(d_ref, w_ref, hg_ref, hu_ref, dhg_ref, dhu_ref):
        dh = _dot_nt(d_ref[...], w_ref[...])
        act, dact = _silu_and_grad(hg_ref[...].astype(F32))
        dhg_ref[...] = (dh * hu_ref[...].astype(F32) * dact).astype(BF)
        dhu_ref[...] = (dh * act).astype(BF)

    out = pl.BlockSpec((tm, tn), lambda i, j: (i, j))
    est = _nbytes((tm, K), dffn.dtype) + _nbytes((tn, K), BF) + 8 * _nbytes((tm, tn), F32)
    return pl.pallas_call(
        body, name="mm_swiglu_bwd", grid=(S // tm, FFN_K // tn),
        in_specs=[pl.BlockSpec((tm, K), lambda i, j: (i, 0)), pl.BlockSpec((tn, K), lambda i, j: (j, 0)), out, out],
        out_specs=[out, out], out_shape=[jax.ShapeDtypeStruct((S, FFN_K), BF)] * 2,
        compiler_params=_cparams(est, ("parallel", "parallel")),
    )(dffn, wd, hg, hu)


def _mm_resid_ln(a, bmat, x, g, b, *, name, tm=512):
    S, K = a.shape
    tm = _tile(S, tm, SUBLANES * 2)

    def body(a_ref, w_ref, x_ref, g_ref, b_ref, pre_ref, y_ref, yb_ref):
        pre = ALPHA * x_ref[...] + _dot(a_ref[...], w_ref[...])
        y = _ln(pre, g_ref[...], b_ref[...])
        pre_ref[...] = pre
        y_ref[...] = y
        yb_ref[...] = y.astype(BF)

    blk = pl.BlockSpec((tm, D_MODEL), lambda i: (i, 0))
    vec = pl.BlockSpec((1, D_MODEL), lambda i: (0, 0))
    est = _nbytes((tm, K), BF) + _nbytes((K, D_MODEL), BF) + 8 * _nbytes((tm, D_MODEL), F32)
    return pl.pallas_call(
        body, name=name, grid=(S // tm,),
        in_specs=[pl.BlockSpec((tm, K), lambda i: (i, 0)), pl.BlockSpec((K, D_MODEL), lambda i: (0, 0)), blk, vec, vec],
        out_specs=[blk, blk, blk],
        out_shape=[jax.ShapeDtypeStruct((S, D_MODEL), F32)] * 2 + [jax.ShapeDtypeStruct((S, D_MODEL), BF)],
        compiler_params=_cparams(est, ("parallel",)),
    )(a, bmat, x, g, b)


def _ln_bwd(pre, g, b, dy, *, tm=512):
    S = pre.shape[0]
    tm = _tile(S, tm, SUBLANES)

    def body(p_ref, g_ref, b_ref, dy_ref, dp_ref, dg_ref, db_ref):
        @pl.when(pl.program_id(0) == 0)
        def _():
            dg_ref[...] = jnp.zeros_like(dg_ref)
            db_ref[...] = jnp.zeros_like(db_ref)

        _, vj = jax.vjp(_ln, p_ref[...], g_ref[...], b_ref[...])
        dp, dg, db = vj(dy_ref[...])
        dp_ref[...] = dp
        dg_ref[...] += _bcast_rows(dg)
        db_ref[...] += _bcast_rows(db)

    blk = pl.BlockSpec((tm, D_MODEL), lambda i: (i, 0))
    vec = pl.BlockSpec((1, D_MODEL), lambda i: (0, 0))
    acc = pl.BlockSpec((SUBLANES, D_MODEL), lambda i: (0, 0))
    return pl.pallas_call(
        body, name="ln_bwd", grid=(S // tm,),
        in_specs=[blk, vec, vec, blk], out_specs=[blk, acc, acc],
        out_shape=[jax.ShapeDtypeStruct((S, D_MODEL), F32)] + [jax.ShapeDtypeStruct((SUBLANES, D_MODEL), F32)] * 2,
        compiler_params=_cparams(10 * _nbytes((tm, D_MODEL), F32), ("arbitrary",)),
    )(pre, g, b, dy)


def _loss_ln_bwd(y, tgt, pre, g, b, *, tm=512):
    S = y.shape[0]
    tm = _tile(S, tm, SUBLANES)

    def body(y_ref, t_ref, p_ref, g_ref, b_ref, dp_ref, dg_ref, db_ref, l_ref):
        @pl.when(pl.program_id(0) == 0)
        def _():
            for r in (dg_ref, db_ref, l_ref):
                r[...] = jnp.zeros_like(r)

        e = y_ref[...] - t_ref[...]
        l_ref[...] += 0.5 * jnp.sum(jnp.mean(e * e, -1, keepdims=True), keepdims=True)
        _, vj = jax.vjp(_ln, p_ref[...], g_ref[...], b_ref[...])
        dp, dg, db = vj(e * (1.0 / D_MODEL))
        dp_ref[...] = dp
        dg_ref[...] += _bcast_rows(dg)
        db_ref[...] += _bcast_rows(db)

    blk = pl.BlockSpec((tm, D_MODEL), lambda i: (i, 0))
    vec = pl.BlockSpec((1, D_MODEL), lambda i: (0, 0))
    acc = pl.BlockSpec((SUBLANES, D_MODEL), lambda i: (0, 0))
    return pl.pallas_call(
        body, name="loss_ln_bwd", grid=(S // tm,),
        in_specs=[blk, blk, blk, vec, vec], out_specs=[blk, acc, acc, pl.BlockSpec((SUBLANES, LANES), lambda i: (0, 0))],
        out_shape=[jax.ShapeDtypeStruct((S, D_MODEL), F32)] + [jax.ShapeDtypeStruct((SUBLANES, D_MODEL), F32)] * 2
                  + [jax.ShapeDtypeStruct((SUBLANES, LANES), F32)],
        compiler_params=_cparams(12 * _nbytes((tm, D_MODEL), F32), ("arbitrary",)),
    )(y, tgt, pre, g, b)


def _layer_fwd(x, xb, w, late):
    pq = _mm(xb, w["win"], mode="nn", name="mm_in_qkv", tm=1024, tn=1024, cols=(0, 3 * D_MODEL))
    proj = _mm(xb, w["win"], mode="nn", name="mm_in_rest", tm=1024, tn=1024, cols=(3 * D_MODEL, 5 * D_MODEL), out_dtype=BF)
    ba = _mm(xb, w["wba"], mode="nn", name="mm_in_ba", tm=1024, tn=LANES)
    qn, kn, vv, gb, conv_out = _qkv_prep(pq, ba, w["convw"], w["arow"], w["dtrow"])
    o, st, tinv, wy_u, wy_w = _delta_fwd(qn, kn, vv, gb)
    ya, yb, phi = _mix_prep(o, proj, w["onw"], w["sg"], w["sb"], w["ws"], w["bst"])
    w = {**w, **late(ya)}
    pa, pb, m = _mm_gate_merge(ya, yb, w["wpa"], w["wpb"], proj)
    pre1, x1, x1b = _mm_resid_ln(m, w["wo"], x, w["ln1g"], w["ln1b"], name="mm_out_ln")
    hg, hu, h = _mm_swiglu(x1b, w["wgt"], w["wut"])
    pre2, x2, x2b = _mm_resid_ln(h, w["wd"], x1, w["ln2g"], w["ln2b"], name="mm_down_ln")
    saved = dict(xb=xb, pq=pq, conv_out=conv_out, proj=proj, phi=phi, ba=ba, qn=qn, kn=kn, vv=vv, gb=gb, o=o, st=st, tinv=tinv, wy_u=wy_u, wy_w=wy_w,
                 ya=ya, yb=yb,
                 pa=pa, pb=pb, m=m, pre1=pre1, x1b=x1b, hg=hg, hu=hu, h=h, pre2=pre2)
    return x2, x2b, saved, w


def _layer_bwd(dpre2, ln2_grads, w, s, on_part=None):
    g = {}
    started = lambda part: on_part(part, g) if on_part is not None else None
    after = lambda v, token: v if token is None else v + token.astype(v.dtype)
    g["ln2g"], g["ln2b"] = ln2_grads
    dhg, dhu = _mm_swiglu_bwd(dpre2, w["wd"], s["hg"], s["hu"])
    g["wd"] = _mm(s["h"], dpre2, mode="tn", name="mm_tn_down", tm=1536, tk=1024, out_dtype=BF)
    dx1 = _mm(dhg, w["wgt"], mode="nn", name="mm_nn_gu", pair=(dhu, w["wut"]), add=dpre2, add_scale=ALPHA, tm=1024, tk=1536)
    g["wgt"] = _mm(dhg, s["x1b"], mode="tn", name="mm_tn_gu", tm=1536, tn=1024, tk=2048, out_dtype=BF)
    g["wut"] = _mm(dhu, s["x1b"], mode="tn", name="mm_tn_gu", tm=1536, tn=1024, tk=2048, out_dtype=BF)
    dpre1, g["ln1g"], g["ln1b"] = _ln_bwd(s["pre1"], w["ln1g"], w["ln1b"], dx1)
    g["wo"] = _mm(s["m"], dpre1, mode="tn", name="mm_tn_sq", tm=1024, tk=1024, out_dtype=BF)
    dpa, dpb, dproj = _mm_gate_merge_bwd(dpre1, w["wo"], s["pa"], s["pb"], s["proj"])
    g["wpa"] = _mm(s["ya"], dpa, mode="tn", name="mm_tn_sq", tm=1024, tk=1024, out_dtype=BF)
    g["wpb"] = _mm(s["yb"], dpb, mode="tn", name="mm_tn_sq", tm=1024, tk=1024, out_dtype=BF)
    do, dproj, g["onw"], g["sg"], g["sb"], g["ws"], g["bst"] = _mix_prep_bwd(
        s["o"], s["proj"], s["phi"], after(w["onw"], started("late")), w["sg"], w["sb"], w["ws"], w["bst"],
        dpa, w["wpa"], dpb, w["wpb"], dproj)
    dqn, dkn, dvv, dgb = _delta_bwd(s["qn"], s["kn"], s["vv"], s["gb"], s["st"], s["tinv"], s["wy_u"], s["wy_w"], do)
    dc, dba, g["convw"], g["arow"], g["dtrow"] = _qkv_prep_bwd(
        s["pq"], s["conv_out"], s["ba"], w["arow"], w["dtrow"], dqn, dkn, dvv, dgb)
    dproj = _conv_bwd(dc, w["convw"], dproj)
    g["win"] = _mm(s["xb"], dproj, mode="tn", name="mm_tn_in", tm=1024, tn=1024, tk=2048, out_dtype=BF)
    g["wba"] = _mm(s["xb"], dba, mode="tn", name="mm_tn_ba", tm=1024, tn=LANES, tk=1024, out_dtype=BF)
    dx = _mm(dba, after(w["wba"], started("early")), mode="nt", name="mm_nt_ba", add=dpre1, add_scale=ALPHA, tm=1024)
    dx = _mm(dproj, w["win"], mode="nt", name="mm_nt_in", add=dx, add_scale=1.0, tm=1024, tk=2048)
    return dx, g


def _local_step(x, xb, tgt, layers, on_grads=None):
    saved, weights = [], []
    for layer in layers:
        x, xb, s, w = _layer_fwd(x, xb, *layer(x))
        saved.append(s)
        weights.append(w)
    last = len(layers) - 1
    dpre2, dg, db, lacc = _loss_ln_bwd(x, tgt, saved[last]["pre2"], weights[last]["ln2g"], weights[last]["ln2b"])
    grads = [None] * len(layers)
    for l in reversed(range(len(layers))):
        on_part = functools.partial(on_grads, l) if on_grads is not None else None
        dx, grads[l] = _layer_bwd(dpre2, (dg, db), weights[l], saved[l], on_part)
        if l > 0:
            dpre2, dg, db = _ln_bwd(saved[l - 1]["pre2"], weights[l - 1]["ln2g"], weights[l - 1]["ln2b"], dx)
    return lacc[0, 0], dx, grads


_QKVZ = 4 * D_MODEL
_BA = 2 * N_HEADS


WEIGHT_NAMES = ("w_in", "conv_w", "a_log", "dt_bias", "o_norm_w", "sgu_ln_g", "sgu_ln_b", "w_s", "b_s", "w_pa", "w_pb",
                "w_o", "ln1_g", "ln1_b", "w_ffn_gate", "w_ffn_up", "w_ffn_down", "ln2_g", "ln2_b")
WIRE = ("w_in", "w_ffn_gate", "w_ffn_up", "w_ffn_down", "w_pa", "w_pb", "w_o", "conv_w")
SMALL = (("a_log", N_HEADS), ("dt_bias", N_HEADS), ("o_norm_w", D_HEAD), ("sgu_ln_g", D_MODEL), ("sgu_ln_b", D_MODEL),
         ("w_s", SGU_GROUPS * SGU_BLOCK * SGU_BLOCK), ("b_s", SGU_GROUPS * SGU_BLOCK),
         ("ln1_g", D_MODEL), ("ln1_b", D_MODEL), ("ln2_g", D_MODEL), ("ln2_b", D_MODEL))
SMALL_ROWS = -(-sum(n for _, n in SMALL) // (LANES * SUBLANES)) * SUBLANES
N_MAIN_TILES = (N_IN - _BA) // D_MODEL
ADAM_TILES = dict(w_in=(128, "adamw_in"), w_ffn_gate=(32, "adamw_ffn_rows"), w_ffn_up=(32, "adamw_ffn_rows"),
                  w_ffn_down=(32, "adamw_ffn_rows"), w_pa=(128, "adamw_sq"), w_pb=(128, "adamw_sq"), w_o=(128, "adamw_sq"),
                  conv_w=(CONV_K, "adamw_conv"))


def _pad_to(a, axis, size):
    pads = [(0, 0)] * a.ndim
    pads[axis] = (0, size - a.shape[axis])
    return jnp.pad(a, pads)


def _t(a):
    return jnp.swapaxes(a, 1, 2)


def _wire_blocks(p):
    return dict(
        w_in=_pad_to(p["w_in"].astype(BF), 2, IN_PAD),
        w_ffn_gate=_pad_to(_t(p["w_ffn_gate"]).astype(BF), 1, FFN_PAD), w_ffn_up=_pad_to(_t(p["w_ffn_up"]).astype(BF), 1, FFN_PAD),
        w_ffn_down=_pad_to(p["w_ffn_down"].astype(BF), 1, FFN_PAD),
        w_pa=p["w_pa"].astype(BF), w_pb=p["w_pb"].astype(BF), w_o=p["w_o"].astype(BF),
        conv_w=_pad_to(p["conv_w"], 1, SUBLANES),
    )


def _by_columns(blocks):
    n, r, c = blocks.shape
    return jnp.transpose(blocks, (1, 0, 2)).reshape(r, n * c)


def _to_slots(full, c):
    r = full.shape[0]
    return jnp.transpose(full.reshape(r, N_DEV, c), (1, 0, 2))


def _lane_row(v, at):
    return jnp.pad(v[None], ((0, 0), (at, LANES - at - v.shape[0])))


TRANSPOSED = ("w_ffn_gate", "w_ffn_up")
EARLY = ("w_in", "conv_w")
LATE = ("w_pa", "w_pb", "w_o", "w_ffn_gate", "w_ffn_up", "w_ffn_down")


def _early_weights(stacks, p, l):
    return dict(
        win=_perm_in(stacks["w_in"], D_MODEL, N_MAIN_TILES), wba=_perm_in(stacks["w_in"], LANES, 1),
        convw=_by_columns(stacks["conv_w"][:, :CONV_K]),
        arow=_lane_row(p["a_log"][l], N_HEADS), dtrow=_lane_row(p["dt_bias"][l], N_HEADS),
        onw=p["o_norm_w"][l][None], sg=p["sgu_ln_g"][l][None], sb=p["sgu_ln_b"][l][None],
        ws=p["w_s"][l], bst=_pad_to(p["b_s"][l].T, 1, LANES),
        ln1g=p["ln1_g"][l][None], ln1b=p["ln1_b"][l][None], ln2g=p["ln2_g"][l][None], ln2b=p["ln2_b"][l][None],
    )


def _late_weights(stacks):
    return dict(
        wpa=stacks["w_pa"].reshape(D_MODEL, D_MODEL), wpb=stacks["w_pb"].reshape(D_MODEL, D_MODEL),
        wo=stacks["w_o"].reshape(D_MODEL, D_MODEL),
        wgt=stacks["w_ffn_gate"].reshape(FFN_K, D_MODEL), wut=stacks["w_ffn_up"].reshape(FFN_K, D_MODEL),
        wd=stacks["w_ffn_down"].reshape(FFN_K, D_MODEL),
    )


def _small_pack(parts):
    flat = jnp.concatenate([parts[n].reshape(-1) for n, _ in SMALL])
    return _pad_to(flat, 0, SMALL_ROWS * LANES).reshape(SMALL_ROWS, LANES)


def _small_unpack(rows, like):
    flat, out, off = rows.reshape(-1), {}, 0
    for n, size in SMALL:
        out[n] = flat[off:off + size].reshape(like[n].shape[1:])
        off += size
    return out


def _late_slots(g):
    slots = dict(
        w_ffn_gate=g["wgt"].reshape(N_DEV, FFN_PAD, D_MODEL), w_ffn_up=g["wut"].reshape(N_DEV, FFN_PAD, D_MODEL),
        w_ffn_down=g["wd"].reshape(N_DEV, FFN_PAD, D_MODEL),
        w_pa=g["wpa"].reshape(N_DEV, D_MODEL // N_DEV, D_MODEL), w_pb=g["wpb"].reshape(N_DEV, D_MODEL // N_DEV, D_MODEL),
        w_o=g["wo"].reshape(N_DEV, D_MODEL // N_DEV, D_MODEL),
    )
    return [slots[n] for n in LATE]


def _early_slots(g):
    slots = [_perm_out(g["win"], g["wba"]), _pad_to(_to_slots(g["convw"][:CONV_K], 3 * D_MODEL // N_DEV), 1, SUBLANES)]
    small = _small_pack(dict(
        a_log=g["arow"][0, N_HEADS:2 * N_HEADS], dt_bias=g["dtrow"][0, N_HEADS:2 * N_HEADS], o_norm_w=g["onw"][0],
        sgu_ln_g=g["sg"][0], sgu_ln_b=g["sb"][0], w_s=g["ws"], b_s=g["bst"][:, :SGU_GROUPS].T,
        ln1_g=g["ln1g"][0], ln1_b=g["ln1b"][0], ln2_g=g["ln2g"][0], ln2_b=g["ln2b"][0]))
    return slots, small


def _in_tile_start(j, tile_w):
    if tile_w == LANES:
        return jnp.int32(_QKVZ)
    return j * D_MODEL + jnp.where(j >= _QKVZ // D_MODEL, _BA, 0)


def _select(rows_iota, cols_iota, dev, start, valid):
    hit = (rows_iota + (dev * IN_BLOCK - start) == cols_iota) & (rows_iota < IN_BLOCK) & (cols_iota < valid)
    return jnp.where(hit, 1.0, 0.0).astype(BF)


def _perm_in(stack, tile_w, n_tiles):
    valid = _BA if tile_w == LANES else tile_w

    def first_dev(j):
        return lax.div(_in_tile_start(j, tile_w), jnp.int32(IN_BLOCK))

    def body(w_ref, o_ref, acc_ref):
        j, k = pl.program_id(0), pl.program_id(1)
        sel = _select(_iota((IN_PAD, tile_w), 0), _iota((IN_PAD, tile_w), 1), first_dev(j) + k,
                      _in_tile_start(j, tile_w), valid)
        part = jnp.dot(w_ref[0], sel, preferred_element_type=F32)

        @pl.when(k == 0)
        def _():
            acc_ref[...] = part

        @pl.when(k == 1)
        def _():
            o_ref[...] = (acc_ref[...] + part).astype(BF)

    est = _nbytes((D_MODEL, IN_PAD), BF) + 3 * _nbytes((D_MODEL, tile_w), F32) + 2 * _nbytes((IN_PAD, tile_w), F32)
    return pl.pallas_call(
        body, name="perm_in" if tile_w != LANES else "perm_in_ba", grid=(n_tiles, 2),
        in_specs=[pl.BlockSpec((1, D_MODEL, IN_PAD), lambda j, k: (jnp.minimum(first_dev(j) + k, N_DEV - 1), 0, 0))],
        out_specs=pl.BlockSpec((D_MODEL, tile_w), lambda j, k: (0, j)),
        out_shape=jax.ShapeDtypeStruct((D_MODEL, n_tiles * tile_w), BF),
        scratch_shapes=[pltpu.VMEM((D_MODEL, tile_w), F32)],
        compiler_params=_cparams(est, ("parallel", "arbitrary")),
    )(_in_hbm(stack))


def _perm_out(dmain, dba):
    def tile(d, s):
        c0 = d * IN_BLOCK
        first = lax.div(c0 - jnp.where(c0 < _QKVZ, 0, jnp.minimum(c0 - _QKVZ, _BA)), jnp.int32(D_MODEL))
        return jnp.minimum(first + jnp.minimum(s, 1), N_MAIN_TILES - 1)

    def body(dm_ref, db_ref, o_ref, acc_ref):
        d, s = pl.program_id(0), pl.program_id(1)

        @pl.when(s == 0)
        def _():
            acc_ref[...] = jnp.zeros_like(acc_ref)

        start = _in_tile_start(tile(d, s), D_MODEL)
        overlaps = (start < (d + 1) * IN_BLOCK) & (d * IN_BLOCK < start + D_MODEL)

        @pl.when((s < 2) & overlaps)
        def _():
            sel = _select(_iota((D_MODEL, IN_PAD), 1), _iota((D_MODEL, IN_PAD), 0), d, start, D_MODEL)
            acc_ref[...] += jnp.dot(dm_ref[...], sel, preferred_element_type=F32)

        @pl.when(s == 2)
        def _():
            sel = _select(_iota((LANES, IN_PAD), 1), _iota((LANES, IN_PAD), 0), d, jnp.int32(_QKVZ), _BA)
            o_ref[0] = (acc_ref[...] + jnp.dot(db_ref[...], sel, preferred_element_type=F32)).astype(BF)

    est = 2 * _nbytes((D_MODEL, D_MODEL), BF) + 4 * _nbytes((D_MODEL, IN_PAD), F32)
    return pl.pallas_call(
        body, name="perm_out", grid=(N_DEV, 3),
        in_specs=[pl.BlockSpec((D_MODEL, D_MODEL), lambda d, s: (0, tile(d, s))),
                  pl.BlockSpec((D_MODEL, LANES), lambda d, s: (0, 0))],
        out_specs=pl.BlockSpec((1, D_MODEL, IN_PAD), lambda d, t: (d, 0, 0)),
        out_shape=jax.ShapeDtypeStruct((N_DEV, D_MODEL, IN_PAD), BF),
        scratch_shapes=[pltpu.VMEM((D_MODEL, IN_PAD), F32)],
        compiler_params=_cparams(est, ("parallel", "arbitrary")),
    )(dmain, dba)


def _mesh_place():
    x, y, c = (lax.axis_index(a) for a in MESH_AXES)
    return x, y, c


def _slot(x, y, c):
    return 4 * x + 2 * y + c


def _peer(place, j):
    x, y, c = place
    return (1 - x if j & 4 else x, 1 - y if j & 2 else y, 1 - c if j & 1 else c)


_HBM = pl.BlockSpec(memory_space=pltpu.HBM)
_SEM = pl.BlockSpec(memory_space=pltpu.SEMAPHORE)
_EFFECT = pltpu.SideEffectType.DATAFLOW_SIDE_EFFECTING


def _remote_copy(src_ref, land_ref, slot, per_slot, pslot, sems, u, j, peer):
    return pltpu.make_async_remote_copy(
        src_ref=src_ref.at[pslot] if per_slot else src_ref, dst_ref=land_ref.at[slot],
        send_sem=sems[0].at[u * (N_DEV - 1) + j - 1], recv_sem=sems[1].at[u * (N_DEV - 1) + j - 1],
        device_id=peer, device_id_type=pl.DeviceIdType.MESH)


def _own_copy(src_ref, land_ref, me, per_slot, sems, u):
    return pltpu.make_async_copy(src_ref.at[me] if per_slot else src_ref, land_ref.at[me], sems[2].at[u])


def _exchange_start(name, srcs, per_slot):
    n = len(srcs)
    lands = [jax.ShapeDtypeStruct(s.shape if p else (N_DEV,) + s.shape, s.dtype) for s, p in zip(srcs, per_slot)]

    def body(*refs):
        src_refs, sems, land_refs, token = refs[:n], refs[n:n + 3], refs[2 * n + 3:3 * n + 3], refs[-1]
        place = _mesh_place()
        me = _slot(*place)
        for u in range(n):
            _own_copy(src_refs[u], land_refs[u], me, per_slot[u], sems, u).start()
            for j in range(1, N_DEV):
                peer = _peer(place, j)
                _remote_copy(src_refs[u], land_refs[u], me, per_slot[u], _slot(*peer), sems, u, j, peer).start()
        token[...] = jnp.zeros_like(token)

    hbm = lambda a: pltpu.HBM(a.shape, a.dtype)
    sem = pltpu.SemaphoreType.DMA((n * (N_DEV - 1),))
    outs = pl.pallas_call(
        body, name=name,
        out_shape=(sem, sem, pltpu.SemaphoreType.DMA((n,)), *[hbm(a) for a in srcs], *[hbm(a) for a in lands],
                   jax.ShapeDtypeStruct((SUBLANES, LANES), F32)),
        in_specs=[_HBM] * n, out_specs=(_SEM, _SEM, _SEM, *[_HBM] * (2 * n), pl.BlockSpec(memory_space=pltpu.VMEM)),
        input_output_aliases={i: 3 + i for i in range(n)},
        compiler_params=pltpu.CompilerParams(has_side_effects=_EFFECT),
    )(*[pltpu.with_memory_space_constraint(a, pltpu.HBM) for a in srcs])
    return tuple(outs[:3]), list(outs[3:3 + n]), list(outs[3 + n:3 + 2 * n]), outs[-1]


def _exchange_wait(name, sems, srcs, lands, units, per_slot, after):
    m = len(units)
    after = list(after) if isinstance(after, (list, tuple)) else [after]

    def body(*refs):
        src_refs, land_refs, sem_refs = refs[:m], refs[m:2 * m], refs[2 * m:2 * m + 3]
        place = _mesh_place()
        me = _slot(*place)
        for i, u in enumerate(units):
            _own_copy(src_refs[i], land_refs[i], me, per_slot[u], sem_refs, u).wait()
            for j in range(1, N_DEV):
                peer = _peer(place, j)
                pslot = _slot(*peer)
                cp = _remote_copy(src_refs[i], land_refs[i], pslot, per_slot[u], pslot, sem_refs, u, j, peer)
                cp.wait_send()
                cp.wait_recv()

    hbm = lambda a: pltpu.HBM(a.shape, a.dtype)
    outs = pl.pallas_call(
        body, name=name, out_shape=tuple(hbm(a) for a in list(srcs) + list(lands)),
        in_specs=[_HBM] * (2 * m) + [_SEM] * 3 + [pl.BlockSpec(memory_space=pl.ANY)] * len(after),
        out_specs=tuple([_HBM] * (2 * m)),
        input_output_aliases={i: i for i in range(2 * m)},
        compiler_params=pltpu.CompilerParams(has_side_effects=_EFFECT),
    )(*srcs, *lands, *sems, *after)
    return list(outs[m:])


def _adam_update(g, w, m, v):
    m = ADAM_B1 * m + (1.0 - ADAM_B1) * g
    v = ADAM_B2 * v + (1.0 - ADAM_B2) * jnp.square(g)
    m_hat = m / (1.0 - ADAM_B1 ** ADAM_STEP)
    v_hat = v / (1.0 - ADAM_B2 ** ADAM_STEP)
    return -ADAM_LR * (m_hat / (jnp.sqrt(v_hat) + ADAM_EPS) + ADAM_WD * w), m, v


def _adamw(recvs, w, m, v, *, tr, name):
    L, R, C = w.shape
    rp = max(tr, SUBLANES * (4 // jnp.dtype(recvs[0].dtype).itemsize))
    Cp = recvs[0].shape[2]

    def body(*refs):
        r_refs, (w_ref, m_ref, v_ref, g_ref, d_ref, nm_ref, nv_ref) = refs[:L], refs[L:]
        for l in range(L):
            @pl.when(pl.program_id(0) == l)
            def _(r_ref=r_refs[l]):
                g = r_ref[0, :tr, :C].astype(F32)
                for s in range(1, N_DEV):
                    g = g + r_ref[s, :tr, :C].astype(F32)
                d, nm, nv = _adam_update(g, w_ref[0], m_ref[0], v_ref[0])
                g_ref[0], d_ref[0], nm_ref[0], nv_ref[0] = g, d, nm, nv

    blk = pl.BlockSpec((1, tr, C), lambda l, i: (l, i, 0))
    r_specs = [pl.BlockSpec((N_DEV, rp, Cp), lambda l, i, k=k: (0, jnp.where(l == k, i, 0), 0)) for k in range(L)]
    est = 2 * _nbytes((N_DEV, rp, Cp), recvs[0].dtype) + 8 * _nbytes((tr, Cp), F32)
    return pl.pallas_call(
        body, name=name, grid=(L, R // tr),
        in_specs=r_specs + [blk] * 3, out_specs=[blk] * 4,
        out_shape=[jax.ShapeDtypeStruct((L, R, C), F32)] * 4,
        compiler_params=_cparams(est, ("arbitrary", "arbitrary")),
    )(*recvs, w, m, v)


def _adamw_small(recv, w, m, v):
    def body(r_ref, w_ref, m_ref, v_ref, g_ref, d_ref, nm_ref, nv_ref):
        g = r_ref[0]
        for s in range(1, N_DEV):
            g = g + r_ref[s]
        g_ref[...] = g
        d_ref[...], nm_ref[...], nv_ref[...] = _adam_update(g, w_ref[...], m_ref[...], v_ref[...])

    vm = pl.BlockSpec(memory_space=pltpu.VMEM)
    return pl.pallas_call(
        body, name="adamw_small", in_specs=[vm] * 4, out_specs=[vm] * 4,
        out_shape=[jax.ShapeDtypeStruct((SMALL_ROWS, LANES), F32)] * 4,
        compiler_params=_cparams(20 * _nbytes((SMALL_ROWS, LANES), F32)),
    )(recv, w, m, v)


def kernel(x, w_in, conv_w, a_log, dt_bias, o_norm_w, sgu_ln_g, sgu_ln_b, w_s, b_s, w_pa, w_pb, w_o, ln1_g, ln1_b, w_ffn_gate, w_ffn_up, w_ffn_down, ln2_g, ln2_b, loss_target, m_w_in, m_conv_w, m_a_log, m_dt_bias, m_o_norm_w, m_sgu_ln_g, m_sgu_ln_b, m_w_s, m_b_s, m_w_pa, m_w_pb, m_w_o, m_ln1_g, m_ln1_b, m_w_ffn_gate, m_w_ffn_up, m_w_ffn_down, m_ln2_g, m_ln2_b, v_w_in, v_conv_w, v_a_log, v_dt_bias, v_o_norm_w, v_sgu_ln_g, v_sgu_ln_b, v_w_s, v_b_s, v_w_pa, v_w_pb, v_w_o, v_ln1_g, v_ln1_b, v_w_ffn_gate, v_w_ffn_up, v_w_ffn_down, v_ln2_g, v_ln2_b):
    given = dict(locals())
    P = {n: given[n] for n in WEIGHT_NAMES}
    M = {n: given["m_" + n] for n in WEIGHT_NAMES}
    V = {n: given["v_" + n] for n in WEIGHT_NAMES}

    wire = _wire_blocks(P)
    units = [(n, l) for l in range(DEPTH) for n in EARLY + LATE]
    whole = [False] * len(units)
    g_sems, g_srcs, g_lands, g_token = _exchange_start("gather_start", [wire[n][l] for n, l in units], whole)

    one = 1.0 + g_token[0, 0]
    xb = (x[0] * one).astype(BF)
    small_in = [[_small_pack({n: T[n][l] * one for n, _ in SMALL}) for T in (P, M, V)] for l in range(DEPTH)]
    adam_in = {n: (P[n], M[n], V[n]) for n in WIRE}
    adam_in["w_in"], _ = lax.optimization_barrier((adam_in["w_in"], g_token))
    prepared = [xb, *[a for packs in small_in for a in packs], *adam_in["w_in"]]

    def gathered(name, names, l, after):
        idx = [units.index((n, l)) for n in names]
        got = _exchange_wait(name, g_sems, [g_srcs[i] for i in idx], [g_lands[i] for i in idx], idx, whole, after)
        return dict(zip(names, got))

    def layer(l):
        def weights(x_in):
            after = prepared if l == 0 else x_in
            early = _early_weights(gathered(f"gather_wait_early{l}", EARLY, l, after), P, l)
            return early, lambda ya: _late_weights(gathered(f"gather_wait_late{l}", LATE, l, ya))
        return weights

    pending = {}

    def on_grads(l, part, g):
        if part == "late":
            srcs, names = _late_slots(g), LATE
            per_slot = [True] * len(srcs)
        else:
            slots, small = _early_slots(g)
            srcs, names = slots + [small], EARLY + ("small",)
            per_slot = [True] * len(slots) + [False]
        sems, s_thru, l_thru, token = _exchange_start(f"exchange_start_{part}{l}", srcs, per_slot)
        pending[l, part] = (names, sems, s_thru, l_thru, per_slot)
        return token[0, 0]

    loss_local, dx, _ = _local_step(x[0], xb, loss_target[0], [layer(l) for l in range(DEPTH)], on_grads)
    loss = lax.psum(loss_local, MESH_AXES)

    recv = [{} for _ in range(DEPTH)]

    def received(l, part, after):
        names, sems, s_thru, l_thru, per_slot = pending[l, part]
        got = _exchange_wait(f"exchange_wait_{part}{l}", sems, s_thru, l_thru, list(range(len(s_thru))), per_slot, after)
        recv[l].update(zip(names, got))

    out = {}

    def adamw(names):
        for n in names:
            tr, name = ADAM_TILES[n]
            view = _t if n in TRANSPOSED else (lambda a: a)
            res = _adamw([recv[l][n] for l in range(DEPTH)], *[view(a) for a in adam_in[n]], tr=tr, name=name)
            out[n] = [view(r) for r in res]

    for l in reversed(range(DEPTH)):
        received(l, "late", dx)
    adamw(LATE)
    for l in reversed(range(DEPTH)):
        received(l, "early", out[LATE[-1]][0])
    adamw(EARLY)
    small = [_adamw_small(recv[l]["small"], *small_in[l]) for l in range(DEPTH)]
    for n, _ in SMALL:
        out[n] = [jnp.stack([_small_unpack(small[l][i], P)[n] for l in range(DEPTH)]) for i in range(4)]
    return (loss, dx[None], *[out[n][i] for i in range(4) for n in WEIGHT_NAMES])
```

```python
import functools
import math

import jax
import jax.numpy as jnp
from jax import lax
from jax.experimental import pallas as pl
from jax.experimental.pallas import tpu as pltpu

F32 = jnp.float32
BF = jnp.bfloat16
HIGHEST = lax.Precision.HIGHEST

D_MODEL = 1024
DEPTH = 2
N_HEADS = 8
D_HEAD = 128
CONV_K = 4
SGU_BLOCK = 128
SGU_GROUPS = 8
SGU_CHUNK = 64
FFN_HIDDEN = 2816
N_IN = 8208
N_DEV = 8
IN_BLOCK, IN_PAD = N_IN // N_DEV, 1152
FFN_BLOCK, FFN_PAD = FFN_HIDDEN // N_DEV, 384
FFN_K = N_DEV * FFN_PAD
ALPHA = (2 * DEPTH) ** 0.25
LN_EPS = 1e-5
RMS_EPS = 1e-6
ADAM_LR, ADAM_B1, ADAM_B2, ADAM_EPS, ADAM_WD, ADAM_STEP = 0.001, 0.9, 0.999, 1e-08, 0.01, 10

MESH_AXES = ("x", "y", "c")
DELTA_CHUNK = 128
DELTA_HEADS_PER_STEP = 8
LANES = 128
SUBLANES = 8
VMEM_BYTES = 64 * 1024 * 1024
HALO = SUBLANES
HALO_BF = 2 * SUBLANES


def _cparams(est_bytes, dims=None):
    limit = int(min(max(2 * est_bytes + (8 << 20), 32 << 20), VMEM_BYTES - (6 << 20)))
    kw = dict(vmem_limit_bytes=limit)
    if dims is not None:
        kw["dimension_semantics"] = dims
    return pltpu.CompilerParams(**kw)


def _nbytes(shape, dtype):
    return math.prod(shape) * jnp.dtype(dtype).itemsize


def _dims(kind, ndim):
    lhs, rhs = {"nn": (1, 0), "nt": (1, 1), "tn": (0, 0)}[kind]
    b = ndim - 2
    return (((lhs + b,), (rhs + b,)), (tuple(range(b)), tuple(range(b))))


def _mxu(a, b, kind):
    return lax.dot_general(a, b, _dims(kind, a.ndim), preferred_element_type=F32)


def _dot(a, b):
    return _mxu(a.astype(BF), b.astype(BF), "nn")


def _dot_nt(a, b):
    return _mxu(a.astype(BF), b.astype(BF), "nt")


def _dot_tn(a, b):
    return _mxu(a.astype(BF), b.astype(BF), "tn")


def _split(a):
    hi = a.astype(BF)
    return hi, (a - hi.astype(F32)).astype(BF)


def _dot3(a, b, kind):
    (ah, al), (bh, bl) = _split(a), _split(b)
    return _mxu(ah, bh, kind) + (_mxu(ah, bl, kind) + _mxu(al, bh, kind))


def _dotf(a, b):
    return _dot3(a, b, "nn")


def _dotf_nt(a, b):
    return _dot3(a, b, "nt")


def _dot01(sel, x, kind="nn"):
    s = jnp.broadcast_to(sel.astype(BF), x.shape[:-2] + sel.shape)
    h1 = x.astype(BF)
    r1 = x - h1.astype(F32)
    h2 = r1.astype(BF)
    h3 = (r1 - h2.astype(F32)).astype(BF)
    return _mxu(s, h1, kind) + (_mxu(s, h2, kind) + _mxu(s, h3, kind))


def _sigmoid(x):
    return 0.5 * jnp.tanh(0.5 * x) + 0.5


def _silu(x):
    return x * _sigmoid(x)


def _silu_and_grad(x):
    s = _sigmoid(x)
    return x * s, s * (1.0 + x * (1.0 - s))


def _softplus(x):
    return jnp.maximum(x, 0.0) + jnp.log1p(jnp.exp(-jnp.abs(x)))


def _ln(x, g, b):
    mu = jnp.mean(x, -1, keepdims=True)
    xc = x - mu
    var = jnp.mean(xc * xc, -1, keepdims=True)
    return xc * lax.rsqrt(var + LN_EPS) * g + b


def _iota(shape, dim):
    return lax.broadcasted_iota(jnp.int32, shape, dim)


def _tile(n, pref, align):
    if n <= pref:
        return n
    t = (pref // align) * align
    while t >= align:
        if n % t == 0:
            return t
        t -= align
    raise ValueError(f"no tile for {n} (pref {pref}, align {align})")


def _bcast_rows(v, rows=SUBLANES):
    return jnp.broadcast_to(v, (rows, v.shape[-1]))


def _in_hbm(t):
    return pltpu.with_memory_space_constraint(t, pltpu.HBM)


def _mm(a, b, *, mode, name, out_dtype=F32, add=None, add_scale=1.0, tm=512, tn=1024, tk=1024, cols=None, pair=None):
    if mode == "nn":
        (M, K), N = a.shape, b.shape[1]
    elif mode == "nt":
        (M, K), N = a.shape, b.shape[0]
    else:
        (K, M), N = a.shape, b.shape[1]
    col0 = 0
    if cols is not None:
        col0, N = cols
    tm = _tile(M, tm, LANES if mode == "tn" else SUBLANES * 2)
    tn = _tile(N, tn, LANES)
    tk = _tile(K, tk, LANES)
    nk = K // tk
    j0 = col0 // tn
    if mode == "nn":
        a_spec = pl.BlockSpec((tm, tk), lambda i, j, k: (i, k))
        b_spec = pl.BlockSpec((tk, tn), lambda i, j, k: (k, j + j0))
        dot = _dot
    elif mode == "nt":
        a_spec = pl.BlockSpec((tm, tk), lambda i, j, k: (i, k))
        b_spec = pl.BlockSpec((tn, tk), lambda i, j, k: (j, k))
        dot = _dot_nt
    else:
        a_spec = pl.BlockSpec((tk, tm), lambda i, j, k: (k, i))
        b_spec = pl.BlockSpec((tk, tn), lambda i, j, k: (k, j))
        dot = _dot_tn
    o_spec = pl.BlockSpec((tm, tn), lambda i, j, k: (i, j))
    has_add = add is not None

    n_ab = 2 if pair is None else 4

    def body(*refs):
        ab, (o_ref, acc_ref) = refs[:n_ab], refs[-2:]
        add_ref = refs[n_ab] if has_add else None
        k = pl.program_id(2)
        part = dot(ab[0][...], ab[1][...])
        if pair is not None:
            part = part + dot(ab[2][...], ab[3][...])

        def finish(total):
            if has_add:
                total = total + add_scale * add_ref[...]
            o_ref[...] = total.astype(out_dtype)

        if nk == 1:
            finish(part)
        else:
            @pl.when(k == 0)
            def _():
                acc_ref[...] = part

            @pl.when(jnp.logical_and(k > 0, k < nk - 1))
            def _():
                acc_ref[...] += part

            @pl.when(k == nk - 1)
            def _():
                finish(acc_ref[...] + part)

    in_specs = [a_spec, b_spec] * (n_ab // 2) + ([o_spec] if has_add else [])
    args = (a, b) + (tuple(pair) if pair is not None else ()) + ((add,) if has_add else ())
    est = ((n_ab // 2) * (_nbytes((tm, tk), a.dtype) + _nbytes((tk, tn), b.dtype)) + 2 * _nbytes((tm, tn), F32)
           + (_nbytes((tm, tn), F32) if has_add else 0)) + 2 * _nbytes((tm, tn), F32)
    return pl.pallas_call(
        body, name=name,
        grid=(M // tm, N // tn, nk),
        in_specs=in_specs, out_specs=o_spec,
        out_shape=jax.ShapeDtypeStruct((M, N), out_dtype),
        scratch_shapes=[pltpu.VMEM((tm, tn) if nk > 1 else (SUBLANES, LANES), F32)],
        compiler_params=_cparams(est, ("parallel", "parallel", "arbitrary")),
    )(*[_in_hbm(t) for t in args])


def _shifted(xt, halo, first):
    halo = jnp.where(first, 0.0, halo)
    xc = jnp.concatenate([halo, xt], axis=0)
    return [xt] + [pltpu.roll(xc, s, 0)[HALO:] for s in range(1, CONV_K)]


def _conv_taps(shifted, w_ref):
    out = shifted[0] * w_ref[CONV_K - 1:CONV_K, :]
    for s in range(1, CONV_K):
        out = out + shifted[s] * w_ref[CONV_K - 1 - s:CONV_K - s, :]
    return out


def _gates(ba, arow, dtrow):
    lane = _iota(ba.shape, 1)
    beta = _sigmoid(ba)
    g = -jnp.exp(arow) * _softplus(ba + dtrow)
    return jnp.where(lane < N_HEADS, beta, jnp.where(lane < 2 * N_HEADS, g, 0.0))


def _l2n(x):
    return x * lax.rsqrt(jnp.sum(x * x, -1, keepdims=True) + RMS_EPS)


def _qkv_prep(proj, ba, convw, arow, dtrow, *, tm=256):
    S = proj.shape[0]
    tm = _tile(S, tm, SUBLANES)
    W3 = 3 * D_MODEL
    hb = tm // HALO

    def body(xt_ref, halo_ref, ba_ref, w_ref, a_ref, dt_ref, q_ref, k_ref, v_ref, gb_ref, c_ref):
        c = _conv_taps(_shifted(xt_ref[...], halo_ref[...], pl.program_id(0) == 0), w_ref)
        c_ref[...] = c
        c = _silu(c)
        for h in range(N_HEADS):
            lo = h * D_HEAD
            q_ref[:, lo:lo + D_HEAD] = _l2n(c[:, lo:lo + D_HEAD])
            k_ref[:, lo:lo + D_HEAD] = _l2n(c[:, D_MODEL + lo:D_MODEL + lo + D_HEAD])
        v_ref[...] = c[:, 2 * D_MODEL:]
        gb_ref[...] = _gates(ba_ref[...], a_ref[...], dt_ref[...])

    row = lambda w, col=0: pl.BlockSpec((tm, w), lambda i: (i, col))
    full = lambda shape: pl.BlockSpec(shape, lambda i: (0,) * len(shape))
    est = 4 * _nbytes((tm, W3), F32)
    return pl.pallas_call(
        body, name="qkv_prep", grid=(S // tm,),
        in_specs=[row(W3), pl.BlockSpec((HALO, W3), lambda i: (jnp.maximum(i * hb - 1, 0), 0)), row(LANES),
                  full((CONV_K, W3)), full((1, LANES)), full((1, LANES))],
        out_specs=[row(D_MODEL), row(D_MODEL), row(D_MODEL), row(LANES), row(W3)],
        out_shape=[jax.ShapeDtypeStruct((S, D_MODEL), F32)] * 3 + [jax.ShapeDtypeStruct((S, LANES), F32),
                                                                   jax.ShapeDtypeStruct((S, W3), F32)],
        compiler_params=_cparams(est, ("arbitrary",)),
    )(proj, proj, ba, convw, arow, dtrow)


def _qkv_prep_bwd(proj, conv_out, ba, arow, dtrow, dq, dk, dv, dgb, *, tm=256):
    S = proj.shape[0]
    tm = _tile(S, tm, SUBLANES * 2)
    W3 = 3 * D_MODEL
    hb = tm // HALO

    def body(xt_ref, halo_ref, c_ref, ba_ref, a_ref, dt_ref, dq_ref, dk_ref, dv_ref, dgb_ref,
             dcb_ref, dba_ref, dw_ref, da_ref, ddt_ref, dc_ref):
        i = pl.program_id(0)

        @pl.when(i == 0)
        def _():
            dw_ref[...] = jnp.zeros_like(dw_ref)
            da_ref[...] = jnp.zeros_like(da_ref)
            ddt_ref[...] = jnp.zeros_like(ddt_ref)

        shifted = _shifted(xt_ref[...], halo_ref[...], i == 0)
        a, ds = _silu_and_grad(c_ref[...])
        for h in range(N_HEADS):
            for base, d_ref in ((0, dq_ref), (D_MODEL, dk_ref)):
                lo = base + h * D_HEAD
                _, vj = jax.vjp(_l2n, a[:, lo:lo + D_HEAD])
                (dx,) = vj(d_ref[:, h * D_HEAD:(h + 1) * D_HEAD])
                dc_ref[:, lo:lo + D_HEAD] = dx * ds[:, lo:lo + D_HEAD]
        dc_ref[:, 2 * D_MODEL:] = dv_ref[...] * ds[:, 2 * D_MODEL:]
        dc = dc_ref[...]
        dcb_ref[...] = dc.astype(BF)
        for s in range(CONV_K):
            kk = CONV_K - 1 - s
            dw_ref[kk:kk + 1, :] += jnp.sum(dc * shifted[s], axis=0, keepdims=True)
        _, vj = jax.vjp(_gates, ba_ref[...], a_ref[...], dt_ref[...])
        dba, da, ddt = vj(dgb_ref[...])
        dba_ref[...] = dba.astype(BF)
        da_ref[...] += _bcast_rows(da)
        ddt_ref[...] += _bcast_rows(ddt)

    row = lambda w, col=0: pl.BlockSpec((tm, w), lambda i: (i, col))
    full = lambda shape: pl.BlockSpec(shape, lambda i: (0,) * len(shape))
    est = 8 * _nbytes((tm, W3), F32)
    return pl.pallas_call(
        body, name="qkv_prep_bwd", grid=(S // tm,),
        in_specs=[row(W3), pl.BlockSpec((HALO, W3), lambda i: (jnp.maximum(i * hb - 1, 0), 0)), row(W3), row(LANES),
                  full((1, LANES)), full((1, LANES)),
                  row(D_MODEL), row(D_MODEL), row(D_MODEL), row(LANES)],
        out_specs=[row(W3), row(LANES), full((SUBLANES, W3)), full((SUBLANES, LANES)), full((SUBLANES, LANES))],
        out_shape=[jax.ShapeDtypeStruct((S, W3), BF), jax.ShapeDtypeStruct((S, LANES), BF),
                   jax.ShapeDtypeStruct((SUBLANES, W3), F32), jax.ShapeDtypeStruct((SUBLANES, LANES), F32),
                   jax.ShapeDtypeStruct((SUBLANES, LANES), F32)],
        scratch_shapes=[pltpu.VMEM((tm, W3), F32)],
        compiler_params=_cparams(est, ("arbitrary",)),
    )(proj, proj, conv_out, ba, arow, dtrow, dq, dk, dv, dgb)


def _conv_bwd(dc, convw, dproj, *, tm=256):
    S, W3 = dc.shape
    tm = _tile(S, tm, HALO_BF)
    hb = tm // HALO_BF
    nt = S // tm

    def body(dc_ref, nxt_ref, w_ref, dproj_ref, o_ref):
        last = pl.program_id(0) == nt - 1
        nxt = jnp.where(last, 0.0, nxt_ref[...].astype(F32))
        cur = dc_ref[...].astype(F32)
        xc = jnp.concatenate([cur, nxt], axis=0)
        out = cur * w_ref[CONV_K - 1:CONV_K, :]
        for s in range(1, CONV_K):
            out = out + pltpu.roll(xc, tm + HALO_BF - s, 0)[:tm] * w_ref[CONV_K - 1 - s:CONV_K - s, :]
        o_ref[...] = out.astype(BF)

    est = 5 * _nbytes((tm, W3), F32)
    return pl.pallas_call(
        body, name="conv_bwd", grid=(nt,),
        in_specs=[pl.BlockSpec((tm, W3), lambda i: (i, 0)),
                  pl.BlockSpec((HALO_BF, W3), lambda i: (jnp.minimum((i + 1) * hb, S // HALO_BF - 1), 0)),
                  pl.BlockSpec((CONV_K, W3), lambda i: (0, 0)), pl.BlockSpec(memory_space=pl.ANY)],
        out_specs=pl.BlockSpec((tm, W3), lambda i: (i, 0)),
        out_shape=jax.ShapeDtypeStruct(dproj.shape, BF),
        input_output_aliases={3: 0},
        compiler_params=_cparams(est, ("parallel",)),
    )(dc, dc, convw, dproj)


NEUMANN_BLOCK = 8


def _inv_unit_lower(A):
    C = A.shape[-1]
    row, col = _iota((C, C), 0), _iota((C, C), 1)
    eye = jnp.where(row == col, 1.0, 0.0).astype(F32)
    Ab = A.astype(BF)
    sh = jnp.int32(int(math.log2(NEUMANN_BLOCK)))
    B = jnp.where(lax.shift_right_logical(row, sh) == lax.shift_right_logical(col, sh), Ab, jnp.zeros_like(Ab))
    B2 = _mxu(B, B, "nn")
    B4 = _dot3(B2, B2, "nn")
    b2h, b2l = _split(B2)
    P = eye - B.astype(F32) + B2 - (_mxu(B, b2h, "nn") + _mxu(B, b2l, "nn"))
    T = P + _dot3(P, B4, "nn")
    b = NEUMANN_BLOCK
    while b < C:
        hi = ~(2 * b - 1)
        off = ((row & hi) == (col & hi)) & ((row & b) != 0) & ((col & b) == 0)
        Aoff = jnp.where(off, Ab, jnp.zeros_like(Ab))
        th, tl = _split(T)
        xh, xl = _split(_mxu(th, Aoff, "nn") + _mxu(tl, Aoff, "nn"))
        T = T - (_mxu(xh, th, "nn") + (_mxu(xh, tl, "nn") + _mxu(xl, th, "nn")))
        b *= 2
    return T


def _delta_common(q, k, g, beta):
    C = q.shape[-2]
    row, col = _iota((C, C), 0), _iota((C, C), 1)
    tril = row >= col
    qs = q * (D_HEAD ** -0.5)
    gcb = _dot01(jnp.where(tril, 1.0, 0.0), jnp.broadcast_to(g, g.shape[:-1] + (LANES,)))
    gc = gcb[..., :1]
    gr = jnp.swapaxes(gcb, -1, -2)
    Dm = jnp.exp(jnp.where(tril, gc - gr, -1e30))
    Dmt = jnp.exp(jnp.where(row <= col, gr - gc, -1e30))
    eg = jnp.exp(gc)
    gl = jnp.sum(jnp.where(_iota((C, 1), 0) == C - 1, gc, 0.0), axis=(-2, -1), keepdims=True)
    el = jnp.exp(gl)
    er = jnp.exp(gl - gc)
    kb = k * beta
    KK = _dot_nt(kb, k)
    QK = _dot_nt(qs, k)
    return dict(row=row, col=col, tril=tril, qs=qs, gc=gc, Dm=Dm, Dmt=Dmt, eg=eg, el=el, er=er, kb=kb, KK=KK, QK=QK)


def _delta_chunk_fwd(S0, q, k, v, g, beta):
    m = _delta_common(q, k, g, beta)
    T = _inv_unit_lower(jnp.where(m["row"] > m["col"], m["KK"] * m["Dm"], 0.0))
    u = _dotf(T, v * beta)
    w = _dotf(T, m["kb"] * m["eg"])
    vn = u - _dot(w, S0)
    o = _dot(m["qs"] * m["eg"], S0) + _dot(m["QK"] * m["Dm"], vn)
    S1 = S0 * m["el"] + _dot_tn(k * m["er"], vn)
    return o, S1, jnp.swapaxes(T, -1, -2), u, w


def _delta_chunk_bwd(S0, q, k, v, g, beta, Tt, u, w, do, dS1):
    m = _delta_common(q, k, g, beta)
    C = q.shape[-2]
    qs, Dm, Dmt, eg, el, er, kb, KK, QK = (m[n] for n in ("qs", "Dm", "Dmt", "eg", "el", "er", "kb", "KK", "QK"))
    strict = m["row"] > m["col"]
    total = lambda x: jnp.sum(x, axis=(-2, -1), keepdims=True)
    vn = u - _dot(w, S0)
    qg = qs * eg
    kr = k * er

    dvn = _dot(_dot_nt(k, qs) * Dmt, do) + _dot(kr, dS1)
    dS0 = dS1 * el + _dot_tn(qg, do) - _dot_tn(w, dvn)
    d_el = total(dS1 * S0)
    dqg = _dot_nt(do, S0)
    dqs = dqg * eg
    deg = jnp.sum(dqg * qs, -1, keepdims=True)
    dP = _dot_nt(do, vn)
    dPD = dP * Dm
    dqs = dqs + _dot(dPD, k)
    dk = _dot(_dot_nt(vn, do) * Dmt, qs)
    dD = dP * QK
    dkr = _dot_nt(vn, dS1)
    dk = dk + dkr * er
    der = jnp.sum(dkr * k, -1, keepdims=True)
    dw = -_dot_nt(dvn, S0)
    th, tl = _split(Tt)

    def tt_times(x):
        xh, xl = _split(x)
        return _mxu(th, xh, "nn") + (_mxu(th, xl, "nn") + _mxu(tl, xh, "nn"))

    dru = tt_times(dvn)
    drw = tt_times(dw)
    dA = -(_dotf_nt(dru, u) + _dotf_nt(drw, w))
    dAm = jnp.where(strict, dA, 0.0)
    dKK = dAm * Dm
    dkb = _dot(dKK, k)
    dk = dk + _dot_tn(dKK, kb)
    dD = dD + dAm * KK
    dv = dru * beta
    dbeta = jnp.sum(dru * v, -1, keepdims=True)
    dkb = dkb + drw * eg
    deg = deg + jnp.sum(drw * kb, -1, keepdims=True)
    dk = dk + dkb * beta
    dbeta = dbeta + jnp.sum(dkb * k, -1, keepdims=True)
    E = dD * Dm
    dgc = jnp.sum(E, -1, keepdims=True) - jnp.sum(jnp.swapaxes(E, -1, -2), -1, keepdims=True)
    dgc = dgc + deg * eg - der * er
    dgl = total(der * er) + d_el * el
    dgc = dgc + jnp.where(_iota((C, 1), 0) == C - 1, dgl, 0.0)
    triu = jnp.where(m["row"] <= m["col"], 1.0, 0.0)
    dg = _dot01(triu, jnp.broadcast_to(dgc, dgc.shape[:-1] + (LANES,)))[..., :1]
    dq = dqs * (D_HEAD ** -0.5)
    return dq, dk, dv, dg, dbeta, dS0


def _head_cols(gb, h):
    lane = _iota(gb.shape, 1)
    beta = jnp.sum(jnp.where(lane == h, gb, 0.0), -1, keepdims=True)
    g = jnp.sum(jnp.where(lane == N_HEADS + h, gb, 0.0), -1, keepdims=True)
    return g, beta


def _delta_fwd(q, k, v, gb):
    S = q.shape[0]
    C = DELTA_CHUNK
    N = S // C

    HB = DELTA_HEADS_PER_STEP

    def body(q_ref, k_ref, v_ref, gb_ref, o_ref, st_ref, t_ref, u_ref, w_ref, s_scr):
        n, hb = pl.program_id(0), pl.program_id(1)
        gb = gb_ref[...]

        @pl.when(n == 0)
        def _():
            for hh in range(HB):
                s_scr[hb * HB + hh] = jnp.zeros((D_HEAD, D_HEAD), F32)

        heads = [hb * HB + hh for hh in range(HB)]
        cols = [slice(hh * D_HEAD, (hh + 1) * D_HEAD) for hh in range(HB)]
        per_head = lambda ref: jnp.stack([ref[:, c] for c in cols])
        g, beta = (jnp.stack(t) for t in zip(*[_head_cols(gb, h) for h in heads]))
        S0 = jnp.stack([s_scr[h] for h in heads])
        o, S1, Tt, u, w = _delta_chunk_fwd(S0, per_head(q_ref), per_head(k_ref), per_head(v_ref), g, beta)
        for hh in range(HB):
            st_ref[hh, 0] = S0[hh]
            t_ref[hh, 0] = Tt[hh]
            o_ref[:, cols[hh]] = o[hh]
            u_ref[:, cols[hh]] = u[hh]
            w_ref[:, cols[hh]] = w[hh]
            s_scr[heads[hh]] = S1[hh]

    hd = pl.BlockSpec((C, HB * D_HEAD), lambda n, h: (n, h))
    mat = pl.BlockSpec((HB, 1, D_HEAD, D_HEAD), lambda n, h: (h, n, 0, 0))
    est = 40 * HB * _nbytes((C, D_HEAD), F32)
    seq = jax.ShapeDtypeStruct((S, N_HEADS * D_HEAD), F32)
    return pl.pallas_call(
        body, name="delta_fwd", grid=(N, N_HEADS // HB),
        in_specs=[hd, hd, hd, pl.BlockSpec((C, LANES), lambda n, h: (n, 0))],
        out_specs=[hd, mat, mat, hd, hd],
        out_shape=[seq, jax.ShapeDtypeStruct((N_HEADS, N, D_HEAD, D_HEAD), F32),
                   jax.ShapeDtypeStruct((N_HEADS, N, C, C), F32), seq, seq],
        scratch_shapes=[pltpu.VMEM((N_HEADS, D_HEAD, D_HEAD), F32)],
        compiler_params=_cparams(est, ("arbitrary", "arbitrary")),
    )(q, k, v, gb)


def _delta_bwd(q, k, v, gb, st, tinv, u, w, do):
    S = q.shape[0]
    C = DELTA_CHUNK
    N = S // C

    HB = DELTA_HEADS_PER_STEP

    def body(q_ref, k_ref, v_ref, gb_ref, st_ref, t_ref, u_ref, w_ref, do_ref, dq_ref, dk_ref, dv_ref, dgb_ref, ds_scr):
        n, hb = pl.program_id(0), pl.program_id(1)
        gb = gb_ref[...]
        lane = _iota((C, LANES), 1)
        dgb = jnp.zeros((C, LANES), F32)

        @pl.when(n == 0)
        def _():
            for hh in range(HB):
                ds_scr[hb * HB + hh] = jnp.zeros((D_HEAD, D_HEAD), F32)

        heads = [hb * HB + hh for hh in range(HB)]
        cols = [slice(hh * D_HEAD, (hh + 1) * D_HEAD) for hh in range(HB)]
        per_head = lambda ref: jnp.stack([ref[:, c] for c in cols])
        g, beta = (jnp.stack(t) for t in zip(*[_head_cols(gb, h) for h in heads]))
        dS1 = jnp.stack([ds_scr[h] for h in heads])
        dq, dk, dv, dg, dbeta, dS0 = _delta_chunk_bwd(
            st_ref[:, 0], per_head(q_ref), per_head(k_ref), per_head(v_ref), g, beta, t_ref[:, 0],
            per_head(u_ref), per_head(w_ref), per_head(do_ref), dS1)
        for hh, h in enumerate(heads):
            dq_ref[:, cols[hh]] = dq[hh]
            dk_ref[:, cols[hh]] = dk[hh]
            dv_ref[:, cols[hh]] = dv[hh]
            dgb = dgb + jnp.where(lane == h, dbeta[hh], 0.0) + jnp.where(lane == N_HEADS + h, dg[hh], 0.0)
            ds_scr[h] = dS0[hh]

        @pl.when(hb == 0)
        def _():
            dgb_ref[...] = dgb

        @pl.when(hb > 0)
        def _():
            dgb_ref[...] += dgb

    hd = pl.BlockSpec((C, HB * D_HEAD), lambda n, h: (N - 1 - n, h))
    mat = pl.BlockSpec((HB, 1, D_HEAD, D_HEAD), lambda n, h: (h, N - 1 - n, 0, 0))
    gbs = pl.BlockSpec((C, LANES), lambda n, h: (N - 1 - n, 0))
    est = 60 * HB * _nbytes((C, D_HEAD), F32)
    return pl.pallas_call(
        body, name="delta_bwd", grid=(N, N_HEADS // HB),
        in_specs=[hd, hd, hd, gbs, mat, mat, hd, hd, hd],
        out_specs=[hd, hd, hd, gbs],
        out_shape=[jax.ShapeDtypeStruct((S, N_HEADS * D_HEAD), F32)] * 3 + [jax.ShapeDtypeStruct((S, LANES), F32)],
        scratch_shapes=[pltpu.VMEM((N_HEADS, D_HEAD, D_HEAD), F32)],
        compiler_params=_cparams(est, ("arbitrary", "arbitrary")),
    )(q, k, v, gb, st, tinv, u, w, do)


def _ya_head(o, z, onw):
    return o * lax.rsqrt(jnp.mean(o * o, -1, keepdims=True) + RMS_EPS) * onw * _silu(z)


def _norm_cdf(x):
    return 0.5 * (1.0 + lax.erf(x * 0.7071067811865476))


def _norm_pdf(x):
    return jnp.exp(-0.5 * x * x) * 0.3989422804014327


def _chunk_causal(shape, di, dj):
    sh = jnp.int32(int(math.log2(SGU_CHUNK)))
    return lax.shift_right_logical(_iota(shape, di), sh) >= lax.shift_right_logical(_iota(shape, dj), sh)


def _ws_masked(ws):
    return jnp.where(_chunk_causal(ws.shape, 1, 2), ws, 0.0)


def _mix_prep(o, proj, onw, sg, sb, ws, bst, wpa, wpb, *, tm=256):
    S = o.shape[0]
    tm = _tile(S, tm, SGU_BLOCK)

    def body(o_ref, z_ref, u_ref, vg_ref, ga_ref, gb_ref, onw_ref, sg_ref, sb_ref, ws_ref, bst_ref, wa_ref, wb_ref,
             ya_ref, yb_ref, phi_ref, pa_ref, pb_ref, m_ref):
        onw = onw_ref[...]
        for h in range(N_HEADS):
            sl = slice(h * D_HEAD, (h + 1) * D_HEAD)
            ya_ref[:, sl] = _ya_head(o_ref[:, sl], z_ref[:, sl].astype(F32), onw).astype(BF)
        u, vg = u_ref[...].astype(F32), vg_ref[...].astype(F32)
        phi_u, phi_v = _norm_cdf(u), _norm_cdf(vg)
        phi_ref[:, :D_MODEL] = phi_u
        phi_ref[:, D_MODEL:] = phi_v
        ua, vl = u * phi_u, _ln(vg * phi_v, sg_ref[...], sb_ref[...])
        wsm = _ws_masked(ws_ref[...])
        bst = bst_ref[...]
        for blk in range(tm // SGU_BLOCK):
            rs = slice(blk * SGU_BLOCK, (blk + 1) * SGU_BLOCK)
            for gi in range(SGU_GROUPS):
                cs = slice(gi * D_HEAD, (gi + 1) * D_HEAD)
                sp = _dot(wsm[gi], vl[rs, cs]) + bst[:, gi:gi + 1]
                yb_ref[rs, cs] = (ua[rs, cs] * sp).astype(BF)
        pa = _dot(ya_ref[...], wa_ref[...]).astype(BF)
        pb = _dot(yb_ref[...], wb_ref[...]).astype(BF)
        pa_ref[...] = pa
        pb_ref[...] = pb
        m_ref[...] = (_sigmoid(ga_ref[...].astype(F32)) * pa.astype(F32)
                      + _sigmoid(gb_ref[...].astype(F32)) * pb.astype(F32)).astype(BF)

    blk = lambda col: pl.BlockSpec((tm, D_MODEL), lambda i: (i, col))
    full = lambda shape: pl.BlockSpec(shape, lambda i: (0,) * len(shape))
    est = 14 * _nbytes((tm, D_MODEL), F32) + 4 * _nbytes((D_MODEL, D_MODEL), BF)
    return pl.pallas_call(
        body, name="mix_prep", grid=(S // tm,),
        in_specs=[blk(0), blk(0), blk(1), blk(2), blk(3), blk(4), full((1, D_HEAD)), full((1, D_MODEL)),
                  full((1, D_MODEL)), full((SGU_GROUPS, SGU_BLOCK, SGU_BLOCK)), full((SGU_BLOCK, LANES)),
                  full((D_MODEL, D_MODEL)), full((D_MODEL, D_MODEL))],
        out_specs=[blk(0), blk(0), pl.BlockSpec((tm, 2 * D_MODEL), lambda i: (i, 0)), blk(0), blk(0), blk(0)],
        out_shape=[jax.ShapeDtypeStruct((S, D_MODEL), BF)] * 2 + [jax.ShapeDtypeStruct((S, 2 * D_MODEL), F32)]
                  + [jax.ShapeDtypeStruct((S, D_MODEL), BF)] * 3,
        compiler_params=_cparams(est, ("parallel",)),
    )(o, proj, proj, proj, proj, proj, onw, sg, sb, ws, bst, wpa, wpb)


def _mix_prep_bwd(o, proj, phi, onw, sg, sb, ws, bst, dpa, wpa, dpb, wpb, dproj, *, tm=256):
    S = o.shape[0]
    tm = _tile(S, tm, SGU_BLOCK)

    def body(o_ref, z_ref, u_ref, vg_ref, phi_ref, onw_ref, sg_ref, sb_ref, ws_ref, bst_ref, dpa_ref, wpa_ref, dpb_ref,
             wpb_ref, dproj_in, do_ref, dzuv_ref, donw_ref, dsg_ref, dsb_ref, dws_ref, dbst_ref, dvl_scr, dua_scr,
             dya_ref, dyb_ref):
        dz_ref, du_ref, dvg_ref = (dzuv_ref.at[:, k * D_MODEL:(k + 1) * D_MODEL] for k in range(3))
        @pl.when(pl.program_id(0) == 0)
        def _():
            for r in (donw_ref, dsg_ref, dsb_ref, dws_ref, dbst_ref):
                r[...] = jnp.zeros_like(r)

        dya_ref[...] = _dot_nt(dpa_ref[...], wpa_ref[...])
        dyb_ref[...] = _dot_nt(dpb_ref[...], wpb_ref[...])

        onw = onw_ref[...]
        donw = jnp.zeros((1, D_HEAD), F32)
        for h in range(N_HEADS):
            sl = slice(h * D_HEAD, (h + 1) * D_HEAD)
            _, vj = jax.vjp(_ya_head, o_ref[:, sl], z_ref[:, sl].astype(F32), onw)
            do_h, dz_h, donw_h = vj(dya_ref[:, sl])
            do_ref[:, sl] = do_h.astype(BF)
            dz_ref[:, sl] = dz_h.astype(BF)
            donw = donw + donw_h
        donw_ref[...] += _bcast_rows(donw)

        u, vg = u_ref[...].astype(F32), vg_ref[...].astype(F32)
        phi_u, phi_v = phi_ref[:, :D_MODEL], phi_ref[:, D_MODEL:]
        ua = u * phi_u
        vl, vj = jax.vjp(_ln, vg * phi_v, sg_ref[...], sb_ref[...])
        wsm = _ws_masked(ws_ref[...])
        bst = bst_ref[...]
        lane = _iota((SGU_BLOCK, LANES), 1)
        dbst = jnp.zeros((SGU_BLOCK, LANES), F32)
        cmask = _chunk_causal((SGU_BLOCK, SGU_BLOCK), 0, 1)
        for gi in range(SGU_GROUPS):
            cs = slice(gi * D_HEAD, (gi + 1) * D_HEAD)
            wg = wsm[gi]
            wgt = jnp.transpose(wg)
            dwg = jnp.zeros((SGU_BLOCK, SGU_BLOCK), F32)
            for blk in range(tm // SGU_BLOCK):
                rs = slice(blk * SGU_BLOCK, (blk + 1) * SGU_BLOCK)
                sp = _dot(wg, vl[rs, cs]) + bst[:, gi:gi + 1]
                dyb = dyb_ref[rs, cs]
                dsp = dyb * ua[rs, cs]
                dua_scr[rs, cs] = dyb * sp
                dvl_scr[rs, cs] = _dot(wgt, dsp)
                dwg = dwg + _dot_nt(dsp, vl[rs, cs])
                dbst = dbst + jnp.where(lane == gi, jnp.sum(dsp, -1, keepdims=True), 0.0)
            dws_ref[gi] += jnp.where(cmask, dwg, 0.0)
        dbst_ref[...] += dbst
        dgv, dsg, dsb = vj(dvl_scr[...])
        du_ref[...] = (dua_scr[...] * (phi_u + u * _norm_pdf(u))).astype(BF)
        dvg_ref[...] = (dgv * (phi_v + vg * _norm_pdf(vg))).astype(BF)
        dsg_ref[...] += _bcast_rows(dsg)
        dsb_ref[...] += _bcast_rows(dsb)

    blk = lambda col: pl.BlockSpec((tm, D_MODEL), lambda i: (i, col))
    full = lambda shape: pl.BlockSpec(shape, lambda i: (0,) * len(shape))
    est = 18 * _nbytes((tm, D_MODEL), F32) + 4 * _nbytes((D_MODEL, D_MODEL), BF)
    outs = pl.pallas_call(
        body, name="mix_prep_bwd", grid=(S // tm,),
        in_specs=[blk(0), blk(0), blk(1), blk(2), pl.BlockSpec((tm, 2 * D_MODEL), lambda i: (i, 0)),
                  full((1, D_HEAD)), full((1, D_MODEL)), full((1, D_MODEL)),
                  full((SGU_GROUPS, SGU_BLOCK, SGU_BLOCK)), full((SGU_BLOCK, LANES)),
                  blk(0), full((D_MODEL, D_MODEL)), blk(0), full((D_MODEL, D_MODEL)),
                  pl.BlockSpec(memory_space=pl.ANY)],
        out_specs=[blk(0), pl.BlockSpec((tm, 3 * D_MODEL), lambda i: (i, 1)),
                   full((SUBLANES, D_HEAD)), full((SUBLANES, D_MODEL)), full((SUBLANES, D_MODEL)),
                   full((SGU_GROUPS, SGU_BLOCK, SGU_BLOCK)), full((SGU_BLOCK, LANES))],
        out_shape=[jax.ShapeDtypeStruct((S, D_MODEL), BF), jax.ShapeDtypeStruct(dproj.shape, BF),
                   jax.ShapeDtypeStruct((SUBLANES, D_HEAD), F32), jax.ShapeDtypeStruct((SUBLANES, D_MODEL), F32),
                   jax.ShapeDtypeStruct((SUBLANES, D_MODEL), F32),
                   jax.ShapeDtypeStruct((SGU_GROUPS, SGU_BLOCK, SGU_BLOCK), F32),
                   jax.ShapeDtypeStruct((SGU_BLOCK, LANES), F32)],
        input_output_aliases={14: 1},
        scratch_shapes=[pltpu.VMEM((tm, D_MODEL), F32)] * 4,
        compiler_params=_cparams(est, ("arbitrary",)),
    )(o, proj, proj, proj, phi, onw, sg, sb, ws, bst, dpa, wpa, dpb, wpb, dproj)
    return outs


def _mm_gate_merge_bwd(dmix, wo, pa, pb, proj, *, tm=512):
    S = pa.shape[0]
    tm = _tile(S, tm, SUBLANES * 2)

    def body(d_ref, w_ref, pa_ref, pb_ref, ga_ref, gb_ref, dpa_ref, dpb_ref, dg_ref):
        dm = _dot_nt(d_ref[...], w_ref[...])
        sa, sb = _sigmoid(ga_ref[...].astype(F32)), _sigmoid(gb_ref[...].astype(F32))
        dpa_ref[...] = (dm * sa).astype(BF)
        dpb_ref[...] = (dm * sb).astype(BF)
        dg_ref[:, :D_MODEL] = (dm * pa_ref[...].astype(F32) * sa * (1.0 - sa)).astype(BF)
        dg_ref[:, D_MODEL:] = (dm * pb_ref[...].astype(F32) * sb * (1.0 - sb)).astype(BF)

    blk = lambda col: pl.BlockSpec((tm, D_MODEL), lambda i: (i, col))
    est = _nbytes((D_MODEL, D_MODEL), BF) + 10 * _nbytes((tm, D_MODEL), F32)
    return pl.pallas_call(
        body, name="mm_gate_merge_bwd", grid=(S // tm,),
        in_specs=[blk(0), pl.BlockSpec((D_MODEL, D_MODEL), lambda i: (0, 0)), blk(0), blk(0), blk(3), blk(4)],
        out_specs=[blk(0), blk(0), pl.BlockSpec((tm, 2 * D_MODEL), lambda i: (i, 3))],
        out_shape=[jax.ShapeDtypeStruct((S, D_MODEL), BF)] * 2 + [jax.ShapeDtypeStruct((S, 8 * D_MODEL), BF)],
        compiler_params=_cparams(est, ("parallel",)),
    )(dmix, wo, pa, pb, proj, proj)


def _mm_swiglu(xb, wgt, wut, *, tm=1024, tn=768):
    S, K = xb.shape
    tm = _tile(S, tm, SUBLANES * 2)
    tn = _tile(FFN_K, tn, LANES)

    def body(x_ref, wg_ref, wu_ref, hg_ref, hu_ref, h_ref):
        x = x_ref[...]
        hg = _dot_nt(x, wg_ref[...]).astype(BF)
        hu = _dot_nt(x, wu_ref[...]).astype(BF)
        hg_ref[...] = hg
        hu_ref[...] = hu
        h_ref[...] = (_silu(hg.astype(F32)) * hu.astype(F32)).astype(BF)

    out = pl.BlockSpec((tm, tn), lambda i, j: (i, j))
    est = _nbytes((tm, K), BF) + 2 * _nbytes((K, tn), BF) + 6 * _nbytes((tm, tn), F32)
    return pl.pallas_call(
        body, name="mm_swiglu", grid=(S // tm, FFN_K // tn),
        in_specs=[pl.BlockSpec((tm, K), lambda i, j: (i, 0)), pl.BlockSpec((tn, K), lambda i, j: (j, 0)),
                  pl.BlockSpec((tn, K), lambda i, j: (j, 0))],
        out_specs=[out] * 3, out_shape=[jax.ShapeDtypeStruct((S, FFN_K), BF)] * 3,
        compiler_params=_cparams(est, ("parallel", "parallel")),
    )(xb, wgt, wut)


def _mm_swiglu_bwd(dffn, wd, hg, hu, *, tm=1024, tn=768):
    S, K = dffn.shape
    tm = _tile(S, tm, SUBLANES * 2)
    tn = _tile(FFN_K, tn, LANES)

    def body(d_ref, w_ref, hg_ref, hu_ref, dhg_ref, dhu_ref):
        dh = _dot_nt(d_ref[...], w_ref[...])
        act, dact = _silu_and_grad(hg_ref[...].astype(F32))
        dhg_ref[...] = (dh * hu_ref[...].astype(F32) * dact).astype(BF)
        dhu_ref[...] = (dh * act).astype(BF)

    out = pl.BlockSpec((tm, tn), lambda i, j: (i, j))
    est = _nbytes((tm, K), dffn.dtype) + _nbytes((tn, K), BF) + 8 * _nbytes((tm, tn), F32)
    return pl.pallas_call(
        body, name="mm_swiglu_bwd", grid=(S // tm, FFN_K // tn),
        in_specs=[pl.BlockSpec((tm, K), lambda i, j: (i, 0)), pl.BlockSpec((tn, K), lambda i, j: (j, 0)), out, out],
        out_specs=[out, out], out_shape=[jax.ShapeDtypeStruct((S, FFN_K), BF)] * 2,
        compiler_params=_cparams(est, ("parallel", "parallel")),
    )(dffn, wd, hg, hu)


def _mm_resid_ln(a, bmat, x, g, b, *, name, tm=512):
    S, K = a.shape
    tm = _tile(S, tm, SUBLANES * 2)

    def body(a_ref, w_ref, x_ref, g_ref, b_ref, pre_ref, y_ref, yb_ref):
        pre = ALPHA * x_ref[...] + _dot(a_ref[...], w_ref[...])
        y = _ln(pre, g_ref[...], b_ref[...])
        pre_ref[...] = pre
        y_ref[...] = y
        yb_ref[...] = y.astype(BF)

    blk = pl.BlockSpec((tm, D_MODEL), lambda i: (i, 0))
    vec = pl.BlockSpec((1, D_MODEL), lambda i: (0, 0))
    est = _nbytes((tm, K), BF) + _nbytes((K, D_MODEL), BF) + 8 * _nbytes((tm, D_MODEL), F32)
    return pl.pallas_call(
        body, name=name, grid=(S // tm,),
        in_specs=[pl.BlockSpec((tm, K), lambda i: (i, 0)), pl.BlockSpec((K, D_MODEL), lambda i: (0, 0)), blk, vec, vec],
        out_specs=[blk, blk, blk],
        out_shape=[jax.ShapeDtypeStruct((S, D_MODEL), F32)] * 2 + [jax.ShapeDtypeStruct((S, D_MODEL), BF)],
        compiler_params=_cparams(est, ("parallel",)),
    )(a, bmat, x, g, b)


def _ln_bwd(pre, g, b, dy, *, tm=512):
    S = pre.shape[0]
    tm = _tile(S, tm, SUBLANES)

    def body(p_ref, g_ref, b_ref, dy_ref, dp_ref, dg_ref, db_ref):
        @pl.when(pl.program_id(0) == 0)
        def _():
            dg_ref[...] = jnp.zeros_like(dg_ref)
            db_ref[...] = jnp.zeros_like(db_ref)

        _, vj = jax.vjp(_ln, p_ref[...], g_ref[...], b_ref[...])
        dp, dg, db = vj(dy_ref[...])
        dp_ref[...] = dp
        dg_ref[...] += _bcast_rows(dg)
        db_ref[...] += _bcast_rows(db)

    blk = pl.BlockSpec((tm, D_MODEL), lambda i: (i, 0))
    vec = pl.BlockSpec((1, D_MODEL), lambda i: (0, 0))
    acc = pl.BlockSpec((SUBLANES, D_MODEL), lambda i: (0, 0))
    return pl.pallas_call(
        body, name="ln_bwd", grid=(S // tm,),
        in_specs=[blk, vec, vec, blk], out_specs=[blk, acc, acc],
        out_shape=[jax.ShapeDtypeStruct((S, D_MODEL), F32)] + [jax.ShapeDtypeStruct((SUBLANES, D_MODEL), F32)] * 2,
        compiler_params=_cparams(10 * _nbytes((tm, D_MODEL), F32), ("arbitrary",)),
    )(pre, g, b, dy)


def _loss_ln_bwd(y, tgt, pre, g, b, *, tm=512):
    S = y.shape[0]
    tm = _tile(S, tm, SUBLANES)

    def body(y_ref, t_ref, p_ref, g_ref, b_ref, dp_ref, dg_ref, db_ref, l_ref):
        @pl.when(pl.program_id(0) == 0)
        def _():
            for r in (dg_ref, db_ref, l_ref):
                r[...] = jnp.zeros_like(r)

        e = y_ref[...] - t_ref[...]
        l_ref[...] += 0.5 * jnp.sum(jnp.mean(e * e, -1, keepdims=True), keepdims=True)
        _, vj = jax.vjp(_ln, p_ref[...], g_ref[...], b_ref[...])
        dp, dg, db = vj(e * (1.0 / D_MODEL))
        dp_ref[...] = dp
        dg_ref[...] += _bcast_rows(dg)
        db_ref[...] += _bcast_rows(db)

    blk = pl.BlockSpec((tm, D_MODEL), lambda i: (i, 0))
    vec = pl.BlockSpec((1, D_MODEL), lambda i: (0, 0))
    acc = pl.BlockSpec((SUBLANES, D_MODEL), lambda i: (0, 0))
    return pl.pallas_call(
        body, name="loss_ln_bwd", grid=(S // tm,),
        in_specs=[blk, blk, blk, vec, vec], out_specs=[blk, acc, acc, pl.BlockSpec((SUBLANES, LANES), lambda i: (0, 0))],
        out_shape=[jax.ShapeDtypeStruct((S, D_MODEL), F32)] + [jax.ShapeDtypeStruct((SUBLANES, D_MODEL), F32)] * 2
                  + [jax.ShapeDtypeStruct((SUBLANES, LANES), F32)],
        compiler_params=_cparams(12 * _nbytes((tm, D_MODEL), F32), ("arbitrary",)),
    )(y, tgt, pre, g, b)


def _layer_fwd(x, xb, w, late):
    pq = _mm(xb, w["win"], mode="nn", name="mm_in_qkv", tm=1024, tn=1024, cols=(0, 3 * D_MODEL))
    proj = _mm(xb, w["win"], mode="nn", name="mm_in_rest", tm=1024, tn=1024, cols=(3 * D_MODEL, 5 * D_MODEL), out_dtype=BF)
    ba = _mm(xb, w["wba"], mode="nn", name="mm_in_ba", tm=1024, tn=LANES)
    qn, kn, vv, gb, conv_out = _qkv_prep(pq, ba, w["convw"], w["arow"], w["dtrow"])
    o, st, tinv, wy_u, wy_w = _delta_fwd(qn, kn, vv, gb)
    w = {**w, **late(o)}
    ya, yb, phi, pa, pb, m = _mix_prep(o, proj, w["onw"], w["sg"], w["sb"], w["ws"], w["bst"], w["wpa"], w["wpb"])
    pre1, x1, x1b = _mm_resid_ln(m, w["wo"], x, w["ln1g"], w["ln1b"], name="mm_out_ln")
    hg, hu, h = _mm_swiglu(x1b, w["wgt"], w["wut"])
    pre2, x2, x2b = _mm_resid_ln(h, w["wd"], x1, w["ln2g"], w["ln2b"], name="mm_down_ln")
    saved = dict(xb=xb, pq=pq, conv_out=conv_out, proj=proj, phi=phi, ba=ba, qn=qn, kn=kn, vv=vv, gb=gb, o=o, st=st, tinv=tinv, wy_u=wy_u, wy_w=wy_w,
                 ya=ya, yb=yb,
                 pa=pa, pb=pb, m=m, pre1=pre1, x1b=x1b, hg=hg, hu=hu, h=h, pre2=pre2)
    return x2, x2b, saved, w


def _layer_bwd(dpre2, ln2_grads, w, s, on_part=None):
    g = {}
    started = lambda part: on_part(part, g) if on_part is not None else None
    after = lambda v, token: v if token is None else v + token.astype(v.dtype)
    g["ln2g"], g["ln2b"] = ln2_grads
    dhg, dhu = _mm_swiglu_bwd(dpre2, w["wd"], s["hg"], s["hu"])
    g["wd"] = _mm(s["h"], dpre2, mode="tn", name="mm_tn_down", tm=1536, tk=1024, out_dtype=BF)
    dx1 = _mm(dhg, w["wgt"], mode="nn", name="mm_nn_gu", pair=(dhu, w["wut"]), add=dpre2, add_scale=ALPHA, tm=1024, tk=1536)
    g["wgt"] = _mm(dhg, s["x1b"], mode="tn", name="mm_tn_gu", tm=1536, tn=1024, tk=2048, out_dtype=BF)
    g["wut"] = _mm(dhu, s["x1b"], mode="tn", name="mm_tn_gu", tm=1536, tn=1024, tk=2048, out_dtype=BF)
    dpre1, g["ln1g"], g["ln1b"] = _ln_bwd(s["pre1"], w["ln1g"], w["ln1b"], dx1)
    g["wo"] = _mm(s["m"], dpre1, mode="tn", name="mm_tn_sq", tm=1024, tk=1024, out_dtype=BF)
    dpa, dpb, dproj = _mm_gate_merge_bwd(dpre1, w["wo"], s["pa"], s["pb"], s["proj"])
    g["wpa"] = _mm(s["ya"], dpa, mode="tn", name="mm_tn_sq", tm=1024, tk=1024, out_dtype=BF)
    g["wpb"] = _mm(s["yb"], dpb, mode="tn", name="mm_tn_sq", tm=1024, tk=1024, out_dtype=BF)
    do, dproj, g["onw"], g["sg"], g["sb"], g["ws"], g["bst"] = _mix_prep_bwd(
        s["o"], s["proj"], s["phi"], after(w["onw"], started("late")), w["sg"], w["sb"], w["ws"], w["bst"],
        dpa, w["wpa"], dpb, w["wpb"], dproj)
    dqn, dkn, dvv, dgb = _delta_bwd(s["qn"], s["kn"], s["vv"], s["gb"], s["st"], s["tinv"], s["wy_u"], s["wy_w"], do)
    dc, dba, g["convw"], g["arow"], g["dtrow"] = _qkv_prep_bwd(
        s["pq"], s["conv_out"], s["ba"], w["arow"], w["dtrow"], dqn, dkn, dvv, dgb)
    dproj = _conv_bwd(dc, w["convw"], dproj)
    g["win"] = _mm(s["xb"], dproj, mode="tn", name="mm_tn_in", tm=1024, tn=1024, tk=2048, out_dtype=BF)
    g["wba"] = _mm(s["xb"], dba, mode="tn", name="mm_tn_ba", tm=1024, tn=LANES, tk=1024, out_dtype=BF)
    dx = _mm(dba, after(w["wba"], started("early")), mode="nt", name="mm_nt_ba", add=dpre1, add_scale=ALPHA, tm=1024)
    dx = _mm(dproj, w["win"], mode="nt", name="mm_nt_in", add=dx, add_scale=1.0, tm=1024, tk=2048)
    return dx, g


def _local_step(x, xb, tgt, layers, on_grads=None):
    saved, weights = [], []
    for layer in layers:
        x, xb, s, w = _layer_fwd(x, xb, *layer(x))
        saved.append(s)
        weights.append(w)
    last = len(layers) - 1
    dpre2, dg, db, lacc = _loss_ln_bwd(x, tgt, saved[last]["pre2"], weights[last]["ln2g"], weights[last]["ln2b"])
    grads = [None] * len(layers)
    for l in reversed(range(len(layers))):
        on_part = functools.partial(on_grads, l) if on_grads is not None else None
        dx, grads[l] = _layer_bwd(dpre2, (dg, db), weights[l], saved[l], on_part)
        if l > 0:
            dpre2, dg, db = _ln_bwd(saved[l - 1]["pre2"], weights[l - 1]["ln2g"], weights[l - 1]["ln2b"], dx)
    return lacc[0, 0], dx, grads


_QKVZ = 4 * D_MODEL
_BA = 2 * N_HEADS


WEIGHT_NAMES = ("w_in", "conv_w", "a_log", "dt_bias", "o_norm_w", "sgu_ln_g", "sgu_ln_b", "w_s", "b_s", "w_pa", "w_pb",
                "w_o", "ln1_g", "ln1_b", "w_ffn_gate", "w_ffn_up", "w_ffn_down", "ln2_g", "ln2_b")
WIRE = ("w_in", "w_ffn_gate", "w_ffn_up", "w_ffn_down", "w_pa", "w_pb", "w_o", "conv_w")
SMALL = (("a_log", N_HEADS), ("dt_bias", N_HEADS), ("o_norm_w", D_HEAD), ("sgu_ln_g", D_MODEL), ("sgu_ln_b", D_MODEL),
         ("w_s", SGU_GROUPS * SGU_BLOCK * SGU_BLOCK), ("b_s", SGU_GROUPS * SGU_BLOCK),
         ("ln1_g", D_MODEL), ("ln1_b", D_MODEL), ("ln2_g", D_MODEL), ("ln2_b", D_MODEL))
SMALL_ROWS = -(-sum(n for _, n in SMALL) // (LANES * SUBLANES)) * SUBLANES
N_MAIN_TILES = (N_IN - _BA) // D_MODEL
ADAM_TILES = dict(w_in=(128, "adamw_in"), w_ffn_gate=(32, "adamw_ffn_rows"), w_ffn_up=(32, "adamw_ffn_rows"),
                  w_ffn_down=(32, "adamw_ffn_rows"), w_pa=(128, "adamw_sq"), w_pb=(128, "adamw_sq"), w_o=(128, "adamw_sq"),
                  conv_w=(CONV_K, "adamw_conv"))


def _pad_to(a, axis, size):
    pads = [(0, 0)] * a.ndim
    pads[axis] = (0, size - a.shape[axis])
    return jnp.pad(a, pads)


def _t(a):
    return jnp.swapaxes(a, 1, 2)


def _wire_blocks(p):
    return dict(
        w_in=_pad_to(p["w_in"].astype(BF), 2, IN_PAD),
        w_ffn_gate=_pad_to(_t(p["w_ffn_gate"]).astype(BF), 1, FFN_PAD), w_ffn_up=_pad_to(_t(p["w_ffn_up"]).astype(BF), 1, FFN_PAD),
        w_ffn_down=_pad_to(p["w_ffn_down"].astype(BF), 1, FFN_PAD),
        w_pa=p["w_pa"].astype(BF), w_pb=p["w_pb"].astype(BF), w_o=p["w_o"].astype(BF),
        conv_w=_pad_to(p["conv_w"], 1, SUBLANES),
    )


def _by_columns(blocks):
    n, r, c = blocks.shape
    return jnp.transpose(blocks, (1, 0, 2)).reshape(r, n * c)


def _to_slots(full, c):
    r = full.shape[0]
    return jnp.transpose(full.reshape(r, N_DEV, c), (1, 0, 2))


def _lane_row(v, at):
    return jnp.pad(v[None], ((0, 0), (at, LANES - at - v.shape[0])))


TRANSPOSED = ("w_ffn_gate", "w_ffn_up")
EARLY = ("w_in", "conv_w")
LATE = ("w_pa", "w_pb", "w_o", "w_ffn_gate", "w_ffn_up", "w_ffn_down")


def _early_weights(stacks, p, l):
    return dict(
        win=_perm_in(stacks["w_in"], D_MODEL, N_MAIN_TILES), wba=_perm_in(stacks["w_in"], LANES, 1),
        convw=_by_columns(stacks["conv_w"][:, :CONV_K]),
        arow=_lane_row(p["a_log"][l], N_HEADS), dtrow=_lane_row(p["dt_bias"][l], N_HEADS),
        onw=p["o_norm_w"][l][None], sg=p["sgu_ln_g"][l][None], sb=p["sgu_ln_b"][l][None],
        ws=p["w_s"][l], bst=_pad_to(p["b_s"][l].T, 1, LANES),
        ln1g=p["ln1_g"][l][None], ln1b=p["ln1_b"][l][None], ln2g=p["ln2_g"][l][None], ln2b=p["ln2_b"][l][None],
    )


def _late_weights(stacks):
    return dict(
        wpa=stacks["w_pa"].reshape(D_MODEL, D_MODEL), wpb=stacks["w_pb"].reshape(D_MODEL, D_MODEL),
        wo=stacks["w_o"].reshape(D_MODEL, D_MODEL),
        wgt=stacks["w_ffn_gate"].reshape(FFN_K, D_MODEL), wut=stacks["w_ffn_up"].reshape(FFN_K, D_MODEL),
        wd=stacks["w_ffn_down"].reshape(FFN_K, D_MODEL),
    )


def _small_pack(parts):
    flat = jnp.concatenate([parts[n].reshape(-1) for n, _ in SMALL])
    return _pad_to(flat, 0, SMALL_ROWS * LANES).reshape(SMALL_ROWS, LANES)


def _small_unpack(rows, like):
    flat, out, off = rows.reshape(-1), {}, 0
    for n, size in SMALL:
        out[n] = flat[off:off + size].reshape(like[n].shape[1:])
        off += size
    return out


def _late_slots(g):
    slots = dict(
        w_ffn_gate=g["wgt"].reshape(N_DEV, FFN_PAD, D_MODEL), w_ffn_up=g["wut"].reshape(N_DEV, FFN_PAD, D_MODEL),
        w_ffn_down=g["wd"].reshape(N_DEV, FFN_PAD, D_MODEL),
        w_pa=g["wpa"].reshape(N_DEV, D_MODEL // N_DEV, D_MODEL), w_pb=g["wpb"].reshape(N_DEV, D_MODEL // N_DEV, D_MODEL),
        w_o=g["wo"].reshape(N_DEV, D_MODEL // N_DEV, D_MODEL),
    )
    return [slots[n] for n in LATE]


def _early_slots(g):
    slots = [_perm_out(g["win"], g["wba"]), _pad_to(_to_slots(g["convw"][:CONV_K], 3 * D_MODEL // N_DEV), 1, SUBLANES)]
    small = _small_pack(dict(
        a_log=g["arow"][0, N_HEADS:2 * N_HEADS], dt_bias=g["dtrow"][0, N_HEADS:2 * N_HEADS], o_norm_w=g["onw"][0],
        sgu_ln_g=g["sg"][0], sgu_ln_b=g["sb"][0], w_s=g["ws"], b_s=g["bst"][:, :SGU_GROUPS].T,
        ln1_g=g["ln1g"][0], ln1_b=g["ln1b"][0], ln2_g=g["ln2g"][0], ln2_b=g["ln2b"][0]))
    return slots, small


def _in_tile_start(j, tile_w):
    if tile_w == LANES:
        return jnp.int32(_QKVZ)
    return j * D_MODEL + jnp.where(j >= _QKVZ // D_MODEL, _BA, 0)


def _select(rows_iota, cols_iota, dev, start, valid):
    hit = (rows_iota + (dev * IN_BLOCK - start) == cols_iota) & (rows_iota < IN_BLOCK) & (cols_iota < valid)
    return jnp.where(hit, 1.0, 0.0).astype(BF)


def _perm_in(stack, tile_w, n_tiles):
    valid = _BA if tile_w == LANES else tile_w

    def first_dev(j):
        return lax.div(_in_tile_start(j, tile_w), jnp.int32(IN_BLOCK))

    def body(w_ref, o_ref, acc_ref):
        j, k = pl.program_id(0), pl.program_id(1)
        sel = _select(_iota((IN_PAD, tile_w), 0), _iota((IN_PAD, tile_w), 1), first_dev(j) + k,
                      _in_tile_start(j, tile_w), valid)
        part = jnp.dot(w_ref[0], sel, preferred_element_type=F32)

        @pl.when(k == 0)
        def _():
            acc_ref[...] = part

        @pl.when(k == 1)
        def _():
            o_ref[...] = (acc_ref[...] + part).astype(BF)

    est = _nbytes((D_MODEL, IN_PAD), BF) + 3 * _nbytes((D_MODEL, tile_w), F32) + 2 * _nbytes((IN_PAD, tile_w), F32)
    return pl.pallas_call(
        body, name="perm_in" if tile_w != LANES else "perm_in_ba", grid=(n_tiles, 2),
        in_specs=[pl.BlockSpec((1, D_MODEL, IN_PAD), lambda j, k: (jnp.minimum(first_dev(j) + k, N_DEV - 1), 0, 0))],
        out_specs=pl.BlockSpec((D_MODEL, tile_w), lambda j, k: (0, j)),
        out_shape=jax.ShapeDtypeStruct((D_MODEL, n_tiles * tile_w), BF),
        scratch_shapes=[pltpu.VMEM((D_MODEL, tile_w), F32)],
        compiler_params=_cparams(est, ("parallel", "arbitrary")),
    )(_in_hbm(stack))


def _perm_out(dmain, dba):
    def tile(d, s):
        c0 = d * IN_BLOCK
        first = lax.div(c0 - jnp.where(c0 < _QKVZ, 0, jnp.minimum(c0 - _QKVZ, _BA)), jnp.int32(D_MODEL))
        return jnp.minimum(first + jnp.minimum(s, 1), N_MAIN_TILES - 1)

    def body(dm_ref, db_ref, o_ref, acc_ref):
        d, s = pl.program_id(0), pl.program_id(1)

        @pl.when(s == 0)
        def _():
            acc_ref[...] = jnp.zeros_like(acc_ref)

        start = _in_tile_start(tile(d, s), D_MODEL)
        overlaps = (start < (d + 1) * IN_BLOCK) & (d * IN_BLOCK < start + D_MODEL)

        @pl.when((s < 2) & overlaps)
        def _():
            sel = _select(_iota((D_MODEL, IN_PAD), 1), _iota((D_MODEL, IN_PAD), 0), d, start, D_MODEL)
            acc_ref[...] += jnp.dot(dm_ref[...], sel, preferred_element_type=F32)

        @pl.when(s == 2)
        def _():
            sel = _select(_iota((LANES, IN_PAD), 1), _iota((LANES, IN_PAD), 0), d, jnp.int32(_QKVZ), _BA)
            o_ref[0] = (acc_ref[...] + jnp.dot(db_ref[...], sel, preferred_element_type=F32)).astype(BF)

    est = 2 * _nbytes((D_MODEL, D_MODEL), BF) + 4 * _nbytes((D_MODEL, IN_PAD), F32)
    return pl.pallas_call(
        body, name="perm_out", grid=(N_DEV, 3),
        in_specs=[pl.BlockSpec((D_MODEL, D_MODEL), lambda d, s: (0, tile(d, s))),
                  pl.BlockSpec((D_MODEL, LANES), lambda d, s: (0, 0))],
        out_specs=pl.BlockSpec((1, D_MODEL, IN_PAD), lambda d, t: (d, 0, 0)),
        out_shape=jax.ShapeDtypeStruct((N_DEV, D_MODEL, IN_PAD), BF),
        scratch_shapes=[pltpu.VMEM((D_MODEL, IN_PAD), F32)],
        compiler_params=_cparams(est, ("parallel", "arbitrary")),
    )(dmain, dba)


def _mesh_place():
    x, y, c = (lax.axis_index(a) for a in MESH_AXES)
    return x, y, c


def _slot(x, y, c):
    return 4 * x + 2 * y + c


def _peer(place, j):
    x, y, c = place
    return (1 - x if j & 4 else x, 1 - y if j & 2 else y, 1 - c if j & 1 else c)


_HBM = pl.BlockSpec(memory_space=pltpu.HBM)
_SEM = pl.BlockSpec(memory_space=pltpu.SEMAPHORE)
_EFFECT = pltpu.SideEffectType.DATAFLOW_SIDE_EFFECTING


def _remote_copy(src_ref, land_ref, slot, per_slot, pslot, sems, u, j, peer):
    return pltpu.make_async_remote_copy(
        src_ref=src_ref.at[pslot] if per_slot else src_ref, dst_ref=land_ref.at[slot],
        send_sem=sems[0].at[u * (N_DEV - 1) + j - 1], recv_sem=sems[1].at[u * (N_DEV - 1) + j - 1],
        device_id=peer, device_id_type=pl.DeviceIdType.MESH)


def _own_copy(src_ref, land_ref, me, per_slot, sems, u):
    return pltpu.make_async_copy(src_ref.at[me] if per_slot else src_ref, land_ref.at[me], sems[2].at[u])


def _exchange_start(name, srcs, per_slot):
    n = len(srcs)
    lands = [jax.ShapeDtypeStruct(s.shape if p else (N_DEV,) + s.shape, s.dtype) for s, p in zip(srcs, per_slot)]

    def body(*refs):
        src_refs, sems, land_refs, token = refs[:n], refs[n:n + 3], refs[2 * n + 3:3 * n + 3], refs[-1]
        place = _mesh_place()
        me = _slot(*place)
        for u in range(n):
            _own_copy(src_refs[u], land_refs[u], me, per_slot[u], sems, u).start()
            for j in range(1, N_DEV):
                peer = _peer(place, j)
                _remote_copy(src_refs[u], land_refs[u], me, per_slot[u], _slot(*peer), sems, u, j, peer).start()
        token[...] = jnp.zeros_like(token)

    hbm = lambda a: pltpu.HBM(a.shape, a.dtype)
    sem = pltpu.SemaphoreType.DMA((n * (N_DEV - 1),))
    outs = pl.pallas_call(
        body, name=name,
        out_shape=(sem, sem, pltpu.SemaphoreType.DMA((n,)), *[hbm(a) for a in srcs], *[hbm(a) for a in lands],
                   jax.ShapeDtypeStruct((SUBLANES, LANES), F32)),
        in_specs=[_HBM] * n, out_specs=(_SEM, _SEM, _SEM, *[_HBM] * (2 * n), pl.BlockSpec(memory_space=pltpu.VMEM)),
        input_output_aliases={i: 3 + i for i in range(n)},
        compiler_params=pltpu.CompilerParams(has_side_effects=_EFFECT),
    )(*[pltpu.with_memory_space_constraint(a, pltpu.HBM) for a in srcs])
    return tuple(outs[:3]), list(outs[3:3 + n]), list(outs[3 + n:3 + 2 * n]), outs[-1]


def _exchange_wait(name, sems, srcs, lands, units, per_slot, after):
    m = len(units)
    after = list(after) if isinstance(after, (list, tuple)) else [after]

    def body(*refs):
        src_refs, land_refs, sem_refs = refs[:m], refs[m:2 * m], refs[2 * m:2 * m + 3]
        place = _mesh_place()
        me = _slot(*place)
        for i, u in enumerate(units):
            _own_copy(src_refs[i], land_refs[i], me, per_slot[u], sem_refs, u).wait()
            for j in range(1, N_DEV):
                peer = _peer(place, j)
                pslot = _slot(*peer)
                cp = _remote_copy(src_refs[i], land_refs[i], pslot, per_slot[u], pslot, sem_refs, u, j, peer)
                cp.wait_send()
                cp.wait_recv()

    hbm = lambda a: pltpu.HBM(a.shape, a.dtype)
    outs = pl.pallas_call(
        body, name=name, out_shape=tuple(hbm(a) for a in list(srcs) + list(lands)),
        in_specs=[_HBM] * (2 * m) + [_SEM] * 3 + [pl.BlockSpec(memory_space=pl.ANY)] * len(after),
        out_specs=tuple([_HBM] * (2 * m)),
        input_output_aliases={i: i for i in range(2 * m)},
        compiler_params=pltpu.CompilerParams(has_side_effects=_EFFECT),
    )(*srcs, *lands, *sems, *after)
    return list(outs[m:])


def _adam_update(g, w, m, v):
    m = ADAM_B1 * m + (1.0 - ADAM_B1) * g
    v = ADAM_B2 * v + (1.0 - ADAM_B2) * jnp.square(g)
    m_hat = m / (1.0 - ADAM_B1 ** ADAM_STEP)
    v_hat = v / (1.0 - ADAM_B2 ** ADAM_STEP)
    return -ADAM_LR * (m_hat / (jnp.sqrt(v_hat) + ADAM_EPS) + ADAM_WD * w), m, v


def _adamw(recvs, w, m, v, *, tr, name):
    L, R, C = w.shape
    rp = max(tr, SUBLANES * (4 // jnp.dtype(recvs[0].dtype).itemsize))
    Cp = recvs[0].shape[2]

    def body(*refs):
        r_refs, (w_ref, m_ref, v_ref, g_ref, d_ref, nm_ref, nv_ref) = refs[:L], refs[L:]
        for l in range(L):
            @pl.when(pl.program_id(0) == l)
            def _(r_ref=r_refs[l]):
                g = r_ref[0, :tr, :C].astype(F32)
                for s in range(1, N_DEV):
                    g = g + r_ref[s, :tr, :C].astype(F32)
                d, nm, nv = _adam_update(g, w_ref[0], m_ref[0], v_ref[0])
                g_ref[0], d_ref[0], nm_ref[0], nv_ref[0] = g, d, nm, nv

    blk = pl.BlockSpec((1, tr, C), lambda l, i: (l, i, 0))
    r_specs = [pl.BlockSpec((N_DEV, rp, Cp), lambda l, i, k=k: (0, jnp.where(l == k, i, 0), 0)) for k in range(L)]
    est = 2 * _nbytes((N_DEV, rp, Cp), recvs[0].dtype) + 8 * _nbytes((tr, Cp), F32)
    return pl.pallas_call(
        body, name=name, grid=(L, R // tr),
        in_specs=r_specs + [blk] * 3, out_specs=[blk] * 4,
        out_shape=[jax.ShapeDtypeStruct((L, R, C), F32)] * 4,
        compiler_params=_cparams(est, ("arbitrary", "arbitrary")),
    )(*recvs, w, m, v)


def _adamw_small(recv, w, m, v):
    def body(r_ref, w_ref, m_ref, v_ref, g_ref, d_ref, nm_ref, nv_ref):
        g = r_ref[0]
        for s in range(1, N_DEV):
            g = g + r_ref[s]
        g_ref[...] = g
        d_ref[...], nm_ref[...], nv_ref[...] = _adam_update(g, w_ref[...], m_ref[...], v_ref[...])

    vm = pl.BlockSpec(memory_space=pltpu.VMEM)
    return pl.pallas_call(
        body, name="adamw_small", in_specs=[vm] * 4, out_specs=[vm] * 4,
        out_shape=[jax.ShapeDtypeStruct((SMALL_ROWS, LANES), F32)] * 4,
        compiler_params=_cparams(20 * _nbytes((SMALL_ROWS, LANES), F32)),
    )(recv, w, m, v)


def kernel(x, w_in, conv_w, a_log, dt_bias, o_norm_w, sgu_ln_g, sgu_ln_b, w_s, b_s, w_pa, w_pb, w_o, ln1_g, ln1_b, w_ffn_gate, w_ffn_up, w_ffn_down, ln2_g, ln2_b, loss_target, m_w_in, m_conv_w, m_a_log, m_dt_bias, m_o_norm_w, m_sgu_ln_g, m_sgu_ln_b, m_w_s, m_b_s, m_w_pa, m_w_pb, m_w_o, m_ln1_g, m_ln1_b, m_w_ffn_gate, m_w_ffn_up, m_w_ffn_down, m_ln2_g, m_ln2_b, v_w_in, v_conv_w, v_a_log, v_dt_bias, v_o_norm_w, v_sgu_ln_g, v_sgu_ln_b, v_w_s, v_b_s, v_w_pa, v_w_pb, v_w_o, v_ln1_g, v_ln1_b, v_w_ffn_gate, v_w_ffn_up, v_w_ffn_down, v_ln2_g, v_ln2_b):
    given = dict(locals())
    P = {n: given[n] for n in WEIGHT_NAMES}
    M = {n: given["m_" + n] for n in WEIGHT_NAMES}
    V = {n: given["v_" + n] for n in WEIGHT_NAMES}

    wire = _wire_blocks(P)
    units = [(n, l) for l in range(DEPTH) for n in EARLY + LATE]
    whole = [False] * len(units)
    g_sems, g_srcs, g_lands, g_token = _exchange_start("gather_start", [wire[n][l] for n, l in units], whole)

    one = 1.0 + g_token[0, 0]
    xb = (x[0] * one).astype(BF)
    small_in = [[_small_pack({n: T[n][l] * one for n, _ in SMALL}) for T in (P, M, V)] for l in range(DEPTH)]
    adam_in = {n: (P[n], M[n], V[n]) for n in WIRE}
    adam_in["w_in"], _ = lax.optimization_barrier((adam_in["w_in"], g_token))
    prepared = [xb, *[a for packs in small_in for a in packs], *adam_in["w_in"]]

    def gathered(name, names, l, after):
        idx = [units.index((n, l)) for n in names]
        got = _exchange_wait(name, g_sems, [g_srcs[i] for i in idx], [g_lands[i] for i in idx], idx, whole, after)
        return dict(zip(names, got))

    def layer(l):
        def weights(x_in):
            after = prepared if l == 0 else x_in
            early = _early_weights(gathered(f"gather_wait_early{l}", EARLY, l, after), P, l)
            return early, lambda ya: _late_weights(gathered(f"gather_wait_late{l}", LATE, l, ya))
        return weights

    pending = {}

    def on_grads(l, part, g):
        if part == "late":
            srcs, names = _late_slots(g), LATE
            per_slot = [True] * len(srcs)
        else:
            slots, small = _early_slots(g)
            srcs, names = slots + [small], EARLY + ("small",)
            per_slot = [True] * len(slots) + [False]
        sems, s_thru, l_thru, token = _exchange_start(f"exchange_start_{part}{l}", srcs, per_slot)
        pending[l, part] = (names, sems, s_thru, l_thru, per_slot)
        return token[0, 0]

    loss_local, dx, _ = _local_step(x[0], xb, loss_target[0], [layer(l) for l in range(DEPTH)], on_grads)
    loss = lax.psum(loss_local, MESH_AXES)

    recv = [{} for _ in range(DEPTH)]

    def received(l, part, after):
        names, sems, s_thru, l_thru, per_slot = pending[l, part]
        got = _exchange_wait(f"exchange_wait_{part}{l}", sems, s_thru, l_thru, list(range(len(s_thru))), per_slot, after)
        recv[l].update(zip(names, got))

    out = {}

    def adamw(names):
        for n in names:
            tr, name = ADAM_TILES[n]
            view = _t if n in TRANSPOSED else (lambda a: a)
            res = _adamw([recv[l][n] for l in range(DEPTH)], *[view(a) for a in adam_in[n]], tr=tr, name=name)
            out[n] = [view(r) for r in res]

    for l in reversed(range(DEPTH)):
        received(l, "late", dx)
    adamw(LATE)
    for l in reversed(range(DEPTH)):
        received(l, "early", out[LATE[-1]][0])
    adamw(EARLY)
    small = [_adamw_small(recv[l]["small"], *small_in[l]) for l in range(DEPTH)]
    for n, _ in SMALL:
        out[n] = [jnp.stack([_small_unpack(small[l][i], P)[n] for l in range(DEPTH)]) for i in range(4)]
    return (loss, dx[None], *[out[n][i] for i in range(4) for n in WEIGHT_NAMES])
```

```python
import functools
import math

import jax
import jax.numpy as jnp
from jax import lax
from jax.experimental import pallas as pl
from jax.experimental.pallas import tpu as pltpu

F32 = jnp.float32
BF = jnp.bfloat16
HIGHEST = lax.Precision.HIGHEST

D_MODEL = 1024
DEPTH = 2
N_HEADS = 8
D_HEAD = 128
CONV_K = 4
SGU_BLOCK = 128
SGU_GROUPS = 8
SGU_CHUNK = 64
FFN_HIDDEN = 2816
N_IN = 8208
N_DEV = 8
IN_BLOCK, IN_PAD = N_IN // N_DEV, 1152
FFN_BLOCK, FFN_PAD = FFN_HIDDEN // N_DEV, 384
FFN_K = N_DEV * FFN_PAD
ALPHA = (2 * DEPTH) ** 0.25
LN_EPS = 1e-5
RMS_EPS = 1e-6
ADAM_LR, ADAM_B1, ADAM_B2, ADAM_EPS, ADAM_WD, ADAM_STEP = 0.001, 0.9, 0.999, 1e-08, 0.01, 10

MESH_AXES = ("x", "y", "c")
DELTA_CHUNK = 128
DELTA_HEADS_PER_STEP = 8
LANES = 128
SUBLANES = 8
VMEM_BYTES = 64 * 1024 * 1024
HALO = SUBLANES
HALO_BF = 2 * SUBLANES


def _cparams(est_bytes, dims=None):
    limit = int(min(max(2 * est_bytes + (8 << 20), 32 << 20), VMEM_BYTES - (6 << 20)))
    kw = dict(vmem_limit_bytes=limit)
    if dims is not None:
        kw["dimension_semantics"] = dims
    return pltpu.CompilerParams(**kw)


def _nbytes(shape, dtype):
    return math.prod(shape) * jnp.dtype(dtype).itemsize


def _dims(kind, ndim):
    lhs, rhs = {"nn": (1, 0), "nt": (1, 1), "tn": (0, 0)}[kind]
    b = ndim - 2
    return (((lhs + b,), (rhs + b,)), (tuple(range(b)), tuple(range(b))))


def _mxu(a, b, kind):
    return lax.dot_general(a, b, _dims(kind, a.ndim), preferred_element_type=F32)


def _dot(a, b):
    return _mxu(a.astype(BF), b.astype(BF), "nn")


def _dot_nt(a, b):
    return _mxu(a.astype(BF), b.astype(BF), "nt")


def _dot_tn(a, b):
    return _mxu(a.astype(BF), b.astype(BF), "tn")


def _split(a):
    hi = a.astype(BF)
    return hi, (a - hi.astype(F32)).astype(BF)


def _dot3(a, b, kind):
    (ah, al), (bh, bl) = _split(a), _split(b)
    return _mxu(ah, bh, kind) + (_mxu(ah, bl, kind) + _mxu(al, bh, kind))


def _dotf(a, b):
    return _dot3(a, b, "nn")


def _dotf_nt(a, b):
    return _dot3(a, b, "nt")


def _dot01(sel, x, kind="nn"):
    s = jnp.broadcast_to(sel.astype(BF), x.shape[:-2] + sel.shape)
    h1 = x.astype(BF)
    r1 = x - h1.astype(F32)
    h2 = r1.astype(BF)
    h3 = (r1 - h2.astype(F32)).astype(BF)
    return _mxu(s, h1, kind) + (_mxu(s, h2, kind) + _mxu(s, h3, kind))


def _sigmoid(x):
    return 0.5 * jnp.tanh(0.5 * x) + 0.5


def _silu(x):
    return x * _sigmoid(x)


def _silu_and_grad(x):
    s = _sigmoid(x)
    return x * s, s * (1.0 + x * (1.0 - s))


def _softplus(x):
    return jnp.maximum(x, 0.0) + jnp.log1p(jnp.exp(-jnp.abs(x)))


def _ln(x, g, b):
    mu = jnp.mean(x, -1, keepdims=True)
    xc = x - mu
    var = jnp.mean(xc * xc, -1, keepdims=True)
    return xc * lax.rsqrt(var + LN_EPS) * g + b


def _iota(shape, dim):
    return lax.broadcasted_iota(jnp.int32, shape, dim)


def _tile(n, pref, align):
    if n <= pref:
        return n
    t = (pref // align) * align
    while t >= align:
        if n % t == 0:
            return t
        t -= align
    raise ValueError(f"no tile for {n} (pref {pref}, align {align})")


def _bcast_rows(v, rows=SUBLANES):
    return jnp.broadcast_to(v, (rows, v.shape[-1]))


def _in_hbm(t):
    return pltpu.with_memory_space_constraint(t, pltpu.HBM)


def _mm(a, b, *, mode, name, out_dtype=F32, add=None, add_scale=1.0, tm=512, tn=1024, tk=1024, cols=None, pair=None,
        thin=None):
    if mode == "nn":
        (M, K), N = a.shape, b.shape[1]
    elif mode == "nt":
        (M, K), N = a.shape, b.shape[0]
    else:
        (K, M), N = a.shape, b.shape[1]
    col0 = 0
    if cols is not None:
        col0, N = cols
    tm = _tile(M, tm, LANES if mode == "tn" else SUBLANES * 2)
    tn = _tile(N, tn, LANES)
    tk = _tile(K, tk, LANES)
    nk = K // tk
    j0 = col0 // tn
    if mode == "nn":
        a_spec = pl.BlockSpec((tm, tk), lambda i, j, k: (i, k))
        b_spec = pl.BlockSpec((tk, tn), lambda i, j, k: (k, j + j0))
        dot = _dot
    elif mode == "nt":
        a_spec = pl.BlockSpec((tm, tk), lambda i, j, k: (i, k))
        b_spec = pl.BlockSpec((tn, tk), lambda i, j, k: (j, k))
        dot = _dot_nt
    else:
        a_spec = pl.BlockSpec((tk, tm), lambda i, j, k: (k, i))
        b_spec = pl.BlockSpec((tk, tn), lambda i, j, k: (k, j))
        dot = _dot_tn
    o_spec = pl.BlockSpec((tm, tn), lambda i, j, k: (i, j))
    has_add = add is not None
    assert thin is None or mode == "nt"

    n_ab = 2 if pair is None else 4

    def body(*refs):
        ab, (o_ref, acc_ref) = refs[:n_ab], refs[-2:]
        add_ref = refs[n_ab] if has_add else None
        thin_refs = refs[n_ab + has_add:-2]
        k = pl.program_id(2)
        part = dot(ab[0][...], ab[1][...])
        if pair is not None:
            part = part + dot(ab[2][...], ab[3][...])

        def finish(total):
            if has_add:
                total = total + add_scale * add_ref[...]
            if thin is not None:
                total = total + _dot_nt(thin_refs[0][...], thin_refs[1][...])
            o_ref[...] = total.astype(out_dtype)

        if nk == 1:
            finish(part)
        else:
            @pl.when(k == 0)
            def _():
                acc_ref[...] = part

            @pl.when(jnp.logical_and(k > 0, k < nk - 1))
            def _():
                acc_ref[...] += part

            @pl.when(k == nk - 1)
            def _():
                finish(acc_ref[...] + part)

    in_specs = [a_spec, b_spec] * (n_ab // 2) + ([o_spec] if has_add else [])
    args = (a, b) + (tuple(pair) if pair is not None else ()) + ((add,) if has_add else ())
    est = ((n_ab // 2) * (_nbytes((tm, tk), a.dtype) + _nbytes((tk, tn), b.dtype)) + 2 * _nbytes((tm, tn), F32)
           + (_nbytes((tm, tn), F32) if has_add else 0)) + 2 * _nbytes((tm, tn), F32)
    if thin is not None:
        k3 = thin[0].shape[1]
        in_specs += [pl.BlockSpec((tm, k3), lambda i, j, k: (i, 0)), pl.BlockSpec((tn, k3), lambda i, j, k: (j, 0))]
        args += tuple(thin)
        est += _nbytes((tm, k3), thin[0].dtype) + _nbytes((tn, k3), thin[1].dtype)
    return pl.pallas_call(
        body, name=name,
        grid=(M // tm, N // tn, nk),
        in_specs=in_specs, out_specs=o_spec,
        out_shape=jax.ShapeDtypeStruct((M, N), out_dtype),
        scratch_shapes=[pltpu.VMEM((tm, tn) if nk > 1 else (SUBLANES, LANES), F32)],
        compiler_params=_cparams(est, ("parallel", "parallel", "arbitrary")),
    )(*[_in_hbm(t) for t in args])


def _shifted(xt, halo, first):
    halo = jnp.where(first, 0.0, halo)
    xc = jnp.concatenate([halo, xt], axis=0)
    return [xt] + [pltpu.roll(xc, s, 0)[HALO:] for s in range(1, CONV_K)]


def _conv_taps(shifted, w_ref):
    out = shifted[0] * w_ref[CONV_K - 1:CONV_K, :]
    for s in range(1, CONV_K):
        out = out + shifted[s] * w_ref[CONV_K - 1 - s:CONV_K - s, :]
    return out


def _gates(ba, arow, dtrow):
    lane = _iota(ba.shape, 1)
    beta = _sigmoid(ba)
    g = -jnp.exp(arow) * _softplus(ba + dtrow)
    return jnp.where(lane < N_HEADS, beta, jnp.where(lane < 2 * N_HEADS, g, 0.0))


def _l2n(x):
    return x * lax.rsqrt(jnp.sum(x * x, -1, keepdims=True) + RMS_EPS)


def _qkv_prep(proj, ba, convw, arow, dtrow, *, tm=256):
    S = proj.shape[0]
    tm = _tile(S, tm, SUBLANES)
    W3 = 3 * D_MODEL
    hb = tm // HALO

    def body(xt_ref, halo_ref, ba_ref, w_ref, a_ref, dt_ref, q_ref, k_ref, v_ref, gb_ref, c_ref):
        c = _conv_taps(_shifted(xt_ref[...], halo_ref[...], pl.program_id(0) == 0), w_ref)
        c_ref[...] = c
        c = _silu(c)
        for h in range(N_HEADS):
            lo = h * D_HEAD
            q_ref[:, lo:lo + D_HEAD] = _l2n(c[:, lo:lo + D_HEAD])
            k_ref[:, lo:lo + D_HEAD] = _l2n(c[:, D_MODEL + lo:D_MODEL + lo + D_HEAD])
        v_ref[...] = c[:, 2 * D_MODEL:]
        gb_ref[...] = _gates(ba_ref[...], a_ref[...], dt_ref[...])

    row = lambda w, col=0: pl.BlockSpec((tm, w), lambda i: (i, col))
    full = lambda shape: pl.BlockSpec(shape, lambda i: (0,) * len(shape))
    est = 4 * _nbytes((tm, W3), F32)
    return pl.pallas_call(
        body, name="qkv_prep", grid=(S // tm,),
        in_specs=[row(W3), pl.BlockSpec((HALO, W3), lambda i: (jnp.maximum(i * hb - 1, 0), 0)), row(LANES),
                  full((CONV_K, W3)), full((1, LANES)), full((1, LANES))],
        out_specs=[row(D_MODEL), row(D_MODEL), row(D_MODEL), row(LANES), row(W3)],
        out_shape=[jax.ShapeDtypeStruct((S, D_MODEL), F32)] * 3 + [jax.ShapeDtypeStruct((S, LANES), F32),
                                                                   jax.ShapeDtypeStruct((S, W3), F32)],
        compiler_params=_cparams(est, ("arbitrary",)),
    )(proj, proj, ba, convw, arow, dtrow)


def _qkv_prep_bwd(proj, conv_out, ba, arow, dtrow, dq, dk, dv, dgb, *, tm=256):
    S = proj.shape[0]
    tm = _tile(S, tm, SUBLANES * 2)
    W3 = 3 * D_MODEL
    hb = tm // HALO

    def body(xt_ref, halo_ref, c_ref, ba_ref, a_ref, dt_ref, dq_ref, dk_ref, dv_ref, dgb_ref,
             dcb_ref, dba_ref, dw_ref, da_ref, ddt_ref, dc_ref):
        i = pl.program_id(0)

        @pl.when(i == 0)
        def _():
            dw_ref[...] = jnp.zeros_like(dw_ref)
            da_ref[...] = jnp.zeros_like(da_ref)
            ddt_ref[...] = jnp.zeros_like(ddt_ref)

        shifted = _shifted(xt_ref[...], halo_ref[...], i == 0)
        a, ds = _silu_and_grad(c_ref[...])
        for h in range(N_HEADS):
            for base, d_ref in ((0, dq_ref), (D_MODEL, dk_ref)):
                lo = base + h * D_HEAD
                _, vj = jax.vjp(_l2n, a[:, lo:lo + D_HEAD])
                (dx,) = vj(d_ref[:, h * D_HEAD:(h + 1) * D_HEAD])
                dc_ref[:, lo:lo + D_HEAD] = dx * ds[:, lo:lo + D_HEAD]
        dc_ref[:, 2 * D_MODEL:] = dv_ref[...] * ds[:, 2 * D_MODEL:]
        dc = dc_ref[...]
        dcb_ref[...] = dc.astype(BF)
        for s in range(CONV_K):
            kk = CONV_K - 1 - s
            dw_ref[kk:kk + 1, :] += jnp.sum(dc * shifted[s], axis=0, keepdims=True)
        _, vj = jax.vjp(_gates, ba_ref[...], a_ref[...], dt_ref[...])
        dba, da, ddt = vj(dgb_ref[...])
        dba_ref[...] = dba.astype(BF)
        da_ref[...] += _bcast_rows(da)
        ddt_ref[...] += _bcast_rows(ddt)

    row = lambda w, col=0: pl.BlockSpec((tm, w), lambda i: (i, col))
    full = lambda shape: pl.BlockSpec(shape, lambda i: (0,) * len(shape))
    est = 8 * _nbytes((tm, W3), F32)
    return pl.pallas_call(
        body, name="qkv_prep_bwd", grid=(S // tm,),
        in_specs=[row(W3), pl.BlockSpec((HALO, W3), lambda i: (jnp.maximum(i * hb - 1, 0), 0)), row(W3), row(LANES),
                  full((1, LANES)), full((1, LANES)),
                  row(D_MODEL), row(D_MODEL), row(D_MODEL), row(LANES)],
        out_specs=[row(W3), row(LANES), full((SUBLANES, W3)), full((SUBLANES, LANES)), full((SUBLANES, LANES))],
        out_shape=[jax.ShapeDtypeStruct((S, W3), BF), jax.ShapeDtypeStruct((S, LANES), BF),
                   jax.ShapeDtypeStruct((SUBLANES, W3), F32), jax.ShapeDtypeStruct((SUBLANES, LANES), F32),
                   jax.ShapeDtypeStruct((SUBLANES, LANES), F32)],
        scratch_shapes=[pltpu.VMEM((tm, W3), F32)],
        compiler_params=_cparams(est, ("arbitrary",)),
    )(proj, proj, conv_out, ba, arow, dtrow, dq, dk, dv, dgb)


def _conv_bwd(dc, convw, dproj, *, tm=256):
    S, W3 = dc.shape
    tm = _tile(S, tm, HALO_BF)
    hb = tm // HALO_BF
    nt = S // tm

    def body(dc_ref, nxt_ref, w_ref, dproj_ref, o_ref):
        last = pl.program_id(0) == nt - 1
        nxt = jnp.where(last, 0.0, nxt_ref[...].astype(F32))
        cur = dc_ref[...].astype(F32)
        xc = jnp.concatenate([cur, nxt], axis=0)
        out = cur * w_ref[CONV_K - 1:CONV_K, :]
        for s in range(1, CONV_K):
            out = out + pltpu.roll(xc, tm + HALO_BF - s, 0)[:tm] * w_ref[CONV_K - 1 - s:CONV_K - s, :]
        o_ref[...] = out.astype(BF)

    est = 5 * _nbytes((tm, W3), F32)
    return pl.pallas_call(
        body, name="conv_bwd", grid=(nt,),
        in_specs=[pl.BlockSpec((tm, W3), lambda i: (i, 0)),
                  pl.BlockSpec((HALO_BF, W3), lambda i: (jnp.minimum((i + 1) * hb, S // HALO_BF - 1), 0)),
                  pl.BlockSpec((CONV_K, W3), lambda i: (0, 0)), pl.BlockSpec(memory_space=pl.ANY)],
        out_specs=pl.BlockSpec((tm, W3), lambda i: (i, 0)),
        out_shape=jax.ShapeDtypeStruct(dproj.shape, BF),
        input_output_aliases={3: 0},
        compiler_params=_cparams(est, ("parallel",)),
    )(dc, dc, convw, dproj)


NEUMANN_BLOCK = 8


def _inv_unit_lower(A):
    C = A.shape[-1]
    row, col = _iota((C, C), 0), _iota((C, C), 1)
    eye = jnp.where(row == col, 1.0, 0.0).astype(F32)
    Ab = A.astype(BF)
    sh = jnp.int32(int(math.log2(NEUMANN_BLOCK)))
    B = jnp.where(lax.shift_right_logical(row, sh) == lax.shift_right_logical(col, sh), Ab, jnp.zeros_like(Ab))
    B2 = _mxu(B, B, "nn")
    B4 = _dot3(B2, B2, "nn")
    b2h, b2l = _split(B2)
    P = eye - B.astype(F32) + B2 - (_mxu(B, b2h, "nn") + _mxu(B, b2l, "nn"))
    T = P + _dot3(P, B4, "nn")
    b = NEUMANN_BLOCK
    while b < C:
        hi = ~(2 * b - 1)
        off = ((row & hi) == (col & hi)) & ((row & b) != 0) & ((col & b) == 0)
        Aoff = jnp.where(off, Ab, jnp.zeros_like(Ab))
        th, tl = _split(T)
        xh, xl = _split(_mxu(th, Aoff, "nn") + _mxu(tl, Aoff, "nn"))
        T = T - (_mxu(xh, th, "nn") + (_mxu(xh, tl, "nn") + _mxu(xl, th, "nn")))
        b *= 2
    return T


def _delta_common(q, k, g, beta):
    C = q.shape[-2]
    row, col = _iota((C, C), 0), _iota((C, C), 1)
    tril = row >= col
    qs = q * (D_HEAD ** -0.5)
    gcb = _dot01(jnp.where(tril, 1.0, 0.0), jnp.broadcast_to(g, g.shape[:-1] + (LANES,)))
    gc = gcb[..., :1]
    gr = jnp.swapaxes(gcb, -1, -2)
    Dm = jnp.exp(jnp.where(tril, gc - gr, -1e30))
    Dmt = jnp.exp(jnp.where(row <= col, gr - gc, -1e30))
    eg = jnp.exp(gc)
    gl = jnp.sum(jnp.where(_iota((C, 1), 0) == C - 1, gc, 0.0), axis=(-2, -1), keepdims=True)
    el = jnp.exp(gl)
    er = jnp.exp(gl - gc)
    kb = k * beta
    KK = _dot_nt(kb, k)
    QK = _dot_nt(qs, k)
    return dict(row=row, col=col, tril=tril, qs=qs, gc=gc, Dm=Dm, Dmt=Dmt, eg=eg, el=el, er=er, kb=kb, KK=KK, QK=QK)


def _delta_chunk_fwd(S0, q, k, v, g, beta):
    m = _delta_common(q, k, g, beta)
    T = _inv_unit_lower(jnp.where(m["row"] > m["col"], m["KK"] * m["Dm"], 0.0))
    u = _dotf(T, v * beta)
    w = _dotf(T, m["kb"] * m["eg"])
    vn = u - _dot(w, S0)
    o = _dot(m["qs"] * m["eg"], S0) + _dot(m["QK"] * m["Dm"], vn)
    S1 = S0 * m["el"] + _dot_tn(k * m["er"], vn)
    return o, S1, jnp.swapaxes(T, -1, -2), u, w


def _delta_chunk_bwd(S0, q, k, v, g, beta, Tt, u, w, do, dS1):
    m = _delta_common(q, k, g, beta)
    C = q.shape[-2]
    qs, Dm, Dmt, eg, el, er, kb, KK, QK = (m[n] for n in ("qs", "Dm", "Dmt", "eg", "el", "er", "kb", "KK", "QK"))
    strict = m["row"] > m["col"]
    total = lambda x: jnp.sum(x, axis=(-2, -1), keepdims=True)
    vn = u - _dot(w, S0)
    qg = qs * eg
    kr = k * er

    dvn = _dot(_dot_nt(k, qs) * Dmt, do) + _dot(kr, dS1)
    dS0 = dS1 * el + _dot_tn(qg, do) - _dot_tn(w, dvn)
    d_el = total(dS1 * S0)
    dqg = _dot_nt(do, S0)
    dqs = dqg * eg
    deg = jnp.sum(dqg * qs, -1, keepdims=True)
    dP = _dot_nt(do, vn)
    dPD = dP * Dm
    dqs = dqs + _dot(dPD, k)
    dk = _dot(_dot_nt(vn, do) * Dmt, qs)
    dD = dP * QK
    dkr = _dot_nt(vn, dS1)
    dk = dk + dkr * er
    der = jnp.sum(dkr * k, -1, keepdims=True)
    dw = -_dot_nt(dvn, S0)
    th, tl = _split(Tt)

    def tt_times(x):
        xh, xl = _split(x)
        return _mxu(th, xh, "nn") + (_mxu(th, xl, "nn") + _mxu(tl, xh, "nn"))

    dru = tt_times(dvn)
    drw = tt_times(dw)
    dA = -(_dotf_nt(dru, u) + _dotf_nt(drw, w))
    dAm = jnp.where(strict, dA, 0.0)
    dKK = dAm * Dm
    dkb = _dot(dKK, k)
    dk = dk + _dot_tn(dKK, kb)
    dD = dD + dAm * KK
    dv = dru * beta
    dbeta = jnp.sum(dru * v, -1, keepdims=True)
    dkb = dkb + drw * eg
    deg = deg + jnp.sum(drw * kb, -1, keepdims=True)
    dk = dk + dkb * beta
    dbeta = dbeta + jnp.sum(dkb * k, -1, keepdims=True)
    E = dD * Dm
    dgc = jnp.sum(E, -1, keepdims=True) - jnp.sum(jnp.swapaxes(E, -1, -2), -1, keepdims=True)
    dgc = dgc + deg * eg - der * er
    dgl = total(der * er) + d_el * el
    dgc = dgc + jnp.where(_iota((C, 1), 0) == C - 1, dgl, 0.0)
    triu = jnp.where(m["row"] <= m["col"], 1.0, 0.0)
    dg = _dot01(triu, jnp.broadcast_to(dgc, dgc.shape[:-1] + (LANES,)))[..., :1]
    dq = dqs * (D_HEAD ** -0.5)
    return dq, dk, dv, dg, dbeta, dS0


def _head_cols(gb, h):
    lane = _iota(gb.shape, 1)
    beta = jnp.sum(jnp.where(lane == h, gb, 0.0), -1, keepdims=True)
    g = jnp.sum(jnp.where(lane == N_HEADS + h, gb, 0.0), -1, keepdims=True)
    return g, beta


def _delta_fwd(q, k, v, gb):
    S = q.shape[0]
    C = DELTA_CHUNK
    N = S // C

    HB = DELTA_HEADS_PER_STEP

    def body(q_ref, k_ref, v_ref, gb_ref, o_ref, st_ref, t_ref, u_ref, w_ref, s_scr):
        n, hb = pl.program_id(0), pl.program_id(1)
        gb = gb_ref[...]

        @pl.when(n == 0)
        def _():
            for hh in range(HB):
                s_scr[hb * HB + hh] = jnp.zeros((D_HEAD, D_HEAD), F32)

        heads = [hb * HB + hh for hh in range(HB)]
        cols = [slice(hh * D_HEAD, (hh + 1) * D_HEAD) for hh in range(HB)]
        per_head = lambda ref: jnp.stack([ref[:, c] for c in cols])
        g, beta = (jnp.stack(t) for t in zip(*[_head_cols(gb, h) for h in heads]))
        S0 = jnp.stack([s_scr[h] for h in heads])
        o, S1, Tt, u, w = _delta_chunk_fwd(S0, per_head(q_ref), per_head(k_ref), per_head(v_ref), g, beta)
        for hh in range(HB):
            st_ref[hh, 0] = S0[hh]
            t_ref[hh, 0] = Tt[hh]
            o_ref[:, cols[hh]] = o[hh]
            u_ref[:, cols[hh]] = u[hh]
            w_ref[:, cols[hh]] = w[hh]
            s_scr[heads[hh]] = S1[hh]

    hd = pl.BlockSpec((C, HB * D_HEAD), lambda n, h: (n, h))
    mat = pl.BlockSpec((HB, 1, D_HEAD, D_HEAD), lambda n, h: (h, n, 0, 0))
    est = 40 * HB * _nbytes((C, D_HEAD), F32)
    seq = jax.ShapeDtypeStruct((S, N_HEADS * D_HEAD), F32)
    return pl.pallas_call(
        body, name="delta_fwd", grid=(N, N_HEADS // HB),
        in_specs=[hd, hd, hd, pl.BlockSpec((C, LANES), lambda n, h: (n, 0))],
        out_specs=[hd, mat, mat, hd, hd],
        out_shape=[seq, jax.ShapeDtypeStruct((N_HEADS, N, D_HEAD, D_HEAD), F32),
                   jax.ShapeDtypeStruct((N_HEADS, N, C, C), F32), seq, seq],
        scratch_shapes=[pltpu.VMEM((N_HEADS, D_HEAD, D_HEAD), F32)],
        compiler_params=_cparams(est, ("arbitrary", "arbitrary")),
    )(q, k, v, gb)


def _delta_bwd(q, k, v, gb, st, tinv, u, w, do):
    S = q.shape[0]
    C = DELTA_CHUNK
    N = S // C

    HB = DELTA_HEADS_PER_STEP

    def body(q_ref, k_ref, v_ref, gb_ref, st_ref, t_ref, u_ref, w_ref, do_ref, dq_ref, dk_ref, dv_ref, dgb_ref, ds_scr):
        n, hb = pl.program_id(0), pl.program_id(1)
        gb = gb_ref[...]
        lane = _iota((C, LANES), 1)
        dgb = jnp.zeros((C, LANES), F32)

        @pl.when(n == 0)
        def _():
            for hh in range(HB):
                ds_scr[hb * HB + hh] = jnp.zeros((D_HEAD, D_HEAD), F32)

        heads = [hb * HB + hh for hh in range(HB)]
        cols = [slice(hh * D_HEAD, (hh + 1) * D_HEAD) for hh in range(HB)]
        per_head = lambda ref: jnp.stack([ref[:, c] for c in cols])
        g, beta = (jnp.stack(t) for t in zip(*[_head_cols(gb, h) for h in heads]))
        dS1 = jnp.stack([ds_scr[h] for h in heads])
        dq, dk, dv, dg, dbeta, dS0 = _delta_chunk_bwd(
            st_ref[:, 0], per_head(q_ref), per_head(k_ref), per_head(v_ref), g, beta, t_ref[:, 0],
            per_head(u_ref), per_head(w_ref), per_head(do_ref), dS1)
        for hh, h in enumerate(heads):
            dq_ref[:, cols[hh]] = dq[hh]
            dk_ref[:, cols[hh]] = dk[hh]
            dv_ref[:, cols[hh]] = dv[hh]
            dgb = dgb + jnp.where(lane == h, dbeta[hh], 0.0) + jnp.where(lane == N_HEADS + h, dg[hh], 0.0)
            ds_scr[h] = dS0[hh]

        @pl.when(hb == 0)
        def _():
            dgb_ref[...] = dgb

        @pl.when(hb > 0)
        def _():
            dgb_ref[...] += dgb

    hd = pl.BlockSpec((C, HB * D_HEAD), lambda n, h: (N - 1 - n, h))
    mat = pl.BlockSpec((HB, 1, D_HEAD, D_HEAD), lambda n, h: (h, N - 1 - n, 0, 0))
    gbs = pl.BlockSpec((C, LANES), lambda n, h: (N - 1 - n, 0))
    est = 60 * HB * _nbytes((C, D_HEAD), F32)
    return pl.pallas_call(
        body, name="delta_bwd", grid=(N, N_HEADS // HB),
        in_specs=[hd, hd, hd, gbs, mat, mat, hd, hd, hd],
        out_specs=[hd, hd, hd, gbs],
        out_shape=[jax.ShapeDtypeStruct((S, N_HEADS * D_HEAD), F32)] * 3 + [jax.ShapeDtypeStruct((S, LANES), F32)],
        scratch_shapes=[pltpu.VMEM((N_HEADS, D_HEAD, D_HEAD), F32)],
        compiler_params=_cparams(est, ("arbitrary", "arbitrary")),
    )(q, k, v, gb, st, tinv, u, w, do)


def _ya_head(o, z, onw):
    return o * lax.rsqrt(jnp.mean(o * o, -1, keepdims=True) + RMS_EPS) * onw * _silu(z)


def _norm_cdf(x):
    return 0.5 * (1.0 + lax.erf(x * 0.7071067811865476))


def _norm_pdf(x):
    return jnp.exp(-0.5 * x * x) * 0.3989422804014327


def _chunk_causal(shape, di, dj):
    sh = jnp.int32(int(math.log2(SGU_CHUNK)))
    return lax.shift_right_logical(_iota(shape, di), sh) >= lax.shift_right_logical(_iota(shape, dj), sh)


def _ws_masked(ws):
    return jnp.where(_chunk_causal(ws.shape, 1, 2), ws, 0.0)


def _mix_prep(o, proj, onw, sg, sb, ws, bst, wpa, wpb, *, tm=256):
    S = o.shape[0]
    tm = _tile(S, tm, SGU_BLOCK)

    def body(o_ref, z_ref, u_ref, vg_ref, ga_ref, gb_ref, onw_ref, sg_ref, sb_ref, ws_ref, bst_ref, wa_ref, wb_ref,
             ya_ref, yb_ref, phi_ref, pa_ref, pb_ref, m_ref):
        onw = onw_ref[...]
        for h in range(N_HEADS):
            sl = slice(h * D_HEAD, (h + 1) * D_HEAD)
            ya_ref[:, sl] = _ya_head(o_ref[:, sl], z_ref[:, sl].astype(F32), onw).astype(BF)
        u, vg = u_ref[...].astype(F32), vg_ref[...].astype(F32)
        phi_u, phi_v = _norm_cdf(u), _norm_cdf(vg)
        phi_ref[:, :D_MODEL] = phi_u
        phi_ref[:, D_MODEL:] = phi_v
        ua, vl = u * phi_u, _ln(vg * phi_v, sg_ref[...], sb_ref[...])
        wsm = _ws_masked(ws_ref[...])
        bst = bst_ref[...]
        for blk in range(tm // SGU_BLOCK):
            rs = slice(blk * SGU_BLOCK, (blk + 1) * SGU_BLOCK)
            for gi in range(SGU_GROUPS):
                cs = slice(gi * D_HEAD, (gi + 1) * D_HEAD)
                sp = _dot(wsm[gi], vl[rs, cs]) + bst[:, gi:gi + 1]
                yb_ref[rs, cs] = (ua[rs, cs] * sp).astype(BF)
        pa = _dot(ya_ref[...], wa_ref[...]).astype(BF)
        pb = _dot(yb_ref[...], wb_ref[...]).astype(BF)
        pa_ref[...] = pa
        pb_ref[...] = pb
        m_ref[...] = (_sigmoid(ga_ref[...].astype(F32)) * pa.astype(F32)
                      + _sigmoid(gb_ref[...].astype(F32)) * pb.astype(F32)).astype(BF)

    blk = lambda col: pl.BlockSpec((tm, D_MODEL), lambda i: (i, col))
    full = lambda shape: pl.BlockSpec(shape, lambda i: (0,) * len(shape))
    est = 14 * _nbytes((tm, D_MODEL), F32) + 4 * _nbytes((D_MODEL, D_MODEL), BF)
    return pl.pallas_call(
        body, name="mix_prep", grid=(S // tm,),
        in_specs=[blk(0), blk(0), blk(1), blk(2), blk(3), blk(4), full((1, D_HEAD)), full((1, D_MODEL)),
                  full((1, D_MODEL)), full((SGU_GROUPS, SGU_BLOCK, SGU_BLOCK)), full((SGU_BLOCK, LANES)),
                  full((D_MODEL, D_MODEL)), full((D_MODEL, D_MODEL))],
        out_specs=[blk(0), blk(0), pl.BlockSpec((tm, 2 * D_MODEL), lambda i: (i, 0)), blk(0), blk(0), blk(0)],
        out_shape=[jax.ShapeDtypeStruct((S, D_MODEL), BF)] * 2 + [jax.ShapeDtypeStruct((S, 2 * D_MODEL), F32)]
                  + [jax.ShapeDtypeStruct((S, D_MODEL), BF)] * 3,
        compiler_params=_cparams(est, ("parallel",)),
    )(o, proj, proj, proj, proj, proj, onw, sg, sb, ws, bst, wpa, wpb)


def _mix_prep_bwd(o, proj, phi, onw, sg, sb, ws, bst, dpa, wpa, dpb, wpb, dproj, *, tm=256):
    S = o.shape[0]
    tm = _tile(S, tm, SGU_BLOCK)

    def body(o_ref, z_ref, u_ref, vg_ref, phi_ref, onw_ref, sg_ref, sb_ref, ws_ref, bst_ref, dpa_ref, wpa_ref, dpb_ref,
             wpb_ref, dproj_in, do_ref, dzuv_ref, donw_ref, dsg_ref, dsb_ref, dws_ref, dbst_ref, dvl_scr, dua_scr,
             dya_ref, dyb_ref):
        dz_ref, du_ref, dvg_ref = (dzuv_ref.at[:, k * D_MODEL:(k + 1) * D_MODEL] for k in range(3))
        @pl.when(pl.program_id(0) == 0)
        def _():
            for r in (donw_ref, dsg_ref, dsb_ref, dws_ref, dbst_ref):
                r[...] = jnp.zeros_like(r)

        dya_ref[...] = _dot_nt(dpa_ref[...], wpa_ref[...])
        dyb_ref[...] = _dot_nt(dpb_ref[...], wpb_ref[...])

        onw = onw_ref[...]
        donw = jnp.zeros((1, D_HEAD), F32)
        for h in range(N_HEADS):
            sl = slice(h * D_HEAD, (h + 1) * D_HEAD)
            _, vj = jax.vjp(_ya_head, o_ref[:, sl], z_ref[:, sl].astype(F32), onw)
            do_h, dz_h, donw_h = vj(dya_ref[:, sl])
            do_ref[:, sl] = do_h.astype(BF)
            dz_ref[:, sl] = dz_h.astype(BF)
            donw = donw + donw_h
        donw_ref[...] += _bcast_rows(donw)

        u, vg = u_ref[...].astype(F32), vg_ref[...].astype(F32)
        phi_u, phi_v = phi_ref[:, :D_MODEL], phi_ref[:, D_MODEL:]
        ua = u * phi_u
        vl, vj = jax.vjp(_ln, vg * phi_v, sg_ref[...], sb_ref[...])
        wsm = _ws_masked(ws_ref[...])
        bst = bst_ref[...]
        lane = _iota((SGU_BLOCK, LANES), 1)
        dbst = jnp.zeros((SGU_BLOCK, LANES), F32)
        cmask = _chunk_causal((SGU_BLOCK, SGU_BLOCK), 0, 1)
        for gi in range(SGU_GROUPS):
            cs = slice(gi * D_HEAD, (gi + 1) * D_HEAD)
            wg = wsm[gi]
            wgt = jnp.transpose(wg)
            dwg = jnp.zeros((SGU_BLOCK, SGU_BLOCK), F32)
            for blk in range(tm // SGU_BLOCK):
                rs = slice(blk * SGU_BLOCK, (blk + 1) * SGU_BLOCK)
                sp = _dot(wg, vl[rs, cs]) + bst[:, gi:gi + 1]
                dyb = dyb_ref[rs, cs]
                dsp = dyb * ua[rs, cs]
                dua_scr[rs, cs] = dyb * sp
                dvl_scr[rs, cs] = _dot(wgt, dsp)
                dwg = dwg + _dot_nt(dsp, vl[rs, cs])
                dbst = dbst + jnp.where(lane == gi, jnp.sum(dsp, -1, keepdims=True), 0.0)
            dws_ref[gi] += jnp.where(cmask, dwg, 0.0)
        dbst_ref[...] += dbst
        dgv, dsg, dsb = vj(dvl_scr[...])
        du_ref[...] = (dua_scr[...] * (phi_u + u * _norm_pdf(u))).astype(BF)
        dvg_ref[...] = (dgv * (phi_v + vg * _norm_pdf(vg))).astype(BF)
        dsg_ref[...] += _bcast_rows(dsg)
        dsb_ref[...] += _bcast_rows(dsb)

    blk = lambda col: pl.BlockSpec((tm, D_MODEL), lambda i: (i, col))
    full = lambda shape: pl.BlockSpec(shape, lambda i: (0,) * len(shape))
    est = 18 * _nbytes((tm, D_MODEL), F32) + 4 * _nbytes((D_MODEL, D_MODEL), BF)
    outs = pl.pallas_call(
        body, name="mix_prep_bwd", grid=(S // tm,),
        in_specs=[blk(0), blk(0), blk(1), blk(2), pl.BlockSpec((tm, 2 * D_MODEL), lambda i: (i, 0)),
                  full((1, D_HEAD)), full((1, D_MODEL)), full((1, D_MODEL)),
                  full((SGU_GROUPS, SGU_BLOCK, SGU_BLOCK)), full((SGU_BLOCK, LANES)),
                  blk(0), full((D_MODEL, D_MODEL)), blk(0), full((D_MODEL, D_MODEL)),
                  pl.BlockSpec(memory_space=pl.ANY)],
        out_specs=[blk(0), pl.BlockSpec((tm, 3 * D_MODEL), lambda i: (i, 1)),
                   full((SUBLANES, D_HEAD)), full((SUBLANES, D_MODEL)), full((SUBLANES, D_MODEL)),
                   full((SGU_GROUPS, SGU_BLOCK, SGU_BLOCK)), full((SGU_BLOCK, LANES))],
        out_shape=[jax.ShapeDtypeStruct((S, D_MODEL), BF), jax.ShapeDtypeStruct(dproj.shape, BF),
                   jax.ShapeDtypeStruct((SUBLANES, D_HEAD), F32), jax.ShapeDtypeStruct((SUBLANES, D_MODEL), F32),
                   jax.ShapeDtypeStruct((SUBLANES, D_MODEL), F32),
                   jax.ShapeDtypeStruct((SGU_GROUPS, SGU_BLOCK, SGU_BLOCK), F32),
                   jax.ShapeDtypeStruct((SGU_BLOCK, LANES), F32)],
        input_output_aliases={14: 1},
        scratch_shapes=[pltpu.VMEM((tm, D_MODEL), F32)] * 4,
        compiler_params=_cparams(est, ("arbitrary",)),
    )(o, proj, proj, proj, phi, onw, sg, sb, ws, bst, dpa, wpa, dpb, wpb, dproj)
    return outs


def _mm_gate_merge_bwd(dmix, wo, pa, pb, proj, *, tm=512):
    S = pa.shape[0]
    tm = _tile(S, tm, SUBLANES * 2)

    def body(d_ref, w_ref, pa_ref, pb_ref, ga_ref, gb_ref, dpa_ref, dpb_ref, dg_ref):
        dm = _dot_nt(d_ref[...], w_ref[...])
        sa, sb = _sigmoid(ga_ref[...].astype(F32)), _sigmoid(gb_ref[...].astype(F32))
        dpa_ref[...] = (dm * sa).astype(BF)
        dpb_ref[...] = (dm * sb).astype(BF)
        dg_ref[:, :D_MODEL] = (dm * pa_ref[...].astype(F32) * sa * (1.0 - sa)).astype(BF)
        dg_ref[:, D_MODEL:] = (dm * pb_ref[...].astype(F32) * sb * (1.0 - sb)).astype(BF)

    blk = lambda col: pl.BlockSpec((tm, D_MODEL), lambda i: (i, col))
    est = _nbytes((D_MODEL, D_MODEL), BF) + 10 * _nbytes((tm, D_MODEL), F32)
    return pl.pallas_call(
        body, name="mm_gate_merge_bwd", grid=(S // tm,),
        in_specs=[blk(0), pl.BlockSpec((D_MODEL, D_MODEL), lambda i: (0, 0)), blk(0), blk(0), blk(3), blk(4)],
        out_specs=[blk(0), blk(0), pl.BlockSpec((tm, 2 * D_MODEL), lambda i: (i, 3))],
        out_shape=[jax.ShapeDtypeStruct((S, D_MODEL), BF)] * 2 + [jax.ShapeDtypeStruct((S, 8 * D_MODEL), BF)],
        compiler_params=_cparams(est, ("parallel",)),
    )(dmix, wo, pa, pb, proj, proj)


def _mm_swiglu(xb, wgt, wut, *, tm=1024, tn=768):
    S, K = xb.shape
    tm = _tile(S, tm, SUBLANES * 2)
    tn = _tile(FFN_K, tn, LANES)

    def body(x_ref, wg_ref, wu_ref, hg_ref, hu_ref, h_ref):
        x = x_ref[...]
        hg = _dot_nt(x, wg_ref[...]).astype(BF)
        hu = _dot_nt(x, wu_ref[...]).astype(BF)
        hg_ref[...] = hg
        hu_ref[...] = hu
        h_ref[...] = (_silu(hg.astype(F32)) * hu.astype(F32)).astype(BF)

    out = pl.BlockSpec((tm, tn), lambda i, j: (i, j))
    est = _nbytes((tm, K), BF) + 2 * _nbytes((K, tn), BF) + 6 * _nbytes((tm, tn), F32)
    return pl.pallas_call(
        body, name="mm_swiglu", grid=(S // tm, FFN_K // tn),
        in_specs=[pl.BlockSpec((tm, K), lambda i, j: (i, 0)), pl.BlockSpec((tn, K), lambda i, j: (j, 0)),
                  pl.BlockSpec((tn, K), lambda i, j: (j, 0))],
        out_specs=[out] * 3, out_shape=[jax.ShapeDtypeStruct((S, FFN_K), BF)] * 3,
        compiler_params=_cparams(est, ("parallel", "parallel")),
    )(xb, wgt, wut)


def _mm_swiglu_bwd(dffn, wd, hg, hu, *, tm=1024, tn=768):
    S, K = dffn.shape
    tm = _tile(S, tm, SUBLANES * 2)
    tn = _tile(FFN_K, tn, LANES)

    def body(d_ref, w_ref, hg_ref, hu_ref, dhg_ref, dhu_ref):
        dh = _dot_nt(d_ref[...], w_ref[...])
        act, dact = _silu_and_grad(hg_ref[...].astype(F32))
        dhg_ref[...] = (dh * hu_ref[...].astype(F32) * dact).astype(BF)
        dhu_ref[...] = (dh * act).astype(BF)

    out = pl.BlockSpec((tm, tn), lambda i, j: (i, j))
    est = _nbytes((tm, K), dffn.dtype) + _nbytes((tn, K), BF) + 8 * _nbytes((tm, tn), F32)
    return pl.pallas_call(
        body, name="mm_swiglu_bwd", grid=(S // tm, FFN_K // tn),
        in_specs=[pl.BlockSpec((tm, K), lambda i, j: (i, 0)), pl.BlockSpec((tn, K), lambda i, j: (j, 0)), out, out],
        out_specs=[out, out], out_shape=[jax.ShapeDtypeStruct((S, FFN_K), BF)] * 2,
        compiler_params=_cparams(est, ("parallel", "parallel")),
    )(dffn, wd, hg, hu)


def _mm_resid_ln(a, bmat, x, g, b, *, name, tm=512):
    S, K = a.shape
    tm = _tile(S, tm, SUBLANES * 2)

    def body(a_ref, w_ref, x_ref, g_ref, b_ref, pre_ref, y_ref, yb_ref):
        pre = ALPHA * x_ref[...] + _dot(a_ref[...], w_ref[...])
        y = _ln(pre, g_ref[...], b_ref[...])
        pre_ref[...] = pre
        y_ref[...] = y
        yb_ref[...] = y.astype(BF)

    blk = pl.BlockSpec((tm, D_MODEL), lambda i: (i, 0))
    vec = pl.BlockSpec((1, D_MODEL), lambda i: (0, 0))
    est = _nbytes((tm, K), BF) + _nbytes((K, D_MODEL), BF) + 8 * _nbytes((tm, D_MODEL), F32)
    return pl.pallas_call(
        body, name=name, grid=(S // tm,),
        in_specs=[pl.BlockSpec((tm, K), lambda i: (i, 0)), pl.BlockSpec((K, D_MODEL), lambda i: (0, 0)), blk, vec, vec],
        out_specs=[blk, blk, blk],
        out_shape=[jax.ShapeDtypeStruct((S, D_MODEL), F32)] * 2 + [jax.ShapeDtypeStruct((S, D_MODEL), BF)],
        compiler_params=_cparams(est, ("parallel",)),
    )(a, bmat, x, g, b)


def _ln_bwd(pre, g, b, dy, *, tm=512):
    S = pre.shape[0]
    tm = _tile(S, tm, SUBLANES)

    def body(p_ref, g_ref, b_ref, dy_ref, dp_ref, dg_ref, db_ref):
        @pl.when(pl.program_id(0) == 0)
        def _():
            dg_ref[...] = jnp.zeros_like(dg_ref)
            db_ref[...] = jnp.zeros_like(db_ref)

        _, vj = jax.vjp(_ln, p_ref[...], g_ref[...], b_ref[...])
        dp, dg, db = vj(dy_ref[...])
        dp_ref[...] = dp
        dg_ref[...] += _bcast_rows(dg)
        db_ref[...] += _bcast_rows(db)

    blk = pl.BlockSpec((tm, D_MODEL), lambda i: (i, 0))
    vec = pl.BlockSpec((1, D_MODEL), lambda i: (0, 0))
    acc = pl.BlockSpec((SUBLANES, D_MODEL), lambda i: (0, 0))
    return pl.pallas_call(
        body, name="ln_bwd", grid=(S // tm,),
        in_specs=[blk, vec, vec, blk], out_specs=[blk, acc, acc],
        out_shape=[jax.ShapeDtypeStruct((S, D_MODEL), F32)] + [jax.ShapeDtypeStruct((SUBLANES, D_MODEL), F32)] * 2,
        compiler_params=_cparams(10 * _nbytes((tm, D_MODEL), F32), ("arbitrary",)),
    )(pre, g, b, dy)


def _loss_ln_bwd(y, tgt, pre, g, b, *, tm=512):
    S = y.shape[0]
    tm = _tile(S, tm, SUBLANES)

    def body(y_ref, t_ref, p_ref, g_ref, b_ref, dp_ref, dg_ref, db_ref, l_ref):
        @pl.when(pl.program_id(0) == 0)
        def _():
            for r in (dg_ref, db_ref, l_ref):
                r[...] = jnp.zeros_like(r)

        e = y_ref[...] - t_ref[...]
        l_ref[...] += 0.5 * jnp.sum(jnp.mean(e * e, -1, keepdims=True), keepdims=True)
        _, vj = jax.vjp(_ln, p_ref[...], g_ref[...], b_ref[...])
        dp, dg, db = vj(e * (1.0 / D_MODEL))
        dp_ref[...] = dp
        dg_ref[...] += _bcast_rows(dg)
        db_ref[...] += _bcast_rows(db)

    blk = pl.BlockSpec((tm, D_MODEL), lambda i: (i, 0))
    vec = pl.BlockSpec((1, D_MODEL), lambda i: (0, 0))
    acc = pl.BlockSpec((SUBLANES, D_MODEL), lambda i: (0, 0))
    return pl.pallas_call(
        body, name="loss_ln_bwd", grid=(S // tm,),
        in_specs=[blk, blk, blk, vec, vec], out_specs=[blk, acc, acc, pl.BlockSpec((SUBLANES, LANES), lambda i: (0, 0))],
        out_shape=[jax.ShapeDtypeStruct((S, D_MODEL), F32)] + [jax.ShapeDtypeStruct((SUBLANES, D_MODEL), F32)] * 2
                  + [jax.ShapeDtypeStruct((SUBLANES, LANES), F32)],
        compiler_params=_cparams(12 * _nbytes((tm, D_MODEL), F32), ("arbitrary",)),
    )(y, tgt, pre, g, b)


def _layer_fwd(x, xb, w, late):
    pq = _mm(xb, w["win"], mode="nn", name="mm_in_qkv", tm=1024, tn=1024, cols=(0, 3 * D_MODEL))
    proj = _mm(xb, w["win"], mode="nn", name="mm_in_rest", tm=1024, tn=1024, cols=(3 * D_MODEL, 5 * D_MODEL), out_dtype=BF)
    ba = _mm(xb, w["wba"], mode="nn", name="mm_in_ba", tm=1024, tn=LANES)
    qn, kn, vv, gb, conv_out = _qkv_prep(pq, ba, w["convw"], w["arow"], w["dtrow"])
    o, st, tinv, wy_u, wy_w = _delta_fwd(qn, kn, vv, gb)
    w = {**w, **late(o)}
    ya, yb, phi, pa, pb, m = _mix_prep(o, proj, w["onw"], w["sg"], w["sb"], w["ws"], w["bst"], w["wpa"], w["wpb"])
    pre1, x1, x1b = _mm_resid_ln(m, w["wo"], x, w["ln1g"], w["ln1b"], name="mm_out_ln")
    hg, hu, h = _mm_swiglu(x1b, w["wgt"], w["wut"])
    pre2, x2, x2b = _mm_resid_ln(h, w["wd"], x1, w["ln2g"], w["ln2b"], name="mm_down_ln")
    saved = dict(xb=xb, pq=pq, conv_out=conv_out, proj=proj, phi=phi, ba=ba, qn=qn, kn=kn, vv=vv, gb=gb, o=o, st=st, tinv=tinv, wy_u=wy_u, wy_w=wy_w,
                 ya=ya, yb=yb,
                 pa=pa, pb=pb, m=m, pre1=pre1, x1b=x1b, hg=hg, hu=hu, h=h, pre2=pre2)
    return x2, x2b, saved, w


def _layer_bwd(dpre2, ln2_grads, w, s, on_part=None):
    g = {}
    started = lambda part: on_part(part, g) if on_part is not None else None
    after = lambda v, token: v if token is None else v + token.astype(v.dtype)
    g["ln2g"], g["ln2b"] = ln2_grads
    dhg, dhu = _mm_swiglu_bwd(dpre2, w["wd"], s["hg"], s["hu"])
    g["wd"] = _mm(s["h"], dpre2, mode="tn", name="mm_tn_down", tm=1536, tk=1024, out_dtype=BF)
    dx1 = _mm(dhg, w["wgt"], mode="nn", name="mm_nn_gu", pair=(dhu, w["wut"]), add=dpre2, add_scale=ALPHA, tm=1024, tk=1536)
    g["wgt"] = _mm(dhg, s["x1b"], mode="tn", name="mm_tn_gu", tm=1536, tn=1024, tk=2048, out_dtype=BF)
    g["wut"] = _mm(dhu, s["x1b"], mode="tn", name="mm_tn_gu", tm=1536, tn=1024, tk=2048, out_dtype=BF)
    dpre1, g["ln1g"], g["ln1b"] = _ln_bwd(s["pre1"], w["ln1g"], w["ln1b"], dx1)
    g["wo"] = _mm(s["m"], dpre1, mode="tn", name="mm_tn_sq", tm=1024, tk=1024, out_dtype=BF)
    dpa, dpb, dproj = _mm_gate_merge_bwd(dpre1, w["wo"], s["pa"], s["pb"], s["proj"])
    g["wpa"] = _mm(s["ya"], dpa, mode="tn", name="mm_tn_sq", tm=1024, tk=1024, out_dtype=BF)
    g["wpb"] = _mm(s["yb"], dpb, mode="tn", name="mm_tn_sq", tm=1024, tk=1024, out_dtype=BF)
    do, dproj, g["onw"], g["sg"], g["sb"], g["ws"], g["bst"] = _mix_prep_bwd(
        s["o"], s["proj"], s["phi"], after(w["onw"], started("late")), w["sg"], w["sb"], w["ws"], w["bst"],
        dpa, w["wpa"], dpb, w["wpb"], dproj)
    dqn, dkn, dvv, dgb = _delta_bwd(s["qn"], s["kn"], s["vv"], s["gb"], s["st"], s["tinv"], s["wy_u"], s["wy_w"], do)
    dc, dba, g["convw"], g["arow"], g["dtrow"] = _qkv_prep_bwd(
        s["pq"], s["conv_out"], s["ba"], w["arow"], w["dtrow"], dqn, dkn, dvv, dgb)
    dproj = _conv_bwd(dc, w["convw"], dproj)
    g["win"] = _mm(s["xb"], dproj, mode="tn", name="mm_tn_in", tm=1024, tn=1024, tk=2048, out_dtype=BF)
    g["wba"] = _mm(s["xb"], dba, mode="tn", name="mm_tn_ba", tm=1024, tn=LANES, tk=1024, out_dtype=BF)
    dx = _mm(dproj, w["win"], mode="nt", name="mm_nt_in", add=dpre1, add_scale=ALPHA, tm=1024, tk=2048,
             thin=(dba, after(w["wba"], started("early"))))
    return dx, g


def _local_step(x, xb, tgt, layers, on_grads=None):
    saved, weights = [], []
    for layer in layers:
        x, xb, s, w = _layer_fwd(x, xb, *layer(x))
        saved.append(s)
        weights.append(w)
    last = len(layers) - 1
    dpre2, dg, db, lacc = _loss_ln_bwd(x, tgt, saved[last]["pre2"], weights[last]["ln2g"], weights[last]["ln2b"])
    grads = [None] * len(layers)
    for l in reversed(range(len(layers))):
        on_part = functools.partial(on_grads, l) if on_grads is not None else None
        dx, grads[l] = _layer_bwd(dpre2, (dg, db), weights[l], saved[l], on_part)
        if l > 0:
            dpre2, dg, db = _ln_bwd(saved[l - 1]["pre2"], weights[l - 1]["ln2g"], weights[l - 1]["ln2b"], dx)
    return lacc[0, 0], dx, grads


_QKVZ = 4 * D_MODEL
_BA = 2 * N_HEADS


WEIGHT_NAMES = ("w_in", "conv_w", "a_log", "dt_bias", "o_norm_w", "sgu_ln_g", "sgu_ln_b", "w_s", "b_s", "w_pa", "w_pb",
                "w_o", "ln1_g", "ln1_b", "w_ffn_gate", "w_ffn_up", "w_ffn_down", "ln2_g", "ln2_b")
WIRE = ("w_in", "w_ffn_gate", "w_ffn_up", "w_ffn_down", "w_pa", "w_pb", "w_o", "conv_w")
SMALL = (("a_log", N_HEADS), ("dt_bias", N_HEADS), ("o_norm_w", D_HEAD), ("sgu_ln_g", D_MODEL), ("sgu_ln_b", D_MODEL),
         ("w_s", SGU_GROUPS * SGU_BLOCK * SGU_BLOCK), ("b_s", SGU_GROUPS * SGU_BLOCK),
         ("ln1_g", D_MODEL), ("ln1_b", D_MODEL), ("ln2_g", D_MODEL), ("ln2_b", D_MODEL))
SMALL_ROWS = -(-sum(n for _, n in SMALL) // (LANES * SUBLANES)) * SUBLANES
N_MAIN_TILES = (N_IN - _BA) // D_MODEL
ADAM_TILES = dict(w_in=(128, "adamw_in"), w_ffn_gate=(32, "adamw_ffn_rows"), w_ffn_up=(32, "adamw_ffn_rows"),
                  w_ffn_down=(32, "adamw_ffn_rows"), w_pa=(128, "adamw_sq"), w_pb=(128, "adamw_sq"), w_o=(128, "adamw_sq"),
                  conv_w=(CONV_K, "adamw_conv"))


def _pad_to(a, axis, size):
    pads = [(0, 0)] * a.ndim
    pads[axis] = (0, size - a.shape[axis])
    return jnp.pad(a, pads)


def _t(a):
    return jnp.swapaxes(a, 1, 2)


def _wire_blocks(p):
    return dict(
        w_in=_pad_to(p["w_in"].astype(BF), 2, IN_PAD),
        w_ffn_gate=_pad_to(_t(p["w_ffn_gate"]).astype(BF), 1, FFN_PAD), w_ffn_up=_pad_to(_t(p["w_ffn_up"]).astype(BF), 1, FFN_PAD),
        w_ffn_down=_pad_to(p["w_ffn_down"].astype(BF), 1, FFN_PAD),
        w_pa=p["w_pa"].astype(BF), w_pb=p["w_pb"].astype(BF), w_o=p["w_o"].astype(BF),
        conv_w=_pad_to(p["conv_w"], 1, SUBLANES),
    )


def _by_columns(blocks):
    n, r, c = blocks.shape
    return jnp.transpose(blocks, (1, 0, 2)).reshape(r, n * c)


def _to_slots(full, c):
    r = full.shape[0]
    return jnp.transpose(full.reshape(r, N_DEV, c), (1, 0, 2))


def _lane_row(v, at):
    return jnp.pad(v[None], ((0, 0), (at, LANES - at - v.shape[0])))


TRANSPOSED = ("w_ffn_gate", "w_ffn_up")
EARLY = ("w_in", "conv_w")
LATE = ("w_pa", "w_pb", "w_o", "w_ffn_gate", "w_ffn_up", "w_ffn_down")


def _early_weights(stacks, p, l):
    return dict(
        win=_perm_in(stacks["w_in"], D_MODEL, N_MAIN_TILES), wba=_perm_in(stacks["w_in"], LANES, 1),
        convw=_by_columns(stacks["conv_w"][:, :CONV_K]),
        arow=_lane_row(p["a_log"][l], N_HEADS), dtrow=_lane_row(p["dt_bias"][l], N_HEADS),
        onw=p["o_norm_w"][l][None], sg=p["sgu_ln_g"][l][None], sb=p["sgu_ln_b"][l][None],
        ws=p["w_s"][l], bst=_pad_to(p["b_s"][l].T, 1, LANES),
        ln1g=p["ln1_g"][l][None], ln1b=p["ln1_b"][l][None], ln2g=p["ln2_g"][l][None], ln2b=p["ln2_b"][l][None],
    )


def _late_weights(stacks):
    return dict(
        wpa=stacks["w_pa"].reshape(D_MODEL, D_MODEL), wpb=stacks["w_pb"].reshape(D_MODEL, D_MODEL),
        wo=stacks["w_o"].reshape(D_MODEL, D_MODEL),
        wgt=stacks["w_ffn_gate"].reshape(FFN_K, D_MODEL), wut=stacks["w_ffn_up"].reshape(FFN_K, D_MODEL),
        wd=stacks["w_ffn_down"].reshape(FFN_K, D_MODEL),
    )


def _small_pack(parts):
    flat = jnp.concatenate([parts[n].reshape(-1) for n, _ in SMALL])
    return _pad_to(flat, 0, SMALL_ROWS * LANES).reshape(SMALL_ROWS, LANES)


def _small_unpack(rows, like):
    flat, out, off = rows.reshape(-1), {}, 0
    for n, size in SMALL:
        out[n] = flat[off:off + size].reshape(like[n].shape[1:])
        off += size
    return out


def _late_slots(g):
    slots = dict(
        w_ffn_gate=g["wgt"].reshape(N_DEV, FFN_PAD, D_MODEL), w_ffn_up=g["wut"].reshape(N_DEV, FFN_PAD, D_MODEL),
        w_ffn_down=g["wd"].reshape(N_DEV, FFN_PAD, D_MODEL),
        w_pa=g["wpa"].reshape(N_DEV, D_MODEL // N_DEV, D_MODEL), w_pb=g["wpb"].reshape(N_DEV, D_MODEL // N_DEV, D_MODEL),
        w_o=g["wo"].reshape(N_DEV, D_MODEL // N_DEV, D_MODEL),
    )
    return [slots[n] for n in LATE]


def _early_slots(g):
    slots = [_perm_out(g["win"], g["wba"]), _pad_to(_to_slots(g["convw"][:CONV_K], 3 * D_MODEL // N_DEV), 1, SUBLANES)]
    small = _small_pack(dict(
        a_log=g["arow"][0, N_HEADS:2 * N_HEADS], dt_bias=g["dtrow"][0, N_HEADS:2 * N_HEADS], o_norm_w=g["onw"][0],
        sgu_ln_g=g["sg"][0], sgu_ln_b=g["sb"][0], w_s=g["ws"], b_s=g["bst"][:, :SGU_GROUPS].T,
        ln1_g=g["ln1g"][0], ln1_b=g["ln1b"][0], ln2_g=g["ln2g"][0], ln2_b=g["ln2b"][0]))
    return slots, small


def _in_tile_start(j, tile_w):
    if tile_w == LANES:
        return jnp.int32(_QKVZ)
    return j * D_MODEL + jnp.where(j >= _QKVZ // D_MODEL, _BA, 0)


def _select(rows_iota, cols_iota, dev, start, valid):
    hit = (rows_iota + (dev * IN_BLOCK - start) == cols_iota) & (rows_iota < IN_BLOCK) & (cols_iota < valid)
    return jnp.where(hit, 1.0, 0.0).astype(BF)


def _perm_in(stack, tile_w, n_tiles):
    valid = _BA if tile_w == LANES else tile_w

    def first_dev(j):
        return lax.div(_in_tile_start(j, tile_w), jnp.int32(IN_BLOCK))

    def body(w_ref, o_ref, acc_ref):
        j, k = pl.program_id(0), pl.program_id(1)
        sel = _select(_iota((IN_PAD, tile_w), 0), _iota((IN_PAD, tile_w), 1), first_dev(j) + k,
                      _in_tile_start(j, tile_w), valid)
        part = jnp.dot(w_ref[0], sel, preferred_element_type=F32)

        @pl.when(k == 0)
        def _():
            acc_ref[...] = part

        @pl.when(k == 1)
        def _():
            o_ref[...] = (acc_ref[...] + part).astype(BF)

    est = _nbytes((D_MODEL, IN_PAD), BF) + 3 * _nbytes((D_MODEL, tile_w), F32) + 2 * _nbytes((IN_PAD, tile_w), F32)
    return pl.pallas_call(
        body, name="perm_in" if tile_w != LANES else "perm_in_ba", grid=(n_tiles, 2),
        in_specs=[pl.BlockSpec((1, D_MODEL, IN_PAD), lambda j, k: (jnp.minimum(first_dev(j) + k, N_DEV - 1), 0, 0))],
        out_specs=pl.BlockSpec((D_MODEL, tile_w), lambda j, k: (0, j)),
        out_shape=jax.ShapeDtypeStruct((D_MODEL, n_tiles * tile_w), BF),
        scratch_shapes=[pltpu.VMEM((D_MODEL, tile_w), F32)],
        compiler_params=_cparams(est, ("parallel", "arbitrary")),
    )(_in_hbm(stack))


def _perm_out(dmain, dba):
    def tile(d, s):
        c0 = d * IN_BLOCK
        first = lax.div(c0 - jnp.where(c0 < _QKVZ, 0, jnp.minimum(c0 - _QKVZ, _BA)), jnp.int32(D_MODEL))
        return jnp.minimum(first + jnp.minimum(s, 1), N_MAIN_TILES - 1)

    def body(dm_ref, db_ref, o_ref, acc_ref):
        d, s = pl.program_id(0), pl.program_id(1)

        @pl.when(s == 0)
        def _():
            acc_ref[...] = jnp.zeros_like(acc_ref)

        start = _in_tile_start(tile(d, s), D_MODEL)
        overlaps = (start < (d + 1) * IN_BLOCK) & (d * IN_BLOCK < start + D_MODEL)

        @pl.when((s < 2) & overlaps)
        def _():
            sel = _select(_iota((D_MODEL, IN_PAD), 1), _iota((D_MODEL, IN_PAD), 0), d, start, D_MODEL)
            acc_ref[...] += jnp.dot(dm_ref[...], sel, preferred_element_type=F32)

        @pl.when(s == 2)
        def _():
            sel = _select(_iota((LANES, IN_PAD), 1), _iota((LANES, IN_PAD), 0), d, jnp.int32(_QKVZ), _BA)
            o_ref[0] = (acc_ref[...] + jnp.dot(db_ref[...], sel, preferred_element_type=F32)).astype(BF)

    est = 2 * _nbytes((D_MODEL, D_MODEL), BF) + 4 * _nbytes((D_MODEL, IN_PAD), F32)
    return pl.pallas_call(
        body, name="perm_out", grid=(N_DEV, 3),
        in_specs=[pl.BlockSpec((D_MODEL, D_MODEL), lambda d, s: (0, tile(d, s))),
                  pl.BlockSpec((D_MODEL, LANES), lambda d, s: (0, 0))],
        out_specs=pl.BlockSpec((1, D_MODEL, IN_PAD), lambda d, t: (d, 0, 0)),
        out_shape=jax.ShapeDtypeStruct((N_DEV, D_MODEL, IN_PAD), BF),
        scratch_shapes=[pltpu.VMEM((D_MODEL, IN_PAD), F32)],
        compiler_params=_cparams(est, ("parallel", "arbitrary")),
    )(dmain, dba)


def _mesh_place():
    x, y, c = (lax.axis_index(a) for a in MESH_AXES)
    return x, y, c


def _slot(x, y, c):
    return 4 * x + 2 * y + c


def _peer(place, j):
    x, y, c = place
    return (1 - x if j & 4 else x, 1 - y if j & 2 else y, 1 - c if j & 1 else c)


_HBM = pl.BlockSpec(memory_space=pltpu.HBM)
_SEM = pl.BlockSpec(memory_space=pltpu.SEMAPHORE)
_EFFECT = pltpu.SideEffectType.DATAFLOW_SIDE_EFFECTING


def _remote_copy(src_ref, land_ref, slot, per_slot, pslot, sems, u, j, peer):
    return pltpu.make_async_remote_copy(
        src_ref=src_ref.at[pslot] if per_slot else src_ref, dst_ref=land_ref.at[slot],
        send_sem=sems[0].at[u * (N_DEV - 1) + j - 1], recv_sem=sems[1].at[u * (N_DEV - 1) + j - 1],
        device_id=peer, device_id_type=pl.DeviceIdType.MESH)


def _own_copy(src_ref, land_ref, me, per_slot, sems, u):
    return pltpu.make_async_copy(src_ref.at[me] if per_slot else src_ref, land_ref.at[me], sems[2].at[u])


def _exchange_start(name, srcs, per_slot):
    n = len(srcs)
    lands = [jax.ShapeDtypeStruct(s.shape if p else (N_DEV,) + s.shape, s.dtype) for s, p in zip(srcs, per_slot)]

    def body(*refs):
        src_refs, sems, land_refs, token = refs[:n], refs[n:n + 3], refs[2 * n + 3:3 * n + 3], refs[-1]
        place = _mesh_place()
        me = _slot(*place)
        for u in range(n):
            _own_copy(src_refs[u], land_refs[u], me, per_slot[u], sems, u).start()
            for j in range(1, N_DEV):
                peer = _peer(place, j)
                _remote_copy(src_refs[u], land_refs[u], me, per_slot[u], _slot(*peer), sems, u, j, peer).start()
        token[...] = jnp.zeros_like(token)

    hbm = lambda a: pltpu.HBM(a.shape, a.dtype)
    sem = pltpu.SemaphoreType.DMA((n * (N_DEV - 1),))
    outs = pl.pallas_call(
        body, name=name,
        out_shape=(sem, sem, pltpu.SemaphoreType.DMA((n,)), *[hbm(a) for a in srcs], *[hbm(a) for a in lands],
                   jax.ShapeDtypeStruct((SUBLANES, LANES), F32)),
        in_specs=[_HBM] * n, out_specs=(_SEM, _SEM, _SEM, *[_HBM] * (2 * n), pl.BlockSpec(memory_space=pltpu.VMEM)),
        input_output_aliases={i: 3 + i for i in range(n)},
        compiler_params=pltpu.CompilerParams(has_side_effects=_EFFECT),
    )(*[pltpu.with_memory_space_constraint(a, pltpu.HBM) for a in srcs])
    return tuple(outs[:3]), list(outs[3:3 + n]), list(outs[3 + n:3 + 2 * n]), outs[-1]


def _exchange_wait(name, sems, srcs, lands, units, per_slot, after):
    m = len(units)
    after = list(after) if isinstance(after, (list, tuple)) else [after]

    def body(*refs):
        src_refs, land_refs, sem_refs = refs[:m], refs[m:2 * m], refs[2 * m:2 * m + 3]
        place = _mesh_place()
        me = _slot(*place)
        for i, u in enumerate(units):
            _own_copy(src_refs[i], land_refs[i], me, per_slot[u], sem_refs, u).wait()
            for j in range(1, N_DEV):
                peer = _peer(place, j)
                pslot = _slot(*peer)
                cp = _remote_copy(src_refs[i], land_refs[i], pslot, per_slot[u], pslot, sem_refs, u, j, peer)
                cp.wait_send()
                cp.wait_recv()

    hbm = lambda a: pltpu.HBM(a.shape, a.dtype)
    outs = pl.pallas_call(
        body, name=name, out_shape=tuple(hbm(a) for a in list(srcs) + list(lands)),
        in_specs=[_HBM] * (2 * m) + [_SEM] * 3 + [pl.BlockSpec(memory_space=pl.ANY)] * len(after),
        out_specs=tuple([_HBM] * (2 * m)),
        input_output_aliases={i: i for i in range(2 * m)},
        compiler_params=pltpu.CompilerParams(has_side_effects=_EFFECT),
    )(*srcs, *lands, *sems, *after)
    return list(outs[m:])


def _adam_update(g, w, m, v):
    m = ADAM_B1 * m + (1.0 - ADAM_B1) * g
    v = ADAM_B2 * v + (1.0 - ADAM_B2) * jnp.square(g)
    m_hat = m / (1.0 - ADAM_B1 ** ADAM_STEP)
    v_hat = v / (1.0 - ADAM_B2 ** ADAM_STEP)
    return -ADAM_LR * (m_hat / (jnp.sqrt(v_hat) + ADAM_EPS) + ADAM_WD * w), m, v


def _adamw(recvs, w, m, v, *, tr, name):
    L, R, C = w.shape
    rp = max(tr, SUBLANES * (4 // jnp.dtype(recvs[0].dtype).itemsize))
    Cp = recvs[0].shape[2]

    def body(*refs):
        r_refs, (w_ref, m_ref, v_ref, g_ref, d_ref, nm_ref, nv_ref) = refs[:L], refs[L:]
        for l in range(L):
            @pl.when(pl.program_id(0) == l)
            def _(r_ref=r_refs[l]):
                g = r_ref[0, :tr, :C].astype(F32)
                for s in range(1, N_DEV):
                    g = g + r_ref[s, :tr, :C].astype(F32)
                d, nm, nv = _adam_update(g, w_ref[0], m_ref[0], v_ref[0])
                g_ref[0], d_ref[0], nm_ref[0], nv_ref[0] = g, d, nm, nv

    blk = pl.BlockSpec((1, tr, C), lambda l, i: (l, i, 0))
    r_specs = [pl.BlockSpec((N_DEV, rp, Cp), lambda l, i, k=k: (0, jnp.where(l == k, i, 0), 0)) for k in range(L)]
    est = 2 * _nbytes((N_DEV, rp, Cp), recvs[0].dtype) + 8 * _nbytes((tr, Cp), F32)
    return pl.pallas_call(
        body, name=name, grid=(L, R // tr),
        in_specs=r_specs + [blk] * 3, out_specs=[blk] * 4,
        out_shape=[jax.ShapeDtypeStruct((L, R, C), F32)] * 4,
        compiler_params=_cparams(est, ("arbitrary", "arbitrary")),
    )(*recvs, w, m, v)


def _adamw_small(recv, w, m, v):
    def body(r_ref, w_ref, m_ref, v_ref, g_ref, d_ref, nm_ref, nv_ref):
        g = r_ref[0]
        for s in range(1, N_DEV):
            g = g + r_ref[s]
        g_ref[...] = g
        d_ref[...], nm_ref[...], nv_ref[...] = _adam_update(g, w_ref[...], m_ref[...], v_ref[...])

    vm = pl.BlockSpec(memory_space=pltpu.VMEM)
    return pl.pallas_call(
        body, name="adamw_small", in_specs=[vm] * 4, out_specs=[vm] * 4,
        out_shape=[jax.ShapeDtypeStruct((SMALL_ROWS, LANES), F32)] * 4,
        compiler_params=_cparams(20 * _nbytes((SMALL_ROWS, LANES), F32)),
    )(recv, w, m, v)


def kernel(x, w_in, conv_w, a_log, dt_bias, o_norm_w, sgu_ln_g, sgu_ln_b, w_s, b_s, w_pa, w_pb, w_o, ln1_g, ln1_b, w_ffn_gate, w_ffn_up, w_ffn_down, ln2_g, ln2_b, loss_target, m_w_in, m_conv_w, m_a_log, m_dt_bias, m_o_norm_w, m_sgu_ln_g, m_sgu_ln_b, m_w_s, m_b_s, m_w_pa, m_w_pb, m_w_o, m_ln1_g, m_ln1_b, m_w_ffn_gate, m_w_ffn_up, m_w_ffn_down, m_ln2_g, m_ln2_b, v_w_in, v_conv_w, v_a_log, v_dt_bias, v_o_norm_w, v_sgu_ln_g, v_sgu_ln_b, v_w_s, v_b_s, v_w_pa, v_w_pb, v_w_o, v_ln1_g, v_ln1_b, v_w_ffn_gate, v_w_ffn_up, v_w_ffn_down, v_ln2_g, v_ln2_b):
    given = dict(locals())
    P = {n: given[n] for n in WEIGHT_NAMES}
    M = {n: given["m_" + n] for n in WEIGHT_NAMES}
    V = {n: given["v_" + n] for n in WEIGHT_NAMES}

    wire = _wire_blocks(P)
    units = [(n, l) for l in range(DEPTH) for n in EARLY + LATE]
    whole = [False] * len(units)
    g_sems, g_srcs, g_lands, g_token = _exchange_start("gather_start", [wire[n][l] for n, l in units], whole)

    one = 1.0 + g_token[0, 0]
    xb = (x[0] * one).astype(BF)
    small_in = [[_small_pack({n: T[n][l] * one for n, _ in SMALL}) for T in (P, M, V)] for l in range(DEPTH)]
    adam_in = {n: (P[n], M[n], V[n]) for n in WIRE}
    adam_in["w_in"], _ = lax.optimization_barrier((adam_in["w_in"], g_token))
    prepared = [xb, *[a for packs in small_in for a in packs], *adam_in["w_in"]]

    def gathered(name, names, l, after):
        idx = [units.index((n, l)) for n in names]
        got = _exchange_wait(name, g_sems, [g_srcs[i] for i in idx], [g_lands[i] for i in idx], idx, whole, after)
        return dict(zip(names, got))

    def layer(l):
        def weights(x_in):
            after = prepared if l == 0 else x_in
            early = _early_weights(gathered(f"gather_wait_early{l}", EARLY, l, after), P, l)
            return early, lambda ya: _late_weights(gathered(f"gather_wait_late{l}", LATE, l, ya))
        return weights

    pending = {}

    def on_grads(l, part, g):
        if part == "late":
            srcs, names = _late_slots(g), LATE
            per_slot = [True] * len(srcs)
        else:
            slots, small = _early_slots(g)
            srcs, names = slots + [small], EARLY + ("small",)
            per_slot = [True] * len(slots) + [False]
        sems, s_thru, l_thru, token = _exchange_start(f"exchange_start_{part}{l}", srcs, per_slot)
        pending[l, part] = (names, sems, s_thru, l_thru, per_slot)
        return token[0, 0]

    loss_local, dx, _ = _local_step(x[0], xb, loss_target[0], [layer(l) for l in range(DEPTH)], on_grads)
    loss = lax.psum(loss_local, MESH_AXES)

    recv = [{} for _ in range(DEPTH)]

    def received(l, part, after):
        names, sems, s_thru, l_thru, per_slot = pending[l, part]
        got = _exchange_wait(f"exchange_wait_{part}{l}", sems, s_thru, l_thru, list(range(len(s_thru))), per_slot, after)
        recv[l].update(zip(names, got))

    out = {}

    def adamw(names):
        for n in names:
            tr, name = ADAM_TILES[n]
            view = _t if n in TRANSPOSED else (lambda a: a)
            res = _adamw([recv[l][n] for l in range(DEPTH)], *[view(a) for a in adam_in[n]], tr=tr, name=name)
            out[n] = [view(r) for r in res]

    for l in reversed(range(DEPTH)):
        received(l, "late", dx)
    adamw(LATE)
    for l in reversed(range(DEPTH)):
        received(l, "early", out[LATE[-1]][0])
    adamw(EARLY)
    small = [_adamw_small(recv[l]["small"], *small_in[l]) for l in range(DEPTH)]
    for n, _ in SMALL:
        out[n] = [jnp.stack([_small_unpack(small[l][i], P)[n] for l in range(DEPTH)]) for i in range(4)]
    return (loss, dx[None], *[out[n][i] for i in range(4) for n in WEIGHT_NAMES])
```

```python
import functools
import math

import jax
import jax.numpy as jnp
from jax import lax
from jax.experimental import pallas as pl
from jax.experimental.pallas import tpu as pltpu

F32 = jnp.float32
BF = jnp.bfloat16
HIGHEST = lax.Precision.HIGHEST

D_MODEL = 1024
DEPTH = 2
N_HEADS = 8
D_HEAD = 128
CONV_K = 4
SGU_BLOCK = 128
SGU_GROUPS = 8
SGU_CHUNK = 64
FFN_HIDDEN = 2816
N_IN = 8208
N_DEV = 8
IN_BLOCK, IN_PAD = N_IN // N_DEV, 1152
FFN_BLOCK, FFN_PAD = FFN_HIDDEN // N_DEV, 384
FFN_K = N_DEV * FFN_PAD
ALPHA = (2 * DEPTH) ** 0.25
LN_EPS = 1e-5
RMS_EPS = 1e-6
ADAM_LR, ADAM_B1, ADAM_B2, ADAM_EPS, ADAM_WD, ADAM_STEP = 0.001, 0.9, 0.999, 1e-08, 0.01, 10

MESH_AXES = ("x", "y", "c")
DELTA_CHUNK = 128
DELTA_HEADS_PER_STEP = 8
LANES = 128
SUBLANES = 8
VMEM_BYTES = 64 * 1024 * 1024
HALO = SUBLANES
HALO_BF = 2 * SUBLANES


def _cparams(est_bytes, dims=None):
    limit = int(min(max(2 * est_bytes + (8 << 20), 32 << 20), VMEM_BYTES - (6 << 20)))
    kw = dict(vmem_limit_bytes=limit)
    if dims is not None:
        kw["dimension_semantics"] = dims
    return pltpu.CompilerParams(**kw)


def _nbytes(shape, dtype):
    return math.prod(shape) * jnp.dtype(dtype).itemsize


def _dims(kind, ndim):
    lhs, rhs = {"nn": (1, 0), "nt": (1, 1), "tn": (0, 0)}[kind]
    b = ndim - 2
    return (((lhs + b,), (rhs + b,)), (tuple(range(b)), tuple(range(b))))


def _mxu(a, b, kind):
    return lax.dot_general(a, b, _dims(kind, a.ndim), preferred_element_type=F32)


def _dot(a, b):
    return _mxu(a.astype(BF), b.astype(BF), "nn")


def _dot_nt(a, b):
    return _mxu(a.astype(BF), b.astype(BF), "nt")


def _dot_tn(a, b):
    return _mxu(a.astype(BF), b.astype(BF), "tn")


def _split(a):
    hi = a.astype(BF)
    return hi, (a - hi.astype(F32)).astype(BF)


def _dot3(a, b, kind):
    (ah, al), (bh, bl) = _split(a), _split(b)
    return _mxu(ah, bh, kind) + (_mxu(ah, bl, kind) + _mxu(al, bh, kind))


def _dotf(a, b):
    return _dot3(a, b, "nn")


def _dotf_nt(a, b):
    return _dot3(a, b, "nt")


def _dot01(sel, x, kind="nn"):
    s = jnp.broadcast_to(sel.astype(BF), x.shape[:-2] + sel.shape)
    h1 = x.astype(BF)
    r1 = x - h1.astype(F32)
    h2 = r1.astype(BF)
    h3 = (r1 - h2.astype(F32)).astype(BF)
    return _mxu(s, h1, kind) + (_mxu(s, h2, kind) + _mxu(s, h3, kind))


def _sigmoid(x):
    return 0.5 * jnp.tanh(0.5 * x) + 0.5


def _silu(x):
    return x * _sigmoid(x)


def _silu_and_grad(x):
    s = _sigmoid(x)
    return x * s, s * (1.0 + x * (1.0 - s))


def _softplus(x):
    return jnp.maximum(x, 0.0) + jnp.log1p(jnp.exp(-jnp.abs(x)))


def _ln(x, g, b):
    mu = jnp.mean(x, -1, keepdims=True)
    xc = x - mu
    var = jnp.mean(xc * xc, -1, keepdims=True)
    return xc * lax.rsqrt(var + LN_EPS) * g + b


def _iota(shape, dim):
    return lax.broadcasted_iota(jnp.int32, shape, dim)


def _tile(n, pref, align):
    if n <= pref:
        return n
    t = (pref // align) * align
    while t >= align:
        if n % t == 0:
            return t
        t -= align
    raise ValueError(f"no tile for {n} (pref {pref}, align {align})")


def _bcast_rows(v, rows=SUBLANES):
    return jnp.broadcast_to(v, (rows, v.shape[-1]))


def _in_hbm(t):
    return pltpu.with_memory_space_constraint(t, pltpu.HBM)


def _mm(a, b, *, mode, name, out_dtype=F32, add=None, add_scale=1.0, tm=512, tn=1024, tk=1024, cols=None, pair=None,
        thin=None):
    if mode == "nn":
        (M, K), N = a.shape, b.shape[1]
    elif mode == "nt":
        (M, K), N = a.shape, b.shape[0]
    else:
        (K, M), N = a.shape, b.shape[1]
    col0 = 0
    if cols is not None:
        col0, N = cols
    tm = _tile(M, tm, LANES if mode == "tn" else SUBLANES * 2)
    tn = _tile(N, tn, LANES)
    tk = _tile(K, tk, LANES)
    nk = K // tk
    j0 = col0 // tn
    if mode == "nn":
        a_spec = pl.BlockSpec((tm, tk), lambda i, j, k: (i, k))
        b_spec = pl.BlockSpec((tk, tn), lambda i, j, k: (k, j + j0))
        dot = _dot
    elif mode == "nt":
        a_spec = pl.BlockSpec((tm, tk), lambda i, j, k: (i, k))
        b_spec = pl.BlockSpec((tn, tk), lambda i, j, k: (j, k))
        dot = _dot_nt
    else:
        a_spec = pl.BlockSpec((tk, tm), lambda i, j, k: (k, i))
        b_spec = pl.BlockSpec((tk, tn), lambda i, j, k: (k, j))
        dot = _dot_tn
    o_spec = pl.BlockSpec((tm, tn), lambda i, j, k: (i, j))
    has_add = add is not None
    assert thin is None or mode == "nt"

    n_ab = 2 if pair is None else 4

    def body(*refs):
        ab, (o_ref, acc_ref) = refs[:n_ab], refs[-2:]
        add_ref = refs[n_ab] if has_add else None
        thin_refs = refs[n_ab + has_add:-2]
        k = pl.program_id(2)
        part = dot(ab[0][...], ab[1][...])
        if pair is not None:
            part = part + dot(ab[2][...], ab[3][...])

        def finish(total):
            if has_add:
                total = total + add_scale * add_ref[...]
            if thin is not None:
                total = total + _dot_nt(thin_refs[0][...], thin_refs[1][...])
            o_ref[...] = total.astype(out_dtype)

        if nk == 1:
            finish(part)
        else:
            @pl.when(k == 0)
            def _():
                acc_ref[...] = part

            @pl.when(jnp.logical_and(k > 0, k < nk - 1))
            def _():
                acc_ref[...] += part

            @pl.when(k == nk - 1)
            def _():
                finish(acc_ref[...] + part)

    in_specs = [a_spec, b_spec] * (n_ab // 2) + ([o_spec] if has_add else [])
    args = (a, b) + (tuple(pair) if pair is not None else ()) + ((add,) if has_add else ())
    est = ((n_ab // 2) * (_nbytes((tm, tk), a.dtype) + _nbytes((tk, tn), b.dtype)) + 2 * _nbytes((tm, tn), F32)
           + (_nbytes((tm, tn), F32) if has_add else 0)) + 2 * _nbytes((tm, tn), F32)
    if thin is not None:
        k3 = thin[0].shape[1]
        in_specs += [pl.BlockSpec((tm, k3), lambda i, j, k: (i, 0)), pl.BlockSpec((tn, k3), lambda i, j, k: (j, 0))]
        args += tuple(thin)
        est += _nbytes((tm, k3), thin[0].dtype) + _nbytes((tn, k3), thin[1].dtype)
    return pl.pallas_call(
        body, name=name,
        grid=(M // tm, N // tn, nk),
        in_specs=in_specs, out_specs=o_spec,
        out_shape=jax.ShapeDtypeStruct((M, N), out_dtype),
        scratch_shapes=[pltpu.VMEM((tm, tn) if nk > 1 else (SUBLANES, LANES), F32)],
        compiler_params=_cparams(est, ("parallel", "parallel", "arbitrary")),
    )(*[_in_hbm(t) for t in args])


def _shifted(xt, halo, first):
    halo = jnp.where(first, 0.0, halo)
    xc = jnp.concatenate([halo, xt], axis=0)
    return [xt] + [pltpu.roll(xc, s, 0)[HALO:] for s in range(1, CONV_K)]


def _conv_taps(shifted, w_ref):
    out = shifted[0] * w_ref[CONV_K - 1:CONV_K, :]
    for s in range(1, CONV_K):
        out = out + shifted[s] * w_ref[CONV_K - 1 - s:CONV_K - s, :]
    return out


def _gates(ba, arow, dtrow):
    lane = _iota(ba.shape, 1)
    beta = _sigmoid(ba)
    g = -jnp.exp(arow) * _softplus(ba + dtrow)
    return jnp.where(lane < N_HEADS, beta, jnp.where(lane < 2 * N_HEADS, g, 0.0))


def _l2n(x):
    return x * lax.rsqrt(jnp.sum(x * x, -1, keepdims=True) + RMS_EPS)


def _qkv_prep(proj, xb, wba, convw, arow, dtrow, *, tm=256):
    S = proj.shape[0]
    tm = _tile(S, tm, SUBLANES * 2)
    W3 = 3 * D_MODEL
    hb = tm // HALO

    def body(xt_ref, halo_ref, xb_ref, wba_ref, w_ref, a_ref, dt_ref, q_ref, k_ref, v_ref, gb_ref, c_ref, ba_ref):
        ba_ref[...] = _dot(xb_ref[...], wba_ref[...])
        c = _conv_taps(_shifted(xt_ref[...], halo_ref[...], pl.program_id(0) == 0), w_ref)
        c_ref[...] = c
        c = _silu(c)
        for h in range(N_HEADS):
            lo = h * D_HEAD
            q_ref[:, lo:lo + D_HEAD] = _l2n(c[:, lo:lo + D_HEAD])
            k_ref[:, lo:lo + D_HEAD] = _l2n(c[:, D_MODEL + lo:D_MODEL + lo + D_HEAD])
        v_ref[...] = c[:, 2 * D_MODEL:]
        gb_ref[...] = _gates(ba_ref[...], a_ref[...], dt_ref[...])

    row = lambda w, col=0: pl.BlockSpec((tm, w), lambda i: (i, col))
    full = lambda shape: pl.BlockSpec(shape, lambda i: (0,) * len(shape))
    est = 4 * _nbytes((tm, W3), F32) + _nbytes((tm, D_MODEL), F32)
    return pl.pallas_call(
        body, name="qkv_prep", grid=(S // tm,),
        in_specs=[row(W3), pl.BlockSpec((HALO, W3), lambda i: (jnp.maximum(i * hb - 1, 0), 0)), row(D_MODEL),
                  full((D_MODEL, LANES)), full((CONV_K, W3)), full((1, LANES)), full((1, LANES))],
        out_specs=[row(D_MODEL), row(D_MODEL), row(D_MODEL), row(LANES), row(W3), row(LANES)],
        out_shape=[jax.ShapeDtypeStruct((S, D_MODEL), F32)] * 3 + [jax.ShapeDtypeStruct((S, LANES), F32),
                                                                   jax.ShapeDtypeStruct((S, W3), F32),
                                                                   jax.ShapeDtypeStruct((S, LANES), F32)],
        compiler_params=_cparams(est, ("arbitrary",)),
    )(proj, proj, xb, wba, convw, arow, dtrow)


def _qkv_prep_bwd(proj, conv_out, ba, arow, dtrow, dq, dk, dv, dgb, xb, *, tm=256):
    S = proj.shape[0]
    tm = _tile(S, tm, SUBLANES * 2)
    W3 = 3 * D_MODEL
    hb = tm // HALO

    def body(xt_ref, halo_ref, c_ref, ba_ref, a_ref, dt_ref, dq_ref, dk_ref, dv_ref, dgb_ref, xb_ref,
             dcb_ref, dba_ref, dw_ref, da_ref, ddt_ref, dwba_ref, dc_ref):
        i = pl.program_id(0)

        @pl.when(i == 0)
        def _():
            dw_ref[...] = jnp.zeros_like(dw_ref)
            da_ref[...] = jnp.zeros_like(da_ref)
            ddt_ref[...] = jnp.zeros_like(ddt_ref)
            dwba_ref[...] = jnp.zeros_like(dwba_ref)

        shifted = _shifted(xt_ref[...], halo_ref[...], i == 0)
        a, ds = _silu_and_grad(c_ref[...])
        for h in range(N_HEADS):
            for base, d_ref in ((0, dq_ref), (D_MODEL, dk_ref)):
                lo = base + h * D_HEAD
                _, vj = jax.vjp(_l2n, a[:, lo:lo + D_HEAD])
                (dx,) = vj(d_ref[:, h * D_HEAD:(h + 1) * D_HEAD])
                dc_ref[:, lo:lo + D_HEAD] = dx * ds[:, lo:lo + D_HEAD]
        dc_ref[:, 2 * D_MODEL:] = dv_ref[...] * ds[:, 2 * D_MODEL:]
        dc = dc_ref[...]
        dcb_ref[...] = dc.astype(BF)
        for s in range(CONV_K):
            kk = CONV_K - 1 - s
            dw_ref[kk:kk + 1, :] += jnp.sum(dc * shifted[s], axis=0, keepdims=True)
        _, vj = jax.vjp(_gates, ba_ref[...], a_ref[...], dt_ref[...])
        dba, da, ddt = vj(dgb_ref[...])
        dba_ref[...] = dba.astype(BF)
        da_ref[...] += _bcast_rows(da)
        ddt_ref[...] += _bcast_rows(ddt)
        dwba_ref[...] += _dot_tn(xb_ref[...], dba.astype(BF))

    row = lambda w, col=0: pl.BlockSpec((tm, w), lambda i: (i, col))
    full = lambda shape: pl.BlockSpec(shape, lambda i: (0,) * len(shape))
    est = 8 * _nbytes((tm, W3), F32) + _nbytes((tm, D_MODEL), F32)
    return pl.pallas_call(
        body, name="qkv_prep_bwd", grid=(S // tm,),
        in_specs=[row(W3), pl.BlockSpec((HALO, W3), lambda i: (jnp.maximum(i * hb - 1, 0), 0)), row(W3), row(LANES),
                  full((1, LANES)), full((1, LANES)),
                  row(D_MODEL), row(D_MODEL), row(D_MODEL), row(LANES), row(D_MODEL)],
        out_specs=[row(W3), row(LANES), full((SUBLANES, W3)), full((SUBLANES, LANES)), full((SUBLANES, LANES)),
                   full((D_MODEL, LANES))],
        out_shape=[jax.ShapeDtypeStruct((S, W3), BF), jax.ShapeDtypeStruct((S, LANES), BF),
                   jax.ShapeDtypeStruct((SUBLANES, W3), F32), jax.ShapeDtypeStruct((SUBLANES, LANES), F32),
                   jax.ShapeDtypeStruct((SUBLANES, LANES), F32), jax.ShapeDtypeStruct((D_MODEL, LANES), F32)],
        scratch_shapes=[pltpu.VMEM((tm, W3), F32)],
        compiler_params=_cparams(est, ("arbitrary",)),
    )(proj, proj, conv_out, ba, arow, dtrow, dq, dk, dv, dgb, xb)


def _conv_bwd(dc, convw, dproj, *, tm=256):
    S, W3 = dc.shape
    tm = _tile(S, tm, HALO_BF)
    hb = tm // HALO_BF
    nt = S // tm

    def body(dc_ref, nxt_ref, w_ref, dproj_ref, o_ref):
        last = pl.program_id(0) == nt - 1
        nxt = jnp.where(last, 0.0, nxt_ref[...].astype(F32))
        cur = dc_ref[...].astype(F32)
        xc = jnp.concatenate([cur, nxt], axis=0)
        out = cur * w_ref[CONV_K - 1:CONV_K, :]
        for s in range(1, CONV_K):
            out = out + pltpu.roll(xc, tm + HALO_BF - s, 0)[:tm] * w_ref[CONV_K - 1 - s:CONV_K - s, :]
        o_ref[...] = out.astype(BF)

    est = 5 * _nbytes((tm, W3), F32)
    return pl.pallas_call(
        body, name="conv_bwd", grid=(nt,),
        in_specs=[pl.BlockSpec((tm, W3), lambda i: (i, 0)),
                  pl.BlockSpec((HALO_BF, W3), lambda i: (jnp.minimum((i + 1) * hb, S // HALO_BF - 1), 0)),
                  pl.BlockSpec((CONV_K, W3), lambda i: (0, 0)), pl.BlockSpec(memory_space=pl.ANY)],
        out_specs=pl.BlockSpec((tm, W3), lambda i: (i, 0)),
        out_shape=jax.ShapeDtypeStruct(dproj.shape, BF),
        input_output_aliases={3: 0},
        compiler_params=_cparams(est, ("parallel",)),
    )(dc, dc, convw, dproj)


NEUMANN_BLOCK = 8


def _inv_unit_lower(A):
    C = A.shape[-1]
    row, col = _iota((C, C), 0), _iota((C, C), 1)
    eye = jnp.where(row == col, 1.0, 0.0).astype(F32)
    Ab = A.astype(BF)
    sh = jnp.int32(int(math.log2(NEUMANN_BLOCK)))
    B = jnp.where(lax.shift_right_logical(row, sh) == lax.shift_right_logical(col, sh), Ab, jnp.zeros_like(Ab))
    B2 = _mxu(B, B, "nn")
    B4 = _dot3(B2, B2, "nn")
    b2h, b2l = _split(B2)
    P = eye - B.astype(F32) + B2 - (_mxu(B, b2h, "nn") + _mxu(B, b2l, "nn"))
    T = P + _dot3(P, B4, "nn")
    b = NEUMANN_BLOCK
    while b < C:
        hi = ~(2 * b - 1)
        off = ((row & hi) == (col & hi)) & ((row & b) != 0) & ((col & b) == 0)
        Aoff = jnp.where(off, Ab, jnp.zeros_like(Ab))
        th, tl = _split(T)
        xh, xl = _split(_mxu(th, Aoff, "nn") + _mxu(tl, Aoff, "nn"))
        T = T - (_mxu(xh, th, "nn") + (_mxu(xh, tl, "nn") + _mxu(xl, th, "nn")))
        b *= 2
    return T


def _delta_common(q, k, g, beta):
    C = q.shape[-2]
    row, col = _iota((C, C), 0), _iota((C, C), 1)
    tril = row >= col
    qs = q * (D_HEAD ** -0.5)
    gcb = _dot01(jnp.where(tril, 1.0, 0.0), jnp.broadcast_to(g, g.shape[:-1] + (LANES,)))
    gc = gcb[..., :1]
    gr = jnp.swapaxes(gcb, -1, -2)
    Dm = jnp.exp(jnp.where(tril, gc - gr, -1e30))
    Dmt = jnp.exp(jnp.where(row <= col, gr - gc, -1e30))
    eg = jnp.exp(gc)
    gl = jnp.sum(jnp.where(_iota((C, 1), 0) == C - 1, gc, 0.0), axis=(-2, -1), keepdims=True)
    el = jnp.exp(gl)
    er = jnp.exp(gl - gc)
    kb = k * beta
    KK = _dot_nt(kb, k)
    QK = _dot_nt(qs, k)
    return dict(row=row, col=col, tril=tril, qs=qs, gc=gc, Dm=Dm, Dmt=Dmt, eg=eg, el=el, er=er, kb=kb, KK=KK, QK=QK)


def _delta_chunk_fwd(S0, q, k, v, g, beta):
    m = _delta_common(q, k, g, beta)
    T = _inv_unit_lower(jnp.where(m["row"] > m["col"], m["KK"] * m["Dm"], 0.0))
    u = _dotf(T, v * beta)
    w = _dotf(T, m["kb"] * m["eg"])
    vn = u - _dot(w, S0)
    o = _dot(m["qs"] * m["eg"], S0) + _dot(m["QK"] * m["Dm"], vn)
    S1 = S0 * m["el"] + _dot_tn(k * m["er"], vn)
    return o, S1, jnp.swapaxes(T, -1, -2), u, w


def _delta_chunk_bwd(S0, q, k, v, g, beta, Tt, u, w, do, dS1):
    m = _delta_common(q, k, g, beta)
    C = q.shape[-2]
    qs, Dm, Dmt, eg, el, er, kb, KK, QK = (m[n] for n in ("qs", "Dm", "Dmt", "eg", "el", "er", "kb", "KK", "QK"))
    strict = m["row"] > m["col"]
    total = lambda x: jnp.sum(x, axis=(-2, -1), keepdims=True)
    vn = u - _dot(w, S0)
    qg = qs * eg
    kr = k * er

    dvn = _dot(_dot_nt(k, qs) * Dmt, do) + _dot(kr, dS1)
    dS0 = dS1 * el + _dot_tn(qg, do) - _dot_tn(w, dvn)
    d_el = total(dS1 * S0)
    dqg = _dot_nt(do, S0)
    dqs = dqg * eg
    deg = jnp.sum(dqg * qs, -1, keepdims=True)
    dP = _dot_nt(do, vn)
    dPD = dP * Dm
    dqs = dqs + _dot(dPD, k)
    dk = _dot(_dot_nt(vn, do) * Dmt, qs)
    dD = dP * QK
    dkr = _dot_nt(vn, dS1)
    dk = dk + dkr * er
    der = jnp.sum(dkr * k, -1, keepdims=True)
    dw = -_dot_nt(dvn, S0)
    th, tl = _split(Tt)

    def tt_times(x):
        xh, xl = _split(x)
        return _mxu(th, xh, "nn") + (_mxu(th, xl, "nn") + _mxu(tl, xh, "nn"))

    dru = tt_times(dvn)
    drw = tt_times(dw)
    dA = -(_dotf_nt(dru, u) + _dotf_nt(drw, w))
    dAm = jnp.where(strict, dA, 0.0)
    dKK = dAm * Dm
    dkb = _dot(dKK, k)
    dk = dk + _dot_tn(dKK, kb)
    dD = dD + dAm * KK
    dv = dru * beta
    dbeta = jnp.sum(dru * v, -1, keepdims=True)
    dkb = dkb + drw * eg
    deg = deg + jnp.sum(drw * kb, -1, keepdims=True)
    dk = dk + dkb * beta
    dbeta = dbeta + jnp.sum(dkb * k, -1, keepdims=True)
    E = dD * Dm
    dgc = jnp.sum(E, -1, keepdims=True) - jnp.sum(jnp.swapaxes(E, -1, -2), -1, keepdims=True)
    dgc = dgc + deg * eg - der * er
    dgl = total(der * er) + d_el * el
    dgc = dgc + jnp.where(_iota((C, 1), 0) == C - 1, dgl, 0.0)
    triu = jnp.where(m["row"] <= m["col"], 1.0, 0.0)
    dg = _dot01(triu, jnp.broadcast_to(dgc, dgc.shape[:-1] + (LANES,)))[..., :1]
    dq = dqs * (D_HEAD ** -0.5)
    return dq, dk, dv, dg, dbeta, dS0


def _head_cols(gb, h):
    lane = _iota(gb.shape, 1)
    beta = jnp.sum(jnp.where(lane == h, gb, 0.0), -1, keepdims=True)
    g = jnp.sum(jnp.where(lane == N_HEADS + h, gb, 0.0), -1, keepdims=True)
    return g, beta


def _delta_fwd(q, k, v, gb):
    S = q.shape[0]
    C = DELTA_CHUNK
    N = S // C

    HB = DELTA_HEADS_PER_STEP

    def body(q_ref, k_ref, v_ref, gb_ref, o_ref, st_ref, t_ref, u_ref, w_ref, s_scr):
        n, hb = pl.program_id(0), pl.program_id(1)
        gb = gb_ref[...]

        @pl.when(n == 0)
        def _():
            for hh in range(HB):
                s_scr[hb * HB + hh] = jnp.zeros((D_HEAD, D_HEAD), F32)

        heads = [hb * HB + hh for hh in range(HB)]
        cols = [slice(hh * D_HEAD, (hh + 1) * D_HEAD) for hh in range(HB)]
        per_head = lambda ref: jnp.stack([ref[:, c] for c in cols])
        g, beta = (jnp.stack(t) for t in zip(*[_head_cols(gb, h) for h in heads]))
        S0 = jnp.stack([s_scr[h] for h in heads])
        o, S1, Tt, u, w = _delta_chunk_fwd(S0, per_head(q_ref), per_head(k_ref), per_head(v_ref), g, beta)
        for hh in range(HB):
            st_ref[hh, 0] = S0[hh]
            t_ref[hh, 0] = Tt[hh]
            o_ref[:, cols[hh]] = o[hh]
            u_ref[:, cols[hh]] = u[hh]
            w_ref[:, cols[hh]] = w[hh]
            s_scr[heads[hh]] = S1[hh]

    hd = pl.BlockSpec((C, HB * D_HEAD), lambda n, h: (n, h))
    mat = pl.BlockSpec((HB, 1, D_HEAD, D_HEAD), lambda n, h: (h, n, 0, 0))
    est = 40 * HB * _nbytes((C, D_HEAD), F32)
    seq = jax.ShapeDtypeStruct((S, N_HEADS * D_HEAD), F32)
    return pl.pallas_call(
        body, name="delta_fwd", grid=(N, N_HEADS // HB),
        in_specs=[hd, hd, hd, pl.BlockSpec((C, LANES), lambda n, h: (n, 0))],
        out_specs=[hd, mat, mat, hd, hd],
        out_shape=[seq, jax.ShapeDtypeStruct((N_HEADS, N, D_HEAD, D_HEAD), F32),
                   jax.ShapeDtypeStruct((N_HEADS, N, C, C), F32), seq, seq],
        scratch_shapes=[pltpu.VMEM((N_HEADS, D_HEAD, D_HEAD), F32)],
        compiler_params=_cparams(est, ("arbitrary", "arbitrary")),
    )(q, k, v, gb)


def _delta_bwd(q, k, v, gb, st, tinv, u, w, do):
    S = q.shape[0]
    C = DELTA_CHUNK
    N = S // C

    HB = DELTA_HEADS_PER_STEP

    def body(q_ref, k_ref, v_ref, gb_ref, st_ref, t_ref, u_ref, w_ref, do_ref, dq_ref, dk_ref, dv_ref, dgb_ref, ds_scr):
        n, hb = pl.program_id(0), pl.program_id(1)
        gb = gb_ref[...]
        lane = _iota((C, LANES), 1)
        dgb = jnp.zeros((C, LANES), F32)

        @pl.when(n == 0)
        def _():
            for hh in range(HB):
                ds_scr[hb * HB + hh] = jnp.zeros((D_HEAD, D_HEAD), F32)

        heads = [hb * HB + hh for hh in range(HB)]
        cols = [slice(hh * D_HEAD, (hh + 1) * D_HEAD) for hh in range(HB)]
        per_head = lambda ref: jnp.stack([ref[:, c] for c in cols])
        g, beta = (jnp.stack(t) for t in zip(*[_head_cols(gb, h) for h in heads]))
        dS1 = jnp.stack([ds_scr[h] for h in heads])
        dq, dk, dv, dg, dbeta, dS0 = _delta_chunk_bwd(
            st_ref[:, 0], per_head(q_ref), per_head(k_ref), per_head(v_ref), g, beta, t_ref[:, 0],
            per_head(u_ref), per_head(w_ref), per_head(do_ref), dS1)
        for hh, h in enumerate(heads):
            dq_ref[:, cols[hh]] = dq[hh]
            dk_ref[:, cols[hh]] = dk[hh]
            dv_ref[:, cols[hh]] = dv[hh]
            dgb = dgb + jnp.where(lane == h, dbeta[hh], 0.0) + jnp.where(lane == N_HEADS + h, dg[hh], 0.0)
            ds_scr[h] = dS0[hh]

        @pl.when(hb == 0)
        def _():
            dgb_ref[...] = dgb

        @pl.when(hb > 0)
        def _():
            dgb_ref[...] += dgb

    hd = pl.BlockSpec((C, HB * D_HEAD), lambda n, h: (N - 1 - n, h))
    mat = pl.BlockSpec((HB, 1, D_HEAD, D_HEAD), lambda n, h: (h, N - 1 - n, 0, 0))
    gbs = pl.BlockSpec((C, LANES), lambda n, h: (N - 1 - n, 0))
    est = 60 * HB * _nbytes((C, D_HEAD), F32)
    return pl.pallas_call(
        body, name="delta_bwd", grid=(N, N_HEADS // HB),
        in_specs=[hd, hd, hd, gbs, mat, mat, hd, hd, hd],
        out_specs=[hd, hd, hd, gbs],
        out_shape=[jax.ShapeDtypeStruct((S, N_HEADS * D_HEAD), F32)] * 3 + [jax.ShapeDtypeStruct((S, LANES), F32)],
        scratch_shapes=[pltpu.VMEM((N_HEADS, D_HEAD, D_HEAD), F32)],
        compiler_params=_cparams(est, ("arbitrary", "arbitrary")),
    )(q, k, v, gb, st, tinv, u, w, do)


def _ya_head(o, z, onw):
    return o * lax.rsqrt(jnp.mean(o * o, -1, keepdims=True) + RMS_EPS) * onw * _silu(z)


def _norm_cdf(x):
    return 0.5 * (1.0 + lax.erf(x * 0.7071067811865476))


def _norm_pdf(x):
    return jnp.exp(-0.5 * x * x) * 0.3989422804014327


def _chunk_causal(shape, di, dj):
    sh = jnp.int32(int(math.log2(SGU_CHUNK)))
    return lax.shift_right_logical(_iota(shape, di), sh) >= lax.shift_right_logical(_iota(shape, dj), sh)


def _ws_masked(ws):
    return jnp.where(_chunk_causal(ws.shape, 1, 2), ws, 0.0)


def _mix_prep(o, proj, onw, sg, sb, ws, bst, wpa, wpb, *, tm=256):
    S = o.shape[0]
    tm = _tile(S, tm, SGU_BLOCK)

    def body(o_ref, z_ref, u_ref, vg_ref, ga_ref, gb_ref, onw_ref, sg_ref, sb_ref, ws_ref, bst_ref, wa_ref, wb_ref,
             ya_ref, yb_ref, phi_ref, pa_ref, pb_ref, m_ref):
        onw = onw_ref[...]
        for h in range(N_HEADS):
            sl = slice(h * D_HEAD, (h + 1) * D_HEAD)
            ya_ref[:, sl] = _ya_head(o_ref[:, sl], z_ref[:, sl].astype(F32), onw).astype(BF)
        u, vg = u_ref[...].astype(F32), vg_ref[...].astype(F32)
        phi_u, phi_v = _norm_cdf(u), _norm_cdf(vg)
        phi_ref[:, :D_MODEL] = phi_u
        phi_ref[:, D_MODEL:] = phi_v
        ua, vl = u * phi_u, _ln(vg * phi_v, sg_ref[...], sb_ref[...])
        wsm = _ws_masked(ws_ref[...])
        bst = bst_ref[...]
        for blk in range(tm // SGU_BLOCK):
            rs = slice(blk * SGU_BLOCK, (blk + 1) * SGU_BLOCK)
            for gi in range(SGU_GROUPS):
                cs = slice(gi * D_HEAD, (gi + 1) * D_HEAD)
                sp = _dot(wsm[gi], vl[rs, cs]) + bst[:, gi:gi + 1]
                yb_ref[rs, cs] = (ua[rs, cs] * sp).astype(BF)
        pa = _dot(ya_ref[...], wa_ref[...]).astype(BF)
        pb = _dot(yb_ref[...], wb_ref[...]).astype(BF)
        pa_ref[...] = pa
        pb_ref[...] = pb
        m_ref[...] = (_sigmoid(ga_ref[...].astype(F32)) * pa.astype(F32)
                      + _sigmoid(gb_ref[...].astype(F32)) * pb.astype(F32)).astype(BF)

    blk = lambda col: pl.BlockSpec((tm, D_MODEL), lambda i: (i, col))
    full = lambda shape: pl.BlockSpec(shape, lambda i: (0,) * len(shape))
    est = 14 * _nbytes((tm, D_MODEL), F32) + 4 * _nbytes((D_MODEL, D_MODEL), BF)
    return pl.pallas_call(
        body, name="mix_prep", grid=(S // tm,),
        in_specs=[blk(0), blk(0), blk(1), blk(2), blk(3), blk(4), full((1, D_HEAD)), full((1, D_MODEL)),
                  full((1, D_MODEL)), full((SGU_GROUPS, SGU_BLOCK, SGU_BLOCK)), full((SGU_BLOCK, LANES)),
                  full((D_MODEL, D_MODEL)), full((D_MODEL, D_MODEL))],
        out_specs=[blk(0), blk(0), pl.BlockSpec((tm, 2 * D_MODEL), lambda i: (i, 0)), blk(0), blk(0), blk(0)],
        out_shape=[jax.ShapeDtypeStruct((S, D_MODEL), BF)] * 2 + [jax.ShapeDtypeStruct((S, 2 * D_MODEL), F32)]
                  + [jax.ShapeDtypeStruct((S, D_MODEL), BF)] * 3,
        compiler_params=_cparams(est, ("parallel",)),
    )(o, proj, proj, proj, proj, proj, onw, sg, sb, ws, bst, wpa, wpb)


def _mix_prep_bwd(o, proj, phi, onw, sg, sb, ws, bst, dpa, wpa, dpb, wpb, dproj, *, tm=256):
    S = o.shape[0]
    tm = _tile(S, tm, SGU_BLOCK)

    def body(o_ref, z_ref, u_ref, vg_ref, phi_ref, onw_ref, sg_ref, sb_ref, ws_ref, bst_ref, dpa_ref, wpa_ref, dpb_ref,
             wpb_ref, dproj_in, do_ref, dzuv_ref, donw_ref, dsg_ref, dsb_ref, dws_ref, dbst_ref, dvl_scr, dua_scr,
             dya_ref, dyb_ref):
        dz_ref, du_ref, dvg_ref = (dzuv_ref.at[:, k * D_MODEL:(k + 1) * D_MODEL] for k in range(3))
        @pl.when(pl.program_id(0) == 0)
        def _():
            for r in (donw_ref, dsg_ref, dsb_ref, dws_ref, dbst_ref):
                r[...] = jnp.zeros_like(r)

        dya_ref[...] = _dot_nt(dpa_ref[...], wpa_ref[...])
        dyb_ref[...] = _dot_nt(dpb_ref[...], wpb_ref[...])

        onw = onw_ref[...]
        donw = jnp.zeros((1, D_HEAD), F32)
        for h in range(N_HEADS):
            sl = slice(h * D_HEAD, (h + 1) * D_HEAD)
            _, vj = jax.vjp(_ya_head, o_ref[:, sl], z_ref[:, sl].astype(F32), onw)
            do_h, dz_h, donw_h = vj(dya_ref[:, sl])
            do_ref[:, sl] = do_h.astype(BF)
            dz_ref[:, sl] = dz_h.astype(BF)
            donw = donw + donw_h
        donw_ref[...] += _bcast_rows(donw)

        u, vg = u_ref[...].astype(F32), vg_ref[...].astype(F32)
        phi_u, phi_v = phi_ref[:, :D_MODEL], phi_ref[:, D_MODEL:]
        ua = u * phi_u
        vl, vj = jax.vjp(_ln, vg * phi_v, sg_ref[...], sb_ref[...])
        wsm = _ws_masked(ws_ref[...])
        bst = bst_ref[...]
        lane = _iota((SGU_BLOCK, LANES), 1)
        dbst = jnp.zeros((SGU_BLOCK, LANES), F32)
        cmask = _chunk_causal((SGU_BLOCK, SGU_BLOCK), 0, 1)
        for gi in range(SGU_GROUPS):
            cs = slice(gi * D_HEAD, (gi + 1) * D_HEAD)
            wg = wsm[gi]
            wgt = jnp.transpose(wg)
            dwg = jnp.zeros((SGU_BLOCK, SGU_BLOCK), F32)
            for blk in range(tm // SGU_BLOCK):
                rs = slice(blk * SGU_BLOCK, (blk + 1) * SGU_BLOCK)
                sp = _dot(wg, vl[rs, cs]) + bst[:, gi:gi + 1]
                dyb = dyb_ref[rs, cs]
                dsp = dyb * ua[rs, cs]
                dua_scr[rs, cs] = dyb * sp
                dvl_scr[rs, cs] = _dot(wgt, dsp)
                dwg = dwg + _dot_nt(dsp, vl[rs, cs])
                dbst = dbst + jnp.where(lane == gi, jnp.sum(dsp, -1, keepdims=True), 0.0)
            dws_ref[gi] += jnp.where(cmask, dwg, 0.0)
        dbst_ref[...] += dbst
        dgv, dsg, dsb = vj(dvl_scr[...])
        du_ref[...] = (dua_scr[...] * (phi_u + u * _norm_pdf(u))).astype(BF)
        dvg_ref[...] = (dgv * (phi_v + vg * _norm_pdf(vg))).astype(BF)
        dsg_ref[...] += _bcast_rows(dsg)
        dsb_ref[...] += _bcast_rows(dsb)

    blk = lambda col: pl.BlockSpec((tm, D_MODEL), lambda i: (i, col))
    full = lambda shape: pl.BlockSpec(shape, lambda i: (0,) * len(shape))
    est = 18 * _nbytes((tm, D_MODEL), F32) + 4 * _nbytes((D_MODEL, D_MODEL), BF)
    outs = pl.pallas_call(
        body, name="mix_prep_bwd", grid=(S // tm,),
        in_specs=[blk(0), blk(0), blk(1), blk(2), pl.BlockSpec((tm, 2 * D_MODEL), lambda i: (i, 0)),
                  full((1, D_HEAD)), full((1, D_MODEL)), full((1, D_MODEL)),
                  full((SGU_GROUPS, SGU_BLOCK, SGU_BLOCK)), full((SGU_BLOCK, LANES)),
                  blk(0), full((D_MODEL, D_MODEL)), blk(0), full((D_MODEL, D_MODEL)),
                  pl.BlockSpec(memory_space=pl.ANY)],
        out_specs=[blk(0), pl.BlockSpec((tm, 3 * D_MODEL), lambda i: (i, 1)),
                   full((SUBLANES, D_HEAD)), full((SUBLANES, D_MODEL)), full((SUBLANES, D_MODEL)),
                   full((SGU_GROUPS, SGU_BLOCK, SGU_BLOCK)), full((SGU_BLOCK, LANES))],
        out_shape=[jax.ShapeDtypeStruct((S, D_MODEL), BF), jax.ShapeDtypeStruct(dproj.shape, BF),
                   jax.ShapeDtypeStruct((SUBLANES, D_HEAD), F32), jax.ShapeDtypeStruct((SUBLANES, D_MODEL), F32),
                   jax.ShapeDtypeStruct((SUBLANES, D_MODEL), F32),
                   jax.ShapeDtypeStruct((SGU_GROUPS, SGU_BLOCK, SGU_BLOCK), F32),
                   jax.ShapeDtypeStruct((SGU_BLOCK, LANES), F32)],
        input_output_aliases={14: 1},
        scratch_shapes=[pltpu.VMEM((tm, D_MODEL), F32)] * 4,
        compiler_params=_cparams(est, ("arbitrary",)),
    )(o, proj, proj, proj, phi, onw, sg, sb, ws, bst, dpa, wpa, dpb, wpb, dproj)
    return outs


def _mm_gate_merge_bwd(dmix, wo, pa, pb, proj, *, tm=512):
    S = pa.shape[0]
    tm = _tile(S, tm, SUBLANES * 2)

    def body(d_ref, w_ref, pa_ref, pb_ref, ga_ref, gb_ref, dpa_ref, dpb_ref, dg_ref):
        dm = _dot_nt(d_ref[...], w_ref[...])
        sa, sb = _sigmoid(ga_ref[...].astype(F32)), _sigmoid(gb_ref[...].astype(F32))
        dpa_ref[...] = (dm * sa).astype(BF)
        dpb_ref[...] = (dm * sb).astype(BF)
        dg_ref[:, :D_MODEL] = (dm * pa_ref[...].astype(F32) * sa * (1.0 - sa)).astype(BF)
        dg_ref[:, D_MODEL:] = (dm * pb_ref[...].astype(F32) * sb * (1.0 - sb)).astype(BF)

    blk = lambda col: pl.BlockSpec((tm, D_MODEL), lambda i: (i, col))
    est = _nbytes((D_MODEL, D_MODEL), BF) + 10 * _nbytes((tm, D_MODEL), F32)
    return pl.pallas_call(
        body, name="mm_gate_merge_bwd", grid=(S // tm,),
        in_specs=[blk(0), pl.BlockSpec((D_MODEL, D_MODEL), lambda i: (0, 0)), blk(0), blk(0), blk(3), blk(4)],
        out_specs=[blk(0), blk(0), pl.BlockSpec((tm, 2 * D_MODEL), lambda i: (i, 3))],
        out_shape=[jax.ShapeDtypeStruct((S, D_MODEL), BF)] * 2 + [jax.ShapeDtypeStruct((S, 8 * D_MODEL), BF)],
        compiler_params=_cparams(est, ("parallel",)),
    )(dmix, wo, pa, pb, proj, proj)


def _mm_swiglu(xb, wgt, wut, *, tm=1024, tn=768):
    S, K = xb.shape
    tm = _tile(S, tm, SUBLANES * 2)
    tn = _tile(FFN_K, tn, LANES)

    def body(x_ref, wg_ref, wu_ref, hg_ref, hu_ref, h_ref):
        x = x_ref[...]
        hg = _dot_nt(x, wg_ref[...]).astype(BF)
        hu = _dot_nt(x, wu_ref[...]).astype(BF)
        hg_ref[...] = hg
        hu_ref[...] = hu
        h_ref[...] = (_silu(hg.astype(F32)) * hu.astype(F32)).astype(BF)

    out = pl.BlockSpec((tm, tn), lambda i, j: (i, j))
    est = _nbytes((tm, K), BF) + 2 * _nbytes((K, tn), BF) + 6 * _nbytes((tm, tn), F32)
    return pl.pallas_call(
        body, name="mm_swiglu", grid=(S // tm, FFN_K // tn),
        in_specs=[pl.BlockSpec((tm, K), lambda i, j: (i, 0)), pl.BlockSpec((tn, K), lambda i, j: (j, 0)),
                  pl.BlockSpec((tn, K), lambda i, j: (j, 0))],
        out_specs=[out] * 3, out_shape=[jax.ShapeDtypeStruct((S, FFN_K), BF)] * 3,
        compiler_params=_cparams(est, ("parallel", "parallel")),
    )(xb, wgt, wut)


def _mm_swiglu_bwd(dffn, wd, hg, hu, *, tm=1024, tn=768):
    S, K = dffn.shape
    tm = _tile(S, tm, SUBLANES * 2)
    tn = _tile(FFN_K, tn, LANES)

    def body(d_ref, w_ref, hg_ref, hu_ref, dhg_ref, dhu_ref):
        dh = _dot_nt(d_ref[...], w_ref[...])
        act, dact = _silu_and_grad(hg_ref[...].astype(F32))
        dhg_ref[...] = (dh * hu_ref[...].astype(F32) * dact).astype(BF)
        dhu_ref[...] = (dh * act).astype(BF)

    out = pl.BlockSpec((tm, tn), lambda i, j: (i, j))
    est = _nbytes((tm, K), dffn.dtype) + _nbytes((tn, K), BF) + 8 * _nbytes((tm, tn), F32)
    return pl.pallas_call(
        body, name="mm_swiglu_bwd", grid=(S // tm, FFN_K // tn),
        in_specs=[pl.BlockSpec((tm, K), lambda i, j: (i, 0)), pl.BlockSpec((tn, K), lambda i, j: (j, 0)), out, out],
        out_specs=[out, out], out_shape=[jax.ShapeDtypeStruct((S, FFN_K), BF)] * 2,
        compiler_params=_cparams(est, ("parallel", "parallel")),
    )(dffn, wd, hg, hu)


def _mm_resid_ln(a, bmat, x, g, b, *, name, tm=512):
    S, K = a.shape
    tm = _tile(S, tm, SUBLANES * 2)

    def body(a_ref, w_ref, x_ref, g_ref, b_ref, pre_ref, y_ref, yb_ref):
        pre = ALPHA * x_ref[...] + _dot(a_ref[...], w_ref[...])
        y = _ln(pre, g_ref[...], b_ref[...])
        pre_ref[...] = pre
        y_ref[...] = y
        yb_ref[...] = y.astype(BF)

    blk = pl.BlockSpec((tm, D_MODEL), lambda i: (i, 0))
    vec = pl.BlockSpec((1, D_MODEL), lambda i: (0, 0))
    est = _nbytes((tm, K), BF) + _nbytes((K, D_MODEL), BF) + 8 * _nbytes((tm, D_MODEL), F32)
    return pl.pallas_call(
        body, name=name, grid=(S // tm,),
        in_specs=[pl.BlockSpec((tm, K), lambda i: (i, 0)), pl.BlockSpec((K, D_MODEL), lambda i: (0, 0)), blk, vec, vec],
        out_specs=[blk, blk, blk],
        out_shape=[jax.ShapeDtypeStruct((S, D_MODEL), F32)] * 2 + [jax.ShapeDtypeStruct((S, D_MODEL), BF)],
        compiler_params=_cparams(est, ("parallel",)),
    )(a, bmat, x, g, b)


def _ln_bwd(pre, g, b, dy, *, tm=512):
    S = pre.shape[0]
    tm = _tile(S, tm, SUBLANES)

    def body(p_ref, g_ref, b_ref, dy_ref, dp_ref, dg_ref, db_ref):
        @pl.when(pl.program_id(0) == 0)
        def _():
            dg_ref[...] = jnp.zeros_like(dg_ref)
            db_ref[...] = jnp.zeros_like(db_ref)

        _, vj = jax.vjp(_ln, p_ref[...], g_ref[...], b_ref[...])
        dp, dg, db = vj(dy_ref[...])
        dp_ref[...] = dp
        dg_ref[...] += _bcast_rows(dg)
        db_ref[...] += _bcast_rows(db)

    blk = pl.BlockSpec((tm, D_MODEL), lambda i: (i, 0))
    vec = pl.BlockSpec((1, D_MODEL), lambda i: (0, 0))
    acc = pl.BlockSpec((SUBLANES, D_MODEL), lambda i: (0, 0))
    return pl.pallas_call(
        body, name="ln_bwd", grid=(S // tm,),
        in_specs=[blk, vec, vec, blk], out_specs=[blk, acc, acc],
        out_shape=[jax.ShapeDtypeStruct((S, D_MODEL), F32)] + [jax.ShapeDtypeStruct((SUBLANES, D_MODEL), F32)] * 2,
        compiler_params=_cparams(10 * _nbytes((tm, D_MODEL), F32), ("arbitrary",)),
    )(pre, g, b, dy)


def _loss_ln_bwd(y, tgt, pre, g, b, *, tm=512):
    S = y.shape[0]
    tm = _tile(S, tm, SUBLANES)

    def body(y_ref, t_ref, p_ref, g_ref, b_ref, dp_ref, dg_ref, db_ref, l_ref):
        @pl.when(pl.program_id(0) == 0)
        def _():
            for r in (dg_ref, db_ref, l_ref):
                r[...] = jnp.zeros_like(r)

        e = y_ref[...] - t_ref[...]
        l_ref[...] += 0.5 * jnp.sum(jnp.mean(e * e, -1, keepdims=True), keepdims=True)
        _, vj = jax.vjp(_ln, p_ref[...], g_ref[...], b_ref[...])
        dp, dg, db = vj(e * (1.0 / D_MODEL))
        dp_ref[...] = dp
        dg_ref[...] += _bcast_rows(dg)
        db_ref[...] += _bcast_rows(db)

    blk = pl.BlockSpec((tm, D_MODEL), lambda i: (i, 0))
    vec = pl.BlockSpec((1, D_MODEL), lambda i: (0, 0))
    acc = pl.BlockSpec((SUBLANES, D_MODEL), lambda i: (0, 0))
    return pl.pallas_call(
        body, name="loss_ln_bwd", grid=(S // tm,),
        in_specs=[blk, blk, blk, vec, vec], out_specs=[blk, acc, acc, pl.BlockSpec((SUBLANES, LANES), lambda i: (0, 0))],
        out_shape=[jax.ShapeDtypeStruct((S, D_MODEL), F32)] + [jax.ShapeDtypeStruct((SUBLANES, D_MODEL), F32)] * 2
                  + [jax.ShapeDtypeStruct((SUBLANES, LANES), F32)],
        compiler_params=_cparams(12 * _nbytes((tm, D_MODEL), F32), ("arbitrary",)),
    )(y, tgt, pre, g, b)


def _layer_fwd(x, xb, w, late):
    pq = _mm(xb, w["win"], mode="nn", name="mm_in_qkv", tm=1024, tn=1024, cols=(0, 3 * D_MODEL))
    proj = _mm(xb, w["win"], mode="nn", name="mm_in_rest", tm=1024, tn=1024, cols=(3 * D_MODEL, 5 * D_MODEL), out_dtype=BF)
    qn, kn, vv, gb, conv_out, ba = _qkv_prep(pq, xb, w["wba"], w["convw"], w["arow"], w["dtrow"])
    o, st, tinv, wy_u, wy_w = _delta_fwd(qn, kn, vv, gb)
    w = {**w, **late(o)}
    ya, yb, phi, pa, pb, m = _mix_prep(o, proj, w["onw"], w["sg"], w["sb"], w["ws"], w["bst"], w["wpa"], w["wpb"])
    pre1, x1, x1b = _mm_resid_ln(m, w["wo"], x, w["ln1g"], w["ln1b"], name="mm_out_ln")
    hg, hu, h = _mm_swiglu(x1b, w["wgt"], w["wut"])
    pre2, x2, x2b = _mm_resid_ln(h, w["wd"], x1, w["ln2g"], w["ln2b"], name="mm_down_ln")
    saved = dict(xb=xb, pq=pq, conv_out=conv_out, proj=proj, phi=phi, ba=ba, qn=qn, kn=kn, vv=vv, gb=gb, o=o, st=st, tinv=tinv, wy_u=wy_u, wy_w=wy_w,
                 ya=ya, yb=yb,
                 pa=pa, pb=pb, m=m, pre1=pre1, x1b=x1b, hg=hg, hu=hu, h=h, pre2=pre2)
    return x2, x2b, saved, w


def _layer_bwd(dpre2, ln2_grads, w, s, on_part=None):
    g = {}
    started = lambda part: on_part(part, g) if on_part is not None else None
    after = lambda v, token: v if token is None else v + token.astype(v.dtype)
    g["ln2g"], g["ln2b"] = ln2_grads
    dhg, dhu = _mm_swiglu_bwd(dpre2, w["wd"], s["hg"], s["hu"])
    g["wd"] = _mm(s["h"], dpre2, mode="tn", name="mm_tn_down", tm=1536, tk=1024, out_dtype=BF)
    dx1 = _mm(dhg, w["wgt"], mode="nn", name="mm_nn_gu", pair=(dhu, w["wut"]), add=dpre2, add_scale=ALPHA, tm=1024, tk=1536)
    g["wgt"] = _mm(dhg, s["x1b"], mode="tn", name="mm_tn_gu", tm=1536, tn=1024, tk=2048, out_dtype=BF)
    g["wut"] = _mm(dhu, s["x1b"], mode="tn", name="mm_tn_gu", tm=1536, tn=1024, tk=2048, out_dtype=BF)
    dpre1, g["ln1g"], g["ln1b"] = _ln_bwd(s["pre1"], w["ln1g"], w["ln1b"], dx1)
    g["wo"] = _mm(s["m"], dpre1, mode="tn", name="mm_tn_sq", tm=1024, tk=1024, out_dtype=BF)
    dpa, dpb, dproj = _mm_gate_merge_bwd(dpre1, w["wo"], s["pa"], s["pb"], s["proj"])
    g["wpa"] = _mm(s["ya"], dpa, mode="tn", name="mm_tn_sq", tm=1024, tk=1024, out_dtype=BF)
    g["wpb"] = _mm(s["yb"], dpb, mode="tn", name="mm_tn_sq", tm=1024, tk=1024, out_dtype=BF)
    do, dproj, g["onw"], g["sg"], g["sb"], g["ws"], g["bst"] = _mix_prep_bwd(
        s["o"], s["proj"], s["phi"], after(w["onw"], started("late")), w["sg"], w["sb"], w["ws"], w["bst"],
        dpa, w["wpa"], dpb, w["wpb"], dproj)
    dqn, dkn, dvv, dgb = _delta_bwd(s["qn"], s["kn"], s["vv"], s["gb"], s["st"], s["tinv"], s["wy_u"], s["wy_w"], do)
    dc, dba, g["convw"], g["arow"], g["dtrow"], dwba = _qkv_prep_bwd(
        s["pq"], s["conv_out"], s["ba"], w["arow"], w["dtrow"], dqn, dkn, dvv, dgb, s["xb"])
    g["wba"] = dwba.astype(BF)
    dproj = _conv_bwd(dc, w["convw"], dproj)
    g["win"] = _mm(s["xb"], dproj, mode="tn", name="mm_tn_in", tm=1024, tn=1024, tk=2048, out_dtype=BF)
    dx = _mm(dproj, w["win"], mode="nt", name="mm_nt_in", add=dpre1, add_scale=ALPHA, tm=1024, tk=2048,
             thin=(dba, after(w["wba"], started("early"))))
    return dx, g


def _local_step(x, xb, tgt, layers, on_grads=None):
    saved, weights = [], []
    for layer in layers:
        x, xb, s, w = _layer_fwd(x, xb, *layer(x))
        saved.append(s)
        weights.append(w)
    last = len(layers) - 1
    dpre2, dg, db, lacc = _loss_ln_bwd(x, tgt, saved[last]["pre2"], weights[last]["ln2g"], weights[last]["ln2b"])
    grads = [None] * len(layers)
    for l in reversed(range(len(layers))):
        on_part = functools.partial(on_grads, l) if on_grads is not None else None
        dx, grads[l] = _layer_bwd(dpre2, (dg, db), weights[l], saved[l], on_part)
        if l > 0:
            dpre2, dg, db = _ln_bwd(saved[l - 1]["pre2"], weights[l - 1]["ln2g"], weights[l - 1]["ln2b"], dx)
    return lacc[0, 0], dx, grads


_QKVZ = 4 * D_MODEL
_BA = 2 * N_HEADS


WEIGHT_NAMES = ("w_in", "conv_w", "a_log", "dt_bias", "o_norm_w", "sgu_ln_g", "sgu_ln_b", "w_s", "b_s", "w_pa", "w_pb",
                "w_o", "ln1_g", "ln1_b", "w_ffn_gate", "w_ffn_up", "w_ffn_down", "ln2_g", "ln2_b")
WIRE = ("w_in", "w_ffn_gate", "w_ffn_up", "w_ffn_down", "w_pa", "w_pb", "w_o", "conv_w")
SMALL = (("a_log", N_HEADS), ("dt_bias", N_HEADS), ("o_norm_w", D_HEAD), ("sgu_ln_g", D_MODEL), ("sgu_ln_b", D_MODEL),
         ("w_s", SGU_GROUPS * SGU_BLOCK * SGU_BLOCK), ("b_s", SGU_GROUPS * SGU_BLOCK),
         ("ln1_g", D_MODEL), ("ln1_b", D_MODEL), ("ln2_g", D_MODEL), ("ln2_b", D_MODEL))
SMALL_ROWS = -(-sum(n for _, n in SMALL) // (LANES * SUBLANES)) * SUBLANES
N_MAIN_TILES = (N_IN - _BA) // D_MODEL
ADAM_TILES = dict(w_in=(128, "adamw_in"), w_ffn_gate=(32, "adamw_ffn_rows"), w_ffn_up=(32, "adamw_ffn_rows"),
                  w_ffn_down=(32, "adamw_ffn_rows"), w_pa=(128, "adamw_sq"), w_pb=(128, "adamw_sq"), w_o=(128, "adamw_sq"),
                  conv_w=(CONV_K, "adamw_conv"))


def _pad_to(a, axis, size):
    pads = [(0, 0)] * a.ndim
    pads[axis] = (0, size - a.shape[axis])
    return jnp.pad(a, pads)


def _t(a):
    return jnp.swapaxes(a, 1, 2)


def _wire_blocks(p):
    return dict(
        w_in=_pad_to(p["w_in"].astype(BF), 2, IN_PAD),
        w_ffn_gate=_pad_to(_t(p["w_ffn_gate"]).astype(BF), 1, FFN_PAD), w_ffn_up=_pad_to(_t(p["w_ffn_up"]).astype(BF), 1, FFN_PAD),
        w_ffn_down=_pad_to(p["w_ffn_down"].astype(BF), 1, FFN_PAD),
        w_pa=p["w_pa"].astype(BF), w_pb=p["w_pb"].astype(BF), w_o=p["w_o"].astype(BF),
        conv_w=_pad_to(p["conv_w"], 1, SUBLANES),
    )


def _by_columns(blocks):
    n, r, c = blocks.shape
    return jnp.transpose(blocks, (1, 0, 2)).reshape(r, n * c)


def _to_slots(full, c):
    r = full.shape[0]
    return jnp.transpose(full.reshape(r, N_DEV, c), (1, 0, 2))


def _lane_row(v, at):
    return jnp.pad(v[None], ((0, 0), (at, LANES - at - v.shape[0])))


TRANSPOSED = ("w_ffn_gate", "w_ffn_up")
EARLY = ("w_in", "conv_w")
LATE = ("w_pa", "w_pb", "w_o", "w_ffn_gate", "w_ffn_up", "w_ffn_down")


def _early_weights(stacks, p, l):
    return dict(
        win=_perm_in(stacks["w_in"], D_MODEL, N_MAIN_TILES), wba=_perm_in(stacks["w_in"], LANES, 1),
        convw=_by_columns(stacks["conv_w"][:, :CONV_K]),
        arow=_lane_row(p["a_log"][l], N_HEADS), dtrow=_lane_row(p["dt_bias"][l], N_HEADS),
        onw=p["o_norm_w"][l][None], sg=p["sgu_ln_g"][l][None], sb=p["sgu_ln_b"][l][None],
        ws=p["w_s"][l], bst=_pad_to(p["b_s"][l].T, 1, LANES),
        ln1g=p["ln1_g"][l][None], ln1b=p["ln1_b"][l][None], ln2g=p["ln2_g"][l][None], ln2b=p["ln2_b"][l][None],
    )


def _late_weights(stacks):
    return dict(
        wpa=stacks["w_pa"].reshape(D_MODEL, D_MODEL), wpb=stacks["w_pb"].reshape(D_MODEL, D_MODEL),
        wo=stacks["w_o"].reshape(D_MODEL, D_MODEL),
        wgt=stacks["w_ffn_gate"].reshape(FFN_K, D_MODEL), wut=stacks["w_ffn_up"].reshape(FFN_K, D_MODEL),
        wd=stacks["w_ffn_down"].reshape(FFN_K, D_MODEL),
    )


def _small_pack(parts):
    flat = jnp.concatenate([parts[n].reshape(-1) for n, _ in SMALL])
    return _pad_to(flat, 0, SMALL_ROWS * LANES).reshape(SMALL_ROWS, LANES)


def _small_unpack(rows, like):
    flat, out, off = rows.reshape(-1), {}, 0
    for n, size in SMALL:
        out[n] = flat[off:off + size].reshape(like[n].shape[1:])
        off += size
    return out


def _late_slots(g):
    slots = dict(
        w_ffn_gate=g["wgt"].reshape(N_DEV, FFN_PAD, D_MODEL), w_ffn_up=g["wut"].reshape(N_DEV, FFN_PAD, D_MODEL),
        w_ffn_down=g["wd"].reshape(N_DEV, FFN_PAD, D_MODEL),
        w_pa=g["wpa"].reshape(N_DEV, D_MODEL // N_DEV, D_MODEL), w_pb=g["wpb"].reshape(N_DEV, D_MODEL // N_DEV, D_MODEL),
        w_o=g["wo"].reshape(N_DEV, D_MODEL // N_DEV, D_MODEL),
    )
    return [slots[n] for n in LATE]


def _early_slots(g):
    slots = [_perm_out(g["win"], g["wba"]), _pad_to(_to_slots(g["convw"][:CONV_K], 3 * D_MODEL // N_DEV), 1, SUBLANES)]
    small = _small_pack(dict(
        a_log=g["arow"][0, N_HEADS:2 * N_HEADS], dt_bias=g["dtrow"][0, N_HEADS:2 * N_HEADS], o_norm_w=g["onw"][0],
        sgu_ln_g=g["sg"][0], sgu_ln_b=g["sb"][0], w_s=g["ws"], b_s=g["bst"][:, :SGU_GROUPS].T,
        ln1_g=g["ln1g"][0], ln1_b=g["ln1b"][0], ln2_g=g["ln2g"][0], ln2_b=g["ln2b"][0]))
    return slots, small


def _in_tile_start(j, tile_w):
    if tile_w == LANES:
        return jnp.int32(_QKVZ)
    return j * D_MODEL + jnp.where(j >= _QKVZ // D_MODEL, _BA, 0)


def _select(rows_iota, cols_iota, dev, start, valid):
    hit = (rows_iota + (dev * IN_BLOCK - start) == cols_iota) & (rows_iota < IN_BLOCK) & (cols_iota < valid)
    return jnp.where(hit, 1.0, 0.0).astype(BF)


def _perm_in(stack, tile_w, n_tiles):
    valid = _BA if tile_w == LANES else tile_w

    def first_dev(j):
        return lax.div(_in_tile_start(j, tile_w), jnp.int32(IN_BLOCK))

    def body(w_ref, o_ref, acc_ref):
        j, k = pl.program_id(0), pl.program_id(1)
        sel = _select(_iota((IN_PAD, tile_w), 0), _iota((IN_PAD, tile_w), 1), first_dev(j) + k,
                      _in_tile_start(j, tile_w), valid)
        part = jnp.dot(w_ref[0], sel, preferred_element_type=F32)

        @pl.when(k == 0)
        def _():
            acc_ref[...] = part

        @pl.when(k == 1)
        def _():
            o_ref[...] = (acc_ref[...] + part).astype(BF)

    est = _nbytes((D_MODEL, IN_PAD), BF) + 3 * _nbytes((D_MODEL, tile_w), F32) + 2 * _nbytes((IN_PAD, tile_w), F32)
    return pl.pallas_call(
        body, name="perm_in" if tile_w != LANES else "perm_in_ba", grid=(n_tiles, 2),
        in_specs=[pl.BlockSpec((1, D_MODEL, IN_PAD), lambda j, k: (jnp.minimum(first_dev(j) + k, N_DEV - 1), 0, 0))],
        out_specs=pl.BlockSpec((D_MODEL, tile_w), lambda j, k: (0, j)),
        out_shape=jax.ShapeDtypeStruct((D_MODEL, n_tiles * tile_w), BF),
        scratch_shapes=[pltpu.VMEM((D_MODEL, tile_w), F32)],
        compiler_params=_cparams(est, ("parallel", "arbitrary")),
    )(_in_hbm(stack))


def _perm_out(dmain, dba):
    def tile(d, s):
        c0 = d * IN_BLOCK
        first = lax.div(c0 - jnp.where(c0 < _QKVZ, 0, jnp.minimum(c0 - _QKVZ, _BA)), jnp.int32(D_MODEL))
        return jnp.minimum(first + jnp.minimum(s, 1), N_MAIN_TILES - 1)

    def body(dm_ref, db_ref, o_ref, acc_ref):
        d, s = pl.program_id(0), pl.program_id(1)

        @pl.when(s == 0)
        def _():
            acc_ref[...] = jnp.zeros_like(acc_ref)

        start = _in_tile_start(tile(d, s), D_MODEL)
        overlaps = (start < (d + 1) * IN_BLOCK) & (d * IN_BLOCK < start + D_MODEL)

        @pl.when((s < 2) & overlaps)
        def _():
            sel = _select(_iota((D_MODEL, IN_PAD), 1), _iota((D_MODEL, IN_PAD), 0), d, start, D_MODEL)
            acc_ref[...] += jnp.dot(dm_ref[...], sel, preferred_element_type=F32)

        @pl.when(s == 2)
        def _():
            sel = _select(_iota((LANES, IN_PAD), 1), _iota((LANES, IN_PAD), 0), d, jnp.int32(_QKVZ), _BA)
            o_ref[0] = (acc_ref[...] + jnp.dot(db_ref[...], sel, preferred_element_type=F32)).astype(BF)

    est = 2 * _nbytes((D_MODEL, D_MODEL), BF) + 4 * _nbytes((D_MODEL, IN_PAD), F32)
    return pl.pallas_call(
        body, name="perm_out", grid=(N_DEV, 3),
        in_specs=[pl.BlockSpec((D_MODEL, D_MODEL), lambda d, s: (0, tile(d, s))),
                  pl.BlockSpec((D_MODEL, LANES), lambda d, s: (0, 0))],
        out_specs=pl.BlockSpec((1, D_MODEL, IN_PAD), lambda d, t: (d, 0, 0)),
        out_shape=jax.ShapeDtypeStruct((N_DEV, D_MODEL, IN_PAD), BF),
        scratch_shapes=[pltpu.VMEM((D_MODEL, IN_PAD), F32)],
        compiler_params=_cparams(est, ("parallel", "arbitrary")),
    )(dmain, dba)


def _mesh_place():
    x, y, c = (lax.axis_index(a) for a in MESH_AXES)
    return x, y, c


def _slot(x, y, c):
    return 4 * x + 2 * y + c


def _peer(place, j):
    x, y, c = place
    return (1 - x if j & 4 else x, 1 - y if j & 2 else y, 1 - c if j & 1 else c)


_HBM = pl.BlockSpec(memory_space=pltpu.HBM)
_SEM = pl.BlockSpec(memory_space=pltpu.SEMAPHORE)
_EFFECT = pltpu.SideEffectType.DATAFLOW_SIDE_EFFECTING


def _remote_copy(src_ref, land_ref, slot, per_slot, pslot, sems, u, j, peer):
    return pltpu.make_async_remote_copy(
        src_ref=src_ref.at[pslot] if per_slot else src_ref, dst_ref=land_ref.at[slot],
        send_sem=sems[0].at[u * (N_DEV - 1) + j - 1], recv_sem=sems[1].at[u * (N_DEV - 1) + j - 1],
        device_id=peer, device_id_type=pl.DeviceIdType.MESH)


def _own_copy(src_ref, land_ref, me, per_slot, sems, u):
    return pltpu.make_async_copy(src_ref.at[me] if per_slot else src_ref, land_ref.at[me], sems[2].at[u])


def _exchange_start(name, srcs, per_slot):
    n = len(srcs)
    lands = [jax.ShapeDtypeStruct(s.shape if p else (N_DEV,) + s.shape, s.dtype) for s, p in zip(srcs, per_slot)]

    def body(*refs):
        src_refs, sems, land_refs, token = refs[:n], refs[n:n + 3], refs[2 * n + 3:3 * n + 3], refs[-1]
        place = _mesh_place()
        me = _slot(*place)
        for u in range(n):
            _own_copy(src_refs[u], land_refs[u], me, per_slot[u], sems, u).start()
            for j in range(1, N_DEV):
                peer = _peer(place, j)
                _remote_copy(src_refs[u], land_refs[u], me, per_slot[u], _slot(*peer), sems, u, j, peer).start()
        token[...] = jnp.zeros_like(token)

    hbm = lambda a: pltpu.HBM(a.shape, a.dtype)
    sem = pltpu.SemaphoreType.DMA((n * (N_DEV - 1),))
    outs = pl.pallas_call(
        body, name=name,
        out_shape=(sem, sem, pltpu.SemaphoreType.DMA((n,)), *[hbm(a) for a in srcs], *[hbm(a) for a in lands],
                   jax.ShapeDtypeStruct((SUBLANES, LANES), F32)),
        in_specs=[_HBM] * n, out_specs=(_SEM, _SEM, _SEM, *[_HBM] * (2 * n), pl.BlockSpec(memory_space=pltpu.VMEM)),
        input_output_aliases={i: 3 + i for i in range(n)},
        compiler_params=pltpu.CompilerParams(has_side_effects=_EFFECT),
    )(*[pltpu.with_memory_space_constraint(a, pltpu.HBM) for a in srcs])
    return tuple(outs[:3]), list(outs[3:3 + n]), list(outs[3 + n:3 + 2 * n]), outs[-1]


def _exchange_wait(name, sems, srcs, lands, units, per_slot, after):
    m = len(units)
    after = list(after) if isinstance(after, (list, tuple)) else [after]

    def body(*refs):
        src_refs, land_refs, sem_refs = refs[:m], refs[m:2 * m], refs[2 * m:2 * m + 3]
        place = _mesh_place()
        me = _slot(*place)
        for i, u in enumerate(units):
            _own_copy(src_refs[i], land_refs[i], me, per_slot[u], sem_refs, u).wait()
            for j in range(1, N_DEV):
                peer = _peer(place, j)
                pslot = _slot(*peer)
                cp = _remote_copy(src_refs[i], land_refs[i], pslot, per_slot[u], pslot, sem_refs, u, j, peer)
                cp.wait_send()
                cp.wait_recv()

    hbm = lambda a: pltpu.HBM(a.shape, a.dtype)
    outs = pl.pallas_call(
        body, name=name, out_shape=tuple(hbm(a) for a in list(srcs) + list(lands)),
        in_specs=[_HBM] * (2 * m) + [_SEM] * 3 + [pl.BlockSpec(memory_space=pl.ANY)] * len(after),
        out_specs=tuple([_HBM] * (2 * m)),
        input_output_aliases={i: i for i in range(2 * m)},
        compiler_params=pltpu.CompilerParams(has_side_effects=_EFFECT),
    )(*srcs, *lands, *sems, *after)
    return list(outs[m:])


def _adam_update(g, w, m, v):
    m = ADAM_B1 * m + (1.0 - ADAM_B1) * g
    v = ADAM_B2 * v + (1.0 - ADAM_B2) * jnp.square(g)
    m_hat = m / (1.0 - ADAM_B1 ** ADAM_STEP)
    v_hat = v / (1.0 - ADAM_B2 ** ADAM_STEP)
    return -ADAM_LR * (m_hat / (jnp.sqrt(v_hat) + ADAM_EPS) + ADAM_WD * w), m, v


def _adamw(recvs, w, m, v, *, tr, name):
    L, R, C = w.shape
    rp = max(tr, SUBLANES * (4 // jnp.dtype(recvs[0].dtype).itemsize))
    Cp = recvs[0].shape[2]

    def body(*refs):
        r_refs, (w_ref, m_ref, v_ref, g_ref, d_ref, nm_ref, nv_ref) = refs[:L], refs[L:]
        for l in range(L):
            @pl.when(pl.program_id(0) == l)
            def _(r_ref=r_refs[l]):
                g = r_ref[0, :tr, :C].astype(F32)
                for s in range(1, N_DEV):
                    g = g + r_ref[s, :tr, :C].astype(F32)
                d, nm, nv = _adam_update(g, w_ref[0], m_ref[0], v_ref[0])
                g_ref[0], d_ref[0], nm_ref[0], nv_ref[0] = g, d, nm, nv

    blk = pl.BlockSpec((1, tr, C), lambda l, i: (l, i, 0))
    r_specs = [pl.BlockSpec((N_DEV, rp, Cp), lambda l, i, k=k: (0, jnp.where(l == k, i, 0), 0)) for k in range(L)]
    est = 2 * _nbytes((N_DEV, rp, Cp), recvs[0].dtype) + 8 * _nbytes((tr, Cp), F32)
    return pl.pallas_call(
        body, name=name, grid=(L, R // tr),
        in_specs=r_specs + [blk] * 3, out_specs=[blk] * 4,
        out_shape=[jax.ShapeDtypeStruct((L, R, C), F32)] * 4,
        compiler_params=_cparams(est, ("arbitrary", "arbitrary")),
    )(*recvs, w, m, v)


def _adamw_small(recv, w, m, v):
    def body(r_ref, w_ref, m_ref, v_ref, g_ref, d_ref, nm_ref, nv_ref):
        g = r_ref[0]
        for s in range(1, N_DEV):
            g = g + r_ref[s]
        g_ref[...] = g
        d_ref[...], nm_ref[...], nv_ref[...] = _adam_update(g, w_ref[...], m_ref[...], v_ref[...])

    vm = pl.BlockSpec(memory_space=pltpu.VMEM)
    return pl.pallas_call(
        body, name="adamw_small", in_specs=[vm] * 4, out_specs=[vm] * 4,
        out_shape=[jax.ShapeDtypeStruct((SMALL_ROWS, LANES), F32)] * 4,
        compiler_params=_cparams(20 * _nbytes((SMALL_ROWS, LANES), F32)),
    )(recv, w, m, v)


def kernel(x, w_in, conv_w, a_log, dt_bias, o_norm_w, sgu_ln_g, sgu_ln_b, w_s, b_s, w_pa, w_pb, w_o, ln1_g, ln1_b, w_ffn_gate, w_ffn_up, w_ffn_down, ln2_g, ln2_b, loss_target, m_w_in, m_conv_w, m_a_log, m_dt_bias, m_o_norm_w, m_sgu_ln_g, m_sgu_ln_b, m_w_s, m_b_s, m_w_pa, m_w_pb, m_w_o, m_ln1_g, m_ln1_b, m_w_ffn_gate, m_w_ffn_up, m_w_ffn_down, m_ln2_g, m_ln2_b, v_w_in, v_conv_w, v_a_log, v_dt_bias, v_o_norm_w, v_sgu_ln_g, v_sgu_ln_b, v_w_s, v_b_s, v_w_pa, v_w_pb, v_w_o, v_ln1_g, v_ln1_b, v_w_ffn_gate, v_w_ffn_up, v_w_ffn_down, v_ln2_g, v_ln2_b):
    given = dict(locals())
    P = {n: given[n] for n in WEIGHT_NAMES}
    M = {n: given["m_" + n] for n in WEIGHT_NAMES}
    V = {n: given["v_" + n] for n in WEIGHT_NAMES}

    wire = _wire_blocks(P)
    units = [(n, l) for l in range(DEPTH) for n in EARLY + LATE]
    whole = [False] * len(units)
    g_sems, g_srcs, g_lands, g_token = _exchange_start("gather_start", [wire[n][l] for n, l in units], whole)

    one = 1.0 + g_token[0, 0]
    xb = (x[0] * one).astype(BF)
    small_in = [[_small_pack({n: T[n][l] * one for n, _ in SMALL}) for T in (P, M, V)] for l in range(DEPTH)]
    adam_in = {n: (P[n], M[n], V[n]) for n in WIRE}
    adam_in["w_in"], _ = lax.optimization_barrier((adam_in["w_in"], g_token))
    prepared = [xb, *[a for packs in small_in for a in packs], *adam_in["w_in"]]

    def gathered(name, names, l, after):
        idx = [units.index((n, l)) for n in names]
        got = _exchange_wait(name, g_sems, [g_srcs[i] for i in idx], [g_lands[i] for i in idx], idx, whole, after)
        return dict(zip(names, got))

    def layer(l):
        def weights(x_in):
            after = prepared if l == 0 else x_in
            early = _early_weights(gathered(f"gather_wait_early{l}", EARLY, l, after), P, l)
            return early, lambda ya: _late_weights(gathered(f"gather_wait_late{l}", LATE, l, ya))
        return weights

    pending = {}

    def on_grads(l, part, g):
        if part == "late":
            srcs, names = _late_slots(g), LATE
            per_slot = [True] * len(srcs)
        else:
            slots, small = _early_slots(g)
            srcs, names = slots + [small], EARLY + ("small",)
            per_slot = [True] * len(slots) + [False]
        sems, s_thru, l_thru, token = _exchange_start(f"exchange_start_{part}{l}", srcs, per_slot)
        pending[l, part] = (names, sems, s_thru, l_thru, per_slot)
        return token[0, 0]

    loss_local, dx, _ = _local_step(x[0], xb, loss_target[0], [layer(l) for l in range(DEPTH)], on_grads)
    loss = lax.psum(loss_local, MESH_AXES)

    recv = [{} for _ in range(DEPTH)]

    def received(l, part, after):
        names, sems, s_thru, l_thru, per_slot = pending[l, part]
        got = _exchange_wait(f"exchange_wait_{part}{l}", sems, s_thru, l_thru, list(range(len(s_thru))), per_slot, after)
        recv[l].update(zip(names, got))

    out = {}

    def adamw(names):
        for n in names:
            tr, name = ADAM_TILES[n]
            view = _t if n in TRANSPOSED else (lambda a: a)
            res = _adamw([recv[l][n] for l in range(DEPTH)], *[view(a) for a in adam_in[n]], tr=tr, name=name)
            out[n] = [view(r) for r in res]

    for l in reversed(range(DEPTH)):
        received(l, "late", dx)
    adamw(LATE)
    for l in reversed(range(DEPTH)):
        received(l, "early", out[LATE[-1]][0])
    adamw(EARLY)
    small = [_adamw_small(recv[l]["small"], *small_in[l]) for l in range(DEPTH)]
    for n, _ in SMALL:
        out[n] = [jnp.stack([_small_unpack(small[l][i], P)[n] for l in range(DEPTH)]) for i in range(4)]
    return (loss, dx[None], *[out[n][i] for i in range(4) for n in WEIGHT_NAMES])
```

```python
import functools
import math

import jax
import jax.numpy as jnp
from jax import lax
from jax.experimental import pallas as pl
from jax.experimental.pallas import tpu as pltpu

F32 = jnp.float32
BF = jnp.bfloat16
HIGHEST = lax.Precision.HIGHEST

D_MODEL = 1024
DEPTH = 2
N_HEADS = 8
D_HEAD = 128
CONV_K = 4
SGU_BLOCK = 128
SGU_GROUPS = 8
SGU_CHUNK = 64
FFN_HIDDEN = 2816
N_IN = 8208
N_DEV = 8
IN_BLOCK, IN_PAD = N_IN // N_DEV, 1152
FFN_BLOCK, FFN_PAD = FFN_HIDDEN // N_DEV, 384
FFN_K = N_DEV * FFN_PAD
ALPHA = (2 * DEPTH) ** 0.25
LN_EPS = 1e-5
RMS_EPS = 1e-6
ADAM_LR, ADAM_B1, ADAM_B2, ADAM_EPS, ADAM_WD, ADAM_STEP = 0.001, 0.9, 0.999, 1e-08, 0.01, 10

MESH_AXES = ("x", "y", "c")
DELTA_CHUNK = 128
DELTA_HEADS_PER_STEP = 8
LANES = 128
SUBLANES = 8
VMEM_BYTES = 64 * 1024 * 1024
HALO = SUBLANES
HALO_BF = 2 * SUBLANES


def _cparams(est_bytes, dims=None):
    limit = int(min(max(2 * est_bytes + (8 << 20), 32 << 20), VMEM_BYTES - (6 << 20)))
    kw = dict(vmem_limit_bytes=limit)
    if dims is not None:
        kw["dimension_semantics"] = dims
    return pltpu.CompilerParams(**kw)


def _nbytes(shape, dtype):
    return math.prod(shape) * jnp.dtype(dtype).itemsize


def _dims(kind, ndim):
    lhs, rhs = {"nn": (1, 0), "nt": (1, 1), "tn": (0, 0)}[kind]
    b = ndim - 2
    return (((lhs + b,), (rhs + b,)), (tuple(range(b)), tuple(range(b))))


def _mxu(a, b, kind):
    return lax.dot_general(a, b, _dims(kind, a.ndim), preferred_element_type=F32)


def _dot(a, b):
    return _mxu(a.astype(BF), b.astype(BF), "nn")


def _dot_nt(a, b):
    return _mxu(a.astype(BF), b.astype(BF), "nt")


def _dot_tn(a, b):
    return _mxu(a.astype(BF), b.astype(BF), "tn")


def _split(a):
    hi = a.astype(BF)
    return hi, (a - hi.astype(F32)).astype(BF)


def _dot3(a, b, kind):
    (ah, al), (bh, bl) = _split(a), _split(b)
    return _mxu(ah, bh, kind) + (_mxu(ah, bl, kind) + _mxu(al, bh, kind))


def _dotf(a, b):
    return _dot3(a, b, "nn")


def _dotf_nt(a, b):
    return _dot3(a, b, "nt")


def _dot01(sel, x, kind="nn"):
    s = jnp.broadcast_to(sel.astype(BF), x.shape[:-2] + sel.shape)
    h1 = x.astype(BF)
    r1 = x - h1.astype(F32)
    h2 = r1.astype(BF)
    h3 = (r1 - h2.astype(F32)).astype(BF)
    return _mxu(s, h1, kind) + (_mxu(s, h2, kind) + _mxu(s, h3, kind))


def _sigmoid(x):
    return 0.5 * jnp.tanh(0.5 * x) + 0.5


def _silu(x):
    return x * _sigmoid(x)


def _silu_and_grad(x):
    s = _sigmoid(x)
    return x * s, s * (1.0 + x * (1.0 - s))


def _softplus(x):
    return jnp.maximum(x, 0.0) + jnp.log1p(jnp.exp(-jnp.abs(x)))


def _ln(x, g, b):
    mu = jnp.mean(x, -1, keepdims=True)
    xc = x - mu
    var = jnp.mean(xc * xc, -1, keepdims=True)
    return xc * lax.rsqrt(var + LN_EPS) * g + b


def _iota(shape, dim):
    return lax.broadcasted_iota(jnp.int32, shape, dim)


def _tile(n, pref, align):
    if n <= pref:
        return n
    t = (pref // align) * align
    while t >= align:
        if n % t == 0:
            return t
        t -= align
    raise ValueError(f"no tile for {n} (pref {pref}, align {align})")


def _bcast_rows(v, rows=SUBLANES):
    return jnp.broadcast_to(v, (rows, v.shape[-1]))


def _in_hbm(t):
    return pltpu.with_memory_space_constraint(t, pltpu.HBM)


def _mm(a, b, *, mode, name, out_dtype=F32, add=None, add_scale=1.0, tm=512, tn=1024, tk=1024, cols=None, pair=None,
        thin=None):
    if mode == "nn":
        (M, K), N = a.shape, b.shape[1]
    elif mode == "nt":
        (M, K), N = a.shape, b.shape[0]
    else:
        (K, M), N = a.shape, b.shape[1]
    col0 = 0
    if cols is not None:
        col0, N = cols
    tm = _tile(M, tm, LANES if mode == "tn" else SUBLANES * 2)
    tn = _tile(N, tn, LANES)
    tk = _tile(K, tk, LANES)
    nk = K // tk
    j0 = col0 // tn
    if mode == "nn":
        a_spec = pl.BlockSpec((tm, tk), lambda i, j, k: (i, k))
        b_spec = pl.BlockSpec((tk, tn), lambda i, j, k: (k, j + j0))
        dot = _dot
    elif mode == "nt":
        a_spec = pl.BlockSpec((tm, tk), lambda i, j, k: (i, k))
        b_spec = pl.BlockSpec((tn, tk), lambda i, j, k: (j, k))
        dot = _dot_nt
    else:
        a_spec = pl.BlockSpec((tk, tm), lambda i, j, k: (k, i))
        b_spec = pl.BlockSpec((tk, tn), lambda i, j, k: (k, j))
        dot = _dot_tn
    o_spec = pl.BlockSpec((tm, tn), lambda i, j, k: (i, j))
    has_add = add is not None
    assert thin is None or mode == "nt"

    n_ab = 2 if pair is None else 4

    def body(*refs):
        ab, (o_ref, acc_ref) = refs[:n_ab], refs[-2:]
        add_ref = refs[n_ab] if has_add else None
        thin_refs = refs[n_ab + has_add:-2]
        k = pl.program_id(2)
        part = dot(ab[0][...], ab[1][...])
        if pair is not None:
            part = part + dot(ab[2][...], ab[3][...])

        def finish(total):
            if has_add:
                total = total + add_scale * add_ref[...]
            if thin is not None:
                total = total + _dot_nt(thin_refs[0][...], thin_refs[1][...])
            o_ref[...] = total.astype(out_dtype)

        if nk == 1:
            finish(part)
        else:
            @pl.when(k == 0)
            def _():
                acc_ref[...] = part

            @pl.when(jnp.logical_and(k > 0, k < nk - 1))
            def _():
                acc_ref[...] += part

            @pl.when(k == nk - 1)
            def _():
                finish(acc_ref[...] + part)

    in_specs = [a_spec, b_spec] * (n_ab // 2) + ([o_spec] if has_add else [])
    args = (a, b) + (tuple(pair) if pair is not None else ()) + ((add,) if has_add else ())
    est = ((n_ab // 2) * (_nbytes((tm, tk), a.dtype) + _nbytes((tk, tn), b.dtype)) + 2 * _nbytes((tm, tn), F32)
           + (_nbytes((tm, tn), F32) if has_add else 0)) + 2 * _nbytes((tm, tn), F32)
    if thin is not None:
        k3 = thin[0].shape[1]
        in_specs += [pl.BlockSpec((tm, k3), lambda i, j, k: (i, 0)), pl.BlockSpec((tn, k3), lambda i, j, k: (j, 0))]
        args += tuple(thin)
        est += _nbytes((tm, k3), thin[0].dtype) + _nbytes((tn, k3), thin[1].dtype)
    return pl.pallas_call(
        body, name=name,
        grid=(M // tm, N // tn, nk),
        in_specs=in_specs, out_specs=o_spec,
        out_shape=jax.ShapeDtypeStruct((M, N), out_dtype),
        scratch_shapes=[pltpu.VMEM((tm, tn) if nk > 1 else (SUBLANES, LANES), F32)],
        compiler_params=_cparams(est, ("parallel", "parallel", "arbitrary")),
    )(*[_in_hbm(t) for t in args])


def _shifted(xt, halo, first):
    halo = jnp.where(first, 0.0, halo)
    xc = jnp.concatenate([halo, xt], axis=0)
    return [xt] + [pltpu.roll(xc, s, 0)[HALO:] for s in range(1, CONV_K)]


def _conv_taps(shifted, w_ref):
    out = shifted[0] * w_ref[CONV_K - 1:CONV_K, :]
    for s in range(1, CONV_K):
        out = out + shifted[s] * w_ref[CONV_K - 1 - s:CONV_K - s, :]
    return out


def _gates(ba, arow, dtrow):
    lane = _iota(ba.shape, 1)
    beta = _sigmoid(ba)
    g = -jnp.exp(arow) * _softplus(ba + dtrow)
    return jnp.where(lane < N_HEADS, beta, jnp.where(lane < 2 * N_HEADS, g, 0.0))


def _l2n(x):
    return x * lax.rsqrt(jnp.sum(x * x, -1, keepdims=True) + RMS_EPS)


def _qkv_prep(proj, xb, wba, convw, arow, dtrow, *, tm=256):
    S = proj.shape[0]
    tm = _tile(S, tm, SUBLANES * 2)
    W3 = 3 * D_MODEL
    hb = tm // HALO

    def body(xt_ref, halo_ref, xb_ref, wba_ref, w_ref, a_ref, dt_ref, q_ref, k_ref, v_ref, gb_ref, c_ref, ba_ref):
        ba_ref[...] = _dot(xb_ref[...], wba_ref[...])
        c = _conv_taps(_shifted(xt_ref[...], halo_ref[...], pl.program_id(0) == 0), w_ref)
        c_ref[...] = c
        c = _silu(c)
        for h in range(N_HEADS):
            lo = h * D_HEAD
            q_ref[:, lo:lo + D_HEAD] = _l2n(c[:, lo:lo + D_HEAD])
            k_ref[:, lo:lo + D_HEAD] = _l2n(c[:, D_MODEL + lo:D_MODEL + lo + D_HEAD])
        v_ref[...] = c[:, 2 * D_MODEL:]
        gb_ref[...] = _gates(ba_ref[...], a_ref[...], dt_ref[...])

    row = lambda w, col=0: pl.BlockSpec((tm, w), lambda i: (i, col))
    full = lambda shape: pl.BlockSpec(shape, lambda i: (0,) * len(shape))
    est = 4 * _nbytes((tm, W3), F32) + _nbytes((tm, D_MODEL), F32)
    return pl.pallas_call(
        body, name="qkv_prep", grid=(S // tm,),
        in_specs=[row(W3), pl.BlockSpec((HALO, W3), lambda i: (jnp.maximum(i * hb - 1, 0), 0)), row(D_MODEL),
                  full((D_MODEL, LANES)), full((CONV_K, W3)), full((1, LANES)), full((1, LANES))],
        out_specs=[row(D_MODEL), row(D_MODEL), row(D_MODEL), row(LANES), row(W3), row(LANES)],
        out_shape=[jax.ShapeDtypeStruct((S, D_MODEL), F32)] * 3 + [jax.ShapeDtypeStruct((S, LANES), F32),
                                                                   jax.ShapeDtypeStruct((S, W3), F32),
                                                                   jax.ShapeDtypeStruct((S, LANES), F32)],
        compiler_params=_cparams(est, ("arbitrary",)),
    )(proj, proj, xb, wba, convw, arow, dtrow)


def _qkv_prep_bwd(proj, conv_out, ba, arow, dtrow, dq, dk, dv, dgb, xb, *, tm=256):
    S = proj.shape[0]
    tm = _tile(S, tm, SUBLANES * 2)
    W3 = 3 * D_MODEL
    hb = tm // HALO

    def body(xt_ref, halo_ref, c_ref, ba_ref, a_ref, dt_ref, dq_ref, dk_ref, dv_ref, dgb_ref, xb_ref,
             dcb_ref, dba_ref, dw_ref, da_ref, ddt_ref, dwba_ref, dc_ref):
        i = pl.program_id(0)

        @pl.when(i == 0)
        def _():
            dw_ref[...] = jnp.zeros_like(dw_ref)
            da_ref[...] = jnp.zeros_like(da_ref)
            ddt_ref[...] = jnp.zeros_like(ddt_ref)
            dwba_ref[...] = jnp.zeros_like(dwba_ref)

        shifted = _shifted(xt_ref[...], halo_ref[...], i == 0)
        a, ds = _silu_and_grad(c_ref[...])
        for h in range(N_HEADS):
            for base, d_ref in ((0, dq_ref), (D_MODEL, dk_ref)):
                lo = base + h * D_HEAD
                _, vj = jax.vjp(_l2n, a[:, lo:lo + D_HEAD])
                (dx,) = vj(d_ref[:, h * D_HEAD:(h + 1) * D_HEAD])
                dc_ref[:, lo:lo + D_HEAD] = dx * ds[:, lo:lo + D_HEAD]
        dc_ref[:, 2 * D_MODEL:] = dv_ref[...] * ds[:, 2 * D_MODEL:]
        dc = dc_ref[...]
        dcb_ref[...] = dc.astype(BF)
        for s in range(CONV_K):
            kk = CONV_K - 1 - s
            dw_ref[kk:kk + 1, :] += jnp.sum(dc * shifted[s], axis=0, keepdims=True)
        _, vj = jax.vjp(_gates, ba_ref[...], a_ref[...], dt_ref[...])
        dba, da, ddt = vj(dgb_ref[...])
        dba_ref[...] = dba.astype(BF)
        da_ref[...] += _bcast_rows(da)
        ddt_ref[...] += _bcast_rows(ddt)
        dwba_ref[...] += _dot_tn(xb_ref[...], dba.astype(BF))

    row = lambda w, col=0: pl.BlockSpec((tm, w), lambda i: (i, col))
    full = lambda shape: pl.BlockSpec(shape, lambda i: (0,) * len(shape))
    est = 8 * _nbytes((tm, W3), F32) + _nbytes((tm, D_MODEL), F32)
    return pl.pallas_call(
        body, name="qkv_prep_bwd", grid=(S // tm,),
        in_specs=[row(W3), pl.BlockSpec((HALO, W3), lambda i: (jnp.maximum(i * hb - 1, 0), 0)), row(W3), row(LANES),
                  full((1, LANES)), full((1, LANES)),
                  row(D_MODEL), row(D_MODEL), row(D_MODEL), row(LANES), row(D_MODEL)],
        out_specs=[row(W3), row(LANES), full((SUBLANES, W3)), full((SUBLANES, LANES)), full((SUBLANES, LANES)),
                   full((D_MODEL, LANES))],
        out_shape=[jax.ShapeDtypeStruct((S, W3), BF), jax.ShapeDtypeStruct((S, LANES), BF),
                   jax.ShapeDtypeStruct((SUBLANES, W3), F32), jax.ShapeDtypeStruct((SUBLANES, LANES), F32),
                   jax.ShapeDtypeStruct((SUBLANES, LANES), F32), jax.ShapeDtypeStruct((D_MODEL, LANES), F32)],
        scratch_shapes=[pltpu.VMEM((tm, W3), F32)],
        compiler_params=_cparams(est, ("arbitrary",)),
    )(proj, proj, conv_out, ba, arow, dtrow, dq, dk, dv, dgb, xb)


def _conv_bwd(dc, convw, dproj, *, tm=256):
    S, W3 = dc.shape
    tm = _tile(S, tm, HALO_BF)
    hb = tm // HALO_BF
    nt = S // tm

    def body(dc_ref, nxt_ref, w_ref, dproj_ref, o_ref):
        last = pl.program_id(0) == nt - 1
        nxt = jnp.where(last, 0.0, nxt_ref[...].astype(F32))
        cur = dc_ref[...].astype(F32)
        xc = jnp.concatenate([cur, nxt], axis=0)
        out = cur * w_ref[CONV_K - 1:CONV_K, :]
        for s in range(1, CONV_K):
            out = out + pltpu.roll(xc, tm + HALO_BF - s, 0)[:tm] * w_ref[CONV_K - 1 - s:CONV_K - s, :]
        o_ref[...] = out.astype(BF)

    est = 5 * _nbytes((tm, W3), F32)
    return pl.pallas_call(
        body, name="conv_bwd", grid=(nt,),
        in_specs=[pl.BlockSpec((tm, W3), lambda i: (i, 0)),
                  pl.BlockSpec((HALO_BF, W3), lambda i: (jnp.minimum((i + 1) * hb, S // HALO_BF - 1), 0)),
                  pl.BlockSpec((CONV_K, W3), lambda i: (0, 0)), pl.BlockSpec(memory_space=pl.ANY)],
        out_specs=pl.BlockSpec((tm, W3), lambda i: (i, 0)),
        out_shape=jax.ShapeDtypeStruct(dproj.shape, BF),
        input_output_aliases={3: 0},
        compiler_params=_cparams(est, ("parallel",)),
    )(dc, dc, convw, dproj)


NEUMANN_BLOCK = 8


def _inv_unit_lower(A):
    C = A.shape[-1]
    row, col = _iota((C, C), 0), _iota((C, C), 1)
    eye = jnp.where(row == col, 1.0, 0.0).astype(F32)
    Ab = A.astype(BF)
    sh = jnp.int32(int(math.log2(NEUMANN_BLOCK)))
    B = jnp.where(lax.shift_right_logical(row, sh) == lax.shift_right_logical(col, sh), Ab, jnp.zeros_like(Ab))
    B2 = _mxu(B, B, "nn")
    B4 = _dot3(B2, B2, "nn")
    b2h, b2l = _split(B2)
    P = eye - B.astype(F32) + B2 - (_mxu(B, b2h, "nn") + _mxu(B, b2l, "nn"))
    T = P + _dot3(P, B4, "nn")
    b = NEUMANN_BLOCK
    while b < C:
        hi = ~(2 * b - 1)
        off = ((row & hi) == (col & hi)) & ((row & b) != 0) & ((col & b) == 0)
        Aoff = jnp.where(off, Ab, jnp.zeros_like(Ab))
        th, tl = _split(T)
        xh, xl = _split(_mxu(th, Aoff, "nn") + _mxu(tl, Aoff, "nn"))
        T = T - (_mxu(xh, th, "nn") + (_mxu(xh, tl, "nn") + _mxu(xl, th, "nn")))
        b *= 2
    return T


def _delta_common(q, k, g, beta):
    C = q.shape[-2]
    row, col = _iota((C, C), 0), _iota((C, C), 1)
    tril = row >= col
    qs = q * (D_HEAD ** -0.5)
    gcb = _dot01(jnp.where(tril, 1.0, 0.0), jnp.broadcast_to(g, g.shape[:-1] + (LANES,)))
    gc = gcb[..., :1]
    gr = jnp.swapaxes(gcb, -1, -2)
    Dm = jnp.exp(jnp.where(tril, gc - gr, -1e30))
    Dmt = jnp.exp(jnp.where(row <= col, gr - gc, -1e30))
    eg = jnp.exp(gc)
    gl = jnp.sum(jnp.where(_iota((C, 1), 0) == C - 1, gc, 0.0), axis=(-2, -1), keepdims=True)
    el = jnp.exp(gl)
    er = jnp.exp(gl - gc)
    kb = k * beta
    KK = _dot_nt(kb, k)
    QK = _dot_nt(qs, k)
    return dict(row=row, col=col, tril=tril, qs=qs, gc=gc, Dm=Dm, Dmt=Dmt, eg=eg, el=el, er=er, kb=kb, KK=KK, QK=QK)


def _delta_chunk_fwd(S0, q, k, v, g, beta):
    m = _delta_common(q, k, g, beta)
    T = _inv_unit_lower(jnp.where(m["row"] > m["col"], m["KK"] * m["Dm"], 0.0))
    u = _dotf(T, v * beta)
    w = _dotf(T, m["kb"] * m["eg"])
    vn = u - _dot(w, S0)
    o = _dot(m["qs"] * m["eg"], S0) + _dot(m["QK"] * m["Dm"], vn)
    S1 = S0 * m["el"] + _dot_tn(k * m["er"], vn)
    return o, S1, jnp.swapaxes(T, -1, -2), u, w


def _delta_chunk_bwd(S0, q, k, v, g, beta, Tt, u, w, do, dS1):
    m = _delta_common(q, k, g, beta)
    C = q.shape[-2]
    qs, Dm, Dmt, eg, el, er, kb, KK, QK = (m[n] for n in ("qs", "Dm", "Dmt", "eg", "el", "er", "kb", "KK", "QK"))
    strict = m["row"] > m["col"]
    total = lambda x: jnp.sum(x, axis=(-2, -1), keepdims=True)
    vn = u - _dot(w, S0)
    qg = qs * eg
    kr = k * er

    dvn = _dot(_dot_nt(k, qs) * Dmt, do) + _dot(kr, dS1)
    dS0 = dS1 * el + _dot_tn(qg, do) - _dot_tn(w, dvn)
    d_el = total(dS1 * S0)
    dqg = _dot_nt(do, S0)
    dqs = dqg * eg
    deg = jnp.sum(dqg * qs, -1, keepdims=True)
    dP = _dot_nt(do, vn)
    dPD = dP * Dm
    dqs = dqs + _dot(dPD, k)
    dk = _dot(_dot_nt(vn, do) * Dmt, qs)
    dD = dP * QK
    dkr = _dot_nt(vn, dS1)
    dk = dk + dkr * er
    der = jnp.sum(dkr * k, -1, keepdims=True)
    dw = -_dot_nt(dvn, S0)
    th, tl = _split(Tt)

    def tt_times(x):
        xh, xl = _split(x)
        return _mxu(th, xh, "nn") + (_mxu(th, xl, "nn") + _mxu(tl, xh, "nn"))

    dru = tt_times(dvn)
    drw = tt_times(dw)
    dA = -(_dotf_nt(dru, u) + _dotf_nt(drw, w))
    dAm = jnp.where(strict, dA, 0.0)
    dKK = dAm * Dm
    dkb = _dot(dKK, k)
    dk = dk + _dot_tn(dKK, kb)
    dD = dD + dAm * KK
    dv = dru * beta
    dbeta = jnp.sum(dru * v, -1, keepdims=True)
    dkb = dkb + drw * eg
    deg = deg + jnp.sum(drw * kb, -1, keepdims=True)
    dk = dk + dkb * beta
    dbeta = dbeta + jnp.sum(dkb * k, -1, keepdims=True)
    E = dD * Dm
    dgc = jnp.sum(E, -1, keepdims=True) - jnp.sum(jnp.swapaxes(E, -1, -2), -1, keepdims=True)
    dgc = dgc + deg * eg - der * er
    dgl = total(der * er) + d_el * el
    dgc = dgc + jnp.where(_iota((C, 1), 0) == C - 1, dgl, 0.0)
    triu = jnp.where(m["row"] <= m["col"], 1.0, 0.0)
    dg = _dot01(triu, jnp.broadcast_to(dgc, dgc.shape[:-1] + (LANES,)))[..., :1]
    dq = dqs * (D_HEAD ** -0.5)
    return dq, dk, dv, dg, dbeta, dS0


def _head_cols(gb, h):
    lane = _iota(gb.shape, 1)
    beta = jnp.sum(jnp.where(lane == h, gb, 0.0), -1, keepdims=True)
    g = jnp.sum(jnp.where(lane == N_HEADS + h, gb, 0.0), -1, keepdims=True)
    return g, beta


def _delta_fwd(q, k, v, gb):
    S = q.shape[0]
    C = DELTA_CHUNK
    N = S // C

    HB = DELTA_HEADS_PER_STEP

    def body(q_ref, k_ref, v_ref, gb_ref, o_ref, st_ref, t_ref, u_ref, w_ref, s_scr):
        n, hb = pl.program_id(0), pl.program_id(1)
        gb = gb_ref[...]

        @pl.when(n == 0)
        def _():
            for hh in range(HB):
                s_scr[hb * HB + hh] = jnp.zeros((D_HEAD, D_HEAD), F32)

        heads = [hb * HB + hh for hh in range(HB)]
        cols = [slice(hh * D_HEAD, (hh + 1) * D_HEAD) for hh in range(HB)]
        per_head = lambda ref: jnp.stack([ref[:, c] for c in cols])
        g, beta = (jnp.stack(t) for t in zip(*[_head_cols(gb, h) for h in heads]))
        S0 = jnp.stack([s_scr[h] for h in heads])
        o, S1, Tt, u, w = _delta_chunk_fwd(S0, per_head(q_ref), per_head(k_ref), per_head(v_ref), g, beta)
        for hh in range(HB):
            st_ref[hh, 0] = S0[hh]
            t_ref[hh, 0] = Tt[hh]
            o_ref[:, cols[hh]] = o[hh]
            u_ref[:, cols[hh]] = u[hh]
            w_ref[:, cols[hh]] = w[hh]
            s_scr[heads[hh]] = S1[hh]

    hd = pl.BlockSpec((C, HB * D_HEAD), lambda n, h: (n, h))
    mat = pl.BlockSpec((HB, 1, D_HEAD, D_HEAD), lambda n, h: (h, n, 0, 0))
    est = 40 * HB * _nbytes((C, D_HEAD), F32)
    seq = jax.ShapeDtypeStruct((S, N_HEADS * D_HEAD), F32)
    return pl.pallas_call(
        body, name="delta_fwd", grid=(N, N_HEADS // HB),
        in_specs=[hd, hd, hd, pl.BlockSpec((C, LANES), lambda n, h: (n, 0))],
        out_specs=[hd, mat, mat, hd, hd],
        out_shape=[seq, jax.ShapeDtypeStruct((N_HEADS, N, D_HEAD, D_HEAD), F32),
                   jax.ShapeDtypeStruct((N_HEADS, N, C, C), F32), seq, seq],
        scratch_shapes=[pltpu.VMEM((N_HEADS, D_HEAD, D_HEAD), F32)],
        compiler_params=_cparams(est, ("arbitrary", "arbitrary")),
    )(q, k, v, gb)


def _delta_bwd(q, k, v, gb, st, tinv, u, w, do):
    S = q.shape[0]
    C = DELTA_CHUNK
    N = S // C

    HB = DELTA_HEADS_PER_STEP

    def body(q_ref, k_ref, v_ref, gb_ref, st_ref, t_ref, u_ref, w_ref, do_ref, dq_ref, dk_ref, dv_ref, dgb_ref, ds_scr):
        n, hb = pl.program_id(0), pl.program_id(1)
        gb = gb_ref[...]
        lane = _iota((C, LANES), 1)
        dgb = jnp.zeros((C, LANES), F32)

        @pl.when(n == 0)
        def _():
            for hh in range(HB):
                ds_scr[hb * HB + hh] = jnp.zeros((D_HEAD, D_HEAD), F32)

        heads = [hb * HB + hh for hh in range(HB)]
        cols = [slice(hh * D_HEAD, (hh + 1) * D_HEAD) for hh in range(HB)]
        per_head = lambda ref: jnp.stack([ref[:, c] for c in cols])
        g, beta = (jnp.stack(t) for t in zip(*[_head_cols(gb, h) for h in heads]))
        dS1 = jnp.stack([ds_scr[h] for h in heads])
        dq, dk, dv, dg, dbeta, dS0 = _delta_chunk_bwd(
            st_ref[:, 0], per_head(q_ref), per_head(k_ref), per_head(v_ref), g, beta, t_ref[:, 0],
            per_head(u_ref), per_head(w_ref), per_head(do_ref), dS1)
        for hh, h in enumerate(heads):
            dq_ref[:, cols[hh]] = dq[hh]
            dk_ref[:, cols[hh]] = dk[hh]
            dv_ref[:, cols[hh]] = dv[hh]
            dgb = dgb + jnp.where(lane == h, dbeta[hh], 0.0) + jnp.where(lane == N_HEADS + h, dg[hh], 0.0)
            ds_scr[h] = dS0[hh]

        @pl.when(hb == 0)
        def _():
            dgb_ref[...] = dgb

        @pl.when(hb > 0)
        def _():
            dgb_ref[...] += dgb

    hd = pl.BlockSpec((C, HB * D_HEAD), lambda n, h: (N - 1 - n, h))
    mat = pl.BlockSpec((HB, 1, D_HEAD, D_HEAD), lambda n, h: (h, N - 1 - n, 0, 0))
    gbs = pl.BlockSpec((C, LANES), lambda n, h: (N - 1 - n, 0))
    est = 60 * HB * _nbytes((C, D_HEAD), F32)
    return pl.pallas_call(
        body, name="delta_bwd", grid=(N, N_HEADS // HB),
        in_specs=[hd, hd, hd, gbs, mat, mat, hd, hd, hd],
        out_specs=[hd, hd, hd, gbs],
        out_shape=[jax.ShapeDtypeStruct((S, N_HEADS * D_HEAD), F32)] * 3 + [jax.ShapeDtypeStruct((S, LANES), F32)],
        scratch_shapes=[pltpu.VMEM((N_HEADS, D_HEAD, D_HEAD), F32)],
        compiler_params=_cparams(est, ("arbitrary", "arbitrary")),
    )(q, k, v, gb, st, tinv, u, w, do)


def _ya_head(o, z, onw):
    return o * lax.rsqrt(jnp.mean(o * o, -1, keepdims=True) + RMS_EPS) * onw * _silu(z)


def _norm_cdf(x):
    return 0.5 * (1.0 + lax.erf(x * 0.7071067811865476))


def _norm_pdf(x):
    return jnp.exp(-0.5 * x * x) * 0.3989422804014327


def _chunk_causal(shape, di, dj):
    sh = jnp.int32(int(math.log2(SGU_CHUNK)))
    return lax.shift_right_logical(_iota(shape, di), sh) >= lax.shift_right_logical(_iota(shape, dj), sh)


def _ws_masked(ws):
    return jnp.where(_chunk_causal(ws.shape, 1, 2), ws, 0.0)


def _mix_prep(o, proj, onw, sg, sb, ws, bst, wpa, wpb, *, tm=512):
    S = o.shape[0]
    tm = _tile(S, tm, SGU_BLOCK)

    def body(o_ref, z_ref, u_ref, vg_ref, ga_ref, gb_ref, onw_ref, sg_ref, sb_ref, ws_ref, bst_ref, wa_ref, wb_ref,
             ya_ref, yb_ref, phi_ref, pa_ref, pb_ref, m_ref):
        onw = onw_ref[...]
        for h in range(N_HEADS):
            sl = slice(h * D_HEAD, (h + 1) * D_HEAD)
            ya_ref[:, sl] = _ya_head(o_ref[:, sl], z_ref[:, sl].astype(F32), onw).astype(BF)
        u, vg = u_ref[...].astype(F32), vg_ref[...].astype(F32)
        phi_u, phi_v = _norm_cdf(u), _norm_cdf(vg)
        phi_ref[:, :D_MODEL] = phi_u
        phi_ref[:, D_MODEL:] = phi_v
        ua, vl = u * phi_u, _ln(vg * phi_v, sg_ref[...], sb_ref[...])
        wsm = _ws_masked(ws_ref[...])
        bst = bst_ref[...]
        for blk in range(tm // SGU_BLOCK):
            rs = slice(blk * SGU_BLOCK, (blk + 1) * SGU_BLOCK)
            for gi in range(SGU_GROUPS):
                cs = slice(gi * D_HEAD, (gi + 1) * D_HEAD)
                sp = _dot(wsm[gi], vl[rs, cs]) + bst[:, gi:gi + 1]
                yb_ref[rs, cs] = (ua[rs, cs] * sp).astype(BF)
        pa = _dot(ya_ref[...], wa_ref[...]).astype(BF)
        pb = _dot(yb_ref[...], wb_ref[...]).astype(BF)
        pa_ref[...] = pa
        pb_ref[...] = pb
        m_ref[...] = (_sigmoid(ga_ref[...].astype(F32)) * pa.astype(F32)
                      + _sigmoid(gb_ref[...].astype(F32)) * pb.astype(F32)).astype(BF)

    blk = lambda col: pl.BlockSpec((tm, D_MODEL), lambda i: (i, col))
    full = lambda shape: pl.BlockSpec(shape, lambda i: (0,) * len(shape))
    est = 14 * _nbytes((tm, D_MODEL), F32) + 4 * _nbytes((D_MODEL, D_MODEL), BF)
    return pl.pallas_call(
        body, name="mix_prep", grid=(S // tm,),
        in_specs=[blk(0), blk(0), blk(1), blk(2), blk(3), blk(4), full((1, D_HEAD)), full((1, D_MODEL)),
                  full((1, D_MODEL)), full((SGU_GROUPS, SGU_BLOCK, SGU_BLOCK)), full((SGU_BLOCK, LANES)),
                  full((D_MODEL, D_MODEL)), full((D_MODEL, D_MODEL))],
        out_specs=[blk(0), blk(0), pl.BlockSpec((tm, 2 * D_MODEL), lambda i: (i, 0)), blk(0), blk(0), blk(0)],
        out_shape=[jax.ShapeDtypeStruct((S, D_MODEL), BF)] * 2 + [jax.ShapeDtypeStruct((S, 2 * D_MODEL), F32)]
                  + [jax.ShapeDtypeStruct((S, D_MODEL), BF)] * 3,
        compiler_params=_cparams(est, ("parallel",)),
    )(o, proj, proj, proj, proj, proj, onw, sg, sb, ws, bst, wpa, wpb)


def _mix_prep_bwd(o, proj, phi, onw, sg, sb, ws, bst, dpa, wpa, dpb, wpb, dproj, *, tm=256):
    S = o.shape[0]
    tm = _tile(S, tm, SGU_BLOCK)

    def body(o_ref, z_ref, u_ref, vg_ref, phi_ref, onw_ref, sg_ref, sb_ref, ws_ref, bst_ref, dpa_ref, wpa_ref, dpb_ref,
             wpb_ref, dproj_in, do_ref, dzuv_ref, donw_ref, dsg_ref, dsb_ref, dws_ref, dbst_ref, dvl_scr, dua_scr,
             dya_ref, dyb_ref):
        dz_ref, du_ref, dvg_ref = (dzuv_ref.at[:, k * D_MODEL:(k + 1) * D_MODEL] for k in range(3))
        @pl.when(pl.program_id(0) == 0)
        def _():
            for r in (donw_ref, dsg_ref, dsb_ref, dws_ref, dbst_ref):
                r[...] = jnp.zeros_like(r)

        dya_ref[...] = _dot_nt(dpa_ref[...], wpa_ref[...])
        dyb_ref[...] = _dot_nt(dpb_ref[...], wpb_ref[...])

        onw = onw_ref[...]
        donw = jnp.zeros((1, D_HEAD), F32)
        for h in range(N_HEADS):
            sl = slice(h * D_HEAD, (h + 1) * D_HEAD)
            _, vj = jax.vjp(_ya_head, o_ref[:, sl], z_ref[:, sl].astype(F32), onw)
            do_h, dz_h, donw_h = vj(dya_ref[:, sl])
            do_ref[:, sl] = do_h.astype(BF)
            dz_ref[:, sl] = dz_h.astype(BF)
            donw = donw + donw_h
        donw_ref[...] += _bcast_rows(donw)

        u, vg = u_ref[...].astype(F32), vg_ref[...].astype(F32)
        phi_u, phi_v = phi_ref[:, :D_MODEL], phi_ref[:, D_MODEL:]
        ua = u * phi_u
        vl, vj = jax.vjp(_ln, vg * phi_v, sg_ref[...], sb_ref[...])
        wsm = _ws_masked(ws_ref[...])
        bst = bst_ref[...]
        lane = _iota((SGU_BLOCK, LANES), 1)
        dbst = jnp.zeros((SGU_BLOCK, LANES), F32)
        cmask = _chunk_causal((SGU_BLOCK, SGU_BLOCK), 0, 1)
        for gi in range(SGU_GROUPS):
            cs = slice(gi * D_HEAD, (gi + 1) * D_HEAD)
            wg = wsm[gi]
            wgt = jnp.transpose(wg)
            dwg = jnp.zeros((SGU_BLOCK, SGU_BLOCK), F32)
            for blk in range(tm // SGU_BLOCK):
                rs = slice(blk * SGU_BLOCK, (blk + 1) * SGU_BLOCK)
                sp = _dot(wg, vl[rs, cs]) + bst[:, gi:gi + 1]
                dyb = dyb_ref[rs, cs]
                dsp = dyb * ua[rs, cs]
                dua_scr[rs, cs] = dyb * sp
                dvl_scr[rs, cs] = _dot(wgt, dsp)
                dwg = dwg + _dot_nt(dsp, vl[rs, cs])
                dbst = dbst + jnp.where(lane == gi, jnp.sum(dsp, -1, keepdims=True), 0.0)
            dws_ref[gi] += jnp.where(cmask, dwg, 0.0)
        dbst_ref[...] += dbst
        dgv, dsg, dsb = vj(dvl_scr[...])
        du_ref[...] = (dua_scr[...] * (phi_u + u * _norm_pdf(u))).astype(BF)
        dvg_ref[...] = (dgv * (phi_v + vg * _norm_pdf(vg))).astype(BF)
        dsg_ref[...] += _bcast_rows(dsg)
        dsb_ref[...] += _bcast_rows(dsb)

    blk = lambda col: pl.BlockSpec((tm, D_MODEL), lambda i: (i, col))
    full = lambda shape: pl.BlockSpec(shape, lambda i: (0,) * len(shape))
    est = 18 * _nbytes((tm, D_MODEL), F32) + 4 * _nbytes((D_MODEL, D_MODEL), BF)
    outs = pl.pallas_call(
        body, name="mix_prep_bwd", grid=(S // tm,),
        in_specs=[blk(0), blk(0), blk(1), blk(2), pl.BlockSpec((tm, 2 * D_MODEL), lambda i: (i, 0)),
                  full((1, D_HEAD)), full((1, D_MODEL)), full((1, D_MODEL)),
                  full((SGU_GROUPS, SGU_BLOCK, SGU_BLOCK)), full((SGU_BLOCK, LANES)),
                  blk(0), full((D_MODEL, D_MODEL)), blk(0), full((D_MODEL, D_MODEL)),
                  pl.BlockSpec(memory_space=pl.ANY)],
        out_specs=[blk(0), pl.BlockSpec((tm, 3 * D_MODEL), lambda i: (i, 1)),
                   full((SUBLANES, D_HEAD)), full((SUBLANES, D_MODEL)), full((SUBLANES, D_MODEL)),
                   full((SGU_GROUPS, SGU_BLOCK, SGU_BLOCK)), full((SGU_BLOCK, LANES))],
        out_shape=[jax.ShapeDtypeStruct((S, D_MODEL), BF), jax.ShapeDtypeStruct(dproj.shape, BF),
                   jax.ShapeDtypeStruct((SUBLANES, D_HEAD), F32), jax.ShapeDtypeStruct((SUBLANES, D_MODEL), F32),
                   jax.ShapeDtypeStruct((SUBLANES, D_MODEL), F32),
                   jax.ShapeDtypeStruct((SGU_GROUPS, SGU_BLOCK, SGU_BLOCK), F32),
                   jax.ShapeDtypeStruct((SGU_BLOCK, LANES), F32)],
        input_output_aliases={14: 1},
        scratch_shapes=[pltpu.VMEM((tm, D_MODEL), F32)] * 4,
        compiler_params=_cparams(est, ("arbitrary",)),
    )(o, proj, proj, proj, phi, onw, sg, sb, ws, bst, dpa, wpa, dpb, wpb, dproj)
    return outs


def _mm_gate_merge_bwd(dmix, wo, pa, pb, proj, *, tm=512):
    S = pa.shape[0]
    tm = _tile(S, tm, SUBLANES * 2)

    def body(d_ref, w_ref, pa_ref, pb_ref, ga_ref, gb_ref, dpa_ref, dpb_ref, dg_ref):
        dm = _dot_nt(d_ref[...], w_ref[...])
        sa, sb = _sigmoid(ga_ref[...].astype(F32)), _sigmoid(gb_ref[...].astype(F32))
        dpa_ref[...] = (dm * sa).astype(BF)
        dpb_ref[...] = (dm * sb).astype(BF)
        dg_ref[:, :D_MODEL] = (dm * pa_ref[...].astype(F32) * sa * (1.0 - sa)).astype(BF)
        dg_ref[:, D_MODEL:] = (dm * pb_ref[...].astype(F32) * sb * (1.0 - sb)).astype(BF)

    blk = lambda col: pl.BlockSpec((tm, D_MODEL), lambda i: (i, col))
    est = _nbytes((D_MODEL, D_MODEL), BF) + 10 * _nbytes((tm, D_MODEL), F32)
    return pl.pallas_call(
        body, name="mm_gate_merge_bwd", grid=(S // tm,),
        in_specs=[blk(0), pl.BlockSpec((D_MODEL, D_MODEL), lambda i: (0, 0)), blk(0), blk(0), blk(3), blk(4)],
        out_specs=[blk(0), blk(0), pl.BlockSpec((tm, 2 * D_MODEL), lambda i: (i, 3))],
        out_shape=[jax.ShapeDtypeStruct((S, D_MODEL), BF)] * 2 + [jax.ShapeDtypeStruct((S, 8 * D_MODEL), BF)],
        compiler_params=_cparams(est, ("parallel",)),
    )(dmix, wo, pa, pb, proj, proj)


def _mm_swiglu(xb, wgt, wut, *, tm=1024, tn=768):
    S, K = xb.shape
    tm = _tile(S, tm, SUBLANES * 2)
    tn = _tile(FFN_K, tn, LANES)

    def body(x_ref, wg_ref, wu_ref, hg_ref, hu_ref, h_ref):
        x = x_ref[...]
        hg = _dot_nt(x, wg_ref[...]).astype(BF)
        hu = _dot_nt(x, wu_ref[...]).astype(BF)
        hg_ref[...] = hg
        hu_ref[...] = hu
        h_ref[...] = (_silu(hg.astype(F32)) * hu.astype(F32)).astype(BF)

    out = pl.BlockSpec((tm, tn), lambda i, j: (i, j))
    est = _nbytes((tm, K), BF) + 2 * _nbytes((K, tn), BF) + 6 * _nbytes((tm, tn), F32)
    return pl.pallas_call(
        body, name="mm_swiglu", grid=(S // tm, FFN_K // tn),
        in_specs=[pl.BlockSpec((tm, K), lambda i, j: (i, 0)), pl.BlockSpec((tn, K), lambda i, j: (j, 0)),
                  pl.BlockSpec((tn, K), lambda i, j: (j, 0))],
        out_specs=[out] * 3, out_shape=[jax.ShapeDtypeStruct((S, FFN_K), BF)] * 3,
        compiler_params=_cparams(est, ("parallel", "parallel")),
    )(xb, wgt, wut)


def _mm_swiglu_bwd(dffn, wd, hg, hu, *, tm=1024, tn=768):
    S, K = dffn.shape
    tm = _tile(S, tm, SUBLANES * 2)
    tn = _tile(FFN_K, tn, LANES)

    def body(d_ref, w_ref, hg_ref, hu_ref, dhg_ref, dhu_ref):
        dh = _dot_nt(d_ref[...], w_ref[...])
        act, dact = _silu_and_grad(hg_ref[...].astype(F32))
        dhg_ref[...] = (dh * hu_ref[...].astype(F32) * dact).astype(BF)
        dhu_ref[...] = (dh * act).astype(BF)

    out = pl.BlockSpec((tm, tn), lambda i, j: (i, j))
    est = _nbytes((tm, K), dffn.dtype) + _nbytes((tn, K), BF) + 8 * _nbytes((tm, tn), F32)
    return pl.pallas_call(
        body, name="mm_swiglu_bwd", grid=(S // tm, FFN_K // tn),
        in_specs=[pl.BlockSpec((tm, K), lambda i, j: (i, 0)), pl.BlockSpec((tn, K), lambda i, j: (j, 0)), out, out],
        out_specs=[out, out], out_shape=[jax.ShapeDtypeStruct((S, FFN_K), BF)] * 2,
        compiler_params=_cparams(est, ("parallel", "parallel")),
    )(dffn, wd, hg, hu)


def _mm_resid_ln(a, bmat, x, g, b, *, name, tm=512):
    S, K = a.shape
    tm = _tile(S, tm, SUBLANES * 2)

    def body(a_ref, w_ref, x_ref, g_ref, b_ref, pre_ref, y_ref, yb_ref):
        pre = ALPHA * x_ref[...] + _dot(a_ref[...], w_ref[...])
        y = _ln(pre, g_ref[...], b_ref[...])
        pre_ref[...] = pre
        y_ref[...] = y
        yb_ref[...] = y.astype(BF)

    blk = pl.BlockSpec((tm, D_MODEL), lambda i: (i, 0))
    vec = pl.BlockSpec((1, D_MODEL), lambda i: (0, 0))
    est = _nbytes((tm, K), BF) + _nbytes((K, D_MODEL), BF) + 8 * _nbytes((tm, D_MODEL), F32)
    return pl.pallas_call(
        body, name=name, grid=(S // tm,),
        in_specs=[pl.BlockSpec((tm, K), lambda i: (i, 0)), pl.BlockSpec((K, D_MODEL), lambda i: (0, 0)), blk, vec, vec],
        out_specs=[blk, blk, blk],
        out_shape=[jax.ShapeDtypeStruct((S, D_MODEL), F32)] * 2 + [jax.ShapeDtypeStruct((S, D_MODEL), BF)],
        compiler_params=_cparams(est, ("parallel",)),
    )(a, bmat, x, g, b)


def _ln_bwd(pre, g, b, dy, *, tm=512):
    S = pre.shape[0]
    tm = _tile(S, tm, SUBLANES)

    def body(p_ref, g_ref, b_ref, dy_ref, dp_ref, dg_ref, db_ref):
        @pl.when(pl.program_id(0) == 0)
        def _():
            dg_ref[...] = jnp.zeros_like(dg_ref)
            db_ref[...] = jnp.zeros_like(db_ref)

        _, vj = jax.vjp(_ln, p_ref[...], g_ref[...], b_ref[...])
        dp, dg, db = vj(dy_ref[...])
        dp_ref[...] = dp
        dg_ref[...] += _bcast_rows(dg)
        db_ref[...] += _bcast_rows(db)

    blk = pl.BlockSpec((tm, D_MODEL), lambda i: (i, 0))
    vec = pl.BlockSpec((1, D_MODEL), lambda i: (0, 0))
    acc = pl.BlockSpec((SUBLANES, D_MODEL), lambda i: (0, 0))
    return pl.pallas_call(
        body, name="ln_bwd", grid=(S // tm,),
        in_specs=[blk, vec, vec, blk], out_specs=[blk, acc, acc],
        out_shape=[jax.ShapeDtypeStruct((S, D_MODEL), F32)] + [jax.ShapeDtypeStruct((SUBLANES, D_MODEL), F32)] * 2,
        compiler_params=_cparams(10 * _nbytes((tm, D_MODEL), F32), ("arbitrary",)),
    )(pre, g, b, dy)


def _loss_ln_bwd(y, tgt, pre, g, b, *, tm=512):
    S = y.shape[0]
    tm = _tile(S, tm, SUBLANES)

    def body(y_ref, t_ref, p_ref, g_ref, b_ref, dp_ref, dg_ref, db_ref, l_ref):
        @pl.when(pl.program_id(0) == 0)
        def _():
            for r in (dg_ref, db_ref, l_ref):
                r[...] = jnp.zeros_like(r)

        e = y_ref[...] - t_ref[...]
        l_ref[...] += 0.5 * jnp.sum(jnp.mean(e * e, -1, keepdims=True), keepdims=True)
        _, vj = jax.vjp(_ln, p_ref[...], g_ref[...], b_ref[...])
        dp, dg, db = vj(e * (1.0 / D_MODEL))
        dp_ref[...] = dp
        dg_ref[...] += _bcast_rows(dg)
        db_ref[...] += _bcast_rows(db)

    blk = pl.BlockSpec((tm, D_MODEL), lambda i: (i, 0))
    vec = pl.BlockSpec((1, D_MODEL), lambda i: (0, 0))
    acc = pl.BlockSpec((SUBLANES, D_MODEL), lambda i: (0, 0))
    return pl.pallas_call(
        body, name="loss_ln_bwd", grid=(S // tm,),
        in_specs=[blk, blk, blk, vec, vec], out_specs=[blk, acc, acc, pl.BlockSpec((SUBLANES, LANES), lambda i: (0, 0))],
        out_shape=[jax.ShapeDtypeStruct((S, D_MODEL), F32)] + [jax.ShapeDtypeStruct((SUBLANES, D_MODEL), F32)] * 2
                  + [jax.ShapeDtypeStruct((SUBLANES, LANES), F32)],
        compiler_params=_cparams(12 * _nbytes((tm, D_MODEL), F32), ("arbitrary",)),
    )(y, tgt, pre, g, b)


def _layer_fwd(x, xb, w, late):
    pq = _mm(xb, w["win"], mode="nn", name="mm_in_qkv", tm=1024, tn=1024, cols=(0, 3 * D_MODEL))
    proj = _mm(xb, w["win"], mode="nn", name="mm_in_rest", tm=1024, tn=1024, cols=(3 * D_MODEL, 5 * D_MODEL), out_dtype=BF)
    qn, kn, vv, gb, conv_out, ba = _qkv_prep(pq, xb, w["wba"], w["convw"], w["arow"], w["dtrow"])
    o, st, tinv, wy_u, wy_w = _delta_fwd(qn, kn, vv, gb)
    w = {**w, **late(o)}
    ya, yb, phi, pa, pb, m = _mix_prep(o, proj, w["onw"], w["sg"], w["sb"], w["ws"], w["bst"], w["wpa"], w["wpb"])
    pre1, x1, x1b = _mm_resid_ln(m, w["wo"], x, w["ln1g"], w["ln1b"], name="mm_out_ln")
    hg, hu, h = _mm_swiglu(x1b, w["wgt"], w["wut"])
    pre2, x2, x2b = _mm_resid_ln(h, w["wd"], x1, w["ln2g"], w["ln2b"], name="mm_down_ln")
    saved = dict(xb=xb, pq=pq, conv_out=conv_out, proj=proj, phi=phi, ba=ba, qn=qn, kn=kn, vv=vv, gb=gb, o=o, st=st, tinv=tinv, wy_u=wy_u, wy_w=wy_w,
                 ya=ya, yb=yb,
                 pa=pa, pb=pb, m=m, pre1=pre1, x1b=x1b, hg=hg, hu=hu, h=h, pre2=pre2)
    return x2, x2b, saved, w


def _layer_bwd(dpre2, ln2_grads, w, s, on_part=None):
    g = {}
    started = lambda part: on_part(part, g) if on_part is not None else None
    after = lambda v, token: v if token is None else v + token.astype(v.dtype)
    g["ln2g"], g["ln2b"] = ln2_grads
    dhg, dhu = _mm_swiglu_bwd(dpre2, w["wd"], s["hg"], s["hu"])
    g["wd"] = _mm(s["h"], dpre2, mode="tn", name="mm_tn_down", tm=1536, tk=1024, out_dtype=BF)
    dx1 = _mm(dhg, w["wgt"], mode="nn", name="mm_nn_gu", pair=(dhu, w["wut"]), add=dpre2, add_scale=ALPHA, tm=1024, tk=1536)
    g["wgt"] = _mm(dhg, s["x1b"], mode="tn", name="mm_tn_gu", tm=1536, tn=1024, tk=2048, out_dtype=BF)
    g["wut"] = _mm(dhu, s["x1b"], mode="tn", name="mm_tn_gu", tm=1536, tn=1024, tk=2048, out_dtype=BF)
    dpre1, g["ln1g"], g["ln1b"] = _ln_bwd(s["pre1"], w["ln1g"], w["ln1b"], dx1)
    g["wo"] = _mm(s["m"], dpre1, mode="tn", name="mm_tn_sq", tm=1024, tk=1024, out_dtype=BF)
    dpa, dpb, dproj = _mm_gate_merge_bwd(dpre1, w["wo"], s["pa"], s["pb"], s["proj"])
    g["wpa"] = _mm(s["ya"], dpa, mode="tn", name="mm_tn_sq", tm=1024, tk=1024, out_dtype=BF)
    g["wpb"] = _mm(s["yb"], dpb, mode="tn", name="mm_tn_sq", tm=1024, tk=1024, out_dtype=BF)
    do, dproj, g["onw"], g["sg"], g["sb"], g["ws"], g["bst"] = _mix_prep_bwd(
        s["o"], s["proj"], s["phi"], after(w["onw"], started("late")), w["sg"], w["sb"], w["ws"], w["bst"],
        dpa, w["wpa"], dpb, w["wpb"], dproj)
    dqn, dkn, dvv, dgb = _delta_bwd(s["qn"], s["kn"], s["vv"], s["gb"], s["st"], s["tinv"], s["wy_u"], s["wy_w"], do)
    dc, dba, g["convw"], g["arow"], g["dtrow"], dwba = _qkv_prep_bwd(
        s["pq"], s["conv_out"], s["ba"], w["arow"], w["dtrow"], dqn, dkn, dvv, dgb, s["xb"])
    g["wba"] = dwba.astype(BF)
    dproj = _conv_bwd(dc, w["convw"], dproj)
    g["win"] = _mm(s["xb"], dproj, mode="tn", name="mm_tn_in", tm=1024, tn=1024, tk=2048, out_dtype=BF)
    dx = _mm(dproj, w["win"], mode="nt", name="mm_nt_in", add=dpre1, add_scale=ALPHA, tm=1024, tk=2048,
             thin=(dba, after(w["wba"], started("early"))))
    return dx, g


def _local_step(x, xb, tgt, layers, on_grads=None):
    saved, weights = [], []
    for layer in layers:
        x, xb, s, w = _layer_fwd(x, xb, *layer(x))
        saved.append(s)
        weights.append(w)
    last = len(layers) - 1
    dpre2, dg, db, lacc = _loss_ln_bwd(x, tgt, saved[last]["pre2"], weights[last]["ln2g"], weights[last]["ln2b"])
    grads = [None] * len(layers)
    for l in reversed(range(len(layers))):
        on_part = functools.partial(on_grads, l) if on_grads is not None else None
        dx, grads[l] = _layer_bwd(dpre2, (dg, db), weights[l], saved[l], on_part)
        if l > 0:
            dpre2, dg, db = _ln_bwd(saved[l - 1]["pre2"], weights[l - 1]["ln2g"], weights[l - 1]["ln2b"], dx)
    return lacc[0, 0], dx, grads


_QKVZ = 4 * D_MODEL
_BA = 2 * N_HEADS


WEIGHT_NAMES = ("w_in", "conv_w", "a_log", "dt_bias", "o_norm_w", "sgu_ln_g", "sgu_ln_b", "w_s", "b_s", "w_pa", "w_pb",
                "w_o", "ln1_g", "ln1_b", "w_ffn_gate", "w_ffn_up", "w_ffn_down", "ln2_g", "ln2_b")
WIRE = ("w_in", "w_ffn_gate", "w_ffn_up", "w_ffn_down", "w_pa", "w_pb", "w_o", "conv_w")
SMALL = (("a_log", N_HEADS), ("dt_bias", N_HEADS), ("o_norm_w", D_HEAD), ("sgu_ln_g", D_MODEL), ("sgu_ln_b", D_MODEL),
         ("w_s", SGU_GROUPS * SGU_BLOCK * SGU_BLOCK), ("b_s", SGU_GROUPS * SGU_BLOCK),
         ("ln1_g", D_MODEL), ("ln1_b", D_MODEL), ("ln2_g", D_MODEL), ("ln2_b", D_MODEL))
SMALL_ROWS = -(-sum(n for _, n in SMALL) // (LANES * SUBLANES)) * SUBLANES
N_MAIN_TILES = (N_IN - _BA) // D_MODEL
ADAM_TILES = dict(w_in=(128, "adamw_in"), w_ffn_gate=(32, "adamw_ffn_rows"), w_ffn_up=(32, "adamw_ffn_rows"),
                  w_ffn_down=(32, "adamw_ffn_rows"), w_pa=(128, "adamw_sq"), w_pb=(128, "adamw_sq"), w_o=(128, "adamw_sq"),
                  conv_w=(CONV_K, "adamw_conv"))


def _pad_to(a, axis, size):
    pads = [(0, 0)] * a.ndim
    pads[axis] = (0, size - a.shape[axis])
    return jnp.pad(a, pads)


def _t(a):
    return jnp.swapaxes(a, 1, 2)


def _wire_blocks(p):
    return dict(
        w_in=_pad_to(p["w_in"].astype(BF), 2, IN_PAD),
        w_ffn_gate=_pad_to(_t(p["w_ffn_gate"]).astype(BF), 1, FFN_PAD), w_ffn_up=_pad_to(_t(p["w_ffn_up"]).astype(BF), 1, FFN_PAD),
        w_ffn_down=_pad_to(p["w_ffn_down"].astype(BF), 1, FFN_PAD),
        w_pa=p["w_pa"].astype(BF), w_pb=p["w_pb"].astype(BF), w_o=p["w_o"].astype(BF),
        conv_w=_pad_to(p["conv_w"], 1, SUBLANES),
    )


def _by_columns(blocks):
    n, r, c = blocks.shape
    return jnp.transpose(blocks, (1, 0, 2)).reshape(r, n * c)


def _to_slots(full, c):
    r = full.shape[0]
    return jnp.transpose(full.reshape(r, N_DEV, c), (1, 0, 2))


def _lane_row(v, at):
    return jnp.pad(v[None], ((0, 0), (at, LANES - at - v.shape[0])))


TRANSPOSED = ("w_ffn_gate", "w_ffn_up")
EARLY = ("w_in", "conv_w")
LATE = ("w_pa", "w_pb", "w_o", "w_ffn_gate", "w_ffn_up", "w_ffn_down")


def _early_weights(stacks, p, l):
    return dict(
        win=_perm_in(stacks["w_in"], D_MODEL, N_MAIN_TILES), wba=_perm_in(stacks["w_in"], LANES, 1),
        convw=_by_columns(stacks["conv_w"][:, :CONV_K]),
        arow=_lane_row(p["a_log"][l], N_HEADS), dtrow=_lane_row(p["dt_bias"][l], N_HEADS),
        onw=p["o_norm_w"][l][None], sg=p["sgu_ln_g"][l][None], sb=p["sgu_ln_b"][l][None],
        ws=p["w_s"][l], bst=_pad_to(p["b_s"][l].T, 1, LANES),
        ln1g=p["ln1_g"][l][None], ln1b=p["ln1_b"][l][None], ln2g=p["ln2_g"][l][None], ln2b=p["ln2_b"][l][None],
    )


def _late_weights(stacks):
    return dict(
        wpa=stacks["w_pa"].reshape(D_MODEL, D_MODEL), wpb=stacks["w_pb"].reshape(D_MODEL, D_MODEL),
        wo=stacks["w_o"].reshape(D_MODEL, D_MODEL),
        wgt=stacks["w_ffn_gate"].reshape(FFN_K, D_MODEL), wut=stacks["w_ffn_up"].reshape(FFN_K, D_MODEL),
        wd=stacks["w_ffn_down"].reshape(FFN_K, D_MODEL),
    )


def _small_pack(parts):
    flat = jnp.concatenate([parts[n].reshape(-1) for n, _ in SMALL])
    return _pad_to(flat, 0, SMALL_ROWS * LANES).reshape(SMALL_ROWS, LANES)


def _small_unpack(rows, like):
    flat, out, off = rows.reshape(-1), {}, 0
    for n, size in SMALL:
        out[n] = flat[off:off + size].reshape(like[n].shape[1:])
        off += size
    return out


def _late_slots(g):
    slots = dict(
        w_ffn_gate=g["wgt"].reshape(N_DEV, FFN_PAD, D_MODEL), w_ffn_up=g["wut"].reshape(N_DEV, FFN_PAD, D_MODEL),
        w_ffn_down=g["wd"].reshape(N_DEV, FFN_PAD, D_MODEL),
        w_pa=g["wpa"].reshape(N_DEV, D_MODEL // N_DEV, D_MODEL), w_pb=g["wpb"].reshape(N_DEV, D_MODEL // N_DEV, D_MODEL),
        w_o=g["wo"].reshape(N_DEV, D_MODEL // N_DEV, D_MODEL),
    )
    return [slots[n] for n in LATE]


def _early_slots(g):
    slots = [_perm_out(g["win"], g["wba"]), _pad_to(_to_slots(g["convw"][:CONV_K], 3 * D_MODEL // N_DEV), 1, SUBLANES)]
    small = _small_pack(dict(
        a_log=g["arow"][0, N_HEADS:2 * N_HEADS], dt_bias=g["dtrow"][0, N_HEADS:2 * N_HEADS], o_norm_w=g["onw"][0],
        sgu_ln_g=g["sg"][0], sgu_ln_b=g["sb"][0], w_s=g["ws"], b_s=g["bst"][:, :SGU_GROUPS].T,
        ln1_g=g["ln1g"][0], ln1_b=g["ln1b"][0], ln2_g=g["ln2g"][0], ln2_b=g["ln2b"][0]))
    return slots, small


def _in_tile_start(j, tile_w):
    if tile_w == LANES:
        return jnp.int32(_QKVZ)
    return j * D_MODEL + jnp.where(j >= _QKVZ // D_MODEL, _BA, 0)


def _select(rows_iota, cols_iota, dev, start, valid):
    hit = (rows_iota + (dev * IN_BLOCK - start) == cols_iota) & (rows_iota < IN_BLOCK) & (cols_iota < valid)
    return jnp.where(hit, 1.0, 0.0).astype(BF)


def _perm_in(stack, tile_w, n_tiles):
    valid = _BA if tile_w == LANES else tile_w

    def first_dev(j):
        return lax.div(_in_tile_start(j, tile_w), jnp.int32(IN_BLOCK))

    def body(w_ref, o_ref, acc_ref):
        j, k = pl.program_id(0), pl.program_id(1)
        sel = _select(_iota((IN_PAD, tile_w), 0), _iota((IN_PAD, tile_w), 1), first_dev(j) + k,
                      _in_tile_start(j, tile_w), valid)
        part = jnp.dot(w_ref[0], sel, preferred_element_type=F32)

        @pl.when(k == 0)
        def _():
            acc_ref[...] = part

        @pl.when(k == 1)
        def _():
            o_ref[...] = (acc_ref[...] + part).astype(BF)

    est = _nbytes((D_MODEL, IN_PAD), BF) + 3 * _nbytes((D_MODEL, tile_w), F32) + 2 * _nbytes((IN_PAD, tile_w), F32)
    return pl.pallas_call(
        body, name="perm_in" if tile_w != LANES else "perm_in_ba", grid=(n_tiles, 2),
        in_specs=[pl.BlockSpec((1, D_MODEL, IN_PAD), lambda j, k: (jnp.minimum(first_dev(j) + k, N_DEV - 1), 0, 0))],
        out_specs=pl.BlockSpec((D_MODEL, tile_w), lambda j, k: (0, j)),
        out_shape=jax.ShapeDtypeStruct((D_MODEL, n_tiles * tile_w), BF),
        scratch_shapes=[pltpu.VMEM((D_MODEL, tile_w), F32)],
        compiler_params=_cparams(est, ("parallel", "arbitrary")),
    )(_in_hbm(stack))


def _perm_out(dmain, dba):
    def tile(d, s):
        c0 = d * IN_BLOCK
        first = lax.div(c0 - jnp.where(c0 < _QKVZ, 0, jnp.minimum(c0 - _QKVZ, _BA)), jnp.int32(D_MODEL))
        return jnp.minimum(first + jnp.minimum(s, 1), N_MAIN_TILES - 1)

    def body(dm_ref, db_ref, o_ref, acc_ref):
        d, s = pl.program_id(0), pl.program_id(1)

        @pl.when(s == 0)
        def _():
            acc_ref[...] = jnp.zeros_like(acc_ref)

        start = _in_tile_start(tile(d, s), D_MODEL)
        overlaps = (start < (d + 1) * IN_BLOCK) & (d * IN_BLOCK < start + D_MODEL)

        @pl.when((s < 2) & overlaps)
        def _():
            sel = _select(_iota((D_MODEL, IN_PAD), 1), _iota((D_MODEL, IN_PAD), 0), d, start, D_MODEL)
            acc_ref[...] += jnp.dot(dm_ref[...], sel, preferred_element_type=F32)

        @pl.when(s == 2)
        def _():
            sel = _select(_iota((LANES, IN_PAD), 1), _iota((LANES, IN_PAD), 0), d, jnp.int32(_QKVZ), _BA)
            o_ref[0] = (acc_ref[...] + jnp.dot(db_ref[...], sel, preferred_element_type=F32)).astype(BF)

    est = 2 * _nbytes((D_MODEL, D_MODEL), BF) + 4 * _nbytes((D_MODEL, IN_PAD), F32)
    return pl.pallas_call(
        body, name="perm_out", grid=(N_DEV, 3),
        in_specs=[pl.BlockSpec((D_MODEL, D_MODEL), lambda d, s: (0, tile(d, s))),
                  pl.BlockSpec((D_MODEL, LANES), lambda d, s: (0, 0))],
        out_specs=pl.BlockSpec((1, D_MODEL, IN_PAD), lambda d, t: (d, 0, 0)),
        out_shape=jax.ShapeDtypeStruct((N_DEV, D_MODEL, IN_PAD), BF),
        scratch_shapes=[pltpu.VMEM((D_MODEL, IN_PAD), F32)],
        compiler_params=_cparams(est, ("parallel", "arbitrary")),
    )(dmain, dba)


def _mesh_place():
    x, y, c = (lax.axis_index(a) for a in MESH_AXES)
    return x, y, c


def _slot(x, y, c):
    return 4 * x + 2 * y + c


def _peer(place, j):
    x, y, c = place
    return (1 - x if j & 4 else x, 1 - y if j & 2 else y, 1 - c if j & 1 else c)


_HBM = pl.BlockSpec(memory_space=pltpu.HBM)
_SEM = pl.BlockSpec(memory_space=pltpu.SEMAPHORE)
_EFFECT = pltpu.SideEffectType.DATAFLOW_SIDE_EFFECTING


def _remote_copy(src_ref, land_ref, slot, per_slot, pslot, sems, u, j, peer):
    return pltpu.make_async_remote_copy(
        src_ref=src_ref.at[pslot] if per_slot else src_ref, dst_ref=land_ref.at[slot],
        send_sem=sems[0].at[u * (N_DEV - 1) + j - 1], recv_sem=sems[1].at[u * (N_DEV - 1) + j - 1],
        device_id=peer, device_id_type=pl.DeviceIdType.MESH)


def _own_copy(src_ref, land_ref, me, per_slot, sems, u):
    return pltpu.make_async_copy(src_ref.at[me] if per_slot else src_ref, land_ref.at[me], sems[2].at[u])


def _exchange_start(name, srcs, per_slot):
    n = len(srcs)
    lands = [jax.ShapeDtypeStruct(s.shape if p else (N_DEV,) + s.shape, s.dtype) for s, p in zip(srcs, per_slot)]

    def body(*refs):
        src_refs, sems, land_refs, token = refs[:n], refs[n:n + 3], refs[2 * n + 3:3 * n + 3], refs[-1]
        place = _mesh_place()
        me = _slot(*place)
        for u in range(n):
            _own_copy(src_refs[u], land_refs[u], me, per_slot[u], sems, u).start()
            for j in range(1, N_DEV):
                peer = _peer(place, j)
                _remote_copy(src_refs[u], land_refs[u], me, per_slot[u], _slot(*peer), sems, u, j, peer).start()
        token[...] = jnp.zeros_like(token)

    hbm = lambda a: pltpu.HBM(a.shape, a.dtype)
    sem = pltpu.SemaphoreType.DMA((n * (N_DEV - 1),))
    outs = pl.pallas_call(
        body, name=name,
        out_shape=(sem, sem, pltpu.SemaphoreType.DMA((n,)), *[hbm(a) for a in srcs], *[hbm(a) for a in lands],
                   jax.ShapeDtypeStruct((SUBLANES, LANES), F32)),
        in_specs=[_HBM] * n, out_specs=(_SEM, _SEM, _SEM, *[_HBM] * (2 * n), pl.BlockSpec(memory_space=pltpu.VMEM)),
        input_output_aliases={i: 3 + i for i in range(n)},
        compiler_params=pltpu.CompilerParams(has_side_effects=_EFFECT),
    )(*[pltpu.with_memory_space_constraint(a, pltpu.HBM) for a in srcs])
    return tuple(outs[:3]), list(outs[3:3 + n]), list(outs[3 + n:3 + 2 * n]), outs[-1]


def _exchange_wait(name, sems, srcs, lands, units, per_slot, after):
    m = len(units)
    after = list(after) if isinstance(after, (list, tuple)) else [after]

    def body(*refs):
        src_refs, land_refs, sem_refs = refs[:m], refs[m:2 * m], refs[2 * m:2 * m + 3]
        place = _mesh_place()
        me = _slot(*place)
        for i, u in enumerate(units):
            _own_copy(src_refs[i], land_refs[i], me, per_slot[u], sem_refs, u).wait()
            for j in range(1, N_DEV):
                peer = _peer(place, j)
                pslot = _slot(*peer)
                cp = _remote_copy(src_refs[i], land_refs[i], pslot, per_slot[u], pslot, sem_refs, u, j, peer)
                cp.wait_send()
                cp.wait_recv()

    hbm = lambda a: pltpu.HBM(a.shape, a.dtype)
    outs = pl.pallas_call(
        body, name=name, out_shape=tuple(hbm(a) for a in list(srcs) + list(lands)),
        in_specs=[_HBM] * (2 * m) + [_SEM] * 3 + [pl.BlockSpec(memory_space=pl.ANY)] * len(after),
        out_specs=tuple([_HBM] * (2 * m)),
        input_output_aliases={i: i for i in range(2 * m)},
        compiler_params=pltpu.CompilerParams(has_side_effects=_EFFECT),
    )(*srcs, *lands, *sems, *after)
    return list(outs[m:])


def _adam_update(g, w, m, v):
    m = ADAM_B1 * m + (1.0 - ADAM_B1) * g
    v = ADAM_B2 * v + (1.0 - ADAM_B2) * jnp.square(g)
    m_hat = m / (1.0 - ADAM_B1 ** ADAM_STEP)
    v_hat = v / (1.0 - ADAM_B2 ** ADAM_STEP)
    return -ADAM_LR * (m_hat / (jnp.sqrt(v_hat) + ADAM_EPS) + ADAM_WD * w), m, v


def _adamw(recvs, w, m, v, *, tr, name):
    L, R, C = w.shape
    rp = max(tr, SUBLANES * (4 // jnp.dtype(recvs[0].dtype).itemsize))
    Cp = recvs[0].shape[2]

    def body(*refs):
        r_refs, (w_ref, m_ref, v_ref, g_ref, d_ref, nm_ref, nv_ref) = refs[:L], refs[L:]
        for l in range(L):
            @pl.when(pl.program_id(0) == l)
            def _(r_ref=r_refs[l]):
                g = r_ref[0, :tr, :C].astype(F32)
                for s in range(1, N_DEV):
                    g = g + r_ref[s, :tr, :C].astype(F32)
                d, nm, nv = _adam_update(g, w_ref[0], m_ref[0], v_ref[0])
                g_ref[0], d_ref[0], nm_ref[0], nv_ref[0] = g, d, nm, nv

    blk = pl.BlockSpec((1, tr, C), lambda l, i: (l, i, 0))
    r_specs = [pl.BlockSpec((N_DEV, rp, Cp), lambda l, i, k=k: (0, jnp.where(l == k, i, 0), 0)) for k in range(L)]
    est = 2 * _nbytes((N_DEV, rp, Cp), recvs[0].dtype) + 8 * _nbytes((tr, Cp), F32)
    return pl.pallas_call(
        body, name=name, grid=(L, R // tr),
        in_specs=r_specs + [blk] * 3, out_specs=[blk] * 4,
        out_shape=[jax.ShapeDtypeStruct((L, R, C), F32)] * 4,
        compiler_params=_cparams(est, ("arbitrary", "arbitrary")),
    )(*recvs, w, m, v)


def _adamw_small(recv, w, m, v):
    def body(r_ref, w_ref, m_ref, v_ref, g_ref, d_ref, nm_ref, nv_ref):
        g = r_ref[0]
        for s in range(1, N_DEV):
            g = g + r_ref[s]
        g_ref[...] = g
        d_ref[...], nm_ref[...], nv_ref[...] = _adam_update(g, w_ref[...], m_ref[...], v_ref[...])

    vm = pl.BlockSpec(memory_space=pltpu.VMEM)
    return pl.pallas_call(
        body, name="adamw_small", in_specs=[vm] * 4, out_specs=[vm] * 4,
        out_shape=[jax.ShapeDtypeStruct((SMALL_ROWS, LANES), F32)] * 4,
        compiler_params=_cparams(20 * _nbytes((SMALL_ROWS, LANES), F32)),
    )(recv, w, m, v)


def kernel(x, w_in, conv_w, a_log, dt_bias, o_norm_w, sgu_ln_g, sgu_ln_b, w_s, b_s, w_pa, w_pb, w_o, ln1_g, ln1_b, w_ffn_gate, w_ffn_up, w_ffn_down, ln2_g, ln2_b, loss_target, m_w_in, m_conv_w, m_a_log, m_dt_bias, m_o_norm_w, m_sgu_ln_g, m_sgu_ln_b, m_w_s, m_b_s, m_w_pa, m_w_pb, m_w_o, m_ln1_g, m_ln1_b, m_w_ffn_gate, m_w_ffn_up, m_w_ffn_down, m_ln2_g, m_ln2_b, v_w_in, v_conv_w, v_a_log, v_dt_bias, v_o_norm_w, v_sgu_ln_g, v_sgu_ln_b, v_w_s, v_b_s, v_w_pa, v_w_pb, v_w_o, v_ln1_g, v_ln1_b, v_w_ffn_gate, v_w_ffn_up, v_w_ffn_down, v_ln2_g, v_ln2_b):
    given = dict(locals())
    P = {n: given[n] for n in WEIGHT_NAMES}
    M = {n: given["m_" + n] for n in WEIGHT_NAMES}
    V = {n: given["v_" + n] for n in WEIGHT_NAMES}

    wire = _wire_blocks(P)
    units = [(n, l) for l in range(DEPTH) for n in EARLY + LATE]
    whole = [False] * len(units)
    g_sems, g_srcs, g_lands, g_token = _exchange_start("gather_start", [wire[n][l] for n, l in units], whole)

    one = 1.0 + g_token[0, 0]
    xb = (x[0] * one).astype(BF)
    small_in = [[_small_pack({n: T[n][l] * one for n, _ in SMALL}) for T in (P, M, V)] for l in range(DEPTH)]
    adam_in = {n: (P[n], M[n], V[n]) for n in WIRE}
    adam_in["w_in"], _ = lax.optimization_barrier((adam_in["w_in"], g_token))
    prepared = [xb, *[a for packs in small_in for a in packs], *adam_in["w_in"]]

    def gathered(name, names, l, after):
        idx = [units.index((n, l)) for n in names]
        got = _exchange_wait(name, g_sems, [g_srcs[i] for i in idx], [g_lands[i] for i in idx], idx, whole, after)
        return dict(zip(names, got))

    def layer(l):
        def weights(x_in):
            after = prepared if l == 0 else x_in
            early = _early_weights(gathered(f"gather_wait_early{l}", EARLY, l, after), P, l)
            return early, lambda ya: _late_weights(gathered(f"gather_wait_late{l}", LATE, l, ya))
        return weights

    pending = {}

    def on_grads(l, part, g):
        if part == "late":
            srcs, names = _late_slots(g), LATE
            per_slot = [True] * len(srcs)
        else:
            slots, small = _early_slots(g)
            srcs, names = slots + [small], EARLY + ("small",)
            per_slot = [True] * len(slots) + [False]
        sems, s_thru, l_thru, token = _exchange_start(f"exchange_start_{part}{l}", srcs, per_slot)
        pending[l, part] = (names, sems, s_thru, l_thru, per_slot)
        return token[0, 0]

    loss_local, dx, _ = _local_step(x[0], xb, loss_target[0], [layer(l) for l in range(DEPTH)], on_grads)
    loss = lax.psum(loss_local, MESH_AXES)

    recv = [{} for _ in range(DEPTH)]

    def received(l, part, after):
        names, sems, s_thru, l_thru, per_slot = pending[l, part]
        got = _exchange_wait(f"exchange_wait_{part}{l}", sems, s_thru, l_thru, list(range(len(s_thru))), per_slot, after)
        recv[l].update(zip(names, got))

    out = {}

    def adamw(names):
        for n in names:
            tr, name = ADAM_TILES[n]
            view = _t if n in TRANSPOSED else (lambda a: a)
            res = _adamw([recv[l][n] for l in range(DEPTH)], *[view(a) for a in adam_in[n]], tr=tr, name=name)
            out[n] = [view(r) for r in res]

    for l in reversed(range(DEPTH)):
        received(l, "late", dx)
    adamw(LATE)
    for l in reversed(range(DEPTH)):
        received(l, "early", out[LATE[-1]][0])
    adamw(EARLY)
    small = [_adamw_small(recv[l]["small"], *small_in[l]) for l in range(DEPTH)]
    for n, _ in SMALL:
        out[n] = [jnp.stack([_small_unpack(small[l][i], P)[n] for l in range(DEPTH)]) for i in range(4)]
    return (loss, dx[None], *[out[n][i] for i in range(4) for n in WEIGHT_NAMES])
```
